```python
import jax, jax.numpy as jnp
from jax import lax
import numpy as np

D_MODEL = 1024
BATCH = 8
SEQ = 4096
DEPTH = 4

HEAD_DIM = 64
N_ATT_HEADS = 8
D_ATT = N_ATT_HEADS * HEAD_DIM
N_CONV_GROUPS = 4
D_CONV = N_CONV_GROUPS * HEAD_DIM
N_SGU_GROUPS = 4
D_SGU = N_SGU_GROUPS * HEAD_DIM
SGU_CHUNK = 128
Q_BLOCK = 128
CONV_WIDTH = 3
D_FF = 2816
N_BRANCHES = 3
RMS_EPS = 1e-6
LN_EPS = 1e-5
IN_WIDTHS = (D_ATT, D_ATT, D_ATT, N_ATT_HEADS, D_CONV, D_CONV, D_CONV, D_SGU, D_SGU, N_BRANCHES * D_MODEL)
IN_WIDTH = 3 * D_ATT + N_ATT_HEADS + 3 * D_CONV + 2 * D_SGU + N_BRANCHES * D_MODEL

kernel_name = "fox_shortconv_sgu_gated_hybrid"


def rms_norm(x, g):
    xf = x.astype(jnp.float32)
    y = xf * lax.rsqrt(jnp.mean(xf * xf, axis=-1, keepdims=True) + RMS_EPS)
    return (y * g.astype(jnp.float32)).astype(x.dtype)


def layer_norm(x, g, b):
    xf = x.astype(jnp.float32)
    mu = jnp.mean(xf, axis=-1, keepdims=True)
    xc = xf - mu
    var = jnp.mean(xc * xc, axis=-1, keepdims=True)
    y = xc * lax.rsqrt(var + LN_EPS) * g.astype(jnp.float32) + b.astype(jnp.float32)
    return y.astype(x.dtype)


def split_cols(h, widths):
    offs = np.cumsum(np.array(widths))[:-1].tolist()
    return jnp.split(h, offs, axis=-1)


def causal_dwconv(x, w):
    K = w.shape[0]
    S = x.shape[1]
    xp = jnp.pad(x, ((0, 0), (K - 1, 0), (0, 0)))
    y = xp[:, K - 1:K - 1 + S] * w[K - 1]
    for k in range(K - 1):
        y = y + xp[:, k:k + S] * w[k]
    return y


def fox_attention(q, k, v, logf):
    S = q.shape[1]
    scale = HEAD_DIM ** -0.5
    c = jnp.cumsum(logf, axis=1).transpose(0, 2, 1)
    outs = []
    for i in range(S // Q_BLOCK):
        q0 = i * Q_BLOCK
        q1 = q0 + Q_BLOCK
        s = jnp.einsum('bqhd,bkhd->bhqk', q[:, q0:q1], k[:, :q1]).astype(jnp.float32) * scale
        bias = c[:, :, q0:q1, None] - c[:, :, None, :q1]
        mask = jnp.arange(q0, q1)[:, None] >= jnp.arange(q1)[None, :]
        s = jnp.where(mask, s + bias, -jnp.inf)
        p = jax.nn.softmax(s, axis=-1).astype(v.dtype)
        outs.append(jnp.einsum('bhqk,bkhd->bqhd', p, v[:, :q1]))
    return jnp.concatenate(outs, axis=1)


def short_conv_mixer(b_gate, c_gate, h, conv_w):
    return b_gate * causal_dwconv(c_gate * h, conv_w)


def chunked_sgu(u, v, ln_g, ln_b, w_s, b_s):
    B_, S, _ = u.shape
    u = jax.nn.gelu(u, approximate=True)
    v = layer_norm(jax.nn.gelu(v, approximate=True), ln_g, ln_b)
    n = S // SGU_CHUNK
    vc = v.reshape(B_, n, SGU_CHUNK, N_SGU_GROUPS, HEAD_DIM)
    mask = jnp.tril(jnp.ones((SGU_CHUNK, SGU_CHUNK), w_s.dtype))
    mixed = jnp.einsum('gts,bnsgd->bntgd', w_s * mask, vc) + b_s.T[:, :, None]
    return u * mixed.reshape(B_, S, D_SGU)


def conv_gated_ffn(x, w_up, conv_w, w_down):
    h = causal_dwconv(x @ w_up, conv_w)
    a, b = jnp.split(h, 2, axis=-1)
    return (jax.nn.gelu(a, approximate=True) * b) @ w_down


def _fwd_setup_inputs(seed: int = 0) -> dict:
    key = jax.random.key(seed)
    ks = jax.random.split(key, 24)
    f32 = jnp.float32

    def nrm(k, shape, scale):
        return jax.random.normal(k, shape, f32) * scale

    L = DEPTH
    return {
        "x": nrm(ks[0], (BATCH, SEQ, D_MODEL), 1.0),
        "pre_mix_g": 1.0 + nrm(ks[1], (L, D_MODEL), 0.02),
        "post_mix_g": 1.0 + nrm(ks[2], (L, D_MODEL), 0.02),
        "pre_ffn_g": 1.0 + nrm(ks[3], (L, D_MODEL), 0.02),
        "post_ffn_g": 1.0 + nrm(ks[4], (L, D_MODEL), 0.02),
        "w_in": nrm(ks[5], (L, D_MODEL, IN_WIDTH), D_MODEL ** -0.5),
        "b_forget": jnp.linspace(1.0, 6.0, N_ATT_HEADS, dtype=f32)[None, :] + nrm(ks[6], (L, N_ATT_HEADS), 0.1),
        "b_gate": nrm(ks[7], (L, N_BRANCHES, D_MODEL), 0.02),
        "conv_mix_w": nrm(ks[8], (L, CONV_WIDTH, D_CONV), CONV_WIDTH ** -0.5),
        "sgu_ln_g": 1.0 + nrm(ks[9], (L, D_SGU), 0.02),
        "sgu_ln_b": nrm(ks[10], (L, D_SGU), 0.02),
        "sgu_w": nrm(ks[11], (L, N_SGU_GROUPS, SGU_CHUNK, SGU_CHUNK), SGU_CHUNK ** -0.5),
        "sgu_b": 1.0 + nrm(ks[12], (L, N_SGU_GROUPS, SGU_CHUNK), 0.02),
        "w_branch_att": nrm(ks[13], (L, D_ATT, D_MODEL), D_ATT ** -0.5),
        "w_branch_conv": nrm(ks[14], (L, D_CONV, D_MODEL), D_CONV ** -0.5),
        "w_branch_sgu": nrm(ks[15], (L, D_SGU, D_MODEL), D_SGU ** -0.5),
        "w_out": nrm(ks[16], (L, D_MODEL, D_MODEL), D_MODEL ** -0.5),
        "w_ffn_up": nrm(ks[17], (L, D_MODEL, 2 * D_FF), D_MODEL ** -0.5),
        "conv_ffn_w": nrm(ks[18], (L, CONV_WIDTH, 2 * D_FF), CONV_WIDTH ** -0.5),
        "w_ffn_down": nrm(ks[19], (L, D_FF, D_MODEL), D_FF ** -0.5),
    }


def _fwd_reference(x, pre_mix_g, post_mix_g, pre_ffn_g, post_ffn_g, w_in, b_forget, b_gate,
              conv_mix_w, sgu_ln_g, sgu_ln_b, sgu_w, sgu_b, w_branch_att, w_branch_conv,
              w_branch_sgu, w_out, w_ffn_up, conv_ffn_w, w_ffn_down):
    B_, S, D = x.shape
    for l in range(DEPTH):
        xn = rms_norm(x, pre_mix_g[l])
        h = xn @ w_in[l]
        q, k, v, f_logit, bg, cg, hc, u, vs, g_logit = split_cols(h, IN_WIDTHS)
        q = q.reshape(B_, S, N_ATT_HEADS, HEAD_DIM)
        k = k.reshape(B_, S, N_ATT_HEADS, HEAD_DIM)
        v = v.reshape(B_, S, N_ATT_HEADS, HEAD_DIM)
        logf = jax.nn.log_sigmoid((f_logit + b_forget[l]).astype(jnp.float32))
        y_att = fox_attention(q, k, v, logf).reshape(B_, S, D_ATT) @ w_branch_att[l]
        y_conv = short_conv_mixer(bg, cg, hc, conv_mix_w[l]) @ w_branch_conv[l]
        y_sgu = chunked_sgu(u, vs, sgu_ln_g[l], sgu_ln_b[l], sgu_w[l], sgu_b[l]) @ w_branch_sgu[l]
        gates = jax.nn.sigmoid(g_logit.reshape(B_, S, N_BRANCHES, D) + b_gate[l])
        merged = gates[:, :, 0] * y_att + gates[:, :, 1] * y_conv + gates[:, :, 2] * y_sgu
        x = x + rms_norm(merged @ w_out[l], post_mix_g[l])
        xn = rms_norm(x, pre_ffn_g[l])
        x = x + rms_norm(conv_gated_ffn(xn, w_ffn_up[l], conv_ffn_w[l], w_ffn_down[l]), post_ffn_g[l])
    return x


import jax as _jax
import jax.numpy as _jnp

TWIN_FORMAT = 'train_step'
FWD_PARAMS = ['x', 'pre_mix_g', 'post_mix_g', 'pre_ffn_g', 'post_ffn_g', 'w_in', 'b_forget', 'b_gate', 'conv_mix_w', 'sgu_ln_g', 'sgu_ln_b', 'sgu_w', 'sgu_b', 'w_branch_att', 'w_branch_conv', 'w_branch_sgu', 'w_out', 'w_ffn_up', 'conv_ffn_w', 'w_ffn_down']
TWIN_WEIGHTS = ['pre_mix_g', 'post_mix_g', 'pre_ffn_g', 'post_ffn_g', 'w_in', 'b_forget', 'b_gate', 'conv_mix_w', 'sgu_ln_g', 'sgu_ln_b', 'sgu_w', 'sgu_b', 'w_branch_att', 'w_branch_conv', 'w_branch_sgu', 'w_out', 'w_ffn_up', 'conv_ffn_w', 'w_ffn_down']
TWIN_DIFF_INPUT = 'x'
TWIN_INPUTS = ['x', 'pre_mix_g', 'post_mix_g', 'pre_ffn_g', 'post_ffn_g', 'w_in', 'b_forget', 'b_gate', 'conv_mix_w', 'sgu_ln_g', 'sgu_ln_b', 'sgu_w', 'sgu_b', 'w_branch_att', 'w_branch_conv', 'w_branch_sgu', 'w_out', 'w_ffn_up', 'conv_ffn_w', 'w_ffn_down', 'loss_target', 'm_pre_mix_g', 'm_post_mix_g', 'm_pre_ffn_g', 'm_post_ffn_g', 'm_w_in', 'm_b_forget', 'm_b_gate', 'm_conv_mix_w', 'm_sgu_ln_g', 'm_sgu_ln_b', 'm_sgu_w', 'm_sgu_b', 'm_w_branch_att', 'm_w_branch_conv', 'm_w_branch_sgu', 'm_w_out', 'm_w_ffn_up', 'm_conv_ffn_w', 'm_w_ffn_down', 'v_pre_mix_g', 'v_post_mix_g', 'v_pre_ffn_g', 'v_post_ffn_g', 'v_w_in', 'v_b_forget', 'v_b_gate', 'v_conv_mix_w', 'v_sgu_ln_g', 'v_sgu_ln_b', 'v_sgu_w', 'v_sgu_b', 'v_w_branch_att', 'v_w_branch_conv', 'v_w_branch_sgu', 'v_w_out', 'v_w_ffn_up', 'v_conv_ffn_w', 'v_w_ffn_down']
TWIN_OUTPUTS = ['loss', 'grad_x', 'grad_pre_mix_g', 'grad_post_mix_g', 'grad_pre_ffn_g', 'grad_post_ffn_g', 'grad_w_in', 'grad_b_forget', 'grad_b_gate', 'grad_conv_mix_w', 'grad_sgu_ln_g', 'grad_sgu_ln_b', 'grad_sgu_w', 'grad_sgu_b', 'grad_w_branch_att', 'grad_w_branch_conv', 'grad_w_branch_sgu', 'grad_w_out', 'grad_w_ffn_up', 'grad_conv_ffn_w', 'grad_w_ffn_down', 'delta_pre_mix_g', 'delta_post_mix_g', 'delta_pre_ffn_g', 'delta_post_ffn_g', 'delta_w_in', 'delta_b_forget', 'delta_b_gate', 'delta_conv_mix_w', 'delta_sgu_ln_g', 'delta_sgu_ln_b', 'delta_sgu_w', 'delta_sgu_b', 'delta_w_branch_att', 'delta_w_branch_conv', 'delta_w_branch_sgu', 'delta_w_out', 'delta_w_ffn_up', 'delta_conv_ffn_w', 'delta_w_ffn_down', 'new_m_pre_mix_g', 'new_m_post_mix_g', 'new_m_pre_ffn_g', 'new_m_post_ffn_g', 'new_m_w_in', 'new_m_b_forget', 'new_m_b_gate', 'new_m_conv_mix_w', 'new_m_sgu_ln_g', 'new_m_sgu_ln_b', 'new_m_sgu_w', 'new_m_sgu_b', 'new_m_w_branch_att', 'new_m_w_branch_conv', 'new_m_w_branch_sgu', 'new_m_w_out', 'new_m_w_ffn_up', 'new_m_conv_ffn_w', 'new_m_w_ffn_down', 'new_v_pre_mix_g', 'new_v_post_mix_g', 'new_v_pre_ffn_g', 'new_v_post_ffn_g', 'new_v_w_in', 'new_v_b_forget', 'new_v_b_gate', 'new_v_conv_mix_w', 'new_v_sgu_ln_g', 'new_v_sgu_ln_b', 'new_v_sgu_w', 'new_v_sgu_b', 'new_v_w_branch_att', 'new_v_w_branch_conv', 'new_v_w_branch_sgu', 'new_v_w_out', 'new_v_w_ffn_up', 'new_v_conv_ffn_w', 'new_v_w_ffn_down']
TWIN_LEAF_KINDS = {'loss': 'loss', 'grad_x': 'grad_x', 'grad_pre_mix_g': 'grad_w', 'grad_post_mix_g': 'grad_w', 'grad_pre_ffn_g': 'grad_w', 'grad_post_ffn_g': 'grad_w', 'grad_w_in': 'grad_w', 'grad_b_forget': 'grad_w', 'grad_b_gate': 'grad_w', 'grad_conv_mix_w': 'grad_w', 'grad_sgu_ln_g': 'grad_w', 'grad_sgu_ln_b': 'grad_w', 'grad_sgu_w': 'grad_w', 'grad_sgu_b': 'grad_w', 'grad_w_branch_att': 'grad_w', 'grad_w_branch_conv': 'grad_w', 'grad_w_branch_sgu': 'grad_w', 'grad_w_out': 'grad_w', 'grad_w_ffn_up': 'grad_w', 'grad_conv_ffn_w': 'grad_w', 'grad_w_ffn_down': 'grad_w', 'delta_pre_mix_g': 'delta_w', 'delta_post_mix_g': 'delta_w', 'delta_pre_ffn_g': 'delta_w', 'delta_post_ffn_g': 'delta_w', 'delta_w_in': 'delta_w', 'delta_b_forget': 'delta_w', 'delta_b_gate': 'delta_w', 'delta_conv_mix_w': 'delta_w', 'delta_sgu_ln_g': 'delta_w', 'delta_sgu_ln_b': 'delta_w', 'delta_sgu_w': 'delta_w', 'delta_sgu_b': 'delta_w', 'delta_w_branch_att': 'delta_w', 'delta_w_branch_conv': 'delta_w', 'delta_w_branch_sgu': 'delta_w', 'delta_w_out': 'delta_w', 'delta_w_ffn_up': 'delta_w', 'delta_conv_ffn_w': 'delta_w', 'delta_w_ffn_down': 'delta_w', 'new_m_pre_mix_g': 'new_m', 'new_m_post_mix_g': 'new_m', 'new_m_pre_ffn_g': 'new_m', 'new_m_post_ffn_g': 'new_m', 'new_m_w_in': 'new_m', 'new_m_b_forget': 'new_m', 'new_m_b_gate': 'new_m', 'new_m_conv_mix_w': 'new_m', 'new_m_sgu_ln_g': 'new_m', 'new_m_sgu_ln_b': 'new_m', 'new_m_sgu_w': 'new_m', 'new_m_sgu_b': 'new_m', 'new_m_w_branch_att': 'new_m', 'new_m_w_branch_conv': 'new_m', 'new_m_w_branch_sgu': 'new_m', 'new_m_w_out': 'new_m', 'new_m_w_ffn_up': 'new_m', 'new_m_conv_ffn_w': 'new_m', 'new_m_w_ffn_down': 'new_m', 'new_v_pre_mix_g': 'new_v', 'new_v_post_mix_g': 'new_v', 'new_v_pre_ffn_g': 'new_v', 'new_v_post_ffn_g': 'new_v', 'new_v_w_in': 'new_v', 'new_v_b_forget': 'new_v', 'new_v_b_gate': 'new_v', 'new_v_conv_mix_w': 'new_v', 'new_v_sgu_ln_g': 'new_v', 'new_v_sgu_ln_b': 'new_v', 'new_v_sgu_w': 'new_v', 'new_v_sgu_b': 'new_v', 'new_v_w_branch_att': 'new_v', 'new_v_w_branch_conv': 'new_v', 'new_v_w_branch_sgu': 'new_v', 'new_v_w_out': 'new_v', 'new_v_w_ffn_up': 'new_v', 'new_v_conv_ffn_w': 'new_v', 'new_v_w_ffn_down': 'new_v'}


def _forward(args):
    return _fwd_reference(*[args[k] for k in FWD_PARAMS])


def _output_shape():
    def fwd():
        inp = _fwd_setup_inputs(0)
        return _fwd_reference(*[inp[k] for k in FWD_PARAMS])
    out = _jax.eval_shape(fwd)
    return out.shape, out.dtype

N_MICROBATCH = 1
ADAM_LR = 0.001
ADAM_B1 = 0.9
ADAM_B2 = 0.999
ADAM_EPS = 1e-08
ADAM_WD = 0.01
ADAM_STEP = 10
PER_EXAMPLE_BATCH_AXIS = {'x': 0, 'loss_target': 0}
SHARED_INPUTS = []
_WEIGHT_DTYPES = {'pre_mix_g': _jnp.float32, 'post_mix_g': _jnp.float32, 'pre_ffn_g': _jnp.float32, 'post_ffn_g': _jnp.float32, 'w_in': _jnp.float32, 'b_forget': _jnp.float32, 'b_gate': _jnp.float32, 'conv_mix_w': _jnp.float32, 'sgu_ln_g': _jnp.float32, 'sgu_ln_b': _jnp.float32, 'sgu_w': _jnp.float32, 'sgu_b': _jnp.float32, 'w_branch_att': _jnp.float32, 'w_branch_conv': _jnp.float32, 'w_branch_sgu': _jnp.float32, 'w_out': _jnp.float32, 'w_ffn_up': _jnp.float32, 'conv_ffn_w': _jnp.float32, 'w_ffn_down': _jnp.float32}
MOMENT_SCALE = {'pre_mix_g': 2.587173e+00, 'post_mix_g': 3.200702e+01, 'pre_ffn_g': 1.866870e+00, 'post_ffn_g': 3.195752e+01, 'w_in': 1.083201e+00, 'b_forget': 2.455330e+00, 'b_gate': 7.890810e-01, 'conv_mix_w': 2.406509e+00, 'sgu_ln_g': 1.015676e+00, 'sgu_ln_b': 9.795620e-01, 'sgu_w': 7.002741e-01, 'sgu_b': 1.019122e+00, 'w_branch_att': 1.071353e+00, 'w_branch_conv': 1.146362e+00, 'w_branch_sgu': 3.077623e+00, 'w_out': 3.284489e+00, 'w_ffn_up': 8.005991e-01, 'conv_ffn_w': 8.714707e-01, 'w_ffn_down': 1.513605e+00}


def _to_microbatches(a, axis):
    t = _jnp.moveaxis(a, axis, 0)
    t = t.reshape((N_MICROBATCH, t.shape[0] // N_MICROBATCH) + t.shape[1:])
    return _jnp.moveaxis(t, 1, axis + 1)


def setup_inputs(seed: int = 0) -> dict:
    inp = _fwd_setup_inputs(seed)
    key = _jax.random.fold_in(_jax.random.key(seed), 7919)
    shape, _ = _output_shape()
    out = dict(inp)
    out["loss_target"] = _jax.random.normal(_jax.random.fold_in(key, 0), shape, _jnp.float32)
    for i, name in enumerate(TWIN_WEIGHTS):
        w = inp[name].astype(_jnp.float32)
        if MOMENT_SCALE is None:
            s = _jnp.sqrt(_jnp.mean(_jnp.square(w)) + 1e-30)
        else:
            s = MOMENT_SCALE[name]
        km, kv = _jax.random.split(_jax.random.fold_in(key, i + 1))
        out[name] = w
        out["m_" + name] = s * _jax.random.normal(km, w.shape, _jnp.float32)
        out["v_" + name] = (s * s) * _jax.random.uniform(kv, w.shape, _jnp.float32, 0.5, 1.5)
    if N_MICROBATCH > 1:
        for name, axis in PER_EXAMPLE_BATCH_AXIS.items():
            out[name] = _to_microbatches(out[name], axis)
    return {'x': out['x'], 'pre_mix_g': out['pre_mix_g'], 'post_mix_g': out['post_mix_g'], 'pre_ffn_g': out['pre_ffn_g'], 'post_ffn_g': out['post_ffn_g'], 'w_in': out['w_in'], 'b_forget': out['b_forget'], 'b_gate': out['b_gate'], 'conv_mix_w': out['conv_mix_w'], 'sgu_ln_g': out['sgu_ln_g'], 'sgu_ln_b': out['sgu_ln_b'], 'sgu_w': out['sgu_w'], 'sgu_b': out['sgu_b'], 'w_branch_att': out['w_branch_att'], 'w_branch_conv': out['w_branch_conv'], 'w_branch_sgu': out['w_branch_sgu'], 'w_out': out['w_out'], 'w_ffn_up': out['w_ffn_up'], 'conv_ffn_w': out['conv_ffn_w'], 'w_ffn_down': out['w_ffn_down'], 'loss_target': out['loss_target'], 'm_pre_mix_g': out['m_pre_mix_g'], 'm_post_mix_g': out['m_post_mix_g'], 'm_pre_ffn_g': out['m_pre_ffn_g'], 'm_post_ffn_g': out['m_post_ffn_g'], 'm_w_in': out['m_w_in'], 'm_b_forget': out['m_b_forget'], 'm_b_gate': out['m_b_gate'], 'm_conv_mix_w': out['m_conv_mix_w'], 'm_sgu_ln_g': out['m_sgu_ln_g'], 'm_sgu_ln_b': out['m_sgu_ln_b'], 'm_sgu_w': out['m_sgu_w'], 'm_sgu_b': out['m_sgu_b'], 'm_w_branch_att': out['m_w_branch_att'], 'm_w_branch_conv': out['m_w_branch_conv'], 'm_w_branch_sgu': out['m_w_branch_sgu'], 'm_w_out': out['m_w_out'], 'm_w_ffn_up': out['m_w_ffn_up'], 'm_conv_ffn_w': out['m_conv_ffn_w'], 'm_w_ffn_down': out['m_w_ffn_down'], 'v_pre_mix_g': out['v_pre_mix_g'], 'v_post_mix_g': out['v_post_mix_g'], 'v_pre_ffn_g': out['v_pre_ffn_g'], 'v_post_ffn_g': out['v_post_ffn_g'], 'v_w_in': out['v_w_in'], 'v_b_forget': out['v_b_forget'], 'v_b_gate': out['v_b_gate'], 'v_conv_mix_w': out['v_conv_mix_w'], 'v_sgu_ln_g': out['v_sgu_ln_g'], 'v_sgu_ln_b': out['v_sgu_ln_b'], 'v_sgu_w': out['v_sgu_w'], 'v_sgu_b': out['v_sgu_b'], 'v_w_branch_att': out['v_w_branch_att'], 'v_w_branch_conv': out['v_w_branch_conv'], 'v_w_branch_sgu': out['v_w_branch_sgu'], 'v_w_out': out['v_w_out'], 'v_w_ffn_up': out['v_w_ffn_up'], 'v_conv_ffn_w': out['v_conv_ffn_w'], 'v_w_ffn_down': out['v_w_ffn_down']}


def _loss(weights, diff, rest, loss_target):
    with _jax.named_scope("forward"):
        args = {**rest, TWIN_DIFF_INPUT: diff, **{k: w.astype(_WEIGHT_DTYPES[k]) for k, w in weights.items()}}
        y = _forward(args)
    with _jax.named_scope("loss_head"):
        err = _jnp.square(y.astype(_jnp.float32) - loss_target)
        return 0.5 * _jnp.sum(_jnp.mean(err, axis=-1)) if err.ndim else 0.5 * err


def _adamw(w, g, m, v):
    m = ADAM_B1 * m + (1.0 - ADAM_B1) * g
    v = ADAM_B2 * v + (1.0 - ADAM_B2) * _jnp.square(g)
    m_hat = m / (1.0 - ADAM_B1 ** ADAM_STEP)
    v_hat = v / (1.0 - ADAM_B2 ** ADAM_STEP)
    delta = -ADAM_LR * (m_hat / (_jnp.sqrt(v_hat) + ADAM_EPS) + ADAM_WD * w)
    return delta, m, v


def reference(x, pre_mix_g, post_mix_g, pre_ffn_g, post_ffn_g, w_in, b_forget, b_gate, conv_mix_w, sgu_ln_g, sgu_ln_b, sgu_w, sgu_b, w_branch_att, w_branch_conv, w_branch_sgu, w_out, w_ffn_up, conv_ffn_w, w_ffn_down, loss_target, m_pre_mix_g, m_post_mix_g, m_pre_ffn_g, m_post_ffn_g, m_w_in, m_b_forget, m_b_gate, m_conv_mix_w, m_sgu_ln_g, m_sgu_ln_b, m_sgu_w, m_sgu_b, m_w_branch_att, m_w_branch_conv, m_w_branch_sgu, m_w_out, m_w_ffn_up, m_conv_ffn_w, m_w_ffn_down, v_pre_mix_g, v_post_mix_g, v_pre_ffn_g, v_post_ffn_g, v_w_in, v_b_forget, v_b_gate, v_conv_mix_w, v_sgu_ln_g, v_sgu_ln_b, v_sgu_w, v_sgu_b, v_w_branch_att, v_w_branch_conv, v_w_branch_sgu, v_w_out, v_w_ffn_up, v_conv_ffn_w, v_w_ffn_down):
    given = dict(x=x, pre_mix_g=pre_mix_g, post_mix_g=post_mix_g, pre_ffn_g=pre_ffn_g, post_ffn_g=post_ffn_g, w_in=w_in, b_forget=b_forget, b_gate=b_gate, conv_mix_w=conv_mix_w, sgu_ln_g=sgu_ln_g, sgu_ln_b=sgu_ln_b, sgu_w=sgu_w, sgu_b=sgu_b, w_branch_att=w_branch_att, w_branch_conv=w_branch_conv, w_branch_sgu=w_branch_sgu, w_out=w_out, w_ffn_up=w_ffn_up, conv_ffn_w=conv_ffn_w, w_ffn_down=w_ffn_down, loss_target=loss_target, m_pre_mix_g=m_pre_mix_g, m_post_mix_g=m_post_mix_g, m_pre_ffn_g=m_pre_ffn_g, m_post_ffn_g=m_post_ffn_g, m_w_in=m_w_in, m_b_forget=m_b_forget, m_b_gate=m_b_gate, m_conv_mix_w=m_conv_mix_w, m_sgu_ln_g=m_sgu_ln_g, m_sgu_ln_b=m_sgu_ln_b, m_sgu_w=m_sgu_w, m_sgu_b=m_sgu_b, m_w_branch_att=m_w_branch_att, m_w_branch_conv=m_w_branch_conv, m_w_branch_sgu=m_w_branch_sgu, m_w_out=m_w_out, m_w_ffn_up=m_w_ffn_up, m_conv_ffn_w=m_conv_ffn_w, m_w_ffn_down=m_w_ffn_down, v_pre_mix_g=v_pre_mix_g, v_post_mix_g=v_post_mix_g, v_pre_ffn_g=v_pre_ffn_g, v_post_ffn_g=v_post_ffn_g, v_w_in=v_w_in, v_b_forget=v_b_forget, v_b_gate=v_b_gate, v_conv_mix_w=v_conv_mix_w, v_sgu_ln_g=v_sgu_ln_g, v_sgu_ln_b=v_sgu_ln_b, v_sgu_w=v_sgu_w, v_sgu_b=v_sgu_b, v_w_branch_att=v_w_branch_att, v_w_branch_conv=v_w_branch_conv, v_w_branch_sgu=v_w_branch_sgu, v_w_out=v_w_out, v_w_ffn_up=v_w_ffn_up, v_conv_ffn_w=v_conv_ffn_w, v_w_ffn_down=v_w_ffn_down)
    weights = {n: given[n] for n in TWIN_WEIGHTS}
    shared = {n: given[n] for n in SHARED_INPUTS}
    per_example = {n: given[n] for n in ['x']}
    grad_fn = _jax.value_and_grad(_loss, argnums=(0, 1))

    def one_microbatch(ex, loss_target):
        ex = dict(ex)
        diff = ex.pop(TWIN_DIFF_INPUT)
        return grad_fn(weights, diff, {**shared, **ex}, loss_target)

    if N_MICROBATCH == 1:
        loss, (grad_w, grad_x) = one_microbatch(per_example, given["loss_target"])
    else:
        def body(carry, xs):
            loss_sum, grad_sum = carry
            l_k, (gw_k, gx_k) = one_microbatch(xs[0], xs[1])
            with _jax.named_scope("update"):
                return (loss_sum + l_k, _jax.tree.map(_jnp.add, grad_sum, gw_k)), gx_k

        init = (_jnp.zeros((), _jnp.float32), _jax.tree.map(_jnp.zeros_like, weights))
        (loss, grad_w), grad_x = _jax.lax.scan(body, init, (per_example, given["loss_target"]))
    with _jax.named_scope("update"):
        delta_w, new_m, new_v = {}, {}, {}
        for n in TWIN_WEIGHTS:
            delta_w[n], new_m[n], new_v[n] = _adamw(weights[n], grad_w[n], given["m_" + n], given["v_" + n])
    return (loss, grad_x, *[grad_w[n] for n in TWIN_WEIGHTS], *[delta_w[n] for n in TWIN_WEIGHTS],
            *[new_m[n] for n in TWIN_WEIGHTS], *[new_v[n] for n in TWIN_WEIGHTS])
```

```python
import functools
import math

import jax
import jax.numpy as jnp
from jax import lax
from jax.experimental import pallas as pl
from jax.experimental.pallas import tpu as pltpu

F32 = jnp.float32
BF16 = jnp.bfloat16
MXU_DTYPE = jnp.bfloat16

D_MODEL = 1024
HEAD_DIM = 64
N_HEADS = 8
D_ATT = 512
D_CONV = 256
D_SGU = 256
N_GROUPS = 4
CHUNK = 128
D_FF = 2816
DEPTH = 4
RMS_EPS = 1e-6
LN_EPS = 1e-5
N_CHIPS = 4
LANES = 128
PACK_COLS = 1024
HALO = 16

ADAM_LR = 0.001
ADAM_B1 = 0.9
ADAM_B2 = 0.999
ADAM_EPS = 1e-08
ADAM_WD = 0.01
ADAM_STEP = 10

OFF_GL = 0
OFF_Q = 3 * D_MODEL
OFF_K = OFF_Q + D_ATT
OFF_V = OFF_K + D_ATT
OFF_BG = OFF_V + D_ATT
OFF_CG = OFF_BG + D_CONV
OFF_HC = OFF_CG + D_CONV
OFF_U = OFF_HC + D_CONV
OFF_VS = OFF_U + D_SGU
W_P = OFF_VS + D_SGU
F_ROWS = 16

VMEM_LIMIT = 56 * 1024 * 1024
MESH = pl.DeviceIdType.MESH


def _params(sem=None):
    if sem is None:
        return pltpu.CompilerParams(vmem_limit_bytes=VMEM_LIMIT)
    return pltpu.CompilerParams(dimension_semantics=sem, vmem_limit_bytes=VMEM_LIMIT)


def _tile(dim, pref):
    if dim <= pref:
        return dim
    if dim % pref == 0:
        return pref
    return dim


_DIMS = {"nn": (((1,), (0,)), ((), ())), "nt": (((1,), (1,)), ((), ())), "tn": (((0,), (0,)), ((), ()))}


def _mm(a, b, mode, out_dtype, name, tm, tn, tk):
    if mode == "tn":
        K, M = a.shape
    else:
        M, K = a.shape
    N = b.shape[0] if mode == "nt" else b.shape[1]
    tm, tn, tk = _tile(M, tm), _tile(N, tn), _tile(K, tk)
    nk = K // tk
    dims = _DIMS[mode]

    def body(a_ref, b_ref, o_ref, *acc):
        part = lax.dot_general(a_ref[...].astype(MXU_DTYPE), b_ref[...].astype(MXU_DTYPE), dims,
                               preferred_element_type=F32)
        if nk == 1:
            o_ref[...] = part.astype(o_ref.dtype)
        else:
            acc_ref = acc[0]
            k = pl.program_id(2)

            @pl.when(k == 0)
            def _():
                acc_ref[...] = part

            @pl.when(k > 0)
            def _():
                acc_ref[...] += part

            @pl.when(k == nk - 1)
            def _():
                o_ref[...] = acc_ref[...].astype(o_ref.dtype)

    if mode == "tn":
        a_spec = pl.BlockSpec((tk, tm), lambda i, j, k: (k, i))
    else:
        a_spec = pl.BlockSpec((tm, tk), lambda i, j, k: (i, k))
    if mode == "nt":
        b_spec = pl.BlockSpec((tn, tk), lambda i, j, k: (j, k))
    else:
        b_spec = pl.BlockSpec((tk, tn), lambda i, j, k: (k, j))
    return pl.pallas_call(
        body,
        name=name,
        grid=(M // tm, N // tn, nk),
        in_specs=[a_spec, b_spec],
        out_specs=pl.BlockSpec((tm, tn), lambda i, j, k: (i, j)),
        out_shape=jax.ShapeDtypeStruct((M, N), out_dtype),
        scratch_shapes=[pltpu.VMEM((tm, tn), F32)] if nk > 1 else [],
        compiler_params=_params(("parallel", "parallel", "arbitrary")),
    )(a, b)


_GELU_K = math.sqrt(2.0 / math.pi)
_GELU_C = 0.044715


def _gelu(x):
    t = jnp.tanh(_GELU_K * (x + _GELU_C * (x * x * x)))
    return x * (0.5 * (1.0 + t))


def _gelu_and_grad(x):
    x2 = x * x
    t = jnp.tanh(_GELU_K * (x + _GELU_C * (x2 * x)))
    cdf = 0.5 * (1.0 + t)
    dcdf = 0.5 * (1.0 - t * t) * (_GELU_K * (1.0 + 3.0 * _GELU_C * x2))
    return x * cdf, cdf + x * dcdf


def _sigmoid(x):
    return 1.0 / (1.0 + jnp.exp(-x))


def _shift_down(cur, prev, k):
    h = prev.shape[0]
    ext = jnp.concatenate([prev, cur], axis=0)
    return pltpu.roll(ext, k, 0)[h:]


def _shift_up(cur, nxt, k):
    t, h = cur.shape[0], nxt.shape[0]
    ext = jnp.concatenate([cur, nxt], axis=0)
    return pltpu.roll(ext, t + h - k, 0)[:t]


def _row_sum8(x):
    t, c = x.shape
    return jnp.sum(x.reshape(t // 8, 8, c), axis=0)


def _rms_fwd(x, g, name):
    s, d = x.shape
    t = _tile(s, 512)

    def body(x_ref, g_ref, o_ref):
        xv = x_ref[...]
        r = lax.rsqrt(jnp.mean(xv * xv, axis=-1, keepdims=True) + RMS_EPS)
        o_ref[...] = (xv * r * g_ref[...]).astype(o_ref.dtype)

    return pl.pallas_call(
        body, name=name, grid=(s // t,),
        in_specs=[pl.BlockSpec((t, d), lambda i: (i, 0)), pl.BlockSpec((1, d), lambda i: (0, 0))],
        out_specs=pl.BlockSpec((t, d), lambda i: (i, 0)),
        out_shape=jax.ShapeDtypeStruct((s, d), BF16),
        compiler_params=_params(("parallel",)),
    )(x, g)


def _resid_post(x, y, g, name):
    s, d = x.shape
    t = _tile(s, 512)

    def body(x_ref, y_ref, g_ref, o_ref):
        yv = y_ref[...]
        r = lax.rsqrt(jnp.mean(yv * yv, axis=-1, keepdims=True) + RMS_EPS)
        o_ref[...] = x_ref[...] + yv * r * g_ref[...]

    row = pl.BlockSpec((t, d), lambda i: (i, 0))
    return pl.pallas_call(
        body, name=name, grid=(s // t,),
        in_specs=[row, row, pl.BlockSpec((1, d), lambda i: (0, 0))],
        out_specs=row,
        out_shape=jax.ShapeDtypeStruct((s, d), F32),
        compiler_params=_params(("parallel",)),
    )(x, y, g)


def _rms_bwd(xin, g, dys, dres, out_dtype, name):
    s, d = xin.shape
    t = _tile(s, 512)
    n = s // t
    n_dy = len(dys)
    has_res = dres is not None

    def body(*refs):
        x_ref, g_ref = refs[0], refs[1]
        dy_refs = refs[2:2 + n_dy]
        pos = 2 + n_dy
        res_ref = refs[pos] if has_res else None
        pos += 1 if has_res else 0
        dx_ref, dg_ref, acc_ref = refs[pos], refs[pos + 1], refs[pos + 2]
        i = pl.program_id(0)
        xv = x_ref[...]
        dy = dy_refs[0][...].astype(F32)
        for extra in dy_refs[1:]:
            dy = dy + extra[...].astype(F32)
        r = lax.rsqrt(jnp.mean(xv * xv, axis=-1, keepdims=True) + RMS_EPS)
        u = dy * g_ref[...]
        xr = xv * r
        dx = r * (u - xr * jnp.mean(u * xr, axis=-1, keepdims=True))
        if has_res:
            dx = dx + res_ref[...]
        dx_ref[...] = dx.astype(dx_ref.dtype)
        part = _row_sum8(dy * xr)

        @pl.when(i == 0)
        def _():
            acc_ref[...] = part

        @pl.when(i > 0)
        def _():
            acc_ref[...] += part

        @pl.when(i == n - 1)
        def _():
            dg_ref[...] = jnp.sum(acc_ref[...], axis=0, keepdims=True)

    row = pl.BlockSpec((t, d), lambda i: (i, 0))
    vec = pl.BlockSpec((1, d), lambda i: (0, 0))
    ins = [xin, g, *dys] + ([dres] if has_res else [])
    return pl.pallas_call(
        body, name=name, grid=(n,),
        in_specs=[row, vec] + [row] * (n_dy + (1 if has_res else 0)),
        out_specs=[row, vec],
        out_shape=[jax.ShapeDtypeStruct((s, d), out_dtype), jax.ShapeDtypeStruct((1, d), F32)],
        scratch_shapes=[pltpu.VMEM((8, d), F32)],
        compiler_params=_params(("arbitrary",)),
    )(*ins)


def _loss_head(y, target, name):
    s, d = y.shape
    t = _tile(s, 512)
    n = s // t

    def body(y_ref, t_ref, dy_ref, loss_ref, acc_ref):
        i = pl.program_id(0)
        e = y_ref[...] - t_ref[...]
        dy_ref[...] = e * (1.0 / d)
        part = _row_sum8(e * e)

        @pl.when(i == 0)
        def _():
            acc_ref[...] = part

        @pl.when(i > 0)
        def _():
            acc_ref[...] += part

        @pl.when(i == n - 1)
        def _():
            tot = jnp.sum(jnp.sum(acc_ref[...], axis=0, keepdims=True), axis=1, keepdims=True)
            loss_ref[...] = tot * (0.5 / d)

    row = pl.BlockSpec((t, d), lambda i: (i, 0))
    return pl.pallas_call(
        body, name=name, grid=(n,),
        in_specs=[row, row],
        out_specs=[row, pl.BlockSpec((1, 1), lambda i: (0, 0))],
        out_shape=[jax.ShapeDtypeStruct((s, d), F32), jax.ShapeDtypeStruct((1, 1), F32)],
        scratch_shapes=[pltpu.VMEM((8, d), F32)],
        compiler_params=_params(("arbitrary",)),
    )(y, target)


def _split3(x):
    hi = x.astype(BF16)
    r1 = x - hi.astype(F32)
    mid = r1.astype(BF16)
    lo = (r1 - mid.astype(F32)).astype(BF16)
    return hi, mid, lo


def _tri_dot(x, tri):
    hi, mid, lo = _split3(x)
    dn = _DIMS["nn"]
    out = lax.dot_general(hi, tri, dn, preferred_element_type=F32)
    out = out + lax.dot_general(mid, tri, dn, preferred_element_type=F32)
    return out + lax.dot_general(lo, tri, dn, preferred_element_type=F32)


def _log_sigmoid(z):
    return jnp.minimum(z, 0.0) - jnp.log(1.0 + jnp.exp(-jnp.abs(z)))


def _gate_fwd(f_row, b_col, name):
    rows, s = f_row.shape
    t = _tile(s, 512)
    n = s // t

    def body(f_ref, b_ref, c_ref, carry_ref):
        i = pl.program_id(0)

        @pl.when(i == 0)
        def _():
            carry_ref[...] = jnp.zeros_like(carry_ref)

        logf = _log_sigmoid(f_ref[...] + b_ref[...])
        r = lax.broadcasted_iota(jnp.int32, (t, t), 0)
        c = lax.broadcasted_iota(jnp.int32, (t, t), 1)
        tri = jnp.where(r <= c, 1.0, 0.0).astype(BF16)
        cs = _tri_dot(logf, tri) + carry_ref[...]
        carry_ref[...] = cs[:, t - 1:t]
        for h in range(N_HEADS):
            c_ref[h] = jnp.broadcast_to(cs[h:h + 1, :], (8, t))

    return pl.pallas_call(
        body, name=name, grid=(n,),
        in_specs=[pl.BlockSpec((rows, t), lambda i: (0, i)), pl.BlockSpec((rows, 1), lambda i: (0, 0))],
        out_specs=pl.BlockSpec((N_HEADS, 8, t), lambda i: (0, 0, i)),
        out_shape=jax.ShapeDtypeStruct((N_HEADS, 8, s), F32),
        scratch_shapes=[pltpu.VMEM((rows, 1), F32)],
        compiler_params=_params(("arbitrary",)),
    )(f_row, b_col)


def _gate_bwd(f_row, b_col, dc_even, dc_odd, name):
    rows, s = f_row.shape
    t = _tile(s, 512)
    n = s // t

    def body(f_ref, b_ref, dce_ref, dco_ref, df_ref, db_ref, carry_ref, acc_ref):
        i = pl.program_id(0)

        @pl.when(i == 0)
        def _():
            carry_ref[...] = jnp.zeros_like(carry_ref)
            acc_ref[...] = jnp.zeros_like(acc_ref)

        head = lax.broadcasted_iota(jnp.int32, (rows, t), 0)
        dcv = jnp.zeros((rows, t), F32)
        for h in range(N_HEADS):
            src = dce_ref if h % 2 == 0 else dco_ref
            dcv = jnp.where(head == h, jnp.broadcast_to(src[h // 2, 0:1, :], (rows, t)), dcv)
        r = lax.broadcasted_iota(jnp.int32, (t, t), 0)
        c = lax.broadcasted_iota(jnp.int32, (t, t), 1)
        tri = jnp.where(r >= c, 1.0, 0.0).astype(BF16)
        dlogf = _tri_dot(dcv, tri) + carry_ref[...]
        carry_ref[...] = dlogf[:, 0:1]
        z = f_ref[...] + b_ref[...]
        df = dlogf * _sigmoid(-z)
        df_ref[...] = df.astype(df_ref.dtype)
        acc_ref[...] += jnp.sum(df, axis=1, keepdims=True)

        @pl.when(i == n - 1)
        def _():
            db_ref[...] = acc_ref[...]

    rev = lambda i: (0, n - 1 - i)
    dc_spec = pl.BlockSpec((N_HEADS // 2, 8, t), lambda i: (0, 0, n - 1 - i))
    return pl.pallas_call(
        body, name=name, grid=(n,),
        in_specs=[pl.BlockSpec((rows, t), rev), pl.BlockSpec((rows, 1), lambda i: (0, 0)), dc_spec, dc_spec],
        out_specs=[pl.BlockSpec((rows, t), rev), pl.BlockSpec((rows, 1), lambda i: (0, 0))],
        out_shape=[jax.ShapeDtypeStruct((rows, s), BF16), jax.ShapeDtypeStruct((rows, 1), F32)],
        scratch_shapes=[pltpu.VMEM((rows, 1), F32), pltpu.VMEM((rows, 1), F32)],
        compiler_params=_params(("arbitrary",)),
    )(f_row, b_col, dc_even, dc_odd)


_NEG = -1e30
_SCALE = HEAD_DIM ** -0.5


def _head_masks():
    lane = lax.broadcasted_iota(jnp.int32, (1, LANES), 1)
    return [lane < HEAD_DIM, lane >= HEAD_DIM]


def _attn_fwd(h, c, name):
    s = h.shape[0]
    t = _tile(s, 512)
    n = s // t
    qb, kb, vb = OFF_Q // LANES, OFF_K // LANES, OFF_V // LANES

    def body(q_ref, k_ref, v_ref, c0_ref, c1_ref, o_ref, of_ref, lse_ref, m_ref, l_ref, acc_ref):
        qi, ki = pl.program_id(1), pl.program_id(2)
        masks = _head_masks()

        @pl.when(ki == 0)
        def _():
            m_ref[...] = jnp.full_like(m_ref, _NEG)
            l_ref[...] = jnp.zeros_like(l_ref)
            acc_ref[...] = jnp.zeros_like(acc_ref)

        def step(diag):
            q = q_ref[...] * _SCALE
            k = k_ref[...]
            v = v_ref[...]
            acc = acc_ref[...]
            for hh, c_ref in enumerate((c0_ref, c1_ref)):
                qh = jnp.where(masks[hh], q, jnp.zeros_like(q))
                sc = lax.dot_general(qh, k, _DIMS["nt"], preferred_element_type=F32) - c_ref[0, 0:1, :]
                if diag:
                    r = lax.broadcasted_iota(jnp.int32, (t, t), 0)
                    cc = lax.broadcasted_iota(jnp.int32, (t, t), 1)
                    sc = jnp.where(r >= cc, sc, _NEG)
                m_prev = m_ref[hh]
                m_new = jnp.maximum(m_prev, jnp.max(sc, axis=1, keepdims=True))
                alpha = jnp.exp(m_prev - m_new)
                p = jnp.exp(sc - m_new)
                l_ref[hh] = alpha * l_ref[hh] + jnp.sum(p, axis=1, keepdims=True)
                m_ref[hh] = m_new
                p_hi = p.astype(MXU_DTYPE)
                p_lo = (p - p_hi.astype(F32)).astype(MXU_DTYPE)
                pv = (lax.dot_general(p_hi, v, _DIMS["nn"], preferred_element_type=F32)
                      + lax.dot_general(p_lo, v, _DIMS["nn"], preferred_element_type=F32))
                acc = jnp.where(masks[hh], alpha * acc + pv, acc)
            acc_ref[...] = acc

        @pl.when(ki < qi)
        def _():
            step(False)

        @pl.when(ki == qi)
        def _():
            step(True)
            inv = jnp.where(masks[0], 1.0 / l_ref[0], 1.0 / l_ref[1])
            out = acc_ref[...] * inv
            o_ref[...] = out.astype(o_ref.dtype)
            of_ref[...] = out
            lse_ref[...] = jnp.where(masks[0], m_ref[0] + jnp.log(l_ref[0]), m_ref[1] + jnp.log(l_ref[1]))

    kv_row = lambda p, qi, ki: jnp.minimum(ki, qi)
    return pl.pallas_call(
        body, name=name, grid=(N_HEADS // 2, n, n),
        in_specs=[
            pl.BlockSpec((t, LANES), lambda p, qi, ki: (qi, qb + p)),
            pl.BlockSpec((t, LANES), lambda p, qi, ki: (kv_row(p, qi, ki), kb + p)),
            pl.BlockSpec((t, LANES), lambda p, qi, ki: (kv_row(p, qi, ki), vb + p)),
            pl.BlockSpec((1, 8, t), lambda p, qi, ki: (2 * p, 0, kv_row(p, qi, ki))),
            pl.BlockSpec((1, 8, t), lambda p, qi, ki: (2 * p + 1, 0, kv_row(p, qi, ki))),
        ],
        out_specs=[pl.BlockSpec((t, LANES), lambda p, qi, ki: (qi, p))] * 3,
        out_shape=[jax.ShapeDtypeStruct((s, D_ATT), BF16), jax.ShapeDtypeStruct((s, D_ATT), F32),
                   jax.ShapeDtypeStruct((s, D_ATT), F32)],
        scratch_shapes=[pltpu.VMEM((2, t, 1), F32), pltpu.VMEM((2, t, 1), F32), pltpu.VMEM((t, LANES), F32)],
        compiler_params=_params(("parallel", "parallel", "arbitrary")),
    )(h, h, h, c, c)


def _attn_bwd(h, c, o, lse, do, name):
    s = h.shape[0]
    t = _tile(s, 512)
    n = s // t
    qb, kb, vb = OFF_Q // LANES, OFF_K // LANES, OFF_V // LANES

    def body(q_ref, k_ref, v_ref, c0_ref, c1_ref, o_ref, lse_ref, do_ref,
             dq_ref, dk_ref, dv_ref, dc0_ref, dc1_ref, dk_acc, dv_acc, dc_acc):
        ki, qi = pl.program_id(1), pl.program_id(2)
        masks = _head_masks()

        @pl.when((ki == 0) & (qi == 0))
        def _():
            dq_ref[...] = jnp.zeros_like(dq_ref)

        @pl.when(qi == ki)
        def _():
            dk_acc[...] = jnp.zeros_like(dk_acc)
            dv_acc[...] = jnp.zeros_like(dv_acc)
            dc_acc[...] = jnp.zeros_like(dc_acc)

        def step(diag):
            q = q_ref[...] * _SCALE
            k = k_ref[...]
            v = v_ref[...]
            dov = do_ref[...]
            lsev = lse_ref[...]
            prod = dov.astype(F32) * o_ref[...]
            dq_blk = jnp.zeros((t, LANES), F32)
            dk_blk = dk_acc[...]
            dv_blk = dv_acc[...]
            for hh, c_ref in enumerate((c0_ref, c1_ref)):
                mk = masks[hh]
                delta = jnp.sum(jnp.where(mk, prod, 0.0), axis=1, keepdims=True)
                lse_h = lsev[:, hh * HEAD_DIM:hh * HEAD_DIM + 1]
                qh = jnp.where(mk, q, jnp.zeros_like(q))
                doh = jnp.where(mk, dov, jnp.zeros_like(dov))
                sc = lax.dot_general(qh, k, _DIMS["nt"], preferred_element_type=F32) - c_ref[0, 0:1, :]
                p = jnp.exp(sc - lse_h)
                if diag:
                    r = lax.broadcasted_iota(jnp.int32, (t, t), 0)
                    cc = lax.broadcasted_iota(jnp.int32, (t, t), 1)
                    p = jnp.where(r >= cc, p, 0.0)
                dp = lax.dot_general(doh, v, _DIMS["nt"], preferred_element_type=F32)
                ds = p * (dp - delta)
                dsb = ds.astype(MXU_DTYPE)
                pb = p.astype(MXU_DTYPE)
                dv_blk = jnp.where(mk, dv_blk + lax.dot_general(pb, dov, _DIMS["tn"], preferred_element_type=F32),
                                   dv_blk)
                dk_blk = jnp.where(mk, dk_blk + lax.dot_general(dsb, q, _DIMS["tn"], preferred_element_type=F32),
                                   dk_blk)
                dq_blk = jnp.where(mk, lax.dot_general(dsb, k, _DIMS["nn"], preferred_element_type=F32), dq_blk)
                dc_acc[hh] = dc_acc[hh] - jnp.sum(ds, axis=0, keepdims=True)
            dk_acc[...] = dk_blk
            dv_acc[...] = dv_blk
            rows = pl.ds(pl.multiple_of(qi * t, t), t)
            dq_ref[rows, :] = dq_ref[rows, :] + dq_blk * _SCALE

        @pl.when(qi > ki)
        def _():
            step(False)

        @pl.when(qi == ki)
        def _():
            step(True)

        @pl.when(qi == n - 1)
        def _():
            dk_ref[...] = dk_acc[...].astype(dk_ref.dtype)
            dv_ref[...] = dv_acc[...].astype(dv_ref.dtype)
            dc0_ref[0] = jnp.broadcast_to(dc_acc[0], (8, t))
            dc1_ref[0] = jnp.broadcast_to(dc_acc[1], (8, t))

    q_row = lambda p, ki, qi: jnp.maximum(qi, ki)
    return pl.pallas_call(
        body, name=name, grid=(N_HEADS // 2, n, n),
        in_specs=[
            pl.BlockSpec((t, LANES), lambda p, ki, qi: (q_row(p, ki, qi), qb + p)),
            pl.BlockSpec((t, LANES), lambda p, ki, qi: (ki, kb + p)),
            pl.BlockSpec((t, LANES), lambda p, ki, qi: (ki, vb + p)),
            pl.BlockSpec((1, 8, t), lambda p, ki, qi: (2 * p, 0, ki)),
            pl.BlockSpec((1, 8, t), lambda p, ki, qi: (2 * p + 1, 0, ki)),
            pl.BlockSpec((t, LANES), lambda p, ki, qi: (q_row(p, ki, qi), p)),
            pl.BlockSpec((t, LANES), lambda p, ki, qi: (q_row(p, ki, qi), p)),
            pl.BlockSpec((t, LANES), lambda p, ki, qi: (q_row(p, ki, qi), p)),
        ],
        out_specs=[
            pl.BlockSpec((s, LANES), lambda p, ki, qi: (0, p)),
            pl.BlockSpec((t, LANES), lambda p, ki, qi: (ki, p)),
            pl.BlockSpec((t, LANES), lambda p, ki, qi: (ki, p)),
            pl.BlockSpec((1, 8, t), lambda p, ki, qi: (p, 0, ki)),
            pl.BlockSpec((1, 8, t), lambda p, ki, qi: (p, 0, ki)),
        ],
        out_shape=[jax.ShapeDtypeStruct((s, D_ATT), F32), jax.ShapeDtypeStruct((s, D_ATT), BF16),
                   jax.ShapeDtypeStruct((s, D_ATT), BF16), jax.ShapeDtypeStruct((N_HEADS // 2, 8, s), F32),
                   jax.ShapeDtypeStruct((N_HEADS // 2, 8, s), F32)],
        scratch_shapes=[pltpu.VMEM((t, LANES), F32), pltpu.VMEM((t, LANES), F32), pltpu.VMEM((2, 1, t), F32)],
        compiler_params=_params(("parallel", "arbitrary", "arbitrary")),
    )(h, h, h, c, c, o, lse, do)


def _conv3(z, z_prev, w_ref):
    return (w_ref[2:3, :] * z + w_ref[1:2, :] * _shift_down(z, z_prev, 1)
            + w_ref[0:1, :] * _shift_down(z, z_prev, 2))


def _sconv_fwd(h, w, name):
    s = h.shape[0]
    t = _tile(s, 512)
    r = t // HALO
    c = D_CONV
    b_bg, b_cg, b_hc = OFF_BG // c, OFF_CG // c, OFF_HC // c

    def body(bg_ref, cg_ref, hc_ref, cgp_ref, hcp_ref, w_ref, y_ref):
        i = pl.program_id(0)
        live = (i > 0).astype(F32)
        z = cg_ref[...].astype(F32) * hc_ref[...].astype(F32)
        zp = cgp_ref[...].astype(F32) * hcp_ref[...].astype(F32) * live
        y_ref[...] = (bg_ref[...].astype(F32) * _conv3(z, zp, w_ref)).astype(y_ref.dtype)

    cur = lambda b: pl.BlockSpec((t, c), lambda i: (i, b))
    prev = lambda b: pl.BlockSpec((HALO, c), lambda i: (jnp.maximum(i * r - 1, 0), b))
    return pl.pallas_call(
        body, name=name, grid=(s // t,),
        in_specs=[cur(b_bg), cur(b_cg), cur(b_hc), prev(b_cg), prev(b_hc), pl.BlockSpec((8, c), lambda i: (0, 0))],
        out_specs=pl.BlockSpec((t, c), lambda i: (i, 0)),
        out_shape=jax.ShapeDtypeStruct((s, c), BF16),
        compiler_params=_params(("parallel",)),
    )(h, h, h, h, h, w)


def _sconv_bwd(h, w, dy, name):
    s = h.shape[0]
    t = _tile(s, 512)
    n = s // t
    r = t // HALO
    nh = s // HALO
    c = D_CONV
    b_bg, b_cg, b_hc = OFF_BG // c, OFF_CG // c, OFF_HC // c

    def body(bg_ref, cg_ref, hc_ref, cgp_ref, hcp_ref, bgn_ref, dy_ref, dyn_ref, w_ref, d_ref, dw_ref, acc_ref):
        i = pl.program_id(0)
        has_prev = (i > 0).astype(F32)
        has_next = (i < n - 1).astype(F32)
        bg = bg_ref[...].astype(F32)
        cg = cg_ref[...].astype(F32)
        hc = hc_ref[...].astype(F32)
        dyv = dy_ref[...].astype(F32)
        z = cg * hc
        zp = cgp_ref[...].astype(F32) * hcp_ref[...].astype(F32) * has_prev
        z1 = _shift_down(z, zp, 1)
        z2 = _shift_down(z, zp, 2)
        cz = w_ref[2:3, :] * z + w_ref[1:2, :] * z1 + w_ref[0:1, :] * z2
        dcz = dyv * bg
        dczn = dyn_ref[...].astype(F32) * bgn_ref[...].astype(F32) * has_next
        dz = (w_ref[2:3, :] * dcz + w_ref[1:2, :] * _shift_up(dcz, dczn, 1)
              + w_ref[0:1, :] * _shift_up(dcz, dczn, 2))
        d_ref[:, 0:c] = (dyv * cz).astype(d_ref.dtype)
        d_ref[:, c:2 * c] = (dz * hc).astype(d_ref.dtype)
        d_ref[:, 2 * c:3 * c] = (dz * cg).astype(d_ref.dtype)

        @pl.when(i == 0)
        def _():
            acc_ref[...] = jnp.zeros_like(acc_ref)

        acc_ref[0] += _row_sum8(dcz * z2)
        acc_ref[1] += _row_sum8(dcz * z1)
        acc_ref[2] += _row_sum8(dcz * z)

        @pl.when(i == n - 1)
        def _():
            rows = [jnp.sum(acc_ref[k], axis=0, keepdims=True) for k in range(3)]
            dw_ref[...] = jnp.concatenate(rows + [jnp.zeros((5, c), F32)], axis=0)

    cur = lambda b: pl.BlockSpec((t, c), lambda i: (i, b))
    prev = lambda b: pl.BlockSpec((HALO, c), lambda i: (jnp.maximum(i * r - 1, 0), b))
    nxt = lambda b: pl.BlockSpec((HALO, c), lambda i: (jnp.minimum((i + 1) * r, nh - 1), b))
    return pl.pallas_call(
        body, name=name, grid=(n,),
        in_specs=[cur(b_bg), cur(b_cg), cur(b_hc), prev(b_cg), prev(b_hc), nxt(b_bg),
                  cur(0), nxt(0), pl.BlockSpec((8, c), lambda i: (0, 0))],
        out_specs=[pl.BlockSpec((t, 3 * c), lambda i: (i, 0)), pl.BlockSpec((8, c), lambda i: (0, 0))],
        out_shape=[jax.ShapeDtypeStruct((s, 3 * c), BF16), jax.ShapeDtypeStruct((8, c), F32)],
        scratch_shapes=[pltpu.VMEM((3, 8, c), F32)],
        compiler_params=_params(("arbitrary",)),
    )(h, h, h, h, h, h, dy, dy, w)


def _group_masks():
    lane = lax.broadcasted_iota(jnp.int32, (1, D_SGU), 1)
    return [(lane >= g * HEAD_DIM) & (lane < (g + 1) * HEAD_DIM) for g in range(N_GROUPS)]


def _tril_weights(w_ref):
    r = lax.broadcasted_iota(jnp.int32, (CHUNK, CHUNK), 0)
    c = lax.broadcasted_iota(jnp.int32, (CHUNK, CHUNK), 1)
    return [jnp.where(r >= c, w_ref[g], 0.0).astype(MXU_DTYPE) for g in range(N_GROUPS)]


def _sgu_ln(vs, g_ref, b_ref):
    vg, dvg = _gelu_and_grad(vs)
    mu = jnp.mean(vg, axis=-1, keepdims=True)
    xc = vg - mu
    rstd = lax.rsqrt(jnp.mean(xc * xc, axis=-1, keepdims=True) + LN_EPS)
    xhat = xc * rstd
    return xhat * g_ref[...] + b_ref[...], xhat, rstd, dvg


def _sgu_fwd(h, ln_g, ln_b, w_s, bias, name):
    s = h.shape[0]
    t = _tile(s, 512)
    c = D_SGU
    b_u, b_v = OFF_U // c, OFF_VS // c

    def body(u_ref, v_ref, g_ref, b_ref, w_ref, bias_ref, y_ref):
        gm = _group_masks()
        wm = _tril_weights(w_ref)
        ug = _gelu(u_ref[...].astype(F32))
        vn, _, _, _ = _sgu_ln(v_ref[...].astype(F32), g_ref, b_ref)
        vnb = vn.astype(MXU_DTYPE)
        for ch in range(t // CHUNK):
            rows = slice(ch * CHUNK, (ch + 1) * CHUNK)
            mixed = bias_ref[...]
            for g in range(N_GROUPS):
                mg = lax.dot_general(wm[g], vnb[rows], _DIMS["nn"], preferred_element_type=F32)
                mixed = jnp.where(gm[g], mixed + mg, mixed)
            y_ref[rows, :] = (ug[rows] * mixed).astype(y_ref.dtype)

    full = lambda shp: pl.BlockSpec(shp, lambda i: (0,) * len(shp))
    return pl.pallas_call(
        body, name=name, grid=(s // t,),
        in_specs=[pl.BlockSpec((t, c), lambda i: (i, b_u)), pl.BlockSpec((t, c), lambda i: (i, b_v)),
                  full((1, c)), full((1, c)), full((N_GROUPS, CHUNK, CHUNK)), full((CHUNK, c))],
        out_specs=pl.BlockSpec((t, c), lambda i: (i, 0)),
        out_shape=jax.ShapeDtypeStruct((s, c), BF16),
        compiler_params=_params(("parallel",)),
    )(h, h, ln_g, ln_b, w_s, bias)


def _sgu_bwd(h, ln_g, ln_b, w_s, bias, dy, name):
    s = h.shape[0]
    t = _tile(s, 512)
    n = s // t
    c = D_SGU
    b_u, b_v = OFF_U // c, OFF_VS // c

    def body(u_ref, v_ref, g_ref, b_ref, w_ref, bias_ref, dy_ref,
             d_ref, dg_ref, db_ref, dw_ref, dbias_ref, dg_acc, db_acc):
        i = pl.program_id(0)
        gm = _group_masks()
        wm = _tril_weights(w_ref)

        @pl.when(i == 0)
        def _():
            dg_acc[...] = jnp.zeros_like(dg_acc)
            db_acc[...] = jnp.zeros_like(db_acc)
            dw_ref[...] = jnp.zeros_like(dw_ref)
            dbias_ref[...] = jnp.zeros_like(dbias_ref)

        ug, dug = _gelu_and_grad(u_ref[...].astype(F32))
        vn, xhat, rstd, dvg = _sgu_ln(v_ref[...].astype(F32), g_ref, b_ref)
        vnb = vn.astype(MXU_DTYPE)
        dyv = dy_ref[...].astype(F32)
        dmixed = dyv * ug
        dmb = dmixed.astype(MXU_DTYPE)
        dvn_parts = []
        for ch in range(t // CHUNK):
            rows = slice(ch * CHUNK, (ch + 1) * CHUNK)
            mixed = bias_ref[...]
            dvn = jnp.zeros((CHUNK, c), F32)
            for g in range(N_GROUPS):
                mg = lax.dot_general(wm[g], vnb[rows], _DIMS["nn"], preferred_element_type=F32)
                mixed = jnp.where(gm[g], mixed + mg, mixed)
                dvn = jnp.where(gm[g], lax.dot_general(wm[g], dmb[rows], _DIMS["tn"], preferred_element_type=F32),
                                dvn)
                dmg = jnp.where(gm[g], dmb[rows], jnp.zeros_like(dmb[rows]))
                dw_ref[g] += lax.dot_general(dmg, vnb[rows], _DIMS["nt"], preferred_element_type=F32)
            d_ref[rows, 0:c] = (dyv[rows] * mixed * dug[rows]).astype(d_ref.dtype)
            dbias_ref[...] += dmixed[rows]
            dvn_parts.append(dvn)
        dvn = jnp.concatenate(dvn_parts, axis=0)
        dg_acc[...] += _row_sum8(dvn * xhat)
        db_acc[...] += _row_sum8(dvn)
        dxh = dvn * g_ref[...]
        dvgl = rstd * (dxh - jnp.mean(dxh, axis=-1, keepdims=True)
                       - xhat * jnp.mean(dxh * xhat, axis=-1, keepdims=True))
        d_ref[:, c:2 * c] = (dvgl * dvg).astype(d_ref.dtype)

        @pl.when(i == n - 1)
        def _():
            dg_ref[...] = jnp.sum(dg_acc[...], axis=0, keepdims=True)
            db_ref[...] = jnp.sum(db_acc[...], axis=0, keepdims=True)
            r = lax.broadcasted_iota(jnp.int32, (CHUNK, CHUNK), 0)
            cc = lax.broadcasted_iota(jnp.int32, (CHUNK, CHUNK), 1)
            for g in range(N_GROUPS):
                dw_ref[g] = jnp.where(r >= cc, dw_ref[g], 0.0)

    full = lambda shp: pl.BlockSpec(shp, lambda i: (0,) * len(shp))
    return pl.pallas_call(
        body, name=name, grid=(n,),
        in_specs=[pl.BlockSpec((t, c), lambda i: (i, b_u)), pl.BlockSpec((t, c), lambda i: (i, b_v)),
                  full((1, c)), full((1, c)), full((N_GROUPS, CHUNK, CHUNK)), full((CHUNK, c)),
                  pl.BlockSpec((t, c), lambda i: (i, 0))],
        out_specs=[pl.BlockSpec((t, 2 * c), lambda i: (i, 0)), full((1, c)), full((1, c)),
                   full((N_GROUPS, CHUNK, CHUNK)), full((CHUNK, c))],
        out_shape=[jax.ShapeDtypeStruct((s, 2 * c), BF16), jax.ShapeDtypeStruct((1, c), F32),
                   jax.ShapeDtypeStruct((1, c), F32), jax.ShapeDtypeStruct((N_GROUPS, CHUNK, CHUNK), F32),
                   jax.ShapeDtypeStruct((CHUNK, c), F32)],
        scratch_shapes=[pltpu.VMEM((8, c), F32), pltpu.VMEM((8, c), F32)],
        compiler_params=_params(("arbitrary",)),
    )(h, h, ln_g, ln_b, w_s, bias, dy)


def _merge_fwd(h, acts, ws, b_gate, name):
    s = h.shape[0]
    d = D_MODEL
    t = _tile(s, 512)

    def body(gl0, gl1, gl2, a0, a1, a2, w0, w1, w2, b_ref, o_ref):
        acc = jnp.zeros((t, d), F32)
        for i, (gl, a, w) in enumerate(((gl0, a0, w0), (gl1, a1, w1), (gl2, a2, w2))):
            y = lax.dot_general(a[...], w[...], _DIMS["nn"], preferred_element_type=F32)
            acc = acc + _sigmoid(gl[...].astype(F32) + b_ref[i:i + 1, :]) * y
        o_ref[...] = acc.astype(o_ref.dtype)

    full = lambda arr: pl.BlockSpec(arr.shape, lambda i: (0, 0))
    return pl.pallas_call(
        body, name=name, grid=(s // t,),
        in_specs=[pl.BlockSpec((t, d), lambda i, b=b: (i, b)) for b in range(3)]
                 + [pl.BlockSpec((t, a.shape[1]), lambda i: (i, 0)) for a in acts]
                 + [full(w) for w in ws] + [full(b_gate)],
        out_specs=pl.BlockSpec((t, d), lambda i: (i, 0)),
        out_shape=jax.ShapeDtypeStruct((s, d), BF16),
        compiler_params=_params(("parallel",)),
    )(h, h, h, *acts, *ws, b_gate)


def _merge_bwd(h, acts, ws, b_gate, dmerged, name):
    s = h.shape[0]
    d = D_MODEL
    t = _tile(s, 512)
    n = s // t

    def body(gl0, gl1, gl2, a0, a1, a2, w0, w1, w2, b_ref, dm_ref, dy0, dy1, dy2, dgl_ref, db_ref, acc_ref):
        step = pl.program_id(0)

        @pl.when(step == 0)
        def _():
            acc_ref[...] = jnp.zeros_like(acc_ref)

        dm = dm_ref[...]
        for i, (gl, a, w, dy) in enumerate(((gl0, a0, w0, dy0), (gl1, a1, w1, dy1), (gl2, a2, w2, dy2))):
            y = lax.dot_general(a[...], w[...], _DIMS["nn"], preferred_element_type=F32)
            gate = _sigmoid(gl[...].astype(F32) + b_ref[i:i + 1, :])
            dy[...] = (dm * gate).astype(dy.dtype)
            dgl = dm * y * (gate * (1.0 - gate))
            dgl_ref[:, i * d:(i + 1) * d] = dgl.astype(dgl_ref.dtype)
            acc_ref[i] += _row_sum8(dgl)

        @pl.when(step == n - 1)
        def _():
            rows = [jnp.sum(acc_ref[k], axis=0, keepdims=True) for k in range(3)]
            db_ref[...] = jnp.concatenate(rows + [jnp.zeros((5, d), F32)], axis=0)

    full = lambda arr: pl.BlockSpec(arr.shape, lambda i: (0, 0))
    row = pl.BlockSpec((t, d), lambda i: (i, 0))
    return pl.pallas_call(
        body, name=name, grid=(n,),
        in_specs=[pl.BlockSpec((t, d), lambda i, b=b: (i, b)) for b in range(3)]
                 + [pl.BlockSpec((t, a.shape[1]), lambda i: (i, 0)) for a in acts]
                 + [full(w) for w in ws] + [full(b_gate), row],
        out_specs=[row, row, row, pl.BlockSpec((t, 3 * d), lambda i: (i, 0)), pl.BlockSpec((8, d), lambda i: (0, 0))],
        out_shape=[jax.ShapeDtypeStruct((s, d), BF16)] * 3
                  + [jax.ShapeDtypeStruct((s, 3 * d), BF16), jax.ShapeDtypeStruct((8, d), F32)],
        scratch_shapes=[pltpu.VMEM((3, 8, d), F32)],
        compiler_params=_params(("arbitrary",)),
    )(h, h, h, *acts, *ws, b_gate, dmerged)


FF_BLK = D_FF // 2


def _ffn_act_fwd(h2, w, name):
    s = h2.shape[0]
    t = _tile(s, 512)
    r = t // HALO
    cw = 2 * FF_BLK

    def body(x_ref, xp_ref, w_ref, p_ref):
        i = pl.program_id(0)
        live = (i > 0).astype(F32)
        hc = _conv3(x_ref[...].astype(F32), xp_ref[...].astype(F32) * live, w_ref)
        p_ref[...] = (_gelu(hc[:, :FF_BLK]) * hc[:, FF_BLK:]).astype(p_ref.dtype)

    return pl.pallas_call(
        body, name=name, grid=(s // t, 2),
        in_specs=[pl.BlockSpec((t, cw), lambda i, j: (i, j)),
                  pl.BlockSpec((HALO, cw), lambda i, j: (jnp.maximum(i * r - 1, 0), j)),
                  pl.BlockSpec((8, cw), lambda i, j: (0, j))],
        out_specs=pl.BlockSpec((t, FF_BLK), lambda i, j: (i, j)),
        out_shape=jax.ShapeDtypeStruct((s, D_FF), BF16),
        compiler_params=_params(("parallel", "parallel")),
    )(h2, h2, w)


def _ffn_act_bwd(h2, w, dp, name):
    s = h2.shape[0]
    t = _tile(s, 512)
    r = t // HALO
    cw = 2 * FF_BLK

    def body(x_ref, xp_ref, w_ref, dp_ref, d_ref):
        i = pl.program_id(0)
        live = (i > 0).astype(F32)
        hc = _conv3(x_ref[...].astype(F32), xp_ref[...].astype(F32) * live, w_ref)
        ga, dga = _gelu_and_grad(hc[:, :FF_BLK])
        dpv = dp_ref[...].astype(F32)
        d_ref[:, :FF_BLK] = (dpv * hc[:, FF_BLK:] * dga).astype(d_ref.dtype)
        d_ref[:, FF_BLK:] = (dpv * ga).astype(d_ref.dtype)

    return pl.pallas_call(
        body, name=name, grid=(s // t, 2),
        in_specs=[pl.BlockSpec((t, cw), lambda i, j: (i, j)),
                  pl.BlockSpec((HALO, cw), lambda i, j: (jnp.maximum(i * r - 1, 0), j)),
                  pl.BlockSpec((8, cw), lambda i, j: (0, j)),
                  pl.BlockSpec((t, FF_BLK), lambda i, j: (i, j))],
        out_specs=pl.BlockSpec((t, cw), lambda i, j: (i, j)),
        out_shape=jax.ShapeDtypeStruct((s, 2 * D_FF), BF16),
        compiler_params=_params(("parallel", "parallel")),
    )(h2, h2, w, dp)


def _dwconv_bwd(x, w, dy, name):
    s, c = x.shape
    t = _tile(s, 512)
    n = s // t
    r = t // HALO
    nh = s // HALO
    cw = FF_BLK
    nc = c // cw

    def body(x_ref, xp_ref, dy_ref, dyn_ref, w_ref, dx_ref, dw_ref, acc_ref):
        i = pl.program_id(1)
        has_prev = (i > 0).astype(F32)
        has_next = (i < n - 1).astype(F32)
        xv = x_ref[...].astype(F32)
        xp = xp_ref[...].astype(F32) * has_prev
        dyv = dy_ref[...].astype(F32)
        dyn = dyn_ref[...].astype(F32) * has_next
        dx = (w_ref[2:3, :] * dyv + w_ref[1:2, :] * _shift_up(dyv, dyn, 1)
              + w_ref[0:1, :] * _shift_up(dyv, dyn, 2))
        dx_ref[...] = dx.astype(dx_ref.dtype)

        @pl.when(i == 0)
        def _():
            acc_ref[...] = jnp.zeros_like(acc_ref)

        acc_ref[0] += _row_sum8(dyv * _shift_down(xv, xp, 2))
        acc_ref[1] += _row_sum8(dyv * _shift_down(xv, xp, 1))
        acc_ref[2] += _row_sum8(dyv * xv)

        @pl.when(i == n - 1)
        def _():
            rows = [jnp.sum(acc_ref[k], axis=0, keepdims=True) for k in range(3)]
            dw_ref[...] = jnp.concatenate(rows + [jnp.zeros((5, cw), F32)], axis=0)

    return pl.pallas_call(
        body, name=name, grid=(nc, n),
        in_specs=[pl.BlockSpec((t, cw), lambda j, i: (i, j)),
                  pl.BlockSpec((HALO, cw), lambda j, i: (jnp.maximum(i * r - 1, 0), j)),
                  pl.BlockSpec((t, cw), lambda j, i: (i, j)),
                  pl.BlockSpec((HALO, cw), lambda j, i: (jnp.minimum((i + 1) * r, nh - 1), j)),
                  pl.BlockSpec((8, cw), lambda j, i: (0, j))],
        out_specs=[pl.BlockSpec((t, cw), lambda j, i: (i, j)), pl.BlockSpec((8, cw), lambda j, i: (0, j))],
        out_shape=[jax.ShapeDtypeStruct((s, c), BF16), jax.ShapeDtypeStruct((8, c), F32)],
        scratch_shapes=[pltpu.VMEM((3, 8, cw), F32)],
        compiler_params=_params(("parallel", "arbitrary")),
    )(x, x, dy, dy, w)


def _adamw(w, g, m, v, name):
    shape = w.shape
    c = shape[-1]
    rows = math.prod(shape[:-1])
    to2d = lambda a: a.reshape(rows, c)
    cap = max(8, (1 << 18) // c)
    tr = rows
    for cand in (2048, 1024, 512, 256, 128, 64, 32, 16, 8):
        if cand <= cap and rows % cand == 0:
            tr = cand
            break

    def body(w_ref, g_ref, m_ref, v_ref, d_ref, nm_ref, nv_ref):
        gv = g_ref[...]
        nm = ADAM_B1 * m_ref[...] + (1.0 - ADAM_B1) * gv
        nv = ADAM_B2 * v_ref[...] + (1.0 - ADAM_B2) * (gv * gv)
        m_hat = nm / (1.0 - ADAM_B1 ** ADAM_STEP)
        v_hat = nv / (1.0 - ADAM_B2 ** ADAM_STEP)
        d_ref[...] = -ADAM_LR * (m_hat / (jnp.sqrt(v_hat) + ADAM_EPS) + ADAM_WD * w_ref[...])
        nm_ref[...] = nm
        nv_ref[...] = nv

    blk = pl.BlockSpec((tr, c), lambda i: (i, 0))
    outs = pl.pallas_call(
        body, name=name, grid=(rows // tr,),
        in_specs=[blk] * 4, out_specs=[blk] * 3,
        out_shape=[jax.ShapeDtypeStruct((rows, c), F32)] * 3,
        compiler_params=_params(("parallel",)),
    )(to2d(w), to2d(g), to2d(m), to2d(v))
    return tuple(o.reshape(shape) for o in outs)


_ANY = pl.BlockSpec(memory_space=pl.ANY)


def _place():
    x, y, c = lax.axis_index("x"), lax.axis_index("y"), lax.axis_index("c")
    others = [(1 - x, y), (x, 1 - y), (1 - x, 1 - y)]
    return x, y, c, others


def _all_gather_chips(shard, name):
    rws, cols = shard.shape
    half = rws // 2

    def body(x_ref, out_ref, send_sems, recv_sems, local_sem):
        x, y, c, others = _place()
        me = 2 * x + y
        sib = (x, y, 1 - c)

        def rows(chip, cc):
            return out_ref.at[chip, pl.ds(pl.multiple_of(cc * half, 16), half), :]

        def copy(k, src, dst, to):
            return pltpu.make_async_remote_copy(src_ref=src, dst_ref=dst, send_sem=send_sems.at[k],
                                                recv_sem=recv_sems.at[k], device_id=to, device_id_type=MESH)

        mine = pltpu.make_async_copy(x_ref, out_ref.at[me], local_sem)
        mine.start()
        my_half = x_ref.at[pl.ds(pl.multiple_of(c * half, 16), half), :]
        first = [copy(j, my_half, rows(me, c), (ox, oy, c)) for j, (ox, oy) in enumerate(others)]
        for cp in first:
            cp.start()
        passed = []
        for j, (ox, oy) in enumerate(others):
            blk = rows(2 * ox + oy, c)
            copy(j, blk, blk, (x, y, c)).wait_recv()
            fwd = copy(3 + j, blk, blk, sib)
            fwd.start()
            passed.append(fwd)
        for j, (ox, oy) in enumerate(others):
            blk = rows(2 * ox + oy, 1 - c)
            copy(3 + j, blk, blk, (x, y, c)).wait_recv()
        for cp in first + passed:
            cp.wait_send()
        mine.wait()

    return pl.pallas_call(
        body, name=name,
        in_specs=[_ANY], out_specs=_ANY,
        out_shape=jax.ShapeDtypeStruct((N_CHIPS, rws, cols), shard.dtype),
        scratch_shapes=[pltpu.SemaphoreType.DMA((6,)), pltpu.SemaphoreType.DMA((6,)), pltpu.SemaphoreType.DMA],
        compiler_params=pltpu.CompilerParams(has_side_effects=True),
    )(shard)


def _swap_halves(buf, name):
    nb, rws, cols = buf.shape
    half = rws // 2

    def body(b_ref, own_ref, sib_ref, send_sem, recv_sem, local_sem):
        x, y, c, _ = _place()
        keep = b_ref.at[:, pl.ds(pl.multiple_of(c * half, 16), half), :]
        give = b_ref.at[:, pl.ds(pl.multiple_of((1 - c) * half, 16), half), :]
        mine = pltpu.make_async_copy(keep, own_ref, local_sem)
        mine.start()
        cp = pltpu.make_async_remote_copy(src_ref=give, dst_ref=sib_ref, send_sem=send_sem, recv_sem=recv_sem,
                                          device_id=(x, y, 1 - c), device_id_type=MESH)
        cp.start()
        cp.wait()
        mine.wait()

    shp = jax.ShapeDtypeStruct((nb, half, cols), buf.dtype)
    return pl.pallas_call(
        body, name=name,
        in_specs=[_ANY], out_specs=[_ANY, _ANY], out_shape=[shp, shp],
        scratch_shapes=[pltpu.SemaphoreType.DMA, pltpu.SemaphoreType.DMA, pltpu.SemaphoreType.DMA],
        compiler_params=pltpu.CompilerParams(has_side_effects=True),
    )(buf)


def _add2(a, b, name):
    nb, rws, cols = a.shape
    t = _tile(rws, 256)
    if rws % t:
        t = rws

    def body(a_ref, b_ref, o_ref):
        o_ref[...] = (a_ref[...].astype(F32) + b_ref[...].astype(F32)).astype(o_ref.dtype)

    blk = pl.BlockSpec((1, t, cols), lambda i, j: (i, j, 0))
    return pl.pallas_call(
        body, name=name, grid=(nb, rws // t), in_specs=[blk, blk], out_specs=blk,
        out_shape=jax.ShapeDtypeStruct(a.shape, a.dtype),
        compiler_params=_params(("parallel", "parallel")),
    )(a, b)


def _exchange_chips(pre, name):
    nb, half, cols = pre.shape

    def body(p_ref, out_ref, send_sems, recv_sems, local_sem):
        x, y, c, others = _place()
        me = 2 * x + y
        mine = pltpu.make_async_copy(p_ref.at[me], out_ref.at[me], local_sem)
        mine.start()
        sends = []
        for j, (ox, oy) in enumerate(others):
            cp = pltpu.make_async_remote_copy(src_ref=p_ref.at[2 * ox + oy], dst_ref=out_ref.at[me],
                                              send_sem=send_sems.at[j], recv_sem=recv_sems.at[j],
                                              device_id=(ox, oy, c), device_id_type=MESH)
            cp.start()
            sends.append(cp)
        for j, (ox, oy) in enumerate(others):
            blk = out_ref.at[2 * ox + oy]
            pltpu.make_async_remote_copy(src_ref=blk, dst_ref=blk, send_sem=send_sems.at[j],
                                         recv_sem=recv_sems.at[j], device_id=(x, y, c),
                                         device_id_type=MESH).wait_recv()
        for cp in sends:
            cp.wait_send()
        mine.wait()

    return pl.pallas_call(
        body, name=name,
        in_specs=[_ANY], out_specs=_ANY, out_shape=jax.ShapeDtypeStruct(pre.shape, pre.dtype),
        scratch_shapes=[pltpu.SemaphoreType.DMA((3,)), pltpu.SemaphoreType.DMA((3,)), pltpu.SemaphoreType.DMA],
        compiler_params=pltpu.CompilerParams(has_side_effects=True),
    )(pre)


def _add4(parts, name):
    nb, half, cols = parts.shape
    t = _tile(half, 256)
    if half % t:
        t = half

    def body(p_ref, o_ref):
        acc = p_ref[0].astype(F32)
        for k in range(1, nb):
            acc = acc + p_ref[k].astype(F32)
        o_ref[...] = acc

    return pl.pallas_call(
        body, name=name, grid=(half // t,),
        in_specs=[pl.BlockSpec((nb, t, cols), lambda i: (0, i, 0))],
        out_specs=pl.BlockSpec((t, cols), lambda i: (i, 0)),
        out_shape=jax.ShapeDtypeStruct((half, cols), F32),
        compiler_params=_params(("parallel",)),
    )(parts)


def _join_halves(mine_half, name):
    half, cols = mine_half.shape

    def body(h_ref, out_ref, send_sem, recv_sem, local_sem):
        x, y, c, _ = _place()
        dst = out_ref.at[pl.ds(pl.multiple_of(c * half, 8), half), :]
        mine = pltpu.make_async_copy(h_ref, dst, local_sem)
        mine.start()
        cp = pltpu.make_async_remote_copy(src_ref=h_ref, dst_ref=dst, send_sem=send_sem, recv_sem=recv_sem,
                                          device_id=(x, y, 1 - c), device_id_type=MESH)
        cp.start()
        cp.wait()
        mine.wait()

    return pl.pallas_call(
        body, name=name,
        in_specs=[_ANY], out_specs=_ANY, out_shape=jax.ShapeDtypeStruct((2 * half, cols), mine_half.dtype),
        scratch_shapes=[pltpu.SemaphoreType.DMA, pltpu.SemaphoreType.DMA, pltpu.SemaphoreType.DMA],
        compiler_params=pltpu.CompilerParams(has_side_effects=True),
    )(mine_half)


def _reduce_scatter_chips(buf, tag):
    own, sib = _swap_halves(buf, "rs_swap_" + tag)
    pre = _add2(own, sib, "rs_add2_" + tag)
    parts = _exchange_chips(pre, "rs_xchg_" + tag)
    red = _add4(parts, "rs_add4_" + tag)
    return _join_halves(red, "rs_join_" + tag)


def _pack_rows(pieces, rows, dtype):
    flat = jnp.concatenate([p.astype(dtype).reshape(-1) for p in pieces])
    return jnp.pad(flat, (0, rows * PACK_COLS - flat.shape[0])).reshape(rows, PACK_COLS)


def _unpack(flat, shapes):
    out, off = [], 0
    for shp in shapes:
        size = math.prod(shp)
        out.append(flat[off:off + size].reshape(shp))
        off += size
    return out


def _rows_for(n_elems, mult):
    rows = -(-n_elems // PACK_COLS)
    return -(-rows // mult) * mult


BIG_SHARDS = [("w_in", (D_MODEL, 1474)), ("w_branch_att", (D_ATT, 256)), ("w_branch_conv", (D_CONV, 256)),
              ("w_branch_sgu", (D_SGU, 256)), ("w_out", (256, D_MODEL)), ("w_ffn_up", (D_MODEL, FF_BLK)),
              ("w_ffn_down", (D_FF // N_CHIPS, D_MODEL))]
SMALL_SHARDS = [("b_gate", (3, 256)), ("conv_mix_w", (3, 64)), ("conv_ffn_w", (3, FF_BLK))]
REPLICATED = [("pre_mix_g", (D_MODEL,)), ("post_mix_g", (D_MODEL,)), ("pre_ffn_g", (D_MODEL,)),
              ("post_ffn_g", (D_MODEL,)), ("b_forget", (N_HEADS,)), ("sgu_ln_g", (D_SGU,)), ("sgu_ln_b", (D_SGU,)),
              ("sgu_w", (N_GROUPS, CHUNK, CHUNK)), ("sgu_b", (N_GROUPS, CHUNK))]
WEIGHT_ORDER = ["pre_mix_g", "post_mix_g", "pre_ffn_g", "post_ffn_g", "w_in", "b_forget", "b_gate", "conv_mix_w",
                "sgu_ln_g", "sgu_ln_b", "sgu_w", "sgu_b", "w_branch_att", "w_branch_conv", "w_branch_sgu", "w_out",
                "w_ffn_up", "conv_ffn_w", "w_ffn_down"]

_BIG_ELEMS = sum(math.prod(s) for _, s in BIG_SHARDS)
_SMALL_ELEMS = sum(math.prod(s) for _, s in SMALL_SHARDS)
_REP_ELEMS = sum(math.prod(s) for _, s in REPLICATED)
_REP_QUARTER = -(-(DEPTH * _REP_ELEMS) // N_CHIPS)
GATHER_ROWS = _rows_for(_BIG_ELEMS + 2 * _SMALL_ELEMS, 32)
GRAD_ROWS = _rows_for(_BIG_ELEMS, 32)
SMALL_ROWS = _rows_for(DEPTH * _SMALL_ELEMS + _REP_QUARTER, 32)


def _gather_layer(wts, l):
    big = [wts[n][l].astype(BF16) for n, _ in BIG_SHARDS]
    small = [lax.bitcast_convert_type(wts[n][l], BF16) for n, _ in SMALL_SHARDS]
    shard = _pack_rows(big + small, GATHER_ROWS, BF16)
    full = _all_gather_chips(shard, "gather_weights").reshape(N_CHIPS, -1)
    per_chip = []
    for j in range(N_CHIPS):
        bigs = _unpack(full[j], [s for _, s in BIG_SHARDS])
        smalls = _unpack(full[j, _BIG_ELEMS:], [s + (2,) for _, s in SMALL_SHARDS])
        smalls = [lax.bitcast_convert_type(a, F32) for a in smalls]
        per_chip.append(bigs + smalls)
    names = [n for n, _ in BIG_SHARDS + SMALL_SHARDS]
    out = {}
    for i, n in enumerate(names):
        axis = 0 if n in ("w_out", "w_ffn_down") else 1
        out[n] = jnp.concatenate([per_chip[j][i] for j in range(N_CHIPS)], axis=axis)
    return out


def _pad_rows(a, rows):
    return jnp.pad(a, ((0, rows - a.shape[0]), (0, 0)))


def _prep_layer(wts, full, l):
    w_in = full["w_in"]
    o_f = 3 * D_ATT
    o_b = o_f + N_HEADS
    o_gl = o_b + 3 * D_CONV + 2 * D_SGU
    up = full["w_ffn_up"]
    cf = full["conv_ffn_w"]
    blk = lambda a, j: a[:, j * FF_BLK:(j + 1) * FF_BLK]
    return {
        "w_p": jnp.concatenate([w_in[:, o_gl:], w_in[:, :o_f], w_in[:, o_b:o_gl]], axis=1),
        "wf_t": _pad_rows(w_in[:, o_f:o_b].T, F_ROWS),
        "b_forget": _pad_rows(wts["b_forget"][l].reshape(N_HEADS, 1), F_ROWS),
        "b_gate": _pad_rows(full["b_gate"], 8),
        "conv_mix_w": _pad_rows(full["conv_mix_w"], 8),
        "w_att": full["w_branch_att"], "w_conv": full["w_branch_conv"], "w_sgu": full["w_branch_sgu"],
        "w_out": full["w_out"],
        "w_up": jnp.concatenate([blk(up, 0), blk(up, 2), blk(up, 1), blk(up, 3)], axis=1),
        "conv_ffn_w": _pad_rows(jnp.concatenate([blk(cf, 0), blk(cf, 2), blk(cf, 1), blk(cf, 3)], axis=1), 8),
        "w_down": full["w_ffn_down"],
        "pre_mix_g": wts["pre_mix_g"][l].reshape(1, -1), "post_mix_g": wts["post_mix_g"][l].reshape(1, -1),
        "pre_ffn_g": wts["pre_ffn_g"][l].reshape(1, -1), "post_ffn_g": wts["post_ffn_g"][l].reshape(1, -1),
        "ln_g": wts["sgu_ln_g"][l].reshape(1, -1), "ln_b": wts["sgu_ln_b"][l].reshape(1, -1),
        "sgu_w": wts["sgu_w"][l],
        "sgu_bias": jnp.repeat(wts["sgu_b"][l].T, HEAD_DIM, axis=1),
    }


def _layer_fwd(x, p):
    s = x.shape[0]
    xn = _rms_fwd(x, p["pre_mix_g"], "rms_pre_mix")
    h = _mm(xn, p["w_p"], "nn", BF16, "mm_in", s, 256, D_MODEL)
    f_row = _mm(p["wf_t"], xn, "nt", F32, "mm_forget", F_ROWS, 2048, D_MODEL)
    c = _gate_fwd(f_row, p["b_forget"], "gate_fwd")
    o, o_f32, lse = _attn_fwd(h, c, "attn_fwd")
    yc = _sconv_fwd(h, p["conv_mix_w"], "sconv_fwd")
    ys = _sgu_fwd(h, p["ln_g"], p["ln_b"], p["sgu_w"], p["sgu_bias"], "sgu_fwd")
    merged = _merge_fwd(h, (o, yc, ys), (p["w_att"], p["w_conv"], p["w_sgu"]), p["b_gate"], "merge_fwd")
    mo = _mm(merged, p["w_out"], "nn", F32, "mm_out", 2048, 512, D_MODEL)
    x1 = _resid_post(x, mo, p["post_mix_g"], "post_mix")
    xn2 = _rms_fwd(x1, p["pre_ffn_g"], "rms_pre_ffn")
    h2 = _mm(xn2, p["w_up"], "nn", BF16, "mm_up", 2048, 512, D_MODEL)
    pact = _ffn_act_fwd(h2, p["conv_ffn_w"], "ffn_act_fwd")
    ff = _mm(pact, p["w_down"], "nn", F32, "mm_down", 2048, 512, FF_BLK)
    x2 = _resid_post(x1, ff, p["post_ffn_g"], "post_ffn")
    saved = dict(x=x, xn=xn, h=h, f_row=f_row, c=c, o=o, o_f32=o_f32, lse=lse, yc=yc, ys=ys, merged=merged, mo=mo, x1=x1,
                 xn2=xn2, h2=h2, pact=pact, ff=ff)
    return x2, saved


def _layer_bwd(dx2, p, sv):
    s = dx2.shape[0]
    g = {}
    dff, g["post_ffn_g"] = _rms_bwd(sv["ff"], p["post_ffn_g"], [dx2], None, BF16, "post_ffn_bwd")
    dpact = _mm(dff, p["w_down"], "nt", BF16, "mm_down_dx", 1024, FF_BLK, D_MODEL)
    g["w_ffn_down"] = _mm(sv["pact"], dff, "tn", F32, "mm_down_dw", 256, D_MODEL, s)
    dhc2 = _ffn_act_bwd(sv["h2"], p["conv_ffn_w"], dpact, "ffn_act_bwd")
    dh2, dconv_ffn = _dwconv_bwd(sv["h2"], p["conv_ffn_w"], dhc2, "ffn_conv_bwd")
    dxn2 = _mm(dh2, p["w_up"], "nt", F32, "mm_up_dx", 1024, D_MODEL, FF_BLK)
    dw_up = _mm(sv["xn2"], dh2, "tn", F32, "mm_up_dw", D_MODEL, 512, s)
    dx1, g["pre_ffn_g"] = _rms_bwd(sv["x1"], p["pre_ffn_g"], [dxn2], dx2, F32, "pre_ffn_bwd")
    dmo, g["post_mix_g"] = _rms_bwd(sv["mo"], p["post_mix_g"], [dx1], None, BF16, "post_mix_bwd")
    dmerged = _mm(dmo, p["w_out"], "nt", F32, "mm_out_dx", 2048, 512, D_MODEL)
    g["w_out"] = _mm(sv["merged"], dmo, "tn", F32, "mm_out_dw", 512, D_MODEL, s)
    acts = (sv["o"], sv["yc"], sv["ys"])
    ws = (p["w_att"], p["w_conv"], p["w_sgu"])
    dy_a, dy_c, dy_s, dgl, db_gate = _merge_bwd(sv["h"], acts, ws, p["b_gate"], dmerged, "merge_bwd")
    do = _mm(dy_a, p["w_att"], "nt", BF16, "mm_att_dx", 2048, D_ATT, D_MODEL)
    dyc = _mm(dy_c, p["w_conv"], "nt", BF16, "mm_conv_dx", 2048, D_CONV, D_MODEL)
    dys = _mm(dy_s, p["w_sgu"], "nt", BF16, "mm_sgu_dx", 2048, D_SGU, D_MODEL)
    g["w_branch_att"] = _mm(sv["o"], dy_a, "tn", F32, "mm_att_dw", D_ATT, D_MODEL, s)
    g["w_branch_conv"] = _mm(sv["yc"], dy_c, "tn", F32, "mm_conv_dw", D_CONV, D_MODEL, s)
    g["w_branch_sgu"] = _mm(sv["ys"], dy_s, "tn", F32, "mm_sgu_dw", D_SGU, D_MODEL, s)
    d_conv, dconv_mix = _sconv_bwd(sv["h"], p["conv_mix_w"], dyc, "sconv_bwd")
    d_sgu, g["sgu_ln_g"], g["sgu_ln_b"], g["sgu_w"], dbias = _sgu_bwd(
        sv["h"], p["ln_g"], p["ln_b"], p["sgu_w"], p["sgu_bias"], dys, "sgu_bwd")
    dq, dk, dv, dc_even, dc_odd = _attn_bwd(sv["h"], sv["c"], sv["o_f32"], sv["lse"], do, "attn_bwd")
    df, db_forget = _gate_bwd(sv["f_row"], p["b_forget"], dc_even, dc_odd, "gate_bwd")
    dh = jnp.concatenate([dgl, dq.astype(BF16), dk, dv, d_conv, d_sgu], axis=1)
    dxn = _mm(dh, p["w_p"], "nt", F32, "mm_in_dx", 512, D_MODEL, W_P // 2)
    dxn_f = _mm(df, p["wf_t"], "tn", F32, "mm_forget_dx", 2048, D_MODEL, F_ROWS)
    dw_p = _mm(sv["xn"], dh, "tn", F32, "mm_in_dw", D_MODEL, 256, s)
    dwf_t = _mm(df, sv["xn"], "nn", F32, "mm_forget_dw", F_ROWS, D_MODEL, s)
    dx, g["pre_mix_g"] = _rms_bwd(sv["x"], p["pre_mix_g"], [dxn, dxn_f], dx1, F32, "pre_mix_bwd")
    g["w_in"] = jnp.concatenate([dw_p[:, OFF_Q:OFF_BG], dwf_t[:N_HEADS].T, dw_p[:, OFF_BG:], dw_p[:, :OFF_Q]], axis=1)
    blk = lambda a, j: a[:, j * FF_BLK:(j + 1) * FF_BLK]
    g["w_ffn_up"] = jnp.concatenate([blk(dw_up, 0), blk(dw_up, 2), blk(dw_up, 1), blk(dw_up, 3)], axis=1)
    g["conv_ffn_w"] = jnp.concatenate([blk(dconv_ffn, 0), blk(dconv_ffn, 2), blk(dconv_ffn, 1),
                                       blk(dconv_ffn, 3)], axis=1)[:3]
    g["conv_mix_w"] = dconv_mix[:3]
    g["b_gate"] = db_gate[:3]
    g["b_forget"] = db_forget[:N_HEADS, 0]
    g["sgu_b"] = jnp.sum(dbias.reshape(CHUNK, N_GROUPS, HEAD_DIM), axis=-1).T
    for n in ("pre_mix_g", "post_mix_g", "pre_ffn_g", "post_ffn_g", "sgu_ln_g", "sgu_ln_b"):
        g[n] = g[n].reshape(-1)
    return dx, g


def _shard_cols(a, j):
    w = a.shape[-1] // N_CHIPS
    return a[..., j * w:(j + 1) * w]


def _shard_rows(a, j):
    w = a.shape[0] // N_CHIPS
    return a[j * w:(j + 1) * w]


def _grad_shard(name, g, j):
    return _shard_rows(g, j) if name in ("w_out", "w_ffn_down") else _shard_cols(g, j)


def kernel(x, pre_mix_g, post_mix_g, pre_ffn_g, post_ffn_g, w_in, b_forget, b_gate, conv_mix_w, sgu_ln_g, sgu_ln_b, sgu_w, sgu_b, w_branch_att, w_branch_conv, w_branch_sgu, w_out, w_ffn_up, conv_ffn_w, w_ffn_down, loss_target, m_pre_mix_g, m_post_mix_g, m_pre_ffn_g, m_post_ffn_g, m_w_in, m_b_forget, m_b_gate, m_conv_mix_w, m_sgu_ln_g, m_sgu_ln_b, m_sgu_w, m_sgu_b, m_w_branch_att, m_w_branch_conv, m_w_branch_sgu, m_w_out, m_w_ffn_up, m_conv_ffn_w, m_w_ffn_down, v_pre_mix_g, v_post_mix_g, v_pre_ffn_g, v_post_ffn_g, v_w_in, v_b_forget, v_b_gate, v_conv_mix_w, v_sgu_ln_g, v_sgu_ln_b, v_sgu_w, v_sgu_b, v_w_branch_att, v_w_branch_conv, v_w_branch_sgu, v_w_out, v_w_ffn_up, v_conv_ffn_w, v_w_ffn_down):
    wts = dict(pre_mix_g=pre_mix_g, post_mix_g=post_mix_g, pre_ffn_g=pre_ffn_g, post_ffn_g=post_ffn_g, w_in=w_in,
               b_forget=b_forget, b_gate=b_gate, conv_mix_w=conv_mix_w, sgu_ln_g=sgu_ln_g, sgu_ln_b=sgu_ln_b,
               sgu_w=sgu_w, sgu_b=sgu_b, w_branch_att=w_branch_att, w_branch_conv=w_branch_conv,
               w_branch_sgu=w_branch_sgu, w_out=w_out, w_ffn_up=w_ffn_up, conv_ffn_w=conv_ffn_w,
               w_ffn_down=w_ffn_down)
    moms = dict(pre_mix_g=m_pre_mix_g, post_mix_g=m_post_mix_g, pre_ffn_g=m_pre_ffn_g, post_ffn_g=m_post_ffn_g,
                w_in=m_w_in, b_forget=m_b_forget, b_gate=m_b_gate, conv_mix_w=m_conv_mix_w, sgu_ln_g=m_sgu_ln_g,
                sgu_ln_b=m_sgu_ln_b, sgu_w=m_sgu_w, sgu_b=m_sgu_b, w_branch_att=m_w_branch_att,
                w_branch_conv=m_w_branch_conv, w_branch_sgu=m_w_branch_sgu, w_out=m_w_out, w_ffn_up=m_w_ffn_up,
                conv_ffn_w=m_conv_ffn_w, w_ffn_down=m_w_ffn_down)
    vels = dict(pre_mix_g=v_pre_mix_g, post_mix_g=v_post_mix_g, pre_ffn_g=v_pre_ffn_g, post_ffn_g=v_post_ffn_g,
                w_in=v_w_in, b_forget=v_b_forget, b_gate=v_b_gate, conv_mix_w=v_conv_mix_w, sgu_ln_g=v_sgu_ln_g,
                sgu_ln_b=v_sgu_ln_b, sgu_w=v_sgu_w, sgu_b=v_sgu_b, w_branch_att=v_w_branch_att,
                w_branch_conv=v_w_branch_conv, w_branch_sgu=v_w_branch_sgu, w_out=v_w_out, w_ffn_up=v_w_ffn_up,
                conv_ffn_w=v_conv_ffn_w, w_ffn_down=v_w_ffn_down)

    xs = x[0]
    layers, saved = [], []
    for l in range(DEPTH):
        p = _prep_layer(wts, _gather_layer(wts, l), l)
        xs, sv = _layer_fwd(xs, p)
        layers.append(p)
        saved.append(sv)
    dy, loss_part = _loss_head(xs, loss_target[0], "loss_head")
    loss = lax.psum(loss_part[0, 0], ("x", "y", "c"))

    big_red = [None] * DEPTH
    small_grads = [None] * DEPTH
    for l in reversed(range(DEPTH)):
        dy, g = _layer_bwd(dy, layers[l], saved[l])
        rows = [_pack_rows([_grad_shard(n, g[n], j) for n, _ in BIG_SHARDS], GRAD_ROWS, BF16)
                for j in range(N_CHIPS)]
        big_red[l] = _reduce_scatter_chips(jnp.stack(rows), "big")
        small_grads[l] = g
    grad_x = dy[None]

    rep_flat = jnp.concatenate([small_grads[l][n].reshape(-1) for l in range(DEPTH) for n, _ in REPLICATED])
    rep_flat = jnp.pad(rep_flat, (0, N_CHIPS * _REP_QUARTER - rep_flat.shape[0]))
    rows = []
    for j in range(N_CHIPS):
        pieces = [_grad_shard(n, small_grads[l][n], j) for l in range(DEPTH) for n, _ in SMALL_SHARDS]
        pieces.append(rep_flat[j * _REP_QUARTER:(j + 1) * _REP_QUARTER])
        rows.append(_pack_rows(pieces, SMALL_ROWS, F32))
    small_red = _reduce_scatter_chips(jnp.stack(rows), "small")
    small_all = _all_gather_chips(small_red, "gather_small").reshape(N_CHIPS, -1)

    grads = {}
    big_parts = [_unpack(big_red[l].reshape(-1), [s for _, s in BIG_SHARDS]) for l in range(DEPTH)]
    for i, (n, _) in enumerate(BIG_SHARDS):
        grads[n] = jnp.stack([big_parts[l][i] for l in range(DEPTH)])
    mine_small = small_red.reshape(-1)
    parts = _unpack(mine_small, [s for _ in range(DEPTH) for _, s in SMALL_SHARDS])
    for i, (n, _) in enumerate(SMALL_SHARDS):
        grads[n] = jnp.stack([parts[l * len(SMALL_SHARDS) + i] for l in range(DEPTH)])
    off = DEPTH * _SMALL_ELEMS
    rep_all = jnp.concatenate([small_all[j, off:off + _REP_QUARTER] for j in range(N_CHIPS)])
    parts = _unpack(rep_all, [s for _ in range(DEPTH) for _, s in REPLICATED])
    for i, (n, _) in enumerate(REPLICATED):
        grads[n] = jnp.stack([parts[l * len(REPLICATED) + i] for l in range(DEPTH)])

    deltas, new_m, new_v = {}, {}, {}
    for n in WEIGHT_ORDER:
        deltas[n], new_m[n], new_v[n] = _adamw(wts[n], grads[n], moms[n], vels[n], "adamw_" + n)
    return (loss, grad_x, *[grads[n] for n in WEIGHT_ORDER], *[deltas[n] for n in WEIGHT_ORDER],
            *[new_m[n] for n in WEIGHT_ORDER], *[new_v[n] for n in WEIGHT_ORDER])
```

```python
import functools
import math

import jax
import jax.numpy as jnp
from jax import lax
from jax.experimental import pallas as pl
from jax.experimental.pallas import tpu as pltpu

F32 = jnp.float32
BF16 = jnp.bfloat16
MXU_DTYPE = jnp.bfloat16

D_MODEL = 1024
HEAD_DIM = 64
N_HEADS = 8
D_ATT = 512
D_CONV = 256
D_SGU = 256
N_GROUPS = 4
CHUNK = 128
D_FF = 2816
DEPTH = 4
RMS_EPS = 1e-6
LN_EPS = 1e-5
N_CHIPS = 4
LANES = 128
PACK_COLS = 1024
HALO = 16

ADAM_LR = 0.001
ADAM_B1 = 0.9
ADAM_B2 = 0.999
ADAM_EPS = 1e-08
ADAM_WD = 0.01
ADAM_STEP = 10

OFF_GL = 0
OFF_Q = 3 * D_MODEL
OFF_K = OFF_Q + D_ATT
OFF_V = OFF_K + D_ATT
OFF_BG = OFF_V + D_ATT
OFF_CG = OFF_BG + D_CONV
OFF_HC = OFF_CG + D_CONV
OFF_U = OFF_HC + D_CONV
OFF_VS = OFF_U + D_SGU
W_P = OFF_VS + D_SGU
F_ROWS = 16

VMEM_LIMIT = 56 * 1024 * 1024
MESH = pl.DeviceIdType.MESH


def _params(sem=None):
    if sem is None:
        return pltpu.CompilerParams(vmem_limit_bytes=VMEM_LIMIT)
    return pltpu.CompilerParams(dimension_semantics=sem, vmem_limit_bytes=VMEM_LIMIT)


def _tile(dim, pref):
    if dim <= pref:
        return dim
    if dim % pref == 0:
        return pref
    return dim


_DIMS = {"nn": (((1,), (0,)), ((), ())), "nt": (((1,), (1,)), ((), ())), "tn": (((0,), (0,)), ((), ()))}


def _mm(a, b, mode, out_dtype, name, tm, tn, tk, chip_of=None):
    if mode == "tn":
        K, M = a.shape
    else:
        M, K = a.shape
    N = b.shape[0] if mode == "nt" else b.shape[1]
    tm, tn, tk = _tile(M, tm), _tile(N // N_CHIPS if chip_of else N, tn), _tile(K, tk)
    nk = K // tk
    dims = _DIMS[mode]

    def body(a_ref, b_ref, o_ref, *acc):
        part = lax.dot_general(a_ref[...].astype(MXU_DTYPE), b_ref[...].astype(MXU_DTYPE), dims,
                               preferred_element_type=F32)
        if nk == 1:
            o_ref[...] = part.astype(o_ref.dtype)
        else:
            acc_ref = acc[0]
            k = pl.program_id(2)

            @pl.when(k == 0)
            def _():
                acc_ref[...] = part

            @pl.when(k > 0)
            def _():
                acc_ref[...] += part

            @pl.when(k == nk - 1)
            def _():
                o_ref[...] = acc_ref[...].astype(o_ref.dtype)

    if mode == "tn":
        a_spec = pl.BlockSpec((tk, tm), lambda i, j, k: (k, i))
    else:
        a_spec = pl.BlockSpec((tm, tk), lambda i, j, k: (i, k))
    if mode == "nt":
        b_spec = pl.BlockSpec((tn, tk), lambda i, j, k: (j, k))
    else:
        b_spec = pl.BlockSpec((tk, tn), lambda i, j, k: (k, j))
    if chip_of is None:
        out_spec = pl.BlockSpec((tm, tn), lambda i, j, k: (i, j))
        out_shape = jax.ShapeDtypeStruct((M, N), out_dtype)
    else:
        per = (N // N_CHIPS) // tn
        out_spec = pl.BlockSpec((None, tm, tn), lambda i, j, k: (chip_of(j // per), i, j % per))
        out_shape = jax.ShapeDtypeStruct((N_CHIPS, M, N // N_CHIPS), out_dtype)
    return pl.pallas_call(
        body,
        name=name,
        grid=(M // tm, N // tn, nk),
        in_specs=[a_spec, b_spec],
        out_specs=out_spec,
        out_shape=out_shape,
        scratch_shapes=[pltpu.VMEM((tm, tn), F32)] if nk > 1 else [],
        compiler_params=_params(("parallel", "parallel", "arbitrary")),
    )(a, b)


_GELU_K = math.sqrt(2.0 / math.pi)
_GELU_C = 0.044715


def _gelu(x):
    t = jnp.tanh(_GELU_K * (x + _GELU_C * (x * x * x)))
    return x * (0.5 * (1.0 + t))


def _gelu_and_grad(x):
    x2 = x * x
    t = jnp.tanh(_GELU_K * (x + _GELU_C * (x2 * x)))
    cdf = 0.5 * (1.0 + t)
    dcdf = 0.5 * (1.0 - t * t) * (_GELU_K * (1.0 + 3.0 * _GELU_C * x2))
    return x * cdf, cdf + x * dcdf


def _sigmoid(x):
    return 1.0 / (1.0 + jnp.exp(-x))


def _shift_down(cur, prev, k):
    h = prev.shape[0]
    ext = jnp.concatenate([prev, cur], axis=0)
    return pltpu.roll(ext, k, 0)[h:]


def _shift_up(cur, nxt, k):
    t, h = cur.shape[0], nxt.shape[0]
    ext = jnp.concatenate([cur, nxt], axis=0)
    return pltpu.roll(ext, t + h - k, 0)[:t]


def _row_sum8(x):
    t, c = x.shape
    return jnp.sum(x.reshape(t // 8, 8, c), axis=0)


def _rms_fwd(x, g, name):
    s, d = x.shape
    t = _tile(s, 512)

    def body(x_ref, g_ref, o_ref):
        xv = x_ref[...]
        r = lax.rsqrt(jnp.mean(xv * xv, axis=-1, keepdims=True) + RMS_EPS)
        o_ref[...] = (xv * r * g_ref[...]).astype(o_ref.dtype)

    return pl.pallas_call(
        body, name=name, grid=(s // t,),
        in_specs=[pl.BlockSpec((t, d), lambda i: (i, 0)), pl.BlockSpec((1, d), lambda i: (0, 0))],
        out_specs=pl.BlockSpec((t, d), lambda i: (i, 0)),
        out_shape=jax.ShapeDtypeStruct((s, d), BF16),
        compiler_params=_params(("parallel",)),
    )(x, g)


def _resid_post(x, y, g, name):
    s, d = x.shape
    t = _tile(s, 512)

    def body(x_ref, y_ref, g_ref, o_ref):
        yv = y_ref[...]
        r = lax.rsqrt(jnp.mean(yv * yv, axis=-1, keepdims=True) + RMS_EPS)
        o_ref[...] = x_ref[...] + yv * r * g_ref[...]

    row = pl.BlockSpec((t, d), lambda i: (i, 0))
    return pl.pallas_call(
        body, name=name, grid=(s // t,),
        in_specs=[row, row, pl.BlockSpec((1, d), lambda i: (0, 0))],
        out_specs=row,
        out_shape=jax.ShapeDtypeStruct((s, d), F32),
        compiler_params=_params(("parallel",)),
    )(x, y, g)


def _rms_bwd(xin, g, dys, dres, out_dtype, name):
    s, d = xin.shape
    t = _tile(s, 512)
    n = s // t
    n_dy = len(dys)
    has_res = dres is not None

    def body(*refs):
        x_ref, g_ref = refs[0], refs[1]
        dy_refs = refs[2:2 + n_dy]
        pos = 2 + n_dy
        res_ref = refs[pos] if has_res else None
        pos += 1 if has_res else 0
        dx_ref, dg_ref, acc_ref = refs[pos], refs[pos + 1], refs[pos + 2]
        i = pl.program_id(0)
        xv = x_ref[...]
        dy = dy_refs[0][...].astype(F32)
        for extra in dy_refs[1:]:
            dy = dy + extra[...].astype(F32)
        r = lax.rsqrt(jnp.mean(xv * xv, axis=-1, keepdims=True) + RMS_EPS)
        u = dy * g_ref[...]
        xr = xv * r
        dx = r * (u - xr * jnp.mean(u * xr, axis=-1, keepdims=True))
        if has_res:
            dx = dx + res_ref[...]
        dx_ref[...] = dx.astype(dx_ref.dtype)
        part = _row_sum8(dy * xr)

        @pl.when(i == 0)
        def _():
            acc_ref[...] = part

        @pl.when(i > 0)
        def _():
            acc_ref[...] += part

        @pl.when(i == n - 1)
        def _():
            dg_ref[...] = jnp.sum(acc_ref[...], axis=0, keepdims=True)

    row = pl.BlockSpec((t, d), lambda i: (i, 0))
    vec = pl.BlockSpec((1, d), lambda i: (0, 0))
    ins = [xin, g, *dys] + ([dres] if has_res else [])
    return pl.pallas_call(
        body, name=name, grid=(n,),
        in_specs=[row, vec] + [row] * (n_dy + (1 if has_res else 0)),
        out_specs=[row, vec],
        out_shape=[jax.ShapeDtypeStruct((s, d), out_dtype), jax.ShapeDtypeStruct((1, d), F32)],
        scratch_shapes=[pltpu.VMEM((8, d), F32)],
        compiler_params=_params(("arbitrary",)),
    )(*ins)


def _loss_head(y, target, name):
    s, d = y.shape
    t = _tile(s, 512)
    n = s // t

    def body(y_ref, t_ref, dy_ref, loss_ref, acc_ref):
        i = pl.program_id(0)
        e = y_ref[...] - t_ref[...]
        dy_ref[...] = e * (1.0 / d)
        part = _row_sum8(e * e)

        @pl.when(i == 0)
        def _():
            acc_ref[...] = part

        @pl.when(i > 0)
        def _():
            acc_ref[...] += part

        @pl.when(i == n - 1)
        def _():
            tot = jnp.sum(jnp.sum(acc_ref[...], axis=0, keepdims=True), axis=1, keepdims=True)
            loss_ref[...] = tot * (0.5 / d)

    row = pl.BlockSpec((t, d), lambda i: (i, 0))
    return pl.pallas_call(
        body, name=name, grid=(n,),
        in_specs=[row, row],
        out_specs=[row, pl.BlockSpec((1, 1), lambda i: (0, 0))],
        out_shape=[jax.ShapeDtypeStruct((s, d), F32), jax.ShapeDtypeStruct((1, 1), F32)],
        scratch_shapes=[pltpu.VMEM((8, d), F32)],
        compiler_params=_params(("arbitrary",)),
    )(y, target)


def _split3(x):
    hi = x.astype(BF16)
    r1 = x - hi.astype(F32)
    mid = r1.astype(BF16)
    lo = (r1 - mid.astype(F32)).astype(BF16)
    return hi, mid, lo


def _tri_dot(x, tri):
    hi, mid, lo = _split3(x)
    dn = _DIMS["nn"]
    out = lax.dot_general(hi, tri, dn, preferred_element_type=F32)
    out = out + lax.dot_general(mid, tri, dn, preferred_element_type=F32)
    return out + lax.dot_general(lo, tri, dn, preferred_element_type=F32)


def _log_sigmoid(z):
    return jnp.minimum(z, 0.0) - jnp.log(1.0 + jnp.exp(-jnp.abs(z)))


def _gate_fwd(f_row, b_col, name):
    rows, s = f_row.shape
    t = _tile(s, 512)
    n = s // t

    def body(f_ref, b_ref, c_ref, carry_ref):
        i = pl.program_id(0)

        @pl.when(i == 0)
        def _():
            carry_ref[...] = jnp.zeros_like(carry_ref)

        logf = _log_sigmoid(f_ref[...] + b_ref[...])
        r = lax.broadcasted_iota(jnp.int32, (t, t), 0)
        c = lax.broadcasted_iota(jnp.int32, (t, t), 1)
        tri = jnp.where(r <= c, 1.0, 0.0).astype(BF16)
        cs = _tri_dot(logf, tri) + carry_ref[...]
        carry_ref[...] = cs[:, t - 1:t]
        for h in range(N_HEADS):
            c_ref[h] = jnp.broadcast_to(cs[h:h + 1, :], (8, t))

    return pl.pallas_call(
        body, name=name, grid=(n,),
        in_specs=[pl.BlockSpec((rows, t), lambda i: (0, i)), pl.BlockSpec((rows, 1), lambda i: (0, 0))],
        out_specs=pl.BlockSpec((N_HEADS, 8, t), lambda i: (0, 0, i)),
        out_shape=jax.ShapeDtypeStruct((N_HEADS, 8, s), F32),
        scratch_shapes=[pltpu.VMEM((rows, 1), F32)],
        compiler_params=_params(("arbitrary",)),
    )(f_row, b_col)


def _gate_bwd(f_row, b_col, dc_even, dc_odd, name):
    rows, s = f_row.shape
    t = _tile(s, 512)
    n = s // t

    def body(f_ref, b_ref, dce_ref, dco_ref, df_ref, db_ref, carry_ref, acc_ref):
        i = pl.program_id(0)

        @pl.when(i == 0)
        def _():
            carry_ref[...] = jnp.zeros_like(carry_ref)
            acc_ref[...] = jnp.zeros_like(acc_ref)

        head = lax.broadcasted_iota(jnp.int32, (rows, t), 0)
        dcv = jnp.zeros((rows, t), F32)
        for h in range(N_HEADS):
            src = dce_ref if h % 2 == 0 else dco_ref
            dcv = jnp.where(head == h, jnp.broadcast_to(src[h // 2, 0:1, :], (rows, t)), dcv)
        r = lax.broadcasted_iota(jnp.int32, (t, t), 0)
        c = lax.broadcasted_iota(jnp.int32, (t, t), 1)
        tri = jnp.where(r >= c, 1.0, 0.0).astype(BF16)
        dlogf = _tri_dot(dcv, tri) + carry_ref[...]
        carry_ref[...] = dlogf[:, 0:1]
        z = f_ref[...] + b_ref[...]
        df = dlogf * _sigmoid(-z)
        df_ref[...] = df.astype(df_ref.dtype)
        acc_ref[...] += jnp.sum(df, axis=1, keepdims=True)

        @pl.when(i == n - 1)
        def _():
            db_ref[...] = acc_ref[...]

    rev = lambda i: (0, n - 1 - i)
    dc_spec = pl.BlockSpec((N_HEADS // 2, 8, t), lambda i: (0, 0, n - 1 - i))
    return pl.pallas_call(
        body, name=name, grid=(n,),
        in_specs=[pl.BlockSpec((rows, t), rev), pl.BlockSpec((rows, 1), lambda i: (0, 0)), dc_spec, dc_spec],
        out_specs=[pl.BlockSpec((rows, t), rev), pl.BlockSpec((rows, 1), lambda i: (0, 0))],
        out_shape=[jax.ShapeDtypeStruct((rows, s), BF16), jax.ShapeDtypeStruct((rows, 1), F32)],
        scratch_shapes=[pltpu.VMEM((rows, 1), F32), pltpu.VMEM((rows, 1), F32)],
        compiler_params=_params(("arbitrary",)),
    )(f_row, b_col, dc_even, dc_odd)


_NEG = -1e30
_SCALE = HEAD_DIM ** -0.5


def _head_masks():
    lane = lax.broadcasted_iota(jnp.int32, (1, LANES), 1)
    return [lane < HEAD_DIM, lane >= HEAD_DIM]


def _attn_fwd(h, c, name):
    s = h.shape[0]
    t = _tile(s, 512)
    n = s // t
    qb, kb, vb = OFF_Q // LANES, OFF_K // LANES, OFF_V // LANES

    def body(q_ref, k_ref, v_ref, c0_ref, c1_ref, o_ref, of_ref, lse_ref, m_ref, l_ref, acc_ref):
        qi, ki = pl.program_id(1), pl.program_id(2)
        masks = _head_masks()

        @pl.when(ki == 0)
        def _():
            m_ref[...] = jnp.full_like(m_ref, _NEG)
            l_ref[...] = jnp.zeros_like(l_ref)
            acc_ref[...] = jnp.zeros_like(acc_ref)

        def step(diag):
            q = q_ref[...] * _SCALE
            k = k_ref[...]
            v = v_ref[...]
            acc = acc_ref[...]
            for hh, c_ref in enumerate((c0_ref, c1_ref)):
                qh = jnp.where(masks[hh], q, jnp.zeros_like(q))
                sc = lax.dot_general(qh, k, _DIMS["nt"], preferred_element_type=F32) - c_ref[0, 0:1, :]
                if diag:
                    r = lax.broadcasted_iota(jnp.int32, (t, t), 0)
                    cc = lax.broadcasted_iota(jnp.int32, (t, t), 1)
                    sc = jnp.where(r >= cc, sc, _NEG)
                m_prev = m_ref[hh]
                m_new = jnp.maximum(m_prev, jnp.max(sc, axis=1, keepdims=True))
                alpha = jnp.exp(m_prev - m_new)
                p = jnp.exp(sc - m_new)
                l_ref[hh] = alpha * l_ref[hh] + jnp.sum(p, axis=1, keepdims=True)
                m_ref[hh] = m_new
                p_hi = p.astype(MXU_DTYPE)
                p_lo = (p - p_hi.astype(F32)).astype(MXU_DTYPE)
                pv = (lax.dot_general(p_hi, v, _DIMS["nn"], preferred_element_type=F32)
                      + lax.dot_general(p_lo, v, _DIMS["nn"], preferred_element_type=F32))
                acc = jnp.where(masks[hh], alpha * acc + pv, acc)
            acc_ref[...] = acc

        @pl.when(ki < qi)
        def _():
            step(False)

        @pl.when(ki == qi)
        def _():
            step(True)
            inv = jnp.where(masks[0], 1.0 / l_ref[0], 1.0 / l_ref[1])
            out = acc_ref[...] * inv
            o_ref[...] = out.astype(o_ref.dtype)
            of_ref[...] = out
            lse_ref[...] = jnp.where(masks[0], m_ref[0] + jnp.log(l_ref[0]), m_ref[1] + jnp.log(l_ref[1]))

    kv_row = lambda p, qi, ki: jnp.minimum(ki, qi)
    return pl.pallas_call(
        body, name=name, grid=(N_HEADS // 2, n, n),
        in_specs=[
            pl.BlockSpec((t, LANES), lambda p, qi, ki: (qi, qb + p)),
            pl.BlockSpec((t, LANES), lambda p, qi, ki: (kv_row(p, qi, ki), kb + p)),
            pl.BlockSpec((t, LANES), lambda p, qi, ki: (kv_row(p, qi, ki), vb + p)),
            pl.BlockSpec((1, 8, t), lambda p, qi, ki: (2 * p, 0, kv_row(p, qi, ki))),
            pl.BlockSpec((1, 8, t), lambda p, qi, ki: (2 * p + 1, 0, kv_row(p, qi, ki))),
        ],
        out_specs=[pl.BlockSpec((t, LANES), lambda p, qi, ki: (qi, p))] * 3,
        out_shape=[jax.ShapeDtypeStruct((s, D_ATT), BF16), jax.ShapeDtypeStruct((s, D_ATT), F32),
                   jax.ShapeDtypeStruct((s, D_ATT), F32)],
        scratch_shapes=[pltpu.VMEM((2, t, 1), F32), pltpu.VMEM((2, t, 1), F32), pltpu.VMEM((t, LANES), F32)],
        compiler_params=_params(("parallel", "parallel", "arbitrary")),
    )(h, h, h, c, c)


def _attn_bwd(h, c, o, lse, do, name):
    s = h.shape[0]
    t = _tile(s, 512)
    n = s // t
    qb, kb, vb = OFF_Q // LANES, OFF_K // LANES, OFF_V // LANES

    def body(q_ref, k_ref, v_ref, c0_ref, c1_ref, o_ref, lse_ref, do_ref,
             dq_ref, dk_ref, dv_ref, dc0_ref, dc1_ref, dk_acc, dv_acc, dc_acc):
        ki, qi = pl.program_id(1), pl.program_id(2)
        masks = _head_masks()

        @pl.when((ki == 0) & (qi == 0))
        def _():
            dq_ref[...] = jnp.zeros_like(dq_ref)

        @pl.when(qi == ki)
        def _():
            dk_acc[...] = jnp.zeros_like(dk_acc)
            dv_acc[...] = jnp.zeros_like(dv_acc)
            dc_acc[...] = jnp.zeros_like(dc_acc)

        def step(diag):
            q = q_ref[...] * _SCALE
            k = k_ref[...]
            v = v_ref[...]
            dov = do_ref[...]
            lsev = lse_ref[...]
            prod = dov.astype(F32) * o_ref[...]
            dq_blk = jnp.zeros((t, LANES), F32)
            dk_blk = dk_acc[...]
            dv_blk = dv_acc[...]
            for hh, c_ref in enumerate((c0_ref, c1_ref)):
                mk = masks[hh]
                delta = jnp.sum(jnp.where(mk, prod, 0.0), axis=1, keepdims=True)
                lse_h = lsev[:, hh * HEAD_DIM:hh * HEAD_DIM + 1]
                qh = jnp.where(mk, q, jnp.zeros_like(q))
                doh = jnp.where(mk, dov, jnp.zeros_like(dov))
                sc = lax.dot_general(qh, k, _DIMS["nt"], preferred_element_type=F32) - c_ref[0, 0:1, :]
                p = jnp.exp(sc - lse_h)
                if diag:
                    r = lax.broadcasted_iota(jnp.int32, (t, t), 0)
                    cc = lax.broadcasted_iota(jnp.int32, (t, t), 1)
                    p = jnp.where(r >= cc, p, 0.0)
                dp = lax.dot_general(doh, v, _DIMS["nt"], preferred_element_type=F32)
                ds = p * (dp - delta)
                dsb = ds.astype(MXU_DTYPE)
                pb = p.astype(MXU_DTYPE)
                dv_blk = jnp.where(mk, dv_blk + lax.dot_general(pb, dov, _DIMS["tn"], preferred_element_type=F32),
                                   dv_blk)
                dk_blk = jnp.where(mk, dk_blk + lax.dot_general(dsb, q, _DIMS["tn"], preferred_element_type=F32),
                                   dk_blk)
                dq_blk = jnp.where(mk, lax.dot_general(dsb, k, _DIMS["nn"], preferred_element_type=F32), dq_blk)
                dc_acc[hh] = dc_acc[hh] - jnp.sum(ds, axis=0, keepdims=True)
            dk_acc[...] = dk_blk
            dv_acc[...] = dv_blk
            rows = pl.ds(pl.multiple_of(qi * t, t), t)
            dq_ref[rows, :] = dq_ref[rows, :] + dq_blk * _SCALE

        @pl.when(qi > ki)
        def _():
            step(False)

        @pl.when(qi == ki)
        def _():
            step(True)

        @pl.when(qi == n - 1)
        def _():
            dk_ref[...] = dk_acc[...].astype(dk_ref.dtype)
            dv_ref[...] = dv_acc[...].astype(dv_ref.dtype)
            dc0_ref[0] = jnp.broadcast_to(dc_acc[0], (8, t))
            dc1_ref[0] = jnp.broadcast_to(dc_acc[1], (8, t))

    q_row = lambda p, ki, qi: jnp.maximum(qi, ki)
    return pl.pallas_call(
        body, name=name, grid=(N_HEADS // 2, n, n),
        in_specs=[
            pl.BlockSpec((t, LANES), lambda p, ki, qi: (q_row(p, ki, qi), qb + p)),
            pl.BlockSpec((t, LANES), lambda p, ki, qi: (ki, kb + p)),
            pl.BlockSpec((t, LANES), lambda p, ki, qi: (ki, vb + p)),
            pl.BlockSpec((1, 8, t), lambda p, ki, qi: (2 * p, 0, ki)),
            pl.BlockSpec((1, 8, t), lambda p, ki, qi: (2 * p + 1, 0, ki)),
            pl.BlockSpec((t, LANES), lambda p, ki, qi: (q_row(p, ki, qi), p)),
            pl.BlockSpec((t, LANES), lambda p, ki, qi: (q_row(p, ki, qi), p)),
            pl.BlockSpec((t, LANES), lambda p, ki, qi: (q_row(p, ki, qi), p)),
        ],
        out_specs=[
            pl.BlockSpec((s, LANES), lambda p, ki, qi: (0, p)),
            pl.BlockSpec((t, LANES), lambda p, ki, qi: (ki, p)),
            pl.BlockSpec((t, LANES), lambda p, ki, qi: (ki, p)),
            pl.BlockSpec((1, 8, t), lambda p, ki, qi: (p, 0, ki)),
            pl.BlockSpec((1, 8, t), lambda p, ki, qi: (p, 0, ki)),
        ],
        out_shape=[jax.ShapeDtypeStruct((s, D_ATT), F32), jax.ShapeDtypeStruct((s, D_ATT), BF16),
                   jax.ShapeDtypeStruct((s, D_ATT), BF16), jax.ShapeDtypeStruct((N_HEADS // 2, 8, s), F32),
                   jax.ShapeDtypeStruct((N_HEADS // 2, 8, s), F32)],
        scratch_shapes=[pltpu.VMEM((t, LANES), F32), pltpu.VMEM((t, LANES), F32), pltpu.VMEM((2, 1, t), F32)],
        compiler_params=_params(("parallel", "arbitrary", "arbitrary")),
    )(h, h, h, c, c, o, lse, do)


def _conv3(z, z_prev, w_ref):
    return (w_ref[2:3, :] * z + w_ref[1:2, :] * _shift_down(z, z_prev, 1)
            + w_ref[0:1, :] * _shift_down(z, z_prev, 2))


def _sconv_fwd(h, w, name):
    s = h.shape[0]
    t = _tile(s, 512)
    r = t // HALO
    c = D_CONV
    b_bg, b_cg, b_hc = OFF_BG // c, OFF_CG // c, OFF_HC // c

    def body(bg_ref, cg_ref, hc_ref, cgp_ref, hcp_ref, w_ref, y_ref):
        i = pl.program_id(0)
        live = (i > 0).astype(F32)
        z = cg_ref[...].astype(F32) * hc_ref[...].astype(F32)
        zp = cgp_ref[...].astype(F32) * hcp_ref[...].astype(F32) * live
        y_ref[...] = (bg_ref[...].astype(F32) * _conv3(z, zp, w_ref)).astype(y_ref.dtype)

    cur = lambda b: pl.BlockSpec((t, c), lambda i: (i, b))
    prev = lambda b: pl.BlockSpec((HALO, c), lambda i: (jnp.maximum(i * r - 1, 0), b))
    return pl.pallas_call(
        body, name=name, grid=(s // t,),
        in_specs=[cur(b_bg), cur(b_cg), cur(b_hc), prev(b_cg), prev(b_hc), pl.BlockSpec((8, c), lambda i: (0, 0))],
        out_specs=pl.BlockSpec((t, c), lambda i: (i, 0)),
        out_shape=jax.ShapeDtypeStruct((s, c), BF16),
        compiler_params=_params(("parallel",)),
    )(h, h, h, h, h, w)


def _sconv_bwd(h, w, dy, name):
    s = h.shape[0]
    t = _tile(s, 512)
    n = s // t
    r = t // HALO
    nh = s // HALO
    c = D_CONV
    b_bg, b_cg, b_hc = OFF_BG // c, OFF_CG // c, OFF_HC // c

    def body(bg_ref, cg_ref, hc_ref, cgp_ref, hcp_ref, bgn_ref, dy_ref, dyn_ref, w_ref, d_ref, dw_ref, acc_ref):
        i = pl.program_id(0)
        has_prev = (i > 0).astype(F32)
        has_next = (i < n - 1).astype(F32)
        bg = bg_ref[...].astype(F32)
        cg = cg_ref[...].astype(F32)
        hc = hc_ref[...].astype(F32)
        dyv = dy_ref[...].astype(F32)
        z = cg * hc
        zp = cgp_ref[...].astype(F32) * hcp_ref[...].astype(F32) * has_prev
        z1 = _shift_down(z, zp, 1)
        z2 = _shift_down(z, zp, 2)
        cz = w_ref[2:3, :] * z + w_ref[1:2, :] * z1 + w_ref[0:1, :] * z2
        dcz = dyv * bg
        dczn = dyn_ref[...].astype(F32) * bgn_ref[...].astype(F32) * has_next
        dz = (w_ref[2:3, :] * dcz + w_ref[1:2, :] * _shift_up(dcz, dczn, 1)
              + w_ref[0:1, :] * _shift_up(dcz, dczn, 2))
        d_ref[:, 0:c] = (dyv * cz).astype(d_ref.dtype)
        d_ref[:, c:2 * c] = (dz * hc).astype(d_ref.dtype)
        d_ref[:, 2 * c:3 * c] = (dz * cg).astype(d_ref.dtype)

        @pl.when(i == 0)
        def _():
            acc_ref[...] = jnp.zeros_like(acc_ref)

        acc_ref[0] += _row_sum8(dcz * z2)
        acc_ref[1] += _row_sum8(dcz * z1)
        acc_ref[2] += _row_sum8(dcz * z)

        @pl.when(i == n - 1)
        def _():
            rows = [jnp.sum(acc_ref[k], axis=0, keepdims=True) for k in range(3)]
            dw_ref[...] = jnp.concatenate(rows + [jnp.zeros((5, c), F32)], axis=0)

    cur = lambda b: pl.BlockSpec((t, c), lambda i: (i, b))
    prev = lambda b: pl.BlockSpec((HALO, c), lambda i: (jnp.maximum(i * r - 1, 0), b))
    nxt = lambda b: pl.BlockSpec((HALO, c), lambda i: (jnp.minimum((i + 1) * r, nh - 1), b))
    return pl.pallas_call(
        body, name=name, grid=(n,),
        in_specs=[cur(b_bg), cur(b_cg), cur(b_hc), prev(b_cg), prev(b_hc), nxt(b_bg),
                  cur(0), nxt(0), pl.BlockSpec((8, c), lambda i: (0, 0))],
        out_specs=[pl.BlockSpec((t, 3 * c), lambda i: (i, 0)), pl.BlockSpec((8, c), lambda i: (0, 0))],
        out_shape=[jax.ShapeDtypeStruct((s, 3 * c), BF16), jax.ShapeDtypeStruct((8, c), F32)],
        scratch_shapes=[pltpu.VMEM((3, 8, c), F32)],
        compiler_params=_params(("arbitrary",)),
    )(h, h, h, h, h, h, dy, dy, w)


def _group_masks():
    lane = lax.broadcasted_iota(jnp.int32, (1, D_SGU), 1)
    return [(lane >= g * HEAD_DIM) & (lane < (g + 1) * HEAD_DIM) for g in range(N_GROUPS)]


def _tril_weights(w_ref):
    r = lax.broadcasted_iota(jnp.int32, (CHUNK, CHUNK), 0)
    c = lax.broadcasted_iota(jnp.int32, (CHUNK, CHUNK), 1)
    return [jnp.where(r >= c, w_ref[g], 0.0).astype(MXU_DTYPE) for g in range(N_GROUPS)]


def _sgu_ln(vs, g_ref, b_ref):
    vg, dvg = _gelu_and_grad(vs)
    mu = jnp.mean(vg, axis=-1, keepdims=True)
    xc = vg - mu
    rstd = lax.rsqrt(jnp.mean(xc * xc, axis=-1, keepdims=True) + LN_EPS)
    xhat = xc * rstd
    return xhat * g_ref[...] + b_ref[...], xhat, rstd, dvg


def _sgu_fwd(h, ln_g, ln_b, w_s, bias, name):
    s = h.shape[0]
    t = _tile(s, 512)
    c = D_SGU
    b_u, b_v = OFF_U // c, OFF_VS // c

    def body(u_ref, v_ref, g_ref, b_ref, w_ref, bias_ref, y_ref):
        gm = _group_masks()
        wm = _tril_weights(w_ref)
        ug = _gelu(u_ref[...].astype(F32))
        vn, _, _, _ = _sgu_ln(v_ref[...].astype(F32), g_ref, b_ref)
        vnb = vn.astype(MXU_DTYPE)
        for ch in range(t // CHUNK):
            rows = slice(ch * CHUNK, (ch + 1) * CHUNK)
            mixed = bias_ref[...]
            for g in range(N_GROUPS):
                mg = lax.dot_general(wm[g], vnb[rows], _DIMS["nn"], preferred_element_type=F32)
                mixed = jnp.where(gm[g], mixed + mg, mixed)
            y_ref[rows, :] = (ug[rows] * mixed).astype(y_ref.dtype)

    full = lambda shp: pl.BlockSpec(shp, lambda i: (0,) * len(shp))
    return pl.pallas_call(
        body, name=name, grid=(s // t,),
        in_specs=[pl.BlockSpec((t, c), lambda i: (i, b_u)), pl.BlockSpec((t, c), lambda i: (i, b_v)),
                  full((1, c)), full((1, c)), full((N_GROUPS, CHUNK, CHUNK)), full((CHUNK, c))],
        out_specs=pl.BlockSpec((t, c), lambda i: (i, 0)),
        out_shape=jax.ShapeDtypeStruct((s, c), BF16),
        compiler_params=_params(("parallel",)),
    )(h, h, ln_g, ln_b, w_s, bias)


def _sgu_bwd(h, ln_g, ln_b, w_s, bias, dy, name):
    s = h.shape[0]
    t = _tile(s, 512)
    n = s // t
    c = D_SGU
    b_u, b_v = OFF_U // c, OFF_VS // c

    def body(u_ref, v_ref, g_ref, b_ref, w_ref, bias_ref, dy_ref,
             d_ref, dg_ref, db_ref, dw_ref, dbias_ref, dg_acc, db_acc):
        i = pl.program_id(0)
        gm = _group_masks()
        wm = _tril_weights(w_ref)

        @pl.when(i == 0)
        def _():
            dg_acc[...] = jnp.zeros_like(dg_acc)
            db_acc[...] = jnp.zeros_like(db_acc)
            dw_ref[...] = jnp.zeros_like(dw_ref)
            dbias_ref[...] = jnp.zeros_like(dbias_ref)

        ug, dug = _gelu_and_grad(u_ref[...].astype(F32))
        vn, xhat, rstd, dvg = _sgu_ln(v_ref[...].astype(F32), g_ref, b_ref)
        vnb = vn.astype(MXU_DTYPE)
        dyv = dy_ref[...].astype(F32)
        dmixed = dyv * ug
        dmb = dmixed.astype(MXU_DTYPE)
        dvn_parts = []
        for ch in range(t // CHUNK):
            rows = slice(ch * CHUNK, (ch + 1) * CHUNK)
            mixed = bias_ref[...]
            dvn = jnp.zeros((CHUNK, c), F32)
            for g in range(N_GROUPS):
                mg = lax.dot_general(wm[g], vnb[rows], _DIMS["nn"], preferred_element_type=F32)
                mixed = jnp.where(gm[g], mixed + mg, mixed)
                dvn = jnp.where(gm[g], lax.dot_general(wm[g], dmb[rows], _DIMS["tn"], preferred_element_type=F32),
                                dvn)
                dmg = jnp.where(gm[g], dmb[rows], jnp.zeros_like(dmb[rows]))
                dw_ref[g] += lax.dot_general(dmg, vnb[rows], _DIMS["nt"], preferred_element_type=F32)
            d_ref[rows, 0:c] = (dyv[rows] * mixed * dug[rows]).astype(d_ref.dtype)
            dbias_ref[...] += dmixed[rows]
            dvn_parts.append(dvn)
        dvn = jnp.concatenate(dvn_parts, axis=0)
        dg_acc[...] += _row_sum8(dvn * xhat)
        db_acc[...] += _row_sum8(dvn)
        dxh = dvn * g_ref[...]
        dvgl = rstd * (dxh - jnp.mean(dxh, axis=-1, keepdims=True)
                       - xhat * jnp.mean(dxh * xhat, axis=-1, keepdims=True))
        d_ref[:, c:2 * c] = (dvgl * dvg).astype(d_ref.dtype)

        @pl.when(i == n - 1)
        def _():
            dg_ref[...] = jnp.sum(dg_acc[...], axis=0, keepdims=True)
            db_ref[...] = jnp.sum(db_acc[...], axis=0, keepdims=True)
            r = lax.broadcasted_iota(jnp.int32, (CHUNK, CHUNK), 0)
            cc = lax.broadcasted_iota(jnp.int32, (CHUNK, CHUNK), 1)
            for g in range(N_GROUPS):
                dw_ref[g] = jnp.where(r >= cc, dw_ref[g], 0.0)

    full = lambda shp: pl.BlockSpec(shp, lambda i: (0,) * len(shp))
    return pl.pallas_call(
        body, name=name, grid=(n,),
        in_specs=[pl.BlockSpec((t, c), lambda i: (i, b_u)), pl.BlockSpec((t, c), lambda i: (i, b_v)),
                  full((1, c)), full((1, c)), full((N_GROUPS, CHUNK, CHUNK)), full((CHUNK, c)),
                  pl.BlockSpec((t, c), lambda i: (i, 0))],
        out_specs=[pl.BlockSpec((t, 2 * c), lambda i: (i, 0)), full((1, c)), full((1, c)),
                   full((N_GROUPS, CHUNK, CHUNK)), full((CHUNK, c))],
        out_shape=[jax.ShapeDtypeStruct((s, 2 * c), BF16), jax.ShapeDtypeStruct((1, c), F32),
                   jax.ShapeDtypeStruct((1, c), F32), jax.ShapeDtypeStruct((N_GROUPS, CHUNK, CHUNK), F32),
                   jax.ShapeDtypeStruct((CHUNK, c), F32)],
        scratch_shapes=[pltpu.VMEM((8, c), F32), pltpu.VMEM((8, c), F32)],
        compiler_params=_params(("arbitrary",)),
    )(h, h, ln_g, ln_b, w_s, bias, dy)


def _merge_fwd(h, acts, ws, b_gate, name):
    s = h.shape[0]
    d = D_MODEL
    t = _tile(s, 512)

    def body(gl0, gl1, gl2, a0, a1, a2, w0, w1, w2, b_ref, o_ref):
        acc = jnp.zeros((t, d), F32)
        for i, (gl, a, w) in enumerate(((gl0, a0, w0), (gl1, a1, w1), (gl2, a2, w2))):
            y = lax.dot_general(a[...], w[...], _DIMS["nn"], preferred_element_type=F32)
            acc = acc + _sigmoid(gl[...].astype(F32) + b_ref[i:i + 1, :]) * y
        o_ref[...] = acc.astype(o_ref.dtype)

    full = lambda arr: pl.BlockSpec(arr.shape, lambda i: (0, 0))
    return pl.pallas_call(
        body, name=name, grid=(s // t,),
        in_specs=[pl.BlockSpec((t, d), lambda i, b=b: (i, b)) for b in range(3)]
                 + [pl.BlockSpec((t, a.shape[1]), lambda i: (i, 0)) for a in acts]
                 + [full(w) for w in ws] + [full(b_gate)],
        out_specs=pl.BlockSpec((t, d), lambda i: (i, 0)),
        out_shape=jax.ShapeDtypeStruct((s, d), BF16),
        compiler_params=_params(("parallel",)),
    )(h, h, h, *acts, *ws, b_gate)


def _merge_bwd(h, acts, ws, b_gate, dmerged, name):
    s = h.shape[0]
    d = D_MODEL
    t = _tile(s, 512)
    n = s // t

    def body(gl0, gl1, gl2, a0, a1, a2, w0, w1, w2, b_ref, dm_ref, dy0, dy1, dy2, dgl_ref, db_ref, acc_ref):
        step = pl.program_id(0)

        @pl.when(step == 0)
        def _():
            acc_ref[...] = jnp.zeros_like(acc_ref)

        dm = dm_ref[...]
        for i, (gl, a, w, dy) in enumerate(((gl0, a0, w0, dy0), (gl1, a1, w1, dy1), (gl2, a2, w2, dy2))):
            y = lax.dot_general(a[...], w[...], _DIMS["nn"], preferred_element_type=F32)
            gate = _sigmoid(gl[...].astype(F32) + b_ref[i:i + 1, :])
            dy[...] = (dm * gate).astype(dy.dtype)
            dgl = dm * y * (gate * (1.0 - gate))
            dgl_ref[:, i * d:(i + 1) * d] = dgl.astype(dgl_ref.dtype)
            acc_ref[i] += _row_sum8(dgl)

        @pl.when(step == n - 1)
        def _():
            rows = [jnp.sum(acc_ref[k], axis=0, keepdims=True) for k in range(3)]
            db_ref[...] = jnp.concatenate(rows + [jnp.zeros((5, d), F32)], axis=0)

    full = lambda arr: pl.BlockSpec(arr.shape, lambda i: (0, 0))
    row = pl.BlockSpec((t, d), lambda i: (i, 0))
    return pl.pallas_call(
        body, name=name, grid=(n,),
        in_specs=[pl.BlockSpec((t, d), lambda i, b=b: (i, b)) for b in range(3)]
                 + [pl.BlockSpec((t, a.shape[1]), lambda i: (i, 0)) for a in acts]
                 + [full(w) for w in ws] + [full(b_gate), row],
        out_specs=[row, row, row, pl.BlockSpec((t, 3 * d), lambda i: (i, 0)), pl.BlockSpec((8, d), lambda i: (0, 0))],
        out_shape=[jax.ShapeDtypeStruct((s, d), BF16)] * 3
                  + [jax.ShapeDtypeStruct((s, 3 * d), BF16), jax.ShapeDtypeStruct((8, d), F32)],
        scratch_shapes=[pltpu.VMEM((3, 8, d), F32)],
        compiler_params=_params(("arbitrary",)),
    )(h, h, h, *acts, *ws, b_gate, dmerged)


FF_BLK = D_FF // 2


def _ffn_act_fwd(h2, w, name):
    s = h2.shape[0]
    t = _tile(s, 512)
    r = t // HALO
    cw = 2 * FF_BLK

    def body(x_ref, xp_ref, w_ref, p_ref):
        i = pl.program_id(0)
        live = (i > 0).astype(F32)
        hc = _conv3(x_ref[...].astype(F32), xp_ref[...].astype(F32) * live, w_ref)
        p_ref[...] = (_gelu(hc[:, :FF_BLK]) * hc[:, FF_BLK:]).astype(p_ref.dtype)

    return pl.pallas_call(
        body, name=name, grid=(s // t, 2),
        in_specs=[pl.BlockSpec((t, cw), lambda i, j: (i, j)),
                  pl.BlockSpec((HALO, cw), lambda i, j: (jnp.maximum(i * r - 1, 0), j)),
                  pl.BlockSpec((8, cw), lambda i, j: (0, j))],
        out_specs=pl.BlockSpec((t, FF_BLK), lambda i, j: (i, j)),
        out_shape=jax.ShapeDtypeStruct((s, D_FF), BF16),
        compiler_params=_params(("parallel", "parallel")),
    )(h2, h2, w)


def _ffn_act_bwd(h2, w, dp, name):
    s = h2.shape[0]
    t = _tile(s, 512)
    r = t // HALO
    cw = 2 * FF_BLK

    def body(x_ref, xp_ref, w_ref, dp_ref, d_ref):
        i = pl.program_id(0)
        live = (i > 0).astype(F32)
        hc = _conv3(x_ref[...].astype(F32), xp_ref[...].astype(F32) * live, w_ref)
        ga, dga = _gelu_and_grad(hc[:, :FF_BLK])
        dpv = dp_ref[...].astype(F32)
        d_ref[:, :FF_BLK] = (dpv * hc[:, FF_BLK:] * dga).astype(d_ref.dtype)
        d_ref[:, FF_BLK:] = (dpv * ga).astype(d_ref.dtype)

    return pl.pallas_call(
        body, name=name, grid=(s // t, 2),
        in_specs=[pl.BlockSpec((t, cw), lambda i, j: (i, j)),
                  pl.BlockSpec((HALO, cw), lambda i, j: (jnp.maximum(i * r - 1, 0), j)),
                  pl.BlockSpec((8, cw), lambda i, j: (0, j)),
                  pl.BlockSpec((t, FF_BLK), lambda i, j: (i, j))],
        out_specs=pl.BlockSpec((t, cw), lambda i, j: (i, j)),
        out_shape=jax.ShapeDtypeStruct((s, 2 * D_FF), BF16),
        compiler_params=_params(("parallel", "parallel")),
    )(h2, h2, w, dp)


def _dwconv_bwd(x, w, dy, name):
    s, c = x.shape
    t = _tile(s, 512)
    n = s // t
    r = t // HALO
    nh = s // HALO
    cw = FF_BLK
    nc = c // cw

    def body(x_ref, xp_ref, dy_ref, dyn_ref, w_ref, dx_ref, dw_ref, acc_ref):
        i = pl.program_id(1)
        has_prev = (i > 0).astype(F32)
        has_next = (i < n - 1).astype(F32)
        xv = x_ref[...].astype(F32)
        xp = xp_ref[...].astype(F32) * has_prev
        dyv = dy_ref[...].astype(F32)
        dyn = dyn_ref[...].astype(F32) * has_next
        dx = (w_ref[2:3, :] * dyv + w_ref[1:2, :] * _shift_up(dyv, dyn, 1)
              + w_ref[0:1, :] * _shift_up(dyv, dyn, 2))
        dx_ref[...] = dx.astype(dx_ref.dtype)

        @pl.when(i == 0)
        def _():
            acc_ref[...] = jnp.zeros_like(acc_ref)

        acc_ref[0] += _row_sum8(dyv * _shift_down(xv, xp, 2))
        acc_ref[1] += _row_sum8(dyv * _shift_down(xv, xp, 1))
        acc_ref[2] += _row_sum8(dyv * xv)

        @pl.when(i == n - 1)
        def _():
            rows = [jnp.sum(acc_ref[k], axis=0, keepdims=True) for k in range(3)]
            dw_ref[...] = jnp.concatenate(rows + [jnp.zeros((5, cw), F32)], axis=0)

    return pl.pallas_call(
        body, name=name, grid=(nc, n),
        in_specs=[pl.BlockSpec((t, cw), lambda j, i: (i, j)),
                  pl.BlockSpec((HALO, cw), lambda j, i: (jnp.maximum(i * r - 1, 0), j)),
                  pl.BlockSpec((t, cw), lambda j, i: (i, j)),
                  pl.BlockSpec((HALO, cw), lambda j, i: (jnp.minimum((i + 1) * r, nh - 1), j)),
                  pl.BlockSpec((8, cw), lambda j, i: (0, j))],
        out_specs=[pl.BlockSpec((t, cw), lambda j, i: (i, j)), pl.BlockSpec((8, cw), lambda j, i: (0, j))],
        out_shape=[jax.ShapeDtypeStruct((s, c), BF16), jax.ShapeDtypeStruct((8, c), F32)],
        scratch_shapes=[pltpu.VMEM((3, 8, cw), F32)],
        compiler_params=_params(("parallel", "arbitrary")),
    )(x, x, dy, dy, w)


def _adamw(w, g, m, v, name):
    shape = w.shape
    c = shape[-1]
    rows = math.prod(shape[:-1])
    to2d = lambda a: a.reshape(rows, c)
    cap = max(8, (1 << 18) // c)
    tr = rows
    for cand in (2048, 1024, 512, 256, 128, 64, 32, 16, 8):
        if cand <= cap and rows % cand == 0:
            tr = cand
            break

    def body(w_ref, g_ref, m_ref, v_ref, d_ref, nm_ref, nv_ref):
        gv = g_ref[...]
        nm = ADAM_B1 * m_ref[...] + (1.0 - ADAM_B1) * gv
        nv = ADAM_B2 * v_ref[...] + (1.0 - ADAM_B2) * (gv * gv)
        m_hat = nm / (1.0 - ADAM_B1 ** ADAM_STEP)
        v_hat = nv / (1.0 - ADAM_B2 ** ADAM_STEP)
        d_ref[...] = -ADAM_LR * (m_hat / (jnp.sqrt(v_hat) + ADAM_EPS) + ADAM_WD * w_ref[...])
        nm_ref[...] = nm
        nv_ref[...] = nv

    blk = pl.BlockSpec((tr, c), lambda i: (i, 0))
    outs = pl.pallas_call(
        body, name=name, grid=(rows // tr,),
        in_specs=[blk] * 4, out_specs=[blk] * 3,
        out_shape=[jax.ShapeDtypeStruct((rows, c), F32)] * 3,
        compiler_params=_params(("parallel",)),
    )(to2d(w), to2d(g), to2d(m), to2d(v))
    return tuple(o.reshape(shape) for o in outs)


_ANY = pl.BlockSpec(memory_space=pl.ANY)


def _place():
    x, y, c = lax.axis_index("x"), lax.axis_index("y"), lax.axis_index("c")
    others = [(1 - x, y), (x, 1 - y), (1 - x, 1 - y)]
    return x, y, c, others


def _all_gather_chips(shard, name):
    rws, cols = shard.shape
    half = rws // 2

    def body(x_ref, out_ref, send_sems, recv_sems, local_sem):
        x, y, c, others = _place()
        me = 2 * x + y
        sib = (x, y, 1 - c)

        def rows(chip, cc):
            return out_ref.at[chip, pl.ds(pl.multiple_of(cc * half, 16), half), :]

        def copy(k, src, dst, to):
            return pltpu.make_async_remote_copy(src_ref=src, dst_ref=dst, send_sem=send_sems.at[k],
                                                recv_sem=recv_sems.at[k], device_id=to, device_id_type=MESH)

        mine = pltpu.make_async_copy(x_ref, out_ref.at[me], local_sem)
        mine.start()
        my_half = x_ref.at[pl.ds(pl.multiple_of(c * half, 16), half), :]
        first = [copy(j, my_half, rows(me, c), (ox, oy, c)) for j, (ox, oy) in enumerate(others)]
        for cp in first:
            cp.start()
        passed = []
        for j, (ox, oy) in enumerate(others):
            blk = rows(2 * ox + oy, c)
            copy(j, blk, blk, (x, y, c)).wait_recv()
            fwd = copy(3 + j, blk, blk, sib)
            fwd.start()
            passed.append(fwd)
        for j, (ox, oy) in enumerate(others):
            blk = rows(2 * ox + oy, 1 - c)
            copy(3 + j, blk, blk, (x, y, c)).wait_recv()
        for cp in first + passed:
            cp.wait_send()
        mine.wait()

    return pl.pallas_call(
        body, name=name,
        in_specs=[_ANY], out_specs=_ANY,
        out_shape=jax.ShapeDtypeStruct((N_CHIPS, rws, cols), shard.dtype),
        scratch_shapes=[pltpu.SemaphoreType.DMA((6,)), pltpu.SemaphoreType.DMA((6,)), pltpu.SemaphoreType.DMA],
        compiler_params=pltpu.CompilerParams(has_side_effects=True),
    )(shard)


def _swap_halves(buf, name):
    nb, rws, cols = buf.shape
    half = rws // 2

    def body(b_ref, own_ref, sib_ref, send_sem, recv_sem, local_sem):
        x, y, c, _ = _place()
        keep = b_ref.at[:, pl.ds(pl.multiple_of(c * half, 16), half), :]
        give = b_ref.at[:, pl.ds(pl.multiple_of((1 - c) * half, 16), half), :]
        mine = pltpu.make_async_copy(keep, own_ref, local_sem)
        mine.start()
        cp = pltpu.make_async_remote_copy(src_ref=give, dst_ref=sib_ref, send_sem=send_sem, recv_sem=recv_sem,
                                          device_id=(x, y, 1 - c), device_id_type=MESH)
        cp.start()
        cp.wait()
        mine.wait()

    shp = jax.ShapeDtypeStruct((nb, half, cols), buf.dtype)
    return pl.pallas_call(
        body, name=name,
        in_specs=[_ANY], out_specs=[_ANY, _ANY], out_shape=[shp, shp],
        scratch_shapes=[pltpu.SemaphoreType.DMA, pltpu.SemaphoreType.DMA, pltpu.SemaphoreType.DMA],
        compiler_params=pltpu.CompilerParams(has_side_effects=True),
    )(buf)


def _add2(a, b, name):
    nb, rws, cols = a.shape
    t = _tile(rws, 256)
    if rws % t:
        t = rws

    def body(a_ref, b_ref, o_ref):
        o_ref[...] = (a_ref[...].astype(F32) + b_ref[...].astype(F32)).astype(o_ref.dtype)

    blk = pl.BlockSpec((1, t, cols), lambda i, j: (i, j, 0))
    return pl.pallas_call(
        body, name=name, grid=(nb, rws // t), in_specs=[blk, blk], out_specs=blk,
        out_shape=jax.ShapeDtypeStruct(a.shape, a.dtype),
        compiler_params=_params(("parallel", "parallel")),
    )(a, b)


def _exchange_chips(pre, name):
    nb, half, cols = pre.shape

    def body(p_ref, out_ref, send_sems, recv_sems, local_sem):
        x, y, c, others = _place()
        me = 2 * x + y
        mine = pltpu.make_async_copy(p_ref.at[me], out_ref.at[me], local_sem)
        mine.start()
        sends = []
        for j, (ox, oy) in enumerate(others):
            cp = pltpu.make_async_remote_copy(src_ref=p_ref.at[2 * ox + oy], dst_ref=out_ref.at[me],
                                              send_sem=send_sems.at[j], recv_sem=recv_sems.at[j],
                                              device_id=(ox, oy, c), device_id_type=MESH)
            cp.start()
            sends.append(cp)
        for j, (ox, oy) in enumerate(others):
            blk = out_ref.at[2 * ox + oy]
            pltpu.make_async_remote_copy(src_ref=blk, dst_ref=blk, send_sem=send_sems.at[j],
                                         recv_sem=recv_sems.at[j], device_id=(x, y, c),
                                         device_id_type=MESH).wait_recv()
        for cp in sends:
            cp.wait_send()
        mine.wait()

    return pl.pallas_call(
        body, name=name,
        in_specs=[_ANY], out_specs=_ANY, out_shape=jax.ShapeDtypeStruct(pre.shape, pre.dtype),
        scratch_shapes=[pltpu.SemaphoreType.DMA((3,)), pltpu.SemaphoreType.DMA((3,)), pltpu.SemaphoreType.DMA],
        compiler_params=pltpu.CompilerParams(has_side_effects=True),
    )(pre)


def _add4(parts, name):
    nb, half, cols = parts.shape
    t = _tile(half, 256)
    if half % t:
        t = half

    def body(p_ref, o_ref):
        acc = p_ref[0].astype(F32)
        for k in range(1, nb):
            acc = acc + p_ref[k].astype(F32)
        o_ref[...] = acc

    return pl.pallas_call(
        body, name=name, grid=(half // t,),
        in_specs=[pl.BlockSpec((nb, t, cols), lambda i: (0, i, 0))],
        out_specs=pl.BlockSpec((t, cols), lambda i: (i, 0)),
        out_shape=jax.ShapeDtypeStruct((half, cols), F32),
        compiler_params=_params(("parallel",)),
    )(parts)


def _join_halves(mine_half, name):
    half, cols = mine_half.shape

    def body(h_ref, out_ref, send_sem, recv_sem, local_sem):
        x, y, c, _ = _place()
        dst = out_ref.at[pl.ds(pl.multiple_of(c * half, 8), half), :]
        mine = pltpu.make_async_copy(h_ref, dst, local_sem)
        mine.start()
        cp = pltpu.make_async_remote_copy(src_ref=h_ref, dst_ref=dst, send_sem=send_sem, recv_sem=recv_sem,
                                          device_id=(x, y, 1 - c), device_id_type=MESH)
        cp.start()
        cp.wait()
        mine.wait()

    return pl.pallas_call(
        body, name=name,
        in_specs=[_ANY], out_specs=_ANY, out_shape=jax.ShapeDtypeStruct((2 * half, cols), mine_half.dtype),
        scratch_shapes=[pltpu.SemaphoreType.DMA, pltpu.SemaphoreType.DMA, pltpu.SemaphoreType.DMA],
        compiler_params=pltpu.CompilerParams(has_side_effects=True),
    )(mine_half)


def _reduce_scatter_chips(buf, tag):
    own, sib = _swap_halves(buf, "rs_swap_" + tag)
    pre = _add2(own, sib, "rs_add2_" + tag)
    parts = _exchange_chips(pre, "rs_xchg_" + tag)
    red = _add4(parts, "rs_add4_" + tag)
    return _join_halves(red, "rs_join_" + tag)


MAX_DMA_BYTES = 2 * 1024 * 1024
ROW_ALIGN = 16


def _pieces(rows, row_bytes):
    n = max(1, -(-(rows * row_bytes) // MAX_DMA_BYTES))
    step = -(-(-(-rows // n)) // ROW_ALIGN) * ROW_ALIGN
    return [(r, min(step, rows - r)) for r in range(0, rows, step)]


def _half_plan(arrays, row_axis):
    plan = []
    for a, arr in enumerate(arrays):
        row_bytes = math.prod(arr.shape[row_axis + 1:]) * arr.dtype.itemsize * (arr.shape[0] if row_axis else 1)
        plan += [(a, r0, nr) for r0, nr in _pieces(arr.shape[row_axis] // 2, row_bytes)]
    return plan


def _rows(start, size):
    return pl.ds(pl.multiple_of(start, ROW_ALIGN), size)


def _remote(src, dst, send_sems, recv_sems, k, to):
    return pltpu.make_async_remote_copy(src_ref=src, dst_ref=dst, send_sem=send_sems.at[k], recv_sem=recv_sems.at[k],
                                        device_id=to, device_id_type=MESH)


def _comm_call(body, name, ins, out_shapes, n_remote, n_local, aliases=None):
    return pl.pallas_call(
        body, name=name,
        in_specs=[_ANY] * len(ins), out_specs=[_ANY] * len(out_shapes), out_shape=out_shapes,
        scratch_shapes=[pltpu.SemaphoreType.DMA((n_remote,)), pltpu.SemaphoreType.DMA((n_remote,)),
                        pltpu.SemaphoreType.DMA((max(n_local, 1),))],
        input_output_aliases=aliases or {},
        compiler_params=pltpu.CompilerParams(has_side_effects=True),
    )(*ins)


def _gather_ici(shards, name):
    n = len(shards)
    plan = _half_plan(shards, 0)

    def body(*refs):
        x_refs, out_refs = refs[:n], refs[n:2 * n]
        send_sems, recv_sems, local_sems = refs[2 * n:]
        x, y, c, others = _place()
        me = 2 * x + y
        local = [pltpu.make_async_copy(x_refs[a], out_refs[a].at[me], local_sems.at[a]) for a in range(n)]
        for cp in local:
            cp.start()
        sends = []
        for i, (a, r0, nr) in enumerate(plan):
            rows = _rows(c * (shards[a].shape[0] // 2) + r0, nr)
            for j, (ox, oy) in enumerate(others):
                cp = _remote(x_refs[a].at[rows, :], out_refs[a].at[me, rows, :], send_sems, recv_sems, 3 * i + j,
                             (ox, oy, c))
                cp.start()
                sends.append(cp)
        for i, (a, r0, nr) in enumerate(plan):
            rows = _rows(c * (shards[a].shape[0] // 2) + r0, nr)
            for j, (ox, oy) in enumerate(others):
                blk = out_refs[a].at[2 * ox + oy, rows, :]
                _remote(blk, blk, send_sems, recv_sems, 3 * i + j, (x, y, c)).wait_recv()
        for cp in sends:
            cp.wait_send()
        for cp in local:
            cp.wait()

    outs = [jax.ShapeDtypeStruct((N_CHIPS,) + sh.shape, sh.dtype) for sh in shards]
    return _comm_call(body, name, shards, outs, 3 * len(plan), n)


def _gather_d2d(lands, name):
    n = len(lands)
    plan = _half_plan(lands, 1)
    plan = [(a, r0, nr) for a, r0, nr in plan]

    def body(*refs):
        out_refs = refs[n:2 * n]
        send_sems, recv_sems, _ = refs[2 * n:]
        x, y, c, others = _place()
        sends = []
        for i, (a, r0, nr) in enumerate(plan):
            rows = _rows(c * (lands[a].shape[1] // 2) + r0, nr)
            for j, (ox, oy) in enumerate(others):
                blk = out_refs[a].at[2 * ox + oy, rows, :]
                cp = _remote(blk, blk, send_sems, recv_sems, 3 * i + j, (x, y, 1 - c))
                cp.start()
                sends.append(cp)
        for i, (a, r0, nr) in enumerate(plan):
            rows = _rows((1 - c) * (lands[a].shape[1] // 2) + r0, nr)
            for j, (ox, oy) in enumerate(others):
                blk = out_refs[a].at[2 * ox + oy, rows, :]
                _remote(blk, blk, send_sems, recv_sems, 3 * i + j, (x, y, c)).wait_recv()
        for cp in sends:
            cp.wait_send()

    outs = [jax.ShapeDtypeStruct(a.shape, a.dtype) for a in lands]
    return _comm_call(body, name, lands, outs, 3 * len(plan), 0, aliases={a: a for a in range(n)})


def _rs_swap(ts, name):
    n = len(ts)
    plan = _half_plan(ts, 1)

    def body(*refs):
        t_refs, out_refs = refs[:n], refs[n:2 * n]
        send_sems, recv_sems, _ = refs[2 * n:]
        x, y, c, _o = _place()
        sends = []
        for i, (a, r0, nr) in enumerate(plan):
            src = t_refs[a].at[:, _rows((1 - c) * (ts[a].shape[1] // 2) + r0, nr), :]
            cp = _remote(src, out_refs[a].at[:, pl.ds(r0, nr), :], send_sems, recv_sems, i, (x, y, 1 - c))
            cp.start()
            sends.append(cp)
        for i, (a, r0, nr) in enumerate(plan):
            blk = out_refs[a].at[:, pl.ds(r0, nr), :]
            _remote(blk, blk, send_sems, recv_sems, i, (x, y, c)).wait_recv()
        for cp in sends:
            cp.wait_send()

    outs = [jax.ShapeDtypeStruct((t.shape[0], t.shape[1] // 2, t.shape[2]), t.dtype) for t in ts]
    return _comm_call(body, name, ts, outs, len(plan), 0)


def _add_half(t, got, c_idx, name):
    nb, k, cols = t.shape
    half = k // 2

    def body(c_ref, t_ref, g_ref, o_ref):
        del c_ref
        o_ref[...] = (t_ref[...].astype(F32) + g_ref[...].astype(F32)).astype(o_ref.dtype)

    grid_spec = pltpu.PrefetchScalarGridSpec(
        num_scalar_prefetch=1, grid=(nb,),
        in_specs=[pl.BlockSpec((1, half, cols), lambda i, c: (i, c[0], 0)),
                  pl.BlockSpec((1, half, cols), lambda i, c: (i, 0, 0))],
        out_specs=pl.BlockSpec((1, half, cols), lambda i, c: (i, 0, 0)))
    return pl.pallas_call(
        body, name=name, grid_spec=grid_spec, out_shape=jax.ShapeDtypeStruct(got.shape, got.dtype),
        compiler_params=_params(("parallel",)),
    )(c_idx, t, got)


def _rs_xchg(pres, name):
    n = len(pres)
    plan = []
    for a, p in enumerate(pres):
        plan += [(a, r0, nr) for r0, nr in _pieces(p.shape[1], p.shape[2] * p.dtype.itemsize)]

    def body(*refs):
        p_refs, out_refs = refs[:n], refs[n:2 * n]
        send_sems, recv_sems, local_sems = refs[2 * n:]
        x, y, c, others = _place()
        me = 2 * x + y
        local = [pltpu.make_async_copy(p_refs[a].at[me], out_refs[a].at[me], local_sems.at[a]) for a in range(n)]
        for cp in local:
            cp.start()
        sends = []
        for i, (a, r0, nr) in enumerate(plan):
            for j, (ox, oy) in enumerate(others):
                cp = _remote(p_refs[a].at[2 * ox + oy, pl.ds(r0, nr), :], out_refs[a].at[me, pl.ds(r0, nr), :],
                             send_sems, recv_sems, 3 * i + j, (ox, oy, c))
                cp.start()
                sends.append(cp)
        for i, (a, r0, nr) in enumerate(plan):
            for j, (ox, oy) in enumerate(others):
                blk = out_refs[a].at[2 * ox + oy, pl.ds(r0, nr), :]
                _remote(blk, blk, send_sems, recv_sems, 3 * i + j, (x, y, c)).wait_recv()
        for cp in sends:
            cp.wait_send()
        for cp in local:
            cp.wait()

    outs = [jax.ShapeDtypeStruct(p.shape, p.dtype) for p in pres]
    return _comm_call(body, name, pres, outs, 3 * len(plan), n)


def _rs_join(reds, name):
    n = len(reds)
    plan = []
    for a, r in enumerate(reds):
        plan += [(a, r0, nr) for r0, nr in _pieces(r.shape[0], r.shape[1] * r.dtype.itemsize)]

    def body(*refs):
        r_refs, out_refs = refs[:n], refs[n:2 * n]
        send_sems, recv_sems, local_sems = refs[2 * n:]
        x, y, c, _o = _place()
        local = [pltpu.make_async_copy(r_refs[a], out_refs[a].at[_rows(c * reds[a].shape[0], reds[a].shape[0]), :],
                                       local_sems.at[a]) for a in range(n)]
        for cp in local:
            cp.start()
        sends = []
        for i, (a, r0, nr) in enumerate(plan):
            dst = out_refs[a].at[_rows(c * reds[a].shape[0] + r0, nr), :]
            cp = _remote(r_refs[a].at[pl.ds(r0, nr), :], dst, send_sems, recv_sems, i, (x, y, 1 - c))
            cp.start()
            sends.append(cp)
        for i, (a, r0, nr) in enumerate(plan):
            blk = out_refs[a].at[_rows((1 - c) * reds[a].shape[0] + r0, nr), :]
            _remote(blk, blk, send_sems, recv_sems, i, (x, y, c)).wait_recv()
        for cp in sends:
            cp.wait_send()
        for cp in local:
            cp.wait()

    outs = [jax.ShapeDtypeStruct((2 * r.shape[0], r.shape[1]), r.dtype) for r in reds]
    return _comm_call(body, name, reds, outs, len(plan), n)


def _reduce_scatter_list(ts, c_idx, tag):
    got = _rs_swap(ts, "rs_swap_" + tag)
    pres = [_add_half(t, g, c_idx, f"rs_add2_{tag}_{a}") for a, (t, g) in enumerate(zip(ts, got))]
    parts = _rs_xchg(pres, "rs_xchg_" + tag)
    reds = [_add4(p, f"rs_add4_{tag}_{a}") for a, p in enumerate(parts)]
    return _rs_join(reds, "rs_join_" + tag)


def _pack_rows(pieces, rows, dtype):
    flat = jnp.concatenate([p.astype(dtype).reshape(-1) for p in pieces])
    return jnp.pad(flat, (0, rows * PACK_COLS - flat.shape[0])).reshape(rows, PACK_COLS)


def _unpack(flat, shapes):
    out, off = [], 0
    for shp in shapes:
        size = math.prod(shp)
        out.append(flat[off:off + size].reshape(shp))
        off += size
    return out


def _rows_for(n_elems, mult):
    rows = -(-n_elems // PACK_COLS)
    return -(-rows // mult) * mult


BIG_SHARDS = [("w_in", (D_MODEL, 1474)), ("w_branch_att", (D_ATT, 256)), ("w_branch_conv", (D_CONV, 256)),
              ("w_branch_sgu", (D_SGU, 256)), ("w_out", (256, D_MODEL)), ("w_ffn_up", (D_MODEL, FF_BLK)),
              ("w_ffn_down", (D_FF // N_CHIPS, D_MODEL))]
SMALL_SHARDS = [("b_gate", (3, 256)), ("conv_mix_w", (3, 64)), ("conv_ffn_w", (3, FF_BLK))]
REPLICATED = [("pre_mix_g", (D_MODEL,)), ("post_mix_g", (D_MODEL,)), ("pre_ffn_g", (D_MODEL,)),
              ("post_ffn_g", (D_MODEL,)), ("b_forget", (N_HEADS,)), ("sgu_ln_g", (D_SGU,)), ("sgu_ln_b", (D_SGU,)),
              ("sgu_w", (N_GROUPS, CHUNK, CHUNK)), ("sgu_b", (N_GROUPS, CHUNK))]
WEIGHT_ORDER = ["pre_mix_g", "post_mix_g", "pre_ffn_g", "post_ffn_g", "w_in", "b_forget", "b_gate", "conv_mix_w",
                "sgu_ln_g", "sgu_ln_b", "sgu_w", "sgu_b", "w_branch_att", "w_branch_conv", "w_branch_sgu", "w_out",
                "w_ffn_up", "conv_ffn_w", "w_ffn_down"]

_SMALL_ELEMS = sum(math.prod(s) for _, s in SMALL_SHARDS)
_REP_ELEMS = sum(math.prod(s) for _, s in REPLICATED)
_REP_QUARTER = -(-(DEPTH * _REP_ELEMS) // N_CHIPS)
SMALL_PARAM_ROWS = _rows_for(DEPTH * _SMALL_ELEMS, 32)
SMALL_ROWS = _rows_for(DEPTH * _SMALL_ELEMS + _REP_QUARTER, 32)
IN_WIDTH = 5896
IN_SHARD = IN_WIDTH // N_CHIPS
IN_SHARD_PAD = 1536
IN_PAD = 6144


def _gather_small(wts):
    shard = _pack_rows([wts[n] for n, _ in SMALL_SHARDS], SMALL_PARAM_ROWS, F32)
    full = _all_gather_chips(shard, "gather_small_params").reshape(N_CHIPS, -1)
    per_chip = [_unpack(full[j], [(DEPTH,) + s for _, s in SMALL_SHARDS]) for j in range(N_CHIPS)]
    return {n: jnp.concatenate([per_chip[j][i] for j in range(N_CHIPS)], axis=-1)
            for i, (n, _) in enumerate(SMALL_SHARDS)}


def _gather_layer(wts, l):
    names = [n for n, _ in BIG_SHARDS]
    shards = [wts[n][l].astype(BF16) for n in names]
    lands = _gather_d2d(_gather_ici(shards, "gather_ici"), "gather_d2d")
    return dict(zip(names, lands))


def _pad_rows(a, rows):
    return jnp.pad(a, ((0, rows - a.shape[0]), (0, 0)))


def _whole_cols(land):
    return land.transpose(1, 0, 2).reshape(land.shape[1], -1)


_O_F = 3 * D_ATT
_O_B = _O_F + N_HEADS
_O_GL = _O_B + 3 * D_CONV + 2 * D_SGU


def _prep_layer(wts, lands, small, l):
    w_in = _whole_cols(lands["w_in"])
    up = lands["w_ffn_up"]
    cf = small["conv_ffn_w"][l]
    blk = lambda a, j: a[:, j * FF_BLK:(j + 1) * FF_BLK]
    return {
        "w_p": jnp.concatenate([w_in[:, _O_GL:], w_in[:, :_O_F], w_in[:, _O_B:_O_GL]], axis=1),
        "w_in_pad": jnp.pad(w_in, ((0, 0), (0, IN_PAD - IN_WIDTH))),
        "wf_t": _pad_rows(w_in[:, _O_F:_O_B].T, F_ROWS),
        "b_forget": _pad_rows(wts["b_forget"][l].reshape(N_HEADS, 1), F_ROWS),
        "b_gate": _pad_rows(small["b_gate"][l], 8),
        "conv_mix_w": _pad_rows(small["conv_mix_w"][l], 8),
        "w_att": _whole_cols(lands["w_branch_att"]), "w_conv": _whole_cols(lands["w_branch_conv"]),
        "w_sgu": _whole_cols(lands["w_branch_sgu"]),
        "w_out": lands["w_out"].reshape(D_MODEL, D_MODEL),
        "w_up": jnp.concatenate([up[0], up[2], up[1], up[3]], axis=1),
        "conv_ffn_w": _pad_rows(jnp.concatenate([blk(cf, 0), blk(cf, 2), blk(cf, 1), blk(cf, 3)], axis=1), 8),
        "w_down": lands["w_ffn_down"].reshape(D_FF, D_MODEL),
        "pre_mix_g": wts["pre_mix_g"][l].reshape(1, -1), "post_mix_g": wts["post_mix_g"][l].reshape(1, -1),
        "pre_ffn_g": wts["pre_ffn_g"][l].reshape(1, -1), "post_ffn_g": wts["post_ffn_g"][l].reshape(1, -1),
        "ln_g": wts["sgu_ln_g"][l].reshape(1, -1), "ln_b": wts["sgu_ln_b"][l].reshape(1, -1),
        "sgu_w": wts["sgu_w"][l],
        "sgu_bias": jnp.repeat(wts["sgu_b"][l].T, HEAD_DIM, axis=1),
    }


def _layer_fwd(x, p):
    s = x.shape[0]
    xn = _rms_fwd(x, p["pre_mix_g"], "rms_pre_mix")
    h = _mm(xn, p["w_p"], "nn", BF16, "mm_in", s, 256, D_MODEL)
    f_row = _mm(p["wf_t"], xn, "nt", F32, "mm_forget", F_ROWS, 2048, D_MODEL)
    c = _gate_fwd(f_row, p["b_forget"], "gate_fwd")
    o, o_f32, lse = _attn_fwd(h, c, "attn_fwd")
    yc = _sconv_fwd(h, p["conv_mix_w"], "sconv_fwd")
    ys = _sgu_fwd(h, p["ln_g"], p["ln_b"], p["sgu_w"], p["sgu_bias"], "sgu_fwd")
    merged = _merge_fwd(h, (o, yc, ys), (p["w_att"], p["w_conv"], p["w_sgu"]), p["b_gate"], "merge_fwd")
    mo = _mm(merged, p["w_out"], "nn", F32, "mm_out", 2048, 512, D_MODEL)
    x1 = _resid_post(x, mo, p["post_mix_g"], "post_mix")
    xn2 = _rms_fwd(x1, p["pre_ffn_g"], "rms_pre_ffn")
    h2 = _mm(xn2, p["w_up"], "nn", BF16, "mm_up", 2048, 512, D_MODEL)
    pact = _ffn_act_fwd(h2, p["conv_ffn_w"], "ffn_act_fwd")
    ff = _mm(pact, p["w_down"], "nn", F32, "mm_down", 2048, 512, FF_BLK)
    x2 = _resid_post(x1, ff, p["post_ffn_g"], "post_ffn")
    saved = dict(x=x, xn=xn, h=h, f_row=f_row, c=c, o=o, o_f32=o_f32, lse=lse, yc=yc, ys=ys, merged=merged, mo=mo, x1=x1,
                 xn2=xn2, h2=h2, pact=pact, ff=ff)
    return x2, saved


def _layer_bwd(dx2, p, sv):
    s = dx2.shape[0]
    g = {}
    same = lambda b: b
    dff, g["post_ffn_g"] = _rms_bwd(sv["ff"], p["post_ffn_g"], [dx2], None, BF16, "post_ffn_bwd")
    dpact = _mm(dff, p["w_down"], "nt", BF16, "mm_down_dx", 1024, FF_BLK, D_MODEL)
    t_down = _mm(sv["pact"], dff, "tn", BF16, "mm_down_dw", 256, D_MODEL, s).reshape(N_CHIPS, -1, D_MODEL)
    dhc2 = _ffn_act_bwd(sv["h2"], p["conv_ffn_w"], dpact, "ffn_act_bwd")
    dh2, dconv_ffn = _dwconv_bwd(sv["h2"], p["conv_ffn_w"], dhc2, "ffn_conv_bwd")
    dxn2 = _mm(dh2, p["w_up"], "nt", F32, "mm_up_dx", 1024, D_MODEL, FF_BLK)
    t_up = _mm(sv["xn2"], dh2, "tn", BF16, "mm_up_dw", 512, FF_BLK, s, chip_of=lambda b: (b % 2) * 2 + b // 2)
    dx1, g["pre_ffn_g"] = _rms_bwd(sv["x1"], p["pre_ffn_g"], [dxn2], dx2, F32, "pre_ffn_bwd")
    dmo, g["post_mix_g"] = _rms_bwd(sv["mo"], p["post_mix_g"], [dx1], None, BF16, "post_mix_bwd")
    dmerged = _mm(dmo, p["w_out"], "nt", F32, "mm_out_dx", 2048, 512, D_MODEL)
    t_out = _mm(sv["merged"], dmo, "tn", BF16, "mm_out_dw", 512, D_MODEL, s).reshape(N_CHIPS, -1, D_MODEL)
    acts = (sv["o"], sv["yc"], sv["ys"])
    ws = (p["w_att"], p["w_conv"], p["w_sgu"])
    dy_a, dy_c, dy_s, dgl, db_gate = _merge_bwd(sv["h"], acts, ws, p["b_gate"], dmerged, "merge_bwd")
    do = _mm(dy_a, p["w_att"], "nt", BF16, "mm_att_dx", 2048, D_ATT, D_MODEL)
    dyc = _mm(dy_c, p["w_conv"], "nt", BF16, "mm_conv_dx", 2048, D_CONV, D_MODEL)
    dys = _mm(dy_s, p["w_sgu"], "nt", BF16, "mm_sgu_dx", 2048, D_SGU, D_MODEL)
    t_att = _mm(sv["o"], dy_a, "tn", BF16, "mm_att_dw", D_ATT, 256, s, chip_of=same)
    t_conv = _mm(sv["yc"], dy_c, "tn", BF16, "mm_conv_dw", D_CONV, 256, s, chip_of=same)
    t_sgu = _mm(sv["ys"], dy_s, "tn", BF16, "mm_sgu_dw", D_SGU, 256, s, chip_of=same)
    d_conv, dconv_mix = _sconv_bwd(sv["h"], p["conv_mix_w"], dyc, "sconv_bwd")
    d_sgu, g["sgu_ln_g"], g["sgu_ln_b"], g["sgu_w"], dbias = _sgu_bwd(
        sv["h"], p["ln_g"], p["ln_b"], p["sgu_w"], p["sgu_bias"], dys, "sgu_bwd")
    dq, dk, dv, dc_even, dc_odd = _attn_bwd(sv["h"], sv["c"], sv["o_f32"], sv["lse"], do, "attn_bwd")
    df, db_forget = _gate_bwd(sv["f_row"], p["b_forget"], dc_even, dc_odd, "gate_bwd")
    dh = jnp.concatenate([dq.astype(BF16), dk, dv, df[:N_HEADS].T, d_conv, d_sgu, dgl,
                          jnp.zeros((s, IN_PAD - IN_WIDTH), BF16)], axis=1)
    dxn = _mm(dh, p["w_in_pad"], "nt", F32, "mm_in_dx", 512, D_MODEL, 2048)
    dw_in = _mm(sv["xn"], dh, "tn", F32, "mm_in_dw", D_MODEL, 512, s)
    t_in = jnp.stack([dw_in[:, j * IN_SHARD:j * IN_SHARD + IN_SHARD_PAD] for j in range(N_CHIPS)]).astype(BF16)
    dx, g["pre_mix_g"] = _rms_bwd(sv["x"], p["pre_mix_g"], [dxn], dx1, F32, "pre_mix_bwd")
    blk = lambda a, j: a[:, j * FF_BLK:(j + 1) * FF_BLK]
    g["conv_ffn_w"] = jnp.concatenate([blk(dconv_ffn, 0), blk(dconv_ffn, 2), blk(dconv_ffn, 1),
                                       blk(dconv_ffn, 3)], axis=1)[:3]
    g["conv_mix_w"] = dconv_mix[:3]
    g["b_gate"] = db_gate[:3]
    g["b_forget"] = db_forget[:N_HEADS, 0]
    g["sgu_b"] = jnp.sum(dbias.reshape(CHUNK, N_GROUPS, HEAD_DIM), axis=-1).T
    for n in ("pre_mix_g", "post_mix_g", "pre_ffn_g", "post_ffn_g", "sgu_ln_g", "sgu_ln_b"):
        g[n] = g[n].reshape(-1)
    return dx, [t_in, t_att, t_conv, t_sgu, t_out, t_up, t_down], g


def _shard_cols(a, j):
    w = a.shape[-1] // N_CHIPS
    return a[..., j * w:(j + 1) * w]


def kernel(x, pre_mix_g, post_mix_g, pre_ffn_g, post_ffn_g, w_in, b_forget, b_gate, conv_mix_w, sgu_ln_g, sgu_ln_b, sgu_w, sgu_b, w_branch_att, w_branch_conv, w_branch_sgu, w_out, w_ffn_up, conv_ffn_w, w_ffn_down, loss_target, m_pre_mix_g, m_post_mix_g, m_pre_ffn_g, m_post_ffn_g, m_w_in, m_b_forget, m_b_gate, m_conv_mix_w, m_sgu_ln_g, m_sgu_ln_b, m_sgu_w, m_sgu_b, m_w_branch_att, m_w_branch_conv, m_w_branch_sgu, m_w_out, m_w_ffn_up, m_conv_ffn_w, m_w_ffn_down, v_pre_mix_g, v_post_mix_g, v_pre_ffn_g, v_post_ffn_g, v_w_in, v_b_forget, v_b_gate, v_conv_mix_w, v_sgu_ln_g, v_sgu_ln_b, v_sgu_w, v_sgu_b, v_w_branch_att, v_w_branch_conv, v_w_branch_sgu, v_w_out, v_w_ffn_up, v_conv_ffn_w, v_w_ffn_down):
    wts = dict(pre_mix_g=pre_mix_g, post_mix_g=post_mix_g, pre_ffn_g=pre_ffn_g, post_ffn_g=post_ffn_g, w_in=w_in,
               b_forget=b_forget, b_gate=b_gate, conv_mix_w=conv_mix_w, sgu_ln_g=sgu_ln_g, sgu_ln_b=sgu_ln_b,
               sgu_w=sgu_w, sgu_b=sgu_b, w_branch_att=w_branch_att, w_branch_conv=w_branch_conv,
               w_branch_sgu=w_branch_sgu, w_out=w_out, w_ffn_up=w_ffn_up, conv_ffn_w=conv_ffn_w,
               w_ffn_down=w_ffn_down)
    moms = dict(pre_mix_g=m_pre_mix_g, post_mix_g=m_post_mix_g, pre_ffn_g=m_pre_ffn_g, post_ffn_g=m_post_ffn_g,
                w_in=m_w_in, b_forget=m_b_forget, b_gate=m_b_gate, conv_mix_w=m_conv_mix_w, sgu_ln_g=m_sgu_ln_g,
                sgu_ln_b=m_sgu_ln_b, sgu_w=m_sgu_w, sgu_b=m_sgu_b, w_branch_att=m_w_branch_att,
                w_branch_conv=m_w_branch_conv, w_branch_sgu=m_w_branch_sgu, w_out=m_w_out, w_ffn_up=m_w_ffn_up,
                conv_ffn_w=m_conv_ffn_w, w_ffn_down=m_w_ffn_down)
    vels = dict(pre_mix_g=v_pre_mix_g, post_mix_g=v_post_mix_g, pre_ffn_g=v_pre_ffn_g, post_ffn_g=v_post_ffn_g,
                w_in=v_w_in, b_forget=v_b_forget, b_gate=v_b_gate, conv_mix_w=v_conv_mix_w, sgu_ln_g=v_sgu_ln_g,
                sgu_ln_b=v_sgu_ln_b, sgu_w=v_sgu_w, sgu_b=v_sgu_b, w_branch_att=v_w_branch_att,
                w_branch_conv=v_w_branch_conv, w_branch_sgu=v_w_branch_sgu, w_out=v_w_out, w_ffn_up=v_w_ffn_up,
                conv_ffn_w=v_conv_ffn_w, w_ffn_down=v_w_ffn_down)

    c_idx = lax.axis_index("c").astype(jnp.int32).reshape(1)
    small = _gather_small(wts)

    xs = x[0]
    layers, saved = [], []
    for l in range(DEPTH):
        p = _prep_layer(wts, _gather_layer(wts, l), small, l)
        xs, sv = _layer_fwd(xs, p)
        layers.append(p)
        saved.append(sv)
    dy, loss_part = _loss_head(xs, loss_target[0], "loss_head")
    loss = lax.psum(loss_part[0, 0], ("x", "y", "c"))

    big_red = [None] * DEPTH
    small_grads = [None] * DEPTH
    for l in reversed(range(DEPTH)):
        dy, ts, small_grads[l] = _layer_bwd(dy, layers[l], saved[l])
        big_red[l] = _reduce_scatter_list(ts, c_idx, "big")
    grad_x = dy[None]

    rep_flat = jnp.concatenate([small_grads[l][n].reshape(-1) for l in range(DEPTH) for n, _ in REPLICATED])
    rep_flat = jnp.pad(rep_flat, (0, N_CHIPS * _REP_QUARTER - rep_flat.shape[0]))
    rows = []
    for j in range(N_CHIPS):
        pieces = [_shard_cols(small_grads[l][n], j) for l in range(DEPTH) for n, _ in SMALL_SHARDS]
        pieces.append(rep_flat[j * _REP_QUARTER:(j + 1) * _REP_QUARTER])
        rows.append(_pack_rows(pieces, SMALL_ROWS, F32))
    small_red = _reduce_scatter_chips(jnp.stack(rows), "small")
    small_all = _all_gather_chips(small_red, "gather_small").reshape(N_CHIPS, -1)

    grads = {}
    for i, (n, _) in enumerate(BIG_SHARDS):
        grads[n] = jnp.stack([big_red[l][i][:, :IN_SHARD] if n == "w_in" else big_red[l][i] for l in range(DEPTH)])
    mine_small = small_red.reshape(-1)
    parts = _unpack(mine_small, [s for _ in range(DEPTH) for _, s in SMALL_SHARDS])
    for i, (n, _) in enumerate(SMALL_SHARDS):
        grads[n] = jnp.stack([parts[l * len(SMALL_SHARDS) + i] for l in range(DEPTH)])
    off = DEPTH * _SMALL_ELEMS
    rep_all = jnp.concatenate([small_all[j, off:off + _REP_QUARTER] for j in range(N_CHIPS)])
    parts = _unpack(rep_all, [s for _ in range(DEPTH) for _, s in REPLICATED])
    for i, (n, _) in enumerate(REPLICATED):
        grads[n] = jnp.stack([parts[l * len(REPLICATED) + i] for l in range(DEPTH)])

    deltas, new_m, new_v = {}, {}, {}
    for n in WEIGHT_ORDER:
        deltas[n], new_m[n], new_v[n] = _adamw(wts[n], grads[n], moms[n], vels[n], "adamw_" + n)
    return (loss, grad_x, *[grads[n] for n in WEIGHT_ORDER], *[deltas[n] for n in WEIGHT_ORDER],
            *[new_m[n] for n in WEIGHT_ORDER], *[new_v[n] for n in WEIGHT_ORDER])
```

```python
import functools
import math

import jax
import jax.numpy as jnp
from jax import lax
from jax.experimental import pallas as pl
from jax.experimental.pallas import tpu as pltpu

F32 = jnp.float32
BF16 = jnp.bfloat16
MXU_DTYPE = jnp.bfloat16

D_MODEL = 1024
HEAD_DIM = 64
N_HEADS = 8
D_ATT = 512
D_CONV = 256
D_SGU = 256
N_GROUPS = 4
CHUNK = 128
D_FF = 2816
DEPTH = 4
RMS_EPS = 1e-6
LN_EPS = 1e-5
N_CHIPS = 4
LANES = 128
PACK_COLS = 1024
HALO = 16

ADAM_LR = 0.001
ADAM_B1 = 0.9
ADAM_B2 = 0.999
ADAM_EPS = 1e-08
ADAM_WD = 0.01
ADAM_STEP = 10

OFF_GL = 0
OFF_Q = 3 * D_MODEL
OFF_K = OFF_Q + D_ATT
OFF_V = OFF_K + D_ATT
OFF_BG = OFF_V + D_ATT
OFF_CG = OFF_BG + D_CONV
OFF_HC = OFF_CG + D_CONV
OFF_U = OFF_HC + D_CONV
OFF_VS = OFF_U + D_SGU
W_P = OFF_VS + D_SGU
F_ROWS = 16

VMEM_LIMIT = 56 * 1024 * 1024
MESH = pl.DeviceIdType.MESH


def _params(sem=None):
    if sem is None:
        return pltpu.CompilerParams(vmem_limit_bytes=VMEM_LIMIT)
    return pltpu.CompilerParams(dimension_semantics=sem, vmem_limit_bytes=VMEM_LIMIT)


def _tile(dim, pref):
    if dim <= pref:
        return dim
    if dim % pref == 0:
        return pref
    return dim


_DIMS = {"nn": (((1,), (0,)), ((), ())), "nt": (((1,), (1,)), ((), ())), "tn": (((0,), (0,)), ((), ()))}


def _mm(a, b, mode, out_dtype, name, tm, tn, tk, chip_of=None):
    if mode == "tn":
        K, M = a.shape
    else:
        M, K = a.shape
    N = b.shape[0] if mode == "nt" else b.shape[1]
    tm, tn, tk = _tile(M, tm), _tile(N // N_CHIPS if chip_of else N, tn), _tile(K, tk)
    nk = K // tk
    dims = _DIMS[mode]

    def body(a_ref, b_ref, o_ref, *acc):
        part = lax.dot_general(a_ref[...].astype(MXU_DTYPE), b_ref[...].astype(MXU_DTYPE), dims,
                               preferred_element_type=F32)
        if nk == 1:
            o_ref[...] = part.astype(o_ref.dtype)
        else:
            acc_ref = acc[0]
            k = pl.program_id(2)

            @pl.when(k == 0)
            def _():
                acc_ref[...] = part

            @pl.when(k > 0)
            def _():
                acc_ref[...] += part

            @pl.when(k == nk - 1)
            def _():
                o_ref[...] = acc_ref[...].astype(o_ref.dtype)

    if mode == "tn":
        a_spec = pl.BlockSpec((tk, tm), lambda i, j, k: (k, i))
    else:
        a_spec = pl.BlockSpec((tm, tk), lambda i, j, k: (i, k))
    if mode == "nt":
        b_spec = pl.BlockSpec((tn, tk), lambda i, j, k: (j, k))
    else:
        b_spec = pl.BlockSpec((tk, tn), lambda i, j, k: (k, j))
    if chip_of is None:
        out_spec = pl.BlockSpec((tm, tn), lambda i, j, k: (i, j))
        out_shape = jax.ShapeDtypeStruct((M, N), out_dtype)
    else:
        per = (N // N_CHIPS) // tn
        out_spec = pl.BlockSpec((None, tm, tn), lambda i, j, k: (chip_of(j // per), i, j % per))
        out_shape = jax.ShapeDtypeStruct((N_CHIPS, M, N // N_CHIPS), out_dtype)
    return pl.pallas_call(
        body,
        name=name,
        grid=(M // tm, N // tn, nk),
        in_specs=[a_spec, b_spec],
        out_specs=out_spec,
        out_shape=out_shape,
        scratch_shapes=[pltpu.VMEM((tm, tn), F32)] if nk > 1 else [],
        compiler_params=_params(("parallel", "parallel", "arbitrary")),
    )(a, b)


_GELU_K = math.sqrt(2.0 / math.pi)
_GELU_C = 0.044715


def _gelu(x):
    t = jnp.tanh(_GELU_K * (x + _GELU_C * (x * x * x)))
    return x * (0.5 * (1.0 + t))


def _gelu_and_grad(x):
    x2 = x * x
    t = jnp.tanh(_GELU_K * (x + _GELU_C * (x2 * x)))
    cdf = 0.5 * (1.0 + t)
    dcdf = 0.5 * (1.0 - t * t) * (_GELU_K * (1.0 + 3.0 * _GELU_C * x2))
    return x * cdf, cdf + x * dcdf


def _sigmoid(x):
    return 1.0 / (1.0 + jnp.exp(-x))


def _shift_down(cur, prev, k):
    h = prev.shape[0]
    ext = jnp.concatenate([prev, cur], axis=0)
    return pltpu.roll(ext, k, 0)[h:]


def _shift_up(cur, nxt, k):
    t, h = cur.shape[0], nxt.shape[0]
    ext = jnp.concatenate([cur, nxt], axis=0)
    return pltpu.roll(ext, t + h - k, 0)[:t]


def _row_sum8(x):
    t, c = x.shape
    return jnp.sum(x.reshape(t // 8, 8, c), axis=0)


def _rms_fwd(x, g, name):
    s, d = x.shape
    t = _tile(s, 512)

    def body(x_ref, g_ref, o_ref):
        xv = x_ref[...]
        r = lax.rsqrt(jnp.mean(xv * xv, axis=-1, keepdims=True) + RMS_EPS)
        o_ref[...] = (xv * r * g_ref[...]).astype(o_ref.dtype)

    return pl.pallas_call(
        body, name=name, grid=(s // t,),
        in_specs=[pl.BlockSpec((t, d), lambda i: (i, 0)), pl.BlockSpec((1, d), lambda i: (0, 0))],
        out_specs=pl.BlockSpec((t, d), lambda i: (i, 0)),
        out_shape=jax.ShapeDtypeStruct((s, d), BF16),
        compiler_params=_params(("parallel",)),
    )(x, g)


def _resid_post(x, y, g, name):
    s, d = x.shape
    t = _tile(s, 512)

    def body(x_ref, y_ref, g_ref, o_ref):
        yv = y_ref[...]
        r = lax.rsqrt(jnp.mean(yv * yv, axis=-1, keepdims=True) + RMS_EPS)
        o_ref[...] = x_ref[...] + yv * r * g_ref[...]

    row = pl.BlockSpec((t, d), lambda i: (i, 0))
    return pl.pallas_call(
        body, name=name, grid=(s // t,),
        in_specs=[row, row, pl.BlockSpec((1, d), lambda i: (0, 0))],
        out_specs=row,
        out_shape=jax.ShapeDtypeStruct((s, d), F32),
        compiler_params=_params(("parallel",)),
    )(x, y, g)


def _rms_bwd(xin, g, dys, dres, out_dtype, name):
    s, d = xin.shape
    t = _tile(s, 512)
    n = s // t
    n_dy = len(dys)
    has_res = dres is not None

    def body(*refs):
        x_ref, g_ref = refs[0], refs[1]
        dy_refs = refs[2:2 + n_dy]
        pos = 2 + n_dy
        res_ref = refs[pos] if has_res else None
        pos += 1 if has_res else 0
        dx_ref, dg_ref, acc_ref = refs[pos], refs[pos + 1], refs[pos + 2]
        i = pl.program_id(0)
        xv = x_ref[...]
        dy = dy_refs[0][...].astype(F32)
        for extra in dy_refs[1:]:
            dy = dy + extra[...].astype(F32)
        r = lax.rsqrt(jnp.mean(xv * xv, axis=-1, keepdims=True) + RMS_EPS)
        u = dy * g_ref[...]
        xr = xv * r
        dx = r * (u - xr * jnp.mean(u * xr, axis=-1, keepdims=True))
        if has_res:
            dx = dx + res_ref[...]
        dx_ref[...] = dx.astype(dx_ref.dtype)
        part = _row_sum8(dy * xr)

        @pl.when(i == 0)
        def _():
            acc_ref[...] = part

        @pl.when(i > 0)
        def _():
            acc_ref[...] += part

        @pl.when(i == n - 1)
        def _():
            dg_ref[...] = jnp.sum(acc_ref[...], axis=0, keepdims=True)

    row = pl.BlockSpec((t, d), lambda i: (i, 0))
    vec = pl.BlockSpec((1, d), lambda i: (0, 0))
    ins = [xin, g, *dys] + ([dres] if has_res else [])
    return pl.pallas_call(
        body, name=name, grid=(n,),
        in_specs=[row, vec] + [row] * (n_dy + (1 if has_res else 0)),
        out_specs=[row, vec],
        out_shape=[jax.ShapeDtypeStruct((s, d), out_dtype), jax.ShapeDtypeStruct((1, d), F32)],
        scratch_shapes=[pltpu.VMEM((8, d), F32)],
        compiler_params=_params(("arbitrary",)),
    )(*ins)


def _loss_head(y, target, name):
    s, d = y.shape
    t = _tile(s, 512)
    n = s // t

    def body(y_ref, t_ref, dy_ref, loss_ref, acc_ref):
        i = pl.program_id(0)
        e = y_ref[...] - t_ref[...]
        dy_ref[...] = e * (1.0 / d)
        part = _row_sum8(e * e)

        @pl.when(i == 0)
        def _():
            acc_ref[...] = part

        @pl.when(i > 0)
        def _():
            acc_ref[...] += part

        @pl.when(i == n - 1)
        def _():
            tot = jnp.sum(jnp.sum(acc_ref[...], axis=0, keepdims=True), axis=1, keepdims=True)
            loss_ref[...] = tot * (0.5 / d)

    row = pl.BlockSpec((t, d), lambda i: (i, 0))
    return pl.pallas_call(
        body, name=name, grid=(n,),
        in_specs=[row, row],
        out_specs=[row, pl.BlockSpec((1, 1), lambda i: (0, 0))],
        out_shape=[jax.ShapeDtypeStruct((s, d), F32), jax.ShapeDtypeStruct((1, 1), F32)],
        scratch_shapes=[pltpu.VMEM((8, d), F32)],
        compiler_params=_params(("arbitrary",)),
    )(y, target)


def _split3(x):
    hi = x.astype(BF16)
    r1 = x - hi.astype(F32)
    mid = r1.astype(BF16)
    lo = (r1 - mid.astype(F32)).astype(BF16)
    return hi, mid, lo


def _tri_dot(x, tri):
    hi, mid, lo = _split3(x)
    dn = _DIMS["nn"]
    out = lax.dot_general(hi, tri, dn, preferred_element_type=F32)
    out = out + lax.dot_general(mid, tri, dn, preferred_element_type=F32)
    return out + lax.dot_general(lo, tri, dn, preferred_element_type=F32)


def _log_sigmoid(z):
    return jnp.minimum(z, 0.0) - jnp.log(1.0 + jnp.exp(-jnp.abs(z)))


def _gate_fwd(f_row, b_col, name):
    rows, s = f_row.shape
    t = _tile(s, 512)
    n = s // t

    def body(f_ref, b_ref, c_ref, carry_ref):
        i = pl.program_id(0)

        @pl.when(i == 0)
        def _():
            carry_ref[...] = jnp.zeros_like(carry_ref)

        logf = _log_sigmoid(f_ref[...] + b_ref[...])
        r = lax.broadcasted_iota(jnp.int32, (t, t), 0)
        c = lax.broadcasted_iota(jnp.int32, (t, t), 1)
        tri = jnp.where(r <= c, 1.0, 0.0).astype(BF16)
        cs = _tri_dot(logf, tri) + carry_ref[...]
        carry_ref[...] = cs[:, t - 1:t]
        for h in range(N_HEADS):
            c_ref[h] = jnp.broadcast_to(cs[h:h + 1, :], (8, t))

    return pl.pallas_call(
        body, name=name, grid=(n,),
        in_specs=[pl.BlockSpec((rows, t), lambda i: (0, i)), pl.BlockSpec((rows, 1), lambda i: (0, 0))],
        out_specs=pl.BlockSpec((N_HEADS, 8, t), lambda i: (0, 0, i)),
        out_shape=jax.ShapeDtypeStruct((N_HEADS, 8, s), F32),
        scratch_shapes=[pltpu.VMEM((rows, 1), F32)],
        compiler_params=_params(("arbitrary",)),
    )(f_row, b_col)


def _gate_bwd(f_row, b_col, dc_even, dc_odd, name):
    rows, s = f_row.shape
    t = _tile(s, 512)
    n = s // t

    def body(f_ref, b_ref, dce_ref, dco_ref, df_ref, db_ref, carry_ref, acc_ref):
        i = pl.program_id(0)

        @pl.when(i == 0)
        def _():
            carry_ref[...] = jnp.zeros_like(carry_ref)
            acc_ref[...] = jnp.zeros_like(acc_ref)

        head = lax.broadcasted_iota(jnp.int32, (rows, t), 0)
        dcv = jnp.zeros((rows, t), F32)
        for h in range(N_HEADS):
            src = dce_ref if h % 2 == 0 else dco_ref
            dcv = jnp.where(head == h, jnp.broadcast_to(src[h // 2, 0:1, :], (rows, t)), dcv)
        r = lax.broadcasted_iota(jnp.int32, (t, t), 0)
        c = lax.broadcasted_iota(jnp.int32, (t, t), 1)
        tri = jnp.where(r >= c, 1.0, 0.0).astype(BF16)
        dlogf = _tri_dot(dcv, tri) + carry_ref[...]
        carry_ref[...] = dlogf[:, 0:1]
        z = f_ref[...] + b_ref[...]
        df = dlogf * _sigmoid(-z)
        df_ref[...] = df.astype(df_ref.dtype)
        acc_ref[...] += jnp.sum(df, axis=1, keepdims=True)

        @pl.when(i == n - 1)
        def _():
            db_ref[...] = acc_ref[...]

    rev = lambda i: (0, n - 1 - i)
    dc_spec = pl.BlockSpec((N_HEADS // 2, 8, t), lambda i: (0, 0, n - 1 - i))
    return pl.pallas_call(
        body, name=name, grid=(n,),
        in_specs=[pl.BlockSpec((rows, t), rev), pl.BlockSpec((rows, 1), lambda i: (0, 0)), dc_spec, dc_spec],
        out_specs=[pl.BlockSpec((rows, t), rev), pl.BlockSpec((rows, 1), lambda i: (0, 0))],
        out_shape=[jax.ShapeDtypeStruct((rows, s), BF16), jax.ShapeDtypeStruct((rows, 1), F32)],
        scratch_shapes=[pltpu.VMEM((rows, 1), F32), pltpu.VMEM((rows, 1), F32)],
        compiler_params=_params(("arbitrary",)),
    )(f_row, b_col, dc_even, dc_odd)


_NEG = -1e30
_SCALE = HEAD_DIM ** -0.5


def _head_masks():
    lane = lax.broadcasted_iota(jnp.int32, (1, LANES), 1)
    return [lane < HEAD_DIM, lane >= HEAD_DIM]


def _attn_fwd(h, c, name):
    s = h.shape[0]
    t = _tile(s, 512)
    n = s // t
    qb, kb, vb = OFF_Q // LANES, OFF_K // LANES, OFF_V // LANES

    def body(q_ref, k_ref, v_ref, c0_ref, c1_ref, o_ref, of_ref, lse_ref, m_ref, l_ref, acc_ref):
        qi, ki = pl.program_id(1), pl.program_id(2)
        masks = _head_masks()

        @pl.when(ki == 0)
        def _():
            m_ref[...] = jnp.full_like(m_ref, _NEG)
            l_ref[...] = jnp.zeros_like(l_ref)
            acc_ref[...] = jnp.zeros_like(acc_ref)

        def step(diag):
            q = q_ref[...] * _SCALE
            k = k_ref[...]
            v = v_ref[...]
            acc = acc_ref[...]
            for hh, c_ref in enumerate((c0_ref, c1_ref)):
                qh = jnp.where(masks[hh], q, jnp.zeros_like(q))
                sc = lax.dot_general(qh, k, _DIMS["nt"], preferred_element_type=F32) - c_ref[0, 0:1, :]
                if diag:
                    r = lax.broadcasted_iota(jnp.int32, (t, t), 0)
                    cc = lax.broadcasted_iota(jnp.int32, (t, t), 1)
                    sc = jnp.where(r >= cc, sc, _NEG)
                m_prev = m_ref[hh]
                m_new = jnp.maximum(m_prev, jnp.max(sc, axis=1, keepdims=True))
                alpha = jnp.exp(m_prev - m_new)
                p = jnp.exp(sc - m_new)
                l_ref[hh] = alpha * l_ref[hh] + jnp.sum(p, axis=1, keepdims=True)
                m_ref[hh] = m_new
                p_hi = p.astype(MXU_DTYPE)
                p_lo = (p - p_hi.astype(F32)).astype(MXU_DTYPE)
                pv = (lax.dot_general(p_hi, v, _DIMS["nn"], preferred_element_type=F32)
                      + lax.dot_general(p_lo, v, _DIMS["nn"], preferred_element_type=F32))
                acc = jnp.where(masks[hh], alpha * acc + pv, acc)
            acc_ref[...] = acc

        @pl.when(ki < qi)
        def _():
            step(False)

        @pl.when(ki == qi)
        def _():
            step(True)
            inv = jnp.where(masks[0], 1.0 / l_ref[0], 1.0 / l_ref[1])
            out = acc_ref[...] * inv
            o_ref[...] = out.astype(o_ref.dtype)
            of_ref[...] = out
            lse_ref[...] = jnp.where(masks[0], m_ref[0] + jnp.log(l_ref[0]), m_ref[1] + jnp.log(l_ref[1]))

    kv_row = lambda p, qi, ki: jnp.minimum(ki, qi)
    return pl.pallas_call(
        body, name=name, grid=(N_HEADS // 2, n, n),
        in_specs=[
            pl.BlockSpec((t, LANES), lambda p, qi, ki: (qi, qb + p)),
            pl.BlockSpec((t, LANES), lambda p, qi, ki: (kv_row(p, qi, ki), kb + p)),
            pl.BlockSpec((t, LANES), lambda p, qi, ki: (kv_row(p, qi, ki), vb + p)),
            pl.BlockSpec((1, 8, t), lambda p, qi, ki: (2 * p, 0, kv_row(p, qi, ki))),
            pl.BlockSpec((1, 8, t), lambda p, qi, ki: (2 * p + 1, 0, kv_row(p, qi, ki))),
        ],
        out_specs=[pl.BlockSpec((t, LANES), lambda p, qi, ki: (qi, p))] * 3,
        out_shape=[jax.ShapeDtypeStruct((s, D_ATT), BF16), jax.ShapeDtypeStruct((s, D_ATT), F32),
                   jax.ShapeDtypeStruct((s, D_ATT), F32)],
        scratch_shapes=[pltpu.VMEM((2, t, 1), F32), pltpu.VMEM((2, t, 1), F32), pltpu.VMEM((t, LANES), F32)],
        compiler_params=_params(("parallel", "parallel", "arbitrary")),
    )(h, h, h, c, c)


def _attn_bwd(h, c, o, lse, do, name):
    s = h.shape[0]
    t = _tile(s, 512)
    n = s // t
    qb, kb, vb = OFF_Q // LANES, OFF_K // LANES, OFF_V // LANES

    def body(q_ref, k_ref, v_ref, c0_ref, c1_ref, o_ref, lse_ref, do_ref,
             dq_ref, dk_ref, dv_ref, dc0_ref, dc1_ref, dk_acc, dv_acc, dc_acc):
        ki, qi = pl.program_id(1), pl.program_id(2)
        masks = _head_masks()

        @pl.when((ki == 0) & (qi == 0))
        def _():
            dq_ref[...] = jnp.zeros_like(dq_ref)

        @pl.when(qi == ki)
        def _():
            dk_acc[...] = jnp.zeros_like(dk_acc)
            dv_acc[...] = jnp.zeros_like(dv_acc)
            dc_acc[...] = jnp.zeros_like(dc_acc)

        def step(diag):
            q = q_ref[...] * _SCALE
            k = k_ref[...]
            v = v_ref[...]
            dov = do_ref[...]
            lsev = lse_ref[...]
            prod = dov.astype(F32) * o_ref[...]
            dq_blk = jnp.zeros((t, LANES), F32)
            dk_blk = dk_acc[...]
            dv_blk = dv_acc[...]
            for hh, c_ref in enumerate((c0_ref, c1_ref)):
                mk = masks[hh]
                delta = jnp.sum(jnp.where(mk, prod, 0.0), axis=1, keepdims=True)
                lse_h = lsev[:, hh * HEAD_DIM:hh * HEAD_DIM + 1]
                qh = jnp.where(mk, q, jnp.zeros_like(q))
                doh = jnp.where(mk, dov, jnp.zeros_like(dov))
                sc = lax.dot_general(qh, k, _DIMS["nt"], preferred_element_type=F32) - c_ref[0, 0:1, :]
                p = jnp.exp(sc - lse_h)
                if diag:
                    r = lax.broadcasted_iota(jnp.int32, (t, t), 0)
                    cc = lax.broadcasted_iota(jnp.int32, (t, t), 1)
                    p = jnp.where(r >= cc, p, 0.0)
                dp = lax.dot_general(doh, v, _DIMS["nt"], preferred_element_type=F32)
                ds = p * (dp - delta)
                dsb = ds.astype(MXU_DTYPE)
                pb = p.astype(MXU_DTYPE)
                dv_blk = jnp.where(mk, dv_blk + lax.dot_general(pb, dov, _DIMS["tn"], preferred_element_type=F32),
                                   dv_blk)
                dk_blk = jnp.where(mk, dk_blk + lax.dot_general(dsb, q, _DIMS["tn"], preferred_element_type=F32),
                                   dk_blk)
                dq_blk = jnp.where(mk, lax.dot_general(dsb, k, _DIMS["nn"], preferred_element_type=F32), dq_blk)
                dc_acc[hh] = dc_acc[hh] - jnp.sum(ds, axis=0, keepdims=True)
            dk_acc[...] = dk_blk
            dv_acc[...] = dv_blk
            rows = pl.ds(pl.multiple_of(qi * t, t), t)
            dq_ref[rows, :] = dq_ref[rows, :] + dq_blk * _SCALE

        @pl.when(qi > ki)
        def _():
            step(False)

        @pl.when(qi == ki)
        def _():
            step(True)

        @pl.when(qi == n - 1)
        def _():
            dk_ref[...] = dk_acc[...].astype(dk_ref.dtype)
            dv_ref[...] = dv_acc[...].astype(dv_ref.dtype)
            dc0_ref[0] = jnp.broadcast_to(dc_acc[0], (8, t))
            dc1_ref[0] = jnp.broadcast_to(dc_acc[1], (8, t))

    q_row = lambda p, ki, qi: jnp.maximum(qi, ki)
    return pl.pallas_call(
        body, name=name, grid=(N_HEADS // 2, n, n),
        in_specs=[
            pl.BlockSpec((t, LANES), lambda p, ki, qi: (q_row(p, ki, qi), qb + p)),
            pl.BlockSpec((t, LANES), lambda p, ki, qi: (ki, kb + p)),
            pl.BlockSpec((t, LANES), lambda p, ki, qi: (ki, vb + p)),
            pl.BlockSpec((1, 8, t), lambda p, ki, qi: (2 * p, 0, ki)),
            pl.BlockSpec((1, 8, t), lambda p, ki, qi: (2 * p + 1, 0, ki)),
            pl.BlockSpec((t, LANES), lambda p, ki, qi: (q_row(p, ki, qi), p)),
            pl.BlockSpec((t, LANES), lambda p, ki, qi: (q_row(p, ki, qi), p)),
            pl.BlockSpec((t, LANES), lambda p, ki, qi: (q_row(p, ki, qi), p)),
        ],
        out_specs=[
            pl.BlockSpec((s, LANES), lambda p, ki, qi: (0, p)),
            pl.BlockSpec((t, LANES), lambda p, ki, qi: (ki, p)),
            pl.BlockSpec((t, LANES), lambda p, ki, qi: (ki, p)),
            pl.BlockSpec((1, 8, t), lambda p, ki, qi: (p, 0, ki)),
            pl.BlockSpec((1, 8, t), lambda p, ki, qi: (p, 0, ki)),
        ],
        out_shape=[jax.ShapeDtypeStruct((s, D_ATT), F32), jax.ShapeDtypeStruct((s, D_ATT), BF16),
                   jax.ShapeDtypeStruct((s, D_ATT), BF16), jax.ShapeDtypeStruct((N_HEADS // 2, 8, s), F32),
                   jax.ShapeDtypeStruct((N_HEADS // 2, 8, s), F32)],
        scratch_shapes=[pltpu.VMEM((t, LANES), F32), pltpu.VMEM((t, LANES), F32), pltpu.VMEM((2, 1, t), F32)],
        compiler_params=_params(("parallel", "arbitrary", "arbitrary")),
    )(h, h, h, c, c, o, lse, do)


def _conv3(z, z_prev, w_ref):
    return (w_ref[2:3, :] * z + w_ref[1:2, :] * _shift_down(z, z_prev, 1)
            + w_ref[0:1, :] * _shift_down(z, z_prev, 2))


def _sconv_fwd(h, w, name):
    s = h.shape[0]
    t = _tile(s, 512)
    r = t // HALO
    c = D_CONV
    b_bg, b_cg, b_hc = OFF_BG // c, OFF_CG // c, OFF_HC // c

    def body(bg_ref, cg_ref, hc_ref, cgp_ref, hcp_ref, w_ref, y_ref):
        i = pl.program_id(0)
        live = (i > 0).astype(F32)
        z = cg_ref[...].astype(F32) * hc_ref[...].astype(F32)
        zp = cgp_ref[...].astype(F32) * hcp_ref[...].astype(F32) * live
        y_ref[...] = (bg_ref[...].astype(F32) * _conv3(z, zp, w_ref)).astype(y_ref.dtype)

    cur = lambda b: pl.BlockSpec((t, c), lambda i: (i, b))
    prev = lambda b: pl.BlockSpec((HALO, c), lambda i: (jnp.maximum(i * r - 1, 0), b))
    return pl.pallas_call(
        body, name=name, grid=(s // t,),
        in_specs=[cur(b_bg), cur(b_cg), cur(b_hc), prev(b_cg), prev(b_hc), pl.BlockSpec((8, c), lambda i: (0, 0))],
        out_specs=pl.BlockSpec((t, c), lambda i: (i, 0)),
        out_shape=jax.ShapeDtypeStruct((s, c), BF16),
        compiler_params=_params(("parallel",)),
    )(h, h, h, h, h, w)


def _sconv_bwd(h, w, dy, name):
    s = h.shape[0]
    t = _tile(s, 512)
    n = s // t
    r = t // HALO
    nh = s // HALO
    c = D_CONV
    b_bg, b_cg, b_hc = OFF_BG // c, OFF_CG // c, OFF_HC // c

    def body(bg_ref, cg_ref, hc_ref, cgp_ref, hcp_ref, bgn_ref, dy_ref, dyn_ref, w_ref, d_ref, dw_ref, acc_ref):
        i = pl.program_id(0)
        has_prev = (i > 0).astype(F32)
        has_next = (i < n - 1).astype(F32)
        bg = bg_ref[...].astype(F32)
        cg = cg_ref[...].astype(F32)
        hc = hc_ref[...].astype(F32)
        dyv = dy_ref[...].astype(F32)
        z = cg * hc
        zp = cgp_ref[...].astype(F32) * hcp_ref[...].astype(F32) * has_prev
        z1 = _shift_down(z, zp, 1)
        z2 = _shift_down(z, zp, 2)
        cz = w_ref[2:3, :] * z + w_ref[1:2, :] * z1 + w_ref[0:1, :] * z2
        dcz = dyv * bg
        dczn = dyn_ref[...].astype(F32) * bgn_ref[...].astype(F32) * has_next
        dz = (w_ref[2:3, :] * dcz + w_ref[1:2, :] * _shift_up(dcz, dczn, 1)
              + w_ref[0:1, :] * _shift_up(dcz, dczn, 2))
        d_ref[:, 0:c] = (dyv * cz).astype(d_ref.dtype)
        d_ref[:, c:2 * c] = (dz * hc).astype(d_ref.dtype)
        d_ref[:, 2 * c:3 * c] = (dz * cg).astype(d_ref.dtype)

        @pl.when(i == 0)
        def _():
            acc_ref[...] = jnp.zeros_like(acc_ref)

        acc_ref[0] += _row_sum8(dcz * z2)
        acc_ref[1] += _row_sum8(dcz * z1)
        acc_ref[2] += _row_sum8(dcz * z)

        @pl.when(i == n - 1)
        def _():
            rows = [jnp.sum(acc_ref[k], axis=0, keepdims=True) for k in range(3)]
            dw_ref[...] = jnp.concatenate(rows + [jnp.zeros((5, c), F32)], axis=0)

    cur = lambda b: pl.BlockSpec((t, c), lambda i: (i, b))
    prev = lambda b: pl.BlockSpec((HALO, c), lambda i: (jnp.maximum(i * r - 1, 0), b))
    nxt = lambda b: pl.BlockSpec((HALO, c), lambda i: (jnp.minimum((i + 1) * r, nh - 1), b))
    return pl.pallas_call(
        body, name=name, grid=(n,),
        in_specs=[cur(b_bg), cur(b_cg), cur(b_hc), prev(b_cg), prev(b_hc), nxt(b_bg),
                  cur(0), nxt(0), pl.BlockSpec((8, c), lambda i: (0, 0))],
        out_specs=[pl.BlockSpec((t, 3 * c), lambda i: (i, 0)), pl.BlockSpec((8, c), lambda i: (0, 0))],
        out_shape=[jax.ShapeDtypeStruct((s, 3 * c), BF16), jax.ShapeDtypeStruct((8, c), F32)],
        scratch_shapes=[pltpu.VMEM((3, 8, c), F32)],
        compiler_params=_params(("arbitrary",)),
    )(h, h, h, h, h, h, dy, dy, w)


def _group_masks():
    lane = lax.broadcasted_iota(jnp.int32, (1, D_SGU), 1)
    return [(lane >= g * HEAD_DIM) & (lane < (g + 1) * HEAD_DIM) for g in range(N_GROUPS)]


def _tril_weights(w_ref):
    r = lax.broadcasted_iota(jnp.int32, (CHUNK, CHUNK), 0)
    c = lax.broadcasted_iota(jnp.int32, (CHUNK, CHUNK), 1)
    return [jnp.where(r >= c, w_ref[g], 0.0).astype(MXU_DTYPE) for g in range(N_GROUPS)]


def _sgu_ln(vs, g_ref, b_ref):
    vg, dvg = _gelu_and_grad(vs)
    mu = jnp.mean(vg, axis=-1, keepdims=True)
    xc = vg - mu
    rstd = lax.rsqrt(jnp.mean(xc * xc, axis=-1, keepdims=True) + LN_EPS)
    xhat = xc * rstd
    return xhat * g_ref[...] + b_ref[...], xhat, rstd, dvg


def _sgu_fwd(h, ln_g, ln_b, w_s, bias, name):
    s = h.shape[0]
    t = _tile(s, 512)
    c = D_SGU
    b_u, b_v = OFF_U // c, OFF_VS // c

    def body(u_ref, v_ref, g_ref, b_ref, w_ref, bias_ref, y_ref):
        gm = _group_masks()
        wm = _tril_weights(w_ref)
        ug = _gelu(u_ref[...].astype(F32))
        vn, _, _, _ = _sgu_ln(v_ref[...].astype(F32), g_ref, b_ref)
        vnb = vn.astype(MXU_DTYPE)
        for ch in range(t // CHUNK):
            rows = slice(ch * CHUNK, (ch + 1) * CHUNK)
            mixed = bias_ref[...]
            for g in range(N_GROUPS):
                mg = lax.dot_general(wm[g], vnb[rows], _DIMS["nn"], preferred_element_type=F32)
                mixed = jnp.where(gm[g], mixed + mg, mixed)
            y_ref[rows, :] = (ug[rows] * mixed).astype(y_ref.dtype)

    full = lambda shp: pl.BlockSpec(shp, lambda i: (0,) * len(shp))
    return pl.pallas_call(
        body, name=name, grid=(s // t,),
        in_specs=[pl.BlockSpec((t, c), lambda i: (i, b_u)), pl.BlockSpec((t, c), lambda i: (i, b_v)),
                  full((1, c)), full((1, c)), full((N_GROUPS, CHUNK, CHUNK)), full((CHUNK, c))],
        out_specs=pl.BlockSpec((t, c), lambda i: (i, 0)),
        out_shape=jax.ShapeDtypeStruct((s, c), BF16),
        compiler_params=_params(("parallel",)),
    )(h, h, ln_g, ln_b, w_s, bias)


def _sgu_bwd(h, ln_g, ln_b, w_s, bias, dy, name):
    s = h.shape[0]
    t = _tile(s, 512)
    n = s // t
    c = D_SGU
    b_u, b_v = OFF_U // c, OFF_VS // c

    def body(u_ref, v_ref, g_ref, b_ref, w_ref, bias_ref, dy_ref,
             d_ref, dg_ref, db_ref, dw_ref, dbias_ref, dg_acc, db_acc):
        i = pl.program_id(0)
        gm = _group_masks()
        wm = _tril_weights(w_ref)

        @pl.when(i == 0)
        def _():
            dg_acc[...] = jnp.zeros_like(dg_acc)
            db_acc[...] = jnp.zeros_like(db_acc)
            dw_ref[...] = jnp.zeros_like(dw_ref)
            dbias_ref[...] = jnp.zeros_like(dbias_ref)

        ug, dug = _gelu_and_grad(u_ref[...].astype(F32))
        vn, xhat, rstd, dvg = _sgu_ln(v_ref[...].astype(F32), g_ref, b_ref)
        vnb = vn.astype(MXU_DTYPE)
        dyv = dy_ref[...].astype(F32)
        dmixed = dyv * ug
        dmb = dmixed.astype(MXU_DTYPE)
        dvn_parts = []
        for ch in range(t // CHUNK):
            rows = slice(ch * CHUNK, (ch + 1) * CHUNK)
            mixed = bias_ref[...]
            dvn = jnp.zeros((CHUNK, c), F32)
            for g in range(N_GROUPS):
                mg = lax.dot_general(wm[g], vnb[rows], _DIMS["nn"], preferred_element_type=F32)
                mixed = jnp.where(gm[g], mixed + mg, mixed)
                dvn = jnp.where(gm[g], lax.dot_general(wm[g], dmb[rows], _DIMS["tn"], preferred_element_type=F32),
                                dvn)
                dmg = jnp.where(gm[g], dmb[rows], jnp.zeros_like(dmb[rows]))
                dw_ref[g] += lax.dot_general(dmg, vnb[rows], _DIMS["nt"], preferred_element_type=F32)
            d_ref[rows, 0:c] = (dyv[rows] * mixed * dug[rows]).astype(d_ref.dtype)
            dbias_ref[...] += dmixed[rows]
            dvn_parts.append(dvn)
        dvn = jnp.concatenate(dvn_parts, axis=0)
        dg_acc[...] += _row_sum8(dvn * xhat)
        db_acc[...] += _row_sum8(dvn)
        dxh = dvn * g_ref[...]
        dvgl = rstd * (dxh - jnp.mean(dxh, axis=-1, keepdims=True)
                       - xhat * jnp.mean(dxh * xhat, axis=-1, keepdims=True))
        d_ref[:, c:2 * c] = (dvgl * dvg).astype(d_ref.dtype)

        @pl.when(i == n - 1)
        def _():
            dg_ref[...] = jnp.sum(dg_acc[...], axis=0, keepdims=True)
            db_ref[...] = jnp.sum(db_acc[...], axis=0, keepdims=True)
            r = lax.broadcasted_iota(jnp.int32, (CHUNK, CHUNK), 0)
            cc = lax.broadcasted_iota(jnp.int32, (CHUNK, CHUNK), 1)
            for g in range(N_GROUPS):
                dw_ref[g] = jnp.where(r >= cc, dw_ref[g], 0.0)

    full = lambda shp: pl.BlockSpec(shp, lambda i: (0,) * len(shp))
    return pl.pallas_call(
        body, name=name, grid=(n,),
        in_specs=[pl.BlockSpec((t, c), lambda i: (i, b_u)), pl.BlockSpec((t, c), lambda i: (i, b_v)),
                  full((1, c)), full((1, c)), full((N_GROUPS, CHUNK, CHUNK)), full((CHUNK, c)),
                  pl.BlockSpec((t, c), lambda i: (i, 0))],
        out_specs=[pl.BlockSpec((t, 2 * c), lambda i: (i, 0)), full((1, c)), full((1, c)),
                   full((N_GROUPS, CHUNK, CHUNK)), full((CHUNK, c))],
        out_shape=[jax.ShapeDtypeStruct((s, 2 * c), BF16), jax.ShapeDtypeStruct((1, c), F32),
                   jax.ShapeDtypeStruct((1, c), F32), jax.ShapeDtypeStruct((N_GROUPS, CHUNK, CHUNK), F32),
                   jax.ShapeDtypeStruct((CHUNK, c), F32)],
        scratch_shapes=[pltpu.VMEM((8, c), F32), pltpu.VMEM((8, c), F32)],
        compiler_params=_params(("arbitrary",)),
    )(h, h, ln_g, ln_b, w_s, bias, dy)


def _merge_fwd(h, acts, ws, b_gate, name):
    s = h.shape[0]
    d = D_MODEL
    t = _tile(s, 512)

    def body(gl0, gl1, gl2, a0, a1, a2, w0, w1, w2, b_ref, o_ref):
        acc = jnp.zeros((t, d), F32)
        for i, (gl, a, w) in enumerate(((gl0, a0, w0), (gl1, a1, w1), (gl2, a2, w2))):
            y = lax.dot_general(a[...], w[...], _DIMS["nn"], preferred_element_type=F32)
            acc = acc + _sigmoid(gl[...].astype(F32) + b_ref[i:i + 1, :]) * y
        o_ref[...] = acc.astype(o_ref.dtype)

    full = lambda arr: pl.BlockSpec(arr.shape, lambda i: (0, 0))
    return pl.pallas_call(
        body, name=name, grid=(s // t,),
        in_specs=[pl.BlockSpec((t, d), lambda i, b=b: (i, b)) for b in range(3)]
                 + [pl.BlockSpec((t, a.shape[1]), lambda i: (i, 0)) for a in acts]
                 + [full(w) for w in ws] + [full(b_gate)],
        out_specs=pl.BlockSpec((t, d), lambda i: (i, 0)),
        out_shape=jax.ShapeDtypeStruct((s, d), BF16),
        compiler_params=_params(("parallel",)),
    )(h, h, h, *acts, *ws, b_gate)


def _merge_bwd(h, acts, ws, b_gate, dmerged, name):
    s = h.shape[0]
    d = D_MODEL
    t = _tile(s, 512)
    n = s // t

    def body(gl0, gl1, gl2, a0, a1, a2, w0, w1, w2, b_ref, dm_ref, dy0, dy1, dy2, dgl_ref, db_ref, acc_ref):
        step = pl.program_id(0)

        @pl.when(step == 0)
        def _():
            acc_ref[...] = jnp.zeros_like(acc_ref)

        dm = dm_ref[...]
        for i, (gl, a, w, dy) in enumerate(((gl0, a0, w0, dy0), (gl1, a1, w1, dy1), (gl2, a2, w2, dy2))):
            y = lax.dot_general(a[...], w[...], _DIMS["nn"], preferred_element_type=F32)
            gate = _sigmoid(gl[...].astype(F32) + b_ref[i:i + 1, :])
            dy[...] = (dm * gate).astype(dy.dtype)
            dgl = dm * y * (gate * (1.0 - gate))
            dgl_ref[:, i * d:(i + 1) * d] = dgl.astype(dgl_ref.dtype)
            acc_ref[i] += _row_sum8(dgl)

        @pl.when(step == n - 1)
        def _():
            rows = [jnp.sum(acc_ref[k], axis=0, keepdims=True) for k in range(3)]
            db_ref[...] = jnp.concatenate(rows + [jnp.zeros((5, d), F32)], axis=0)

    full = lambda arr: pl.BlockSpec(arr.shape, lambda i: (0, 0))
    row = pl.BlockSpec((t, d), lambda i: (i, 0))
    return pl.pallas_call(
        body, name=name, grid=(n,),
        in_specs=[pl.BlockSpec((t, d), lambda i, b=b: (i, b)) for b in range(3)]
                 + [pl.BlockSpec((t, a.shape[1]), lambda i: (i, 0)) for a in acts]
                 + [full(w) for w in ws] + [full(b_gate), row],
        out_specs=[row, row, row, pl.BlockSpec((t, 3 * d), lambda i: (i, 0)), pl.BlockSpec((8, d), lambda i: (0, 0))],
        out_shape=[jax.ShapeDtypeStruct((s, d), BF16)] * 3
                  + [jax.ShapeDtypeStruct((s, 3 * d), BF16), jax.ShapeDtypeStruct((8, d), F32)],
        scratch_shapes=[pltpu.VMEM((3, 8, d), F32)],
        compiler_params=_params(("arbitrary",)),
    )(h, h, h, *acts, *ws, b_gate, dmerged)


FF_BLK = D_FF // 2


def _ffn_act_fwd(h2, w, name):
    s = h2.shape[0]
    t = _tile(s, 512)
    r = t // HALO
    cw = 2 * FF_BLK

    def body(x_ref, xp_ref, w_ref, p_ref):
        i = pl.program_id(0)
        live = (i > 0).astype(F32)
        hc = _conv3(x_ref[...].astype(F32), xp_ref[...].astype(F32) * live, w_ref)
        p_ref[...] = (_gelu(hc[:, :FF_BLK]) * hc[:, FF_BLK:]).astype(p_ref.dtype)

    return pl.pallas_call(
        body, name=name, grid=(s // t, 2),
        in_specs=[pl.BlockSpec((t, cw), lambda i, j: (i, j)),
                  pl.BlockSpec((HALO, cw), lambda i, j: (jnp.maximum(i * r - 1, 0), j)),
                  pl.BlockSpec((8, cw), lambda i, j: (0, j))],
        out_specs=pl.BlockSpec((t, FF_BLK), lambda i, j: (i, j)),
        out_shape=jax.ShapeDtypeStruct((s, D_FF), BF16),
        compiler_params=_params(("parallel", "parallel")),
    )(h2, h2, w)


def _ffn_act_bwd(h2, w, dp, name):
    s = h2.shape[0]
    t = _tile(s, 512)
    r = t // HALO
    cw = 2 * FF_BLK

    def body(x_ref, xp_ref, w_ref, dp_ref, d_ref):
        i = pl.program_id(0)
        live = (i > 0).astype(F32)
        hc = _conv3(x_ref[...].astype(F32), xp_ref[...].astype(F32) * live, w_ref)
        ga, dga = _gelu_and_grad(hc[:, :FF_BLK])
        dpv = dp_ref[...].astype(F32)
        d_ref[:, :FF_BLK] = (dpv * hc[:, FF_BLK:] * dga).astype(d_ref.dtype)
        d_ref[:, FF_BLK:] = (dpv * ga).astype(d_ref.dtype)

    return pl.pallas_call(
        body, name=name, grid=(s // t, 2),
        in_specs=[pl.BlockSpec((t, cw), lambda i, j: (i, j)),
                  pl.BlockSpec((HALO, cw), lambda i, j: (jnp.maximum(i * r - 1, 0), j)),
                  pl.BlockSpec((8, cw), lambda i, j: (0, j)),
                  pl.BlockSpec((t, FF_BLK), lambda i, j: (i, j))],
        out_specs=pl.BlockSpec((t, cw), lambda i, j: (i, j)),
        out_shape=jax.ShapeDtypeStruct((s, 2 * D_FF), BF16),
        compiler_params=_params(("parallel", "parallel")),
    )(h2, h2, w, dp)


def _dwconv_bwd(x, w, dy, name):
    s, c = x.shape
    t = _tile(s, 512)
    n = s // t
    r = t // HALO
    nh = s // HALO
    cw = FF_BLK
    nc = c // cw

    def body(x_ref, xp_ref, dy_ref, dyn_ref, w_ref, dx_ref, dw_ref, acc_ref):
        i = pl.program_id(1)
        has_prev = (i > 0).astype(F32)
        has_next = (i < n - 1).astype(F32)
        xv = x_ref[...].astype(F32)
        xp = xp_ref[...].astype(F32) * has_prev
        dyv = dy_ref[...].astype(F32)
        dyn = dyn_ref[...].astype(F32) * has_next
        dx = (w_ref[2:3, :] * dyv + w_ref[1:2, :] * _shift_up(dyv, dyn, 1)
              + w_ref[0:1, :] * _shift_up(dyv, dyn, 2))
        dx_ref[...] = dx.astype(dx_ref.dtype)

        @pl.when(i == 0)
        def _():
            acc_ref[...] = jnp.zeros_like(acc_ref)

        acc_ref[0] += _row_sum8(dyv * _shift_down(xv, xp, 2))
        acc_ref[1] += _row_sum8(dyv * _shift_down(xv, xp, 1))
        acc_ref[2] += _row_sum8(dyv * xv)

        @pl.when(i == n - 1)
        def _():
            rows = [jnp.sum(acc_ref[k], axis=0, keepdims=True) for k in range(3)]
            dw_ref[...] = jnp.concatenate(rows + [jnp.zeros((5, cw), F32)], axis=0)

    return pl.pallas_call(
        body, name=name, grid=(nc, n),
        in_specs=[pl.BlockSpec((t, cw), lambda j, i: (i, j)),
                  pl.BlockSpec((HALO, cw), lambda j, i: (jnp.maximum(i * r - 1, 0), j)),
                  pl.BlockSpec((t, cw), lambda j, i: (i, j)),
                  pl.BlockSpec((HALO, cw), lambda j, i: (jnp.minimum((i + 1) * r, nh - 1), j)),
                  pl.BlockSpec((8, cw), lambda j, i: (0, j))],
        out_specs=[pl.BlockSpec((t, cw), lambda j, i: (i, j)), pl.BlockSpec((8, cw), lambda j, i: (0, j))],
        out_shape=[jax.ShapeDtypeStruct((s, c), BF16), jax.ShapeDtypeStruct((8, c), F32)],
        scratch_shapes=[pltpu.VMEM((3, 8, cw), F32)],
        compiler_params=_params(("parallel", "arbitrary")),
    )(x, x, dy, dy, w)


def _adamw(w, g, m, v, name):
    shape = w.shape
    c = shape[-1]
    rows = math.prod(shape[:-1])
    to2d = lambda a: a.reshape(rows, c)
    cap = max(8, (1 << 18) // c)
    tr = rows
    for cand in (2048, 1024, 512, 256, 128, 64, 32, 16, 8):
        if cand <= cap and rows % cand == 0:
            tr = cand
            break

    def body(w_ref, g_ref, m_ref, v_ref, d_ref, nm_ref, nv_ref):
        gv = g_ref[...]
        nm = ADAM_B1 * m_ref[...] + (1.0 - ADAM_B1) * gv
        nv = ADAM_B2 * v_ref[...] + (1.0 - ADAM_B2) * (gv * gv)
        m_hat = nm / (1.0 - ADAM_B1 ** ADAM_STEP)
        v_hat = nv / (1.0 - ADAM_B2 ** ADAM_STEP)
        d_ref[...] = -ADAM_LR * (m_hat / (jnp.sqrt(v_hat) + ADAM_EPS) + ADAM_WD * w_ref[...])
        nm_ref[...] = nm
        nv_ref[...] = nv

    blk = pl.BlockSpec((tr, c), lambda i: (i, 0))
    outs = pl.pallas_call(
        body, name=name, grid=(rows // tr,),
        in_specs=[blk] * 4, out_specs=[blk] * 3,
        out_shape=[jax.ShapeDtypeStruct((rows, c), F32)] * 3,
        compiler_params=_params(("parallel",)),
    )(to2d(w), to2d(g), to2d(m), to2d(v))
    return tuple(o.reshape(shape) for o in outs)


_ANY = pl.BlockSpec(memory_space=pl.ANY)


def _place():
    x, y, c = lax.axis_index("x"), lax.axis_index("y"), lax.axis_index("c")
    others = [(1 - x, y), (x, 1 - y), (1 - x, 1 - y)]
    return x, y, c, others


def _all_gather_chips(shard, name):
    rws, cols = shard.shape
    half = rws // 2

    def body(x_ref, out_ref, send_sems, recv_sems, local_sem):
        x, y, c, others = _place()
        me = 2 * x + y
        sib = (x, y, 1 - c)

        def rows(chip, cc):
            return out_ref.at[chip, pl.ds(pl.multiple_of(cc * half, 16), half), :]

        def copy(k, src, dst, to):
            return pltpu.make_async_remote_copy(src_ref=src, dst_ref=dst, send_sem=send_sems.at[k],
                                                recv_sem=recv_sems.at[k], device_id=to, device_id_type=MESH)

        mine = pltpu.make_async_copy(x_ref, out_ref.at[me], local_sem)
        mine.start()
        my_half = x_ref.at[pl.ds(pl.multiple_of(c * half, 16), half), :]
        first = [copy(j, my_half, rows(me, c), (ox, oy, c)) for j, (ox, oy) in enumerate(others)]
        for cp in first:
            cp.start()
        passed = []
        for j, (ox, oy) in enumerate(others):
            blk = rows(2 * ox + oy, c)
            copy(j, blk, blk, (x, y, c)).wait_recv()
            fwd = copy(3 + j, blk, blk, sib)
            fwd.start()
            passed.append(fwd)
        for j, (ox, oy) in enumerate(others):
            blk = rows(2 * ox + oy, 1 - c)
            copy(3 + j, blk, blk, (x, y, c)).wait_recv()
        for cp in first + passed:
            cp.wait_send()
        mine.wait()

    return pl.pallas_call(
        body, name=name,
        in_specs=[_ANY], out_specs=_ANY,
        out_shape=jax.ShapeDtypeStruct((N_CHIPS, rws, cols), shard.dtype),
        scratch_shapes=[pltpu.SemaphoreType.DMA((6,)), pltpu.SemaphoreType.DMA((6,)), pltpu.SemaphoreType.DMA],
        compiler_params=pltpu.CompilerParams(has_side_effects=True),
    )(shard)


def _swap_halves(buf, name):
    nb, rws, cols = buf.shape
    half = rws // 2

    def body(b_ref, own_ref, sib_ref, send_sem, recv_sem, local_sem):
        x, y, c, _ = _place()
        keep = b_ref.at[:, pl.ds(pl.multiple_of(c * half, 16), half), :]
        give = b_ref.at[:, pl.ds(pl.multiple_of((1 - c) * half, 16), half), :]
        mine = pltpu.make_async_copy(keep, own_ref, local_sem)
        mine.start()
        cp = pltpu.make_async_remote_copy(src_ref=give, dst_ref=sib_ref, send_sem=send_sem, recv_sem=recv_sem,
                                          device_id=(x, y, 1 - c), device_id_type=MESH)
        cp.start()
        cp.wait()
        mine.wait()

    shp = jax.ShapeDtypeStruct((nb, half, cols), buf.dtype)
    return pl.pallas_call(
        body, name=name,
        in_specs=[_ANY], out_specs=[_ANY, _ANY], out_shape=[shp, shp],
        scratch_shapes=[pltpu.SemaphoreType.DMA, pltpu.SemaphoreType.DMA, pltpu.SemaphoreType.DMA],
        compiler_params=pltpu.CompilerParams(has_side_effects=True),
    )(buf)


def _add2(a, b, name):
    nb, rws, cols = a.shape
    t = _tile(rws, 256)
    if rws % t:
        t = rws

    def body(a_ref, b_ref, o_ref):
        o_ref[...] = (a_ref[...].astype(F32) + b_ref[...].astype(F32)).astype(o_ref.dtype)

    blk = pl.BlockSpec((1, t, cols), lambda i, j: (i, j, 0))
    return pl.pallas_call(
        body, name=name, grid=(nb, rws // t), in_specs=[blk, blk], out_specs=blk,
        out_shape=jax.ShapeDtypeStruct(a.shape, a.dtype),
        compiler_params=_params(("parallel", "parallel")),
    )(a, b)


def _exchange_chips(pre, name):
    nb, half, cols = pre.shape

    def body(p_ref, out_ref, send_sems, recv_sems, local_sem):
        x, y, c, others = _place()
        me = 2 * x + y
        mine = pltpu.make_async_copy(p_ref.at[me], out_ref.at[me], local_sem)
        mine.start()
        sends = []
        for j, (ox, oy) in enumerate(others):
            cp = pltpu.make_async_remote_copy(src_ref=p_ref.at[2 * ox + oy], dst_ref=out_ref.at[me],
                                              send_sem=send_sems.at[j], recv_sem=recv_sems.at[j],
                                              device_id=(ox, oy, c), device_id_type=MESH)
            cp.start()
            sends.append(cp)
        for j, (ox, oy) in enumerate(others):
            blk = out_ref.at[2 * ox + oy]
            pltpu.make_async_remote_copy(src_ref=blk, dst_ref=blk, send_sem=send_sems.at[j],
                                         recv_sem=recv_sems.at[j], device_id=(x, y, c),
                                         device_id_type=MESH).wait_recv()
        for cp in sends:
            cp.wait_send()
        mine.wait()

    return pl.pallas_call(
        body, name=name,
        in_specs=[_ANY], out_specs=_ANY, out_shape=jax.ShapeDtypeStruct(pre.shape, pre.dtype),
        scratch_shapes=[pltpu.SemaphoreType.DMA((3,)), pltpu.SemaphoreType.DMA((3,)), pltpu.SemaphoreType.DMA],
        compiler_params=pltpu.CompilerParams(has_side_effects=True),
    )(pre)


def _add4(parts, name):
    nb, half, cols = parts.shape
    t = _tile(half, 256)
    if half % t:
        t = half

    def body(p_ref, o_ref):
        acc = p_ref[0].astype(F32)
        for k in range(1, nb):
            acc = acc + p_ref[k].astype(F32)
        o_ref[...] = acc

    return pl.pallas_call(
        body, name=name, grid=(half // t,),
        in_specs=[pl.BlockSpec((nb, t, cols), lambda i: (0, i, 0))],
        out_specs=pl.BlockSpec((t, cols), lambda i: (i, 0)),
        out_shape=jax.ShapeDtypeStruct((half, cols), F32),
        compiler_params=_params(("parallel",)),
    )(parts)


def _join_halves(mine_half, name):
    half, cols = mine_half.shape

    def body(h_ref, out_ref, send_sem, recv_sem, local_sem):
        x, y, c, _ = _place()
        dst = out_ref.at[pl.ds(pl.multiple_of(c * half, 8), half), :]
        mine = pltpu.make_async_copy(h_ref, dst, local_sem)
        mine.start()
        cp = pltpu.make_async_remote_copy(src_ref=h_ref, dst_ref=dst, send_sem=send_sem, recv_sem=recv_sem,
                                          device_id=(x, y, 1 - c), device_id_type=MESH)
        cp.start()
        cp.wait()
        mine.wait()

    return pl.pallas_call(
        body, name=name,
        in_specs=[_ANY], out_specs=_ANY, out_shape=jax.ShapeDtypeStruct((2 * half, cols), mine_half.dtype),
        scratch_shapes=[pltpu.SemaphoreType.DMA, pltpu.SemaphoreType.DMA, pltpu.SemaphoreType.DMA],
        compiler_params=pltpu.CompilerParams(has_side_effects=True),
    )(mine_half)


def _reduce_scatter_chips(buf, tag):
    own, sib = _swap_halves(buf, "rs_swap_" + tag)
    pre = _add2(own, sib, "rs_add2_" + tag)
    parts = _exchange_chips(pre, "rs_xchg_" + tag)
    red = _add4(parts, "rs_add4_" + tag)
    return _join_halves(red, "rs_join_" + tag)


MAX_DMA_BYTES = 2 * 1024 * 1024
ROW_ALIGN = 16


def _pieces(rows, row_bytes):
    n = max(1, -(-(rows * row_bytes) // MAX_DMA_BYTES))
    step = -(-(-(-rows // n)) // ROW_ALIGN) * ROW_ALIGN
    return [(r, min(step, rows - r)) for r in range(0, rows, step)]


def _half_plan(arrays, row_axis):
    plan = []
    for a, arr in enumerate(arrays):
        row_bytes = math.prod(arr.shape[row_axis + 1:]) * arr.dtype.itemsize * (arr.shape[0] if row_axis else 1)
        plan += [(a, r0, nr) for r0, nr in _pieces(arr.shape[row_axis] // 2, row_bytes)]
    return plan


def _rows(start, size):
    return pl.ds(pl.multiple_of(start, ROW_ALIGN), size)


def _remote(src, dst, send_sems, recv_sems, k, to):
    return pltpu.make_async_remote_copy(src_ref=src, dst_ref=dst, send_sem=send_sems.at[k], recv_sem=recv_sems.at[k],
                                        device_id=to, device_id_type=MESH)


def _comm_call(body, name, ins, out_shapes, n_remote, n_local, aliases=None):
    return pl.pallas_call(
        body, name=name,
        in_specs=[_ANY] * len(ins), out_specs=[_ANY] * len(out_shapes), out_shape=out_shapes,
        scratch_shapes=[pltpu.SemaphoreType.DMA((n_remote,)), pltpu.SemaphoreType.DMA((n_remote,)),
                        pltpu.SemaphoreType.DMA((max(n_local, 1),))],
        input_output_aliases=aliases or {},
        compiler_params=pltpu.CompilerParams(has_side_effects=True),
    )(*ins)


def _cast_shard(w, l, me_idx, name):
    _, k, cols = w.shape
    tr = _tile(k, 256)
    if k % tr:
        tr = k

    def body(me_ref, w_ref, s_ref, land_ref):
        del me_ref
        v = w_ref[...].astype(BF16)
        s_ref[...] = v
        land_ref[...] = v

    grid_spec = pltpu.PrefetchScalarGridSpec(
        num_scalar_prefetch=1, grid=(k // tr,),
        in_specs=[pl.BlockSpec((None, tr, cols), lambda i, me: (l, i, 0))],
        out_specs=[pl.BlockSpec((tr, cols), lambda i, me: (i, 0)),
                   pl.BlockSpec((None, tr, cols), lambda i, me: (me[0], i, 0))])
    return pl.pallas_call(
        body, name=name, grid_spec=grid_spec,
        out_shape=[jax.ShapeDtypeStruct((k, cols), BF16), jax.ShapeDtypeStruct((N_CHIPS, k, cols), BF16)],
        compiler_params=_params(("parallel",)),
    )(me_idx, w)


def _gather_ici(shards, lands, name):
    n = len(shards)
    plan = _half_plan(shards, 0)

    def body(*refs):
        x_refs, out_refs = refs[:n], refs[2 * n:3 * n]
        send_sems, recv_sems, _ = refs[3 * n:]
        x, y, c, others = _place()
        me = 2 * x + y
        sends = []
        for i, (a, r0, nr) in enumerate(plan):
            rows = _rows(c * (shards[a].shape[0] // 2) + r0, nr)
            for j, (ox, oy) in enumerate(others):
                cp = _remote(x_refs[a].at[rows, :], out_refs[a].at[me, rows, :], send_sems, recv_sems, 3 * i + j,
                             (ox, oy, c))
                cp.start()
                sends.append(cp)
        for i, (a, r0, nr) in enumerate(plan):
            rows = _rows(c * (shards[a].shape[0] // 2) + r0, nr)
            for j, (ox, oy) in enumerate(others):
                blk = out_refs[a].at[2 * ox + oy, rows, :]
                _remote(blk, blk, send_sems, recv_sems, 3 * i + j, (x, y, c)).wait_recv()
        for cp in sends:
            cp.wait_send()

    outs = [jax.ShapeDtypeStruct(a.shape, a.dtype) for a in lands]
    return _comm_call(body, name, list(shards) + list(lands), outs, 3 * len(plan), 0,
                      aliases={n + a: a for a in range(n)})


def _gather_d2d(lands, name):
    n = len(lands)
    plan = _half_plan(lands, 1)
    plan = [(a, r0, nr) for a, r0, nr in plan]

    def body(*refs):
        out_refs = refs[n:2 * n]
        send_sems, recv_sems, _ = refs[2 * n:]
        x, y, c, others = _place()
        sends = []
        for i, (a, r0, nr) in enumerate(plan):
            rows = _rows(c * (lands[a].shape[1] // 2) + r0, nr)
            for j, (ox, oy) in enumerate(others):
                blk = out_refs[a].at[2 * ox + oy, rows, :]
                cp = _remote(blk, blk, send_sems, recv_sems, 3 * i + j, (x, y, 1 - c))
                cp.start()
                sends.append(cp)
        for i, (a, r0, nr) in enumerate(plan):
            rows = _rows((1 - c) * (lands[a].shape[1] // 2) + r0, nr)
            for j, (ox, oy) in enumerate(others):
                blk = out_refs[a].at[2 * ox + oy, rows, :]
                _remote(blk, blk, send_sems, recv_sems, 3 * i + j, (x, y, c)).wait_recv()
        for cp in sends:
            cp.wait_send()

    outs = [jax.ShapeDtypeStruct(a.shape, a.dtype) for a in lands]
    return _comm_call(body, name, lands, outs, 3 * len(plan), 0, aliases={a: a for a in range(n)})


def _rs_swap(ts, name):
    n = len(ts)
    plan = _half_plan(ts, 1)

    def body(*refs):
        t_refs, out_refs = refs[:n], refs[n:2 * n]
        send_sems, recv_sems, _ = refs[2 * n:]
        x, y, c, _o = _place()
        sends = []
        for i, (a, r0, nr) in enumerate(plan):
            src = t_refs[a].at[:, _rows((1 - c) * (ts[a].shape[1] // 2) + r0, nr), :]
            cp = _remote(src, out_refs[a].at[:, pl.ds(r0, nr), :], send_sems, recv_sems, i, (x, y, 1 - c))
            cp.start()
            sends.append(cp)
        for i, (a, r0, nr) in enumerate(plan):
            blk = out_refs[a].at[:, pl.ds(r0, nr), :]
            _remote(blk, blk, send_sems, recv_sems, i, (x, y, c)).wait_recv()
        for cp in sends:
            cp.wait_send()

    outs = [jax.ShapeDtypeStruct((t.shape[0], t.shape[1] // 2, t.shape[2]), t.dtype) for t in ts]
    return _comm_call(body, name, ts, outs, len(plan), 0)


def _add_half(t, got, c_idx, me_idx, name):
    nb, k, cols = t.shape
    half = k // 2

    def body(c_ref, me_ref, t_ref, g_ref, o_ref, mine_ref):
        del c_ref
        v = (t_ref[...].astype(F32) + g_ref[...].astype(F32)).astype(o_ref.dtype)
        o_ref[...] = v

        @pl.when(pl.program_id(0) == me_ref[0])
        def _():
            mine_ref[...] = v

    blk = pl.BlockSpec((1, half, cols), lambda i, c, me: (i, 0, 0))
    grid_spec = pltpu.PrefetchScalarGridSpec(
        num_scalar_prefetch=2, grid=(nb,),
        in_specs=[pl.BlockSpec((1, half, cols), lambda i, c, me: (i, c[0], 0)), blk],
        out_specs=[blk, pl.BlockSpec((1, half, cols), lambda i, c, me: (me[0], 0, 0))])
    shp = jax.ShapeDtypeStruct(got.shape, got.dtype)
    return pl.pallas_call(
        body, name=name, grid_spec=grid_spec, out_shape=[shp, shp],
        compiler_params=_params(("arbitrary",)),
    )(c_idx, me_idx, t, got)


def _rs_xchg(pres, parts, name):
    n = len(pres)
    plan = []
    for a, p in enumerate(pres):
        plan += [(a, r0, nr) for r0, nr in _pieces(p.shape[1], p.shape[2] * p.dtype.itemsize)]

    def body(*refs):
        p_refs, out_refs = refs[:n], refs[2 * n:3 * n]
        send_sems, recv_sems, _ = refs[3 * n:]
        x, y, c, others = _place()
        me = 2 * x + y
        sends = []
        for i, (a, r0, nr) in enumerate(plan):
            for j, (ox, oy) in enumerate(others):
                cp = _remote(p_refs[a].at[2 * ox + oy, pl.ds(r0, nr), :], out_refs[a].at[me, pl.ds(r0, nr), :],
                             send_sems, recv_sems, 3 * i + j, (ox, oy, c))
                cp.start()
                sends.append(cp)
        for i, (a, r0, nr) in enumerate(plan):
            for j, (ox, oy) in enumerate(others):
                blk = out_refs[a].at[2 * ox + oy, pl.ds(r0, nr), :]
                _remote(blk, blk, send_sems, recv_sems, 3 * i + j, (x, y, c)).wait_recv()
        for cp in sends:
            cp.wait_send()

    outs = [jax.ShapeDtypeStruct(p.shape, p.dtype) for p in parts]
    return _comm_call(body, name, list(pres) + list(parts), outs, 3 * len(plan), 0,
                      aliases={n + a: a for a in range(n)})


def _add4_half(parts, c_idx, name):
    nb, half, cols = parts.shape
    t = _tile(half, 256)
    if half % t:
        t = half
    steps = half // t

    def body(c_ref, p_ref, o_ref):
        del c_ref
        acc = p_ref[0].astype(F32)
        for k in range(1, nb):
            acc = acc + p_ref[k].astype(F32)
        o_ref[...] = acc

    grid_spec = pltpu.PrefetchScalarGridSpec(
        num_scalar_prefetch=1, grid=(steps,),
        in_specs=[pl.BlockSpec((nb, t, cols), lambda i, c: (0, i, 0))],
        out_specs=pl.BlockSpec((t, cols), lambda i, c: (c[0] * steps + i, 0)))
    return pl.pallas_call(
        body, name=name, grid_spec=grid_spec, out_shape=jax.ShapeDtypeStruct((2 * half, cols), F32),
        compiler_params=_params(("parallel",)),
    )(c_idx, parts)


def _rs_join(fulls, name):
    n = len(fulls)
    plan = _half_plan(fulls, 0)

    def body(*refs):
        out_refs = refs[n:2 * n]
        send_sems, recv_sems, _ = refs[2 * n:]
        x, y, c, _o = _place()
        sends = []
        for i, (a, r0, nr) in enumerate(plan):
            blk = out_refs[a].at[_rows(c * (fulls[a].shape[0] // 2) + r0, nr), :]
            cp = _remote(blk, blk, send_sems, recv_sems, i, (x, y, 1 - c))
            cp.start()
            sends.append(cp)
        for i, (a, r0, nr) in enumerate(plan):
            blk = out_refs[a].at[_rows((1 - c) * (fulls[a].shape[0] // 2) + r0, nr), :]
            _remote(blk, blk, send_sems, recv_sems, i, (x, y, c)).wait_recv()
        for cp in sends:
            cp.wait_send()

    outs = [jax.ShapeDtypeStruct(f.shape, f.dtype) for f in fulls]
    return _comm_call(body, name, fulls, outs, len(plan), 0, aliases={a: a for a in range(n)})


def _reduce_scatter_list(ts, c_idx, me_idx, tag):
    got = _rs_swap(ts, "rs_swap_" + tag)
    pairs = [_add_half(t, g, c_idx, me_idx, f"rs_add2_{tag}_{a}") for a, (t, g) in enumerate(zip(ts, got))]
    parts = _rs_xchg([p for p, _ in pairs], [m for _, m in pairs], "rs_xchg_" + tag)
    fulls = [_add4_half(p, c_idx, f"rs_add4_{tag}_{a}") for a, p in enumerate(parts)]
    return _rs_join(fulls, "rs_join_" + tag)


def _pack_rows(pieces, rows, dtype):
    flat = jnp.concatenate([p.astype(dtype).reshape(-1) for p in pieces])
    return jnp.pad(flat, (0, rows * PACK_COLS - flat.shape[0])).reshape(rows, PACK_COLS)


def _unpack(flat, shapes):
    out, off = [], 0
    for shp in shapes:
        size = math.prod(shp)
        out.append(flat[off:off + size].reshape(shp))
        off += size
    return out


def _rows_for(n_elems, mult):
    rows = -(-n_elems // PACK_COLS)
    return -(-rows // mult) * mult


BIG_SHARDS = [("w_in", (D_MODEL, 1474)), ("w_branch_att", (D_ATT, 256)), ("w_branch_conv", (D_CONV, 256)),
              ("w_branch_sgu", (D_SGU, 256)), ("w_out", (256, D_MODEL)), ("w_ffn_up", (D_MODEL, FF_BLK)),
              ("w_ffn_down", (D_FF // N_CHIPS, D_MODEL))]
SMALL_SHARDS = [("b_gate", (3, 256)), ("conv_mix_w", (3, 64)), ("conv_ffn_w", (3, FF_BLK))]
REPLICATED = [("pre_mix_g", (D_MODEL,)), ("post_mix_g", (D_MODEL,)), ("pre_ffn_g", (D_MODEL,)),
              ("post_ffn_g", (D_MODEL,)), ("b_forget", (N_HEADS,)), ("sgu_ln_g", (D_SGU,)), ("sgu_ln_b", (D_SGU,)),
              ("sgu_w", (N_GROUPS, CHUNK, CHUNK)), ("sgu_b", (N_GROUPS, CHUNK))]
WEIGHT_ORDER = ["pre_mix_g", "post_mix_g", "pre_ffn_g", "post_ffn_g", "w_in", "b_forget", "b_gate", "conv_mix_w",
                "sgu_ln_g", "sgu_ln_b", "sgu_w", "sgu_b", "w_branch_att", "w_branch_conv", "w_branch_sgu", "w_out",
                "w_ffn_up", "conv_ffn_w", "w_ffn_down"]

_SMALL_ELEMS = sum(math.prod(s) for _, s in SMALL_SHARDS)
_REP_ELEMS = sum(math.prod(s) for _, s in REPLICATED)
_REP_QUARTER = -(-(DEPTH * _REP_ELEMS) // N_CHIPS)
SMALL_PARAM_ROWS = _rows_for(DEPTH * _SMALL_ELEMS, 32)
SMALL_ROWS = _rows_for(DEPTH * _SMALL_ELEMS + _REP_QUARTER, 32)
IN_WIDTH = 5896
IN_SHARD = IN_WIDTH // N_CHIPS
IN_SHARD_PAD = 1536
IN_PAD = 6144


def _gather_small(wts):
    shard = _pack_rows([wts[n] for n, _ in SMALL_SHARDS], SMALL_PARAM_ROWS, F32)
    full = _all_gather_chips(shard, "gather_small_params").reshape(N_CHIPS, -1)
    per_chip = [_unpack(full[j], [(DEPTH,) + s for _, s in SMALL_SHARDS]) for j in range(N_CHIPS)]
    return {n: jnp.concatenate([per_chip[j][i] for j in range(N_CHIPS)], axis=-1)
            for i, (n, _) in enumerate(SMALL_SHARDS)}


def _gather_layer(wts, l, me_idx):
    names = [n for n, _ in BIG_SHARDS]
    cast = [_cast_shard(wts[n], l, me_idx, "cast_" + n) for n in names]
    lands = _gather_ici([sh for sh, _ in cast], [ld for _, ld in cast], "gather_ici")
    return dict(zip(names, _gather_d2d(lands, "gather_d2d")))


def _pad_rows(a, rows):
    return jnp.pad(a, ((0, rows - a.shape[0]), (0, 0)))


def _whole_cols(land):
    return land.transpose(1, 0, 2).reshape(land.shape[1], -1)


_O_F = 3 * D_ATT
_O_B = _O_F + N_HEADS
_O_GL = _O_B + 3 * D_CONV + 2 * D_SGU


def _prep_layer(wts, lands, small, l):
    w_in = _whole_cols(lands["w_in"])
    up = lands["w_ffn_up"]
    cf = small["conv_ffn_w"][l]
    blk = lambda a, j: a[:, j * FF_BLK:(j + 1) * FF_BLK]
    return {
        "w_p": jnp.concatenate([w_in[:, _O_GL:], w_in[:, :_O_F], w_in[:, _O_B:_O_GL]], axis=1),
        "w_in_pad": jnp.pad(w_in, ((0, 0), (0, IN_PAD - IN_WIDTH))),
        "wf_t": _pad_rows(w_in[:, _O_F:_O_B].T, F_ROWS),
        "b_forget": _pad_rows(wts["b_forget"][l].reshape(N_HEADS, 1), F_ROWS),
        "b_gate": _pad_rows(small["b_gate"][l], 8),
        "conv_mix_w": _pad_rows(small["conv_mix_w"][l], 8),
        "w_att": _whole_cols(lands["w_branch_att"]), "w_conv": _whole_cols(lands["w_branch_conv"]),
        "w_sgu": _whole_cols(lands["w_branch_sgu"]),
        "w_out": lands["w_out"].reshape(D_MODEL, D_MODEL),
        "w_up": jnp.concatenate([up[0], up[2], up[1], up[3]], axis=1),
        "conv_ffn_w": _pad_rows(jnp.concatenate([blk(cf, 0), blk(cf, 2), blk(cf, 1), blk(cf, 3)], axis=1), 8),
        "w_down": lands["w_ffn_down"].reshape(D_FF, D_MODEL),
        "pre_mix_g": wts["pre_mix_g"][l].reshape(1, -1), "post_mix_g": wts["post_mix_g"][l].reshape(1, -1),
        "pre_ffn_g": wts["pre_ffn_g"][l].reshape(1, -1), "post_ffn_g": wts["post_ffn_g"][l].reshape(1, -1),
        "ln_g": wts["sgu_ln_g"][l].reshape(1, -1), "ln_b": wts["sgu_ln_b"][l].reshape(1, -1),
        "sgu_w": wts["sgu_w"][l],
        "sgu_bias": jnp.repeat(wts["sgu_b"][l].T, HEAD_DIM, axis=1),
    }


def _layer_fwd(x, p):
    s = x.shape[0]
    xn = _rms_fwd(x, p["pre_mix_g"], "rms_pre_mix")
    h = _mm(xn, p["w_p"], "nn", BF16, "mm_in", s, 256, D_MODEL)
    f_row = _mm(p["wf_t"], xn, "nt", F32, "mm_forget", F_ROWS, 2048, D_MODEL)
    c = _gate_fwd(f_row, p["b_forget"], "gate_fwd")
    o, o_f32, lse = _attn_fwd(h, c, "attn_fwd")
    yc = _sconv_fwd(h, p["conv_mix_w"], "sconv_fwd")
    ys = _sgu_fwd(h, p["ln_g"], p["ln_b"], p["sgu_w"], p["sgu_bias"], "sgu_fwd")
    merged = _merge_fwd(h, (o, yc, ys), (p["w_att"], p["w_conv"], p["w_sgu"]), p["b_gate"], "merge_fwd")
    mo = _mm(merged, p["w_out"], "nn", F32, "mm_out", 2048, 512, D_MODEL)
    x1 = _resid_post(x, mo, p["post_mix_g"], "post_mix")
    xn2 = _rms_fwd(x1, p["pre_ffn_g"], "rms_pre_ffn")
    h2 = _mm(xn2, p["w_up"], "nn", BF16, "mm_up", 2048, 512, D_MODEL)
    pact = _ffn_act_fwd(h2, p["conv_ffn_w"], "ffn_act_fwd")
    ff = _mm(pact, p["w_down"], "nn", F32, "mm_down", 2048, 512, FF_BLK)
    x2 = _resid_post(x1, ff, p["post_ffn_g"], "post_ffn")
    saved = dict(x=x, xn=xn, h=h, f_row=f_row, c=c, o=o, o_f32=o_f32, lse=lse, yc=yc, ys=ys, merged=merged, mo=mo, x1=x1,
                 xn2=xn2, h2=h2, pact=pact, ff=ff)
    return x2, saved


def _layer_bwd(dx2, p, sv):
    s = dx2.shape[0]
    g = {}
    same = lambda b: b
    dff, g["post_ffn_g"] = _rms_bwd(sv["ff"], p["post_ffn_g"], [dx2], None, BF16, "post_ffn_bwd")
    dpact = _mm(dff, p["w_down"], "nt", BF16, "mm_down_dx", 1024, FF_BLK, D_MODEL)
    t_down = _mm(sv["pact"], dff, "tn", BF16, "mm_down_dw", 256, D_MODEL, s).reshape(N_CHIPS, -1, D_MODEL)
    dhc2 = _ffn_act_bwd(sv["h2"], p["conv_ffn_w"], dpact, "ffn_act_bwd")
    dh2, dconv_ffn = _dwconv_bwd(sv["h2"], p["conv_ffn_w"], dhc2, "ffn_conv_bwd")
    dxn2 = _mm(dh2, p["w_up"], "nt", F32, "mm_up_dx", 1024, D_MODEL, FF_BLK)
    t_up = _mm(sv["xn2"], dh2, "tn", BF16, "mm_up_dw", 512, FF_BLK, s, chip_of=lambda b: (b % 2) * 2 + b // 2)
    dx1, g["pre_ffn_g"] = _rms_bwd(sv["x1"], p["pre_ffn_g"], [dxn2], dx2, F32, "pre_ffn_bwd")
    dmo, g["post_mix_g"] = _rms_bwd(sv["mo"], p["post_mix_g"], [dx1], None, BF16, "post_mix_bwd")
    dmerged = _mm(dmo, p["w_out"], "nt", F32, "mm_out_dx", 2048, 512, D_MODEL)
    t_out = _mm(sv["merged"], dmo, "tn", BF16, "mm_out_dw", 512, D_MODEL, s).reshape(N_CHIPS, -1, D_MODEL)
    acts = (sv["o"], sv["yc"], sv["ys"])
    ws = (p["w_att"], p["w_conv"], p["w_sgu"])
    dy_a, dy_c, dy_s, dgl, db_gate = _merge_bwd(sv["h"], acts, ws, p["b_gate"], dmerged, "merge_bwd")
    do = _mm(dy_a, p["w_att"], "nt", BF16, "mm_att_dx", 2048, D_ATT, D_MODEL)
    dyc = _mm(dy_c, p["w_conv"], "nt", BF16, "mm_conv_dx", 2048, D_CONV, D_MODEL)
    dys = _mm(dy_s, p["w_sgu"], "nt", BF16, "mm_sgu_dx", 2048, D_SGU, D_MODEL)
    t_att = _mm(sv["o"], dy_a, "tn", BF16, "mm_att_dw", D_ATT, 256, s, chip_of=same)
    t_conv = _mm(sv["yc"], dy_c, "tn", BF16, "mm_conv_dw", D_CONV, 256, s, chip_of=same)
    t_sgu = _mm(sv["ys"], dy_s, "tn", BF16, "mm_sgu_dw", D_SGU, 256, s, chip_of=same)
    d_conv, dconv_mix = _sconv_bwd(sv["h"], p["conv_mix_w"], dyc, "sconv_bwd")
    d_sgu, g["sgu_ln_g"], g["sgu_ln_b"], g["sgu_w"], dbias = _sgu_bwd(
        sv["h"], p["ln_g"], p["ln_b"], p["sgu_w"], p["sgu_bias"], dys, "sgu_bwd")
    dq, dk, dv, dc_even, dc_odd = _attn_bwd(sv["h"], sv["c"], sv["o_f32"], sv["lse"], do, "attn_bwd")
    df, db_forget = _gate_bwd(sv["f_row"], p["b_forget"], dc_even, dc_odd, "gate_bwd")
    dh = jnp.concatenate([dq.astype(BF16), dk, dv, df[:N_HEADS].T, d_conv, d_sgu, dgl,
                          jnp.zeros((s, IN_PAD - IN_WIDTH), BF16)], axis=1)
    dxn = _mm(dh, p["w_in_pad"], "nt", F32, "mm_in_dx", 512, D_MODEL, 2048)
    dw_in = _mm(sv["xn"], dh, "tn", F32, "mm_in_dw", D_MODEL, 512, s)
    t_in = jnp.stack([dw_in[:, j * IN_SHARD:j * IN_SHARD + IN_SHARD_PAD] for j in range(N_CHIPS)]).astype(BF16)
    dx, g["pre_mix_g"] = _rms_bwd(sv["x"], p["pre_mix_g"], [dxn], dx1, F32, "pre_mix_bwd")
    blk = lambda a, j: a[:, j * FF_BLK:(j + 1) * FF_BLK]
    g["conv_ffn_w"] = jnp.concatenate([blk(dconv_ffn, 0), blk(dconv_ffn, 2), blk(dconv_ffn, 1),
                                       blk(dconv_ffn, 3)], axis=1)[:3]
    g["conv_mix_w"] = dconv_mix[:3]
    g["b_gate"] = db_gate[:3]
    g["b_forget"] = db_forget[:N_HEADS, 0]
    g["sgu_b"] = jnp.sum(dbias.reshape(CHUNK, N_GROUPS, HEAD_DIM), axis=-1).T
    for n in ("pre_mix_g", "post_mix_g", "pre_ffn_g", "post_ffn_g", "sgu_ln_g", "sgu_ln_b"):
        g[n] = g[n].reshape(-1)
    return dx, [t_in, t_att, t_conv, t_sgu, t_out, t_up, t_down], g


def _shard_cols(a, j):
    w = a.shape[-1] // N_CHIPS
    return a[..., j * w:(j + 1) * w]


def kernel(x, pre_mix_g, post_mix_g, pre_ffn_g, post_ffn_g, w_in, b_forget, b_gate, conv_mix_w, sgu_ln_g, sgu_ln_b, sgu_w, sgu_b, w_branch_att, w_branch_conv, w_branch_sgu, w_out, w_ffn_up, conv_ffn_w, w_ffn_down, loss_target, m_pre_mix_g, m_post_mix_g, m_pre_ffn_g, m_post_ffn_g, m_w_in, m_b_forget, m_b_gate, m_conv_mix_w, m_sgu_ln_g, m_sgu_ln_b, m_sgu_w, m_sgu_b, m_w_branch_att, m_w_branch_conv, m_w_branch_sgu, m_w_out, m_w_ffn_up, m_conv_ffn_w, m_w_ffn_down, v_pre_mix_g, v_post_mix_g, v_pre_ffn_g, v_post_ffn_g, v_w_in, v_b_forget, v_b_gate, v_conv_mix_w, v_sgu_ln_g, v_sgu_ln_b, v_sgu_w, v_sgu_b, v_w_branch_att, v_w_branch_conv, v_w_branch_sgu, v_w_out, v_w_ffn_up, v_conv_ffn_w, v_w_ffn_down):
    wts = dict(pre_mix_g=pre_mix_g, post_mix_g=post_mix_g, pre_ffn_g=pre_ffn_g, post_ffn_g=post_ffn_g, w_in=w_in,
               b_forget=b_forget, b_gate=b_gate, conv_mix_w=conv_mix_w, sgu_ln_g=sgu_ln_g, sgu_ln_b=sgu_ln_b,
               sgu_w=sgu_w, sgu_b=sgu_b, w_branch_att=w_branch_att, w_branch_conv=w_branch_conv,
               w_branch_sgu=w_branch_sgu, w_out=w_out, w_ffn_up=w_ffn_up, conv_ffn_w=conv_ffn_w,
               w_ffn_down=w_ffn_down)
    moms = dict(pre_mix_g=m_pre_mix_g, post_mix_g=m_post_mix_g, pre_ffn_g=m_pre_ffn_g, post_ffn_g=m_post_ffn_g,
                w_in=m_w_in, b_forget=m_b_forget, b_gate=m_b_gate, conv_mix_w=m_conv_mix_w, sgu_ln_g=m_sgu_ln_g,
                sgu_ln_b=m_sgu_ln_b, sgu_w=m_sgu_w, sgu_b=m_sgu_b, w_branch_att=m_w_branch_att,
                w_branch_conv=m_w_branch_conv, w_branch_sgu=m_w_branch_sgu, w_out=m_w_out, w_ffn_up=m_w_ffn_up,
                conv_ffn_w=m_conv_ffn_w, w_ffn_down=m_w_ffn_down)
    vels = dict(pre_mix_g=v_pre_mix_g, post_mix_g=v_post_mix_g, pre_ffn_g=v_pre_ffn_g, post_ffn_g=v_post_ffn_g,
                w_in=v_w_in, b_forget=v_b_forget, b_gate=v_b_gate, conv_mix_w=v_conv_mix_w, sgu_ln_g=v_sgu_ln_g,
                sgu_ln_b=v_sgu_ln_b, sgu_w=v_sgu_w, sgu_b=v_sgu_b, w_branch_att=v_w_branch_att,
                w_branch_conv=v_w_branch_conv, w_branch_sgu=v_w_branch_sgu, w_out=v_w_out, w_ffn_up=v_w_ffn_up,
                conv_ffn_w=v_conv_ffn_w, w_ffn_down=v_w_ffn_down)

    c_idx = lax.axis_index("c").astype(jnp.int32).reshape(1)
    me_idx = (2 * lax.axis_index("x") + lax.axis_index("y")).astype(jnp.int32).reshape(1)
    small = _gather_small(wts)

    xs = x[0]
    layers, saved = [], []
    for l in range(DEPTH):
        p = _prep_layer(wts, _gather_layer(wts, l, me_idx), small, l)
        xs, sv = _layer_fwd(xs, p)
        layers.append(p)
        saved.append(sv)
    dy, loss_part = _loss_head(xs, loss_target[0], "loss_head")
    loss = lax.psum(loss_part[0, 0], ("x", "y", "c"))

    big_red = [None] * DEPTH
    small_grads = [None] * DEPTH
    for l in reversed(range(DEPTH)):
        dy, ts, small_grads[l] = _layer_bwd(dy, layers[l], saved[l])
        big_red[l] = _reduce_scatter_list(ts, c_idx, me_idx, "big")
    grad_x = dy[None]

    rep_flat = jnp.concatenate([small_grads[l][n].reshape(-1) for l in range(DEPTH) for n, _ in REPLICATED])
    rep_flat = jnp.pad(rep_flat, (0, N_CHIPS * _REP_QUARTER - rep_flat.shape[0]))
    rows = []
    for j in range(N_CHIPS):
        pieces = [_shard_cols(small_grads[l][n], j) for l in range(DEPTH) for n, _ in SMALL_SHARDS]
        pieces.append(rep_flat[j * _REP_QUARTER:(j + 1) * _REP_QUARTER])
        rows.append(_pack_rows(pieces, SMALL_ROWS, F32))
    small_red = _reduce_scatter_chips(jnp.stack(rows), "small")
    small_all = _all_gather_chips(small_red, "gather_small").reshape(N_CHIPS, -1)

    grads = {}
    for i, (n, _) in enumerate(BIG_SHARDS):
        grads[n] = jnp.stack([big_red[l][i][:, :IN_SHARD] if n == "w_in" else big_red[l][i] for l in range(DEPTH)])
    mine_small = small_red.reshape(-1)
    parts = _unpack(mine_small, [s for _ in range(DEPTH) for _, s in SMALL_SHARDS])
    for i, (n, _) in enumerate(SMALL_SHARDS):
        grads[n] = jnp.stack([parts[l * len(SMALL_SHARDS) + i] for l in range(DEPTH)])
    off = DEPTH * _SMALL_ELEMS
    rep_all = jnp.concatenate([small_all[j, off:off + _REP_QUARTER] for j in range(N_CHIPS)])
    parts = _unpack(rep_all, [s for _ in range(DEPTH) for _, s in REPLICATED])
    for i, (n, _) in enumerate(REPLICATED):
        grads[n] = jnp.stack([parts[l * len(REPLICATED) + i] for l in range(DEPTH)])

    deltas, new_m, new_v = {}, {}, {}
    for n in WEIGHT_ORDER:
        deltas[n], new_m[n], new_v[n] = _adamw(wts[n], grads[n], moms[n], vels[n], "adamw_" + n)
    return (loss, grad_x, *[grads[n] for n in WEIGHT_ORDER], *[deltas[n] for n in WEIGHT_ORDER],
            *[new_m[n] for n in WEIGHT_ORDER], *[new_v[n] for n in WEIGHT_ORDER])
```

```python
import functools
import math

import jax
import jax.numpy as jnp
from jax import lax
from jax.experimental import pallas as pl
from jax.experimental.pallas import tpu as pltpu

F32 = jnp.float32
BF16 = jnp.bfloat16
MXU_DTYPE = jnp.bfloat16

D_MODEL = 1024
HEAD_DIM = 64
N_HEADS = 8
D_ATT = 512
D_CONV = 256
D_SGU = 256
N_GROUPS = 4
CHUNK = 128
D_FF = 2816
DEPTH = 4
RMS_EPS = 1e-6
LN_EPS = 1e-5
N_CHIPS = 4
LANES = 128
PACK_COLS = 1024
HALO = 16

ADAM_LR = 0.001
ADAM_B1 = 0.9
ADAM_B2 = 0.999
ADAM_EPS = 1e-08
ADAM_WD = 0.01
ADAM_STEP = 10

OFF_GL = 0
OFF_Q = 3 * D_MODEL
OFF_K = OFF_Q + D_ATT
OFF_V = OFF_K + D_ATT
OFF_BG = OFF_V + D_ATT
OFF_CG = OFF_BG + D_CONV
OFF_HC = OFF_CG + D_CONV
OFF_U = OFF_HC + D_CONV
OFF_VS = OFF_U + D_SGU
W_P = OFF_VS + D_SGU
F_ROWS = 16

VMEM_LIMIT = 56 * 1024 * 1024
MESH = pl.DeviceIdType.MESH


def _params(sem=None):
    if sem is None:
        return pltpu.CompilerParams(vmem_limit_bytes=VMEM_LIMIT)
    return pltpu.CompilerParams(dimension_semantics=sem, vmem_limit_bytes=VMEM_LIMIT)


def _tile(dim, pref):
    if dim <= pref:
        return dim
    if dim % pref == 0:
        return pref
    return dim


_DIMS = {"nn": (((1,), (0,)), ((), ())), "nt": (((1,), (1,)), ((), ())), "tn": (((0,), (0,)), ((), ()))}


def _mm(a, b, mode, out_dtype, name, tm, tn, tk, chip_of=None):
    if mode == "tn":
        K, M = a.shape
    else:
        M, K = a.shape
    N = b.shape[0] if mode == "nt" else b.shape[1]
    tm, tn, tk = _tile(M, tm), _tile(N // N_CHIPS if chip_of else N, tn), _tile(K, tk)
    nk = K // tk
    dims = _DIMS[mode]

    def body(a_ref, b_ref, o_ref, *acc):
        part = lax.dot_general(a_ref[...].astype(MXU_DTYPE), b_ref[...].astype(MXU_DTYPE), dims,
                               preferred_element_type=F32)
        if nk == 1:
            o_ref[...] = part.astype(o_ref.dtype)
        else:
            acc_ref = acc[0]
            k = pl.program_id(2)

            @pl.when(k == 0)
            def _():
                acc_ref[...] = part

            @pl.when(k > 0)
            def _():
                acc_ref[...] += part

            @pl.when(k == nk - 1)
            def _():
                o_ref[...] = acc_ref[...].astype(o_ref.dtype)

    if mode == "tn":
        a_spec = pl.BlockSpec((tk, tm), lambda i, j, k: (k, i))
    else:
        a_spec = pl.BlockSpec((tm, tk), lambda i, j, k: (i, k))
    if mode == "nt":
        b_spec = pl.BlockSpec((tn, tk), lambda i, j, k: (j, k))
    else:
        b_spec = pl.BlockSpec((tk, tn), lambda i, j, k: (k, j))
    if chip_of is None:
        out_spec = pl.BlockSpec((tm, tn), lambda i, j, k: (i, j))
        out_shape = jax.ShapeDtypeStruct((M, N), out_dtype)
    else:
        per = (N // N_CHIPS) // tn
        out_spec = pl.BlockSpec((None, tm, tn), lambda i, j, k: (chip_of(j // per), i, j % per))
        out_shape = jax.ShapeDtypeStruct((N_CHIPS, M, N // N_CHIPS), out_dtype)
    return pl.pallas_call(
        body,
        name=name,
        grid=(M // tm, N // tn, nk),
        in_specs=[a_spec, b_spec],
        out_specs=out_spec,
        out_shape=out_shape,
        scratch_shapes=[pltpu.VMEM((tm, tn), F32)] if nk > 1 else [],
        compiler_params=_params(("parallel", "parallel", "arbitrary")),
    )(a, b)


_GELU_K = math.sqrt(2.0 / math.pi)
_GELU_C = 0.044715


def _gelu(x):
    t = jnp.tanh(_GELU_K * (x + _GELU_C * (x * x * x)))
    return x * (0.5 * (1.0 + t))


def _gelu_and_grad(x):
    x2 = x * x
    t = jnp.tanh(_GELU_K * (x + _GELU_C * (x2 * x)))
    cdf = 0.5 * (1.0 + t)
    dcdf = 0.5 * (1.0 - t * t) * (_GELU_K * (1.0 + 3.0 * _GELU_C * x2))
    return x * cdf, cdf + x * dcdf


def _sigmoid(x):
    return 1.0 / (1.0 + jnp.exp(-x))


def _shift_down(cur, prev, k):
    h = prev.shape[0]
    ext = jnp.concatenate([prev, cur], axis=0)
    return pltpu.roll(ext, k, 0)[h:]


def _shift_up(cur, nxt, k):
    t, h = cur.shape[0], nxt.shape[0]
    ext = jnp.concatenate([cur, nxt], axis=0)
    return pltpu.roll(ext, t + h - k, 0)[:t]


def _row_sum8(x):
    t, c = x.shape
    return jnp.sum(x.reshape(t // 8, 8, c), axis=0)


_DEP = pl.BlockSpec((8, LANES), lambda i: (0, 0))


def _rms_fwd(x, g, name, dep=None):
    s, d = x.shape
    t = _tile(s, 512)

    def body(x_ref, g_ref, *rest):
        o_ref = rest[-1]
        xv = x_ref[...]
        r = lax.rsqrt(jnp.mean(xv * xv, axis=-1, keepdims=True) + RMS_EPS)
        o_ref[...] = (xv * r * g_ref[...]).astype(o_ref.dtype)

    deps = [] if dep is None else [dep]
    return pl.pallas_call(
        body, name=name, grid=(s // t,),
        in_specs=[pl.BlockSpec((t, d), lambda i: (i, 0)), pl.BlockSpec((1, d), lambda i: (0, 0))] + [_DEP] * len(deps),
        out_specs=pl.BlockSpec((t, d), lambda i: (i, 0)),
        out_shape=jax.ShapeDtypeStruct((s, d), BF16),
        compiler_params=_params(("parallel",)),
    )(x, g, *deps)


def _resid_post(x, y, g, name):
    s, d = x.shape
    t = _tile(s, 512)

    def body(x_ref, y_ref, g_ref, o_ref):
        yv = y_ref[...]
        r = lax.rsqrt(jnp.mean(yv * yv, axis=-1, keepdims=True) + RMS_EPS)
        o_ref[...] = x_ref[...] + yv * r * g_ref[...]

    row = pl.BlockSpec((t, d), lambda i: (i, 0))
    return pl.pallas_call(
        body, name=name, grid=(s // t,),
        in_specs=[row, row, pl.BlockSpec((1, d), lambda i: (0, 0))],
        out_specs=row,
        out_shape=jax.ShapeDtypeStruct((s, d), F32),
        compiler_params=_params(("parallel",)),
    )(x, y, g)


def _rms_bwd(xin, g, dys, dres, out_dtype, name, dep=None):
    s, d = xin.shape
    t = _tile(s, 512)
    n = s // t
    n_dy = len(dys)
    has_res = dres is not None
    deps = [] if dep is None else [dep]

    def body(*refs):
        x_ref, g_ref = refs[0], refs[1]
        dy_refs = refs[2:2 + n_dy]
        pos = 2 + n_dy
        res_ref = refs[pos] if has_res else None
        pos += (1 if has_res else 0) + len(deps)
        dx_ref, dg_ref, acc_ref = refs[pos], refs[pos + 1], refs[pos + 2]
        i = pl.program_id(0)
        xv = x_ref[...]
        dy = dy_refs[0][...].astype(F32)
        for extra in dy_refs[1:]:
            dy = dy + extra[...].astype(F32)
        r = lax.rsqrt(jnp.mean(xv * xv, axis=-1, keepdims=True) + RMS_EPS)
        u = dy * g_ref[...]
        xr = xv * r
        dx = r * (u - xr * jnp.mean(u * xr, axis=-1, keepdims=True))
        if has_res:
            dx = dx + res_ref[...]
        dx_ref[...] = dx.astype(dx_ref.dtype)
        part = _row_sum8(dy * xr)

        @pl.when(i == 0)
        def _():
            acc_ref[...] = part

        @pl.when(i > 0)
        def _():
            acc_ref[...] += part

        @pl.when(i == n - 1)
        def _():
            dg_ref[...] = jnp.sum(acc_ref[...], axis=0, keepdims=True)

    row = pl.BlockSpec((t, d), lambda i: (i, 0))
    vec = pl.BlockSpec((1, d), lambda i: (0, 0))
    ins = [xin, g, *dys] + ([dres] if has_res else []) + deps
    return pl.pallas_call(
        body, name=name, grid=(n,),
        in_specs=[row, vec] + [row] * (n_dy + (1 if has_res else 0)) + [_DEP] * len(deps),
        out_specs=[row, vec],
        out_shape=[jax.ShapeDtypeStruct((s, d), out_dtype), jax.ShapeDtypeStruct((1, d), F32)],
        scratch_shapes=[pltpu.VMEM((8, d), F32)],
        compiler_params=_params(("arbitrary",)),
    )(*ins)


def _loss_head(y, target, name):
    s, d = y.shape
    t = _tile(s, 512)
    n = s // t

    def body(y_ref, t_ref, dy_ref, loss_ref, acc_ref):
        i = pl.program_id(0)
        e = y_ref[...] - t_ref[...]
        dy_ref[...] = e * (1.0 / d)
        part = _row_sum8(e * e)

        @pl.when(i == 0)
        def _():
            acc_ref[...] = part

        @pl.when(i > 0)
        def _():
            acc_ref[...] += part

        @pl.when(i == n - 1)
        def _():
            tot = jnp.sum(jnp.sum(acc_ref[...], axis=0, keepdims=True), axis=1, keepdims=True)
            loss_ref[...] = tot * (0.5 / d)

    row = pl.BlockSpec((t, d), lambda i: (i, 0))
    return pl.pallas_call(
        body, name=name, grid=(n,),
        in_specs=[row, row],
        out_specs=[row, pl.BlockSpec((1, 1), lambda i: (0, 0))],
        out_shape=[jax.ShapeDtypeStruct((s, d), F32), jax.ShapeDtypeStruct((1, 1), F32)],
        scratch_shapes=[pltpu.VMEM((8, d), F32)],
        compiler_params=_params(("arbitrary",)),
    )(y, target)


def _split3(x):
    hi = x.astype(BF16)
    r1 = x - hi.astype(F32)
    mid = r1.astype(BF16)
    lo = (r1 - mid.astype(F32)).astype(BF16)
    return hi, mid, lo


def _tri_dot(x, tri):
    hi, mid, lo = _split3(x)
    dn = _DIMS["nn"]
    out = lax.dot_general(hi, tri, dn, preferred_element_type=F32)
    out = out + lax.dot_general(mid, tri, dn, preferred_element_type=F32)
    return out + lax.dot_general(lo, tri, dn, preferred_element_type=F32)


def _log_sigmoid(z):
    return jnp.minimum(z, 0.0) - jnp.log(1.0 + jnp.exp(-jnp.abs(z)))


def _gate_fwd(f_row, b_col, name):
    rows, s = f_row.shape
    t = _tile(s, 512)
    n = s // t

    def body(f_ref, b_ref, c_ref, carry_ref):
        i = pl.program_id(0)

        @pl.when(i == 0)
        def _():
            carry_ref[...] = jnp.zeros_like(carry_ref)

        logf = _log_sigmoid(f_ref[...] + b_ref[...])
        r = lax.broadcasted_iota(jnp.int32, (t, t), 0)
        c = lax.broadcasted_iota(jnp.int32, (t, t), 1)
        tri = jnp.where(r <= c, 1.0, 0.0).astype(BF16)
        cs = _tri_dot(logf, tri) + carry_ref[...]
        carry_ref[...] = cs[:, t - 1:t]
        for h in range(N_HEADS):
            c_ref[h] = jnp.broadcast_to(cs[h:h + 1, :], (8, t))

    return pl.pallas_call(
        body, name=name, grid=(n,),
        in_specs=[pl.BlockSpec((rows, t), lambda i: (0, i)), pl.BlockSpec((rows, 1), lambda i: (0, 0))],
        out_specs=pl.BlockSpec((N_HEADS, 8, t), lambda i: (0, 0, i)),
        out_shape=jax.ShapeDtypeStruct((N_HEADS, 8, s), F32),
        scratch_shapes=[pltpu.VMEM((rows, 1), F32)],
        compiler_params=_params(("arbitrary",)),
    )(f_row, b_col)


def _gate_bwd(f_row, b_col, dc_even, dc_odd, name):
    rows, s = f_row.shape
    t = _tile(s, 512)
    n = s // t

    def body(f_ref, b_ref, dce_ref, dco_ref, df_ref, db_ref, carry_ref, acc_ref):
        i = pl.program_id(0)

        @pl.when(i == 0)
        def _():
            carry_ref[...] = jnp.zeros_like(carry_ref)
            acc_ref[...] = jnp.zeros_like(acc_ref)

        head = lax.broadcasted_iota(jnp.int32, (rows, t), 0)
        dcv = jnp.zeros((rows, t), F32)
        for h in range(N_HEADS):
            src = dce_ref if h % 2 == 0 else dco_ref
            dcv = jnp.where(head == h, jnp.broadcast_to(src[h // 2, 0:1, :], (rows, t)), dcv)
        r = lax.broadcasted_iota(jnp.int32, (t, t), 0)
        c = lax.broadcasted_iota(jnp.int32, (t, t), 1)
        tri = jnp.where(r >= c, 1.0, 0.0).astype(BF16)
        dlogf = _tri_dot(dcv, tri) + carry_ref[...]
        carry_ref[...] = dlogf[:, 0:1]
        z = f_ref[...] + b_ref[...]
        df = dlogf * _sigmoid(-z)
        df_ref[...] = df.astype(df_ref.dtype)
        acc_ref[...] += jnp.sum(df, axis=1, keepdims=True)

        @pl.when(i == n - 1)
        def _():
            db_ref[...] = acc_ref[...]

    rev = lambda i: (0, n - 1 - i)
    dc_spec = pl.BlockSpec((N_HEADS // 2, 8, t), lambda i: (0, 0, n - 1 - i))
    return pl.pallas_call(
        body, name=name, grid=(n,),
        in_specs=[pl.BlockSpec((rows, t), rev), pl.BlockSpec((rows, 1), lambda i: (0, 0)), dc_spec, dc_spec],
        out_specs=[pl.BlockSpec((rows, t), rev), pl.BlockSpec((rows, 1), lambda i: (0, 0))],
        out_shape=[jax.ShapeDtypeStruct((rows, s), BF16), jax.ShapeDtypeStruct((rows, 1), F32)],
        scratch_shapes=[pltpu.VMEM((rows, 1), F32), pltpu.VMEM((rows, 1), F32)],
        compiler_params=_params(("arbitrary",)),
    )(f_row, b_col, dc_even, dc_odd)


_NEG = -1e30
_SCALE = HEAD_DIM ** -0.5


def _head_masks():
    lane = lax.broadcasted_iota(jnp.int32, (1, LANES), 1)
    return [lane < HEAD_DIM, lane >= HEAD_DIM]


def _attn_fwd(h, c, name):
    s = h.shape[0]
    t = _tile(s, 512)
    n = s // t
    qb, kb, vb = OFF_Q // LANES, OFF_K // LANES, OFF_V // LANES

    def body(q_ref, k_ref, v_ref, c0_ref, c1_ref, o_ref, of_ref, lse_ref, m_ref, l_ref, acc_ref):
        qi, ki = pl.program_id(1), pl.program_id(2)
        masks = _head_masks()

        @pl.when(ki == 0)
        def _():
            m_ref[...] = jnp.full_like(m_ref, _NEG)
            l_ref[...] = jnp.zeros_like(l_ref)
            acc_ref[...] = jnp.zeros_like(acc_ref)

        def step(diag):
            q = q_ref[...] * _SCALE
            k = k_ref[...]
            v = v_ref[...]
            acc = acc_ref[...]
            for hh, c_ref in enumerate((c0_ref, c1_ref)):
                qh = jnp.where(masks[hh], q, jnp.zeros_like(q))
                sc = lax.dot_general(qh, k, _DIMS["nt"], preferred_element_type=F32) - c_ref[0, 0:1, :]
                if diag:
                    r = lax.broadcasted_iota(jnp.int32, (t, t), 0)
                    cc = lax.broadcasted_iota(jnp.int32, (t, t), 1)
                    sc = jnp.where(r >= cc, sc, _NEG)
                m_prev = m_ref[hh]
                m_new = jnp.maximum(m_prev, jnp.max(sc, axis=1, keepdims=True))
                alpha = jnp.exp(m_prev - m_new)
                p = jnp.exp(sc - m_new)
                l_ref[hh] = alpha * l_ref[hh] + jnp.sum(p, axis=1, keepdims=True)
                m_ref[hh] = m_new
                p_hi = p.astype(MXU_DTYPE)
                p_lo = (p - p_hi.astype(F32)).astype(MXU_DTYPE)
                pv = (lax.dot_general(p_hi, v, _DIMS["nn"], preferred_element_type=F32)
                      + lax.dot_general(p_lo, v, _DIMS["nn"], preferred_element_type=F32))
                acc = jnp.where(masks[hh], alpha * acc + pv, acc)
            acc_ref[...] = acc

        @pl.when(ki < qi)
        def _():
            step(False)

        @pl.when(ki == qi)
        def _():
            step(True)
            inv = jnp.where(masks[0], 1.0 / l_ref[0], 1.0 / l_ref[1])
            out = acc_ref[...] * inv
            o_ref[...] = out.astype(o_ref.dtype)
            of_ref[...] = out
            lse_ref[...] = jnp.where(masks[0], m_ref[0] + jnp.log(l_ref[0]), m_ref[1] + jnp.log(l_ref[1]))

    kv_row = lambda p, qi, ki: jnp.minimum(ki, qi)
    return pl.pallas_call(
        body, name=name, grid=(N_HEADS // 2, n, n),
        in_specs=[
            pl.BlockSpec((t, LANES), lambda p, qi, ki: (qi, qb + p)),
            pl.BlockSpec((t, LANES), lambda p, qi, ki: (kv_row(p, qi, ki), kb + p)),
            pl.BlockSpec((t, LANES), lambda p, qi, ki: (kv_row(p, qi, ki), vb + p)),
            pl.BlockSpec((1, 8, t), lambda p, qi, ki: (2 * p, 0, kv_row(p, qi, ki))),
            pl.BlockSpec((1, 8, t), lambda p, qi, ki: (2 * p + 1, 0, kv_row(p, qi, ki))),
        ],
        out_specs=[pl.BlockSpec((t, LANES), lambda p, qi, ki: (qi, p))] * 3,
        out_shape=[jax.ShapeDtypeStruct((s, D_ATT), BF16), jax.ShapeDtypeStruct((s, D_ATT), F32),
                   jax.ShapeDtypeStruct((s, D_ATT), F32)],
        scratch_shapes=[pltpu.VMEM((2, t, 1), F32), pltpu.VMEM((2, t, 1), F32), pltpu.VMEM((t, LANES), F32)],
        compiler_params=_params(("parallel", "parallel", "arbitrary")),
    )(h, h, h, c, c)


def _attn_bwd(h, c, o, lse, do, name):
    s = h.shape[0]
    t = _tile(s, 512)
    n = s // t
    qb, kb, vb = OFF_Q // LANES, OFF_K // LANES, OFF_V // LANES

    def body(q_ref, k_ref, v_ref, c0_ref, c1_ref, o_ref, lse_ref, do_ref,
             dq_ref, dk_ref, dv_ref, dc0_ref, dc1_ref, dk_acc, dv_acc, dc_acc):
        ki, qi = pl.program_id(1), pl.program_id(2)
        masks = _head_masks()

        @pl.when((ki == 0) & (qi == 0))
        def _():
            dq_ref[...] = jnp.zeros_like(dq_ref)

        @pl.when(qi == ki)
        def _():
            dk_acc[...] = jnp.zeros_like(dk_acc)
            dv_acc[...] = jnp.zeros_like(dv_acc)
            dc_acc[...] = jnp.zeros_like(dc_acc)

        def step(diag):
            q = q_ref[...] * _SCALE
            k = k_ref[...]
            v = v_ref[...]
            dov = do_ref[...]
            lsev = lse_ref[...]
            prod = dov.astype(F32) * o_ref[...]
            dq_blk = jnp.zeros((t, LANES), F32)
            dk_blk = dk_acc[...]
            dv_blk = dv_acc[...]
            for hh, c_ref in enumerate((c0_ref, c1_ref)):
                mk = masks[hh]
                delta = jnp.sum(jnp.where(mk, prod, 0.0), axis=1, keepdims=True)
                lse_h = lsev[:, hh * HEAD_DIM:hh * HEAD_DIM + 1]
                qh = jnp.where(mk, q, jnp.zeros_like(q))
                doh = jnp.where(mk, dov, jnp.zeros_like(dov))
                sc = lax.dot_general(qh, k, _DIMS["nt"], preferred_element_type=F32) - c_ref[0, 0:1, :]
                p = jnp.exp(sc - lse_h)
                if diag:
                    r = lax.broadcasted_iota(jnp.int32, (t, t), 0)
                    cc = lax.broadcasted_iota(jnp.int32, (t, t), 1)
                    p = jnp.where(r >= cc, p, 0.0)
                dp = lax.dot_general(doh, v, _DIMS["nt"], preferred_element_type=F32)
                ds = p * (dp - delta)
                dsb = ds.astype(MXU_DTYPE)
                pb = p.astype(MXU_DTYPE)
                dv_blk = jnp.where(mk, dv_blk + lax.dot_general(pb, dov, _DIMS["tn"], preferred_element_type=F32),
                                   dv_blk)
                dk_blk = jnp.where(mk, dk_blk + lax.dot_general(dsb, q, _DIMS["tn"], preferred_element_type=F32),
                                   dk_blk)
                dq_blk = jnp.where(mk, lax.dot_general(dsb, k, _DIMS["nn"], preferred_element_type=F32), dq_blk)
                dc_acc[hh] = dc_acc[hh] - jnp.sum(ds, axis=0, keepdims=True)
            dk_acc[...] = dk_blk
            dv_acc[...] = dv_blk
            rows = pl.ds(pl.multiple_of(qi * t, t), t)
            dq_ref[rows, :] = dq_ref[rows, :] + dq_blk * _SCALE

        @pl.when(qi > ki)
        def _():
            step(False)

        @pl.when(qi == ki)
        def _():
            step(True)

        @pl.when(qi == n - 1)
        def _():
            dk_ref[...] = dk_acc[...].astype(dk_ref.dtype)
            dv_ref[...] = dv_acc[...].astype(dv_ref.dtype)
            dc0_ref[0] = jnp.broadcast_to(dc_acc[0], (8, t))
            dc1_ref[0] = jnp.broadcast_to(dc_acc[1], (8, t))

    q_row = lambda p, ki, qi: jnp.maximum(qi, ki)
    return pl.pallas_call(
        body, name=name, grid=(N_HEADS // 2, n, n),
        in_specs=[
            pl.BlockSpec((t, LANES), lambda p, ki, qi: (q_row(p, ki, qi), qb + p)),
            pl.BlockSpec((t, LANES), lambda p, ki, qi: (ki, kb + p)),
            pl.BlockSpec((t, LANES), lambda p, ki, qi: (ki, vb + p)),
            pl.BlockSpec((1, 8, t), lambda p, ki, qi: (2 * p, 0, ki)),
            pl.BlockSpec((1, 8, t), lambda p, ki, qi: (2 * p + 1, 0, ki)),
            pl.BlockSpec((t, LANES), lambda p, ki, qi: (q_row(p, ki, qi), p)),
            pl.BlockSpec((t, LANES), lambda p, ki, qi: (q_row(p, ki, qi), p)),
            pl.BlockSpec((t, LANES), lambda p, ki, qi: (q_row(p, ki, qi), p)),
        ],
        out_specs=[
            pl.BlockSpec((s, LANES), lambda p, ki, qi: (0, p)),
            pl.BlockSpec((t, LANES), lambda p, ki, qi: (ki, p)),
            pl.BlockSpec((t, LANES), lambda p, ki, qi: (ki, p)),
            pl.BlockSpec((1, 8, t), lambda p, ki, qi: (p, 0, ki)),
            pl.BlockSpec((1, 8, t), lambda p, ki, qi: (p, 0, ki)),
        ],
        out_shape=[jax.ShapeDtypeStruct((s, D_ATT), F32), jax.ShapeDtypeStruct((s, D_ATT), BF16),
                   jax.ShapeDtypeStruct((s, D_ATT), BF16), jax.ShapeDtypeStruct((N_HEADS // 2, 8, s), F32),
                   jax.ShapeDtypeStruct((N_HEADS // 2, 8, s), F32)],
        scratch_shapes=[pltpu.VMEM((t, LANES), F32), pltpu.VMEM((t, LANES), F32), pltpu.VMEM((2, 1, t), F32)],
        compiler_params=_params(("parallel", "arbitrary", "arbitrary")),
    )(h, h, h, c, c, o, lse, do)


def _conv3(z, z_prev, w_ref):
    return (w_ref[2:3, :] * z + w_ref[1:2, :] * _shift_down(z, z_prev, 1)
            + w_ref[0:1, :] * _shift_down(z, z_prev, 2))


def _sconv_fwd(h, w, name):
    s = h.shape[0]
    t = _tile(s, 512)
    r = t // HALO
    c = D_CONV
    b_bg, b_cg, b_hc = OFF_BG // c, OFF_CG // c, OFF_HC // c

    def body(bg_ref, cg_ref, hc_ref, cgp_ref, hcp_ref, w_ref, y_ref):
        i = pl.program_id(0)
        live = (i > 0).astype(F32)
        z = cg_ref[...].astype(F32) * hc_ref[...].astype(F32)
        zp = cgp_ref[...].astype(F32) * hcp_ref[...].astype(F32) * live
        y_ref[...] = (bg_ref[...].astype(F32) * _conv3(z, zp, w_ref)).astype(y_ref.dtype)

    cur = lambda b: pl.BlockSpec((t, c), lambda i: (i, b))
    prev = lambda b: pl.BlockSpec((HALO, c), lambda i: (jnp.maximum(i * r - 1, 0), b))
    return pl.pallas_call(
        body, name=name, grid=(s // t,),
        in_specs=[cur(b_bg), cur(b_cg), cur(b_hc), prev(b_cg), prev(b_hc), pl.BlockSpec((8, c), lambda i: (0, 0))],
        out_specs=pl.BlockSpec((t, c), lambda i: (i, 0)),
        out_shape=jax.ShapeDtypeStruct((s, c), BF16),
        compiler_params=_params(("parallel",)),
    )(h, h, h, h, h, w)


def _sconv_bwd(h, w, dy, name):
    s = h.shape[0]
    t = _tile(s, 512)
    n = s // t
    r = t // HALO
    nh = s // HALO
    c = D_CONV
    b_bg, b_cg, b_hc = OFF_BG // c, OFF_CG // c, OFF_HC // c

    def body(bg_ref, cg_ref, hc_ref, cgp_ref, hcp_ref, bgn_ref, dy_ref, dyn_ref, w_ref, d_ref, dw_ref, acc_ref):
        i = pl.program_id(0)
        has_prev = (i > 0).astype(F32)
        has_next = (i < n - 1).astype(F32)
        bg = bg_ref[...].astype(F32)
        cg = cg_ref[...].astype(F32)
        hc = hc_ref[...].astype(F32)
        dyv = dy_ref[...].astype(F32)
        z = cg * hc
        zp = cgp_ref[...].astype(F32) * hcp_ref[...].astype(F32) * has_prev
        z1 = _shift_down(z, zp, 1)
        z2 = _shift_down(z, zp, 2)
        cz = w_ref[2:3, :] * z + w_ref[1:2, :] * z1 + w_ref[0:1, :] * z2
        dcz = dyv * bg
        dczn = dyn_ref[...].astype(F32) * bgn_ref[...].astype(F32) * has_next
        dz = (w_ref[2:3, :] * dcz + w_ref[1:2, :] * _shift_up(dcz, dczn, 1)
              + w_ref[0:1, :] * _shift_up(dcz, dczn, 2))
        d_ref[:, 0:c] = (dyv * cz).astype(d_ref.dtype)
        d_ref[:, c:2 * c] = (dz * hc).astype(d_ref.dtype)
        d_ref[:, 2 * c:3 * c] = (dz * cg).astype(d_ref.dtype)

        @pl.when(i == 0)
        def _():
            acc_ref[...] = jnp.zeros_like(acc_ref)

        acc_ref[0] += _row_sum8(dcz * z2)
        acc_ref[1] += _row_sum8(dcz * z1)
        acc_ref[2] += _row_sum8(dcz * z)

        @pl.when(i == n - 1)
        def _():
            rows = [jnp.sum(acc_ref[k], axis=0, keepdims=True) for k in range(3)]
            dw_ref[...] = jnp.concatenate(rows + [jnp.zeros((5, c), F32)], axis=0)

    cur = lambda b: pl.BlockSpec((t, c), lambda i: (i, b))
    prev = lambda b: pl.BlockSpec((HALO, c), lambda i: (jnp.maximum(i * r - 1, 0), b))
    nxt = lambda b: pl.BlockSpec((HALO, c), lambda i: (jnp.minimum((i + 1) * r, nh - 1), b))
    return pl.pallas_call(
        body, name=name, grid=(n,),
        in_specs=[cur(b_bg), cur(b_cg), cur(b_hc), prev(b_cg), prev(b_hc), nxt(b_bg),
                  cur(0), nxt(0), pl.BlockSpec((8, c), lambda i: (0, 0))],
        out_specs=[pl.BlockSpec((t, 3 * c), lambda i: (i, 0)), pl.BlockSpec((8, c), lambda i: (0, 0))],
        out_shape=[jax.ShapeDtypeStruct((s, 3 * c), BF16), jax.ShapeDtypeStruct((8, c), F32)],
        scratch_shapes=[pltpu.VMEM((3, 8, c), F32)],
        compiler_params=_params(("arbitrary",)),
    )(h, h, h, h, h, h, dy, dy, w)


def _group_masks():
    lane = lax.broadcasted_iota(jnp.int32, (1, D_SGU), 1)
    return [(lane >= g * HEAD_DIM) & (lane < (g + 1) * HEAD_DIM) for g in range(N_GROUPS)]


def _tril_weights(w_ref):
    r = lax.broadcasted_iota(jnp.int32, (CHUNK, CHUNK), 0)
    c = lax.broadcasted_iota(jnp.int32, (CHUNK, CHUNK), 1)
    return [jnp.where(r >= c, w_ref[g], 0.0).astype(MXU_DTYPE) for g in range(N_GROUPS)]


def _sgu_ln(vs, g_ref, b_ref):
    vg, dvg = _gelu_and_grad(vs)
    mu = jnp.mean(vg, axis=-1, keepdims=True)
    xc = vg - mu
    rstd = lax.rsqrt(jnp.mean(xc * xc, axis=-1, keepdims=True) + LN_EPS)
    xhat = xc * rstd
    return xhat * g_ref[...] + b_ref[...], xhat, rstd, dvg


def _sgu_fwd(h, ln_g, ln_b, w_s, bias, name):
    s = h.shape[0]
    t = _tile(s, 512)
    c = D_SGU
    b_u, b_v = OFF_U // c, OFF_VS // c

    def body(u_ref, v_ref, g_ref, b_ref, w_ref, bias_ref, y_ref):
        gm = _group_masks()
        wm = _tril_weights(w_ref)
        ug = _gelu(u_ref[...].astype(F32))
        vn, _, _, _ = _sgu_ln(v_ref[...].astype(F32), g_ref, b_ref)
        vnb = vn.astype(MXU_DTYPE)
        for ch in range(t // CHUNK):
            rows = slice(ch * CHUNK, (ch + 1) * CHUNK)
            mixed = bias_ref[...]
            for g in range(N_GROUPS):
                mg = lax.dot_general(wm[g], vnb[rows], _DIMS["nn"], preferred_element_type=F32)
                mixed = jnp.where(gm[g], mixed + mg, mixed)
            y_ref[rows, :] = (ug[rows] * mixed).astype(y_ref.dtype)

    full = lambda shp: pl.BlockSpec(shp, lambda i: (0,) * len(shp))
    return pl.pallas_call(
        body, name=name, grid=(s // t,),
        in_specs=[pl.BlockSpec((t, c), lambda i: (i, b_u)), pl.BlockSpec((t, c), lambda i: (i, b_v)),
                  full((1, c)), full((1, c)), full((N_GROUPS, CHUNK, CHUNK)), full((CHUNK, c))],
        out_specs=pl.BlockSpec((t, c), lambda i: (i, 0)),
        out_shape=jax.ShapeDtypeStruct((s, c), BF16),
        compiler_params=_params(("parallel",)),
    )(h, h, ln_g, ln_b, w_s, bias)


def _sgu_bwd(h, ln_g, ln_b, w_s, bias, dy, name):
    s = h.shape[0]
    t = _tile(s, 512)
    n = s // t
    c = D_SGU
    b_u, b_v = OFF_U // c, OFF_VS // c

    def body(u_ref, v_ref, g_ref, b_ref, w_ref, bias_ref, dy_ref,
             d_ref, dg_ref, db_ref, dw_ref, dbias_ref, dg_acc, db_acc):
        i = pl.program_id(0)
        gm = _group_masks()
        wm = _tril_weights(w_ref)

        @pl.when(i == 0)
        def _():
            dg_acc[...] = jnp.zeros_like(dg_acc)
            db_acc[...] = jnp.zeros_like(db_acc)
            dw_ref[...] = jnp.zeros_like(dw_ref)
            dbias_ref[...] = jnp.zeros_like(dbias_ref)

        ug, dug = _gelu_and_grad(u_ref[...].astype(F32))
        vn, xhat, rstd, dvg = _sgu_ln(v_ref[...].astype(F32), g_ref, b_ref)
        vnb = vn.astype(MXU_DTYPE)
        dyv = dy_ref[...].astype(F32)
        dmixed = dyv * ug
        dmb = dmixed.astype(MXU_DTYPE)
        dvn_parts = []
        for ch in range(t // CHUNK):
            rows = slice(ch * CHUNK, (ch + 1) * CHUNK)
            mixed = bias_ref[...]
            dvn = jnp.zeros((CHUNK, c), F32)
            for g in range(N_GROUPS):
                mg = lax.dot_general(wm[g], vnb[rows], _DIMS["nn"], preferred_element_type=F32)
                mixed = jnp.where(gm[g], mixed + mg, mixed)
                dvn = jnp.where(gm[g], lax.dot_general(wm[g], dmb[rows], _DIMS["tn"], preferred_element_type=F32),
                                dvn)
                dmg = jnp.where(gm[g], dmb[rows], jnp.zeros_like(dmb[rows]))
                dw_ref[g] += lax.dot_general(dmg, vnb[rows], _DIMS["nt"], preferred_element_type=F32)
            d_ref[rows, 0:c] = (dyv[rows] * mixed * dug[rows]).astype(d_ref.dtype)
            dbias_ref[...] += dmixed[rows]
            dvn_parts.append(dvn)
        dvn = jnp.concatenate(dvn_parts, axis=0)
        dg_acc[...] += _row_sum8(dvn * xhat)
        db_acc[...] += _row_sum8(dvn)
        dxh = dvn * g_ref[...]
        dvgl = rstd * (dxh - jnp.mean(dxh, axis=-1, keepdims=True)
                       - xhat * jnp.mean(dxh * xhat, axis=-1, keepdims=True))
        d_ref[:, c:2 * c] = (dvgl * dvg).astype(d_ref.dtype)

        @pl.when(i == n - 1)
        def _():
            dg_ref[...] = jnp.sum(dg_acc[...], axis=0, keepdims=True)
            db_ref[...] = jnp.sum(db_acc[...], axis=0, keepdims=True)
            r = lax.broadcasted_iota(jnp.int32, (CHUNK, CHUNK), 0)
            cc = lax.broadcasted_iota(jnp.int32, (CHUNK, CHUNK), 1)
            for g in range(N_GROUPS):
                dw_ref[g] = jnp.where(r >= cc, dw_ref[g], 0.0)

    full = lambda shp: pl.BlockSpec(shp, lambda i: (0,) * len(shp))
    return pl.pallas_call(
        body, name=name, grid=(n,),
        in_specs=[pl.BlockSpec((t, c), lambda i: (i, b_u)), pl.BlockSpec((t, c), lambda i: (i, b_v)),
                  full((1, c)), full((1, c)), full((N_GROUPS, CHUNK, CHUNK)), full((CHUNK, c)),
                  pl.BlockSpec((t, c), lambda i: (i, 0))],
        out_specs=[pl.BlockSpec((t, 2 * c), lambda i: (i, 0)), full((1, c)), full((1, c)),
                   full((N_GROUPS, CHUNK, CHUNK)), full((CHUNK, c))],
        out_shape=[jax.ShapeDtypeStruct((s, 2 * c), BF16), jax.ShapeDtypeStruct((1, c), F32),
                   jax.ShapeDtypeStruct((1, c), F32), jax.ShapeDtypeStruct((N_GROUPS, CHUNK, CHUNK), F32),
                   jax.ShapeDtypeStruct((CHUNK, c), F32)],
        scratch_shapes=[pltpu.VMEM((8, c), F32), pltpu.VMEM((8, c), F32)],
        compiler_params=_params(("arbitrary",)),
    )(h, h, ln_g, ln_b, w_s, bias, dy)


def _merge_fwd(h, acts, ws, b_gate, name):
    s = h.shape[0]
    d = D_MODEL
    t = _tile(s, 512)

    def body(gl0, gl1, gl2, a0, a1, a2, w0, w1, w2, b_ref, o_ref):
        acc = jnp.zeros((t, d), F32)
        for i, (gl, a, w) in enumerate(((gl0, a0, w0), (gl1, a1, w1), (gl2, a2, w2))):
            y = lax.dot_general(a[...], w[...], _DIMS["nn"], preferred_element_type=F32)
            acc = acc + _sigmoid(gl[...].astype(F32) + b_ref[i:i + 1, :]) * y
        o_ref[...] = acc.astype(o_ref.dtype)

    full = lambda arr: pl.BlockSpec(arr.shape, lambda i: (0, 0))
    return pl.pallas_call(
        body, name=name, grid=(s // t,),
        in_specs=[pl.BlockSpec((t, d), lambda i, b=b: (i, b)) for b in range(3)]
                 + [pl.BlockSpec((t, a.shape[1]), lambda i: (i, 0)) for a in acts]
                 + [full(w) for w in ws] + [full(b_gate)],
        out_specs=pl.BlockSpec((t, d), lambda i: (i, 0)),
        out_shape=jax.ShapeDtypeStruct((s, d), BF16),
        compiler_params=_params(("parallel",)),
    )(h, h, h, *acts, *ws, b_gate)


def _merge_bwd(h, acts, ws, b_gate, dmerged, name):
    s = h.shape[0]
    d = D_MODEL
    t = _tile(s, 512)
    n = s // t

    def body(gl0, gl1, gl2, a0, a1, a2, w0, w1, w2, b_ref, dm_ref, dy0, dy1, dy2, dgl_ref, db_ref, acc_ref):
        step = pl.program_id(0)

        @pl.when(step == 0)
        def _():
            acc_ref[...] = jnp.zeros_like(acc_ref)

        dm = dm_ref[...]
        for i, (gl, a, w, dy) in enumerate(((gl0, a0, w0, dy0), (gl1, a1, w1, dy1), (gl2, a2, w2, dy2))):
            y = lax.dot_general(a[...], w[...], _DIMS["nn"], preferred_element_type=F32)
            gate = _sigmoid(gl[...].astype(F32) + b_ref[i:i + 1, :])
            dy[...] = (dm * gate).astype(dy.dtype)
            dgl = dm * y * (gate * (1.0 - gate))
            dgl_ref[:, i * d:(i + 1) * d] = dgl.astype(dgl_ref.dtype)
            acc_ref[i] += _row_sum8(dgl)

        @pl.when(step == n - 1)
        def _():
            rows = [jnp.sum(acc_ref[k], axis=0, keepdims=True) for k in range(3)]
            db_ref[...] = jnp.concatenate(rows + [jnp.zeros((5, d), F32)], axis=0)

    full = lambda arr: pl.BlockSpec(arr.shape, lambda i: (0, 0))
    row = pl.BlockSpec((t, d), lambda i: (i, 0))
    return pl.pallas_call(
        body, name=name, grid=(n,),
        in_specs=[pl.BlockSpec((t, d), lambda i, b=b: (i, b)) for b in range(3)]
                 + [pl.BlockSpec((t, a.shape[1]), lambda i: (i, 0)) for a in acts]
                 + [full(w) for w in ws] + [full(b_gate), row],
        out_specs=[row, row, row, pl.BlockSpec((t, 3 * d), lambda i: (i, 0)), pl.BlockSpec((8, d), lambda i: (0, 0))],
        out_shape=[jax.ShapeDtypeStruct((s, d), BF16)] * 3
                  + [jax.ShapeDtypeStruct((s, 3 * d), BF16), jax.ShapeDtypeStruct((8, d), F32)],
        scratch_shapes=[pltpu.VMEM((3, 8, d), F32)],
        compiler_params=_params(("arbitrary",)),
    )(h, h, h, *acts, *ws, b_gate, dmerged)


FF_BLK = D_FF // 2


def _ffn_act_fwd(h2, w, name):
    s = h2.shape[0]
    t = _tile(s, 512)
    r = t // HALO
    cw = 2 * FF_BLK

    def body(x_ref, xp_ref, w_ref, p_ref):
        i = pl.program_id(0)
        live = (i > 0).astype(F32)
        hc = _conv3(x_ref[...].astype(F32), xp_ref[...].astype(F32) * live, w_ref)
        p_ref[...] = (_gelu(hc[:, :FF_BLK]) * hc[:, FF_BLK:]).astype(p_ref.dtype)

    return pl.pallas_call(
        body, name=name, grid=(s // t, 2),
        in_specs=[pl.BlockSpec((t, cw), lambda i, j: (i, j)),
                  pl.BlockSpec((HALO, cw), lambda i, j: (jnp.maximum(i * r - 1, 0), j)),
                  pl.BlockSpec((8, cw), lambda i, j: (0, j))],
        out_specs=pl.BlockSpec((t, FF_BLK), lambda i, j: (i, j)),
        out_shape=jax.ShapeDtypeStruct((s, D_FF), BF16),
        compiler_params=_params(("parallel", "parallel")),
    )(h2, h2, w)


def _ffn_act_bwd(h2, w, dp, name):
    s = h2.shape[0]
    t = _tile(s, 512)
    r = t // HALO
    cw = 2 * FF_BLK

    def body(x_ref, xp_ref, w_ref, dp_ref, d_ref):
        i = pl.program_id(0)
        live = (i > 0).astype(F32)
        hc = _conv3(x_ref[...].astype(F32), xp_ref[...].astype(F32) * live, w_ref)
        ga, dga = _gelu_and_grad(hc[:, :FF_BLK])
        dpv = dp_ref[...].astype(F32)
        d_ref[:, :FF_BLK] = (dpv * hc[:, FF_BLK:] * dga).astype(d_ref.dtype)
        d_ref[:, FF_BLK:] = (dpv * ga).astype(d_ref.dtype)

    return pl.pallas_call(
        body, name=name, grid=(s // t, 2),
        in_specs=[pl.BlockSpec((t, cw), lambda i, j: (i, j)),
                  pl.BlockSpec((HALO, cw), lambda i, j: (jnp.maximum(i * r - 1, 0), j)),
                  pl.BlockSpec((8, cw), lambda i, j: (0, j)),
                  pl.BlockSpec((t, FF_BLK), lambda i, j: (i, j))],
        out_specs=pl.BlockSpec((t, cw), lambda i, j: (i, j)),
        out_shape=jax.ShapeDtypeStruct((s, 2 * D_FF), BF16),
        compiler_params=_params(("parallel", "parallel")),
    )(h2, h2, w, dp)


def _dwconv_bwd(x, w, dy, name):
    s, c = x.shape
    t = _tile(s, 512)
    n = s // t
    r = t // HALO
    nh = s // HALO
    cw = FF_BLK
    nc = c // cw

    def body(x_ref, xp_ref, dy_ref, dyn_ref, w_ref, dx_ref, dw_ref, acc_ref):
        i = pl.program_id(1)
        has_prev = (i > 0).astype(F32)
        has_next = (i < n - 1).astype(F32)
        xv = x_ref[...].astype(F32)
        xp = xp_ref[...].astype(F32) * has_prev
        dyv = dy_ref[...].astype(F32)
        dyn = dyn_ref[...].astype(F32) * has_next
        dx = (w_ref[2:3, :] * dyv + w_ref[1:2, :] * _shift_up(dyv, dyn, 1)
              + w_ref[0:1, :] * _shift_up(dyv, dyn, 2))
        dx_ref[...] = dx.astype(dx_ref.dtype)

        @pl.when(i == 0)
        def _():
            acc_ref[...] = jnp.zeros_like(acc_ref)

        acc_ref[0] += _row_sum8(dyv * _shift_down(xv, xp, 2))
        acc_ref[1] += _row_sum8(dyv * _shift_down(xv, xp, 1))
        acc_ref[2] += _row_sum8(dyv * xv)

        @pl.when(i == n - 1)
        def _():
            rows = [jnp.sum(acc_ref[k], axis=0, keepdims=True) for k in range(3)]
            dw_ref[...] = jnp.concatenate(rows + [jnp.zeros((5, cw), F32)], axis=0)

    return pl.pallas_call(
        body, name=name, grid=(nc, n),
        in_specs=[pl.BlockSpec((t, cw), lambda j, i: (i, j)),
                  pl.BlockSpec((HALO, cw), lambda j, i: (jnp.maximum(i * r - 1, 0), j)),
                  pl.BlockSpec((t, cw), lambda j, i: (i, j)),
                  pl.BlockSpec((HALO, cw), lambda j, i: (jnp.minimum((i + 1) * r, nh - 1), j)),
                  pl.BlockSpec((8, cw), lambda j, i: (0, j))],
        out_specs=[pl.BlockSpec((t, cw), lambda j, i: (i, j)), pl.BlockSpec((8, cw), lambda j, i: (0, j))],
        out_shape=[jax.ShapeDtypeStruct((s, c), BF16), jax.ShapeDtypeStruct((8, c), F32)],
        scratch_shapes=[pltpu.VMEM((3, 8, cw), F32)],
        compiler_params=_params(("parallel", "arbitrary")),
    )(x, x, dy, dy, w)


def _adamw(w, g, m, v, name):
    shape = w.shape
    c = shape[-1]
    rows = math.prod(shape[:-1])
    to2d = lambda a: a.reshape(rows, c)
    cap = max(8, (1 << 18) // c)
    tr = rows
    for cand in (2048, 1024, 512, 256, 128, 64, 32, 16, 8):
        if cand <= cap and rows % cand == 0:
            tr = cand
            break

    def body(w_ref, g_ref, m_ref, v_ref, d_ref, nm_ref, nv_ref):
        gv = g_ref[...]
        nm = ADAM_B1 * m_ref[...] + (1.0 - ADAM_B1) * gv
        nv = ADAM_B2 * v_ref[...] + (1.0 - ADAM_B2) * (gv * gv)
        m_hat = nm / (1.0 - ADAM_B1 ** ADAM_STEP)
        v_hat = nv / (1.0 - ADAM_B2 ** ADAM_STEP)
        d_ref[...] = -ADAM_LR * (m_hat / (jnp.sqrt(v_hat) + ADAM_EPS) + ADAM_WD * w_ref[...])
        nm_ref[...] = nm
        nv_ref[...] = nv

    blk = pl.BlockSpec((tr, c), lambda i: (i, 0))
    outs = pl.pallas_call(
        body, name=name, grid=(rows // tr,),
        in_specs=[blk] * 4, out_specs=[blk] * 3,
        out_shape=[jax.ShapeDtypeStruct((rows, c), F32)] * 3,
        compiler_params=_params(("parallel",)),
    )(to2d(w), to2d(g), to2d(m), to2d(v))
    return tuple(o.reshape(shape) for o in outs)


_ANY = pl.BlockSpec(memory_space=pl.ANY)


def _place():
    x, y, c = lax.axis_index("x"), lax.axis_index("y"), lax.axis_index("c")
    others = [(1 - x, y), (x, 1 - y), (1 - x, 1 - y)]
    return x, y, c, others


def _all_gather_chips(shard, name):
    rws, cols = shard.shape
    half = rws // 2

    def body(x_ref, out_ref, send_sems, recv_sems, local_sem):
        x, y, c, others = _place()
        me = 2 * x + y
        sib = (x, y, 1 - c)

        def rows(chip, cc):
            return out_ref.at[chip, pl.ds(pl.multiple_of(cc * half, 16), half), :]

        def copy(k, src, dst, to):
            return pltpu.make_async_remote_copy(src_ref=src, dst_ref=dst, send_sem=send_sems.at[k],
                                                recv_sem=recv_sems.at[k], device_id=to, device_id_type=MESH)

        mine = pltpu.make_async_copy(x_ref, out_ref.at[me], local_sem)
        mine.start()
        my_half = x_ref.at[pl.ds(pl.multiple_of(c * half, 16), half), :]
        first = [copy(j, my_half, rows(me, c), (ox, oy, c)) for j, (ox, oy) in enumerate(others)]
        for cp in first:
            cp.start()
        passed = []
        for j, (ox, oy) in enumerate(others):
            blk = rows(2 * ox + oy, c)
            copy(j, blk, blk, (x, y, c)).wait_recv()
            fwd = copy(3 + j, blk, blk, sib)
            fwd.start()
            passed.append(fwd)
        for j, (ox, oy) in enumerate(others):
            blk = rows(2 * ox + oy, 1 - c)
            copy(3 + j, blk, blk, (x, y, c)).wait_recv()
        for cp in first + passed:
            cp.wait_send()
        mine.wait()

    return pl.pallas_call(
        body, name=name,
        in_specs=[_ANY], out_specs=_ANY,
        out_shape=jax.ShapeDtypeStruct((N_CHIPS, rws, cols), shard.dtype),
        scratch_shapes=[pltpu.SemaphoreType.DMA((6,)), pltpu.SemaphoreType.DMA((6,)), pltpu.SemaphoreType.DMA],
        compiler_params=pltpu.CompilerParams(has_side_effects=True),
    )(shard)


def _swap_halves(buf, name):
    nb, rws, cols = buf.shape
    half = rws // 2

    def body(b_ref, own_ref, sib_ref, send_sem, recv_sem, local_sem):
        x, y, c, _ = _place()
        keep = b_ref.at[:, pl.ds(pl.multiple_of(c * half, 16), half), :]
        give = b_ref.at[:, pl.ds(pl.multiple_of((1 - c) * half, 16), half), :]
        mine = pltpu.make_async_copy(keep, own_ref, local_sem)
        mine.start()
        cp = pltpu.make_async_remote_copy(src_ref=give, dst_ref=sib_ref, send_sem=send_sem, recv_sem=recv_sem,
                                          device_id=(x, y, 1 - c), device_id_type=MESH)
        cp.start()
        cp.wait()
        mine.wait()

    shp = jax.ShapeDtypeStruct((nb, half, cols), buf.dtype)
    return pl.pallas_call(
        body, name=name,
        in_specs=[_ANY], out_specs=[_ANY, _ANY], out_shape=[shp, shp],
        scratch_shapes=[pltpu.SemaphoreType.DMA, pltpu.SemaphoreType.DMA, pltpu.SemaphoreType.DMA],
        compiler_params=pltpu.CompilerParams(has_side_effects=True),
    )(buf)


def _add2(a, b, name):
    nb, rws, cols = a.shape
    t = _tile(rws, 256)
    if rws % t:
        t = rws

    def body(a_ref, b_ref, o_ref):
        o_ref[...] = (a_ref[...].astype(F32) + b_ref[...].astype(F32)).astype(o_ref.dtype)

    blk = pl.BlockSpec((1, t, cols), lambda i, j: (i, j, 0))
    return pl.pallas_call(
        body, name=name, grid=(nb, rws // t), in_specs=[blk, blk], out_specs=blk,
        out_shape=jax.ShapeDtypeStruct(a.shape, a.dtype),
        compiler_params=_params(("parallel", "parallel")),
    )(a, b)


def _exchange_chips(pre, name):
    nb, half, cols = pre.shape

    def body(p_ref, out_ref, send_sems, recv_sems, local_sem):
        x, y, c, others = _place()
        me = 2 * x + y
        mine = pltpu.make_async_copy(p_ref.at[me], out_ref.at[me], local_sem)
        mine.start()
        sends = []
        for j, (ox, oy) in enumerate(others):
            cp = pltpu.make_async_remote_copy(src_ref=p_ref.at[2 * ox + oy], dst_ref=out_ref.at[me],
                                              send_sem=send_sems.at[j], recv_sem=recv_sems.at[j],
                                              device_id=(ox, oy, c), device_id_type=MESH)
            cp.start()
            sends.append(cp)
        for j, (ox, oy) in enumerate(others):
            blk = out_ref.at[2 * ox + oy]
            pltpu.make_async_remote_copy(src_ref=blk, dst_ref=blk, send_sem=send_sems.at[j],
                                         recv_sem=recv_sems.at[j], device_id=(x, y, c),
                                         device_id_type=MESH).wait_recv()
        for cp in sends:
            cp.wait_send()
        mine.wait()

    return pl.pallas_call(
        body, name=name,
        in_specs=[_ANY], out_specs=_ANY, out_shape=jax.ShapeDtypeStruct(pre.shape, pre.dtype),
        scratch_shapes=[pltpu.SemaphoreType.DMA((3,)), pltpu.SemaphoreType.DMA((3,)), pltpu.SemaphoreType.DMA],
        compiler_params=pltpu.CompilerParams(has_side_effects=True),
    )(pre)


def _add4(parts, name):
    nb, half, cols = parts.shape
    t = _tile(half, 256)
    if half % t:
        t = half

    def body(p_ref, o_ref):
        acc = p_ref[0].astype(F32)
        for k in range(1, nb):
            acc = acc + p_ref[k].astype(F32)
        o_ref[...] = acc

    return pl.pallas_call(
        body, name=name, grid=(half // t,),
        in_specs=[pl.BlockSpec((nb, t, cols), lambda i: (0, i, 0))],
        out_specs=pl.BlockSpec((t, cols), lambda i: (i, 0)),
        out_shape=jax.ShapeDtypeStruct((half, cols), F32),
        compiler_params=_params(("parallel",)),
    )(parts)


def _join_halves(mine_half, name):
    half, cols = mine_half.shape

    def body(h_ref, out_ref, send_sem, recv_sem, local_sem):
        x, y, c, _ = _place()
        dst = out_ref.at[pl.ds(pl.multiple_of(c * half, 8), half), :]
        mine = pltpu.make_async_copy(h_ref, dst, local_sem)
        mine.start()
        cp = pltpu.make_async_remote_copy(src_ref=h_ref, dst_ref=dst, send_sem=send_sem, recv_sem=recv_sem,
                                          device_id=(x, y, 1 - c), device_id_type=MESH)
        cp.start()
        cp.wait()
        mine.wait()

    return pl.pallas_call(
        body, name=name,
        in_specs=[_ANY], out_specs=_ANY, out_shape=jax.ShapeDtypeStruct((2 * half, cols), mine_half.dtype),
        scratch_shapes=[pltpu.SemaphoreType.DMA, pltpu.SemaphoreType.DMA, pltpu.SemaphoreType.DMA],
        compiler_params=pltpu.CompilerParams(has_side_effects=True),
    )(mine_half)


def _reduce_scatter_chips(buf, tag):
    own, sib = _swap_halves(buf, "rs_swap_" + tag)
    pre = _add2(own, sib, "rs_add2_" + tag)
    parts = _exchange_chips(pre, "rs_xchg_" + tag)
    red = _add4(parts, "rs_add4_" + tag)
    return _join_halves(red, "rs_join_" + tag)


MAX_DMA_BYTES = 2 * 1024 * 1024
ROW_ALIGN = 16


def _pieces(rows, row_bytes):
    n = max(1, -(-(rows * row_bytes) // MAX_DMA_BYTES))
    step = -(-(-(-rows // n)) // ROW_ALIGN) * ROW_ALIGN
    return [(r, min(step, rows - r)) for r in range(0, rows, step)]


def _half_plan(arrays, row_axis):
    plan = []
    for a, arr in enumerate(arrays):
        row_bytes = math.prod(arr.shape[row_axis + 1:]) * arr.dtype.itemsize * (arr.shape[0] if row_axis else 1)
        plan += [(a, r0, nr) for r0, nr in _pieces(arr.shape[row_axis] // 2, row_bytes)]
    return plan


def _rows(start, size):
    return pl.ds(pl.multiple_of(start, ROW_ALIGN), size)


def _remote(src, dst, send_sems, recv_sems, k, to):
    return pltpu.make_async_remote_copy(src_ref=src, dst_ref=dst, send_sem=send_sems.at[k], recv_sem=recv_sems.at[k],
                                        device_id=to, device_id_type=MESH)


def _comm_call(body, name, ins, out_shapes, n_remote, n_local, aliases=None):
    return pl.pallas_call(
        body, name=name,
        in_specs=[_ANY] * len(ins), out_specs=[_ANY] * len(out_shapes), out_shape=out_shapes,
        scratch_shapes=[pltpu.SemaphoreType.DMA((n_remote,)), pltpu.SemaphoreType.DMA((n_remote,)),
                        pltpu.SemaphoreType.DMA((max(n_local, 1),))],
        input_output_aliases=aliases or {},
        compiler_params=pltpu.CompilerParams(has_side_effects=True),
    )(*ins)


def _cast_shard(w, l, me_idx, name):
    _, k, cols = w.shape
    tr = _tile(k, 256)
    if k % tr:
        tr = k

    def body(me_ref, w_ref, s_ref, land_ref):
        del me_ref
        v = w_ref[...].astype(BF16)
        s_ref[...] = v
        land_ref[...] = v

    grid_spec = pltpu.PrefetchScalarGridSpec(
        num_scalar_prefetch=1, grid=(k // tr,),
        in_specs=[pl.BlockSpec((None, tr, cols), lambda i, me: (l, i, 0))],
        out_specs=[pl.BlockSpec((tr, cols), lambda i, me: (i, 0)),
                   pl.BlockSpec((None, tr, cols), lambda i, me: (me[0], i, 0))])
    return pl.pallas_call(
        body, name=name, grid_spec=grid_spec,
        out_shape=[jax.ShapeDtypeStruct((k, cols), BF16), jax.ShapeDtypeStruct((N_CHIPS, k, cols), BF16)],
        compiler_params=_params(("parallel",)),
    )(me_idx, w)


def _gather_d2d(lands, name):
    n = len(lands)
    plan = _half_plan(lands, 1)
    plan = [(a, r0, nr) for a, r0, nr in plan]

    def body(*refs):
        out_refs = refs[n:2 * n]
        send_sems, recv_sems, _ = refs[2 * n:]
        x, y, c, others = _place()
        sends = []
        for i, (a, r0, nr) in enumerate(plan):
            rows = _rows(c * (lands[a].shape[1] // 2) + r0, nr)
            for j, (ox, oy) in enumerate(others):
                blk = out_refs[a].at[2 * ox + oy, rows, :]
                cp = _remote(blk, blk, send_sems, recv_sems, 3 * i + j, (x, y, 1 - c))
                cp.start()
                sends.append(cp)
        for i, (a, r0, nr) in enumerate(plan):
            rows = _rows((1 - c) * (lands[a].shape[1] // 2) + r0, nr)
            for j, (ox, oy) in enumerate(others):
                blk = out_refs[a].at[2 * ox + oy, rows, :]
                _remote(blk, blk, send_sems, recv_sems, 3 * i + j, (x, y, c)).wait_recv()
        for cp in sends:
            cp.wait_send()

    outs = [jax.ShapeDtypeStruct(a.shape, a.dtype) for a in lands]
    return _comm_call(body, name, lands, outs, 3 * len(plan), 0, aliases={a: a for a in range(n)})


def _rs_swap(ts, name):
    n = len(ts)
    plan = _half_plan(ts, 1)

    def body(*refs):
        t_refs, out_refs = refs[:n], refs[n:2 * n]
        send_sems, recv_sems, _ = refs[2 * n:]
        x, y, c, _o = _place()
        sends = []
        for i, (a, r0, nr) in enumerate(plan):
            src = t_refs[a].at[:, _rows((1 - c) * (ts[a].shape[1] // 2) + r0, nr), :]
            cp = _remote(src, out_refs[a].at[:, pl.ds(r0, nr), :], send_sems, recv_sems, i, (x, y, 1 - c))
            cp.start()
            sends.append(cp)
        for i, (a, r0, nr) in enumerate(plan):
            blk = out_refs[a].at[:, pl.ds(r0, nr), :]
            _remote(blk, blk, send_sems, recv_sems, i, (x, y, c)).wait_recv()
        for cp in sends:
            cp.wait_send()

    outs = [jax.ShapeDtypeStruct((t.shape[0], t.shape[1] // 2, t.shape[2]), t.dtype) for t in ts]
    return _comm_call(body, name, ts, outs, len(plan), 0)


def _add_half(t, got, c_idx, me_idx, name):
    nb, k, cols = t.shape
    half = k // 2

    def body(c_ref, me_ref, t_ref, g_ref, o_ref, mine_ref):
        del c_ref
        v = (t_ref[...].astype(F32) + g_ref[...].astype(F32)).astype(o_ref.dtype)
        o_ref[...] = v

        @pl.when(pl.program_id(0) == me_ref[0])
        def _():
            mine_ref[...] = v

    blk = pl.BlockSpec((1, half, cols), lambda i, c, me: (i, 0, 0))
    grid_spec = pltpu.PrefetchScalarGridSpec(
        num_scalar_prefetch=2, grid=(nb,),
        in_specs=[pl.BlockSpec((1, half, cols), lambda i, c, me: (i, c[0], 0)), blk],
        out_specs=[blk, pl.BlockSpec((1, half, cols), lambda i, c, me: (me[0], 0, 0))])
    shp = jax.ShapeDtypeStruct(got.shape, got.dtype)
    return pl.pallas_call(
        body, name=name, grid_spec=grid_spec, out_shape=[shp, shp],
        compiler_params=_params(("arbitrary",)),
    )(c_idx, me_idx, t, got)


def _add4_half(parts, c_idx, name):
    nb, half, cols = parts.shape
    t = _tile(half, 256)
    if half % t:
        t = half
    steps = half // t

    def body(c_ref, p_ref, o_ref):
        del c_ref
        acc = p_ref[0].astype(F32)
        for k in range(1, nb):
            acc = acc + p_ref[k].astype(F32)
        o_ref[...] = acc

    grid_spec = pltpu.PrefetchScalarGridSpec(
        num_scalar_prefetch=1, grid=(steps,),
        in_specs=[pl.BlockSpec((nb, t, cols), lambda i, c: (0, i, 0))],
        out_specs=pl.BlockSpec((t, cols), lambda i, c: (c[0] * steps + i, 0)))
    return pl.pallas_call(
        body, name=name, grid_spec=grid_spec, out_shape=jax.ShapeDtypeStruct((2 * half, cols), F32),
        compiler_params=_params(("parallel",)),
    )(c_idx, parts)


def _rs_join(fulls, name):
    n = len(fulls)
    plan = _half_plan(fulls, 0)

    def body(*refs):
        out_refs = refs[n:2 * n]
        send_sems, recv_sems, _ = refs[2 * n:]
        x, y, c, _o = _place()
        sends = []
        for i, (a, r0, nr) in enumerate(plan):
            blk = out_refs[a].at[_rows(c * (fulls[a].shape[0] // 2) + r0, nr), :]
            cp = _remote(blk, blk, send_sems, recv_sems, i, (x, y, 1 - c))
            cp.start()
            sends.append(cp)
        for i, (a, r0, nr) in enumerate(plan):
            blk = out_refs[a].at[_rows((1 - c) * (fulls[a].shape[0] // 2) + r0, nr), :]
            _remote(blk, blk, send_sems, recv_sems, i, (x, y, c)).wait_recv()
        for cp in sends:
            cp.wait_send()

    outs = [jax.ShapeDtypeStruct(f.shape, f.dtype) for f in fulls]
    return _comm_call(body, name, fulls, outs, len(plan), 0, aliases={a: a for a in range(n)})


_HBM = pl.BlockSpec(memory_space=pltpu.HBM)
_SEM = pl.BlockSpec(memory_space=pltpu.SEMAPHORE)
_EFFECT = pltpu.SideEffectType.DATAFLOW_SIDE_EFFECTING


def _ici_plan(kind, a_list):
    if kind == "gather":
        return _half_plan(a_list, 0)
    plan = []
    for a, p in enumerate(a_list):
        plan += [(a, r0, nr) for r0, nr in _pieces(p.shape[1], p.shape[2] * p.dtype.itemsize)]
    return plan


def _ici_refs(kind, a_ref, b_ref, a_shape, r0, nr, c, me, peer):
    if kind == "gather":
        rows = _rows(c * (a_shape[0] // 2) + r0, nr)
        return a_ref.at[rows, :], b_ref.at[me, rows, :], b_ref.at[peer, rows, :]
    rows = pl.ds(r0, nr)
    return a_ref.at[peer, rows, :], b_ref.at[me, rows, :], b_ref.at[peer, rows, :]


def _ici_start(kind, a_list, b_list, name):
    n = len(a_list)
    plan = _ici_plan(kind, a_list)
    shapes = [a.shape for a in a_list]

    def body(*refs):
        a_refs, b_refs = refs[:n], refs[n:2 * n]
        send_sems, recv_sems = refs[2 * n], refs[2 * n + 1]
        token = refs[4 * n + 2]
        x, y, c, others = _place()
        me = 2 * x + y
        for i, (a, r0, nr) in enumerate(plan):
            for j, (ox, oy) in enumerate(others):
                src, dst, _ = _ici_refs(kind, a_refs[a], b_refs[a], shapes[a], r0, nr, c, me, 2 * ox + oy)
                _remote(src, dst, send_sems, recv_sems, 3 * i + j, (ox, oy, c)).start()
        token[...] = jnp.zeros_like(token)

    hbm = lambda v: pltpu.HBM(v.shape, v.dtype)
    ncp = 3 * len(plan)
    outs = pl.pallas_call(
        body, name=name,
        in_specs=[_HBM] * (2 * n),
        out_specs=[_SEM, _SEM] + [_HBM] * (2 * n) + [pl.BlockSpec(memory_space=pltpu.VMEM)],
        out_shape=[pltpu.SemaphoreType.DMA((ncp,)), pltpu.SemaphoreType.DMA((ncp,))]
                  + [hbm(v) for v in a_list] + [hbm(v) for v in b_list] + [jax.ShapeDtypeStruct((8, LANES), F32)],
        input_output_aliases={i: 2 + i for i in range(2 * n)},
        compiler_params=pltpu.CompilerParams(has_side_effects=_EFFECT),
    )(*[pltpu.with_memory_space_constraint(v, pltpu.HBM) for v in list(a_list) + list(b_list)])
    return outs[0], outs[1], outs[2:2 + n], outs[2 + n:2 + 2 * n], outs[2 + 2 * n]


def _ici_wait(kind, started, after, name):
    send_sems, recv_sems, a_list, b_list, _ = started
    n = len(a_list)
    plan = _ici_plan(kind, a_list)
    shapes = [a.shape for a in a_list]

    def body(*refs):
        a_refs, b_refs = refs[:n], refs[n:2 * n]
        send_sems, recv_sems = refs[2 * n], refs[2 * n + 1]
        x, y, c, others = _place()
        me = 2 * x + y
        for i, (a, r0, nr) in enumerate(plan):
            for j, (ox, oy) in enumerate(others):
                src, dst, land = _ici_refs(kind, a_refs[a], b_refs[a], shapes[a], r0, nr, c, me, 2 * ox + oy)
                _remote(src, dst, send_sems, recv_sems, 3 * i + j, (ox, oy, c)).wait_send()
                _remote(land, land, send_sems, recv_sems, 3 * i + j, (x, y, c)).wait_recv()

    hbm = lambda v: pltpu.HBM(v.shape, v.dtype)
    outs = pl.pallas_call(
        body, name=name,
        in_specs=[_HBM] * (2 * n) + [_SEM, _SEM, _ANY],
        out_specs=[_HBM] * (2 * n),
        out_shape=[hbm(v) for v in a_list] + [hbm(v) for v in b_list],
        input_output_aliases={i: i for i in range(2 * n)},
        compiler_params=pltpu.CompilerParams(has_side_effects=_EFFECT),
    )(*a_list, *b_list, send_sems, recv_sems, after)
    return outs[n:]


def _rs_begin(ts, c_idx, me_idx, tag):
    got = _rs_swap(ts, "rs_swap_" + tag)
    pairs = [_add_half(t, g, c_idx, me_idx, f"rs_add2_{tag}_{a}") for a, (t, g) in enumerate(zip(ts, got))]
    return _ici_start("scatter", [p for p, _ in pairs], [m for _, m in pairs], "rs_xchg_start_" + tag)


def _rs_finish(started, after, c_idx, tag):
    parts = _ici_wait("scatter", started, after, "rs_xchg_wait_" + tag)
    fulls = [_add4_half(p, c_idx, f"rs_add4_{tag}_{a}") for a, p in enumerate(parts)]
    return _rs_join(fulls, "rs_join_" + tag)


def _pack_rows(pieces, rows, dtype):
    flat = jnp.concatenate([p.astype(dtype).reshape(-1) for p in pieces])
    return jnp.pad(flat, (0, rows * PACK_COLS - flat.shape[0])).reshape(rows, PACK_COLS)


def _unpack(flat, shapes):
    out, off = [], 0
    for shp in shapes:
        size = math.prod(shp)
        out.append(flat[off:off + size].reshape(shp))
        off += size
    return out


def _rows_for(n_elems, mult):
    rows = -(-n_elems // PACK_COLS)
    return -(-rows // mult) * mult


BIG_SHARDS = [("w_in", (D_MODEL, 1474)), ("w_branch_att", (D_ATT, 256)), ("w_branch_conv", (D_CONV, 256)),
              ("w_branch_sgu", (D_SGU, 256)), ("w_out", (256, D_MODEL)), ("w_ffn_up", (D_MODEL, FF_BLK)),
              ("w_ffn_down", (D_FF // N_CHIPS, D_MODEL))]
SMALL_SHARDS = [("b_gate", (3, 256)), ("conv_mix_w", (3, 64)), ("conv_ffn_w", (3, FF_BLK))]
REPLICATED = [("pre_mix_g", (D_MODEL,)), ("post_mix_g", (D_MODEL,)), ("pre_ffn_g", (D_MODEL,)),
              ("post_ffn_g", (D_MODEL,)), ("b_forget", (N_HEADS,)), ("sgu_ln_g", (D_SGU,)), ("sgu_ln_b", (D_SGU,)),
              ("sgu_w", (N_GROUPS, CHUNK, CHUNK)), ("sgu_b", (N_GROUPS, CHUNK))]
WEIGHT_ORDER = ["pre_mix_g", "post_mix_g", "pre_ffn_g", "post_ffn_g", "w_in", "b_forget", "b_gate", "conv_mix_w",
                "sgu_ln_g", "sgu_ln_b", "sgu_w", "sgu_b", "w_branch_att", "w_branch_conv", "w_branch_sgu", "w_out",
                "w_ffn_up", "conv_ffn_w", "w_ffn_down"]

_SMALL_ELEMS = sum(math.prod(s) for _, s in SMALL_SHARDS)
_REP_ELEMS = sum(math.prod(s) for _, s in REPLICATED)
_REP_QUARTER = -(-(DEPTH * _REP_ELEMS) // N_CHIPS)
SMALL_PARAM_ROWS = _rows_for(DEPTH * _SMALL_ELEMS, 32)
SMALL_ROWS = _rows_for(DEPTH * _SMALL_ELEMS + _REP_QUARTER, 32)
IN_WIDTH = 5896
IN_SHARD = IN_WIDTH // N_CHIPS
IN_SHARD_PAD = 1536
IN_PAD = 6144


def _gather_small(wts):
    shard = _pack_rows([wts[n] for n, _ in SMALL_SHARDS], SMALL_PARAM_ROWS, F32)
    full = _all_gather_chips(shard, "gather_small_params").reshape(N_CHIPS, -1)
    per_chip = [_unpack(full[j], [(DEPTH,) + s for _, s in SMALL_SHARDS]) for j in range(N_CHIPS)]
    return {n: jnp.concatenate([per_chip[j][i] for j in range(N_CHIPS)], axis=-1)
            for i, (n, _) in enumerate(SMALL_SHARDS)}


def _gather_begin(wts, l, me_idx):
    cast = [_cast_shard(wts[n], l, me_idx, "cast_" + n) for n, _ in BIG_SHARDS]
    return _ici_start("gather", [sh for sh, _ in cast], [ld for _, ld in cast], "gather_ici_start")


def _gather_finish(started, after):
    lands = _ici_wait("gather", started, after, "gather_ici_wait")
    return dict(zip([n for n, _ in BIG_SHARDS], _gather_d2d(lands, "gather_d2d")))


def _pad_rows(a, rows):
    return jnp.pad(a, ((0, rows - a.shape[0]), (0, 0)))


def _whole_cols(land):
    return land.transpose(1, 0, 2).reshape(land.shape[1], -1)


_O_F = 3 * D_ATT
_O_B = _O_F + N_HEADS
_O_GL = _O_B + 3 * D_CONV + 2 * D_SGU


def _prep_layer(wts, lands, small, l):
    w_in = _whole_cols(lands["w_in"])
    up = lands["w_ffn_up"]
    cf = small["conv_ffn_w"][l]
    blk = lambda a, j: a[:, j * FF_BLK:(j + 1) * FF_BLK]
    return {
        "w_p": jnp.concatenate([w_in[:, _O_GL:], w_in[:, :_O_F], w_in[:, _O_B:_O_GL]], axis=1),
        "w_in_pad": jnp.pad(w_in, ((0, 0), (0, IN_PAD - IN_WIDTH))),
        "wf_t": _pad_rows(w_in[:, _O_F:_O_B].T, F_ROWS),
        "b_forget": _pad_rows(wts["b_forget"][l].reshape(N_HEADS, 1), F_ROWS),
        "b_gate": _pad_rows(small["b_gate"][l], 8),
        "conv_mix_w": _pad_rows(small["conv_mix_w"][l], 8),
        "w_att": _whole_cols(lands["w_branch_att"]), "w_conv": _whole_cols(lands["w_branch_conv"]),
        "w_sgu": _whole_cols(lands["w_branch_sgu"]),
        "w_out": lands["w_out"].reshape(D_MODEL, D_MODEL),
        "w_up": jnp.concatenate([up[0], up[2], up[1], up[3]], axis=1),
        "conv_ffn_w": _pad_rows(jnp.concatenate([blk(cf, 0), blk(cf, 2), blk(cf, 1), blk(cf, 3)], axis=1), 8),
        "w_down": lands["w_ffn_down"].reshape(D_FF, D_MODEL),
        "pre_mix_g": wts["pre_mix_g"][l].reshape(1, -1), "post_mix_g": wts["post_mix_g"][l].reshape(1, -1),
        "pre_ffn_g": wts["pre_ffn_g"][l].reshape(1, -1), "post_ffn_g": wts["post_ffn_g"][l].reshape(1, -1),
        "ln_g": wts["sgu_ln_g"][l].reshape(1, -1), "ln_b": wts["sgu_ln_b"][l].reshape(1, -1),
        "sgu_w": wts["sgu_w"][l],
        "sgu_bias": jnp.repeat(wts["sgu_b"][l].T, HEAD_DIM, axis=1),
    }


def _layer_fwd(x, p, dep=None):
    s = x.shape[0]
    xn = _rms_fwd(x, p["pre_mix_g"], "rms_pre_mix", dep)
    h = _mm(xn, p["w_p"], "nn", BF16, "mm_in", s, 256, D_MODEL)
    f_row = _mm(p["wf_t"], xn, "nt", F32, "mm_forget", F_ROWS, 2048, D_MODEL)
    c = _gate_fwd(f_row, p["b_forget"], "gate_fwd")
    o, o_f32, lse = _attn_fwd(h, c, "attn_fwd")
    yc = _sconv_fwd(h, p["conv_mix_w"], "sconv_fwd")
    ys = _sgu_fwd(h, p["ln_g"], p["ln_b"], p["sgu_w"], p["sgu_bias"], "sgu_fwd")
    merged = _merge_fwd(h, (o, yc, ys), (p["w_att"], p["w_conv"], p["w_sgu"]), p["b_gate"], "merge_fwd")
    mo = _mm(merged, p["w_out"], "nn", F32, "mm_out", 2048, 512, D_MODEL)
    x1 = _resid_post(x, mo, p["post_mix_g"], "post_mix")
    xn2 = _rms_fwd(x1, p["pre_ffn_g"], "rms_pre_ffn")
    h2 = _mm(xn2, p["w_up"], "nn", BF16, "mm_up", 2048, 512, D_MODEL)
    pact = _ffn_act_fwd(h2, p["conv_ffn_w"], "ffn_act_fwd")
    ff = _mm(pact, p["w_down"], "nn", F32, "mm_down", 2048, 512, FF_BLK)
    x2 = _resid_post(x1, ff, p["post_ffn_g"], "post_ffn")
    saved = dict(x=x, xn=xn, h=h, f_row=f_row, c=c, o=o, o_f32=o_f32, lse=lse, yc=yc, ys=ys, merged=merged, mo=mo, x1=x1,
                 xn2=xn2, h2=h2, pact=pact, ff=ff)
    return x2, saved


def _layer_bwd(dx2, p, sv, dep=None):
    s = dx2.shape[0]
    g = {}
    same = lambda b: b
    dff, g["post_ffn_g"] = _rms_bwd(sv["ff"], p["post_ffn_g"], [dx2], None, BF16, "post_ffn_bwd", dep)
    dpact = _mm(dff, p["w_down"], "nt", BF16, "mm_down_dx", 1024, FF_BLK, D_MODEL)
    t_down = _mm(sv["pact"], dff, "tn", BF16, "mm_down_dw", 256, D_MODEL, s).reshape(N_CHIPS, -1, D_MODEL)
    dhc2 = _ffn_act_bwd(sv["h2"], p["conv_ffn_w"], dpact, "ffn_act_bwd")
    dh2, dconv_ffn = _dwconv_bwd(sv["h2"], p["conv_ffn_w"], dhc2, "ffn_conv_bwd")
    dxn2 = _mm(dh2, p["w_up"], "nt", F32, "mm_up_dx", 1024, D_MODEL, FF_BLK)
    t_up = _mm(sv["xn2"], dh2, "tn", BF16, "mm_up_dw", 512, FF_BLK, s, chip_of=lambda b: (b % 2) * 2 + b // 2)
    dx1, g["pre_ffn_g"] = _rms_bwd(sv["x1"], p["pre_ffn_g"], [dxn2], dx2, F32, "pre_ffn_bwd")
    dmo, g["post_mix_g"] = _rms_bwd(sv["mo"], p["post_mix_g"], [dx1], None, BF16, "post_mix_bwd")
    dmerged = _mm(dmo, p["w_out"], "nt", F32, "mm_out_dx", 2048, 512, D_MODEL)
    t_out = _mm(sv["merged"], dmo, "tn", BF16, "mm_out_dw", 512, D_MODEL, s).reshape(N_CHIPS, -1, D_MODEL)
    acts = (sv["o"], sv["yc"], sv["ys"])
    ws = (p["w_att"], p["w_conv"], p["w_sgu"])
    dy_a, dy_c, dy_s, dgl, db_gate = _merge_bwd(sv["h"], acts, ws, p["b_gate"], dmerged, "merge_bwd")
    do = _mm(dy_a, p["w_att"], "nt", BF16, "mm_att_dx", 2048, D_ATT, D_MODEL)
    dyc = _mm(dy_c, p["w_conv"], "nt", BF16, "mm_conv_dx", 2048, D_CONV, D_MODEL)
    dys = _mm(dy_s, p["w_sgu"], "nt", BF16, "mm_sgu_dx", 2048, D_SGU, D_MODEL)
    t_att = _mm(sv["o"], dy_a, "tn", BF16, "mm_att_dw", D_ATT, 256, s, chip_of=same)
    t_conv = _mm(sv["yc"], dy_c, "tn", BF16, "mm_conv_dw", D_CONV, 256, s, chip_of=same)
    t_sgu = _mm(sv["ys"], dy_s, "tn", BF16, "mm_sgu_dw", D_SGU, 256, s, chip_of=same)
    d_conv, dconv_mix = _sconv_bwd(sv["h"], p["conv_mix_w"], dyc, "sconv_bwd")
    d_sgu, g["sgu_ln_g"], g["sgu_ln_b"], g["sgu_w"], dbias = _sgu_bwd(
        sv["h"], p["ln_g"], p["ln_b"], p["sgu_w"], p["sgu_bias"], dys, "sgu_bwd")
    dq, dk, dv, dc_even, dc_odd = _attn_bwd(sv["h"], sv["c"], sv["o_f32"], sv["lse"], do, "attn_bwd")
    df, db_forget = _gate_bwd(sv["f_row"], p["b_forget"], dc_even, dc_odd, "gate_bwd")
    dh = jnp.concatenate([dq.astype(BF16), dk, dv, df[:N_HEADS].T, d_conv, d_sgu, dgl,
                          jnp.zeros((s, IN_PAD - IN_WIDTH), BF16)], axis=1)
    dxn = _mm(dh, p["w_in_pad"], "nt", F32, "mm_in_dx", 512, D_MODEL, 2048)
    dw_in = _mm(sv["xn"], dh, "tn", F32, "mm_in_dw", D_MODEL, 512, s)
    t_in = jnp.stack([dw_in[:, j * IN_SHARD:j * IN_SHARD + IN_SHARD_PAD] for j in range(N_CHIPS)]).astype(BF16)
    dx, g["pre_mix_g"] = _rms_bwd(sv["x"], p["pre_mix_g"], [dxn], dx1, F32, "pre_mix_bwd")
    blk = lambda a, j: a[:, j * FF_BLK:(j + 1) * FF_BLK]
    g["conv_ffn_w"] = jnp.concatenate([blk(dconv_ffn, 0), blk(dconv_ffn, 2), blk(dconv_ffn, 1),
                                       blk(dconv_ffn, 3)], axis=1)[:3]
    g["conv_mix_w"] = dconv_mix[:3]
    g["b_gate"] = db_gate[:3]
    g["b_forget"] = db_forget[:N_HEADS, 0]
    g["sgu_b"] = jnp.sum(dbias.reshape(CHUNK, N_GROUPS, HEAD_DIM), axis=-1).T
    for n in ("pre_mix_g", "post_mix_g", "pre_ffn_g", "post_ffn_g", "sgu_ln_g", "sgu_ln_b"):
        g[n] = g[n].reshape(-1)
    return dx, [t_in, t_att, t_conv, t_sgu, t_out, t_up, t_down], g


def _shard_cols(a, j):
    w = a.shape[-1] // N_CHIPS
    return a[..., j * w:(j + 1) * w]


def kernel(x, pre_mix_g, post_mix_g, pre_ffn_g, post_ffn_g, w_in, b_forget, b_gate, conv_mix_w, sgu_ln_g, sgu_ln_b, sgu_w, sgu_b, w_branch_att, w_branch_conv, w_branch_sgu, w_out, w_ffn_up, conv_ffn_w, w_ffn_down, loss_target, m_pre_mix_g, m_post_mix_g, m_pre_ffn_g, m_post_ffn_g, m_w_in, m_b_forget, m_b_gate, m_conv_mix_w, m_sgu_ln_g, m_sgu_ln_b, m_sgu_w, m_sgu_b, m_w_branch_att, m_w_branch_conv, m_w_branch_sgu, m_w_out, m_w_ffn_up, m_conv_ffn_w, m_w_ffn_down, v_pre_mix_g, v_post_mix_g, v_pre_ffn_g, v_post_ffn_g, v_w_in, v_b_forget, v_b_gate, v_conv_mix_w, v_sgu_ln_g, v_sgu_ln_b, v_sgu_w, v_sgu_b, v_w_branch_att, v_w_branch_conv, v_w_branch_sgu, v_w_out, v_w_ffn_up, v_conv_ffn_w, v_w_ffn_down):
    wts = dict(pre_mix_g=pre_mix_g, post_mix_g=post_mix_g, pre_ffn_g=pre_ffn_g, post_ffn_g=post_ffn_g, w_in=w_in,
               b_forget=b_forget, b_gate=b_gate, conv_mix_w=conv_mix_w, sgu_ln_g=sgu_ln_g, sgu_ln_b=sgu_ln_b,
               sgu_w=sgu_w, sgu_b=sgu_b, w_branch_att=w_branch_att, w_branch_conv=w_branch_conv,
               w_branch_sgu=w_branch_sgu, w_out=w_out, w_ffn_up=w_ffn_up, conv_ffn_w=conv_ffn_w,
               w_ffn_down=w_ffn_down)
    moms = dict(pre_mix_g=m_pre_mix_g, post_mix_g=m_post_mix_g, pre_ffn_g=m_pre_ffn_g, post_ffn_g=m_post_ffn_g,
                w_in=m_w_in, b_forget=m_b_forget, b_gate=m_b_gate, conv_mix_w=m_conv_mix_w, sgu_ln_g=m_sgu_ln_g,
                sgu_ln_b=m_sgu_ln_b, sgu_w=m_sgu_w, sgu_b=m_sgu_b, w_branch_att=m_w_branch_att,
                w_branch_conv=m_w_branch_conv, w_branch_sgu=m_w_branch_sgu, w_out=m_w_out, w_ffn_up=m_w_ffn_up,
                conv_ffn_w=m_conv_ffn_w, w_ffn_down=m_w_ffn_down)
    vels = dict(pre_mix_g=v_pre_mix_g, post_mix_g=v_post_mix_g, pre_ffn_g=v_pre_ffn_g, post_ffn_g=v_post_ffn_g,
                w_in=v_w_in, b_forget=v_b_forget, b_gate=v_b_gate, conv_mix_w=v_conv_mix_w, sgu_ln_g=v_sgu_ln_g,
                sgu_ln_b=v_sgu_ln_b, sgu_w=v_sgu_w, sgu_b=v_sgu_b, w_branch_att=v_w_branch_att,
                w_branch_conv=v_w_branch_conv, w_branch_sgu=v_w_branch_sgu, w_out=v_w_out, w_ffn_up=v_w_ffn_up,
                conv_ffn_w=v_conv_ffn_w, w_ffn_down=v_w_ffn_down)

    c_idx = lax.axis_index("c").astype(jnp.int32).reshape(1)
    me_idx = (2 * lax.axis_index("x") + lax.axis_index("y")).astype(jnp.int32).reshape(1)
    small = _gather_small(wts)

    xs = x[0]
    layers, saved = [], []
    lands = _gather_finish(_gather_begin(wts, 0, me_idx), xs)
    for l in range(DEPTH):
        p = _prep_layer(wts, lands, small, l)
        nxt = _gather_begin(wts, l + 1, me_idx) if l + 1 < DEPTH else None
        xs, sv = _layer_fwd(xs, p, nxt[4] if nxt else None)
        if nxt:
            lands = _gather_finish(nxt, xs)
        layers.append(p)
        saved.append(sv)
    dy, loss_part = _loss_head(xs, loss_target[0], "loss_head")
    loss = lax.psum(loss_part[0, 0], ("x", "y", "c"))

    big_red = [None] * DEPTH
    small_grads = [None] * DEPTH
    pending = None
    for l in reversed(range(DEPTH)):
        dy, ts, small_grads[l] = _layer_bwd(dy, layers[l], saved[l], pending[4] if pending else None)
        if pending:
            big_red[l + 1] = _rs_finish(pending, dy, c_idx, "big")
        pending = _rs_begin(ts, c_idx, me_idx, "big")
    big_red[0] = _rs_finish(pending, dy, c_idx, "big")
    grad_x = dy[None]

    rep_flat = jnp.concatenate([small_grads[l][n].reshape(-1) for l in range(DEPTH) for n, _ in REPLICATED])
    rep_flat = jnp.pad(rep_flat, (0, N_CHIPS * _REP_QUARTER - rep_flat.shape[0]))
    rows = []
    for j in range(N_CHIPS):
        pieces = [_shard_cols(small_grads[l][n], j) for l in range(DEPTH) for n, _ in SMALL_SHARDS]
        pieces.append(rep_flat[j * _REP_QUARTER:(j + 1) * _REP_QUARTER])
        rows.append(_pack_rows(pieces, SMALL_ROWS, F32))
    small_red = _reduce_scatter_chips(jnp.stack(rows), "small")
    small_all = _all_gather_chips(small_red, "gather_small").reshape(N_CHIPS, -1)

    grads = {}
    for i, (n, _) in enumerate(BIG_SHARDS):
        grads[n] = jnp.stack([big_red[l][i][:, :IN_SHARD] if n == "w_in" else big_red[l][i] for l in range(DEPTH)])
    mine_small = small_red.reshape(-1)
    parts = _unpack(mine_small, [s for _ in range(DEPTH) for _, s in SMALL_SHARDS])
    for i, (n, _) in enumerate(SMALL_SHARDS):
        grads[n] = jnp.stack([parts[l * len(SMALL_SHARDS) + i] for l in range(DEPTH)])
    off = DEPTH * _SMALL_ELEMS
    rep_all = jnp.concatenate([small_all[j, off:off + _REP_QUARTER] for j in range(N_CHIPS)])
    parts = _unpack(rep_all, [s for _ in range(DEPTH) for _, s in REPLICATED])
    for i, (n, _) in enumerate(REPLICATED):
        grads[n] = jnp.stack([parts[l * len(REPLICATED) + i] for l in range(DEPTH)])

    deltas, new_m, new_v = {}, {}, {}
    for n in WEIGHT_ORDER:
        deltas[n], new_m[n], new_v[n] = _adamw(wts[n], grads[n], moms[n], vels[n], "adamw_" + n)
    return (loss, grad_x, *[grads[n] for n in WEIGHT_ORDER], *[deltas[n] for n in WEIGHT_ORDER],
            *[new_m[n] for n in WEIGHT_ORDER], *[new_v[n] for n in WEIGHT_ORDER])
```

```python
import functools
import math

import jax
import jax.numpy as jnp
from jax import lax
from jax.experimental import pallas as pl
from jax.experimental.pallas import tpu as pltpu

F32 = jnp.float32
BF16 = jnp.bfloat16
MXU_DTYPE = jnp.bfloat16

D_MODEL = 1024
HEAD_DIM = 64
N_HEADS = 8
D_ATT = 512
D_CONV = 256
D_SGU = 256
N_GROUPS = 4
CHUNK = 128
D_FF = 2816
DEPTH = 4
RMS_EPS = 1e-6
LN_EPS = 1e-5
N_CHIPS = 4
LANES = 128
PACK_COLS = 1024
HALO = 16

ADAM_LR = 0.001
ADAM_B1 = 0.9
ADAM_B2 = 0.999
ADAM_EPS = 1e-08
ADAM_WD = 0.01
ADAM_STEP = 10

OFF_GL = 0
OFF_Q = 3 * D_MODEL
OFF_K = OFF_Q + D_ATT
OFF_V = OFF_K + D_ATT
OFF_BG = OFF_V + D_ATT
OFF_CG = OFF_BG + D_CONV
OFF_HC = OFF_CG + D_CONV
OFF_U = OFF_HC + D_CONV
OFF_VS = OFF_U + D_SGU
W_P = OFF_VS + D_SGU
F_ROWS = 16

VMEM_LIMIT = 56 * 1024 * 1024
MESH = pl.DeviceIdType.MESH


def _params(sem=None):
    if sem is None:
        return pltpu.CompilerParams(vmem_limit_bytes=VMEM_LIMIT)
    return pltpu.CompilerParams(dimension_semantics=sem, vmem_limit_bytes=VMEM_LIMIT)


def _tile(dim, pref):
    if dim <= pref:
        return dim
    if dim % pref == 0:
        return pref
    return dim


_DIMS = {"nn": (((1,), (0,)), ((), ())), "nt": (((1,), (1,)), ((), ())), "tn": (((0,), (0,)), ((), ()))}


def _mm(a, b, mode, out_dtype, name, tm, tn, tk, chip_of=None):
    if mode == "tn":
        K, M = a.shape
    else:
        M, K = a.shape
    N = b.shape[0] if mode == "nt" else b.shape[1]
    tm, tn, tk = _tile(M, tm), _tile(N // N_CHIPS if chip_of else N, tn), _tile(K, tk)
    nk = K // tk
    dims = _DIMS[mode]

    def body(a_ref, b_ref, o_ref, *acc):
        part = lax.dot_general(a_ref[...].astype(MXU_DTYPE), b_ref[...].astype(MXU_DTYPE), dims,
                               preferred_element_type=F32)
        if nk == 1:
            o_ref[...] = part.astype(o_ref.dtype)
        else:
            acc_ref = acc[0]
            k = pl.program_id(2)

            @pl.when(k == 0)
            def _():
                acc_ref[...] = part

            @pl.when(k > 0)
            def _():
                acc_ref[...] += part

            @pl.when(k == nk - 1)
            def _():
                o_ref[...] = acc_ref[...].astype(o_ref.dtype)

    if mode == "tn":
        a_spec = pl.BlockSpec((tk, tm), lambda i, j, k: (k, i))
    else:
        a_spec = pl.BlockSpec((tm, tk), lambda i, j, k: (i, k))
    if mode == "nt":
        b_spec = pl.BlockSpec((tn, tk), lambda i, j, k: (j, k))
    else:
        b_spec = pl.BlockSpec((tk, tn), lambda i, j, k: (k, j))
    if chip_of is None:
        out_spec = pl.BlockSpec((tm, tn), lambda i, j, k: (i, j))
        out_shape = jax.ShapeDtypeStruct((M, N), out_dtype)
    else:
        per = (N // N_CHIPS) // tn
        out_spec = pl.BlockSpec((None, tm, tn), lambda i, j, k: (chip_of(j // per), i, j % per))
        out_shape = jax.ShapeDtypeStruct((N_CHIPS, M, N // N_CHIPS), out_dtype)
    return pl.pallas_call(
        body,
        name=name,
        grid=(M // tm, N // tn, nk),
        in_specs=[a_spec, b_spec],
        out_specs=out_spec,
        out_shape=out_shape,
        scratch_shapes=[pltpu.VMEM((tm, tn), F32)] if nk > 1 else [],
        compiler_params=_params(("parallel", "parallel", "arbitrary")),
    )(a, b)


_GELU_K = math.sqrt(2.0 / math.pi)
_GELU_C = 0.044715


def _gelu(x):
    t = jnp.tanh(_GELU_K * (x + _GELU_C * (x * x * x)))
    return x * (0.5 * (1.0 + t))


def _gelu_and_grad(x):
    x2 = x * x
    t = jnp.tanh(_GELU_K * (x + _GELU_C * (x2 * x)))
    cdf = 0.5 * (1.0 + t)
    dcdf = 0.5 * (1.0 - t * t) * (_GELU_K * (1.0 + 3.0 * _GELU_C * x2))
    return x * cdf, cdf + x * dcdf


def _sigmoid(x):
    return 1.0 / (1.0 + jnp.exp(-x))


def _shift_down(cur, prev, k):
    h = prev.shape[0]
    ext = jnp.concatenate([prev, cur], axis=0)
    return pltpu.roll(ext, k, 0)[h:]


def _shift_up(cur, nxt, k):
    t, h = cur.shape[0], nxt.shape[0]
    ext = jnp.concatenate([cur, nxt], axis=0)
    return pltpu.roll(ext, t + h - k, 0)[:t]


def _row_sum8(x):
    t, c = x.shape
    return jnp.sum(x.reshape(t // 8, 8, c), axis=0)


_DEP = pl.BlockSpec((8, LANES), lambda i: (0, 0))


def _rms_fwd(x, g, name, dep=None):
    s, d = x.shape
    t = _tile(s, 512)

    def body(x_ref, g_ref, *rest):
        o_ref = rest[-1]
        xv = x_ref[...]
        r = lax.rsqrt(jnp.mean(xv * xv, axis=-1, keepdims=True) + RMS_EPS)
        o_ref[...] = (xv * r * g_ref[...]).astype(o_ref.dtype)

    deps = [] if dep is None else [dep]
    return pl.pallas_call(
        body, name=name, grid=(s // t,),
        in_specs=[pl.BlockSpec((t, d), lambda i: (i, 0)), pl.BlockSpec((1, d), lambda i: (0, 0))] + [_DEP] * len(deps),
        out_specs=pl.BlockSpec((t, d), lambda i: (i, 0)),
        out_shape=jax.ShapeDtypeStruct((s, d), BF16),
        compiler_params=_params(("parallel",)),
    )(x, g, *deps)


def _resid_post(x, y, g, name):
    s, d = x.shape
    t = _tile(s, 512)

    def body(x_ref, y_ref, g_ref, o_ref):
        yv = y_ref[...]
        r = lax.rsqrt(jnp.mean(yv * yv, axis=-1, keepdims=True) + RMS_EPS)
        o_ref[...] = x_ref[...] + yv * r * g_ref[...]

    row = pl.BlockSpec((t, d), lambda i: (i, 0))
    return pl.pallas_call(
        body, name=name, grid=(s // t,),
        in_specs=[row, row, pl.BlockSpec((1, d), lambda i: (0, 0))],
        out_specs=row,
        out_shape=jax.ShapeDtypeStruct((s, d), F32),
        compiler_params=_params(("parallel",)),
    )(x, y, g)


def _rms_bwd(xin, g, dys, dres, out_dtype, name, dep=None):
    s, d = xin.shape
    t = _tile(s, 512)
    n = s // t
    n_dy = len(dys)
    has_res = dres is not None
    deps = [] if dep is None else [dep]

    def body(*refs):
        x_ref, g_ref = refs[0], refs[1]
        dy_refs = refs[2:2 + n_dy]
        pos = 2 + n_dy
        res_ref = refs[pos] if has_res else None
        pos += (1 if has_res else 0) + len(deps)
        dx_ref, dg_ref, acc_ref = refs[pos], refs[pos + 1], refs[pos + 2]
        i = pl.program_id(0)
        xv = x_ref[...]
        dy = dy_refs[0][...].astype(F32)
        for extra in dy_refs[1:]:
            dy = dy + extra[...].astype(F32)
        r = lax.rsqrt(jnp.mean(xv * xv, axis=-1, keepdims=True) + RMS_EPS)
        u = dy * g_ref[...]
        xr = xv * r
        dx = r * (u - xr * jnp.mean(u * xr, axis=-1, keepdims=True))
        if has_res:
            dx = dx + res_ref[...]
        dx_ref[...] = dx.astype(dx_ref.dtype)
        part = _row_sum8(dy * xr)

        @pl.when(i == 0)
        def _():
            acc_ref[...] = part

        @pl.when(i > 0)
        def _():
            acc_ref[...] += part

        @pl.when(i == n - 1)
        def _():
            dg_ref[...] = jnp.sum(acc_ref[...], axis=0, keepdims=True)

    row = pl.BlockSpec((t, d), lambda i: (i, 0))
    vec = pl.BlockSpec((1, d), lambda i: (0, 0))
    ins = [xin, g, *dys] + ([dres] if has_res else []) + deps
    return pl.pallas_call(
        body, name=name, grid=(n,),
        in_specs=[row, vec] + [row] * (n_dy + (1 if has_res else 0)) + [_DEP] * len(deps),
        out_specs=[row, vec],
        out_shape=[jax.ShapeDtypeStruct((s, d), out_dtype), jax.ShapeDtypeStruct((1, d), F32)],
        scratch_shapes=[pltpu.VMEM((8, d), F32)],
        compiler_params=_params(("arbitrary",)),
    )(*ins)


def _loss_head(y, target, name):
    s, d = y.shape
    t = _tile(s, 512)
    n = s // t

    def body(y_ref, t_ref, dy_ref, loss_ref, acc_ref):
        i = pl.program_id(0)
        e = y_ref[...] - t_ref[...]
        dy_ref[...] = e * (1.0 / d)
        part = _row_sum8(e * e)

        @pl.when(i == 0)
        def _():
            acc_ref[...] = part

        @pl.when(i > 0)
        def _():
            acc_ref[...] += part

        @pl.when(i == n - 1)
        def _():
            tot = jnp.sum(jnp.sum(acc_ref[...], axis=0, keepdims=True), axis=1, keepdims=True)
            loss_ref[...] = tot * (0.5 / d)

    row = pl.BlockSpec((t, d), lambda i: (i, 0))
    return pl.pallas_call(
        body, name=name, grid=(n,),
        in_specs=[row, row],
        out_specs=[row, pl.BlockSpec((1, 1), lambda i: (0, 0))],
        out_shape=[jax.ShapeDtypeStruct((s, d), F32), jax.ShapeDtypeStruct((1, 1), F32)],
        scratch_shapes=[pltpu.VMEM((8, d), F32)],
        compiler_params=_params(("arbitrary",)),
    )(y, target)


def _split3(x):
    hi = x.astype(BF16)
    r1 = x - hi.astype(F32)
    mid = r1.astype(BF16)
    lo = (r1 - mid.astype(F32)).astype(BF16)
    return hi, mid, lo


def _tri_dot(x, tri):
    hi, mid, lo = _split3(x)
    dn = _DIMS["nn"]
    out = lax.dot_general(hi, tri, dn, preferred_element_type=F32)
    out = out + lax.dot_general(mid, tri, dn, preferred_element_type=F32)
    return out + lax.dot_general(lo, tri, dn, preferred_element_type=F32)


def _log_sigmoid(z):
    return jnp.minimum(z, 0.0) - jnp.log(1.0 + jnp.exp(-jnp.abs(z)))


def _gate_fwd(f_row, b_col, name):
    rows, s = f_row.shape
    t = _tile(s, 512)
    n = s // t

    def body(f_ref, b_ref, c_ref, ck_ref, carry_ref):
        i = pl.program_id(0)

        @pl.when(i == 0)
        def _():
            carry_ref[...] = jnp.zeros_like(carry_ref)

        logf = _log_sigmoid(f_ref[...] + b_ref[...])
        r = lax.broadcasted_iota(jnp.int32, (t, t), 0)
        c = lax.broadcasted_iota(jnp.int32, (t, t), 1)
        tri = jnp.where(r <= c, 1.0, 0.0).astype(BF16)
        cs = _tri_dot(logf, tri) + carry_ref[...]
        carry_ref[...] = cs[:, t - 1:t]
        for h in range(N_HEADS):
            c_ref[h] = jnp.broadcast_to(cs[h:h + 1, :], (8, t))
        terms = [part.astype(F32) for part in _split3(-cs)]
        sub = lax.broadcasted_iota(jnp.int32, (LANES, t), 0)
        for p in range(N_HEADS // 2):
            stacked = jnp.zeros((LANES, t), F32)
            for hh in range(2):
                for j, term in enumerate(terms):
                    h = 2 * p + hh
                    stacked = jnp.where(sub == 3 * hh + j, jnp.broadcast_to(term[h:h + 1, :], (LANES, t)), stacked)
            ck_ref[p] = stacked.T.astype(ck_ref.dtype)

    return pl.pallas_call(
        body, name=name, grid=(n,),
        in_specs=[pl.BlockSpec((rows, t), lambda i: (0, i)), pl.BlockSpec((rows, 1), lambda i: (0, 0))],
        out_specs=[pl.BlockSpec((N_HEADS, 8, t), lambda i: (0, 0, i)),
                   pl.BlockSpec((N_HEADS // 2, t, LANES), lambda i: (0, i, 0))],
        out_shape=[jax.ShapeDtypeStruct((N_HEADS, 8, s), F32), jax.ShapeDtypeStruct((N_HEADS // 2, s, LANES), BF16)],
        scratch_shapes=[pltpu.VMEM((rows, 1), F32)],
        compiler_params=_params(("arbitrary",)),
    )(f_row, b_col)


def _gate_bwd(f_row, b_col, dc_even, dc_odd, name):
    rows, s = f_row.shape
    t = _tile(s, 512)
    n = s // t

    def body(f_ref, b_ref, dce_ref, dco_ref, df_ref, db_ref, carry_ref, acc_ref):
        i = pl.program_id(0)

        @pl.when(i == 0)
        def _():
            carry_ref[...] = jnp.zeros_like(carry_ref)
            acc_ref[...] = jnp.zeros_like(acc_ref)

        head = lax.broadcasted_iota(jnp.int32, (rows, t), 0)
        dcv = jnp.zeros((rows, t), F32)
        for h in range(N_HEADS):
            src = dce_ref if h % 2 == 0 else dco_ref
            dcv = jnp.where(head == h, jnp.broadcast_to(src[h // 2, 0:1, :], (rows, t)), dcv)
        r = lax.broadcasted_iota(jnp.int32, (t, t), 0)
        c = lax.broadcasted_iota(jnp.int32, (t, t), 1)
        tri = jnp.where(r >= c, 1.0, 0.0).astype(BF16)
        dlogf = _tri_dot(dcv, tri) + carry_ref[...]
        carry_ref[...] = dlogf[:, 0:1]
        z = f_ref[...] + b_ref[...]
        df = dlogf * _sigmoid(-z)
        df_ref[...] = df.astype(df_ref.dtype)
        acc_ref[...] += jnp.sum(df, axis=1, keepdims=True)

        @pl.when(i == n - 1)
        def _():
            db_ref[...] = acc_ref[...]

    rev = lambda i: (0, n - 1 - i)
    dc_spec = pl.BlockSpec((N_HEADS // 2, 8, t), lambda i: (0, 0, n - 1 - i))
    return pl.pallas_call(
        body, name=name, grid=(n,),
        in_specs=[pl.BlockSpec((rows, t), rev), pl.BlockSpec((rows, 1), lambda i: (0, 0)), dc_spec, dc_spec],
        out_specs=[pl.BlockSpec((rows, t), rev), pl.BlockSpec((rows, 1), lambda i: (0, 0))],
        out_shape=[jax.ShapeDtypeStruct((rows, s), BF16), jax.ShapeDtypeStruct((rows, 1), F32)],
        scratch_shapes=[pltpu.VMEM((rows, 1), F32), pltpu.VMEM((rows, 1), F32)],
        compiler_params=_params(("arbitrary",)),
    )(f_row, b_col, dc_even, dc_odd)


_NEG = -1e30
_SCALE = HEAD_DIM ** -0.5


def _head_masks():
    lane = lax.broadcasted_iota(jnp.int32, (1, LANES), 1)
    return [lane < HEAD_DIM, lane >= HEAD_DIM]


def _attn_fwd(h, ck, name):
    s = h.shape[0]
    t = _tile(s, 512)
    n = s // t
    qb, kb, vb = OFF_Q // LANES, OFF_K // LANES, OFF_V // LANES

    def body(q_ref, k_ref, v_ref, ck_ref, o_ref, of_ref, lse_ref, m_ref, l_ref, acc_ref):
        qi, ki = pl.program_id(1), pl.program_id(2)
        masks = _head_masks()
        lane = lax.broadcasted_iota(jnp.int32, (1, LANES), 1)

        @pl.when(ki == 0)
        def _():
            m_ref[...] = jnp.full_like(m_ref, _NEG)
            l_ref[...] = jnp.zeros_like(l_ref)
            acc_ref[...] = jnp.zeros_like(acc_ref)

        def step(diag):
            q = q_ref[...] * _SCALE
            k_aug = jnp.concatenate([k_ref[...], ck_ref[0]], axis=1)
            v = v_ref[...]
            for hh in range(2):
                ones = jnp.where((lane >= 3 * hh) & (lane < 3 * hh + 3), 1.0, 0.0).astype(q.dtype)
                q_aug = jnp.concatenate([jnp.where(masks[hh], q, jnp.zeros_like(q)),
                                         jnp.broadcast_to(ones, q.shape)], axis=1)
                sc = lax.dot_general(k_aug, q_aug, _DIMS["nt"], preferred_element_type=F32)
                if diag:
                    r = lax.broadcasted_iota(jnp.int32, (t, t), 0)
                    cc = lax.broadcasted_iota(jnp.int32, (t, t), 1)
                    sc = jnp.where(r <= cc, sc, _NEG)
                m_prev = m_ref[hh]
                m_new = jnp.maximum(m_prev, jnp.max(sc, axis=0, keepdims=True))
                alpha = jnp.exp(m_prev - m_new)
                p = jnp.exp(sc - m_new)
                l_ref[hh] = alpha * l_ref[hh] + jnp.sum(p, axis=0, keepdims=True)
                m_ref[hh] = m_new
                p_hi = p.astype(MXU_DTYPE)
                p_lo = (p - p_hi.astype(F32)).astype(MXU_DTYPE)
                pv = (lax.dot_general(v, p_hi, _DIMS["tn"], preferred_element_type=F32)
                      + lax.dot_general(v, p_lo, _DIMS["tn"], preferred_element_type=F32))
                rows = slice(hh * HEAD_DIM, (hh + 1) * HEAD_DIM)
                acc_ref[rows, :] = alpha * acc_ref[rows, :] + pv[rows]

        @pl.when(ki < qi)
        def _():
            step(False)

        @pl.when(ki == qi)
        def _():
            step(True)
            inv = jnp.concatenate([jnp.broadcast_to(1.0 / l_ref[hh], (HEAD_DIM, t)) for hh in range(2)], axis=0)
            out = (acc_ref[...] * inv).T
            o_ref[...] = out.astype(o_ref.dtype)
            of_ref[...] = out
            lse = jnp.concatenate([jnp.broadcast_to(m_ref[hh] + jnp.log(l_ref[hh]), (HEAD_DIM, t))
                                   for hh in range(2)], axis=0)
            lse_ref[...] = lse.T

    kv_row = lambda p, qi, ki: jnp.minimum(ki, qi)
    return pl.pallas_call(
        body, name=name, grid=(N_HEADS // 2, n, n),
        in_specs=[
            pl.BlockSpec((t, LANES), lambda p, qi, ki: (qi, qb + p)),
            pl.BlockSpec((t, LANES), lambda p, qi, ki: (kv_row(p, qi, ki), kb + p)),
            pl.BlockSpec((t, LANES), lambda p, qi, ki: (kv_row(p, qi, ki), vb + p)),
            pl.BlockSpec((1, t, LANES), lambda p, qi, ki: (p, kv_row(p, qi, ki), 0)),
        ],
        out_specs=[pl.BlockSpec((t, LANES), lambda p, qi, ki: (qi, p))] * 3,
        out_shape=[jax.ShapeDtypeStruct((s, D_ATT), BF16), jax.ShapeDtypeStruct((s, D_ATT), F32),
                   jax.ShapeDtypeStruct((s, D_ATT), F32)],
        scratch_shapes=[pltpu.VMEM((2, 1, t), F32), pltpu.VMEM((2, 1, t), F32), pltpu.VMEM((LANES, t), F32)],
        compiler_params=_params(("parallel", "parallel", "arbitrary")),
    )(h, h, h, ck)


def _attn_bwd(h, c, o, lse, do, name):
    s = h.shape[0]
    t = _tile(s, 512)
    n = s // t
    qb, kb, vb = OFF_Q // LANES, OFF_K // LANES, OFF_V // LANES

    def body(q_ref, k_ref, v_ref, c0_ref, c1_ref, o_ref, lse_ref, do_ref,
             dq_ref, dk_ref, dv_ref, dc0_ref, dc1_ref, dk_acc, dv_acc, dc_acc):
        ki, qi = pl.program_id(1), pl.program_id(2)
        masks = _head_masks()

        @pl.when((ki == 0) & (qi == 0))
        def _():
            dq_ref[...] = jnp.zeros_like(dq_ref)

        @pl.when(qi == ki)
        def _():
            dk_acc[...] = jnp.zeros_like(dk_acc)
            dv_acc[...] = jnp.zeros_like(dv_acc)
            dc_acc[...] = jnp.zeros_like(dc_acc)

        def step(diag):
            q = q_ref[...] * _SCALE
            k = k_ref[...]
            v = v_ref[...]
            dov = do_ref[...]
            lsev = lse_ref[...]
            prod = dov.astype(F32) * o_ref[...]
            dq_blk = jnp.zeros((t, LANES), F32)
            dk_blk = dk_acc[...]
            dv_blk = dv_acc[...]
            for hh, c_ref in enumerate((c0_ref, c1_ref)):
                mk = masks[hh]
                delta = jnp.sum(jnp.where(mk, prod, 0.0), axis=1, keepdims=True)
                lse_h = lsev[:, hh * HEAD_DIM:hh * HEAD_DIM + 1]
                qh = jnp.where(mk, q, jnp.zeros_like(q))
                doh = jnp.where(mk, dov, jnp.zeros_like(dov))
                sc = lax.dot_general(qh, k, _DIMS["nt"], preferred_element_type=F32) - c_ref[0, 0:1, :]
                p = jnp.exp(sc - lse_h)
                if diag:
                    r = lax.broadcasted_iota(jnp.int32, (t, t), 0)
                    cc = lax.broadcasted_iota(jnp.int32, (t, t), 1)
                    p = jnp.where(r >= cc, p, 0.0)
                dp = lax.dot_general(doh, v, _DIMS["nt"], preferred_element_type=F32)
                ds = p * (dp - delta)
                dsb = ds.astype(MXU_DTYPE)
                pb = p.astype(MXU_DTYPE)
                dv_blk = jnp.where(mk, dv_blk + lax.dot_general(pb, dov, _DIMS["tn"], preferred_element_type=F32),
                                   dv_blk)
                dk_blk = jnp.where(mk, dk_blk + lax.dot_general(dsb, q, _DIMS["tn"], preferred_element_type=F32),
                                   dk_blk)
                dq_blk = jnp.where(mk, lax.dot_general(dsb, k, _DIMS["nn"], preferred_element_type=F32), dq_blk)
                dc_acc[hh] = dc_acc[hh] - jnp.sum(ds, axis=0, keepdims=True)
            dk_acc[...] = dk_blk
            dv_acc[...] = dv_blk
            rows = pl.ds(pl.multiple_of(qi * t, t), t)
            dq_ref[rows, :] = dq_ref[rows, :] + dq_blk * _SCALE

        @pl.when(qi > ki)
        def _():
            step(False)

        @pl.when(qi == ki)
        def _():
            step(True)

        @pl.when(qi == n - 1)
        def _():
            dk_ref[...] = dk_acc[...].astype(dk_ref.dtype)
            dv_ref[...] = dv_acc[...].astype(dv_ref.dtype)
            dc0_ref[0] = jnp.broadcast_to(dc_acc[0], (8, t))
            dc1_ref[0] = jnp.broadcast_to(dc_acc[1], (8, t))

    q_row = lambda p, ki, qi: jnp.maximum(qi, ki)
    return pl.pallas_call(
        body, name=name, grid=(N_HEADS // 2, n, n),
        in_specs=[
            pl.BlockSpec((t, LANES), lambda p, ki, qi: (q_row(p, ki, qi), qb + p)),
            pl.BlockSpec((t, LANES), lambda p, ki, qi: (ki, kb + p)),
            pl.BlockSpec((t, LANES), lambda p, ki, qi: (ki, vb + p)),
            pl.BlockSpec((1, 8, t), lambda p, ki, qi: (2 * p, 0, ki)),
            pl.BlockSpec((1, 8, t), lambda p, ki, qi: (2 * p + 1, 0, ki)),
            pl.BlockSpec((t, LANES), lambda p, ki, qi: (q_row(p, ki, qi), p)),
            pl.BlockSpec((t, LANES), lambda p, ki, qi: (q_row(p, ki, qi), p)),
            pl.BlockSpec((t, LANES), lambda p, ki, qi: (q_row(p, ki, qi), p)),
        ],
        out_specs=[
            pl.BlockSpec((s, LANES), lambda p, ki, qi: (0, p)),
            pl.BlockSpec((t, LANES), lambda p, ki, qi: (ki, p)),
            pl.BlockSpec((t, LANES), lambda p, ki, qi: (ki, p)),
            pl.BlockSpec((1, 8, t), lambda p, ki, qi: (p, 0, ki)),
            pl.BlockSpec((1, 8, t), lambda p, ki, qi: (p, 0, ki)),
        ],
        out_shape=[jax.ShapeDtypeStruct((s, D_ATT), F32), jax.ShapeDtypeStruct((s, D_ATT), BF16),
                   jax.ShapeDtypeStruct((s, D_ATT), BF16), jax.ShapeDtypeStruct((N_HEADS // 2, 8, s), F32),
                   jax.ShapeDtypeStruct((N_HEADS // 2, 8, s), F32)],
        scratch_shapes=[pltpu.VMEM((t, LANES), F32), pltpu.VMEM((t, LANES), F32), pltpu.VMEM((2, 1, t), F32)],
        compiler_params=_params(("parallel", "arbitrary", "arbitrary")),
    )(h, h, h, c, c, o, lse, do)


def _conv3(z, z_prev, w_ref):
    return (w_ref[2:3, :] * z + w_ref[1:2, :] * _shift_down(z, z_prev, 1)
            + w_ref[0:1, :] * _shift_down(z, z_prev, 2))


def _sconv_fwd(h, w, name):
    s = h.shape[0]
    t = _tile(s, 512)
    r = t // HALO
    c = D_CONV
    b_bg, b_cg, b_hc = OFF_BG // c, OFF_CG // c, OFF_HC // c

    def body(bg_ref, cg_ref, hc_ref, cgp_ref, hcp_ref, w_ref, y_ref):
        i = pl.program_id(0)
        live = (i > 0).astype(F32)
        z = cg_ref[...].astype(F32) * hc_ref[...].astype(F32)
        zp = cgp_ref[...].astype(F32) * hcp_ref[...].astype(F32) * live
        y_ref[...] = (bg_ref[...].astype(F32) * _conv3(z, zp, w_ref)).astype(y_ref.dtype)

    cur = lambda b: pl.BlockSpec((t, c), lambda i: (i, b))
    prev = lambda b: pl.BlockSpec((HALO, c), lambda i: (jnp.maximum(i * r - 1, 0), b))
    return pl.pallas_call(
        body, name=name, grid=(s // t,),
        in_specs=[cur(b_bg), cur(b_cg), cur(b_hc), prev(b_cg), prev(b_hc), pl.BlockSpec((8, c), lambda i: (0, 0))],
        out_specs=pl.BlockSpec((t, c), lambda i: (i, 0)),
        out_shape=jax.ShapeDtypeStruct((s, c), BF16),
        compiler_params=_params(("parallel",)),
    )(h, h, h, h, h, w)


def _sconv_bwd(h, w, dy, name):
    s = h.shape[0]
    t = _tile(s, 512)
    n = s // t
    r = t // HALO
    nh = s // HALO
    c = D_CONV
    b_bg, b_cg, b_hc = OFF_BG // c, OFF_CG // c, OFF_HC // c

    def body(bg_ref, cg_ref, hc_ref, cgp_ref, hcp_ref, bgn_ref, dy_ref, dyn_ref, w_ref, d_ref, dw_ref, acc_ref):
        i = pl.program_id(0)
        has_prev = (i > 0).astype(F32)
        has_next = (i < n - 1).astype(F32)
        bg = bg_ref[...].astype(F32)
        cg = cg_ref[...].astype(F32)
        hc = hc_ref[...].astype(F32)
        dyv = dy_ref[...].astype(F32)
        z = cg * hc
        zp = cgp_ref[...].astype(F32) * hcp_ref[...].astype(F32) * has_prev
        z1 = _shift_down(z, zp, 1)
        z2 = _shift_down(z, zp, 2)
        cz = w_ref[2:3, :] * z + w_ref[1:2, :] * z1 + w_ref[0:1, :] * z2
        dcz = dyv * bg
        dczn = dyn_ref[...].astype(F32) * bgn_ref[...].astype(F32) * has_next
        dz = (w_ref[2:3, :] * dcz + w_ref[1:2, :] * _shift_up(dcz, dczn, 1)
              + w_ref[0:1, :] * _shift_up(dcz, dczn, 2))
        d_ref[:, 0:c] = (dyv * cz).astype(d_ref.dtype)
        d_ref[:, c:2 * c] = (dz * hc).astype(d_ref.dtype)
        d_ref[:, 2 * c:3 * c] = (dz * cg).astype(d_ref.dtype)

        @pl.when(i == 0)
        def _():
            acc_ref[...] = jnp.zeros_like(acc_ref)

        acc_ref[0] += _row_sum8(dcz * z2)
        acc_ref[1] += _row_sum8(dcz * z1)
        acc_ref[2] += _row_sum8(dcz * z)

        @pl.when(i == n - 1)
        def _():
            rows = [jnp.sum(acc_ref[k], axis=0, keepdims=True) for k in range(3)]
            dw_ref[...] = jnp.concatenate(rows + [jnp.zeros((5, c), F32)], axis=0)

    cur = lambda b: pl.BlockSpec((t, c), lambda i: (i, b))
    prev = lambda b: pl.BlockSpec((HALO, c), lambda i: (jnp.maximum(i * r - 1, 0), b))
    nxt = lambda b: pl.BlockSpec((HALO, c), lambda i: (jnp.minimum((i + 1) * r, nh - 1), b))
    return pl.pallas_call(
        body, name=name, grid=(n,),
        in_specs=[cur(b_bg), cur(b_cg), cur(b_hc), prev(b_cg), prev(b_hc), nxt(b_bg),
                  cur(0), nxt(0), pl.BlockSpec((8, c), lambda i: (0, 0))],
        out_specs=[pl.BlockSpec((t, 3 * c), lambda i: (i, 0)), pl.BlockSpec((8, c), lambda i: (0, 0))],
        out_shape=[jax.ShapeDtypeStruct((s, 3 * c), BF16), jax.ShapeDtypeStruct((8, c), F32)],
        scratch_shapes=[pltpu.VMEM((3, 8, c), F32)],
        compiler_params=_params(("arbitrary",)),
    )(h, h, h, h, h, h, dy, dy, w)


def _group_masks():
    lane = lax.broadcasted_iota(jnp.int32, (1, D_SGU), 1)
    return [(lane >= g * HEAD_DIM) & (lane < (g + 1) * HEAD_DIM) for g in range(N_GROUPS)]


def _tril_weights(w_ref):
    r = lax.broadcasted_iota(jnp.int32, (CHUNK, CHUNK), 0)
    c = lax.broadcasted_iota(jnp.int32, (CHUNK, CHUNK), 1)
    return [jnp.where(r >= c, w_ref[g], 0.0).astype(MXU_DTYPE) for g in range(N_GROUPS)]


def _sgu_ln(vs, g_ref, b_ref):
    vg, dvg = _gelu_and_grad(vs)
    mu = jnp.mean(vg, axis=-1, keepdims=True)
    xc = vg - mu
    rstd = lax.rsqrt(jnp.mean(xc * xc, axis=-1, keepdims=True) + LN_EPS)
    xhat = xc * rstd
    return xhat * g_ref[...] + b_ref[...], xhat, rstd, dvg


def _sgu_fwd(h, ln_g, ln_b, w_s, bias, name):
    s = h.shape[0]
    t = _tile(s, 512)
    c = D_SGU
    b_u, b_v = OFF_U // c, OFF_VS // c

    def body(u_ref, v_ref, g_ref, b_ref, w_ref, bias_ref, y_ref):
        gm = _group_masks()
        wm = _tril_weights(w_ref)
        ug = _gelu(u_ref[...].astype(F32))
        vn, _, _, _ = _sgu_ln(v_ref[...].astype(F32), g_ref, b_ref)
        vnb = vn.astype(MXU_DTYPE)
        for ch in range(t // CHUNK):
            rows = slice(ch * CHUNK, (ch + 1) * CHUNK)
            mixed = bias_ref[...]
            for g in range(N_GROUPS):
                mg = lax.dot_general(wm[g], vnb[rows], _DIMS["nn"], preferred_element_type=F32)
                mixed = jnp.where(gm[g], mixed + mg, mixed)
            y_ref[rows, :] = (ug[rows] * mixed).astype(y_ref.dtype)

    full = lambda shp: pl.BlockSpec(shp, lambda i: (0,) * len(shp))
    return pl.pallas_call(
        body, name=name, grid=(s // t,),
        in_specs=[pl.BlockSpec((t, c), lambda i: (i, b_u)), pl.BlockSpec((t, c), lambda i: (i, b_v)),
                  full((1, c)), full((1, c)), full((N_GROUPS, CHUNK, CHUNK)), full((CHUNK, c))],
        out_specs=pl.BlockSpec((t, c), lambda i: (i, 0)),
        out_shape=jax.ShapeDtypeStruct((s, c), BF16),
        compiler_params=_params(("parallel",)),
    )(h, h, ln_g, ln_b, w_s, bias)


def _sgu_bwd(h, ln_g, ln_b, w_s, bias, dy, name):
    s = h.shape[0]
    t = _tile(s, 512)
    n = s // t
    c = D_SGU
    b_u, b_v = OFF_U // c, OFF_VS // c

    def body(u_ref, v_ref, g_ref, b_ref, w_ref, bias_ref, dy_ref,
             d_ref, dg_ref, db_ref, dw_ref, dbias_ref, dg_acc, db_acc):
        i = pl.program_id(0)
        gm = _group_masks()
        wm = _tril_weights(w_ref)

        @pl.when(i == 0)
        def _():
            dg_acc[...] = jnp.zeros_like(dg_acc)
            db_acc[...] = jnp.zeros_like(db_acc)
            dw_ref[...] = jnp.zeros_like(dw_ref)
            dbias_ref[...] = jnp.zeros_like(dbias_ref)

        ug, dug = _gelu_and_grad(u_ref[...].astype(F32))
        vn, xhat, rstd, dvg = _sgu_ln(v_ref[...].astype(F32), g_ref, b_ref)
        vnb = vn.astype(MXU_DTYPE)
        dyv = dy_ref[...].astype(F32)
        dmixed = dyv * ug
        dmb = dmixed.astype(MXU_DTYPE)
        dvn_parts = []
        for ch in range(t // CHUNK):
            rows = slice(ch * CHUNK, (ch + 1) * CHUNK)
            mixed = bias_ref[...]
            dvn = jnp.zeros((CHUNK, c), F32)
            for g in range(N_GROUPS):
                mg = lax.dot_general(wm[g], vnb[rows], _DIMS["nn"], preferred_element_type=F32)
                mixed = jnp.where(gm[g], mixed + mg, mixed)
                dvn = jnp.where(gm[g], lax.dot_general(wm[g], dmb[rows], _DIMS["tn"], preferred_element_type=F32),
                                dvn)
                dmg = jnp.where(gm[g], dmb[rows], jnp.zeros_like(dmb[rows]))
                dw_ref[g] += lax.dot_general(dmg, vnb[rows], _DIMS["nt"], preferred_element_type=F32)
            d_ref[rows, 0:c] = (dyv[rows] * mixed * dug[rows]).astype(d_ref.dtype)
            dbias_ref[...] += dmixed[rows]
            dvn_parts.append(dvn)
        dvn = jnp.concatenate(dvn_parts, axis=0)
        dg_acc[...] += _row_sum8(dvn * xhat)
        db_acc[...] += _row_sum8(dvn)
        dxh = dvn * g_ref[...]
        dvgl = rstd * (dxh - jnp.mean(dxh, axis=-1, keepdims=True)
                       - xhat * jnp.mean(dxh * xhat, axis=-1, keepdims=True))
        d_ref[:, c:2 * c] = (dvgl * dvg).astype(d_ref.dtype)

        @pl.when(i == n - 1)
        def _():
            dg_ref[...] = jnp.sum(dg_acc[...], axis=0, keepdims=True)
            db_ref[...] = jnp.sum(db_acc[...], axis=0, keepdims=True)
            r = lax.broadcasted_iota(jnp.int32, (CHUNK, CHUNK), 0)
            cc = lax.broadcasted_iota(jnp.int32, (CHUNK, CHUNK), 1)
            for g in range(N_GROUPS):
                dw_ref[g] = jnp.where(r >= cc, dw_ref[g], 0.0)

    full = lambda shp: pl.BlockSpec(shp, lambda i: (0,) * len(shp))
    return pl.pallas_call(
        body, name=name, grid=(n,),
        in_specs=[pl.BlockSpec((t, c), lambda i: (i, b_u)), pl.BlockSpec((t, c), lambda i: (i, b_v)),
                  full((1, c)), full((1, c)), full((N_GROUPS, CHUNK, CHUNK)), full((CHUNK, c)),
                  pl.BlockSpec((t, c), lambda i: (i, 0))],
        out_specs=[pl.BlockSpec((t, 2 * c), lambda i: (i, 0)), full((1, c)), full((1, c)),
                   full((N_GROUPS, CHUNK, CHUNK)), full((CHUNK, c))],
        out_shape=[jax.ShapeDtypeStruct((s, 2 * c), BF16), jax.ShapeDtypeStruct((1, c), F32),
                   jax.ShapeDtypeStruct((1, c), F32), jax.ShapeDtypeStruct((N_GROUPS, CHUNK, CHUNK), F32),
                   jax.ShapeDtypeStruct((CHUNK, c), F32)],
        scratch_shapes=[pltpu.VMEM((8, c), F32), pltpu.VMEM((8, c), F32)],
        compiler_params=_params(("arbitrary",)),
    )(h, h, ln_g, ln_b, w_s, bias, dy)


def _merge_fwd(h, acts, ws, b_gate, name):
    s = h.shape[0]
    d = D_MODEL
    t = _tile(s, 512)

    def body(gl0, gl1, gl2, a0, a1, a2, w0, w1, w2, b_ref, o_ref):
        acc = jnp.zeros((t, d), F32)
        for i, (gl, a, w) in enumerate(((gl0, a0, w0), (gl1, a1, w1), (gl2, a2, w2))):
            y = lax.dot_general(a[...], w[...], _DIMS["nn"], preferred_element_type=F32)
            acc = acc + _sigmoid(gl[...].astype(F32) + b_ref[i:i + 1, :]) * y
        o_ref[...] = acc.astype(o_ref.dtype)

    full = lambda arr: pl.BlockSpec(arr.shape, lambda i: (0, 0))
    return pl.pallas_call(
        body, name=name, grid=(s // t,),
        in_specs=[pl.BlockSpec((t, d), lambda i, b=b: (i, b)) for b in range(3)]
                 + [pl.BlockSpec((t, a.shape[1]), lambda i: (i, 0)) for a in acts]
                 + [full(w) for w in ws] + [full(b_gate)],
        out_specs=pl.BlockSpec((t, d), lambda i: (i, 0)),
        out_shape=jax.ShapeDtypeStruct((s, d), BF16),
        compiler_params=_params(("parallel",)),
    )(h, h, h, *acts, *ws, b_gate)


def _merge_bwd(h, acts, ws, b_gate, dmerged, name):
    s = h.shape[0]
    d = D_MODEL
    t = _tile(s, 512)
    n = s // t

    def body(gl0, gl1, gl2, a0, a1, a2, w0, w1, w2, b_ref, dm_ref, dy0, dy1, dy2, dgl_ref, db_ref, acc_ref):
        step = pl.program_id(0)

        @pl.when(step == 0)
        def _():
            acc_ref[...] = jnp.zeros_like(acc_ref)

        dm = dm_ref[...]
        for i, (gl, a, w, dy) in enumerate(((gl0, a0, w0, dy0), (gl1, a1, w1, dy1), (gl2, a2, w2, dy2))):
            y = lax.dot_general(a[...], w[...], _DIMS["nn"], preferred_element_type=F32)
            gate = _sigmoid(gl[...].astype(F32) + b_ref[i:i + 1, :])
            dy[...] = (dm * gate).astype(dy.dtype)
            dgl = dm * y * (gate * (1.0 - gate))
            dgl_ref[:, i * d:(i + 1) * d] = dgl.astype(dgl_ref.dtype)
            acc_ref[i] += _row_sum8(dgl)

        @pl.when(step == n - 1)
        def _():
            rows = [jnp.sum(acc_ref[k], axis=0, keepdims=True) for k in range(3)]
            db_ref[...] = jnp.concatenate(rows + [jnp.zeros((5, d), F32)], axis=0)

    full = lambda arr: pl.BlockSpec(arr.shape, lambda i: (0, 0))
    row = pl.BlockSpec((t, d), lambda i: (i, 0))
    return pl.pallas_call(
        body, name=name, grid=(n,),
        in_specs=[pl.BlockSpec((t, d), lambda i, b=b: (i, b)) for b in range(3)]
                 + [pl.BlockSpec((t, a.shape[1]), lambda i: (i, 0)) for a in acts]
                 + [full(w) for w in ws] + [full(b_gate), row],
        out_specs=[row, row, row, pl.BlockSpec((t, 3 * d), lambda i: (i, 0)), pl.BlockSpec((8, d), lambda i: (0, 0))],
        out_shape=[jax.ShapeDtypeStruct((s, d), BF16)] * 3
                  + [jax.ShapeDtypeStruct((s, 3 * d), BF16), jax.ShapeDtypeStruct((8, d), F32)],
        scratch_shapes=[pltpu.VMEM((3, 8, d), F32)],
        compiler_params=_params(("arbitrary",)),
    )(h, h, h, *acts, *ws, b_gate, dmerged)


FF_BLK = D_FF // 2


def _ffn_act_fwd(h2, w, name):
    s = h2.shape[0]
    t = _tile(s, 512)
    r = t // HALO
    cw = 2 * FF_BLK

    def body(x_ref, xp_ref, w_ref, p_ref):
        i = pl.program_id(0)
        live = (i > 0).astype(F32)
        hc = _conv3(x_ref[...].astype(F32), xp_ref[...].astype(F32) * live, w_ref)
        p_ref[...] = (_gelu(hc[:, :FF_BLK]) * hc[:, FF_BLK:]).astype(p_ref.dtype)

    return pl.pallas_call(
        body, name=name, grid=(s // t, 2),
        in_specs=[pl.BlockSpec((t, cw), lambda i, j: (i, j)),
                  pl.BlockSpec((HALO, cw), lambda i, j: (jnp.maximum(i * r - 1, 0), j)),
                  pl.BlockSpec((8, cw), lambda i, j: (0, j))],
        out_specs=pl.BlockSpec((t, FF_BLK), lambda i, j: (i, j)),
        out_shape=jax.ShapeDtypeStruct((s, D_FF), BF16),
        compiler_params=_params(("parallel", "parallel")),
    )(h2, h2, w)


def _ffn_act_bwd(h2, w, dp, name):
    s = h2.shape[0]
    t = _tile(s, 512)
    r = t // HALO
    cw = 2 * FF_BLK

    def body(x_ref, xp_ref, w_ref, dp_ref, d_ref):
        i = pl.program_id(0)
        live = (i > 0).astype(F32)
        hc = _conv3(x_ref[...].astype(F32), xp_ref[...].astype(F32) * live, w_ref)
        ga, dga = _gelu_and_grad(hc[:, :FF_BLK])
        dpv = dp_ref[...].astype(F32)
        d_ref[:, :FF_BLK] = (dpv * hc[:, FF_BLK:] * dga).astype(d_ref.dtype)
        d_ref[:, FF_BLK:] = (dpv * ga).astype(d_ref.dtype)

    return pl.pallas_call(
        body, name=name, grid=(s // t, 2),
        in_specs=[pl.BlockSpec((t, cw), lambda i, j: (i, j)),
                  pl.BlockSpec((HALO, cw), lambda i, j: (jnp.maximum(i * r - 1, 0), j)),
                  pl.BlockSpec((8, cw), lambda i, j: (0, j)),
                  pl.BlockSpec((t, FF_BLK), lambda i, j: (i, j))],
        out_specs=pl.BlockSpec((t, cw), lambda i, j: (i, j)),
        out_shape=jax.ShapeDtypeStruct((s, 2 * D_FF), BF16),
        compiler_params=_params(("parallel", "parallel")),
    )(h2, h2, w, dp)


def _dwconv_bwd(x, w, dy, name):
    s, c = x.shape
    t = _tile(s, 512)
    n = s // t
    r = t // HALO
    nh = s // HALO
    cw = FF_BLK
    nc = c // cw

    def body(x_ref, xp_ref, dy_ref, dyn_ref, w_ref, dx_ref, dw_ref, acc_ref):
        i = pl.program_id(1)
        has_prev = (i > 0).astype(F32)
        has_next = (i < n - 1).astype(F32)
        xv = x_ref[...].astype(F32)
        xp = xp_ref[...].astype(F32) * has_prev
        dyv = dy_ref[...].astype(F32)
        dyn = dyn_ref[...].astype(F32) * has_next
        dx = (w_ref[2:3, :] * dyv + w_ref[1:2, :] * _shift_up(dyv, dyn, 1)
              + w_ref[0:1, :] * _shift_up(dyv, dyn, 2))
        dx_ref[...] = dx.astype(dx_ref.dtype)

        @pl.when(i == 0)
        def _():
            acc_ref[...] = jnp.zeros_like(acc_ref)

        acc_ref[0] += _row_sum8(dyv * _shift_down(xv, xp, 2))
        acc_ref[1] += _row_sum8(dyv * _shift_down(xv, xp, 1))
        acc_ref[2] += _row_sum8(dyv * xv)

        @pl.when(i == n - 1)
        def _():
            rows = [jnp.sum(acc_ref[k], axis=0, keepdims=True) for k in range(3)]
            dw_ref[...] = jnp.concatenate(rows + [jnp.zeros((5, cw), F32)], axis=0)

    return pl.pallas_call(
        body, name=name, grid=(nc, n),
        in_specs=[pl.BlockSpec((t, cw), lambda j, i: (i, j)),
                  pl.BlockSpec((HALO, cw), lambda j, i: (jnp.maximum(i * r - 1, 0), j)),
                  pl.BlockSpec((t, cw), lambda j, i: (i, j)),
                  pl.BlockSpec((HALO, cw), lambda j, i: (jnp.minimum((i + 1) * r, nh - 1), j)),
                  pl.BlockSpec((8, cw), lambda j, i: (0, j))],
        out_specs=[pl.BlockSpec((t, cw), lambda j, i: (i, j)), pl.BlockSpec((8, cw), lambda j, i: (0, j))],
        out_shape=[jax.ShapeDtypeStruct((s, c), BF16), jax.ShapeDtypeStruct((8, c), F32)],
        scratch_shapes=[pltpu.VMEM((3, 8, cw), F32)],
        compiler_params=_params(("parallel", "arbitrary")),
    )(x, x, dy, dy, w)


def _adamw(w, g, m, v, name):
    shape = w.shape
    c = shape[-1]
    rows = math.prod(shape[:-1])
    to2d = lambda a: a.reshape(rows, c)
    cap = max(8, (1 << 18) // c)
    tr = rows
    for cand in (2048, 1024, 512, 256, 128, 64, 32, 16, 8):
        if cand <= cap and rows % cand == 0:
            tr = cand
            break

    def body(w_ref, g_ref, m_ref, v_ref, d_ref, nm_ref, nv_ref):
        gv = g_ref[...]
        nm = ADAM_B1 * m_ref[...] + (1.0 - ADAM_B1) * gv
        nv = ADAM_B2 * v_ref[...] + (1.0 - ADAM_B2) * (gv * gv)
        m_hat = nm / (1.0 - ADAM_B1 ** ADAM_STEP)
        v_hat = nv / (1.0 - ADAM_B2 ** ADAM_STEP)
        d_ref[...] = -ADAM_LR * (m_hat / (jnp.sqrt(v_hat) + ADAM_EPS) + ADAM_WD * w_ref[...])
        nm_ref[...] = nm
        nv_ref[...] = nv

    blk = pl.BlockSpec((tr, c), lambda i: (i, 0))
    outs = pl.pallas_call(
        body, name=name, grid=(rows // tr,),
        in_specs=[blk] * 4, out_specs=[blk] * 3,
        out_shape=[jax.ShapeDtypeStruct((rows, c), F32)] * 3,
        compiler_params=_params(("parallel",)),
    )(to2d(w), to2d(g), to2d(m), to2d(v))
    return tuple(o.reshape(shape) for o in outs)


_ANY = pl.BlockSpec(memory_space=pl.ANY)


def _place():
    x, y, c = lax.axis_index("x"), lax.axis_index("y"), lax.axis_index("c")
    others = [(1 - x, y), (x, 1 - y), (1 - x, 1 - y)]
    return x, y, c, others


def _all_gather_chips(shard, name):
    rws, cols = shard.shape
    half = rws // 2

    def body(x_ref, out_ref, send_sems, recv_sems, local_sem):
        x, y, c, others = _place()
        me = 2 * x + y
        sib = (x, y, 1 - c)

        def rows(chip, cc):
            return out_ref.at[chip, pl.ds(pl.multiple_of(cc * half, 16), half), :]

        def copy(k, src, dst, to):
            return pltpu.make_async_remote_copy(src_ref=src, dst_ref=dst, send_sem=send_sems.at[k],
                                                recv_sem=recv_sems.at[k], device_id=to, device_id_type=MESH)

        mine = pltpu.make_async_copy(x_ref, out_ref.at[me], local_sem)
        mine.start()
        my_half = x_ref.at[pl.ds(pl.multiple_of(c * half, 16), half), :]
        first = [copy(j, my_half, rows(me, c), (ox, oy, c)) for j, (ox, oy) in enumerate(others)]
        for cp in first:
            cp.start()
        passed = []
        for j, (ox, oy) in enumerate(others):
            blk = rows(2 * ox + oy, c)
            copy(j, blk, blk, (x, y, c)).wait_recv()
            fwd = copy(3 + j, blk, blk, sib)
            fwd.start()
            passed.append(fwd)
        for j, (ox, oy) in enumerate(others):
            blk = rows(2 * ox + oy, 1 - c)
            copy(3 + j, blk, blk, (x, y, c)).wait_recv()
        for cp in first + passed:
            cp.wait_send()
        mine.wait()

    return pl.pallas_call(
        body, name=name,
        in_specs=[_ANY], out_specs=_ANY,
        out_shape=jax.ShapeDtypeStruct((N_CHIPS, rws, cols), shard.dtype),
        scratch_shapes=[pltpu.SemaphoreType.DMA((6,)), pltpu.SemaphoreType.DMA((6,)), pltpu.SemaphoreType.DMA],
        compiler_params=pltpu.CompilerParams(has_side_effects=True),
    )(shard)


def _swap_halves(buf, name):
    nb, rws, cols = buf.shape
    half = rws // 2

    def body(b_ref, own_ref, sib_ref, send_sem, recv_sem, local_sem):
        x, y, c, _ = _place()
        keep = b_ref.at[:, pl.ds(pl.multiple_of(c * half, 16), half), :]
        give = b_ref.at[:, pl.ds(pl.multiple_of((1 - c) * half, 16), half), :]
        mine = pltpu.make_async_copy(keep, own_ref, local_sem)
        mine.start()
        cp = pltpu.make_async_remote_copy(src_ref=give, dst_ref=sib_ref, send_sem=send_sem, recv_sem=recv_sem,
                                          device_id=(x, y, 1 - c), device_id_type=MESH)
        cp.start()
        cp.wait()
        mine.wait()

    shp = jax.ShapeDtypeStruct((nb, half, cols), buf.dtype)
    return pl.pallas_call(
        body, name=name,
        in_specs=[_ANY], out_specs=[_ANY, _ANY], out_shape=[shp, shp],
        scratch_shapes=[pltpu.SemaphoreType.DMA, pltpu.SemaphoreType.DMA, pltpu.SemaphoreType.DMA],
        compiler_params=pltpu.CompilerParams(has_side_effects=True),
    )(buf)


def _add2(a, b, name):
    nb, rws, cols = a.shape
    t = _tile(rws, 256)
    if rws % t:
        t = rws

    def body(a_ref, b_ref, o_ref):
        o_ref[...] = (a_ref[...].astype(F32) + b_ref[...].astype(F32)).astype(o_ref.dtype)

    blk = pl.BlockSpec((1, t, cols), lambda i, j: (i, j, 0))
    return pl.pallas_call(
        body, name=name, grid=(nb, rws // t), in_specs=[blk, blk], out_specs=blk,
        out_shape=jax.ShapeDtypeStruct(a.shape, a.dtype),
        compiler_params=_params(("parallel", "parallel")),
    )(a, b)


def _exchange_chips(pre, name):
    nb, half, cols = pre.shape

    def body(p_ref, out_ref, send_sems, recv_sems, local_sem):
        x, y, c, others = _place()
        me = 2 * x + y
        mine = pltpu.make_async_copy(p_ref.at[me], out_ref.at[me], local_sem)
        mine.start()
        sends = []
        for j, (ox, oy) in enumerate(others):
            cp = pltpu.make_async_remote_copy(src_ref=p_ref.at[2 * ox + oy], dst_ref=out_ref.at[me],
                                              send_sem=send_sems.at[j], recv_sem=recv_sems.at[j],
                                              device_id=(ox, oy, c), device_id_type=MESH)
            cp.start()
            sends.append(cp)
        for j, (ox, oy) in enumerate(others):
            blk = out_ref.at[2 * ox + oy]
            pltpu.make_async_remote_copy(src_ref=blk, dst_ref=blk, send_sem=send_sems.at[j],
                                         recv_sem=recv_sems.at[j], device_id=(x, y, c),
                                         device_id_type=MESH).wait_recv()
        for cp in sends:
            cp.wait_send()
        mine.wait()

    return pl.pallas_call(
        body, name=name,
        in_specs=[_ANY], out_specs=_ANY, out_shape=jax.ShapeDtypeStruct(pre.shape, pre.dtype),
        scratch_shapes=[pltpu.SemaphoreType.DMA((3,)), pltpu.SemaphoreType.DMA((3,)), pltpu.SemaphoreType.DMA],
        compiler_params=pltpu.CompilerParams(has_side_effects=True),
    )(pre)


def _add4(parts, name):
    nb, half, cols = parts.shape
    t = _tile(half, 256)
    if half % t:
        t = half

    def body(p_ref, o_ref):
        acc = p_ref[0].astype(F32)
        for k in range(1, nb):
            acc = acc + p_ref[k].astype(F32)
        o_ref[...] = acc

    return pl.pallas_call(
        body, name=name, grid=(half // t,),
        in_specs=[pl.BlockSpec((nb, t, cols), lambda i: (0, i, 0))],
        out_specs=pl.BlockSpec((t, cols), lambda i: (i, 0)),
        out_shape=jax.ShapeDtypeStruct((half, cols), F32),
        compiler_params=_params(("parallel",)),
    )(parts)


def _join_halves(mine_half, name):
    half, cols = mine_half.shape

    def body(h_ref, out_ref, send_sem, recv_sem, local_sem):
        x, y, c, _ = _place()
        dst = out_ref.at[pl.ds(pl.multiple_of(c * half, 8), half), :]
        mine = pltpu.make_async_copy(h_ref, dst, local_sem)
        mine.start()
        cp = pltpu.make_async_remote_copy(src_ref=h_ref, dst_ref=dst, send_sem=send_sem, recv_sem=recv_sem,
                                          device_id=(x, y, 1 - c), device_id_type=MESH)
        cp.start()
        cp.wait()
        mine.wait()

    return pl.pallas_call(
        body, name=name,
        in_specs=[_ANY], out_specs=_ANY, out_shape=jax.ShapeDtypeStruct((2 * half, cols), mine_half.dtype),
        scratch_shapes=[pltpu.SemaphoreType.DMA, pltpu.SemaphoreType.DMA, pltpu.SemaphoreType.DMA],
        compiler_params=pltpu.CompilerParams(has_side_effects=True),
    )(mine_half)


def _reduce_scatter_chips(buf, tag):
    own, sib = _swap_halves(buf, "rs_swap_" + tag)
    pre = _add2(own, sib, "rs_add2_" + tag)
    parts = _exchange_chips(pre, "rs_xchg_" + tag)
    red = _add4(parts, "rs_add4_" + tag)
    return _join_halves(red, "rs_join_" + tag)


MAX_DMA_BYTES = 2 * 1024 * 1024
ROW_ALIGN = 16


def _pieces(rows, row_bytes):
    n = max(1, -(-(rows * row_bytes) // MAX_DMA_BYTES))
    step = -(-(-(-rows // n)) // ROW_ALIGN) * ROW_ALIGN
    return [(r, min(step, rows - r)) for r in range(0, rows, step)]


def _half_plan(arrays, row_axis):
    plan = []
    for a, arr in enumerate(arrays):
        row_bytes = math.prod(arr.shape[row_axis + 1:]) * arr.dtype.itemsize * (arr.shape[0] if row_axis else 1)
        plan += [(a, r0, nr) for r0, nr in _pieces(arr.shape[row_axis] // 2, row_bytes)]
    return plan


def _rows(start, size):
    return pl.ds(pl.multiple_of(start, ROW_ALIGN), size)


def _remote(src, dst, send_sems, recv_sems, k, to):
    return pltpu.make_async_remote_copy(src_ref=src, dst_ref=dst, send_sem=send_sems.at[k], recv_sem=recv_sems.at[k],
                                        device_id=to, device_id_type=MESH)


def _comm_call(body, name, ins, out_shapes, n_remote, n_local, aliases=None):
    return pl.pallas_call(
        body, name=name,
        in_specs=[_ANY] * len(ins), out_specs=[_ANY] * len(out_shapes), out_shape=out_shapes,
        scratch_shapes=[pltpu.SemaphoreType.DMA((n_remote,)), pltpu.SemaphoreType.DMA((n_remote,)),
                        pltpu.SemaphoreType.DMA((max(n_local, 1),))],
        input_output_aliases=aliases or {},
        compiler_params=pltpu.CompilerParams(has_side_effects=True),
    )(*ins)


def _cast_shard(w, l, me_idx, name):
    _, k, cols = w.shape
    tr = _tile(k, 256)
    if k % tr:
        tr = k

    def body(me_ref, w_ref, s_ref, land_ref):
        del me_ref
        v = w_ref[...].astype(BF16)
        s_ref[...] = v
        land_ref[...] = v

    grid_spec = pltpu.PrefetchScalarGridSpec(
        num_scalar_prefetch=1, grid=(k // tr,),
        in_specs=[pl.BlockSpec((None, tr, cols), lambda i, me: (l, i, 0))],
        out_specs=[pl.BlockSpec((tr, cols), lambda i, me: (i, 0)),
                   pl.BlockSpec((None, tr, cols), lambda i, me: (me[0], i, 0))])
    return pl.pallas_call(
        body, name=name, grid_spec=grid_spec,
        out_shape=[jax.ShapeDtypeStruct((k, cols), BF16), jax.ShapeDtypeStruct((N_CHIPS, k, cols), BF16)],
        compiler_params=_params(("parallel",)),
    )(me_idx, w)


def _gather_d2d(lands, name):
    n = len(lands)
    plan = _half_plan(lands, 1)
    plan = [(a, r0, nr) for a, r0, nr in plan]

    def body(*refs):
        out_refs = refs[n:2 * n]
        send_sems, recv_sems, _ = refs[2 * n:]
        x, y, c, others = _place()
        sends = []
        for i, (a, r0, nr) in enumerate(plan):
            rows = _rows(c * (lands[a].shape[1] // 2) + r0, nr)
            for j, (ox, oy) in enumerate(others):
                blk = out_refs[a].at[2 * ox + oy, rows, :]
                cp = _remote(blk, blk, send_sems, recv_sems, 3 * i + j, (x, y, 1 - c))
                cp.start()
                sends.append(cp)
        for i, (a, r0, nr) in enumerate(plan):
            rows = _rows((1 - c) * (lands[a].shape[1] // 2) + r0, nr)
            for j, (ox, oy) in enumerate(others):
                blk = out_refs[a].at[2 * ox + oy, rows, :]
                _remote(blk, blk, send_sems, recv_sems, 3 * i + j, (x, y, c)).wait_recv()
        for cp in sends:
            cp.wait_send()

    outs = [jax.ShapeDtypeStruct(a.shape, a.dtype) for a in lands]
    return _comm_call(body, name, lands, outs, 3 * len(plan), 0, aliases={a: a for a in range(n)})


def _rs_swap(ts, name):
    n = len(ts)
    plan = _half_plan(ts, 1)

    def body(*refs):
        t_refs, out_refs = refs[:n], refs[n:2 * n]
        send_sems, recv_sems, _ = refs[2 * n:]
        x, y, c, _o = _place()
        sends = []
        for i, (a, r0, nr) in enumerate(plan):
            src = t_refs[a].at[:, _rows((1 - c) * (ts[a].shape[1] // 2) + r0, nr), :]
            cp = _remote(src, out_refs[a].at[:, pl.ds(r0, nr), :], send_sems, recv_sems, i, (x, y, 1 - c))
            cp.start()
            sends.append(cp)
        for i, (a, r0, nr) in enumerate(plan):
            blk = out_refs[a].at[:, pl.ds(r0, nr), :]
            _remote(blk, blk, send_sems, recv_sems, i, (x, y, c)).wait_recv()
        for cp in sends:
            cp.wait_send()

    outs = [jax.ShapeDtypeStruct((t.shape[0], t.shape[1] // 2, t.shape[2]), t.dtype) for t in ts]
    return _comm_call(body, name, ts, outs, len(plan), 0)


def _add_half(t, got, c_idx, me_idx, name):
    nb, k, cols = t.shape
    half = k // 2

    def body(c_ref, me_ref, t_ref, g_ref, o_ref, mine_ref):
        del c_ref
        v = (t_ref[...].astype(F32) + g_ref[...].astype(F32)).astype(o_ref.dtype)
        o_ref[...] = v

        @pl.when(pl.program_id(0) == me_ref[0])
        def _():
            mine_ref[...] = v

    blk = pl.BlockSpec((1, half, cols), lambda i, c, me: (i, 0, 0))
    grid_spec = pltpu.PrefetchScalarGridSpec(
        num_scalar_prefetch=2, grid=(nb,),
        in_specs=[pl.BlockSpec((1, half, cols), lambda i, c, me: (i, c[0], 0)), blk],
        out_specs=[blk, pl.BlockSpec((1, half, cols), lambda i, c, me: (me[0], 0, 0))])
    shp = jax.ShapeDtypeStruct(got.shape, got.dtype)
    return pl.pallas_call(
        body, name=name, grid_spec=grid_spec, out_shape=[shp, shp],
        compiler_params=_params(("arbitrary",)),
    )(c_idx, me_idx, t, got)


def _add4_half(parts, c_idx, name):
    nb, half, cols = parts.shape
    t = _tile(half, 256)
    if half % t:
        t = half
    steps = half // t

    def body(c_ref, p_ref, o_ref):
        del c_ref
        acc = p_ref[0].astype(F32)
        for k in range(1, nb):
            acc = acc + p_ref[k].astype(F32)
        o_ref[...] = acc

    grid_spec = pltpu.PrefetchScalarGridSpec(
        num_scalar_prefetch=1, grid=(steps,),
        in_specs=[pl.BlockSpec((nb, t, cols), lambda i, c: (0, i, 0))],
        out_specs=pl.BlockSpec((t, cols), lambda i, c: (c[0] * steps + i, 0)))
    return pl.pallas_call(
        body, name=name, grid_spec=grid_spec, out_shape=jax.ShapeDtypeStruct((2 * half, cols), F32),
        compiler_params=_params(("parallel",)),
    )(c_idx, parts)


def _rs_join(fulls, name):
    n = len(fulls)
    plan = _half_plan(fulls, 0)

    def body(*refs):
        out_refs = refs[n:2 * n]
        send_sems, recv_sems, _ = refs[2 * n:]
        x, y, c, _o = _place()
        sends = []
        for i, (a, r0, nr) in enumerate(plan):
            blk = out_refs[a].at[_rows(c * (fulls[a].shape[0] // 2) + r0, nr), :]
            cp = _remote(blk, blk, send_sems, recv_sems, i, (x, y, 1 - c))
            cp.start()
            sends.append(cp)
        for i, (a, r0, nr) in enumerate(plan):
            blk = out_refs[a].at[_rows((1 - c) * (fulls[a].shape[0] // 2) + r0, nr), :]
            _remote(blk, blk, send_sems, recv_sems, i, (x, y, c)).wait_recv()
        for cp in sends:
            cp.wait_send()

    outs = [jax.ShapeDtypeStruct(f.shape, f.dtype) for f in fulls]
    return _comm_call(body, name, fulls, outs, len(plan), 0, aliases={a: a for a in range(n)})


_HBM = pl.BlockSpec(memory_space=pltpu.HBM)
_SEM = pl.BlockSpec(memory_space=pltpu.SEMAPHORE)
_EFFECT = pltpu.SideEffectType.DATAFLOW_SIDE_EFFECTING


def _ici_plan(kind, a_list):
    if kind == "gather":
        return _half_plan(a_list, 0)
    plan = []
    for a, p in enumerate(a_list):
        plan += [(a, r0, nr) for r0, nr in _pieces(p.shape[1], p.shape[2] * p.dtype.itemsize)]
    return plan


def _ici_refs(kind, a_ref, b_ref, a_shape, r0, nr, c, me, peer):
    if kind == "gather":
        rows = _rows(c * (a_shape[0] // 2) + r0, nr)
        return a_ref.at[rows, :], b_ref.at[me, rows, :], b_ref.at[peer, rows, :]
    rows = pl.ds(r0, nr)
    return a_ref.at[peer, rows, :], b_ref.at[me, rows, :], b_ref.at[peer, rows, :]


def _ici_start(kind, a_list, b_list, name):
    n = len(a_list)
    plan = _ici_plan(kind, a_list)
    shapes = [a.shape for a in a_list]

    def body(*refs):
        a_refs, b_refs = refs[:n], refs[n:2 * n]
        send_sems, recv_sems = refs[2 * n], refs[2 * n + 1]
        token = refs[4 * n + 2]
        x, y, c, others = _place()
        me = 2 * x + y
        for i, (a, r0, nr) in enumerate(plan):
            for j, (ox, oy) in enumerate(others):
                src, dst, _ = _ici_refs(kind, a_refs[a], b_refs[a], shapes[a], r0, nr, c, me, 2 * ox + oy)
                _remote(src, dst, send_sems, recv_sems, 3 * i + j, (ox, oy, c)).start()
        token[...] = jnp.zeros_like(token)

    hbm = lambda v: pltpu.HBM(v.shape, v.dtype)
    ncp = 3 * len(plan)
    outs = pl.pallas_call(
        body, name=name,
        in_specs=[_HBM] * (2 * n),
        out_specs=[_SEM, _SEM] + [_HBM] * (2 * n) + [pl.BlockSpec(memory_space=pltpu.VMEM)],
        out_shape=[pltpu.SemaphoreType.DMA((ncp,)), pltpu.SemaphoreType.DMA((ncp,))]
                  + [hbm(v) for v in a_list] + [hbm(v) for v in b_list] + [jax.ShapeDtypeStruct((8, LANES), F32)],
        input_output_aliases={i: 2 + i for i in range(2 * n)},
        compiler_params=pltpu.CompilerParams(has_side_effects=_EFFECT),
    )(*[pltpu.with_memory_space_constraint(v, pltpu.HBM) for v in list(a_list) + list(b_list)])
    return outs[0], outs[1], outs[2:2 + n], outs[2 + n:2 + 2 * n], outs[2 + 2 * n]


def _ici_wait(kind, started, after, name):
    send_sems, recv_sems, a_list, b_list, _ = started
    n = len(a_list)
    plan = _ici_plan(kind, a_list)
    shapes = [a.shape for a in a_list]

    def body(*refs):
        a_refs, b_refs = refs[:n], refs[n:2 * n]
        send_sems, recv_sems = refs[2 * n], refs[2 * n + 1]
        x, y, c, others = _place()
        me = 2 * x + y
        for i, (a, r0, nr) in enumerate(plan):
            for j, (ox, oy) in enumerate(others):
                src, dst, land = _ici_refs(kind, a_refs[a], b_refs[a], shapes[a], r0, nr, c, me, 2 * ox + oy)
                _remote(src, dst, send_sems, recv_sems, 3 * i + j, (ox, oy, c)).wait_send()
                _remote(land, land, send_sems, recv_sems, 3 * i + j, (x, y, c)).wait_recv()

    hbm = lambda v: pltpu.HBM(v.shape, v.dtype)
    outs = pl.pallas_call(
        body, name=name,
        in_specs=[_HBM] * (2 * n) + [_SEM, _SEM, _ANY],
        out_specs=[_HBM] * (2 * n),
        out_shape=[hbm(v) for v in a_list] + [hbm(v) for v in b_list],
        input_output_aliases={i: i for i in range(2 * n)},
        compiler_params=pltpu.CompilerParams(has_side_effects=_EFFECT),
    )(*a_list, *b_list, send_sems, recv_sems, after)
    return outs[n:]


def _rs_begin(ts, c_idx, me_idx, tag):
    got = _rs_swap(ts, "rs_swap_" + tag)
    pairs = [_add_half(t, g, c_idx, me_idx, f"rs_add2_{tag}_{a}") for a, (t, g) in enumerate(zip(ts, got))]
    return _ici_start("scatter", [p for p, _ in pairs], [m for _, m in pairs], "rs_xchg_start_" + tag)


def _rs_finish(started, after, c_idx, tag):
    parts = _ici_wait("scatter", started, after, "rs_xchg_wait_" + tag)
    fulls = [_add4_half(p, c_idx, f"rs_add4_{tag}_{a}") for a, p in enumerate(parts)]
    return _rs_join(fulls, "rs_join_" + tag)


def _pack_rows(pieces, rows, dtype):
    flat = jnp.concatenate([p.astype(dtype).reshape(-1) for p in pieces])
    return jnp.pad(flat, (0, rows * PACK_COLS - flat.shape[0])).reshape(rows, PACK_COLS)


def _unpack(flat, shapes):
    out, off = [], 0
    for shp in shapes:
        size = math.prod(shp)
        out.append(flat[off:off + size].reshape(shp))
        off += size
    return out


def _rows_for(n_elems, mult):
    rows = -(-n_elems // PACK_COLS)
    return -(-rows // mult) * mult


BIG_SHARDS = [("w_in", (D_MODEL, 1474)), ("w_branch_att", (D_ATT, 256)), ("w_branch_conv", (D_CONV, 256)),
              ("w_branch_sgu", (D_SGU, 256)), ("w_out", (256, D_MODEL)), ("w_ffn_up", (D_MODEL, FF_BLK)),
              ("w_ffn_down", (D_FF // N_CHIPS, D_MODEL))]
SMALL_SHARDS = [("b_gate", (3, 256)), ("conv_mix_w", (3, 64)), ("conv_ffn_w", (3, FF_BLK))]
REPLICATED = [("pre_mix_g", (D_MODEL,)), ("post_mix_g", (D_MODEL,)), ("pre_ffn_g", (D_MODEL,)),
              ("post_ffn_g", (D_MODEL,)), ("b_forget", (N_HEADS,)), ("sgu_ln_g", (D_SGU,)), ("sgu_ln_b", (D_SGU,)),
              ("sgu_w", (N_GROUPS, CHUNK, CHUNK)), ("sgu_b", (N_GROUPS, CHUNK))]
WEIGHT_ORDER = ["pre_mix_g", "post_mix_g", "pre_ffn_g", "post_ffn_g", "w_in", "b_forget", "b_gate", "conv_mix_w",
                "sgu_ln_g", "sgu_ln_b", "sgu_w", "sgu_b", "w_branch_att", "w_branch_conv", "w_branch_sgu", "w_out",
                "w_ffn_up", "conv_ffn_w", "w_ffn_down"]

_SMALL_ELEMS = sum(math.prod(s) for _, s in SMALL_SHARDS)
_REP_ELEMS = sum(math.prod(s) for _, s in REPLICATED)
_REP_QUARTER = -(-(DEPTH * _REP_ELEMS) // N_CHIPS)
SMALL_PARAM_ROWS = _rows_for(DEPTH * _SMALL_ELEMS, 32)
SMALL_ROWS = _rows_for(DEPTH * _SMALL_ELEMS + _REP_QUARTER, 32)
IN_WIDTH = 5896
IN_SHARD = IN_WIDTH // N_CHIPS
IN_SHARD_PAD = 1536
IN_PAD = 6144


def _gather_small(wts):
    shard = _pack_rows([wts[n] for n, _ in SMALL_SHARDS], SMALL_PARAM_ROWS, F32)
    full = _all_gather_chips(shard, "gather_small_params").reshape(N_CHIPS, -1)
    per_chip = [_unpack(full[j], [(DEPTH,) + s for _, s in SMALL_SHARDS]) for j in range(N_CHIPS)]
    return {n: jnp.concatenate([per_chip[j][i] for j in range(N_CHIPS)], axis=-1)
            for i, (n, _) in enumerate(SMALL_SHARDS)}


def _gather_begin(wts, l, me_idx):
    cast = [_cast_shard(wts[n], l, me_idx, "cast_" + n) for n, _ in BIG_SHARDS]
    return _ici_start("gather", [sh for sh, _ in cast], [ld for _, ld in cast], "gather_ici_start")


def _gather_finish(started, after):
    lands = _ici_wait("gather", started, after, "gather_ici_wait")
    return dict(zip([n for n, _ in BIG_SHARDS], _gather_d2d(lands, "gather_d2d")))


def _pad_rows(a, rows):
    return jnp.pad(a, ((0, rows - a.shape[0]), (0, 0)))


def _whole_cols(land):
    return land.transpose(1, 0, 2).reshape(land.shape[1], -1)


_O_F = 3 * D_ATT
_O_B = _O_F + N_HEADS
_O_GL = _O_B + 3 * D_CONV + 2 * D_SGU


def _prep_layer(wts, lands, small, l):
    w_in = _whole_cols(lands["w_in"])
    up = lands["w_ffn_up"]
    cf = small["conv_ffn_w"][l]
    blk = lambda a, j: a[:, j * FF_BLK:(j + 1) * FF_BLK]
    return {
        "w_p": jnp.concatenate([w_in[:, _O_GL:], w_in[:, :_O_F], w_in[:, _O_B:_O_GL]], axis=1),
        "w_in_bwd": jnp.concatenate([w_in[:, :_O_F], w_in[:, _O_B:], w_in[:, _O_F:_O_B],
                                     jnp.zeros((D_MODEL, IN_PAD - IN_WIDTH), BF16)], axis=1),
        "wf_t": _pad_rows(w_in[:, _O_F:_O_B].T, F_ROWS),
        "b_forget": _pad_rows(wts["b_forget"][l].reshape(N_HEADS, 1), F_ROWS),
        "b_gate": _pad_rows(small["b_gate"][l], 8),
        "conv_mix_w": _pad_rows(small["conv_mix_w"][l], 8),
        "w_att": _whole_cols(lands["w_branch_att"]), "w_conv": _whole_cols(lands["w_branch_conv"]),
        "w_sgu": _whole_cols(lands["w_branch_sgu"]),
        "w_out": lands["w_out"].reshape(D_MODEL, D_MODEL),
        "w_up": jnp.concatenate([up[0], up[2], up[1], up[3]], axis=1),
        "conv_ffn_w": _pad_rows(jnp.concatenate([blk(cf, 0), blk(cf, 2), blk(cf, 1), blk(cf, 3)], axis=1), 8),
        "w_down": lands["w_ffn_down"].reshape(D_FF, D_MODEL),
        "pre_mix_g": wts["pre_mix_g"][l].reshape(1, -1), "post_mix_g": wts["post_mix_g"][l].reshape(1, -1),
        "pre_ffn_g": wts["pre_ffn_g"][l].reshape(1, -1), "post_ffn_g": wts["post_ffn_g"][l].reshape(1, -1),
        "ln_g": wts["sgu_ln_g"][l].reshape(1, -1), "ln_b": wts["sgu_ln_b"][l].reshape(1, -1),
        "sgu_w": wts["sgu_w"][l],
        "sgu_bias": jnp.repeat(wts["sgu_b"][l].T, HEAD_DIM, axis=1),
    }


def _layer_fwd(x, p, dep=None):
    s = x.shape[0]
    xn = _rms_fwd(x, p["pre_mix_g"], "rms_pre_mix", dep)
    h = _mm(xn, p["w_p"], "nn", BF16, "mm_in", s, 256, D_MODEL)
    f_row = _mm(p["wf_t"], xn, "nt", F32, "mm_forget", F_ROWS, 2048, D_MODEL)
    c, ck = _gate_fwd(f_row, p["b_forget"], "gate_fwd")
    o, o_f32, lse = _attn_fwd(h, ck, "attn_fwd")
    yc = _sconv_fwd(h, p["conv_mix_w"], "sconv_fwd")
    ys = _sgu_fwd(h, p["ln_g"], p["ln_b"], p["sgu_w"], p["sgu_bias"], "sgu_fwd")
    merged = _merge_fwd(h, (o, yc, ys), (p["w_att"], p["w_conv"], p["w_sgu"]), p["b_gate"], "merge_fwd")
    mo = _mm(merged, p["w_out"], "nn", F32, "mm_out", 2048, 512, D_MODEL)
    x1 = _resid_post(x, mo, p["post_mix_g"], "post_mix")
    xn2 = _rms_fwd(x1, p["pre_ffn_g"], "rms_pre_ffn")
    h2 = _mm(xn2, p["w_up"], "nn", BF16, "mm_up", 2048, 512, D_MODEL)
    pact = _ffn_act_fwd(h2, p["conv_ffn_w"], "ffn_act_fwd")
    ff = _mm(pact, p["w_down"], "nn", F32, "mm_down", 2048, 512, FF_BLK)
    x2 = _resid_post(x1, ff, p["post_ffn_g"], "post_ffn")
    saved = dict(x=x, xn=xn, h=h, f_row=f_row, c=c, o=o, o_f32=o_f32, lse=lse, yc=yc, ys=ys, merged=merged, mo=mo, x1=x1,
                 xn2=xn2, h2=h2, pact=pact, ff=ff)
    return x2, saved


def _layer_bwd(dx2, p, sv, dep=None):
    s = dx2.shape[0]
    g = {}
    same = lambda b: b
    dff, g["post_ffn_g"] = _rms_bwd(sv["ff"], p["post_ffn_g"], [dx2], None, BF16, "post_ffn_bwd", dep)
    dpact = _mm(dff, p["w_down"], "nt", BF16, "mm_down_dx", 1024, FF_BLK, D_MODEL)
    t_down = _mm(sv["pact"], dff, "tn", BF16, "mm_down_dw", 256, D_MODEL, s).reshape(N_CHIPS, -1, D_MODEL)
    dhc2 = _ffn_act_bwd(sv["h2"], p["conv_ffn_w"], dpact, "ffn_act_bwd")
    dh2, dconv_ffn = _dwconv_bwd(sv["h2"], p["conv_ffn_w"], dhc2, "ffn_conv_bwd")
    dxn2 = _mm(dh2, p["w_up"], "nt", F32, "mm_up_dx", 1024, D_MODEL, FF_BLK)
    t_up = _mm(sv["xn2"], dh2, "tn", BF16, "mm_up_dw", 512, FF_BLK, s, chip_of=lambda b: (b % 2) * 2 + b // 2)
    dx1, g["pre_ffn_g"] = _rms_bwd(sv["x1"], p["pre_ffn_g"], [dxn2], dx2, F32, "pre_ffn_bwd")
    dmo, g["post_mix_g"] = _rms_bwd(sv["mo"], p["post_mix_g"], [dx1], None, BF16, "post_mix_bwd")
    dmerged = _mm(dmo, p["w_out"], "nt", F32, "mm_out_dx", 2048, 512, D_MODEL)
    t_out = _mm(sv["merged"], dmo, "tn", BF16, "mm_out_dw", 512, D_MODEL, s).reshape(N_CHIPS, -1, D_MODEL)
    acts = (sv["o"], sv["yc"], sv["ys"])
    ws = (p["w_att"], p["w_conv"], p["w_sgu"])
    dy_a, dy_c, dy_s, dgl, db_gate = _merge_bwd(sv["h"], acts, ws, p["b_gate"], dmerged, "merge_bwd")
    do = _mm(dy_a, p["w_att"], "nt", BF16, "mm_att_dx", 2048, D_ATT, D_MODEL)
    dyc = _mm(dy_c, p["w_conv"], "nt", BF16, "mm_conv_dx", 2048, D_CONV, D_MODEL)
    dys = _mm(dy_s, p["w_sgu"], "nt", BF16, "mm_sgu_dx", 2048, D_SGU, D_MODEL)
    t_att = _mm(sv["o"], dy_a, "tn", BF16, "mm_att_dw", D_ATT, 256, s, chip_of=same)
    t_conv = _mm(sv["yc"], dy_c, "tn", BF16, "mm_conv_dw", D_CONV, 256, s, chip_of=same)
    t_sgu = _mm(sv["ys"], dy_s, "tn", BF16, "mm_sgu_dw", D_SGU, 256, s, chip_of=same)
    d_conv, dconv_mix = _sconv_bwd(sv["h"], p["conv_mix_w"], dyc, "sconv_bwd")
    d_sgu, g["sgu_ln_g"], g["sgu_ln_b"], g["sgu_w"], dbias = _sgu_bwd(
        sv["h"], p["ln_g"], p["ln_b"], p["sgu_w"], p["sgu_bias"], dys, "sgu_bwd")
    dq, dk, dv, dc_even, dc_odd = _attn_bwd(sv["h"], sv["c"], sv["o_f32"], sv["lse"], do, "attn_bwd")
    df, db_forget = _gate_bwd(sv["f_row"], p["b_forget"], dc_even, dc_odd, "gate_bwd")
    f_cols = jnp.concatenate([df[:N_HEADS].T, jnp.zeros((s, IN_PAD - IN_WIDTH), BF16)], axis=1)
    dh = jnp.concatenate([dq.astype(BF16), dk, dv, d_conv, d_sgu, dgl, f_cols], axis=1)
    dxn = _mm(dh, p["w_in_bwd"], "nt", F32, "mm_in_dx", 512, D_MODEL, 2048)
    dw_bwd = _mm(sv["xn"], dh, "tn", F32, "mm_in_dw", D_MODEL, 512, s)
    n_rest = IN_WIDTH - N_HEADS
    dw_in = jnp.concatenate([dw_bwd[:, :_O_F], dw_bwd[:, n_rest:IN_WIDTH], dw_bwd[:, _O_F:n_rest],
                             jnp.zeros((D_MODEL, IN_SHARD_PAD - IN_SHARD), F32)], axis=1)
    t_in = jnp.stack([dw_in[:, j * IN_SHARD:j * IN_SHARD + IN_SHARD_PAD] for j in range(N_CHIPS)]).astype(BF16)
    dx, g["pre_mix_g"] = _rms_bwd(sv["x"], p["pre_mix_g"], [dxn], dx1, F32, "pre_mix_bwd")
    blk = lambda a, j: a[:, j * FF_BLK:(j + 1) * FF_BLK]
    g["conv_ffn_w"] = jnp.concatenate([blk(dconv_ffn, 0), blk(dconv_ffn, 2), blk(dconv_ffn, 1),
                                       blk(dconv_ffn, 3)], axis=1)[:3]
    g["conv_mix_w"] = dconv_mix[:3]
    g["b_gate"] = db_gate[:3]
    g["b_forget"] = db_forget[:N_HEADS, 0]
    g["sgu_b"] = jnp.sum(dbias.reshape(CHUNK, N_GROUPS, HEAD_DIM), axis=-1).T
    for n in ("pre_mix_g", "post_mix_g", "pre_ffn_g", "post_ffn_g", "sgu_ln_g", "sgu_ln_b"):
        g[n] = g[n].reshape(-1)
    return dx, [t_in, t_att, t_conv, t_sgu, t_out, t_up, t_down], g


def _shard_cols(a, j):
    w = a.shape[-1] // N_CHIPS
    return a[..., j * w:(j + 1) * w]


def kernel(x, pre_mix_g, post_mix_g, pre_ffn_g, post_ffn_g, w_in, b_forget, b_gate, conv_mix_w, sgu_ln_g, sgu_ln_b, sgu_w, sgu_b, w_branch_att, w_branch_conv, w_branch_sgu, w_out, w_ffn_up, conv_ffn_w, w_ffn_down, loss_target, m_pre_mix_g, m_post_mix_g, m_pre_ffn_g, m_post_ffn_g, m_w_in, m_b_forget, m_b_gate, m_conv_mix_w, m_sgu_ln_g, m_sgu_ln_b, m_sgu_w, m_sgu_b, m_w_branch_att, m_w_branch_conv, m_w_branch_sgu, m_w_out, m_w_ffn_up, m_conv_ffn_w, m_w_ffn_down, v_pre_mix_g, v_post_mix_g, v_pre_ffn_g, v_post_ffn_g, v_w_in, v_b_forget, v_b_gate, v_conv_mix_w, v_sgu_ln_g, v_sgu_ln_b, v_sgu_w, v_sgu_b, v_w_branch_att, v_w_branch_conv, v_w_branch_sgu, v_w_out, v_w_ffn_up, v_conv_ffn_w, v_w_ffn_down):
    wts = dict(pre_mix_g=pre_mix_g, post_mix_g=post_mix_g, pre_ffn_g=pre_ffn_g, post_ffn_g=post_ffn_g, w_in=w_in,
               b_forget=b_forget, b_gate=b_gate, conv_mix_w=conv_mix_w, sgu_ln_g=sgu_ln_g, sgu_ln_b=sgu_ln_b,
               sgu_w=sgu_w, sgu_b=sgu_b, w_branch_att=w_branch_att, w_branch_conv=w_branch_conv,
               w_branch_sgu=w_branch_sgu, w_out=w_out, w_ffn_up=w_ffn_up, conv_ffn_w=conv_ffn_w,
               w_ffn_down=w_ffn_down)
    moms = dict(pre_mix_g=m_pre_mix_g, post_mix_g=m_post_mix_g, pre_ffn_g=m_pre_ffn_g, post_ffn_g=m_post_ffn_g,
                w_in=m_w_in, b_forget=m_b_forget, b_gate=m_b_gate, conv_mix_w=m_conv_mix_w, sgu_ln_g=m_sgu_ln_g,
                sgu_ln_b=m_sgu_ln_b, sgu_w=m_sgu_w, sgu_b=m_sgu_b, w_branch_att=m_w_branch_att,
                w_branch_conv=m_w_branch_conv, w_branch_sgu=m_w_branch_sgu, w_out=m_w_out, w_ffn_up=m_w_ffn_up,
                conv_ffn_w=m_conv_ffn_w, w_ffn_down=m_w_ffn_down)
    vels = dict(pre_mix_g=v_pre_mix_g, post_mix_g=v_post_mix_g, pre_ffn_g=v_pre_ffn_g, post_ffn_g=v_post_ffn_g,
                w_in=v_w_in, b_forget=v_b_forget, b_gate=v_b_gate, conv_mix_w=v_conv_mix_w, sgu_ln_g=v_sgu_ln_g,
                sgu_ln_b=v_sgu_ln_b, sgu_w=v_sgu_w, sgu_b=v_sgu_b, w_branch_att=v_w_branch_att,
                w_branch_conv=v_w_branch_conv, w_branch_sgu=v_w_branch_sgu, w_out=v_w_out, w_ffn_up=v_w_ffn_up,
                conv_ffn_w=v_conv_ffn_w, w_ffn_down=v_w_ffn_down)

    c_idx = lax.axis_index("c").astype(jnp.int32).reshape(1)
    me_idx = (2 * lax.axis_index("x") + lax.axis_index("y")).astype(jnp.int32).reshape(1)
    small = _gather_small(wts)

    xs = x[0]
    layers, saved = [], []
    lands = _gather_finish(_gather_begin(wts, 0, me_idx), xs)
    for l in range(DEPTH):
        p = _prep_layer(wts, lands, small, l)
        nxt = _gather_begin(wts, l + 1, me_idx) if l + 1 < DEPTH else None
        xs, sv = _layer_fwd(xs, p, nxt[4] if nxt else None)
        if nxt:
            lands = _gather_finish(nxt, xs)
        layers.append(p)
        saved.append(sv)
    dy, loss_part = _loss_head(xs, loss_target[0], "loss_head")
    loss = lax.psum(loss_part[0, 0], ("x", "y", "c"))

    big_red = [None] * DEPTH
    small_grads = [None] * DEPTH
    pending = None
    for l in reversed(range(DEPTH)):
        dy, ts, small_grads[l] = _layer_bwd(dy, layers[l], saved[l], pending[4] if pending else None)
        if pending:
            big_red[l + 1] = _rs_finish(pending, dy, c_idx, "big")
        pending = _rs_begin(ts, c_idx, me_idx, "big")
    big_red[0] = _rs_finish(pending, dy, c_idx, "big")
    grad_x = dy[None]

    rep_flat = jnp.concatenate([small_grads[l][n].reshape(-1) for l in range(DEPTH) for n, _ in REPLICATED])
    rep_flat = jnp.pad(rep_flat, (0, N_CHIPS * _REP_QUARTER - rep_flat.shape[0]))
    rows = []
    for j in range(N_CHIPS):
        pieces = [_shard_cols(small_grads[l][n], j) for l in range(DEPTH) for n, _ in SMALL_SHARDS]
        pieces.append(rep_flat[j * _REP_QUARTER:(j + 1) * _REP_QUARTER])
        rows.append(_pack_rows(pieces, SMALL_ROWS, F32))
    small_red = _reduce_scatter_chips(jnp.stack(rows), "small")
    small_all = _all_gather_chips(small_red, "gather_small").reshape(N_CHIPS, -1)

    grads = {}
    for i, (n, _) in enumerate(BIG_SHARDS):
        grads[n] = jnp.stack([big_red[l][i][:, :IN_SHARD] if n == "w_in" else big_red[l][i] for l in range(DEPTH)])
    mine_small = small_red.reshape(-1)
    parts = _unpack(mine_small, [s for _ in range(DEPTH) for _, s in SMALL_SHARDS])
    for i, (n, _) in enumerate(SMALL_SHARDS):
        grads[n] = jnp.stack([parts[l * len(SMALL_SHARDS) + i] for l in range(DEPTH)])
    off = DEPTH * _SMALL_ELEMS
    rep_all = jnp.concatenate([small_all[j, off:off + _REP_QUARTER] for j in range(N_CHIPS)])
    parts = _unpack(rep_all, [s for _ in range(DEPTH) for _, s in REPLICATED])
    for i, (n, _) in enumerate(REPLICATED):
        grads[n] = jnp.stack([parts[l * len(REPLICATED) + i] for l in range(DEPTH)])

    deltas, new_m, new_v = {}, {}, {}
    for n in WEIGHT_ORDER:
        deltas[n], new_m[n], new_v[n] = _adamw(wts[n], grads[n], moms[n], vels[n], "adamw_" + n)
    return (loss, grad_x, *[grads[n] for n in WEIGHT_ORDER], *[deltas[n] for n in WEIGHT_ORDER],
            *[new_m[n] for n in WEIGHT_ORDER], *[new_v[n] for n in WEIGHT_ORDER])
```

```python
import functools
import math

import jax
import jax.numpy as jnp
from jax import lax
from jax.experimental import pallas as pl
from jax.experimental.pallas import tpu as pltpu

F32 = jnp.float32
BF16 = jnp.bfloat16
MXU_DTYPE = jnp.bfloat16

D_MODEL = 1024
HEAD_DIM = 64
N_HEADS = 8
D_ATT = 512
D_CONV = 256
D_SGU = 256
N_GROUPS = 4
CHUNK = 128
D_FF = 2816
DEPTH = 4
RMS_EPS = 1e-6
LN_EPS = 1e-5
N_CHIPS = 4
LANES = 128
PACK_COLS = 1024
HALO = 16

ADAM_LR = 0.001
ADAM_B1 = 0.9
ADAM_B2 = 0.999
ADAM_EPS = 1e-08
ADAM_WD = 0.01
ADAM_STEP = 10

OFF_GL = 0
OFF_Q = 3 * D_MODEL
OFF_K = OFF_Q + D_ATT
OFF_V = OFF_K + D_ATT
OFF_BG = OFF_V + D_ATT
OFF_CG = OFF_BG + D_CONV
OFF_HC = OFF_CG + D_CONV
OFF_U = OFF_HC + D_CONV
OFF_VS = OFF_U + D_SGU
W_P = OFF_VS + D_SGU
F_ROWS = 16

VMEM_LIMIT = 56 * 1024 * 1024
MESH = pl.DeviceIdType.MESH


def _params(sem=None):
    if sem is None:
        return pltpu.CompilerParams(vmem_limit_bytes=VMEM_LIMIT)
    return pltpu.CompilerParams(dimension_semantics=sem, vmem_limit_bytes=VMEM_LIMIT)


def _tile(dim, pref):
    if dim <= pref:
        return dim
    if dim % pref == 0:
        return pref
    return dim


_DIMS = {"nn": (((1,), (0,)), ((), ())), "nt": (((1,), (1,)), ((), ())), "tn": (((0,), (0,)), ((), ()))}


def _mm(a, b, mode, out_dtype, name, tm, tn, tk, chip_of=None):
    if mode == "tn":
        K, M = a.shape
    else:
        M, K = a.shape
    N = b.shape[0] if mode == "nt" else b.shape[1]
    tm, tn, tk = _tile(M, tm), _tile(N // N_CHIPS if chip_of else N, tn), _tile(K, tk)
    nk = K // tk
    dims = _DIMS[mode]

    def body(a_ref, b_ref, o_ref, *acc):
        part = lax.dot_general(a_ref[...].astype(MXU_DTYPE), b_ref[...].astype(MXU_DTYPE), dims,
                               preferred_element_type=F32)
        if nk == 1:
            o_ref[...] = part.astype(o_ref.dtype)
        else:
            acc_ref = acc[0]
            k = pl.program_id(2)

            @pl.when(k == 0)
            def _():
                acc_ref[...] = part

            @pl.when(k > 0)
            def _():
                acc_ref[...] += part

            @pl.when(k == nk - 1)
            def _():
                o_ref[...] = acc_ref[...].astype(o_ref.dtype)

    if mode == "tn":
        a_spec = pl.BlockSpec((tk, tm), lambda i, j, k: (k, i))
    else:
        a_spec = pl.BlockSpec((tm, tk), lambda i, j, k: (i, k))
    if mode == "nt":
        b_spec = pl.BlockSpec((tn, tk), lambda i, j, k: (j, k))
    else:
        b_spec = pl.BlockSpec((tk, tn), lambda i, j, k: (k, j))
    if chip_of is None:
        out_spec = pl.BlockSpec((tm, tn), lambda i, j, k: (i, j))
        out_shape = jax.ShapeDtypeStruct((M, N), out_dtype)
    else:
        per = (N // N_CHIPS) // tn
        out_spec = pl.BlockSpec((None, tm, tn), lambda i, j, k: (chip_of(j // per), i, j % per))
        out_shape = jax.ShapeDtypeStruct((N_CHIPS, M, N // N_CHIPS), out_dtype)
    return pl.pallas_call(
        body,
        name=name,
        grid=(M // tm, N // tn, nk),
        in_specs=[a_spec, b_spec],
        out_specs=out_spec,
        out_shape=out_shape,
        scratch_shapes=[pltpu.VMEM((tm, tn), F32)] if nk > 1 else [],
        compiler_params=_params(("parallel", "parallel", "arbitrary")),
    )(a, b)


_GELU_K = math.sqrt(2.0 / math.pi)
_GELU_C = 0.044715


def _gelu(x):
    t = jnp.tanh(_GELU_K * (x + _GELU_C * (x * x * x)))
    return x * (0.5 * (1.0 + t))


def _gelu_and_grad(x):
    x2 = x * x
    t = jnp.tanh(_GELU_K * (x + _GELU_C * (x2 * x)))
    cdf = 0.5 * (1.0 + t)
    dcdf = 0.5 * (1.0 - t * t) * (_GELU_K * (1.0 + 3.0 * _GELU_C * x2))
    return x * cdf, cdf + x * dcdf


def _sigmoid(x):
    return 1.0 / (1.0 + jnp.exp(-x))


def _shift_down(cur, prev, k):
    h = prev.shape[0]
    ext = jnp.concatenate([prev, cur], axis=0)
    return pltpu.roll(ext, k, 0)[h:]


def _shift_up(cur, nxt, k):
    t, h = cur.shape[0], nxt.shape[0]
    ext = jnp.concatenate([cur, nxt], axis=0)
    return pltpu.roll(ext, t + h - k, 0)[:t]


def _row_sum8(x):
    t, c = x.shape
    return jnp.sum(x.reshape(t // 8, 8, c), axis=0)


_DEP = pl.BlockSpec((8, LANES), lambda i: (0, 0))


def _rms_fwd(x, g, name, dep=None):
    s, d = x.shape
    t = _tile(s, 512)

    def body(x_ref, g_ref, *rest):
        o_ref = rest[-1]
        xv = x_ref[...]
        r = lax.rsqrt(jnp.mean(xv * xv, axis=-1, keepdims=True) + RMS_EPS)
        o_ref[...] = (xv * r * g_ref[...]).astype(o_ref.dtype)

    deps = [] if dep is None else [dep]
    return pl.pallas_call(
        body, name=name, grid=(s // t,),
        in_specs=[pl.BlockSpec((t, d), lambda i: (i, 0)), pl.BlockSpec((1, d), lambda i: (0, 0))] + [_DEP] * len(deps),
        out_specs=pl.BlockSpec((t, d), lambda i: (i, 0)),
        out_shape=jax.ShapeDtypeStruct((s, d), BF16),
        compiler_params=_params(("parallel",)),
    )(x, g, *deps)


def _resid_post(x, y, g, name):
    s, d = x.shape
    t = _tile(s, 512)

    def body(x_ref, y_ref, g_ref, o_ref):
        yv = y_ref[...]
        r = lax.rsqrt(jnp.mean(yv * yv, axis=-1, keepdims=True) + RMS_EPS)
        o_ref[...] = x_ref[...] + yv * r * g_ref[...]

    row = pl.BlockSpec((t, d), lambda i: (i, 0))
    return pl.pallas_call(
        body, name=name, grid=(s // t,),
        in_specs=[row, row, pl.BlockSpec((1, d), lambda i: (0, 0))],
        out_specs=row,
        out_shape=jax.ShapeDtypeStruct((s, d), F32),
        compiler_params=_params(("parallel",)),
    )(x, y, g)


def _rms_bwd(xin, g, dys, dres, out_dtype, name, dep=None):
    s, d = xin.shape
    t = _tile(s, 512)
    n = s // t
    n_dy = len(dys)
    has_res = dres is not None
    deps = [] if dep is None else [dep]

    def body(*refs):
        x_ref, g_ref = refs[0], refs[1]
        dy_refs = refs[2:2 + n_dy]
        pos = 2 + n_dy
        res_ref = refs[pos] if has_res else None
        pos += (1 if has_res else 0) + len(deps)
        dx_ref, dg_ref, acc_ref = refs[pos], refs[pos + 1], refs[pos + 2]
        i = pl.program_id(0)
        xv = x_ref[...]
        dy = dy_refs[0][...].astype(F32)
        for extra in dy_refs[1:]:
            dy = dy + extra[...].astype(F32)
        r = lax.rsqrt(jnp.mean(xv * xv, axis=-1, keepdims=True) + RMS_EPS)
        u = dy * g_ref[...]
        xr = xv * r
        dx = r * (u - xr * jnp.mean(u * xr, axis=-1, keepdims=True))
        if has_res:
            dx = dx + res_ref[...]
        dx_ref[...] = dx.astype(dx_ref.dtype)
        part = _row_sum8(dy * xr)

        @pl.when(i == 0)
        def _():
            acc_ref[...] = part

        @pl.when(i > 0)
        def _():
            acc_ref[...] += part

        @pl.when(i == n - 1)
        def _():
            dg_ref[...] = jnp.sum(acc_ref[...], axis=0, keepdims=True)

    row = pl.BlockSpec((t, d), lambda i: (i, 0))
    vec = pl.BlockSpec((1, d), lambda i: (0, 0))
    ins = [xin, g, *dys] + ([dres] if has_res else []) + deps
    return pl.pallas_call(
        body, name=name, grid=(n,),
        in_specs=[row, vec] + [row] * (n_dy + (1 if has_res else 0)) + [_DEP] * len(deps),
        out_specs=[row, vec],
        out_shape=[jax.ShapeDtypeStruct((s, d), out_dtype), jax.ShapeDtypeStruct((1, d), F32)],
        scratch_shapes=[pltpu.VMEM((8, d), F32)],
        compiler_params=_params(("arbitrary",)),
    )(*ins)


def _loss_head(y, target, name):
    s, d = y.shape
    t = _tile(s, 512)
    n = s // t

    def body(y_ref, t_ref, dy_ref, loss_ref, acc_ref):
        i = pl.program_id(0)
        e = y_ref[...] - t_ref[...]
        dy_ref[...] = e * (1.0 / d)
        part = _row_sum8(e * e)

        @pl.when(i == 0)
        def _():
            acc_ref[...] = part

        @pl.when(i > 0)
        def _():
            acc_ref[...] += part

        @pl.when(i == n - 1)
        def _():
            tot = jnp.sum(jnp.sum(acc_ref[...], axis=0, keepdims=True), axis=1, keepdims=True)
            loss_ref[...] = tot * (0.5 / d)

    row = pl.BlockSpec((t, d), lambda i: (i, 0))
    return pl.pallas_call(
        body, name=name, grid=(n,),
        in_specs=[row, row],
        out_specs=[row, pl.BlockSpec((1, 1), lambda i: (0, 0))],
        out_shape=[jax.ShapeDtypeStruct((s, d), F32), jax.ShapeDtypeStruct((1, 1), F32)],
        scratch_shapes=[pltpu.VMEM((8, d), F32)],
        compiler_params=_params(("arbitrary",)),
    )(y, target)


def _split3(x):
    hi = x.astype(BF16)
    r1 = x - hi.astype(F32)
    mid = r1.astype(BF16)
    lo = (r1 - mid.astype(F32)).astype(BF16)
    return hi, mid, lo


def _tri_dot(x, tri):
    hi, mid, lo = _split3(x)
    dn = _DIMS["nn"]
    out = lax.dot_general(hi, tri, dn, preferred_element_type=F32)
    out = out + lax.dot_general(mid, tri, dn, preferred_element_type=F32)
    return out + lax.dot_general(lo, tri, dn, preferred_element_type=F32)


def _log_sigmoid(z):
    return jnp.minimum(z, 0.0) - jnp.log(1.0 + jnp.exp(-jnp.abs(z)))


def _gate_fwd(f_row, b_col, name):
    rows, s = f_row.shape
    t = _tile(s, 512)
    n = s // t

    def body(f_ref, b_ref, ck_ref, carry_ref):
        i = pl.program_id(0)

        @pl.when(i == 0)
        def _():
            carry_ref[...] = jnp.zeros_like(carry_ref)

        logf = _log_sigmoid(f_ref[...] + b_ref[...])
        r = lax.broadcasted_iota(jnp.int32, (t, t), 0)
        c = lax.broadcasted_iota(jnp.int32, (t, t), 1)
        tri = jnp.where(r <= c, 1.0, 0.0).astype(BF16)
        cs = _tri_dot(logf, tri) + carry_ref[...]
        carry_ref[...] = cs[:, t - 1:t]
        terms = [part.astype(F32) for part in _split3(-cs)]
        sub = lax.broadcasted_iota(jnp.int32, (LANES, t), 0)
        for p in range(N_HEADS // 2):
            stacked = jnp.zeros((LANES, t), F32)
            for hh in range(2):
                for j, term in enumerate(terms):
                    h = 2 * p + hh
                    stacked = jnp.where(sub == 3 * hh + j, jnp.broadcast_to(term[h:h + 1, :], (LANES, t)), stacked)
            ck_ref[p] = stacked.T.astype(ck_ref.dtype)

    return pl.pallas_call(
        body, name=name, grid=(n,),
        in_specs=[pl.BlockSpec((rows, t), lambda i: (0, i)), pl.BlockSpec((rows, 1), lambda i: (0, 0))],
        out_specs=pl.BlockSpec((N_HEADS // 2, t, LANES), lambda i: (0, i, 0)),
        out_shape=jax.ShapeDtypeStruct((N_HEADS // 2, s, LANES), BF16),
        scratch_shapes=[pltpu.VMEM((rows, 1), F32)],
        compiler_params=_params(("arbitrary",)),
    )(f_row, b_col)


def _gate_bwd(f_row, b_col, dc_even, dc_odd, name):
    rows, s = f_row.shape
    t = _tile(s, 512)
    n = s // t

    def body(f_ref, b_ref, dce_ref, dco_ref, df_ref, db_ref, carry_ref, acc_ref):
        i = pl.program_id(0)

        @pl.when(i == 0)
        def _():
            carry_ref[...] = jnp.zeros_like(carry_ref)
            acc_ref[...] = jnp.zeros_like(acc_ref)

        head = lax.broadcasted_iota(jnp.int32, (rows, t), 0)
        dcv = jnp.zeros((rows, t), F32)
        for h in range(N_HEADS):
            src = dce_ref if h % 2 == 0 else dco_ref
            dcv = jnp.where(head == h, jnp.broadcast_to(src[h // 2, 0:1, :], (rows, t)), dcv)
        r = lax.broadcasted_iota(jnp.int32, (t, t), 0)
        c = lax.broadcasted_iota(jnp.int32, (t, t), 1)
        tri = jnp.where(r >= c, 1.0, 0.0).astype(BF16)
        dlogf = _tri_dot(dcv, tri) + carry_ref[...]
        carry_ref[...] = dlogf[:, 0:1]
        z = f_ref[...] + b_ref[...]
        df = dlogf * _sigmoid(-z)
        df_ref[...] = df.astype(df_ref.dtype)
        acc_ref[...] += jnp.sum(df, axis=1, keepdims=True)

        @pl.when(i == n - 1)
        def _():
            db_ref[...] = acc_ref[...]

    rev = lambda i: (0, n - 1 - i)
    dc_spec = pl.BlockSpec((N_HEADS // 2, 8, t), lambda i: (0, 0, n - 1 - i))
    return pl.pallas_call(
        body, name=name, grid=(n,),
        in_specs=[pl.BlockSpec((rows, t), rev), pl.BlockSpec((rows, 1), lambda i: (0, 0)), dc_spec, dc_spec],
        out_specs=[pl.BlockSpec((rows, t), rev), pl.BlockSpec((rows, 1), lambda i: (0, 0))],
        out_shape=[jax.ShapeDtypeStruct((rows, s), BF16), jax.ShapeDtypeStruct((rows, 1), F32)],
        scratch_shapes=[pltpu.VMEM((rows, 1), F32), pltpu.VMEM((rows, 1), F32)],
        compiler_params=_params(("arbitrary",)),
    )(f_row, b_col, dc_even, dc_odd)


_NEG = -1e30
_SCALE = HEAD_DIM ** -0.5


def _head_masks():
    lane = lax.broadcasted_iota(jnp.int32, (1, LANES), 1)
    return [lane < HEAD_DIM, lane >= HEAD_DIM]


def _attn_fwd(h, ck, name):
    s = h.shape[0]
    t = _tile(s, 512)
    n = s // t
    qb, kb, vb = OFF_Q // LANES, OFF_K // LANES, OFF_V // LANES

    def body(q_ref, k_ref, v_ref, ck_ref, o_ref, of_ref, lse_ref, m_ref, l_ref, acc_ref):
        qi, ki = pl.program_id(1), pl.program_id(2)
        masks = _head_masks()
        lane = lax.broadcasted_iota(jnp.int32, (1, LANES), 1)

        @pl.when(ki == 0)
        def _():
            m_ref[...] = jnp.full_like(m_ref, _NEG)
            l_ref[...] = jnp.zeros_like(l_ref)
            acc_ref[...] = jnp.zeros_like(acc_ref)

        def step(diag):
            q = q_ref[...] * _SCALE
            k_aug = jnp.concatenate([k_ref[...], ck_ref[0]], axis=1)
            v = v_ref[...]
            for hh in range(2):
                ones = jnp.where((lane >= 3 * hh) & (lane < 3 * hh + 3), 1.0, 0.0).astype(q.dtype)
                q_aug = jnp.concatenate([jnp.where(masks[hh], q, jnp.zeros_like(q)),
                                         jnp.broadcast_to(ones, q.shape)], axis=1)
                sc = lax.dot_general(k_aug, q_aug, _DIMS["nt"], preferred_element_type=F32)
                if diag:
                    r = lax.broadcasted_iota(jnp.int32, (t, t), 0)
                    cc = lax.broadcasted_iota(jnp.int32, (t, t), 1)
                    sc = jnp.where(r <= cc, sc, _NEG)
                m_prev = m_ref[hh]
                m_new = jnp.maximum(m_prev, jnp.max(sc, axis=0, keepdims=True))
                alpha = jnp.exp(m_prev - m_new)
                p = jnp.exp(sc - m_new)
                l_ref[hh] = alpha * l_ref[hh] + jnp.sum(p, axis=0, keepdims=True)
                m_ref[hh] = m_new
                p_hi = p.astype(MXU_DTYPE)
                p_lo = (p - p_hi.astype(F32)).astype(MXU_DTYPE)
                pv = (lax.dot_general(v, p_hi, _DIMS["tn"], preferred_element_type=F32)
                      + lax.dot_general(v, p_lo, _DIMS["tn"], preferred_element_type=F32))
                rows = slice(hh * HEAD_DIM, (hh + 1) * HEAD_DIM)
                acc_ref[rows, :] = alpha * acc_ref[rows, :] + pv[rows]

        @pl.when(ki < qi)
        def _():
            step(False)

        @pl.when(ki == qi)
        def _():
            step(True)
            inv = jnp.concatenate([jnp.broadcast_to(1.0 / l_ref[hh], (HEAD_DIM, t)) for hh in range(2)], axis=0)
            out = (acc_ref[...] * inv).T
            o_ref[...] = out.astype(o_ref.dtype)
            of_ref[...] = out
            lse = jnp.concatenate([jnp.broadcast_to(m_ref[hh] + jnp.log(l_ref[hh]), (HEAD_DIM, t))
                                   for hh in range(2)], axis=0)
            lse_ref[...] = lse.T

    kv_row = lambda p, qi, ki: jnp.minimum(ki, qi)
    return pl.pallas_call(
        body, name=name, grid=(N_HEADS // 2, n, n),
        in_specs=[
            pl.BlockSpec((t, LANES), lambda p, qi, ki: (qi, qb + p)),
            pl.BlockSpec((t, LANES), lambda p, qi, ki: (kv_row(p, qi, ki), kb + p)),
            pl.BlockSpec((t, LANES), lambda p, qi, ki: (kv_row(p, qi, ki), vb + p)),
            pl.BlockSpec((1, t, LANES), lambda p, qi, ki: (p, kv_row(p, qi, ki), 0)),
        ],
        out_specs=[pl.BlockSpec((t, LANES), lambda p, qi, ki: (qi, p))] * 3,
        out_shape=[jax.ShapeDtypeStruct((s, D_ATT), BF16), jax.ShapeDtypeStruct((s, D_ATT), F32),
                   jax.ShapeDtypeStruct((s, D_ATT), F32)],
        scratch_shapes=[pltpu.VMEM((2, 1, t), F32), pltpu.VMEM((2, 1, t), F32), pltpu.VMEM((LANES, t), F32)],
        compiler_params=_params(("parallel", "parallel", "arbitrary")),
    )(h, h, h, ck)


def _attn_bwd(h, ck, o, lse, do, name):
    s = h.shape[0]
    t = _tile(s, 512)
    n = s // t
    qb, kb, vb = OFF_Q // LANES, OFF_K // LANES, OFF_V // LANES

    def body(q_ref, k_ref, v_ref, ck_ref, o_ref, lse_ref, do_ref,
             dq_ref, dk_ref, dv_ref, dc0_ref, dc1_ref, dk_acc, dv_acc, dc_acc):
        ki, qi = pl.program_id(1), pl.program_id(2)
        masks = _head_masks()
        lane = lax.broadcasted_iota(jnp.int32, (1, LANES), 1)

        @pl.when((ki == 0) & (qi == 0))
        def _():
            dq_ref[...] = jnp.zeros_like(dq_ref)

        @pl.when(qi == ki)
        def _():
            dk_acc[...] = jnp.zeros_like(dk_acc)
            dv_acc[...] = jnp.zeros_like(dv_acc)
            dc_acc[...] = jnp.zeros_like(dc_acc)

        def step(diag):
            q = q_ref[...] * _SCALE
            k = k_ref[...]
            v = v_ref[...]
            dov = do_ref[...]
            k_aug = jnp.concatenate([k, ck_ref[0]], axis=1)
            prod_t = (dov.astype(F32) * o_ref[...]).T
            lse_t = lse_ref[...].T
            dq_blk = jnp.zeros((t, LANES), F32)
            for hh in range(2):
                mk = masks[hh]
                rows = slice(hh * HEAD_DIM, (hh + 1) * HEAD_DIM)
                qh = jnp.where(mk, q, jnp.zeros_like(q))
                kh = jnp.where(mk, k, jnp.zeros_like(k))
                doh = jnp.where(mk, dov, jnp.zeros_like(dov))
                ones = jnp.where((lane >= 3 * hh) & (lane < 3 * hh + 3), 1.0, 0.0).astype(q.dtype)
                q_aug = jnp.concatenate([qh, jnp.broadcast_to(ones, q.shape)], axis=1)
                sc = lax.dot_general(k_aug, q_aug, _DIMS["nt"], preferred_element_type=F32)
                p = jnp.exp(sc - lse_t[hh * HEAD_DIM:hh * HEAD_DIM + 1, :])
                if diag:
                    r = lax.broadcasted_iota(jnp.int32, (t, t), 0)
                    cc = lax.broadcasted_iota(jnp.int32, (t, t), 1)
                    p = jnp.where(r <= cc, p, 0.0)
                dp = lax.dot_general(v, doh, _DIMS["nt"], preferred_element_type=F32)
                delta = jnp.sum(prod_t[rows], axis=0, keepdims=True)
                ds = p * (dp - delta)
                dsb = ds.astype(MXU_DTYPE)
                pb = p.astype(MXU_DTYPE)
                dv_acc[...] += lax.dot_general(pb, doh, _DIMS["nn"], preferred_element_type=F32)
                dk_acc[...] += lax.dot_general(dsb, qh, _DIMS["nn"], preferred_element_type=F32)
                dq_blk = dq_blk + lax.dot_general(dsb, kh, _DIMS["tn"], preferred_element_type=F32)
                dc_acc[hh] = dc_acc[hh] - jnp.sum(ds, axis=1, keepdims=True)
            rows_q = pl.ds(pl.multiple_of(qi * t, t), t)
            dq_ref[rows_q, :] = dq_ref[rows_q, :] + dq_blk * _SCALE

        @pl.when(qi > ki)
        def _():
            step(False)

        @pl.when(qi == ki)
        def _():
            step(True)

        @pl.when(qi == n - 1)
        def _():
            dk_ref[...] = dk_acc[...].astype(dk_ref.dtype)
            dv_ref[...] = dv_acc[...].astype(dv_ref.dtype)
            dc0_ref[0] = jnp.broadcast_to(dc_acc[0], (t, LANES)).T[0:8]
            dc1_ref[0] = jnp.broadcast_to(dc_acc[1], (t, LANES)).T[0:8]

    q_row = lambda p, ki, qi: jnp.maximum(qi, ki)
    return pl.pallas_call(
        body, name=name, grid=(N_HEADS // 2, n, n),
        in_specs=[
            pl.BlockSpec((t, LANES), lambda p, ki, qi: (q_row(p, ki, qi), qb + p)),
            pl.BlockSpec((t, LANES), lambda p, ki, qi: (ki, kb + p)),
            pl.BlockSpec((t, LANES), lambda p, ki, qi: (ki, vb + p)),
            pl.BlockSpec((1, t, LANES), lambda p, ki, qi: (p, ki, 0)),
            pl.BlockSpec((t, LANES), lambda p, ki, qi: (q_row(p, ki, qi), p)),
            pl.BlockSpec((t, LANES), lambda p, ki, qi: (q_row(p, ki, qi), p)),
            pl.BlockSpec((t, LANES), lambda p, ki, qi: (q_row(p, ki, qi), p)),
        ],
        out_specs=[
            pl.BlockSpec((s, LANES), lambda p, ki, qi: (0, p)),
            pl.BlockSpec((t, LANES), lambda p, ki, qi: (ki, p)),
            pl.BlockSpec((t, LANES), lambda p, ki, qi: (ki, p)),
            pl.BlockSpec((1, 8, t), lambda p, ki, qi: (p, 0, ki)),
            pl.BlockSpec((1, 8, t), lambda p, ki, qi: (p, 0, ki)),
        ],
        out_shape=[jax.ShapeDtypeStruct((s, D_ATT), F32), jax.ShapeDtypeStruct((s, D_ATT), BF16),
                   jax.ShapeDtypeStruct((s, D_ATT), BF16), jax.ShapeDtypeStruct((N_HEADS // 2, 8, s), F32),
                   jax.ShapeDtypeStruct((N_HEADS // 2, 8, s), F32)],
        scratch_shapes=[pltpu.VMEM((t, LANES), F32), pltpu.VMEM((t, LANES), F32), pltpu.VMEM((2, t, 1), F32)],
        compiler_params=_params(("parallel", "arbitrary", "arbitrary")),
    )(h, h, h, ck, o, lse, do)


def _conv3(z, z_prev, w_ref):
    return (w_ref[2:3, :] * z + w_ref[1:2, :] * _shift_down(z, z_prev, 1)
            + w_ref[0:1, :] * _shift_down(z, z_prev, 2))


def _sconv_fwd(h, w, name):
    s = h.shape[0]
    t = _tile(s, 512)
    r = t // HALO
    c = D_CONV
    b_bg, b_cg, b_hc = OFF_BG // c, OFF_CG // c, OFF_HC // c

    def body(bg_ref, cg_ref, hc_ref, cgp_ref, hcp_ref, w_ref, y_ref):
        i = pl.program_id(0)
        live = (i > 0).astype(F32)
        z = cg_ref[...].astype(F32) * hc_ref[...].astype(F32)
        zp = cgp_ref[...].astype(F32) * hcp_ref[...].astype(F32) * live
        y_ref[...] = (bg_ref[...].astype(F32) * _conv3(z, zp, w_ref)).astype(y_ref.dtype)

    cur = lambda b: pl.BlockSpec((t, c), lambda i: (i, b))
    prev = lambda b: pl.BlockSpec((HALO, c), lambda i: (jnp.maximum(i * r - 1, 0), b))
    return pl.pallas_call(
        body, name=name, grid=(s // t,),
        in_specs=[cur(b_bg), cur(b_cg), cur(b_hc), prev(b_cg), prev(b_hc), pl.BlockSpec((8, c), lambda i: (0, 0))],
        out_specs=pl.BlockSpec((t, c), lambda i: (i, 0)),
        out_shape=jax.ShapeDtypeStruct((s, c), BF16),
        compiler_params=_params(("parallel",)),
    )(h, h, h, h, h, w)


def _sconv_bwd(h, w, dy, name):
    s = h.shape[0]
    t = _tile(s, 512)
    n = s // t
    r = t // HALO
    nh = s // HALO
    c = D_CONV
    b_bg, b_cg, b_hc = OFF_BG // c, OFF_CG // c, OFF_HC // c

    def body(bg_ref, cg_ref, hc_ref, cgp_ref, hcp_ref, bgn_ref, dy_ref, dyn_ref, w_ref, d_ref, dw_ref, acc_ref):
        i = pl.program_id(0)
        has_prev = (i > 0).astype(F32)
        has_next = (i < n - 1).astype(F32)
        bg = bg_ref[...].astype(F32)
        cg = cg_ref[...].astype(F32)
        hc = hc_ref[...].astype(F32)
        dyv = dy_ref[...].astype(F32)
        z = cg * hc
        zp = cgp_ref[...].astype(F32) * hcp_ref[...].astype(F32) * has_prev
        z1 = _shift_down(z, zp, 1)
        z2 = _shift_down(z, zp, 2)
        cz = w_ref[2:3, :] * z + w_ref[1:2, :] * z1 + w_ref[0:1, :] * z2
        dcz = dyv * bg
        dczn = dyn_ref[...].astype(F32) * bgn_ref[...].astype(F32) * has_next
        dz = (w_ref[2:3, :] * dcz + w_ref[1:2, :] * _shift_up(dcz, dczn, 1)
              + w_ref[0:1, :] * _shift_up(dcz, dczn, 2))
        d_ref[:, 0:c] = (dyv * cz).astype(d_ref.dtype)
        d_ref[:, c:2 * c] = (dz * hc).astype(d_ref.dtype)
        d_ref[:, 2 * c:3 * c] = (dz * cg).astype(d_ref.dtype)

        @pl.when(i == 0)
        def _():
            acc_ref[...] = jnp.zeros_like(acc_ref)

        acc_ref[0] += _row_sum8(dcz * z2)
        acc_ref[1] += _row_sum8(dcz * z1)
        acc_ref[2] += _row_sum8(dcz * z)

        @pl.when(i == n - 1)
        def _():
            rows = [jnp.sum(acc_ref[k], axis=0, keepdims=True) for k in range(3)]
            dw_ref[...] = jnp.concatenate(rows + [jnp.zeros((5, c), F32)], axis=0)

    cur = lambda b: pl.BlockSpec((t, c), lambda i: (i, b))
    prev = lambda b: pl.BlockSpec((HALO, c), lambda i: (jnp.maximum(i * r - 1, 0), b))
    nxt = lambda b: pl.BlockSpec((HALO, c), lambda i: (jnp.minimum((i + 1) * r, nh - 1), b))
    return pl.pallas_call(
        body, name=name, grid=(n,),
        in_specs=[cur(b_bg), cur(b_cg), cur(b_hc), prev(b_cg), prev(b_hc), nxt(b_bg),
                  cur(0), nxt(0), pl.BlockSpec((8, c), lambda i: (0, 0))],
        out_specs=[pl.BlockSpec((t, 3 * c), lambda i: (i, 0)), pl.BlockSpec((8, c), lambda i: (0, 0))],
        out_shape=[jax.ShapeDtypeStruct((s, 3 * c), BF16), jax.ShapeDtypeStruct((8, c), F32)],
        scratch_shapes=[pltpu.VMEM((3, 8, c), F32)],
        compiler_params=_params(("arbitrary",)),
    )(h, h, h, h, h, h, dy, dy, w)


def _group_masks():
    lane = lax.broadcasted_iota(jnp.int32, (1, D_SGU), 1)
    return [(lane >= g * HEAD_DIM) & (lane < (g + 1) * HEAD_DIM) for g in range(N_GROUPS)]


def _tril_weights(w_ref):
    r = lax.broadcasted_iota(jnp.int32, (CHUNK, CHUNK), 0)
    c = lax.broadcasted_iota(jnp.int32, (CHUNK, CHUNK), 1)
    return [jnp.where(r >= c, w_ref[g], 0.0).astype(MXU_DTYPE) for g in range(N_GROUPS)]


def _sgu_ln(vs, g_ref, b_ref):
    vg, dvg = _gelu_and_grad(vs)
    mu = jnp.mean(vg, axis=-1, keepdims=True)
    xc = vg - mu
    rstd = lax.rsqrt(jnp.mean(xc * xc, axis=-1, keepdims=True) + LN_EPS)
    xhat = xc * rstd
    return xhat * g_ref[...] + b_ref[...], xhat, rstd, dvg


def _sgu_fwd(h, ln_g, ln_b, w_s, bias, name):
    s = h.shape[0]
    t = _tile(s, 512)
    c = D_SGU
    b_u, b_v = OFF_U // c, OFF_VS // c

    def body(u_ref, v_ref, g_ref, b_ref, w_ref, bias_ref, y_ref):
        gm = _group_masks()
        wm = _tril_weights(w_ref)
        ug = _gelu(u_ref[...].astype(F32))
        vn, _, _, _ = _sgu_ln(v_ref[...].astype(F32), g_ref, b_ref)
        vnb = vn.astype(MXU_DTYPE)
        for ch in range(t // CHUNK):
            rows = slice(ch * CHUNK, (ch + 1) * CHUNK)
            mixed = bias_ref[...]
            for g in range(N_GROUPS):
                mg = lax.dot_general(wm[g], vnb[rows], _DIMS["nn"], preferred_element_type=F32)
                mixed = jnp.where(gm[g], mixed + mg, mixed)
            y_ref[rows, :] = (ug[rows] * mixed).astype(y_ref.dtype)

    full = lambda shp: pl.BlockSpec(shp, lambda i: (0,) * len(shp))
    return pl.pallas_call(
        body, name=name, grid=(s // t,),
        in_specs=[pl.BlockSpec((t, c), lambda i: (i, b_u)), pl.BlockSpec((t, c), lambda i: (i, b_v)),
                  full((1, c)), full((1, c)), full((N_GROUPS, CHUNK, CHUNK)), full((CHUNK, c))],
        out_specs=pl.BlockSpec((t, c), lambda i: (i, 0)),
        out_shape=jax.ShapeDtypeStruct((s, c), BF16),
        compiler_params=_params(("parallel",)),
    )(h, h, ln_g, ln_b, w_s, bias)


def _sgu_bwd(h, ln_g, ln_b, w_s, bias, dy, name):
    s = h.shape[0]
    t = _tile(s, 512)
    n = s // t
    c = D_SGU
    b_u, b_v = OFF_U // c, OFF_VS // c

    def body(u_ref, v_ref, g_ref, b_ref, w_ref, bias_ref, dy_ref,
             d_ref, dg_ref, db_ref, dw_ref, dbias_ref, dg_acc, db_acc):
        i = pl.program_id(0)
        gm = _group_masks()
        wm = _tril_weights(w_ref)

        @pl.when(i == 0)
        def _():
            dg_acc[...] = jnp.zeros_like(dg_acc)
            db_acc[...] = jnp.zeros_like(db_acc)
            dw_ref[...] = jnp.zeros_like(dw_ref)
            dbias_ref[...] = jnp.zeros_like(dbias_ref)

        ug, dug = _gelu_and_grad(u_ref[...].astype(F32))
        vn, xhat, rstd, dvg = _sgu_ln(v_ref[...].astype(F32), g_ref, b_ref)
        vnb = vn.astype(MXU_DTYPE)
        dyv = dy_ref[...].astype(F32)
        dmixed = dyv * ug
        dmb = dmixed.astype(MXU_DTYPE)
        dvn_parts = []
        for ch in range(t // CHUNK):
            rows = slice(ch * CHUNK, (ch + 1) * CHUNK)
            mixed = bias_ref[...]
            dvn = jnp.zeros((CHUNK, c), F32)
            for g in range(N_GROUPS):
                mg = lax.dot_general(wm[g], vnb[rows], _DIMS["nn"], preferred_element_type=F32)
                mixed = jnp.where(gm[g], mixed + mg, mixed)
                dvn = jnp.where(gm[g], lax.dot_general(wm[g], dmb[rows], _DIMS["tn"], preferred_element_type=F32),
                                dvn)
                dmg = jnp.where(gm[g], dmb[rows], jnp.zeros_like(dmb[rows]))
                dw_ref[g] += lax.dot_general(dmg, vnb[rows], _DIMS["nt"], preferred_element_type=F32)
            d_ref[rows, 0:c] = (dyv[rows] * mixed * dug[rows]).astype(d_ref.dtype)
            dbias_ref[...] += dmixed[rows]
            dvn_parts.append(dvn)
        dvn = jnp.concatenate(dvn_parts, axis=0)
        dg_acc[...] += _row_sum8(dvn * xhat)
        db_acc[...] += _row_sum8(dvn)
        dxh = dvn * g_ref[...]
        dvgl = rstd * (dxh - jnp.mean(dxh, axis=-1, keepdims=True)
                       - xhat * jnp.mean(dxh * xhat, axis=-1, keepdims=True))
        d_ref[:, c:2 * c] = (dvgl * dvg).astype(d_ref.dtype)

        @pl.when(i == n - 1)
        def _():
            dg_ref[...] = jnp.sum(dg_acc[...], axis=0, keepdims=True)
            db_ref[...] = jnp.sum(db_acc[...], axis=0, keepdims=True)
            r = lax.broadcasted_iota(jnp.int32, (CHUNK, CHUNK), 0)
            cc = lax.broadcasted_iota(jnp.int32, (CHUNK, CHUNK), 1)
            for g in range(N_GROUPS):
                dw_ref[g] = jnp.where(r >= cc, dw_ref[g], 0.0)

    full = lambda shp: pl.BlockSpec(shp, lambda i: (0,) * len(shp))
    return pl.pallas_call(
        body, name=name, grid=(n,),
        in_specs=[pl.BlockSpec((t, c), lambda i: (i, b_u)), pl.BlockSpec((t, c), lambda i: (i, b_v)),
                  full((1, c)), full((1, c)), full((N_GROUPS, CHUNK, CHUNK)), full((CHUNK, c)),
                  pl.BlockSpec((t, c), lambda i: (i, 0))],
        out_specs=[pl.BlockSpec((t, 2 * c), lambda i: (i, 0)), full((1, c)), full((1, c)),
                   full((N_GROUPS, CHUNK, CHUNK)), full((CHUNK, c))],
        out_shape=[jax.ShapeDtypeStruct((s, 2 * c), BF16), jax.ShapeDtypeStruct((1, c), F32),
                   jax.ShapeDtypeStruct((1, c), F32), jax.ShapeDtypeStruct((N_GROUPS, CHUNK, CHUNK), F32),
                   jax.ShapeDtypeStruct((CHUNK, c), F32)],
        scratch_shapes=[pltpu.VMEM((8, c), F32), pltpu.VMEM((8, c), F32)],
        compiler_params=_params(("arbitrary",)),
    )(h, h, ln_g, ln_b, w_s, bias, dy)


def _merge_fwd(h, acts, ws, b_gate, name):
    s = h.shape[0]
    d = D_MODEL
    t = _tile(s, 512)

    def body(gl0, gl1, gl2, a0, a1, a2, w0, w1, w2, b_ref, o_ref):
        acc = jnp.zeros((t, d), F32)
        for i, (gl, a, w) in enumerate(((gl0, a0, w0), (gl1, a1, w1), (gl2, a2, w2))):
            y = lax.dot_general(a[...], w[...], _DIMS["nn"], preferred_element_type=F32)
            acc = acc + _sigmoid(gl[...].astype(F32) + b_ref[i:i + 1, :]) * y
        o_ref[...] = acc.astype(o_ref.dtype)

    full = lambda arr: pl.BlockSpec(arr.shape, lambda i: (0, 0))
    return pl.pallas_call(
        body, name=name, grid=(s // t,),
        in_specs=[pl.BlockSpec((t, d), lambda i, b=b: (i, b)) for b in range(3)]
                 + [pl.BlockSpec((t, a.shape[1]), lambda i: (i, 0)) for a in acts]
                 + [full(w) for w in ws] + [full(b_gate)],
        out_specs=pl.BlockSpec((t, d), lambda i: (i, 0)),
        out_shape=jax.ShapeDtypeStruct((s, d), BF16),
        compiler_params=_params(("parallel",)),
    )(h, h, h, *acts, *ws, b_gate)


def _merge_bwd(h, acts, ws, b_gate, dmerged, name):
    s = h.shape[0]
    d = D_MODEL
    t = _tile(s, 512)
    n = s // t

    def body(gl0, gl1, gl2, a0, a1, a2, w0, w1, w2, b_ref, dm_ref, dy0, dy1, dy2, dgl_ref, db_ref, acc_ref):
        step = pl.program_id(0)

        @pl.when(step == 0)
        def _():
            acc_ref[...] = jnp.zeros_like(acc_ref)

        dm = dm_ref[...]
        for i, (gl, a, w, dy) in enumerate(((gl0, a0, w0, dy0), (gl1, a1, w1, dy1), (gl2, a2, w2, dy2))):
            y = lax.dot_general(a[...], w[...], _DIMS["nn"], preferred_element_type=F32)
            gate = _sigmoid(gl[...].astype(F32) + b_ref[i:i + 1, :])
            dy[...] = (dm * gate).astype(dy.dtype)
            dgl = dm * y * (gate * (1.0 - gate))
            dgl_ref[:, i * d:(i + 1) * d] = dgl.astype(dgl_ref.dtype)
            acc_ref[i] += _row_sum8(dgl)

        @pl.when(step == n - 1)
        def _():
            rows = [jnp.sum(acc_ref[k], axis=0, keepdims=True) for k in range(3)]
            db_ref[...] = jnp.concatenate(rows + [jnp.zeros((5, d), F32)], axis=0)

    full = lambda arr: pl.BlockSpec(arr.shape, lambda i: (0, 0))
    row = pl.BlockSpec((t, d), lambda i: (i, 0))
    return pl.pallas_call(
        body, name=name, grid=(n,),
        in_specs=[pl.BlockSpec((t, d), lambda i, b=b: (i, b)) for b in range(3)]
                 + [pl.BlockSpec((t, a.shape[1]), lambda i: (i, 0)) for a in acts]
                 + [full(w) for w in ws] + [full(b_gate), row],
        out_specs=[row, row, row, pl.BlockSpec((t, 3 * d), lambda i: (i, 0)), pl.BlockSpec((8, d), lambda i: (0, 0))],
        out_shape=[jax.ShapeDtypeStruct((s, d), BF16)] * 3
                  + [jax.ShapeDtypeStruct((s, 3 * d), BF16), jax.ShapeDtypeStruct((8, d), F32)],
        scratch_shapes=[pltpu.VMEM((3, 8, d), F32)],
        compiler_params=_params(("arbitrary",)),
    )(h, h, h, *acts, *ws, b_gate, dmerged)


FF_BLK = D_FF // 2


def _ffn_act_fwd(h2, w, name):
    s = h2.shape[0]
    t = _tile(s, 512)
    r = t // HALO
    cw = 2 * FF_BLK

    def body(x_ref, xp_ref, w_ref, p_ref):
        i = pl.program_id(0)
        live = (i > 0).astype(F32)
        hc = _conv3(x_ref[...].astype(F32), xp_ref[...].astype(F32) * live, w_ref)
        p_ref[...] = (_gelu(hc[:, :FF_BLK]) * hc[:, FF_BLK:]).astype(p_ref.dtype)

    return pl.pallas_call(
        body, name=name, grid=(s // t, 2),
        in_specs=[pl.BlockSpec((t, cw), lambda i, j: (i, j)),
                  pl.BlockSpec((HALO, cw), lambda i, j: (jnp.maximum(i * r - 1, 0), j)),
                  pl.BlockSpec((8, cw), lambda i, j: (0, j))],
        out_specs=pl.BlockSpec((t, FF_BLK), lambda i, j: (i, j)),
        out_shape=jax.ShapeDtypeStruct((s, D_FF), BF16),
        compiler_params=_params(("parallel", "parallel")),
    )(h2, h2, w)


def _ffn_act_conv_bwd(h2, w, dp, name):
    s = h2.shape[0]
    t = _tile(s, 512)
    n = s // t
    r = t // HALO
    nh = s // HALO
    cw = 2 * FF_BLK

    def body(x_ref, xp_ref, xn_ref, dp_ref, dpn_ref, w_ref, dx_ref, dw_ref, acc_ref):
        i = pl.program_id(1)
        has_prev = (i > 0).astype(F32)
        has_next = (i < n - 1).astype(F32)
        x = jnp.concatenate([x_ref[...].astype(F32), xn_ref[...].astype(F32)], axis=0)
        xp = xp_ref[...].astype(F32) * has_prev
        x1 = _shift_down(x, xp, 1)
        x2 = _shift_down(x, xp, 2)
        hc = w_ref[2:3, :] * x + w_ref[1:2, :] * x1 + w_ref[0:1, :] * x2
        ga, dga = _gelu_and_grad(hc[:, :FF_BLK])
        dpv = jnp.concatenate([dp_ref[...].astype(F32), dpn_ref[...].astype(F32) * has_next], axis=0)
        dhc = jnp.concatenate([dpv * hc[:, FF_BLK:] * dga, dpv * ga], axis=1)
        cur, nxt = dhc[:t], dhc[t:]
        dx = w_ref[2:3, :] * cur + w_ref[1:2, :] * _shift_up(cur, nxt, 1) + w_ref[0:1, :] * _shift_up(cur, nxt, 2)
        dx_ref[...] = dx.astype(dx_ref.dtype)

        @pl.when(i == 0)
        def _():
            acc_ref[...] = jnp.zeros_like(acc_ref)

        acc_ref[0] += _row_sum8(cur * x2[:t])
        acc_ref[1] += _row_sum8(cur * x1[:t])
        acc_ref[2] += _row_sum8(cur * x[:t])

        @pl.when(i == n - 1)
        def _():
            rows = [jnp.sum(acc_ref[k], axis=0, keepdims=True) for k in range(3)]
            dw_ref[...] = jnp.concatenate(rows + [jnp.zeros((5, cw), F32)], axis=0)

    nxt_row = lambda j, i: jnp.minimum((i + 1) * r, nh - 1)
    return pl.pallas_call(
        body, name=name, grid=(2, n),
        in_specs=[pl.BlockSpec((t, cw), lambda j, i: (i, j)),
                  pl.BlockSpec((HALO, cw), lambda j, i: (jnp.maximum(i * r - 1, 0), j)),
                  pl.BlockSpec((HALO, cw), lambda j, i: (nxt_row(j, i), j)),
                  pl.BlockSpec((t, FF_BLK), lambda j, i: (i, j)),
                  pl.BlockSpec((HALO, FF_BLK), lambda j, i: (nxt_row(j, i), j)),
                  pl.BlockSpec((8, cw), lambda j, i: (0, j))],
        out_specs=[pl.BlockSpec((t, cw), lambda j, i: (i, j)), pl.BlockSpec((8, cw), lambda j, i: (0, j))],
        out_shape=[jax.ShapeDtypeStruct((s, 2 * D_FF), BF16), jax.ShapeDtypeStruct((8, 2 * D_FF), F32)],
        scratch_shapes=[pltpu.VMEM((3, 8, cw), F32)],
        compiler_params=_params(("parallel", "arbitrary")),
    )(h2, h2, h2, dp, dp, w)


def _adamw(w, g, m, v, name):
    shape = w.shape
    c = shape[-1]
    rows = math.prod(shape[:-1])
    to2d = lambda a: a.reshape(rows, c)
    cap = max(8, (1 << 18) // c)
    tr = rows
    for cand in (2048, 1024, 512, 256, 128, 64, 32, 16, 8):
        if cand <= cap and rows % cand == 0:
            tr = cand
            break

    def body(w_ref, g_ref, m_ref, v_ref, d_ref, nm_ref, nv_ref):
        gv = g_ref[...]
        nm = ADAM_B1 * m_ref[...] + (1.0 - ADAM_B1) * gv
        nv = ADAM_B2 * v_ref[...] + (1.0 - ADAM_B2) * (gv * gv)
        m_hat = nm / (1.0 - ADAM_B1 ** ADAM_STEP)
        v_hat = nv / (1.0 - ADAM_B2 ** ADAM_STEP)
        d_ref[...] = -ADAM_LR * (m_hat / (jnp.sqrt(v_hat) + ADAM_EPS) + ADAM_WD * w_ref[...])
        nm_ref[...] = nm
        nv_ref[...] = nv

    blk = pl.BlockSpec((tr, c), lambda i: (i, 0))
    outs = pl.pallas_call(
        body, name=name, grid=(rows // tr,),
        in_specs=[blk] * 4, out_specs=[blk] * 3,
        out_shape=[jax.ShapeDtypeStruct((rows, c), F32)] * 3,
        compiler_params=_params(("parallel",)),
    )(to2d(w), to2d(g), to2d(m), to2d(v))
    return tuple(o.reshape(shape) for o in outs)


_ANY = pl.BlockSpec(memory_space=pl.ANY)


def _place():
    x, y, c = lax.axis_index("x"), lax.axis_index("y"), lax.axis_index("c")
    others = [(1 - x, y), (x, 1 - y), (1 - x, 1 - y)]
    return x, y, c, others


def _all_gather_chips(shard, name):
    rws, cols = shard.shape
    half = rws // 2

    def body(x_ref, out_ref, send_sems, recv_sems, local_sem):
        x, y, c, others = _place()
        me = 2 * x + y
        sib = (x, y, 1 - c)

        def rows(chip, cc):
            return out_ref.at[chip, pl.ds(pl.multiple_of(cc * half, 16), half), :]

        def copy(k, src, dst, to):
            return pltpu.make_async_remote_copy(src_ref=src, dst_ref=dst, send_sem=send_sems.at[k],
                                                recv_sem=recv_sems.at[k], device_id=to, device_id_type=MESH)

        mine = pltpu.make_async_copy(x_ref, out_ref.at[me], local_sem)
        mine.start()
        my_half = x_ref.at[pl.ds(pl.multiple_of(c * half, 16), half), :]
        first = [copy(j, my_half, rows(me, c), (ox, oy, c)) for j, (ox, oy) in enumerate(others)]
        for cp in first:
            cp.start()
        passed = []
        for j, (ox, oy) in enumerate(others):
            blk = rows(2 * ox + oy, c)
            copy(j, blk, blk, (x, y, c)).wait_recv()
            fwd = copy(3 + j, blk, blk, sib)
            fwd.start()
            passed.append(fwd)
        for j, (ox, oy) in enumerate(others):
            blk = rows(2 * ox + oy, 1 - c)
            copy(3 + j, blk, blk, (x, y, c)).wait_recv()
        for cp in first + passed:
            cp.wait_send()
        mine.wait()

    return pl.pallas_call(
        body, name=name,
        in_specs=[_ANY], out_specs=_ANY,
        out_shape=jax.ShapeDtypeStruct((N_CHIPS, rws, cols), shard.dtype),
        scratch_shapes=[pltpu.SemaphoreType.DMA((6,)), pltpu.SemaphoreType.DMA((6,)), pltpu.SemaphoreType.DMA],
        compiler_params=pltpu.CompilerParams(has_side_effects=True),
    )(shard)


def _swap_halves(buf, name):
    nb, rws, cols = buf.shape
    half = rws // 2

    def body(b_ref, own_ref, sib_ref, send_sem, recv_sem, local_sem):
        x, y, c, _ = _place()
        keep = b_ref.at[:, pl.ds(pl.multiple_of(c * half, 16), half), :]
        give = b_ref.at[:, pl.ds(pl.multiple_of((1 - c) * half, 16), half), :]
        mine = pltpu.make_async_copy(keep, own_ref, local_sem)
        mine.start()
        cp = pltpu.make_async_remote_copy(src_ref=give, dst_ref=sib_ref, send_sem=send_sem, recv_sem=recv_sem,
                                          device_id=(x, y, 1 - c), device_id_type=MESH)
        cp.start()
        cp.wait()
        mine.wait()

    shp = jax.ShapeDtypeStruct((nb, half, cols), buf.dtype)
    return pl.pallas_call(
        body, name=name,
        in_specs=[_ANY], out_specs=[_ANY, _ANY], out_shape=[shp, shp],
        scratch_shapes=[pltpu.SemaphoreType.DMA, pltpu.SemaphoreType.DMA, pltpu.SemaphoreType.DMA],
        compiler_params=pltpu.CompilerParams(has_side_effects=True),
    )(buf)


def _add2(a, b, name):
    nb, rws, cols = a.shape
    t = _tile(rws, 256)
    if rws % t:
        t = rws

    def body(a_ref, b_ref, o_ref):
        o_ref[...] = (a_ref[...].astype(F32) + b_ref[...].astype(F32)).astype(o_ref.dtype)

    blk = pl.BlockSpec((1, t, cols), lambda i, j: (i, j, 0))
    return pl.pallas_call(
        body, name=name, grid=(nb, rws // t), in_specs=[blk, blk], out_specs=blk,
        out_shape=jax.ShapeDtypeStruct(a.shape, a.dtype),
        compiler_params=_params(("parallel", "parallel")),
    )(a, b)


def _exchange_chips(pre, name):
    nb, half, cols = pre.shape

    def body(p_ref, out_ref, send_sems, recv_sems, local_sem):
        x, y, c, others = _place()
        me = 2 * x + y
        mine = pltpu.make_async_copy(p_ref.at[me], out_ref.at[me], local_sem)
        mine.start()
        sends = []
        for j, (ox, oy) in enumerate(others):
            cp = pltpu.make_async_remote_copy(src_ref=p_ref.at[2 * ox + oy], dst_ref=out_ref.at[me],
                                              send_sem=send_sems.at[j], recv_sem=recv_sems.at[j],
                                              device_id=(ox, oy, c), device_id_type=MESH)
            cp.start()
            sends.append(cp)
        for j, (ox, oy) in enumerate(others):
            blk = out_ref.at[2 * ox + oy]
            pltpu.make_async_remote_copy(src_ref=blk, dst_ref=blk, send_sem=send_sems.at[j],
                                         recv_sem=recv_sems.at[j], device_id=(x, y, c),
                                         device_id_type=MESH).wait_recv()
        for cp in sends:
            cp.wait_send()
        mine.wait()

    return pl.pallas_call(
        body, name=name,
        in_specs=[_ANY], out_specs=_ANY, out_shape=jax.ShapeDtypeStruct(pre.shape, pre.dtype),
        scratch_shapes=[pltpu.SemaphoreType.DMA((3,)), pltpu.SemaphoreType.DMA((3,)), pltpu.SemaphoreType.DMA],
        compiler_params=pltpu.CompilerParams(has_side_effects=True),
    )(pre)


def _add4(parts, name):
    nb, half, cols = parts.shape
    t = _tile(half, 256)
    if half % t:
        t = half

    def body(p_ref, o_ref):
        acc = p_ref[0].astype(F32)
        for k in range(1, nb):
            acc = acc + p_ref[k].astype(F32)
        o_ref[...] = acc

    return pl.pallas_call(
        body, name=name, grid=(half // t,),
        in_specs=[pl.BlockSpec((nb, t, cols), lambda i: (0, i, 0))],
        out_specs=pl.BlockSpec((t, cols), lambda i: (i, 0)),
        out_shape=jax.ShapeDtypeStruct((half, cols), F32),
        compiler_params=_params(("parallel",)),
    )(parts)


def _join_halves(mine_half, name):
    half, cols = mine_half.shape

    def body(h_ref, out_ref, send_sem, recv_sem, local_sem):
        x, y, c, _ = _place()
        dst = out_ref.at[pl.ds(pl.multiple_of(c * half, 8), half), :]
        mine = pltpu.make_async_copy(h_ref, dst, local_sem)
        mine.start()
        cp = pltpu.make_async_remote_copy(src_ref=h_ref, dst_ref=dst, send_sem=send_sem, recv_sem=recv_sem,
                                          device_id=(x, y, 1 - c), device_id_type=MESH)
        cp.start()
        cp.wait()
        mine.wait()

    return pl.pallas_call(
        body, name=name,
        in_specs=[_ANY], out_specs=_ANY, out_shape=jax.ShapeDtypeStruct((2 * half, cols), mine_half.dtype),
        scratch_shapes=[pltpu.SemaphoreType.DMA, pltpu.SemaphoreType.DMA, pltpu.SemaphoreType.DMA],
        compiler_params=pltpu.CompilerParams(has_side_effects=True),
    )(mine_half)


def _reduce_scatter_chips(buf, tag):
    own, sib = _swap_halves(buf, "rs_swap_" + tag)
    pre = _add2(own, sib, "rs_add2_" + tag)
    parts = _exchange_chips(pre, "rs_xchg_" + tag)
    red = _add4(parts, "rs_add4_" + tag)
    return _join_halves(red, "rs_join_" + tag)


MAX_DMA_BYTES = 2 * 1024 * 1024
ROW_ALIGN = 16


def _pieces(rows, row_bytes):
    n = max(1, -(-(rows * row_bytes) // MAX_DMA_BYTES))
    step = -(-(-(-rows // n)) // ROW_ALIGN) * ROW_ALIGN
    return [(r, min(step, rows - r)) for r in range(0, rows, step)]


def _half_plan(arrays, row_axis):
    plan = []
    for a, arr in enumerate(arrays):
        row_bytes = math.prod(arr.shape[row_axis + 1:]) * arr.dtype.itemsize * (arr.shape[0] if row_axis else 1)
        plan += [(a, r0, nr) for r0, nr in _pieces(arr.shape[row_axis] // 2, row_bytes)]
    return plan


def _rows(start, size):
    return pl.ds(pl.multiple_of(start, ROW_ALIGN), size)


def _remote(src, dst, send_sems, recv_sems, k, to):
    return pltpu.make_async_remote_copy(src_ref=src, dst_ref=dst, send_sem=send_sems.at[k], recv_sem=recv_sems.at[k],
                                        device_id=to, device_id_type=MESH)


def _comm_call(body, name, ins, out_shapes, n_remote, n_local, aliases=None):
    return pl.pallas_call(
        body, name=name,
        in_specs=[_ANY] * len(ins), out_specs=[_ANY] * len(out_shapes), out_shape=out_shapes,
        scratch_shapes=[pltpu.SemaphoreType.DMA((n_remote,)), pltpu.SemaphoreType.DMA((n_remote,)),
                        pltpu.SemaphoreType.DMA((max(n_local, 1),))],
        input_output_aliases=aliases or {},
        compiler_params=pltpu.CompilerParams(has_side_effects=True),
    )(*ins)


def _cast_shard(w, l, me_idx, name):
    _, k, cols = w.shape
    tr = _tile(k, 256)
    if k % tr:
        tr = k

    def body(me_ref, w_ref, s_ref, land_ref):
        del me_ref
        v = w_ref[...].astype(BF16)
        s_ref[...] = v
        land_ref[...] = v

    grid_spec = pltpu.PrefetchScalarGridSpec(
        num_scalar_prefetch=1, grid=(k // tr,),
        in_specs=[pl.BlockSpec((None, tr, cols), lambda i, me: (l, i, 0))],
        out_specs=[pl.BlockSpec((tr, cols), lambda i, me: (i, 0)),
                   pl.BlockSpec((None, tr, cols), lambda i, me: (me[0], i, 0))])
    return pl.pallas_call(
        body, name=name, grid_spec=grid_spec,
        out_shape=[jax.ShapeDtypeStruct((k, cols), BF16), jax.ShapeDtypeStruct((N_CHIPS, k, cols), BF16)],
        compiler_params=_params(("parallel",)),
    )(me_idx, w)


def _gather_d2d(lands, name):
    n = len(lands)
    plan = _half_plan(lands, 1)
    plan = [(a, r0, nr) for a, r0, nr in plan]

    def body(*refs):
        out_refs = refs[n:2 * n]
        send_sems, recv_sems, _ = refs[2 * n:]
        x, y, c, others = _place()
        sends = []
        for i, (a, r0, nr) in enumerate(plan):
            rows = _rows(c * (lands[a].shape[1] // 2) + r0, nr)
            for j, (ox, oy) in enumerate(others):
                blk = out_refs[a].at[2 * ox + oy, rows, :]
                cp = _remote(blk, blk, send_sems, recv_sems, 3 * i + j, (x, y, 1 - c))
                cp.start()
                sends.append(cp)
        for i, (a, r0, nr) in enumerate(plan):
            rows = _rows((1 - c) * (lands[a].shape[1] // 2) + r0, nr)
            for j, (ox, oy) in enumerate(others):
                blk = out_refs[a].at[2 * ox + oy, rows, :]
                _remote(blk, blk, send_sems, recv_sems, 3 * i + j, (x, y, c)).wait_recv()
        for cp in sends:
            cp.wait_send()

    outs = [jax.ShapeDtypeStruct(a.shape, a.dtype) for a in lands]
    return _comm_call(body, name, lands, outs, 3 * len(plan), 0, aliases={a: a for a in range(n)})


def _rs_swap(ts, name):
    n = len(ts)
    plan = _half_plan(ts, 1)

    def body(*refs):
        t_refs, out_refs = refs[:n], refs[n:2 * n]
        send_sems, recv_sems, _ = refs[2 * n:]
        x, y, c, _o = _place()
        sends = []
        for i, (a, r0, nr) in enumerate(plan):
            src = t_refs[a].at[:, _rows((1 - c) * (ts[a].shape[1] // 2) + r0, nr), :]
            cp = _remote(src, out_refs[a].at[:, pl.ds(r0, nr), :], send_sems, recv_sems, i, (x, y, 1 - c))
            cp.start()
            sends.append(cp)
        for i, (a, r0, nr) in enumerate(plan):
            blk = out_refs[a].at[:, pl.ds(r0, nr), :]
            _remote(blk, blk, send_sems, recv_sems, i, (x, y, c)).wait_recv()
        for cp in sends:
            cp.wait_send()

    outs = [jax.ShapeDtypeStruct((t.shape[0], t.shape[1] // 2, t.shape[2]), t.dtype) for t in ts]
    return _comm_call(body, name, ts, outs, len(plan), 0)


def _add_half(t, got, c_idx, me_idx, name):
    nb, k, cols = t.shape
    half = k // 2

    def body(c_ref, me_ref, t_ref, g_ref, o_ref, mine_ref):
        del c_ref
        v = (t_ref[...].astype(F32) + g_ref[...].astype(F32)).astype(o_ref.dtype)
        o_ref[...] = v

        @pl.when(pl.program_id(0) == me_ref[0])
        def _():
            mine_ref[...] = v

    blk = pl.BlockSpec((1, half, cols), lambda i, c, me: (i, 0, 0))
    grid_spec = pltpu.PrefetchScalarGridSpec(
        num_scalar_prefetch=2, grid=(nb,),
        in_specs=[pl.BlockSpec((1, half, cols), lambda i, c, me: (i, c[0], 0)), blk],
        out_specs=[blk, pl.BlockSpec((1, half, cols), lambda i, c, me: (me[0], 0, 0))])
    shp = jax.ShapeDtypeStruct(got.shape, got.dtype)
    return pl.pallas_call(
        body, name=name, grid_spec=grid_spec, out_shape=[shp, shp],
        compiler_params=_params(("arbitrary",)),
    )(c_idx, me_idx, t, got)


def _add4_half(parts, c_idx, name):
    nb, half, cols = parts.shape
    t = _tile(half, 256)
    if half % t:
        t = half
    steps = half // t

    def body(c_ref, p_ref, o_ref):
        del c_ref
        acc = p_ref[0].astype(F32)
        for k in range(1, nb):
            acc = acc + p_ref[k].astype(F32)
        o_ref[...] = acc

    grid_spec = pltpu.PrefetchScalarGridSpec(
        num_scalar_prefetch=1, grid=(steps,),
        in_specs=[pl.BlockSpec((nb, t, cols), lambda i, c: (0, i, 0))],
        out_specs=pl.BlockSpec((t, cols), lambda i, c: (c[0] * steps + i, 0)))
    return pl.pallas_call(
        body, name=name, grid_spec=grid_spec, out_shape=jax.ShapeDtypeStruct((2 * half, cols), F32),
        compiler_params=_params(("parallel",)),
    )(c_idx, parts)


def _rs_join(fulls, name):
    n = len(fulls)
    plan = _half_plan(fulls, 0)

    def body(*refs):
        out_refs = refs[n:2 * n]
        send_sems, recv_sems, _ = refs[2 * n:]
        x, y, c, _o = _place()
        sends = []
        for i, (a, r0, nr) in enumerate(plan):
            blk = out_refs[a].at[_rows(c * (fulls[a].shape[0] // 2) + r0, nr), :]
            cp = _remote(blk, blk, send_sems, recv_sems, i, (x, y, 1 - c))
            cp.start()
            sends.append(cp)
        for i, (a, r0, nr) in enumerate(plan):
            blk = out_refs[a].at[_rows((1 - c) * (fulls[a].shape[0] // 2) + r0, nr), :]
            _remote(blk, blk, send_sems, recv_sems, i, (x, y, c)).wait_recv()
        for cp in sends:
            cp.wait_send()

    outs = [jax.ShapeDtypeStruct(f.shape, f.dtype) for f in fulls]
    return _comm_call(body, name, fulls, outs, len(plan), 0, aliases={a: a for a in range(n)})


_HBM = pl.BlockSpec(memory_space=pltpu.HBM)
_SEM = pl.BlockSpec(memory_space=pltpu.SEMAPHORE)
_EFFECT = pltpu.SideEffectType.DATAFLOW_SIDE_EFFECTING


def _ici_plan(kind, a_list):
    if kind == "gather":
        return _half_plan(a_list, 0)
    plan = []
    for a, p in enumerate(a_list):
        plan += [(a, r0, nr) for r0, nr in _pieces(p.shape[1], p.shape[2] * p.dtype.itemsize)]
    return plan


def _ici_refs(kind, a_ref, b_ref, a_shape, r0, nr, c, me, peer):
    if kind == "gather":
        rows = _rows(c * (a_shape[0] // 2) + r0, nr)
        return a_ref.at[rows, :], b_ref.at[me, rows, :], b_ref.at[peer, rows, :]
    rows = pl.ds(r0, nr)
    return a_ref.at[peer, rows, :], b_ref.at[me, rows, :], b_ref.at[peer, rows, :]


def _ici_start(kind, a_list, b_list, name):
    n = len(a_list)
    plan = _ici_plan(kind, a_list)
    shapes = [a.shape for a in a_list]

    def body(*refs):
        a_refs, b_refs = refs[:n], refs[n:2 * n]
        send_sems, recv_sems = refs[2 * n], refs[2 * n + 1]
        token = refs[4 * n + 2]
        x, y, c, others = _place()
        me = 2 * x + y
        for i, (a, r0, nr) in enumerate(plan):
            for j, (ox, oy) in enumerate(others):
                src, dst, _ = _ici_refs(kind, a_refs[a], b_refs[a], shapes[a], r0, nr, c, me, 2 * ox + oy)
                _remote(src, dst, send_sems, recv_sems, 3 * i + j, (ox, oy, c)).start()
        token[...] = jnp.zeros_like(token)

    hbm = lambda v: pltpu.HBM(v.shape, v.dtype)
    ncp = 3 * len(plan)
    outs = pl.pallas_call(
        body, name=name,
        in_specs=[_HBM] * (2 * n),
        out_specs=[_SEM, _SEM] + [_HBM] * (2 * n) + [pl.BlockSpec(memory_space=pltpu.VMEM)],
        out_shape=[pltpu.SemaphoreType.DMA((ncp,)), pltpu.SemaphoreType.DMA((ncp,))]
                  + [hbm(v) for v in a_list] + [hbm(v) for v in b_list] + [jax.ShapeDtypeStruct((8, LANES), F32)],
        input_output_aliases={i: 2 + i for i in range(2 * n)},
        compiler_params=pltpu.CompilerParams(has_side_effects=_EFFECT),
    )(*[pltpu.with_memory_space_constraint(v, pltpu.HBM) for v in list(a_list) + list(b_list)])
    return outs[0], outs[1], outs[2:2 + n], outs[2 + n:2 + 2 * n], outs[2 + 2 * n]


def _ici_wait(kind, started, after, name):
    send_sems, recv_sems, a_list, b_list, _ = started
    n = len(a_list)
    plan = _ici_plan(kind, a_list)
    shapes = [a.shape for a in a_list]

    def body(*refs):
        a_refs, b_refs = refs[:n], refs[n:2 * n]
        send_sems, recv_sems = refs[2 * n], refs[2 * n + 1]
        x, y, c, others = _place()
        me = 2 * x + y
        for i, (a, r0, nr) in enumerate(plan):
            for j, (ox, oy) in enumerate(others):
                src, dst, land = _ici_refs(kind, a_refs[a], b_refs[a], shapes[a], r0, nr, c, me, 2 * ox + oy)
                _remote(src, dst, send_sems, recv_sems, 3 * i + j, (ox, oy, c)).wait_send()
                _remote(land, land, send_sems, recv_sems, 3 * i + j, (x, y, c)).wait_recv()

    hbm = lambda v: pltpu.HBM(v.shape, v.dtype)
    outs = pl.pallas_call(
        body, name=name,
        in_specs=[_HBM] * (2 * n) + [_SEM, _SEM, _ANY],
        out_specs=[_HBM] * (2 * n),
        out_shape=[hbm(v) for v in a_list] + [hbm(v) for v in b_list],
        input_output_aliases={i: i for i in range(2 * n)},
        compiler_params=pltpu.CompilerParams(has_side_effects=_EFFECT),
    )(*a_list, *b_list, send_sems, recv_sems, after)
    return outs[n:]


def _rs_begin(ts, c_idx, me_idx, tag):
    got = _rs_swap(ts, "rs_swap_" + tag)
    pairs = [_add_half(t, g, c_idx, me_idx, f"rs_add2_{tag}_{a}") for a, (t, g) in enumerate(zip(ts, got))]
    return _ici_start("scatter", [p for p, _ in pairs], [m for _, m in pairs], "rs_xchg_start_" + tag)


def _rs_finish(started, after, c_idx, tag):
    parts = _ici_wait("scatter", started, after, "rs_xchg_wait_" + tag)
    fulls = [_add4_half(p, c_idx, f"rs_add4_{tag}_{a}") for a, p in enumerate(parts)]
    return _rs_join(fulls, "rs_join_" + tag)


def _pack_rows(pieces, rows, dtype):
    flat = jnp.concatenate([p.astype(dtype).reshape(-1) for p in pieces])
    return jnp.pad(flat, (0, rows * PACK_COLS - flat.shape[0])).reshape(rows, PACK_COLS)


def _unpack(flat, shapes):
    out, off = [], 0
    for shp in shapes:
        size = math.prod(shp)
        out.append(flat[off:off + size].reshape(shp))
        off += size
    return out


def _rows_for(n_elems, mult):
    rows = -(-n_elems // PACK_COLS)
    return -(-rows // mult) * mult


BIG_SHARDS = [("w_in", (D_MODEL, 1474)), ("w_branch_att", (D_ATT, 256)), ("w_branch_conv", (D_CONV, 256)),
              ("w_branch_sgu", (D_SGU, 256)), ("w_out", (256, D_MODEL)), ("w_ffn_up", (D_MODEL, FF_BLK)),
              ("w_ffn_down", (D_FF // N_CHIPS, D_MODEL))]
SMALL_SHARDS = [("b_gate", (3, 256)), ("conv_mix_w", (3, 64)), ("conv_ffn_w", (3, FF_BLK))]
REPLICATED = [("pre_mix_g", (D_MODEL,)), ("post_mix_g", (D_MODEL,)), ("pre_ffn_g", (D_MODEL,)),
              ("post_ffn_g", (D_MODEL,)), ("b_forget", (N_HEADS,)), ("sgu_ln_g", (D_SGU,)), ("sgu_ln_b", (D_SGU,)),
              ("sgu_w", (N_GROUPS, CHUNK, CHUNK)), ("sgu_b", (N_GROUPS, CHUNK))]
WEIGHT_ORDER = ["pre_mix_g", "post_mix_g", "pre_ffn_g", "post_ffn_g", "w_in", "b_forget", "b_gate", "conv_mix_w",
                "sgu_ln_g", "sgu_ln_b", "sgu_w", "sgu_b", "w_branch_att", "w_branch_conv", "w_branch_sgu", "w_out",
                "w_ffn_up", "conv_ffn_w", "w_ffn_down"]

_SMALL_ELEMS = sum(math.prod(s) for _, s in SMALL_SHARDS)
_REP_ELEMS = sum(math.prod(s) for _, s in REPLICATED)
_REP_QUARTER = -(-(DEPTH * _REP_ELEMS) // N_CHIPS)
SMALL_PARAM_ROWS = _rows_for(DEPTH * _SMALL_ELEMS, 32)
SMALL_ROWS = _rows_for(DEPTH * _SMALL_ELEMS + _REP_QUARTER, 32)
IN_WIDTH = 5896
IN_SHARD = IN_WIDTH // N_CHIPS
IN_SHARD_PAD = 1536
IN_PAD = 6144


def _gather_small(wts):
    shard = _pack_rows([wts[n] for n, _ in SMALL_SHARDS], SMALL_PARAM_ROWS, F32)
    full = _all_gather_chips(shard, "gather_small_params").reshape(N_CHIPS, -1)
    per_chip = [_unpack(full[j], [(DEPTH,) + s for _, s in SMALL_SHARDS]) for j in range(N_CHIPS)]
    return {n: jnp.concatenate([per_chip[j][i] for j in range(N_CHIPS)], axis=-1)
            for i, (n, _) in enumerate(SMALL_SHARDS)}


def _gather_begin(wts, l, me_idx):
    cast = [_cast_shard(wts[n], l, me_idx, "cast_" + n) for n, _ in BIG_SHARDS]
    return _ici_start("gather", [sh for sh, _ in cast], [ld for _, ld in cast], "gather_ici_start")


def _gather_finish(started, after):
    lands = _ici_wait("gather", started, after, "gather_ici_wait")
    return dict(zip([n for n, _ in BIG_SHARDS], _gather_d2d(lands, "gather_d2d")))


def _pad_rows(a, rows):
    return jnp.pad(a, ((0, rows - a.shape[0]), (0, 0)))


def _whole_cols(land):
    return land.transpose(1, 0, 2).reshape(land.shape[1], -1)


_O_F = 3 * D_ATT
_O_B = _O_F + N_HEADS
_O_GL = _O_B + 3 * D_CONV + 2 * D_SGU


def _prep_layer(wts, lands, small, l):
    w_in = _whole_cols(lands["w_in"])
    up = lands["w_ffn_up"]
    cf = small["conv_ffn_w"][l]
    blk = lambda a, j: a[:, j * FF_BLK:(j + 1) * FF_BLK]
    return {
        "w_p": jnp.concatenate([w_in[:, _O_GL:], w_in[:, :_O_F], w_in[:, _O_B:_O_GL]], axis=1),
        "w_in_bwd": jnp.concatenate([w_in[:, :_O_F], w_in[:, _O_B:], w_in[:, _O_F:_O_B],
                                     jnp.zeros((D_MODEL, IN_PAD - IN_WIDTH), BF16)], axis=1),
        "wf_t": _pad_rows(w_in[:, _O_F:_O_B].T, F_ROWS),
        "b_forget": _pad_rows(wts["b_forget"][l].reshape(N_HEADS, 1), F_ROWS),
        "b_gate": _pad_rows(small["b_gate"][l], 8),
        "conv_mix_w": _pad_rows(small["conv_mix_w"][l], 8),
        "w_att": _whole_cols(lands["w_branch_att"]), "w_conv": _whole_cols(lands["w_branch_conv"]),
        "w_sgu": _whole_cols(lands["w_branch_sgu"]),
        "w_out": lands["w_out"].reshape(D_MODEL, D_MODEL),
        "w_up": jnp.concatenate([up[0], up[2], up[1], up[3]], axis=1),
        "conv_ffn_w": _pad_rows(jnp.concatenate([blk(cf, 0), blk(cf, 2), blk(cf, 1), blk(cf, 3)], axis=1), 8),
        "w_down": lands["w_ffn_down"].reshape(D_FF, D_MODEL),
        "pre_mix_g": wts["pre_mix_g"][l].reshape(1, -1), "post_mix_g": wts["post_mix_g"][l].reshape(1, -1),
        "pre_ffn_g": wts["pre_ffn_g"][l].reshape(1, -1), "post_ffn_g": wts["post_ffn_g"][l].reshape(1, -1),
        "ln_g": wts["sgu_ln_g"][l].reshape(1, -1), "ln_b": wts["sgu_ln_b"][l].reshape(1, -1),
        "sgu_w": wts["sgu_w"][l],
        "sgu_bias": jnp.repeat(wts["sgu_b"][l].T, HEAD_DIM, axis=1),
    }


def _layer_fwd(x, p, dep=None):
    s = x.shape[0]
    xn = _rms_fwd(x, p["pre_mix_g"], "rms_pre_mix", dep)
    h = _mm(xn, p["w_p"], "nn", BF16, "mm_in", s, 256, D_MODEL)
    f_row = _mm(p["wf_t"], xn, "nt", F32, "mm_forget", F_ROWS, 2048, D_MODEL)
    ck = _gate_fwd(f_row, p["b_forget"], "gate_fwd")
    o, o_f32, lse = _attn_fwd(h, ck, "attn_fwd")
    yc = _sconv_fwd(h, p["conv_mix_w"], "sconv_fwd")
    ys = _sgu_fwd(h, p["ln_g"], p["ln_b"], p["sgu_w"], p["sgu_bias"], "sgu_fwd")
    merged = _merge_fwd(h, (o, yc, ys), (p["w_att"], p["w_conv"], p["w_sgu"]), p["b_gate"], "merge_fwd")
    mo = _mm(merged, p["w_out"], "nn", F32, "mm_out", 2048, 512, D_MODEL)
    x1 = _resid_post(x, mo, p["post_mix_g"], "post_mix")
    xn2 = _rms_fwd(x1, p["pre_ffn_g"], "rms_pre_ffn")
    h2 = _mm(xn2, p["w_up"], "nn", BF16, "mm_up", 2048, 512, D_MODEL)
    pact = _ffn_act_fwd(h2, p["conv_ffn_w"], "ffn_act_fwd")
    ff = _mm(pact, p["w_down"], "nn", F32, "mm_down", 2048, 512, FF_BLK)
    x2 = _resid_post(x1, ff, p["post_ffn_g"], "post_ffn")
    saved = dict(x=x, xn=xn, h=h, f_row=f_row, ck=ck, o=o, o_f32=o_f32, lse=lse, yc=yc, ys=ys, merged=merged, mo=mo, x1=x1,
                 xn2=xn2, h2=h2, pact=pact, ff=ff)
    return x2, saved


def _layer_bwd(dx2, p, sv, dep=None):
    s = dx2.shape[0]
    g = {}
    same = lambda b: b
    dff, g["post_ffn_g"] = _rms_bwd(sv["ff"], p["post_ffn_g"], [dx2], None, BF16, "post_ffn_bwd", dep)
    dpact = _mm(dff, p["w_down"], "nt", BF16, "mm_down_dx", 1024, FF_BLK, D_MODEL)
    t_down = _mm(sv["pact"], dff, "tn", BF16, "mm_down_dw", 256, D_MODEL, s).reshape(N_CHIPS, -1, D_MODEL)
    dh2, dconv_ffn = _ffn_act_conv_bwd(sv["h2"], p["conv_ffn_w"], dpact, "ffn_act_conv_bwd")
    dxn2 = _mm(dh2, p["w_up"], "nt", F32, "mm_up_dx", 1024, D_MODEL, FF_BLK)
    t_up = _mm(sv["xn2"], dh2, "tn", BF16, "mm_up_dw", 512, FF_BLK, s, chip_of=lambda b: (b % 2) * 2 + b // 2)
    dx1, g["pre_ffn_g"] = _rms_bwd(sv["x1"], p["pre_ffn_g"], [dxn2], dx2, F32, "pre_ffn_bwd")
    dmo, g["post_mix_g"] = _rms_bwd(sv["mo"], p["post_mix_g"], [dx1], None, BF16, "post_mix_bwd")
    dmerged = _mm(dmo, p["w_out"], "nt", F32, "mm_out_dx", 2048, 512, D_MODEL)
    t_out = _mm(sv["merged"], dmo, "tn", BF16, "mm_out_dw", 512, D_MODEL, s).reshape(N_CHIPS, -1, D_MODEL)
    acts = (sv["o"], sv["yc"], sv["ys"])
    ws = (p["w_att"], p["w_conv"], p["w_sgu"])
    dy_a, dy_c, dy_s, dgl, db_gate = _merge_bwd(sv["h"], acts, ws, p["b_gate"], dmerged, "merge_bwd")
    do = _mm(dy_a, p["w_att"], "nt", BF16, "mm_att_dx", 2048, D_ATT, D_MODEL)
    dyc = _mm(dy_c, p["w_conv"], "nt", BF16, "mm_conv_dx", 2048, D_CONV, D_MODEL)
    dys = _mm(dy_s, p["w_sgu"], "nt", BF16, "mm_sgu_dx", 2048, D_SGU, D_MODEL)
    t_att = _mm(sv["o"], dy_a, "tn", BF16, "mm_att_dw", D_ATT, 256, s, chip_of=same)
    t_conv = _mm(sv["yc"], dy_c, "tn", BF16, "mm_conv_dw", D_CONV, 256, s, chip_of=same)
    t_sgu = _mm(sv["ys"], dy_s, "tn", BF16, "mm_sgu_dw", D_SGU, 256, s, chip_of=same)
    d_conv, dconv_mix = _sconv_bwd(sv["h"], p["conv_mix_w"], dyc, "sconv_bwd")
    d_sgu, g["sgu_ln_g"], g["sgu_ln_b"], g["sgu_w"], dbias = _sgu_bwd(
        sv["h"], p["ln_g"], p["ln_b"], p["sgu_w"], p["sgu_bias"], dys, "sgu_bwd")
    dq, dk, dv, dc_even, dc_odd = _attn_bwd(sv["h"], sv["ck"], sv["o_f32"], sv["lse"], do, "attn_bwd")
    df, db_forget = _gate_bwd(sv["f_row"], p["b_forget"], dc_even, dc_odd, "gate_bwd")
    f_cols = jnp.concatenate([df[:N_HEADS].T, jnp.zeros((s, IN_PAD - IN_WIDTH), BF16)], axis=1)
    dh = jnp.concatenate([dq.astype(BF16), dk, dv, d_conv, d_sgu, dgl, f_cols], axis=1)
    dxn = _mm(dh, p["w_in_bwd"], "nt", F32, "mm_in_dx", 1024, D_MODEL, 2048)
    dw_bwd = _mm(sv["xn"], dh, "tn", F32, "mm_in_dw", D_MODEL, 512, s)
    n_rest = IN_WIDTH - N_HEADS
    dw_in = jnp.concatenate([dw_bwd[:, :_O_F], dw_bwd[:, n_rest:IN_WIDTH], dw_bwd[:, _O_F:n_rest],
                             jnp.zeros((D_MODEL, IN_SHARD_PAD - IN_SHARD), F32)], axis=1)
    t_in = jnp.stack([dw_in[:, j * IN_SHARD:j * IN_SHARD + IN_SHARD_PAD] for j in range(N_CHIPS)]).astype(BF16)
    dx, g["pre_mix_g"] = _rms_bwd(sv["x"], p["pre_mix_g"], [dxn], dx1, F32, "pre_mix_bwd")
    blk = lambda a, j: a[:, j * FF_BLK:(j + 1) * FF_BLK]
    g["conv_ffn_w"] = jnp.concatenate([blk(dconv_ffn, 0), blk(dconv_ffn, 2), blk(dconv_ffn, 1),
                                       blk(dconv_ffn, 3)], axis=1)[:3]
    g["conv_mix_w"] = dconv_mix[:3]
    g["b_gate"] = db_gate[:3]
    g["b_forget"] = db_forget[:N_HEADS, 0]
    g["sgu_b"] = jnp.sum(dbias.reshape(CHUNK, N_GROUPS, HEAD_DIM), axis=-1).T
    for n in ("pre_mix_g", "post_mix_g", "pre_ffn_g", "post_ffn_g", "sgu_ln_g", "sgu_ln_b"):
        g[n] = g[n].reshape(-1)
    return dx, [t_in, t_att, t_conv, t_sgu, t_out, t_up, t_down], g


def _shard_cols(a, j):
    w = a.shape[-1] // N_CHIPS
    return a[..., j * w:(j + 1) * w]


def kernel(x, pre_mix_g, post_mix_g, pre_ffn_g, post_ffn_g, w_in, b_forget, b_gate, conv_mix_w, sgu_ln_g, sgu_ln_b, sgu_w, sgu_b, w_branch_att, w_branch_conv, w_branch_sgu, w_out, w_ffn_up, conv_ffn_w, w_ffn_down, loss_target, m_pre_mix_g, m_post_mix_g, m_pre_ffn_g, m_post_ffn_g, m_w_in, m_b_forget, m_b_gate, m_conv_mix_w, m_sgu_ln_g, m_sgu_ln_b, m_sgu_w, m_sgu_b, m_w_branch_att, m_w_branch_conv, m_w_branch_sgu, m_w_out, m_w_ffn_up, m_conv_ffn_w, m_w_ffn_down, v_pre_mix_g, v_post_mix_g, v_pre_ffn_g, v_post_ffn_g, v_w_in, v_b_forget, v_b_gate, v_conv_mix_w, v_sgu_ln_g, v_sgu_ln_b, v_sgu_w, v_sgu_b, v_w_branch_att, v_w_branch_conv, v_w_branch_sgu, v_w_out, v_w_ffn_up, v_conv_ffn_w, v_w_ffn_down):
    wts = dict(pre_mix_g=pre_mix_g, post_mix_g=post_mix_g, pre_ffn_g=pre_ffn_g, post_ffn_g=post_ffn_g, w_in=w_in,
               b_forget=b_forget, b_gate=b_gate, conv_mix_w=conv_mix_w, sgu_ln_g=sgu_ln_g, sgu_ln_b=sgu_ln_b,
               sgu_w=sgu_w, sgu_b=sgu_b, w_branch_att=w_branch_att, w_branch_conv=w_branch_conv,
               w_branch_sgu=w_branch_sgu, w_out=w_out, w_ffn_up=w_ffn_up, conv_ffn_w=conv_ffn_w,
               w_ffn_down=w_ffn_down)
    moms = dict(pre_mix_g=m_pre_mix_g, post_mix_g=m_post_mix_g, pre_ffn_g=m_pre_ffn_g, post_ffn_g=m_post_ffn_g,
                w_in=m_w_in, b_forget=m_b_forget, b_gate=m_b_gate, conv_mix_w=m_conv_mix_w, sgu_ln_g=m_sgu_ln_g,
                sgu_ln_b=m_sgu_ln_b, sgu_w=m_sgu_w, sgu_b=m_sgu_b, w_branch_att=m_w_branch_att,
                w_branch_conv=m_w_branch_conv, w_branch_sgu=m_w_branch_sgu, w_out=m_w_out, w_ffn_up=m_w_ffn_up,
                conv_ffn_w=m_conv_ffn_w, w_ffn_down=m_w_ffn_down)
    vels = dict(pre_mix_g=v_pre_mix_g, post_mix_g=v_post_mix_g, pre_ffn_g=v_pre_ffn_g, post_ffn_g=v_post_ffn_g,
                w_in=v_w_in, b_forget=v_b_forget, b_gate=v_b_gate, conv_mix_w=v_conv_mix_w, sgu_ln_g=v_sgu_ln_g,
                sgu_ln_b=v_sgu_ln_b, sgu_w=v_sgu_w, sgu_b=v_sgu_b, w_branch_att=v_w_branch_att,
                w_branch_conv=v_w_branch_conv, w_branch_sgu=v_w_branch_sgu, w_out=v_w_out, w_ffn_up=v_w_ffn_up,
                conv_ffn_w=v_conv_ffn_w, w_ffn_down=v_w_ffn_down)

    c_idx = lax.axis_index("c").astype(jnp.int32).reshape(1)
    me_idx = (2 * lax.axis_index("x") + lax.axis_index("y")).astype(jnp.int32).reshape(1)
    small = _gather_small(wts)

    xs = x[0]
    layers, saved = [], []
    lands = _gather_finish(_gather_begin(wts, 0, me_idx), xs)
    for l in range(DEPTH):
        p = _prep_layer(wts, lands, small, l)
        nxt = _gather_begin(wts, l + 1, me_idx) if l + 1 < DEPTH else None
        xs, sv = _layer_fwd(xs, p, nxt[4] if nxt else None)
        if nxt:
            lands = _gather_finish(nxt, xs)
        layers.append(p)
        saved.append(sv)
    dy, loss_part = _loss_head(xs, loss_target[0], "loss_head")
    loss = lax.psum(loss_part[0, 0], ("x", "y", "c"))

    big_red = [None] * DEPTH
    small_grads = [None] * DEPTH
    pending = None
    for l in reversed(range(DEPTH)):
        dy, ts, small_grads[l] = _layer_bwd(dy, layers[l], saved[l], pending[4] if pending else None)
        if pending:
            big_red[l + 1] = _rs_finish(pending, dy, c_idx, "big")
        pending = _rs_begin(ts, c_idx, me_idx, "big")
    grad_x = dy[None]

    rep_flat = jnp.concatenate([small_grads[l][n].reshape(-1) for l in range(DEPTH) for n, _ in REPLICATED])
    rep_flat = jnp.pad(rep_flat, (0, N_CHIPS * _REP_QUARTER - rep_flat.shape[0]))
    rows = []
    for j in range(N_CHIPS):
        pieces = [_shard_cols(small_grads[l][n], j) for l in range(DEPTH) for n, _ in SMALL_SHARDS]
        pieces.append(rep_flat[j * _REP_QUARTER:(j + 1) * _REP_QUARTER])
        rows.append(_pack_rows(pieces, SMALL_ROWS, F32))
    small_red = _reduce_scatter_chips(jnp.stack(rows), "small")
    small_all = _all_gather_chips(small_red, "gather_small")
    big_red[0] = _rs_finish(pending, small_all, c_idx, "big")
    small_all = small_all.reshape(N_CHIPS, -1)

    grads = {}
    for i, (n, _) in enumerate(BIG_SHARDS):
        grads[n] = jnp.stack([big_red[l][i][:, :IN_SHARD] if n == "w_in" else big_red[l][i] for l in range(DEPTH)])
    mine_small = small_red.reshape(-1)
    parts = _unpack(mine_small, [s for _ in range(DEPTH) for _, s in SMALL_SHARDS])
    for i, (n, _) in enumerate(SMALL_SHARDS):
        grads[n] = jnp.stack([parts[l * len(SMALL_SHARDS) + i] for l in range(DEPTH)])
    off = DEPTH * _SMALL_ELEMS
    rep_all = jnp.concatenate([small_all[j, off:off + _REP_QUARTER] for j in range(N_CHIPS)])
    parts = _unpack(rep_all, [s for _ in range(DEPTH) for _, s in REPLICATED])
    for i, (n, _) in enumerate(REPLICATED):
        grads[n] = jnp.stack([parts[l * len(REPLICATED) + i] for l in range(DEPTH)])

    deltas, new_m, new_v = {}, {}, {}
    for n in WEIGHT_ORDER:
        deltas[n], new_m[n], new_v[n] = _adamw(wts[n], grads[n], moms[n], vels[n], "adamw_" + n)
    return (loss, grad_x, *[grads[n] for n in WEIGHT_ORDER], *[deltas[n] for n in WEIGHT_ORDER],
            *[new_m[n] for n in WEIGHT_ORDER], *[new_v[n] for n in WEIGHT_ORDER])
```

```python
import functools
import math

import jax
import jax.numpy as jnp
from jax import lax
from jax.experimental import pallas as pl
from jax.experimental.pallas import tpu as pltpu

F32 = jnp.float32
BF16 = jnp.bfloat16
MXU_DTYPE = jnp.bfloat16

D_MODEL = 1024
HEAD_DIM = 64
N_HEADS = 8
D_ATT = 512
D_CONV = 256
D_SGU = 256
N_GROUPS = 4
CHUNK = 128
D_FF = 2816
DEPTH = 4
RMS_EPS = 1e-6
LN_EPS = 1e-5
N_CHIPS = 4
LANES = 128
PACK_COLS = 1024
HALO = 16

ADAM_LR = 0.001
ADAM_B1 = 0.9
ADAM_B2 = 0.999
ADAM_EPS = 1e-08
ADAM_WD = 0.01
ADAM_STEP = 10

OFF_GL = 0
OFF_Q = 3 * D_MODEL
OFF_K = OFF_Q + D_ATT
OFF_V = OFF_K + D_ATT
OFF_BG = OFF_V + D_ATT
OFF_CG = OFF_BG + D_CONV
OFF_HC = OFF_CG + D_CONV
OFF_U = OFF_HC + D_CONV
OFF_VS = OFF_U + D_SGU
W_P = OFF_VS + D_SGU
F_ROWS = 16

VMEM_LIMIT = 56 * 1024 * 1024
MESH = pl.DeviceIdType.MESH


def _params(sem=None):
    if sem is None:
        return pltpu.CompilerParams(vmem_limit_bytes=VMEM_LIMIT)
    return pltpu.CompilerParams(dimension_semantics=sem, vmem_limit_bytes=VMEM_LIMIT)


def _tile(dim, pref):
    if dim <= pref:
        return dim
    if dim % pref == 0:
        return pref
    return dim


_DIMS = {"nn": (((1,), (0,)), ((), ())), "nt": (((1,), (1,)), ((), ())), "tn": (((0,), (0,)), ((), ()))}


def _mm(a, b, mode, out_dtype, name, tm, tn, tk, chip_of=None):
    if mode == "tn":
        K, M = a.shape
    else:
        M, K = a.shape
    N = b.shape[0] if mode == "nt" else b.shape[1]
    tm, tn, tk = _tile(M, tm), _tile(N // N_CHIPS if chip_of else N, tn), _tile(K, tk)
    nk = K // tk
    dims = _DIMS[mode]

    def body(a_ref, b_ref, o_ref, *acc):
        part = lax.dot_general(a_ref[...].astype(MXU_DTYPE), b_ref[...].astype(MXU_DTYPE), dims,
                               preferred_element_type=F32)
        if nk == 1:
            o_ref[...] = part.astype(o_ref.dtype)
        else:
            acc_ref = acc[0]
            k = pl.program_id(2)

            @pl.when(k == 0)
            def _():
                acc_ref[...] = part

            @pl.when(k > 0)
            def _():
                acc_ref[...] += part

            @pl.when(k == nk - 1)
            def _():
                o_ref[...] = acc_ref[...].astype(o_ref.dtype)

    if mode == "tn":
        a_spec = pl.BlockSpec((tk, tm), lambda i, j, k: (k, i))
    else:
        a_spec = pl.BlockSpec((tm, tk), lambda i, j, k: (i, k))
    if mode == "nt":
        b_spec = pl.BlockSpec((tn, tk), lambda i, j, k: (j, k))
    else:
        b_spec = pl.BlockSpec((tk, tn), lambda i, j, k: (k, j))
    if chip_of is None:
        out_spec = pl.BlockSpec((tm, tn), lambda i, j, k: (i, j))
        out_shape = jax.ShapeDtypeStruct((M, N), out_dtype)
    else:
        per = (N // N_CHIPS) // tn
        out_spec = pl.BlockSpec((None, tm, tn), lambda i, j, k: (chip_of(j // per), i, j % per))
        out_shape = jax.ShapeDtypeStruct((N_CHIPS, M, N // N_CHIPS), out_dtype)
    return pl.pallas_call(
        body,
        name=name,
        grid=(M // tm, N // tn, nk),
        in_specs=[a_spec, b_spec],
        out_specs=out_spec,
        out_shape=out_shape,
        scratch_shapes=[pltpu.VMEM((tm, tn), F32)] if nk > 1 else [],
        compiler_params=_params(("parallel", "parallel", "arbitrary")),
    )(a, b)


_GELU_K = math.sqrt(2.0 / math.pi)
_GELU_C = 0.044715


def _gelu(x):
    t = jnp.tanh(_GELU_K * (x + _GELU_C * (x * x * x)))
    return x * (0.5 * (1.0 + t))


def _gelu_and_grad(x):
    x2 = x * x
    t = jnp.tanh(_GELU_K * (x + _GELU_C * (x2 * x)))
    cdf = 0.5 * (1.0 + t)
    dcdf = 0.5 * (1.0 - t * t) * (_GELU_K * (1.0 + 3.0 * _GELU_C * x2))
    return x * cdf, cdf + x * dcdf


def _sigmoid(x):
    return 1.0 / (1.0 + jnp.exp(-x))


def _shift_down(cur, prev, k):
    h = prev.shape[0]
    ext = jnp.concatenate([prev, cur], axis=0)
    return pltpu.roll(ext, k, 0)[h:]


def _shift_up(cur, nxt, k):
    t, h = cur.shape[0], nxt.shape[0]
    ext = jnp.concatenate([cur, nxt], axis=0)
    return pltpu.roll(ext, t + h - k, 0)[:t]


def _row_sum8(x):
    t, c = x.shape
    return jnp.sum(x.reshape(t // 8, 8, c), axis=0)


_DEP = pl.BlockSpec((8, LANES), lambda i: (0, 0))


def _rms_fwd(x, g, name, dep=None):
    s, d = x.shape
    t = _tile(s, 512)

    def body(x_ref, g_ref, *rest):
        o_ref = rest[-1]
        xv = x_ref[...]
        r = lax.rsqrt(jnp.mean(xv * xv, axis=-1, keepdims=True) + RMS_EPS)
        o_ref[...] = (xv * r * g_ref[...]).astype(o_ref.dtype)

    deps = [] if dep is None else [dep]
    return pl.pallas_call(
        body, name=name, grid=(s // t,),
        in_specs=[pl.BlockSpec((t, d), lambda i: (i, 0)), pl.BlockSpec((1, d), lambda i: (0, 0))] + [_DEP] * len(deps),
        out_specs=pl.BlockSpec((t, d), lambda i: (i, 0)),
        out_shape=jax.ShapeDtypeStruct((s, d), BF16),
        compiler_params=_params(("parallel",)),
    )(x, g, *deps)


def _resid_post(x, y, g, name):
    s, d = x.shape
    t = _tile(s, 512)

    def body(x_ref, y_ref, g_ref, o_ref):
        yv = y_ref[...]
        r = lax.rsqrt(jnp.mean(yv * yv, axis=-1, keepdims=True) + RMS_EPS)
        o_ref[...] = x_ref[...] + yv * r * g_ref[...]

    row = pl.BlockSpec((t, d), lambda i: (i, 0))
    return pl.pallas_call(
        body, name=name, grid=(s // t,),
        in_specs=[row, row, pl.BlockSpec((1, d), lambda i: (0, 0))],
        out_specs=row,
        out_shape=jax.ShapeDtypeStruct((s, d), F32),
        compiler_params=_params(("parallel",)),
    )(x, y, g)


def _rms_bwd(xin, g, dys, dres, out_dtype, name, dep=None):
    s, d = xin.shape
    t = _tile(s, 512)
    n = s // t
    n_dy = len(dys)
    has_res = dres is not None
    deps = [] if dep is None else [dep]

    def body(*refs):
        x_ref, g_ref = refs[0], refs[1]
        dy_refs = refs[2:2 + n_dy]
        pos = 2 + n_dy
        res_ref = refs[pos] if has_res else None
        pos += (1 if has_res else 0) + len(deps)
        dx_ref, dg_ref, acc_ref = refs[pos], refs[pos + 1], refs[pos + 2]
        i = pl.program_id(0)
        xv = x_ref[...]
        dy = dy_refs[0][...].astype(F32)
        for extra in dy_refs[1:]:
            dy = dy + extra[...].astype(F32)
        r = lax.rsqrt(jnp.mean(xv * xv, axis=-1, keepdims=True) + RMS_EPS)
        u = dy * g_ref[...]
        xr = xv * r
        dx = r * (u - xr * jnp.mean(u * xr, axis=-1, keepdims=True))
        if has_res:
            dx = dx + res_ref[...]
        dx_ref[...] = dx.astype(dx_ref.dtype)
        part = _row_sum8(dy * xr)

        @pl.when(i == 0)
        def _():
            acc_ref[...] = part

        @pl.when(i > 0)
        def _():
            acc_ref[...] += part

        @pl.when(i == n - 1)
        def _():
            dg_ref[...] = jnp.sum(acc_ref[...], axis=0, keepdims=True)

    row = pl.BlockSpec((t, d), lambda i: (i, 0))
    vec = pl.BlockSpec((1, d), lambda i: (0, 0))
    ins = [xin, g, *dys] + ([dres] if has_res else []) + deps
    return pl.pallas_call(
        body, name=name, grid=(n,),
        in_specs=[row, vec] + [row] * (n_dy + (1 if has_res else 0)) + [_DEP] * len(deps),
        out_specs=[row, vec],
        out_shape=[jax.ShapeDtypeStruct((s, d), out_dtype), jax.ShapeDtypeStruct((1, d), F32)],
        scratch_shapes=[pltpu.VMEM((8, d), F32)],
        compiler_params=_params(("arbitrary",)),
    )(*ins)


def _loss_head(y, target, name):
    s, d = y.shape
    t = _tile(s, 512)
    n = s // t

    def body(y_ref, t_ref, dy_ref, loss_ref, acc_ref):
        i = pl.program_id(0)
        e = y_ref[...] - t_ref[...]
        dy_ref[...] = e * (1.0 / d)
        part = _row_sum8(e * e)

        @pl.when(i == 0)
        def _():
            acc_ref[...] = part

        @pl.when(i > 0)
        def _():
            acc_ref[...] += part

        @pl.when(i == n - 1)
        def _():
            tot = jnp.sum(jnp.sum(acc_ref[...], axis=0, keepdims=True), axis=1, keepdims=True)
            loss_ref[...] = tot * (0.5 / d)

    row = pl.BlockSpec((t, d), lambda i: (i, 0))
    return pl.pallas_call(
        body, name=name, grid=(n,),
        in_specs=[row, row],
        out_specs=[row, pl.BlockSpec((1, 1), lambda i: (0, 0))],
        out_shape=[jax.ShapeDtypeStruct((s, d), F32), jax.ShapeDtypeStruct((1, 1), F32)],
        scratch_shapes=[pltpu.VMEM((8, d), F32)],
        compiler_params=_params(("arbitrary",)),
    )(y, target)


def _split3(x):
    hi = x.astype(BF16)
    r1 = x - hi.astype(F32)
    mid = r1.astype(BF16)
    lo = (r1 - mid.astype(F32)).astype(BF16)
    return hi, mid, lo


def _tri_dot(x, tri):
    hi, mid, lo = _split3(x)
    dn = _DIMS["nn"]
    out = lax.dot_general(hi, tri, dn, preferred_element_type=F32)
    out = out + lax.dot_general(mid, tri, dn, preferred_element_type=F32)
    return out + lax.dot_general(lo, tri, dn, preferred_element_type=F32)


def _log_sigmoid(z):
    return jnp.minimum(z, 0.0) - jnp.log(1.0 + jnp.exp(-jnp.abs(z)))


def _gate_fwd(f_row, b_col, name):
    rows, s = f_row.shape
    t = _tile(s, 512)
    n = s // t

    def body(f_ref, b_ref, ck_ref, carry_ref):
        i = pl.program_id(0)

        @pl.when(i == 0)
        def _():
            carry_ref[...] = jnp.zeros_like(carry_ref)

        logf = _log_sigmoid(f_ref[...] + b_ref[...])
        r = lax.broadcasted_iota(jnp.int32, (t, t), 0)
        c = lax.broadcasted_iota(jnp.int32, (t, t), 1)
        tri = jnp.where(r <= c, 1.0, 0.0).astype(BF16)
        cs = _tri_dot(logf, tri) + carry_ref[...]
        carry_ref[...] = cs[:, t - 1:t]
        terms = [part.astype(F32) for part in _split3(-cs)]
        sub = lax.broadcasted_iota(jnp.int32, (LANES, t), 0)
        for p in range(N_HEADS // 2):
            stacked = jnp.zeros((LANES, t), F32)
            for hh in range(2):
                for j, term in enumerate(terms):
                    h = 2 * p + hh
                    stacked = jnp.where(sub == 3 * hh + j, jnp.broadcast_to(term[h:h + 1, :], (LANES, t)), stacked)
            ck_ref[p] = stacked.T.astype(ck_ref.dtype)

    return pl.pallas_call(
        body, name=name, grid=(n,),
        in_specs=[pl.BlockSpec((rows, t), lambda i: (0, i)), pl.BlockSpec((rows, 1), lambda i: (0, 0))],
        out_specs=pl.BlockSpec((N_HEADS // 2, t, LANES), lambda i: (0, i, 0)),
        out_shape=jax.ShapeDtypeStruct((N_HEADS // 2, s, LANES), BF16),
        scratch_shapes=[pltpu.VMEM((rows, 1), F32)],
        compiler_params=_params(("arbitrary",)),
    )(f_row, b_col)


def _gate_bwd(f_row, b_col, dc_even, dc_odd, name):
    rows, s = f_row.shape
    t = _tile(s, 512)
    n = s // t

    def body(f_ref, b_ref, dce_ref, dco_ref, df_ref, db_ref, carry_ref, acc_ref):
        i = pl.program_id(0)

        @pl.when(i == 0)
        def _():
            carry_ref[...] = jnp.zeros_like(carry_ref)
            acc_ref[...] = jnp.zeros_like(acc_ref)

        head = lax.broadcasted_iota(jnp.int32, (rows, t), 0)
        dcv = jnp.zeros((rows, t), F32)
        for h in range(N_HEADS):
            src = dce_ref if h % 2 == 0 else dco_ref
            dcv = jnp.where(head == h, jnp.broadcast_to(src[h // 2, 0:1, :], (rows, t)), dcv)
        r = lax.broadcasted_iota(jnp.int32, (t, t), 0)
        c = lax.broadcasted_iota(jnp.int32, (t, t), 1)
        tri = jnp.where(r >= c, 1.0, 0.0).astype(BF16)
        dlogf = _tri_dot(dcv, tri) + carry_ref[...]
        carry_ref[...] = dlogf[:, 0:1]
        z = f_ref[...] + b_ref[...]
        df = dlogf * _sigmoid(-z)
        df_ref[...] = df.astype(df_ref.dtype)
        acc_ref[...] += jnp.sum(df, axis=1, keepdims=True)

        @pl.when(i == n - 1)
        def _():
            db_ref[...] = acc_ref[...]

    rev = lambda i: (0, n - 1 - i)
    dc_spec = pl.BlockSpec((N_HEADS // 2, 8, t), lambda i: (0, 0, n - 1 - i))
    return pl.pallas_call(
        body, name=name, grid=(n,),
        in_specs=[pl.BlockSpec((rows, t), rev), pl.BlockSpec((rows, 1), lambda i: (0, 0)), dc_spec, dc_spec],
        out_specs=[pl.BlockSpec((rows, t), rev), pl.BlockSpec((rows, 1), lambda i: (0, 0))],
        out_shape=[jax.ShapeDtypeStruct((rows, s), BF16), jax.ShapeDtypeStruct((rows, 1), F32)],
        scratch_shapes=[pltpu.VMEM((rows, 1), F32), pltpu.VMEM((rows, 1), F32)],
        compiler_params=_params(("arbitrary",)),
    )(f_row, b_col, dc_even, dc_odd)


_NEG = -1e30
_SCALE = HEAD_DIM ** -0.5


def _head_masks():
    lane = lax.broadcasted_iota(jnp.int32, (1, LANES), 1)
    return [lane < HEAD_DIM, lane >= HEAD_DIM]


def _attn_fwd(h, ck, name):
    s = h.shape[0]
    t = _tile(s, 512)
    n = s // t
    qb, kb, vb = OFF_Q // LANES, OFF_K // LANES, OFF_V // LANES

    pairs = [(qi, ki) for qi in range(n) for ki in range(qi + 1)]
    qi_tab = jnp.asarray([qi for qi, _ in pairs], jnp.int32)
    ki_tab = jnp.asarray([ki for _, ki in pairs], jnp.int32)

    def body(qi_ref, ki_ref, q_ref, k_ref, v_ref, ck_ref, o_ref, of_ref, lse_ref, m_ref, l_ref, acc_ref):
        qi, ki = qi_ref[pl.program_id(1)], ki_ref[pl.program_id(1)]
        masks = _head_masks()
        lane = lax.broadcasted_iota(jnp.int32, (1, LANES), 1)

        @pl.when(ki == 0)
        def _():
            m_ref[...] = jnp.full_like(m_ref, _NEG)
            l_ref[...] = jnp.zeros_like(l_ref)
            acc_ref[...] = jnp.zeros_like(acc_ref)

        def step(diag):
            q = q_ref[...] * _SCALE
            k_aug = jnp.concatenate([k_ref[...], ck_ref[0]], axis=1)
            v = v_ref[...]
            for hh in range(2):
                ones = jnp.where((lane >= 3 * hh) & (lane < 3 * hh + 3), 1.0, 0.0).astype(q.dtype)
                q_aug = jnp.concatenate([jnp.where(masks[hh], q, jnp.zeros_like(q)),
                                         jnp.broadcast_to(ones, q.shape)], axis=1)
                sc = lax.dot_general(k_aug, q_aug, _DIMS["nt"], preferred_element_type=F32)
                if diag:
                    r = lax.broadcasted_iota(jnp.int32, (t, t), 0)
                    cc = lax.broadcasted_iota(jnp.int32, (t, t), 1)
                    sc = jnp.where(r <= cc, sc, _NEG)
                m_prev = m_ref[hh]
                m_new = jnp.maximum(m_prev, jnp.max(sc, axis=0, keepdims=True))
                alpha = jnp.exp(m_prev - m_new)
                p = jnp.exp(sc - m_new)
                l_ref[hh] = alpha * l_ref[hh] + jnp.sum(p, axis=0, keepdims=True)
                m_ref[hh] = m_new
                p_hi = p.astype(MXU_DTYPE)
                p_lo = (p - p_hi.astype(F32)).astype(MXU_DTYPE)
                pv = (lax.dot_general(v, p_hi, _DIMS["tn"], preferred_element_type=F32)
                      + lax.dot_general(v, p_lo, _DIMS["tn"], preferred_element_type=F32))
                rows = slice(hh * HEAD_DIM, (hh + 1) * HEAD_DIM)
                acc_ref[rows, :] = alpha * acc_ref[rows, :] + pv[rows]

        @pl.when(ki < qi)
        def _():
            step(False)

        @pl.when(ki == qi)
        def _():
            step(True)
            inv = jnp.concatenate([jnp.broadcast_to(1.0 / l_ref[hh], (HEAD_DIM, t)) for hh in range(2)], axis=0)
            out = (acc_ref[...] * inv).T
            o_ref[...] = out.astype(o_ref.dtype)
            of_ref[...] = out
            lse = jnp.concatenate([jnp.broadcast_to(m_ref[hh] + jnp.log(l_ref[hh]), (HEAD_DIM, t))
                                   for hh in range(2)], axis=0)
            lse_ref[...] = lse.T

    grid_spec = pltpu.PrefetchScalarGridSpec(
        num_scalar_prefetch=2, grid=(N_HEADS // 2, len(pairs)),
        in_specs=[
            pl.BlockSpec((t, LANES), lambda p, i, qt, kt: (qt[i], qb + p)),
            pl.BlockSpec((t, LANES), lambda p, i, qt, kt: (kt[i], kb + p)),
            pl.BlockSpec((t, LANES), lambda p, i, qt, kt: (kt[i], vb + p)),
            pl.BlockSpec((1, t, LANES), lambda p, i, qt, kt: (p, kt[i], 0)),
        ],
        out_specs=[pl.BlockSpec((t, LANES), lambda p, i, qt, kt: (qt[i], p))] * 3,
        scratch_shapes=[pltpu.VMEM((2, 1, t), F32), pltpu.VMEM((2, 1, t), F32), pltpu.VMEM((LANES, t), F32)])
    return pl.pallas_call(
        body, name=name, grid_spec=grid_spec,
        out_shape=[jax.ShapeDtypeStruct((s, D_ATT), BF16), jax.ShapeDtypeStruct((s, D_ATT), F32),
                   jax.ShapeDtypeStruct((s, D_ATT), F32)],
        compiler_params=_params(("parallel", "arbitrary")),
    )(qi_tab, ki_tab, h, h, h, ck)


def _attn_bwd(h, ck, o, lse, do, name):
    s = h.shape[0]
    t = _tile(s, 512)
    n = s // t
    qb, kb, vb = OFF_Q // LANES, OFF_K // LANES, OFF_V // LANES

    pairs = [(ki, qi) for ki in range(n) for qi in range(ki, n)]
    ki_tab = jnp.asarray([ki for ki, _ in pairs], jnp.int32)
    qi_tab = jnp.asarray([qi for _, qi in pairs], jnp.int32)

    def body(ki_ref, qi_ref, q_ref, k_ref, v_ref, ck_ref, o_ref, lse_ref, do_ref,
             dq_ref, dk_ref, dv_ref, dc0_ref, dc1_ref, dk_acc, dv_acc, dc_acc):
        ki, qi = ki_ref[pl.program_id(1)], qi_ref[pl.program_id(1)]
        masks = _head_masks()
        lane = lax.broadcasted_iota(jnp.int32, (1, LANES), 1)

        @pl.when((ki == 0) & (qi == 0))
        def _():
            dq_ref[...] = jnp.zeros_like(dq_ref)

        @pl.when(qi == ki)
        def _():
            dk_acc[...] = jnp.zeros_like(dk_acc)
            dv_acc[...] = jnp.zeros_like(dv_acc)
            dc_acc[...] = jnp.zeros_like(dc_acc)

        def step(diag):
            q = q_ref[...] * _SCALE
            k = k_ref[...]
            v = v_ref[...]
            dov = do_ref[...]
            k_aug = jnp.concatenate([k, ck_ref[0]], axis=1)
            prod_t = (dov.astype(F32) * o_ref[...]).T
            lse_t = lse_ref[...].T
            dq_blk = jnp.zeros((t, LANES), F32)
            for hh in range(2):
                mk = masks[hh]
                rows = slice(hh * HEAD_DIM, (hh + 1) * HEAD_DIM)
                qh = jnp.where(mk, q, jnp.zeros_like(q))
                kh = jnp.where(mk, k, jnp.zeros_like(k))
                doh = jnp.where(mk, dov, jnp.zeros_like(dov))
                ones = jnp.where((lane >= 3 * hh) & (lane < 3 * hh + 3), 1.0, 0.0).astype(q.dtype)
                q_aug = jnp.concatenate([qh, jnp.broadcast_to(ones, q.shape)], axis=1)
                sc = lax.dot_general(k_aug, q_aug, _DIMS["nt"], preferred_element_type=F32)
                p = jnp.exp(sc - lse_t[hh * HEAD_DIM:hh * HEAD_DIM + 1, :])
                if diag:
                    r = lax.broadcasted_iota(jnp.int32, (t, t), 0)
                    cc = lax.broadcasted_iota(jnp.int32, (t, t), 1)
                    p = jnp.where(r <= cc, p, 0.0)
                dp = lax.dot_general(v, doh, _DIMS["nt"], preferred_element_type=F32)
                delta = jnp.sum(prod_t[rows], axis=0, keepdims=True)
                ds = p * (dp - delta)
                dsb = ds.astype(MXU_DTYPE)
                pb = p.astype(MXU_DTYPE)
                dv_acc[...] += lax.dot_general(pb, doh, _DIMS["nn"], preferred_element_type=F32)
                dk_acc[...] += lax.dot_general(dsb, qh, _DIMS["nn"], preferred_element_type=F32)
                dq_blk = dq_blk + lax.dot_general(dsb, kh, _DIMS["tn"], preferred_element_type=F32)
                dc_acc[hh] = dc_acc[hh] - jnp.sum(ds, axis=1, keepdims=True)
            rows_q = pl.ds(pl.multiple_of(qi * t, t), t)
            dq_ref[rows_q, :] = dq_ref[rows_q, :] + dq_blk * _SCALE

        @pl.when(qi > ki)
        def _():
            step(False)

        @pl.when(qi == ki)
        def _():
            step(True)

        @pl.when(qi == n - 1)
        def _():
            dk_ref[...] = dk_acc[...].astype(dk_ref.dtype)
            dv_ref[...] = dv_acc[...].astype(dv_ref.dtype)
            dc0_ref[0] = jnp.broadcast_to(dc_acc[0], (t, LANES)).T[0:8]
            dc1_ref[0] = jnp.broadcast_to(dc_acc[1], (t, LANES)).T[0:8]

    q_blk = lambda col: pl.BlockSpec((t, LANES), lambda p, i, kt, qt: (qt[i], col(p)))
    k_blk = lambda col: pl.BlockSpec((t, LANES), lambda p, i, kt, qt: (kt[i], col(p)))
    dc_blk = pl.BlockSpec((1, 8, t), lambda p, i, kt, qt: (p, 0, kt[i]))
    grid_spec = pltpu.PrefetchScalarGridSpec(
        num_scalar_prefetch=2, grid=(N_HEADS // 2, len(pairs)),
        in_specs=[q_blk(lambda p: qb + p), k_blk(lambda p: kb + p), k_blk(lambda p: vb + p),
                  pl.BlockSpec((1, t, LANES), lambda p, i, kt, qt: (p, kt[i], 0)),
                  q_blk(lambda p: p), q_blk(lambda p: p), q_blk(lambda p: p)],
        out_specs=[pl.BlockSpec((s, LANES), lambda p, i, kt, qt: (0, p)), k_blk(lambda p: p), k_blk(lambda p: p),
                   dc_blk, dc_blk],
        scratch_shapes=[pltpu.VMEM((t, LANES), F32), pltpu.VMEM((t, LANES), F32), pltpu.VMEM((2, t, 1), F32)])
    return pl.pallas_call(
        body, name=name, grid_spec=grid_spec,
        out_shape=[jax.ShapeDtypeStruct((s, D_ATT), F32), jax.ShapeDtypeStruct((s, D_ATT), BF16),
                   jax.ShapeDtypeStruct((s, D_ATT), BF16), jax.ShapeDtypeStruct((N_HEADS // 2, 8, s), F32),
                   jax.ShapeDtypeStruct((N_HEADS // 2, 8, s), F32)],
        compiler_params=_params(("parallel", "arbitrary")),
    )(ki_tab, qi_tab, h, h, h, ck, o, lse, do)


def _conv3(z, z_prev, w_ref):
    return (w_ref[2:3, :] * z + w_ref[1:2, :] * _shift_down(z, z_prev, 1)
            + w_ref[0:1, :] * _shift_down(z, z_prev, 2))


def _sconv_fwd(h, w, name):
    s = h.shape[0]
    t = _tile(s, 512)
    r = t // HALO
    c = D_CONV
    b_bg, b_cg, b_hc = OFF_BG // c, OFF_CG // c, OFF_HC // c

    def body(bg_ref, cg_ref, hc_ref, cgp_ref, hcp_ref, w_ref, y_ref):
        i = pl.program_id(0)
        live = (i > 0).astype(F32)
        z = cg_ref[...].astype(F32) * hc_ref[...].astype(F32)
        zp = cgp_ref[...].astype(F32) * hcp_ref[...].astype(F32) * live
        y_ref[...] = (bg_ref[...].astype(F32) * _conv3(z, zp, w_ref)).astype(y_ref.dtype)

    cur = lambda b: pl.BlockSpec((t, c), lambda i: (i, b))
    prev = lambda b: pl.BlockSpec((HALO, c), lambda i: (jnp.maximum(i * r - 1, 0), b))
    return pl.pallas_call(
        body, name=name, grid=(s // t,),
        in_specs=[cur(b_bg), cur(b_cg), cur(b_hc), prev(b_cg), prev(b_hc), pl.BlockSpec((8, c), lambda i: (0, 0))],
        out_specs=pl.BlockSpec((t, c), lambda i: (i, 0)),
        out_shape=jax.ShapeDtypeStruct((s, c), BF16),
        compiler_params=_params(("parallel",)),
    )(h, h, h, h, h, w)


def _sconv_bwd(h, w, dy, name):
    s = h.shape[0]
    t = _tile(s, 512)
    n = s // t
    r = t // HALO
    nh = s // HALO
    c = D_CONV
    b_bg, b_cg, b_hc = OFF_BG // c, OFF_CG // c, OFF_HC // c

    def body(bg_ref, cg_ref, hc_ref, cgp_ref, hcp_ref, bgn_ref, dy_ref, dyn_ref, w_ref, d_ref, dw_ref, acc_ref):
        i = pl.program_id(0)
        has_prev = (i > 0).astype(F32)
        has_next = (i < n - 1).astype(F32)
        bg = bg_ref[...].astype(F32)
        cg = cg_ref[...].astype(F32)
        hc = hc_ref[...].astype(F32)
        dyv = dy_ref[...].astype(F32)
        z = cg * hc
        zp = cgp_ref[...].astype(F32) * hcp_ref[...].astype(F32) * has_prev
        z1 = _shift_down(z, zp, 1)
        z2 = _shift_down(z, zp, 2)
        cz = w_ref[2:3, :] * z + w_ref[1:2, :] * z1 + w_ref[0:1, :] * z2
        dcz = dyv * bg
        dczn = dyn_ref[...].astype(F32) * bgn_ref[...].astype(F32) * has_next
        dz = (w_ref[2:3, :] * dcz + w_ref[1:2, :] * _shift_up(dcz, dczn, 1)
              + w_ref[0:1, :] * _shift_up(dcz, dczn, 2))
        d_ref[:, 0:c] = (dyv * cz).astype(d_ref.dtype)
        d_ref[:, c:2 * c] = (dz * hc).astype(d_ref.dtype)
        d_ref[:, 2 * c:3 * c] = (dz * cg).astype(d_ref.dtype)

        @pl.when(i == 0)
        def _():
            acc_ref[...] = jnp.zeros_like(acc_ref)

        acc_ref[0] += _row_sum8(dcz * z2)
        acc_ref[1] += _row_sum8(dcz * z1)
        acc_ref[2] += _row_sum8(dcz * z)

        @pl.when(i == n - 1)
        def _():
            rows = [jnp.sum(acc_ref[k], axis=0, keepdims=True) for k in range(3)]
            dw_ref[...] = jnp.concatenate(rows + [jnp.zeros((5, c), F32)], axis=0)

    cur = lambda b: pl.BlockSpec((t, c), lambda i: (i, b))
    prev = lambda b: pl.BlockSpec((HALO, c), lambda i: (jnp.maximum(i * r - 1, 0), b))
    nxt = lambda b: pl.BlockSpec((HALO, c), lambda i: (jnp.minimum((i + 1) * r, nh - 1), b))
    return pl.pallas_call(
        body, name=name, grid=(n,),
        in_specs=[cur(b_bg), cur(b_cg), cur(b_hc), prev(b_cg), prev(b_hc), nxt(b_bg),
                  cur(0), nxt(0), pl.BlockSpec((8, c), lambda i: (0, 0))],
        out_specs=[pl.BlockSpec((t, 3 * c), lambda i: (i, 0)), pl.BlockSpec((8, c), lambda i: (0, 0))],
        out_shape=[jax.ShapeDtypeStruct((s, 3 * c), BF16), jax.ShapeDtypeStruct((8, c), F32)],
        scratch_shapes=[pltpu.VMEM((3, 8, c), F32)],
        compiler_params=_params(("arbitrary",)),
    )(h, h, h, h, h, h, dy, dy, w)


def _group_masks():
    lane = lax.broadcasted_iota(jnp.int32, (1, D_SGU), 1)
    return [(lane >= g * HEAD_DIM) & (lane < (g + 1) * HEAD_DIM) for g in range(N_GROUPS)]


def _tril_weights(w_ref):
    r = lax.broadcasted_iota(jnp.int32, (CHUNK, CHUNK), 0)
    c = lax.broadcasted_iota(jnp.int32, (CHUNK, CHUNK), 1)
    return [jnp.where(r >= c, w_ref[g], 0.0).astype(MXU_DTYPE) for g in range(N_GROUPS)]


def _sgu_ln(vs, g_ref, b_ref):
    vg, dvg = _gelu_and_grad(vs)
    mu = jnp.mean(vg, axis=-1, keepdims=True)
    xc = vg - mu
    rstd = lax.rsqrt(jnp.mean(xc * xc, axis=-1, keepdims=True) + LN_EPS)
    xhat = xc * rstd
    return xhat * g_ref[...] + b_ref[...], xhat, rstd, dvg


def _sgu_fwd(h, ln_g, ln_b, w_s, bias, name):
    s = h.shape[0]
    t = _tile(s, 512)
    c = D_SGU
    b_u, b_v = OFF_U // c, OFF_VS // c

    def body(u_ref, v_ref, g_ref, b_ref, w_ref, bias_ref, y_ref):
        gm = _group_masks()
        wm = _tril_weights(w_ref)
        ug = _gelu(u_ref[...].astype(F32))
        vn, _, _, _ = _sgu_ln(v_ref[...].astype(F32), g_ref, b_ref)
        vnb = vn.astype(MXU_DTYPE)
        for ch in range(t // CHUNK):
            rows = slice(ch * CHUNK, (ch + 1) * CHUNK)
            mixed = bias_ref[...]
            for g in range(N_GROUPS):
                mg = lax.dot_general(wm[g], vnb[rows], _DIMS["nn"], preferred_element_type=F32)
                mixed = jnp.where(gm[g], mixed + mg, mixed)
            y_ref[rows, :] = (ug[rows] * mixed).astype(y_ref.dtype)

    full = lambda shp: pl.BlockSpec(shp, lambda i: (0,) * len(shp))
    return pl.pallas_call(
        body, name=name, grid=(s // t,),
        in_specs=[pl.BlockSpec((t, c), lambda i: (i, b_u)), pl.BlockSpec((t, c), lambda i: (i, b_v)),
                  full((1, c)), full((1, c)), full((N_GROUPS, CHUNK, CHUNK)), full((CHUNK, c))],
        out_specs=pl.BlockSpec((t, c), lambda i: (i, 0)),
        out_shape=jax.ShapeDtypeStruct((s, c), BF16),
        compiler_params=_params(("parallel",)),
    )(h, h, ln_g, ln_b, w_s, bias)


def _sgu_bwd(h, ln_g, ln_b, w_s, bias, dy, name):
    s = h.shape[0]
    t = _tile(s, 512)
    n = s // t
    c = D_SGU
    b_u, b_v = OFF_U // c, OFF_VS // c

    def body(u_ref, v_ref, g_ref, b_ref, w_ref, bias_ref, dy_ref,
             d_ref, dg_ref, db_ref, dw_ref, dbias_ref, dg_acc, db_acc):
        i = pl.program_id(0)
        gm = _group_masks()
        wm = _tril_weights(w_ref)

        @pl.when(i == 0)
        def _():
            dg_acc[...] = jnp.zeros_like(dg_acc)
            db_acc[...] = jnp.zeros_like(db_acc)
            dw_ref[...] = jnp.zeros_like(dw_ref)
            dbias_ref[...] = jnp.zeros_like(dbias_ref)

        ug, dug = _gelu_and_grad(u_ref[...].astype(F32))
        vn, xhat, rstd, dvg = _sgu_ln(v_ref[...].astype(F32), g_ref, b_ref)
        vnb = vn.astype(MXU_DTYPE)
        dyv = dy_ref[...].astype(F32)
        dmixed = dyv * ug
        dmb = dmixed.astype(MXU_DTYPE)
        dvn_parts = []
        for ch in range(t // CHUNK):
            rows = slice(ch * CHUNK, (ch + 1) * CHUNK)
            mixed = bias_ref[...]
            dvn = jnp.zeros((CHUNK, c), F32)
            for g in range(N_GROUPS):
                mg = lax.dot_general(wm[g], vnb[rows], _DIMS["nn"], preferred_element_type=F32)
                mixed = jnp.where(gm[g], mixed + mg, mixed)
                dvn = jnp.where(gm[g], lax.dot_general(wm[g], dmb[rows], _DIMS["tn"], preferred_element_type=F32),
                                dvn)
                dmg = jnp.where(gm[g], dmb[rows], jnp.zeros_like(dmb[rows]))
                dw_ref[g] += lax.dot_general(dmg, vnb[rows], _DIMS["nt"], preferred_element_type=F32)
            d_ref[rows, 0:c] = (dyv[rows] * mixed * dug[rows]).astype(d_ref.dtype)
            dbias_ref[...] += dmixed[rows]
            dvn_parts.append(dvn)
        dvn = jnp.concatenate(dvn_parts, axis=0)
        dg_acc[...] += _row_sum8(dvn * xhat)
        db_acc[...] += _row_sum8(dvn)
        dxh = dvn * g_ref[...]
        dvgl = rstd * (dxh - jnp.mean(dxh, axis=-1, keepdims=True)
                       - xhat * jnp.mean(dxh * xhat, axis=-1, keepdims=True))
        d_ref[:, c:2 * c] = (dvgl * dvg).astype(d_ref.dtype)

        @pl.when(i == n - 1)
        def _():
            dg_ref[...] = jnp.sum(dg_acc[...], axis=0, keepdims=True)
            db_ref[...] = jnp.sum(db_acc[...], axis=0, keepdims=True)
            r = lax.broadcasted_iota(jnp.int32, (CHUNK, CHUNK), 0)
            cc = lax.broadcasted_iota(jnp.int32, (CHUNK, CHUNK), 1)
            for g in range(N_GROUPS):
                dw_ref[g] = jnp.where(r >= cc, dw_ref[g], 0.0)

    full = lambda shp: pl.BlockSpec(shp, lambda i: (0,) * len(shp))
    return pl.pallas_call(
        body, name=name, grid=(n,),
        in_specs=[pl.BlockSpec((t, c), lambda i: (i, b_u)), pl.BlockSpec((t, c), lambda i: (i, b_v)),
                  full((1, c)), full((1, c)), full((N_GROUPS, CHUNK, CHUNK)), full((CHUNK, c)),
                  pl.BlockSpec((t, c), lambda i: (i, 0))],
        out_specs=[pl.BlockSpec((t, 2 * c), lambda i: (i, 0)), full((1, c)), full((1, c)),
                   full((N_GROUPS, CHUNK, CHUNK)), full((CHUNK, c))],
        out_shape=[jax.ShapeDtypeStruct((s, 2 * c), BF16), jax.ShapeDtypeStruct((1, c), F32),
                   jax.ShapeDtypeStruct((1, c), F32), jax.ShapeDtypeStruct((N_GROUPS, CHUNK, CHUNK), F32),
                   jax.ShapeDtypeStruct((CHUNK, c), F32)],
        scratch_shapes=[pltpu.VMEM((8, c), F32), pltpu.VMEM((8, c), F32)],
        compiler_params=_params(("arbitrary",)),
    )(h, h, ln_g, ln_b, w_s, bias, dy)


def _merge_fwd(h, acts, ws, b_gate, name):
    s = h.shape[0]
    d = D_MODEL
    t = _tile(s, 512)

    def body(gl0, gl1, gl2, a0, a1, a2, w0, w1, w2, b_ref, o_ref):
        acc = jnp.zeros((t, d), F32)
        for i, (gl, a, w) in enumerate(((gl0, a0, w0), (gl1, a1, w1), (gl2, a2, w2))):
            y = lax.dot_general(a[...], w[...], _DIMS["nn"], preferred_element_type=F32)
            acc = acc + _sigmoid(gl[...].astype(F32) + b_ref[i:i + 1, :]) * y
        o_ref[...] = acc.astype(o_ref.dtype)

    full = lambda arr: pl.BlockSpec(arr.shape, lambda i: (0, 0))
    return pl.pallas_call(
        body, name=name, grid=(s // t,),
        in_specs=[pl.BlockSpec((t, d), lambda i, b=b: (i, b)) for b in range(3)]
                 + [pl.BlockSpec((t, a.shape[1]), lambda i: (i, 0)) for a in acts]
                 + [full(w) for w in ws] + [full(b_gate)],
        out_specs=pl.BlockSpec((t, d), lambda i: (i, 0)),
        out_shape=jax.ShapeDtypeStruct((s, d), BF16),
        compiler_params=_params(("parallel",)),
    )(h, h, h, *acts, *ws, b_gate)


def _merge_bwd(h, acts, ws, b_gate, dmerged, name):
    s = h.shape[0]
    d = D_MODEL
    t = _tile(s, 512)
    n = s // t

    def body(gl0, gl1, gl2, a0, a1, a2, w0, w1, w2, b_ref, dm_ref, dy0, dy1, dy2, dgl_ref, db_ref, acc_ref):
        step = pl.program_id(0)

        @pl.when(step == 0)
        def _():
            acc_ref[...] = jnp.zeros_like(acc_ref)

        dm = dm_ref[...]
        for i, (gl, a, w, dy) in enumerate(((gl0, a0, w0, dy0), (gl1, a1, w1, dy1), (gl2, a2, w2, dy2))):
            y = lax.dot_general(a[...], w[...], _DIMS["nn"], preferred_element_type=F32)
            gate = _sigmoid(gl[...].astype(F32) + b_ref[i:i + 1, :])
            dy[...] = (dm * gate).astype(dy.dtype)
            dgl = dm * y * (gate * (1.0 - gate))
            dgl_ref[:, i * d:(i + 1) * d] = dgl.astype(dgl_ref.dtype)
            acc_ref[i] += _row_sum8(dgl)

        @pl.when(step == n - 1)
        def _():
            rows = [jnp.sum(acc_ref[k], axis=0, keepdims=True) for k in range(3)]
            db_ref[...] = jnp.concatenate(rows + [jnp.zeros((5, d), F32)], axis=0)

    full = lambda arr: pl.BlockSpec(arr.shape, lambda i: (0, 0))
    row = pl.BlockSpec((t, d), lambda i: (i, 0))
    return pl.pallas_call(
        body, name=name, grid=(n,),
        in_specs=[pl.BlockSpec((t, d), lambda i, b=b: (i, b)) for b in range(3)]
                 + [pl.BlockSpec((t, a.shape[1]), lambda i: (i, 0)) for a in acts]
                 + [full(w) for w in ws] + [full(b_gate), row],
        out_specs=[row, row, row, pl.BlockSpec((t, 3 * d), lambda i: (i, 0)), pl.BlockSpec((8, d), lambda i: (0, 0))],
        out_shape=[jax.ShapeDtypeStruct((s, d), BF16)] * 3
                  + [jax.ShapeDtypeStruct((s, 3 * d), BF16), jax.ShapeDtypeStruct((8, d), F32)],
        scratch_shapes=[pltpu.VMEM((3, 8, d), F32)],
        compiler_params=_params(("arbitrary",)),
    )(h, h, h, *acts, *ws, b_gate, dmerged)


FF_BLK = D_FF // 2


def _ffn_act_fwd(h2, w, name):
    s = h2.shape[0]
    t = _tile(s, 512)
    r = t // HALO
    cw = 2 * FF_BLK

    def body(x_ref, xp_ref, w_ref, p_ref):
        i = pl.program_id(0)
        live = (i > 0).astype(F32)
        hc = _conv3(x_ref[...].astype(F32), xp_ref[...].astype(F32) * live, w_ref)
        p_ref[...] = (_gelu(hc[:, :FF_BLK]) * hc[:, FF_BLK:]).astype(p_ref.dtype)

    return pl.pallas_call(
        body, name=name, grid=(s // t, 2),
        in_specs=[pl.BlockSpec((t, cw), lambda i, j: (i, j)),
                  pl.BlockSpec((HALO, cw), lambda i, j: (jnp.maximum(i * r - 1, 0), j)),
                  pl.BlockSpec((8, cw), lambda i, j: (0, j))],
        out_specs=pl.BlockSpec((t, FF_BLK), lambda i, j: (i, j)),
        out_shape=jax.ShapeDtypeStruct((s, D_FF), BF16),
        compiler_params=_params(("parallel", "parallel")),
    )(h2, h2, w)


def _ffn_act_conv_bwd(h2, w, dp, name):
    s = h2.shape[0]
    t = _tile(s, 512)
    n = s // t
    r = t // HALO
    nh = s // HALO
    cw = 2 * FF_BLK

    def body(x_ref, xp_ref, xn_ref, dp_ref, dpn_ref, w_ref, dx_ref, dw_ref, acc_ref):
        i = pl.program_id(1)
        has_prev = (i > 0).astype(F32)
        has_next = (i < n - 1).astype(F32)
        x = jnp.concatenate([x_ref[...].astype(F32), xn_ref[...].astype(F32)], axis=0)
        xp = xp_ref[...].astype(F32) * has_prev
        x1 = _shift_down(x, xp, 1)
        x2 = _shift_down(x, xp, 2)
        hc = w_ref[2:3, :] * x + w_ref[1:2, :] * x1 + w_ref[0:1, :] * x2
        ga, dga = _gelu_and_grad(hc[:, :FF_BLK])
        dpv = jnp.concatenate([dp_ref[...].astype(F32), dpn_ref[...].astype(F32) * has_next], axis=0)
        dhc = jnp.concatenate([dpv * hc[:, FF_BLK:] * dga, dpv * ga], axis=1)
        cur, nxt = dhc[:t], dhc[t:]
        dx = w_ref[2:3, :] * cur + w_ref[1:2, :] * _shift_up(cur, nxt, 1) + w_ref[0:1, :] * _shift_up(cur, nxt, 2)
        dx_ref[...] = dx.astype(dx_ref.dtype)

        @pl.when(i == 0)
        def _():
            acc_ref[...] = jnp.zeros_like(acc_ref)

        acc_ref[0] += _row_sum8(cur * x2[:t])
        acc_ref[1] += _row_sum8(cur * x1[:t])
        acc_ref[2] += _row_sum8(cur * x[:t])

        @pl.when(i == n - 1)
        def _():
            rows = [jnp.sum(acc_ref[k], axis=0, keepdims=True) for k in range(3)]
            dw_ref[...] = jnp.concatenate(rows + [jnp.zeros((5, cw), F32)], axis=0)

    nxt_row = lambda j, i: jnp.minimum((i + 1) * r, nh - 1)
    return pl.pallas_call(
        body, name=name, grid=(2, n),
        in_specs=[pl.BlockSpec((t, cw), lambda j, i: (i, j)),
                  pl.BlockSpec((HALO, cw), lambda j, i: (jnp.maximum(i * r - 1, 0), j)),
                  pl.BlockSpec((HALO, cw), lambda j, i: (nxt_row(j, i), j)),
                  pl.BlockSpec((t, FF_BLK), lambda j, i: (i, j)),
                  pl.BlockSpec((HALO, FF_BLK), lambda j, i: (nxt_row(j, i), j)),
                  pl.BlockSpec((8, cw), lambda j, i: (0, j))],
        out_specs=[pl.BlockSpec((t, cw), lambda j, i: (i, j)), pl.BlockSpec((8, cw), lambda j, i: (0, j))],
        out_shape=[jax.ShapeDtypeStruct((s, 2 * D_FF), BF16), jax.ShapeDtypeStruct((8, 2 * D_FF), F32)],
        scratch_shapes=[pltpu.VMEM((3, 8, cw), F32)],
        compiler_params=_params(("parallel", "arbitrary")),
    )(h2, h2, h2, dp, dp, w)


def _adamw(w, g, m, v, name):
    shape = w.shape
    c = shape[-1]
    rows = math.prod(shape[:-1])
    to2d = lambda a: a.reshape(rows, c)
    cap = max(8, (1 << 18) // c)
    tr = rows
    for cand in (2048, 1024, 512, 256, 128, 64, 32, 16, 8):
        if cand <= cap and rows % cand == 0:
            tr = cand
            break

    def body(w_ref, g_ref, m_ref, v_ref, d_ref, nm_ref, nv_ref):
        gv = g_ref[...]
        nm = ADAM_B1 * m_ref[...] + (1.0 - ADAM_B1) * gv
        nv = ADAM_B2 * v_ref[...] + (1.0 - ADAM_B2) * (gv * gv)
        m_hat = nm / (1.0 - ADAM_B1 ** ADAM_STEP)
        v_hat = nv / (1.0 - ADAM_B2 ** ADAM_STEP)
        d_ref[...] = -ADAM_LR * (m_hat / (jnp.sqrt(v_hat) + ADAM_EPS) + ADAM_WD * w_ref[...])
        nm_ref[...] = nm
        nv_ref[...] = nv

    blk = pl.BlockSpec((tr, c), lambda i: (i, 0))
    outs = pl.pallas_call(
        body, name=name, grid=(rows // tr,),
        in_specs=[blk] * 4, out_specs=[blk] * 3,
        out_shape=[jax.ShapeDtypeStruct((rows, c), F32)] * 3,
        compiler_params=_params(("parallel",)),
    )(to2d(w), to2d(g), to2d(m), to2d(v))
    return tuple(o.reshape(shape) for o in outs)


_ANY = pl.BlockSpec(memory_space=pl.ANY)


def _place():
    x, y, c = lax.axis_index("x"), lax.axis_index("y"), lax.axis_index("c")
    others = [(1 - x, y), (x, 1 - y), (1 - x, 1 - y)]
    return x, y, c, others


def _all_gather_chips(shard, name):
    rws, cols = shard.shape
    half = rws // 2

    def body(x_ref, out_ref, send_sems, recv_sems, local_sem):
        x, y, c, others = _place()
        me = 2 * x + y
        sib = (x, y, 1 - c)

        def rows(chip, cc):
            return out_ref.at[chip, pl.ds(pl.multiple_of(cc * half, 16), half), :]

        def copy(k, src, dst, to):
            return pltpu.make_async_remote_copy(src_ref=src, dst_ref=dst, send_sem=send_sems.at[k],
                                                recv_sem=recv_sems.at[k], device_id=to, device_id_type=MESH)

        mine = pltpu.make_async_copy(x_ref, out_ref.at[me], local_sem)
        mine.start()
        my_half = x_ref.at[pl.ds(pl.multiple_of(c * half, 16), half), :]
        first = [copy(j, my_half, rows(me, c), (ox, oy, c)) for j, (ox, oy) in enumerate(others)]
        for cp in first:
            cp.start()
        passed = []
        for j, (ox, oy) in enumerate(others):
            blk = rows(2 * ox + oy, c)
            copy(j, blk, blk, (x, y, c)).wait_recv()
            fwd = copy(3 + j, blk, blk, sib)
            fwd.start()
            passed.append(fwd)
        for j, (ox, oy) in enumerate(others):
            blk = rows(2 * ox + oy, 1 - c)
            copy(3 + j, blk, blk, (x, y, c)).wait_recv()
        for cp in first + passed:
            cp.wait_send()
        mine.wait()

    return pl.pallas_call(
        body, name=name,
        in_specs=[_ANY], out_specs=_ANY,
        out_shape=jax.ShapeDtypeStruct((N_CHIPS, rws, cols), shard.dtype),
        scratch_shapes=[pltpu.SemaphoreType.DMA((6,)), pltpu.SemaphoreType.DMA((6,)), pltpu.SemaphoreType.DMA],
        compiler_params=pltpu.CompilerParams(has_side_effects=True),
    )(shard)


def _swap_halves(buf, name):
    nb, rws, cols = buf.shape
    half = rws // 2

    def body(b_ref, own_ref, sib_ref, send_sem, recv_sem, local_sem):
        x, y, c, _ = _place()
        keep = b_ref.at[:, pl.ds(pl.multiple_of(c * half, 16), half), :]
        give = b_ref.at[:, pl.ds(pl.multiple_of((1 - c) * half, 16), half), :]
        mine = pltpu.make_async_copy(keep, own_ref, local_sem)
        mine.start()
        cp = pltpu.make_async_remote_copy(src_ref=give, dst_ref=sib_ref, send_sem=send_sem, recv_sem=recv_sem,
                                          device_id=(x, y, 1 - c), device_id_type=MESH)
        cp.start()
        cp.wait()
        mine.wait()

    shp = jax.ShapeDtypeStruct((nb, half, cols), buf.dtype)
    return pl.pallas_call(
        body, name=name,
        in_specs=[_ANY], out_specs=[_ANY, _ANY], out_shape=[shp, shp],
        scratch_shapes=[pltpu.SemaphoreType.DMA, pltpu.SemaphoreType.DMA, pltpu.SemaphoreType.DMA],
        compiler_params=pltpu.CompilerParams(has_side_effects=True),
    )(buf)


def _add2(a, b, name):
    nb, rws, cols = a.shape
    t = _tile(rws, 256)
    if rws % t:
        t = rws

    def body(a_ref, b_ref, o_ref):
        o_ref[...] = (a_ref[...].astype(F32) + b_ref[...].astype(F32)).astype(o_ref.dtype)

    blk = pl.BlockSpec((1, t, cols), lambda i, j: (i, j, 0))
    return pl.pallas_call(
        body, name=name, grid=(nb, rws // t), in_specs=[blk, blk], out_specs=blk,
        out_shape=jax.ShapeDtypeStruct(a.shape, a.dtype),
        compiler_params=_params(("parallel", "parallel")),
    )(a, b)


def _exchange_chips(pre, name):
    nb, half, cols = pre.shape

    def body(p_ref, out_ref, send_sems, recv_sems, local_sem):
        x, y, c, others = _place()
        me = 2 * x + y
        mine = pltpu.make_async_copy(p_ref.at[me], out_ref.at[me], local_sem)
        mine.start()
        sends = []
        for j, (ox, oy) in enumerate(others):
            cp = pltpu.make_async_remote_copy(src_ref=p_ref.at[2 * ox + oy], dst_ref=out_ref.at[me],
                                              send_sem=send_sems.at[j], recv_sem=recv_sems.at[j],
                                              device_id=(ox, oy, c), device_id_type=MESH)
            cp.start()
            sends.append(cp)
        for j, (ox, oy) in enumerate(others):
            blk = out_ref.at[2 * ox + oy]
            pltpu.make_async_remote_copy(src_ref=blk, dst_ref=blk, send_sem=send_sems.at[j],
                                         recv_sem=recv_sems.at[j], device_id=(x, y, c),
                                         device_id_type=MESH).wait_recv()
        for cp in sends:
            cp.wait_send()
        mine.wait()

    return pl.pallas_call(
        body, name=name,
        in_specs=[_ANY], out_specs=_ANY, out_shape=jax.ShapeDtypeStruct(pre.shape, pre.dtype),
        scratch_shapes=[pltpu.SemaphoreType.DMA((3,)), pltpu.SemaphoreType.DMA((3,)), pltpu.SemaphoreType.DMA],
        compiler_params=pltpu.CompilerParams(has_side_effects=True),
    )(pre)


def _add4(parts, name):
    nb, half, cols = parts.shape
    t = _tile(half, 256)
    if half % t:
        t = half

    def body(p_ref, o_ref):
        acc = p_ref[0].astype(F32)
        for k in range(1, nb):
            acc = acc + p_ref[k].astype(F32)
        o_ref[...] = acc

    return pl.pallas_call(
        body, name=name, grid=(half // t,),
        in_specs=[pl.BlockSpec((nb, t, cols), lambda i: (0, i, 0))],
        out_specs=pl.BlockSpec((t, cols), lambda i: (i, 0)),
        out_shape=jax.ShapeDtypeStruct((half, cols), F32),
        compiler_params=_params(("parallel",)),
    )(parts)


def _join_halves(mine_half, name):
    half, cols = mine_half.shape

    def body(h_ref, out_ref, send_sem, recv_sem, local_sem):
        x, y, c, _ = _place()
        dst = out_ref.at[pl.ds(pl.multiple_of(c * half, 8), half), :]
        mine = pltpu.make_async_copy(h_ref, dst, local_sem)
        mine.start()
        cp = pltpu.make_async_remote_copy(src_ref=h_ref, dst_ref=dst, send_sem=send_sem, recv_sem=recv_sem,
                                          device_id=(x, y, 1 - c), device_id_type=MESH)
        cp.start()
        cp.wait()
        mine.wait()

    return pl.pallas_call(
        body, name=name,
        in_specs=[_ANY], out_specs=_ANY, out_shape=jax.ShapeDtypeStruct((2 * half, cols), mine_half.dtype),
        scratch_shapes=[pltpu.SemaphoreType.DMA, pltpu.SemaphoreType.DMA, pltpu.SemaphoreType.DMA],
        compiler_params=pltpu.CompilerParams(has_side_effects=True),
    )(mine_half)


def _reduce_scatter_chips(buf, tag):
    own, sib = _swap_halves(buf, "rs_swap_" + tag)
    pre = _add2(own, sib, "rs_add2_" + tag)
    parts = _exchange_chips(pre, "rs_xchg_" + tag)
    red = _add4(parts, "rs_add4_" + tag)
    return _join_halves(red, "rs_join_" + tag)


MAX_DMA_BYTES = 2 * 1024 * 1024
ROW_ALIGN = 16


def _pieces(rows, row_bytes):
    n = max(1, -(-(rows * row_bytes) // MAX_DMA_BYTES))
    step = -(-(-(-rows // n)) // ROW_ALIGN) * ROW_ALIGN
    return [(r, min(step, rows - r)) for r in range(0, rows, step)]


def _half_plan(arrays, row_axis):
    plan = []
    for a, arr in enumerate(arrays):
        row_bytes = math.prod(arr.shape[row_axis + 1:]) * arr.dtype.itemsize * (arr.shape[0] if row_axis else 1)
        plan += [(a, r0, nr) for r0, nr in _pieces(arr.shape[row_axis] // 2, row_bytes)]
    return plan


def _rows(start, size):
    return pl.ds(pl.multiple_of(start, ROW_ALIGN), size)


def _remote(src, dst, send_sems, recv_sems, k, to):
    return pltpu.make_async_remote_copy(src_ref=src, dst_ref=dst, send_sem=send_sems.at[k], recv_sem=recv_sems.at[k],
                                        device_id=to, device_id_type=MESH)


def _comm_call(body, name, ins, out_shapes, n_remote, n_local, aliases=None):
    return pl.pallas_call(
        body, name=name,
        in_specs=[_ANY] * len(ins), out_specs=[_ANY] * len(out_shapes), out_shape=out_shapes,
        scratch_shapes=[pltpu.SemaphoreType.DMA((n_remote,)), pltpu.SemaphoreType.DMA((n_remote,)),
                        pltpu.SemaphoreType.DMA((max(n_local, 1),))],
        input_output_aliases=aliases or {},
        compiler_params=pltpu.CompilerParams(has_side_effects=True),
    )(*ins)


def _cast_shard(w, l, me_idx, name):
    _, k, cols = w.shape
    tr = _tile(k, 256)
    if k % tr:
        tr = k

    def body(me_ref, w_ref, s_ref, land_ref):
        del me_ref
        v = w_ref[...].astype(BF16)
        s_ref[...] = v
        land_ref[...] = v

    grid_spec = pltpu.PrefetchScalarGridSpec(
        num_scalar_prefetch=1, grid=(k // tr,),
        in_specs=[pl.BlockSpec((None, tr, cols), lambda i, me: (l, i, 0))],
        out_specs=[pl.BlockSpec((tr, cols), lambda i, me: (i, 0)),
                   pl.BlockSpec((None, tr, cols), lambda i, me: (me[0], i, 0))])
    return pl.pallas_call(
        body, name=name, grid_spec=grid_spec,
        out_shape=[jax.ShapeDtypeStruct((k, cols), BF16), jax.ShapeDtypeStruct((N_CHIPS, k, cols), BF16)],
        compiler_params=_params(("parallel",)),
    )(me_idx, w)


def _gather_d2d(lands, name):
    n = len(lands)
    plan = _half_plan(lands, 1)
    plan = [(a, r0, nr) for a, r0, nr in plan]

    def body(*refs):
        out_refs = refs[n:2 * n]
        send_sems, recv_sems, _ = refs[2 * n:]
        x, y, c, others = _place()
        sends = []
        for i, (a, r0, nr) in enumerate(plan):
            rows = _rows(c * (lands[a].shape[1] // 2) + r0, nr)
            for j, (ox, oy) in enumerate(others):
                blk = out_refs[a].at[2 * ox + oy, rows, :]
                cp = _remote(blk, blk, send_sems, recv_sems, 3 * i + j, (x, y, 1 - c))
                cp.start()
                sends.append(cp)
        for i, (a, r0, nr) in enumerate(plan):
            rows = _rows((1 - c) * (lands[a].shape[1] // 2) + r0, nr)
            for j, (ox, oy) in enumerate(others):
                blk = out_refs[a].at[2 * ox + oy, rows, :]
                _remote(blk, blk, send_sems, recv_sems, 3 * i + j, (x, y, c)).wait_recv()
        for cp in sends:
            cp.wait_send()

    outs = [jax.ShapeDtypeStruct(a.shape, a.dtype) for a in lands]
    return _comm_call(body, name, lands, outs, 3 * len(plan), 0, aliases={a: a for a in range(n)})


def _rs_swap(ts, name):
    n = len(ts)
    plan = _half_plan(ts, 1)

    def body(*refs):
        t_refs, out_refs = refs[:n], refs[n:2 * n]
        send_sems, recv_sems, _ = refs[2 * n:]
        x, y, c, _o = _place()
        sends = []
        for i, (a, r0, nr) in enumerate(plan):
            src = t_refs[a].at[:, _rows((1 - c) * (ts[a].shape[1] // 2) + r0, nr), :]
            cp = _remote(src, out_refs[a].at[:, pl.ds(r0, nr), :], send_sems, recv_sems, i, (x, y, 1 - c))
            cp.start()
            sends.append(cp)
        for i, (a, r0, nr) in enumerate(plan):
            blk = out_refs[a].at[:, pl.ds(r0, nr), :]
            _remote(blk, blk, send_sems, recv_sems, i, (x, y, c)).wait_recv()
        for cp in sends:
            cp.wait_send()

    outs = [jax.ShapeDtypeStruct((t.shape[0], t.shape[1] // 2, t.shape[2]), t.dtype) for t in ts]
    return _comm_call(body, name, ts, outs, len(plan), 0)


def _add_half(t, got, c_idx, me_idx, name):
    nb, k, cols = t.shape
    half = k // 2

    def body(c_ref, me_ref, t_ref, g_ref, o_ref, mine_ref):
        del c_ref
        v = (t_ref[...].astype(F32) + g_ref[...].astype(F32)).astype(o_ref.dtype)
        o_ref[...] = v

        @pl.when(pl.program_id(0) == me_ref[0])
        def _():
            mine_ref[...] = v

    blk = pl.BlockSpec((1, half, cols), lambda i, c, me: (i, 0, 0))
    grid_spec = pltpu.PrefetchScalarGridSpec(
        num_scalar_prefetch=2, grid=(nb,),
        in_specs=[pl.BlockSpec((1, half, cols), lambda i, c, me: (i, c[0], 0)), blk],
        out_specs=[blk, pl.BlockSpec((1, half, cols), lambda i, c, me: (me[0], 0, 0))])
    shp = jax.ShapeDtypeStruct(got.shape, got.dtype)
    return pl.pallas_call(
        body, name=name, grid_spec=grid_spec, out_shape=[shp, shp],
        compiler_params=_params(("arbitrary",)),
    )(c_idx, me_idx, t, got)


def _add4_half(parts, c_idx, name):
    nb, half, cols = parts.shape
    t = _tile(half, 256)
    if half % t:
        t = half
    steps = half // t

    def body(c_ref, p_ref, o_ref):
        del c_ref
        acc = p_ref[0].astype(F32)
        for k in range(1, nb):
            acc = acc + p_ref[k].astype(F32)
        o_ref[...] = acc

    grid_spec = pltpu.PrefetchScalarGridSpec(
        num_scalar_prefetch=1, grid=(steps,),
        in_specs=[pl.BlockSpec((nb, t, cols), lambda i, c: (0, i, 0))],
        out_specs=pl.BlockSpec((t, cols), lambda i, c: (c[0] * steps + i, 0)))
    return pl.pallas_call(
        body, name=name, grid_spec=grid_spec, out_shape=jax.ShapeDtypeStruct((2 * half, cols), F32),
        compiler_params=_params(("parallel",)),
    )(c_idx, parts)


def _rs_join(fulls, name):
    n = len(fulls)
    plan = _half_plan(fulls, 0)

    def body(*refs):
        out_refs = refs[n:2 * n]
        send_sems, recv_sems, _ = refs[2 * n:]
        x, y, c, _o = _place()
        sends = []
        for i, (a, r0, nr) in enumerate(plan):
            blk = out_refs[a].at[_rows(c * (fulls[a].shape[0] // 2) + r0, nr), :]
            cp = _remote(blk, blk, send_sems, recv_sems, i, (x, y, 1 - c))
            cp.start()
            sends.append(cp)
        for i, (a, r0, nr) in enumerate(plan):
            blk = out_refs[a].at[_rows((1 - c) * (fulls[a].shape[0] // 2) + r0, nr), :]
            _remote(blk, blk, send_sems, recv_sems, i, (x, y, c)).wait_recv()
        for cp in sends:
            cp.wait_send()

    outs = [jax.ShapeDtypeStruct(f.shape, f.dtype) for f in fulls]
    return _comm_call(body, name, fulls, outs, len(plan), 0, aliases={a: a for a in range(n)})


_HBM = pl.BlockSpec(memory_space=pltpu.HBM)
_SEM = pl.BlockSpec(memory_space=pltpu.SEMAPHORE)
_EFFECT = pltpu.SideEffectType.DATAFLOW_SIDE_EFFECTING


def _ici_plan(kind, a_list):
    if kind == "gather":
        return _half_plan(a_list, 0)
    plan = []
    for a, p in enumerate(a_list):
        plan += [(a, r0, nr) for r0, nr in _pieces(p.shape[1], p.shape[2] * p.dtype.itemsize)]
    return plan


def _ici_refs(kind, a_ref, b_ref, a_shape, r0, nr, c, me, peer):
    if kind == "gather":
        rows = _rows(c * (a_shape[0] // 2) + r0, nr)
        return a_ref.at[rows, :], b_ref.at[me, rows, :], b_ref.at[peer, rows, :]
    rows = pl.ds(r0, nr)
    return a_ref.at[peer, rows, :], b_ref.at[me, rows, :], b_ref.at[peer, rows, :]


def _ici_start(kind, a_list, b_list, name):
    n = len(a_list)
    plan = _ici_plan(kind, a_list)
    shapes = [a.shape for a in a_list]

    def body(*refs):
        a_refs, b_refs = refs[:n], refs[n:2 * n]
        send_sems, recv_sems = refs[2 * n], refs[2 * n + 1]
        token = refs[4 * n + 2]
        x, y, c, others = _place()
        me = 2 * x + y
        for i, (a, r0, nr) in enumerate(plan):
            for j, (ox, oy) in enumerate(others):
                src, dst, _ = _ici_refs(kind, a_refs[a], b_refs[a], shapes[a], r0, nr, c, me, 2 * ox + oy)
                _remote(src, dst, send_sems, recv_sems, 3 * i + j, (ox, oy, c)).start()
        token[...] = jnp.zeros_like(token)

    hbm = lambda v: pltpu.HBM(v.shape, v.dtype)
    ncp = 3 * len(plan)
    outs = pl.pallas_call(
        body, name=name,
        in_specs=[_HBM] * (2 * n),
        out_specs=[_SEM, _SEM] + [_HBM] * (2 * n) + [pl.BlockSpec(memory_space=pltpu.VMEM)],
        out_shape=[pltpu.SemaphoreType.DMA((ncp,)), pltpu.SemaphoreType.DMA((ncp,))]
                  + [hbm(v) for v in a_list] + [hbm(v) for v in b_list] + [jax.ShapeDtypeStruct((8, LANES), F32)],
        input_output_aliases={i: 2 + i for i in range(2 * n)},
        compiler_params=pltpu.CompilerParams(has_side_effects=_EFFECT),
    )(*[pltpu.with_memory_space_constraint(v, pltpu.HBM) for v in list(a_list) + list(b_list)])
    return outs[0], outs[1], outs[2:2 + n], outs[2 + n:2 + 2 * n], outs[2 + 2 * n]


def _ici_wait(kind, started, after, name):
    send_sems, recv_sems, a_list, b_list, _ = started
    n = len(a_list)
    plan = _ici_plan(kind, a_list)
    shapes = [a.shape for a in a_list]

    def body(*refs):
        a_refs, b_refs = refs[:n], refs[n:2 * n]
        send_sems, recv_sems = refs[2 * n], refs[2 * n + 1]
        x, y, c, others = _place()
        me = 2 * x + y
        for i, (a, r0, nr) in enumerate(plan):
            for j, (ox, oy) in enumerate(others):
                src, dst, land = _ici_refs(kind, a_refs[a], b_refs[a], shapes[a], r0, nr, c, me, 2 * ox + oy)
                _remote(src, dst, send_sems, recv_sems, 3 * i + j, (ox, oy, c)).wait_send()
                _remote(land, land, send_sems, recv_sems, 3 * i + j, (x, y, c)).wait_recv()

    hbm = lambda v: pltpu.HBM(v.shape, v.dtype)
    outs = pl.pallas_call(
        body, name=name,
        in_specs=[_HBM] * (2 * n) + [_SEM, _SEM, _ANY],
        out_specs=[_HBM] * (2 * n),
        out_shape=[hbm(v) for v in a_list] + [hbm(v) for v in b_list],
        input_output_aliases={i: i for i in range(2 * n)},
        compiler_params=pltpu.CompilerParams(has_side_effects=_EFFECT),
    )(*a_list, *b_list, send_sems, recv_sems, after)
    return outs[n:]


def _rs_begin(ts, c_idx, me_idx, tag):
    got = _rs_swap(ts, "rs_swap_" + tag)
    pairs = [_add_half(t, g, c_idx, me_idx, f"rs_add2_{tag}_{a}") for a, (t, g) in enumerate(zip(ts, got))]
    return _ici_start("scatter", [p for p, _ in pairs], [m for _, m in pairs], "rs_xchg_start_" + tag)


def _rs_finish(started, after, c_idx, tag):
    parts = _ici_wait("scatter", started, after, "rs_xchg_wait_" + tag)
    fulls = [_add4_half(p, c_idx, f"rs_add4_{tag}_{a}") for a, p in enumerate(parts)]
    return _rs_join(fulls, "rs_join_" + tag)


def _pack_rows(pieces, rows, dtype):
    flat = jnp.concatenate([p.astype(dtype).reshape(-1) for p in pieces])
    return jnp.pad(flat, (0, rows * PACK_COLS - flat.shape[0])).reshape(rows, PACK_COLS)


def _unpack(flat, shapes):
    out, off = [], 0
    for shp in shapes:
        size = math.prod(shp)
        out.append(flat[off:off + size].reshape(shp))
        off += size
    return out


def _rows_for(n_elems, mult):
    rows = -(-n_elems // PACK_COLS)
    return -(-rows // mult) * mult


BIG_SHARDS = [("w_in", (D_MODEL, 1474)), ("w_branch_att", (D_ATT, 256)), ("w_branch_conv", (D_CONV, 256)),
              ("w_branch_sgu", (D_SGU, 256)), ("w_out", (256, D_MODEL)), ("w_ffn_up", (D_MODEL, FF_BLK)),
              ("w_ffn_down", (D_FF // N_CHIPS, D_MODEL))]
SMALL_SHARDS = [("b_gate", (3, 256)), ("conv_mix_w", (3, 64)), ("conv_ffn_w", (3, FF_BLK))]
REPLICATED = [("pre_mix_g", (D_MODEL,)), ("post_mix_g", (D_MODEL,)), ("pre_ffn_g", (D_MODEL,)),
              ("post_ffn_g", (D_MODEL,)), ("b_forget", (N_HEADS,)), ("sgu_ln_g", (D_SGU,)), ("sgu_ln_b", (D_SGU,)),
              ("sgu_w", (N_GROUPS, CHUNK, CHUNK)), ("sgu_b", (N_GROUPS, CHUNK))]
WEIGHT_ORDER = ["pre_mix_g", "post_mix_g", "pre_ffn_g", "post_ffn_g", "w_in", "b_forget", "b_gate", "conv_mix_w",
                "sgu_ln_g", "sgu_ln_b", "sgu_w", "sgu_b", "w_branch_att", "w_branch_conv", "w_branch_sgu", "w_out",
                "w_ffn_up", "conv_ffn_w", "w_ffn_down"]

_SMALL_ELEMS = sum(math.prod(s) for _, s in SMALL_SHARDS)
_REP_ELEMS = sum(math.prod(s) for _, s in REPLICATED)
_REP_QUARTER = -(-(DEPTH * _REP_ELEMS) // N_CHIPS)
SMALL_PARAM_ROWS = _rows_for(DEPTH * _SMALL_ELEMS, 32)
SMALL_ROWS = _rows_for(DEPTH * _SMALL_ELEMS + _REP_QUARTER, 32)
IN_WIDTH = 5896
IN_SHARD = IN_WIDTH // N_CHIPS
IN_SHARD_PAD = 1536
IN_PAD = 6144


def _gather_small(wts):
    shard = _pack_rows([wts[n] for n, _ in SMALL_SHARDS], SMALL_PARAM_ROWS, F32)
    full = _all_gather_chips(shard, "gather_small_params").reshape(N_CHIPS, -1)
    per_chip = [_unpack(full[j], [(DEPTH,) + s for _, s in SMALL_SHARDS]) for j in range(N_CHIPS)]
    return {n: jnp.concatenate([per_chip[j][i] for j in range(N_CHIPS)], axis=-1)
            for i, (n, _) in enumerate(SMALL_SHARDS)}


def _gather_begin(wts, l, me_idx):
    cast = [_cast_shard(wts[n], l, me_idx, "cast_" + n) for n, _ in BIG_SHARDS]
    return _ici_start("gather", [sh for sh, _ in cast], [ld for _, ld in cast], "gather_ici_start")


def _gather_finish(started, after):
    lands = _ici_wait("gather", started, after, "gather_ici_wait")
    return dict(zip([n for n, _ in BIG_SHARDS], _gather_d2d(lands, "gather_d2d")))


def _pad_rows(a, rows):
    return jnp.pad(a, ((0, rows - a.shape[0]), (0, 0)))


def _whole_cols(land):
    return land.transpose(1, 0, 2).reshape(land.shape[1], -1)


_O_F = 3 * D_ATT
_O_B = _O_F + N_HEADS
_O_GL = _O_B + 3 * D_CONV + 2 * D_SGU


def _prep_layer(wts, lands, small, l):
    w_in = _whole_cols(lands["w_in"])
    up = lands["w_ffn_up"]
    cf = small["conv_ffn_w"][l]
    blk = lambda a, j: a[:, j * FF_BLK:(j + 1) * FF_BLK]
    return {
        "w_p": jnp.concatenate([w_in[:, _O_GL:], w_in[:, :_O_F], w_in[:, _O_B:_O_GL]], axis=1),
        "w_in_bwd": jnp.concatenate([w_in[:, :_O_F], w_in[:, _O_B:], w_in[:, _O_F:_O_B],
                                     jnp.zeros((D_MODEL, IN_PAD - IN_WIDTH), BF16)], axis=1),
        "wf_t": _pad_rows(w_in[:, _O_F:_O_B].T, F_ROWS),
        "b_forget": _pad_rows(wts["b_forget"][l].reshape(N_HEADS, 1), F_ROWS),
        "b_gate": _pad_rows(small["b_gate"][l], 8),
        "conv_mix_w": _pad_rows(small["conv_mix_w"][l], 8),
        "w_att": _whole_cols(lands["w_branch_att"]), "w_conv": _whole_cols(lands["w_branch_conv"]),
        "w_sgu": _whole_cols(lands["w_branch_sgu"]),
        "w_out": lands["w_out"].reshape(D_MODEL, D_MODEL),
        "w_up": jnp.concatenate([up[0], up[2], up[1], up[3]], axis=1),
        "conv_ffn_w": _pad_rows(jnp.concatenate([blk(cf, 0), blk(cf, 2), blk(cf, 1), blk(cf, 3)], axis=1), 8),
        "w_down": lands["w_ffn_down"].reshape(D_FF, D_MODEL),
        "pre_mix_g": wts["pre_mix_g"][l].reshape(1, -1), "post_mix_g": wts["post_mix_g"][l].reshape(1, -1),
        "pre_ffn_g": wts["pre_ffn_g"][l].reshape(1, -1), "post_ffn_g": wts["post_ffn_g"][l].reshape(1, -1),
        "ln_g": wts["sgu_ln_g"][l].reshape(1, -1), "ln_b": wts["sgu_ln_b"][l].reshape(1, -1),
        "sgu_w": wts["sgu_w"][l],
        "sgu_bias": jnp.repeat(wts["sgu_b"][l].T, HEAD_DIM, axis=1),
    }


def _layer_fwd(x, p, dep=None):
    s = x.shape[0]
    xn = _rms_fwd(x, p["pre_mix_g"], "rms_pre_mix", dep)
    h = _mm(xn, p["w_p"], "nn", BF16, "mm_in", s, 256, D_MODEL)
    f_row = _mm(p["wf_t"], xn, "nt", F32, "mm_forget", F_ROWS, 2048, D_MODEL)
    ck = _gate_fwd(f_row, p["b_forget"], "gate_fwd")
    o, o_f32, lse = _attn_fwd(h, ck, "attn_fwd")
    yc = _sconv_fwd(h, p["conv_mix_w"], "sconv_fwd")
    ys = _sgu_fwd(h, p["ln_g"], p["ln_b"], p["sgu_w"], p["sgu_bias"], "sgu_fwd")
    merged = _merge_fwd(h, (o, yc, ys), (p["w_att"], p["w_conv"], p["w_sgu"]), p["b_gate"], "merge_fwd")
    mo = _mm(merged, p["w_out"], "nn", F32, "mm_out", 2048, 512, D_MODEL)
    x1 = _resid_post(x, mo, p["post_mix_g"], "post_mix")
    xn2 = _rms_fwd(x1, p["pre_ffn_g"], "rms_pre_ffn")
    h2 = _mm(xn2, p["w_up"], "nn", BF16, "mm_up", 2048, 512, D_MODEL)
    pact = _ffn_act_fwd(h2, p["conv_ffn_w"], "ffn_act_fwd")
    ff = _mm(pact, p["w_down"], "nn", F32, "mm_down", 2048, 512, FF_BLK)
    x2 = _resid_post(x1, ff, p["post_ffn_g"], "post_ffn")
    saved = dict(x=x, xn=xn, h=h, f_row=f_row, ck=ck, o=o, o_f32=o_f32, lse=lse, yc=yc, ys=ys, merged=merged, mo=mo, x1=x1,
                 xn2=xn2, h2=h2, pact=pact, ff=ff)
    return x2, saved


def _layer_bwd(dx2, p, sv, dep=None):
    s = dx2.shape[0]
    g = {}
    same = lambda b: b
    dff, g["post_ffn_g"] = _rms_bwd(sv["ff"], p["post_ffn_g"], [dx2], None, BF16, "post_ffn_bwd", dep)
    dpact = _mm(dff, p["w_down"], "nt", BF16, "mm_down_dx", 1024, FF_BLK, D_MODEL)
    t_down = _mm(sv["pact"], dff, "tn", BF16, "mm_down_dw", 256, D_MODEL, s).reshape(N_CHIPS, -1, D_MODEL)
    dh2, dconv_ffn = _ffn_act_conv_bwd(sv["h2"], p["conv_ffn_w"], dpact, "ffn_act_conv_bwd")
    dxn2 = _mm(dh2, p["w_up"], "nt", F32, "mm_up_dx", 1024, D_MODEL, FF_BLK)
    t_up = _mm(sv["xn2"], dh2, "tn", BF16, "mm_up_dw", 512, FF_BLK, s, chip_of=lambda b: (b % 2) * 2 + b // 2)
    dx1, g["pre_ffn_g"] = _rms_bwd(sv["x1"], p["pre_ffn_g"], [dxn2], dx2, F32, "pre_ffn_bwd")
    dmo, g["post_mix_g"] = _rms_bwd(sv["mo"], p["post_mix_g"], [dx1], None, BF16, "post_mix_bwd")
    dmerged = _mm(dmo, p["w_out"], "nt", F32, "mm_out_dx", 2048, 512, D_MODEL)
    t_out = _mm(sv["merged"], dmo, "tn", BF16, "mm_out_dw", 512, D_MODEL, s).reshape(N_CHIPS, -1, D_MODEL)
    acts = (sv["o"], sv["yc"], sv["ys"])
    ws = (p["w_att"], p["w_conv"], p["w_sgu"])
    dy_a, dy_c, dy_s, dgl, db_gate = _merge_bwd(sv["h"], acts, ws, p["b_gate"], dmerged, "merge_bwd")
    do = _mm(dy_a, p["w_att"], "nt", BF16, "mm_att_dx", 2048, D_ATT, D_MODEL)
    dyc = _mm(dy_c, p["w_conv"], "nt", BF16, "mm_conv_dx", 2048, D_CONV, D_MODEL)
    dys = _mm(dy_s, p["w_sgu"], "nt", BF16, "mm_sgu_dx", 2048, D_SGU, D_MODEL)
    t_att = _mm(sv["o"], dy_a, "tn", BF16, "mm_att_dw", D_ATT, 256, s, chip_of=same)
    t_conv = _mm(sv["yc"], dy_c, "tn", BF16, "mm_conv_dw", D_CONV, 256, s, chip_of=same)
    t_sgu = _mm(sv["ys"], dy_s, "tn", BF16, "mm_sgu_dw", D_SGU, 256, s, chip_of=same)
    d_conv, dconv_mix = _sconv_bwd(sv["h"], p["conv_mix_w"], dyc, "sconv_bwd")
    d_sgu, g["sgu_ln_g"], g["sgu_ln_b"], g["sgu_w"], dbias = _sgu_bwd(
        sv["h"], p["ln_g"], p["ln_b"], p["sgu_w"], p["sgu_bias"], dys, "sgu_bwd")
    dq, dk, dv, dc_even, dc_odd = _attn_bwd(sv["h"], sv["ck"], sv["o_f32"], sv["lse"], do, "attn_bwd")
    df, db_forget = _gate_bwd(sv["f_row"], p["b_forget"], dc_even, dc_odd, "gate_bwd")
    f_cols = jnp.concatenate([df[:N_HEADS].T, jnp.zeros((s, IN_PAD - IN_WIDTH), BF16)], axis=1)
    dh = _assemble_dh([dq, dk, dv, d_conv, d_sgu, dgl, f_cols], "assemble_dh")
    dxn = _mm(dh, p["w_in_bwd"], "nt", F32, "mm_in_dx", 1024, D_MODEL, 2048)
    dw_bwd = _mm(sv["xn"], dh, "tn", F32, "mm_in_dw", D_MODEL, 512, s)
    n_rest = IN_WIDTH - N_HEADS
    dw_in = jnp.concatenate([dw_bwd[:, :_O_F], dw_bwd[:, n_rest:IN_WIDTH], dw_bwd[:, _O_F:n_rest],
                             jnp.zeros((D_MODEL, IN_SHARD_PAD - IN_SHARD), F32)], axis=1)
    t_in = jnp.stack([dw_in[:, j * IN_SHARD:j * IN_SHARD + IN_SHARD_PAD] for j in range(N_CHIPS)]).astype(BF16)
    dx, g["pre_mix_g"] = _rms_bwd(sv["x"], p["pre_mix_g"], [dxn], dx1, F32, "pre_mix_bwd")
    blk = lambda a, j: a[:, j * FF_BLK:(j + 1) * FF_BLK]
    g["conv_ffn_w"] = jnp.concatenate([blk(dconv_ffn, 0), blk(dconv_ffn, 2), blk(dconv_ffn, 1),
                                       blk(dconv_ffn, 3)], axis=1)[:3]
    g["conv_mix_w"] = dconv_mix[:3]
    g["b_gate"] = db_gate[:3]
    g["b_forget"] = db_forget[:N_HEADS, 0]
    g["sgu_b"] = jnp.sum(dbias.reshape(CHUNK, N_GROUPS, HEAD_DIM), axis=-1).T
    for n in ("pre_mix_g", "post_mix_g", "pre_ffn_g", "post_ffn_g", "sgu_ln_g", "sgu_ln_b"):
        g[n] = g[n].reshape(-1)
    return dx, [t_in, t_att, t_conv, t_sgu, t_out, t_up, t_down], g


def _assemble_dh(pieces, name):
    s = pieces[0].shape[0]
    t = _tile(s, 512)
    width = sum(a.shape[1] for a in pieces)

    def body(*refs):
        out = refs[-1]
        col = 0
        for ref in refs[:-1]:
            w = ref.shape[1]
            out[:, col:col + w] = ref[...].astype(out.dtype)
            col += w

    return pl.pallas_call(
        body, name=name, grid=(s // t,),
        in_specs=[pl.BlockSpec((t, a.shape[1]), lambda i: (i, 0)) for a in pieces],
        out_specs=pl.BlockSpec((t, width), lambda i: (i, 0)),
        out_shape=jax.ShapeDtypeStruct((s, width), BF16),
        compiler_params=_params(("parallel",)),
    )(*pieces)


def _shard_cols(a, j):
    w = a.shape[-1] // N_CHIPS
    return a[..., j * w:(j + 1) * w]


def kernel(x, pre_mix_g, post_mix_g, pre_ffn_g, post_ffn_g, w_in, b_forget, b_gate, conv_mix_w, sgu_ln_g, sgu_ln_b, sgu_w, sgu_b, w_branch_att, w_branch_conv, w_branch_sgu, w_out, w_ffn_up, conv_ffn_w, w_ffn_down, loss_target, m_pre_mix_g, m_post_mix_g, m_pre_ffn_g, m_post_ffn_g, m_w_in, m_b_forget, m_b_gate, m_conv_mix_w, m_sgu_ln_g, m_sgu_ln_b, m_sgu_w, m_sgu_b, m_w_branch_att, m_w_branch_conv, m_w_branch_sgu, m_w_out, m_w_ffn_up, m_conv_ffn_w, m_w_ffn_down, v_pre_mix_g, v_post_mix_g, v_pre_ffn_g, v_post_ffn_g, v_w_in, v_b_forget, v_b_gate, v_conv_mix_w, v_sgu_ln_g, v_sgu_ln_b, v_sgu_w, v_sgu_b, v_w_branch_att, v_w_branch_conv, v_w_branch_sgu, v_w_out, v_w_ffn_up, v_conv_ffn_w, v_w_ffn_down):
    wts = dict(pre_mix_g=pre_mix_g, post_mix_g=post_mix_g, pre_ffn_g=pre_ffn_g, post_ffn_g=post_ffn_g, w_in=w_in,
               b_forget=b_forget, b_gate=b_gate, conv_mix_w=conv_mix_w, sgu_ln_g=sgu_ln_g, sgu_ln_b=sgu_ln_b,
               sgu_w=sgu_w, sgu_b=sgu_b, w_branch_att=w_branch_att, w_branch_conv=w_branch_conv,
               w_branch_sgu=w_branch_sgu, w_out=w_out, w_ffn_up=w_ffn_up, conv_ffn_w=conv_ffn_w,
               w_ffn_down=w_ffn_down)
    moms = dict(pre_mix_g=m_pre_mix_g, post_mix_g=m_post_mix_g, pre_ffn_g=m_pre_ffn_g, post_ffn_g=m_post_ffn_g,
                w_in=m_w_in, b_forget=m_b_forget, b_gate=m_b_gate, conv_mix_w=m_conv_mix_w, sgu_ln_g=m_sgu_ln_g,
                sgu_ln_b=m_sgu_ln_b, sgu_w=m_sgu_w, sgu_b=m_sgu_b, w_branch_att=m_w_branch_att,
                w_branch_conv=m_w_branch_conv, w_branch_sgu=m_w_branch_sgu, w_out=m_w_out, w_ffn_up=m_w_ffn_up,
                conv_ffn_w=m_conv_ffn_w, w_ffn_down=m_w_ffn_down)
    vels = dict(pre_mix_g=v_pre_mix_g, post_mix_g=v_post_mix_g, pre_ffn_g=v_pre_ffn_g, post_ffn_g=v_post_ffn_g,
                w_in=v_w_in, b_forget=v_b_forget, b_gate=v_b_gate, conv_mix_w=v_conv_mix_w, sgu_ln_g=v_sgu_ln_g,
                sgu_ln_b=v_sgu_ln_b, sgu_w=v_sgu_w, sgu_b=v_sgu_b, w_branch_att=v_w_branch_att,
                w_branch_conv=v_w_branch_conv, w_branch_sgu=v_w_branch_sgu, w_out=v_w_out, w_ffn_up=v_w_ffn_up,
                conv_ffn_w=v_conv_ffn_w, w_ffn_down=v_w_ffn_down)

    c_idx = lax.axis_index("c").astype(jnp.int32).reshape(1)
    me_idx = (2 * lax.axis_index("x") + lax.axis_index("y")).astype(jnp.int32).reshape(1)
    small = _gather_small(wts)

    xs = x[0]
    layers, saved = [], []
    lands = _gather_finish(_gather_begin(wts, 0, me_idx), xs)
    for l in range(DEPTH):
        p = _prep_layer(wts, lands, small, l)
        nxt = _gather_begin(wts, l + 1, me_idx) if l + 1 < DEPTH else None
        xs, sv = _layer_fwd(xs, p, nxt[4] if nxt else None)
        if nxt:
            lands = _gather_finish(nxt, xs)
        layers.append(p)
        saved.append(sv)
    dy, loss_part = _loss_head(xs, loss_target[0], "loss_head")
    loss = lax.psum(loss_part[0, 0], ("x", "y", "c"))

    big_red = [None] * DEPTH
    small_grads = [None] * DEPTH
    pending = None
    for l in reversed(range(DEPTH)):
        dy, ts, small_grads[l] = _layer_bwd(dy, layers[l], saved[l], pending[4] if pending else None)
        if pending:
            big_red[l + 1] = _rs_finish(pending, dy, c_idx, "big")
        pending = _rs_begin(ts, c_idx, me_idx, "big")
    grad_x = dy[None]

    rep_flat = jnp.concatenate([small_grads[l][n].reshape(-1) for l in range(DEPTH) for n, _ in REPLICATED])
    rep_flat = jnp.pad(rep_flat, (0, N_CHIPS * _REP_QUARTER - rep_flat.shape[0]))
    rows = []
    for j in range(N_CHIPS):
        pieces = [_shard_cols(small_grads[l][n], j) for l in range(DEPTH) for n, _ in SMALL_SHARDS]
        pieces.append(rep_flat[j * _REP_QUARTER:(j + 1) * _REP_QUARTER])
        rows.append(_pack_rows(pieces, SMALL_ROWS, F32))
    small_red = _reduce_scatter_chips(jnp.stack(rows), "small")
    small_all = _all_gather_chips(small_red, "gather_small")
    big_red[0] = _rs_finish(pending, small_all, c_idx, "big")
    small_all = small_all.reshape(N_CHIPS, -1)

    grads = {}
    for i, (n, _) in enumerate(BIG_SHARDS):
        grads[n] = jnp.stack([big_red[l][i][:, :IN_SHARD] if n == "w_in" else big_red[l][i] for l in range(DEPTH)])
    mine_small = small_red.reshape(-1)
    parts = _unpack(mine_small, [s for _ in range(DEPTH) for _, s in SMALL_SHARDS])
    for i, (n, _) in enumerate(SMALL_SHARDS):
        grads[n] = jnp.stack([parts[l * len(SMALL_SHARDS) + i] for l in range(DEPTH)])
    off = DEPTH * _SMALL_ELEMS
    rep_all = jnp.concatenate([small_all[j, off:off + _REP_QUARTER] for j in range(N_CHIPS)])
    parts = _unpack(rep_all, [s for _ in range(DEPTH) for _, s in REPLICATED])
    for i, (n, _) in enumerate(REPLICATED):
        grads[n] = jnp.stack([parts[l * len(REPLICATED) + i] for l in range(DEPTH)])

    deltas, new_m, new_v = {}, {}, {}
    for n in WEIGHT_ORDER:
        deltas[n], new_m[n], new_v[n] = _adamw(wts[n], grads[n], moms[n], vels[n], "adamw_" + n)
    return (loss, grad_x, *[grads[n] for n in WEIGHT_ORDER], *[deltas[n] for n in WEIGHT_ORDER],
            *[new_m[n] for n in WEIGHT_ORDER], *[new_v[n] for n in WEIGHT_ORDER])
```

```python
import functools
import math

import jax
import jax.numpy as jnp
from jax import lax
from jax.experimental import pallas as pl
from jax.experimental.pallas import tpu as pltpu

F32 = jnp.float32
BF16 = jnp.bfloat16
MXU_DTYPE = jnp.bfloat16

D_MODEL = 1024
HEAD_DIM = 64
N_HEADS = 8
D_ATT = 512
D_CONV = 256
D_SGU = 256
N_GROUPS = 4
CHUNK = 128
D_FF = 2816
DEPTH = 4
RMS_EPS = 1e-6
LN_EPS = 1e-5
N_CHIPS = 4
LANES = 128
PACK_COLS = 1024
HALO = 16

ADAM_LR = 0.001
ADAM_B1 = 0.9
ADAM_B2 = 0.999
ADAM_EPS = 1e-08
ADAM_WD = 0.01
ADAM_STEP = 10

OFF_GL = 0
OFF_Q = 3 * D_MODEL
OFF_K = OFF_Q + D_ATT
OFF_V = OFF_K + D_ATT
OFF_BG = OFF_V + D_ATT
OFF_CG = OFF_BG + D_CONV
OFF_HC = OFF_CG + D_CONV
OFF_U = OFF_HC + D_CONV
OFF_VS = OFF_U + D_SGU
W_P = OFF_VS + D_SGU
F_ROWS = 16

VMEM_LIMIT = 56 * 1024 * 1024
MESH = pl.DeviceIdType.MESH


def _params(sem=None):
    if sem is None:
        return pltpu.CompilerParams(vmem_limit_bytes=VMEM_LIMIT)
    return pltpu.CompilerParams(dimension_semantics=sem, vmem_limit_bytes=VMEM_LIMIT)


def _tile(dim, pref):
    if dim <= pref:
        return dim
    if dim % pref == 0:
        return pref
    return dim


_DIMS = {"nn": (((1,), (0,)), ((), ())), "nt": (((1,), (1,)), ((), ())), "tn": (((0,), (0,)), ((), ()))}


def _mm(a, b, mode, out_dtype, name, tm, tn, tk, chip_of=None):
    if mode == "tn":
        K, M = a.shape
    else:
        M, K = a.shape
    N = b.shape[0] if mode == "nt" else b.shape[1]
    tm, tn, tk = _tile(M, tm), _tile(N // N_CHIPS if chip_of else N, tn), _tile(K, tk)
    nk = K // tk
    dims = _DIMS[mode]

    def body(a_ref, b_ref, o_ref, *acc):
        part = lax.dot_general(a_ref[...].astype(MXU_DTYPE), b_ref[...].astype(MXU_DTYPE), dims,
                               preferred_element_type=F32)
        if nk == 1:
            o_ref[...] = part.astype(o_ref.dtype)
        else:
            acc_ref = acc[0]
            k = pl.program_id(2)

            @pl.when(k == 0)
            def _():
                acc_ref[...] = part

            @pl.when(k > 0)
            def _():
                acc_ref[...] += part

            @pl.when(k == nk - 1)
            def _():
                o_ref[...] = acc_ref[...].astype(o_ref.dtype)

    if mode == "tn":
        a_spec = pl.BlockSpec((tk, tm), lambda i, j, k: (k, i))
    else:
        a_spec = pl.BlockSpec((tm, tk), lambda i, j, k: (i, k))
    if mode == "nt":
        b_spec = pl.BlockSpec((tn, tk), lambda i, j, k: (j, k))
    else:
        b_spec = pl.BlockSpec((tk, tn), lambda i, j, k: (k, j))
    if chip_of is None:
        out_spec = pl.BlockSpec((tm, tn), lambda i, j, k: (i, j))
        out_shape = jax.ShapeDtypeStruct((M, N), out_dtype)
    else:
        per = (N // N_CHIPS) // tn
        out_spec = pl.BlockSpec((None, tm, tn), lambda i, j, k: (chip_of(j // per), i, j % per))
        out_shape = jax.ShapeDtypeStruct((N_CHIPS, M, N // N_CHIPS), out_dtype)
    return pl.pallas_call(
        body,
        name=name,
        grid=(M // tm, N // tn, nk),
        in_specs=[a_spec, b_spec],
        out_specs=out_spec,
        out_shape=out_shape,
        scratch_shapes=[pltpu.VMEM((tm, tn), F32)] if nk > 1 else [],
        compiler_params=_params(("parallel", "parallel", "arbitrary")),
    )(a, b)


_GELU_K = math.sqrt(2.0 / math.pi)
_GELU_C = 0.044715


def _gelu(x):
    t = jnp.tanh(_GELU_K * (x + _GELU_C * (x * x * x)))
    return x * (0.5 * (1.0 + t))


def _gelu_and_grad(x):
    x2 = x * x
    t = jnp.tanh(_GELU_K * (x + _GELU_C * (x2 * x)))
    cdf = 0.5 * (1.0 + t)
    dcdf = 0.5 * (1.0 - t * t) * (_GELU_K * (1.0 + 3.0 * _GELU_C * x2))
    return x * cdf, cdf + x * dcdf


def _sigmoid(x):
    return 1.0 / (1.0 + jnp.exp(-x))


def _shift_down(cur, prev, k):
    h = prev.shape[0]
    ext = jnp.concatenate([prev, cur], axis=0)
    return pltpu.roll(ext, k, 0)[h:]


def _shift_up(cur, nxt, k):
    t, h = cur.shape[0], nxt.shape[0]
    ext = jnp.concatenate([cur, nxt], axis=0)
    return pltpu.roll(ext, t + h - k, 0)[:t]


def _row_sum8(x):
    t, c = x.shape
    return jnp.sum(x.reshape(t // 8, 8, c), axis=0)


_DEP = pl.BlockSpec((8, LANES), lambda i: (0, 0))


def _rms_fwd(x, g, name, dep=None):
    s, d = x.shape
    t = _tile(s, 512)

    def body(x_ref, g_ref, *rest):
        o_ref = rest[-1]
        xv = x_ref[...]
        r = lax.rsqrt(jnp.mean(xv * xv, axis=-1, keepdims=True) + RMS_EPS)
        o_ref[...] = (xv * r * g_ref[...]).astype(o_ref.dtype)

    deps = [] if dep is None else list(dep) if isinstance(dep, (list, tuple)) else [dep]
    return pl.pallas_call(
        body, name=name, grid=(s // t,),
        in_specs=[pl.BlockSpec((t, d), lambda i: (i, 0)), pl.BlockSpec((1, d), lambda i: (0, 0))] + [_DEP] * len(deps),
        out_specs=pl.BlockSpec((t, d), lambda i: (i, 0)),
        out_shape=jax.ShapeDtypeStruct((s, d), BF16),
        compiler_params=_params(("parallel",)),
    )(x, g, *deps)


def _resid_post(x, y, g, name):
    s, d = x.shape
    t = _tile(s, 512)

    def body(x_ref, y_ref, g_ref, o_ref):
        yv = y_ref[...]
        r = lax.rsqrt(jnp.mean(yv * yv, axis=-1, keepdims=True) + RMS_EPS)
        o_ref[...] = x_ref[...] + yv * r * g_ref[...]

    row = pl.BlockSpec((t, d), lambda i: (i, 0))
    return pl.pallas_call(
        body, name=name, grid=(s // t,),
        in_specs=[row, row, pl.BlockSpec((1, d), lambda i: (0, 0))],
        out_specs=row,
        out_shape=jax.ShapeDtypeStruct((s, d), F32),
        compiler_params=_params(("parallel",)),
    )(x, y, g)


def _rms_bwd(xin, g, dys, dres, out_dtype, name, dep=None):
    s, d = xin.shape
    t = _tile(s, 512)
    n = s // t
    n_dy = len(dys)
    has_res = dres is not None
    deps = [] if dep is None else [dep]

    def body(*refs):
        x_ref, g_ref = refs[0], refs[1]
        dy_refs = refs[2:2 + n_dy]
        pos = 2 + n_dy
        res_ref = refs[pos] if has_res else None
        pos += (1 if has_res else 0) + len(deps)
        dx_ref, dg_ref, acc_ref = refs[pos], refs[pos + 1], refs[pos + 2]
        i = pl.program_id(0)
        xv = x_ref[...]
        dy = dy_refs[0][...].astype(F32)
        for extra in dy_refs[1:]:
            dy = dy + extra[...].astype(F32)
        r = lax.rsqrt(jnp.mean(xv * xv, axis=-1, keepdims=True) + RMS_EPS)
        u = dy * g_ref[...]
        xr = xv * r
        dx = r * (u - xr * jnp.mean(u * xr, axis=-1, keepdims=True))
        if has_res:
            dx = dx + res_ref[...]
        dx_ref[...] = dx.astype(dx_ref.dtype)
        part = _row_sum8(dy * xr)

        @pl.when(i == 0)
        def _():
            acc_ref[...] = part

        @pl.when(i > 0)
        def _():
            acc_ref[...] += part

        @pl.when(i == n - 1)
        def _():
            dg_ref[...] = jnp.sum(acc_ref[...], axis=0, keepdims=True)

    row = pl.BlockSpec((t, d), lambda i: (i, 0))
    vec = pl.BlockSpec((1, d), lambda i: (0, 0))
    ins = [xin, g, *dys] + ([dres] if has_res else []) + deps
    return pl.pallas_call(
        body, name=name, grid=(n,),
        in_specs=[row, vec] + [row] * (n_dy + (1 if has_res else 0)) + [_DEP] * len(deps),
        out_specs=[row, vec],
        out_shape=[jax.ShapeDtypeStruct((s, d), out_dtype), jax.ShapeDtypeStruct((1, d), F32)],
        scratch_shapes=[pltpu.VMEM((8, d), F32)],
        compiler_params=_params(("arbitrary",)),
    )(*ins)


def _loss_head(y, target, name):
    s, d = y.shape
    t = _tile(s, 512)
    n = s // t

    def body(y_ref, t_ref, dy_ref, loss_ref, acc_ref):
        i = pl.program_id(0)
        e = y_ref[...] - t_ref[...]
        dy_ref[...] = e * (1.0 / d)
        part = _row_sum8(e * e)

        @pl.when(i == 0)
        def _():
            acc_ref[...] = part

        @pl.when(i > 0)
        def _():
            acc_ref[...] += part

        @pl.when(i == n - 1)
        def _():
            tot = jnp.sum(jnp.sum(acc_ref[...], axis=0, keepdims=True), axis=1, keepdims=True)
            loss_ref[...] = tot * (0.5 / d)

    row = pl.BlockSpec((t, d), lambda i: (i, 0))
    return pl.pallas_call(
        body, name=name, grid=(n,),
        in_specs=[row, row],
        out_specs=[row, pl.BlockSpec((1, 1), lambda i: (0, 0))],
        out_shape=[jax.ShapeDtypeStruct((s, d), F32), jax.ShapeDtypeStruct((1, 1), F32)],
        scratch_shapes=[pltpu.VMEM((8, d), F32)],
        compiler_params=_params(("arbitrary",)),
    )(y, target)


def _split3(x):
    hi = x.astype(BF16)
    r1 = x - hi.astype(F32)
    mid = r1.astype(BF16)
    lo = (r1 - mid.astype(F32)).astype(BF16)
    return hi, mid, lo


def _tri_dot(x, tri):
    hi, mid, lo = _split3(x)
    dn = _DIMS["nn"]
    out = lax.dot_general(hi, tri, dn, preferred_element_type=F32)
    out = out + lax.dot_general(mid, tri, dn, preferred_element_type=F32)
    return out + lax.dot_general(lo, tri, dn, preferred_element_type=F32)


def _log_sigmoid(z):
    return jnp.minimum(z, 0.0) - jnp.log(1.0 + jnp.exp(-jnp.abs(z)))


def _gate_fwd(f_row, b_col, name):
    rows, s = f_row.shape
    t = _tile(s, 512)
    n = s // t

    def body(f_ref, b_ref, ck_ref, carry_ref):
        i = pl.program_id(0)

        @pl.when(i == 0)
        def _():
            carry_ref[...] = jnp.zeros_like(carry_ref)

        logf = _log_sigmoid(f_ref[...] + b_ref[...])
        r = lax.broadcasted_iota(jnp.int32, (t, t), 0)
        c = lax.broadcasted_iota(jnp.int32, (t, t), 1)
        tri = jnp.where(r <= c, 1.0, 0.0).astype(BF16)
        cs = _tri_dot(logf, tri) + carry_ref[...]
        carry_ref[...] = cs[:, t - 1:t]
        terms = [part.astype(F32) for part in _split3(-cs)]
        sub = lax.broadcasted_iota(jnp.int32, (LANES, t), 0)
        for p in range(N_HEADS // 2):
            stacked = jnp.zeros((LANES, t), F32)
            for hh in range(2):
                for j, term in enumerate(terms):
                    h = 2 * p + hh
                    stacked = jnp.where(sub == 3 * hh + j, jnp.broadcast_to(term[h:h + 1, :], (LANES, t)), stacked)
            ck_ref[p] = stacked.T.astype(ck_ref.dtype)

    return pl.pallas_call(
        body, name=name, grid=(n,),
        in_specs=[pl.BlockSpec((rows, t), lambda i: (0, i)), pl.BlockSpec((rows, 1), lambda i: (0, 0))],
        out_specs=pl.BlockSpec((N_HEADS // 2, t, LANES), lambda i: (0, i, 0)),
        out_shape=jax.ShapeDtypeStruct((N_HEADS // 2, s, LANES), BF16),
        scratch_shapes=[pltpu.VMEM((rows, 1), F32)],
        compiler_params=_params(("arbitrary",)),
    )(f_row, b_col)


def _gate_bwd(f_row, b_col, dc_even, dc_odd, name):
    rows, s = f_row.shape
    t = _tile(s, 512)
    n = s // t

    def body(f_ref, b_ref, dce_ref, dco_ref, df_ref, db_ref, carry_ref, acc_ref):
        i = pl.program_id(0)

        @pl.when(i == 0)
        def _():
            carry_ref[...] = jnp.zeros_like(carry_ref)
            acc_ref[...] = jnp.zeros_like(acc_ref)

        head = lax.broadcasted_iota(jnp.int32, (rows, t), 0)
        dcv = jnp.zeros((rows, t), F32)
        for h in range(N_HEADS):
            src = dce_ref if h % 2 == 0 else dco_ref
            dcv = jnp.where(head == h, jnp.broadcast_to(src[h // 2, 0:1, :], (rows, t)), dcv)
        r = lax.broadcasted_iota(jnp.int32, (t, t), 0)
        c = lax.broadcasted_iota(jnp.int32, (t, t), 1)
        tri = jnp.where(r >= c, 1.0, 0.0).astype(BF16)
        dlogf = _tri_dot(dcv, tri) + carry_ref[...]
        carry_ref[...] = dlogf[:, 0:1]
        z = f_ref[...] + b_ref[...]
        df = dlogf * _sigmoid(-z)
        df_ref[...] = df.astype(df_ref.dtype)
        acc_ref[...] += jnp.sum(df, axis=1, keepdims=True)

        @pl.when(i == n - 1)
        def _():
            db_ref[...] = acc_ref[...]

    rev = lambda i: (0, n - 1 - i)
    dc_spec = pl.BlockSpec((N_HEADS // 2, 8, t), lambda i: (0, 0, n - 1 - i))
    return pl.pallas_call(
        body, name=name, grid=(n,),
        in_specs=[pl.BlockSpec((rows, t), rev), pl.BlockSpec((rows, 1), lambda i: (0, 0)), dc_spec, dc_spec],
        out_specs=[pl.BlockSpec((rows, t), rev), pl.BlockSpec((rows, 1), lambda i: (0, 0))],
        out_shape=[jax.ShapeDtypeStruct((rows, s), BF16), jax.ShapeDtypeStruct((rows, 1), F32)],
        scratch_shapes=[pltpu.VMEM((rows, 1), F32), pltpu.VMEM((rows, 1), F32)],
        compiler_params=_params(("arbitrary",)),
    )(f_row, b_col, dc_even, dc_odd)


_NEG = -1e30
_SCALE = HEAD_DIM ** -0.5


def _head_masks():
    lane = lax.broadcasted_iota(jnp.int32, (1, LANES), 1)
    return [lane < HEAD_DIM, lane >= HEAD_DIM]


def _attn_fwd(h, ck, name):
    s = h.shape[0]
    t = _tile(s, 512)
    n = s // t
    qb, kb, vb = OFF_Q // LANES, OFF_K // LANES, OFF_V // LANES

    pairs = [(qi, ki) for qi in range(n) for ki in range(qi + 1)]
    qi_tab = jnp.asarray([qi for qi, _ in pairs], jnp.int32)
    ki_tab = jnp.asarray([ki for _, ki in pairs], jnp.int32)

    def body(qi_ref, ki_ref, q_ref, k_ref, v_ref, ck_ref, o_ref, of_ref, lse_ref, m_ref, l_ref, acc_ref):
        qi, ki = qi_ref[pl.program_id(1)], ki_ref[pl.program_id(1)]
        masks = _head_masks()
        lane = lax.broadcasted_iota(jnp.int32, (1, LANES), 1)

        @pl.when(ki == 0)
        def _():
            m_ref[...] = jnp.full_like(m_ref, _NEG)
            l_ref[...] = jnp.zeros_like(l_ref)
            acc_ref[...] = jnp.zeros_like(acc_ref)

        def step(diag):
            q = q_ref[...] * _SCALE
            k_aug = jnp.concatenate([k_ref[...], ck_ref[0]], axis=1)
            v = v_ref[...]
            for hh in range(2):
                ones = jnp.where((lane >= 3 * hh) & (lane < 3 * hh + 3), 1.0, 0.0).astype(q.dtype)
                q_aug = jnp.concatenate([jnp.where(masks[hh], q, jnp.zeros_like(q)),
                                         jnp.broadcast_to(ones, q.shape)], axis=1)
                sc = lax.dot_general(k_aug, q_aug, _DIMS["nt"], preferred_element_type=F32)
                if diag:
                    r = lax.broadcasted_iota(jnp.int32, (t, t), 0)
                    cc = lax.broadcasted_iota(jnp.int32, (t, t), 1)
                    sc = jnp.where(r <= cc, sc, _NEG)
                m_prev = m_ref[hh]
                m_new = jnp.maximum(m_prev, jnp.max(sc, axis=0, keepdims=True))
                alpha = jnp.exp(m_prev - m_new)
                p = jnp.exp(sc - m_new)
                l_ref[hh] = alpha * l_ref[hh] + jnp.sum(p, axis=0, keepdims=True)
                m_ref[hh] = m_new
                p_hi = p.astype(MXU_DTYPE)
                p_lo = (p - p_hi.astype(F32)).astype(MXU_DTYPE)
                pv = (lax.dot_general(v, p_hi, _DIMS["tn"], preferred_element_type=F32)
                      + lax.dot_general(v, p_lo, _DIMS["tn"], preferred_element_type=F32))
                rows = slice(hh * HEAD_DIM, (hh + 1) * HEAD_DIM)
                acc_ref[rows, :] = alpha * acc_ref[rows, :] + pv[rows]

        @pl.when(ki < qi)
        def _():
            step(False)

        @pl.when(ki == qi)
        def _():
            step(True)
            inv = jnp.concatenate([jnp.broadcast_to(1.0 / l_ref[hh], (HEAD_DIM, t)) for hh in range(2)], axis=0)
            out = (acc_ref[...] * inv).T
            o_ref[...] = out.astype(o_ref.dtype)
            of_ref[...] = out
            lse = jnp.concatenate([jnp.broadcast_to(m_ref[hh] + jnp.log(l_ref[hh]), (HEAD_DIM, t))
                                   for hh in range(2)], axis=0)
            lse_ref[...] = lse.T

    grid_spec = pltpu.PrefetchScalarGridSpec(
        num_scalar_prefetch=2, grid=(N_HEADS // 2, len(pairs)),
        in_specs=[
            pl.BlockSpec((t, LANES), lambda p, i, qt, kt: (qt[i], qb + p)),
            pl.BlockSpec((t, LANES), lambda p, i, qt, kt: (kt[i], kb + p)),
            pl.BlockSpec((t, LANES), lambda p, i, qt, kt: (kt[i], vb + p)),
            pl.BlockSpec((1, t, LANES), lambda p, i, qt, kt: (p, kt[i], 0)),
        ],
        out_specs=[pl.BlockSpec((t, LANES), lambda p, i, qt, kt: (qt[i], p))] * 3,
        scratch_shapes=[pltpu.VMEM((2, 1, t), F32), pltpu.VMEM((2, 1, t), F32), pltpu.VMEM((LANES, t), F32)])
    return pl.pallas_call(
        body, name=name, grid_spec=grid_spec,
        out_shape=[jax.ShapeDtypeStruct((s, D_ATT), BF16), jax.ShapeDtypeStruct((s, D_ATT), F32),
                   jax.ShapeDtypeStruct((s, D_ATT), F32)],
        compiler_params=_params(("parallel", "arbitrary")),
    )(qi_tab, ki_tab, h, h, h, ck)


def _attn_bwd(h, ck, o, lse, do, name):
    s = h.shape[0]
    t = _tile(s, 512)
    n = s // t
    qb, kb, vb = OFF_Q // LANES, OFF_K // LANES, OFF_V // LANES

    pairs = [(ki, qi) for ki in range(n) for qi in range(ki, n)]
    ki_tab = jnp.asarray([ki for ki, _ in pairs], jnp.int32)
    qi_tab = jnp.asarray([qi for _, qi in pairs], jnp.int32)

    def body(ki_ref, qi_ref, q_ref, k_ref, v_ref, ck_ref, o_ref, lse_ref, do_ref,
             dq_ref, dk_ref, dv_ref, dc0_ref, dc1_ref, dk_acc, dv_acc, dc_acc):
        ki, qi = ki_ref[pl.program_id(1)], qi_ref[pl.program_id(1)]
        masks = _head_masks()
        lane = lax.broadcasted_iota(jnp.int32, (1, LANES), 1)

        @pl.when((ki == 0) & (qi == 0))
        def _():
            dq_ref[...] = jnp.zeros_like(dq_ref)

        @pl.when(qi == ki)
        def _():
            dk_acc[...] = jnp.zeros_like(dk_acc)
            dv_acc[...] = jnp.zeros_like(dv_acc)
            dc_acc[...] = jnp.zeros_like(dc_acc)

        def step(diag):
            q = q_ref[...] * _SCALE
            k = k_ref[...]
            v = v_ref[...]
            dov = do_ref[...]
            k_aug = jnp.concatenate([k, ck_ref[0]], axis=1)
            prod_t = (dov.astype(F32) * o_ref[...]).T
            lse_t = lse_ref[...].T
            dq_blk = jnp.zeros((t, LANES), F32)
            for hh in range(2):
                mk = masks[hh]
                rows = slice(hh * HEAD_DIM, (hh + 1) * HEAD_DIM)
                qh = jnp.where(mk, q, jnp.zeros_like(q))
                kh = jnp.where(mk, k, jnp.zeros_like(k))
                doh = jnp.where(mk, dov, jnp.zeros_like(dov))
                ones = jnp.where((lane >= 3 * hh) & (lane < 3 * hh + 3), 1.0, 0.0).astype(q.dtype)
                q_aug = jnp.concatenate([qh, jnp.broadcast_to(ones, q.shape)], axis=1)
                sc = lax.dot_general(k_aug, q_aug, _DIMS["nt"], preferred_element_type=F32)
                p = jnp.exp(sc - lse_t[hh * HEAD_DIM:hh * HEAD_DIM + 1, :])
                if diag:
                    r = lax.broadcasted_iota(jnp.int32, (t, t), 0)
                    cc = lax.broadcasted_iota(jnp.int32, (t, t), 1)
                    p = jnp.where(r <= cc, p, 0.0)
                dp = lax.dot_general(v, doh, _DIMS["nt"], preferred_element_type=F32)
                delta = jnp.sum(prod_t[rows], axis=0, keepdims=True)
                ds = p * (dp - delta)
                dsb = ds.astype(MXU_DTYPE)
                pb = p.astype(MXU_DTYPE)
                dv_acc[...] += lax.dot_general(pb, doh, _DIMS["nn"], preferred_element_type=F32)
                dk_acc[...] += lax.dot_general(dsb, qh, _DIMS["nn"], preferred_element_type=F32)
                dq_blk = dq_blk + lax.dot_general(dsb, kh, _DIMS["tn"], preferred_element_type=F32)
                dc_acc[hh] = dc_acc[hh] - jnp.sum(ds, axis=1, keepdims=True)
            rows_q = pl.ds(pl.multiple_of(qi * t, t), t)
            dq_ref[rows_q, :] = dq_ref[rows_q, :] + dq_blk * _SCALE

        @pl.when(qi > ki)
        def _():
            step(False)

        @pl.when(qi == ki)
        def _():
            step(True)

        @pl.when(qi == n - 1)
        def _():
            dk_ref[...] = dk_acc[...].astype(dk_ref.dtype)
            dv_ref[...] = dv_acc[...].astype(dv_ref.dtype)
            dc0_ref[0] = jnp.broadcast_to(dc_acc[0], (t, LANES)).T[0:8]
            dc1_ref[0] = jnp.broadcast_to(dc_acc[1], (t, LANES)).T[0:8]

    q_blk = lambda col: pl.BlockSpec((t, LANES), lambda p, i, kt, qt: (qt[i], col(p)))
    k_blk = lambda col: pl.BlockSpec((t, LANES), lambda p, i, kt, qt: (kt[i], col(p)))
    dc_blk = pl.BlockSpec((1, 8, t), lambda p, i, kt, qt: (p, 0, kt[i]))
    grid_spec = pltpu.PrefetchScalarGridSpec(
        num_scalar_prefetch=2, grid=(N_HEADS // 2, len(pairs)),
        in_specs=[q_blk(lambda p: qb + p), k_blk(lambda p: kb + p), k_blk(lambda p: vb + p),
                  pl.BlockSpec((1, t, LANES), lambda p, i, kt, qt: (p, kt[i], 0)),
                  q_blk(lambda p: p), q_blk(lambda p: p), q_blk(lambda p: p)],
        out_specs=[pl.BlockSpec((s, LANES), lambda p, i, kt, qt: (0, p)), k_blk(lambda p: p), k_blk(lambda p: p),
                   dc_blk, dc_blk],
        scratch_shapes=[pltpu.VMEM((t, LANES), F32), pltpu.VMEM((t, LANES), F32), pltpu.VMEM((2, t, 1), F32)])
    return pl.pallas_call(
        body, name=name, grid_spec=grid_spec,
        out_shape=[jax.ShapeDtypeStruct((s, D_ATT), F32), jax.ShapeDtypeStruct((s, D_ATT), BF16),
                   jax.ShapeDtypeStruct((s, D_ATT), BF16), jax.ShapeDtypeStruct((N_HEADS // 2, 8, s), F32),
                   jax.ShapeDtypeStruct((N_HEADS // 2, 8, s), F32)],
        compiler_params=_params(("parallel", "arbitrary")),
    )(ki_tab, qi_tab, h, h, h, ck, o, lse, do)


def _conv3(z, z_prev, w_ref):
    return (w_ref[2:3, :] * z + w_ref[1:2, :] * _shift_down(z, z_prev, 1)
            + w_ref[0:1, :] * _shift_down(z, z_prev, 2))


def _sconv_fwd(h, w, name):
    s = h.shape[0]
    t = _tile(s, 512)
    r = t // HALO
    c = D_CONV
    b_bg, b_cg, b_hc = OFF_BG // c, OFF_CG // c, OFF_HC // c

    def body(bg_ref, cg_ref, hc_ref, cgp_ref, hcp_ref, w_ref, y_ref):
        i = pl.program_id(0)
        live = (i > 0).astype(F32)
        z = cg_ref[...].astype(F32) * hc_ref[...].astype(F32)
        zp = cgp_ref[...].astype(F32) * hcp_ref[...].astype(F32) * live
        y_ref[...] = (bg_ref[...].astype(F32) * _conv3(z, zp, w_ref)).astype(y_ref.dtype)

    cur = lambda b: pl.BlockSpec((t, c), lambda i: (i, b))
    prev = lambda b: pl.BlockSpec((HALO, c), lambda i: (jnp.maximum(i * r - 1, 0), b))
    return pl.pallas_call(
        body, name=name, grid=(s // t,),
        in_specs=[cur(b_bg), cur(b_cg), cur(b_hc), prev(b_cg), prev(b_hc), pl.BlockSpec((8, c), lambda i: (0, 0))],
        out_specs=pl.BlockSpec((t, c), lambda i: (i, 0)),
        out_shape=jax.ShapeDtypeStruct((s, c), BF16),
        compiler_params=_params(("parallel",)),
    )(h, h, h, h, h, w)


def _sconv_bwd(h, w, dy, name):
    s = h.shape[0]
    t = _tile(s, 512)
    n = s // t
    r = t // HALO
    nh = s // HALO
    c = D_CONV
    b_bg, b_cg, b_hc = OFF_BG // c, OFF_CG // c, OFF_HC // c

    def body(bg_ref, cg_ref, hc_ref, cgp_ref, hcp_ref, bgn_ref, dy_ref, dyn_ref, w_ref, d_ref, dw_ref, acc_ref):
        i = pl.program_id(0)
        has_prev = (i > 0).astype(F32)
        has_next = (i < n - 1).astype(F32)
        bg = bg_ref[...].astype(F32)
        cg = cg_ref[...].astype(F32)
        hc = hc_ref[...].astype(F32)
        dyv = dy_ref[...].astype(F32)
        z = cg * hc
        zp = cgp_ref[...].astype(F32) * hcp_ref[...].astype(F32) * has_prev
        z1 = _shift_down(z, zp, 1)
        z2 = _shift_down(z, zp, 2)
        cz = w_ref[2:3, :] * z + w_ref[1:2, :] * z1 + w_ref[0:1, :] * z2
        dcz = dyv * bg
        dczn = dyn_ref[...].astype(F32) * bgn_ref[...].astype(F32) * has_next
        dz = (w_ref[2:3, :] * dcz + w_ref[1:2, :] * _shift_up(dcz, dczn, 1)
              + w_ref[0:1, :] * _shift_up(dcz, dczn, 2))
        d_ref[:, 0:c] = (dyv * cz).astype(d_ref.dtype)
        d_ref[:, c:2 * c] = (dz * hc).astype(d_ref.dtype)
        d_ref[:, 2 * c:3 * c] = (dz * cg).astype(d_ref.dtype)

        @pl.when(i == 0)
        def _():
            acc_ref[...] = jnp.zeros_like(acc_ref)

        acc_ref[0] += _row_sum8(dcz * z2)
        acc_ref[1] += _row_sum8(dcz * z1)
        acc_ref[2] += _row_sum8(dcz * z)

        @pl.when(i == n - 1)
        def _():
            rows = [jnp.sum(acc_ref[k], axis=0, keepdims=True) for k in range(3)]
            dw_ref[...] = jnp.concatenate(rows + [jnp.zeros((5, c), F32)], axis=0)

    cur = lambda b: pl.BlockSpec((t, c), lambda i: (i, b))
    prev = lambda b: pl.BlockSpec((HALO, c), lambda i: (jnp.maximum(i * r - 1, 0), b))
    nxt = lambda b: pl.BlockSpec((HALO, c), lambda i: (jnp.minimum((i + 1) * r, nh - 1), b))
    return pl.pallas_call(
        body, name=name, grid=(n,),
        in_specs=[cur(b_bg), cur(b_cg), cur(b_hc), prev(b_cg), prev(b_hc), nxt(b_bg),
                  cur(0), nxt(0), pl.BlockSpec((8, c), lambda i: (0, 0))],
        out_specs=[pl.BlockSpec((t, 3 * c), lambda i: (i, 0)), pl.BlockSpec((8, c), lambda i: (0, 0))],
        out_shape=[jax.ShapeDtypeStruct((s, 3 * c), BF16), jax.ShapeDtypeStruct((8, c), F32)],
        scratch_shapes=[pltpu.VMEM((3, 8, c), F32)],
        compiler_params=_params(("arbitrary",)),
    )(h, h, h, h, h, h, dy, dy, w)


def _group_masks():
    lane = lax.broadcasted_iota(jnp.int32, (1, D_SGU), 1)
    return [(lane >= g * HEAD_DIM) & (lane < (g + 1) * HEAD_DIM) for g in range(N_GROUPS)]


def _tril_weights(w_ref):
    r = lax.broadcasted_iota(jnp.int32, (CHUNK, CHUNK), 0)
    c = lax.broadcasted_iota(jnp.int32, (CHUNK, CHUNK), 1)
    return [jnp.where(r >= c, w_ref[g], 0.0).astype(MXU_DTYPE) for g in range(N_GROUPS)]


def _sgu_ln(vs, g_ref, b_ref):
    vg, dvg = _gelu_and_grad(vs)
    mu = jnp.mean(vg, axis=-1, keepdims=True)
    xc = vg - mu
    rstd = lax.rsqrt(jnp.mean(xc * xc, axis=-1, keepdims=True) + LN_EPS)
    xhat = xc * rstd
    return xhat * g_ref[...] + b_ref[...], xhat, rstd, dvg


def _sgu_fwd(h, ln_g, ln_b, w_s, bias, name):
    s = h.shape[0]
    t = _tile(s, 512)
    c = D_SGU
    b_u, b_v = OFF_U // c, OFF_VS // c

    def body(u_ref, v_ref, g_ref, b_ref, w_ref, bias_ref, y_ref):
        gm = _group_masks()
        wm = _tril_weights(w_ref)
        ug = _gelu(u_ref[...].astype(F32))
        vn, _, _, _ = _sgu_ln(v_ref[...].astype(F32), g_ref, b_ref)
        vnb = vn.astype(MXU_DTYPE)
        for ch in range(t // CHUNK):
            rows = slice(ch * CHUNK, (ch + 1) * CHUNK)
            mixed = bias_ref[...]
            for g in range(N_GROUPS):
                mg = lax.dot_general(wm[g], vnb[rows], _DIMS["nn"], preferred_element_type=F32)
                mixed = jnp.where(gm[g], mixed + mg, mixed)
            y_ref[rows, :] = (ug[rows] * mixed).astype(y_ref.dtype)

    full = lambda shp: pl.BlockSpec(shp, lambda i: (0,) * len(shp))
    return pl.pallas_call(
        body, name=name, grid=(s // t,),
        in_specs=[pl.BlockSpec((t, c), lambda i: (i, b_u)), pl.BlockSpec((t, c), lambda i: (i, b_v)),
                  full((1, c)), full((1, c)), full((N_GROUPS, CHUNK, CHUNK)), full((CHUNK, c))],
        out_specs=pl.BlockSpec((t, c), lambda i: (i, 0)),
        out_shape=jax.ShapeDtypeStruct((s, c), BF16),
        compiler_params=_params(("parallel",)),
    )(h, h, ln_g, ln_b, w_s, bias)


def _sgu_bwd(h, ln_g, ln_b, w_s, bias, dy, name):
    s = h.shape[0]
    t = _tile(s, 512)
    n = s // t
    c = D_SGU
    b_u, b_v = OFF_U // c, OFF_VS // c

    def body(u_ref, v_ref, g_ref, b_ref, w_ref, bias_ref, dy_ref,
             d_ref, dg_ref, db_ref, dw_ref, dbias_ref, dg_acc, db_acc):
        i = pl.program_id(0)
        gm = _group_masks()
        wm = _tril_weights(w_ref)

        @pl.when(i == 0)
        def _():
            dg_acc[...] = jnp.zeros_like(dg_acc)
            db_acc[...] = jnp.zeros_like(db_acc)
            dw_ref[...] = jnp.zeros_like(dw_ref)
            dbias_ref[...] = jnp.zeros_like(dbias_ref)

        ug, dug = _gelu_and_grad(u_ref[...].astype(F32))
        vn, xhat, rstd, dvg = _sgu_ln(v_ref[...].astype(F32), g_ref, b_ref)
        vnb = vn.astype(MXU_DTYPE)
        dyv = dy_ref[...].astype(F32)
        dmixed = dyv * ug
        dmb = dmixed.astype(MXU_DTYPE)
        dvn_parts = []
        for ch in range(t // CHUNK):
            rows = slice(ch * CHUNK, (ch + 1) * CHUNK)
            mixed = bias_ref[...]
            dvn = jnp.zeros((CHUNK, c), F32)
            for g in range(N_GROUPS):
                mg = lax.dot_general(wm[g], vnb[rows], _DIMS["nn"], preferred_element_type=F32)
                mixed = jnp.where(gm[g], mixed + mg, mixed)
                dvn = jnp.where(gm[g], lax.dot_general(wm[g], dmb[rows], _DIMS["tn"], preferred_element_type=F32),
                                dvn)
                dmg = jnp.where(gm[g], dmb[rows], jnp.zeros_like(dmb[rows]))
                dw_ref[g] += lax.dot_general(dmg, vnb[rows], _DIMS["nt"], preferred_element_type=F32)
            d_ref[rows, 0:c] = (dyv[rows] * mixed * dug[rows]).astype(d_ref.dtype)
            dbias_ref[...] += dmixed[rows]
            dvn_parts.append(dvn)
        dvn = jnp.concatenate(dvn_parts, axis=0)
        dg_acc[...] += _row_sum8(dvn * xhat)
        db_acc[...] += _row_sum8(dvn)
        dxh = dvn * g_ref[...]
        dvgl = rstd * (dxh - jnp.mean(dxh, axis=-1, keepdims=True)
                       - xhat * jnp.mean(dxh * xhat, axis=-1, keepdims=True))
        d_ref[:, c:2 * c] = (dvgl * dvg).astype(d_ref.dtype)

        @pl.when(i == n - 1)
        def _():
            dg_ref[...] = jnp.sum(dg_acc[...], axis=0, keepdims=True)
            db_ref[...] = jnp.sum(db_acc[...], axis=0, keepdims=True)
            r = lax.broadcasted_iota(jnp.int32, (CHUNK, CHUNK), 0)
            cc = lax.broadcasted_iota(jnp.int32, (CHUNK, CHUNK), 1)
            for g in range(N_GROUPS):
                dw_ref[g] = jnp.where(r >= cc, dw_ref[g], 0.0)

    full = lambda shp: pl.BlockSpec(shp, lambda i: (0,) * len(shp))
    return pl.pallas_call(
        body, name=name, grid=(n,),
        in_specs=[pl.BlockSpec((t, c), lambda i: (i, b_u)), pl.BlockSpec((t, c), lambda i: (i, b_v)),
                  full((1, c)), full((1, c)), full((N_GROUPS, CHUNK, CHUNK)), full((CHUNK, c)),
                  pl.BlockSpec((t, c), lambda i: (i, 0))],
        out_specs=[pl.BlockSpec((t, 2 * c), lambda i: (i, 0)), full((1, c)), full((1, c)),
                   full((N_GROUPS, CHUNK, CHUNK)), full((CHUNK, c))],
        out_shape=[jax.ShapeDtypeStruct((s, 2 * c), BF16), jax.ShapeDtypeStruct((1, c), F32),
                   jax.ShapeDtypeStruct((1, c), F32), jax.ShapeDtypeStruct((N_GROUPS, CHUNK, CHUNK), F32),
                   jax.ShapeDtypeStruct((CHUNK, c), F32)],
        scratch_shapes=[pltpu.VMEM((8, c), F32), pltpu.VMEM((8, c), F32)],
        compiler_params=_params(("arbitrary",)),
    )(h, h, ln_g, ln_b, w_s, bias, dy)


def _merge_fwd(h, acts, ws, b_gate, name):
    s = h.shape[0]
    d = D_MODEL
    t = _tile(s, 512)

    def body(gl0, gl1, gl2, a0, a1, a2, w0, w1, w2, b_ref, o_ref):
        acc = jnp.zeros((t, d), F32)
        for i, (gl, a, w) in enumerate(((gl0, a0, w0), (gl1, a1, w1), (gl2, a2, w2))):
            y = lax.dot_general(a[...], w[...], _DIMS["nn"], preferred_element_type=F32)
            acc = acc + _sigmoid(gl[...].astype(F32) + b_ref[i:i + 1, :]) * y
        o_ref[...] = acc.astype(o_ref.dtype)

    full = lambda arr: pl.BlockSpec(arr.shape, lambda i: (0, 0))
    return pl.pallas_call(
        body, name=name, grid=(s // t,),
        in_specs=[pl.BlockSpec((t, d), lambda i, b=b: (i, b)) for b in range(3)]
                 + [pl.BlockSpec((t, a.shape[1]), lambda i: (i, 0)) for a in acts]
                 + [full(w) for w in ws] + [full(b_gate)],
        out_specs=pl.BlockSpec((t, d), lambda i: (i, 0)),
        out_shape=jax.ShapeDtypeStruct((s, d), BF16),
        compiler_params=_params(("parallel",)),
    )(h, h, h, *acts, *ws, b_gate)


def _merge_bwd(h, acts, ws, b_gate, dmerged, name):
    s = h.shape[0]
    d = D_MODEL
    t = _tile(s, 512)
    n = s // t

    def body(gl0, gl1, gl2, a0, a1, a2, w0, w1, w2, b_ref, dm_ref, dy0, dy1, dy2, dgl_ref, db_ref, acc_ref):
        step = pl.program_id(0)

        @pl.when(step == 0)
        def _():
            acc_ref[...] = jnp.zeros_like(acc_ref)

        dm = dm_ref[...]
        for i, (gl, a, w, dy) in enumerate(((gl0, a0, w0, dy0), (gl1, a1, w1, dy1), (gl2, a2, w2, dy2))):
            y = lax.dot_general(a[...], w[...], _DIMS["nn"], preferred_element_type=F32)
            gate = _sigmoid(gl[...].astype(F32) + b_ref[i:i + 1, :])
            dy[...] = (dm * gate).astype(dy.dtype)
            dgl = dm * y * (gate * (1.0 - gate))
            dgl_ref[:, i * d:(i + 1) * d] = dgl.astype(dgl_ref.dtype)
            acc_ref[i] += _row_sum8(dgl)

        @pl.when(step == n - 1)
        def _():
            rows = [jnp.sum(acc_ref[k], axis=0, keepdims=True) for k in range(3)]
            db_ref[...] = jnp.concatenate(rows + [jnp.zeros((5, d), F32)], axis=0)

    full = lambda arr: pl.BlockSpec(arr.shape, lambda i: (0, 0))
    row = pl.BlockSpec((t, d), lambda i: (i, 0))
    return pl.pallas_call(
        body, name=name, grid=(n,),
        in_specs=[pl.BlockSpec((t, d), lambda i, b=b: (i, b)) for b in range(3)]
                 + [pl.BlockSpec((t, a.shape[1]), lambda i: (i, 0)) for a in acts]
                 + [full(w) for w in ws] + [full(b_gate), row],
        out_specs=[row, row, row, pl.BlockSpec((t, 3 * d), lambda i: (i, 0)), pl.BlockSpec((8, d), lambda i: (0, 0))],
        out_shape=[jax.ShapeDtypeStruct((s, d), BF16)] * 3
                  + [jax.ShapeDtypeStruct((s, 3 * d), BF16), jax.ShapeDtypeStruct((8, d), F32)],
        scratch_shapes=[pltpu.VMEM((3, 8, d), F32)],
        compiler_params=_params(("arbitrary",)),
    )(h, h, h, *acts, *ws, b_gate, dmerged)


FF_BLK = D_FF // 2


def _ffn_act_fwd(h2, w, name):
    s = h2.shape[0]
    t = _tile(s, 512)
    r = t // HALO
    cw = 2 * FF_BLK

    def body(x_ref, xp_ref, w_ref, p_ref):
        i = pl.program_id(0)
        live = (i > 0).astype(F32)
        hc = _conv3(x_ref[...].astype(F32), xp_ref[...].astype(F32) * live, w_ref)
        p_ref[...] = (_gelu(hc[:, :FF_BLK]) * hc[:, FF_BLK:]).astype(p_ref.dtype)

    return pl.pallas_call(
        body, name=name, grid=(s // t, 2),
        in_specs=[pl.BlockSpec((t, cw), lambda i, j: (i, j)),
                  pl.BlockSpec((HALO, cw), lambda i, j: (jnp.maximum(i * r - 1, 0), j)),
                  pl.BlockSpec((8, cw), lambda i, j: (0, j))],
        out_specs=pl.BlockSpec((t, FF_BLK), lambda i, j: (i, j)),
        out_shape=jax.ShapeDtypeStruct((s, D_FF), BF16),
        compiler_params=_params(("parallel", "parallel")),
    )(h2, h2, w)


def _ffn_act_conv_bwd(h2, w, dp, name):
    s = h2.shape[0]
    t = _tile(s, 512)
    n = s // t
    r = t // HALO
    nh = s // HALO
    cw = 2 * FF_BLK

    def body(x_ref, xp_ref, xn_ref, dp_ref, dpn_ref, w_ref, dx_ref, dw_ref, acc_ref):
        i = pl.program_id(1)
        has_prev = (i > 0).astype(F32)
        has_next = (i < n - 1).astype(F32)
        x = jnp.concatenate([x_ref[...].astype(F32), xn_ref[...].astype(F32)], axis=0)
        xp = xp_ref[...].astype(F32) * has_prev
        x1 = _shift_down(x, xp, 1)
        x2 = _shift_down(x, xp, 2)
        hc = w_ref[2:3, :] * x + w_ref[1:2, :] * x1 + w_ref[0:1, :] * x2
        ga, dga = _gelu_and_grad(hc[:, :FF_BLK])
        dpv = jnp.concatenate([dp_ref[...].astype(F32), dpn_ref[...].astype(F32) * has_next], axis=0)
        dhc = jnp.concatenate([dpv * hc[:, FF_BLK:] * dga, dpv * ga], axis=1)
        cur, nxt = dhc[:t], dhc[t:]
        dx = w_ref[2:3, :] * cur + w_ref[1:2, :] * _shift_up(cur, nxt, 1) + w_ref[0:1, :] * _shift_up(cur, nxt, 2)
        dx_ref[...] = dx.astype(dx_ref.dtype)

        @pl.when(i == 0)
        def _():
            acc_ref[...] = jnp.zeros_like(acc_ref)

        acc_ref[0] += _row_sum8(cur * x2[:t])
        acc_ref[1] += _row_sum8(cur * x1[:t])
        acc_ref[2] += _row_sum8(cur * x[:t])

        @pl.when(i == n - 1)
        def _():
            rows = [jnp.sum(acc_ref[k], axis=0, keepdims=True) for k in range(3)]
            dw_ref[...] = jnp.concatenate(rows + [jnp.zeros((5, cw), F32)], axis=0)

    nxt_row = lambda j, i: jnp.minimum((i + 1) * r, nh - 1)
    return pl.pallas_call(
        body, name=name, grid=(2, n),
        in_specs=[pl.BlockSpec((t, cw), lambda j, i: (i, j)),
                  pl.BlockSpec((HALO, cw), lambda j, i: (jnp.maximum(i * r - 1, 0), j)),
                  pl.BlockSpec((HALO, cw), lambda j, i: (nxt_row(j, i), j)),
                  pl.BlockSpec((t, FF_BLK), lambda j, i: (i, j)),
                  pl.BlockSpec((HALO, FF_BLK), lambda j, i: (nxt_row(j, i), j)),
                  pl.BlockSpec((8, cw), lambda j, i: (0, j))],
        out_specs=[pl.BlockSpec((t, cw), lambda j, i: (i, j)), pl.BlockSpec((8, cw), lambda j, i: (0, j))],
        out_shape=[jax.ShapeDtypeStruct((s, 2 * D_FF), BF16), jax.ShapeDtypeStruct((8, 2 * D_FF), F32)],
        scratch_shapes=[pltpu.VMEM((3, 8, cw), F32)],
        compiler_params=_params(("parallel", "arbitrary")),
    )(h2, h2, h2, dp, dp, w)


def _adamw(w, g, m, v, name):
    shape = w.shape
    c = shape[-1]
    rows = math.prod(shape[:-1])
    to2d = lambda a: a.reshape(rows, c)
    cap = max(8, (1 << 18) // c)
    tr = rows
    for cand in (2048, 1024, 512, 256, 128, 64, 32, 16, 8):
        if cand <= cap and rows % cand == 0:
            tr = cand
            break

    def body(w_ref, g_ref, m_ref, v_ref, d_ref, nm_ref, nv_ref):
        gv = g_ref[...]
        nm = ADAM_B1 * m_ref[...] + (1.0 - ADAM_B1) * gv
        nv = ADAM_B2 * v_ref[...] + (1.0 - ADAM_B2) * (gv * gv)
        m_hat = nm / (1.0 - ADAM_B1 ** ADAM_STEP)
        v_hat = nv / (1.0 - ADAM_B2 ** ADAM_STEP)
        d_ref[...] = -ADAM_LR * (m_hat / (jnp.sqrt(v_hat) + ADAM_EPS) + ADAM_WD * w_ref[...])
        nm_ref[...] = nm
        nv_ref[...] = nv

    blk = pl.BlockSpec((tr, c), lambda i: (i, 0))
    outs = pl.pallas_call(
        body, name=name, grid=(rows // tr,),
        in_specs=[blk] * 4, out_specs=[blk] * 3,
        out_shape=[jax.ShapeDtypeStruct((rows, c), F32)] * 3,
        compiler_params=_params(("parallel",)),
    )(to2d(w), to2d(g), to2d(m), to2d(v))
    return tuple(o.reshape(shape) for o in outs)


_ANY = pl.BlockSpec(memory_space=pl.ANY)


def _place():
    x, y, c = lax.axis_index("x"), lax.axis_index("y"), lax.axis_index("c")
    others = [(1 - x, y), (x, 1 - y), (1 - x, 1 - y)]
    return x, y, c, others


def _all_gather_chips(shard, name):
    rws, cols = shard.shape
    half = rws // 2

    def body(x_ref, out_ref, send_sems, recv_sems, local_sem):
        x, y, c, others = _place()
        me = 2 * x + y
        sib = (x, y, 1 - c)

        def rows(chip, cc):
            return out_ref.at[chip, pl.ds(pl.multiple_of(cc * half, 16), half), :]

        def copy(k, src, dst, to):
            return pltpu.make_async_remote_copy(src_ref=src, dst_ref=dst, send_sem=send_sems.at[k],
                                                recv_sem=recv_sems.at[k], device_id=to, device_id_type=MESH)

        mine = pltpu.make_async_copy(x_ref, out_ref.at[me], local_sem)
        mine.start()
        my_half = x_ref.at[pl.ds(pl.multiple_of(c * half, 16), half), :]
        first = [copy(j, my_half, rows(me, c), (ox, oy, c)) for j, (ox, oy) in enumerate(others)]
        for cp in first:
            cp.start()
        passed = []
        for j, (ox, oy) in enumerate(others):
            blk = rows(2 * ox + oy, c)
            copy(j, blk, blk, (x, y, c)).wait_recv()
            fwd = copy(3 + j, blk, blk, sib)
            fwd.start()
            passed.append(fwd)
        for j, (ox, oy) in enumerate(others):
            blk = rows(2 * ox + oy, 1 - c)
            copy(3 + j, blk, blk, (x, y, c)).wait_recv()
        for cp in first + passed:
            cp.wait_send()
        mine.wait()

    return pl.pallas_call(
        body, name=name,
        in_specs=[_ANY], out_specs=_ANY,
        out_shape=jax.ShapeDtypeStruct((N_CHIPS, rws, cols), shard.dtype),
        scratch_shapes=[pltpu.SemaphoreType.DMA((6,)), pltpu.SemaphoreType.DMA((6,)), pltpu.SemaphoreType.DMA],
        compiler_params=pltpu.CompilerParams(has_side_effects=True),
    )(shard)


def _swap_halves(buf, name):
    nb, rws, cols = buf.shape
    half = rws // 2

    def body(b_ref, own_ref, sib_ref, send_sem, recv_sem, local_sem):
        x, y, c, _ = _place()
        keep = b_ref.at[:, pl.ds(pl.multiple_of(c * half, 16), half), :]
        give = b_ref.at[:, pl.ds(pl.multiple_of((1 - c) * half, 16), half), :]
        mine = pltpu.make_async_copy(keep, own_ref, local_sem)
        mine.start()
        cp = pltpu.make_async_remote_copy(src_ref=give, dst_ref=sib_ref, send_sem=send_sem, recv_sem=recv_sem,
                                          device_id=(x, y, 1 - c), device_id_type=MESH)
        cp.start()
        cp.wait()
        mine.wait()

    shp = jax.ShapeDtypeStruct((nb, half, cols), buf.dtype)
    return pl.pallas_call(
        body, name=name,
        in_specs=[_ANY], out_specs=[_ANY, _ANY], out_shape=[shp, shp],
        scratch_shapes=[pltpu.SemaphoreType.DMA, pltpu.SemaphoreType.DMA, pltpu.SemaphoreType.DMA],
        compiler_params=pltpu.CompilerParams(has_side_effects=True),
    )(buf)


def _add2(a, b, name):
    nb, rws, cols = a.shape
    t = _tile(rws, 256)
    if rws % t:
        t = rws

    def body(a_ref, b_ref, o_ref):
        o_ref[...] = (a_ref[...].astype(F32) + b_ref[...].astype(F32)).astype(o_ref.dtype)

    blk = pl.BlockSpec((1, t, cols), lambda i, j: (i, j, 0))
    return pl.pallas_call(
        body, name=name, grid=(nb, rws // t), in_specs=[blk, blk], out_specs=blk,
        out_shape=jax.ShapeDtypeStruct(a.shape, a.dtype),
        compiler_params=_params(("parallel", "parallel")),
    )(a, b)


def _exchange_chips(pre, name):
    nb, half, cols = pre.shape

    def body(p_ref, out_ref, send_sems, recv_sems, local_sem):
        x, y, c, others = _place()
        me = 2 * x + y
        mine = pltpu.make_async_copy(p_ref.at[me], out_ref.at[me], local_sem)
        mine.start()
        sends = []
        for j, (ox, oy) in enumerate(others):
            cp = pltpu.make_async_remote_copy(src_ref=p_ref.at[2 * ox + oy], dst_ref=out_ref.at[me],
                                              send_sem=send_sems.at[j], recv_sem=recv_sems.at[j],
                                              device_id=(ox, oy, c), device_id_type=MESH)
            cp.start()
            sends.append(cp)
        for j, (ox, oy) in enumerate(others):
            blk = out_ref.at[2 * ox + oy]
            pltpu.make_async_remote_copy(src_ref=blk, dst_ref=blk, send_sem=send_sems.at[j],
                                         recv_sem=recv_sems.at[j], device_id=(x, y, c),
                                         device_id_type=MESH).wait_recv()
        for cp in sends:
            cp.wait_send()
        mine.wait()

    return pl.pallas_call(
        body, name=name,
        in_specs=[_ANY], out_specs=_ANY, out_shape=jax.ShapeDtypeStruct(pre.shape, pre.dtype),
        scratch_shapes=[pltpu.SemaphoreType.DMA((3,)), pltpu.SemaphoreType.DMA((3,)), pltpu.SemaphoreType.DMA],
        compiler_params=pltpu.CompilerParams(has_side_effects=True),
    )(pre)


def _add4(parts, name):
    nb, half, cols = parts.shape
    t = _tile(half, 256)
    if half % t:
        t = half

    def body(p_ref, o_ref):
        acc = p_ref[0].astype(F32)
        for k in range(1, nb):
            acc = acc + p_ref[k].astype(F32)
        o_ref[...] = acc

    return pl.pallas_call(
        body, name=name, grid=(half // t,),
        in_specs=[pl.BlockSpec((nb, t, cols), lambda i: (0, i, 0))],
        out_specs=pl.BlockSpec((t, cols), lambda i: (i, 0)),
        out_shape=jax.ShapeDtypeStruct((half, cols), F32),
        compiler_params=_params(("parallel",)),
    )(parts)


def _join_halves(mine_half, name):
    half, cols = mine_half.shape

    def body(h_ref, out_ref, send_sem, recv_sem, local_sem):
        x, y, c, _ = _place()
        dst = out_ref.at[pl.ds(pl.multiple_of(c * half, 8), half), :]
        mine = pltpu.make_async_copy(h_ref, dst, local_sem)
        mine.start()
        cp = pltpu.make_async_remote_copy(src_ref=h_ref, dst_ref=dst, send_sem=send_sem, recv_sem=recv_sem,
                                          device_id=(x, y, 1 - c), device_id_type=MESH)
        cp.start()
        cp.wait()
        mine.wait()

    return pl.pallas_call(
        body, name=name,
        in_specs=[_ANY], out_specs=_ANY, out_shape=jax.ShapeDtypeStruct((2 * half, cols), mine_half.dtype),
        scratch_shapes=[pltpu.SemaphoreType.DMA, pltpu.SemaphoreType.DMA, pltpu.SemaphoreType.DMA],
        compiler_params=pltpu.CompilerParams(has_side_effects=True),
    )(mine_half)


def _reduce_scatter_chips(buf, tag):
    own, sib = _swap_halves(buf, "rs_swap_" + tag)
    pre = _add2(own, sib, "rs_add2_" + tag)
    parts = _exchange_chips(pre, "rs_xchg_" + tag)
    red = _add4(parts, "rs_add4_" + tag)
    return _join_halves(red, "rs_join_" + tag)


MAX_DMA_BYTES = 2 * 1024 * 1024
ROW_ALIGN = 16


def _pieces(rows, row_bytes):
    n = max(1, -(-(rows * row_bytes) // MAX_DMA_BYTES))
    step = -(-(-(-rows // n)) // ROW_ALIGN) * ROW_ALIGN
    return [(r, min(step, rows - r)) for r in range(0, rows, step)]


def _half_plan(arrays, row_axis):
    plan = []
    for a, arr in enumerate(arrays):
        row_bytes = math.prod(arr.shape[row_axis + 1:]) * arr.dtype.itemsize * (arr.shape[0] if row_axis else 1)
        plan += [(a, r0, nr) for r0, nr in _pieces(arr.shape[row_axis] // 2, row_bytes)]
    return plan


def _rows(start, size):
    return pl.ds(pl.multiple_of(start, ROW_ALIGN), size)


def _remote(src, dst, send_sems, recv_sems, k, to):
    return pltpu.make_async_remote_copy(src_ref=src, dst_ref=dst, send_sem=send_sems.at[k], recv_sem=recv_sems.at[k],
                                        device_id=to, device_id_type=MESH)


def _comm_call(body, name, ins, out_shapes, n_remote, n_local, aliases=None):
    return pl.pallas_call(
        body, name=name,
        in_specs=[_ANY] * len(ins), out_specs=[_ANY] * len(out_shapes), out_shape=out_shapes,
        scratch_shapes=[pltpu.SemaphoreType.DMA((n_remote,)), pltpu.SemaphoreType.DMA((n_remote,)),
                        pltpu.SemaphoreType.DMA((max(n_local, 1),))],
        input_output_aliases=aliases or {},
        compiler_params=pltpu.CompilerParams(has_side_effects=True),
    )(*ins)


def _cast_shard(w, l, me_idx, name):
    _, k, cols = w.shape
    tr = _tile(k, 256)
    if k % tr:
        tr = k

    def body(me_ref, w_ref, s_ref, land_ref):
        del me_ref
        v = w_ref[...].astype(BF16)
        s_ref[...] = v
        land_ref[...] = v

    grid_spec = pltpu.PrefetchScalarGridSpec(
        num_scalar_prefetch=1, grid=(k // tr,),
        in_specs=[pl.BlockSpec((None, tr, cols), lambda i, me: (l, i, 0))],
        out_specs=[pl.BlockSpec((tr, cols), lambda i, me: (i, 0)),
                   pl.BlockSpec((None, tr, cols), lambda i, me: (me[0], i, 0))])
    return pl.pallas_call(
        body, name=name, grid_spec=grid_spec,
        out_shape=[jax.ShapeDtypeStruct((k, cols), BF16), jax.ShapeDtypeStruct((N_CHIPS, k, cols), BF16)],
        compiler_params=_params(("parallel",)),
    )(me_idx, w)


def _gather_d2d(lands, name):
    n = len(lands)
    plan = _half_plan(lands, 1)
    plan = [(a, r0, nr) for a, r0, nr in plan]

    def body(*refs):
        out_refs = refs[n:2 * n]
        send_sems, recv_sems, _ = refs[2 * n:]
        x, y, c, others = _place()
        sends = []
        for i, (a, r0, nr) in enumerate(plan):
            rows = _rows(c * (lands[a].shape[1] // 2) + r0, nr)
            for j, (ox, oy) in enumerate(others):
                blk = out_refs[a].at[2 * ox + oy, rows, :]
                cp = _remote(blk, blk, send_sems, recv_sems, 3 * i + j, (x, y, 1 - c))
                cp.start()
                sends.append(cp)
        for i, (a, r0, nr) in enumerate(plan):
            rows = _rows((1 - c) * (lands[a].shape[1] // 2) + r0, nr)
            for j, (ox, oy) in enumerate(others):
                blk = out_refs[a].at[2 * ox + oy, rows, :]
                _remote(blk, blk, send_sems, recv_sems, 3 * i + j, (x, y, c)).wait_recv()
        for cp in sends:
            cp.wait_send()

    outs = [jax.ShapeDtypeStruct(a.shape, a.dtype) for a in lands]
    return _comm_call(body, name, lands, outs, 3 * len(plan), 0, aliases={a: a for a in range(n)})


def _rs_swap(ts, name):
    n = len(ts)
    plan = _half_plan(ts, 1)

    def body(*refs):
        t_refs, out_refs = refs[:n], refs[n:2 * n]
        send_sems, recv_sems, _ = refs[2 * n:]
        x, y, c, _o = _place()
        sends = []
        for i, (a, r0, nr) in enumerate(plan):
            src = t_refs[a].at[:, _rows((1 - c) * (ts[a].shape[1] // 2) + r0, nr), :]
            cp = _remote(src, out_refs[a].at[:, pl.ds(r0, nr), :], send_sems, recv_sems, i, (x, y, 1 - c))
            cp.start()
            sends.append(cp)
        for i, (a, r0, nr) in enumerate(plan):
            blk = out_refs[a].at[:, pl.ds(r0, nr), :]
            _remote(blk, blk, send_sems, recv_sems, i, (x, y, c)).wait_recv()
        for cp in sends:
            cp.wait_send()

    outs = [jax.ShapeDtypeStruct((t.shape[0], t.shape[1] // 2, t.shape[2]), t.dtype) for t in ts]
    return _comm_call(body, name, ts, outs, len(plan), 0)


def _add_half(t, got, c_idx, me_idx, name):
    nb, k, cols = t.shape
    half = k // 2

    def body(c_ref, me_ref, t_ref, g_ref, o_ref, mine_ref):
        del c_ref
        v = (t_ref[...].astype(F32) + g_ref[...].astype(F32)).astype(o_ref.dtype)
        o_ref[...] = v

        @pl.when(pl.program_id(0) == me_ref[0])
        def _():
            mine_ref[...] = v

    blk = pl.BlockSpec((1, half, cols), lambda i, c, me: (i, 0, 0))
    grid_spec = pltpu.PrefetchScalarGridSpec(
        num_scalar_prefetch=2, grid=(nb,),
        in_specs=[pl.BlockSpec((1, half, cols), lambda i, c, me: (i, c[0], 0)), blk],
        out_specs=[blk, pl.BlockSpec((1, half, cols), lambda i, c, me: (me[0], 0, 0))])
    shp = jax.ShapeDtypeStruct(got.shape, got.dtype)
    return pl.pallas_call(
        body, name=name, grid_spec=grid_spec, out_shape=[shp, shp],
        compiler_params=_params(("arbitrary",)),
    )(c_idx, me_idx, t, got)


def _add4_half(parts, c_idx, name):
    nb, half, cols = parts.shape
    t = _tile(half, 256)
    if half % t:
        t = half
    steps = half // t

    def body(c_ref, p_ref, o_ref):
        del c_ref
        acc = p_ref[0].astype(F32)
        for k in range(1, nb):
            acc = acc + p_ref[k].astype(F32)
        o_ref[...] = acc

    grid_spec = pltpu.PrefetchScalarGridSpec(
        num_scalar_prefetch=1, grid=(steps,),
        in_specs=[pl.BlockSpec((nb, t, cols), lambda i, c: (0, i, 0))],
        out_specs=pl.BlockSpec((t, cols), lambda i, c: (c[0] * steps + i, 0)))
    return pl.pallas_call(
        body, name=name, grid_spec=grid_spec, out_shape=jax.ShapeDtypeStruct((2 * half, cols), F32),
        compiler_params=_params(("parallel",)),
    )(c_idx, parts)


def _rs_join(fulls, name):
    n = len(fulls)
    plan = _half_plan(fulls, 0)

    def body(*refs):
        out_refs = refs[n:2 * n]
        send_sems, recv_sems, _ = refs[2 * n:]
        x, y, c, _o = _place()
        sends = []
        for i, (a, r0, nr) in enumerate(plan):
            blk = out_refs[a].at[_rows(c * (fulls[a].shape[0] // 2) + r0, nr), :]
            cp = _remote(blk, blk, send_sems, recv_sems, i, (x, y, 1 - c))
            cp.start()
            sends.append(cp)
        for i, (a, r0, nr) in enumerate(plan):
            blk = out_refs[a].at[_rows((1 - c) * (fulls[a].shape[0] // 2) + r0, nr), :]
            _remote(blk, blk, send_sems, recv_sems, i, (x, y, c)).wait_recv()
        for cp in sends:
            cp.wait_send()

    outs = [jax.ShapeDtypeStruct(f.shape, f.dtype) for f in fulls]
    return _comm_call(body, name, fulls, outs, len(plan), 0, aliases={a: a for a in range(n)})


_HBM = pl.BlockSpec(memory_space=pltpu.HBM)
_SEM = pl.BlockSpec(memory_space=pltpu.SEMAPHORE)
_EFFECT = pltpu.SideEffectType.DATAFLOW_SIDE_EFFECTING


def _ici_plan(kind, a_list):
    if kind == "gather":
        return _half_plan(a_list, 0)
    plan = []
    for a, p in enumerate(a_list):
        plan += [(a, r0, nr) for r0, nr in _pieces(p.shape[1], p.shape[2] * p.dtype.itemsize)]
    return plan


def _ici_refs(kind, a_ref, b_ref, a_shape, r0, nr, c, me, peer):
    if kind == "gather":
        rows = _rows(c * (a_shape[0] // 2) + r0, nr)
        return a_ref.at[rows, :], b_ref.at[me, rows, :], b_ref.at[peer, rows, :]
    rows = pl.ds(r0, nr)
    return a_ref.at[peer, rows, :], b_ref.at[me, rows, :], b_ref.at[peer, rows, :]


def _ici_start(kind, a_list, b_list, name):
    n = len(a_list)
    plan = _ici_plan(kind, a_list)
    shapes = [a.shape for a in a_list]

    def body(*refs):
        a_refs, b_refs = refs[:n], refs[n:2 * n]
        send_sems, recv_sems = refs[2 * n], refs[2 * n + 1]
        token = refs[4 * n + 2]
        x, y, c, others = _place()
        me = 2 * x + y
        for i, (a, r0, nr) in enumerate(plan):
            for j, (ox, oy) in enumerate(others):
                src, dst, _ = _ici_refs(kind, a_refs[a], b_refs[a], shapes[a], r0, nr, c, me, 2 * ox + oy)
                _remote(src, dst, send_sems, recv_sems, 3 * i + j, (ox, oy, c)).start()
        token[...] = jnp.zeros_like(token)

    hbm = lambda v: pltpu.HBM(v.shape, v.dtype)
    ncp = 3 * len(plan)
    outs = pl.pallas_call(
        body, name=name,
        in_specs=[_HBM] * (2 * n),
        out_specs=[_SEM, _SEM] + [_HBM] * (2 * n) + [pl.BlockSpec(memory_space=pltpu.VMEM)],
        out_shape=[pltpu.SemaphoreType.DMA((ncp,)), pltpu.SemaphoreType.DMA((ncp,))]
                  + [hbm(v) for v in a_list] + [hbm(v) for v in b_list] + [jax.ShapeDtypeStruct((8, LANES), F32)],
        input_output_aliases={i: 2 + i for i in range(2 * n)},
        compiler_params=pltpu.CompilerParams(has_side_effects=_EFFECT),
    )(*[pltpu.with_memory_space_constraint(v, pltpu.HBM) for v in list(a_list) + list(b_list)])
    return outs[0], outs[1], outs[2:2 + n], outs[2 + n:2 + 2 * n], outs[2 + 2 * n]


def _ici_wait(kind, started, after, name):
    send_sems, recv_sems, a_list, b_list, _ = started
    n = len(a_list)
    plan = _ici_plan(kind, a_list)
    shapes = [a.shape for a in a_list]

    def body(*refs):
        a_refs, b_refs = refs[:n], refs[n:2 * n]
        send_sems, recv_sems = refs[2 * n], refs[2 * n + 1]
        x, y, c, others = _place()
        me = 2 * x + y
        for i, (a, r0, nr) in enumerate(plan):
            for j, (ox, oy) in enumerate(others):
                src, dst, land = _ici_refs(kind, a_refs[a], b_refs[a], shapes[a], r0, nr, c, me, 2 * ox + oy)
                _remote(src, dst, send_sems, recv_sems, 3 * i + j, (ox, oy, c)).wait_send()
                _remote(land, land, send_sems, recv_sems, 3 * i + j, (x, y, c)).wait_recv()

    hbm = lambda v: pltpu.HBM(v.shape, v.dtype)
    outs = pl.pallas_call(
        body, name=name,
        in_specs=[_HBM] * (2 * n) + [_SEM, _SEM, _ANY],
        out_specs=[_HBM] * (2 * n),
        out_shape=[hbm(v) for v in a_list] + [hbm(v) for v in b_list],
        input_output_aliases={i: i for i in range(2 * n)},
        compiler_params=pltpu.CompilerParams(has_side_effects=_EFFECT),
    )(*a_list, *b_list, send_sems, recv_sems, after)
    return outs[n:]


def _rs_begin(ts, c_idx, me_idx, tag):
    got = _rs_swap(ts, "rs_swap_" + tag)
    pairs = [_add_half(t, g, c_idx, me_idx, f"rs_add2_{tag}_{a}") for a, (t, g) in enumerate(zip(ts, got))]
    return _ici_start("scatter", [p for p, _ in pairs], [m for _, m in pairs], "rs_xchg_start_" + tag)


def _rs_finish(started, after, c_idx, tag):
    parts = _ici_wait("scatter", started, after, "rs_xchg_wait_" + tag)
    fulls = [_add4_half(p, c_idx, f"rs_add4_{tag}_{a}") for a, p in enumerate(parts)]
    return _rs_join(fulls, "rs_join_" + tag)


def _pack_rows(pieces, rows, dtype):
    flat = jnp.concatenate([p.astype(dtype).reshape(-1) for p in pieces])
    return jnp.pad(flat, (0, rows * PACK_COLS - flat.shape[0])).reshape(rows, PACK_COLS)


def _unpack(flat, shapes):
    out, off = [], 0
    for shp in shapes:
        size = math.prod(shp)
        out.append(flat[off:off + size].reshape(shp))
        off += size
    return out


def _rows_for(n_elems, mult):
    rows = -(-n_elems // PACK_COLS)
    return -(-rows // mult) * mult


BIG_SHARDS = [("w_in", (D_MODEL, 1474)), ("w_branch_att", (D_ATT, 256)), ("w_branch_conv", (D_CONV, 256)),
              ("w_branch_sgu", (D_SGU, 256)), ("w_out", (256, D_MODEL)), ("w_ffn_up", (D_MODEL, FF_BLK)),
              ("w_ffn_down", (D_FF // N_CHIPS, D_MODEL))]
SMALL_SHARDS = [("b_gate", (3, 256)), ("conv_mix_w", (3, 64)), ("conv_ffn_w", (3, FF_BLK))]
REPLICATED = [("pre_mix_g", (D_MODEL,)), ("post_mix_g", (D_MODEL,)), ("pre_ffn_g", (D_MODEL,)),
              ("post_ffn_g", (D_MODEL,)), ("b_forget", (N_HEADS,)), ("sgu_ln_g", (D_SGU,)), ("sgu_ln_b", (D_SGU,)),
              ("sgu_w", (N_GROUPS, CHUNK, CHUNK)), ("sgu_b", (N_GROUPS, CHUNK))]
WEIGHT_ORDER = ["pre_mix_g", "post_mix_g", "pre_ffn_g", "post_ffn_g", "w_in", "b_forget", "b_gate", "conv_mix_w",
                "sgu_ln_g", "sgu_ln_b", "sgu_w", "sgu_b", "w_branch_att", "w_branch_conv", "w_branch_sgu", "w_out",
                "w_ffn_up", "conv_ffn_w", "w_ffn_down"]

_SMALL_ELEMS = sum(math.prod(s) for _, s in SMALL_SHARDS)
_REP_ELEMS = sum(math.prod(s) for _, s in REPLICATED)
_REP_QUARTER = -(-(DEPTH * _REP_ELEMS) // N_CHIPS)
SMALL_PARAM_ROWS = _rows_for(DEPTH * _SMALL_ELEMS, 32)
SMALL_ROWS = _rows_for(DEPTH * _SMALL_ELEMS + _REP_QUARTER, 32)
IN_WIDTH = 5896
IN_SHARD = IN_WIDTH // N_CHIPS
IN_SHARD_PAD = 1536
IN_PAD = 6144


def _gather_small(wts):
    shard = _pack_rows([wts[n] for n, _ in SMALL_SHARDS], SMALL_PARAM_ROWS, F32)
    full = _all_gather_chips(shard, "gather_small_params").reshape(N_CHIPS, -1)
    per_chip = [_unpack(full[j], [(DEPTH,) + s for _, s in SMALL_SHARDS]) for j in range(N_CHIPS)]
    return {n: jnp.concatenate([per_chip[j][i] for j in range(N_CHIPS)], axis=-1)
            for i, (n, _) in enumerate(SMALL_SHARDS)}


BIG_NAMES = [n for n, _ in BIG_SHARDS]
FIRST_NAMES = ["w_in"]
LATE_NAMES = BIG_NAMES[1:]


def _gather_begin(wts, l, me_idx, names, tag):
    cast = [_cast_shard(wts[n], l, me_idx, "cast_" + n) for n in names]
    return _ici_start("gather", [sh for sh, _ in cast], [ld for _, ld in cast], "gather_ici_start_" + tag)


def _gather_finish(started, after, names, tag):
    lands = _ici_wait("gather", started, after, "gather_ici_wait_" + tag)
    return dict(zip(names, _gather_d2d(lands, "gather_d2d_" + tag)))


def _pad_rows(a, rows):
    return jnp.pad(a, ((0, rows - a.shape[0]), (0, 0)))


def _whole_cols(land):
    return land.transpose(1, 0, 2).reshape(land.shape[1], -1)


_O_F = 3 * D_ATT
_O_B = _O_F + N_HEADS
_O_GL = _O_B + 3 * D_CONV + 2 * D_SGU


def _prep_first(wts, lands, small, l):
    w_in = _whole_cols(lands["w_in"])
    cf = small["conv_ffn_w"][l]
    blk = lambda a, j: a[:, j * FF_BLK:(j + 1) * FF_BLK]
    return {
        "w_p": jnp.concatenate([w_in[:, _O_GL:], w_in[:, :_O_F], w_in[:, _O_B:_O_GL]], axis=1),
        "w_in_bwd": jnp.concatenate([w_in[:, :_O_F], w_in[:, _O_B:], w_in[:, _O_F:_O_B],
                                     jnp.zeros((D_MODEL, IN_PAD - IN_WIDTH), BF16)], axis=1),
        "wf_t": _pad_rows(w_in[:, _O_F:_O_B].T, F_ROWS),
        "b_forget": _pad_rows(wts["b_forget"][l].reshape(N_HEADS, 1), F_ROWS),
        "b_gate": _pad_rows(small["b_gate"][l], 8),
        "conv_mix_w": _pad_rows(small["conv_mix_w"][l], 8),
        "conv_ffn_w": _pad_rows(jnp.concatenate([blk(cf, 0), blk(cf, 2), blk(cf, 1), blk(cf, 3)], axis=1), 8),
        "pre_mix_g": wts["pre_mix_g"][l].reshape(1, -1), "post_mix_g": wts["post_mix_g"][l].reshape(1, -1),
        "pre_ffn_g": wts["pre_ffn_g"][l].reshape(1, -1), "post_ffn_g": wts["post_ffn_g"][l].reshape(1, -1),
        "ln_g": wts["sgu_ln_g"][l].reshape(1, -1), "ln_b": wts["sgu_ln_b"][l].reshape(1, -1),
        "sgu_w": wts["sgu_w"][l],
        "sgu_bias": jnp.repeat(wts["sgu_b"][l].T, HEAD_DIM, axis=1),
    }


def _prep_late(lands):
    up = lands["w_ffn_up"]
    return {
        "w_att": _whole_cols(lands["w_branch_att"]), "w_conv": _whole_cols(lands["w_branch_conv"]),
        "w_sgu": _whole_cols(lands["w_branch_sgu"]),
        "w_out": lands["w_out"].reshape(D_MODEL, D_MODEL),
        "w_up": jnp.concatenate([up[0], up[2], up[1], up[3]], axis=1),
        "w_down": lands["w_ffn_down"].reshape(D_FF, D_MODEL),
    }


def _layer_fwd(x, p, dep=None, late=None):
    s = x.shape[0]
    xn = _rms_fwd(x, p["pre_mix_g"], "rms_pre_mix", dep)
    h = _mm(xn, p["w_p"], "nn", BF16, "mm_in", s, 256, D_MODEL)
    f_row = _mm(p["wf_t"], xn, "nt", F32, "mm_forget", F_ROWS, 2048, D_MODEL)
    ck = _gate_fwd(f_row, p["b_forget"], "gate_fwd")
    o, o_f32, lse = _attn_fwd(h, ck, "attn_fwd")
    yc = _sconv_fwd(h, p["conv_mix_w"], "sconv_fwd")
    ys = _sgu_fwd(h, p["ln_g"], p["ln_b"], p["sgu_w"], p["sgu_bias"], "sgu_fwd")
    if late is not None:
        p.update(late(o))
    merged = _merge_fwd(h, (o, yc, ys), (p["w_att"], p["w_conv"], p["w_sgu"]), p["b_gate"], "merge_fwd")
    mo = _mm(merged, p["w_out"], "nn", F32, "mm_out", 2048, 512, D_MODEL)
    x1 = _resid_post(x, mo, p["post_mix_g"], "post_mix")
    xn2 = _rms_fwd(x1, p["pre_ffn_g"], "rms_pre_ffn")
    h2 = _mm(xn2, p["w_up"], "nn", BF16, "mm_up", 2048, 512, D_MODEL)
    pact = _ffn_act_fwd(h2, p["conv_ffn_w"], "ffn_act_fwd")
    ff = _mm(pact, p["w_down"], "nn", F32, "mm_down", 2048, 512, FF_BLK)
    x2 = _resid_post(x1, ff, p["post_ffn_g"], "post_ffn")
    saved = dict(x=x, xn=xn, h=h, f_row=f_row, ck=ck, o=o, o_f32=o_f32, lse=lse, yc=yc, ys=ys, merged=merged, mo=mo, x1=x1,
                 xn2=xn2, h2=h2, pact=pact, ff=ff)
    return x2, saved


def _layer_bwd(dx2, p, sv, dep=None, early=None):
    s = dx2.shape[0]
    g = {}
    same = lambda b: b
    dff, g["post_ffn_g"] = _rms_bwd(sv["ff"], p["post_ffn_g"], [dx2], None, BF16, "post_ffn_bwd", dep)
    dpact = _mm(dff, p["w_down"], "nt", BF16, "mm_down_dx", 1024, FF_BLK, D_MODEL)
    t_down = _mm(sv["pact"], dff, "tn", BF16, "mm_down_dw", 256, D_MODEL, s).reshape(N_CHIPS, -1, D_MODEL)
    dh2, dconv_ffn = _ffn_act_conv_bwd(sv["h2"], p["conv_ffn_w"], dpact, "ffn_act_conv_bwd")
    dxn2 = _mm(dh2, p["w_up"], "nt", F32, "mm_up_dx", 1024, D_MODEL, FF_BLK)
    t_up = _mm(sv["xn2"], dh2, "tn", BF16, "mm_up_dw", 512, FF_BLK, s, chip_of=lambda b: (b % 2) * 2 + b // 2)
    dx1, g["pre_ffn_g"] = _rms_bwd(sv["x1"], p["pre_ffn_g"], [dxn2], dx2, F32, "pre_ffn_bwd")
    dep_mix = early([t_up, t_down]) if early is not None else None
    dmo, g["post_mix_g"] = _rms_bwd(sv["mo"], p["post_mix_g"], [dx1], None, BF16, "post_mix_bwd", dep_mix)
    dmerged = _mm(dmo, p["w_out"], "nt", F32, "mm_out_dx", 2048, 512, D_MODEL)
    t_out = _mm(sv["merged"], dmo, "tn", BF16, "mm_out_dw", 512, D_MODEL, s).reshape(N_CHIPS, -1, D_MODEL)
    acts = (sv["o"], sv["yc"], sv["ys"])
    ws = (p["w_att"], p["w_conv"], p["w_sgu"])
    dy_a, dy_c, dy_s, dgl, db_gate = _merge_bwd(sv["h"], acts, ws, p["b_gate"], dmerged, "merge_bwd")
    do = _mm(dy_a, p["w_att"], "nt", BF16, "mm_att_dx", 2048, D_ATT, D_MODEL)
    dyc = _mm(dy_c, p["w_conv"], "nt", BF16, "mm_conv_dx", 2048, D_CONV, D_MODEL)
    dys = _mm(dy_s, p["w_sgu"], "nt", BF16, "mm_sgu_dx", 2048, D_SGU, D_MODEL)
    t_att = _mm(sv["o"], dy_a, "tn", BF16, "mm_att_dw", D_ATT, 256, s, chip_of=same)
    t_conv = _mm(sv["yc"], dy_c, "tn", BF16, "mm_conv_dw", D_CONV, 256, s, chip_of=same)
    t_sgu = _mm(sv["ys"], dy_s, "tn", BF16, "mm_sgu_dw", D_SGU, 256, s, chip_of=same)
    d_conv, dconv_mix = _sconv_bwd(sv["h"], p["conv_mix_w"], dyc, "sconv_bwd")
    d_sgu, g["sgu_ln_g"], g["sgu_ln_b"], g["sgu_w"], dbias = _sgu_bwd(
        sv["h"], p["ln_g"], p["ln_b"], p["sgu_w"], p["sgu_bias"], dys, "sgu_bwd")
    dq, dk, dv, dc_even, dc_odd = _attn_bwd(sv["h"], sv["ck"], sv["o_f32"], sv["lse"], do, "attn_bwd")
    df, db_forget = _gate_bwd(sv["f_row"], p["b_forget"], dc_even, dc_odd, "gate_bwd")
    f_cols = jnp.concatenate([df[:N_HEADS].T, jnp.zeros((s, IN_PAD - IN_WIDTH), BF16)], axis=1)
    dh = _assemble_dh([dq, dk, dv, d_conv, d_sgu, dgl, f_cols], "assemble_dh")
    dxn = _mm(dh, p["w_in_bwd"], "nt", F32, "mm_in_dx", 1024, D_MODEL, 2048)
    dw_bwd = _mm(sv["xn"], dh, "tn", F32, "mm_in_dw", D_MODEL, 512, s)
    n_rest = IN_WIDTH - N_HEADS
    dw_in = jnp.concatenate([dw_bwd[:, :_O_F], dw_bwd[:, n_rest:IN_WIDTH], dw_bwd[:, _O_F:n_rest],
                             jnp.zeros((D_MODEL, IN_SHARD_PAD - IN_SHARD), F32)], axis=1)
    t_in = jnp.stack([dw_in[:, j * IN_SHARD:j * IN_SHARD + IN_SHARD_PAD] for j in range(N_CHIPS)]).astype(BF16)
    dx, g["pre_mix_g"] = _rms_bwd(sv["x"], p["pre_mix_g"], [dxn], dx1, F32, "pre_mix_bwd")
    blk = lambda a, j: a[:, j * FF_BLK:(j + 1) * FF_BLK]
    g["conv_ffn_w"] = jnp.concatenate([blk(dconv_ffn, 0), blk(dconv_ffn, 2), blk(dconv_ffn, 1),
                                       blk(dconv_ffn, 3)], axis=1)[:3]
    g["conv_mix_w"] = dconv_mix[:3]
    g["b_gate"] = db_gate[:3]
    g["b_forget"] = db_forget[:N_HEADS, 0]
    g["sgu_b"] = jnp.sum(dbias.reshape(CHUNK, N_GROUPS, HEAD_DIM), axis=-1).T
    for n in ("pre_mix_g", "post_mix_g", "pre_ffn_g", "post_ffn_g", "sgu_ln_g", "sgu_ln_b"):
        g[n] = g[n].reshape(-1)
    mix = [t_in, t_att, t_conv, t_sgu, t_out]
    return dx, (mix if early is not None else mix + [t_up, t_down]), g


def _assemble_dh(pieces, name):
    s = pieces[0].shape[0]
    t = _tile(s, 512)
    width = sum(a.shape[1] for a in pieces)

    def body(*refs):
        out = refs[-1]
        col = 0
        for ref in refs[:-1]:
            w = ref.shape[1]
            out[:, col:col + w] = ref[...].astype(out.dtype)
            col += w

    return pl.pallas_call(
        body, name=name, grid=(s // t,),
        in_specs=[pl.BlockSpec((t, a.shape[1]), lambda i: (i, 0)) for a in pieces],
        out_specs=pl.BlockSpec((t, width), lambda i: (i, 0)),
        out_shape=jax.ShapeDtypeStruct((s, width), BF16),
        compiler_params=_params(("parallel",)),
    )(*pieces)


def _shard_cols(a, j):
    w = a.shape[-1] // N_CHIPS
    return a[..., j * w:(j + 1) * w]


def kernel(x, pre_mix_g, post_mix_g, pre_ffn_g, post_ffn_g, w_in, b_forget, b_gate, conv_mix_w, sgu_ln_g, sgu_ln_b, sgu_w, sgu_b, w_branch_att, w_branch_conv, w_branch_sgu, w_out, w_ffn_up, conv_ffn_w, w_ffn_down, loss_target, m_pre_mix_g, m_post_mix_g, m_pre_ffn_g, m_post_ffn_g, m_w_in, m_b_forget, m_b_gate, m_conv_mix_w, m_sgu_ln_g, m_sgu_ln_b, m_sgu_w, m_sgu_b, m_w_branch_att, m_w_branch_conv, m_w_branch_sgu, m_w_out, m_w_ffn_up, m_conv_ffn_w, m_w_ffn_down, v_pre_mix_g, v_post_mix_g, v_pre_ffn_g, v_post_ffn_g, v_w_in, v_b_forget, v_b_gate, v_conv_mix_w, v_sgu_ln_g, v_sgu_ln_b, v_sgu_w, v_sgu_b, v_w_branch_att, v_w_branch_conv, v_w_branch_sgu, v_w_out, v_w_ffn_up, v_conv_ffn_w, v_w_ffn_down):
    wts = dict(pre_mix_g=pre_mix_g, post_mix_g=post_mix_g, pre_ffn_g=pre_ffn_g, post_ffn_g=post_ffn_g, w_in=w_in,
               b_forget=b_forget, b_gate=b_gate, conv_mix_w=conv_mix_w, sgu_ln_g=sgu_ln_g, sgu_ln_b=sgu_ln_b,
               sgu_w=sgu_w, sgu_b=sgu_b, w_branch_att=w_branch_att, w_branch_conv=w_branch_conv,
               w_branch_sgu=w_branch_sgu, w_out=w_out, w_ffn_up=w_ffn_up, conv_ffn_w=conv_ffn_w,
               w_ffn_down=w_ffn_down)
    moms = dict(pre_mix_g=m_pre_mix_g, post_mix_g=m_post_mix_g, pre_ffn_g=m_pre_ffn_g, post_ffn_g=m_post_ffn_g,
                w_in=m_w_in, b_forget=m_b_forget, b_gate=m_b_gate, conv_mix_w=m_conv_mix_w, sgu_ln_g=m_sgu_ln_g,
                sgu_ln_b=m_sgu_ln_b, sgu_w=m_sgu_w, sgu_b=m_sgu_b, w_branch_att=m_w_branch_att,
                w_branch_conv=m_w_branch_conv, w_branch_sgu=m_w_branch_sgu, w_out=m_w_out, w_ffn_up=m_w_ffn_up,
                conv_ffn_w=m_conv_ffn_w, w_ffn_down=m_w_ffn_down)
    vels = dict(pre_mix_g=v_pre_mix_g, post_mix_g=v_post_mix_g, pre_ffn_g=v_pre_ffn_g, post_ffn_g=v_post_ffn_g,
                w_in=v_w_in, b_forget=v_b_forget, b_gate=v_b_gate, conv_mix_w=v_conv_mix_w, sgu_ln_g=v_sgu_ln_g,
                sgu_ln_b=v_sgu_ln_b, sgu_w=v_sgu_w, sgu_b=v_sgu_b, w_branch_att=v_w_branch_att,
                w_branch_conv=v_w_branch_conv, w_branch_sgu=v_w_branch_sgu, w_out=v_w_out, w_ffn_up=v_w_ffn_up,
                conv_ffn_w=v_conv_ffn_w, w_ffn_down=v_w_ffn_down)

    c_idx = lax.axis_index("c").astype(jnp.int32).reshape(1)
    me_idx = (2 * lax.axis_index("x") + lax.axis_index("y")).astype(jnp.int32).reshape(1)
    small = _gather_small(wts)

    xs = x[0]
    layers, saved = [], []
    first = _gather_begin(wts, 0, me_idx, FIRST_NAMES, "first")
    rest = _gather_begin(wts, 0, me_idx, LATE_NAMES, "late")
    lands = _gather_finish(first, xs, FIRST_NAMES, "first")
    late = lambda after: _prep_late(_gather_finish(rest, after, LATE_NAMES, "late"))
    for l in range(DEPTH):
        p = _prep_first(wts, lands, small, l)
        if l > 0:
            p.update(_prep_late(lands))
        nxt = _gather_begin(wts, l + 1, me_idx, BIG_NAMES, "all") if l + 1 < DEPTH else None
        dep = ([nxt[4]] if nxt else []) + ([rest[4]] if l == 0 else [])
        xs, sv = _layer_fwd(xs, p, dep or None, late if l == 0 else None)
        if nxt:
            lands = _gather_finish(nxt, xs, BIG_NAMES, "all")
        layers.append(p)
        saved.append(sv)
    dy, loss_part = _loss_head(xs, loss_target[0], "loss_head")
    loss = lax.psum(loss_part[0, 0], ("x", "y", "c"))

    big_red = [None] * DEPTH
    small_grads = [None] * DEPTH
    pending = None
    ffn = []
    for l in reversed(range(DEPTH)):
        early = None
        if l == 0:
            def early(ts_ffn):
                ffn.append(_rs_begin(ts_ffn, c_idx, me_idx, "ffn"))
                return ffn[0][4]
        dy, ts, small_grads[l] = _layer_bwd(dy, layers[l], saved[l], pending[4] if pending else None, early)
        if pending:
            big_red[l + 1] = _rs_finish(pending, dy, c_idx, "big")
        pending = _rs_begin(ts, c_idx, me_idx, "mix" if l == 0 else "big")
    red_ffn = _rs_finish(ffn[0], dy, c_idx, "ffn")
    grad_x = dy[None]

    rep_flat = jnp.concatenate([small_grads[l][n].reshape(-1) for l in range(DEPTH) for n, _ in REPLICATED])
    rep_flat = jnp.pad(rep_flat, (0, N_CHIPS * _REP_QUARTER - rep_flat.shape[0]))
    rows = []
    for j in range(N_CHIPS):
        pieces = [_shard_cols(small_grads[l][n], j) for l in range(DEPTH) for n, _ in SMALL_SHARDS]
        pieces.append(rep_flat[j * _REP_QUARTER:(j + 1) * _REP_QUARTER])
        rows.append(_pack_rows(pieces, SMALL_ROWS, F32))
    small_red = _reduce_scatter_chips(jnp.stack(rows), "small")
    small_all = _all_gather_chips(small_red, "gather_small")
    big_red[0] = _rs_finish(pending, small_all, c_idx, "mix") + red_ffn
    small_all = small_all.reshape(N_CHIPS, -1)

    grads = {}
    for i, (n, _) in enumerate(BIG_SHARDS):
        grads[n] = jnp.stack([big_red[l][i][:, :IN_SHARD] if n == "w_in" else big_red[l][i] for l in range(DEPTH)])
    mine_small = small_red.reshape(-1)
    parts = _unpack(mine_small, [s for _ in range(DEPTH) for _, s in SMALL_SHARDS])
    for i, (n, _) in enumerate(SMALL_SHARDS):
        grads[n] = jnp.stack([parts[l * len(SMALL_SHARDS) + i] for l in range(DEPTH)])
    off = DEPTH * _SMALL_ELEMS
    rep_all = jnp.concatenate([small_all[j, off:off + _REP_QUARTER] for j in range(N_CHIPS)])
    parts = _unpack(rep_all, [s for _ in range(DEPTH) for _, s in REPLICATED])
    for i, (n, _) in enumerate(REPLICATED):
        grads[n] = jnp.stack([parts[l * len(REPLICATED) + i] for l in range(DEPTH)])

    deltas, new_m, new_v = {}, {}, {}
    for n in WEIGHT_ORDER:
        deltas[n], new_m[n], new_v[n] = _adamw(wts[n], grads[n], moms[n], vels[n], "adamw_" + n)
    return (loss, grad_x, *[grads[n] for n in WEIGHT_ORDER], *[deltas[n] for n in WEIGHT_ORDER],
            *[new_m[n] for n in WEIGHT_ORDER], *[new_v[n] for n in WEIGHT_ORDER])
```

```python
import functools
import math

import jax
import jax.numpy as jnp
from jax import lax
from jax.experimental import pallas as pl
from jax.experimental.pallas import tpu as pltpu

F32 = jnp.float32
BF16 = jnp.bfloat16
MXU_DTYPE = jnp.bfloat16

D_MODEL = 1024
HEAD_DIM = 64
N_HEADS = 8
D_ATT = 512
D_CONV = 256
D_SGU = 256
N_GROUPS = 4
CHUNK = 128
D_FF = 2816
DEPTH = 4
RMS_EPS = 1e-6
LN_EPS = 1e-5
N_CHIPS = 4
LANES = 128
PACK_COLS = 1024
HALO = 16

ADAM_LR = 0.001
ADAM_B1 = 0.9
ADAM_B2 = 0.999
ADAM_EPS = 1e-08
ADAM_WD = 0.01
ADAM_STEP = 10

OFF_GL = 0
OFF_Q = 3 * D_MODEL
OFF_K = OFF_Q + D_ATT
OFF_V = OFF_K + D_ATT
OFF_BG = OFF_V + D_ATT
OFF_CG = OFF_BG + D_CONV
OFF_HC = OFF_CG + D_CONV
OFF_U = OFF_HC + D_CONV
OFF_VS = OFF_U + D_SGU
W_P = OFF_VS + D_SGU
F_ROWS = 16

VMEM_LIMIT = 56 * 1024 * 1024
MESH = pl.DeviceIdType.MESH


def _params(sem=None):
    if sem is None:
        return pltpu.CompilerParams(vmem_limit_bytes=VMEM_LIMIT)
    return pltpu.CompilerParams(dimension_semantics=sem, vmem_limit_bytes=VMEM_LIMIT)


def _tile(dim, pref):
    if dim <= pref:
        return dim
    if dim % pref == 0:
        return pref
    return dim


_DIMS = {"nn": (((1,), (0,)), ((), ())), "nt": (((1,), (1,)), ((), ())), "tn": (((0,), (0,)), ((), ()))}


def _mm(a, b, mode, out_dtype, name, tm, tn, tk, chip_of=None):
    if mode == "tn":
        K, M = a.shape
    else:
        M, K = a.shape
    N = b.shape[0] if mode == "nt" else b.shape[1]
    tm, tn, tk = _tile(M, tm), _tile(N // N_CHIPS if chip_of else N, tn), _tile(K, tk)
    nk = K // tk
    dims = _DIMS[mode]

    def body(a_ref, b_ref, o_ref, *acc):
        part = lax.dot_general(a_ref[...].astype(MXU_DTYPE), b_ref[...].astype(MXU_DTYPE), dims,
                               preferred_element_type=F32)
        if nk == 1:
            o_ref[...] = part.astype(o_ref.dtype)
        else:
            acc_ref = acc[0]
            k = pl.program_id(2)

            @pl.when(k == 0)
            def _():
                acc_ref[...] = part

            @pl.when(k > 0)
            def _():
                acc_ref[...] += part

            @pl.when(k == nk - 1)
            def _():
                o_ref[...] = acc_ref[...].astype(o_ref.dtype)

    if mode == "tn":
        a_spec = pl.BlockSpec((tk, tm), lambda i, j, k: (k, i))
    else:
        a_spec = pl.BlockSpec((tm, tk), lambda i, j, k: (i, k))
    if mode == "nt":
        b_spec = pl.BlockSpec((tn, tk), lambda i, j, k: (j, k))
    else:
        b_spec = pl.BlockSpec((tk, tn), lambda i, j, k: (k, j))
    if chip_of is None:
        out_spec = pl.BlockSpec((tm, tn), lambda i, j, k: (i, j))
        out_shape = jax.ShapeDtypeStruct((M, N), out_dtype)
    else:
        per = (N // N_CHIPS) // tn
        out_spec = pl.BlockSpec((None, tm, tn), lambda i, j, k: (chip_of(j // per), i, j % per))
        out_shape = jax.ShapeDtypeStruct((N_CHIPS, M, N // N_CHIPS), out_dtype)
    return pl.pallas_call(
        body,
        name=name,
        grid=(M // tm, N // tn, nk),
        in_specs=[a_spec, b_spec],
        out_specs=out_spec,
        out_shape=out_shape,
        scratch_shapes=[pltpu.VMEM((tm, tn), F32)] if nk > 1 else [],
        compiler_params=_params(("parallel", "parallel", "arbitrary")),
    )(a, b)


_GELU_K = math.sqrt(2.0 / math.pi)
_GELU_C = 0.044715


def _gelu(x):
    t = jnp.tanh(_GELU_K * (x + _GELU_C * (x * x * x)))
    return x * (0.5 * (1.0 + t))


def _gelu_and_grad(x):
    x2 = x * x
    t = jnp.tanh(_GELU_K * (x + _GELU_C * (x2 * x)))
    cdf = 0.5 * (1.0 + t)
    dcdf = 0.5 * (1.0 - t * t) * (_GELU_K * (1.0 + 3.0 * _GELU_C * x2))
    return x * cdf, cdf + x * dcdf


def _sigmoid(x):
    return 1.0 / (1.0 + jnp.exp(-x))


def _shift_down(cur, prev, k):
    h = prev.shape[0]
    ext = jnp.concatenate([prev, cur], axis=0)
    return pltpu.roll(ext, k, 0)[h:]


def _shift_up(cur, nxt, k):
    t, h = cur.shape[0], nxt.shape[0]
    ext = jnp.concatenate([cur, nxt], axis=0)
    return pltpu.roll(ext, t + h - k, 0)[:t]


def _row_sum8(x):
    t, c = x.shape
    return jnp.sum(x.reshape(t // 8, 8, c), axis=0)


_DEP = pl.BlockSpec((8, LANES), lambda i: (0, 0))


def _rms_fwd(x, g, name, dep=None):
    s, d = x.shape
    t = _tile(s, 512)

    def body(x_ref, g_ref, *rest):
        o_ref = rest[-1]
        xv = x_ref[...]
        r = lax.rsqrt(jnp.mean(xv * xv, axis=-1, keepdims=True) + RMS_EPS)
        o_ref[...] = (xv * r * g_ref[...]).astype(o_ref.dtype)

    deps = [] if dep is None else list(dep) if isinstance(dep, (list, tuple)) else [dep]
    return pl.pallas_call(
        body, name=name, grid=(s // t,),
        in_specs=[pl.BlockSpec((t, d), lambda i: (i, 0)), pl.BlockSpec((1, d), lambda i: (0, 0))] + [_DEP] * len(deps),
        out_specs=pl.BlockSpec((t, d), lambda i: (i, 0)),
        out_shape=jax.ShapeDtypeStruct((s, d), BF16),
        compiler_params=_params(("parallel",)),
    )(x, g, *deps)


def _resid_post(x, y, g, name):
    s, d = x.shape
    t = _tile(s, 512)

    def body(x_ref, y_ref, g_ref, o_ref):
        yv = y_ref[...]
        r = lax.rsqrt(jnp.mean(yv * yv, axis=-1, keepdims=True) + RMS_EPS)
        o_ref[...] = x_ref[...] + yv * r * g_ref[...]

    row = pl.BlockSpec((t, d), lambda i: (i, 0))
    return pl.pallas_call(
        body, name=name, grid=(s // t,),
        in_specs=[row, row, pl.BlockSpec((1, d), lambda i: (0, 0))],
        out_specs=row,
        out_shape=jax.ShapeDtypeStruct((s, d), F32),
        compiler_params=_params(("parallel",)),
    )(x, y, g)


def _rms_bwd(xin, g, dys, dres, out_dtype, name, dep=None):
    s, d = xin.shape
    t = _tile(s, 512)
    n = s // t
    n_dy = len(dys)
    has_res = dres is not None
    deps = [] if dep is None else [dep]

    def body(*refs):
        x_ref, g_ref = refs[0], refs[1]
        dy_refs = refs[2:2 + n_dy]
        pos = 2 + n_dy
        res_ref = refs[pos] if has_res else None
        pos += (1 if has_res else 0) + len(deps)
        dx_ref, dg_ref, acc_ref = refs[pos], refs[pos + 1], refs[pos + 2]
        i = pl.program_id(0)
        xv = x_ref[...]
        dy = dy_refs[0][...].astype(F32)
        for extra in dy_refs[1:]:
            dy = dy + extra[...].astype(F32)
        r = lax.rsqrt(jnp.mean(xv * xv, axis=-1, keepdims=True) + RMS_EPS)
        u = dy * g_ref[...]
        xr = xv * r
        dx = r * (u - xr * jnp.mean(u * xr, axis=-1, keepdims=True))
        if has_res:
            dx = dx + res_ref[...]
        dx_ref[...] = dx.astype(dx_ref.dtype)
        part = _row_sum8(dy * xr)

        @pl.when(i == 0)
        def _():
            acc_ref[...] = part

        @pl.when(i > 0)
        def _():
            acc_ref[...] += part

        @pl.when(i == n - 1)
        def _():
            dg_ref[...] = jnp.sum(acc_ref[...], axis=0, keepdims=True)

    row = pl.BlockSpec((t, d), lambda i: (i, 0))
    vec = pl.BlockSpec((1, d), lambda i: (0, 0))
    ins = [xin, g, *dys] + ([dres] if has_res else []) + deps
    return pl.pallas_call(
        body, name=name, grid=(n,),
        in_specs=[row, vec] + [row] * (n_dy + (1 if has_res else 0)) + [_DEP] * len(deps),
        out_specs=[row, vec],
        out_shape=[jax.ShapeDtypeStruct((s, d), out_dtype), jax.ShapeDtypeStruct((1, d), F32)],
        scratch_shapes=[pltpu.VMEM((8, d), F32)],
        compiler_params=_params(("arbitrary",)),
    )(*ins)


def _loss_head(y, target, name):
    s, d = y.shape
    t = _tile(s, 512)
    n = s // t

    def body(y_ref, t_ref, dy_ref, loss_ref, acc_ref):
        i = pl.program_id(0)
        e = y_ref[...] - t_ref[...]
        dy_ref[...] = e * (1.0 / d)
        part = _row_sum8(e * e)

        @pl.when(i == 0)
        def _():
            acc_ref[...] = part

        @pl.when(i > 0)
        def _():
            acc_ref[...] += part

        @pl.when(i == n - 1)
        def _():
            tot = jnp.sum(jnp.sum(acc_ref[...], axis=0, keepdims=True), axis=1, keepdims=True)
            loss_ref[...] = tot * (0.5 / d)

    row = pl.BlockSpec((t, d), lambda i: (i, 0))
    return pl.pallas_call(
        body, name=name, grid=(n,),
        in_specs=[row, row],
        out_specs=[row, pl.BlockSpec((1, 1), lambda i: (0, 0))],
        out_shape=[jax.ShapeDtypeStruct((s, d), F32), jax.ShapeDtypeStruct((1, 1), F32)],
        scratch_shapes=[pltpu.VMEM((8, d), F32)],
        compiler_params=_params(("arbitrary",)),
    )(y, target)


def _split3(x):
    hi = x.astype(BF16)
    r1 = x - hi.astype(F32)
    mid = r1.astype(BF16)
    lo = (r1 - mid.astype(F32)).astype(BF16)
    return hi, mid, lo


def _tri_dot(x, tri):
    hi, mid, lo = _split3(x)
    dn = _DIMS["nn"]
    out = lax.dot_general(hi, tri, dn, preferred_element_type=F32)
    out = out + lax.dot_general(mid, tri, dn, preferred_element_type=F32)
    return out + lax.dot_general(lo, tri, dn, preferred_element_type=F32)


def _log_sigmoid(z):
    return jnp.minimum(z, 0.0) - jnp.log(1.0 + jnp.exp(-jnp.abs(z)))


def _gate_fwd(f_row, b_col, name):
    rows, s = f_row.shape
    t = _tile(s, 512)
    n = s // t

    def body(f_ref, b_ref, ck_ref, carry_ref):
        i = pl.program_id(0)

        @pl.when(i == 0)
        def _():
            carry_ref[...] = jnp.zeros_like(carry_ref)

        logf = _log_sigmoid(f_ref[...] + b_ref[...])
        r = lax.broadcasted_iota(jnp.int32, (t, t), 0)
        c = lax.broadcasted_iota(jnp.int32, (t, t), 1)
        tri = jnp.where(r <= c, 1.0, 0.0).astype(BF16)
        cs = _tri_dot(logf, tri) + carry_ref[...]
        carry_ref[...] = cs[:, t - 1:t]
        terms = [part.astype(F32) for part in _split3(-cs)]
        sub = lax.broadcasted_iota(jnp.int32, (LANES, t), 0)
        for p in range(N_HEADS // 2):
            stacked = jnp.zeros((LANES, t), F32)
            for hh in range(2):
                for j, term in enumerate(terms):
                    h = 2 * p + hh
                    stacked = jnp.where(sub == 3 * hh + j, jnp.broadcast_to(term[h:h + 1, :], (LANES, t)), stacked)
            ck_ref[p] = stacked.T.astype(ck_ref.dtype)

    return pl.pallas_call(
        body, name=name, grid=(n,),
        in_specs=[pl.BlockSpec((rows, t), lambda i: (0, i)), pl.BlockSpec((rows, 1), lambda i: (0, 0))],
        out_specs=pl.BlockSpec((N_HEADS // 2, t, LANES), lambda i: (0, i, 0)),
        out_shape=jax.ShapeDtypeStruct((N_HEADS // 2, s, LANES), BF16),
        scratch_shapes=[pltpu.VMEM((rows, 1), F32)],
        compiler_params=_params(("arbitrary",)),
    )(f_row, b_col)


def _gate_bwd(f_row, b_col, dc_even, dc_odd, name):
    rows, s = f_row.shape
    t = _tile(s, 512)
    n = s // t

    def body(f_ref, b_ref, dce_ref, dco_ref, df_ref, db_ref, carry_ref, acc_ref):
        i = pl.program_id(0)

        @pl.when(i == 0)
        def _():
            carry_ref[...] = jnp.zeros_like(carry_ref)
            acc_ref[...] = jnp.zeros_like(acc_ref)

        head = lax.broadcasted_iota(jnp.int32, (rows, t), 0)
        dcv = jnp.zeros((rows, t), F32)
        for h in range(N_HEADS):
            src = dce_ref if h % 2 == 0 else dco_ref
            dcv = jnp.where(head == h, jnp.broadcast_to(src[h // 2, 0:1, :], (rows, t)), dcv)
        r = lax.broadcasted_iota(jnp.int32, (t, t), 0)
        c = lax.broadcasted_iota(jnp.int32, (t, t), 1)
        tri = jnp.where(r >= c, 1.0, 0.0).astype(BF16)
        dlogf = _tri_dot(dcv, tri) + carry_ref[...]
        carry_ref[...] = dlogf[:, 0:1]
        z = f_ref[...] + b_ref[...]
        df = dlogf * _sigmoid(-z)
        df_ref[...] = df.astype(df_ref.dtype)
        acc_ref[...] += jnp.sum(df, axis=1, keepdims=True)

        @pl.when(i == n - 1)
        def _():
            db_ref[...] = acc_ref[...]

    rev = lambda i: (0, n - 1 - i)
    dc_spec = pl.BlockSpec((N_HEADS // 2, 8, t), lambda i: (0, 0, n - 1 - i))
    return pl.pallas_call(
        body, name=name, grid=(n,),
        in_specs=[pl.BlockSpec((rows, t), rev), pl.BlockSpec((rows, 1), lambda i: (0, 0)), dc_spec, dc_spec],
        out_specs=[pl.BlockSpec((rows, t), rev), pl.BlockSpec((rows, 1), lambda i: (0, 0))],
        out_shape=[jax.ShapeDtypeStruct((rows, s), BF16), jax.ShapeDtypeStruct((rows, 1), F32)],
        scratch_shapes=[pltpu.VMEM((rows, 1), F32), pltpu.VMEM((rows, 1), F32)],
        compiler_params=_params(("arbitrary",)),
    )(f_row, b_col, dc_even, dc_odd)


_NEG = -1e30
_SCALE = HEAD_DIM ** -0.5


def _head_masks():
    lane = lax.broadcasted_iota(jnp.int32, (1, LANES), 1)
    return [lane < HEAD_DIM, lane >= HEAD_DIM]


def _attn_fwd(h, ck, name):
    s = h.shape[0]
    t = _tile(s, 512)
    n = s // t
    qb, kb, vb = OFF_Q // LANES, OFF_K // LANES, OFF_V // LANES

    pairs = [(qi, ki) for qi in range(n) for ki in range(qi + 1)]
    qi_tab = jnp.asarray([qi for qi, _ in pairs], jnp.int32)
    ki_tab = jnp.asarray([ki for _, ki in pairs], jnp.int32)

    def body(qi_ref, ki_ref, q_ref, k_ref, v_ref, ck_ref, o_ref, of_ref, lse_ref, m_ref, l_ref, acc_ref):
        qi, ki = qi_ref[pl.program_id(1)], ki_ref[pl.program_id(1)]
        masks = _head_masks()
        lane = lax.broadcasted_iota(jnp.int32, (1, LANES), 1)

        @pl.when(ki == 0)
        def _():
            m_ref[...] = jnp.full_like(m_ref, _NEG)
            l_ref[...] = jnp.zeros_like(l_ref)
            acc_ref[...] = jnp.zeros_like(acc_ref)

        def step(diag):
            q = q_ref[...] * _SCALE
            k_aug = jnp.concatenate([k_ref[...], ck_ref[0]], axis=1)
            v = v_ref[...]
            scores = []
            for hh in range(2):
                ones = jnp.where((lane >= 3 * hh) & (lane < 3 * hh + 3), 1.0, 0.0).astype(q.dtype)
                q_aug = jnp.concatenate([jnp.where(masks[hh], q, jnp.zeros_like(q)),
                                         jnp.broadcast_to(ones, q.shape)], axis=1)
                scores.append(lax.dot_general(k_aug, q_aug, _DIMS["nt"], preferred_element_type=F32))
            probs = []
            for hh in range(2):
                sc = scores[hh]
                if diag:
                    r = lax.broadcasted_iota(jnp.int32, (t, t), 0)
                    cc = lax.broadcasted_iota(jnp.int32, (t, t), 1)
                    sc = jnp.where(r <= cc, sc, _NEG)
                m_prev = m_ref[hh]
                m_new = jnp.maximum(m_prev, jnp.max(sc, axis=0, keepdims=True))
                alpha = jnp.exp(m_prev - m_new)
                p = jnp.exp(sc - m_new)
                l_ref[hh] = alpha * l_ref[hh] + jnp.sum(p, axis=0, keepdims=True)
                m_ref[hh] = m_new
                p_hi = p.astype(MXU_DTYPE)
                p_lo = (p - p_hi.astype(F32)).astype(MXU_DTYPE)
                probs.append((alpha, p_hi, p_lo))
            for hh, (alpha, p_hi, p_lo) in enumerate(probs):
                pv = (lax.dot_general(v, p_hi, _DIMS["tn"], preferred_element_type=F32)
                      + lax.dot_general(v, p_lo, _DIMS["tn"], preferred_element_type=F32))
                rows = slice(hh * HEAD_DIM, (hh + 1) * HEAD_DIM)
                acc_ref[rows, :] = alpha * acc_ref[rows, :] + pv[rows]

        @pl.when(ki < qi)
        def _():
            step(False)

        @pl.when(ki == qi)
        def _():
            step(True)
            inv = jnp.concatenate([jnp.broadcast_to(1.0 / l_ref[hh], (HEAD_DIM, t)) for hh in range(2)], axis=0)
            out = (acc_ref[...] * inv).T
            o_ref[...] = out.astype(o_ref.dtype)
            of_ref[...] = out
            lse = jnp.concatenate([jnp.broadcast_to(m_ref[hh] + jnp.log(l_ref[hh]), (HEAD_DIM, t))
                                   for hh in range(2)], axis=0)
            lse_ref[...] = lse.T

    grid_spec = pltpu.PrefetchScalarGridSpec(
        num_scalar_prefetch=2, grid=(N_HEADS // 2, len(pairs)),
        in_specs=[
            pl.BlockSpec((t, LANES), lambda p, i, qt, kt: (qt[i], qb + p)),
            pl.BlockSpec((t, LANES), lambda p, i, qt, kt: (kt[i], kb + p)),
            pl.BlockSpec((t, LANES), lambda p, i, qt, kt: (kt[i], vb + p)),
            pl.BlockSpec((1, t, LANES), lambda p, i, qt, kt: (p, kt[i], 0)),
        ],
        out_specs=[pl.BlockSpec((t, LANES), lambda p, i, qt, kt: (qt[i], p))] * 3,
        scratch_shapes=[pltpu.VMEM((2, 1, t), F32), pltpu.VMEM((2, 1, t), F32), pltpu.VMEM((LANES, t), F32)])
    return pl.pallas_call(
        body, name=name, grid_spec=grid_spec,
        out_shape=[jax.ShapeDtypeStruct((s, D_ATT), BF16), jax.ShapeDtypeStruct((s, D_ATT), F32),
                   jax.ShapeDtypeStruct((s, D_ATT), F32)],
        compiler_params=_params(("parallel", "arbitrary")),
    )(qi_tab, ki_tab, h, h, h, ck)


def _attn_bwd(h, ck, o, lse, do, name):
    s = h.shape[0]
    t = _tile(s, 512)
    n = s // t
    qb, kb, vb = OFF_Q // LANES, OFF_K // LANES, OFF_V // LANES

    pairs = [(ki, qi) for ki in range(n) for qi in range(ki, n)]
    ki_tab = jnp.asarray([ki for ki, _ in pairs], jnp.int32)
    qi_tab = jnp.asarray([qi for _, qi in pairs], jnp.int32)

    def body(ki_ref, qi_ref, q_ref, k_ref, v_ref, ck_ref, o_ref, lse_ref, do_ref,
             dq_ref, dk_ref, dv_ref, dc0_ref, dc1_ref, dk_acc, dv_acc, dc_acc):
        ki, qi = ki_ref[pl.program_id(1)], qi_ref[pl.program_id(1)]
        masks = _head_masks()
        lane = lax.broadcasted_iota(jnp.int32, (1, LANES), 1)

        @pl.when((ki == 0) & (qi == 0))
        def _():
            dq_ref[...] = jnp.zeros_like(dq_ref)

        @pl.when(qi == ki)
        def _():
            dk_acc[...] = jnp.zeros_like(dk_acc)
            dv_acc[...] = jnp.zeros_like(dv_acc)
            dc_acc[...] = jnp.zeros_like(dc_acc)

        def step(diag):
            q = q_ref[...] * _SCALE
            k = k_ref[...]
            v = v_ref[...]
            dov = do_ref[...]
            k_aug = jnp.concatenate([k, ck_ref[0]], axis=1)
            prod_t = (dov.astype(F32) * o_ref[...]).T
            lse_t = lse_ref[...].T
            heads = []
            for hh in range(2):
                mk = masks[hh]
                qh = jnp.where(mk, q, jnp.zeros_like(q))
                kh = jnp.where(mk, k, jnp.zeros_like(k))
                doh = jnp.where(mk, dov, jnp.zeros_like(dov))
                ones = jnp.where((lane >= 3 * hh) & (lane < 3 * hh + 3), 1.0, 0.0).astype(q.dtype)
                q_aug = jnp.concatenate([qh, jnp.broadcast_to(ones, q.shape)], axis=1)
                sc = lax.dot_general(k_aug, q_aug, _DIMS["nt"], preferred_element_type=F32)
                dp = lax.dot_general(v, doh, _DIMS["nt"], preferred_element_type=F32)
                heads.append((qh, kh, doh, sc, dp))
            grads = []
            for hh, (qh, kh, doh, sc, dp) in enumerate(heads):
                rows = slice(hh * HEAD_DIM, (hh + 1) * HEAD_DIM)
                p = jnp.exp(sc - lse_t[hh * HEAD_DIM:hh * HEAD_DIM + 1, :])
                if diag:
                    r = lax.broadcasted_iota(jnp.int32, (t, t), 0)
                    cc = lax.broadcasted_iota(jnp.int32, (t, t), 1)
                    p = jnp.where(r <= cc, p, 0.0)
                delta = jnp.sum(prod_t[rows], axis=0, keepdims=True)
                ds = p * (dp - delta)
                dc_acc[hh] = dc_acc[hh] - jnp.sum(ds, axis=1, keepdims=True)
                grads.append((ds.astype(MXU_DTYPE), p.astype(MXU_DTYPE)))
            dq_blk = jnp.zeros((t, LANES), F32)
            for (qh, kh, doh, _, _), (dsb, pb) in zip(heads, grads):
                dv_acc[...] += lax.dot_general(pb, doh, _DIMS["nn"], preferred_element_type=F32)
                dk_acc[...] += lax.dot_general(dsb, qh, _DIMS["nn"], preferred_element_type=F32)
                dq_blk = dq_blk + lax.dot_general(dsb, kh, _DIMS["tn"], preferred_element_type=F32)
            rows_q = pl.ds(pl.multiple_of(qi * t, t), t)
            dq_ref[rows_q, :] = dq_ref[rows_q, :] + dq_blk * _SCALE

        @pl.when(qi > ki)
        def _():
            step(False)

        @pl.when(qi == ki)
        def _():
            step(True)

        @pl.when(qi == n - 1)
        def _():
            dk_ref[...] = dk_acc[...].astype(dk_ref.dtype)
            dv_ref[...] = dv_acc[...].astype(dv_ref.dtype)
            dc0_ref[0] = jnp.broadcast_to(dc_acc[0], (t, LANES)).T[0:8]
            dc1_ref[0] = jnp.broadcast_to(dc_acc[1], (t, LANES)).T[0:8]

    q_blk = lambda col: pl.BlockSpec((t, LANES), lambda p, i, kt, qt: (qt[i], col(p)))
    k_blk = lambda col: pl.BlockSpec((t, LANES), lambda p, i, kt, qt: (kt[i], col(p)))
    dc_blk = pl.BlockSpec((1, 8, t), lambda p, i, kt, qt: (p, 0, kt[i]))
    grid_spec = pltpu.PrefetchScalarGridSpec(
        num_scalar_prefetch=2, grid=(N_HEADS // 2, len(pairs)),
        in_specs=[q_blk(lambda p: qb + p), k_blk(lambda p: kb + p), k_blk(lambda p: vb + p),
                  pl.BlockSpec((1, t, LANES), lambda p, i, kt, qt: (p, kt[i], 0)),
                  q_blk(lambda p: p), q_blk(lambda p: p), q_blk(lambda p: p)],
        out_specs=[pl.BlockSpec((s, LANES), lambda p, i, kt, qt: (0, p)), k_blk(lambda p: p), k_blk(lambda p: p),
                   dc_blk, dc_blk],
        scratch_shapes=[pltpu.VMEM((t, LANES), F32), pltpu.VMEM((t, LANES), F32), pltpu.VMEM((2, t, 1), F32)])
    return pl.pallas_call(
        body, name=name, grid_spec=grid_spec,
        out_shape=[jax.ShapeDtypeStruct((s, D_ATT), F32), jax.ShapeDtypeStruct((s, D_ATT), BF16),
                   jax.ShapeDtypeStruct((s, D_ATT), BF16), jax.ShapeDtypeStruct((N_HEADS // 2, 8, s), F32),
                   jax.ShapeDtypeStruct((N_HEADS // 2, 8, s), F32)],
        compiler_params=_params(("parallel", "arbitrary")),
    )(ki_tab, qi_tab, h, h, h, ck, o, lse, do)


def _conv3(z, z_prev, w_ref):
    return (w_ref[2:3, :] * z + w_ref[1:2, :] * _shift_down(z, z_prev, 1)
            + w_ref[0:1, :] * _shift_down(z, z_prev, 2))


def _sconv_fwd(h, w, name):
    s = h.shape[0]
    t = _tile(s, 512)
    r = t // HALO
    c = D_CONV
    b_bg, b_cg, b_hc = OFF_BG // c, OFF_CG // c, OFF_HC // c

    def body(bg_ref, cg_ref, hc_ref, cgp_ref, hcp_ref, w_ref, y_ref):
        i = pl.program_id(0)
        live = (i > 0).astype(F32)
        z = cg_ref[...].astype(F32) * hc_ref[...].astype(F32)
        zp = cgp_ref[...].astype(F32) * hcp_ref[...].astype(F32) * live
        y_ref[...] = (bg_ref[...].astype(F32) * _conv3(z, zp, w_ref)).astype(y_ref.dtype)

    cur = lambda b: pl.BlockSpec((t, c), lambda i: (i, b))
    prev = lambda b: pl.BlockSpec((HALO, c), lambda i: (jnp.maximum(i * r - 1, 0), b))
    return pl.pallas_call(
        body, name=name, grid=(s // t,),
        in_specs=[cur(b_bg), cur(b_cg), cur(b_hc), prev(b_cg), prev(b_hc), pl.BlockSpec((8, c), lambda i: (0, 0))],
        out_specs=pl.BlockSpec((t, c), lambda i: (i, 0)),
        out_shape=jax.ShapeDtypeStruct((s, c), BF16),
        compiler_params=_params(("parallel",)),
    )(h, h, h, h, h, w)


def _sconv_bwd(h, w, dy, name):
    s = h.shape[0]
    t = _tile(s, 512)
    n = s // t
    r = t // HALO
    nh = s // HALO
    c = D_CONV
    b_bg, b_cg, b_hc = OFF_BG // c, OFF_CG // c, OFF_HC // c

    def body(bg_ref, cg_ref, hc_ref, cgp_ref, hcp_ref, bgn_ref, dy_ref, dyn_ref, w_ref, d_ref, dw_ref, acc_ref):
        i = pl.program_id(0)
        has_prev = (i > 0).astype(F32)
        has_next = (i < n - 1).astype(F32)
        bg = bg_ref[...].astype(F32)
        cg = cg_ref[...].astype(F32)
        hc = hc_ref[...].astype(F32)
        dyv = dy_ref[...].astype(F32)
        z = cg * hc
        zp = cgp_ref[...].astype(F32) * hcp_ref[...].astype(F32) * has_prev
        z1 = _shift_down(z, zp, 1)
        z2 = _shift_down(z, zp, 2)
        cz = w_ref[2:3, :] * z + w_ref[1:2, :] * z1 + w_ref[0:1, :] * z2
        dcz = dyv * bg
        dczn = dyn_ref[...].astype(F32) * bgn_ref[...].astype(F32) * has_next
        dz = (w_ref[2:3, :] * dcz + w_ref[1:2, :] * _shift_up(dcz, dczn, 1)
              + w_ref[0:1, :] * _shift_up(dcz, dczn, 2))
        d_ref[:, 0:c] = (dyv * cz).astype(d_ref.dtype)
        d_ref[:, c:2 * c] = (dz * hc).astype(d_ref.dtype)
        d_ref[:, 2 * c:3 * c] = (dz * cg).astype(d_ref.dtype)

        @pl.when(i == 0)
        def _():
            acc_ref[...] = jnp.zeros_like(acc_ref)

        acc_ref[0] += _row_sum8(dcz * z2)
        acc_ref[1] += _row_sum8(dcz * z1)
        acc_ref[2] += _row_sum8(dcz * z)

        @pl.when(i == n - 1)
        def _():
            rows = [jnp.sum(acc_ref[k], axis=0, keepdims=True) for k in range(3)]
            dw_ref[...] = jnp.concatenate(rows + [jnp.zeros((5, c), F32)], axis=0)

    cur = lambda b: pl.BlockSpec((t, c), lambda i: (i, b))
    prev = lambda b: pl.BlockSpec((HALO, c), lambda i: (jnp.maximum(i * r - 1, 0), b))
    nxt = lambda b: pl.BlockSpec((HALO, c), lambda i: (jnp.minimum((i + 1) * r, nh - 1), b))
    return pl.pallas_call(
        body, name=name, grid=(n,),
        in_specs=[cur(b_bg), cur(b_cg), cur(b_hc), prev(b_cg), prev(b_hc), nxt(b_bg),
                  cur(0), nxt(0), pl.BlockSpec((8, c), lambda i: (0, 0))],
        out_specs=[pl.BlockSpec((t, 3 * c), lambda i: (i, 0)), pl.BlockSpec((8, c), lambda i: (0, 0))],
        out_shape=[jax.ShapeDtypeStruct((s, 3 * c), BF16), jax.ShapeDtypeStruct((8, c), F32)],
        scratch_shapes=[pltpu.VMEM((3, 8, c), F32)],
        compiler_params=_params(("arbitrary",)),
    )(h, h, h, h, h, h, dy, dy, w)


def _group_masks():
    lane = lax.broadcasted_iota(jnp.int32, (1, D_SGU), 1)
    return [(lane >= g * HEAD_DIM) & (lane < (g + 1) * HEAD_DIM) for g in range(N_GROUPS)]


def _tril_weights(w_ref):
    r = lax.broadcasted_iota(jnp.int32, (CHUNK, CHUNK), 0)
    c = lax.broadcasted_iota(jnp.int32, (CHUNK, CHUNK), 1)
    return [jnp.where(r >= c, w_ref[g], 0.0).astype(MXU_DTYPE) for g in range(N_GROUPS)]


def _sgu_ln(vs, g_ref, b_ref):
    vg, dvg = _gelu_and_grad(vs)
    mu = jnp.mean(vg, axis=-1, keepdims=True)
    xc = vg - mu
    rstd = lax.rsqrt(jnp.mean(xc * xc, axis=-1, keepdims=True) + LN_EPS)
    xhat = xc * rstd
    return xhat * g_ref[...] + b_ref[...], xhat, rstd, dvg


def _sgu_fwd(h, ln_g, ln_b, w_s, bias, name):
    s = h.shape[0]
    t = _tile(s, 512)
    c = D_SGU
    b_u, b_v = OFF_U // c, OFF_VS // c

    def body(u_ref, v_ref, g_ref, b_ref, w_ref, bias_ref, y_ref):
        gm = _group_masks()
        wm = _tril_weights(w_ref)
        ug = _gelu(u_ref[...].astype(F32))
        vn, _, _, _ = _sgu_ln(v_ref[...].astype(F32), g_ref, b_ref)
        vnb = vn.astype(MXU_DTYPE)
        for ch in range(t // CHUNK):
            rows = slice(ch * CHUNK, (ch + 1) * CHUNK)
            mixed = bias_ref[...]
            for g in range(N_GROUPS):
                mg = lax.dot_general(wm[g], vnb[rows], _DIMS["nn"], preferred_element_type=F32)
                mixed = jnp.where(gm[g], mixed + mg, mixed)
            y_ref[rows, :] = (ug[rows] * mixed).astype(y_ref.dtype)

    full = lambda shp: pl.BlockSpec(shp, lambda i: (0,) * len(shp))
    return pl.pallas_call(
        body, name=name, grid=(s // t,),
        in_specs=[pl.BlockSpec((t, c), lambda i: (i, b_u)), pl.BlockSpec((t, c), lambda i: (i, b_v)),
                  full((1, c)), full((1, c)), full((N_GROUPS, CHUNK, CHUNK)), full((CHUNK, c))],
        out_specs=pl.BlockSpec((t, c), lambda i: (i, 0)),
        out_shape=jax.ShapeDtypeStruct((s, c), BF16),
        compiler_params=_params(("parallel",)),
    )(h, h, ln_g, ln_b, w_s, bias)


def _sgu_bwd(h, ln_g, ln_b, w_s, bias, dy, name):
    s = h.shape[0]
    t = _tile(s, 512)
    n = s // t
    c = D_SGU
    b_u, b_v = OFF_U // c, OFF_VS // c

    def body(u_ref, v_ref, g_ref, b_ref, w_ref, bias_ref, dy_ref,
             d_ref, dg_ref, db_ref, dw_ref, dbias_ref, dg_acc, db_acc):
        i = pl.program_id(0)
        gm = _group_masks()
        wm = _tril_weights(w_ref)

        @pl.when(i == 0)
        def _():
            dg_acc[...] = jnp.zeros_like(dg_acc)
            db_acc[...] = jnp.zeros_like(db_acc)
            dw_ref[...] = jnp.zeros_like(dw_ref)
            dbias_ref[...] = jnp.zeros_like(dbias_ref)

        ug, dug = _gelu_and_grad(u_ref[...].astype(F32))
        vn, xhat, rstd, dvg = _sgu_ln(v_ref[...].astype(F32), g_ref, b_ref)
        vnb = vn.astype(MXU_DTYPE)
        dyv = dy_ref[...].astype(F32)
        dmixed = dyv * ug
        dmb = dmixed.astype(MXU_DTYPE)
        dvn_parts = []
        for ch in range(t // CHUNK):
            rows = slice(ch * CHUNK, (ch + 1) * CHUNK)
            mixed = bias_ref[...]
            dvn = jnp.zeros((CHUNK, c), F32)
            for g in range(N_GROUPS):
                mg = lax.dot_general(wm[g], vnb[rows], _DIMS["nn"], preferred_element_type=F32)
                mixed = jnp.where(gm[g], mixed + mg, mixed)
                dvn = jnp.where(gm[g], lax.dot_general(wm[g], dmb[rows], _DIMS["tn"], preferred_element_type=F32),
                                dvn)
                dmg = jnp.where(gm[g], dmb[rows], jnp.zeros_like(dmb[rows]))
                dw_ref[g] += lax.dot_general(dmg, vnb[rows], _DIMS["nt"], preferred_element_type=F32)
            d_ref[rows, 0:c] = (dyv[rows] * mixed * dug[rows]).astype(d_ref.dtype)
            dbias_ref[...] += dmixed[rows]
            dvn_parts.append(dvn)
        dvn = jnp.concatenate(dvn_parts, axis=0)
        dg_acc[...] += _row_sum8(dvn * xhat)
        db_acc[...] += _row_sum8(dvn)
        dxh = dvn * g_ref[...]
        dvgl = rstd * (dxh - jnp.mean(dxh, axis=-1, keepdims=True)
                       - xhat * jnp.mean(dxh * xhat, axis=-1, keepdims=True))
        d_ref[:, c:2 * c] = (dvgl * dvg).astype(d_ref.dtype)

        @pl.when(i == n - 1)
        def _():
            dg_ref[...] = jnp.sum(dg_acc[...], axis=0, keepdims=True)
            db_ref[...] = jnp.sum(db_acc[...], axis=0, keepdims=True)
            r = lax.broadcasted_iota(jnp.int32, (CHUNK, CHUNK), 0)
            cc = lax.broadcasted_iota(jnp.int32, (CHUNK, CHUNK), 1)
            for g in range(N_GROUPS):
                dw_ref[g] = jnp.where(r >= cc, dw_ref[g], 0.0)

    full = lambda shp: pl.BlockSpec(shp, lambda i: (0,) * len(shp))
    return pl.pallas_call(
        body, name=name, grid=(n,),
        in_specs=[pl.BlockSpec((t, c), lambda i: (i, b_u)), pl.BlockSpec((t, c), lambda i: (i, b_v)),
                  full((1, c)), full((1, c)), full((N_GROUPS, CHUNK, CHUNK)), full((CHUNK, c)),
                  pl.BlockSpec((t, c), lambda i: (i, 0))],
        out_specs=[pl.BlockSpec((t, 2 * c), lambda i: (i, 0)), full((1, c)), full((1, c)),
                   full((N_GROUPS, CHUNK, CHUNK)), full((CHUNK, c))],
        out_shape=[jax.ShapeDtypeStruct((s, 2 * c), BF16), jax.ShapeDtypeStruct((1, c), F32),
                   jax.ShapeDtypeStruct((1, c), F32), jax.ShapeDtypeStruct((N_GROUPS, CHUNK, CHUNK), F32),
                   jax.ShapeDtypeStruct((CHUNK, c), F32)],
        scratch_shapes=[pltpu.VMEM((8, c), F32), pltpu.VMEM((8, c), F32)],
        compiler_params=_params(("arbitrary",)),
    )(h, h, ln_g, ln_b, w_s, bias, dy)


def _merge_fwd(h, acts, ws, b_gate, name):
    s = h.shape[0]
    d = D_MODEL
    t = _tile(s, 512)

    def body(gl0, gl1, gl2, a0, a1, a2, w0, w1, w2, b_ref, o_ref):
        acc = jnp.zeros((t, d), F32)
        for i, (gl, a, w) in enumerate(((gl0, a0, w0), (gl1, a1, w1), (gl2, a2, w2))):
            y = lax.dot_general(a[...], w[...], _DIMS["nn"], preferred_element_type=F32)
            acc = acc + _sigmoid(gl[...].astype(F32) + b_ref[i:i + 1, :]) * y
        o_ref[...] = acc.astype(o_ref.dtype)

    full = lambda arr: pl.BlockSpec(arr.shape, lambda i: (0, 0))
    return pl.pallas_call(
        body, name=name, grid=(s // t,),
        in_specs=[pl.BlockSpec((t, d), lambda i, b=b: (i, b)) for b in range(3)]
                 + [pl.BlockSpec((t, a.shape[1]), lambda i: (i, 0)) for a in acts]
                 + [full(w) for w in ws] + [full(b_gate)],
        out_specs=pl.BlockSpec((t, d), lambda i: (i, 0)),
        out_shape=jax.ShapeDtypeStruct((s, d), BF16),
        compiler_params=_params(("parallel",)),
    )(h, h, h, *acts, *ws, b_gate)


def _merge_bwd(h, acts, ws, b_gate, dmerged, name):
    s = h.shape[0]
    d = D_MODEL
    t = _tile(s, 512)
    n = s // t

    def body(gl0, gl1, gl2, a0, a1, a2, w0, w1, w2, b_ref, dm_ref, dy0, dy1, dy2, dgl_ref, db_ref, acc_ref):
        step = pl.program_id(0)

        @pl.when(step == 0)
        def _():
            acc_ref[...] = jnp.zeros_like(acc_ref)

        dm = dm_ref[...]
        for i, (gl, a, w, dy) in enumerate(((gl0, a0, w0, dy0), (gl1, a1, w1, dy1), (gl2, a2, w2, dy2))):
            y = lax.dot_general(a[...], w[...], _DIMS["nn"], preferred_element_type=F32)
            gate = _sigmoid(gl[...].astype(F32) + b_ref[i:i + 1, :])
            dy[...] = (dm * gate).astype(dy.dtype)
            dgl = dm * y * (gate * (1.0 - gate))
            dgl_ref[:, i * d:(i + 1) * d] = dgl.astype(dgl_ref.dtype)
            acc_ref[i] += _row_sum8(dgl)

        @pl.when(step == n - 1)
        def _():
            rows = [jnp.sum(acc_ref[k], axis=0, keepdims=True) for k in range(3)]
            db_ref[...] = jnp.concatenate(rows + [jnp.zeros((5, d), F32)], axis=0)

    full = lambda arr: pl.BlockSpec(arr.shape, lambda i: (0, 0))
    row = pl.BlockSpec((t, d), lambda i: (i, 0))
    return pl.pallas_call(
        body, name=name, grid=(n,),
        in_specs=[pl.BlockSpec((t, d), lambda i, b=b: (i, b)) for b in range(3)]
                 + [pl.BlockSpec((t, a.shape[1]), lambda i: (i, 0)) for a in acts]
                 + [full(w) for w in ws] + [full(b_gate), row],
        out_specs=[row, row, row, pl.BlockSpec((t, 3 * d), lambda i: (i, 0)), pl.BlockSpec((8, d), lambda i: (0, 0))],
        out_shape=[jax.ShapeDtypeStruct((s, d), BF16)] * 3
                  + [jax.ShapeDtypeStruct((s, 3 * d), BF16), jax.ShapeDtypeStruct((8, d), F32)],
        scratch_shapes=[pltpu.VMEM((3, 8, d), F32)],
        compiler_params=_params(("arbitrary",)),
    )(h, h, h, *acts, *ws, b_gate, dmerged)


FF_BLK = D_FF // 2


def _ffn_act_fwd(h2, w, name):
    s = h2.shape[0]
    t = _tile(s, 512)
    r = t // HALO
    cw = 2 * FF_BLK

    def body(x_ref, xp_ref, w_ref, p_ref):
        i = pl.program_id(0)
        live = (i > 0).astype(F32)
        hc = _conv3(x_ref[...].astype(F32), xp_ref[...].astype(F32) * live, w_ref)
        p_ref[...] = (_gelu(hc[:, :FF_BLK]) * hc[:, FF_BLK:]).astype(p_ref.dtype)

    return pl.pallas_call(
        body, name=name, grid=(s // t, 2),
        in_specs=[pl.BlockSpec((t, cw), lambda i, j: (i, j)),
                  pl.BlockSpec((HALO, cw), lambda i, j: (jnp.maximum(i * r - 1, 0), j)),
                  pl.BlockSpec((8, cw), lambda i, j: (0, j))],
        out_specs=pl.BlockSpec((t, FF_BLK), lambda i, j: (i, j)),
        out_shape=jax.ShapeDtypeStruct((s, D_FF), BF16),
        compiler_params=_params(("parallel", "parallel")),
    )(h2, h2, w)


def _ffn_act_conv_bwd(h2, w, dp, name):
    s = h2.shape[0]
    t = _tile(s, 512)
    n = s // t
    r = t // HALO
    nh = s // HALO
    cw = 2 * FF_BLK

    def body(x_ref, xp_ref, xn_ref, dp_ref, dpn_ref, w_ref, dx_ref, dw_ref, acc_ref):
        i = pl.program_id(1)
        has_prev = (i > 0).astype(F32)
        has_next = (i < n - 1).astype(F32)
        x = jnp.concatenate([x_ref[...].astype(F32), xn_ref[...].astype(F32)], axis=0)
        xp = xp_ref[...].astype(F32) * has_prev
        x1 = _shift_down(x, xp, 1)
        x2 = _shift_down(x, xp, 2)
        hc = w_ref[2:3, :] * x + w_ref[1:2, :] * x1 + w_ref[0:1, :] * x2
        ga, dga = _gelu_and_grad(hc[:, :FF_BLK])
        dpv = jnp.concatenate([dp_ref[...].astype(F32), dpn_ref[...].astype(F32) * has_next], axis=0)
        dhc = jnp.concatenate([dpv * hc[:, FF_BLK:] * dga, dpv * ga], axis=1)
        cur, nxt = dhc[:t], dhc[t:]
        dx = w_ref[2:3, :] * cur + w_ref[1:2, :] * _shift_up(cur, nxt, 1) + w_ref[0:1, :] * _shift_up(cur, nxt, 2)
        dx_ref[...] = dx.astype(dx_ref.dtype)

        @pl.when(i == 0)
        def _():
            acc_ref[...] = jnp.zeros_like(acc_ref)

        acc_ref[0] += _row_sum8(cur * x2[:t])
        acc_ref[1] += _row_sum8(cur * x1[:t])
        acc_ref[2] += _row_sum8(cur * x[:t])

        @pl.when(i == n - 1)
        def _():
            rows = [jnp.sum(acc_ref[k], axis=0, keepdims=True) for k in range(3)]
            dw_ref[...] = jnp.concatenate(rows + [jnp.zeros((5, cw), F32)], axis=0)

    nxt_row = lambda j, i: jnp.minimum((i + 1) * r, nh - 1)
    return pl.pallas_call(
        body, name=name, grid=(2, n),
        in_specs=[pl.BlockSpec((t, cw), lambda j, i: (i, j)),
                  pl.BlockSpec((HALO, cw), lambda j, i: (jnp.maximum(i * r - 1, 0), j)),
                  pl.BlockSpec((HALO, cw), lambda j, i: (nxt_row(j, i), j)),
                  pl.BlockSpec((t, FF_BLK), lambda j, i: (i, j)),
                  pl.BlockSpec((HALO, FF_BLK), lambda j, i: (nxt_row(j, i), j)),
                  pl.BlockSpec((8, cw), lambda j, i: (0, j))],
        out_specs=[pl.BlockSpec((t, cw), lambda j, i: (i, j)), pl.BlockSpec((8, cw), lambda j, i: (0, j))],
        out_shape=[jax.ShapeDtypeStruct((s, 2 * D_FF), BF16), jax.ShapeDtypeStruct((8, 2 * D_FF), F32)],
        scratch_shapes=[pltpu.VMEM((3, 8, cw), F32)],
        compiler_params=_params(("parallel", "arbitrary")),
    )(h2, h2, h2, dp, dp, w)


def _adamw(w, g, m, v, name):
    shape = w.shape
    c = shape[-1]
    rows = math.prod(shape[:-1])
    to2d = lambda a: a.reshape(rows, c)
    cap = max(8, (1 << 18) // c)
    tr = rows
    for cand in (2048, 1024, 512, 256, 128, 64, 32, 16, 8):
        if cand <= cap and rows % cand == 0:
            tr = cand
            break

    def body(w_ref, g_ref, m_ref, v_ref, d_ref, nm_ref, nv_ref):
        gv = g_ref[...]
        nm = ADAM_B1 * m_ref[...] + (1.0 - ADAM_B1) * gv
        nv = ADAM_B2 * v_ref[...] + (1.0 - ADAM_B2) * (gv * gv)
        m_hat = nm / (1.0 - ADAM_B1 ** ADAM_STEP)
        v_hat = nv / (1.0 - ADAM_B2 ** ADAM_STEP)
        d_ref[...] = -ADAM_LR * (m_hat / (jnp.sqrt(v_hat) + ADAM_EPS) + ADAM_WD * w_ref[...])
        nm_ref[...] = nm
        nv_ref[...] = nv

    blk = pl.BlockSpec((tr, c), lambda i: (i, 0))
    outs = pl.pallas_call(
        body, name=name, grid=(rows // tr,),
        in_specs=[blk] * 4, out_specs=[blk] * 3,
        out_shape=[jax.ShapeDtypeStruct((rows, c), F32)] * 3,
        compiler_params=_params(("parallel",)),
    )(to2d(w), to2d(g), to2d(m), to2d(v))
    return tuple(o.reshape(shape) for o in outs)


_ANY = pl.BlockSpec(memory_space=pl.ANY)


def _place():
    x, y, c = lax.axis_index("x"), lax.axis_index("y"), lax.axis_index("c")
    others = [(1 - x, y), (x, 1 - y), (1 - x, 1 - y)]
    return x, y, c, others


def _all_gather_chips(shard, name):
    rws, cols = shard.shape
    half = rws // 2

    def body(x_ref, out_ref, send_sems, recv_sems, local_sem):
        x, y, c, others = _place()
        me = 2 * x + y
        sib = (x, y, 1 - c)

        def rows(chip, cc):
            return out_ref.at[chip, pl.ds(pl.multiple_of(cc * half, 16), half), :]

        def copy(k, src, dst, to):
            return pltpu.make_async_remote_copy(src_ref=src, dst_ref=dst, send_sem=send_sems.at[k],
                                                recv_sem=recv_sems.at[k], device_id=to, device_id_type=MESH)

        mine = pltpu.make_async_copy(x_ref, out_ref.at[me], local_sem)
        mine.start()
        my_half = x_ref.at[pl.ds(pl.multiple_of(c * half, 16), half), :]
        first = [copy(j, my_half, rows(me, c), (ox, oy, c)) for j, (ox, oy) in enumerate(others)]
        for cp in first:
            cp.start()
        passed = []
        for j, (ox, oy) in enumerate(others):
            blk = rows(2 * ox + oy, c)
            copy(j, blk, blk, (x, y, c)).wait_recv()
            fwd = copy(3 + j, blk, blk, sib)
            fwd.start()
            passed.append(fwd)
        for j, (ox, oy) in enumerate(others):
            blk = rows(2 * ox + oy, 1 - c)
            copy(3 + j, blk, blk, (x, y, c)).wait_recv()
        for cp in first + passed:
            cp.wait_send()
        mine.wait()

    return pl.pallas_call(
        body, name=name,
        in_specs=[_ANY], out_specs=_ANY,
        out_shape=jax.ShapeDtypeStruct((N_CHIPS, rws, cols), shard.dtype),
        scratch_shapes=[pltpu.SemaphoreType.DMA((6,)), pltpu.SemaphoreType.DMA((6,)), pltpu.SemaphoreType.DMA],
        compiler_params=pltpu.CompilerParams(has_side_effects=True),
    )(shard)


def _swap_halves(buf, name):
    nb, rws, cols = buf.shape
    half = rws // 2

    def body(b_ref, own_ref, sib_ref, send_sem, recv_sem, local_sem):
        x, y, c, _ = _place()
        keep = b_ref.at[:, pl.ds(pl.multiple_of(c * half, 16), half), :]
        give = b_ref.at[:, pl.ds(pl.multiple_of((1 - c) * half, 16), half), :]
        mine = pltpu.make_async_copy(keep, own_ref, local_sem)
        mine.start()
        cp = pltpu.make_async_remote_copy(src_ref=give, dst_ref=sib_ref, send_sem=send_sem, recv_sem=recv_sem,
                                          device_id=(x, y, 1 - c), device_id_type=MESH)
        cp.start()
        cp.wait()
        mine.wait()

    shp = jax.ShapeDtypeStruct((nb, half, cols), buf.dtype)
    return pl.pallas_call(
        body, name=name,
        in_specs=[_ANY], out_specs=[_ANY, _ANY], out_shape=[shp, shp],
        scratch_shapes=[pltpu.SemaphoreType.DMA, pltpu.SemaphoreType.DMA, pltpu.SemaphoreType.DMA],
        compiler_params=pltpu.CompilerParams(has_side_effects=True),
    )(buf)


def _add2(a, b, name):
    nb, rws, cols = a.shape
    t = _tile(rws, 256)
    if rws % t:
        t = rws

    def body(a_ref, b_ref, o_ref):
        o_ref[...] = (a_ref[...].astype(F32) + b_ref[...].astype(F32)).astype(o_ref.dtype)

    blk = pl.BlockSpec((1, t, cols), lambda i, j: (i, j, 0))
    return pl.pallas_call(
        body, name=name, grid=(nb, rws // t), in_specs=[blk, blk], out_specs=blk,
        out_shape=jax.ShapeDtypeStruct(a.shape, a.dtype),
        compiler_params=_params(("parallel", "parallel")),
    )(a, b)


def _exchange_chips(pre, name):
    nb, half, cols = pre.shape

    def body(p_ref, out_ref, send_sems, recv_sems, local_sem):
        x, y, c, others = _place()
        me = 2 * x + y
        mine = pltpu.make_async_copy(p_ref.at[me], out_ref.at[me], local_sem)
        mine.start()
        sends = []
        for j, (ox, oy) in enumerate(others):
            cp = pltpu.make_async_remote_copy(src_ref=p_ref.at[2 * ox + oy], dst_ref=out_ref.at[me],
                                              send_sem=send_sems.at[j], recv_sem=recv_sems.at[j],
                                              device_id=(ox, oy, c), device_id_type=MESH)
            cp.start()
            sends.append(cp)
        for j, (ox, oy) in enumerate(others):
            blk = out_ref.at[2 * ox + oy]
            pltpu.make_async_remote_copy(src_ref=blk, dst_ref=blk, send_sem=send_sems.at[j],
                                         recv_sem=recv_sems.at[j], device_id=(x, y, c),
                                         device_id_type=MESH).wait_recv()
        for cp in sends:
            cp.wait_send()
        mine.wait()

    return pl.pallas_call(
        body, name=name,
        in_specs=[_ANY], out_specs=_ANY, out_shape=jax.ShapeDtypeStruct(pre.shape, pre.dtype),
        scratch_shapes=[pltpu.SemaphoreType.DMA((3,)), pltpu.SemaphoreType.DMA((3,)), pltpu.SemaphoreType.DMA],
        compiler_params=pltpu.CompilerParams(has_side_effects=True),
    )(pre)


def _add4(parts, name):
    nb, half, cols = parts.shape
    t = _tile(half, 256)
    if half % t:
        t = half

    def body(p_ref, o_ref):
        acc = p_ref[0].astype(F32)
        for k in range(1, nb):
            acc = acc + p_ref[k].astype(F32)
        o_ref[...] = acc

    return pl.pallas_call(
        body, name=name, grid=(half // t,),
        in_specs=[pl.BlockSpec((nb, t, cols), lambda i: (0, i, 0))],
        out_specs=pl.BlockSpec((t, cols), lambda i: (i, 0)),
        out_shape=jax.ShapeDtypeStruct((half, cols), F32),
        compiler_params=_params(("parallel",)),
    )(parts)


def _join_halves(mine_half, name):
    half, cols = mine_half.shape

    def body(h_ref, out_ref, send_sem, recv_sem, local_sem):
        x, y, c, _ = _place()
        dst = out_ref.at[pl.ds(pl.multiple_of(c * half, 8), half), :]
        mine = pltpu.make_async_copy(h_ref, dst, local_sem)
        mine.start()
        cp = pltpu.make_async_remote_copy(src_ref=h_ref, dst_ref=dst, send_sem=send_sem, recv_sem=recv_sem,
                                          device_id=(x, y, 1 - c), device_id_type=MESH)
        cp.start()
        cp.wait()
        mine.wait()

    return pl.pallas_call(
        body, name=name,
        in_specs=[_ANY], out_specs=_ANY, out_shape=jax.ShapeDtypeStruct((2 * half, cols), mine_half.dtype),
        scratch_shapes=[pltpu.SemaphoreType.DMA, pltpu.SemaphoreType.DMA, pltpu.SemaphoreType.DMA],
        compiler_params=pltpu.CompilerParams(has_side_effects=True),
    )(mine_half)


def _reduce_scatter_chips(buf, tag):
    own, sib = _swap_halves(buf, "rs_swap_" + tag)
    pre = _add2(own, sib, "rs_add2_" + tag)
    parts = _exchange_chips(pre, "rs_xchg_" + tag)
    red = _add4(parts, "rs_add4_" + tag)
    return _join_halves(red, "rs_join_" + tag)


MAX_DMA_BYTES = 2 * 1024 * 1024
ROW_ALIGN = 16


def _pieces(rows, row_bytes):
    n = max(1, -(-(rows * row_bytes) // MAX_DMA_BYTES))
    step = -(-(-(-rows // n)) // ROW_ALIGN) * ROW_ALIGN
    return [(r, min(step, rows - r)) for r in range(0, rows, step)]


def _half_plan(arrays, row_axis):
    plan = []
    for a, arr in enumerate(arrays):
        row_bytes = math.prod(arr.shape[row_axis + 1:]) * arr.dtype.itemsize * (arr.shape[0] if row_axis else 1)
        plan += [(a, r0, nr) for r0, nr in _pieces(arr.shape[row_axis] // 2, row_bytes)]
    return plan


def _rows(start, size):
    return pl.ds(pl.multiple_of(start, ROW_ALIGN), size)


def _remote(src, dst, send_sems, recv_sems, k, to):
    return pltpu.make_async_remote_copy(src_ref=src, dst_ref=dst, send_sem=send_sems.at[k], recv_sem=recv_sems.at[k],
                                        device_id=to, device_id_type=MESH)


def _comm_call(body, name, ins, out_shapes, n_remote, n_local, aliases=None):
    return pl.pallas_call(
        body, name=name,
        in_specs=[_ANY] * len(ins), out_specs=[_ANY] * len(out_shapes), out_shape=out_shapes,
        scratch_shapes=[pltpu.SemaphoreType.DMA((n_remote,)), pltpu.SemaphoreType.DMA((n_remote,)),
                        pltpu.SemaphoreType.DMA((max(n_local, 1),))],
        input_output_aliases=aliases or {},
        compiler_params=pltpu.CompilerParams(has_side_effects=True),
    )(*ins)


def _cast_shard(w, l, me_idx, name):
    _, k, cols = w.shape
    tr = _tile(k, 256)
    if k % tr:
        tr = k

    def body(me_ref, w_ref, s_ref, land_ref):
        del me_ref
        v = w_ref[...].astype(BF16)
        s_ref[...] = v
        land_ref[...] = v

    grid_spec = pltpu.PrefetchScalarGridSpec(
        num_scalar_prefetch=1, grid=(k // tr,),
        in_specs=[pl.BlockSpec((None, tr, cols), lambda i, me: (l, i, 0))],
        out_specs=[pl.BlockSpec((tr, cols), lambda i, me: (i, 0)),
                   pl.BlockSpec((None, tr, cols), lambda i, me: (me[0], i, 0))])
    return pl.pallas_call(
        body, name=name, grid_spec=grid_spec,
        out_shape=[jax.ShapeDtypeStruct((k, cols), BF16), jax.ShapeDtypeStruct((N_CHIPS, k, cols), BF16)],
        compiler_params=_params(("parallel",)),
    )(me_idx, w)


def _gather_d2d(lands, name):
    n = len(lands)
    plan = _half_plan(lands, 1)
    plan = [(a, r0, nr) for a, r0, nr in plan]

    def body(*refs):
        out_refs = refs[n:2 * n]
        send_sems, recv_sems, _ = refs[2 * n:]
        x, y, c, others = _place()
        sends = []
        for i, (a, r0, nr) in enumerate(plan):
            rows = _rows(c * (lands[a].shape[1] // 2) + r0, nr)
            for j, (ox, oy) in enumerate(others):
                blk = out_refs[a].at[2 * ox + oy, rows, :]
                cp = _remote(blk, blk, send_sems, recv_sems, 3 * i + j, (x, y, 1 - c))
                cp.start()
                sends.append(cp)
        for i, (a, r0, nr) in enumerate(plan):
            rows = _rows((1 - c) * (lands[a].shape[1] // 2) + r0, nr)
            for j, (ox, oy) in enumerate(others):
                blk = out_refs[a].at[2 * ox + oy, rows, :]
                _remote(blk, blk, send_sems, recv_sems, 3 * i + j, (x, y, c)).wait_recv()
        for cp in sends:
            cp.wait_send()

    outs = [jax.ShapeDtypeStruct(a.shape, a.dtype) for a in lands]
    return _comm_call(body, name, lands, outs, 3 * len(plan), 0, aliases={a: a for a in range(n)})


def _rs_swap(ts, name):
    n = len(ts)
    plan = _half_plan(ts, 1)

    def body(*refs):
        t_refs, out_refs = refs[:n], refs[n:2 * n]
        send_sems, recv_sems, _ = refs[2 * n:]
        x, y, c, _o = _place()
        sends = []
        for i, (a, r0, nr) in enumerate(plan):
            src = t_refs[a].at[:, _rows((1 - c) * (ts[a].shape[1] // 2) + r0, nr), :]
            cp = _remote(src, out_refs[a].at[:, pl.ds(r0, nr), :], send_sems, recv_sems, i, (x, y, 1 - c))
            cp.start()
            sends.append(cp)
        for i, (a, r0, nr) in enumerate(plan):
            blk = out_refs[a].at[:, pl.ds(r0, nr), :]
            _remote(blk, blk, send_sems, recv_sems, i, (x, y, c)).wait_recv()
        for cp in sends:
            cp.wait_send()

    outs = [jax.ShapeDtypeStruct((t.shape[0], t.shape[1] // 2, t.shape[2]), t.dtype) for t in ts]
    return _comm_call(body, name, ts, outs, len(plan), 0)


def _add_half(t, got, c_idx, me_idx, name):
    nb, k, cols = t.shape
    half = k // 2

    def body(c_ref, me_ref, t_ref, g_ref, o_ref, mine_ref):
        del c_ref
        v = (t_ref[...].astype(F32) + g_ref[...].astype(F32)).astype(o_ref.dtype)
        o_ref[...] = v

        @pl.when(pl.program_id(0) == me_ref[0])
        def _():
            mine_ref[...] = v

    blk = pl.BlockSpec((1, half, cols), lambda i, c, me: (i, 0, 0))
    grid_spec = pltpu.PrefetchScalarGridSpec(
        num_scalar_prefetch=2, grid=(nb,),
        in_specs=[pl.BlockSpec((1, half, cols), lambda i, c, me: (i, c[0], 0)), blk],
        out_specs=[blk, pl.BlockSpec((1, half, cols), lambda i, c, me: (me[0], 0, 0))])
    shp = jax.ShapeDtypeStruct(got.shape, got.dtype)
    return pl.pallas_call(
        body, name=name, grid_spec=grid_spec, out_shape=[shp, shp],
        compiler_params=_params(("arbitrary",)),
    )(c_idx, me_idx, t, got)


def _add4_half(parts, c_idx, name):
    nb, half, cols = parts.shape
    t = _tile(half, 256)
    if half % t:
        t = half
    steps = half // t

    def body(c_ref, p_ref, o_ref):
        del c_ref
        acc = p_ref[0].astype(F32)
        for k in range(1, nb):
            acc = acc + p_ref[k].astype(F32)
        o_ref[...] = acc

    grid_spec = pltpu.PrefetchScalarGridSpec(
        num_scalar_prefetch=1, grid=(steps,),
        in_specs=[pl.BlockSpec((nb, t, cols), lambda i, c: (0, i, 0))],
        out_specs=pl.BlockSpec((t, cols), lambda i, c: (c[0] * steps + i, 0)))
    return pl.pallas_call(
        body, name=name, grid_spec=grid_spec, out_shape=jax.ShapeDtypeStruct((2 * half, cols), F32),
        compiler_params=_params(("parallel",)),
    )(c_idx, parts)


def _rs_join(fulls, name):
    n = len(fulls)
    plan = _half_plan(fulls, 0)

    def body(*refs):
        out_refs = refs[n:2 * n]
        send_sems, recv_sems, _ = refs[2 * n:]
        x, y, c, _o = _place()
        sends = []
        for i, (a, r0, nr) in enumerate(plan):
            blk = out_refs[a].at[_rows(c * (fulls[a].shape[0] // 2) + r0, nr), :]
            cp = _remote(blk, blk, send_sems, recv_sems, i, (x, y, 1 - c))
            cp.start()
            sends.append(cp)
        for i, (a, r0, nr) in enumerate(plan):
            blk = out_refs[a].at[_rows((1 - c) * (fulls[a].shape[0] // 2) + r0, nr), :]
            _remote(blk, blk, send_sems, recv_sems, i, (x, y, c)).wait_recv()
        for cp in sends:
            cp.wait_send()

    outs = [jax.ShapeDtypeStruct(f.shape, f.dtype) for f in fulls]
    return _comm_call(body, name, fulls, outs, len(plan), 0, aliases={a: a for a in range(n)})


_HBM = pl.BlockSpec(memory_space=pltpu.HBM)
_SEM = pl.BlockSpec(memory_space=pltpu.SEMAPHORE)
_EFFECT = pltpu.SideEffectType.DATAFLOW_SIDE_EFFECTING


def _ici_plan(kind, a_list):
    if kind == "gather":
        return _half_plan(a_list, 0)
    plan = []
    for a, p in enumerate(a_list):
        plan += [(a, r0, nr) for r0, nr in _pieces(p.shape[1], p.shape[2] * p.dtype.itemsize)]
    return plan


def _ici_refs(kind, a_ref, b_ref, a_shape, r0, nr, c, me, peer):
    if kind == "gather":
        rows = _rows(c * (a_shape[0] // 2) + r0, nr)
        return a_ref.at[rows, :], b_ref.at[me, rows, :], b_ref.at[peer, rows, :]
    rows = pl.ds(r0, nr)
    return a_ref.at[peer, rows, :], b_ref.at[me, rows, :], b_ref.at[peer, rows, :]


def _ici_start(kind, a_list, b_list, name):
    n = len(a_list)
    plan = _ici_plan(kind, a_list)
    shapes = [a.shape for a in a_list]

    def body(*refs):
        a_refs, b_refs = refs[:n], refs[n:2 * n]
        send_sems, recv_sems = refs[2 * n], refs[2 * n + 1]
        token = refs[4 * n + 2]
        x, y, c, others = _place()
        me = 2 * x + y
        for i, (a, r0, nr) in enumerate(plan):
            for j, (ox, oy) in enumerate(others):
                src, dst, _ = _ici_refs(kind, a_refs[a], b_refs[a], shapes[a], r0, nr, c, me, 2 * ox + oy)
                _remote(src, dst, send_sems, recv_sems, 3 * i + j, (ox, oy, c)).start()
        token[...] = jnp.zeros_like(token)

    hbm = lambda v: pltpu.HBM(v.shape, v.dtype)
    ncp = 3 * len(plan)
    outs = pl.pallas_call(
        body, name=name,
        in_specs=[_HBM] * (2 * n),
        out_specs=[_SEM, _SEM] + [_HBM] * (2 * n) + [pl.BlockSpec(memory_space=pltpu.VMEM)],
        out_shape=[pltpu.SemaphoreType.DMA((ncp,)), pltpu.SemaphoreType.DMA((ncp,))]
                  + [hbm(v) for v in a_list] + [hbm(v) for v in b_list] + [jax.ShapeDtypeStruct((8, LANES), F32)],
        input_output_aliases={i: 2 + i for i in range(2 * n)},
        compiler_params=pltpu.CompilerParams(has_side_effects=_EFFECT),
    )(*[pltpu.with_memory_space_constraint(v, pltpu.HBM) for v in list(a_list) + list(b_list)])
    return outs[0], outs[1], outs[2:2 + n], outs[2 + n:2 + 2 * n], outs[2 + 2 * n]


def _ici_wait(kind, started, after, name):
    send_sems, recv_sems, a_list, b_list, _ = started
    n = len(a_list)
    plan = _ici_plan(kind, a_list)
    shapes = [a.shape for a in a_list]

    def body(*refs):
        a_refs, b_refs = refs[:n], refs[n:2 * n]
        send_sems, recv_sems = refs[2 * n], refs[2 * n + 1]
        x, y, c, others = _place()
        me = 2 * x + y
        for i, (a, r0, nr) in enumerate(plan):
            for j, (ox, oy) in enumerate(others):
                src, dst, land = _ici_refs(kind, a_refs[a], b_refs[a], shapes[a], r0, nr, c, me, 2 * ox + oy)
                _remote(src, dst, send_sems, recv_sems, 3 * i + j, (ox, oy, c)).wait_send()
                _remote(land, land, send_sems, recv_sems, 3 * i + j, (x, y, c)).wait_recv()

    hbm = lambda v: pltpu.HBM(v.shape, v.dtype)
    outs = pl.pallas_call(
        body, name=name,
        in_specs=[_HBM] * (2 * n) + [_SEM, _SEM, _ANY],
        out_specs=[_HBM] * (2 * n),
        out_shape=[hbm(v) for v in a_list] + [hbm(v) for v in b_list],
        input_output_aliases={i: i for i in range(2 * n)},
        compiler_params=pltpu.CompilerParams(has_side_effects=_EFFECT),
    )(*a_list, *b_list, send_sems, recv_sems, after)
    return outs[n:]


def _rs_begin(ts, c_idx, me_idx, tag):
    got = _rs_swap(ts, "rs_swap_" + tag)
    pairs = [_add_half(t, g, c_idx, me_idx, f"rs_add2_{tag}_{a}") for a, (t, g) in enumerate(zip(ts, got))]
    return _ici_start("scatter", [p for p, _ in pairs], [m for _, m in pairs], "rs_xchg_start_" + tag)


def _rs_finish(started, after, c_idx, tag):
    parts = _ici_wait("scatter", started, after, "rs_xchg_wait_" + tag)
    fulls = [_add4_half(p, c_idx, f"rs_add4_{tag}_{a}") for a, p in enumerate(parts)]
    return _rs_join(fulls, "rs_join_" + tag)


def _pack_rows(pieces, rows, dtype):
    flat = jnp.concatenate([p.astype(dtype).reshape(-1) for p in pieces])
    return jnp.pad(flat, (0, rows * PACK_COLS - flat.shape[0])).reshape(rows, PACK_COLS)


def _unpack(flat, shapes):
    out, off = [], 0
    for shp in shapes:
        size = math.prod(shp)
        out.append(flat[off:off + size].reshape(shp))
        off += size
    return out


def _rows_for(n_elems, mult):
    rows = -(-n_elems // PACK_COLS)
    return -(-rows // mult) * mult


BIG_SHARDS = [("w_in", (D_MODEL, 1474)), ("w_branch_att", (D_ATT, 256)), ("w_branch_conv", (D_CONV, 256)),
              ("w_branch_sgu", (D_SGU, 256)), ("w_out", (256, D_MODEL)), ("w_ffn_up", (D_MODEL, FF_BLK)),
              ("w_ffn_down", (D_FF // N_CHIPS, D_MODEL))]
SMALL_SHARDS = [("b_gate", (3, 256)), ("conv_mix_w", (3, 64)), ("conv_ffn_w", (3, FF_BLK))]
REPLICATED = [("pre_mix_g", (D_MODEL,)), ("post_mix_g", (D_MODEL,)), ("pre_ffn_g", (D_MODEL,)),
              ("post_ffn_g", (D_MODEL,)), ("b_forget", (N_HEADS,)), ("sgu_ln_g", (D_SGU,)), ("sgu_ln_b", (D_SGU,)),
              ("sgu_w", (N_GROUPS, CHUNK, CHUNK)), ("sgu_b", (N_GROUPS, CHUNK))]
WEIGHT_ORDER = ["pre_mix_g", "post_mix_g", "pre_ffn_g", "post_ffn_g", "w_in", "b_forget", "b_gate", "conv_mix_w",
                "sgu_ln_g", "sgu_ln_b", "sgu_w", "sgu_b", "w_branch_att", "w_branch_conv", "w_branch_sgu", "w_out",
                "w_ffn_up", "conv_ffn_w", "w_ffn_down"]

_SMALL_ELEMS = sum(math.prod(s) for _, s in SMALL_SHARDS)
_REP_ELEMS = sum(math.prod(s) for _, s in REPLICATED)
_REP_QUARTER = -(-(DEPTH * _REP_ELEMS) // N_CHIPS)
SMALL_PARAM_ROWS = _rows_for(DEPTH * _SMALL_ELEMS, 32)
SMALL_ROWS = _rows_for(DEPTH * _SMALL_ELEMS + _REP_QUARTER, 32)
IN_WIDTH = 5896
IN_SHARD = IN_WIDTH // N_CHIPS
IN_SHARD_PAD = 1536
IN_PAD = 6144


def _gather_small(wts):
    shard = _pack_rows([wts[n] for n, _ in SMALL_SHARDS], SMALL_PARAM_ROWS, F32)
    full = _all_gather_chips(shard, "gather_small_params").reshape(N_CHIPS, -1)
    per_chip = [_unpack(full[j], [(DEPTH,) + s for _, s in SMALL_SHARDS]) for j in range(N_CHIPS)]
    return {n: jnp.concatenate([per_chip[j][i] for j in range(N_CHIPS)], axis=-1)
            for i, (n, _) in enumerate(SMALL_SHARDS)}


BIG_NAMES = [n for n, _ in BIG_SHARDS]
FIRST_NAMES = ["w_in"]
LATE_NAMES = BIG_NAMES[1:]


def _gather_begin(wts, l, me_idx, names, tag):
    cast = [_cast_shard(wts[n], l, me_idx, "cast_" + n) for n in names]
    return _ici_start("gather", [sh for sh, _ in cast], [ld for _, ld in cast], "gather_ici_start_" + tag)


def _gather_finish(started, after, names, tag):
    lands = _ici_wait("gather", started, after, "gather_ici_wait_" + tag)
    return dict(zip(names, _gather_d2d(lands, "gather_d2d_" + tag)))


def _pad_rows(a, rows):
    return jnp.pad(a, ((0, rows - a.shape[0]), (0, 0)))


def _whole_cols(land):
    return land.transpose(1, 0, 2).reshape(land.shape[1], -1)


_O_F = 3 * D_ATT
_O_B = _O_F + N_HEADS
_O_GL = _O_B + 3 * D_CONV + 2 * D_SGU


def _prep_first(wts, lands, small, l):
    w_in = _whole_cols(lands["w_in"])
    cf = small["conv_ffn_w"][l]
    blk = lambda a, j: a[:, j * FF_BLK:(j + 1) * FF_BLK]
    return {
        "w_p": jnp.concatenate([w_in[:, _O_GL:], w_in[:, :_O_F], w_in[:, _O_B:_O_GL]], axis=1),
        "w_in_bwd": jnp.concatenate([w_in[:, :_O_F], w_in[:, _O_B:], w_in[:, _O_F:_O_B],
                                     jnp.zeros((D_MODEL, IN_PAD - IN_WIDTH), BF16)], axis=1),
        "wf_t": _pad_rows(w_in[:, _O_F:_O_B].T, F_ROWS),
        "b_forget": _pad_rows(wts["b_forget"][l].reshape(N_HEADS, 1), F_ROWS),
        "b_gate": _pad_rows(small["b_gate"][l], 8),
        "conv_mix_w": _pad_rows(small["conv_mix_w"][l], 8),
        "conv_ffn_w": _pad_rows(jnp.concatenate([blk(cf, 0), blk(cf, 2), blk(cf, 1), blk(cf, 3)], axis=1), 8),
        "pre_mix_g": wts["pre_mix_g"][l].reshape(1, -1), "post_mix_g": wts["post_mix_g"][l].reshape(1, -1),
        "pre_ffn_g": wts["pre_ffn_g"][l].reshape(1, -1), "post_ffn_g": wts["post_ffn_g"][l].reshape(1, -1),
        "ln_g": wts["sgu_ln_g"][l].reshape(1, -1), "ln_b": wts["sgu_ln_b"][l].reshape(1, -1),
        "sgu_w": wts["sgu_w"][l],
        "sgu_bias": jnp.repeat(wts["sgu_b"][l].T, HEAD_DIM, axis=1),
    }


def _prep_late(lands):
    up = lands["w_ffn_up"]
    return {
        "w_att": _whole_cols(lands["w_branch_att"]), "w_conv": _whole_cols(lands["w_branch_conv"]),
        "w_sgu": _whole_cols(lands["w_branch_sgu"]),
        "w_out": lands["w_out"].reshape(D_MODEL, D_MODEL),
        "w_up": jnp.concatenate([up[0], up[2], up[1], up[3]], axis=1),
        "w_down": lands["w_ffn_down"].reshape(D_FF, D_MODEL),
    }


def _layer_fwd(x, p, dep=None, late=None):
    s = x.shape[0]
    xn = _rms_fwd(x, p["pre_mix_g"], "rms_pre_mix", dep)
    h = _mm(xn, p["w_p"], "nn", BF16, "mm_in", s, 256, D_MODEL)
    f_row = _mm(p["wf_t"], xn, "nt", F32, "mm_forget", F_ROWS, 2048, D_MODEL)
    ck = _gate_fwd(f_row, p["b_forget"], "gate_fwd")
    o, o_f32, lse = _attn_fwd(h, ck, "attn_fwd")
    yc = _sconv_fwd(h, p["conv_mix_w"], "sconv_fwd")
    ys = _sgu_fwd(h, p["ln_g"], p["ln_b"], p["sgu_w"], p["sgu_bias"], "sgu_fwd")
    if late is not None:
        p.update(late(o))
    merged = _merge_fwd(h, (o, yc, ys), (p["w_att"], p["w_conv"], p["w_sgu"]), p["b_gate"], "merge_fwd")
    mo = _mm(merged, p["w_out"], "nn", F32, "mm_out", 2048, 512, D_MODEL)
    x1 = _resid_post(x, mo, p["post_mix_g"], "post_mix")
    xn2 = _rms_fwd(x1, p["pre_ffn_g"], "rms_pre_ffn")
    h2 = _mm(xn2, p["w_up"], "nn", BF16, "mm_up", 2048, 512, D_MODEL)
    pact = _ffn_act_fwd(h2, p["conv_ffn_w"], "ffn_act_fwd")
    ff = _mm(pact, p["w_down"], "nn", F32, "mm_down", 2048, 512, FF_BLK)
    x2 = _resid_post(x1, ff, p["post_ffn_g"], "post_ffn")
    saved = dict(x=x, xn=xn, h=h, f_row=f_row, ck=ck, o=o, o_f32=o_f32, lse=lse, yc=yc, ys=ys, merged=merged, mo=mo, x1=x1,
                 xn2=xn2, h2=h2, pact=pact, ff=ff)
    return x2, saved


def _layer_bwd(dx2, p, sv, dep=None, early=None):
    s = dx2.shape[0]
    g = {}
    same = lambda b: b
    dff, g["post_ffn_g"] = _rms_bwd(sv["ff"], p["post_ffn_g"], [dx2], None, BF16, "post_ffn_bwd", dep)
    dpact = _mm(dff, p["w_down"], "nt", BF16, "mm_down_dx", 1024, FF_BLK, D_MODEL)
    t_down = _mm(sv["pact"], dff, "tn", BF16, "mm_down_dw", 256, D_MODEL, s).reshape(N_CHIPS, -1, D_MODEL)
    dh2, dconv_ffn = _ffn_act_conv_bwd(sv["h2"], p["conv_ffn_w"], dpact, "ffn_act_conv_bwd")
    dxn2 = _mm(dh2, p["w_up"], "nt", F32, "mm_up_dx", 1024, D_MODEL, FF_BLK)
    t_up = _mm(sv["xn2"], dh2, "tn", BF16, "mm_up_dw", 512, FF_BLK, s, chip_of=lambda b: (b % 2) * 2 + b // 2)
    dx1, g["pre_ffn_g"] = _rms_bwd(sv["x1"], p["pre_ffn_g"], [dxn2], dx2, F32, "pre_ffn_bwd")
    dep_mix = early([t_up, t_down]) if early is not None else None
    dmo, g["post_mix_g"] = _rms_bwd(sv["mo"], p["post_mix_g"], [dx1], None, BF16, "post_mix_bwd", dep_mix)
    dmerged = _mm(dmo, p["w_out"], "nt", F32, "mm_out_dx", 2048, 512, D_MODEL)
    t_out = _mm(sv["merged"], dmo, "tn", BF16, "mm_out_dw", 512, D_MODEL, s).reshape(N_CHIPS, -1, D_MODEL)
    acts = (sv["o"], sv["yc"], sv["ys"])
    ws = (p["w_att"], p["w_conv"], p["w_sgu"])
    dy_a, dy_c, dy_s, dgl, db_gate = _merge_bwd(sv["h"], acts, ws, p["b_gate"], dmerged, "merge_bwd")
    do = _mm(dy_a, p["w_att"], "nt", BF16, "mm_att_dx", 2048, D_ATT, D_MODEL)
    dyc = _mm(dy_c, p["w_conv"], "nt", BF16, "mm_conv_dx", 2048, D_CONV, D_MODEL)
    dys = _mm(dy_s, p["w_sgu"], "nt", BF16, "mm_sgu_dx", 2048, D_SGU, D_MODEL)
    t_att = _mm(sv["o"], dy_a, "tn", BF16, "mm_att_dw", D_ATT, 256, s, chip_of=same)
    t_conv = _mm(sv["yc"], dy_c, "tn", BF16, "mm_conv_dw", D_CONV, 256, s, chip_of=same)
    t_sgu = _mm(sv["ys"], dy_s, "tn", BF16, "mm_sgu_dw", D_SGU, 256, s, chip_of=same)
    d_conv, dconv_mix = _sconv_bwd(sv["h"], p["conv_mix_w"], dyc, "sconv_bwd")
    d_sgu, g["sgu_ln_g"], g["sgu_ln_b"], g["sgu_w"], dbias = _sgu_bwd(
        sv["h"], p["ln_g"], p["ln_b"], p["sgu_w"], p["sgu_bias"], dys, "sgu_bwd")
    dq, dk, dv, dc_even, dc_odd = _attn_bwd(sv["h"], sv["ck"], sv["o_f32"], sv["lse"], do, "attn_bwd")
    df, db_forget = _gate_bwd(sv["f_row"], p["b_forget"], dc_even, dc_odd, "gate_bwd")
    f_cols = jnp.concatenate([df[:N_HEADS].T, jnp.zeros((s, IN_PAD - IN_WIDTH), BF16)], axis=1)
    dh = _assemble_dh([dq, dk, dv, d_conv, d_sgu, dgl, f_cols], "assemble_dh")
    dxn = _mm(dh, p["w_in_bwd"], "nt", F32, "mm_in_dx", 1024, D_MODEL, 2048)
    dw_bwd = _mm(sv["xn"], dh, "tn", F32, "mm_in_dw", D_MODEL, 512, s)
    n_rest = IN_WIDTH - N_HEADS
    dw_in = jnp.concatenate([dw_bwd[:, :_O_F], dw_bwd[:, n_rest:IN_WIDTH], dw_bwd[:, _O_F:n_rest],
                             jnp.zeros((D_MODEL, IN_SHARD_PAD - IN_SHARD), F32)], axis=1)
    t_in = jnp.stack([dw_in[:, j * IN_SHARD:j * IN_SHARD + IN_SHARD_PAD] for j in range(N_CHIPS)]).astype(BF16)
    dx, g["pre_mix_g"] = _rms_bwd(sv["x"], p["pre_mix_g"], [dxn], dx1, F32, "pre_mix_bwd")
    blk = lambda a, j: a[:, j * FF_BLK:(j + 1) * FF_BLK]
    g["conv_ffn_w"] = jnp.concatenate([blk(dconv_ffn, 0), blk(dconv_ffn, 2), blk(dconv_ffn, 1),
                                       blk(dconv_ffn, 3)], axis=1)[:3]
    g["conv_mix_w"] = dconv_mix[:3]
    g["b_gate"] = db_gate[:3]
    g["b_forget"] = db_forget[:N_HEADS, 0]
    g["sgu_b"] = jnp.sum(dbias.reshape(CHUNK, N_GROUPS, HEAD_DIM), axis=-1).T
    for n in ("pre_mix_g", "post_mix_g", "pre_ffn_g", "post_ffn_g", "sgu_ln_g", "sgu_ln_b"):
        g[n] = g[n].reshape(-1)
    mix = [t_in, t_att, t_conv, t_sgu, t_out]
    return dx, (mix if early is not None else mix + [t_up, t_down]), g


def _assemble_dh(pieces, name):
    s = pieces[0].shape[0]
    t = _tile(s, 512)
    width = sum(a.shape[1] for a in pieces)

    def body(*refs):
        out = refs[-1]
        col = 0
        for ref in refs[:-1]:
            w = ref.shape[1]
            out[:, col:col + w] = ref[...].astype(out.dtype)
            col += w

    return pl.pallas_call(
        body, name=name, grid=(s // t,),
        in_specs=[pl.BlockSpec((t, a.shape[1]), lambda i: (i, 0)) for a in pieces],
        out_specs=pl.BlockSpec((t, width), lambda i: (i, 0)),
        out_shape=jax.ShapeDtypeStruct((s, width), BF16),
        compiler_params=_params(("parallel",)),
    )(*pieces)


def _shard_cols(a, j):
    w = a.shape[-1] // N_CHIPS
    return a[..., j * w:(j + 1) * w]


def kernel(x, pre_mix_g, post_mix_g, pre_ffn_g, post_ffn_g, w_in, b_forget, b_gate, conv_mix_w, sgu_ln_g, sgu_ln_b, sgu_w, sgu_b, w_branch_att, w_branch_conv, w_branch_sgu, w_out, w_ffn_up, conv_ffn_w, w_ffn_down, loss_target, m_pre_mix_g, m_post_mix_g, m_pre_ffn_g, m_post_ffn_g, m_w_in, m_b_forget, m_b_gate, m_conv_mix_w, m_sgu_ln_g, m_sgu_ln_b, m_sgu_w, m_sgu_b, m_w_branch_att, m_w_branch_conv, m_w_branch_sgu, m_w_out, m_w_ffn_up, m_conv_ffn_w, m_w_ffn_down, v_pre_mix_g, v_post_mix_g, v_pre_ffn_g, v_post_ffn_g, v_w_in, v_b_forget, v_b_gate, v_conv_mix_w, v_sgu_ln_g, v_sgu_ln_b, v_sgu_w, v_sgu_b, v_w_branch_att, v_w_branch_conv, v_w_branch_sgu, v_w_out, v_w_ffn_up, v_conv_ffn_w, v_w_ffn_down):
    wts = dict(pre_mix_g=pre_mix_g, post_mix_g=post_mix_g, pre_ffn_g=pre_ffn_g, post_ffn_g=post_ffn_g, w_in=w_in,
               b_forget=b_forget, b_gate=b_gate, conv_mix_w=conv_mix_w, sgu_ln_g=sgu_ln_g, sgu_ln_b=sgu_ln_b,
               sgu_w=sgu_w, sgu_b=sgu_b, w_branch_att=w_branch_att, w_branch_conv=w_branch_conv,
               w_branch_sgu=w_branch_sgu, w_out=w_out, w_ffn_up=w_ffn_up, conv_ffn_w=conv_ffn_w,
               w_ffn_down=w_ffn_down)
    moms = dict(pre_mix_g=m_pre_mix_g, post_mix_g=m_post_mix_g, pre_ffn_g=m_pre_ffn_g, post_ffn_g=m_post_ffn_g,
                w_in=m_w_in, b_forget=m_b_forget, b_gate=m_b_gate, conv_mix_w=m_conv_mix_w, sgu_ln_g=m_sgu_ln_g,
                sgu_ln_b=m_sgu_ln_b, sgu_w=m_sgu_w, sgu_b=m_sgu_b, w_branch_att=m_w_branch_att,
                w_branch_conv=m_w_branch_conv, w_branch_sgu=m_w_branch_sgu, w_out=m_w_out, w_ffn_up=m_w_ffn_up,
                conv_ffn_w=m_conv_ffn_w, w_ffn_down=m_w_ffn_down)
    vels = dict(pre_mix_g=v_pre_mix_g, post_mix_g=v_post_mix_g, pre_ffn_g=v_pre_ffn_g, post_ffn_g=v_post_ffn_g,
                w_in=v_w_in, b_forget=v_b_forget, b_gate=v_b_gate, conv_mix_w=v_conv_mix_w, sgu_ln_g=v_sgu_ln_g,
                sgu_ln_b=v_sgu_ln_b, sgu_w=v_sgu_w, sgu_b=v_sgu_b, w_branch_att=v_w_branch_att,
                w_branch_conv=v_w_branch_conv, w_branch_sgu=v_w_branch_sgu, w_out=v_w_out, w_ffn_up=v_w_ffn_up,
                conv_ffn_w=v_conv_ffn_w, w_ffn_down=v_w_ffn_down)

    c_idx = lax.axis_index("c").astype(jnp.int32).reshape(1)
    me_idx = (2 * lax.axis_index("x") + lax.axis_index("y")).astype(jnp.int32).reshape(1)
    small = _gather_small(wts)

    xs = x[0]
    layers, saved = [], []
    first = _gather_begin(wts, 0, me_idx, FIRST_NAMES, "first")
    rest = _gather_begin(wts, 0, me_idx, LATE_NAMES, "late")
    lands = _gather_finish(first, xs, FIRST_NAMES, "first")
    late = lambda after: _prep_late(_gather_finish(rest, after, LATE_NAMES, "late"))
    for l in range(DEPTH):
        p = _prep_first(wts, lands, small, l)
        if l > 0:
            p.update(_prep_late(lands))
        nxt = _gather_begin(wts, l + 1, me_idx, BIG_NAMES, "all") if l + 1 < DEPTH else None
        dep = ([nxt[4]] if nxt else []) + ([rest[4]] if l == 0 else [])
        xs, sv = _layer_fwd(xs, p, dep or None, late if l == 0 else None)
        if nxt:
            lands = _gather_finish(nxt, xs, BIG_NAMES, "all")
        layers.append(p)
        saved.append(sv)
    dy, loss_part = _loss_head(xs, loss_target[0], "loss_head")
    loss = lax.psum(loss_part[0, 0], ("x", "y", "c"))

    big_red = [None] * DEPTH
    small_grads = [None] * DEPTH
    pending = None
    ffn = []
    for l in reversed(range(DEPTH)):
        early = None
        if l == 0:
            def early(ts_ffn):
                ffn.append(_rs_begin(ts_ffn, c_idx, me_idx, "ffn"))
                return ffn[0][4]
        dy, ts, small_grads[l] = _layer_bwd(dy, layers[l], saved[l], pending[4] if pending else None, early)
        if pending:
            big_red[l + 1] = _rs_finish(pending, dy, c_idx, "big")
        pending = _rs_begin(ts, c_idx, me_idx, "mix" if l == 0 else "big")
    red_ffn = _rs_finish(ffn[0], dy, c_idx, "ffn")
    grad_x = dy[None]

    rep_flat = jnp.concatenate([small_grads[l][n].reshape(-1) for l in range(DEPTH) for n, _ in REPLICATED])
    rep_flat = jnp.pad(rep_flat, (0, N_CHIPS * _REP_QUARTER - rep_flat.shape[0]))
    rows = []
    for j in range(N_CHIPS):
        pieces = [_shard_cols(small_grads[l][n], j) for l in range(DEPTH) for n, _ in SMALL_SHARDS]
        pieces.append(rep_flat[j * _REP_QUARTER:(j + 1) * _REP_QUARTER])
        rows.append(_pack_rows(pieces, SMALL_ROWS, F32))
    small_red = _reduce_scatter_chips(jnp.stack(rows), "small")
    small_all = _all_gather_chips(small_red, "gather_small")
    big_red[0] = _rs_finish(pending, small_all, c_idx, "mix") + red_ffn
    small_all = small_all.reshape(N_CHIPS, -1)

    grads = {}
    for i, (n, _) in enumerate(BIG_SHARDS):
        grads[n] = jnp.stack([big_red[l][i][:, :IN_SHARD] if n == "w_in" else big_red[l][i] for l in range(DEPTH)])
    mine_small = small_red.reshape(-1)
    parts = _unpack(mine_small, [s for _ in range(DEPTH) for _, s in SMALL_SHARDS])
    for i, (n, _) in enumerate(SMALL_SHARDS):
        grads[n] = jnp.stack([parts[l * len(SMALL_SHARDS) + i] for l in range(DEPTH)])
    off = DEPTH * _SMALL_ELEMS
    rep_all = jnp.concatenate([small_all[j, off:off + _REP_QUARTER] for j in range(N_CHIPS)])
    parts = _unpack(rep_all, [s for _ in range(DEPTH) for _, s in REPLICATED])
    for i, (n, _) in enumerate(REPLICATED):
        grads[n] = jnp.stack([parts[l * len(REPLICATED) + i] for l in range(DEPTH)])

    deltas, new_m, new_v = {}, {}, {}
    for n in WEIGHT_ORDER:
        deltas[n], new_m[n], new_v[n] = _adamw(wts[n], grads[n], moms[n], vels[n], "adamw_" + n)
    return (loss, grad_x, *[grads[n] for n in WEIGHT_ORDER], *[deltas[n] for n in WEIGHT_ORDER],
            *[new_m[n] for n in WEIGHT_ORDER], *[new_v[n] for n in WEIGHT_ORDER])
```

```python
import functools
import math

import jax
import jax.numpy as jnp
from jax import lax
from jax.experimental import pallas as pl
from jax.experimental.pallas import tpu as pltpu

F32 = jnp.float32
BF16 = jnp.bfloat16
MXU_DTYPE = jnp.bfloat16

D_MODEL = 1024
HEAD_DIM = 64
N_HEADS = 8
D_ATT = 512
D_CONV = 256
D_SGU = 256
N_GROUPS = 4
CHUNK = 128
D_FF = 2816
DEPTH = 4
RMS_EPS = 1e-6
LN_EPS = 1e-5
N_CHIPS = 4
LANES = 128
PACK_COLS = 1024
HALO = 16

ADAM_LR = 0.001
ADAM_B1 = 0.9
ADAM_B2 = 0.999
ADAM_EPS = 1e-08
ADAM_WD = 0.01
ADAM_STEP = 10

OFF_GL = 0
OFF_Q = 3 * D_MODEL
OFF_K = OFF_Q + D_ATT
OFF_V = OFF_K + D_ATT
OFF_BG = OFF_V + D_ATT
OFF_CG = OFF_BG + D_CONV
OFF_HC = OFF_CG + D_CONV
OFF_U = OFF_HC + D_CONV
OFF_VS = OFF_U + D_SGU
W_P = OFF_VS + D_SGU
F_ROWS = 16

VMEM_LIMIT = 56 * 1024 * 1024
MESH = pl.DeviceIdType.MESH


def _params(sem=None):
    if sem is None:
        return pltpu.CompilerParams(vmem_limit_bytes=VMEM_LIMIT)
    return pltpu.CompilerParams(dimension_semantics=sem, vmem_limit_bytes=VMEM_LIMIT)


def _tile(dim, pref):
    if dim <= pref:
        return dim
    if dim % pref == 0:
        return pref
    return dim


_DIMS = {"nn": (((1,), (0,)), ((), ())), "nt": (((1,), (1,)), ((), ())), "tn": (((0,), (0,)), ((), ()))}


def _mm(a, b, mode, out_dtype, name, tm, tn, tk, chip_of=None):
    if mode == "tn":
        K, M = a.shape
    else:
        M, K = a.shape
    N = b.shape[0] if mode == "nt" else b.shape[1]
    tm, tn, tk = _tile(M, tm), _tile(N // N_CHIPS if chip_of else N, tn), _tile(K, tk)
    nk = K // tk
    dims = _DIMS[mode]

    def body(a_ref, b_ref, o_ref, *acc):
        part = lax.dot_general(a_ref[...].astype(MXU_DTYPE), b_ref[...].astype(MXU_DTYPE), dims,
                               preferred_element_type=F32)
        if nk == 1:
            o_ref[...] = part.astype(o_ref.dtype)
        else:
            acc_ref = acc[0]
            k = pl.program_id(2)

            @pl.when(k == 0)
            def _():
                acc_ref[...] = part

            @pl.when(k > 0)
            def _():
                acc_ref[...] += part

            @pl.when(k == nk - 1)
            def _():
                o_ref[...] = acc_ref[...].astype(o_ref.dtype)

    if mode == "tn":
        a_spec = pl.BlockSpec((tk, tm), lambda i, j, k: (k, i))
    else:
        a_spec = pl.BlockSpec((tm, tk), lambda i, j, k: (i, k))
    if mode == "nt":
        b_spec = pl.BlockSpec((tn, tk), lambda i, j, k: (j, k))
    else:
        b_spec = pl.BlockSpec((tk, tn), lambda i, j, k: (k, j))
    if chip_of is None:
        out_spec = pl.BlockSpec((tm, tn), lambda i, j, k: (i, j))
        out_shape = jax.ShapeDtypeStruct((M, N), out_dtype)
    else:
        per = (N // N_CHIPS) // tn
        out_spec = pl.BlockSpec((None, tm, tn), lambda i, j, k: (chip_of(j // per), i, j % per))
        out_shape = jax.ShapeDtypeStruct((N_CHIPS, M, N // N_CHIPS), out_dtype)
    return pl.pallas_call(
        body,
        name=name,
        grid=(M // tm, N // tn, nk),
        in_specs=[a_spec, b_spec],
        out_specs=out_spec,
        out_shape=out_shape,
        scratch_shapes=[pltpu.VMEM((tm, tn), F32)] if nk > 1 else [],
        compiler_params=_params(("parallel", "parallel", "arbitrary")),
    )(a, b)


_GELU_K = math.sqrt(2.0 / math.pi)
_GELU_C = 0.044715


def _gelu(x):
    t = jnp.tanh(_GELU_K * (x + _GELU_C * (x * x * x)))
    return x * (0.5 * (1.0 + t))


def _gelu_and_grad(x):
    x2 = x * x
    t = jnp.tanh(_GELU_K * (x + _GELU_C * (x2 * x)))
    cdf = 0.5 * (1.0 + t)
    dcdf = 0.5 * (1.0 - t * t) * (_GELU_K * (1.0 + 3.0 * _GELU_C * x2))
    return x * cdf, cdf + x * dcdf


def _sigmoid(x):
    return 1.0 / (1.0 + jnp.exp(-x))


def _shift_down(cur, prev, k):
    h = prev.shape[0]
    ext = jnp.concatenate([prev, cur], axis=0)
    return pltpu.roll(ext, k, 0)[h:]


def _shift_up(cur, nxt, k):
    t, h = cur.shape[0], nxt.shape[0]
    ext = jnp.concatenate([cur, nxt], axis=0)
    return pltpu.roll(ext, t + h - k, 0)[:t]


def _row_sum8(x):
    t, c = x.shape
    return jnp.sum(x.reshape(t // 8, 8, c), axis=0)


_DEP = pl.BlockSpec((8, LANES), lambda i: (0, 0))


def _rms_fwd(x, g, name, dep=None):
    s, d = x.shape
    t = _tile(s, 512)

    def body(x_ref, g_ref, *rest):
        o_ref = rest[-1]
        xv = x_ref[...]
        r = lax.rsqrt(jnp.mean(xv * xv, axis=-1, keepdims=True) + RMS_EPS)
        o_ref[...] = (xv * r * g_ref[...]).astype(o_ref.dtype)

    deps = [] if dep is None else list(dep) if isinstance(dep, (list, tuple)) else [dep]
    return pl.pallas_call(
        body, name=name, grid=(s // t,),
        in_specs=[pl.BlockSpec((t, d), lambda i: (i, 0)), pl.BlockSpec((1, d), lambda i: (0, 0))] + [_DEP] * len(deps),
        out_specs=pl.BlockSpec((t, d), lambda i: (i, 0)),
        out_shape=jax.ShapeDtypeStruct((s, d), BF16),
        compiler_params=_params(("parallel",)),
    )(x, g, *deps)


def _resid_post(x, y, g, name):
    s, d = x.shape
    t = _tile(s, 512)

    def body(x_ref, y_ref, g_ref, o_ref):
        yv = y_ref[...]
        r = lax.rsqrt(jnp.mean(yv * yv, axis=-1, keepdims=True) + RMS_EPS)
        o_ref[...] = x_ref[...] + yv * r * g_ref[...]

    row = pl.BlockSpec((t, d), lambda i: (i, 0))
    return pl.pallas_call(
        body, name=name, grid=(s // t,),
        in_specs=[row, row, pl.BlockSpec((1, d), lambda i: (0, 0))],
        out_specs=row,
        out_shape=jax.ShapeDtypeStruct((s, d), F32),
        compiler_params=_params(("parallel",)),
    )(x, y, g)


def _rms_bwd(xin, g, dys, dres, out_dtype, name, dep=None):
    s, d = xin.shape
    t = _tile(s, 512)
    n = s // t
    n_dy = len(dys)
    has_res = dres is not None
    deps = [] if dep is None else [dep]

    def body(*refs):
        x_ref, g_ref = refs[0], refs[1]
        dy_refs = refs[2:2 + n_dy]
        pos = 2 + n_dy
        res_ref = refs[pos] if has_res else None
        pos += (1 if has_res else 0) + len(deps)
        dx_ref, dg_ref, acc_ref = refs[pos], refs[pos + 1], refs[pos + 2]
        i = pl.program_id(0)
        xv = x_ref[...]
        dy = dy_refs[0][...].astype(F32)
        for extra in dy_refs[1:]:
            dy = dy + extra[...].astype(F32)
        r = lax.rsqrt(jnp.mean(xv * xv, axis=-1, keepdims=True) + RMS_EPS)
        u = dy * g_ref[...]
        xr = xv * r
        dx = r * (u - xr * jnp.mean(u * xr, axis=-1, keepdims=True))
        if has_res:
            dx = dx + res_ref[...]
        dx_ref[...] = dx.astype(dx_ref.dtype)
        part = _row_sum8(dy * xr)

        @pl.when(i == 0)
        def _():
            acc_ref[...] = part

        @pl.when(i > 0)
        def _():
            acc_ref[...] += part

        @pl.when(i == n - 1)
        def _():
            dg_ref[...] = jnp.sum(acc_ref[...], axis=0, keepdims=True)

    row = pl.BlockSpec((t, d), lambda i: (i, 0))
    vec = pl.BlockSpec((1, d), lambda i: (0, 0))
    ins = [xin, g, *dys] + ([dres] if has_res else []) + deps
    return pl.pallas_call(
        body, name=name, grid=(n,),
        in_specs=[row, vec] + [row] * (n_dy + (1 if has_res else 0)) + [_DEP] * len(deps),
        out_specs=[row, vec],
        out_shape=[jax.ShapeDtypeStruct((s, d), out_dtype), jax.ShapeDtypeStruct((1, d), F32)],
        scratch_shapes=[pltpu.VMEM((8, d), F32)],
        compiler_params=_params(("arbitrary",)),
    )(*ins)


def _loss_head(y, target, name):
    s, d = y.shape
    t = _tile(s, 512)
    n = s // t

    def body(y_ref, t_ref, dy_ref, loss_ref, acc_ref):
        i = pl.program_id(0)
        e = y_ref[...] - t_ref[...]
        dy_ref[...] = e * (1.0 / d)
        part = _row_sum8(e * e)

        @pl.when(i == 0)
        def _():
            acc_ref[...] = part

        @pl.when(i > 0)
        def _():
            acc_ref[...] += part

        @pl.when(i == n - 1)
        def _():
            tot = jnp.sum(jnp.sum(acc_ref[...], axis=0, keepdims=True), axis=1, keepdims=True)
            loss_ref[...] = tot * (0.5 / d)

    row = pl.BlockSpec((t, d), lambda i: (i, 0))
    return pl.pallas_call(
        body, name=name, grid=(n,),
        in_specs=[row, row],
        out_specs=[row, pl.BlockSpec((1, 1), lambda i: (0, 0))],
        out_shape=[jax.ShapeDtypeStruct((s, d), F32), jax.ShapeDtypeStruct((1, 1), F32)],
        scratch_shapes=[pltpu.VMEM((8, d), F32)],
        compiler_params=_params(("arbitrary",)),
    )(y, target)


def _split3(x):
    hi = x.astype(BF16)
    r1 = x - hi.astype(F32)
    mid = r1.astype(BF16)
    lo = (r1 - mid.astype(F32)).astype(BF16)
    return hi, mid, lo


def _tri_dot(x, tri):
    hi, mid, lo = _split3(x)
    dn = _DIMS["nn"]
    out = lax.dot_general(hi, tri, dn, preferred_element_type=F32)
    out = out + lax.dot_general(mid, tri, dn, preferred_element_type=F32)
    return out + lax.dot_general(lo, tri, dn, preferred_element_type=F32)


def _log_sigmoid(z):
    return jnp.minimum(z, 0.0) - jnp.log(1.0 + jnp.exp(-jnp.abs(z)))


def _gate_fwd(f_row, b_col, name):
    rows, s = f_row.shape
    t = _tile(s, 512)
    n = s // t

    def body(f_ref, b_ref, ck_ref, carry_ref):
        i = pl.program_id(0)

        @pl.when(i == 0)
        def _():
            carry_ref[...] = jnp.zeros_like(carry_ref)

        logf = _log_sigmoid(f_ref[...] + b_ref[...])
        r = lax.broadcasted_iota(jnp.int32, (t, t), 0)
        c = lax.broadcasted_iota(jnp.int32, (t, t), 1)
        tri = jnp.where(r <= c, 1.0, 0.0).astype(BF16)
        cs = _tri_dot(logf, tri) + carry_ref[...]
        carry_ref[...] = cs[:, t - 1:t]
        terms = [part.astype(F32) for part in _split3(-cs)]
        sub = lax.broadcasted_iota(jnp.int32, (LANES, t), 0)
        for p in range(N_HEADS // 2):
            stacked = jnp.zeros((LANES, t), F32)
            for hh in range(2):
                for j, term in enumerate(terms):
                    h = 2 * p + hh
                    stacked = jnp.where(sub == 3 * hh + j, jnp.broadcast_to(term[h:h + 1, :], (LANES, t)), stacked)
            ck_ref[p] = stacked.T.astype(ck_ref.dtype)

    return pl.pallas_call(
        body, name=name, grid=(n,),
        in_specs=[pl.BlockSpec((rows, t), lambda i: (0, i)), pl.BlockSpec((rows, 1), lambda i: (0, 0))],
        out_specs=pl.BlockSpec((N_HEADS // 2, t, LANES), lambda i: (0, i, 0)),
        out_shape=jax.ShapeDtypeStruct((N_HEADS // 2, s, LANES), BF16),
        scratch_shapes=[pltpu.VMEM((rows, 1), F32)],
        compiler_params=_params(("arbitrary",)),
    )(f_row, b_col)


def _gate_bwd(f_row, b_col, dc_even, dc_odd, name):
    rows, s = f_row.shape
    t = _tile(s, 512)
    n = s // t

    def body(f_ref, b_ref, dce_ref, dco_ref, df_ref, db_ref, carry_ref, acc_ref):
        i = pl.program_id(0)

        @pl.when(i == 0)
        def _():
            carry_ref[...] = jnp.zeros_like(carry_ref)
            acc_ref[...] = jnp.zeros_like(acc_ref)

        head = lax.broadcasted_iota(jnp.int32, (rows, t), 0)
        dcv = jnp.zeros((rows, t), F32)
        for h in range(N_HEADS):
            src = dce_ref if h % 2 == 0 else dco_ref
            dcv = jnp.where(head == h, jnp.broadcast_to(src[h // 2, 0:1, :], (rows, t)), dcv)
        r = lax.broadcasted_iota(jnp.int32, (t, t), 0)
        c = lax.broadcasted_iota(jnp.int32, (t, t), 1)
        tri = jnp.where(r >= c, 1.0, 0.0).astype(BF16)
        dlogf = _tri_dot(dcv, tri) + carry_ref[...]
        carry_ref[...] = dlogf[:, 0:1]
        z = f_ref[...] + b_ref[...]
        df = dlogf * _sigmoid(-z)
        df_ref[...] = df.astype(df_ref.dtype)
        acc_ref[...] += jnp.sum(df, axis=1, keepdims=True)

        @pl.when(i == n - 1)
        def _():
            db_ref[...] = acc_ref[...]

    rev = lambda i: (0, n - 1 - i)
    dc_spec = pl.BlockSpec((N_HEADS // 2, 8, t), lambda i: (0, 0, n - 1 - i))
    return pl.pallas_call(
        body, name=name, grid=(n,),
        in_specs=[pl.BlockSpec((rows, t), rev), pl.BlockSpec((rows, 1), lambda i: (0, 0)), dc_spec, dc_spec],
        out_specs=[pl.BlockSpec((rows, t), rev), pl.BlockSpec((rows, 1), lambda i: (0, 0))],
        out_shape=[jax.ShapeDtypeStruct((rows, s), BF16), jax.ShapeDtypeStruct((rows, 1), F32)],
        scratch_shapes=[pltpu.VMEM((rows, 1), F32), pltpu.VMEM((rows, 1), F32)],
        compiler_params=_params(("arbitrary",)),
    )(f_row, b_col, dc_even, dc_odd)


_NEG = -1e30
_SCALE = HEAD_DIM ** -0.5


def _head_masks():
    lane = lax.broadcasted_iota(jnp.int32, (1, LANES), 1)
    return [lane < HEAD_DIM, lane >= HEAD_DIM]


def _attn_fwd(h, ck, name):
    s = h.shape[0]
    t = _tile(s, 512)
    n = s // t
    qb, kb, vb = OFF_Q // LANES, OFF_K // LANES, OFF_V // LANES

    pairs = [(qi, ki) for qi in range(n) for ki in range(qi + 1)]
    qi_tab = jnp.asarray([qi for qi, _ in pairs], jnp.int32)
    ki_tab = jnp.asarray([ki for _, ki in pairs], jnp.int32)

    def body(qi_ref, ki_ref, q_ref, k_ref, v_ref, ck_ref, o_ref, of_ref, lse_ref, m_ref, l_ref, acc_ref):
        qi, ki = qi_ref[pl.program_id(1)], ki_ref[pl.program_id(1)]
        masks = _head_masks()
        lane = lax.broadcasted_iota(jnp.int32, (1, LANES), 1)

        @pl.when(ki == 0)
        def _():
            m_ref[...] = jnp.full_like(m_ref, _NEG)
            l_ref[...] = jnp.zeros_like(l_ref)
            acc_ref[...] = jnp.zeros_like(acc_ref)

        def step(diag):
            q = q_ref[...] * _SCALE
            k_aug = jnp.concatenate([k_ref[...], ck_ref[0]], axis=1)
            v = v_ref[...]
            nq = max(1, t // 256)
            wq = t // nq
            chains = [(hh, j) for hh in range(2) for j in range(nq)]
            scores = []
            for hh, j in chains:
                qs = q[j * wq:(j + 1) * wq]
                ones = jnp.where((lane >= 3 * hh) & (lane < 3 * hh + 3), 1.0, 0.0).astype(q.dtype)
                q_aug = jnp.concatenate([jnp.where(masks[hh], qs, jnp.zeros_like(qs)),
                                         jnp.broadcast_to(ones, qs.shape)], axis=1)
                scores.append(lax.dot_general(k_aug, q_aug, _DIMS["nt"], preferred_element_type=F32))
            probs = []
            for (hh, j), sc in zip(chains, scores):
                cols = slice(j * wq, (j + 1) * wq)
                if diag:
                    r = lax.broadcasted_iota(jnp.int32, (t, wq), 0)
                    cc = lax.broadcasted_iota(jnp.int32, (t, wq), 1) + j * wq
                    sc = jnp.where(r <= cc, sc, _NEG)
                m_prev = m_ref[hh, :, cols]
                m_new = jnp.maximum(m_prev, jnp.max(sc, axis=0, keepdims=True))
                alpha = jnp.exp(m_prev - m_new)
                p = jnp.exp(sc - m_new)
                l_ref[hh, :, cols] = alpha * l_ref[hh, :, cols] + jnp.sum(p, axis=0, keepdims=True)
                m_ref[hh, :, cols] = m_new
                p_hi = p.astype(MXU_DTYPE)
                p_lo = (p - p_hi.astype(F32)).astype(MXU_DTYPE)
                probs.append((alpha, p_hi, p_lo))
            for (hh, j), (alpha, p_hi, p_lo) in zip(chains, probs):
                pv = (lax.dot_general(v, p_hi, _DIMS["tn"], preferred_element_type=F32)
                      + lax.dot_general(v, p_lo, _DIMS["tn"], preferred_element_type=F32))
                rows = slice(hh * HEAD_DIM, (hh + 1) * HEAD_DIM)
                cols = slice(j * wq, (j + 1) * wq)
                acc_ref[rows, cols] = alpha * acc_ref[rows, cols] + pv[rows]

        @pl.when(ki < qi)
        def _():
            step(False)

        @pl.when(ki == qi)
        def _():
            step(True)
            inv = jnp.concatenate([jnp.broadcast_to(1.0 / l_ref[hh], (HEAD_DIM, t)) for hh in range(2)], axis=0)
            out = (acc_ref[...] * inv).T
            o_ref[...] = out.astype(o_ref.dtype)
            of_ref[...] = out
            lse = jnp.concatenate([jnp.broadcast_to(m_ref[hh] + jnp.log(l_ref[hh]), (HEAD_DIM, t))
                                   for hh in range(2)], axis=0)
            lse_ref[...] = lse.T

    grid_spec = pltpu.PrefetchScalarGridSpec(
        num_scalar_prefetch=2, grid=(N_HEADS // 2, len(pairs)),
        in_specs=[
            pl.BlockSpec((t, LANES), lambda p, i, qt, kt: (qt[i], qb + p)),
            pl.BlockSpec((t, LANES), lambda p, i, qt, kt: (kt[i], kb + p)),
            pl.BlockSpec((t, LANES), lambda p, i, qt, kt: (kt[i], vb + p)),
            pl.BlockSpec((1, t, LANES), lambda p, i, qt, kt: (p, kt[i], 0)),
        ],
        out_specs=[pl.BlockSpec((t, LANES), lambda p, i, qt, kt: (qt[i], p))] * 3,
        scratch_shapes=[pltpu.VMEM((2, 1, t), F32), pltpu.VMEM((2, 1, t), F32), pltpu.VMEM((LANES, t), F32)])
    return pl.pallas_call(
        body, name=name, grid_spec=grid_spec,
        out_shape=[jax.ShapeDtypeStruct((s, D_ATT), BF16), jax.ShapeDtypeStruct((s, D_ATT), F32),
                   jax.ShapeDtypeStruct((s, D_ATT), F32)],
        compiler_params=_params(("parallel", "arbitrary")),
    )(qi_tab, ki_tab, h, h, h, ck)


def _attn_bwd(h, ck, o, lse, do, name):
    s = h.shape[0]
    t = _tile(s, 512)
    n = s // t
    qb, kb, vb = OFF_Q // LANES, OFF_K // LANES, OFF_V // LANES

    pairs = [(ki, qi) for ki in range(n) for qi in range(ki, n)]
    ki_tab = jnp.asarray([ki for ki, _ in pairs], jnp.int32)
    qi_tab = jnp.asarray([qi for _, qi in pairs], jnp.int32)

    def body(ki_ref, qi_ref, q_ref, k_ref, v_ref, ck_ref, o_ref, lse_ref, do_ref,
             dq_ref, dk_ref, dv_ref, dc0_ref, dc1_ref, dk_acc, dv_acc, dc_acc):
        ki, qi = ki_ref[pl.program_id(1)], qi_ref[pl.program_id(1)]
        masks = _head_masks()
        lane = lax.broadcasted_iota(jnp.int32, (1, LANES), 1)

        @pl.when((ki == 0) & (qi == 0))
        def _():
            dq_ref[...] = jnp.zeros_like(dq_ref)

        @pl.when(qi == ki)
        def _():
            dk_acc[...] = jnp.zeros_like(dk_acc)
            dv_acc[...] = jnp.zeros_like(dv_acc)
            dc_acc[...] = jnp.zeros_like(dc_acc)

        def step(diag):
            q = q_ref[...] * _SCALE
            k = k_ref[...]
            v = v_ref[...]
            dov = do_ref[...]
            k_aug = jnp.concatenate([k, ck_ref[0]], axis=1)
            prod_t = (dov.astype(F32) * o_ref[...]).T
            lse_t = lse_ref[...].T
            q_t = q.astype(F32).T.astype(MXU_DTYPE)
            do_t = dov.astype(F32).T.astype(MXU_DTYPE)
            sub = lax.broadcasted_iota(jnp.int32, (LANES, 1), 0)
            heads = []
            for hh in range(2):
                mk = masks[hh]
                qh = jnp.where(mk, q, jnp.zeros_like(q))
                kh = jnp.where(mk, k, jnp.zeros_like(k))
                doh = jnp.where(mk, dov, jnp.zeros_like(dov))
                ones = jnp.where((lane >= 3 * hh) & (lane < 3 * hh + 3), 1.0, 0.0).astype(q.dtype)
                q_aug = jnp.concatenate([qh, jnp.broadcast_to(ones, q.shape)], axis=1)
                sc = lax.dot_general(k_aug, q_aug, _DIMS["nt"], preferred_element_type=F32)
                dp = lax.dot_general(v, doh, _DIMS["nt"], preferred_element_type=F32)
                heads.append((qh, kh, doh, sc, dp))
            grads = []
            for hh, (qh, kh, doh, sc, dp) in enumerate(heads):
                rows = slice(hh * HEAD_DIM, (hh + 1) * HEAD_DIM)
                p = jnp.exp(sc - lse_t[hh * HEAD_DIM:hh * HEAD_DIM + 1, :])
                if diag:
                    r = lax.broadcasted_iota(jnp.int32, (t, t), 0)
                    cc = lax.broadcasted_iota(jnp.int32, (t, t), 1)
                    p = jnp.where(r <= cc, p, 0.0)
                delta = jnp.sum(prod_t[rows], axis=0, keepdims=True)
                ds = p * (dp - delta)
                dc_acc[hh] = dc_acc[hh] - jnp.sum(ds, axis=1, keepdims=True)
                grads.append((ds.astype(MXU_DTYPE), p.astype(MXU_DTYPE)))
            dq_t = jnp.zeros((LANES, t), F32)
            for hh, ((qh, kh, doh, _, _), (dsb, pb)) in enumerate(zip(heads, grads)):
                mine = (sub >= hh * HEAD_DIM) & (sub < (hh + 1) * HEAD_DIM)
                qh_t = jnp.where(mine, q_t, jnp.zeros_like(q_t))
                doh_t = jnp.where(mine, do_t, jnp.zeros_like(do_t))
                dv_acc[...] += lax.dot_general(doh_t, pb, _DIMS["nt"], preferred_element_type=F32)
                dk_acc[...] += lax.dot_general(qh_t, dsb, _DIMS["nt"], preferred_element_type=F32)
                dq_t = dq_t + lax.dot_general(kh, dsb, _DIMS["tn"], preferred_element_type=F32)
            rows_q = pl.ds(pl.multiple_of(qi * t, t), t)
            dq_ref[rows_q, :] = dq_ref[rows_q, :] + dq_t.T * _SCALE

        @pl.when(qi > ki)
        def _():
            step(False)

        @pl.when(qi == ki)
        def _():
            step(True)

        @pl.when(qi == n - 1)
        def _():
            dk_ref[...] = dk_acc[...].T.astype(dk_ref.dtype)
            dv_ref[...] = dv_acc[...].T.astype(dv_ref.dtype)
            dc0_ref[0] = jnp.broadcast_to(dc_acc[0], (t, LANES)).T[0:8]
            dc1_ref[0] = jnp.broadcast_to(dc_acc[1], (t, LANES)).T[0:8]

    q_blk = lambda col: pl.BlockSpec((t, LANES), lambda p, i, kt, qt: (qt[i], col(p)))
    k_blk = lambda col: pl.BlockSpec((t, LANES), lambda p, i, kt, qt: (kt[i], col(p)))
    dc_blk = pl.BlockSpec((1, 8, t), lambda p, i, kt, qt: (p, 0, kt[i]))
    grid_spec = pltpu.PrefetchScalarGridSpec(
        num_scalar_prefetch=2, grid=(N_HEADS // 2, len(pairs)),
        in_specs=[q_blk(lambda p: qb + p), k_blk(lambda p: kb + p), k_blk(lambda p: vb + p),
                  pl.BlockSpec((1, t, LANES), lambda p, i, kt, qt: (p, kt[i], 0)),
                  q_blk(lambda p: p), q_blk(lambda p: p), q_blk(lambda p: p)],
        out_specs=[pl.BlockSpec((s, LANES), lambda p, i, kt, qt: (0, p)), k_blk(lambda p: p), k_blk(lambda p: p),
                   dc_blk, dc_blk],
        scratch_shapes=[pltpu.VMEM((LANES, t), F32), pltpu.VMEM((LANES, t), F32), pltpu.VMEM((2, t, 1), F32)])
    return pl.pallas_call(
        body, name=name, grid_spec=grid_spec,
        out_shape=[jax.ShapeDtypeStruct((s, D_ATT), F32), jax.ShapeDtypeStruct((s, D_ATT), BF16),
                   jax.ShapeDtypeStruct((s, D_ATT), BF16), jax.ShapeDtypeStruct((N_HEADS // 2, 8, s), F32),
                   jax.ShapeDtypeStruct((N_HEADS // 2, 8, s), F32)],
        compiler_params=_params(("parallel", "arbitrary")),
    )(ki_tab, qi_tab, h, h, h, ck, o, lse, do)


def _conv3(z, z_prev, w_ref):
    return (w_ref[2:3, :] * z + w_ref[1:2, :] * _shift_down(z, z_prev, 1)
            + w_ref[0:1, :] * _shift_down(z, z_prev, 2))


def _sconv_fwd(h, w, name):
    s = h.shape[0]
    t = _tile(s, 512)
    r = t // HALO
    c = D_CONV
    b_bg, b_cg, b_hc = OFF_BG // c, OFF_CG // c, OFF_HC // c

    def body(bg_ref, cg_ref, hc_ref, cgp_ref, hcp_ref, w_ref, y_ref):
        i = pl.program_id(0)
        live = (i > 0).astype(F32)
        z = cg_ref[...].astype(F32) * hc_ref[...].astype(F32)
        zp = cgp_ref[...].astype(F32) * hcp_ref[...].astype(F32) * live
        y_ref[...] = (bg_ref[...].astype(F32) * _conv3(z, zp, w_ref)).astype(y_ref.dtype)

    cur = lambda b: pl.BlockSpec((t, c), lambda i: (i, b))
    prev = lambda b: pl.BlockSpec((HALO, c), lambda i: (jnp.maximum(i * r - 1, 0), b))
    return pl.pallas_call(
        body, name=name, grid=(s // t,),
        in_specs=[cur(b_bg), cur(b_cg), cur(b_hc), prev(b_cg), prev(b_hc), pl.BlockSpec((8, c), lambda i: (0, 0))],
        out_specs=pl.BlockSpec((t, c), lambda i: (i, 0)),
        out_shape=jax.ShapeDtypeStruct((s, c), BF16),
        compiler_params=_params(("parallel",)),
    )(h, h, h, h, h, w)


def _sconv_bwd(h, w, dy, name):
    s = h.shape[0]
    t = _tile(s, 512)
    n = s // t
    r = t // HALO
    nh = s // HALO
    c = D_CONV
    b_bg, b_cg, b_hc = OFF_BG // c, OFF_CG // c, OFF_HC // c

    def body(bg_ref, cg_ref, hc_ref, cgp_ref, hcp_ref, bgn_ref, dy_ref, dyn_ref, w_ref, d_ref, dw_ref, acc_ref):
        i = pl.program_id(0)
        has_prev = (i > 0).astype(F32)
        has_next = (i < n - 1).astype(F32)
        bg = bg_ref[...].astype(F32)
        cg = cg_ref[...].astype(F32)
        hc = hc_ref[...].astype(F32)
        dyv = dy_ref[...].astype(F32)
        z = cg * hc
        zp = cgp_ref[...].astype(F32) * hcp_ref[...].astype(F32) * has_prev
        z1 = _shift_down(z, zp, 1)
        z2 = _shift_down(z, zp, 2)
        cz = w_ref[2:3, :] * z + w_ref[1:2, :] * z1 + w_ref[0:1, :] * z2
        dcz = dyv * bg
        dczn = dyn_ref[...].astype(F32) * bgn_ref[...].astype(F32) * has_next
        dz = (w_ref[2:3, :] * dcz + w_ref[1:2, :] * _shift_up(dcz, dczn, 1)
              + w_ref[0:1, :] * _shift_up(dcz, dczn, 2))
        d_ref[:, 0:c] = (dyv * cz).astype(d_ref.dtype)
        d_ref[:, c:2 * c] = (dz * hc).astype(d_ref.dtype)
        d_ref[:, 2 * c:3 * c] = (dz * cg).astype(d_ref.dtype)

        @pl.when(i == 0)
        def _():
            acc_ref[...] = jnp.zeros_like(acc_ref)

        acc_ref[0] += _row_sum8(dcz * z2)
        acc_ref[1] += _row_sum8(dcz * z1)
        acc_ref[2] += _row_sum8(dcz * z)

        @pl.when(i == n - 1)
        def _():
            rows = [jnp.sum(acc_ref[k], axis=0, keepdims=True) for k in range(3)]
            dw_ref[...] = jnp.concatenate(rows + [jnp.zeros((5, c), F32)], axis=0)

    cur = lambda b: pl.BlockSpec((t, c), lambda i: (i, b))
    prev = lambda b: pl.BlockSpec((HALO, c), lambda i: (jnp.maximum(i * r - 1, 0), b))
    nxt = lambda b: pl.BlockSpec((HALO, c), lambda i: (jnp.minimum((i + 1) * r, nh - 1), b))
    return pl.pallas_call(
        body, name=name, grid=(n,),
        in_specs=[cur(b_bg), cur(b_cg), cur(b_hc), prev(b_cg), prev(b_hc), nxt(b_bg),
                  cur(0), nxt(0), pl.BlockSpec((8, c), lambda i: (0, 0))],
        out_specs=[pl.BlockSpec((t, 3 * c), lambda i: (i, 0)), pl.BlockSpec((8, c), lambda i: (0, 0))],
        out_shape=[jax.ShapeDtypeStruct((s, 3 * c), BF16), jax.ShapeDtypeStruct((8, c), F32)],
        scratch_shapes=[pltpu.VMEM((3, 8, c), F32)],
        compiler_params=_params(("arbitrary",)),
    )(h, h, h, h, h, h, dy, dy, w)


def _group_masks():
    lane = lax.broadcasted_iota(jnp.int32, (1, D_SGU), 1)
    return [(lane >= g * HEAD_DIM) & (lane < (g + 1) * HEAD_DIM) for g in range(N_GROUPS)]


def _tril_weights(w_ref):
    r = lax.broadcasted_iota(jnp.int32, (CHUNK, CHUNK), 0)
    c = lax.broadcasted_iota(jnp.int32, (CHUNK, CHUNK), 1)
    return [jnp.where(r >= c, w_ref[g], 0.0).astype(MXU_DTYPE) for g in range(N_GROUPS)]


def _sgu_ln(vs, g_ref, b_ref):
    vg, dvg = _gelu_and_grad(vs)
    mu = jnp.mean(vg, axis=-1, keepdims=True)
    xc = vg - mu
    rstd = lax.rsqrt(jnp.mean(xc * xc, axis=-1, keepdims=True) + LN_EPS)
    xhat = xc * rstd
    return xhat * g_ref[...] + b_ref[...], xhat, rstd, dvg


def _sgu_fwd(h, ln_g, ln_b, w_s, bias, name):
    s = h.shape[0]
    t = _tile(s, 512)
    c = D_SGU
    b_u, b_v = OFF_U // c, OFF_VS // c

    def body(u_ref, v_ref, g_ref, b_ref, w_ref, bias_ref, y_ref):
        gm = _group_masks()
        wm = _tril_weights(w_ref)
        ug = _gelu(u_ref[...].astype(F32))
        vn, _, _, _ = _sgu_ln(v_ref[...].astype(F32), g_ref, b_ref)
        vnb = vn.astype(MXU_DTYPE)
        for ch in range(t // CHUNK):
            rows = slice(ch * CHUNK, (ch + 1) * CHUNK)
            mixed = bias_ref[...]
            for g in range(N_GROUPS):
                mg = lax.dot_general(wm[g], vnb[rows], _DIMS["nn"], preferred_element_type=F32)
                mixed = jnp.where(gm[g], mixed + mg, mixed)
            y_ref[rows, :] = (ug[rows] * mixed).astype(y_ref.dtype)

    full = lambda shp: pl.BlockSpec(shp, lambda i: (0,) * len(shp))
    return pl.pallas_call(
        body, name=name, grid=(s // t,),
        in_specs=[pl.BlockSpec((t, c), lambda i: (i, b_u)), pl.BlockSpec((t, c), lambda i: (i, b_v)),
                  full((1, c)), full((1, c)), full((N_GROUPS, CHUNK, CHUNK)), full((CHUNK, c))],
        out_specs=pl.BlockSpec((t, c), lambda i: (i, 0)),
        out_shape=jax.ShapeDtypeStruct((s, c), BF16),
        compiler_params=_params(("parallel",)),
    )(h, h, ln_g, ln_b, w_s, bias)


def _sgu_bwd(h, ln_g, ln_b, w_s, bias, dy, name):
    s = h.shape[0]
    t = _tile(s, 512)
    n = s // t
    c = D_SGU
    b_u, b_v = OFF_U // c, OFF_VS // c

    def body(u_ref, v_ref, g_ref, b_ref, w_ref, bias_ref, dy_ref,
             d_ref, dg_ref, db_ref, dw_ref, dbias_ref, dg_acc, db_acc):
        i = pl.program_id(0)
        gm = _group_masks()
        wm = _tril_weights(w_ref)

        @pl.when(i == 0)
        def _():
            dg_acc[...] = jnp.zeros_like(dg_acc)
            db_acc[...] = jnp.zeros_like(db_acc)
            dw_ref[...] = jnp.zeros_like(dw_ref)
            dbias_ref[...] = jnp.zeros_like(dbias_ref)

        ug, dug = _gelu_and_grad(u_ref[...].astype(F32))
        vn, xhat, rstd, dvg = _sgu_ln(v_ref[...].astype(F32), g_ref, b_ref)
        vnb = vn.astype(MXU_DTYPE)
        dyv = dy_ref[...].astype(F32)
        dmixed = dyv * ug
        dmb = dmixed.astype(MXU_DTYPE)
        dvn_parts = []
        for ch in range(t // CHUNK):
            rows = slice(ch * CHUNK, (ch + 1) * CHUNK)
            mixed = bias_ref[...]
            dvn = jnp.zeros((CHUNK, c), F32)
            for g in range(N_GROUPS):
                mg = lax.dot_general(wm[g], vnb[rows], _DIMS["nn"], preferred_element_type=F32)
                mixed = jnp.where(gm[g], mixed + mg, mixed)
                dvn = jnp.where(gm[g], lax.dot_general(wm[g], dmb[rows], _DIMS["tn"], preferred_element_type=F32),
                                dvn)
                dmg = jnp.where(gm[g], dmb[rows], jnp.zeros_like(dmb[rows]))
                dw_ref[g] += lax.dot_general(dmg, vnb[rows], _DIMS["nt"], preferred_element_type=F32)
            d_ref[rows, 0:c] = (dyv[rows] * mixed * dug[rows]).astype(d_ref.dtype)
            dbias_ref[...] += dmixed[rows]
            dvn_parts.append(dvn)
        dvn = jnp.concatenate(dvn_parts, axis=0)
        dg_acc[...] += _row_sum8(dvn * xhat)
        db_acc[...] += _row_sum8(dvn)
        dxh = dvn * g_ref[...]
        dvgl = rstd * (dxh - jnp.mean(dxh, axis=-1, keepdims=True)
                       - xhat * jnp.mean(dxh * xhat, axis=-1, keepdims=True))
        d_ref[:, c:2 * c] = (dvgl * dvg).astype(d_ref.dtype)

        @pl.when(i == n - 1)
        def _():
            dg_ref[...] = jnp.sum(dg_acc[...], axis=0, keepdims=True)
            db_ref[...] = jnp.sum(db_acc[...], axis=0, keepdims=True)
            r = lax.broadcasted_iota(jnp.int32, (CHUNK, CHUNK), 0)
            cc = lax.broadcasted_iota(jnp.int32, (CHUNK, CHUNK), 1)
            for g in range(N_GROUPS):
                dw_ref[g] = jnp.where(r >= cc, dw_ref[g], 0.0)

    full = lambda shp: pl.BlockSpec(shp, lambda i: (0,) * len(shp))
    return pl.pallas_call(
        body, name=name, grid=(n,),
        in_specs=[pl.BlockSpec((t, c), lambda i: (i, b_u)), pl.BlockSpec((t, c), lambda i: (i, b_v)),
                  full((1, c)), full((1, c)), full((N_GROUPS, CHUNK, CHUNK)), full((CHUNK, c)),
                  pl.BlockSpec((t, c), lambda i: (i, 0))],
        out_specs=[pl.BlockSpec((t, 2 * c), lambda i: (i, 0)), full((1, c)), full((1, c)),
                   full((N_GROUPS, CHUNK, CHUNK)), full((CHUNK, c))],
        out_shape=[jax.ShapeDtypeStruct((s, 2 * c), BF16), jax.ShapeDtypeStruct((1, c), F32),
                   jax.ShapeDtypeStruct((1, c), F32), jax.ShapeDtypeStruct((N_GROUPS, CHUNK, CHUNK), F32),
                   jax.ShapeDtypeStruct((CHUNK, c), F32)],
        scratch_shapes=[pltpu.VMEM((8, c), F32), pltpu.VMEM((8, c), F32)],
        compiler_params=_params(("arbitrary",)),
    )(h, h, ln_g, ln_b, w_s, bias, dy)


def _merge_fwd(h, acts, ws, b_gate, name):
    s = h.shape[0]
    d = D_MODEL
    t = _tile(s, 512)

    def body(gl0, gl1, gl2, a0, a1, a2, w0, w1, w2, b_ref, o_ref):
        acc = jnp.zeros((t, d), F32)
        for i, (gl, a, w) in enumerate(((gl0, a0, w0), (gl1, a1, w1), (gl2, a2, w2))):
            y = lax.dot_general(a[...], w[...], _DIMS["nn"], preferred_element_type=F32)
            acc = acc + _sigmoid(gl[...].astype(F32) + b_ref[i:i + 1, :]) * y
        o_ref[...] = acc.astype(o_ref.dtype)

    full = lambda arr: pl.BlockSpec(arr.shape, lambda i: (0, 0))
    return pl.pallas_call(
        body, name=name, grid=(s // t,),
        in_specs=[pl.BlockSpec((t, d), lambda i, b=b: (i, b)) for b in range(3)]
                 + [pl.BlockSpec((t, a.shape[1]), lambda i: (i, 0)) for a in acts]
                 + [full(w) for w in ws] + [full(b_gate)],
        out_specs=pl.BlockSpec((t, d), lambda i: (i, 0)),
        out_shape=jax.ShapeDtypeStruct((s, d), BF16),
        compiler_params=_params(("parallel",)),
    )(h, h, h, *acts, *ws, b_gate)


def _merge_bwd(h, acts, ws, b_gate, dmerged, name):
    s = h.shape[0]
    d = D_MODEL
    t = _tile(s, 512)
    n = s // t

    def body(gl0, gl1, gl2, a0, a1, a2, w0, w1, w2, b_ref, dm_ref, dy0, dy1, dy2, dgl_ref, db_ref, acc_ref):
        step = pl.program_id(0)

        @pl.when(step == 0)
        def _():
            acc_ref[...] = jnp.zeros_like(acc_ref)

        dm = dm_ref[...]
        for i, (gl, a, w, dy) in enumerate(((gl0, a0, w0, dy0), (gl1, a1, w1, dy1), (gl2, a2, w2, dy2))):
            y = lax.dot_general(a[...], w[...], _DIMS["nn"], preferred_element_type=F32)
            gate = _sigmoid(gl[...].astype(F32) + b_ref[i:i + 1, :])
            dy[...] = (dm * gate).astype(dy.dtype)
            dgl = dm * y * (gate * (1.0 - gate))
            dgl_ref[:, i * d:(i + 1) * d] = dgl.astype(dgl_ref.dtype)
            acc_ref[i] += _row_sum8(dgl)

        @pl.when(step == n - 1)
        def _():
            rows = [jnp.sum(acc_ref[k], axis=0, keepdims=True) for k in range(3)]
            db_ref[...] = jnp.concatenate(rows + [jnp.zeros((5, d), F32)], axis=0)

    full = lambda arr: pl.BlockSpec(arr.shape, lambda i: (0, 0))
    row = pl.BlockSpec((t, d), lambda i: (i, 0))
    return pl.pallas_call(
        body, name=name, grid=(n,),
        in_specs=[pl.BlockSpec((t, d), lambda i, b=b: (i, b)) for b in range(3)]
                 + [pl.BlockSpec((t, a.shape[1]), lambda i: (i, 0)) for a in acts]
                 + [full(w) for w in ws] + [full(b_gate), row],
        out_specs=[row, row, row, pl.BlockSpec((t, 3 * d), lambda i: (i, 0)), pl.BlockSpec((8, d), lambda i: (0, 0))],
        out_shape=[jax.ShapeDtypeStruct((s, d), BF16)] * 3
                  + [jax.ShapeDtypeStruct((s, 3 * d), BF16), jax.ShapeDtypeStruct((8, d), F32)],
        scratch_shapes=[pltpu.VMEM((3, 8, d), F32)],
        compiler_params=_params(("arbitrary",)),
    )(h, h, h, *acts, *ws, b_gate, dmerged)


FF_BLK = D_FF // 2


def _ffn_act_fwd(h2, w, name):
    s = h2.shape[0]
    t = _tile(s, 512)
    r = t // HALO
    cw = 2 * FF_BLK

    def body(x_ref, xp_ref, w_ref, p_ref):
        i = pl.program_id(0)
        live = (i > 0).astype(F32)
        hc = _conv3(x_ref[...].astype(F32), xp_ref[...].astype(F32) * live, w_ref)
        p_ref[...] = (_gelu(hc[:, :FF_BLK]) * hc[:, FF_BLK:]).astype(p_ref.dtype)

    return pl.pallas_call(
        body, name=name, grid=(s // t, 2),
        in_specs=[pl.BlockSpec((t, cw), lambda i, j: (i, j)),
                  pl.BlockSpec((HALO, cw), lambda i, j: (jnp.maximum(i * r - 1, 0), j)),
                  pl.BlockSpec((8, cw), lambda i, j: (0, j))],
        out_specs=pl.BlockSpec((t, FF_BLK), lambda i, j: (i, j)),
        out_shape=jax.ShapeDtypeStruct((s, D_FF), BF16),
        compiler_params=_params(("parallel", "parallel")),
    )(h2, h2, w)


def _ffn_act_conv_bwd(h2, w, dp, name):
    s = h2.shape[0]
    t = _tile(s, 512)
    n = s // t
    r = t // HALO
    nh = s // HALO
    cw = 2 * FF_BLK

    def body(x_ref, xp_ref, xn_ref, dp_ref, dpn_ref, w_ref, dx_ref, dw_ref, acc_ref):
        i = pl.program_id(1)
        has_prev = (i > 0).astype(F32)
        has_next = (i < n - 1).astype(F32)
        x = jnp.concatenate([x_ref[...].astype(F32), xn_ref[...].astype(F32)], axis=0)
        xp = xp_ref[...].astype(F32) * has_prev
        x1 = _shift_down(x, xp, 1)
        x2 = _shift_down(x, xp, 2)
        hc = w_ref[2:3, :] * x + w_ref[1:2, :] * x1 + w_ref[0:1, :] * x2
        ga, dga = _gelu_and_grad(hc[:, :FF_BLK])
        dpv = jnp.concatenate([dp_ref[...].astype(F32), dpn_ref[...].astype(F32) * has_next], axis=0)
        dhc = jnp.concatenate([dpv * hc[:, FF_BLK:] * dga, dpv * ga], axis=1)
        cur, nxt = dhc[:t], dhc[t:]
        dx = w_ref[2:3, :] * cur + w_ref[1:2, :] * _shift_up(cur, nxt, 1) + w_ref[0:1, :] * _shift_up(cur, nxt, 2)
        dx_ref[...] = dx.astype(dx_ref.dtype)

        @pl.when(i == 0)
        def _():
            acc_ref[...] = jnp.zeros_like(acc_ref)

        acc_ref[0] += _row_sum8(cur * x2[:t])
        acc_ref[1] += _row_sum8(cur * x1[:t])
        acc_ref[2] += _row_sum8(cur * x[:t])

        @pl.when(i == n - 1)
        def _():
            rows = [jnp.sum(acc_ref[k], axis=0, keepdims=True) for k in range(3)]
            dw_ref[...] = jnp.concatenate(rows + [jnp.zeros((5, cw), F32)], axis=0)

    nxt_row = lambda j, i: jnp.minimum((i + 1) * r, nh - 1)
    return pl.pallas_call(
        body, name=name, grid=(2, n),
        in_specs=[pl.BlockSpec((t, cw), lambda j, i: (i, j)),
                  pl.BlockSpec((HALO, cw), lambda j, i: (jnp.maximum(i * r - 1, 0), j)),
                  pl.BlockSpec((HALO, cw), lambda j, i: (nxt_row(j, i), j)),
                  pl.BlockSpec((t, FF_BLK), lambda j, i: (i, j)),
                  pl.BlockSpec((HALO, FF_BLK), lambda j, i: (nxt_row(j, i), j)),
                  pl.BlockSpec((8, cw), lambda j, i: (0, j))],
        out_specs=[pl.BlockSpec((t, cw), lambda j, i: (i, j)), pl.BlockSpec((8, cw), lambda j, i: (0, j))],
        out_shape=[jax.ShapeDtypeStruct((s, 2 * D_FF), BF16), jax.ShapeDtypeStruct((8, 2 * D_FF), F32)],
        scratch_shapes=[pltpu.VMEM((3, 8, cw), F32)],
        compiler_params=_params(("parallel", "arbitrary")),
    )(h2, h2, h2, dp, dp, w)


def _adamw(w, g, m, v, name):
    shape = w.shape
    c = shape[-1]
    lead = shape[0] if len(shape) > 2 else 1
    rows = math.prod(shape[1:-1]) if len(shape) > 2 else shape[0]
    to3d = lambda a: a.reshape(lead, rows, c)
    cap = max(8, (1 << 18) // c)
    tr = rows
    for cand in (2048, 1024, 512, 256, 128, 64, 32, 16, 8):
        if cand <= cap and rows % cand == 0:
            tr = cand
            break

    def body(w_ref, g_ref, m_ref, v_ref, d_ref, nm_ref, nv_ref):
        gv = g_ref[...]
        nm = ADAM_B1 * m_ref[...] + (1.0 - ADAM_B1) * gv
        nv = ADAM_B2 * v_ref[...] + (1.0 - ADAM_B2) * (gv * gv)
        m_hat = nm / (1.0 - ADAM_B1 ** ADAM_STEP)
        v_hat = nv / (1.0 - ADAM_B2 ** ADAM_STEP)
        d_ref[...] = -ADAM_LR * (m_hat / (jnp.sqrt(v_hat) + ADAM_EPS) + ADAM_WD * w_ref[...])
        nm_ref[...] = nm
        nv_ref[...] = nv

    blk = pl.BlockSpec((None, tr, c), lambda l, i: (l, i, 0))
    outs = pl.pallas_call(
        body, name=name, grid=(lead, rows // tr),
        in_specs=[blk] * 4, out_specs=[blk] * 3,
        out_shape=[jax.ShapeDtypeStruct((lead, rows, c), F32)] * 3,
        compiler_params=_params(("parallel", "parallel")),
    )(to3d(w), to3d(g), to3d(m), to3d(v))
    return tuple(o.reshape(shape) for o in outs)


_ANY = pl.BlockSpec(memory_space=pl.ANY)


def _place():
    x, y, c = lax.axis_index("x"), lax.axis_index("y"), lax.axis_index("c")
    others = [(1 - x, y), (x, 1 - y), (1 - x, 1 - y)]
    return x, y, c, others


def _all_gather_chips(shard, name):
    rws, cols = shard.shape
    half = rws // 2

    def body(x_ref, out_ref, send_sems, recv_sems, local_sem):
        x, y, c, others = _place()
        me = 2 * x + y
        sib = (x, y, 1 - c)

        def rows(chip, cc):
            return out_ref.at[chip, pl.ds(pl.multiple_of(cc * half, 16), half), :]

        def copy(k, src, dst, to):
            return pltpu.make_async_remote_copy(src_ref=src, dst_ref=dst, send_sem=send_sems.at[k],
                                                recv_sem=recv_sems.at[k], device_id=to, device_id_type=MESH)

        mine = pltpu.make_async_copy(x_ref, out_ref.at[me], local_sem)
        mine.start()
        my_half = x_ref.at[pl.ds(pl.multiple_of(c * half, 16), half), :]
        first = [copy(j, my_half, rows(me, c), (ox, oy, c)) for j, (ox, oy) in enumerate(others)]
        for cp in first:
            cp.start()
        passed = []
        for j, (ox, oy) in enumerate(others):
            blk = rows(2 * ox + oy, c)
            copy(j, blk, blk, (x, y, c)).wait_recv()
            fwd = copy(3 + j, blk, blk, sib)
            fwd.start()
            passed.append(fwd)
        for j, (ox, oy) in enumerate(others):
            blk = rows(2 * ox + oy, 1 - c)
            copy(3 + j, blk, blk, (x, y, c)).wait_recv()
        for cp in first + passed:
            cp.wait_send()
        mine.wait()

    return pl.pallas_call(
        body, name=name,
        in_specs=[_ANY], out_specs=_ANY,
        out_shape=jax.ShapeDtypeStruct((N_CHIPS, rws, cols), shard.dtype),
        scratch_shapes=[pltpu.SemaphoreType.DMA((6,)), pltpu.SemaphoreType.DMA((6,)), pltpu.SemaphoreType.DMA],
        compiler_params=pltpu.CompilerParams(has_side_effects=True),
    )(shard)


def _swap_halves(buf, name):
    nb, rws, cols = buf.shape
    half = rws // 2

    def body(b_ref, own_ref, sib_ref, send_sem, recv_sem, local_sem):
        x, y, c, _ = _place()
        keep = b_ref.at[:, pl.ds(pl.multiple_of(c * half, 16), half), :]
        give = b_ref.at[:, pl.ds(pl.multiple_of((1 - c) * half, 16), half), :]
        mine = pltpu.make_async_copy(keep, own_ref, local_sem)
        mine.start()
        cp = pltpu.make_async_remote_copy(src_ref=give, dst_ref=sib_ref, send_sem=send_sem, recv_sem=recv_sem,
                                          device_id=(x, y, 1 - c), device_id_type=MESH)
        cp.start()
        cp.wait()
        mine.wait()

    shp = jax.ShapeDtypeStruct((nb, half, cols), buf.dtype)
    return pl.pallas_call(
        body, name=name,
        in_specs=[_ANY], out_specs=[_ANY, _ANY], out_shape=[shp, shp],
        scratch_shapes=[pltpu.SemaphoreType.DMA, pltpu.SemaphoreType.DMA, pltpu.SemaphoreType.DMA],
        compiler_params=pltpu.CompilerParams(has_side_effects=True),
    )(buf)


def _add2(a, b, name):
    nb, rws, cols = a.shape
    t = _tile(rws, 256)
    if rws % t:
        t = rws

    def body(a_ref, b_ref, o_ref):
        o_ref[...] = (a_ref[...].astype(F32) + b_ref[...].astype(F32)).astype(o_ref.dtype)

    blk = pl.BlockSpec((1, t, cols), lambda i, j: (i, j, 0))
    return pl.pallas_call(
        body, name=name, grid=(nb, rws // t), in_specs=[blk, blk], out_specs=blk,
        out_shape=jax.ShapeDtypeStruct(a.shape, a.dtype),
        compiler_params=_params(("parallel", "parallel")),
    )(a, b)


def _exchange_chips(pre, name):
    nb, half, cols = pre.shape

    def body(p_ref, out_ref, send_sems, recv_sems, local_sem):
        x, y, c, others = _place()
        me = 2 * x + y
        mine = pltpu.make_async_copy(p_ref.at[me], out_ref.at[me], local_sem)
        mine.start()
        sends = []
        for j, (ox, oy) in enumerate(others):
            cp = pltpu.make_async_remote_copy(src_ref=p_ref.at[2 * ox + oy], dst_ref=out_ref.at[me],
                                              send_sem=send_sems.at[j], recv_sem=recv_sems.at[j],
                                              device_id=(ox, oy, c), device_id_type=MESH)
            cp.start()
            sends.append(cp)
        for j, (ox, oy) in enumerate(others):
            blk = out_ref.at[2 * ox + oy]
            pltpu.make_async_remote_copy(src_ref=blk, dst_ref=blk, send_sem=send_sems.at[j],
                                         recv_sem=recv_sems.at[j], device_id=(x, y, c),
                                         device_id_type=MESH).wait_recv()
        for cp in sends:
            cp.wait_send()
        mine.wait()

    return pl.pallas_call(
        body, name=name,
        in_specs=[_ANY], out_specs=_ANY, out_shape=jax.ShapeDtypeStruct(pre.shape, pre.dtype),
        scratch_shapes=[pltpu.SemaphoreType.DMA((3,)), pltpu.SemaphoreType.DMA((3,)), pltpu.SemaphoreType.DMA],
        compiler_params=pltpu.CompilerParams(has_side_effects=True),
    )(pre)


def _add4(parts, name):
    nb, half, cols = parts.shape
    t = _tile(half, 256)
    if half % t:
        t = half

    def body(p_ref, o_ref):
        acc = p_ref[0].astype(F32)
        for k in range(1, nb):
            acc = acc + p_ref[k].astype(F32)
        o_ref[...] = acc

    return pl.pallas_call(
        body, name=name, grid=(half // t,),
        in_specs=[pl.BlockSpec((nb, t, cols), lambda i: (0, i, 0))],
        out_specs=pl.BlockSpec((t, cols), lambda i: (i, 0)),
        out_shape=jax.ShapeDtypeStruct((half, cols), F32),
        compiler_params=_params(("parallel",)),
    )(parts)


def _join_halves(mine_half, name):
    half, cols = mine_half.shape

    def body(h_ref, out_ref, send_sem, recv_sem, local_sem):
        x, y, c, _ = _place()
        dst = out_ref.at[pl.ds(pl.multiple_of(c * half, 8), half), :]
        mine = pltpu.make_async_copy(h_ref, dst, local_sem)
        mine.start()
        cp = pltpu.make_async_remote_copy(src_ref=h_ref, dst_ref=dst, send_sem=send_sem, recv_sem=recv_sem,
                                          device_id=(x, y, 1 - c), device_id_type=MESH)
        cp.start()
        cp.wait()
        mine.wait()

    return pl.pallas_call(
        body, name=name,
        in_specs=[_ANY], out_specs=_ANY, out_shape=jax.ShapeDtypeStruct((2 * half, cols), mine_half.dtype),
        scratch_shapes=[pltpu.SemaphoreType.DMA, pltpu.SemaphoreType.DMA, pltpu.SemaphoreType.DMA],
        compiler_params=pltpu.CompilerParams(has_side_effects=True),
    )(mine_half)


def _reduce_scatter_chips(buf, tag):
    own, sib = _swap_halves(buf, "rs_swap_" + tag)
    pre = _add2(own, sib, "rs_add2_" + tag)
    parts = _exchange_chips(pre, "rs_xchg_" + tag)
    red = _add4(parts, "rs_add4_" + tag)
    return _join_halves(red, "rs_join_" + tag)


MAX_DMA_BYTES = 2 * 1024 * 1024
ROW_ALIGN = 16


def _pieces(rows, row_bytes):
    n = max(1, -(-(rows * row_bytes) // MAX_DMA_BYTES))
    step = -(-(-(-rows // n)) // ROW_ALIGN) * ROW_ALIGN
    return [(r, min(step, rows - r)) for r in range(0, rows, step)]


def _half_plan(arrays, row_axis):
    plan = []
    for a, arr in enumerate(arrays):
        row_bytes = math.prod(arr.shape[row_axis + 1:]) * arr.dtype.itemsize * (arr.shape[0] if row_axis else 1)
        plan += [(a, r0, nr) for r0, nr in _pieces(arr.shape[row_axis] // 2, row_bytes)]
    return plan


def _rows(start, size):
    return pl.ds(pl.multiple_of(start, ROW_ALIGN), size)


def _remote(src, dst, send_sems, recv_sems, k, to):
    return pltpu.make_async_remote_copy(src_ref=src, dst_ref=dst, send_sem=send_sems.at[k], recv_sem=recv_sems.at[k],
                                        device_id=to, device_id_type=MESH)


def _comm_call(body, name, ins, out_shapes, n_remote, n_local, aliases=None):
    return pl.pallas_call(
        body, name=name,
        in_specs=[_ANY] * len(ins), out_specs=[_ANY] * len(out_shapes), out_shape=out_shapes,
        scratch_shapes=[pltpu.SemaphoreType.DMA((n_remote,)), pltpu.SemaphoreType.DMA((n_remote,)),
                        pltpu.SemaphoreType.DMA((max(n_local, 1),))],
        input_output_aliases=aliases or {},
        compiler_params=pltpu.CompilerParams(has_side_effects=True),
    )(*ins)


def _cast_shard(w, l, me_idx, name):
    _, k, cols = w.shape
    tr = _tile(k, 256)
    if k % tr:
        tr = k

    def body(me_ref, w_ref, s_ref, land_ref):
        del me_ref
        v = w_ref[...].astype(BF16)
        s_ref[...] = v
        land_ref[...] = v

    grid_spec = pltpu.PrefetchScalarGridSpec(
        num_scalar_prefetch=1, grid=(k // tr,),
        in_specs=[pl.BlockSpec((None, tr, cols), lambda i, me: (l, i, 0))],
        out_specs=[pl.BlockSpec((tr, cols), lambda i, me: (i, 0)),
                   pl.BlockSpec((None, tr, cols), lambda i, me: (me[0], i, 0))])
    return pl.pallas_call(
        body, name=name, grid_spec=grid_spec,
        out_shape=[jax.ShapeDtypeStruct((k, cols), BF16), jax.ShapeDtypeStruct((N_CHIPS, k, cols), BF16)],
        compiler_params=_params(("parallel",)),
    )(me_idx, w)


def _gather_d2d(lands, name):
    n = len(lands)
    plan = _half_plan(lands, 1)
    plan = [(a, r0, nr) for a, r0, nr in plan]

    def body(*refs):
        out_refs = refs[n:2 * n]
        send_sems, recv_sems, _ = refs[2 * n:]
        x, y, c, others = _place()
        sends = []
        for i, (a, r0, nr) in enumerate(plan):
            rows = _rows(c * (lands[a].shape[1] // 2) + r0, nr)
            for j, (ox, oy) in enumerate(others):
                blk = out_refs[a].at[2 * ox + oy, rows, :]
                cp = _remote(blk, blk, send_sems, recv_sems, 3 * i + j, (x, y, 1 - c))
                cp.start()
                sends.append(cp)
        for i, (a, r0, nr) in enumerate(plan):
            rows = _rows((1 - c) * (lands[a].shape[1] // 2) + r0, nr)
            for j, (ox, oy) in enumerate(others):
                blk = out_refs[a].at[2 * ox + oy, rows, :]
                _remote(blk, blk, send_sems, recv_sems, 3 * i + j, (x, y, c)).wait_recv()
        for cp in sends:
            cp.wait_send()

    outs = [jax.ShapeDtypeStruct(a.shape, a.dtype) for a in lands]
    return _comm_call(body, name, lands, outs, 3 * len(plan), 0, aliases={a: a for a in range(n)})


def _rs_swap(ts, name):
    n = len(ts)
    plan = _half_plan(ts, 1)

    def body(*refs):
        t_refs, out_refs = refs[:n], refs[n:2 * n]
        send_sems, recv_sems, _ = refs[2 * n:]
        x, y, c, _o = _place()
        sends = []
        for i, (a, r0, nr) in enumerate(plan):
            src = t_refs[a].at[:, _rows((1 - c) * (ts[a].shape[1] // 2) + r0, nr), :]
            cp = _remote(src, out_refs[a].at[:, pl.ds(r0, nr), :], send_sems, recv_sems, i, (x, y, 1 - c))
            cp.start()
            sends.append(cp)
        for i, (a, r0, nr) in enumerate(plan):
            blk = out_refs[a].at[:, pl.ds(r0, nr), :]
            _remote(blk, blk, send_sems, recv_sems, i, (x, y, c)).wait_recv()
        for cp in sends:
            cp.wait_send()

    outs = [jax.ShapeDtypeStruct((t.shape[0], t.shape[1] // 2, t.shape[2]), t.dtype) for t in ts]
    return _comm_call(body, name, ts, outs, len(plan), 0)


def _add_half(t, got, c_idx, me_idx, name):
    nb, k, cols = t.shape
    half = k // 2

    def body(c_ref, me_ref, t_ref, g_ref, o_ref, mine_ref):
        del c_ref
        v = (t_ref[...].astype(F32) + g_ref[...].astype(F32)).astype(o_ref.dtype)
        o_ref[...] = v

        @pl.when(pl.program_id(0) == me_ref[0])
        def _():
            mine_ref[...] = v

    blk = pl.BlockSpec((1, half, cols), lambda i, c, me: (i, 0, 0))
    grid_spec = pltpu.PrefetchScalarGridSpec(
        num_scalar_prefetch=2, grid=(nb,),
        in_specs=[pl.BlockSpec((1, half, cols), lambda i, c, me: (i, c[0], 0)), blk],
        out_specs=[blk, pl.BlockSpec((1, half, cols), lambda i, c, me: (me[0], 0, 0))])
    shp = jax.ShapeDtypeStruct(got.shape, got.dtype)
    return pl.pallas_call(
        body, name=name, grid_spec=grid_spec, out_shape=[shp, shp],
        compiler_params=_params(("arbitrary",)),
    )(c_idx, me_idx, t, got)


def _add4_half(parts, c_idx, name):
    nb, half, cols = parts.shape
    t = _tile(half, 256)
    if half % t:
        t = half
    steps = half // t

    def body(c_ref, p_ref, o_ref):
        del c_ref
        acc = p_ref[0].astype(F32)
        for k in range(1, nb):
            acc = acc + p_ref[k].astype(F32)
        o_ref[...] = acc

    grid_spec = pltpu.PrefetchScalarGridSpec(
        num_scalar_prefetch=1, grid=(steps,),
        in_specs=[pl.BlockSpec((nb, t, cols), lambda i, c: (0, i, 0))],
        out_specs=pl.BlockSpec((t, cols), lambda i, c: (c[0] * steps + i, 0)))
    return pl.pallas_call(
        body, name=name, grid_spec=grid_spec, out_shape=jax.ShapeDtypeStruct((2 * half, cols), F32),
        compiler_params=_params(("parallel",)),
    )(c_idx, parts)


def _rs_join(fulls, name):
    n = len(fulls)
    plan = _half_plan(fulls, 0)

    def body(*refs):
        out_refs = refs[n:2 * n]
        send_sems, recv_sems, _ = refs[2 * n:]
        x, y, c, _o = _place()
        sends = []
        for i, (a, r0, nr) in enumerate(plan):
            blk = out_refs[a].at[_rows(c * (fulls[a].shape[0] // 2) + r0, nr), :]
            cp = _remote(blk, blk, send_sems, recv_sems, i, (x, y, 1 - c))
            cp.start()
            sends.append(cp)
        for i, (a, r0, nr) in enumerate(plan):
            blk = out_refs[a].at[_rows((1 - c) * (fulls[a].shape[0] // 2) + r0, nr), :]
            _remote(blk, blk, send_sems, recv_sems, i, (x, y, c)).wait_recv()
        for cp in sends:
            cp.wait_send()

    outs = [jax.ShapeDtypeStruct(f.shape, f.dtype) for f in fulls]
    return _comm_call(body, name, fulls, outs, len(plan), 0, aliases={a: a for a in range(n)})


_HBM = pl.BlockSpec(memory_space=pltpu.HBM)
_SEM = pl.BlockSpec(memory_space=pltpu.SEMAPHORE)
_EFFECT = pltpu.SideEffectType.DATAFLOW_SIDE_EFFECTING


def _ici_plan(kind, a_list):
    if kind == "gather":
        return _half_plan(a_list, 0)
    plan = []
    for a, p in enumerate(a_list):
        plan += [(a, r0, nr) for r0, nr in _pieces(p.shape[1], p.shape[2] * p.dtype.itemsize)]
    return plan


def _ici_refs(kind, a_ref, b_ref, a_shape, r0, nr, c, me, peer):
    if kind == "gather":
        rows = _rows(c * (a_shape[0] // 2) + r0, nr)
        return a_ref.at[rows, :], b_ref.at[me, rows, :], b_ref.at[peer, rows, :]
    rows = pl.ds(r0, nr)
    return a_ref.at[peer, rows, :], b_ref.at[me, rows, :], b_ref.at[peer, rows, :]


def _ici_start(kind, a_list, b_list, name):
    n = len(a_list)
    plan = _ici_plan(kind, a_list)
    shapes = [a.shape for a in a_list]

    def body(*refs):
        a_refs, b_refs = refs[:n], refs[n:2 * n]
        send_sems, recv_sems = refs[2 * n], refs[2 * n + 1]
        token = refs[4 * n + 2]
        x, y, c, others = _place()
        me = 2 * x + y
        for i, (a, r0, nr) in enumerate(plan):
            for j, (ox, oy) in enumerate(others):
                src, dst, _ = _ici_refs(kind, a_refs[a], b_refs[a], shapes[a], r0, nr, c, me, 2 * ox + oy)
                _remote(src, dst, send_sems, recv_sems, 3 * i + j, (ox, oy, c)).start()
        token[...] = jnp.zeros_like(token)

    hbm = lambda v: pltpu.HBM(v.shape, v.dtype)
    ncp = 3 * len(plan)
    outs = pl.pallas_call(
        body, name=name,
        in_specs=[_HBM] * (2 * n),
        out_specs=[_SEM, _SEM] + [_HBM] * (2 * n) + [pl.BlockSpec(memory_space=pltpu.VMEM)],
        out_shape=[pltpu.SemaphoreType.DMA((ncp,)), pltpu.SemaphoreType.DMA((ncp,))]
                  + [hbm(v) for v in a_list] + [hbm(v) for v in b_list] + [jax.ShapeDtypeStruct((8, LANES), F32)],
        input_output_aliases={i: 2 + i for i in range(2 * n)},
        compiler_params=pltpu.CompilerParams(has_side_effects=_EFFECT),
    )(*[pltpu.with_memory_space_constraint(v, pltpu.HBM) for v in list(a_list) + list(b_list)])
    return outs[0], outs[1], outs[2:2 + n], outs[2 + n:2 + 2 * n], outs[2 + 2 * n]


def _ici_wait(kind, started, after, name):
    send_sems, recv_sems, a_list, b_list, _ = started
    n = len(a_list)
    plan = _ici_plan(kind, a_list)
    shapes = [a.shape for a in a_list]

    def body(*refs):
        a_refs, b_refs = refs[:n], refs[n:2 * n]
        send_sems, recv_sems = refs[2 * n], refs[2 * n + 1]
        x, y, c, others = _place()
        me = 2 * x + y
        for i, (a, r0, nr) in enumerate(plan):
            for j, (ox, oy) in enumerate(others):
                src, dst, land = _ici_refs(kind, a_refs[a], b_refs[a], shapes[a], r0, nr, c, me, 2 * ox + oy)
                _remote(src, dst, send_sems, recv_sems, 3 * i + j, (ox, oy, c)).wait_send()
                _remote(land, land, send_sems, recv_sems, 3 * i + j, (x, y, c)).wait_recv()

    hbm = lambda v: pltpu.HBM(v.shape, v.dtype)
    outs = pl.pallas_call(
        body, name=name,
        in_specs=[_HBM] * (2 * n) + [_SEM, _SEM, _ANY],
        out_specs=[_HBM] * (2 * n),
        out_shape=[hbm(v) for v in a_list] + [hbm(v) for v in b_list],
        input_output_aliases={i: i for i in range(2 * n)},
        compiler_params=pltpu.CompilerParams(has_side_effects=_EFFECT),
    )(*a_list, *b_list, send_sems, recv_sems, after)
    return outs[n:]


def _rs_begin(ts, c_idx, me_idx, tag):
    got = _rs_swap(ts, "rs_swap_" + tag)
    pairs = [_add_half(t, g, c_idx, me_idx, f"rs_add2_{tag}_{a}") for a, (t, g) in enumerate(zip(ts, got))]
    return _ici_start("scatter", [p for p, _ in pairs], [m for _, m in pairs], "rs_xchg_start_" + tag)


def _rs_finish(started, after, c_idx, tag):
    parts = _ici_wait("scatter", started, after, "rs_xchg_wait_" + tag)
    fulls = [_add4_half(p, c_idx, f"rs_add4_{tag}_{a}") for a, p in enumerate(parts)]
    return _rs_join(fulls, "rs_join_" + tag)


def _pack_rows(pieces, rows, dtype):
    flat = jnp.concatenate([p.astype(dtype).reshape(-1) for p in pieces])
    return jnp.pad(flat, (0, rows * PACK_COLS - flat.shape[0])).reshape(rows, PACK_COLS)


def _unpack(flat, shapes):
    out, off = [], 0
    for shp in shapes:
        size = math.prod(shp)
        out.append(flat[off:off + size].reshape(shp))
        off += size
    return out


def _rows_for(n_elems, mult):
    rows = -(-n_elems // PACK_COLS)
    return -(-rows // mult) * mult


BIG_SHARDS = [("w_in", (D_MODEL, 1474)), ("w_branch_att", (D_ATT, 256)), ("w_branch_conv", (D_CONV, 256)),
              ("w_branch_sgu", (D_SGU, 256)), ("w_out", (256, D_MODEL)), ("w_ffn_up", (D_MODEL, FF_BLK)),
              ("w_ffn_down", (D_FF // N_CHIPS, D_MODEL))]
SMALL_SHARDS = [("b_gate", (3, 256)), ("conv_mix_w", (3, 64)), ("conv_ffn_w", (3, FF_BLK))]
REPLICATED = [("pre_mix_g", (D_MODEL,)), ("post_mix_g", (D_MODEL,)), ("pre_ffn_g", (D_MODEL,)),
              ("post_ffn_g", (D_MODEL,)), ("b_forget", (N_HEADS,)), ("sgu_ln_g", (D_SGU,)), ("sgu_ln_b", (D_SGU,)),
              ("sgu_w", (N_GROUPS, CHUNK, CHUNK)), ("sgu_b", (N_GROUPS, CHUNK))]
WEIGHT_ORDER = ["pre_mix_g", "post_mix_g", "pre_ffn_g", "post_ffn_g", "w_in", "b_forget", "b_gate", "conv_mix_w",
                "sgu_ln_g", "sgu_ln_b", "sgu_w", "sgu_b", "w_branch_att", "w_branch_conv", "w_branch_sgu", "w_out",
                "w_ffn_up", "conv_ffn_w", "w_ffn_down"]

_SMALL_ELEMS = sum(math.prod(s) for _, s in SMALL_SHARDS)
_REP_ELEMS = sum(math.prod(s) for _, s in REPLICATED)
_REP_QUARTER = -(-(DEPTH * _REP_ELEMS) // N_CHIPS)
SMALL_PARAM_ROWS = _rows_for(DEPTH * _SMALL_ELEMS, 32)
SMALL_ROWS = _rows_for(DEPTH * _SMALL_ELEMS + _REP_QUARTER, 32)
IN_WIDTH = 5896
IN_SHARD = IN_WIDTH // N_CHIPS
IN_SHARD_PAD = 1536
IN_PAD = 6144


def _gather_small(wts):
    shard = _pack_rows([wts[n] for n, _ in SMALL_SHARDS], SMALL_PARAM_ROWS, F32)
    full = _all_gather_chips(shard, "gather_small_params").reshape(N_CHIPS, -1)
    per_chip = [_unpack(full[j], [(DEPTH,) + s for _, s in SMALL_SHARDS]) for j in range(N_CHIPS)]
    return {n: jnp.concatenate([per_chip[j][i] for j in range(N_CHIPS)], axis=-1)
            for i, (n, _) in enumerate(SMALL_SHARDS)}


BIG_NAMES = [n for n, _ in BIG_SHARDS]
FIRST_NAMES = ["w_in"]
LATE_NAMES = BIG_NAMES[1:]


def _gather_begin(wts, l, me_idx, names, tag):
    cast = [_cast_shard(wts[n], l, me_idx, "cast_" + n) for n in names]
    return _ici_start("gather", [sh for sh, _ in cast], [ld for _, ld in cast], "gather_ici_start_" + tag)


def _gather_finish(started, after, names, tag):
    lands = _ici_wait("gather", started, after, "gather_ici_wait_" + tag)
    return dict(zip(names, _gather_d2d(lands, "gather_d2d_" + tag)))


def _pad_rows(a, rows):
    return jnp.pad(a, ((0, rows - a.shape[0]), (0, 0)))


def _whole_cols(land):
    return land.transpose(1, 0, 2).reshape(land.shape[1], -1)


_O_F = 3 * D_ATT
_O_B = _O_F + N_HEADS
_O_GL = _O_B + 3 * D_CONV + 2 * D_SGU


def _prep_first(wts, lands, small, l):
    w_in = _whole_cols(lands["w_in"])
    cf = small["conv_ffn_w"][l]
    blk = lambda a, j: a[:, j * FF_BLK:(j + 1) * FF_BLK]
    return {
        "w_p": jnp.concatenate([w_in[:, _O_GL:], w_in[:, :_O_F], w_in[:, _O_B:_O_GL], w_in[:, _O_F:_O_B],
                                jnp.zeros((D_MODEL, IN_PAD - IN_WIDTH), BF16)], axis=1),
        "wf_t": _pad_rows(w_in[:, _O_F:_O_B].T, F_ROWS),
        "b_forget": _pad_rows(wts["b_forget"][l].reshape(N_HEADS, 1), F_ROWS),
        "b_gate": _pad_rows(small["b_gate"][l], 8),
        "conv_mix_w": _pad_rows(small["conv_mix_w"][l], 8),
        "conv_ffn_w": _pad_rows(jnp.concatenate([blk(cf, 0), blk(cf, 2), blk(cf, 1), blk(cf, 3)], axis=1), 8),
        "pre_mix_g": wts["pre_mix_g"][l].reshape(1, -1), "post_mix_g": wts["post_mix_g"][l].reshape(1, -1),
        "pre_ffn_g": wts["pre_ffn_g"][l].reshape(1, -1), "post_ffn_g": wts["post_ffn_g"][l].reshape(1, -1),
        "ln_g": wts["sgu_ln_g"][l].reshape(1, -1), "ln_b": wts["sgu_ln_b"][l].reshape(1, -1),
        "sgu_w": wts["sgu_w"][l],
        "sgu_bias": jnp.repeat(wts["sgu_b"][l].T, HEAD_DIM, axis=1),
    }


def _prep_late(lands):
    up = lands["w_ffn_up"]
    return {
        "w_att": _whole_cols(lands["w_branch_att"]), "w_conv": _whole_cols(lands["w_branch_conv"]),
        "w_sgu": _whole_cols(lands["w_branch_sgu"]),
        "w_out": lands["w_out"].reshape(D_MODEL, D_MODEL),
        "w_up": jnp.concatenate([up[0], up[2], up[1], up[3]], axis=1),
        "w_down": lands["w_ffn_down"].reshape(D_FF, D_MODEL),
    }


def _layer_fwd(x, p, dep=None, late=None):
    s = x.shape[0]
    xn = _rms_fwd(x, p["pre_mix_g"], "rms_pre_mix", dep)
    h = _mm(xn, p["w_p"], "nn", BF16, "mm_in", s, 512, D_MODEL)
    f_row = _mm(p["wf_t"], xn, "nt", F32, "mm_forget", F_ROWS, 2048, D_MODEL)
    ck = _gate_fwd(f_row, p["b_forget"], "gate_fwd")
    o, o_f32, lse = _attn_fwd(h, ck, "attn_fwd")
    yc = _sconv_fwd(h, p["conv_mix_w"], "sconv_fwd")
    ys = _sgu_fwd(h, p["ln_g"], p["ln_b"], p["sgu_w"], p["sgu_bias"], "sgu_fwd")
    if late is not None:
        p.update(late(o))
    merged = _merge_fwd(h, (o, yc, ys), (p["w_att"], p["w_conv"], p["w_sgu"]), p["b_gate"], "merge_fwd")
    mo = _mm(merged, p["w_out"], "nn", F32, "mm_out", 2048, 512, D_MODEL)
    x1 = _resid_post(x, mo, p["post_mix_g"], "post_mix")
    xn2 = _rms_fwd(x1, p["pre_ffn_g"], "rms_pre_ffn")
    h2 = _mm(xn2, p["w_up"], "nn", BF16, "mm_up", 2048, 512, D_MODEL)
    pact = _ffn_act_fwd(h2, p["conv_ffn_w"], "ffn_act_fwd")
    ff = _mm(pact, p["w_down"], "nn", F32, "mm_down", 2048, 512, FF_BLK)
    x2 = _resid_post(x1, ff, p["post_ffn_g"], "post_ffn")
    saved = dict(x=x, xn=xn, h=h, f_row=f_row, ck=ck, o=o, o_f32=o_f32, lse=lse, yc=yc, ys=ys, merged=merged, mo=mo, x1=x1,
                 xn2=xn2, h2=h2, pact=pact, ff=ff)
    return x2, saved


def _layer_bwd(dx2, p, sv, dep=None, early=None):
    s = dx2.shape[0]
    g = {}
    same = lambda b: b
    dff, g["post_ffn_g"] = _rms_bwd(sv["ff"], p["post_ffn_g"], [dx2], None, BF16, "post_ffn_bwd", dep)
    dpact = _mm(dff, p["w_down"], "nt", BF16, "mm_down_dx", 1024, FF_BLK, D_MODEL)
    t_down = _mm(sv["pact"], dff, "tn", BF16, "mm_down_dw", 256, D_MODEL, s).reshape(N_CHIPS, -1, D_MODEL)
    dh2, dconv_ffn = _ffn_act_conv_bwd(sv["h2"], p["conv_ffn_w"], dpact, "ffn_act_conv_bwd")
    dxn2 = _mm(dh2, p["w_up"], "nt", F32, "mm_up_dx", 1024, D_MODEL, FF_BLK)
    t_up = _mm(sv["xn2"], dh2, "tn", BF16, "mm_up_dw", 512, FF_BLK, s, chip_of=lambda b: (b % 2) * 2 + b // 2)
    dx1, g["pre_ffn_g"] = _rms_bwd(sv["x1"], p["pre_ffn_g"], [dxn2], dx2, F32, "pre_ffn_bwd")
    dep_mix = early([t_up, t_down]) if early is not None else None
    dmo, g["post_mix_g"] = _rms_bwd(sv["mo"], p["post_mix_g"], [dx1], None, BF16, "post_mix_bwd", dep_mix)
    dmerged = _mm(dmo, p["w_out"], "nt", F32, "mm_out_dx", 2048, 512, D_MODEL)
    t_out = _mm(sv["merged"], dmo, "tn", BF16, "mm_out_dw", 512, D_MODEL, s).reshape(N_CHIPS, -1, D_MODEL)
    acts = (sv["o"], sv["yc"], sv["ys"])
    ws = (p["w_att"], p["w_conv"], p["w_sgu"])
    dy_a, dy_c, dy_s, dgl, db_gate = _merge_bwd(sv["h"], acts, ws, p["b_gate"], dmerged, "merge_bwd")
    do = _mm(dy_a, p["w_att"], "nt", BF16, "mm_att_dx", 2048, D_ATT, D_MODEL)
    dyc = _mm(dy_c, p["w_conv"], "nt", BF16, "mm_conv_dx", 2048, D_CONV, D_MODEL)
    dys = _mm(dy_s, p["w_sgu"], "nt", BF16, "mm_sgu_dx", 2048, D_SGU, D_MODEL)
    t_att = _mm(sv["o"], dy_a, "tn", BF16, "mm_att_dw", D_ATT, 256, s, chip_of=same)
    t_conv = _mm(sv["yc"], dy_c, "tn", BF16, "mm_conv_dw", D_CONV, 256, s, chip_of=same)
    t_sgu = _mm(sv["ys"], dy_s, "tn", BF16, "mm_sgu_dw", D_SGU, 256, s, chip_of=same)
    d_conv, dconv_mix = _sconv_bwd(sv["h"], p["conv_mix_w"], dyc, "sconv_bwd")
    d_sgu, g["sgu_ln_g"], g["sgu_ln_b"], g["sgu_w"], dbias = _sgu_bwd(
        sv["h"], p["ln_g"], p["ln_b"], p["sgu_w"], p["sgu_bias"], dys, "sgu_bwd")
    dq, dk, dv, dc_even, dc_odd = _attn_bwd(sv["h"], sv["ck"], sv["o_f32"], sv["lse"], do, "attn_bwd")
    df, db_forget = _gate_bwd(sv["f_row"], p["b_forget"], dc_even, dc_odd, "gate_bwd")
    f_cols = jnp.concatenate([df[:N_HEADS].T, jnp.zeros((s, IN_PAD - IN_WIDTH), BF16)], axis=1)
    dh = _assemble_dh([dgl, dq, dk, dv, d_conv, d_sgu, f_cols], "assemble_dh")
    dxn = _mm(dh, p["w_p"], "nt", F32, "mm_in_dx", 1024, D_MODEL, 2048)
    dw_p = _mm(sv["xn"], dh, "tn", BF16, "mm_in_dw", D_MODEL, 512, s)
    dw_in = jnp.concatenate([dw_p[:, OFF_Q:OFF_BG], dw_p[:, W_P:W_P + N_HEADS], dw_p[:, OFF_BG:W_P], dw_p[:, :OFF_Q],
                             jnp.zeros((D_MODEL, IN_SHARD_PAD - IN_SHARD), BF16)], axis=1)
    t_in = jnp.stack([dw_in[:, j * IN_SHARD:j * IN_SHARD + IN_SHARD_PAD] for j in range(N_CHIPS)])
    dx, g["pre_mix_g"] = _rms_bwd(sv["x"], p["pre_mix_g"], [dxn], dx1, F32, "pre_mix_bwd")
    blk = lambda a, j: a[:, j * FF_BLK:(j + 1) * FF_BLK]
    g["conv_ffn_w"] = jnp.concatenate([blk(dconv_ffn, 0), blk(dconv_ffn, 2), blk(dconv_ffn, 1),
                                       blk(dconv_ffn, 3)], axis=1)[:3]
    g["conv_mix_w"] = dconv_mix[:3]
    g["b_gate"] = db_gate[:3]
    g["b_forget"] = db_forget[:N_HEADS, 0]
    g["sgu_b"] = jnp.sum(dbias.reshape(CHUNK, N_GROUPS, HEAD_DIM), axis=-1).T
    for n in ("pre_mix_g", "post_mix_g", "pre_ffn_g", "post_ffn_g", "sgu_ln_g", "sgu_ln_b"):
        g[n] = g[n].reshape(-1)
    mix = [t_in, t_att, t_conv, t_sgu, t_out]
    return dx, (mix if early is not None else mix + [t_up, t_down]), g


def _assemble_dh(pieces, name):
    s = pieces[0].shape[0]
    t = _tile(s, 512)
    width = sum(a.shape[1] for a in pieces)

    def body(*refs):
        out = refs[-1]
        col = 0
        for ref in refs[:-1]:
            w = ref.shape[1]
            out[:, col:col + w] = ref[...].astype(out.dtype)
            col += w

    return pl.pallas_call(
        body, name=name, grid=(s // t,),
        in_specs=[pl.BlockSpec((t, a.shape[1]), lambda i: (i, 0)) for a in pieces],
        out_specs=pl.BlockSpec((t, width), lambda i: (i, 0)),
        out_shape=jax.ShapeDtypeStruct((s, width), BF16),
        compiler_params=_params(("parallel",)),
    )(*pieces)


def _shard_cols(a, j):
    w = a.shape[-1] // N_CHIPS
    return a[..., j * w:(j + 1) * w]


def kernel(x, pre_mix_g, post_mix_g, pre_ffn_g, post_ffn_g, w_in, b_forget, b_gate, conv_mix_w, sgu_ln_g, sgu_ln_b, sgu_w, sgu_b, w_branch_att, w_branch_conv, w_branch_sgu, w_out, w_ffn_up, conv_ffn_w, w_ffn_down, loss_target, m_pre_mix_g, m_post_mix_g, m_pre_ffn_g, m_post_ffn_g, m_w_in, m_b_forget, m_b_gate, m_conv_mix_w, m_sgu_ln_g, m_sgu_ln_b, m_sgu_w, m_sgu_b, m_w_branch_att, m_w_branch_conv, m_w_branch_sgu, m_w_out, m_w_ffn_up, m_conv_ffn_w, m_w_ffn_down, v_pre_mix_g, v_post_mix_g, v_pre_ffn_g, v_post_ffn_g, v_w_in, v_b_forget, v_b_gate, v_conv_mix_w, v_sgu_ln_g, v_sgu_ln_b, v_sgu_w, v_sgu_b, v_w_branch_att, v_w_branch_conv, v_w_branch_sgu, v_w_out, v_w_ffn_up, v_conv_ffn_w, v_w_ffn_down):
    wts = dict(pre_mix_g=pre_mix_g, post_mix_g=post_mix_g, pre_ffn_g=pre_ffn_g, post_ffn_g=post_ffn_g, w_in=w_in,
               b_forget=b_forget, b_gate=b_gate, conv_mix_w=conv_mix_w, sgu_ln_g=sgu_ln_g, sgu_ln_b=sgu_ln_b,
               sgu_w=sgu_w, sgu_b=sgu_b, w_branch_att=w_branch_att, w_branch_conv=w_branch_conv,
               w_branch_sgu=w_branch_sgu, w_out=w_out, w_ffn_up=w_ffn_up, conv_ffn_w=conv_ffn_w,
               w_ffn_down=w_ffn_down)
    moms = dict(pre_mix_g=m_pre_mix_g, post_mix_g=m_post_mix_g, pre_ffn_g=m_pre_ffn_g, post_ffn_g=m_post_ffn_g,
                w_in=m_w_in, b_forget=m_b_forget, b_gate=m_b_gate, conv_mix_w=m_conv_mix_w, sgu_ln_g=m_sgu_ln_g,
                sgu_ln_b=m_sgu_ln_b, sgu_w=m_sgu_w, sgu_b=m_sgu_b, w_branch_att=m_w_branch_att,
                w_branch_conv=m_w_branch_conv, w_branch_sgu=m_w_branch_sgu, w_out=m_w_out, w_ffn_up=m_w_ffn_up,
                conv_ffn_w=m_conv_ffn_w, w_ffn_down=m_w_ffn_down)
    vels = dict(pre_mix_g=v_pre_mix_g, post_mix_g=v_post_mix_g, pre_ffn_g=v_pre_ffn_g, post_ffn_g=v_post_ffn_g,
                w_in=v_w_in, b_forget=v_b_forget, b_gate=v_b_gate, conv_mix_w=v_conv_mix_w, sgu_ln_g=v_sgu_ln_g,
                sgu_ln_b=v_sgu_ln_b, sgu_w=v_sgu_w, sgu_b=v_sgu_b, w_branch_att=v_w_branch_att,
                w_branch_conv=v_w_branch_conv, w_branch_sgu=v_w_branch_sgu, w_out=v_w_out, w_ffn_up=v_w_ffn_up,
                conv_ffn_w=v_conv_ffn_w, w_ffn_down=v_w_ffn_down)

    c_idx = lax.axis_index("c").astype(jnp.int32).reshape(1)
    me_idx = (2 * lax.axis_index("x") + lax.axis_index("y")).astype(jnp.int32).reshape(1)
    small = _gather_small(wts)

    xs = x[0]
    layers, saved = [], []
    first = _gather_begin(wts, 0, me_idx, FIRST_NAMES, "first")
    rest = _gather_begin(wts, 0, me_idx, LATE_NAMES, "late")
    lands = _gather_finish(first, xs, FIRST_NAMES, "first")
    late = lambda after: _prep_late(_gather_finish(rest, after, LATE_NAMES, "late"))
    for l in range(DEPTH):
        p = _prep_first(wts, lands, small, l)
        if l > 0:
            p.update(_prep_late(lands))
        nxt = _gather_begin(wts, l + 1, me_idx, BIG_NAMES, "all") if l + 1 < DEPTH else None
        dep = ([nxt[4]] if nxt else []) + ([rest[4]] if l == 0 else [])
        xs, sv = _layer_fwd(xs, p, dep or None, late if l == 0 else None)
        if nxt:
            lands = _gather_finish(nxt, xs, BIG_NAMES, "all")
        layers.append(p)
        saved.append(sv)
    dy, loss_part = _loss_head(xs, loss_target[0], "loss_head")
    loss = lax.psum(loss_part[0, 0], ("x", "y", "c"))

    big_red = [None] * DEPTH
    small_grads = [None] * DEPTH
    pending = None
    ffn = []
    for l in reversed(range(DEPTH)):
        early = None
        if l == 0:
            def early(ts_ffn):
                ffn.append(_rs_begin(ts_ffn, c_idx, me_idx, "ffn"))
                return ffn[0][4]
        dy, ts, small_grads[l] = _layer_bwd(dy, layers[l], saved[l], pending[4] if pending else None, early)
        if pending:
            big_red[l + 1] = _rs_finish(pending, dy, c_idx, "big")
        pending = _rs_begin(ts, c_idx, me_idx, "mix" if l == 0 else "big")
    red_ffn = _rs_finish(ffn[0], dy, c_idx, "ffn")
    grad_x = dy[None]

    rep_flat = jnp.concatenate([small_grads[l][n].reshape(-1) for l in range(DEPTH) for n, _ in REPLICATED])
    rep_flat = jnp.pad(rep_flat, (0, N_CHIPS * _REP_QUARTER - rep_flat.shape[0]))
    rows = []
    for j in range(N_CHIPS):
        pieces = [_shard_cols(small_grads[l][n], j) for l in range(DEPTH) for n, _ in SMALL_SHARDS]
        pieces.append(rep_flat[j * _REP_QUARTER:(j + 1) * _REP_QUARTER])
        rows.append(_pack_rows(pieces, SMALL_ROWS, F32))
    small_red = _reduce_scatter_chips(jnp.stack(rows), "small")
    small_all = _all_gather_chips(small_red, "gather_small")
    big_red[0] = _rs_finish(pending, small_all, c_idx, "mix") + red_ffn
    small_all = small_all.reshape(N_CHIPS, -1)

    grads = {}
    for i, (n, _) in enumerate(BIG_SHARDS):
        grads[n] = jnp.stack([big_red[l][i][:, :IN_SHARD] if n == "w_in" else big_red[l][i] for l in range(DEPTH)])
    mine_small = small_red.reshape(-1)
    parts = _unpack(mine_small, [s for _ in range(DEPTH) for _, s in SMALL_SHARDS])
    for i, (n, _) in enumerate(SMALL_SHARDS):
        grads[n] = jnp.stack([parts[l * len(SMALL_SHARDS) + i] for l in range(DEPTH)])
    off = DEPTH * _SMALL_ELEMS
    rep_all = jnp.concatenate([small_all[j, off:off + _REP_QUARTER] for j in range(N_CHIPS)])
    parts = _unpack(rep_all, [s for _ in range(DEPTH) for _, s in REPLICATED])
    for i, (n, _) in enumerate(REPLICATED):
        grads[n] = jnp.stack([parts[l * len(REPLICATED) + i] for l in range(DEPTH)])

    deltas, new_m, new_v = {}, {}, {}
    for n in WEIGHT_ORDER:
        deltas[n], new_m[n], new_v[n] = _adamw(wts[n], grads[n], moms[n], vels[n], "adamw_" + n)
    return (loss, grad_x, *[grads[n] for n in WEIGHT_ORDER], *[deltas[n] for n in WEIGHT_ORDER],
            *[new_m[n] for n in WEIGHT_ORDER], *[new_v[n] for n in WEIGHT_ORDER])
```

```python
import functools
import math

import jax
import jax.numpy as jnp
from jax import lax
from jax.experimental import pallas as pl
from jax.experimental.pallas import tpu as pltpu

F32 = jnp.float32
BF16 = jnp.bfloat16
MXU_DTYPE = jnp.bfloat16

D_MODEL = 1024
HEAD_DIM = 64
N_HEADS = 8
D_ATT = 512
D_CONV = 256
D_SGU = 256
N_GROUPS = 4
CHUNK = 128
D_FF = 2816
DEPTH = 4
RMS_EPS = 1e-6
LN_EPS = 1e-5
N_CHIPS = 4
LANES = 128
PACK_COLS = 1024
HALO = 16

ADAM_LR = 0.001
ADAM_B1 = 0.9
ADAM_B2 = 0.999
ADAM_EPS = 1e-08
ADAM_WD = 0.01
ADAM_STEP = 10

OFF_GL = 0
OFF_Q = 3 * D_MODEL
OFF_K = OFF_Q + D_ATT
OFF_V = OFF_K + D_ATT
OFF_BG = OFF_V + D_ATT
OFF_CG = OFF_BG + D_CONV
OFF_HC = OFF_CG + D_CONV
OFF_U = OFF_HC + D_CONV
OFF_VS = OFF_U + D_SGU
W_P = OFF_VS + D_SGU
F_ROWS = 16

VMEM_LIMIT = 56 * 1024 * 1024
MESH = pl.DeviceIdType.MESH


def _params(sem=None):
    if sem is None:
        return pltpu.CompilerParams(vmem_limit_bytes=VMEM_LIMIT)
    return pltpu.CompilerParams(dimension_semantics=sem, vmem_limit_bytes=VMEM_LIMIT)


def _tile(dim, pref):
    if dim <= pref:
        return dim
    if dim % pref == 0:
        return pref
    return dim


_DIMS = {"nn": (((1,), (0,)), ((), ())), "nt": (((1,), (1,)), ((), ())), "tn": (((0,), (0,)), ((), ()))}


def _mm(a, b, mode, out_dtype, name, tm, tn, tk, chip_of=None):
    if mode == "tn":
        K, M = a.shape
    else:
        M, K = a.shape
    N = b.shape[0] if mode == "nt" else b.shape[1]
    tm, tn, tk = _tile(M, tm), _tile(N // N_CHIPS if chip_of else N, tn), _tile(K, tk)
    nk = K // tk
    dims = _DIMS[mode]

    def body(a_ref, b_ref, o_ref, *acc):
        part = lax.dot_general(a_ref[...].astype(MXU_DTYPE), b_ref[...].astype(MXU_DTYPE), dims,
                               preferred_element_type=F32)
        if nk == 1:
            o_ref[...] = part.astype(o_ref.dtype)
        else:
            acc_ref = acc[0]
            k = pl.program_id(2)

            @pl.when(k == 0)
            def _():
                acc_ref[...] = part

            @pl.when(k > 0)
            def _():
                acc_ref[...] += part

            @pl.when(k == nk - 1)
            def _():
                o_ref[...] = acc_ref[...].astype(o_ref.dtype)

    if mode == "tn":
        a_spec = pl.BlockSpec((tk, tm), lambda i, j, k: (k, i))
    else:
        a_spec = pl.BlockSpec((tm, tk), lambda i, j, k: (i, k))
    if mode == "nt":
        b_spec = pl.BlockSpec((tn, tk), lambda i, j, k: (j, k))
    else:
        b_spec = pl.BlockSpec((tk, tn), lambda i, j, k: (k, j))
    if chip_of is None:
        out_spec = pl.BlockSpec((tm, tn), lambda i, j, k: (i, j))
        out_shape = jax.ShapeDtypeStruct((M, N), out_dtype)
    else:
        per = (N // N_CHIPS) // tn
        out_spec = pl.BlockSpec((None, tm, tn), lambda i, j, k: (chip_of(j // per), i, j % per))
        out_shape = jax.ShapeDtypeStruct((N_CHIPS, M, N // N_CHIPS), out_dtype)
    return pl.pallas_call(
        body,
        name=name,
        grid=(M // tm, N // tn, nk),
        in_specs=[a_spec, b_spec],
        out_specs=out_spec,
        out_shape=out_shape,
        scratch_shapes=[pltpu.VMEM((tm, tn), F32)] if nk > 1 else [],
        compiler_params=_params(("parallel", "parallel", "arbitrary")),
    )(a, b)


_GELU_K = math.sqrt(2.0 / math.pi)
_GELU_C = 0.044715


def _gelu(x):
    t = jnp.tanh(_GELU_K * (x + _GELU_C * (x * x * x)))
    return x * (0.5 * (1.0 + t))


def _gelu_and_grad(x):
    x2 = x * x
    t = jnp.tanh(_GELU_K * (x + _GELU_C * (x2 * x)))
    cdf = 0.5 * (1.0 + t)
    dcdf = 0.5 * (1.0 - t * t) * (_GELU_K * (1.0 + 3.0 * _GELU_C * x2))
    return x * cdf, cdf + x * dcdf


def _sigmoid(x):
    return 1.0 / (1.0 + jnp.exp(-x))


def _shift_down(cur, prev, k):
    h = prev.shape[0]
    ext = jnp.concatenate([prev, cur], axis=0)
    return pltpu.roll(ext, k, 0)[h:]


def _shift_up(cur, nxt, k):
    t, h = cur.shape[0], nxt.shape[0]
    ext = jnp.concatenate([cur, nxt], axis=0)
    return pltpu.roll(ext, t + h - k, 0)[:t]


def _row_sum8(x):
    t, c = x.shape
    return jnp.sum(x.reshape(t // 8, 8, c), axis=0)


_DEP = pl.BlockSpec((8, LANES), lambda i: (0, 0))


def _rms_fwd(x, g, name, dep=None):
    s, d = x.shape
    t = _tile(s, 512)

    def body(x_ref, g_ref, *rest):
        o_ref = rest[-1]
        xv = x_ref[...]
        r = lax.rsqrt(jnp.mean(xv * xv, axis=-1, keepdims=True) + RMS_EPS)
        o_ref[...] = (xv * r * g_ref[...]).astype(o_ref.dtype)

    deps = [] if dep is None else list(dep) if isinstance(dep, (list, tuple)) else [dep]
    return pl.pallas_call(
        body, name=name, grid=(s // t,),
        in_specs=[pl.BlockSpec((t, d), lambda i: (i, 0)), pl.BlockSpec((1, d), lambda i: (0, 0))] + [_DEP] * len(deps),
        out_specs=pl.BlockSpec((t, d), lambda i: (i, 0)),
        out_shape=jax.ShapeDtypeStruct((s, d), BF16),
        compiler_params=_params(("parallel",)),
    )(x, g, *deps)


def _resid_post(x, y, g, name):
    s, d = x.shape
    t = _tile(s, 512)

    def body(x_ref, y_ref, g_ref, o_ref):
        yv = y_ref[...]
        r = lax.rsqrt(jnp.mean(yv * yv, axis=-1, keepdims=True) + RMS_EPS)
        o_ref[...] = x_ref[...] + yv * r * g_ref[...]

    row = pl.BlockSpec((t, d), lambda i: (i, 0))
    return pl.pallas_call(
        body, name=name, grid=(s // t,),
        in_specs=[row, row, pl.BlockSpec((1, d), lambda i: (0, 0))],
        out_specs=row,
        out_shape=jax.ShapeDtypeStruct((s, d), F32),
        compiler_params=_params(("parallel",)),
    )(x, y, g)


def _rms_bwd(xin, g, dys, dres, out_dtype, name, dep=None):
    s, d = xin.shape
    t = _tile(s, 512)
    n = s // t
    n_dy = len(dys)
    has_res = dres is not None
    deps = [] if dep is None else [dep]

    def body(*refs):
        x_ref, g_ref = refs[0], refs[1]
        dy_refs = refs[2:2 + n_dy]
        pos = 2 + n_dy
        res_ref = refs[pos] if has_res else None
        pos += (1 if has_res else 0) + len(deps)
        dx_ref, dg_ref, acc_ref = refs[pos], refs[pos + 1], refs[pos + 2]
        i = pl.program_id(0)
        xv = x_ref[...]
        dy = dy_refs[0][...].astype(F32)
        for extra in dy_refs[1:]:
            dy = dy + extra[...].astype(F32)
        r = lax.rsqrt(jnp.mean(xv * xv, axis=-1, keepdims=True) + RMS_EPS)
        u = dy * g_ref[...]
        xr = xv * r
        dx = r * (u - xr * jnp.mean(u * xr, axis=-1, keepdims=True))
        if has_res:
            dx = dx + res_ref[...]
        dx_ref[...] = dx.astype(dx_ref.dtype)
        part = _row_sum8(dy * xr)

        @pl.when(i == 0)
        def _():
            acc_ref[...] = part

        @pl.when(i > 0)
        def _():
            acc_ref[...] += part

        @pl.when(i == n - 1)
        def _():
            dg_ref[...] = jnp.sum(acc_ref[...], axis=0, keepdims=True)

    row = pl.BlockSpec((t, d), lambda i: (i, 0))
    vec = pl.BlockSpec((1, d), lambda i: (0, 0))
    ins = [xin, g, *dys] + ([dres] if has_res else []) + deps
    return pl.pallas_call(
        body, name=name, grid=(n,),
        in_specs=[row, vec] + [row] * (n_dy + (1 if has_res else 0)) + [_DEP] * len(deps),
        out_specs=[row, vec],
        out_shape=[jax.ShapeDtypeStruct((s, d), out_dtype), jax.ShapeDtypeStruct((1, d), F32)],
        scratch_shapes=[pltpu.VMEM((8, d), F32)],
        compiler_params=_params(("arbitrary",)),
    )(*ins)


def _loss_head(y, target, name):
    s, d = y.shape
    t = _tile(s, 512)
    n = s // t

    def body(y_ref, t_ref, dy_ref, loss_ref, acc_ref):
        i = pl.program_id(0)
        e = y_ref[...] - t_ref[...]
        dy_ref[...] = e * (1.0 / d)
        part = _row_sum8(e * e)

        @pl.when(i == 0)
        def _():
            acc_ref[...] = part

        @pl.when(i > 0)
        def _():
            acc_ref[...] += part

        @pl.when(i == n - 1)
        def _():
            tot = jnp.sum(jnp.sum(acc_ref[...], axis=0, keepdims=True), axis=1, keepdims=True)
            loss_ref[...] = tot * (0.5 / d)

    row = pl.BlockSpec((t, d), lambda i: (i, 0))
    return pl.pallas_call(
        body, name=name, grid=(n,),
        in_specs=[row, row],
        out_specs=[row, pl.BlockSpec((1, 1), lambda i: (0, 0))],
        out_shape=[jax.ShapeDtypeStruct((s, d), F32), jax.ShapeDtypeStruct((1, 1), F32)],
        scratch_shapes=[pltpu.VMEM((8, d), F32)],
        compiler_params=_params(("arbitrary",)),
    )(y, target)


def _split3(x):
    hi = x.astype(BF16)
    r1 = x - hi.astype(F32)
    mid = r1.astype(BF16)
    lo = (r1 - mid.astype(F32)).astype(BF16)
    return hi, mid, lo


def _tri_dot(x, tri):
    hi, mid, lo = _split3(x)
    dn = _DIMS["nn"]
    out = lax.dot_general(hi, tri, dn, preferred_element_type=F32)
    out = out + lax.dot_general(mid, tri, dn, preferred_element_type=F32)
    return out + lax.dot_general(lo, tri, dn, preferred_element_type=F32)


def _log_sigmoid(z):
    return jnp.minimum(z, 0.0) - jnp.log(1.0 + jnp.exp(-jnp.abs(z)))


def _gate_fwd(f_row, b_col, name):
    rows, s = f_row.shape
    t = _tile(s, 512)
    n = s // t

    def body(f_ref, b_ref, ck_ref, carry_ref):
        i = pl.program_id(0)

        @pl.when(i == 0)
        def _():
            carry_ref[...] = jnp.zeros_like(carry_ref)

        logf = _log_sigmoid(f_ref[...] + b_ref[...])
        r = lax.broadcasted_iota(jnp.int32, (t, t), 0)
        c = lax.broadcasted_iota(jnp.int32, (t, t), 1)
        tri = jnp.where(r <= c, 1.0, 0.0).astype(BF16)
        cs = _tri_dot(logf, tri) + carry_ref[...]
        carry_ref[...] = cs[:, t - 1:t]
        terms = [part.astype(F32) for part in _split3(-cs)]
        sub = lax.broadcasted_iota(jnp.int32, (LANES, t), 0)
        for p in range(N_HEADS // 2):
            stacked = jnp.zeros((LANES, t), F32)
            for hh in range(2):
                for j, term in enumerate(terms):
                    h = 2 * p + hh
                    stacked = jnp.where(sub == 3 * hh + j, jnp.broadcast_to(term[h:h + 1, :], (LANES, t)), stacked)
            ck_ref[p] = stacked.T.astype(ck_ref.dtype)

    return pl.pallas_call(
        body, name=name, grid=(n,),
        in_specs=[pl.BlockSpec((rows, t), lambda i: (0, i)), pl.BlockSpec((rows, 1), lambda i: (0, 0))],
        out_specs=pl.BlockSpec((N_HEADS // 2, t, LANES), lambda i: (0, i, 0)),
        out_shape=jax.ShapeDtypeStruct((N_HEADS // 2, s, LANES), BF16),
        scratch_shapes=[pltpu.VMEM((rows, 1), F32)],
        compiler_params=_params(("arbitrary",)),
    )(f_row, b_col)


def _gate_bwd(f_row, b_col, dc_even, dc_odd, name):
    rows, s = f_row.shape
    t = _tile(s, 512)
    n = s // t

    def body(f_ref, b_ref, dce_ref, dco_ref, df_ref, db_ref, carry_ref, acc_ref):
        i = pl.program_id(0)

        @pl.when(i == 0)
        def _():
            carry_ref[...] = jnp.zeros_like(carry_ref)
            acc_ref[...] = jnp.zeros_like(acc_ref)

        head = lax.broadcasted_iota(jnp.int32, (rows, t), 0)
        dcv = jnp.zeros((rows, t), F32)
        for h in range(N_HEADS):
            src = dce_ref if h % 2 == 0 else dco_ref
            dcv = jnp.where(head == h, jnp.broadcast_to(src[h // 2, 0:1, :], (rows, t)), dcv)
        r = lax.broadcasted_iota(jnp.int32, (t, t), 0)
        c = lax.broadcasted_iota(jnp.int32, (t, t), 1)
        tri = jnp.where(r >= c, 1.0, 0.0).astype(BF16)
        dlogf = _tri_dot(dcv, tri) + carry_ref[...]
        carry_ref[...] = dlogf[:, 0:1]
        z = f_ref[...] + b_ref[...]
        df = dlogf * _sigmoid(-z)
        df_ref[...] = df.astype(df_ref.dtype)
        acc_ref[...] += jnp.sum(df, axis=1, keepdims=True)

        @pl.when(i == n - 1)
        def _():
            db_ref[...] = acc_ref[...]

    rev = lambda i: (0, n - 1 - i)
    dc_spec = pl.BlockSpec((N_HEADS // 2, 8, t), lambda i: (0, 0, n - 1 - i))
    return pl.pallas_call(
        body, name=name, grid=(n,),
        in_specs=[pl.BlockSpec((rows, t), rev), pl.BlockSpec((rows, 1), lambda i: (0, 0)), dc_spec, dc_spec],
        out_specs=[pl.BlockSpec((rows, t), rev), pl.BlockSpec((rows, 1), lambda i: (0, 0))],
        out_shape=[jax.ShapeDtypeStruct((rows, s), BF16), jax.ShapeDtypeStruct((rows, 1), F32)],
        scratch_shapes=[pltpu.VMEM((rows, 1), F32), pltpu.VMEM((rows, 1), F32)],
        compiler_params=_params(("arbitrary",)),
    )(f_row, b_col, dc_even, dc_odd)


_NEG = -1e30
_SCALE = HEAD_DIM ** -0.5


def _head_masks():
    lane = lax.broadcasted_iota(jnp.int32, (1, LANES), 1)
    return [lane < HEAD_DIM, lane >= HEAD_DIM]


def _attn_fwd(h, ck, name):
    s = h.shape[0]
    t = _tile(s, 512)
    n = s // t
    qb, kb, vb = OFF_Q // LANES, OFF_K // LANES, OFF_V // LANES

    pairs = [(qi, ki) for qi in range(n) for ki in range(qi + 1)]
    qi_tab = jnp.asarray([qi for qi, _ in pairs], jnp.int32)
    ki_tab = jnp.asarray([ki for _, ki in pairs], jnp.int32)

    def body(qi_ref, ki_ref, q_ref, k_ref, v_ref, ck_ref, o_ref, of_ref, lse_ref, m_ref, l_ref, acc_ref):
        qi, ki = qi_ref[pl.program_id(1)], ki_ref[pl.program_id(1)]
        masks = _head_masks()
        lane = lax.broadcasted_iota(jnp.int32, (1, LANES), 1)

        @pl.when(ki == 0)
        def _():
            m_ref[...] = jnp.full_like(m_ref, _NEG)
            l_ref[...] = jnp.zeros_like(l_ref)
            acc_ref[...] = jnp.zeros_like(acc_ref)

        def step(diag):
            q = q_ref[...] * _SCALE
            k_aug = jnp.concatenate([k_ref[...], ck_ref[0]], axis=1)
            v = v_ref[...]
            nq = max(1, t // 256)
            wq = t // nq
            chains = [(hh, j) for hh in range(2) for j in range(nq)]
            scores = []
            for hh, j in chains:
                qs = q[j * wq:(j + 1) * wq]
                ones = jnp.where((lane >= 3 * hh) & (lane < 3 * hh + 3), 1.0, 0.0).astype(q.dtype)
                q_aug = jnp.concatenate([jnp.where(masks[hh], qs, jnp.zeros_like(qs)),
                                         jnp.broadcast_to(ones, qs.shape)], axis=1)
                scores.append(lax.dot_general(k_aug, q_aug, _DIMS["nt"], preferred_element_type=F32))
            probs = []
            for (hh, j), sc in zip(chains, scores):
                cols = slice(j * wq, (j + 1) * wq)
                if diag:
                    r = lax.broadcasted_iota(jnp.int32, (t, wq), 0)
                    cc = lax.broadcasted_iota(jnp.int32, (t, wq), 1) + j * wq
                    sc = jnp.where(r <= cc, sc, _NEG)
                m_prev = m_ref[hh, :, cols]
                m_new = jnp.maximum(m_prev, jnp.max(sc, axis=0, keepdims=True))
                alpha = jnp.exp(m_prev - m_new)
                p = jnp.exp(sc - m_new)
                l_ref[hh, :, cols] = alpha * l_ref[hh, :, cols] + jnp.sum(p, axis=0, keepdims=True)
                m_ref[hh, :, cols] = m_new
                p_hi = p.astype(MXU_DTYPE)
                p_lo = (p - p_hi.astype(F32)).astype(MXU_DTYPE)
                probs.append((alpha, p_hi, p_lo))
            for (hh, j), (alpha, p_hi, p_lo) in zip(chains, probs):
                pv = (lax.dot_general(v, p_hi, _DIMS["tn"], preferred_element_type=F32)
                      + lax.dot_general(v, p_lo, _DIMS["tn"], preferred_element_type=F32))
                rows = slice(hh * HEAD_DIM, (hh + 1) * HEAD_DIM)
                cols = slice(j * wq, (j + 1) * wq)
                acc_ref[rows, cols] = alpha * acc_ref[rows, cols] + pv[rows]

        @pl.when(ki < qi)
        def _():
            step(False)

        @pl.when(ki == qi)
        def _():
            step(True)
            inv = jnp.concatenate([jnp.broadcast_to(1.0 / l_ref[hh], (HEAD_DIM, t)) for hh in range(2)], axis=0)
            out = (acc_ref[...] * inv).T
            o_ref[...] = out.astype(o_ref.dtype)
            of_ref[...] = out
            lse = jnp.concatenate([jnp.broadcast_to(m_ref[hh] + jnp.log(l_ref[hh]), (HEAD_DIM, t))
                                   for hh in range(2)], axis=0)
            lse_ref[...] = lse.T

    grid_spec = pltpu.PrefetchScalarGridSpec(
        num_scalar_prefetch=2, grid=(N_HEADS // 2, len(pairs)),
        in_specs=[
            pl.BlockSpec((t, LANES), lambda p, i, qt, kt: (qt[i], qb + p)),
            pl.BlockSpec((t, LANES), lambda p, i, qt, kt: (kt[i], kb + p)),
            pl.BlockSpec((t, LANES), lambda p, i, qt, kt: (kt[i], vb + p)),
            pl.BlockSpec((1, t, LANES), lambda p, i, qt, kt: (p, kt[i], 0)),
        ],
        out_specs=[pl.BlockSpec((t, LANES), lambda p, i, qt, kt: (qt[i], p))] * 3,
        scratch_shapes=[pltpu.VMEM((2, 1, t), F32), pltpu.VMEM((2, 1, t), F32), pltpu.VMEM((LANES, t), F32)])
    return pl.pallas_call(
        body, name=name, grid_spec=grid_spec,
        out_shape=[jax.ShapeDtypeStruct((s, D_ATT), BF16), jax.ShapeDtypeStruct((s, D_ATT), F32),
                   jax.ShapeDtypeStruct((s, D_ATT), F32)],
        compiler_params=_params(("parallel", "arbitrary")),
    )(qi_tab, ki_tab, h, h, h, ck)


def _attn_bwd(h, ck, o, lse, do, name):
    s = h.shape[0]
    t = _tile(s, 512)
    n = s // t
    qb, kb, vb = OFF_Q // LANES, OFF_K // LANES, OFF_V // LANES

    pairs = [(ki, qi) for ki in range(n) for qi in range(ki, n)]
    ki_tab = jnp.asarray([ki for ki, _ in pairs], jnp.int32)
    qi_tab = jnp.asarray([qi for _, qi in pairs], jnp.int32)

    def body(ki_ref, qi_ref, q_ref, k_ref, v_ref, ck_ref, o_ref, lse_ref, do_ref,
             dq_ref, dk_ref, dv_ref, dc0_ref, dc1_ref, dk_acc, dv_acc, dc_acc):
        ki, qi = ki_ref[pl.program_id(1)], qi_ref[pl.program_id(1)]
        masks = _head_masks()
        lane = lax.broadcasted_iota(jnp.int32, (1, LANES), 1)

        @pl.when((ki == 0) & (qi == 0))
        def _():
            dq_ref[...] = jnp.zeros_like(dq_ref)

        @pl.when(qi == ki)
        def _():
            dk_acc[...] = jnp.zeros_like(dk_acc)
            dv_acc[...] = jnp.zeros_like(dv_acc)
            dc_acc[...] = jnp.zeros_like(dc_acc)

        def step(diag):
            q = q_ref[...] * _SCALE
            k = k_ref[...]
            v = v_ref[...]
            dov = do_ref[...]
            k_aug = jnp.concatenate([k, ck_ref[0]], axis=1)
            prod_t = (dov.astype(F32) * o_ref[...]).T
            lse_t = lse_ref[...].T
            heads = []
            for hh in range(2):
                mk = masks[hh]
                qh = jnp.where(mk, q, jnp.zeros_like(q))
                kh = jnp.where(mk, k, jnp.zeros_like(k))
                doh = jnp.where(mk, dov, jnp.zeros_like(dov))
                ones = jnp.where((lane >= 3 * hh) & (lane < 3 * hh + 3), 1.0, 0.0).astype(q.dtype)
                q_aug = jnp.concatenate([qh, jnp.broadcast_to(ones, q.shape)], axis=1)
                sc = lax.dot_general(k_aug, q_aug, _DIMS["nt"], preferred_element_type=F32)
                dp = lax.dot_general(v, doh, _DIMS["nt"], preferred_element_type=F32)
                heads.append((qh, kh, doh, sc, dp))
            grads = []
            for hh, (qh, kh, doh, sc, dp) in enumerate(heads):
                rows = slice(hh * HEAD_DIM, (hh + 1) * HEAD_DIM)
                p = jnp.exp(sc - lse_t[hh * HEAD_DIM:hh * HEAD_DIM + 1, :])
                if diag:
                    r = lax.broadcasted_iota(jnp.int32, (t, t), 0)
                    cc = lax.broadcasted_iota(jnp.int32, (t, t), 1)
                    p = jnp.where(r <= cc, p, 0.0)
                delta = jnp.sum(prod_t[rows], axis=0, keepdims=True)
                ds = p * (dp - delta)
                dc_acc[hh] = dc_acc[hh] - jnp.sum(ds, axis=1, keepdims=True)
                grads.append((ds.astype(MXU_DTYPE), p.astype(MXU_DTYPE)))
            dq_blk = jnp.zeros((t, LANES), F32)
            for (qh, kh, doh, _, _), (dsb, pb) in zip(heads, grads):
                dv_acc[...] += lax.dot_general(pb, doh, _DIMS["nn"], preferred_element_type=F32)
                dk_acc[...] += lax.dot_general(dsb, qh, _DIMS["nn"], preferred_element_type=F32)
                dq_blk = dq_blk + lax.dot_general(dsb, kh, _DIMS["tn"], preferred_element_type=F32)
            rows_q = pl.ds(pl.multiple_of(qi * t, t), t)
            dq_ref[rows_q, :] = dq_ref[rows_q, :] + dq_blk * _SCALE

        @pl.when(qi > ki)
        def _():
            step(False)

        @pl.when(qi == ki)
        def _():
            step(True)

        @pl.when(qi == n - 1)
        def _():
            dk_ref[...] = dk_acc[...].astype(dk_ref.dtype)
            dv_ref[...] = dv_acc[...].astype(dv_ref.dtype)
            dc0_ref[0] = jnp.broadcast_to(dc_acc[0], (t, LANES)).T[0:8]
            dc1_ref[0] = jnp.broadcast_to(dc_acc[1], (t, LANES)).T[0:8]

    q_blk = lambda col: pl.BlockSpec((t, LANES), lambda p, i, kt, qt: (qt[i], col(p)))
    k_blk = lambda col: pl.BlockSpec((t, LANES), lambda p, i, kt, qt: (kt[i], col(p)))
    dc_blk = pl.BlockSpec((1, 8, t), lambda p, i, kt, qt: (p, 0, kt[i]))
    grid_spec = pltpu.PrefetchScalarGridSpec(
        num_scalar_prefetch=2, grid=(N_HEADS // 2, len(pairs)),
        in_specs=[q_blk(lambda p: qb + p), k_blk(lambda p: kb + p), k_blk(lambda p: vb + p),
                  pl.BlockSpec((1, t, LANES), lambda p, i, kt, qt: (p, kt[i], 0)),
                  q_blk(lambda p: p), q_blk(lambda p: p), q_blk(lambda p: p)],
        out_specs=[pl.BlockSpec((s, LANES), lambda p, i, kt, qt: (0, p)), k_blk(lambda p: p), k_blk(lambda p: p),
                   dc_blk, dc_blk],
        scratch_shapes=[pltpu.VMEM((t, LANES), F32), pltpu.VMEM((t, LANES), F32), pltpu.VMEM((2, t, 1), F32)])
    return pl.pallas_call(
        body, name=name, grid_spec=grid_spec,
        out_shape=[jax.ShapeDtypeStruct((s, D_ATT), F32), jax.ShapeDtypeStruct((s, D_ATT), BF16),
                   jax.ShapeDtypeStruct((s, D_ATT), BF16), jax.ShapeDtypeStruct((N_HEADS // 2, 8, s), F32),
                   jax.ShapeDtypeStruct((N_HEADS // 2, 8, s), F32)],
        compiler_params=_params(("parallel", "arbitrary")),
    )(ki_tab, qi_tab, h, h, h, ck, o, lse, do)


def _conv3(z, z_prev, w_ref):
    return (w_ref[2:3, :] * z + w_ref[1:2, :] * _shift_down(z, z_prev, 1)
            + w_ref[0:1, :] * _shift_down(z, z_prev, 2))


def _sconv_fwd(h, w, name):
    s = h.shape[0]
    t = _tile(s, 512)
    r = t // HALO
    c = D_CONV
    b_bg, b_cg, b_hc = OFF_BG // c, OFF_CG // c, OFF_HC // c

    def body(bg_ref, cg_ref, hc_ref, cgp_ref, hcp_ref, w_ref, y_ref):
        i = pl.program_id(0)
        live = (i > 0).astype(F32)
        z = cg_ref[...].astype(F32) * hc_ref[...].astype(F32)
        zp = cgp_ref[...].astype(F32) * hcp_ref[...].astype(F32) * live
        y_ref[...] = (bg_ref[...].astype(F32) * _conv3(z, zp, w_ref)).astype(y_ref.dtype)

    cur = lambda b: pl.BlockSpec((t, c), lambda i: (i, b))
    prev = lambda b: pl.BlockSpec((HALO, c), lambda i: (jnp.maximum(i * r - 1, 0), b))
    return pl.pallas_call(
        body, name=name, grid=(s // t,),
        in_specs=[cur(b_bg), cur(b_cg), cur(b_hc), prev(b_cg), prev(b_hc), pl.BlockSpec((8, c), lambda i: (0, 0))],
        out_specs=pl.BlockSpec((t, c), lambda i: (i, 0)),
        out_shape=jax.ShapeDtypeStruct((s, c), BF16),
        compiler_params=_params(("parallel",)),
    )(h, h, h, h, h, w)


def _sconv_bwd(h, w, dy, name):
    s = h.shape[0]
    t = _tile(s, 512)
    n = s // t
    r = t // HALO
    nh = s // HALO
    c = D_CONV
    b_bg, b_cg, b_hc = OFF_BG // c, OFF_CG // c, OFF_HC // c

    def body(bg_ref, cg_ref, hc_ref, cgp_ref, hcp_ref, bgn_ref, dy_ref, dyn_ref, w_ref, d_ref, dw_ref, acc_ref):
        i = pl.program_id(0)
        has_prev = (i > 0).astype(F32)
        has_next = (i < n - 1).astype(F32)
        bg = bg_ref[...].astype(F32)
        cg = cg_ref[...].astype(F32)
        hc = hc_ref[...].astype(F32)
        dyv = dy_ref[...].astype(F32)
        z = cg * hc
        zp = cgp_ref[...].astype(F32) * hcp_ref[...].astype(F32) * has_prev
        z1 = _shift_down(z, zp, 1)
        z2 = _shift_down(z, zp, 2)
        cz = w_ref[2:3, :] * z + w_ref[1:2, :] * z1 + w_ref[0:1, :] * z2
        dcz = dyv * bg
        dczn = dyn_ref[...].astype(F32) * bgn_ref[...].astype(F32) * has_next
        dz = (w_ref[2:3, :] * dcz + w_ref[1:2, :] * _shift_up(dcz, dczn, 1)
              + w_ref[0:1, :] * _shift_up(dcz, dczn, 2))
        d_ref[:, 0:c] = (dyv * cz).astype(d_ref.dtype)
        d_ref[:, c:2 * c] = (dz * hc).astype(d_ref.dtype)
        d_ref[:, 2 * c:3 * c] = (dz * cg).astype(d_ref.dtype)

        @pl.when(i == 0)
        def _():
            acc_ref[...] = jnp.zeros_like(acc_ref)

        acc_ref[0] += _row_sum8(dcz * z2)
        acc_ref[1] += _row_sum8(dcz * z1)
        acc_ref[2] += _row_sum8(dcz * z)

        @pl.when(i == n - 1)
        def _():
            rows = [jnp.sum(acc_ref[k], axis=0, keepdims=True) for k in range(3)]
            dw_ref[...] = jnp.concatenate(rows + [jnp.zeros((5, c), F32)], axis=0)

    cur = lambda b: pl.BlockSpec((t, c), lambda i: (i, b))
    prev = lambda b: pl.BlockSpec((HALO, c), lambda i: (jnp.maximum(i * r - 1, 0), b))
    nxt = lambda b: pl.BlockSpec((HALO, c), lambda i: (jnp.minimum((i + 1) * r, nh - 1), b))
    return pl.pallas_call(
        body, name=name, grid=(n,),
        in_specs=[cur(b_bg), cur(b_cg), cur(b_hc), prev(b_cg), prev(b_hc), nxt(b_bg),
                  cur(0), nxt(0), pl.BlockSpec((8, c), lambda i: (0, 0))],
        out_specs=[pl.BlockSpec((t, 3 * c), lambda i: (i, 0)), pl.BlockSpec((8, c), lambda i: (0, 0))],
        out_shape=[jax.ShapeDtypeStruct((s, 3 * c), BF16), jax.ShapeDtypeStruct((8, c), F32)],
        scratch_shapes=[pltpu.VMEM((3, 8, c), F32)],
        compiler_params=_params(("arbitrary",)),
    )(h, h, h, h, h, h, dy, dy, w)


def _group_masks():
    lane = lax.broadcasted_iota(jnp.int32, (1, D_SGU), 1)
    return [(lane >= g * HEAD_DIM) & (lane < (g + 1) * HEAD_DIM) for g in range(N_GROUPS)]


def _tril_weights(w_ref):
    r = lax.broadcasted_iota(jnp.int32, (CHUNK, CHUNK), 0)
    c = lax.broadcasted_iota(jnp.int32, (CHUNK, CHUNK), 1)
    return [jnp.where(r >= c, w_ref[g], 0.0).astype(MXU_DTYPE) for g in range(N_GROUPS)]


def _sgu_ln(vs, g_ref, b_ref):
    vg, dvg = _gelu_and_grad(vs)
    mu = jnp.mean(vg, axis=-1, keepdims=True)
    xc = vg - mu
    rstd = lax.rsqrt(jnp.mean(xc * xc, axis=-1, keepdims=True) + LN_EPS)
    xhat = xc * rstd
    return xhat * g_ref[...] + b_ref[...], xhat, rstd, dvg


def _sgu_fwd(h, ln_g, ln_b, w_s, bias, name):
    s = h.shape[0]
    t = _tile(s, 512)
    c = D_SGU
    b_u, b_v = OFF_U // c, OFF_VS // c

    def body(u_ref, v_ref, g_ref, b_ref, w_ref, bias_ref, y_ref):
        gm = _group_masks()
        wm = _tril_weights(w_ref)
        ug = _gelu(u_ref[...].astype(F32))
        vn, _, _, _ = _sgu_ln(v_ref[...].astype(F32), g_ref, b_ref)
        vnb = vn.astype(MXU_DTYPE)
        for ch in range(t // CHUNK):
            rows = slice(ch * CHUNK, (ch + 1) * CHUNK)
            mixed = bias_ref[...]
            for g in range(N_GROUPS):
                mg = lax.dot_general(wm[g], vnb[rows], _DIMS["nn"], preferred_element_type=F32)
                mixed = jnp.where(gm[g], mixed + mg, mixed)
            y_ref[rows, :] = (ug[rows] * mixed).astype(y_ref.dtype)

    full = lambda shp: pl.BlockSpec(shp, lambda i: (0,) * len(shp))
    return pl.pallas_call(
        body, name=name, grid=(s // t,),
        in_specs=[pl.BlockSpec((t, c), lambda i: (i, b_u)), pl.BlockSpec((t, c), lambda i: (i, b_v)),
                  full((1, c)), full((1, c)), full((N_GROUPS, CHUNK, CHUNK)), full((CHUNK, c))],
        out_specs=pl.BlockSpec((t, c), lambda i: (i, 0)),
        out_shape=jax.ShapeDtypeStruct((s, c), BF16),
        compiler_params=_params(("parallel",)),
    )(h, h, ln_g, ln_b, w_s, bias)


def _sgu_bwd(h, ln_g, ln_b, w_s, bias, dy, name):
    s = h.shape[0]
    t = _tile(s, 512)
    n = s // t
    c = D_SGU
    b_u, b_v = OFF_U // c, OFF_VS // c

    def body(u_ref, v_ref, g_ref, b_ref, w_ref, bias_ref, dy_ref,
             d_ref, dg_ref, db_ref, dw_ref, dbias_ref, dg_acc, db_acc):
        i = pl.program_id(0)
        gm = _group_masks()
        wm = _tril_weights(w_ref)

        @pl.when(i == 0)
        def _():
            dg_acc[...] = jnp.zeros_like(dg_acc)
            db_acc[...] = jnp.zeros_like(db_acc)
            dw_ref[...] = jnp.zeros_like(dw_ref)
            dbias_ref[...] = jnp.zeros_like(dbias_ref)

        ug, dug = _gelu_and_grad(u_ref[...].astype(F32))
        vn, xhat, rstd, dvg = _sgu_ln(v_ref[...].astype(F32), g_ref, b_ref)
        vnb = vn.astype(MXU_DTYPE)
        dyv = dy_ref[...].astype(F32)
        dmixed = dyv * ug
        dmb = dmixed.astype(MXU_DTYPE)
        dvn_parts = []
        for ch in range(t // CHUNK):
            rows = slice(ch * CHUNK, (ch + 1) * CHUNK)
            mixed = bias_ref[...]
            dvn = jnp.zeros((CHUNK, c), F32)
            for g in range(N_GROUPS):
                mg = lax.dot_general(wm[g], vnb[rows], _DIMS["nn"], preferred_element_type=F32)
                mixed = jnp.where(gm[g], mixed + mg, mixed)
                dvn = jnp.where(gm[g], lax.dot_general(wm[g], dmb[rows], _DIMS["tn"], preferred_element_type=F32),
                                dvn)
                dmg = jnp.where(gm[g], dmb[rows], jnp.zeros_like(dmb[rows]))
                dw_ref[g] += lax.dot_general(dmg, vnb[rows], _DIMS["nt"], preferred_element_type=F32)
            d_ref[rows, 0:c] = (dyv[rows] * mixed * dug[rows]).astype(d_ref.dtype)
            dbias_ref[...] += dmixed[rows]
            dvn_parts.append(dvn)
        dvn = jnp.concatenate(dvn_parts, axis=0)
        dg_acc[...] += _row_sum8(dvn * xhat)
        db_acc[...] += _row_sum8(dvn)
        dxh = dvn * g_ref[...]
        dvgl = rstd * (dxh - jnp.mean(dxh, axis=-1, keepdims=True)
                       - xhat * jnp.mean(dxh * xhat, axis=-1, keepdims=True))
        d_ref[:, c:2 * c] = (dvgl * dvg).astype(d_ref.dtype)

        @pl.when(i == n - 1)
        def _():
            dg_ref[...] = jnp.sum(dg_acc[...], axis=0, keepdims=True)
            db_ref[...] = jnp.sum(db_acc[...], axis=0, keepdims=True)
            r = lax.broadcasted_iota(jnp.int32, (CHUNK, CHUNK), 0)
            cc = lax.broadcasted_iota(jnp.int32, (CHUNK, CHUNK), 1)
            for g in range(N_GROUPS):
                dw_ref[g] = jnp.where(r >= cc, dw_ref[g], 0.0)

    full = lambda shp: pl.BlockSpec(shp, lambda i: (0,) * len(shp))
    return pl.pallas_call(
        body, name=name, grid=(n,),
        in_specs=[pl.BlockSpec((t, c), lambda i: (i, b_u)), pl.BlockSpec((t, c), lambda i: (i, b_v)),
                  full((1, c)), full((1, c)), full((N_GROUPS, CHUNK, CHUNK)), full((CHUNK, c)),
                  pl.BlockSpec((t, c), lambda i: (i, 0))],
        out_specs=[pl.BlockSpec((t, 2 * c), lambda i: (i, 0)), full((1, c)), full((1, c)),
                   full((N_GROUPS, CHUNK, CHUNK)), full((CHUNK, c))],
        out_shape=[jax.ShapeDtypeStruct((s, 2 * c), BF16), jax.ShapeDtypeStruct((1, c), F32),
                   jax.ShapeDtypeStruct((1, c), F32), jax.ShapeDtypeStruct((N_GROUPS, CHUNK, CHUNK), F32),
                   jax.ShapeDtypeStruct((CHUNK, c), F32)],
        scratch_shapes=[pltpu.VMEM((8, c), F32), pltpu.VMEM((8, c), F32)],
        compiler_params=_params(("arbitrary",)),
    )(h, h, ln_g, ln_b, w_s, bias, dy)


def _merge_fwd(h, acts, ws, b_gate, name):
    s = h.shape[0]
    d = D_MODEL
    t = _tile(s, 512)

    def body(gl0, gl1, gl2, a0, a1, a2, w0, w1, w2, b_ref, o_ref):
        acc = jnp.zeros((t, d), F32)
        for i, (gl, a, w) in enumerate(((gl0, a0, w0), (gl1, a1, w1), (gl2, a2, w2))):
            y = lax.dot_general(a[...], w[...], _DIMS["nn"], preferred_element_type=F32)
            acc = acc + _sigmoid(gl[...].astype(F32) + b_ref[i:i + 1, :]) * y
        o_ref[...] = acc.astype(o_ref.dtype)

    full = lambda arr: pl.BlockSpec(arr.shape, lambda i: (0, 0))
    return pl.pallas_call(
        body, name=name, grid=(s // t,),
        in_specs=[pl.BlockSpec((t, d), lambda i, b=b: (i, b)) for b in range(3)]
                 + [pl.BlockSpec((t, a.shape[1]), lambda i: (i, 0)) for a in acts]
                 + [full(w) for w in ws] + [full(b_gate)],
        out_specs=pl.BlockSpec((t, d), lambda i: (i, 0)),
        out_shape=jax.ShapeDtypeStruct((s, d), BF16),
        compiler_params=_params(("parallel",)),
    )(h, h, h, *acts, *ws, b_gate)


def _merge_bwd(h, acts, ws, b_gate, dmerged, name):
    s = h.shape[0]
    d = D_MODEL
    t = _tile(s, 512)
    n = s // t

    def body(gl0, gl1, gl2, a0, a1, a2, w0, w1, w2, b_ref, dm_ref, dy0, dy1, dy2, dgl_ref, db_ref, acc_ref):
        step = pl.program_id(0)

        @pl.when(step == 0)
        def _():
            acc_ref[...] = jnp.zeros_like(acc_ref)

        dm = dm_ref[...]
        for i, (gl, a, w, dy) in enumerate(((gl0, a0, w0, dy0), (gl1, a1, w1, dy1), (gl2, a2, w2, dy2))):
            y = lax.dot_general(a[...], w[...], _DIMS["nn"], preferred_element_type=F32)
            gate = _sigmoid(gl[...].astype(F32) + b_ref[i:i + 1, :])
            dy[...] = (dm * gate).astype(dy.dtype)
            dgl = dm * y * (gate * (1.0 - gate))
            dgl_ref[:, i * d:(i + 1) * d] = dgl.astype(dgl_ref.dtype)
            acc_ref[i] += _row_sum8(dgl)

        @pl.when(step == n - 1)
        def _():
            rows = [jnp.sum(acc_ref[k], axis=0, keepdims=True) for k in range(3)]
            db_ref[...] = jnp.concatenate(rows + [jnp.zeros((5, d), F32)], axis=0)

    full = lambda arr: pl.BlockSpec(arr.shape, lambda i: (0, 0))
    row = pl.BlockSpec((t, d), lambda i: (i, 0))
    return pl.pallas_call(
        body, name=name, grid=(n,),
        in_specs=[pl.BlockSpec((t, d), lambda i, b=b: (i, b)) for b in range(3)]
                 + [pl.BlockSpec((t, a.shape[1]), lambda i: (i, 0)) for a in acts]
                 + [full(w) for w in ws] + [full(b_gate), row],
        out_specs=[row, row, row, pl.BlockSpec((t, 3 * d), lambda i: (i, 0)), pl.BlockSpec((8, d), lambda i: (0, 0))],
        out_shape=[jax.ShapeDtypeStruct((s, d), BF16)] * 3
                  + [jax.ShapeDtypeStruct((s, 3 * d), BF16), jax.ShapeDtypeStruct((8, d), F32)],
        scratch_shapes=[pltpu.VMEM((3, 8, d), F32)],
        compiler_params=_params(("arbitrary",)),
    )(h, h, h, *acts, *ws, b_gate, dmerged)


FF_BLK = D_FF // 2


def _ffn_act_fwd(h2, w, name):
    s = h2.shape[0]
    t = _tile(s, 512)
    r = t // HALO
    cw = 2 * FF_BLK

    def body(x_ref, xp_ref, w_ref, p_ref):
        i = pl.program_id(0)
        live = (i > 0).astype(F32)
        hc = _conv3(x_ref[...].astype(F32), xp_ref[...].astype(F32) * live, w_ref)
        p_ref[...] = (_gelu(hc[:, :FF_BLK]) * hc[:, FF_BLK:]).astype(p_ref.dtype)

    return pl.pallas_call(
        body, name=name, grid=(s // t, 2),
        in_specs=[pl.BlockSpec((t, cw), lambda i, j: (i, j)),
                  pl.BlockSpec((HALO, cw), lambda i, j: (jnp.maximum(i * r - 1, 0), j)),
                  pl.BlockSpec((8, cw), lambda i, j: (0, j))],
        out_specs=pl.BlockSpec((t, FF_BLK), lambda i, j: (i, j)),
        out_shape=jax.ShapeDtypeStruct((s, D_FF), BF16),
        compiler_params=_params(("parallel", "parallel")),
    )(h2, h2, w)


def _ffn_act_conv_bwd(h2, w, dp, name):
    s = h2.shape[0]
    t = _tile(s, 512)
    n = s // t
    r = t // HALO
    nh = s // HALO
    cw = 2 * FF_BLK

    def body(x_ref, xp_ref, xn_ref, dp_ref, dpn_ref, w_ref, dx_ref, dw_ref, acc_ref):
        i = pl.program_id(1)
        has_prev = (i > 0).astype(F32)
        has_next = (i < n - 1).astype(F32)
        x = jnp.concatenate([x_ref[...].astype(F32), xn_ref[...].astype(F32)], axis=0)
        xp = xp_ref[...].astype(F32) * has_prev
        x1 = _shift_down(x, xp, 1)
        x2 = _shift_down(x, xp, 2)
        hc = w_ref[2:3, :] * x + w_ref[1:2, :] * x1 + w_ref[0:1, :] * x2
        ga, dga = _gelu_and_grad(hc[:, :FF_BLK])
        dpv = jnp.concatenate([dp_ref[...].astype(F32), dpn_ref[...].astype(F32) * has_next], axis=0)
        dhc = jnp.concatenate([dpv * hc[:, FF_BLK:] * dga, dpv * ga], axis=1)
        cur, nxt = dhc[:t], dhc[t:]
        dx = w_ref[2:3, :] * cur + w_ref[1:2, :] * _shift_up(cur, nxt, 1) + w_ref[0:1, :] * _shift_up(cur, nxt, 2)
        dx_ref[...] = dx.astype(dx_ref.dtype)

        @pl.when(i == 0)
        def _():
            acc_ref[...] = jnp.zeros_like(acc_ref)

        acc_ref[0] += _row_sum8(cur * x2[:t])
        acc_ref[1] += _row_sum8(cur * x1[:t])
        acc_ref[2] += _row_sum8(cur * x[:t])

        @pl.when(i == n - 1)
        def _():
            rows = [jnp.sum(acc_ref[k], axis=0, keepdims=True) for k in range(3)]
            dw_ref[...] = jnp.concatenate(rows + [jnp.zeros((5, cw), F32)], axis=0)

    nxt_row = lambda j, i: jnp.minimum((i + 1) * r, nh - 1)
    return pl.pallas_call(
        body, name=name, grid=(2, n),
        in_specs=[pl.BlockSpec((t, cw), lambda j, i: (i, j)),
                  pl.BlockSpec((HALO, cw), lambda j, i: (jnp.maximum(i * r - 1, 0), j)),
                  pl.BlockSpec((HALO, cw), lambda j, i: (nxt_row(j, i), j)),
                  pl.BlockSpec((t, FF_BLK), lambda j, i: (i, j)),
                  pl.BlockSpec((HALO, FF_BLK), lambda j, i: (nxt_row(j, i), j)),
                  pl.BlockSpec((8, cw), lambda j, i: (0, j))],
        out_specs=[pl.BlockSpec((t, cw), lambda j, i: (i, j)), pl.BlockSpec((8, cw), lambda j, i: (0, j))],
        out_shape=[jax.ShapeDtypeStruct((s, 2 * D_FF), BF16), jax.ShapeDtypeStruct((8, 2 * D_FF), F32)],
        scratch_shapes=[pltpu.VMEM((3, 8, cw), F32)],
        compiler_params=_params(("parallel", "arbitrary")),
    )(h2, h2, h2, dp, dp, w)


def _adamw(w, g, m, v, name):
    shape = w.shape
    c = shape[-1]
    rows = math.prod(shape[:-1])
    to2d = lambda a: a.reshape(rows, c)
    cap = max(8, (1 << 18) // c)
    tr = rows
    for cand in (2048, 1024, 512, 256, 128, 64, 32, 16, 8):
        if cand <= cap and rows % cand == 0:
            tr = cand
            break

    def body(w_ref, g_ref, m_ref, v_ref, d_ref, nm_ref, nv_ref):
        gv = g_ref[...]
        nm = ADAM_B1 * m_ref[...] + (1.0 - ADAM_B1) * gv
        nv = ADAM_B2 * v_ref[...] + (1.0 - ADAM_B2) * (gv * gv)
        m_hat = nm / (1.0 - ADAM_B1 ** ADAM_STEP)
        v_hat = nv / (1.0 - ADAM_B2 ** ADAM_STEP)
        d_ref[...] = -ADAM_LR * (m_hat / (jnp.sqrt(v_hat) + ADAM_EPS) + ADAM_WD * w_ref[...])
        nm_ref[...] = nm
        nv_ref[...] = nv

    blk = pl.BlockSpec((tr, c), lambda i: (i, 0))
    outs = pl.pallas_call(
        body, name=name, grid=(rows // tr,),
        in_specs=[blk] * 4, out_specs=[blk] * 3,
        out_shape=[jax.ShapeDtypeStruct((rows, c), F32)] * 3,
        compiler_params=_params(("parallel",)),
    )(to2d(w), to2d(g), to2d(m), to2d(v))
    return tuple(o.reshape(shape) for o in outs)


_ANY = pl.BlockSpec(memory_space=pl.ANY)


def _place():
    x, y, c = lax.axis_index("x"), lax.axis_index("y"), lax.axis_index("c")
    others = [(1 - x, y), (x, 1 - y), (1 - x, 1 - y)]
    return x, y, c, others


def _all_gather_chips(shard, name):
    rws, cols = shard.shape
    half = rws // 2

    def body(x_ref, out_ref, send_sems, recv_sems, local_sem):
        x, y, c, others = _place()
        me = 2 * x + y
        sib = (x, y, 1 - c)

        def rows(chip, cc):
            return out_ref.at[chip, pl.ds(pl.multiple_of(cc * half, 16), half), :]

        def copy(k, src, dst, to):
            return pltpu.make_async_remote_copy(src_ref=src, dst_ref=dst, send_sem=send_sems.at[k],
                                                recv_sem=recv_sems.at[k], device_id=to, device_id_type=MESH)

        mine = pltpu.make_async_copy(x_ref, out_ref.at[me], local_sem)
        mine.start()
        my_half = x_ref.at[pl.ds(pl.multiple_of(c * half, 16), half), :]
        first = [copy(j, my_half, rows(me, c), (ox, oy, c)) for j, (ox, oy) in enumerate(others)]
        for cp in first:
            cp.start()
        passed = []
        for j, (ox, oy) in enumerate(others):
            blk = rows(2 * ox + oy, c)
            copy(j, blk, blk, (x, y, c)).wait_recv()
            fwd = copy(3 + j, blk, blk, sib)
            fwd.start()
            passed.append(fwd)
        for j, (ox, oy) in enumerate(others):
            blk = rows(2 * ox + oy, 1 - c)
            copy(3 + j, blk, blk, (x, y, c)).wait_recv()
        for cp in first + passed:
            cp.wait_send()
        mine.wait()

    return pl.pallas_call(
        body, name=name,
        in_specs=[_ANY], out_specs=_ANY,
        out_shape=jax.ShapeDtypeStruct((N_CHIPS, rws, cols), shard.dtype),
        scratch_shapes=[pltpu.SemaphoreType.DMA((6,)), pltpu.SemaphoreType.DMA((6,)), pltpu.SemaphoreType.DMA],
        compiler_params=pltpu.CompilerParams(has_side_effects=True),
    )(shard)


def _swap_halves(buf, name):
    nb, rws, cols = buf.shape
    half = rws // 2

    def body(b_ref, own_ref, sib_ref, send_sem, recv_sem, local_sem):
        x, y, c, _ = _place()
        keep = b_ref.at[:, pl.ds(pl.multiple_of(c * half, 16), half), :]
        give = b_ref.at[:, pl.ds(pl.multiple_of((1 - c) * half, 16), half), :]
        mine = pltpu.make_async_copy(keep, own_ref, local_sem)
        mine.start()
        cp = pltpu.make_async_remote_copy(src_ref=give, dst_ref=sib_ref, send_sem=send_sem, recv_sem=recv_sem,
                                          device_id=(x, y, 1 - c), device_id_type=MESH)
        cp.start()
        cp.wait()
        mine.wait()

    shp = jax.ShapeDtypeStruct((nb, half, cols), buf.dtype)
    return pl.pallas_call(
        body, name=name,
        in_specs=[_ANY], out_specs=[_ANY, _ANY], out_shape=[shp, shp],
        scratch_shapes=[pltpu.SemaphoreType.DMA, pltpu.SemaphoreType.DMA, pltpu.SemaphoreType.DMA],
        compiler_params=pltpu.CompilerParams(has_side_effects=True),
    )(buf)


def _add2(a, b, name):
    nb, rws, cols = a.shape
    t = _tile(rws, 256)
    if rws % t:
        t = rws

    def body(a_ref, b_ref, o_ref):
        o_ref[...] = (a_ref[...].astype(F32) + b_ref[...].astype(F32)).astype(o_ref.dtype)

    blk = pl.BlockSpec((1, t, cols), lambda i, j: (i, j, 0))
    return pl.pallas_call(
        body, name=name, grid=(nb, rws // t), in_specs=[blk, blk], out_specs=blk,
        out_shape=jax.ShapeDtypeStruct(a.shape, a.dtype),
        compiler_params=_params(("parallel", "parallel")),
    )(a, b)


def _exchange_chips(pre, name):
    nb, half, cols = pre.shape

    def body(p_ref, out_ref, send_sems, recv_sems, local_sem):
        x, y, c, others = _place()
        me = 2 * x + y
        mine = pltpu.make_async_copy(p_ref.at[me], out_ref.at[me], local_sem)
        mine.start()
        sends = []
        for j, (ox, oy) in enumerate(others):
            cp = pltpu.make_async_remote_copy(src_ref=p_ref.at[2 * ox + oy], dst_ref=out_ref.at[me],
                                              send_sem=send_sems.at[j], recv_sem=recv_sems.at[j],
                                              device_id=(ox, oy, c), device_id_type=MESH)
            cp.start()
            sends.append(cp)
        for j, (ox, oy) in enumerate(others):
            blk = out_ref.at[2 * ox + oy]
            pltpu.make_async_remote_copy(src_ref=blk, dst_ref=blk, send_sem=send_sems.at[j],
                                         recv_sem=recv_sems.at[j], device_id=(x, y, c),
                                         device_id_type=MESH).wait_recv()
        for cp in sends:
            cp.wait_send()
        mine.wait()

    return pl.pallas_call(
        body, name=name,
        in_specs=[_ANY], out_specs=_ANY, out_shape=jax.ShapeDtypeStruct(pre.shape, pre.dtype),
        scratch_shapes=[pltpu.SemaphoreType.DMA((3,)), pltpu.SemaphoreType.DMA((3,)), pltpu.SemaphoreType.DMA],
        compiler_params=pltpu.CompilerParams(has_side_effects=True),
    )(pre)


def _add4(parts, name):
    nb, half, cols = parts.shape
    t = _tile(half, 256)
    if half % t:
        t = half

    def body(p_ref, o_ref):
        acc = p_ref[0].astype(F32)
        for k in range(1, nb):
            acc = acc + p_ref[k].astype(F32)
        o_ref[...] = acc

    return pl.pallas_call(
        body, name=name, grid=(half // t,),
        in_specs=[pl.BlockSpec((nb, t, cols), lambda i: (0, i, 0))],
        out_specs=pl.BlockSpec((t, cols), lambda i: (i, 0)),
        out_shape=jax.ShapeDtypeStruct((half, cols), F32),
        compiler_params=_params(("parallel",)),
    )(parts)


def _join_halves(mine_half, name):
    half, cols = mine_half.shape

    def body(h_ref, out_ref, send_sem, recv_sem, local_sem):
        x, y, c, _ = _place()
        dst = out_ref.at[pl.ds(pl.multiple_of(c * half, 8), half), :]
        mine = pltpu.make_async_copy(h_ref, dst, local_sem)
        mine.start()
        cp = pltpu.make_async_remote_copy(src_ref=h_ref, dst_ref=dst, send_sem=send_sem, recv_sem=recv_sem,
                                          device_id=(x, y, 1 - c), device_id_type=MESH)
        cp.start()
        cp.wait()
        mine.wait()

    return pl.pallas_call(
        body, name=name,
        in_specs=[_ANY], out_specs=_ANY, out_shape=jax.ShapeDtypeStruct((2 * half, cols), mine_half.dtype),
        scratch_shapes=[pltpu.SemaphoreType.DMA, pltpu.SemaphoreType.DMA, pltpu.SemaphoreType.DMA],
        compiler_params=pltpu.CompilerParams(has_side_effects=True),
    )(mine_half)


def _reduce_scatter_chips(buf, tag):
    own, sib = _swap_halves(buf, "rs_swap_" + tag)
    pre = _add2(own, sib, "rs_add2_" + tag)
    parts = _exchange_chips(pre, "rs_xchg_" + tag)
    red = _add4(parts, "rs_add4_" + tag)
    return _join_halves(red, "rs_join_" + tag)


MAX_DMA_BYTES = 2 * 1024 * 1024
ROW_ALIGN = 16


def _pieces(rows, row_bytes):
    n = max(1, -(-(rows * row_bytes) // MAX_DMA_BYTES))
    step = -(-(-(-rows // n)) // ROW_ALIGN) * ROW_ALIGN
    return [(r, min(step, rows - r)) for r in range(0, rows, step)]


def _half_plan(arrays, row_axis):
    plan = []
    for a, arr in enumerate(arrays):
        row_bytes = math.prod(arr.shape[row_axis + 1:]) * arr.dtype.itemsize * (arr.shape[0] if row_axis else 1)
        plan += [(a, r0, nr) for r0, nr in _pieces(arr.shape[row_axis] // 2, row_bytes)]
    return plan


def _rows(start, size):
    return pl.ds(pl.multiple_of(start, ROW_ALIGN), size)


def _remote(src, dst, send_sems, recv_sems, k, to):
    return pltpu.make_async_remote_copy(src_ref=src, dst_ref=dst, send_sem=send_sems.at[k], recv_sem=recv_sems.at[k],
                                        device_id=to, device_id_type=MESH)


def _comm_call(body, name, ins, out_shapes, n_remote, n_local, aliases=None):
    return pl.pallas_call(
        body, name=name,
        in_specs=[_ANY] * len(ins), out_specs=[_ANY] * len(out_shapes), out_shape=out_shapes,
        scratch_shapes=[pltpu.SemaphoreType.DMA((n_remote,)), pltpu.SemaphoreType.DMA((n_remote,)),
                        pltpu.SemaphoreType.DMA((max(n_local, 1),))],
        input_output_aliases=aliases or {},
        compiler_params=pltpu.CompilerParams(has_side_effects=True),
    )(*ins)


def _cast_shard(w, l, me_idx, name):
    _, k, cols = w.shape
    tr = _tile(k, 256)
    if k % tr:
        tr = k

    def body(me_ref, w_ref, s_ref, land_ref):
        del me_ref
        v = w_ref[...].astype(BF16)
        s_ref[...] = v
        land_ref[...] = v

    grid_spec = pltpu.PrefetchScalarGridSpec(
        num_scalar_prefetch=1, grid=(k // tr,),
        in_specs=[pl.BlockSpec((None, tr, cols), lambda i, me: (l, i, 0))],
        out_specs=[pl.BlockSpec((tr, cols), lambda i, me: (i, 0)),
                   pl.BlockSpec((None, tr, cols), lambda i, me: (me[0], i, 0))])
    return pl.pallas_call(
        body, name=name, grid_spec=grid_spec,
        out_shape=[jax.ShapeDtypeStruct((k, cols), BF16), jax.ShapeDtypeStruct((N_CHIPS, k, cols), BF16)],
        compiler_params=_params(("parallel",)),
    )(me_idx, w)


def _gather_d2d(lands, name):
    n = len(lands)
    plan = _half_plan(lands, 1)
    plan = [(a, r0, nr) for a, r0, nr in plan]

    def body(*refs):
        out_refs = refs[n:2 * n]
        send_sems, recv_sems, _ = refs[2 * n:]
        x, y, c, others = _place()
        sends = []
        for i, (a, r0, nr) in enumerate(plan):
            rows = _rows(c * (lands[a].shape[1] // 2) + r0, nr)
            for j, (ox, oy) in enumerate(others):
                blk = out_refs[a].at[2 * ox + oy, rows, :]
                cp = _remote(blk, blk, send_sems, recv_sems, 3 * i + j, (x, y, 1 - c))
                cp.start()
                sends.append(cp)
        for i, (a, r0, nr) in enumerate(plan):
            rows = _rows((1 - c) * (lands[a].shape[1] // 2) + r0, nr)
            for j, (ox, oy) in enumerate(others):
                blk = out_refs[a].at[2 * ox + oy, rows, :]
                _remote(blk, blk, send_sems, recv_sems, 3 * i + j, (x, y, c)).wait_recv()
        for cp in sends:
            cp.wait_send()

    outs = [jax.ShapeDtypeStruct(a.shape, a.dtype) for a in lands]
    return _comm_call(body, name, lands, outs, 3 * len(plan), 0, aliases={a: a for a in range(n)})


def _rs_swap(ts, name):
    n = len(ts)
    plan = _half_plan(ts, 1)

    def body(*refs):
        t_refs, out_refs = refs[:n], refs[n:2 * n]
        send_sems, recv_sems, _ = refs[2 * n:]
        x, y, c, _o = _place()
        sends = []
        for i, (a, r0, nr) in enumerate(plan):
            src = t_refs[a].at[:, _rows((1 - c) * (ts[a].shape[1] // 2) + r0, nr), :]
            cp = _remote(src, out_refs[a].at[:, pl.ds(r0, nr), :], send_sems, recv_sems, i, (x, y, 1 - c))
            cp.start()
            sends.append(cp)
        for i, (a, r0, nr) in enumerate(plan):
            blk = out_refs[a].at[:, pl.ds(r0, nr), :]
            _remote(blk, blk, send_sems, recv_sems, i, (x, y, c)).wait_recv()
        for cp in sends:
            cp.wait_send()

    outs = [jax.ShapeDtypeStruct((t.shape[0], t.shape[1] // 2, t.shape[2]), t.dtype) for t in ts]
    return _comm_call(body, name, ts, outs, len(plan), 0)


def _add_half(t, got, c_idx, me_idx, name):
    nb, k, cols = t.shape
    half = k // 2

    def body(c_ref, me_ref, t_ref, g_ref, o_ref, mine_ref):
        del c_ref
        v = (t_ref[...].astype(F32) + g_ref[...].astype(F32)).astype(o_ref.dtype)
        o_ref[...] = v

        @pl.when(pl.program_id(0) == me_ref[0])
        def _():
            mine_ref[...] = v

    blk = pl.BlockSpec((1, half, cols), lambda i, c, me: (i, 0, 0))
    grid_spec = pltpu.PrefetchScalarGridSpec(
        num_scalar_prefetch=2, grid=(nb,),
        in_specs=[pl.BlockSpec((1, half, cols), lambda i, c, me: (i, c[0], 0)), blk],
        out_specs=[blk, pl.BlockSpec((1, half, cols), lambda i, c, me: (me[0], 0, 0))])
    shp = jax.ShapeDtypeStruct(got.shape, got.dtype)
    return pl.pallas_call(
        body, name=name, grid_spec=grid_spec, out_shape=[shp, shp],
        compiler_params=_params(("arbitrary",)),
    )(c_idx, me_idx, t, got)


def _add4_half(parts, c_idx, name):
    nb, half, cols = parts.shape
    t = _tile(half, 256)
    if half % t:
        t = half
    steps = half // t

    def body(c_ref, p_ref, o_ref):
        del c_ref
        acc = p_ref[0].astype(F32)
        for k in range(1, nb):
            acc = acc + p_ref[k].astype(F32)
        o_ref[...] = acc

    grid_spec = pltpu.PrefetchScalarGridSpec(
        num_scalar_prefetch=1, grid=(steps,),
        in_specs=[pl.BlockSpec((nb, t, cols), lambda i, c: (0, i, 0))],
        out_specs=pl.BlockSpec((t, cols), lambda i, c: (c[0] * steps + i, 0)))
    return pl.pallas_call(
        body, name=name, grid_spec=grid_spec, out_shape=jax.ShapeDtypeStruct((2 * half, cols), F32),
        compiler_params=_params(("parallel",)),
    )(c_idx, parts)


def _rs_join(fulls, name):
    n = len(fulls)
    plan = _half_plan(fulls, 0)

    def body(*refs):
        out_refs = refs[n:2 * n]
        send_sems, recv_sems, _ = refs[2 * n:]
        x, y, c, _o = _place()
        sends = []
        for i, (a, r0, nr) in enumerate(plan):
            blk = out_refs[a].at[_rows(c * (fulls[a].shape[0] // 2) + r0, nr), :]
            cp = _remote(blk, blk, send_sems, recv_sems, i, (x, y, 1 - c))
            cp.start()
            sends.append(cp)
        for i, (a, r0, nr) in enumerate(plan):
            blk = out_refs[a].at[_rows((1 - c) * (fulls[a].shape[0] // 2) + r0, nr), :]
            _remote(blk, blk, send_sems, recv_sems, i, (x, y, c)).wait_recv()
        for cp in sends:
            cp.wait_send()

    outs = [jax.ShapeDtypeStruct(f.shape, f.dtype) for f in fulls]
    return _comm_call(body, name, fulls, outs, len(plan), 0, aliases={a: a for a in range(n)})


_HBM = pl.BlockSpec(memory_space=pltpu.HBM)
_SEM = pl.BlockSpec(memory_space=pltpu.SEMAPHORE)
_EFFECT = pltpu.SideEffectType.DATAFLOW_SIDE_EFFECTING


def _ici_plan(kind, a_list):
    if kind == "gather":
        return _half_plan(a_list, 0)
    plan = []
    for a, p in enumerate(a_list):
        plan += [(a, r0, nr) for r0, nr in _pieces(p.shape[1], p.shape[2] * p.dtype.itemsize)]
    return plan


def _ici_refs(kind, a_ref, b_ref, a_shape, r0, nr, c, me, peer):
    if kind == "gather":
        rows = _rows(c * (a_shape[0] // 2) + r0, nr)
        return a_ref.at[rows, :], b_ref.at[me, rows, :], b_ref.at[peer, rows, :]
    rows = pl.ds(r0, nr)
    return a_ref.at[peer, rows, :], b_ref.at[me, rows, :], b_ref.at[peer, rows, :]


def _ici_start(kind, a_list, b_list, name):
    n = len(a_list)
    plan = _ici_plan(kind, a_list)
    shapes = [a.shape for a in a_list]

    def body(*refs):
        a_refs, b_refs = refs[:n], refs[n:2 * n]
        send_sems, recv_sems = refs[2 * n], refs[2 * n + 1]
        token = refs[4 * n + 2]
        x, y, c, others = _place()
        me = 2 * x + y
        for i, (a, r0, nr) in enumerate(plan):
            for j, (ox, oy) in enumerate(others):
                src, dst, _ = _ici_refs(kind, a_refs[a], b_refs[a], shapes[a], r0, nr, c, me, 2 * ox + oy)
                _remote(src, dst, send_sems, recv_sems, 3 * i + j, (ox, oy, c)).start()
        token[...] = jnp.zeros_like(token)

    hbm = lambda v: pltpu.HBM(v.shape, v.dtype)
    ncp = 3 * len(plan)
    outs = pl.pallas_call(
        body, name=name,
        in_specs=[_HBM] * (2 * n),
        out_specs=[_SEM, _SEM] + [_HBM] * (2 * n) + [pl.BlockSpec(memory_space=pltpu.VMEM)],
        out_shape=[pltpu.SemaphoreType.DMA((ncp,)), pltpu.SemaphoreType.DMA((ncp,))]
                  + [hbm(v) for v in a_list] + [hbm(v) for v in b_list] + [jax.ShapeDtypeStruct((8, LANES), F32)],
        input_output_aliases={i: 2 + i for i in range(2 * n)},
        compiler_params=pltpu.CompilerParams(has_side_effects=_EFFECT),
    )(*[pltpu.with_memory_space_constraint(v, pltpu.HBM) for v in list(a_list) + list(b_list)])
    return outs[0], outs[1], outs[2:2 + n], outs[2 + n:2 + 2 * n], outs[2 + 2 * n]


def _ici_wait(kind, started, after, name):
    send_sems, recv_sems, a_list, b_list, _ = started
    n = len(a_list)
    plan = _ici_plan(kind, a_list)
    shapes = [a.shape for a in a_list]

    def body(*refs):
        a_refs, b_refs = refs[:n], refs[n:2 * n]
        send_sems, recv_sems = refs[2 * n], refs[2 * n + 1]
        x, y, c, others = _place()
        me = 2 * x + y
        for i, (a, r0, nr) in enumerate(plan):
            for j, (ox, oy) in enumerate(others):
                src, dst, land = _ici_refs(kind, a_refs[a], b_refs[a], shapes[a], r0, nr, c, me, 2 * ox + oy)
                _remote(src, dst, send_sems, recv_sems, 3 * i + j, (ox, oy, c)).wait_send()
                _remote(land, land, send_sems, recv_sems, 3 * i + j, (x, y, c)).wait_recv()

    hbm = lambda v: pltpu.HBM(v.shape, v.dtype)
    outs = pl.pallas_call(
        body, name=name,
        in_specs=[_HBM] * (2 * n) + [_SEM, _SEM, _ANY],
        out_specs=[_HBM] * (2 * n),
        out_shape=[hbm(v) for v in a_list] + [hbm(v) for v in b_list],
        input_output_aliases={i: i for i in range(2 * n)},
        compiler_params=pltpu.CompilerParams(has_side_effects=_EFFECT),
    )(*a_list, *b_list, send_sems, recv_sems, after)
    return outs[n:]


def _rs_begin(ts, c_idx, me_idx, tag):
    got = _rs_swap(ts, "rs_swap_" + tag)
    pairs = [_add_half(t, g, c_idx, me_idx, f"rs_add2_{tag}_{a}") for a, (t, g) in enumerate(zip(ts, got))]
    return _ici_start("scatter", [p for p, _ in pairs], [m for _, m in pairs], "rs_xchg_start_" + tag)


def _rs_finish(started, after, c_idx, tag):
    parts = _ici_wait("scatter", started, after, "rs_xchg_wait_" + tag)
    fulls = [_add4_half(p, c_idx, f"rs_add4_{tag}_{a}") for a, p in enumerate(parts)]
    return _rs_join(fulls, "rs_join_" + tag)


def _pack_rows(pieces, rows, dtype):
    flat = jnp.concatenate([p.astype(dtype).reshape(-1) for p in pieces])
    return jnp.pad(flat, (0, rows * PACK_COLS - flat.shape[0])).reshape(rows, PACK_COLS)


def _unpack(flat, shapes):
    out, off = [], 0
    for shp in shapes:
        size = math.prod(shp)
        out.append(flat[off:off + size].reshape(shp))
        off += size
    return out


def _rows_for(n_elems, mult):
    rows = -(-n_elems // PACK_COLS)
    return -(-rows // mult) * mult


BIG_SHARDS = [("w_in", (D_MODEL, 1474)), ("w_branch_att", (D_ATT, 256)), ("w_branch_conv", (D_CONV, 256)),
              ("w_branch_sgu", (D_SGU, 256)), ("w_out", (256, D_MODEL)), ("w_ffn_up", (D_MODEL, FF_BLK)),
              ("w_ffn_down", (D_FF // N_CHIPS, D_MODEL))]
SMALL_SHARDS = [("b_gate", (3, 256)), ("conv_mix_w", (3, 64)), ("conv_ffn_w", (3, FF_BLK))]
REPLICATED = [("pre_mix_g", (D_MODEL,)), ("post_mix_g", (D_MODEL,)), ("pre_ffn_g", (D_MODEL,)),
              ("post_ffn_g", (D_MODEL,)), ("b_forget", (N_HEADS,)), ("sgu_ln_g", (D_SGU,)), ("sgu_ln_b", (D_SGU,)),
              ("sgu_w", (N_GROUPS, CHUNK, CHUNK)), ("sgu_b", (N_GROUPS, CHUNK))]
WEIGHT_ORDER = ["pre_mix_g", "post_mix_g", "pre_ffn_g", "post_ffn_g", "w_in", "b_forget", "b_gate", "conv_mix_w",
                "sgu_ln_g", "sgu_ln_b", "sgu_w", "sgu_b", "w_branch_att", "w_branch_conv", "w_branch_sgu", "w_out",
                "w_ffn_up", "conv_ffn_w", "w_ffn_down"]

_SMALL_ELEMS = sum(math.prod(s) for _, s in SMALL_SHARDS)
_REP_ELEMS = sum(math.prod(s) for _, s in REPLICATED)
_REP_QUARTER = -(-(DEPTH * _REP_ELEMS) // N_CHIPS)
SMALL_PARAM_ROWS = _rows_for(DEPTH * _SMALL_ELEMS, 32)
SMALL_ROWS = _rows_for(DEPTH * _SMALL_ELEMS + _REP_QUARTER, 32)
IN_WIDTH = 5896
IN_SHARD = IN_WIDTH // N_CHIPS
IN_SHARD_PAD = 1536
IN_PAD = 6144


def _gather_small(wts):
    shard = _pack_rows([wts[n] for n, _ in SMALL_SHARDS], SMALL_PARAM_ROWS, F32)
    full = _all_gather_chips(shard, "gather_small_params").reshape(N_CHIPS, -1)
    per_chip = [_unpack(full[j], [(DEPTH,) + s for _, s in SMALL_SHARDS]) for j in range(N_CHIPS)]
    return {n: jnp.concatenate([per_chip[j][i] for j in range(N_CHIPS)], axis=-1)
            for i, (n, _) in enumerate(SMALL_SHARDS)}


BIG_NAMES = [n for n, _ in BIG_SHARDS]
FIRST_NAMES = ["w_in"]
LATE_NAMES = BIG_NAMES[1:]


def _gather_begin(wts, l, me_idx, names, tag):
    cast = [_cast_shard(wts[n], l, me_idx, "cast_" + n) for n in names]
    return _ici_start("gather", [sh for sh, _ in cast], [ld for _, ld in cast], "gather_ici_start_" + tag)


def _gather_finish(started, after, names, tag):
    lands = _ici_wait("gather", started, after, "gather_ici_wait_" + tag)
    return dict(zip(names, _gather_d2d(lands, "gather_d2d_" + tag)))


def _pad_rows(a, rows):
    return jnp.pad(a, ((0, rows - a.shape[0]), (0, 0)))


def _whole_cols(land):
    return land.transpose(1, 0, 2).reshape(land.shape[1], -1)


_O_F = 3 * D_ATT
_O_B = _O_F + N_HEADS
_O_GL = _O_B + 3 * D_CONV + 2 * D_SGU


def _prep_first(wts, lands, small, l):
    w_in = _whole_cols(lands["w_in"])
    cf = small["conv_ffn_w"][l]
    blk = lambda a, j: a[:, j * FF_BLK:(j + 1) * FF_BLK]
    return {
        "w_p": jnp.concatenate([w_in[:, _O_GL:], w_in[:, :_O_F], w_in[:, _O_B:_O_GL], w_in[:, _O_F:_O_B],
                                jnp.zeros((D_MODEL, IN_PAD - IN_WIDTH), BF16)], axis=1),
        "wf_t": _pad_rows(w_in[:, _O_F:_O_B].T, F_ROWS),
        "b_forget": _pad_rows(wts["b_forget"][l].reshape(N_HEADS, 1), F_ROWS),
        "b_gate": _pad_rows(small["b_gate"][l], 8),
        "conv_mix_w": _pad_rows(small["conv_mix_w"][l], 8),
        "conv_ffn_w": _pad_rows(jnp.concatenate([blk(cf, 0), blk(cf, 2), blk(cf, 1), blk(cf, 3)], axis=1), 8),
        "pre_mix_g": wts["pre_mix_g"][l].reshape(1, -1), "post_mix_g": wts["post_mix_g"][l].reshape(1, -1),
        "pre_ffn_g": wts["pre_ffn_g"][l].reshape(1, -1), "post_ffn_g": wts["post_ffn_g"][l].reshape(1, -1),
        "ln_g": wts["sgu_ln_g"][l].reshape(1, -1), "ln_b": wts["sgu_ln_b"][l].reshape(1, -1),
        "sgu_w": wts["sgu_w"][l],
        "sgu_bias": jnp.repeat(wts["sgu_b"][l].T, HEAD_DIM, axis=1),
    }


def _prep_late(lands):
    up = lands["w_ffn_up"]
    return {
        "w_att": _whole_cols(lands["w_branch_att"]), "w_conv": _whole_cols(lands["w_branch_conv"]),
        "w_sgu": _whole_cols(lands["w_branch_sgu"]),
        "w_out": lands["w_out"].reshape(D_MODEL, D_MODEL),
        "w_up": jnp.concatenate([up[0], up[2], up[1], up[3]], axis=1),
        "w_down": lands["w_ffn_down"].reshape(D_FF, D_MODEL),
    }


def _layer_fwd(x, p, dep=None, late=None):
    s = x.shape[0]
    xn = _rms_fwd(x, p["pre_mix_g"], "rms_pre_mix", dep)
    h = _mm(xn, p["w_p"], "nn", BF16, "mm_in", s, 512, D_MODEL)
    f_row = _mm(p["wf_t"], xn, "nt", F32, "mm_forget", F_ROWS, 2048, D_MODEL)
    ck = _gate_fwd(f_row, p["b_forget"], "gate_fwd")
    o, o_f32, lse = _attn_fwd(h, ck, "attn_fwd")
    yc = _sconv_fwd(h, p["conv_mix_w"], "sconv_fwd")
    ys = _sgu_fwd(h, p["ln_g"], p["ln_b"], p["sgu_w"], p["sgu_bias"], "sgu_fwd")
    if late is not None:
        p.update(late(o))
    merged = _merge_fwd(h, (o, yc, ys), (p["w_att"], p["w_conv"], p["w_sgu"]), p["b_gate"], "merge_fwd")
    mo = _mm(merged, p["w_out"], "nn", F32, "mm_out", 2048, 512, D_MODEL)
    x1 = _resid_post(x, mo, p["post_mix_g"], "post_mix")
    xn2 = _rms_fwd(x1, p["pre_ffn_g"], "rms_pre_ffn")
    h2 = _mm(xn2, p["w_up"], "nn", BF16, "mm_up", 2048, 512, D_MODEL)
    pact = _ffn_act_fwd(h2, p["conv_ffn_w"], "ffn_act_fwd")
    ff = _mm(pact, p["w_down"], "nn", F32, "mm_down", 2048, 512, FF_BLK)
    x2 = _resid_post(x1, ff, p["post_ffn_g"], "post_ffn")
    saved = dict(x=x, xn=xn, h=h, f_row=f_row, ck=ck, o=o, o_f32=o_f32, lse=lse, yc=yc, ys=ys, merged=merged, mo=mo, x1=x1,
                 xn2=xn2, h2=h2, pact=pact, ff=ff)
    return x2, saved


def _layer_bwd(dx2, p, sv, dep=None, early=None):
    s = dx2.shape[0]
    g = {}
    same = lambda b: b
    dff, g["post_ffn_g"] = _rms_bwd(sv["ff"], p["post_ffn_g"], [dx2], None, BF16, "post_ffn_bwd", dep)
    dpact = _mm(dff, p["w_down"], "nt", BF16, "mm_down_dx", 1024, FF_BLK, D_MODEL)
    t_down = _mm(sv["pact"], dff, "tn", BF16, "mm_down_dw", 256, D_MODEL, s).reshape(N_CHIPS, -1, D_MODEL)
    dh2, dconv_ffn = _ffn_act_conv_bwd(sv["h2"], p["conv_ffn_w"], dpact, "ffn_act_conv_bwd")
    dxn2 = _mm(dh2, p["w_up"], "nt", F32, "mm_up_dx", 1024, D_MODEL, FF_BLK)
    t_up = _mm(sv["xn2"], dh2, "tn", BF16, "mm_up_dw", 512, FF_BLK, s, chip_of=lambda b: (b % 2) * 2 + b // 2)
    dx1, g["pre_ffn_g"] = _rms_bwd(sv["x1"], p["pre_ffn_g"], [dxn2], dx2, F32, "pre_ffn_bwd")
    dep_mix = early([t_up, t_down]) if early is not None else None
    dmo, g["post_mix_g"] = _rms_bwd(sv["mo"], p["post_mix_g"], [dx1], None, BF16, "post_mix_bwd", dep_mix)
    dmerged = _mm(dmo, p["w_out"], "nt", F32, "mm_out_dx", 2048, 512, D_MODEL)
    t_out = _mm(sv["merged"], dmo, "tn", BF16, "mm_out_dw", 512, D_MODEL, s).reshape(N_CHIPS, -1, D_MODEL)
    acts = (sv["o"], sv["yc"], sv["ys"])
    ws = (p["w_att"], p["w_conv"], p["w_sgu"])
    dy_a, dy_c, dy_s, dgl, db_gate = _merge_bwd(sv["h"], acts, ws, p["b_gate"], dmerged, "merge_bwd")
    do = _mm(dy_a, p["w_att"], "nt", BF16, "mm_att_dx", 2048, D_ATT, D_MODEL)
    dyc = _mm(dy_c, p["w_conv"], "nt", BF16, "mm_conv_dx", 2048, D_CONV, D_MODEL)
    dys = _mm(dy_s, p["w_sgu"], "nt", BF16, "mm_sgu_dx", 2048, D_SGU, D_MODEL)
    t_att = _mm(sv["o"], dy_a, "tn", BF16, "mm_att_dw", D_ATT, 256, s, chip_of=same)
    t_conv = _mm(sv["yc"], dy_c, "tn", BF16, "mm_conv_dw", D_CONV, 256, s, chip_of=same)
    t_sgu = _mm(sv["ys"], dy_s, "tn", BF16, "mm_sgu_dw", D_SGU, 256, s, chip_of=same)
    d_conv, dconv_mix = _sconv_bwd(sv["h"], p["conv_mix_w"], dyc, "sconv_bwd")
    d_sgu, g["sgu_ln_g"], g["sgu_ln_b"], g["sgu_w"], dbias = _sgu_bwd(
        sv["h"], p["ln_g"], p["ln_b"], p["sgu_w"], p["sgu_bias"], dys, "sgu_bwd")
    dq, dk, dv, dc_even, dc_odd = _attn_bwd(sv["h"], sv["ck"], sv["o_f32"], sv["lse"], do, "attn_bwd")
    df, db_forget = _gate_bwd(sv["f_row"], p["b_forget"], dc_even, dc_odd, "gate_bwd")
    f_cols = jnp.concatenate([df[:N_HEADS].T, jnp.zeros((s, IN_PAD - IN_WIDTH), BF16)], axis=1)
    dh = _assemble_dh([dgl, dq, dk, dv, d_conv, d_sgu, f_cols], "assemble_dh")
    dxn = _mm(dh, p["w_p"], "nt", F32, "mm_in_dx", 1024, D_MODEL, 2048)
    dw_p = _mm(sv["xn"], dh, "tn", BF16, "mm_in_dw", D_MODEL, 512, s)
    dw_in = jnp.concatenate([dw_p[:, OFF_Q:OFF_BG], dw_p[:, W_P:W_P + N_HEADS], dw_p[:, OFF_BG:W_P], dw_p[:, :OFF_Q],
                             jnp.zeros((D_MODEL, IN_SHARD_PAD - IN_SHARD), BF16)], axis=1)
    t_in = jnp.stack([dw_in[:, j * IN_SHARD:j * IN_SHARD + IN_SHARD_PAD] for j in range(N_CHIPS)])
    dx, g["pre_mix_g"] = _rms_bwd(sv["x"], p["pre_mix_g"], [dxn], dx1, F32, "pre_mix_bwd")
    blk = lambda a, j: a[:, j * FF_BLK:(j + 1) * FF_BLK]
    g["conv_ffn_w"] = jnp.concatenate([blk(dconv_ffn, 0), blk(dconv_ffn, 2), blk(dconv_ffn, 1),
                                       blk(dconv_ffn, 3)], axis=1)[:3]
    g["conv_mix_w"] = dconv_mix[:3]
    g["b_gate"] = db_gate[:3]
    g["b_forget"] = db_forget[:N_HEADS, 0]
    g["sgu_b"] = jnp.sum(dbias.reshape(CHUNK, N_GROUPS, HEAD_DIM), axis=-1).T
    for n in ("pre_mix_g", "post_mix_g", "pre_ffn_g", "post_ffn_g", "sgu_ln_g", "sgu_ln_b"):
        g[n] = g[n].reshape(-1)
    mix = [t_in, t_att, t_conv, t_sgu, t_out]
    return dx, (mix if early is not None else mix + [t_up, t_down]), g


def _assemble_dh(pieces, name):
    s = pieces[0].shape[0]
    t = _tile(s, 512)
    width = sum(a.shape[1] for a in pieces)

    def body(*refs):
        out = refs[-1]
        col = 0
        for ref in refs[:-1]:
            w = ref.shape[1]
            out[:, col:col + w] = ref[...].astype(out.dtype)
            col += w

    return pl.pallas_call(
        body, name=name, grid=(s // t,),
        in_specs=[pl.BlockSpec((t, a.shape[1]), lambda i: (i, 0)) for a in pieces],
        out_specs=pl.BlockSpec((t, width), lambda i: (i, 0)),
        out_shape=jax.ShapeDtypeStruct((s, width), BF16),
        compiler_params=_params(("parallel",)),
    )(*pieces)


def _shard_cols(a, j):
    w = a.shape[-1] // N_CHIPS
    return a[..., j * w:(j + 1) * w]


def kernel(x, pre_mix_g, post_mix_g, pre_ffn_g, post_ffn_g, w_in, b_forget, b_gate, conv_mix_w, sgu_ln_g, sgu_ln_b, sgu_w, sgu_b, w_branch_att, w_branch_conv, w_branch_sgu, w_out, w_ffn_up, conv_ffn_w, w_ffn_down, loss_target, m_pre_mix_g, m_post_mix_g, m_pre_ffn_g, m_post_ffn_g, m_w_in, m_b_forget, m_b_gate, m_conv_mix_w, m_sgu_ln_g, m_sgu_ln_b, m_sgu_w, m_sgu_b, m_w_branch_att, m_w_branch_conv, m_w_branch_sgu, m_w_out, m_w_ffn_up, m_conv_ffn_w, m_w_ffn_down, v_pre_mix_g, v_post_mix_g, v_pre_ffn_g, v_post_ffn_g, v_w_in, v_b_forget, v_b_gate, v_conv_mix_w, v_sgu_ln_g, v_sgu_ln_b, v_sgu_w, v_sgu_b, v_w_branch_att, v_w_branch_conv, v_w_branch_sgu, v_w_out, v_w_ffn_up, v_conv_ffn_w, v_w_ffn_down):
    wts = dict(pre_mix_g=pre_mix_g, post_mix_g=post_mix_g, pre_ffn_g=pre_ffn_g, post_ffn_g=post_ffn_g, w_in=w_in,
               b_forget=b_forget, b_gate=b_gate, conv_mix_w=conv_mix_w, sgu_ln_g=sgu_ln_g, sgu_ln_b=sgu_ln_b,
               sgu_w=sgu_w, sgu_b=sgu_b, w_branch_att=w_branch_att, w_branch_conv=w_branch_conv,
               w_branch_sgu=w_branch_sgu, w_out=w_out, w_ffn_up=w_ffn_up, conv_ffn_w=conv_ffn_w,
               w_ffn_down=w_ffn_down)
    moms = dict(pre_mix_g=m_pre_mix_g, post_mix_g=m_post_mix_g, pre_ffn_g=m_pre_ffn_g, post_ffn_g=m_post_ffn_g,
                w_in=m_w_in, b_forget=m_b_forget, b_gate=m_b_gate, conv_mix_w=m_conv_mix_w, sgu_ln_g=m_sgu_ln_g,
                sgu_ln_b=m_sgu_ln_b, sgu_w=m_sgu_w, sgu_b=m_sgu_b, w_branch_att=m_w_branch_att,
                w_branch_conv=m_w_branch_conv, w_branch_sgu=m_w_branch_sgu, w_out=m_w_out, w_ffn_up=m_w_ffn_up,
                conv_ffn_w=m_conv_ffn_w, w_ffn_down=m_w_ffn_down)
    vels = dict(pre_mix_g=v_pre_mix_g, post_mix_g=v_post_mix_g, pre_ffn_g=v_pre_ffn_g, post_ffn_g=v_post_ffn_g,
                w_in=v_w_in, b_forget=v_b_forget, b_gate=v_b_gate, conv_mix_w=v_conv_mix_w, sgu_ln_g=v_sgu_ln_g,
                sgu_ln_b=v_sgu_ln_b, sgu_w=v_sgu_w, sgu_b=v_sgu_b, w_branch_att=v_w_branch_att,
                w_branch_conv=v_w_branch_conv, w_branch_sgu=v_w_branch_sgu, w_out=v_w_out, w_ffn_up=v_w_ffn_up,
                conv_ffn_w=v_conv_ffn_w, w_ffn_down=v_w_ffn_down)

    c_idx = lax.axis_index("c").astype(jnp.int32).reshape(1)
    me_idx = (2 * lax.axis_index("x") + lax.axis_index("y")).astype(jnp.int32).reshape(1)
    small = _gather_small(wts)

    xs = x[0]
    layers, saved = [], []
    first = _gather_begin(wts, 0, me_idx, FIRST_NAMES, "first")
    rest = _gather_begin(wts, 0, me_idx, LATE_NAMES, "late")
    lands = _gather_finish(first, xs, FIRST_NAMES, "first")
    late = lambda after: _prep_late(_gather_finish(rest, after, LATE_NAMES, "late"))
    for l in range(DEPTH):
        p = _prep_first(wts, lands, small, l)
        if l > 0:
            p.update(_prep_late(lands))
        nxt = _gather_begin(wts, l + 1, me_idx, BIG_NAMES, "all") if l + 1 < DEPTH else None
        dep = ([nxt[4]] if nxt else []) + ([rest[4]] if l == 0 else [])
        xs, sv = _layer_fwd(xs, p, dep or None, late if l == 0 else None)
        if nxt:
            lands = _gather_finish(nxt, xs, BIG_NAMES, "all")
        layers.append(p)
        saved.append(sv)
    dy, loss_part = _loss_head(xs, loss_target[0], "loss_head")
    loss = lax.psum(loss_part[0, 0], ("x", "y", "c"))

    big_red = [None] * DEPTH
    small_grads = [None] * DEPTH
    pending = None
    ffn = []
    for l in reversed(range(DEPTH)):
        early = None
        if l == 0:
            def early(ts_ffn):
                ffn.append(_rs_begin(ts_ffn, c_idx, me_idx, "ffn"))
                return ffn[0][4]
        dy, ts, small_grads[l] = _layer_bwd(dy, layers[l], saved[l], pending[4] if pending else None, early)
        if pending:
            big_red[l + 1] = _rs_finish(pending, dy, c_idx, "big")
        pending = _rs_begin(ts, c_idx, me_idx, "mix" if l == 0 else "big")
    red_ffn = _rs_finish(ffn[0], dy, c_idx, "ffn")
    grad_x = dy[None]

    rep_flat = jnp.concatenate([small_grads[l][n].reshape(-1) for l in range(DEPTH) for n, _ in REPLICATED])
    rep_flat = jnp.pad(rep_flat, (0, N_CHIPS * _REP_QUARTER - rep_flat.shape[0]))
    rows = []
    for j in range(N_CHIPS):
        pieces = [_shard_cols(small_grads[l][n], j) for l in range(DEPTH) for n, _ in SMALL_SHARDS]
        pieces.append(rep_flat[j * _REP_QUARTER:(j + 1) * _REP_QUARTER])
        rows.append(_pack_rows(pieces, SMALL_ROWS, F32))
    small_red = _reduce_scatter_chips(jnp.stack(rows), "small")
    small_all = _all_gather_chips(small_red, "gather_small")
    big_red[0] = _rs_finish(pending, small_all, c_idx, "mix") + red_ffn
    small_all = small_all.reshape(N_CHIPS, -1)

    grads = {}
    for i, (n, _) in enumerate(BIG_SHARDS):
        grads[n] = jnp.stack([big_red[l][i][:, :IN_SHARD] if n == "w_in" else big_red[l][i] for l in range(DEPTH)])
    mine_small = small_red.reshape(-1)
    parts = _unpack(mine_small, [s for _ in range(DEPTH) for _, s in SMALL_SHARDS])
    for i, (n, _) in enumerate(SMALL_SHARDS):
        grads[n] = jnp.stack([parts[l * len(SMALL_SHARDS) + i] for l in range(DEPTH)])
    off = DEPTH * _SMALL_ELEMS
    rep_all = jnp.concatenate([small_all[j, off:off + _REP_QUARTER] for j in range(N_CHIPS)])
    parts = _unpack(rep_all, [s for _ in range(DEPTH) for _, s in REPLICATED])
    for i, (n, _) in enumerate(REPLICATED):
        grads[n] = jnp.stack([parts[l * len(REPLICATED) + i] for l in range(DEPTH)])

    deltas, new_m, new_v = {}, {}, {}
    for n in WEIGHT_ORDER:
        deltas[n], new_m[n], new_v[n] = _adamw(wts[n], grads[n], moms[n], vels[n], "adamw_" + n)
    return (loss, grad_x, *[grads[n] for n in WEIGHT_ORDER], *[deltas[n] for n in WEIGHT_ORDER],
            *[new_m[n] for n in WEIGHT_ORDER], *[new_v[n] for n in WEIGHT_ORDER])
```

```python
import functools
import math

import jax
import jax.numpy as jnp
from jax import lax
from jax.experimental import pallas as pl
from jax.experimental.pallas import tpu as pltpu

F32 = jnp.float32
BF16 = jnp.bfloat16
MXU_DTYPE = jnp.bfloat16

D_MODEL = 1024
HEAD_DIM = 64
N_HEADS = 8
D_ATT = 512
D_CONV = 256
D_SGU = 256
N_GROUPS = 4
CHUNK = 128
D_FF = 2816
DEPTH = 4
RMS_EPS = 1e-6
LN_EPS = 1e-5
N_CHIPS = 4
LANES = 128
PACK_COLS = 1024
HALO = 16

ADAM_LR = 0.001
ADAM_B1 = 0.9
ADAM_B2 = 0.999
ADAM_EPS = 1e-08
ADAM_WD = 0.01
ADAM_STEP = 10

OFF_GL = 0
OFF_Q = 3 * D_MODEL
OFF_K = OFF_Q + D_ATT
OFF_V = OFF_K + D_ATT
OFF_BG = OFF_V + D_ATT
OFF_CG = OFF_BG + D_CONV
OFF_HC = OFF_CG + D_CONV
OFF_U = OFF_HC + D_CONV
OFF_VS = OFF_U + D_SGU
W_P = OFF_VS + D_SGU
F_ROWS = 16

VMEM_LIMIT = 56 * 1024 * 1024
MESH = pl.DeviceIdType.MESH


def _params(sem=None):
    if sem is None:
        return pltpu.CompilerParams(vmem_limit_bytes=VMEM_LIMIT)
    return pltpu.CompilerParams(dimension_semantics=sem, vmem_limit_bytes=VMEM_LIMIT)


def _tile(dim, pref):
    if dim <= pref:
        return dim
    if dim % pref == 0:
        return pref
    return dim


_DIMS = {"nn": (((1,), (0,)), ((), ())), "nt": (((1,), (1,)), ((), ())), "tn": (((0,), (0,)), ((), ()))}


def _mm(a, b, mode, out_dtype, name, tm, tn, tk, chip_of=None):
    if mode == "tn":
        K, M = a.shape
    else:
        M, K = a.shape
    N = b.shape[0] if mode == "nt" else b.shape[1]
    tm, tn, tk = _tile(M, tm), _tile(N // N_CHIPS if chip_of else N, tn), _tile(K, tk)
    nk = K // tk
    dims = _DIMS[mode]

    def body(a_ref, b_ref, o_ref, *acc):
        part = lax.dot_general(a_ref[...].astype(MXU_DTYPE), b_ref[...].astype(MXU_DTYPE), dims,
                               preferred_element_type=F32)
        if nk == 1:
            o_ref[...] = part.astype(o_ref.dtype)
        else:
            acc_ref = acc[0]
            k = pl.program_id(2)

            @pl.when(k == 0)
            def _():
                acc_ref[...] = part

            @pl.when(k > 0)
            def _():
                acc_ref[...] += part

            @pl.when(k == nk - 1)
            def _():
                o_ref[...] = acc_ref[...].astype(o_ref.dtype)

    if mode == "tn":
        a_spec = pl.BlockSpec((tk, tm), lambda i, j, k: (k, i))
    else:
        a_spec = pl.BlockSpec((tm, tk), lambda i, j, k: (i, k))
    if mode == "nt":
        b_spec = pl.BlockSpec((tn, tk), lambda i, j, k: (j, k))
    else:
        b_spec = pl.BlockSpec((tk, tn), lambda i, j, k: (k, j))
    if chip_of is None:
        out_spec = pl.BlockSpec((tm, tn), lambda i, j, k: (i, j))
        out_shape = jax.ShapeDtypeStruct((M, N), out_dtype)
    else:
        per = (N // N_CHIPS) // tn
        out_spec = pl.BlockSpec((None, tm, tn), lambda i, j, k: (chip_of(j // per), i, j % per))
        out_shape = jax.ShapeDtypeStruct((N_CHIPS, M, N // N_CHIPS), out_dtype)
    return pl.pallas_call(
        body,
        name=name,
        grid=(M // tm, N // tn, nk),
        in_specs=[a_spec, b_spec],
        out_specs=out_spec,
        out_shape=out_shape,
        scratch_shapes=[pltpu.VMEM((tm, tn), F32)] if nk > 1 else [],
        compiler_params=_params(("parallel", "parallel", "arbitrary")),
    )(a, b)


_GELU_K = math.sqrt(2.0 / math.pi)
_GELU_C = 0.044715


def _gelu(x):
    t = jnp.tanh(_GELU_K * (x + _GELU_C * (x * x * x)))
    return x * (0.5 * (1.0 + t))


def _gelu_and_grad(x):
    x2 = x * x
    t = jnp.tanh(_GELU_K * (x + _GELU_C * (x2 * x)))
    cdf = 0.5 * (1.0 + t)
    dcdf = 0.5 * (1.0 - t * t) * (_GELU_K * (1.0 + 3.0 * _GELU_C * x2))
    return x * cdf, cdf + x * dcdf


def _sigmoid(x):
    return 1.0 / (1.0 + jnp.exp(-x))


def _shift_down(cur, prev, k):
    h = prev.shape[0]
    ext = jnp.concatenate([prev, cur], axis=0)
    return pltpu.roll(ext, k, 0)[h:]


def _shift_up(cur, nxt, k):
    t, h = cur.shape[0], nxt.shape[0]
    ext = jnp.concatenate([cur, nxt], axis=0)
    return pltpu.roll(ext, t + h - k, 0)[:t]


def _row_sum8(x):
    t, c = x.shape
    return jnp.sum(x.reshape(t // 8, 8, c), axis=0)


_DEP = pl.BlockSpec((8, LANES), lambda i: (0, 0))


def _rms_fwd(x, g, name, dep=None):
    s, d = x.shape
    t = _tile(s, 512)

    def body(x_ref, g_ref, *rest):
        o_ref = rest[-1]
        xv = x_ref[...]
        r = lax.rsqrt(jnp.mean(xv * xv, axis=-1, keepdims=True) + RMS_EPS)
        o_ref[...] = (xv * r * g_ref[...]).astype(o_ref.dtype)

    deps = [] if dep is None else list(dep) if isinstance(dep, (list, tuple)) else [dep]
    return pl.pallas_call(
        body, name=name, grid=(s // t,),
        in_specs=[pl.BlockSpec((t, d), lambda i: (i, 0)), pl.BlockSpec((1, d), lambda i: (0, 0))] + [_DEP] * len(deps),
        out_specs=pl.BlockSpec((t, d), lambda i: (i, 0)),
        out_shape=jax.ShapeDtypeStruct((s, d), BF16),
        compiler_params=_params(("parallel",)),
    )(x, g, *deps)


def _resid_post(x, y, g, name):
    s, d = x.shape
    t = _tile(s, 512)

    def body(x_ref, y_ref, g_ref, o_ref):
        yv = y_ref[...]
        r = lax.rsqrt(jnp.mean(yv * yv, axis=-1, keepdims=True) + RMS_EPS)
        o_ref[...] = x_ref[...] + yv * r * g_ref[...]

    row = pl.BlockSpec((t, d), lambda i: (i, 0))
    return pl.pallas_call(
        body, name=name, grid=(s // t,),
        in_specs=[row, row, pl.BlockSpec((1, d), lambda i: (0, 0))],
        out_specs=row,
        out_shape=jax.ShapeDtypeStruct((s, d), F32),
        compiler_params=_params(("parallel",)),
    )(x, y, g)


def _rms_bwd(xin, g, dys, dres, out_dtype, name, dep=None):
    s, d = xin.shape
    t = _tile(s, 512)
    n = s // t
    n_dy = len(dys)
    has_res = dres is not None
    deps = [] if dep is None else [dep]

    def body(*refs):
        x_ref, g_ref = refs[0], refs[1]
        dy_refs = refs[2:2 + n_dy]
        pos = 2 + n_dy
        res_ref = refs[pos] if has_res else None
        pos += (1 if has_res else 0) + len(deps)
        dx_ref, dg_ref, acc_ref = refs[pos], refs[pos + 1], refs[pos + 2]
        i = pl.program_id(0)
        xv = x_ref[...]
        dy = dy_refs[0][...].astype(F32)
        for extra in dy_refs[1:]:
            dy = dy + extra[...].astype(F32)
        r = lax.rsqrt(jnp.mean(xv * xv, axis=-1, keepdims=True) + RMS_EPS)
        u = dy * g_ref[...]
        xr = xv * r
        dx = r * (u - xr * jnp.mean(u * xr, axis=-1, keepdims=True))
        if has_res:
            dx = dx + res_ref[...]
        dx_ref[...] = dx.astype(dx_ref.dtype)
        part = _row_sum8(dy * xr)

        @pl.when(i == 0)
        def _():
            acc_ref[...] = part

        @pl.when(i > 0)
        def _():
            acc_ref[...] += part

        @pl.when(i == n - 1)
        def _():
            dg_ref[...] = jnp.sum(acc_ref[...], axis=0, keepdims=True)

    row = pl.BlockSpec((t, d), lambda i: (i, 0))
    vec = pl.BlockSpec((1, d), lambda i: (0, 0))
    ins = [xin, g, *dys] + ([dres] if has_res else []) + deps
    return pl.pallas_call(
        body, name=name, grid=(n,),
        in_specs=[row, vec] + [row] * (n_dy + (1 if has_res else 0)) + [_DEP] * len(deps),
        out_specs=[row, vec],
        out_shape=[jax.ShapeDtypeStruct((s, d), out_dtype), jax.ShapeDtypeStruct((1, d), F32)],
        scratch_shapes=[pltpu.VMEM((8, d), F32)],
        compiler_params=_params(("arbitrary",)),
    )(*ins)


def _loss_head(y, target, name):
    s, d = y.shape
    t = _tile(s, 512)
    n = s // t

    def body(y_ref, t_ref, dy_ref, loss_ref, acc_ref):
        i = pl.program_id(0)
        e = y_ref[...] - t_ref[...]
        dy_ref[...] = e * (1.0 / d)
        part = _row_sum8(e * e)

        @pl.when(i == 0)
        def _():
            acc_ref[...] = part

        @pl.when(i > 0)
        def _():
            acc_ref[...] += part

        @pl.when(i == n - 1)
        def _():
            tot = jnp.sum(jnp.sum(acc_ref[...], axis=0, keepdims=True), axis=1, keepdims=True)
            loss_ref[...] = tot * (0.5 / d)

    row = pl.BlockSpec((t, d), lambda i: (i, 0))
    return pl.pallas_call(
        body, name=name, grid=(n,),
        in_specs=[row, row],
        out_specs=[row, pl.BlockSpec((1, 1), lambda i: (0, 0))],
        out_shape=[jax.ShapeDtypeStruct((s, d), F32), jax.ShapeDtypeStruct((1, 1), F32)],
        scratch_shapes=[pltpu.VMEM((8, d), F32)],
        compiler_params=_params(("arbitrary",)),
    )(y, target)


def _split3(x):
    hi = x.astype(BF16)
    r1 = x - hi.astype(F32)
    mid = r1.astype(BF16)
    lo = (r1 - mid.astype(F32)).astype(BF16)
    return hi, mid, lo


def _tri_dot(x, tri):
    hi, mid, lo = _split3(x)
    dn = _DIMS["nn"]
    out = lax.dot_general(hi, tri, dn, preferred_element_type=F32)
    out = out + lax.dot_general(mid, tri, dn, preferred_element_type=F32)
    return out + lax.dot_general(lo, tri, dn, preferred_element_type=F32)


def _log_sigmoid(z):
    return jnp.minimum(z, 0.0) - jnp.log(1.0 + jnp.exp(-jnp.abs(z)))


def _gate_fwd(f_row, b_col, name):
    rows, s = f_row.shape
    t = _tile(s, 512)
    n = s // t

    def body(f_ref, b_ref, ck_ref, carry_ref):
        i = pl.program_id(0)

        @pl.when(i == 0)
        def _():
            carry_ref[...] = jnp.zeros_like(carry_ref)

        logf = _log_sigmoid(f_ref[...] + b_ref[...])
        r = lax.broadcasted_iota(jnp.int32, (t, t), 0)
        c = lax.broadcasted_iota(jnp.int32, (t, t), 1)
        tri = jnp.where(r <= c, 1.0, 0.0).astype(BF16)
        cs = _tri_dot(logf, tri) + carry_ref[...]
        carry_ref[...] = cs[:, t - 1:t]
        terms = [part.astype(F32) for part in _split3(-cs)]
        sub = lax.broadcasted_iota(jnp.int32, (LANES, t), 0)
        for p in range(N_HEADS // 2):
            stacked = jnp.zeros((LANES, t), F32)
            for hh in range(2):
                for j, term in enumerate(terms):
                    h = 2 * p + hh
                    stacked = jnp.where(sub == 3 * hh + j, jnp.broadcast_to(term[h:h + 1, :], (LANES, t)), stacked)
            ck_ref[p] = stacked.T.astype(ck_ref.dtype)

    return pl.pallas_call(
        body, name=name, grid=(n,),
        in_specs=[pl.BlockSpec((rows, t), lambda i: (0, i)), pl.BlockSpec((rows, 1), lambda i: (0, 0))],
        out_specs=pl.BlockSpec((N_HEADS // 2, t, LANES), lambda i: (0, i, 0)),
        out_shape=jax.ShapeDtypeStruct((N_HEADS // 2, s, LANES), BF16),
        scratch_shapes=[pltpu.VMEM((rows, 1), F32)],
        compiler_params=_params(("arbitrary",)),
    )(f_row, b_col)


def _gate_bwd(f_row, b_col, dc_even, dc_odd, name):
    rows, s = f_row.shape
    t = _tile(s, 512)
    n = s // t

    def body(f_ref, b_ref, dce_ref, dco_ref, df_ref, db_ref, carry_ref, acc_ref):
        i = pl.program_id(0)

        @pl.when(i == 0)
        def _():
            carry_ref[...] = jnp.zeros_like(carry_ref)
            acc_ref[...] = jnp.zeros_like(acc_ref)

        head = lax.broadcasted_iota(jnp.int32, (rows, t), 0)
        dcv = jnp.zeros((rows, t), F32)
        for h in range(N_HEADS):
            src = dce_ref if h % 2 == 0 else dco_ref
            dcv = jnp.where(head == h, jnp.broadcast_to(src[h // 2, 0:1, :], (rows, t)), dcv)
        r = lax.broadcasted_iota(jnp.int32, (t, t), 0)
        c = lax.broadcasted_iota(jnp.int32, (t, t), 1)
        tri = jnp.where(r >= c, 1.0, 0.0).astype(BF16)
        dlogf = _tri_dot(dcv, tri) + carry_ref[...]
        carry_ref[...] = dlogf[:, 0:1]
        z = f_ref[...] + b_ref[...]
        df = dlogf * _sigmoid(-z)
        df_ref[...] = df.astype(df_ref.dtype)
        acc_ref[...] += jnp.sum(df, axis=1, keepdims=True)

        @pl.when(i == n - 1)
        def _():
            db_ref[...] = acc_ref[...]

    rev = lambda i: (0, n - 1 - i)
    dc_spec = pl.BlockSpec((N_HEADS // 2, 8, t), lambda i: (0, 0, n - 1 - i))
    return pl.pallas_call(
        body, name=name, grid=(n,),
        in_specs=[pl.BlockSpec((rows, t), rev), pl.BlockSpec((rows, 1), lambda i: (0, 0)), dc_spec, dc_spec],
        out_specs=[pl.BlockSpec((rows, t), rev), pl.BlockSpec((rows, 1), lambda i: (0, 0))],
        out_shape=[jax.ShapeDtypeStruct((rows, s), BF16), jax.ShapeDtypeStruct((rows, 1), F32)],
        scratch_shapes=[pltpu.VMEM((rows, 1), F32), pltpu.VMEM((rows, 1), F32)],
        compiler_params=_params(("arbitrary",)),
    )(f_row, b_col, dc_even, dc_odd)


_NEG = -1e30
_SCALE = HEAD_DIM ** -0.5


def _head_masks():
    lane = lax.broadcasted_iota(jnp.int32, (1, LANES), 1)
    return [lane < HEAD_DIM, lane >= HEAD_DIM]


def _attn_fwd(h, ck, name):
    s = h.shape[0]
    t = _tile(s, 512)
    n = s // t
    qb, kb, vb = OFF_Q // LANES, OFF_K // LANES, OFF_V // LANES

    pairs = [(qi, ki) for qi in range(n) for ki in range(qi + 1)]
    qi_tab = jnp.asarray([qi for qi, _ in pairs], jnp.int32)
    ki_tab = jnp.asarray([ki for _, ki in pairs], jnp.int32)

    def body(qi_ref, ki_ref, q_ref, k_ref, v_ref, ck_ref, o_ref, of_ref, lse_ref, m_ref, l_ref, acc_ref):
        qi, ki = qi_ref[pl.program_id(1)], ki_ref[pl.program_id(1)]
        masks = _head_masks()
        lane = lax.broadcasted_iota(jnp.int32, (1, LANES), 1)

        @pl.when(ki == 0)
        def _():
            m_ref[...] = jnp.full_like(m_ref, _NEG)
            l_ref[...] = jnp.zeros_like(l_ref)
            acc_ref[...] = jnp.zeros_like(acc_ref)

        def step(diag):
            q = q_ref[...] * _SCALE
            k_aug = jnp.concatenate([k_ref[...], ck_ref[0]], axis=1)
            v = v_ref[...]
            nq = max(1, t // 256)
            wq = t // nq
            chains = [(hh, j) for hh in range(2) for j in range(nq)]
            scores = []
            for hh, j in chains:
                qs = q[j * wq:(j + 1) * wq]
                ones = jnp.where((lane >= 3 * hh) & (lane < 3 * hh + 3), 1.0, 0.0).astype(q.dtype)
                q_aug = jnp.concatenate([jnp.where(masks[hh], qs, jnp.zeros_like(qs)),
                                         jnp.broadcast_to(ones, qs.shape)], axis=1)
                scores.append(lax.dot_general(k_aug, q_aug, _DIMS["nt"], preferred_element_type=F32))
            probs = []
            for (hh, j), sc in zip(chains, scores):
                cols = slice(j * wq, (j + 1) * wq)
                if diag:
                    r = lax.broadcasted_iota(jnp.int32, (t, wq), 0)
                    cc = lax.broadcasted_iota(jnp.int32, (t, wq), 1) + j * wq
                    sc = jnp.where(r <= cc, sc, _NEG)
                m_prev = m_ref[hh, :, cols]
                m_new = jnp.maximum(m_prev, jnp.max(sc, axis=0, keepdims=True))
                alpha = jnp.exp(m_prev - m_new)
                p = jnp.exp(sc - m_new)
                l_ref[hh, :, cols] = alpha * l_ref[hh, :, cols] + jnp.sum(p, axis=0, keepdims=True)
                m_ref[hh, :, cols] = m_new
                p_hi = p.astype(MXU_DTYPE)
                p_lo = (p - p_hi.astype(F32)).astype(MXU_DTYPE)
                probs.append((alpha, p_hi, p_lo))
            for (hh, j), (alpha, p_hi, p_lo) in zip(chains, probs):
                pv = (lax.dot_general(v, p_hi, _DIMS["tn"], preferred_element_type=F32)
                      + lax.dot_general(v, p_lo, _DIMS["tn"], preferred_element_type=F32))
                rows = slice(hh * HEAD_DIM, (hh + 1) * HEAD_DIM)
                cols = slice(j * wq, (j + 1) * wq)
                acc_ref[rows, cols] = alpha * acc_ref[rows, cols] + pv[rows]

        @pl.when(ki < qi)
        def _():
            step(False)

        @pl.when(ki == qi)
        def _():
            step(True)
            inv = jnp.concatenate([jnp.broadcast_to(1.0 / l_ref[hh], (HEAD_DIM, t)) for hh in range(2)], axis=0)
            out = (acc_ref[...] * inv).T
            o_ref[...] = out.astype(o_ref.dtype)
            of_ref[...] = out
            lse = jnp.concatenate([jnp.broadcast_to(m_ref[hh] + jnp.log(l_ref[hh]), (HEAD_DIM, t))
                                   for hh in range(2)], axis=0)
            lse_ref[...] = lse.T

    grid_spec = pltpu.PrefetchScalarGridSpec(
        num_scalar_prefetch=2, grid=(N_HEADS // 2, len(pairs)),
        in_specs=[
            pl.BlockSpec((t, LANES), lambda p, i, qt, kt: (qt[i], qb + p)),
            pl.BlockSpec((t, LANES), lambda p, i, qt, kt: (kt[i], kb + p)),
            pl.BlockSpec((t, LANES), lambda p, i, qt, kt: (kt[i], vb + p)),
            pl.BlockSpec((1, t, LANES), lambda p, i, qt, kt: (p, kt[i], 0)),
        ],
        out_specs=[pl.BlockSpec((t, LANES), lambda p, i, qt, kt: (qt[i], p))] * 3,
        scratch_shapes=[pltpu.VMEM((2, 1, t), F32), pltpu.VMEM((2, 1, t), F32), pltpu.VMEM((LANES, t), F32)])
    return pl.pallas_call(
        body, name=name, grid_spec=grid_spec,
        out_shape=[jax.ShapeDtypeStruct((s, D_ATT), BF16), jax.ShapeDtypeStruct((s, D_ATT), F32),
                   jax.ShapeDtypeStruct((s, D_ATT), F32)],
        compiler_params=_params(("parallel", "arbitrary")),
    )(qi_tab, ki_tab, h, h, h, ck)


def _attn_bwd(h, ck, o, lse, do, name):
    s = h.shape[0]
    t = _tile(s, 512)
    n = s // t
    qb, kb, vb = OFF_Q // LANES, OFF_K // LANES, OFF_V // LANES

    pairs = [(ki, qi) for ki in range(n) for qi in range(ki, n)]
    ki_tab = jnp.asarray([ki for ki, _ in pairs], jnp.int32)
    qi_tab = jnp.asarray([qi for _, qi in pairs], jnp.int32)

    def body(ki_ref, qi_ref, q_ref, k_ref, v_ref, ck_ref, o_ref, lse_ref, do_ref,
             dq_ref, dk_ref, dv_ref, dc0_ref, dc1_ref, dk_acc, dv_acc, dc_acc):
        ki, qi = ki_ref[pl.program_id(1)], qi_ref[pl.program_id(1)]
        masks = _head_masks()
        lane = lax.broadcasted_iota(jnp.int32, (1, LANES), 1)

        @pl.when((ki == 0) & (qi == 0))
        def _():
            dq_ref[...] = jnp.zeros_like(dq_ref)

        @pl.when(qi == ki)
        def _():
            dk_acc[...] = jnp.zeros_like(dk_acc)
            dv_acc[...] = jnp.zeros_like(dv_acc)
            dc_acc[...] = jnp.zeros_like(dc_acc)

        def step(diag):
            q = q_ref[...] * _SCALE
            k = k_ref[...]
            v = v_ref[...]
            dov = do_ref[...]
            k_aug = jnp.concatenate([k, ck_ref[0]], axis=1)
            prod_t = (dov.astype(F32) * o_ref[...]).T
            lse_t = lse_ref[...].T
            heads = []
            for hh in range(2):
                mk = masks[hh]
                qh = jnp.where(mk, q, jnp.zeros_like(q))
                kh = jnp.where(mk, k, jnp.zeros_like(k))
                doh = jnp.where(mk, dov, jnp.zeros_like(dov))
                ones = jnp.where((lane >= 3 * hh) & (lane < 3 * hh + 3), 1.0, 0.0).astype(q.dtype)
                q_aug = jnp.concatenate([qh, jnp.broadcast_to(ones, q.shape)], axis=1)
                sc = lax.dot_general(k_aug, q_aug, _DIMS["nt"], preferred_element_type=F32)
                dp = lax.dot_general(v, doh, _DIMS["nt"], preferred_element_type=F32)
                heads.append((qh, kh, doh, sc, dp))
            grads = []
            for hh, (qh, kh, doh, sc, dp) in enumerate(heads):
                rows = slice(hh * HEAD_DIM, (hh + 1) * HEAD_DIM)
                p = jnp.exp(sc - lse_t[hh * HEAD_DIM:hh * HEAD_DIM + 1, :])
                if diag:
                    r = lax.broadcasted_iota(jnp.int32, (t, t), 0)
                    cc = lax.broadcasted_iota(jnp.int32, (t, t), 1)
                    p = jnp.where(r <= cc, p, 0.0)
                delta = jnp.sum(prod_t[rows], axis=0, keepdims=True)
                ds = p * (dp - delta)
                dc_acc[hh] = dc_acc[hh] - jnp.sum(ds, axis=1, keepdims=True)
                grads.append((ds.astype(MXU_DTYPE), p.astype(MXU_DTYPE)))
            dq_blk = jnp.zeros((t, LANES), F32)
            for (qh, kh, doh, _, _), (dsb, pb) in zip(heads, grads):
                dv_acc[...] += lax.dot_general(pb, doh, _DIMS["nn"], preferred_element_type=F32)
                dk_acc[...] += lax.dot_general(dsb, qh, _DIMS["nn"], preferred_element_type=F32)
                dq_blk = dq_blk + lax.dot_general(dsb, kh, _DIMS["tn"], preferred_element_type=F32)
            rows_q = pl.ds(pl.multiple_of(qi * t, t), t)
            dq_ref[rows_q, :] = dq_ref[rows_q, :] + dq_blk * _SCALE

        @pl.when(qi > ki)
        def _():
            step(False)

        @pl.when(qi == ki)
        def _():
            step(True)

        @pl.when(qi == n - 1)
        def _():
            dk_ref[...] = dk_acc[...].astype(dk_ref.dtype)
            dv_ref[...] = dv_acc[...].astype(dv_ref.dtype)
            dc0_ref[0] = jnp.broadcast_to(dc_acc[0], (t, LANES)).T[0:8]
            dc1_ref[0] = jnp.broadcast_to(dc_acc[1], (t, LANES)).T[0:8]

    q_blk = lambda col: pl.BlockSpec((t, LANES), lambda p, i, kt, qt: (qt[i], col(p)))
    k_blk = lambda col: pl.BlockSpec((t, LANES), lambda p, i, kt, qt: (kt[i], col(p)))
    dc_blk = pl.BlockSpec((1, 8, t), lambda p, i, kt, qt: (p, 0, kt[i]))
    grid_spec = pltpu.PrefetchScalarGridSpec(
        num_scalar_prefetch=2, grid=(N_HEADS // 2, len(pairs)),
        in_specs=[q_blk(lambda p: qb + p), k_blk(lambda p: kb + p), k_blk(lambda p: vb + p),
                  pl.BlockSpec((1, t, LANES), lambda p, i, kt, qt: (p, kt[i], 0)),
                  q_blk(lambda p: p), q_blk(lambda p: p), q_blk(lambda p: p)],
        out_specs=[pl.BlockSpec((s, LANES), lambda p, i, kt, qt: (0, p)), k_blk(lambda p: p), k_blk(lambda p: p),
                   dc_blk, dc_blk],
        scratch_shapes=[pltpu.VMEM((t, LANES), F32), pltpu.VMEM((t, LANES), F32), pltpu.VMEM((2, t, 1), F32)])
    return pl.pallas_call(
        body, name=name, grid_spec=grid_spec,
        out_shape=[jax.ShapeDtypeStruct((s, D_ATT), F32), jax.ShapeDtypeStruct((s, D_ATT), BF16),
                   jax.ShapeDtypeStruct((s, D_ATT), BF16), jax.ShapeDtypeStruct((N_HEADS // 2, 8, s), F32),
                   jax.ShapeDtypeStruct((N_HEADS // 2, 8, s), F32)],
        compiler_params=_params(("parallel", "arbitrary")),
    )(ki_tab, qi_tab, h, h, h, ck, o, lse, do)


def _conv3(z, z_prev, w_ref):
    return (w_ref[2:3, :] * z + w_ref[1:2, :] * _shift_down(z, z_prev, 1)
            + w_ref[0:1, :] * _shift_down(z, z_prev, 2))


def _sconv_fwd(h, w, name):
    s = h.shape[0]
    t = _tile(s, 512)
    r = t // HALO
    c = D_CONV
    b_bg, b_cg, b_hc = OFF_BG // c, OFF_CG // c, OFF_HC // c

    def body(bg_ref, cg_ref, hc_ref, cgp_ref, hcp_ref, w_ref, y_ref):
        i = pl.program_id(0)
        live = (i > 0).astype(F32)
        z = cg_ref[...].astype(F32) * hc_ref[...].astype(F32)
        zp = cgp_ref[...].astype(F32) * hcp_ref[...].astype(F32) * live
        y_ref[...] = (bg_ref[...].astype(F32) * _conv3(z, zp, w_ref)).astype(y_ref.dtype)

    cur = lambda b: pl.BlockSpec((t, c), lambda i: (i, b))
    prev = lambda b: pl.BlockSpec((HALO, c), lambda i: (jnp.maximum(i * r - 1, 0), b))
    return pl.pallas_call(
        body, name=name, grid=(s // t,),
        in_specs=[cur(b_bg), cur(b_cg), cur(b_hc), prev(b_cg), prev(b_hc), pl.BlockSpec((8, c), lambda i: (0, 0))],
        out_specs=pl.BlockSpec((t, c), lambda i: (i, 0)),
        out_shape=jax.ShapeDtypeStruct((s, c), BF16),
        compiler_params=_params(("parallel",)),
    )(h, h, h, h, h, w)


def _sconv_bwd(h, w, dy, name):
    s = h.shape[0]
    t = _tile(s, 512)
    n = s // t
    r = t // HALO
    nh = s // HALO
    c = D_CONV
    b_bg, b_cg, b_hc = OFF_BG // c, OFF_CG // c, OFF_HC // c

    def body(bg_ref, cg_ref, hc_ref, cgp_ref, hcp_ref, bgn_ref, dy_ref, dyn_ref, w_ref, d_ref, dw_ref, acc_ref):
        i = pl.program_id(0)
        has_prev = (i > 0).astype(F32)
        has_next = (i < n - 1).astype(F32)
        bg = bg_ref[...].astype(F32)
        cg = cg_ref[...].astype(F32)
        hc = hc_ref[...].astype(F32)
        dyv = dy_ref[...].astype(F32)
        z = cg * hc
        zp = cgp_ref[...].astype(F32) * hcp_ref[...].astype(F32) * has_prev
        z1 = _shift_down(z, zp, 1)
        z2 = _shift_down(z, zp, 2)
        cz = w_ref[2:3, :] * z + w_ref[1:2, :] * z1 + w_ref[0:1, :] * z2
        dcz = dyv * bg
        dczn = dyn_ref[...].astype(F32) * bgn_ref[...].astype(F32) * has_next
        dz = (w_ref[2:3, :] * dcz + w_ref[1:2, :] * _shift_up(dcz, dczn, 1)
              + w_ref[0:1, :] * _shift_up(dcz, dczn, 2))
        d_ref[:, 0:c] = (dyv * cz).astype(d_ref.dtype)
        d_ref[:, c:2 * c] = (dz * hc).astype(d_ref.dtype)
        d_ref[:, 2 * c:3 * c] = (dz * cg).astype(d_ref.dtype)

        @pl.when(i == 0)
        def _():
            acc_ref[...] = jnp.zeros_like(acc_ref)

        acc_ref[0] += _row_sum8(dcz * z2)
        acc_ref[1] += _row_sum8(dcz * z1)
        acc_ref[2] += _row_sum8(dcz * z)

        @pl.when(i == n - 1)
        def _():
            rows = [jnp.sum(acc_ref[k], axis=0, keepdims=True) for k in range(3)]
            dw_ref[...] = jnp.concatenate(rows + [jnp.zeros((5, c), F32)], axis=0)

    cur = lambda b: pl.BlockSpec((t, c), lambda i: (i, b))
    prev = lambda b: pl.BlockSpec((HALO, c), lambda i: (jnp.maximum(i * r - 1, 0), b))
    nxt = lambda b: pl.BlockSpec((HALO, c), lambda i: (jnp.minimum((i + 1) * r, nh - 1), b))
    return pl.pallas_call(
        body, name=name, grid=(n,),
        in_specs=[cur(b_bg), cur(b_cg), cur(b_hc), prev(b_cg), prev(b_hc), nxt(b_bg),
                  cur(0), nxt(0), pl.BlockSpec((8, c), lambda i: (0, 0))],
        out_specs=[pl.BlockSpec((t, 3 * c), lambda i: (i, 0)), pl.BlockSpec((8, c), lambda i: (0, 0))],
        out_shape=[jax.ShapeDtypeStruct((s, 3 * c), BF16), jax.ShapeDtypeStruct((8, c), F32)],
        scratch_shapes=[pltpu.VMEM((3, 8, c), F32)],
        compiler_params=_params(("arbitrary",)),
    )(h, h, h, h, h, h, dy, dy, w)


def _group_masks():
    lane = lax.broadcasted_iota(jnp.int32, (1, D_SGU), 1)
    return [(lane >= g * HEAD_DIM) & (lane < (g + 1) * HEAD_DIM) for g in range(N_GROUPS)]


def _tril_weights(w_ref):
    r = lax.broadcasted_iota(jnp.int32, (CHUNK, CHUNK), 0)
    c = lax.broadcasted_iota(jnp.int32, (CHUNK, CHUNK), 1)
    return [jnp.where(r >= c, w_ref[g], 0.0).astype(MXU_DTYPE) for g in range(N_GROUPS)]


def _sgu_ln(vs, g_ref, b_ref):
    vg, dvg = _gelu_and_grad(vs)
    mu = jnp.mean(vg, axis=-1, keepdims=True)
    xc = vg - mu
    rstd = lax.rsqrt(jnp.mean(xc * xc, axis=-1, keepdims=True) + LN_EPS)
    xhat = xc * rstd
    return xhat * g_ref[...] + b_ref[...], xhat, rstd, dvg


def _sgu_fwd(h, ln_g, ln_b, w_s, bias, name):
    s = h.shape[0]
    t = _tile(s, 512)
    c = D_SGU
    b_u, b_v = OFF_U // c, OFF_VS // c

    def body(u_ref, v_ref, g_ref, b_ref, w_ref, bias_ref, y_ref):
        gm = _group_masks()
        wm = _tril_weights(w_ref)
        ug = _gelu(u_ref[...].astype(F32))
        vn, _, _, _ = _sgu_ln(v_ref[...].astype(F32), g_ref, b_ref)
        vnb = vn.astype(MXU_DTYPE)
        for ch in range(t // CHUNK):
            rows = slice(ch * CHUNK, (ch + 1) * CHUNK)
            mixed = bias_ref[...]
            for g in range(N_GROUPS):
                mg = lax.dot_general(wm[g], vnb[rows], _DIMS["nn"], preferred_element_type=F32)
                mixed = jnp.where(gm[g], mixed + mg, mixed)
            y_ref[rows, :] = (ug[rows] * mixed).astype(y_ref.dtype)

    full = lambda shp: pl.BlockSpec(shp, lambda i: (0,) * len(shp))
    return pl.pallas_call(
        body, name=name, grid=(s // t,),
        in_specs=[pl.BlockSpec((t, c), lambda i: (i, b_u)), pl.BlockSpec((t, c), lambda i: (i, b_v)),
                  full((1, c)), full((1, c)), full((N_GROUPS, CHUNK, CHUNK)), full((CHUNK, c))],
        out_specs=pl.BlockSpec((t, c), lambda i: (i, 0)),
        out_shape=jax.ShapeDtypeStruct((s, c), BF16),
        compiler_params=_params(("parallel",)),
    )(h, h, ln_g, ln_b, w_s, bias)


def _sgu_bwd(h, ln_g, ln_b, w_s, bias, dy, name):
    s = h.shape[0]
    t = _tile(s, 512)
    n = s // t
    c = D_SGU
    b_u, b_v = OFF_U // c, OFF_VS // c

    def body(u_ref, v_ref, g_ref, b_ref, w_ref, bias_ref, dy_ref,
             d_ref, dg_ref, db_ref, dw_ref, dbias_ref, dg_acc, db_acc):
        i = pl.program_id(0)
        gm = _group_masks()
        wm = _tril_weights(w_ref)

        @pl.when(i == 0)
        def _():
            dg_acc[...] = jnp.zeros_like(dg_acc)
            db_acc[...] = jnp.zeros_like(db_acc)
            dw_ref[...] = jnp.zeros_like(dw_ref)
            dbias_ref[...] = jnp.zeros_like(dbias_ref)

        ug, dug = _gelu_and_grad(u_ref[...].astype(F32))
        vn, xhat, rstd, dvg = _sgu_ln(v_ref[...].astype(F32), g_ref, b_ref)
        vnb = vn.astype(MXU_DTYPE)
        dyv = dy_ref[...].astype(F32)
        dmixed = dyv * ug
        dmb = dmixed.astype(MXU_DTYPE)
        dvn_parts = []
        for ch in range(t // CHUNK):
            rows = slice(ch * CHUNK, (ch + 1) * CHUNK)
            mixed = bias_ref[...]
            dvn = jnp.zeros((CHUNK, c), F32)
            for g in range(N_GROUPS):
                mg = lax.dot_general(wm[g], vnb[rows], _DIMS["nn"], preferred_element_type=F32)
                mixed = jnp.where(gm[g], mixed + mg, mixed)
                dvn = jnp.where(gm[g], lax.dot_general(wm[g], dmb[rows], _DIMS["tn"], preferred_element_type=F32),
                                dvn)
                dmg = jnp.where(gm[g], dmb[rows], jnp.zeros_like(dmb[rows]))
                dw_ref[g] += lax.dot_general(dmg, vnb[rows], _DIMS["nt"], preferred_element_type=F32)
            d_ref[rows, 0:c] = (dyv[rows] * mixed * dug[rows]).astype(d_ref.dtype)
            dbias_ref[...] += dmixed[rows]
            dvn_parts.append(dvn)
        dvn = jnp.concatenate(dvn_parts, axis=0)
        dg_acc[...] += _row_sum8(dvn * xhat)
        db_acc[...] += _row_sum8(dvn)
        dxh = dvn * g_ref[...]
        dvgl = rstd * (dxh - jnp.mean(dxh, axis=-1, keepdims=True)
                       - xhat * jnp.mean(dxh * xhat, axis=-1, keepdims=True))
        d_ref[:, c:2 * c] = (dvgl * dvg).astype(d_ref.dtype)

        @pl.when(i == n - 1)
        def _():
            dg_ref[...] = jnp.sum(dg_acc[...], axis=0, keepdims=True)
            db_ref[...] = jnp.sum(db_acc[...], axis=0, keepdims=True)
            r = lax.broadcasted_iota(jnp.int32, (CHUNK, CHUNK), 0)
            cc = lax.broadcasted_iota(jnp.int32, (CHUNK, CHUNK), 1)
            for g in range(N_GROUPS):
                dw_ref[g] = jnp.where(r >= cc, dw_ref[g], 0.0)

    full = lambda shp: pl.BlockSpec(shp, lambda i: (0,) * len(shp))
    return pl.pallas_call(
        body, name=name, grid=(n,),
        in_specs=[pl.BlockSpec((t, c), lambda i: (i, b_u)), pl.BlockSpec((t, c), lambda i: (i, b_v)),
                  full((1, c)), full((1, c)), full((N_GROUPS, CHUNK, CHUNK)), full((CHUNK, c)),
                  pl.BlockSpec((t, c), lambda i: (i, 0))],
        out_specs=[pl.BlockSpec((t, 2 * c), lambda i: (i, 0)), full((1, c)), full((1, c)),
                   full((N_GROUPS, CHUNK, CHUNK)), full((CHUNK, c))],
        out_shape=[jax.ShapeDtypeStruct((s, 2 * c), BF16), jax.ShapeDtypeStruct((1, c), F32),
                   jax.ShapeDtypeStruct((1, c), F32), jax.ShapeDtypeStruct((N_GROUPS, CHUNK, CHUNK), F32),
                   jax.ShapeDtypeStruct((CHUNK, c), F32)],
        scratch_shapes=[pltpu.VMEM((8, c), F32), pltpu.VMEM((8, c), F32)],
        compiler_params=_params(("arbitrary",)),
    )(h, h, ln_g, ln_b, w_s, bias, dy)


def _merge_fwd(h, acts, ws, b_gate, name):
    s = h.shape[0]
    d = D_MODEL
    t = _tile(s, 512)

    def body(gl0, gl1, gl2, a0, a1, a2, w0, w1, w2, b_ref, o_ref):
        acc = jnp.zeros((t, d), F32)
        for i, (gl, a, w) in enumerate(((gl0, a0, w0), (gl1, a1, w1), (gl2, a2, w2))):
            y = lax.dot_general(a[...], w[...], _DIMS["nn"], preferred_element_type=F32)
            acc = acc + _sigmoid(gl[...].astype(F32) + b_ref[i:i + 1, :]) * y
        o_ref[...] = acc.astype(o_ref.dtype)

    full = lambda arr: pl.BlockSpec(arr.shape, lambda i: (0, 0))
    return pl.pallas_call(
        body, name=name, grid=(s // t,),
        in_specs=[pl.BlockSpec((t, d), lambda i, b=b: (i, b)) for b in range(3)]
                 + [pl.BlockSpec((t, a.shape[1]), lambda i: (i, 0)) for a in acts]
                 + [full(w) for w in ws] + [full(b_gate)],
        out_specs=pl.BlockSpec((t, d), lambda i: (i, 0)),
        out_shape=jax.ShapeDtypeStruct((s, d), BF16),
        compiler_params=_params(("parallel",)),
    )(h, h, h, *acts, *ws, b_gate)


def _merge_bwd(h, acts, ws, b_gate, dmerged, name):
    s = h.shape[0]
    d = D_MODEL
    t = _tile(s, 512)
    n = s // t

    def body(gl0, gl1, gl2, a0, a1, a2, w0, w1, w2, b_ref, dm_ref, dy0, dy1, dy2, dgl_ref, db_ref, acc_ref):
        step = pl.program_id(0)

        @pl.when(step == 0)
        def _():
            acc_ref[...] = jnp.zeros_like(acc_ref)

        dm = dm_ref[...]
        for i, (gl, a, w, dy) in enumerate(((gl0, a0, w0, dy0), (gl1, a1, w1, dy1), (gl2, a2, w2, dy2))):
            y = lax.dot_general(a[...], w[...], _DIMS["nn"], preferred_element_type=F32)
            gate = _sigmoid(gl[...].astype(F32) + b_ref[i:i + 1, :])
            dy[...] = (dm * gate).astype(dy.dtype)
            dgl = dm * y * (gate * (1.0 - gate))
            dgl_ref[:, i * d:(i + 1) * d] = dgl.astype(dgl_ref.dtype)
            acc_ref[i] += _row_sum8(dgl)

        @pl.when(step == n - 1)
        def _():
            rows = [jnp.sum(acc_ref[k], axis=0, keepdims=True) for k in range(3)]
            db_ref[...] = jnp.concatenate(rows + [jnp.zeros((5, d), F32)], axis=0)

    full = lambda arr: pl.BlockSpec(arr.shape, lambda i: (0, 0))
    row = pl.BlockSpec((t, d), lambda i: (i, 0))
    return pl.pallas_call(
        body, name=name, grid=(n,),
        in_specs=[pl.BlockSpec((t, d), lambda i, b=b: (i, b)) for b in range(3)]
                 + [pl.BlockSpec((t, a.shape[1]), lambda i: (i, 0)) for a in acts]
                 + [full(w) for w in ws] + [full(b_gate), row],
        out_specs=[row, row, row, pl.BlockSpec((t, 3 * d), lambda i: (i, 0)), pl.BlockSpec((8, d), lambda i: (0, 0))],
        out_shape=[jax.ShapeDtypeStruct((s, d), BF16)] * 3
                  + [jax.ShapeDtypeStruct((s, 3 * d), BF16), jax.ShapeDtypeStruct((8, d), F32)],
        scratch_shapes=[pltpu.VMEM((3, 8, d), F32)],
        compiler_params=_params(("arbitrary",)),
    )(h, h, h, *acts, *ws, b_gate, dmerged)


FF_BLK = D_FF // 2


def _ffn_act_fwd(h2, w, name):
    s = h2.shape[0]
    t = _tile(s, 512)
    r = t // HALO
    cw = 2 * FF_BLK

    def body(x_ref, xp_ref, w_ref, p_ref):
        i = pl.program_id(0)
        live = (i > 0).astype(F32)
        hc = _conv3(x_ref[...].astype(F32), xp_ref[...].astype(F32) * live, w_ref)
        p_ref[...] = (_gelu(hc[:, :FF_BLK]) * hc[:, FF_BLK:]).astype(p_ref.dtype)

    return pl.pallas_call(
        body, name=name, grid=(s // t, 2),
        in_specs=[pl.BlockSpec((t, cw), lambda i, j: (i, j)),
                  pl.BlockSpec((HALO, cw), lambda i, j: (jnp.maximum(i * r - 1, 0), j)),
                  pl.BlockSpec((8, cw), lambda i, j: (0, j))],
        out_specs=pl.BlockSpec((t, FF_BLK), lambda i, j: (i, j)),
        out_shape=jax.ShapeDtypeStruct((s, D_FF), BF16),
        compiler_params=_params(("parallel", "parallel")),
    )(h2, h2, w)


def _ffn_act_conv_bwd(h2, w, dp, name):
    s = h2.shape[0]
    t = _tile(s, 512)
    n = s // t
    r = t // HALO
    nh = s // HALO
    cw = 2 * FF_BLK

    def body(x_ref, xp_ref, xn_ref, dp_ref, dpn_ref, w_ref, dx_ref, dw_ref, acc_ref):
        i = pl.program_id(1)
        has_prev = (i > 0).astype(F32)
        has_next = (i < n - 1).astype(F32)
        x = jnp.concatenate([x_ref[...].astype(F32), xn_ref[...].astype(F32)], axis=0)
        xp = xp_ref[...].astype(F32) * has_prev
        x1 = _shift_down(x, xp, 1)
        x2 = _shift_down(x, xp, 2)
        hc = w_ref[2:3, :] * x + w_ref[1:2, :] * x1 + w_ref[0:1, :] * x2
        ga, dga = _gelu_and_grad(hc[:, :FF_BLK])
        dpv = jnp.concatenate([dp_ref[...].astype(F32), dpn_ref[...].astype(F32) * has_next], axis=0)
        dhc = jnp.concatenate([dpv * hc[:, FF_BLK:] * dga, dpv * ga], axis=1)
        cur, nxt = dhc[:t], dhc[t:]
        dx = w_ref[2:3, :] * cur + w_ref[1:2, :] * _shift_up(cur, nxt, 1) + w_ref[0:1, :] * _shift_up(cur, nxt, 2)
        dx_ref[...] = dx.astype(dx_ref.dtype)

        @pl.when(i == 0)
        def _():
            acc_ref[...] = jnp.zeros_like(acc_ref)

        acc_ref[0] += _row_sum8(cur * x2[:t])
        acc_ref[1] += _row_sum8(cur * x1[:t])
        acc_ref[2] += _row_sum8(cur * x[:t])

        @pl.when(i == n - 1)
        def _():
            rows = [jnp.sum(acc_ref[k], axis=0, keepdims=True) for k in range(3)]
            dw_ref[...] = jnp.concatenate(rows + [jnp.zeros((5, cw), F32)], axis=0)

    nxt_row = lambda j, i: jnp.minimum((i + 1) * r, nh - 1)
    return pl.pallas_call(
        body, name=name, grid=(2, n),
        in_specs=[pl.BlockSpec((t, cw), lambda j, i: (i, j)),
                  pl.BlockSpec((HALO, cw), lambda j, i: (jnp.maximum(i * r - 1, 0), j)),
                  pl.BlockSpec((HALO, cw), lambda j, i: (nxt_row(j, i), j)),
                  pl.BlockSpec((t, FF_BLK), lambda j, i: (i, j)),
                  pl.BlockSpec((HALO, FF_BLK), lambda j, i: (nxt_row(j, i), j)),
                  pl.BlockSpec((8, cw), lambda j, i: (0, j))],
        out_specs=[pl.BlockSpec((t, cw), lambda j, i: (i, j)), pl.BlockSpec((8, cw), lambda j, i: (0, j))],
        out_shape=[jax.ShapeDtypeStruct((s, 2 * D_FF), BF16), jax.ShapeDtypeStruct((8, 2 * D_FF), F32)],
        scratch_shapes=[pltpu.VMEM((3, 8, cw), F32)],
        compiler_params=_params(("parallel", "arbitrary")),
    )(h2, h2, h2, dp, dp, w)


def _adamw(w, g, m, v, name):
    shape = w.shape
    c = shape[-1]
    rows = math.prod(shape[:-1])
    to2d = lambda a: a.reshape(rows, c)
    cap = max(8, (1 << 18) // c)
    tr = rows
    for cand in (2048, 1024, 512, 256, 128, 64, 32, 16, 8):
        if cand <= cap and rows % cand == 0:
            tr = cand
            break

    def body(w_ref, g_ref, m_ref, v_ref, d_ref, nm_ref, nv_ref):
        gv = g_ref[...]
        nm = ADAM_B1 * m_ref[...] + (1.0 - ADAM_B1) * gv
        nv = ADAM_B2 * v_ref[...] + (1.0 - ADAM_B2) * (gv * gv)
        m_hat = nm / (1.0 - ADAM_B1 ** ADAM_STEP)
        v_hat = nv / (1.0 - ADAM_B2 ** ADAM_STEP)
        d_ref[...] = -ADAM_LR * (m_hat / (jnp.sqrt(v_hat) + ADAM_EPS) + ADAM_WD * w_ref[...])
        nm_ref[...] = nm
        nv_ref[...] = nv

    blk = pl.BlockSpec((tr, c), lambda i: (i, 0))
    outs = pl.pallas_call(
        body, name=name, grid=(rows // tr,),
        in_specs=[blk] * 4, out_specs=[blk] * 3,
        out_shape=[jax.ShapeDtypeStruct((rows, c), F32)] * 3,
        compiler_params=_params(("parallel",)),
    )(to2d(w), to2d(g), to2d(m), to2d(v))
    return tuple(o.reshape(shape) for o in outs)


_ANY = pl.BlockSpec(memory_space=pl.ANY)


def _place():
    x, y, c = lax.axis_index("x"), lax.axis_index("y"), lax.axis_index("c")
    others = [(1 - x, y), (x, 1 - y), (1 - x, 1 - y)]
    return x, y, c, others


def _all_gather_chips(shard, name):
    rws, cols = shard.shape
    half = rws // 2

    def body(x_ref, out_ref, send_sems, recv_sems, local_sem):
        x, y, c, others = _place()
        me = 2 * x + y
        sib = (x, y, 1 - c)

        def rows(chip, cc):
            return out_ref.at[chip, pl.ds(pl.multiple_of(cc * half, 16), half), :]

        def copy(k, src, dst, to):
            return pltpu.make_async_remote_copy(src_ref=src, dst_ref=dst, send_sem=send_sems.at[k],
                                                recv_sem=recv_sems.at[k], device_id=to, device_id_type=MESH)

        mine = pltpu.make_async_copy(x_ref, out_ref.at[me], local_sem)
        mine.start()
        my_half = x_ref.at[pl.ds(pl.multiple_of(c * half, 16), half), :]
        first = [copy(j, my_half, rows(me, c), (ox, oy, c)) for j, (ox, oy) in enumerate(others)]
        for cp in first:
            cp.start()
        passed = []
        for j, (ox, oy) in enumerate(others):
            blk = rows(2 * ox + oy, c)
            copy(j, blk, blk, (x, y, c)).wait_recv()
            fwd = copy(3 + j, blk, blk, sib)
            fwd.start()
            passed.append(fwd)
        for j, (ox, oy) in enumerate(others):
            blk = rows(2 * ox + oy, 1 - c)
            copy(3 + j, blk, blk, (x, y, c)).wait_recv()
        for cp in first + passed:
            cp.wait_send()
        mine.wait()

    return pl.pallas_call(
        body, name=name,
        in_specs=[_ANY], out_specs=_ANY,
        out_shape=jax.ShapeDtypeStruct((N_CHIPS, rws, cols), shard.dtype),
        scratch_shapes=[pltpu.SemaphoreType.DMA((6,)), pltpu.SemaphoreType.DMA((6,)), pltpu.SemaphoreType.DMA],
        compiler_params=pltpu.CompilerParams(has_side_effects=True),
    )(shard)


def _swap_halves(buf, name):
    nb, rws, cols = buf.shape
    half = rws // 2

    def body(b_ref, own_ref, sib_ref, send_sem, recv_sem, local_sem):
        x, y, c, _ = _place()
        keep = b_ref.at[:, pl.ds(pl.multiple_of(c * half, 16), half), :]
        give = b_ref.at[:, pl.ds(pl.multiple_of((1 - c) * half, 16), half), :]
        mine = pltpu.make_async_copy(keep, own_ref, local_sem)
        mine.start()
        cp = pltpu.make_async_remote_copy(src_ref=give, dst_ref=sib_ref, send_sem=send_sem, recv_sem=recv_sem,
                                          device_id=(x, y, 1 - c), device_id_type=MESH)
        cp.start()
        cp.wait()
        mine.wait()

    shp = jax.ShapeDtypeStruct((nb, half, cols), buf.dtype)
    return pl.pallas_call(
        body, name=name,
        in_specs=[_ANY], out_specs=[_ANY, _ANY], out_shape=[shp, shp],
        scratch_shapes=[pltpu.SemaphoreType.DMA, pltpu.SemaphoreType.DMA, pltpu.SemaphoreType.DMA],
        compiler_params=pltpu.CompilerParams(has_side_effects=True),
    )(buf)


def _add2(a, b, name):
    nb, rws, cols = a.shape
    t = _tile(rws, 256)
    if rws % t:
        t = rws

    def body(a_ref, b_ref, o_ref):
        o_ref[...] = (a_ref[...].astype(F32) + b_ref[...].astype(F32)).astype(o_ref.dtype)

    blk = pl.BlockSpec((1, t, cols), lambda i, j: (i, j, 0))
    return pl.pallas_call(
        body, name=name, grid=(nb, rws // t), in_specs=[blk, blk], out_specs=blk,
        out_shape=jax.ShapeDtypeStruct(a.shape, a.dtype),
        compiler_params=_params(("parallel", "parallel")),
    )(a, b)


def _exchange_chips(pre, name):
    nb, half, cols = pre.shape

    def body(p_ref, out_ref, send_sems, recv_sems, local_sem):
        x, y, c, others = _place()
        me = 2 * x + y
        mine = pltpu.make_async_copy(p_ref.at[me], out_ref.at[me], local_sem)
        mine.start()
        sends = []
        for j, (ox, oy) in enumerate(others):
            cp = pltpu.make_async_remote_copy(src_ref=p_ref.at[2 * ox + oy], dst_ref=out_ref.at[me],
                                              send_sem=send_sems.at[j], recv_sem=recv_sems.at[j],
                                              device_id=(ox, oy, c), device_id_type=MESH)
            cp.start()
            sends.append(cp)
        for j, (ox, oy) in enumerate(others):
            blk = out_ref.at[2 * ox + oy]
            pltpu.make_async_remote_copy(src_ref=blk, dst_ref=blk, send_sem=send_sems.at[j],
                                         recv_sem=recv_sems.at[j], device_id=(x, y, c),
                                         device_id_type=MESH).wait_recv()
        for cp in sends:
            cp.wait_send()
        mine.wait()

    return pl.pallas_call(
        body, name=name,
        in_specs=[_ANY], out_specs=_ANY, out_shape=jax.ShapeDtypeStruct(pre.shape, pre.dtype),
        scratch_shapes=[pltpu.SemaphoreType.DMA((3,)), pltpu.SemaphoreType.DMA((3,)), pltpu.SemaphoreType.DMA],
        compiler_params=pltpu.CompilerParams(has_side_effects=True),
    )(pre)


def _add4(parts, name):
    nb, half, cols = parts.shape
    t = _tile(half, 256)
    if half % t:
        t = half

    def body(p_ref, o_ref):
        acc = p_ref[0].astype(F32)
        for k in range(1, nb):
            acc = acc + p_ref[k].astype(F32)
        o_ref[...] = acc

    return pl.pallas_call(
        body, name=name, grid=(half // t,),
        in_specs=[pl.BlockSpec((nb, t, cols), lambda i: (0, i, 0))],
        out_specs=pl.BlockSpec((t, cols), lambda i: (i, 0)),
        out_shape=jax.ShapeDtypeStruct((half, cols), F32),
        compiler_params=_params(("parallel",)),
    )(parts)


def _join_halves(mine_half, name):
    half, cols = mine_half.shape

    def body(h_ref, out_ref, send_sem, recv_sem, local_sem):
        x, y, c, _ = _place()
        dst = out_ref.at[pl.ds(pl.multiple_of(c * half, 8), half), :]
        mine = pltpu.make_async_copy(h_ref, dst, local_sem)
        mine.start()
        cp = pltpu.make_async_remote_copy(src_ref=h_ref, dst_ref=dst, send_sem=send_sem, recv_sem=recv_sem,
                                          device_id=(x, y, 1 - c), device_id_type=MESH)
        cp.start()
        cp.wait()
        mine.wait()

    return pl.pallas_call(
        body, name=name,
        in_specs=[_ANY], out_specs=_ANY, out_shape=jax.ShapeDtypeStruct((2 * half, cols), mine_half.dtype),
        scratch_shapes=[pltpu.SemaphoreType.DMA, pltpu.SemaphoreType.DMA, pltpu.SemaphoreType.DMA],
        compiler_params=pltpu.CompilerParams(has_side_effects=True),
    )(mine_half)


def _reduce_scatter_chips(buf, tag):
    own, sib = _swap_halves(buf, "rs_swap_" + tag)
    pre = _add2(own, sib, "rs_add2_" + tag)
    parts = _exchange_chips(pre, "rs_xchg_" + tag)
    red = _add4(parts, "rs_add4_" + tag)
    return _join_halves(red, "rs_join_" + tag)


MAX_DMA_BYTES = 2 * 1024 * 1024
ROW_ALIGN = 16


def _pieces(rows, row_bytes):
    n = max(1, -(-(rows * row_bytes) // MAX_DMA_BYTES))
    step = -(-(-(-rows // n)) // ROW_ALIGN) * ROW_ALIGN
    return [(r, min(step, rows - r)) for r in range(0, rows, step)]


def _half_plan(arrays, row_axis):
    plan = []
    for a, arr in enumerate(arrays):
        row_bytes = math.prod(arr.shape[row_axis + 1:]) * arr.dtype.itemsize * (arr.shape[0] if row_axis else 1)
        plan += [(a, r0, nr) for r0, nr in _pieces(arr.shape[row_axis] // 2, row_bytes)]
    return plan


def _rows(start, size):
    return pl.ds(pl.multiple_of(start, ROW_ALIGN), size)


def _remote(src, dst, send_sems, recv_sems, k, to):
    return pltpu.make_async_remote_copy(src_ref=src, dst_ref=dst, send_sem=send_sems.at[k], recv_sem=recv_sems.at[k],
                                        device_id=to, device_id_type=MESH)


def _comm_call(body, name, ins, out_shapes, n_remote, n_local, aliases=None):
    return pl.pallas_call(
        body, name=name,
        in_specs=[_ANY] * len(ins), out_specs=[_ANY] * len(out_shapes), out_shape=out_shapes,
        scratch_shapes=[pltpu.SemaphoreType.DMA((n_remote,)), pltpu.SemaphoreType.DMA((n_remote,)),
                        pltpu.SemaphoreType.DMA((max(n_local, 1),))],
        input_output_aliases=aliases or {},
        compiler_params=pltpu.CompilerParams(has_side_effects=True),
    )(*ins)


def _cast_shard(w, l, me_idx, name):
    _, k, cols = w.shape
    tr = _tile(k, 256)
    if k % tr:
        tr = k

    def body(me_ref, w_ref, s_ref, land_ref):
        del me_ref
        v = w_ref[...].astype(BF16)
        s_ref[...] = v
        land_ref[...] = v

    grid_spec = pltpu.PrefetchScalarGridSpec(
        num_scalar_prefetch=1, grid=(k // tr,),
        in_specs=[pl.BlockSpec((None, tr, cols), lambda i, me: (l, i, 0))],
        out_specs=[pl.BlockSpec((tr, cols), lambda i, me: (i, 0)),
                   pl.BlockSpec((None, tr, cols), lambda i, me: (me[0], i, 0))])
    return pl.pallas_call(
        body, name=name, grid_spec=grid_spec,
        out_shape=[jax.ShapeDtypeStruct((k, cols), BF16), jax.ShapeDtypeStruct((N_CHIPS, k, cols), BF16)],
        compiler_params=_params(("parallel",)),
    )(me_idx, w)


def _gather_d2d(lands, name):
    n = len(lands)
    plan = _half_plan(lands, 1)
    plan = [(a, r0, nr) for a, r0, nr in plan]

    def body(*refs):
        out_refs = refs[n:2 * n]
        send_sems, recv_sems, _ = refs[2 * n:]
        x, y, c, others = _place()
        sends = []
        for i, (a, r0, nr) in enumerate(plan):
            rows = _rows(c * (lands[a].shape[1] // 2) + r0, nr)
            for j, (ox, oy) in enumerate(others):
                blk = out_refs[a].at[2 * ox + oy, rows, :]
                cp = _remote(blk, blk, send_sems, recv_sems, 3 * i + j, (x, y, 1 - c))
                cp.start()
                sends.append(cp)
        for i, (a, r0, nr) in enumerate(plan):
            rows = _rows((1 - c) * (lands[a].shape[1] // 2) + r0, nr)
            for j, (ox, oy) in enumerate(others):
                blk = out_refs[a].at[2 * ox + oy, rows, :]
                _remote(blk, blk, send_sems, recv_sems, 3 * i + j, (x, y, c)).wait_recv()
        for cp in sends:
            cp.wait_send()

    outs = [jax.ShapeDtypeStruct(a.shape, a.dtype) for a in lands]
    return _comm_call(body, name, lands, outs, 3 * len(plan), 0, aliases={a: a for a in range(n)})


def _rs_swap(ts, name):
    n = len(ts)
    plan = _half_plan(ts, 1)

    def body(*refs):
        t_refs, out_refs = refs[:n], refs[n:2 * n]
        send_sems, recv_sems, _ = refs[2 * n:]
        x, y, c, _o = _place()
        sends = []
        for i, (a, r0, nr) in enumerate(plan):
            src = t_refs[a].at[:, _rows((1 - c) * (ts[a].shape[1] // 2) + r0, nr), :]
            cp = _remote(src, out_refs[a].at[:, pl.ds(r0, nr), :], send_sems, recv_sems, i, (x, y, 1 - c))
            cp.start()
            sends.append(cp)
        for i, (a, r0, nr) in enumerate(plan):
            blk = out_refs[a].at[:, pl.ds(r0, nr), :]
            _remote(blk, blk, send_sems, recv_sems, i, (x, y, c)).wait_recv()
        for cp in sends:
            cp.wait_send()

    outs = [jax.ShapeDtypeStruct((t.shape[0], t.shape[1] // 2, t.shape[2]), t.dtype) for t in ts]
    return _comm_call(body, name, ts, outs, len(plan), 0)


def _add_half(t, got, c_idx, me_idx, name):
    nb, k, cols = t.shape
    half = k // 2

    def body(c_ref, me_ref, t_ref, g_ref, o_ref, mine_ref):
        del c_ref
        v = (t_ref[...].astype(F32) + g_ref[...].astype(F32)).astype(o_ref.dtype)
        o_ref[...] = v

        @pl.when(pl.program_id(0) == me_ref[0])
        def _():
            mine_ref[...] = v

    blk = pl.BlockSpec((1, half, cols), lambda i, c, me: (i, 0, 0))
    grid_spec = pltpu.PrefetchScalarGridSpec(
        num_scalar_prefetch=2, grid=(nb,),
        in_specs=[pl.BlockSpec((1, half, cols), lambda i, c, me: (i, c[0], 0)), blk],
        out_specs=[blk, pl.BlockSpec((1, half, cols), lambda i, c, me: (me[0], 0, 0))])
    shp = jax.ShapeDtypeStruct(got.shape, got.dtype)
    return pl.pallas_call(
        body, name=name, grid_spec=grid_spec, out_shape=[shp, shp],
        compiler_params=_params(("arbitrary",)),
    )(c_idx, me_idx, t, got)


def _add4_half(parts, c_idx, name):
    nb, half, cols = parts.shape
    t = _tile(half, 256)
    if half % t:
        t = half
    steps = half // t

    def body(c_ref, p_ref, o_ref):
        del c_ref
        acc = p_ref[0].astype(F32)
        for k in range(1, nb):
            acc = acc + p_ref[k].astype(F32)
        o_ref[...] = acc

    grid_spec = pltpu.PrefetchScalarGridSpec(
        num_scalar_prefetch=1, grid=(steps,),
        in_specs=[pl.BlockSpec((nb, t, cols), lambda i, c: (0, i, 0))],
        out_specs=pl.BlockSpec((t, cols), lambda i, c: (c[0] * steps + i, 0)))
    return pl.pallas_call(
        body, name=name, grid_spec=grid_spec, out_shape=jax.ShapeDtypeStruct((2 * half, cols), F32),
        compiler_params=_params(("parallel",)),
    )(c_idx, parts)


def _rs_join(fulls, name):
    n = len(fulls)
    plan = _half_plan(fulls, 0)

    def body(*refs):
        out_refs = refs[n:2 * n]
        send_sems, recv_sems, _ = refs[2 * n:]
        x, y, c, _o = _place()
        sends = []
        for i, (a, r0, nr) in enumerate(plan):
            blk = out_refs[a].at[_rows(c * (fulls[a].shape[0] // 2) + r0, nr), :]
            cp = _remote(blk, blk, send_sems, recv_sems, i, (x, y, 1 - c))
            cp.start()
            sends.append(cp)
        for i, (a, r0, nr) in enumerate(plan):
            blk = out_refs[a].at[_rows((1 - c) * (fulls[a].shape[0] // 2) + r0, nr), :]
            _remote(blk, blk, send_sems, recv_sems, i, (x, y, c)).wait_recv()
        for cp in sends:
            cp.wait_send()

    outs = [jax.ShapeDtypeStruct(f.shape, f.dtype) for f in fulls]
    return _comm_call(body, name, fulls, outs, len(plan), 0, aliases={a: a for a in range(n)})


_HBM = pl.BlockSpec(memory_space=pltpu.HBM)
_SEM = pl.BlockSpec(memory_space=pltpu.SEMAPHORE)
_EFFECT = pltpu.SideEffectType.DATAFLOW_SIDE_EFFECTING


def _ici_plan(kind, a_list):
    if kind == "gather":
        return _half_plan(a_list, 0)
    plan = []
    for a, p in enumerate(a_list):
        plan += [(a, r0, nr) for r0, nr in _pieces(p.shape[1], p.shape[2] * p.dtype.itemsize)]
    return plan


def _ici_refs(kind, a_ref, b_ref, a_shape, r0, nr, c, me, peer):
    if kind == "gather":
        rows = _rows(c * (a_shape[0] // 2) + r0, nr)
        return a_ref.at[rows, :], b_ref.at[me, rows, :], b_ref.at[peer, rows, :]
    rows = pl.ds(r0, nr)
    return a_ref.at[peer, rows, :], b_ref.at[me, rows, :], b_ref.at[peer, rows, :]


def _ici_start(kind, a_list, b_list, name):
    n = len(a_list)
    plan = _ici_plan(kind, a_list)
    shapes = [a.shape for a in a_list]

    def body(*refs):
        a_refs, b_refs = refs[:n], refs[n:2 * n]
        send_sems, recv_sems = refs[2 * n], refs[2 * n + 1]
        token = refs[4 * n + 2]
        x, y, c, others = _place()
        me = 2 * x + y
        for i, (a, r0, nr) in enumerate(plan):
            for j, (ox, oy) in enumerate(others):
                src, dst, _ = _ici_refs(kind, a_refs[a], b_refs[a], shapes[a], r0, nr, c, me, 2 * ox + oy)
                _remote(src, dst, send_sems, recv_sems, 3 * i + j, (ox, oy, c)).start()
        token[...] = jnp.zeros_like(token)

    hbm = lambda v: pltpu.HBM(v.shape, v.dtype)
    ncp = 3 * len(plan)
    outs = pl.pallas_call(
        body, name=name,
        in_specs=[_HBM] * (2 * n),
        out_specs=[_SEM, _SEM] + [_HBM] * (2 * n) + [pl.BlockSpec(memory_space=pltpu.VMEM)],
        out_shape=[pltpu.SemaphoreType.DMA((ncp,)), pltpu.SemaphoreType.DMA((ncp,))]
                  + [hbm(v) for v in a_list] + [hbm(v) for v in b_list] + [jax.ShapeDtypeStruct((8, LANES), F32)],
        input_output_aliases={i: 2 + i for i in range(2 * n)},
        compiler_params=pltpu.CompilerParams(has_side_effects=_EFFECT),
    )(*[pltpu.with_memory_space_constraint(v, pltpu.HBM) for v in list(a_list) + list(b_list)])
    return outs[0], outs[1], outs[2:2 + n], outs[2 + n:2 + 2 * n], outs[2 + 2 * n]


def _ici_wait(kind, started, after, name):
    send_sems, recv_sems, a_list, b_list, _ = started
    n = len(a_list)
    plan = _ici_plan(kind, a_list)
    shapes = [a.shape for a in a_list]

    def body(*refs):
        a_refs, b_refs = refs[:n], refs[n:2 * n]
        send_sems, recv_sems = refs[2 * n], refs[2 * n + 1]
        x, y, c, others = _place()
        me = 2 * x + y
        for i, (a, r0, nr) in enumerate(plan):
            for j, (ox, oy) in enumerate(others):
                src, dst, land = _ici_refs(kind, a_refs[a], b_refs[a], shapes[a], r0, nr, c, me, 2 * ox + oy)
                _remote(src, dst, send_sems, recv_sems, 3 * i + j, (ox, oy, c)).wait_send()
                _remote(land, land, send_sems, recv_sems, 3 * i + j, (x, y, c)).wait_recv()

    hbm = lambda v: pltpu.HBM(v.shape, v.dtype)
    outs = pl.pallas_call(
        body, name=name,
        in_specs=[_HBM] * (2 * n) + [_SEM, _SEM, _ANY],
        out_specs=[_HBM] * (2 * n),
        out_shape=[hbm(v) for v in a_list] + [hbm(v) for v in b_list],
        input_output_aliases={i: i for i in range(2 * n)},
        compiler_params=pltpu.CompilerParams(has_side_effects=_EFFECT),
    )(*a_list, *b_list, send_sems, recv_sems, after)
    return outs[n:]


def _rs_begin(ts, c_idx, me_idx, tag):
    got = _rs_swap(ts, "rs_swap_" + tag)
    pairs = [_add_half(t, g, c_idx, me_idx, f"rs_add2_{tag}_{a}") for a, (t, g) in enumerate(zip(ts, got))]
    return _ici_start("scatter", [p for p, _ in pairs], [m for _, m in pairs], "rs_xchg_start_" + tag)


def _rs_finish(started, after, c_idx, tag):
    parts = _ici_wait("scatter", started, after, "rs_xchg_wait_" + tag)
    fulls = [_add4_half(p, c_idx, f"rs_add4_{tag}_{a}") for a, p in enumerate(parts)]
    return _rs_join(fulls, "rs_join_" + tag)


def _pack_rows(pieces, rows, dtype):
    flat = jnp.concatenate([p.astype(dtype).reshape(-1) for p in pieces])
    return jnp.pad(flat, (0, rows * PACK_COLS - flat.shape[0])).reshape(rows, PACK_COLS)


def _unpack(flat, shapes):
    out, off = [], 0
    for shp in shapes:
        size = math.prod(shp)
        out.append(flat[off:off + size].reshape(shp))
        off += size
    return out


def _rows_for(n_elems, mult):
    rows = -(-n_elems // PACK_COLS)
    return -(-rows // mult) * mult


BIG_SHARDS = [("w_in", (D_MODEL, 1474)), ("w_branch_att", (D_ATT, 256)), ("w_branch_conv", (D_CONV, 256)),
              ("w_branch_sgu", (D_SGU, 256)), ("w_out", (256, D_MODEL)), ("w_ffn_up", (D_MODEL, FF_BLK)),
              ("w_ffn_down", (D_FF // N_CHIPS, D_MODEL))]
SMALL_SHARDS = [("b_gate", (3, 256)), ("conv_mix_w", (3, 64)), ("conv_ffn_w", (3, FF_BLK))]
REPLICATED = [("pre_mix_g", (D_MODEL,)), ("post_mix_g", (D_MODEL,)), ("pre_ffn_g", (D_MODEL,)),
              ("post_ffn_g", (D_MODEL,)), ("b_forget", (N_HEADS,)), ("sgu_ln_g", (D_SGU,)), ("sgu_ln_b", (D_SGU,)),
              ("sgu_w", (N_GROUPS, CHUNK, CHUNK)), ("sgu_b", (N_GROUPS, CHUNK))]
WEIGHT_ORDER = ["pre_mix_g", "post_mix_g", "pre_ffn_g", "post_ffn_g", "w_in", "b_forget", "b_gate", "conv_mix_w",
                "sgu_ln_g", "sgu_ln_b", "sgu_w", "sgu_b", "w_branch_att", "w_branch_conv", "w_branch_sgu", "w_out",
                "w_ffn_up", "conv_ffn_w", "w_ffn_down"]

_SMALL_ELEMS = sum(math.prod(s) for _, s in SMALL_SHARDS)
_REP_ELEMS = sum(math.prod(s) for _, s in REPLICATED)
_REP_QUARTER = -(-(DEPTH * _REP_ELEMS) // N_CHIPS)
SMALL_PARAM_ROWS = _rows_for(DEPTH * _SMALL_ELEMS, 32)
SMALL_ROWS = _rows_for(DEPTH * _SMALL_ELEMS + _REP_QUARTER, 32)
IN_WIDTH = 5896
IN_SHARD = IN_WIDTH // N_CHIPS
IN_SHARD_PAD = 1536
IN_PAD = 6144


def _gather_small(wts):
    shard = _pack_rows([wts[n] for n, _ in SMALL_SHARDS], SMALL_PARAM_ROWS, F32)
    full = _all_gather_chips(shard, "gather_small_params").reshape(N_CHIPS, -1)
    per_chip = [_unpack(full[j], [(DEPTH,) + s for _, s in SMALL_SHARDS]) for j in range(N_CHIPS)]
    return {n: jnp.concatenate([per_chip[j][i] for j in range(N_CHIPS)], axis=-1)
            for i, (n, _) in enumerate(SMALL_SHARDS)}


BIG_NAMES = [n for n, _ in BIG_SHARDS]
FIRST_NAMES = ["w_in"]
LATE_NAMES = BIG_NAMES[1:]


def _gather_begin(wts, l, me_idx, names, tag):
    cast = [_cast_shard(wts[n], l, me_idx, "cast_" + n) for n in names]
    return _ici_start("gather", [sh for sh, _ in cast], [ld for _, ld in cast], "gather_ici_start_" + tag)


def _gather_finish(started, after, names, tag):
    lands = _ici_wait("gather", started, after, "gather_ici_wait_" + tag)
    return dict(zip(names, _gather_d2d(lands, "gather_d2d_" + tag)))


def _pad_rows(a, rows):
    return jnp.pad(a, ((0, rows - a.shape[0]), (0, 0)))


def _whole_cols(land):
    return land.transpose(1, 0, 2).reshape(land.shape[1], -1)


_O_F = 3 * D_ATT
_O_B = _O_F + N_HEADS
_O_GL = _O_B + 3 * D_CONV + 2 * D_SGU


_LOCAL_ORDER = [(_O_GL, IN_WIDTH), (0, _O_F), (_O_B, _O_GL), (_O_F, _O_B)]


def _own_cols(land, lo, hi):
    pieces = []
    for j in range(N_CHIPS):
        a, b = max(lo, j * IN_SHARD), min(hi, (j + 1) * IN_SHARD)
        if a < b:
            pieces.append(land[j][:, a - j * IN_SHARD:b - j * IN_SHARD])
    return pieces


def _local_cols(m, lo, hi):
    pieces, off = [], 0
    for a, b in _LOCAL_ORDER:
        x, y = max(lo, a), min(hi, b)
        if x < y:
            pieces.append((x, m[:, off + x - a:off + y - a]))
        off += b - a
    pieces = [p for _, p in sorted(pieces, key=lambda t: t[0])]
    if hi > IN_WIDTH:
        pieces.append(jnp.zeros((m.shape[0], hi - max(lo, IN_WIDTH)), m.dtype))
    return pieces


def _prep_first(wts, lands, small, l):
    land = lands["w_in"]
    cf = small["conv_ffn_w"][l]
    blk = lambda a, j: a[:, j * FF_BLK:(j + 1) * FF_BLK]
    local = [piece for lo, hi in _LOCAL_ORDER for piece in _own_cols(land, lo, hi)]
    return {
        "w_p": jnp.concatenate(local + [jnp.zeros((D_MODEL, IN_PAD - IN_WIDTH), BF16)], axis=1),
        "wf_t": _pad_rows(jnp.concatenate(_own_cols(land, _O_F, _O_B), axis=1).T, F_ROWS),
        "b_forget": _pad_rows(wts["b_forget"][l].reshape(N_HEADS, 1), F_ROWS),
        "b_gate": _pad_rows(small["b_gate"][l], 8),
        "conv_mix_w": _pad_rows(small["conv_mix_w"][l], 8),
        "conv_ffn_w": _pad_rows(jnp.concatenate([blk(cf, 0), blk(cf, 2), blk(cf, 1), blk(cf, 3)], axis=1), 8),
        "pre_mix_g": wts["pre_mix_g"][l].reshape(1, -1), "post_mix_g": wts["post_mix_g"][l].reshape(1, -1),
        "pre_ffn_g": wts["pre_ffn_g"][l].reshape(1, -1), "post_ffn_g": wts["post_ffn_g"][l].reshape(1, -1),
        "ln_g": wts["sgu_ln_g"][l].reshape(1, -1), "ln_b": wts["sgu_ln_b"][l].reshape(1, -1),
        "sgu_w": wts["sgu_w"][l],
        "sgu_bias": jnp.repeat(wts["sgu_b"][l].T, HEAD_DIM, axis=1),
    }


def _prep_late(lands):
    up = lands["w_ffn_up"]
    return {
        "w_att": _whole_cols(lands["w_branch_att"]), "w_conv": _whole_cols(lands["w_branch_conv"]),
        "w_sgu": _whole_cols(lands["w_branch_sgu"]),
        "w_out": lands["w_out"].reshape(D_MODEL, D_MODEL),
        "w_up": jnp.concatenate([up[0], up[2], up[1], up[3]], axis=1),
        "w_down": lands["w_ffn_down"].reshape(D_FF, D_MODEL),
    }


def _layer_fwd(x, p, dep=None, late=None):
    s = x.shape[0]
    xn = _rms_fwd(x, p["pre_mix_g"], "rms_pre_mix", dep)
    h = _mm(xn, p["w_p"], "nn", BF16, "mm_in", s, 512, D_MODEL)
    f_row = _mm(p["wf_t"], xn, "nt", F32, "mm_forget", F_ROWS, 2048, D_MODEL)
    ck = _gate_fwd(f_row, p["b_forget"], "gate_fwd")
    o, o_f32, lse = _attn_fwd(h, ck, "attn_fwd")
    yc = _sconv_fwd(h, p["conv_mix_w"], "sconv_fwd")
    ys = _sgu_fwd(h, p["ln_g"], p["ln_b"], p["sgu_w"], p["sgu_bias"], "sgu_fwd")
    if late is not None:
        p.update(late(o))
    merged = _merge_fwd(h, (o, yc, ys), (p["w_att"], p["w_conv"], p["w_sgu"]), p["b_gate"], "merge_fwd")
    mo = _mm(merged, p["w_out"], "nn", F32, "mm_out", 2048, 512, D_MODEL)
    x1 = _resid_post(x, mo, p["post_mix_g"], "post_mix")
    xn2 = _rms_fwd(x1, p["pre_ffn_g"], "rms_pre_ffn")
    h2 = _mm(xn2, p["w_up"], "nn", BF16, "mm_up", 2048, 512, D_MODEL)
    pact = _ffn_act_fwd(h2, p["conv_ffn_w"], "ffn_act_fwd")
    ff = _mm(pact, p["w_down"], "nn", F32, "mm_down", 2048, 512, FF_BLK)
    x2 = _resid_post(x1, ff, p["post_ffn_g"], "post_ffn")
    saved = dict(x=x, xn=xn, h=h, f_row=f_row, ck=ck, o=o, o_f32=o_f32, lse=lse, yc=yc, ys=ys, merged=merged, mo=mo, x1=x1,
                 xn2=xn2, h2=h2, pact=pact, ff=ff)
    return x2, saved


def _layer_bwd(dx2, p, sv, dep=None, early=None):
    s = dx2.shape[0]
    g = {}
    same = lambda b: b
    dff, g["post_ffn_g"] = _rms_bwd(sv["ff"], p["post_ffn_g"], [dx2], None, BF16, "post_ffn_bwd", dep)
    dpact = _mm(dff, p["w_down"], "nt", BF16, "mm_down_dx", 1024, FF_BLK, D_MODEL)
    t_down = _mm(sv["pact"], dff, "tn", BF16, "mm_down_dw", 256, D_MODEL, s).reshape(N_CHIPS, -1, D_MODEL)
    dh2, dconv_ffn = _ffn_act_conv_bwd(sv["h2"], p["conv_ffn_w"], dpact, "ffn_act_conv_bwd")
    dxn2 = _mm(dh2, p["w_up"], "nt", F32, "mm_up_dx", 1024, D_MODEL, FF_BLK)
    t_up = _mm(sv["xn2"], dh2, "tn", BF16, "mm_up_dw", 512, FF_BLK, s, chip_of=lambda b: (b % 2) * 2 + b // 2)
    dx1, g["pre_ffn_g"] = _rms_bwd(sv["x1"], p["pre_ffn_g"], [dxn2], dx2, F32, "pre_ffn_bwd")
    dep_mix = early([t_up, t_down]) if early is not None else None
    dmo, g["post_mix_g"] = _rms_bwd(sv["mo"], p["post_mix_g"], [dx1], None, BF16, "post_mix_bwd", dep_mix)
    dmerged = _mm(dmo, p["w_out"], "nt", F32, "mm_out_dx", 2048, 512, D_MODEL)
    t_out = _mm(sv["merged"], dmo, "tn", BF16, "mm_out_dw", 512, D_MODEL, s).reshape(N_CHIPS, -1, D_MODEL)
    acts = (sv["o"], sv["yc"], sv["ys"])
    ws = (p["w_att"], p["w_conv"], p["w_sgu"])
    dy_a, dy_c, dy_s, dgl, db_gate = _merge_bwd(sv["h"], acts, ws, p["b_gate"], dmerged, "merge_bwd")
    do = _mm(dy_a, p["w_att"], "nt", BF16, "mm_att_dx", 2048, D_ATT, D_MODEL)
    dyc = _mm(dy_c, p["w_conv"], "nt", BF16, "mm_conv_dx", 2048, D_CONV, D_MODEL)
    dys = _mm(dy_s, p["w_sgu"], "nt", BF16, "mm_sgu_dx", 2048, D_SGU, D_MODEL)
    t_att = _mm(sv["o"], dy_a, "tn", BF16, "mm_att_dw", D_ATT, 256, s, chip_of=same)
    t_conv = _mm(sv["yc"], dy_c, "tn", BF16, "mm_conv_dw", D_CONV, 256, s, chip_of=same)
    t_sgu = _mm(sv["ys"], dy_s, "tn", BF16, "mm_sgu_dw", D_SGU, 256, s, chip_of=same)
    d_conv, dconv_mix = _sconv_bwd(sv["h"], p["conv_mix_w"], dyc, "sconv_bwd")
    d_sgu, g["sgu_ln_g"], g["sgu_ln_b"], g["sgu_w"], dbias = _sgu_bwd(
        sv["h"], p["ln_g"], p["ln_b"], p["sgu_w"], p["sgu_bias"], dys, "sgu_bwd")
    dq, dk, dv, dc_even, dc_odd = _attn_bwd(sv["h"], sv["ck"], sv["o_f32"], sv["lse"], do, "attn_bwd")
    df, db_forget = _gate_bwd(sv["f_row"], p["b_forget"], dc_even, dc_odd, "gate_bwd")
    f_cols = jnp.concatenate([df[:N_HEADS].T, jnp.zeros((s, IN_PAD - IN_WIDTH), BF16)], axis=1)
    dh = _assemble_dh([dgl, dq, dk, dv, d_conv, d_sgu, f_cols], "assemble_dh")
    dxn = _mm(dh, p["w_p"], "nt", F32, "mm_in_dx", 1024, D_MODEL, 2048)
    dw_p = _mm(sv["xn"], dh, "tn", BF16, "mm_in_dw", D_MODEL, 512, s)
    t_in = jnp.stack([jnp.concatenate(_local_cols(dw_p, j * IN_SHARD, j * IN_SHARD + IN_SHARD_PAD), axis=1)
                      for j in range(N_CHIPS)])
    dx, g["pre_mix_g"] = _rms_bwd(sv["x"], p["pre_mix_g"], [dxn], dx1, F32, "pre_mix_bwd")
    blk = lambda a, j: a[:, j * FF_BLK:(j + 1) * FF_BLK]
    g["conv_ffn_w"] = jnp.concatenate([blk(dconv_ffn, 0), blk(dconv_ffn, 2), blk(dconv_ffn, 1),
                                       blk(dconv_ffn, 3)], axis=1)[:3]
    g["conv_mix_w"] = dconv_mix[:3]
    g["b_gate"] = db_gate[:3]
    g["b_forget"] = db_forget[:N_HEADS, 0]
    g["sgu_b"] = jnp.sum(dbias.reshape(CHUNK, N_GROUPS, HEAD_DIM), axis=-1).T
    for n in ("pre_mix_g", "post_mix_g", "pre_ffn_g", "post_ffn_g", "sgu_ln_g", "sgu_ln_b"):
        g[n] = g[n].reshape(-1)
    mix = [t_in, t_att, t_conv, t_sgu, t_out]
    return dx, (mix if early is not None else mix + [t_up, t_down]), g


def _assemble_dh(pieces, name):
    s = pieces[0].shape[0]
    t = _tile(s, 512)
    width = sum(a.shape[1] for a in pieces)

    def body(*refs):
        out = refs[-1]
        col = 0
        for ref in refs[:-1]:
            w = ref.shape[1]
            out[:, col:col + w] = ref[...].astype(out.dtype)
            col += w

    return pl.pallas_call(
        body, name=name, grid=(s // t,),
        in_specs=[pl.BlockSpec((t, a.shape[1]), lambda i: (i, 0)) for a in pieces],
        out_specs=pl.BlockSpec((t, width), lambda i: (i, 0)),
        out_shape=jax.ShapeDtypeStruct((s, width), BF16),
        compiler_params=_params(("parallel",)),
    )(*pieces)


def _shard_cols(a, j):
    w = a.shape[-1] // N_CHIPS
    return a[..., j * w:(j + 1) * w]


def kernel(x, pre_mix_g, post_mix_g, pre_ffn_g, post_ffn_g, w_in, b_forget, b_gate, conv_mix_w, sgu_ln_g, sgu_ln_b, sgu_w, sgu_b, w_branch_att, w_branch_conv, w_branch_sgu, w_out, w_ffn_up, conv_ffn_w, w_ffn_down, loss_target, m_pre_mix_g, m_post_mix_g, m_pre_ffn_g, m_post_ffn_g, m_w_in, m_b_forget, m_b_gate, m_conv_mix_w, m_sgu_ln_g, m_sgu_ln_b, m_sgu_w, m_sgu_b, m_w_branch_att, m_w_branch_conv, m_w_branch_sgu, m_w_out, m_w_ffn_up, m_conv_ffn_w, m_w_ffn_down, v_pre_mix_g, v_post_mix_g, v_pre_ffn_g, v_post_ffn_g, v_w_in, v_b_forget, v_b_gate, v_conv_mix_w, v_sgu_ln_g, v_sgu_ln_b, v_sgu_w, v_sgu_b, v_w_branch_att, v_w_branch_conv, v_w_branch_sgu, v_w_out, v_w_ffn_up, v_conv_ffn_w, v_w_ffn_down):
    wts = dict(pre_mix_g=pre_mix_g, post_mix_g=post_mix_g, pre_ffn_g=pre_ffn_g, post_ffn_g=post_ffn_g, w_in=w_in,
               b_forget=b_forget, b_gate=b_gate, conv_mix_w=conv_mix_w, sgu_ln_g=sgu_ln_g, sgu_ln_b=sgu_ln_b,
               sgu_w=sgu_w, sgu_b=sgu_b, w_branch_att=w_branch_att, w_branch_conv=w_branch_conv,
               w_branch_sgu=w_branch_sgu, w_out=w_out, w_ffn_up=w_ffn_up, conv_ffn_w=conv_ffn_w,
               w_ffn_down=w_ffn_down)
    moms = dict(pre_mix_g=m_pre_mix_g, post_mix_g=m_post_mix_g, pre_ffn_g=m_pre_ffn_g, post_ffn_g=m_post_ffn_g,
                w_in=m_w_in, b_forget=m_b_forget, b_gate=m_b_gate, conv_mix_w=m_conv_mix_w, sgu_ln_g=m_sgu_ln_g,
                sgu_ln_b=m_sgu_ln_b, sgu_w=m_sgu_w, sgu_b=m_sgu_b, w_branch_att=m_w_branch_att,
                w_branch_conv=m_w_branch_conv, w_branch_sgu=m_w_branch_sgu, w_out=m_w_out, w_ffn_up=m_w_ffn_up,
                conv_ffn_w=m_conv_ffn_w, w_ffn_down=m_w_ffn_down)
    vels = dict(pre_mix_g=v_pre_mix_g, post_mix_g=v_post_mix_g, pre_ffn_g=v_pre_ffn_g, post_ffn_g=v_post_ffn_g,
                w_in=v_w_in, b_forget=v_b_forget, b_gate=v_b_gate, conv_mix_w=v_conv_mix_w, sgu_ln_g=v_sgu_ln_g,
                sgu_ln_b=v_sgu_ln_b, sgu_w=v_sgu_w, sgu_b=v_sgu_b, w_branch_att=v_w_branch_att,
                w_branch_conv=v_w_branch_conv, w_branch_sgu=v_w_branch_sgu, w_out=v_w_out, w_ffn_up=v_w_ffn_up,
                conv_ffn_w=v_conv_ffn_w, w_ffn_down=v_w_ffn_down)

    c_idx = lax.axis_index("c").astype(jnp.int32).reshape(1)
    me_idx = (2 * lax.axis_index("x") + lax.axis_index("y")).astype(jnp.int32).reshape(1)
    small = _gather_small(wts)

    xs = x[0]
    layers, saved = [], []
    first = _gather_begin(wts, 0, me_idx, FIRST_NAMES, "first")
    rest = _gather_begin(wts, 0, me_idx, LATE_NAMES, "late")
    lands = _gather_finish(first, xs, FIRST_NAMES, "first")
    late = lambda after: _prep_late(_gather_finish(rest, after, LATE_NAMES, "late"))
    for l in range(DEPTH):
        p = _prep_first(wts, lands, small, l)
        if l > 0:
            p.update(_prep_late(lands))
        nxt = _gather_begin(wts, l + 1, me_idx, BIG_NAMES, "all") if l + 1 < DEPTH else None
        dep = ([nxt[4]] if nxt else []) + ([rest[4]] if l == 0 else [])
        xs, sv = _layer_fwd(xs, p, dep or None, late if l == 0 else None)
        if nxt:
            lands = _gather_finish(nxt, xs, BIG_NAMES, "all")
        layers.append(p)
        saved.append(sv)
    dy, loss_part = _loss_head(xs, loss_target[0], "loss_head")
    loss = lax.psum(loss_part[0, 0], ("x", "y", "c"))

    big_red = [None] * DEPTH
    small_grads = [None] * DEPTH
    pending = None
    ffn = []
    for l in reversed(range(DEPTH)):
        early = None
        if l == 0:
            def early(ts_ffn):
                ffn.append(_rs_begin(ts_ffn, c_idx, me_idx, "ffn"))
                return ffn[0][4]
        dy, ts, small_grads[l] = _layer_bwd(dy, layers[l], saved[l], pending[4] if pending else None, early)
        if pending:
            big_red[l + 1] = _rs_finish(pending, dy, c_idx, "big")
        pending = _rs_begin(ts, c_idx, me_idx, "mix" if l == 0 else "big")
    red_ffn = _rs_finish(ffn[0], dy, c_idx, "ffn")
    grad_x = dy[None]

    rep_flat = jnp.concatenate([small_grads[l][n].reshape(-1) for l in range(DEPTH) for n, _ in REPLICATED])
    rep_flat = jnp.pad(rep_flat, (0, N_CHIPS * _REP_QUARTER - rep_flat.shape[0]))
    rows = []
    for j in range(N_CHIPS):
        pieces = [_shard_cols(small_grads[l][n], j) for l in range(DEPTH) for n, _ in SMALL_SHARDS]
        pieces.append(rep_flat[j * _REP_QUARTER:(j + 1) * _REP_QUARTER])
        rows.append(_pack_rows(pieces, SMALL_ROWS, F32))
    small_red = _reduce_scatter_chips(jnp.stack(rows), "small")
    small_all = _all_gather_chips(small_red, "gather_small")
    big_red[0] = _rs_finish(pending, small_all, c_idx, "mix") + red_ffn
    small_all = small_all.reshape(N_CHIPS, -1)

    grads = {}
    for i, (n, _) in enumerate(BIG_SHARDS):
        grads[n] = jnp.stack([big_red[l][i][:, :IN_SHARD] if n == "w_in" else big_red[l][i] for l in range(DEPTH)])
    mine_small = small_red.reshape(-1)
    parts = _unpack(mine_small, [s for _ in range(DEPTH) for _, s in SMALL_SHARDS])
    for i, (n, _) in enumerate(SMALL_SHARDS):
        grads[n] = jnp.stack([parts[l * len(SMALL_SHARDS) + i] for l in range(DEPTH)])
    off = DEPTH * _SMALL_ELEMS
    rep_all = jnp.concatenate([small_all[j, off:off + _REP_QUARTER] for j in range(N_CHIPS)])
    parts = _unpack(rep_all, [s for _ in range(DEPTH) for _, s in REPLICATED])
    for i, (n, _) in enumerate(REPLICATED):
        grads[n] = jnp.stack([parts[l * len(REPLICATED) + i] for l in range(DEPTH)])

    deltas, new_m, new_v = {}, {}, {}
    for n in WEIGHT_ORDER:
        deltas[n], new_m[n], new_v[n] = _adamw(wts[n], grads[n], moms[n], vels[n], "adamw_" + n)
    return (loss, grad_x, *[grads[n] for n in WEIGHT_ORDER], *[deltas[n] for n in WEIGHT_ORDER],
            *[new_m[n] for n in WEIGHT_ORDER], *[new_v[n] for n in WEIGHT_ORDER])
```

```python
import functools
import math

import jax
import jax.numpy as jnp
from jax import lax
from jax.experimental import pallas as pl
from jax.experimental.pallas import tpu as pltpu

F32 = jnp.float32
BF16 = jnp.bfloat16
MXU_DTYPE = jnp.bfloat16

D_MODEL = 1024
HEAD_DIM = 64
N_HEADS = 8
D_ATT = 512
D_CONV = 256
D_SGU = 256
N_GROUPS = 4
CHUNK = 128
D_FF = 2816
DEPTH = 4
RMS_EPS = 1e-6
LN_EPS = 1e-5
N_CHIPS = 4
LANES = 128
PACK_COLS = 1024
HALO = 16

ADAM_LR = 0.001
ADAM_B1 = 0.9
ADAM_B2 = 0.999
ADAM_EPS = 1e-08
ADAM_WD = 0.01
ADAM_STEP = 10

OFF_GL = 0
OFF_Q = 3 * D_MODEL
OFF_K = OFF_Q + D_ATT
OFF_V = OFF_K + D_ATT
OFF_BG = OFF_V + D_ATT
OFF_CG = OFF_BG + D_CONV
OFF_HC = OFF_CG + D_CONV
OFF_U = OFF_HC + D_CONV
OFF_VS = OFF_U + D_SGU
W_P = OFF_VS + D_SGU
F_ROWS = 16

VMEM_LIMIT = 56 * 1024 * 1024
MESH = pl.DeviceIdType.MESH


def _params(sem=None):
    if sem is None:
        return pltpu.CompilerParams(vmem_limit_bytes=VMEM_LIMIT)
    return pltpu.CompilerParams(dimension_semantics=sem, vmem_limit_bytes=VMEM_LIMIT)


def _tile(dim, pref):
    if dim <= pref:
        return dim
    if dim % pref == 0:
        return pref
    return dim


_DIMS = {"nn": (((1,), (0,)), ((), ())), "nt": (((1,), (1,)), ((), ())), "tn": (((0,), (0,)), ((), ()))}


def _mm(a, b, mode, out_dtype, name, tm, tn, tk, chip_of=None):
    if mode == "tn":
        K, M = a.shape
    else:
        M, K = a.shape
    N = b.shape[0] if mode == "nt" else b.shape[1]
    tm, tn, tk = _tile(M, tm), _tile(N // N_CHIPS if chip_of else N, tn), _tile(K, tk)
    nk = K // tk
    dims = _DIMS[mode]

    def body(a_ref, b_ref, o_ref, *acc):
        part = lax.dot_general(a_ref[...].astype(MXU_DTYPE), b_ref[...].astype(MXU_DTYPE), dims,
                               preferred_element_type=F32)
        if nk == 1:
            o_ref[...] = part.astype(o_ref.dtype)
        else:
            acc_ref = acc[0]
            k = pl.program_id(2)

            @pl.when(k == 0)
            def _():
                acc_ref[...] = part

            @pl.when(k > 0)
            def _():
                acc_ref[...] += part

            @pl.when(k == nk - 1)
            def _():
                o_ref[...] = acc_ref[...].astype(o_ref.dtype)

    if mode == "tn":
        a_spec = pl.BlockSpec((tk, tm), lambda i, j, k: (k, i))
    else:
        a_spec = pl.BlockSpec((tm, tk), lambda i, j, k: (i, k))
    if mode == "nt":
        b_spec = pl.BlockSpec((tn, tk), lambda i, j, k: (j, k))
    else:
        b_spec = pl.BlockSpec((tk, tn), lambda i, j, k: (k, j))
    if chip_of is None:
        out_spec = pl.BlockSpec((tm, tn), lambda i, j, k: (i, j))
        out_shape = jax.ShapeDtypeStruct((M, N), out_dtype)
    else:
        per = (N // N_CHIPS) // tn
        out_spec = pl.BlockSpec((None, tm, tn), lambda i, j, k: (chip_of(j // per), i, j % per))
        out_shape = jax.ShapeDtypeStruct((N_CHIPS, M, N // N_CHIPS), out_dtype)
    return pl.pallas_call(
        body,
        name=name,
        grid=(M // tm, N // tn, nk),
        in_specs=[a_spec, b_spec],
        out_specs=out_spec,
        out_shape=out_shape,
        scratch_shapes=[pltpu.VMEM((tm, tn), F32)] if nk > 1 else [],
        compiler_params=_params(("parallel", "parallel", "arbitrary")),
    )(a, b)


_GELU_K = math.sqrt(2.0 / math.pi)
_GELU_C = 0.044715


def _gelu(x):
    t = jnp.tanh(_GELU_K * (x + _GELU_C * (x * x * x)))
    return x * (0.5 * (1.0 + t))


def _gelu_and_grad(x):
    x2 = x * x
    t = jnp.tanh(_GELU_K * (x + _GELU_C * (x2 * x)))
    cdf = 0.5 * (1.0 + t)
    dcdf = 0.5 * (1.0 - t * t) * (_GELU_K * (1.0 + 3.0 * _GELU_C * x2))
    return x * cdf, cdf + x * dcdf


def _sigmoid(x):
    return 1.0 / (1.0 + jnp.exp(-x))


def _shift_down(cur, prev, k):
    h = prev.shape[0]
    ext = jnp.concatenate([prev, cur], axis=0)
    return pltpu.roll(ext, k, 0)[h:]


def _shift_up(cur, nxt, k):
    t, h = cur.shape[0], nxt.shape[0]
    ext = jnp.concatenate([cur, nxt], axis=0)
    return pltpu.roll(ext, t + h - k, 0)[:t]


def _row_sum8(x):
    t, c = x.shape
    return jnp.sum(x.reshape(t // 8, 8, c), axis=0)


_DEP = pl.BlockSpec((8, LANES), lambda i: (0, 0))


def _rms_fwd(x, g, name, dep=None):
    s, d = x.shape
    t = _tile(s, 512)

    def body(x_ref, g_ref, *rest):
        o_ref = rest[-1]
        xv = x_ref[...]
        r = lax.rsqrt(jnp.mean(xv * xv, axis=-1, keepdims=True) + RMS_EPS)
        o_ref[...] = (xv * r * g_ref[...]).astype(o_ref.dtype)

    deps = [] if dep is None else list(dep) if isinstance(dep, (list, tuple)) else [dep]
    return pl.pallas_call(
        body, name=name, grid=(s // t,),
        in_specs=[pl.BlockSpec((t, d), lambda i: (i, 0)), pl.BlockSpec((1, d), lambda i: (0, 0))] + [_DEP] * len(deps),
        out_specs=pl.BlockSpec((t, d), lambda i: (i, 0)),
        out_shape=jax.ShapeDtypeStruct((s, d), BF16),
        compiler_params=_params(("parallel",)),
    )(x, g, *deps)


def _resid_post(x, y, g, name):
    s, d = x.shape
    t = _tile(s, 512)

    def body(x_ref, y_ref, g_ref, o_ref):
        yv = y_ref[...]
        r = lax.rsqrt(jnp.mean(yv * yv, axis=-1, keepdims=True) + RMS_EPS)
        o_ref[...] = x_ref[...] + yv * r * g_ref[...]

    row = pl.BlockSpec((t, d), lambda i: (i, 0))
    return pl.pallas_call(
        body, name=name, grid=(s // t,),
        in_specs=[row, row, pl.BlockSpec((1, d), lambda i: (0, 0))],
        out_specs=row,
        out_shape=jax.ShapeDtypeStruct((s, d), F32),
        compiler_params=_params(("parallel",)),
    )(x, y, g)


def _rms_bwd(xin, g, dys, dres, out_dtype, name, dep=None):
    s, d = xin.shape
    t = _tile(s, 512)
    n = s // t
    n_dy = len(dys)
    has_res = dres is not None
    deps = [] if dep is None else [dep]

    def body(*refs):
        x_ref, g_ref = refs[0], refs[1]
        dy_refs = refs[2:2 + n_dy]
        pos = 2 + n_dy
        res_ref = refs[pos] if has_res else None
        pos += (1 if has_res else 0) + len(deps)
        dx_ref, dg_ref, acc_ref = refs[pos], refs[pos + 1], refs[pos + 2]
        i = pl.program_id(0)
        xv = x_ref[...]
        dy = dy_refs[0][...].astype(F32)
        for extra in dy_refs[1:]:
            dy = dy + extra[...].astype(F32)
        r = lax.rsqrt(jnp.mean(xv * xv, axis=-1, keepdims=True) + RMS_EPS)
        u = dy * g_ref[...]
        xr = xv * r
        dx = r * (u - xr * jnp.mean(u * xr, axis=-1, keepdims=True))
        if has_res:
            dx = dx + res_ref[...]
        dx_ref[...] = dx.astype(dx_ref.dtype)
        part = _row_sum8(dy * xr)

        @pl.when(i == 0)
        def _():
            acc_ref[...] = part

        @pl.when(i > 0)
        def _():
            acc_ref[...] += part

        @pl.when(i == n - 1)
        def _():
            dg_ref[...] = jnp.sum(acc_ref[...], axis=0, keepdims=True)

    row = pl.BlockSpec((t, d), lambda i: (i, 0))
    vec = pl.BlockSpec((1, d), lambda i: (0, 0))
    ins = [xin, g, *dys] + ([dres] if has_res else []) + deps
    return pl.pallas_call(
        body, name=name, grid=(n,),
        in_specs=[row, vec] + [row] * (n_dy + (1 if has_res else 0)) + [_DEP] * len(deps),
        out_specs=[row, vec],
        out_shape=[jax.ShapeDtypeStruct((s, d), out_dtype), jax.ShapeDtypeStruct((1, d), F32)],
        scratch_shapes=[pltpu.VMEM((8, d), F32)],
        compiler_params=_params(("arbitrary",)),
    )(*ins)


def _loss_head(y, target, name):
    s, d = y.shape
    t = _tile(s, 512)
    n = s // t

    def body(y_ref, t_ref, dy_ref, loss_ref, acc_ref):
        i = pl.program_id(0)
        e = y_ref[...] - t_ref[...]
        dy_ref[...] = e * (1.0 / d)
        part = _row_sum8(e * e)

        @pl.when(i == 0)
        def _():
            acc_ref[...] = part

        @pl.when(i > 0)
        def _():
            acc_ref[...] += part

        @pl.when(i == n - 1)
        def _():
            tot = jnp.sum(jnp.sum(acc_ref[...], axis=0, keepdims=True), axis=1, keepdims=True)
            loss_ref[...] = tot * (0.5 / d)

    row = pl.BlockSpec((t, d), lambda i: (i, 0))
    return pl.pallas_call(
        body, name=name, grid=(n,),
        in_specs=[row, row],
        out_specs=[row, pl.BlockSpec((1, 1), lambda i: (0, 0))],
        out_shape=[jax.ShapeDtypeStruct((s, d), F32), jax.ShapeDtypeStruct((1, 1), F32)],
        scratch_shapes=[pltpu.VMEM((8, d), F32)],
        compiler_params=_params(("arbitrary",)),
    )(y, target)


def _split3(x):
    hi = x.astype(BF16)
    r1 = x - hi.astype(F32)
    mid = r1.astype(BF16)
    lo = (r1 - mid.astype(F32)).astype(BF16)
    return hi, mid, lo


def _tri_dot(x, tri):
    hi, mid, lo = _split3(x)
    dn = _DIMS["nn"]
    out = lax.dot_general(hi, tri, dn, preferred_element_type=F32)
    out = out + lax.dot_general(mid, tri, dn, preferred_element_type=F32)
    return out + lax.dot_general(lo, tri, dn, preferred_element_type=F32)


def _log_sigmoid(z):
    return jnp.minimum(z, 0.0) - jnp.log(1.0 + jnp.exp(-jnp.abs(z)))


def _gate_fwd(f_row, b_col, name):
    rows, s = f_row.shape
    t = _tile(s, 512)
    n = s // t

    def body(f_ref, b_ref, ck_ref, carry_ref):
        i = pl.program_id(0)

        @pl.when(i == 0)
        def _():
            carry_ref[...] = jnp.zeros_like(carry_ref)

        logf = _log_sigmoid(f_ref[...] + b_ref[...])
        r = lax.broadcasted_iota(jnp.int32, (t, t), 0)
        c = lax.broadcasted_iota(jnp.int32, (t, t), 1)
        tri = jnp.where(r <= c, 1.0, 0.0).astype(BF16)
        cs = _tri_dot(logf, tri) + carry_ref[...]
        carry_ref[...] = cs[:, t - 1:t]
        terms = [part.astype(F32) for part in _split3(-cs)]
        sub = lax.broadcasted_iota(jnp.int32, (LANES, t), 0)
        for p in range(N_HEADS // 2):
            stacked = jnp.zeros((LANES, t), F32)
            for hh in range(2):
                for j, term in enumerate(terms):
                    h = 2 * p + hh
                    stacked = jnp.where(sub == 3 * hh + j, jnp.broadcast_to(term[h:h + 1, :], (LANES, t)), stacked)
            ck_ref[p] = stacked.T.astype(ck_ref.dtype)

    return pl.pallas_call(
        body, name=name, grid=(n,),
        in_specs=[pl.BlockSpec((rows, t), lambda i: (0, i)), pl.BlockSpec((rows, 1), lambda i: (0, 0))],
        out_specs=pl.BlockSpec((N_HEADS // 2, t, LANES), lambda i: (0, i, 0)),
        out_shape=jax.ShapeDtypeStruct((N_HEADS // 2, s, LANES), BF16),
        scratch_shapes=[pltpu.VMEM((rows, 1), F32)],
        compiler_params=_params(("arbitrary",)),
    )(f_row, b_col)


def _gate_bwd(f_row, b_col, dc_even, dc_odd, name):
    rows, s = f_row.shape
    t = _tile(s, 512)
    n = s // t

    def body(f_ref, b_ref, dce_ref, dco_ref, df_ref, db_ref, carry_ref, acc_ref):
        i = pl.program_id(0)

        @pl.when(i == 0)
        def _():
            carry_ref[...] = jnp.zeros_like(carry_ref)
            acc_ref[...] = jnp.zeros_like(acc_ref)

        head = lax.broadcasted_iota(jnp.int32, (rows, t), 0)
        dcv = jnp.zeros((rows, t), F32)
        for h in range(N_HEADS):
            src = dce_ref if h % 2 == 0 else dco_ref
            dcv = jnp.where(head == h, jnp.broadcast_to(src[h // 2, 0:1, :], (rows, t)), dcv)
        r = lax.broadcasted_iota(jnp.int32, (t, t), 0)
        c = lax.broadcasted_iota(jnp.int32, (t, t), 1)
        tri = jnp.where(r >= c, 1.0, 0.0).astype(BF16)
        dlogf = _tri_dot(dcv, tri) + carry_ref[...]
        carry_ref[...] = dlogf[:, 0:1]
        z = f_ref[...] + b_ref[...]
        df = dlogf * _sigmoid(-z)
        df_ref[...] = df.astype(df_ref.dtype)
        acc_ref[...] += jnp.sum(df, axis=1, keepdims=True)

        @pl.when(i == n - 1)
        def _():
            db_ref[...] = acc_ref[...]

    rev = lambda i: (0, n - 1 - i)
    dc_spec = pl.BlockSpec((N_HEADS // 2, 8, t), lambda i: (0, 0, n - 1 - i))
    return pl.pallas_call(
        body, name=name, grid=(n,),
        in_specs=[pl.BlockSpec((rows, t), rev), pl.BlockSpec((rows, 1), lambda i: (0, 0)), dc_spec, dc_spec],
        out_specs=[pl.BlockSpec((rows, t), rev), pl.BlockSpec((rows, 1), lambda i: (0, 0))],
        out_shape=[jax.ShapeDtypeStruct((rows, s), BF16), jax.ShapeDtypeStruct((rows, 1), F32)],
        scratch_shapes=[pltpu.VMEM((rows, 1), F32), pltpu.VMEM((rows, 1), F32)],
        compiler_params=_params(("arbitrary",)),
    )(f_row, b_col, dc_even, dc_odd)


_NEG = -1e30
_SCALE = HEAD_DIM ** -0.5


def _head_masks():
    lane = lax.broadcasted_iota(jnp.int32, (1, LANES), 1)
    return [lane < HEAD_DIM, lane >= HEAD_DIM]


def _attn_fwd(h, ck, name):
    s = h.shape[0]
    t = _tile(s, 512)
    n = s // t
    qb, kb, vb = OFF_Q // LANES, OFF_K // LANES, OFF_V // LANES

    pairs = [(qi, ki) for qi in range(n) for ki in range(qi + 1)]
    qi_tab = jnp.asarray([qi for qi, _ in pairs], jnp.int32)
    ki_tab = jnp.asarray([ki for _, ki in pairs], jnp.int32)

    def body(qi_ref, ki_ref, q_ref, k_ref, v_ref, ck_ref, o_ref, of_ref, lse_ref, m_ref, l_ref, acc_ref):
        qi, ki = qi_ref[pl.program_id(1)], ki_ref[pl.program_id(1)]
        masks = _head_masks()
        lane = lax.broadcasted_iota(jnp.int32, (1, LANES), 1)

        @pl.when(ki == 0)
        def _():
            m_ref[...] = jnp.full_like(m_ref, _NEG)
            l_ref[...] = jnp.zeros_like(l_ref)
            acc_ref[...] = jnp.zeros_like(acc_ref)

        def step(diag):
            q = q_ref[...] * _SCALE
            k_aug = jnp.concatenate([k_ref[...], ck_ref[0]], axis=1)
            v = v_ref[...]
            nq = max(1, t // 256)
            wq = t // nq
            chains = [(hh, j) for hh in range(2) for j in range(nq)]
            scores = []
            for hh, j in chains:
                qs = q[j * wq:(j + 1) * wq]
                ones = jnp.where((lane >= 3 * hh) & (lane < 3 * hh + 3), 1.0, 0.0).astype(q.dtype)
                q_aug = jnp.concatenate([jnp.where(masks[hh], qs, jnp.zeros_like(qs)),
                                         jnp.broadcast_to(ones, qs.shape)], axis=1)
                scores.append(lax.dot_general(k_aug, q_aug, _DIMS["nt"], preferred_element_type=F32))
            probs = []
            for (hh, j), sc in zip(chains, scores):
                cols = slice(j * wq, (j + 1) * wq)
                if diag:
                    r = lax.broadcasted_iota(jnp.int32, (t, wq), 0)
                    cc = lax.broadcasted_iota(jnp.int32, (t, wq), 1) + j * wq
                    sc = jnp.where(r <= cc, sc, _NEG)
                m_prev = m_ref[hh, :, cols]
                m_new = jnp.maximum(m_prev, jnp.max(sc, axis=0, keepdims=True))
                alpha = jnp.exp(m_prev - m_new)
                p = jnp.exp(sc - m_new)
                l_ref[hh, :, cols] = alpha * l_ref[hh, :, cols] + jnp.sum(p, axis=0, keepdims=True)
                m_ref[hh, :, cols] = m_new
                p_hi = p.astype(MXU_DTYPE)
                p_lo = (p - p_hi.astype(F32)).astype(MXU_DTYPE)
                probs.append((alpha, p_hi, p_lo))
            for (hh, j), (alpha, p_hi, p_lo) in zip(chains, probs):
                pv = (lax.dot_general(v, p_hi, _DIMS["tn"], preferred_element_type=F32)
                      + lax.dot_general(v, p_lo, _DIMS["tn"], preferred_element_type=F32))
                rows = slice(hh * HEAD_DIM, (hh + 1) * HEAD_DIM)
                cols = slice(j * wq, (j + 1) * wq)
                acc_ref[rows, cols] = alpha * acc_ref[rows, cols] + pv[rows]

        @pl.when(ki < qi)
        def _():
            step(False)

        @pl.when(ki == qi)
        def _():
            step(True)
            inv = jnp.concatenate([jnp.broadcast_to(1.0 / l_ref[hh], (HEAD_DIM, t)) for hh in range(2)], axis=0)
            out = (acc_ref[...] * inv).T
            o_ref[...] = out.astype(o_ref.dtype)
            of_ref[...] = out
            lse = jnp.concatenate([jnp.broadcast_to(m_ref[hh] + jnp.log(l_ref[hh]), (HEAD_DIM, t))
                                   for hh in range(2)], axis=0)
            lse_ref[...] = lse.T

    grid_spec = pltpu.PrefetchScalarGridSpec(
        num_scalar_prefetch=2, grid=(N_HEADS // 2, len(pairs)),
        in_specs=[
            pl.BlockSpec((t, LANES), lambda p, i, qt, kt: (qt[i], qb + p)),
            pl.BlockSpec((t, LANES), lambda p, i, qt, kt: (kt[i], kb + p)),
            pl.BlockSpec((t, LANES), lambda p, i, qt, kt: (kt[i], vb + p)),
            pl.BlockSpec((1, t, LANES), lambda p, i, qt, kt: (p, kt[i], 0)),
        ],
        out_specs=[pl.BlockSpec((t, LANES), lambda p, i, qt, kt: (qt[i], p))] * 3,
        scratch_shapes=[pltpu.VMEM((2, 1, t), F32), pltpu.VMEM((2, 1, t), F32), pltpu.VMEM((LANES, t), F32)])
    return pl.pallas_call(
        body, name=name, grid_spec=grid_spec,
        out_shape=[jax.ShapeDtypeStruct((s, D_ATT), BF16), jax.ShapeDtypeStruct((s, D_ATT), F32),
                   jax.ShapeDtypeStruct((s, D_ATT), F32)],
        compiler_params=_params(("parallel", "arbitrary")),
    )(qi_tab, ki_tab, h, h, h, ck)


def _attn_bwd(h, ck, o, lse, do, name):
    s = h.shape[0]
    t = _tile(s, 512)
    n = s // t
    qb, kb, vb = OFF_Q // LANES, OFF_K // LANES, OFF_V // LANES

    pairs = [(ki, qi) for ki in range(n) for qi in range(ki, n)]
    ki_tab = jnp.asarray([ki for ki, _ in pairs], jnp.int32)
    qi_tab = jnp.asarray([qi for _, qi in pairs], jnp.int32)

    def body(ki_ref, qi_ref, q_ref, k_ref, v_ref, ck_ref, o_ref, lse_ref, do_ref,
             dq_ref, dk_ref, dv_ref, dc0_ref, dc1_ref, dk_acc, dv_acc, dc_acc):
        ki, qi = ki_ref[pl.program_id(1)], qi_ref[pl.program_id(1)]
        masks = _head_masks()
        lane = lax.broadcasted_iota(jnp.int32, (1, LANES), 1)

        @pl.when((ki == 0) & (qi == 0))
        def _():
            dq_ref[...] = jnp.zeros_like(dq_ref)

        @pl.when(qi == ki)
        def _():
            dk_acc[...] = jnp.zeros_like(dk_acc)
            dv_acc[...] = jnp.zeros_like(dv_acc)
            dc_acc[...] = jnp.zeros_like(dc_acc)

        def step(diag):
            q = q_ref[...] * _SCALE
            k = k_ref[...]
            v = v_ref[...]
            dov = do_ref[...]
            k_aug = jnp.concatenate([k, ck_ref[0]], axis=1)
            prod_t = (dov.astype(F32) * o_ref[...]).T
            lse_t = lse_ref[...].T
            heads = []
            for hh in range(2):
                mk = masks[hh]
                qh = jnp.where(mk, q, jnp.zeros_like(q))
                kh = jnp.where(mk, k, jnp.zeros_like(k))
                doh = jnp.where(mk, dov, jnp.zeros_like(dov))
                ones = jnp.where((lane >= 3 * hh) & (lane < 3 * hh + 3), 1.0, 0.0).astype(q.dtype)
                q_aug = jnp.concatenate([qh, jnp.broadcast_to(ones, q.shape)], axis=1)
                sc = lax.dot_general(k_aug, q_aug, _DIMS["nt"], preferred_element_type=F32)
                dp = lax.dot_general(v, doh, _DIMS["nt"], preferred_element_type=F32)
                heads.append((qh, kh, doh, sc, dp))
            grads = []
            for hh, (qh, kh, doh, sc, dp) in enumerate(heads):
                rows = slice(hh * HEAD_DIM, (hh + 1) * HEAD_DIM)
                p = jnp.exp(sc - lse_t[hh * HEAD_DIM:hh * HEAD_DIM + 1, :])
                if diag:
                    r = lax.broadcasted_iota(jnp.int32, (t, t), 0)
                    cc = lax.broadcasted_iota(jnp.int32, (t, t), 1)
                    p = jnp.where(r <= cc, p, 0.0)
                delta = jnp.sum(prod_t[rows], axis=0, keepdims=True)
                ds = p * (dp - delta)
                dc_acc[hh] = dc_acc[hh] - jnp.sum(ds, axis=1, keepdims=True)
                grads.append((ds.astype(MXU_DTYPE), p.astype(MXU_DTYPE)))
            dq_blk = jnp.zeros((t, LANES), F32)
            for (qh, kh, doh, _, _), (dsb, pb) in zip(heads, grads):
                dv_acc[...] += lax.dot_general(pb, doh, _DIMS["nn"], preferred_element_type=F32)
                dk_acc[...] += lax.dot_general(dsb, qh, _DIMS["nn"], preferred_element_type=F32)
                dq_blk = dq_blk + lax.dot_general(dsb, kh, _DIMS["tn"], preferred_element_type=F32)
            rows_q = pl.ds(pl.multiple_of(qi * t, t), t)
            dq_ref[rows_q, :] = dq_ref[rows_q, :] + dq_blk * _SCALE

        @pl.when(qi > ki)
        def _():
            step(False)

        @pl.when(qi == ki)
        def _():
            step(True)

        @pl.when(qi == n - 1)
        def _():
            dk_ref[...] = dk_acc[...].astype(dk_ref.dtype)
            dv_ref[...] = dv_acc[...].astype(dv_ref.dtype)
            dc0_ref[0] = jnp.broadcast_to(dc_acc[0], (t, LANES)).T[0:8]
            dc1_ref[0] = jnp.broadcast_to(dc_acc[1], (t, LANES)).T[0:8]

    q_blk = lambda col: pl.BlockSpec((t, LANES), lambda p, i, kt, qt: (qt[i], col(p)))
    k_blk = lambda col: pl.BlockSpec((t, LANES), lambda p, i, kt, qt: (kt[i], col(p)))
    dc_blk = pl.BlockSpec((1, 8, t), lambda p, i, kt, qt: (p, 0, kt[i]))
    grid_spec = pltpu.PrefetchScalarGridSpec(
        num_scalar_prefetch=2, grid=(N_HEADS // 2, len(pairs)),
        in_specs=[q_blk(lambda p: qb + p), k_blk(lambda p: kb + p), k_blk(lambda p: vb + p),
                  pl.BlockSpec((1, t, LANES), lambda p, i, kt, qt: (p, kt[i], 0)),
                  q_blk(lambda p: p), q_blk(lambda p: p), q_blk(lambda p: p)],
        out_specs=[pl.BlockSpec((s, LANES), lambda p, i, kt, qt: (0, p)), k_blk(lambda p: p), k_blk(lambda p: p),
                   dc_blk, dc_blk],
        scratch_shapes=[pltpu.VMEM((t, LANES), F32), pltpu.VMEM((t, LANES), F32), pltpu.VMEM((2, t, 1), F32)])
    return pl.pallas_call(
        body, name=name, grid_spec=grid_spec,
        out_shape=[jax.ShapeDtypeStruct((s, D_ATT), F32), jax.ShapeDtypeStruct((s, D_ATT), BF16),
                   jax.ShapeDtypeStruct((s, D_ATT), BF16), jax.ShapeDtypeStruct((N_HEADS // 2, 8, s), F32),
                   jax.ShapeDtypeStruct((N_HEADS // 2, 8, s), F32)],
        compiler_params=_params(("parallel", "arbitrary")),
    )(ki_tab, qi_tab, h, h, h, ck, o, lse, do)


def _conv3(z, z_prev, w_ref):
    return (w_ref[2:3, :] * z + w_ref[1:2, :] * _shift_down(z, z_prev, 1)
            + w_ref[0:1, :] * _shift_down(z, z_prev, 2))


def _sconv_fwd(h, w, name):
    s = h.shape[0]
    t = _tile(s, 512)
    r = t // HALO
    c = D_CONV
    b_bg, b_cg, b_hc = OFF_BG // c, OFF_CG // c, OFF_HC // c

    def body(bg_ref, cg_ref, hc_ref, cgp_ref, hcp_ref, w_ref, y_ref):
        i = pl.program_id(0)
        live = (i > 0).astype(F32)
        z = cg_ref[...].astype(F32) * hc_ref[...].astype(F32)
        zp = cgp_ref[...].astype(F32) * hcp_ref[...].astype(F32) * live
        y_ref[...] = (bg_ref[...].astype(F32) * _conv3(z, zp, w_ref)).astype(y_ref.dtype)

    cur = lambda b: pl.BlockSpec((t, c), lambda i: (i, b))
    prev = lambda b: pl.BlockSpec((HALO, c), lambda i: (jnp.maximum(i * r - 1, 0), b))
    return pl.pallas_call(
        body, name=name, grid=(s // t,),
        in_specs=[cur(b_bg), cur(b_cg), cur(b_hc), prev(b_cg), prev(b_hc), pl.BlockSpec((8, c), lambda i: (0, 0))],
        out_specs=pl.BlockSpec((t, c), lambda i: (i, 0)),
        out_shape=jax.ShapeDtypeStruct((s, c), BF16),
        compiler_params=_params(("parallel",)),
    )(h, h, h, h, h, w)


def _sconv_bwd(h, w, dy, name):
    s = h.shape[0]
    t = _tile(s, 512)
    n = s // t
    r = t // HALO
    nh = s // HALO
    c = D_CONV
    b_bg, b_cg, b_hc = OFF_BG // c, OFF_CG // c, OFF_HC // c

    def body(bg_ref, cg_ref, hc_ref, cgp_ref, hcp_ref, bgn_ref, dy_ref, dyn_ref, w_ref, d_ref, dw_ref, acc_ref):
        i = pl.program_id(0)
        has_prev = (i > 0).astype(F32)
        has_next = (i < n - 1).astype(F32)
        bg = bg_ref[...].astype(F32)
        cg = cg_ref[...].astype(F32)
        hc = hc_ref[...].astype(F32)
        dyv = dy_ref[...].astype(F32)
        z = cg * hc
        zp = cgp_ref[...].astype(F32) * hcp_ref[...].astype(F32) * has_prev
        z1 = _shift_down(z, zp, 1)
        z2 = _shift_down(z, zp, 2)
        cz = w_ref[2:3, :] * z + w_ref[1:2, :] * z1 + w_ref[0:1, :] * z2
        dcz = dyv * bg
        dczn = dyn_ref[...].astype(F32) * bgn_ref[...].astype(F32) * has_next
        dz = (w_ref[2:3, :] * dcz + w_ref[1:2, :] * _shift_up(dcz, dczn, 1)
              + w_ref[0:1, :] * _shift_up(dcz, dczn, 2))
        d_ref[:, 0:c] = (dyv * cz).astype(d_ref.dtype)
        d_ref[:, c:2 * c] = (dz * hc).astype(d_ref.dtype)
        d_ref[:, 2 * c:3 * c] = (dz * cg).astype(d_ref.dtype)

        @pl.when(i == 0)
        def _():
            acc_ref[...] = jnp.zeros_like(acc_ref)

        acc_ref[0] += _row_sum8(dcz * z2)
        acc_ref[1] += _row_sum8(dcz * z1)
        acc_ref[2] += _row_sum8(dcz * z)

        @pl.when(i == n - 1)
        def _():
            rows = [jnp.sum(acc_ref[k], axis=0, keepdims=True) for k in range(3)]
            dw_ref[...] = jnp.concatenate(rows + [jnp.zeros((5, c), F32)], axis=0)

    cur = lambda b: pl.BlockSpec((t, c), lambda i: (i, b))
    prev = lambda b: pl.BlockSpec((HALO, c), lambda i: (jnp.maximum(i * r - 1, 0), b))
    nxt = lambda b: pl.BlockSpec((HALO, c), lambda i: (jnp.minimum((i + 1) * r, nh - 1), b))
    return pl.pallas_call(
        body, name=name, grid=(n,),
        in_specs=[cur(b_bg), cur(b_cg), cur(b_hc), prev(b_cg), prev(b_hc), nxt(b_bg),
                  cur(0), nxt(0), pl.BlockSpec((8, c), lambda i: (0, 0))],
        out_specs=[pl.BlockSpec((t, 3 * c), lambda i: (i, 0)), pl.BlockSpec((8, c), lambda i: (0, 0))],
        out_shape=[jax.ShapeDtypeStruct((s, 3 * c), BF16), jax.ShapeDtypeStruct((8, c), F32)],
        scratch_shapes=[pltpu.VMEM((3, 8, c), F32)],
        compiler_params=_params(("arbitrary",)),
    )(h, h, h, h, h, h, dy, dy, w)


def _group_masks():
    lane = lax.broadcasted_iota(jnp.int32, (1, D_SGU), 1)
    return [(lane >= g * HEAD_DIM) & (lane < (g + 1) * HEAD_DIM) for g in range(N_GROUPS)]


def _tril_weights(w_ref):
    r = lax.broadcasted_iota(jnp.int32, (CHUNK, CHUNK), 0)
    c = lax.broadcasted_iota(jnp.int32, (CHUNK, CHUNK), 1)
    return [jnp.where(r >= c, w_ref[g], 0.0).astype(MXU_DTYPE) for g in range(N_GROUPS)]


def _sgu_ln(vs, g_ref, b_ref):
    vg, dvg = _gelu_and_grad(vs)
    mu = jnp.mean(vg, axis=-1, keepdims=True)
    xc = vg - mu
    rstd = lax.rsqrt(jnp.mean(xc * xc, axis=-1, keepdims=True) + LN_EPS)
    xhat = xc * rstd
    return xhat * g_ref[...] + b_ref[...], xhat, rstd, dvg


def _sgu_fwd(h, ln_g, ln_b, w_s, bias, name):
    s = h.shape[0]
    t = _tile(s, 512)
    c = D_SGU
    b_u, b_v = OFF_U // c, OFF_VS // c

    def body(u_ref, v_ref, g_ref, b_ref, w_ref, bias_ref, y_ref):
        gm = _group_masks()
        wm = _tril_weights(w_ref)
        ug = _gelu(u_ref[...].astype(F32))
        vn, _, _, _ = _sgu_ln(v_ref[...].astype(F32), g_ref, b_ref)
        vnb = vn.astype(MXU_DTYPE)
        for ch in range(t // CHUNK):
            rows = slice(ch * CHUNK, (ch + 1) * CHUNK)
            mixed = bias_ref[...]
            for g in range(N_GROUPS):
                mg = lax.dot_general(wm[g], vnb[rows], _DIMS["nn"], preferred_element_type=F32)
                mixed = jnp.where(gm[g], mixed + mg, mixed)
            y_ref[rows, :] = (ug[rows] * mixed).astype(y_ref.dtype)

    full = lambda shp: pl.BlockSpec(shp, lambda i: (0,) * len(shp))
    return pl.pallas_call(
        body, name=name, grid=(s // t,),
        in_specs=[pl.BlockSpec((t, c), lambda i: (i, b_u)), pl.BlockSpec((t, c), lambda i: (i, b_v)),
                  full((1, c)), full((1, c)), full((N_GROUPS, CHUNK, CHUNK)), full((CHUNK, c))],
        out_specs=pl.BlockSpec((t, c), lambda i: (i, 0)),
        out_shape=jax.ShapeDtypeStruct((s, c), BF16),
        compiler_params=_params(("parallel",)),
    )(h, h, ln_g, ln_b, w_s, bias)


def _sgu_bwd(h, ln_g, ln_b, w_s, bias, dy, name):
    s = h.shape[0]
    t = _tile(s, 512)
    n = s // t
    c = D_SGU
    b_u, b_v = OFF_U // c, OFF_VS // c

    def body(u_ref, v_ref, g_ref, b_ref, w_ref, bias_ref, dy_ref,
             d_ref, dg_ref, db_ref, dw_ref, dbias_ref, dg_acc, db_acc):
        i = pl.program_id(0)
        gm = _group_masks()
        wm = _tril_weights(w_ref)

        @pl.when(i == 0)
        def _():
            dg_acc[...] = jnp.zeros_like(dg_acc)
            db_acc[...] = jnp.zeros_like(db_acc)
            dw_ref[...] = jnp.zeros_like(dw_ref)
            dbias_ref[...] = jnp.zeros_like(dbias_ref)

        ug, dug = _gelu_and_grad(u_ref[...].astype(F32))
        vn, xhat, rstd, dvg = _sgu_ln(v_ref[...].astype(F32), g_ref, b_ref)
        vnb = vn.astype(MXU_DTYPE)
        dyv = dy_ref[...].astype(F32)
        dmixed = dyv * ug
        dmb = dmixed.astype(MXU_DTYPE)
        dvn_parts = []
        for ch in range(t // CHUNK):
            rows = slice(ch * CHUNK, (ch + 1) * CHUNK)
            mixed = bias_ref[...]
            dvn = jnp.zeros((CHUNK, c), F32)
            for g in range(N_GROUPS):
                mg = lax.dot_general(wm[g], vnb[rows], _DIMS["nn"], preferred_element_type=F32)
                mixed = jnp.where(gm[g], mixed + mg, mixed)
                dvn = jnp.where(gm[g], lax.dot_general(wm[g], dmb[rows], _DIMS["tn"], preferred_element_type=F32),
                                dvn)
                dmg = jnp.where(gm[g], dmb[rows], jnp.zeros_like(dmb[rows]))
                dw_ref[g] += lax.dot_general(dmg, vnb[rows], _DIMS["nt"], preferred_element_type=F32)
            d_ref[rows, 0:c] = (dyv[rows] * mixed * dug[rows]).astype(d_ref.dtype)
            dbias_ref[...] += dmixed[rows]
            dvn_parts.append(dvn)
        dvn = jnp.concatenate(dvn_parts, axis=0)
        dg_acc[...] += _row_sum8(dvn * xhat)
        db_acc[...] += _row_sum8(dvn)
        dxh = dvn * g_ref[...]
        dvgl = rstd * (dxh - jnp.mean(dxh, axis=-1, keepdims=True)
                       - xhat * jnp.mean(dxh * xhat, axis=-1, keepdims=True))
        d_ref[:, c:2 * c] = (dvgl * dvg).astype(d_ref.dtype)

        @pl.when(i == n - 1)
        def _():
            dg_ref[...] = jnp.sum(dg_acc[...], axis=0, keepdims=True)
            db_ref[...] = jnp.sum(db_acc[...], axis=0, keepdims=True)
            r = lax.broadcasted_iota(jnp.int32, (CHUNK, CHUNK), 0)
            cc = lax.broadcasted_iota(jnp.int32, (CHUNK, CHUNK), 1)
            for g in range(N_GROUPS):
                dw_ref[g] = jnp.where(r >= cc, dw_ref[g], 0.0)

    full = lambda shp: pl.BlockSpec(shp, lambda i: (0,) * len(shp))
    return pl.pallas_call(
        body, name=name, grid=(n,),
        in_specs=[pl.BlockSpec((t, c), lambda i: (i, b_u)), pl.BlockSpec((t, c), lambda i: (i, b_v)),
                  full((1, c)), full((1, c)), full((N_GROUPS, CHUNK, CHUNK)), full((CHUNK, c)),
                  pl.BlockSpec((t, c), lambda i: (i, 0))],
        out_specs=[pl.BlockSpec((t, 2 * c), lambda i: (i, 0)), full((1, c)), full((1, c)),
                   full((N_GROUPS, CHUNK, CHUNK)), full((CHUNK, c))],
        out_shape=[jax.ShapeDtypeStruct((s, 2 * c), BF16), jax.ShapeDtypeStruct((1, c), F32),
                   jax.ShapeDtypeStruct((1, c), F32), jax.ShapeDtypeStruct((N_GROUPS, CHUNK, CHUNK), F32),
                   jax.ShapeDtypeStruct((CHUNK, c), F32)],
        scratch_shapes=[pltpu.VMEM((8, c), F32), pltpu.VMEM((8, c), F32)],
        compiler_params=_params(("arbitrary",)),
    )(h, h, ln_g, ln_b, w_s, bias, dy)


def _merge_fwd(h, acts, ws, b_gate, name):
    s = h.shape[0]
    d = D_MODEL
    t = _tile(s, 512)

    def body(gl0, gl1, gl2, a0, a1, a2, w0, w1, w2, b_ref, o_ref):
        acc = jnp.zeros((t, d), F32)
        for i, (gl, a, w) in enumerate(((gl0, a0, w0), (gl1, a1, w1), (gl2, a2, w2))):
            y = lax.dot_general(a[...], w[...], _DIMS["nn"], preferred_element_type=F32)
            acc = acc + _sigmoid(gl[...].astype(F32) + b_ref[i:i + 1, :]) * y
        o_ref[...] = acc.astype(o_ref.dtype)

    full = lambda arr: pl.BlockSpec(arr.shape, lambda i: (0, 0))
    return pl.pallas_call(
        body, name=name, grid=(s // t,),
        in_specs=[pl.BlockSpec((t, d), lambda i, b=b: (i, b)) for b in range(3)]
                 + [pl.BlockSpec((t, a.shape[1]), lambda i: (i, 0)) for a in acts]
                 + [full(w) for w in ws] + [full(b_gate)],
        out_specs=pl.BlockSpec((t, d), lambda i: (i, 0)),
        out_shape=jax.ShapeDtypeStruct((s, d), BF16),
        compiler_params=_params(("parallel",)),
    )(h, h, h, *acts, *ws, b_gate)


def _merge_bwd(h, acts, ws, b_gate, dmerged, name):
    s = h.shape[0]
    d = D_MODEL
    t = _tile(s, 512)
    n = s // t

    def body(gl0, gl1, gl2, a0, a1, a2, w0, w1, w2, b_ref, dm_ref, dy0, dy1, dy2, dgl_ref, db_ref, acc_ref):
        step = pl.program_id(0)

        @pl.when(step == 0)
        def _():
            acc_ref[...] = jnp.zeros_like(acc_ref)

        dm = dm_ref[...]
        for i, (gl, a, w, dy) in enumerate(((gl0, a0, w0, dy0), (gl1, a1, w1, dy1), (gl2, a2, w2, dy2))):
            y = lax.dot_general(a[...], w[...], _DIMS["nn"], preferred_element_type=F32)
            gate = _sigmoid(gl[...].astype(F32) + b_ref[i:i + 1, :])
            dy[...] = (dm * gate).astype(dy.dtype)
            dgl = dm * y * (gate * (1.0 - gate))
            dgl_ref[:, i * d:(i + 1) * d] = dgl.astype(dgl_ref.dtype)
            acc_ref[i] += _row_sum8(dgl)

        @pl.when(step == n - 1)
        def _():
            rows = [jnp.sum(acc_ref[k], axis=0, keepdims=True) for k in range(3)]
            db_ref[...] = jnp.concatenate(rows + [jnp.zeros((5, d), F32)], axis=0)

    full = lambda arr: pl.BlockSpec(arr.shape, lambda i: (0, 0))
    row = pl.BlockSpec((t, d), lambda i: (i, 0))
    return pl.pallas_call(
        body, name=name, grid=(n,),
        in_specs=[pl.BlockSpec((t, d), lambda i, b=b: (i, b)) for b in range(3)]
                 + [pl.BlockSpec((t, a.shape[1]), lambda i: (i, 0)) for a in acts]
                 + [full(w) for w in ws] + [full(b_gate), row],
        out_specs=[row, row, row, pl.BlockSpec((t, 3 * d), lambda i: (i, 0)), pl.BlockSpec((8, d), lambda i: (0, 0))],
        out_shape=[jax.ShapeDtypeStruct((s, d), BF16)] * 3
                  + [jax.ShapeDtypeStruct((s, 3 * d), BF16), jax.ShapeDtypeStruct((8, d), F32)],
        scratch_shapes=[pltpu.VMEM((3, 8, d), F32)],
        compiler_params=_params(("arbitrary",)),
    )(h, h, h, *acts, *ws, b_gate, dmerged)


FF_BLK = D_FF // 2


def _ffn_act_fwd(h2, w, name):
    s = h2.shape[0]
    t = _tile(s, 512)
    r = t // HALO
    cw = 2 * FF_BLK

    def body(x_ref, xp_ref, w_ref, p_ref):
        i = pl.program_id(0)
        live = (i > 0).astype(F32)
        hc = _conv3(x_ref[...].astype(F32), xp_ref[...].astype(F32) * live, w_ref)
        p_ref[...] = (_gelu(hc[:, :FF_BLK]) * hc[:, FF_BLK:]).astype(p_ref.dtype)

    return pl.pallas_call(
        body, name=name, grid=(s // t, 2),
        in_specs=[pl.BlockSpec((t, cw), lambda i, j: (i, j)),
                  pl.BlockSpec((HALO, cw), lambda i, j: (jnp.maximum(i * r - 1, 0), j)),
                  pl.BlockSpec((8, cw), lambda i, j: (0, j))],
        out_specs=pl.BlockSpec((t, FF_BLK), lambda i, j: (i, j)),
        out_shape=jax.ShapeDtypeStruct((s, D_FF), BF16),
        compiler_params=_params(("parallel", "parallel")),
    )(h2, h2, w)


def _ffn_act_conv_bwd(h2, w, dp, name):
    s = h2.shape[0]
    t = _tile(s, 512)
    n = s // t
    r = t // HALO
    nh = s // HALO
    cw = 2 * FF_BLK

    def body(x_ref, xp_ref, xn_ref, dp_ref, dpn_ref, w_ref, dx_ref, dw_ref, acc_ref):
        i = pl.program_id(1)
        has_prev = (i > 0).astype(F32)
        has_next = (i < n - 1).astype(F32)
        x = jnp.concatenate([x_ref[...].astype(F32), xn_ref[...].astype(F32)], axis=0)
        xp = xp_ref[...].astype(F32) * has_prev
        x1 = _shift_down(x, xp, 1)
        x2 = _shift_down(x, xp, 2)
        hc = w_ref[2:3, :] * x + w_ref[1:2, :] * x1 + w_ref[0:1, :] * x2
        ga, dga = _gelu_and_grad(hc[:, :FF_BLK])
        dpv = jnp.concatenate([dp_ref[...].astype(F32), dpn_ref[...].astype(F32) * has_next], axis=0)
        dhc = jnp.concatenate([dpv * hc[:, FF_BLK:] * dga, dpv * ga], axis=1)
        cur, nxt = dhc[:t], dhc[t:]
        dx = w_ref[2:3, :] * cur + w_ref[1:2, :] * _shift_up(cur, nxt, 1) + w_ref[0:1, :] * _shift_up(cur, nxt, 2)
        dx_ref[...] = dx.astype(dx_ref.dtype)

        @pl.when(i == 0)
        def _():
            acc_ref[...] = jnp.zeros_like(acc_ref)

        acc_ref[0] += _row_sum8(cur * x2[:t])
        acc_ref[1] += _row_sum8(cur * x1[:t])
        acc_ref[2] += _row_sum8(cur * x[:t])

        @pl.when(i == n - 1)
        def _():
            rows = [jnp.sum(acc_ref[k], axis=0, keepdims=True) for k in range(3)]
            dw_ref[...] = jnp.concatenate(rows + [jnp.zeros((5, cw), F32)], axis=0)

    nxt_row = lambda j, i: jnp.minimum((i + 1) * r, nh - 1)
    return pl.pallas_call(
        body, name=name, grid=(2, n),
        in_specs=[pl.BlockSpec((t, cw), lambda j, i: (i, j)),
                  pl.BlockSpec((HALO, cw), lambda j, i: (jnp.maximum(i * r - 1, 0), j)),
                  pl.BlockSpec((HALO, cw), lambda j, i: (nxt_row(j, i), j)),
                  pl.BlockSpec((t, FF_BLK), lambda j, i: (i, j)),
                  pl.BlockSpec((HALO, FF_BLK), lambda j, i: (nxt_row(j, i), j)),
                  pl.BlockSpec((8, cw), lambda j, i: (0, j))],
        out_specs=[pl.BlockSpec((t, cw), lambda j, i: (i, j)), pl.BlockSpec((8, cw), lambda j, i: (0, j))],
        out_shape=[jax.ShapeDtypeStruct((s, 2 * D_FF), BF16), jax.ShapeDtypeStruct((8, 2 * D_FF), F32)],
        scratch_shapes=[pltpu.VMEM((3, 8, cw), F32)],
        compiler_params=_params(("parallel", "arbitrary")),
    )(h2, h2, h2, dp, dp, w)


def _adamw(w, g, m, v, name):
    shape = w.shape
    c = shape[-1]
    rows = math.prod(shape[:-1])
    to2d = lambda a: a.reshape(rows, c)
    cap = max(8, (1 << 18) // c)
    tr = rows
    for cand in (2048, 1024, 512, 256, 128, 64, 32, 16, 8):
        if cand <= cap and rows % cand == 0:
            tr = cand
            break

    def body(w_ref, g_ref, m_ref, v_ref, d_ref, nm_ref, nv_ref):
        gv = g_ref[...]
        nm = ADAM_B1 * m_ref[...] + (1.0 - ADAM_B1) * gv
        nv = ADAM_B2 * v_ref[...] + (1.0 - ADAM_B2) * (gv * gv)
        m_hat = nm / (1.0 - ADAM_B1 ** ADAM_STEP)
        v_hat = nv / (1.0 - ADAM_B2 ** ADAM_STEP)
        d_ref[...] = -ADAM_LR * (m_hat / (jnp.sqrt(v_hat) + ADAM_EPS) + ADAM_WD * w_ref[...])
        nm_ref[...] = nm
        nv_ref[...] = nv

    blk = pl.BlockSpec((tr, c), lambda i: (i, 0))
    outs = pl.pallas_call(
        body, name=name, grid=(rows // tr,),
        in_specs=[blk] * 4, out_specs=[blk] * 3,
        out_shape=[jax.ShapeDtypeStruct((rows, c), F32)] * 3,
        compiler_params=_params(("parallel",)),
    )(to2d(w), to2d(g), to2d(m), to2d(v))
    return tuple(o.reshape(shape) for o in outs)


_ANY = pl.BlockSpec(memory_space=pl.ANY)


def _place():
    x, y, c = lax.axis_index("x"), lax.axis_index("y"), lax.axis_index("c")
    others = [(1 - x, y), (x, 1 - y), (1 - x, 1 - y)]
    return x, y, c, others


def _all_gather_chips(shard, name):
    rws, cols = shard.shape
    half = rws // 2

    def body(x_ref, out_ref, send_sems, recv_sems, local_sem):
        x, y, c, others = _place()
        me = 2 * x + y
        sib = (x, y, 1 - c)

        def rows(chip, cc):
            return out_ref.at[chip, pl.ds(pl.multiple_of(cc * half, 16), half), :]

        def copy(k, src, dst, to):
            return pltpu.make_async_remote_copy(src_ref=src, dst_ref=dst, send_sem=send_sems.at[k],
                                                recv_sem=recv_sems.at[k], device_id=to, device_id_type=MESH)

        mine = pltpu.make_async_copy(x_ref, out_ref.at[me], local_sem)
        mine.start()
        my_half = x_ref.at[pl.ds(pl.multiple_of(c * half, 16), half), :]
        first = [copy(j, my_half, rows(me, c), (ox, oy, c)) for j, (ox, oy) in enumerate(others)]
        for cp in first:
            cp.start()
        passed = []
        for j, (ox, oy) in enumerate(others):
            blk = rows(2 * ox + oy, c)
            copy(j, blk, blk, (x, y, c)).wait_recv()
            fwd = copy(3 + j, blk, blk, sib)
            fwd.start()
            passed.append(fwd)
        for j, (ox, oy) in enumerate(others):
            blk = rows(2 * ox + oy, 1 - c)
            copy(3 + j, blk, blk, (x, y, c)).wait_recv()
        for cp in first + passed:
            cp.wait_send()
        mine.wait()

    return pl.pallas_call(
        body, name=name,
        in_specs=[_ANY], out_specs=_ANY,
        out_shape=jax.ShapeDtypeStruct((N_CHIPS, rws, cols), shard.dtype),
        scratch_shapes=[pltpu.SemaphoreType.DMA((6,)), pltpu.SemaphoreType.DMA((6,)), pltpu.SemaphoreType.DMA],
        compiler_params=pltpu.CompilerParams(has_side_effects=True),
    )(shard)


def _swap_halves(buf, name):
    nb, rws, cols = buf.shape
    half = rws // 2

    def body(b_ref, own_ref, sib_ref, send_sem, recv_sem, local_sem):
        x, y, c, _ = _place()
        keep = b_ref.at[:, pl.ds(pl.multiple_of(c * half, 16), half), :]
        give = b_ref.at[:, pl.ds(pl.multiple_of((1 - c) * half, 16), half), :]
        mine = pltpu.make_async_copy(keep, own_ref, local_sem)
        mine.start()
        cp = pltpu.make_async_remote_copy(src_ref=give, dst_ref=sib_ref, send_sem=send_sem, recv_sem=recv_sem,
                                          device_id=(x, y, 1 - c), device_id_type=MESH)
        cp.start()
        cp.wait()
        mine.wait()

    shp = jax.ShapeDtypeStruct((nb, half, cols), buf.dtype)
    return pl.pallas_call(
        body, name=name,
        in_specs=[_ANY], out_specs=[_ANY, _ANY], out_shape=[shp, shp],
        scratch_shapes=[pltpu.SemaphoreType.DMA, pltpu.SemaphoreType.DMA, pltpu.SemaphoreType.DMA],
        compiler_params=pltpu.CompilerParams(has_side_effects=True),
    )(buf)


def _add2(a, b, name):
    nb, rws, cols = a.shape
    t = _tile(rws, 256)
    if rws % t:
        t = rws

    def body(a_ref, b_ref, o_ref):
        o_ref[...] = (a_ref[...].astype(F32) + b_ref[...].astype(F32)).astype(o_ref.dtype)

    blk = pl.BlockSpec((1, t, cols), lambda i, j: (i, j, 0))
    return pl.pallas_call(
        body, name=name, grid=(nb, rws // t), in_specs=[blk, blk], out_specs=blk,
        out_shape=jax.ShapeDtypeStruct(a.shape, a.dtype),
        compiler_params=_params(("parallel", "parallel")),
    )(a, b)


def _exchange_chips(pre, name):
    nb, half, cols = pre.shape

    def body(p_ref, out_ref, send_sems, recv_sems, local_sem):
        x, y, c, others = _place()
        me = 2 * x + y
        mine = pltpu.make_async_copy(p_ref.at[me], out_ref.at[me], local_sem)
        mine.start()
        sends = []
        for j, (ox, oy) in enumerate(others):
            cp = pltpu.make_async_remote_copy(src_ref=p_ref.at[2 * ox + oy], dst_ref=out_ref.at[me],
                                              send_sem=send_sems.at[j], recv_sem=recv_sems.at[j],
                                              device_id=(ox, oy, c), device_id_type=MESH)
            cp.start()
            sends.append(cp)
        for j, (ox, oy) in enumerate(others):
            blk = out_ref.at[2 * ox + oy]
            pltpu.make_async_remote_copy(src_ref=blk, dst_ref=blk, send_sem=send_sems.at[j],
                                         recv_sem=recv_sems.at[j], device_id=(x, y, c),
                                         device_id_type=MESH).wait_recv()
        for cp in sends:
            cp.wait_send()
        mine.wait()

    return pl.pallas_call(
        body, name=name,
        in_specs=[_ANY], out_specs=_ANY, out_shape=jax.ShapeDtypeStruct(pre.shape, pre.dtype),
        scratch_shapes=[pltpu.SemaphoreType.DMA((3,)), pltpu.SemaphoreType.DMA((3,)), pltpu.SemaphoreType.DMA],
        compiler_params=pltpu.CompilerParams(has_side_effects=True),
    )(pre)


def _add4(parts, name):
    nb, half, cols = parts.shape
    t = _tile(half, 256)
    if half % t:
        t = half

    def body(p_ref, o_ref):
        acc = p_ref[0].astype(F32)
        for k in range(1, nb):
            acc = acc + p_ref[k].astype(F32)
        o_ref[...] = acc

    return pl.pallas_call(
        body, name=name, grid=(half // t,),
        in_specs=[pl.BlockSpec((nb, t, cols), lambda i: (0, i, 0))],
        out_specs=pl.BlockSpec((t, cols), lambda i: (i, 0)),
        out_shape=jax.ShapeDtypeStruct((half, cols), F32),
        compiler_params=_params(("parallel",)),
    )(parts)


def _join_halves(mine_half, name):
    half, cols = mine_half.shape

    def body(h_ref, out_ref, send_sem, recv_sem, local_sem):
        x, y, c, _ = _place()
        dst = out_ref.at[pl.ds(pl.multiple_of(c * half, 8), half), :]
        mine = pltpu.make_async_copy(h_ref, dst, local_sem)
        mine.start()
        cp = pltpu.make_async_remote_copy(src_ref=h_ref, dst_ref=dst, send_sem=send_sem, recv_sem=recv_sem,
                                          device_id=(x, y, 1 - c), device_id_type=MESH)
        cp.start()
        cp.wait()
        mine.wait()

    return pl.pallas_call(
        body, name=name,
        in_specs=[_ANY], out_specs=_ANY, out_shape=jax.ShapeDtypeStruct((2 * half, cols), mine_half.dtype),
        scratch_shapes=[pltpu.SemaphoreType.DMA, pltpu.SemaphoreType.DMA, pltpu.SemaphoreType.DMA],
        compiler_params=pltpu.CompilerParams(has_side_effects=True),
    )(mine_half)


def _reduce_scatter_chips(buf, tag):
    own, sib = _swap_halves(buf, "rs_swap_" + tag)
    pre = _add2(own, sib, "rs_add2_" + tag)
    parts = _exchange_chips(pre, "rs_xchg_" + tag)
    red = _add4(parts, "rs_add4_" + tag)
    return _join_halves(red, "rs_join_" + tag)


MAX_DMA_BYTES = 2 * 1024 * 1024
ROW_ALIGN = 16


def _pieces(rows, row_bytes):
    n = max(1, -(-(rows * row_bytes) // MAX_DMA_BYTES))
    step = -(-(-(-rows // n)) // ROW_ALIGN) * ROW_ALIGN
    return [(r, min(step, rows - r)) for r in range(0, rows, step)]


def _half_plan(arrays, row_axis):
    plan = []
    for a, arr in enumerate(arrays):
        row_bytes = math.prod(arr.shape[row_axis + 1:]) * arr.dtype.itemsize * (arr.shape[0] if row_axis else 1)
        plan += [(a, r0, nr) for r0, nr in _pieces(arr.shape[row_axis] // 2, row_bytes)]
    return plan


def _rows(start, size):
    return pl.ds(pl.multiple_of(start, ROW_ALIGN), size)


def _remote(src, dst, send_sems, recv_sems, k, to):
    return pltpu.make_async_remote_copy(src_ref=src, dst_ref=dst, send_sem=send_sems.at[k], recv_sem=recv_sems.at[k],
                                        device_id=to, device_id_type=MESH)


def _comm_call(body, name, ins, out_shapes, n_remote, n_local, aliases=None):
    return pl.pallas_call(
        body, name=name,
        in_specs=[_ANY] * len(ins), out_specs=[_ANY] * len(out_shapes), out_shape=out_shapes,
        scratch_shapes=[pltpu.SemaphoreType.DMA((n_remote,)), pltpu.SemaphoreType.DMA((n_remote,)),
                        pltpu.SemaphoreType.DMA((max(n_local, 1),))],
        input_output_aliases=aliases or {},
        compiler_params=pltpu.CompilerParams(has_side_effects=True),
    )(*ins)


def _cast_shard(w, l, me_idx, name):
    _, k, cols = w.shape
    tr = _tile(k, 256)
    if k % tr:
        tr = k

    def body(me_ref, w_ref, s_ref, land_ref):
        del me_ref
        v = w_ref[...].astype(BF16)
        s_ref[...] = v
        land_ref[...] = v

    grid_spec = pltpu.PrefetchScalarGridSpec(
        num_scalar_prefetch=1, grid=(k // tr,),
        in_specs=[pl.BlockSpec((None, tr, cols), lambda i, me: (l, i, 0))],
        out_specs=[pl.BlockSpec((tr, cols), lambda i, me: (i, 0)),
                   pl.BlockSpec((None, tr, cols), lambda i, me: (me[0], i, 0))])
    return pl.pallas_call(
        body, name=name, grid_spec=grid_spec,
        out_shape=[jax.ShapeDtypeStruct((k, cols), BF16), jax.ShapeDtypeStruct((N_CHIPS, k, cols), BF16)],
        compiler_params=_params(("parallel",)),
    )(me_idx, w)


def _gather_d2d(lands, name):
    n = len(lands)
    plan = _half_plan(lands, 1)
    plan = [(a, r0, nr) for a, r0, nr in plan]

    def body(*refs):
        out_refs = refs[n:2 * n]
        send_sems, recv_sems, _ = refs[2 * n:]
        x, y, c, others = _place()
        sends = []
        for i, (a, r0, nr) in enumerate(plan):
            rows = _rows(c * (lands[a].shape[1] // 2) + r0, nr)
            for j, (ox, oy) in enumerate(others):
                blk = out_refs[a].at[2 * ox + oy, rows, :]
                cp = _remote(blk, blk, send_sems, recv_sems, 3 * i + j, (x, y, 1 - c))
                cp.start()
                sends.append(cp)
        for i, (a, r0, nr) in enumerate(plan):
            rows = _rows((1 - c) * (lands[a].shape[1] // 2) + r0, nr)
            for j, (ox, oy) in enumerate(others):
                blk = out_refs[a].at[2 * ox + oy, rows, :]
                _remote(blk, blk, send_sems, recv_sems, 3 * i + j, (x, y, c)).wait_recv()
        for cp in sends:
            cp.wait_send()

    outs = [jax.ShapeDtypeStruct(a.shape, a.dtype) for a in lands]
    return _comm_call(body, name, lands, outs, 3 * len(plan), 0, aliases={a: a for a in range(n)})


def _rs_swap(ts, name):
    n = len(ts)
    plan = _half_plan(ts, 1)

    def body(*refs):
        t_refs, out_refs = refs[:n], refs[n:2 * n]
        send_sems, recv_sems, _ = refs[2 * n:]
        x, y, c, _o = _place()
        sends = []
        for i, (a, r0, nr) in enumerate(plan):
            src = t_refs[a].at[:, _rows((1 - c) * (ts[a].shape[1] // 2) + r0, nr), :]
            cp = _remote(src, out_refs[a].at[:, pl.ds(r0, nr), :], send_sems, recv_sems, i, (x, y, 1 - c))
            cp.start()
            sends.append(cp)
        for i, (a, r0, nr) in enumerate(plan):
            blk = out_refs[a].at[:, pl.ds(r0, nr), :]
            _remote(blk, blk, send_sems, recv_sems, i, (x, y, c)).wait_recv()
        for cp in sends:
            cp.wait_send()

    outs = [jax.ShapeDtypeStruct((t.shape[0], t.shape[1] // 2, t.shape[2]), t.dtype) for t in ts]
    return _comm_call(body, name, ts, outs, len(plan), 0)


def _add_halves(ts, gots, c_idx, me_idx, name):
    n = len(ts)

    def body(c_ref, me_ref, *refs):
        del c_ref
        t_refs, g_refs = refs[:n], refs[n:2 * n]
        o_refs, mine_refs = refs[2 * n:3 * n], refs[3 * n:]
        for t_ref, g_ref, o_ref, mine_ref in zip(t_refs, g_refs, o_refs, mine_refs):
            v = (t_ref[...].astype(F32) + g_ref[...].astype(F32)).astype(o_ref.dtype)
            o_ref[...] = v

            @pl.when(pl.program_id(0) == me_ref[0])
            def _():
                mine_ref[...] = v

    blks = [(1, g.shape[1], g.shape[2]) for g in gots]
    same = [pl.BlockSpec(b, lambda i, c, me: (i, 0, 0)) for b in blks]
    grid_spec = pltpu.PrefetchScalarGridSpec(
        num_scalar_prefetch=2, grid=(N_CHIPS,),
        in_specs=[pl.BlockSpec(b, lambda i, c, me: (i, c[0], 0)) for b in blks] + same,
        out_specs=same + [pl.BlockSpec(b, lambda i, c, me: (me[0], 0, 0)) for b in blks])
    shapes = [jax.ShapeDtypeStruct(g.shape, g.dtype) for g in gots]
    outs = pl.pallas_call(
        body, name=name, grid_spec=grid_spec, out_shape=shapes + shapes,
        compiler_params=_params(("arbitrary",)),
    )(c_idx, me_idx, *ts, *gots)
    return outs[:n], outs[n:]


def _add4_halves(parts, c_idx, name):
    n = len(parts)
    steps = 2

    def body(c_ref, *refs):
        del c_ref
        for p_ref, o_ref in zip(refs[:n], refs[n:]):
            acc = p_ref[0].astype(F32)
            for k in range(1, N_CHIPS):
                acc = acc + p_ref[k].astype(F32)
            o_ref[...] = acc

    grid_spec = pltpu.PrefetchScalarGridSpec(
        num_scalar_prefetch=1, grid=(steps,),
        in_specs=[pl.BlockSpec((N_CHIPS, p.shape[1] // steps, p.shape[2]), lambda i, c: (0, i, 0)) for p in parts],
        out_specs=[pl.BlockSpec((p.shape[1] // steps, p.shape[2]), lambda i, c: (c[0] * steps + i, 0))
                   for p in parts])
    return pl.pallas_call(
        body, name=name, grid_spec=grid_spec,
        out_shape=[jax.ShapeDtypeStruct((2 * p.shape[1], p.shape[2]), F32) for p in parts],
        compiler_params=_params(("parallel",)),
    )(c_idx, *parts)


def _rs_join(fulls, name):
    n = len(fulls)
    plan = _half_plan(fulls, 0)

    def body(*refs):
        out_refs = refs[n:2 * n]
        send_sems, recv_sems, _ = refs[2 * n:]
        x, y, c, _o = _place()
        sends = []
        for i, (a, r0, nr) in enumerate(plan):
            blk = out_refs[a].at[_rows(c * (fulls[a].shape[0] // 2) + r0, nr), :]
            cp = _remote(blk, blk, send_sems, recv_sems, i, (x, y, 1 - c))
            cp.start()
            sends.append(cp)
        for i, (a, r0, nr) in enumerate(plan):
            blk = out_refs[a].at[_rows((1 - c) * (fulls[a].shape[0] // 2) + r0, nr), :]
            _remote(blk, blk, send_sems, recv_sems, i, (x, y, c)).wait_recv()
        for cp in sends:
            cp.wait_send()

    outs = [jax.ShapeDtypeStruct(f.shape, f.dtype) for f in fulls]
    return _comm_call(body, name, fulls, outs, len(plan), 0, aliases={a: a for a in range(n)})


_HBM = pl.BlockSpec(memory_space=pltpu.HBM)
_SEM = pl.BlockSpec(memory_space=pltpu.SEMAPHORE)
_EFFECT = pltpu.SideEffectType.DATAFLOW_SIDE_EFFECTING


def _ici_plan(kind, a_list):
    if kind == "gather":
        return _half_plan(a_list, 0)
    plan = []
    for a, p in enumerate(a_list):
        plan += [(a, r0, nr) for r0, nr in _pieces(p.shape[1], p.shape[2] * p.dtype.itemsize)]
    return plan


def _ici_refs(kind, a_ref, b_ref, a_shape, r0, nr, c, me, peer):
    if kind == "gather":
        rows = _rows(c * (a_shape[0] // 2) + r0, nr)
        return a_ref.at[rows, :], b_ref.at[me, rows, :], b_ref.at[peer, rows, :]
    rows = pl.ds(r0, nr)
    return a_ref.at[peer, rows, :], b_ref.at[me, rows, :], b_ref.at[peer, rows, :]


def _ici_start(kind, a_list, b_list, name):
    n = len(a_list)
    plan = _ici_plan(kind, a_list)
    shapes = [a.shape for a in a_list]

    def body(*refs):
        a_refs, b_refs = refs[:n], refs[n:2 * n]
        send_sems, recv_sems = refs[2 * n], refs[2 * n + 1]
        token = refs[4 * n + 2]
        x, y, c, others = _place()
        me = 2 * x + y
        for i, (a, r0, nr) in enumerate(plan):
            for j, (ox, oy) in enumerate(others):
                src, dst, _ = _ici_refs(kind, a_refs[a], b_refs[a], shapes[a], r0, nr, c, me, 2 * ox + oy)
                _remote(src, dst, send_sems, recv_sems, 3 * i + j, (ox, oy, c)).start()
        token[...] = jnp.zeros_like(token)

    hbm = lambda v: pltpu.HBM(v.shape, v.dtype)
    ncp = 3 * len(plan)
    outs = pl.pallas_call(
        body, name=name,
        in_specs=[_HBM] * (2 * n),
        out_specs=[_SEM, _SEM] + [_HBM] * (2 * n) + [pl.BlockSpec(memory_space=pltpu.VMEM)],
        out_shape=[pltpu.SemaphoreType.DMA((ncp,)), pltpu.SemaphoreType.DMA((ncp,))]
                  + [hbm(v) for v in a_list] + [hbm(v) for v in b_list] + [jax.ShapeDtypeStruct((8, LANES), F32)],
        input_output_aliases={i: 2 + i for i in range(2 * n)},
        compiler_params=pltpu.CompilerParams(has_side_effects=_EFFECT),
    )(*[pltpu.with_memory_space_constraint(v, pltpu.HBM) for v in list(a_list) + list(b_list)])
    return outs[0], outs[1], outs[2:2 + n], outs[2 + n:2 + 2 * n], outs[2 + 2 * n]


def _ici_wait(kind, started, after, name):
    send_sems, recv_sems, a_list, b_list, _ = started
    n = len(a_list)
    plan = _ici_plan(kind, a_list)
    shapes = [a.shape for a in a_list]

    def body(*refs):
        a_refs, b_refs = refs[:n], refs[n:2 * n]
        send_sems, recv_sems = refs[2 * n], refs[2 * n + 1]
        x, y, c, others = _place()
        me = 2 * x + y
        for i, (a, r0, nr) in enumerate(plan):
            for j, (ox, oy) in enumerate(others):
                src, dst, land = _ici_refs(kind, a_refs[a], b_refs[a], shapes[a], r0, nr, c, me, 2 * ox + oy)
                _remote(src, dst, send_sems, recv_sems, 3 * i + j, (ox, oy, c)).wait_send()
                _remote(land, land, send_sems, recv_sems, 3 * i + j, (x, y, c)).wait_recv()

    hbm = lambda v: pltpu.HBM(v.shape, v.dtype)
    outs = pl.pallas_call(
        body, name=name,
        in_specs=[_HBM] * (2 * n) + [_SEM, _SEM, _ANY],
        out_specs=[_HBM] * (2 * n),
        out_shape=[hbm(v) for v in a_list] + [hbm(v) for v in b_list],
        input_output_aliases={i: i for i in range(2 * n)},
        compiler_params=pltpu.CompilerParams(has_side_effects=_EFFECT),
    )(*a_list, *b_list, send_sems, recv_sems, after)
    return outs[n:]


def _rs_begin(ts, c_idx, me_idx, tag):
    got = _rs_swap(ts, "rs_swap_" + tag)
    pres, mine = _add_halves(ts, got, c_idx, me_idx, "rs_add2_" + tag)
    return _ici_start("scatter", pres, mine, "rs_xchg_start_" + tag)


def _rs_finish(started, after, c_idx, tag):
    parts = _ici_wait("scatter", started, after, "rs_xchg_wait_" + tag)
    return _rs_join(_add4_halves(parts, c_idx, "rs_add4_" + tag), "rs_join_" + tag)


def _pack_rows(pieces, rows, dtype):
    flat = jnp.concatenate([p.astype(dtype).reshape(-1) for p in pieces])
    return jnp.pad(flat, (0, rows * PACK_COLS - flat.shape[0])).reshape(rows, PACK_COLS)


def _unpack(flat, shapes):
    out, off = [], 0
    for shp in shapes:
        size = math.prod(shp)
        out.append(flat[off:off + size].reshape(shp))
        off += size
    return out


def _rows_for(n_elems, mult):
    rows = -(-n_elems // PACK_COLS)
    return -(-rows // mult) * mult


BIG_SHARDS = [("w_in", (D_MODEL, 1474)), ("w_branch_att", (D_ATT, 256)), ("w_branch_conv", (D_CONV, 256)),
              ("w_branch_sgu", (D_SGU, 256)), ("w_out", (256, D_MODEL)), ("w_ffn_up", (D_MODEL, FF_BLK)),
              ("w_ffn_down", (D_FF // N_CHIPS, D_MODEL))]
SMALL_SHARDS = [("b_gate", (3, 256)), ("conv_mix_w", (3, 64)), ("conv_ffn_w", (3, FF_BLK))]
REPLICATED = [("pre_mix_g", (D_MODEL,)), ("post_mix_g", (D_MODEL,)), ("pre_ffn_g", (D_MODEL,)),
              ("post_ffn_g", (D_MODEL,)), ("b_forget", (N_HEADS,)), ("sgu_ln_g", (D_SGU,)), ("sgu_ln_b", (D_SGU,)),
              ("sgu_w", (N_GROUPS, CHUNK, CHUNK)), ("sgu_b", (N_GROUPS, CHUNK))]
WEIGHT_ORDER = ["pre_mix_g", "post_mix_g", "pre_ffn_g", "post_ffn_g", "w_in", "b_forget", "b_gate", "conv_mix_w",
                "sgu_ln_g", "sgu_ln_b", "sgu_w", "sgu_b", "w_branch_att", "w_branch_conv", "w_branch_sgu", "w_out",
                "w_ffn_up", "conv_ffn_w", "w_ffn_down"]

_SMALL_ELEMS = sum(math.prod(s) for _, s in SMALL_SHARDS)
_REP_ELEMS = sum(math.prod(s) for _, s in REPLICATED)
_REP_QUARTER = -(-(DEPTH * _REP_ELEMS) // N_CHIPS)
SMALL_PARAM_ROWS = _rows_for(DEPTH * _SMALL_ELEMS, 32)
SMALL_ROWS = _rows_for(DEPTH * _SMALL_ELEMS + _REP_QUARTER, 32)
IN_WIDTH = 5896
IN_SHARD = IN_WIDTH // N_CHIPS
IN_SHARD_PAD = 1536
IN_PAD = 6144


def _gather_small(wts):
    shard = _pack_rows([wts[n] for n, _ in SMALL_SHARDS], SMALL_PARAM_ROWS, F32)
    full = _all_gather_chips(shard, "gather_small_params").reshape(N_CHIPS, -1)
    per_chip = [_unpack(full[j], [(DEPTH,) + s for _, s in SMALL_SHARDS]) for j in range(N_CHIPS)]
    return {n: jnp.concatenate([per_chip[j][i] for j in range(N_CHIPS)], axis=-1)
            for i, (n, _) in enumerate(SMALL_SHARDS)}


BIG_NAMES = [n for n, _ in BIG_SHARDS]
FIRST_NAMES = ["w_in"]
LATE_NAMES = BIG_NAMES[1:]


def _gather_begin(wts, l, me_idx, names, tag):
    cast = [_cast_shard(wts[n], l, me_idx, "cast_" + n) for n in names]
    return _ici_start("gather", [sh for sh, _ in cast], [ld for _, ld in cast], "gather_ici_start_" + tag)


def _gather_finish(started, after, names, tag):
    lands = _ici_wait("gather", started, after, "gather_ici_wait_" + tag)
    return dict(zip(names, _gather_d2d(lands, "gather_d2d_" + tag)))


def _pad_rows(a, rows):
    return jnp.pad(a, ((0, rows - a.shape[0]), (0, 0)))


def _whole_cols(land):
    return land.transpose(1, 0, 2).reshape(land.shape[1], -1)


_O_F = 3 * D_ATT
_O_B = _O_F + N_HEADS
_O_GL = _O_B + 3 * D_CONV + 2 * D_SGU


_LOCAL_ORDER = [(_O_GL, IN_WIDTH), (0, _O_F), (_O_B, _O_GL), (_O_F, _O_B)]


def _own_cols(land, lo, hi):
    pieces = []
    for j in range(N_CHIPS):
        a, b = max(lo, j * IN_SHARD), min(hi, (j + 1) * IN_SHARD)
        if a < b:
            pieces.append(land[j][:, a - j * IN_SHARD:b - j * IN_SHARD])
    return pieces


def _local_cols(m, lo, hi):
    pieces, off = [], 0
    for a, b in _LOCAL_ORDER:
        x, y = max(lo, a), min(hi, b)
        if x < y:
            pieces.append((x, m[:, off + x - a:off + y - a]))
        off += b - a
    pieces = [p for _, p in sorted(pieces, key=lambda t: t[0])]
    if hi > IN_WIDTH:
        pieces.append(jnp.zeros((m.shape[0], hi - max(lo, IN_WIDTH)), m.dtype))
    return pieces


def _prep_first(wts, lands, small, l):
    land = lands["w_in"]
    cf = small["conv_ffn_w"][l]
    blk = lambda a, j: a[:, j * FF_BLK:(j + 1) * FF_BLK]
    local = [piece for lo, hi in _LOCAL_ORDER for piece in _own_cols(land, lo, hi)]
    return {
        "w_p": jnp.concatenate(local + [jnp.zeros((D_MODEL, IN_PAD - IN_WIDTH), BF16)], axis=1),
        "wf_t": _pad_rows(jnp.concatenate(_own_cols(land, _O_F, _O_B), axis=1).T, F_ROWS),
        "b_forget": _pad_rows(wts["b_forget"][l].reshape(N_HEADS, 1), F_ROWS),
        "b_gate": _pad_rows(small["b_gate"][l], 8),
        "conv_mix_w": _pad_rows(small["conv_mix_w"][l], 8),
        "conv_ffn_w": _pad_rows(jnp.concatenate([blk(cf, 0), blk(cf, 2), blk(cf, 1), blk(cf, 3)], axis=1), 8),
        "pre_mix_g": wts["pre_mix_g"][l].reshape(1, -1), "post_mix_g": wts["post_mix_g"][l].reshape(1, -1),
        "pre_ffn_g": wts["pre_ffn_g"][l].reshape(1, -1), "post_ffn_g": wts["post_ffn_g"][l].reshape(1, -1),
        "ln_g": wts["sgu_ln_g"][l].reshape(1, -1), "ln_b": wts["sgu_ln_b"][l].reshape(1, -1),
        "sgu_w": wts["sgu_w"][l],
        "sgu_bias": jnp.repeat(wts["sgu_b"][l].T, HEAD_DIM, axis=1),
    }


def _prep_late(lands):
    up = lands["w_ffn_up"]
    return {
        "w_att": _whole_cols(lands["w_branch_att"]), "w_conv": _whole_cols(lands["w_branch_conv"]),
        "w_sgu": _whole_cols(lands["w_branch_sgu"]),
        "w_out": lands["w_out"].reshape(D_MODEL, D_MODEL),
        "w_up": jnp.concatenate([up[0], up[2], up[1], up[3]], axis=1),
        "w_down": lands["w_ffn_down"].reshape(D_FF, D_MODEL),
    }


def _layer_fwd(x, p, dep=None, late=None):
    s = x.shape[0]
    xn = _rms_fwd(x, p["pre_mix_g"], "rms_pre_mix", dep)
    h = _mm(xn, p["w_p"], "nn", BF16, "mm_in", s, 512, D_MODEL)
    f_row = _mm(p["wf_t"], xn, "nt", F32, "mm_forget", F_ROWS, 2048, D_MODEL)
    ck = _gate_fwd(f_row, p["b_forget"], "gate_fwd")
    o, o_f32, lse = _attn_fwd(h, ck, "attn_fwd")
    yc = _sconv_fwd(h, p["conv_mix_w"], "sconv_fwd")
    ys = _sgu_fwd(h, p["ln_g"], p["ln_b"], p["sgu_w"], p["sgu_bias"], "sgu_fwd")
    if late is not None:
        p.update(late(o))
    merged = _merge_fwd(h, (o, yc, ys), (p["w_att"], p["w_conv"], p["w_sgu"]), p["b_gate"], "merge_fwd")
    mo = _mm(merged, p["w_out"], "nn", F32, "mm_out", 2048, 512, D_MODEL)
    x1 = _resid_post(x, mo, p["post_mix_g"], "post_mix")
    xn2 = _rms_fwd(x1, p["pre_ffn_g"], "rms_pre_ffn")
    h2 = _mm(xn2, p["w_up"], "nn", BF16, "mm_up", 2048, 512, D_MODEL)
    pact = _ffn_act_fwd(h2, p["conv_ffn_w"], "ffn_act_fwd")
    ff = _mm(pact, p["w_down"], "nn", F32, "mm_down", 1024, D_MODEL, D_FF)
    x2 = _resid_post(x1, ff, p["post_ffn_g"], "post_ffn")
    saved = dict(x=x, xn=xn, h=h, f_row=f_row, ck=ck, o=o, o_f32=o_f32, lse=lse, yc=yc, ys=ys, merged=merged, mo=mo, x1=x1,
                 xn2=xn2, h2=h2, pact=pact, ff=ff)
    return x2, saved


def _layer_bwd(dx2, p, sv, dep=None, early=None):
    s = dx2.shape[0]
    g = {}
    same = lambda b: b
    dff, g["post_ffn_g"] = _rms_bwd(sv["ff"], p["post_ffn_g"], [dx2], None, BF16, "post_ffn_bwd", dep)
    dpact = _mm(dff, p["w_down"], "nt", BF16, "mm_down_dx", 1024, FF_BLK, D_MODEL)
    t_down = _mm(sv["pact"], dff, "tn", BF16, "mm_down_dw", 256, D_MODEL, s).reshape(N_CHIPS, -1, D_MODEL)
    dh2, dconv_ffn = _ffn_act_conv_bwd(sv["h2"], p["conv_ffn_w"], dpact, "ffn_act_conv_bwd")
    dxn2 = _mm(dh2, p["w_up"], "nt", F32, "mm_up_dx", 512, D_MODEL, 2 * D_FF)
    t_up = _mm(sv["xn2"], dh2, "tn", BF16, "mm_up_dw", 512, FF_BLK, s, chip_of=lambda b: (b % 2) * 2 + b // 2)
    dx1, g["pre_ffn_g"] = _rms_bwd(sv["x1"], p["pre_ffn_g"], [dxn2], dx2, F32, "pre_ffn_bwd")
    dep_mix = early([t_up, t_down]) if early is not None else None
    dmo, g["post_mix_g"] = _rms_bwd(sv["mo"], p["post_mix_g"], [dx1], None, BF16, "post_mix_bwd", dep_mix)
    dmerged = _mm(dmo, p["w_out"], "nt", F32, "mm_out_dx", 2048, 512, D_MODEL)
    t_out = _mm(sv["merged"], dmo, "tn", BF16, "mm_out_dw", 512, D_MODEL, s).reshape(N_CHIPS, -1, D_MODEL)
    acts = (sv["o"], sv["yc"], sv["ys"])
    ws = (p["w_att"], p["w_conv"], p["w_sgu"])
    dy_a, dy_c, dy_s, dgl, db_gate = _merge_bwd(sv["h"], acts, ws, p["b_gate"], dmerged, "merge_bwd")
    do = _mm(dy_a, p["w_att"], "nt", BF16, "mm_att_dx", 2048, D_ATT, D_MODEL)
    dyc = _mm(dy_c, p["w_conv"], "nt", BF16, "mm_conv_dx", 2048, D_CONV, D_MODEL)
    dys = _mm(dy_s, p["w_sgu"], "nt", BF16, "mm_sgu_dx", 2048, D_SGU, D_MODEL)
    t_att = _mm(sv["o"], dy_a, "tn", BF16, "mm_att_dw", D_ATT, 256, s, chip_of=same)
    t_conv = _mm(sv["yc"], dy_c, "tn", BF16, "mm_conv_dw", D_CONV, 256, s, chip_of=same)
    t_sgu = _mm(sv["ys"], dy_s, "tn", BF16, "mm_sgu_dw", D_SGU, 256, s, chip_of=same)
    d_conv, dconv_mix = _sconv_bwd(sv["h"], p["conv_mix_w"], dyc, "sconv_bwd")
    d_sgu, g["sgu_ln_g"], g["sgu_ln_b"], g["sgu_w"], dbias = _sgu_bwd(
        sv["h"], p["ln_g"], p["ln_b"], p["sgu_w"], p["sgu_bias"], dys, "sgu_bwd")
    dq, dk, dv, dc_even, dc_odd = _attn_bwd(sv["h"], sv["ck"], sv["o_f32"], sv["lse"], do, "attn_bwd")
    df, db_forget = _gate_bwd(sv["f_row"], p["b_forget"], dc_even, dc_odd, "gate_bwd")
    f_cols = jnp.concatenate([df[:N_HEADS].T, jnp.zeros((s, IN_PAD - IN_WIDTH), BF16)], axis=1)
    dh = _assemble_dh([dgl, dq, dk, dv, d_conv, d_sgu, f_cols], "assemble_dh")
    dxn = _mm(dh, p["w_p"], "nt", F32, "mm_in_dx", 512, D_MODEL, IN_PAD)
    dw_p = _mm(sv["xn"], dh, "tn", BF16, "mm_in_dw", D_MODEL, 512, s)
    t_in = jnp.stack([jnp.concatenate(_local_cols(dw_p, j * IN_SHARD, j * IN_SHARD + IN_SHARD_PAD), axis=1)
                      for j in range(N_CHIPS)])
    dx, g["pre_mix_g"] = _rms_bwd(sv["x"], p["pre_mix_g"], [dxn], dx1, F32, "pre_mix_bwd")
    blk = lambda a, j: a[:, j * FF_BLK:(j + 1) * FF_BLK]
    g["conv_ffn_w"] = jnp.concatenate([blk(dconv_ffn, 0), blk(dconv_ffn, 2), blk(dconv_ffn, 1),
                                       blk(dconv_ffn, 3)], axis=1)[:3]
    g["conv_mix_w"] = dconv_mix[:3]
    g["b_gate"] = db_gate[:3]
    g["b_forget"] = db_forget[:N_HEADS, 0]
    g["sgu_b"] = jnp.sum(dbias.reshape(CHUNK, N_GROUPS, HEAD_DIM), axis=-1).T
    for n in ("pre_mix_g", "post_mix_g", "pre_ffn_g", "post_ffn_g", "sgu_ln_g", "sgu_ln_b"):
        g[n] = g[n].reshape(-1)
    mix = [t_in, t_att, t_conv, t_sgu, t_out]
    return dx, (mix if early is not None else mix + [t_up, t_down]), g


def _assemble_dh(pieces, name):
    s = pieces[0].shape[0]
    t = _tile(s, 512)
    width = sum(a.shape[1] for a in pieces)

    def body(*refs):
        out = refs[-1]
        col = 0
        for ref in refs[:-1]:
            w = ref.shape[1]
            out[:, col:col + w] = ref[...].astype(out.dtype)
            col += w

    return pl.pallas_call(
        body, name=name, grid=(s // t,),
        in_specs=[pl.BlockSpec((t, a.shape[1]), lambda i: (i, 0)) for a in pieces],
        out_specs=pl.BlockSpec((t, width), lambda i: (i, 0)),
        out_shape=jax.ShapeDtypeStruct((s, width), BF16),
        compiler_params=_params(("parallel",)),
    )(*pieces)


def _shard_cols(a, j):
    w = a.shape[-1] // N_CHIPS
    return a[..., j * w:(j + 1) * w]


def kernel(x, pre_mix_g, post_mix_g, pre_ffn_g, post_ffn_g, w_in, b_forget, b_gate, conv_mix_w, sgu_ln_g, sgu_ln_b, sgu_w, sgu_b, w_branch_att, w_branch_conv, w_branch_sgu, w_out, w_ffn_up, conv_ffn_w, w_ffn_down, loss_target, m_pre_mix_g, m_post_mix_g, m_pre_ffn_g, m_post_ffn_g, m_w_in, m_b_forget, m_b_gate, m_conv_mix_w, m_sgu_ln_g, m_sgu_ln_b, m_sgu_w, m_sgu_b, m_w_branch_att, m_w_branch_conv, m_w_branch_sgu, m_w_out, m_w_ffn_up, m_conv_ffn_w, m_w_ffn_down, v_pre_mix_g, v_post_mix_g, v_pre_ffn_g, v_post_ffn_g, v_w_in, v_b_forget, v_b_gate, v_conv_mix_w, v_sgu_ln_g, v_sgu_ln_b, v_sgu_w, v_sgu_b, v_w_branch_att, v_w_branch_conv, v_w_branch_sgu, v_w_out, v_w_ffn_up, v_conv_ffn_w, v_w_ffn_down):
    wts = dict(pre_mix_g=pre_mix_g, post_mix_g=post_mix_g, pre_ffn_g=pre_ffn_g, post_ffn_g=post_ffn_g, w_in=w_in,
               b_forget=b_forget, b_gate=b_gate, conv_mix_w=conv_mix_w, sgu_ln_g=sgu_ln_g, sgu_ln_b=sgu_ln_b,
               sgu_w=sgu_w, sgu_b=sgu_b, w_branch_att=w_branch_att, w_branch_conv=w_branch_conv,
               w_branch_sgu=w_branch_sgu, w_out=w_out, w_ffn_up=w_ffn_up, conv_ffn_w=conv_ffn_w,
               w_ffn_down=w_ffn_down)
    moms = dict(pre_mix_g=m_pre_mix_g, post_mix_g=m_post_mix_g, pre_ffn_g=m_pre_ffn_g, post_ffn_g=m_post_ffn_g,
                w_in=m_w_in, b_forget=m_b_forget, b_gate=m_b_gate, conv_mix_w=m_conv_mix_w, sgu_ln_g=m_sgu_ln_g,
                sgu_ln_b=m_sgu_ln_b, sgu_w=m_sgu_w, sgu_b=m_sgu_b, w_branch_att=m_w_branch_att,
                w_branch_conv=m_w_branch_conv, w_branch_sgu=m_w_branch_sgu, w_out=m_w_out, w_ffn_up=m_w_ffn_up,
                conv_ffn_w=m_conv_ffn_w, w_ffn_down=m_w_ffn_down)
    vels = dict(pre_mix_g=v_pre_mix_g, post_mix_g=v_post_mix_g, pre_ffn_g=v_pre_ffn_g, post_ffn_g=v_post_ffn_g,
                w_in=v_w_in, b_forget=v_b_forget, b_gate=v_b_gate, conv_mix_w=v_conv_mix_w, sgu_ln_g=v_sgu_ln_g,
                sgu_ln_b=v_sgu_ln_b, sgu_w=v_sgu_w, sgu_b=v_sgu_b, w_branch_att=v_w_branch_att,
                w_branch_conv=v_w_branch_conv, w_branch_sgu=v_w_branch_sgu, w_out=v_w_out, w_ffn_up=v_w_ffn_up,
                conv_ffn_w=v_conv_ffn_w, w_ffn_down=v_w_ffn_down)

    c_idx = lax.axis_index("c").astype(jnp.int32).reshape(1)
    me_idx = (2 * lax.axis_index("x") + lax.axis_index("y")).astype(jnp.int32).reshape(1)
    small = _gather_small(wts)

    xs = x[0]
    layers, saved = [], []
    first = _gather_begin(wts, 0, me_idx, FIRST_NAMES, "first")
    rest = _gather_begin(wts, 0, me_idx, LATE_NAMES, "late")
    lands = _gather_finish(first, xs, FIRST_NAMES, "first")
    late = lambda after: _prep_late(_gather_finish(rest, after, LATE_NAMES, "late"))
    for l in range(DEPTH):
        p = _prep_first(wts, lands, small, l)
        if l > 0:
            p.update(_prep_late(lands))
        nxt = _gather_begin(wts, l + 1, me_idx, BIG_NAMES, "all") if l + 1 < DEPTH else None
        dep = ([nxt[4]] if nxt else []) + ([rest[4]] if l == 0 else [])
        xs, sv = _layer_fwd(xs, p, dep or None, late if l == 0 else None)
        if nxt:
            lands = _gather_finish(nxt, xs, BIG_NAMES, "all")
        layers.append(p)
        saved.append(sv)
    dy, loss_part = _loss_head(xs, loss_target[0], "loss_head")
    loss = lax.psum(loss_part[0, 0], ("x", "y", "c"))

    big_red = [None] * DEPTH
    small_grads = [None] * DEPTH
    pending = None
    ffn = []
    for l in reversed(range(DEPTH)):
        early = None
        if l == 0:
            def early(ts_ffn):
                ffn.append(_rs_begin(ts_ffn, c_idx, me_idx, "ffn"))
                return ffn[0][4]
        dy, ts, small_grads[l] = _layer_bwd(dy, layers[l], saved[l], pending[4] if pending else None, early)
        if pending:
            big_red[l + 1] = _rs_finish(pending, dy, c_idx, "big")
        pending = _rs_begin(ts, c_idx, me_idx, "mix" if l == 0 else "big")
    red_ffn = _rs_finish(ffn[0], dy, c_idx, "ffn")
    grad_x = dy[None]

    rep_flat = jnp.concatenate([small_grads[l][n].reshape(-1) for l in range(DEPTH) for n, _ in REPLICATED])
    rep_flat = jnp.pad(rep_flat, (0, N_CHIPS * _REP_QUARTER - rep_flat.shape[0]))
    rows = []
    for j in range(N_CHIPS):
        pieces = [_shard_cols(small_grads[l][n], j) for l in range(DEPTH) for n, _ in SMALL_SHARDS]
        pieces.append(rep_flat[j * _REP_QUARTER:(j + 1) * _REP_QUARTER])
        rows.append(_pack_rows(pieces, SMALL_ROWS, F32))
    small_red = _reduce_scatter_chips(jnp.stack(rows), "small")
    small_all = _all_gather_chips(small_red, "gather_small")
    big_red[0] = _rs_finish(pending, small_all, c_idx, "mix") + red_ffn
    small_all = small_all.reshape(N_CHIPS, -1)

    grads = {}
    for i, (n, _) in enumerate(BIG_SHARDS):
        grads[n] = jnp.stack([big_red[l][i][:, :IN_SHARD] if n == "w_in" else big_red[l][i] for l in range(DEPTH)])
    mine_small = small_red.reshape(-1)
    parts = _unpack(mine_small, [s for _ in range(DEPTH) for _, s in SMALL_SHARDS])
    for i, (n, _) in enumerate(SMALL_SHARDS):
        grads[n] = jnp.stack([parts[l * len(SMALL_SHARDS) + i] for l in range(DEPTH)])
    off = DEPTH * _SMALL_ELEMS
    rep_all = jnp.concatenate([small_all[j, off:off + _REP_QUARTER] for j in range(N_CHIPS)])
    parts = _unpack(rep_all, [s for _ in range(DEPTH) for _, s in REPLICATED])
    for i, (n, _) in enumerate(REPLICATED):
        grads[n] = jnp.stack([parts[l * len(REPLICATED) + i] for l in range(DEPTH)])

    deltas, new_m, new_v = {}, {}, {}
    for n in WEIGHT_ORDER:
        deltas[n], new_m[n], new_v[n] = _adamw(wts[n], grads[n], moms[n], vels[n], "adamw_" + n)
    return (loss, grad_x, *[grads[n] for n in WEIGHT_ORDER], *[deltas[n] for n in WEIGHT_ORDER],
            *[new_m[n] for n in WEIGHT_ORDER], *[new_v[n] for n in WEIGHT_ORDER])
```

```python
import functools
import math

import jax
import jax.numpy as jnp
from jax import lax
from jax.experimental import pallas as pl
from jax.experimental.pallas import tpu as pltpu

F32 = jnp.float32
BF16 = jnp.bfloat16
MXU_DTYPE = jnp.bfloat16

D_MODEL = 1024
HEAD_DIM = 64
N_HEADS = 8
D_ATT = 512
D_CONV = 256
D_SGU = 256
N_GROUPS = 4
CHUNK = 128
D_FF = 2816
DEPTH = 4
RMS_EPS = 1e-6
LN_EPS = 1e-5
N_CHIPS = 4
LANES = 128
PACK_COLS = 1024
HALO = 16

ADAM_LR = 0.001
ADAM_B1 = 0.9
ADAM_B2 = 0.999
ADAM_EPS = 1e-08
ADAM_WD = 0.01
ADAM_STEP = 10

OFF_GL = 0
OFF_Q = 3 * D_MODEL
OFF_K = OFF_Q + D_ATT
OFF_V = OFF_K + D_ATT
OFF_BG = OFF_V + D_ATT
OFF_CG = OFF_BG + D_CONV
OFF_HC = OFF_CG + D_CONV
OFF_U = OFF_HC + D_CONV
OFF_VS = OFF_U + D_SGU
W_P = OFF_VS + D_SGU
F_ROWS = 16

VMEM_LIMIT = 56 * 1024 * 1024
MESH = pl.DeviceIdType.MESH


def _params(sem=None):
    if sem is None:
        return pltpu.CompilerParams(vmem_limit_bytes=VMEM_LIMIT)
    return pltpu.CompilerParams(dimension_semantics=sem, vmem_limit_bytes=VMEM_LIMIT)


def _tile(dim, pref):
    if dim <= pref:
        return dim
    if dim % pref == 0:
        return pref
    return dim


_DIMS = {"nn": (((1,), (0,)), ((), ())), "nt": (((1,), (1,)), ((), ())), "tn": (((0,), (0,)), ((), ()))}


def _mm(a, b, mode, out_dtype, name, tm, tn, tk, chip_of=None):
    if mode == "tn":
        K, M = a.shape
    else:
        M, K = a.shape
    N = b.shape[0] if mode == "nt" else b.shape[1]
    tm, tn, tk = _tile(M, tm), _tile(N // N_CHIPS if chip_of else N, tn), _tile(K, tk)
    nk = K // tk
    dims = _DIMS[mode]

    def body(a_ref, b_ref, o_ref, *acc):
        part = lax.dot_general(a_ref[...].astype(MXU_DTYPE), b_ref[...].astype(MXU_DTYPE), dims,
                               preferred_element_type=F32)
        if nk == 1:
            o_ref[...] = part.astype(o_ref.dtype)
        else:
            acc_ref = acc[0]
            k = pl.program_id(2)

            @pl.when(k == 0)
            def _():
                acc_ref[...] = part

            @pl.when(k > 0)
            def _():
                acc_ref[...] += part

            @pl.when(k == nk - 1)
            def _():
                o_ref[...] = acc_ref[...].astype(o_ref.dtype)

    if mode == "tn":
        a_spec = pl.BlockSpec((tk, tm), lambda i, j, k: (k, i))
    else:
        a_spec = pl.BlockSpec((tm, tk), lambda i, j, k: (i, k))
    if mode == "nt":
        b_spec = pl.BlockSpec((tn, tk), lambda i, j, k: (j, k))
    else:
        b_spec = pl.BlockSpec((tk, tn), lambda i, j, k: (k, j))
    if chip_of is None:
        out_spec = pl.BlockSpec((tm, tn), lambda i, j, k: (i, j))
        out_shape = jax.ShapeDtypeStruct((M, N), out_dtype)
    else:
        per = (N // N_CHIPS) // tn
        out_spec = pl.BlockSpec((None, tm, tn), lambda i, j, k: (chip_of(j // per), i, j % per))
        out_shape = jax.ShapeDtypeStruct((N_CHIPS, M, N // N_CHIPS), out_dtype)
    return pl.pallas_call(
        body,
        name=name,
        grid=(M // tm, N // tn, nk),
        in_specs=[a_spec, b_spec],
        out_specs=out_spec,
        out_shape=out_shape,
        scratch_shapes=[pltpu.VMEM((tm, tn), F32)] if nk > 1 else [],
        compiler_params=_params(("parallel", "parallel", "arbitrary")),
    )(a, b)


_GELU_K = math.sqrt(2.0 / math.pi)
_GELU_C = 0.044715


def _gelu(x):
    t = jnp.tanh(_GELU_K * (x + _GELU_C * (x * x * x)))
    return x * (0.5 * (1.0 + t))


def _gelu_and_grad(x):
    x2 = x * x
    t = jnp.tanh(_GELU_K * (x + _GELU_C * (x2 * x)))
    cdf = 0.5 * (1.0 + t)
    dcdf = 0.5 * (1.0 - t * t) * (_GELU_K * (1.0 + 3.0 * _GELU_C * x2))
    return x * cdf, cdf + x * dcdf


def _sigmoid(x):
    return 1.0 / (1.0 + jnp.exp(-x))


def _shift_down(cur, prev, k):
    h = prev.shape[0]
    ext = jnp.concatenate([prev, cur], axis=0)
    return pltpu.roll(ext, k, 0)[h:]


def _shift_up(cur, nxt, k):
    t, h = cur.shape[0], nxt.shape[0]
    ext = jnp.concatenate([cur, nxt], axis=0)
    return pltpu.roll(ext, t + h - k, 0)[:t]


def _row_sum8(x):
    t, c = x.shape
    return jnp.sum(x.reshape(t // 8, 8, c), axis=0)


_DEP = pl.BlockSpec((8, LANES), lambda i: (0, 0))


def _rms_fwd(x, g, name, dep=None):
    s, d = x.shape
    t = _tile(s, 512)

    def body(x_ref, g_ref, *rest):
        o_ref = rest[-1]
        xv = x_ref[...]
        r = lax.rsqrt(jnp.mean(xv * xv, axis=-1, keepdims=True) + RMS_EPS)
        o_ref[...] = (xv * r * g_ref[...]).astype(o_ref.dtype)

    deps = [] if dep is None else list(dep) if isinstance(dep, (list, tuple)) else [dep]
    return pl.pallas_call(
        body, name=name, grid=(s // t,),
        in_specs=[pl.BlockSpec((t, d), lambda i: (i, 0)), pl.BlockSpec((1, d), lambda i: (0, 0))] + [_DEP] * len(deps),
        out_specs=pl.BlockSpec((t, d), lambda i: (i, 0)),
        out_shape=jax.ShapeDtypeStruct((s, d), BF16),
        compiler_params=_params(("parallel",)),
    )(x, g, *deps)


def _resid_post(x, y, g, name):
    s, d = x.shape
    t = _tile(s, 512)

    def body(x_ref, y_ref, g_ref, o_ref):
        yv = y_ref[...]
        r = lax.rsqrt(jnp.mean(yv * yv, axis=-1, keepdims=True) + RMS_EPS)
        o_ref[...] = x_ref[...] + yv * r * g_ref[...]

    row = pl.BlockSpec((t, d), lambda i: (i, 0))
    return pl.pallas_call(
        body, name=name, grid=(s // t,),
        in_specs=[row, row, pl.BlockSpec((1, d), lambda i: (0, 0))],
        out_specs=row,
        out_shape=jax.ShapeDtypeStruct((s, d), F32),
        compiler_params=_params(("parallel",)),
    )(x, y, g)


def _rms_bwd(xin, g, dys, dres, out_dtype, name, dep=None):
    s, d = xin.shape
    t = _tile(s, 512)
    n = s // t
    n_dy = len(dys)
    has_res = dres is not None
    deps = [] if dep is None else [dep]

    def body(*refs):
        x_ref, g_ref = refs[0], refs[1]
        dy_refs = refs[2:2 + n_dy]
        pos = 2 + n_dy
        res_ref = refs[pos] if has_res else None
        pos += (1 if has_res else 0) + len(deps)
        dx_ref, dg_ref, acc_ref = refs[pos], refs[pos + 1], refs[pos + 2]
        i = pl.program_id(0)
        xv = x_ref[...]
        dy = dy_refs[0][...].astype(F32)
        for extra in dy_refs[1:]:
            dy = dy + extra[...].astype(F32)
        r = lax.rsqrt(jnp.mean(xv * xv, axis=-1, keepdims=True) + RMS_EPS)
        u = dy * g_ref[...]
        xr = xv * r
        dx = r * (u - xr * jnp.mean(u * xr, axis=-1, keepdims=True))
        if has_res:
            dx = dx + res_ref[...]
        dx_ref[...] = dx.astype(dx_ref.dtype)
        part = _row_sum8(dy * xr)

        @pl.when(i == 0)
        def _():
            acc_ref[...] = part

        @pl.when(i > 0)
        def _():
            acc_ref[...] += part

        @pl.when(i == n - 1)
        def _():
            dg_ref[...] = jnp.sum(acc_ref[...], axis=0, keepdims=True)

    row = pl.BlockSpec((t, d), lambda i: (i, 0))
    vec = pl.BlockSpec((1, d), lambda i: (0, 0))
    ins = [xin, g, *dys] + ([dres] if has_res else []) + deps
    return pl.pallas_call(
        body, name=name, grid=(n,),
        in_specs=[row, vec] + [row] * (n_dy + (1 if has_res else 0)) + [_DEP] * len(deps),
        out_specs=[row, vec],
        out_shape=[jax.ShapeDtypeStruct((s, d), out_dtype), jax.ShapeDtypeStruct((1, d), F32)],
        scratch_shapes=[pltpu.VMEM((8, d), F32)],
        compiler_params=_params(("arbitrary",)),
    )(*ins)


def _loss_head(y, target, name):
    s, d = y.shape
    t = _tile(s, 512)
    n = s // t

    def body(y_ref, t_ref, dy_ref, loss_ref, acc_ref):
        i = pl.program_id(0)
        e = y_ref[...] - t_ref[...]
        dy_ref[...] = e * (1.0 / d)
        part = _row_sum8(e * e)

        @pl.when(i == 0)
        def _():
            acc_ref[...] = part

        @pl.when(i > 0)
        def _():
            acc_ref[...] += part

        @pl.when(i == n - 1)
        def _():
            tot = jnp.sum(jnp.sum(acc_ref[...], axis=0, keepdims=True), axis=1, keepdims=True)
            loss_ref[...] = tot * (0.5 / d)

    row = pl.BlockSpec((t, d), lambda i: (i, 0))
    return pl.pallas_call(
        body, name=name, grid=(n,),
        in_specs=[row, row],
        out_specs=[row, pl.BlockSpec((1, 1), lambda i: (0, 0))],
        out_shape=[jax.ShapeDtypeStruct((s, d), F32), jax.ShapeDtypeStruct((1, 1), F32)],
        scratch_shapes=[pltpu.VMEM((8, d), F32)],
        compiler_params=_params(("arbitrary",)),
    )(y, target)


def _split3(x):
    hi = x.astype(BF16)
    r1 = x - hi.astype(F32)
    mid = r1.astype(BF16)
    lo = (r1 - mid.astype(F32)).astype(BF16)
    return hi, mid, lo


def _tri_dot(x, tri):
    hi, mid, lo = _split3(x)
    dn = _DIMS["nn"]
    out = lax.dot_general(hi, tri, dn, preferred_element_type=F32)
    out = out + lax.dot_general(mid, tri, dn, preferred_element_type=F32)
    return out + lax.dot_general(lo, tri, dn, preferred_element_type=F32)


def _log_sigmoid(z):
    return jnp.minimum(z, 0.0) - jnp.log(1.0 + jnp.exp(-jnp.abs(z)))


def _gate_fwd(f_row, b_col, name):
    rows, s = f_row.shape
    t = _tile(s, 512)
    n = s // t

    def body(f_ref, b_ref, ck_ref, carry_ref):
        i = pl.program_id(0)

        @pl.when(i == 0)
        def _():
            carry_ref[...] = jnp.zeros_like(carry_ref)

        logf = _log_sigmoid(f_ref[...] + b_ref[...])
        r = lax.broadcasted_iota(jnp.int32, (t, t), 0)
        c = lax.broadcasted_iota(jnp.int32, (t, t), 1)
        tri = jnp.where(r <= c, 1.0, 0.0).astype(BF16)
        cs = _tri_dot(logf, tri) + carry_ref[...]
        carry_ref[...] = cs[:, t - 1:t]
        terms = [part.astype(F32) for part in _split3(-cs)]
        sub = lax.broadcasted_iota(jnp.int32, (LANES, t), 0)
        for p in range(N_HEADS // 2):
            stacked = jnp.zeros((LANES, t), F32)
            for hh in range(2):
                for j, term in enumerate(terms):
                    h = 2 * p + hh
                    stacked = jnp.where(sub == 3 * hh + j, jnp.broadcast_to(term[h:h + 1, :], (LANES, t)), stacked)
            ck_ref[p] = stacked.T.astype(ck_ref.dtype)

    return pl.pallas_call(
        body, name=name, grid=(n,),
        in_specs=[pl.BlockSpec((rows, t), lambda i: (0, i)), pl.BlockSpec((rows, 1), lambda i: (0, 0))],
        out_specs=pl.BlockSpec((N_HEADS // 2, t, LANES), lambda i: (0, i, 0)),
        out_shape=jax.ShapeDtypeStruct((N_HEADS // 2, s, LANES), BF16),
        scratch_shapes=[pltpu.VMEM((rows, 1), F32)],
        compiler_params=_params(("arbitrary",)),
    )(f_row, b_col)


def _gate_bwd(f_row, b_col, dc_even, dc_odd, name):
    rows, s = f_row.shape
    t = _tile(s, 512)
    n = s // t

    def body(f_ref, b_ref, dce_ref, dco_ref, df_ref, db_ref, carry_ref, acc_ref):
        i = pl.program_id(0)

        @pl.when(i == 0)
        def _():
            carry_ref[...] = jnp.zeros_like(carry_ref)
            acc_ref[...] = jnp.zeros_like(acc_ref)

        head = lax.broadcasted_iota(jnp.int32, (rows, t), 0)
        dcv = jnp.zeros((rows, t), F32)
        for h in range(N_HEADS):
            src = dce_ref if h % 2 == 0 else dco_ref
            dcv = jnp.where(head == h, jnp.broadcast_to(src[h // 2, 0:1, :], (rows, t)), dcv)
        r = lax.broadcasted_iota(jnp.int32, (t, t), 0)
        c = lax.broadcasted_iota(jnp.int32, (t, t), 1)
        tri = jnp.where(r >= c, 1.0, 0.0).astype(BF16)
        dlogf = _tri_dot(dcv, tri) + carry_ref[...]
        carry_ref[...] = dlogf[:, 0:1]
        z = f_ref[...] + b_ref[...]
        df = dlogf * _sigmoid(-z)
        df_ref[...] = df.astype(df_ref.dtype)
        acc_ref[...] += jnp.sum(df, axis=1, keepdims=True)

        @pl.when(i == n - 1)
        def _():
            db_ref[...] = acc_ref[...]

    rev = lambda i: (0, n - 1 - i)
    dc_spec = pl.BlockSpec((N_HEADS // 2, 8, t), lambda i: (0, 0, n - 1 - i))
    return pl.pallas_call(
        body, name=name, grid=(n,),
        in_specs=[pl.BlockSpec((rows, t), rev), pl.BlockSpec((rows, 1), lambda i: (0, 0)), dc_spec, dc_spec],
        out_specs=[pl.BlockSpec((rows, t), rev), pl.BlockSpec((rows, 1), lambda i: (0, 0))],
        out_shape=[jax.ShapeDtypeStruct((rows, s), BF16), jax.ShapeDtypeStruct((rows, 1), F32)],
        scratch_shapes=[pltpu.VMEM((rows, 1), F32), pltpu.VMEM((rows, 1), F32)],
        compiler_params=_params(("arbitrary",)),
    )(f_row, b_col, dc_even, dc_odd)


_NEG = -1e30
_SCALE = HEAD_DIM ** -0.5


def _head_masks():
    lane = lax.broadcasted_iota(jnp.int32, (1, LANES), 1)
    return [lane < HEAD_DIM, lane >= HEAD_DIM]


def _attn_fwd(h, ck, name):
    s = h.shape[0]
    t = _tile(s, 512)
    n = s // t
    qb, kb, vb = OFF_Q // LANES, OFF_K // LANES, OFF_V // LANES

    pairs = [(qi, ki) for qi in range(n) for ki in range(qi + 1)]
    qi_tab = jnp.asarray([qi for qi, _ in pairs], jnp.int32)
    ki_tab = jnp.asarray([ki for _, ki in pairs], jnp.int32)

    def body(qi_ref, ki_ref, q_ref, k_ref, v_ref, ck_ref, o_ref, of_ref, lse_ref, m_ref, l_ref, acc_ref):
        qi, ki = qi_ref[pl.program_id(1)], ki_ref[pl.program_id(1)]
        masks = _head_masks()
        lane = lax.broadcasted_iota(jnp.int32, (1, LANES), 1)

        @pl.when(ki == 0)
        def _():
            m_ref[...] = jnp.full_like(m_ref, _NEG)
            l_ref[...] = jnp.zeros_like(l_ref)
            acc_ref[...] = jnp.zeros_like(acc_ref)

        def step(diag):
            q = q_ref[...] * _SCALE
            k_aug = jnp.concatenate([k_ref[...], ck_ref[0]], axis=1)
            v = v_ref[...]
            nq = max(1, t // 256)
            wq = t // nq
            chains = [(hh, j) for hh in range(2) for j in range(nq)]
            scores = []
            for hh, j in chains:
                qs = q[j * wq:(j + 1) * wq]
                ones = jnp.where((lane >= 3 * hh) & (lane < 3 * hh + 3), 1.0, 0.0).astype(q.dtype)
                q_aug = jnp.concatenate([jnp.where(masks[hh], qs, jnp.zeros_like(qs)),
                                         jnp.broadcast_to(ones, qs.shape)], axis=1)
                scores.append(lax.dot_general(k_aug, q_aug, _DIMS["nt"], preferred_element_type=F32))
            probs = []
            for (hh, j), sc in zip(chains, scores):
                cols = slice(j * wq, (j + 1) * wq)
                if diag:
                    r = lax.broadcasted_iota(jnp.int32, (t, wq), 0)
                    cc = lax.broadcasted_iota(jnp.int32, (t, wq), 1) + j * wq
                    sc = jnp.where(r <= cc, sc, _NEG)
                m_prev = m_ref[hh, :, cols]
                m_new = jnp.maximum(m_prev, jnp.max(sc, axis=0, keepdims=True))
                alpha = jnp.exp(m_prev - m_new)
                p = jnp.exp(sc - m_new)
                l_ref[hh, :, cols] = alpha * l_ref[hh, :, cols] + jnp.sum(p, axis=0, keepdims=True)
                m_ref[hh, :, cols] = m_new
                p_hi = p.astype(MXU_DTYPE)
                p_lo = (p - p_hi.astype(F32)).astype(MXU_DTYPE)
                probs.append((alpha, p_hi, p_lo))
            for (hh, j), (alpha, p_hi, p_lo) in zip(chains, probs):
                pv = (lax.dot_general(v, p_hi, _DIMS["tn"], preferred_element_type=F32)
                      + lax.dot_general(v, p_lo, _DIMS["tn"], preferred_element_type=F32))
                rows = slice(hh * HEAD_DIM, (hh + 1) * HEAD_DIM)
                cols = slice(j * wq, (j + 1) * wq)
                acc_ref[rows, cols] = alpha * acc_ref[rows, cols] + pv[rows]

        @pl.when(ki < qi)
        def _():
            step(False)

        @pl.when(ki == qi)
        def _():
            step(True)
            inv = jnp.concatenate([jnp.broadcast_to(1.0 / l_ref[hh], (HEAD_DIM, t)) for hh in range(2)], axis=0)
            out = (acc_ref[...] * inv).T
            o_ref[...] = out.astype(o_ref.dtype)
            of_ref[...] = out
            lse = jnp.concatenate([jnp.broadcast_to(m_ref[hh] + jnp.log(l_ref[hh]), (HEAD_DIM, t))
                                   for hh in range(2)], axis=0)
            lse_ref[...] = lse.T

    grid_spec = pltpu.PrefetchScalarGridSpec(
        num_scalar_prefetch=2, grid=(N_HEADS // 2, len(pairs)),
        in_specs=[
            pl.BlockSpec((t, LANES), lambda p, i, qt, kt: (qt[i], qb + p)),
            pl.BlockSpec((t, LANES), lambda p, i, qt, kt: (kt[i], kb + p)),
            pl.BlockSpec((t, LANES), lambda p, i, qt, kt: (kt[i], vb + p)),
            pl.BlockSpec((1, t, LANES), lambda p, i, qt, kt: (p, kt[i], 0)),
        ],
        out_specs=[pl.BlockSpec((t, LANES), lambda p, i, qt, kt: (qt[i], p))] * 3,
        scratch_shapes=[pltpu.VMEM((2, 1, t), F32), pltpu.VMEM((2, 1, t), F32), pltpu.VMEM((LANES, t), F32)])
    return pl.pallas_call(
        body, name=name, grid_spec=grid_spec,
        out_shape=[jax.ShapeDtypeStruct((s, D_ATT), BF16), jax.ShapeDtypeStruct((s, D_ATT), F32),
                   jax.ShapeDtypeStruct((s, D_ATT), F32)],
        compiler_params=_params(("parallel", "arbitrary")),
    )(qi_tab, ki_tab, h, h, h, ck)


def _attn_bwd(h, ck, o, lse, do, name):
    s = h.shape[0]
    t = _tile(s, 512)
    n = s // t
    qb, kb, vb = OFF_Q // LANES, OFF_K // LANES, OFF_V // LANES

    pairs = [(ki, qi) for ki in range(n) for qi in range(ki, n)]
    ki_tab = jnp.asarray([ki for ki, _ in pairs], jnp.int32)
    qi_tab = jnp.asarray([qi for _, qi in pairs], jnp.int32)

    def body(ki_ref, qi_ref, q_ref, k_ref, v_ref, ck_ref, o_ref, lse_ref, do_ref,
             dq_ref, dk_ref, dv_ref, dc0_ref, dc1_ref, dk_acc, dv_acc, dc_acc):
        ki, qi = ki_ref[pl.program_id(1)], qi_ref[pl.program_id(1)]
        masks = _head_masks()
        lane = lax.broadcasted_iota(jnp.int32, (1, LANES), 1)

        @pl.when((ki == 0) & (qi == 0))
        def _():
            dq_ref[...] = jnp.zeros_like(dq_ref)

        @pl.when(qi == ki)
        def _():
            dk_acc[...] = jnp.zeros_like(dk_acc)
            dv_acc[...] = jnp.zeros_like(dv_acc)
            dc_acc[...] = jnp.zeros_like(dc_acc)

        def step(diag):
            q = q_ref[...] * _SCALE
            k = k_ref[...]
            v = v_ref[...]
            dov = do_ref[...]
            k_aug = jnp.concatenate([k, ck_ref[0]], axis=1)
            prod_t = (dov.astype(F32) * o_ref[...]).T
            lse_t = lse_ref[...].T
            heads = []
            for hh in range(2):
                mk = masks[hh]
                qh = jnp.where(mk, q, jnp.zeros_like(q))
                kh = jnp.where(mk, k, jnp.zeros_like(k))
                doh = jnp.where(mk, dov, jnp.zeros_like(dov))
                ones = jnp.where((lane >= 3 * hh) & (lane < 3 * hh + 3), 1.0, 0.0).astype(q.dtype)
                q_aug = jnp.concatenate([qh, jnp.broadcast_to(ones, q.shape)], axis=1)
                sc = lax.dot_general(k_aug, q_aug, _DIMS["nt"], preferred_element_type=F32)
                dp = lax.dot_general(v, doh, _DIMS["nt"], preferred_element_type=F32)
                heads.append((qh, kh, doh, sc, dp))
            grads = []
            for hh, (qh, kh, doh, sc, dp) in enumerate(heads):
                rows = slice(hh * HEAD_DIM, (hh + 1) * HEAD_DIM)
                p = jnp.exp(sc - lse_t[hh * HEAD_DIM:hh * HEAD_DIM + 1, :])
                if diag:
                    r = lax.broadcasted_iota(jnp.int32, (t, t), 0)
                    cc = lax.broadcasted_iota(jnp.int32, (t, t), 1)
                    p = jnp.where(r <= cc, p, 0.0)
                delta = jnp.sum(prod_t[rows], axis=0, keepdims=True)
                ds = p * (dp - delta)
                dc_acc[hh] = dc_acc[hh] - jnp.sum(ds, axis=1, keepdims=True)
                grads.append((ds.astype(MXU_DTYPE), p.astype(MXU_DTYPE)))
            dq_blk = jnp.zeros((t, LANES), F32)
            for (qh, kh, doh, _, _), (dsb, pb) in zip(heads, grads):
                dv_acc[...] += lax.dot_general(pb, doh, _DIMS["nn"], preferred_element_type=F32)
                dk_acc[...] += lax.dot_general(dsb, qh, _DIMS["nn"], preferred_element_type=F32)
                dq_blk = dq_blk + lax.dot_general(dsb, kh, _DIMS["tn"], preferred_element_type=F32)
            rows_q = pl.ds(pl.multiple_of(qi * t, t), t)
            dq_ref[rows_q, :] = dq_ref[rows_q, :] + dq_blk * _SCALE

        @pl.when(qi > ki)
        def _():
            step(False)

        @pl.when(qi == ki)
        def _():
            step(True)

        @pl.when(qi == n - 1)
        def _():
            dk_ref[...] = dk_acc[...].astype(dk_ref.dtype)
            dv_ref[...] = dv_acc[...].astype(dv_ref.dtype)
            dc0_ref[0] = jnp.broadcast_to(dc_acc[0], (t, LANES)).T[0:8]
            dc1_ref[0] = jnp.broadcast_to(dc_acc[1], (t, LANES)).T[0:8]

    q_blk = lambda col: pl.BlockSpec((t, LANES), lambda p, i, kt, qt: (qt[i], col(p)))
    k_blk = lambda col: pl.BlockSpec((t, LANES), lambda p, i, kt, qt: (kt[i], col(p)))
    dc_blk = pl.BlockSpec((1, 8, t), lambda p, i, kt, qt: (p, 0, kt[i]))
    grid_spec = pltpu.PrefetchScalarGridSpec(
        num_scalar_prefetch=2, grid=(N_HEADS // 2, len(pairs)),
        in_specs=[q_blk(lambda p: qb + p), k_blk(lambda p: kb + p), k_blk(lambda p: vb + p),
                  pl.BlockSpec((1, t, LANES), lambda p, i, kt, qt: (p, kt[i], 0)),
                  q_blk(lambda p: p), q_blk(lambda p: p), q_blk(lambda p: p)],
        out_specs=[pl.BlockSpec((s, LANES), lambda p, i, kt, qt: (0, p)), k_blk(lambda p: p), k_blk(lambda p: p),
                   dc_blk, dc_blk],
        scratch_shapes=[pltpu.VMEM((t, LANES), F32), pltpu.VMEM((t, LANES), F32), pltpu.VMEM((2, t, 1), F32)])
    return pl.pallas_call(
        body, name=name, grid_spec=grid_spec,
        out_shape=[jax.ShapeDtypeStruct((s, D_ATT), F32), jax.ShapeDtypeStruct((s, D_ATT), BF16),
                   jax.ShapeDtypeStruct((s, D_ATT), BF16), jax.ShapeDtypeStruct((N_HEADS // 2, 8, s), F32),
                   jax.ShapeDtypeStruct((N_HEADS // 2, 8, s), F32)],
        compiler_params=_params(("parallel", "arbitrary")),
    )(ki_tab, qi_tab, h, h, h, ck, o, lse, do)


def _conv3(z, z_prev, w_ref):
    return (w_ref[2:3, :] * z + w_ref[1:2, :] * _shift_down(z, z_prev, 1)
            + w_ref[0:1, :] * _shift_down(z, z_prev, 2))


def _sconv_fwd(h, w, name):
    s = h.shape[0]
    t = _tile(s, 512)
    r = t // HALO
    c = D_CONV
    b_bg, b_cg, b_hc = OFF_BG // c, OFF_CG // c, OFF_HC // c

    def body(bg_ref, cg_ref, hc_ref, cgp_ref, hcp_ref, w_ref, y_ref):
        i = pl.program_id(0)
        live = (i > 0).astype(F32)
        z = cg_ref[...].astype(F32) * hc_ref[...].astype(F32)
        zp = cgp_ref[...].astype(F32) * hcp_ref[...].astype(F32) * live
        y_ref[...] = (bg_ref[...].astype(F32) * _conv3(z, zp, w_ref)).astype(y_ref.dtype)

    cur = lambda b: pl.BlockSpec((t, c), lambda i: (i, b))
    prev = lambda b: pl.BlockSpec((HALO, c), lambda i: (jnp.maximum(i * r - 1, 0), b))
    return pl.pallas_call(
        body, name=name, grid=(s // t,),
        in_specs=[cur(b_bg), cur(b_cg), cur(b_hc), prev(b_cg), prev(b_hc), pl.BlockSpec((8, c), lambda i: (0, 0))],
        out_specs=pl.BlockSpec((t, c), lambda i: (i, 0)),
        out_shape=jax.ShapeDtypeStruct((s, c), BF16),
        compiler_params=_params(("parallel",)),
    )(h, h, h, h, h, w)


def _sconv_bwd(h, w, dy, name):
    s = h.shape[0]
    t = _tile(s, 512)
    n = s // t
    r = t // HALO
    nh = s // HALO
    c = D_CONV
    b_bg, b_cg, b_hc = OFF_BG // c, OFF_CG // c, OFF_HC // c

    def body(bg_ref, cg_ref, hc_ref, cgp_ref, hcp_ref, bgn_ref, dy_ref, dyn_ref, w_ref, d_ref, dw_ref, acc_ref):
        i = pl.program_id(0)
        has_prev = (i > 0).astype(F32)
        has_next = (i < n - 1).astype(F32)
        bg = bg_ref[...].astype(F32)
        cg = cg_ref[...].astype(F32)
        hc = hc_ref[...].astype(F32)
        dyv = dy_ref[...].astype(F32)
        z = cg * hc
        zp = cgp_ref[...].astype(F32) * hcp_ref[...].astype(F32) * has_prev
        z1 = _shift_down(z, zp, 1)
        z2 = _shift_down(z, zp, 2)
        cz = w_ref[2:3, :] * z + w_ref[1:2, :] * z1 + w_ref[0:1, :] * z2
        dcz = dyv * bg
        dczn = dyn_ref[...].astype(F32) * bgn_ref[...].astype(F32) * has_next
        dz = (w_ref[2:3, :] * dcz + w_ref[1:2, :] * _shift_up(dcz, dczn, 1)
              + w_ref[0:1, :] * _shift_up(dcz, dczn, 2))
        d_ref[:, 0:c] = (dyv * cz).astype(d_ref.dtype)
        d_ref[:, c:2 * c] = (dz * hc).astype(d_ref.dtype)
        d_ref[:, 2 * c:3 * c] = (dz * cg).astype(d_ref.dtype)

        @pl.when(i == 0)
        def _():
            acc_ref[...] = jnp.zeros_like(acc_ref)

        acc_ref[0] += _row_sum8(dcz * z2)
        acc_ref[1] += _row_sum8(dcz * z1)
        acc_ref[2] += _row_sum8(dcz * z)

        @pl.when(i == n - 1)
        def _():
            rows = [jnp.sum(acc_ref[k], axis=0, keepdims=True) for k in range(3)]
            dw_ref[...] = jnp.concatenate(rows + [jnp.zeros((5, c), F32)], axis=0)

    cur = lambda b: pl.BlockSpec((t, c), lambda i: (i, b))
    prev = lambda b: pl.BlockSpec((HALO, c), lambda i: (jnp.maximum(i * r - 1, 0), b))
    nxt = lambda b: pl.BlockSpec((HALO, c), lambda i: (jnp.minimum((i + 1) * r, nh - 1), b))
    return pl.pallas_call(
        body, name=name, grid=(n,),
        in_specs=[cur(b_bg), cur(b_cg), cur(b_hc), prev(b_cg), prev(b_hc), nxt(b_bg),
                  cur(0), nxt(0), pl.BlockSpec((8, c), lambda i: (0, 0))],
        out_specs=[pl.BlockSpec((t, 3 * c), lambda i: (i, 0)), pl.BlockSpec((8, c), lambda i: (0, 0))],
        out_shape=[jax.ShapeDtypeStruct((s, 3 * c), BF16), jax.ShapeDtypeStruct((8, c), F32)],
        scratch_shapes=[pltpu.VMEM((3, 8, c), F32)],
        compiler_params=_params(("arbitrary",)),
    )(h, h, h, h, h, h, dy, dy, w)


def _group_masks():
    lane = lax.broadcasted_iota(jnp.int32, (1, D_SGU), 1)
    return [(lane >= g * HEAD_DIM) & (lane < (g + 1) * HEAD_DIM) for g in range(N_GROUPS)]


def _tril_weights(w_ref):
    r = lax.broadcasted_iota(jnp.int32, (CHUNK, CHUNK), 0)
    c = lax.broadcasted_iota(jnp.int32, (CHUNK, CHUNK), 1)
    return [jnp.where(r >= c, w_ref[g], 0.0).astype(MXU_DTYPE) for g in range(N_GROUPS)]


def _sgu_ln(vs, g_ref, b_ref):
    vg, dvg = _gelu_and_grad(vs)
    mu = jnp.mean(vg, axis=-1, keepdims=True)
    xc = vg - mu
    rstd = lax.rsqrt(jnp.mean(xc * xc, axis=-1, keepdims=True) + LN_EPS)
    xhat = xc * rstd
    return xhat * g_ref[...] + b_ref[...], xhat, rstd, dvg


def _sgu_fwd(h, ln_g, ln_b, w_s, bias, name):
    s = h.shape[0]
    t = _tile(s, 512)
    c = D_SGU
    b_u, b_v = OFF_U // c, OFF_VS // c

    def body(u_ref, v_ref, g_ref, b_ref, w_ref, bias_ref, y_ref):
        gm = _group_masks()
        wm = _tril_weights(w_ref)
        ug = _gelu(u_ref[...].astype(F32))
        vn, _, _, _ = _sgu_ln(v_ref[...].astype(F32), g_ref, b_ref)
        vnb = vn.astype(MXU_DTYPE)
        for ch in range(t // CHUNK):
            rows = slice(ch * CHUNK, (ch + 1) * CHUNK)
            mixed = bias_ref[...]
            for g in range(N_GROUPS):
                mg = lax.dot_general(wm[g], vnb[rows], _DIMS["nn"], preferred_element_type=F32)
                mixed = jnp.where(gm[g], mixed + mg, mixed)
            y_ref[rows, :] = (ug[rows] * mixed).astype(y_ref.dtype)

    full = lambda shp: pl.BlockSpec(shp, lambda i: (0,) * len(shp))
    return pl.pallas_call(
        body, name=name, grid=(s // t,),
        in_specs=[pl.BlockSpec((t, c), lambda i: (i, b_u)), pl.BlockSpec((t, c), lambda i: (i, b_v)),
                  full((1, c)), full((1, c)), full((N_GROUPS, CHUNK, CHUNK)), full((CHUNK, c))],
        out_specs=pl.BlockSpec((t, c), lambda i: (i, 0)),
        out_shape=jax.ShapeDtypeStruct((s, c), BF16),
        compiler_params=_params(("parallel",)),
    )(h, h, ln_g, ln_b, w_s, bias)


def _sgu_bwd(h, ln_g, ln_b, w_s, bias, dy, name):
    s = h.shape[0]
    t = _tile(s, 512)
    n = s // t
    c = D_SGU
    b_u, b_v = OFF_U // c, OFF_VS // c

    def body(u_ref, v_ref, g_ref, b_ref, w_ref, bias_ref, dy_ref,
             d_ref, dg_ref, db_ref, dw_ref, dbias_ref, dg_acc, db_acc):
        i = pl.program_id(0)
        gm = _group_masks()
        wm = _tril_weights(w_ref)

        @pl.when(i == 0)
        def _():
            dg_acc[...] = jnp.zeros_like(dg_acc)
            db_acc[...] = jnp.zeros_like(db_acc)
            dw_ref[...] = jnp.zeros_like(dw_ref)
            dbias_ref[...] = jnp.zeros_like(dbias_ref)

        ug, dug = _gelu_and_grad(u_ref[...].astype(F32))
        vn, xhat, rstd, dvg = _sgu_ln(v_ref[...].astype(F32), g_ref, b_ref)
        vnb = vn.astype(MXU_DTYPE)
        dyv = dy_ref[...].astype(F32)
        dmixed = dyv * ug
        dmb = dmixed.astype(MXU_DTYPE)
        dvn_parts = []
        for ch in range(t // CHUNK):
            rows = slice(ch * CHUNK, (ch + 1) * CHUNK)
            mixed = bias_ref[...]
            dvn = jnp.zeros((CHUNK, c), F32)
            for g in range(N_GROUPS):
                mg = lax.dot_general(wm[g], vnb[rows], _DIMS["nn"], preferred_element_type=F32)
                mixed = jnp.where(gm[g], mixed + mg, mixed)
                dvn = jnp.where(gm[g], lax.dot_general(wm[g], dmb[rows], _DIMS["tn"], preferred_element_type=F32),
                                dvn)
                dmg = jnp.where(gm[g], dmb[rows], jnp.zeros_like(dmb[rows]))
                dw_ref[g] += lax.dot_general(dmg, vnb[rows], _DIMS["nt"], preferred_element_type=F32)
            d_ref[rows, 0:c] = (dyv[rows] * mixed * dug[rows]).astype(d_ref.dtype)
            dbias_ref[...] += dmixed[rows]
            dvn_parts.append(dvn)
        dvn = jnp.concatenate(dvn_parts, axis=0)
        dg_acc[...] += _row_sum8(dvn * xhat)
        db_acc[...] += _row_sum8(dvn)
        dxh = dvn * g_ref[...]
        dvgl = rstd * (dxh - jnp.mean(dxh, axis=-1, keepdims=True)
                       - xhat * jnp.mean(dxh * xhat, axis=-1, keepdims=True))
        d_ref[:, c:2 * c] = (dvgl * dvg).astype(d_ref.dtype)

        @pl.when(i == n - 1)
        def _():
            dg_ref[...] = jnp.sum(dg_acc[...], axis=0, keepdims=True)
            db_ref[...] = jnp.sum(db_acc[...], axis=0, keepdims=True)
            r = lax.broadcasted_iota(jnp.int32, (CHUNK, CHUNK), 0)
            cc = lax.broadcasted_iota(jnp.int32, (CHUNK, CHUNK), 1)
            for g in range(N_GROUPS):
                dw_ref[g] = jnp.where(r >= cc, dw_ref[g], 0.0)

    full = lambda shp: pl.BlockSpec(shp, lambda i: (0,) * len(shp))
    return pl.pallas_call(
        body, name=name, grid=(n,),
        in_specs=[pl.BlockSpec((t, c), lambda i: (i, b_u)), pl.BlockSpec((t, c), lambda i: (i, b_v)),
                  full((1, c)), full((1, c)), full((N_GROUPS, CHUNK, CHUNK)), full((CHUNK, c)),
                  pl.BlockSpec((t, c), lambda i: (i, 0))],
        out_specs=[pl.BlockSpec((t, 2 * c), lambda i: (i, 0)), full((1, c)), full((1, c)),
                   full((N_GROUPS, CHUNK, CHUNK)), full((CHUNK, c))],
        out_shape=[jax.ShapeDtypeStruct((s, 2 * c), BF16), jax.ShapeDtypeStruct((1, c), F32),
                   jax.ShapeDtypeStruct((1, c), F32), jax.ShapeDtypeStruct((N_GROUPS, CHUNK, CHUNK), F32),
                   jax.ShapeDtypeStruct((CHUNK, c), F32)],
        scratch_shapes=[pltpu.VMEM((8, c), F32), pltpu.VMEM((8, c), F32)],
        compiler_params=_params(("arbitrary",)),
    )(h, h, ln_g, ln_b, w_s, bias, dy)


def _merge_fwd(h, acts, ws, b_gate, name):
    s = h.shape[0]
    d = D_MODEL
    t = _tile(s, 512)

    def body(gl0, gl1, gl2, a0, a1, a2, w0, w1, w2, b_ref, o_ref):
        acc = jnp.zeros((t, d), F32)
        for i, (gl, a, w) in enumerate(((gl0, a0, w0), (gl1, a1, w1), (gl2, a2, w2))):
            y = lax.dot_general(a[...], w[...], _DIMS["nn"], preferred_element_type=F32)
            acc = acc + _sigmoid(gl[...].astype(F32) + b_ref[i:i + 1, :]) * y
        o_ref[...] = acc.astype(o_ref.dtype)

    full = lambda arr: pl.BlockSpec(arr.shape, lambda i: (0, 0))
    return pl.pallas_call(
        body, name=name, grid=(s // t,),
        in_specs=[pl.BlockSpec((t, d), lambda i, b=b: (i, b)) for b in range(3)]
                 + [pl.BlockSpec((t, a.shape[1]), lambda i: (i, 0)) for a in acts]
                 + [full(w) for w in ws] + [full(b_gate)],
        out_specs=pl.BlockSpec((t, d), lambda i: (i, 0)),
        out_shape=jax.ShapeDtypeStruct((s, d), BF16),
        compiler_params=_params(("parallel",)),
    )(h, h, h, *acts, *ws, b_gate)


def _merge_bwd(h, acts, ws, b_gate, dmerged, name):
    s = h.shape[0]
    d = D_MODEL
    t = _tile(s, 512)
    n = s // t

    def body(gl0, gl1, gl2, a0, a1, a2, w0, w1, w2, b_ref, dm_ref, dy0, dy1, dy2, dgl_ref, db_ref, acc_ref):
        step = pl.program_id(0)

        @pl.when(step == 0)
        def _():
            acc_ref[...] = jnp.zeros_like(acc_ref)

        dm = dm_ref[...]
        for i, (gl, a, w, dy) in enumerate(((gl0, a0, w0, dy0), (gl1, a1, w1, dy1), (gl2, a2, w2, dy2))):
            y = lax.dot_general(a[...], w[...], _DIMS["nn"], preferred_element_type=F32)
            gate = _sigmoid(gl[...].astype(F32) + b_ref[i:i + 1, :])
            dy[...] = (dm * gate).astype(dy.dtype)
            dgl = dm * y * (gate * (1.0 - gate))
            dgl_ref[:, i * d:(i + 1) * d] = dgl.astype(dgl_ref.dtype)
            acc_ref[i] += _row_sum8(dgl)

        @pl.when(step == n - 1)
        def _():
            rows = [jnp.sum(acc_ref[k], axis=0, keepdims=True) for k in range(3)]
            db_ref[...] = jnp.concatenate(rows + [jnp.zeros((5, d), F32)], axis=0)

    full = lambda arr: pl.BlockSpec(arr.shape, lambda i: (0, 0))
    row = pl.BlockSpec((t, d), lambda i: (i, 0))
    return pl.pallas_call(
        body, name=name, grid=(n,),
        in_specs=[pl.BlockSpec((t, d), lambda i, b=b: (i, b)) for b in range(3)]
                 + [pl.BlockSpec((t, a.shape[1]), lambda i: (i, 0)) for a in acts]
                 + [full(w) for w in ws] + [full(b_gate), row],
        out_specs=[row, row, row, pl.BlockSpec((t, 3 * d), lambda i: (i, 0)), pl.BlockSpec((8, d), lambda i: (0, 0))],
        out_shape=[jax.ShapeDtypeStruct((s, d), BF16)] * 3
                  + [jax.ShapeDtypeStruct((s, IN_PAD), BF16), jax.ShapeDtypeStruct((8, d), F32)],
        scratch_shapes=[pltpu.VMEM((3, 8, d), F32)],
        compiler_params=_params(("arbitrary",)),
    )(h, h, h, *acts, *ws, b_gate, dmerged)


FF_BLK = D_FF // 2


def _ffn_act_fwd(h2, w, name):
    s = h2.shape[0]
    t = _tile(s, 512)
    r = t // HALO
    cw = 2 * FF_BLK

    def body(x_ref, xp_ref, w_ref, p_ref):
        i = pl.program_id(0)
        live = (i > 0).astype(F32)
        hc = _conv3(x_ref[...].astype(F32), xp_ref[...].astype(F32) * live, w_ref)
        p_ref[...] = (_gelu(hc[:, :FF_BLK]) * hc[:, FF_BLK:]).astype(p_ref.dtype)

    return pl.pallas_call(
        body, name=name, grid=(s // t, 2),
        in_specs=[pl.BlockSpec((t, cw), lambda i, j: (i, j)),
                  pl.BlockSpec((HALO, cw), lambda i, j: (jnp.maximum(i * r - 1, 0), j)),
                  pl.BlockSpec((8, cw), lambda i, j: (0, j))],
        out_specs=pl.BlockSpec((t, FF_BLK), lambda i, j: (i, j)),
        out_shape=jax.ShapeDtypeStruct((s, D_FF), BF16),
        compiler_params=_params(("parallel", "parallel")),
    )(h2, h2, w)


def _ffn_act_conv_bwd(h2, w, dp, name):
    s = h2.shape[0]
    t = _tile(s, 512)
    n = s // t
    r = t // HALO
    nh = s // HALO
    cw = 2 * FF_BLK

    def body(x_ref, xp_ref, xn_ref, dp_ref, dpn_ref, w_ref, dx_ref, dw_ref, acc_ref):
        i = pl.program_id(1)
        has_prev = (i > 0).astype(F32)
        has_next = (i < n - 1).astype(F32)
        x = jnp.concatenate([x_ref[...].astype(F32), xn_ref[...].astype(F32)], axis=0)
        xp = xp_ref[...].astype(F32) * has_prev
        x1 = _shift_down(x, xp, 1)
        x2 = _shift_down(x, xp, 2)
        hc = w_ref[2:3, :] * x + w_ref[1:2, :] * x1 + w_ref[0:1, :] * x2
        ga, dga = _gelu_and_grad(hc[:, :FF_BLK])
        dpv = jnp.concatenate([dp_ref[...].astype(F32), dpn_ref[...].astype(F32) * has_next], axis=0)
        dhc = jnp.concatenate([dpv * hc[:, FF_BLK:] * dga, dpv * ga], axis=1)
        cur, nxt = dhc[:t], dhc[t:]
        dx = w_ref[2:3, :] * cur + w_ref[1:2, :] * _shift_up(cur, nxt, 1) + w_ref[0:1, :] * _shift_up(cur, nxt, 2)
        dx_ref[...] = dx.astype(dx_ref.dtype)

        @pl.when(i == 0)
        def _():
            acc_ref[...] = jnp.zeros_like(acc_ref)

        acc_ref[0] += _row_sum8(cur * x2[:t])
        acc_ref[1] += _row_sum8(cur * x1[:t])
        acc_ref[2] += _row_sum8(cur * x[:t])

        @pl.when(i == n - 1)
        def _():
            rows = [jnp.sum(acc_ref[k], axis=0, keepdims=True) for k in range(3)]
            dw_ref[...] = jnp.concatenate(rows + [jnp.zeros((5, cw), F32)], axis=0)

    nxt_row = lambda j, i: jnp.minimum((i + 1) * r, nh - 1)
    return pl.pallas_call(
        body, name=name, grid=(2, n),
        in_specs=[pl.BlockSpec((t, cw), lambda j, i: (i, j)),
                  pl.BlockSpec((HALO, cw), lambda j, i: (jnp.maximum(i * r - 1, 0), j)),
                  pl.BlockSpec((HALO, cw), lambda j, i: (nxt_row(j, i), j)),
                  pl.BlockSpec((t, FF_BLK), lambda j, i: (i, j)),
                  pl.BlockSpec((HALO, FF_BLK), lambda j, i: (nxt_row(j, i), j)),
                  pl.BlockSpec((8, cw), lambda j, i: (0, j))],
        out_specs=[pl.BlockSpec((t, cw), lambda j, i: (i, j)), pl.BlockSpec((8, cw), lambda j, i: (0, j))],
        out_shape=[jax.ShapeDtypeStruct((s, 2 * D_FF), BF16), jax.ShapeDtypeStruct((8, 2 * D_FF), F32)],
        scratch_shapes=[pltpu.VMEM((3, 8, cw), F32)],
        compiler_params=_params(("parallel", "arbitrary")),
    )(h2, h2, h2, dp, dp, w)


def _adamw(w, g, m, v, name):
    shape = w.shape
    c = shape[-1]
    rows = math.prod(shape[:-1])
    to2d = lambda a: a.reshape(rows, c)
    cap = max(8, (1 << 18) // c)
    tr = rows
    for cand in (2048, 1024, 512, 256, 128, 64, 32, 16, 8):
        if cand <= cap and rows % cand == 0:
            tr = cand
            break

    def body(w_ref, g_ref, m_ref, v_ref, d_ref, nm_ref, nv_ref):
        gv = g_ref[...]
        nm = ADAM_B1 * m_ref[...] + (1.0 - ADAM_B1) * gv
        nv = ADAM_B2 * v_ref[...] + (1.0 - ADAM_B2) * (gv * gv)
        m_hat = nm / (1.0 - ADAM_B1 ** ADAM_STEP)
        v_hat = nv / (1.0 - ADAM_B2 ** ADAM_STEP)
        d_ref[...] = -ADAM_LR * (m_hat / (jnp.sqrt(v_hat) + ADAM_EPS) + ADAM_WD * w_ref[...])
        nm_ref[...] = nm
        nv_ref[...] = nv

    blk = pl.BlockSpec((tr, c), lambda i: (i, 0))
    outs = pl.pallas_call(
        body, name=name, grid=(rows // tr,),
        in_specs=[blk] * 4, out_specs=[blk] * 3,
        out_shape=[jax.ShapeDtypeStruct((rows, c), F32)] * 3,
        compiler_params=_params(("parallel",)),
    )(to2d(w), to2d(g), to2d(m), to2d(v))
    return tuple(o.reshape(shape) for o in outs)


_ANY = pl.BlockSpec(memory_space=pl.ANY)


def _place():
    x, y, c = lax.axis_index("x"), lax.axis_index("y"), lax.axis_index("c")
    others = [(1 - x, y), (x, 1 - y), (1 - x, 1 - y)]
    return x, y, c, others


def _all_gather_chips(shard, name):
    rws, cols = shard.shape
    half = rws // 2

    def body(x_ref, out_ref, send_sems, recv_sems, local_sem):
        x, y, c, others = _place()
        me = 2 * x + y
        sib = (x, y, 1 - c)

        def rows(chip, cc):
            return out_ref.at[chip, pl.ds(pl.multiple_of(cc * half, 16), half), :]

        def copy(k, src, dst, to):
            return pltpu.make_async_remote_copy(src_ref=src, dst_ref=dst, send_sem=send_sems.at[k],
                                                recv_sem=recv_sems.at[k], device_id=to, device_id_type=MESH)

        mine = pltpu.make_async_copy(x_ref, out_ref.at[me], local_sem)
        mine.start()
        my_half = x_ref.at[pl.ds(pl.multiple_of(c * half, 16), half), :]
        first = [copy(j, my_half, rows(me, c), (ox, oy, c)) for j, (ox, oy) in enumerate(others)]
        for cp in first:
            cp.start()
        passed = []
        for j, (ox, oy) in enumerate(others):
            blk = rows(2 * ox + oy, c)
            copy(j, blk, blk, (x, y, c)).wait_recv()
            fwd = copy(3 + j, blk, blk, sib)
            fwd.start()
            passed.append(fwd)
        for j, (ox, oy) in enumerate(others):
            blk = rows(2 * ox + oy, 1 - c)
            copy(3 + j, blk, blk, (x, y, c)).wait_recv()
        for cp in first + passed:
            cp.wait_send()
        mine.wait()

    return pl.pallas_call(
        body, name=name,
        in_specs=[_ANY], out_specs=_ANY,
        out_shape=jax.ShapeDtypeStruct((N_CHIPS, rws, cols), shard.dtype),
        scratch_shapes=[pltpu.SemaphoreType.DMA((6,)), pltpu.SemaphoreType.DMA((6,)), pltpu.SemaphoreType.DMA],
        compiler_params=pltpu.CompilerParams(has_side_effects=True),
    )(shard)


def _swap_halves(buf, name):
    nb, rws, cols = buf.shape
    half = rws // 2

    def body(b_ref, own_ref, sib_ref, send_sem, recv_sem, local_sem):
        x, y, c, _ = _place()
        keep = b_ref.at[:, pl.ds(pl.multiple_of(c * half, 16), half), :]
        give = b_ref.at[:, pl.ds(pl.multiple_of((1 - c) * half, 16), half), :]
        mine = pltpu.make_async_copy(keep, own_ref, local_sem)
        mine.start()
        cp = pltpu.make_async_remote_copy(src_ref=give, dst_ref=sib_ref, send_sem=send_sem, recv_sem=recv_sem,
                                          device_id=(x, y, 1 - c), device_id_type=MESH)
        cp.start()
        cp.wait()
        mine.wait()

    shp = jax.ShapeDtypeStruct((nb, half, cols), buf.dtype)
    return pl.pallas_call(
        body, name=name,
        in_specs=[_ANY], out_specs=[_ANY, _ANY], out_shape=[shp, shp],
        scratch_shapes=[pltpu.SemaphoreType.DMA, pltpu.SemaphoreType.DMA, pltpu.SemaphoreType.DMA],
        compiler_params=pltpu.CompilerParams(has_side_effects=True),
    )(buf)


def _add2(a, b, name):
    nb, rws, cols = a.shape
    t = _tile(rws, 256)
    if rws % t:
        t = rws

    def body(a_ref, b_ref, o_ref):
        o_ref[...] = (a_ref[...].astype(F32) + b_ref[...].astype(F32)).astype(o_ref.dtype)

    blk = pl.BlockSpec((1, t, cols), lambda i, j: (i, j, 0))
    return pl.pallas_call(
        body, name=name, grid=(nb, rws // t), in_specs=[blk, blk], out_specs=blk,
        out_shape=jax.ShapeDtypeStruct(a.shape, a.dtype),
        compiler_params=_params(("parallel", "parallel")),
    )(a, b)


def _exchange_chips(pre, name):
    nb, half, cols = pre.shape

    def body(p_ref, out_ref, send_sems, recv_sems, local_sem):
        x, y, c, others = _place()
        me = 2 * x + y
        mine = pltpu.make_async_copy(p_ref.at[me], out_ref.at[me], local_sem)
        mine.start()
        sends = []
        for j, (ox, oy) in enumerate(others):
            cp = pltpu.make_async_remote_copy(src_ref=p_ref.at[2 * ox + oy], dst_ref=out_ref.at[me],
                                              send_sem=send_sems.at[j], recv_sem=recv_sems.at[j],
                                              device_id=(ox, oy, c), device_id_type=MESH)
            cp.start()
            sends.append(cp)
        for j, (ox, oy) in enumerate(others):
            blk = out_ref.at[2 * ox + oy]
            pltpu.make_async_remote_copy(src_ref=blk, dst_ref=blk, send_sem=send_sems.at[j],
                                         recv_sem=recv_sems.at[j], device_id=(x, y, c),
                                         device_id_type=MESH).wait_recv()
        for cp in sends:
            cp.wait_send()
        mine.wait()

    return pl.pallas_call(
        body, name=name,
        in_specs=[_ANY], out_specs=_ANY, out_shape=jax.ShapeDtypeStruct(pre.shape, pre.dtype),
        scratch_shapes=[pltpu.SemaphoreType.DMA((3,)), pltpu.SemaphoreType.DMA((3,)), pltpu.SemaphoreType.DMA],
        compiler_params=pltpu.CompilerParams(has_side_effects=True),
    )(pre)


def _add4(parts, name):
    nb, half, cols = parts.shape
    t = _tile(half, 256)
    if half % t:
        t = half

    def body(p_ref, o_ref):
        acc = p_ref[0].astype(F32)
        for k in range(1, nb):
            acc = acc + p_ref[k].astype(F32)
        o_ref[...] = acc

    return pl.pallas_call(
        body, name=name, grid=(half // t,),
        in_specs=[pl.BlockSpec((nb, t, cols), lambda i: (0, i, 0))],
        out_specs=pl.BlockSpec((t, cols), lambda i: (i, 0)),
        out_shape=jax.ShapeDtypeStruct((half, cols), F32),
        compiler_params=_params(("parallel",)),
    )(parts)


def _join_halves(mine_half, name):
    half, cols = mine_half.shape

    def body(h_ref, out_ref, send_sem, recv_sem, local_sem):
        x, y, c, _ = _place()
        dst = out_ref.at[pl.ds(pl.multiple_of(c * half, 8), half), :]
        mine = pltpu.make_async_copy(h_ref, dst, local_sem)
        mine.start()
        cp = pltpu.make_async_remote_copy(src_ref=h_ref, dst_ref=dst, send_sem=send_sem, recv_sem=recv_sem,
                                          device_id=(x, y, 1 - c), device_id_type=MESH)
        cp.start()
        cp.wait()
        mine.wait()

    return pl.pallas_call(
        body, name=name,
        in_specs=[_ANY], out_specs=_ANY, out_shape=jax.ShapeDtypeStruct((2 * half, cols), mine_half.dtype),
        scratch_shapes=[pltpu.SemaphoreType.DMA, pltpu.SemaphoreType.DMA, pltpu.SemaphoreType.DMA],
        compiler_params=pltpu.CompilerParams(has_side_effects=True),
    )(mine_half)


def _reduce_scatter_chips(buf, tag):
    own, sib = _swap_halves(buf, "rs_swap_" + tag)
    pre = _add2(own, sib, "rs_add2_" + tag)
    parts = _exchange_chips(pre, "rs_xchg_" + tag)
    red = _add4(parts, "rs_add4_" + tag)
    return _join_halves(red, "rs_join_" + tag)


MAX_DMA_BYTES = 2 * 1024 * 1024
ROW_ALIGN = 16


def _pieces(rows, row_bytes):
    n = max(1, -(-(rows * row_bytes) // MAX_DMA_BYTES))
    step = -(-(-(-rows // n)) // ROW_ALIGN) * ROW_ALIGN
    return [(r, min(step, rows - r)) for r in range(0, rows, step)]


def _half_plan(arrays, row_axis):
    plan = []
    for a, arr in enumerate(arrays):
        row_bytes = math.prod(arr.shape[row_axis + 1:]) * arr.dtype.itemsize * (arr.shape[0] if row_axis else 1)
        plan += [(a, r0, nr) for r0, nr in _pieces(arr.shape[row_axis] // 2, row_bytes)]
    return plan


def _rows(start, size):
    return pl.ds(pl.multiple_of(start, ROW_ALIGN), size)


def _remote(src, dst, send_sems, recv_sems, k, to):
    return pltpu.make_async_remote_copy(src_ref=src, dst_ref=dst, send_sem=send_sems.at[k], recv_sem=recv_sems.at[k],
                                        device_id=to, device_id_type=MESH)


def _comm_call(body, name, ins, out_shapes, n_remote, n_local, aliases=None):
    return pl.pallas_call(
        body, name=name,
        in_specs=[_ANY] * len(ins), out_specs=[_ANY] * len(out_shapes), out_shape=out_shapes,
        scratch_shapes=[pltpu.SemaphoreType.DMA((n_remote,)), pltpu.SemaphoreType.DMA((n_remote,)),
                        pltpu.SemaphoreType.DMA((max(n_local, 1),))],
        input_output_aliases=aliases or {},
        compiler_params=pltpu.CompilerParams(has_side_effects=True),
    )(*ins)


def _cast_shard(w, l, me_idx, name):
    _, k, cols = w.shape
    tr = _tile(k, 256)
    if k % tr:
        tr = k

    def body(me_ref, w_ref, s_ref, land_ref):
        del me_ref
        v = w_ref[...].astype(BF16)
        s_ref[...] = v
        land_ref[...] = v

    grid_spec = pltpu.PrefetchScalarGridSpec(
        num_scalar_prefetch=1, grid=(k // tr,),
        in_specs=[pl.BlockSpec((None, tr, cols), lambda i, me: (l, i, 0))],
        out_specs=[pl.BlockSpec((tr, cols), lambda i, me: (i, 0)),
                   pl.BlockSpec((None, tr, cols), lambda i, me: (me[0], i, 0))])
    return pl.pallas_call(
        body, name=name, grid_spec=grid_spec,
        out_shape=[jax.ShapeDtypeStruct((k, cols), BF16), jax.ShapeDtypeStruct((N_CHIPS, k, cols), BF16)],
        compiler_params=_params(("parallel",)),
    )(me_idx, w)


def _gather_d2d(lands, name):
    n = len(lands)
    plan = _half_plan(lands, 1)
    plan = [(a, r0, nr) for a, r0, nr in plan]

    def body(*refs):
        out_refs = refs[n:2 * n]
        send_sems, recv_sems, _ = refs[2 * n:]
        x, y, c, others = _place()
        sends = []
        for i, (a, r0, nr) in enumerate(plan):
            rows = _rows(c * (lands[a].shape[1] // 2) + r0, nr)
            for j, (ox, oy) in enumerate(others):
                blk = out_refs[a].at[2 * ox + oy, rows, :]
                cp = _remote(blk, blk, send_sems, recv_sems, 3 * i + j, (x, y, 1 - c))
                cp.start()
                sends.append(cp)
        for i, (a, r0, nr) in enumerate(plan):
            rows = _rows((1 - c) * (lands[a].shape[1] // 2) + r0, nr)
            for j, (ox, oy) in enumerate(others):
                blk = out_refs[a].at[2 * ox + oy, rows, :]
                _remote(blk, blk, send_sems, recv_sems, 3 * i + j, (x, y, c)).wait_recv()
        for cp in sends:
            cp.wait_send()

    outs = [jax.ShapeDtypeStruct(a.shape, a.dtype) for a in lands]
    return _comm_call(body, name, lands, outs, 3 * len(plan), 0, aliases={a: a for a in range(n)})


def _rs_swap(ts, name):
    n = len(ts)
    plan = _half_plan(ts, 1)

    def body(*refs):
        t_refs, out_refs = refs[:n], refs[n:2 * n]
        send_sems, recv_sems, _ = refs[2 * n:]
        x, y, c, _o = _place()
        sends = []
        for i, (a, r0, nr) in enumerate(plan):
            src = t_refs[a].at[:, _rows((1 - c) * (ts[a].shape[1] // 2) + r0, nr), :]
            cp = _remote(src, out_refs[a].at[:, pl.ds(r0, nr), :], send_sems, recv_sems, i, (x, y, 1 - c))
            cp.start()
            sends.append(cp)
        for i, (a, r0, nr) in enumerate(plan):
            blk = out_refs[a].at[:, pl.ds(r0, nr), :]
            _remote(blk, blk, send_sems, recv_sems, i, (x, y, c)).wait_recv()
        for cp in sends:
            cp.wait_send()

    outs = [jax.ShapeDtypeStruct((t.shape[0], t.shape[1] // 2, t.shape[2]), t.dtype) for t in ts]
    return _comm_call(body, name, ts, outs, len(plan), 0)


def _add_halves(ts, gots, c_idx, me_idx, name):
    n = len(ts)

    def body(c_ref, me_ref, *refs):
        del c_ref
        t_refs, g_refs = refs[:n], refs[n:2 * n]
        o_refs, mine_refs = refs[2 * n:3 * n], refs[3 * n:]
        for t_ref, g_ref, o_ref, mine_ref in zip(t_refs, g_refs, o_refs, mine_refs):
            v = (t_ref[...].astype(F32) + g_ref[...].astype(F32)).astype(o_ref.dtype)
            o_ref[...] = v

            @pl.when(pl.program_id(0) == me_ref[0])
            def _():
                mine_ref[...] = v

    blks = [(1, g.shape[1], g.shape[2]) for g in gots]
    same = [pl.BlockSpec(b, lambda i, c, me: (i, 0, 0)) for b in blks]
    grid_spec = pltpu.PrefetchScalarGridSpec(
        num_scalar_prefetch=2, grid=(N_CHIPS,),
        in_specs=[pl.BlockSpec(b, lambda i, c, me: (i, c[0], 0)) for b in blks] + same,
        out_specs=same + [pl.BlockSpec(b, lambda i, c, me: (me[0], 0, 0)) for b in blks])
    shapes = [jax.ShapeDtypeStruct(g.shape, g.dtype) for g in gots]
    outs = pl.pallas_call(
        body, name=name, grid_spec=grid_spec, out_shape=shapes + shapes,
        compiler_params=_params(("arbitrary",)),
    )(c_idx, me_idx, *ts, *gots)
    return outs[:n], outs[n:]


def _add4_halves(parts, c_idx, name):
    n = len(parts)
    steps = 2

    def body(c_ref, *refs):
        del c_ref
        for p_ref, o_ref in zip(refs[:n], refs[n:]):
            acc = p_ref[0].astype(F32)
            for k in range(1, N_CHIPS):
                acc = acc + p_ref[k].astype(F32)
            o_ref[...] = acc

    grid_spec = pltpu.PrefetchScalarGridSpec(
        num_scalar_prefetch=1, grid=(steps,),
        in_specs=[pl.BlockSpec((N_CHIPS, p.shape[1] // steps, p.shape[2]), lambda i, c: (0, i, 0)) for p in parts],
        out_specs=[pl.BlockSpec((p.shape[1] // steps, p.shape[2]), lambda i, c: (c[0] * steps + i, 0))
                   for p in parts])
    return pl.pallas_call(
        body, name=name, grid_spec=grid_spec,
        out_shape=[jax.ShapeDtypeStruct((2 * p.shape[1], p.shape[2]), F32) for p in parts],
        compiler_params=_params(("parallel",)),
    )(c_idx, *parts)


def _rs_join(fulls, name):
    n = len(fulls)
    plan = _half_plan(fulls, 0)

    def body(*refs):
        out_refs = refs[n:2 * n]
        send_sems, recv_sems, _ = refs[2 * n:]
        x, y, c, _o = _place()
        sends = []
        for i, (a, r0, nr) in enumerate(plan):
            blk = out_refs[a].at[_rows(c * (fulls[a].shape[0] // 2) + r0, nr), :]
            cp = _remote(blk, blk, send_sems, recv_sems, i, (x, y, 1 - c))
            cp.start()
            sends.append(cp)
        for i, (a, r0, nr) in enumerate(plan):
            blk = out_refs[a].at[_rows((1 - c) * (fulls[a].shape[0] // 2) + r0, nr), :]
            _remote(blk, blk, send_sems, recv_sems, i, (x, y, c)).wait_recv()
        for cp in sends:
            cp.wait_send()

    outs = [jax.ShapeDtypeStruct(f.shape, f.dtype) for f in fulls]
    return _comm_call(body, name, fulls, outs, len(plan), 0, aliases={a: a for a in range(n)})


_HBM = pl.BlockSpec(memory_space=pltpu.HBM)
_SEM = pl.BlockSpec(memory_space=pltpu.SEMAPHORE)
_EFFECT = pltpu.SideEffectType.DATAFLOW_SIDE_EFFECTING


def _ici_plan(kind, a_list):
    if kind == "gather":
        return _half_plan(a_list, 0)
    plan = []
    for a, p in enumerate(a_list):
        plan += [(a, r0, nr) for r0, nr in _pieces(p.shape[1], p.shape[2] * p.dtype.itemsize)]
    return plan


def _ici_refs(kind, a_ref, b_ref, a_shape, r0, nr, c, me, peer):
    if kind == "gather":
        rows = _rows(c * (a_shape[0] // 2) + r0, nr)
        return a_ref.at[rows, :], b_ref.at[me, rows, :], b_ref.at[peer, rows, :]
    rows = pl.ds(r0, nr)
    return a_ref.at[peer, rows, :], b_ref.at[me, rows, :], b_ref.at[peer, rows, :]


def _ici_start(kind, a_list, b_list, name):
    n = len(a_list)
    plan = _ici_plan(kind, a_list)
    shapes = [a.shape for a in a_list]

    def body(*refs):
        a_refs, b_refs = refs[:n], refs[n:2 * n]
        send_sems, recv_sems = refs[2 * n], refs[2 * n + 1]
        token = refs[4 * n + 2]
        x, y, c, others = _place()
        me = 2 * x + y
        for i, (a, r0, nr) in enumerate(plan):
            for j, (ox, oy) in enumerate(others):
                src, dst, _ = _ici_refs(kind, a_refs[a], b_refs[a], shapes[a], r0, nr, c, me, 2 * ox + oy)
                _remote(src, dst, send_sems, recv_sems, 3 * i + j, (ox, oy, c)).start()
        token[...] = jnp.zeros_like(token)

    hbm = lambda v: pltpu.HBM(v.shape, v.dtype)
    ncp = 3 * len(plan)
    outs = pl.pallas_call(
        body, name=name,
        in_specs=[_HBM] * (2 * n),
        out_specs=[_SEM, _SEM] + [_HBM] * (2 * n) + [pl.BlockSpec(memory_space=pltpu.VMEM)],
        out_shape=[pltpu.SemaphoreType.DMA((ncp,)), pltpu.SemaphoreType.DMA((ncp,))]
                  + [hbm(v) for v in a_list] + [hbm(v) for v in b_list] + [jax.ShapeDtypeStruct((8, LANES), F32)],
        input_output_aliases={i: 2 + i for i in range(2 * n)},
        compiler_params=pltpu.CompilerParams(has_side_effects=_EFFECT),
    )(*[pltpu.with_memory_space_constraint(v, pltpu.HBM) for v in list(a_list) + list(b_list)])
    return outs[0], outs[1], outs[2:2 + n], outs[2 + n:2 + 2 * n], outs[2 + 2 * n]


def _ici_wait(kind, started, after, name):
    send_sems, recv_sems, a_list, b_list, _ = started
    n = len(a_list)
    plan = _ici_plan(kind, a_list)
    shapes = [a.shape for a in a_list]

    def body(*refs):
        a_refs, b_refs = refs[:n], refs[n:2 * n]
        send_sems, recv_sems = refs[2 * n], refs[2 * n + 1]
        x, y, c, others = _place()
        me = 2 * x + y
        for i, (a, r0, nr) in enumerate(plan):
            for j, (ox, oy) in enumerate(others):
                src, dst, land = _ici_refs(kind, a_refs[a], b_refs[a], shapes[a], r0, nr, c, me, 2 * ox + oy)
                _remote(src, dst, send_sems, recv_sems, 3 * i + j, (ox, oy, c)).wait_send()
                _remote(land, land, send_sems, recv_sems, 3 * i + j, (x, y, c)).wait_recv()

    hbm = lambda v: pltpu.HBM(v.shape, v.dtype)
    outs = pl.pallas_call(
        body, name=name,
        in_specs=[_HBM] * (2 * n) + [_SEM, _SEM, _ANY],
        out_specs=[_HBM] * (2 * n),
        out_shape=[hbm(v) for v in a_list] + [hbm(v) for v in b_list],
        input_output_aliases={i: i for i in range(2 * n)},
        compiler_params=pltpu.CompilerParams(has_side_effects=_EFFECT),
    )(*a_list, *b_list, send_sems, recv_sems, after)
    return outs[n:]


def _rs_begin(ts, c_idx, me_idx, tag):
    got = _rs_swap(ts, "rs_swap_" + tag)
    pres, mine = _add_halves(ts, got, c_idx, me_idx, "rs_add2_" + tag)
    return _ici_start("scatter", pres, mine, "rs_xchg_start_" + tag)


def _rs_finish(started, after, c_idx, tag):
    parts = _ici_wait("scatter", started, after, "rs_xchg_wait_" + tag)
    return _rs_join(_add4_halves(parts, c_idx, "rs_add4_" + tag), "rs_join_" + tag)


def _pack_rows(pieces, rows, dtype):
    flat = jnp.concatenate([p.astype(dtype).reshape(-1) for p in pieces])
    return jnp.pad(flat, (0, rows * PACK_COLS - flat.shape[0])).reshape(rows, PACK_COLS)


def _unpack(flat, shapes):
    out, off = [], 0
    for shp in shapes:
        size = math.prod(shp)
        out.append(flat[off:off + size].reshape(shp))
        off += size
    return out


def _rows_for(n_elems, mult):
    rows = -(-n_elems // PACK_COLS)
    return -(-rows // mult) * mult


BIG_SHARDS = [("w_in", (D_MODEL, 1474)), ("w_branch_att", (D_ATT, 256)), ("w_branch_conv", (D_CONV, 256)),
              ("w_branch_sgu", (D_SGU, 256)), ("w_out", (256, D_MODEL)), ("w_ffn_up", (D_MODEL, FF_BLK)),
              ("w_ffn_down", (D_FF // N_CHIPS, D_MODEL))]
SMALL_SHARDS = [("b_gate", (3, 256)), ("conv_mix_w", (3, 64)), ("conv_ffn_w", (3, FF_BLK))]
REPLICATED = [("pre_mix_g", (D_MODEL,)), ("post_mix_g", (D_MODEL,)), ("pre_ffn_g", (D_MODEL,)),
              ("post_ffn_g", (D_MODEL,)), ("b_forget", (N_HEADS,)), ("sgu_ln_g", (D_SGU,)), ("sgu_ln_b", (D_SGU,)),
              ("sgu_w", (N_GROUPS, CHUNK, CHUNK)), ("sgu_b", (N_GROUPS, CHUNK))]
WEIGHT_ORDER = ["pre_mix_g", "post_mix_g", "pre_ffn_g", "post_ffn_g", "w_in", "b_forget", "b_gate", "conv_mix_w",
                "sgu_ln_g", "sgu_ln_b", "sgu_w", "sgu_b", "w_branch_att", "w_branch_conv", "w_branch_sgu", "w_out",
                "w_ffn_up", "conv_ffn_w", "w_ffn_down"]

_SMALL_ELEMS = sum(math.prod(s) for _, s in SMALL_SHARDS)
_REP_ELEMS = sum(math.prod(s) for _, s in REPLICATED)
_REP_QUARTER = -(-(DEPTH * _REP_ELEMS) // N_CHIPS)
SMALL_PARAM_ROWS = _rows_for(DEPTH * _SMALL_ELEMS, 32)
SMALL_ROWS = _rows_for(DEPTH * _SMALL_ELEMS + _REP_QUARTER, 32)
IN_WIDTH = 5896
IN_SHARD = IN_WIDTH // N_CHIPS
IN_SHARD_PAD = 1536
IN_PAD = 6144


def _gather_small(wts):
    shard = _pack_rows([wts[n] for n, _ in SMALL_SHARDS], SMALL_PARAM_ROWS, F32)
    full = _all_gather_chips(shard, "gather_small_params").reshape(N_CHIPS, -1)
    per_chip = [_unpack(full[j], [(DEPTH,) + s for _, s in SMALL_SHARDS]) for j in range(N_CHIPS)]
    return {n: jnp.concatenate([per_chip[j][i] for j in range(N_CHIPS)], axis=-1)
            for i, (n, _) in enumerate(SMALL_SHARDS)}


BIG_NAMES = [n for n, _ in BIG_SHARDS]
FIRST_NAMES = ["w_in"]
LATE_NAMES = BIG_NAMES[1:]


def _gather_begin(wts, l, me_idx, names, tag):
    cast = [_cast_shard(wts[n], l, me_idx, "cast_" + n) for n in names]
    return _ici_start("gather", [sh for sh, _ in cast], [ld for _, ld in cast], "gather_ici_start_" + tag)


def _gather_finish(started, after, names, tag):
    lands = _ici_wait("gather", started, after, "gather_ici_wait_" + tag)
    return dict(zip(names, _gather_d2d(lands, "gather_d2d_" + tag)))


def _pad_rows(a, rows):
    return jnp.pad(a, ((0, rows - a.shape[0]), (0, 0)))


def _whole_cols(land):
    return land.transpose(1, 0, 2).reshape(land.shape[1], -1)


_O_F = 3 * D_ATT
_O_B = _O_F + N_HEADS
_O_GL = _O_B + 3 * D_CONV + 2 * D_SGU


_LOCAL_ORDER = [(_O_GL, IN_WIDTH), (0, _O_F), (_O_B, _O_GL), (_O_F, _O_B)]


def _own_cols(land, lo, hi):
    pieces = []
    for j in range(N_CHIPS):
        a, b = max(lo, j * IN_SHARD), min(hi, (j + 1) * IN_SHARD)
        if a < b:
            pieces.append(land[j][:, a - j * IN_SHARD:b - j * IN_SHARD])
    return pieces


def _local_cols(m, lo, hi):
    pieces, off = [], 0
    for a, b in _LOCAL_ORDER:
        x, y = max(lo, a), min(hi, b)
        if x < y:
            pieces.append((x, m[:, off + x - a:off + y - a]))
        off += b - a
    pieces = [p for _, p in sorted(pieces, key=lambda t: t[0])]
    if hi > IN_WIDTH:
        pieces.append(jnp.zeros((m.shape[0], hi - max(lo, IN_WIDTH)), m.dtype))
    return pieces


def _prep_first(wts, lands, small, l):
    land = lands["w_in"]
    cf = small["conv_ffn_w"][l]
    blk = lambda a, j: a[:, j * FF_BLK:(j + 1) * FF_BLK]
    local = [piece for lo, hi in _LOCAL_ORDER for piece in _own_cols(land, lo, hi)]
    return {
        "w_p": jnp.concatenate(local + [jnp.zeros((D_MODEL, IN_PAD - IN_WIDTH), BF16)], axis=1),
        "wf_t": _pad_rows(jnp.concatenate(_own_cols(land, _O_F, _O_B), axis=1).T, F_ROWS),
        "b_forget": _pad_rows(wts["b_forget"][l].reshape(N_HEADS, 1), F_ROWS),
        "b_gate": _pad_rows(small["b_gate"][l], 8),
        "conv_mix_w": _pad_rows(small["conv_mix_w"][l], 8),
        "conv_ffn_w": _pad_rows(jnp.concatenate([blk(cf, 0), blk(cf, 2), blk(cf, 1), blk(cf, 3)], axis=1), 8),
        "pre_mix_g": wts["pre_mix_g"][l].reshape(1, -1), "post_mix_g": wts["post_mix_g"][l].reshape(1, -1),
        "pre_ffn_g": wts["pre_ffn_g"][l].reshape(1, -1), "post_ffn_g": wts["post_ffn_g"][l].reshape(1, -1),
        "ln_g": wts["sgu_ln_g"][l].reshape(1, -1), "ln_b": wts["sgu_ln_b"][l].reshape(1, -1),
        "sgu_w": wts["sgu_w"][l],
        "sgu_bias": jnp.repeat(wts["sgu_b"][l].T, HEAD_DIM, axis=1),
    }


def _prep_late(lands):
    up = lands["w_ffn_up"]
    return {
        "w_att": _whole_cols(lands["w_branch_att"]), "w_conv": _whole_cols(lands["w_branch_conv"]),
        "w_sgu": _whole_cols(lands["w_branch_sgu"]),
        "w_out": lands["w_out"].reshape(D_MODEL, D_MODEL),
        "w_up": jnp.concatenate([up[0], up[2], up[1], up[3]], axis=1),
        "w_down": lands["w_ffn_down"].reshape(D_FF, D_MODEL),
    }


def _layer_fwd(x, p, dep=None, late=None):
    s = x.shape[0]
    xn = _rms_fwd(x, p["pre_mix_g"], "rms_pre_mix", dep)
    h = _mm(xn, p["w_p"], "nn", BF16, "mm_in", s, 512, D_MODEL)
    f_row = _mm(p["wf_t"], xn, "nt", F32, "mm_forget", F_ROWS, 2048, D_MODEL)
    ck = _gate_fwd(f_row, p["b_forget"], "gate_fwd")
    o, o_f32, lse = _attn_fwd(h, ck, "attn_fwd")
    yc = _sconv_fwd(h, p["conv_mix_w"], "sconv_fwd")
    ys = _sgu_fwd(h, p["ln_g"], p["ln_b"], p["sgu_w"], p["sgu_bias"], "sgu_fwd")
    if late is not None:
        p.update(late(o))
    merged = _merge_fwd(h, (o, yc, ys), (p["w_att"], p["w_conv"], p["w_sgu"]), p["b_gate"], "merge_fwd")
    mo = _mm(merged, p["w_out"], "nn", F32, "mm_out", 2048, 512, D_MODEL)
    x1 = _resid_post(x, mo, p["post_mix_g"], "post_mix")
    xn2 = _rms_fwd(x1, p["pre_ffn_g"], "rms_pre_ffn")
    h2 = _mm(xn2, p["w_up"], "nn", BF16, "mm_up", 2048, 512, D_MODEL)
    pact = _ffn_act_fwd(h2, p["conv_ffn_w"], "ffn_act_fwd")
    ff = _mm(pact, p["w_down"], "nn", F32, "mm_down", 1024, D_MODEL, D_FF)
    x2 = _resid_post(x1, ff, p["post_ffn_g"], "post_ffn")
    saved = dict(x=x, xn=xn, h=h, f_row=f_row, ck=ck, o=o, o_f32=o_f32, lse=lse, yc=yc, ys=ys, merged=merged, mo=mo, x1=x1,
                 xn2=xn2, h2=h2, pact=pact, ff=ff)
    return x2, saved


def _layer_bwd(dx2, p, sv, dep=None, early=None):
    s = dx2.shape[0]
    g = {}
    same = lambda b: b
    dff, g["post_ffn_g"] = _rms_bwd(sv["ff"], p["post_ffn_g"], [dx2], None, BF16, "post_ffn_bwd", dep)
    dpact = _mm(dff, p["w_down"], "nt", BF16, "mm_down_dx", 2048, FF_BLK, D_MODEL)
    t_down = _mm(sv["pact"], dff, "tn", BF16, "mm_down_dw", 256, D_MODEL, s).reshape(N_CHIPS, -1, D_MODEL)
    dh2, dconv_ffn = _ffn_act_conv_bwd(sv["h2"], p["conv_ffn_w"], dpact, "ffn_act_conv_bwd")
    dxn2 = _mm(dh2, p["w_up"], "nt", F32, "mm_up_dx", 512, D_MODEL, 2 * D_FF)
    t_up = _mm(sv["xn2"], dh2, "tn", BF16, "mm_up_dw", 512, FF_BLK, s, chip_of=lambda b: (b % 2) * 2 + b // 2)
    dx1, g["pre_ffn_g"] = _rms_bwd(sv["x1"], p["pre_ffn_g"], [dxn2], dx2, F32, "pre_ffn_bwd")
    dep_mix = early([t_up, t_down]) if early is not None else None
    dmo, g["post_mix_g"] = _rms_bwd(sv["mo"], p["post_mix_g"], [dx1], None, BF16, "post_mix_bwd", dep_mix)
    dmerged = _mm(dmo, p["w_out"], "nt", F32, "mm_out_dx", 2048, 512, D_MODEL)
    t_out = _mm(sv["merged"], dmo, "tn", BF16, "mm_out_dw", 512, D_MODEL, s).reshape(N_CHIPS, -1, D_MODEL)
    acts = (sv["o"], sv["yc"], sv["ys"])
    ws = (p["w_att"], p["w_conv"], p["w_sgu"])
    dy_a, dy_c, dy_s, dgl, db_gate = _merge_bwd(sv["h"], acts, ws, p["b_gate"], dmerged, "merge_bwd")
    do = _mm(dy_a, p["w_att"], "nt", BF16, "mm_att_dx", 2048, D_ATT, D_MODEL)
    dyc = _mm(dy_c, p["w_conv"], "nt", BF16, "mm_conv_dx", 2048, D_CONV, D_MODEL)
    dys = _mm(dy_s, p["w_sgu"], "nt", BF16, "mm_sgu_dx", 2048, D_SGU, D_MODEL)
    t_att = _mm(sv["o"], dy_a, "tn", BF16, "mm_att_dw", D_ATT, 256, s, chip_of=same)
    t_conv = _mm(sv["yc"], dy_c, "tn", BF16, "mm_conv_dw", D_CONV, 256, s, chip_of=same)
    t_sgu = _mm(sv["ys"], dy_s, "tn", BF16, "mm_sgu_dw", D_SGU, 256, s, chip_of=same)
    d_conv, dconv_mix = _sconv_bwd(sv["h"], p["conv_mix_w"], dyc, "sconv_bwd")
    d_sgu, g["sgu_ln_g"], g["sgu_ln_b"], g["sgu_w"], dbias = _sgu_bwd(
        sv["h"], p["ln_g"], p["ln_b"], p["sgu_w"], p["sgu_bias"], dys, "sgu_bwd")
    dq, dk, dv, dc_even, dc_odd = _attn_bwd(sv["h"], sv["ck"], sv["o_f32"], sv["lse"], do, "attn_bwd")
    df, db_forget = _gate_bwd(sv["f_row"], p["b_forget"], dc_even, dc_odd, "gate_bwd")
    f_cols = jnp.concatenate([df[:N_HEADS].T, jnp.zeros((s, IN_PAD - IN_WIDTH), BF16)], axis=1)
    dh = _assemble_dh(dgl, [dq, dk, dv, d_conv, d_sgu, f_cols], "assemble_dh")
    dxn = _mm(dh, p["w_p"], "nt", F32, "mm_in_dx", 512, D_MODEL, IN_PAD)
    dw_p = _mm(sv["xn"], dh, "tn", BF16, "mm_in_dw", D_MODEL, 512, s)
    t_in = jnp.stack([jnp.concatenate(_local_cols(dw_p, j * IN_SHARD, j * IN_SHARD + IN_SHARD_PAD), axis=1)
                      for j in range(N_CHIPS)])
    dx, g["pre_mix_g"] = _rms_bwd(sv["x"], p["pre_mix_g"], [dxn], dx1, F32, "pre_mix_bwd")
    blk = lambda a, j: a[:, j * FF_BLK:(j + 1) * FF_BLK]
    g["conv_ffn_w"] = jnp.concatenate([blk(dconv_ffn, 0), blk(dconv_ffn, 2), blk(dconv_ffn, 1),
                                       blk(dconv_ffn, 3)], axis=1)[:3]
    g["conv_mix_w"] = dconv_mix[:3]
    g["b_gate"] = db_gate[:3]
    g["b_forget"] = db_forget[:N_HEADS, 0]
    g["sgu_b"] = jnp.sum(dbias.reshape(CHUNK, N_GROUPS, HEAD_DIM), axis=-1).T
    for n in ("pre_mix_g", "post_mix_g", "pre_ffn_g", "post_ffn_g", "sgu_ln_g", "sgu_ln_b"):
        g[n] = g[n].reshape(-1)
    mix = [t_in, t_att, t_conv, t_sgu, t_out]
    return dx, (mix if early is not None else mix + [t_up, t_down]), g


def _assemble_dh(dh, pieces, name):
    s = dh.shape[0]
    t = _tile(s, 512)
    width = sum(a.shape[1] for a in pieces)
    assert 2 * width == dh.shape[1]

    def body(*refs):
        out = refs[-1]
        col = 0
        for ref in refs[1:-1]:
            w = ref.shape[1]
            out[:, col:col + w] = ref[...].astype(out.dtype)
            col += w

    return pl.pallas_call(
        body, name=name, grid=(s // t,),
        in_specs=[_ANY] + [pl.BlockSpec((t, a.shape[1]), lambda i: (i, 0)) for a in pieces],
        out_specs=pl.BlockSpec((t, width), lambda i: (i, 1)),
        out_shape=jax.ShapeDtypeStruct(dh.shape, dh.dtype),
        input_output_aliases={0: 0},
        compiler_params=_params(("parallel",)),
    )(dh, *pieces)


def _shard_cols(a, j):
    w = a.shape[-1] // N_CHIPS
    return a[..., j * w:(j + 1) * w]


def kernel(x, pre_mix_g, post_mix_g, pre_ffn_g, post_ffn_g, w_in, b_forget, b_gate, conv_mix_w, sgu_ln_g, sgu_ln_b, sgu_w, sgu_b, w_branch_att, w_branch_conv, w_branch_sgu, w_out, w_ffn_up, conv_ffn_w, w_ffn_down, loss_target, m_pre_mix_g, m_post_mix_g, m_pre_ffn_g, m_post_ffn_g, m_w_in, m_b_forget, m_b_gate, m_conv_mix_w, m_sgu_ln_g, m_sgu_ln_b, m_sgu_w, m_sgu_b, m_w_branch_att, m_w_branch_conv, m_w_branch_sgu, m_w_out, m_w_ffn_up, m_conv_ffn_w, m_w_ffn_down, v_pre_mix_g, v_post_mix_g, v_pre_ffn_g, v_post_ffn_g, v_w_in, v_b_forget, v_b_gate, v_conv_mix_w, v_sgu_ln_g, v_sgu_ln_b, v_sgu_w, v_sgu_b, v_w_branch_att, v_w_branch_conv, v_w_branch_sgu, v_w_out, v_w_ffn_up, v_conv_ffn_w, v_w_ffn_down):
    wts = dict(pre_mix_g=pre_mix_g, post_mix_g=post_mix_g, pre_ffn_g=pre_ffn_g, post_ffn_g=post_ffn_g, w_in=w_in,
               b_forget=b_forget, b_gate=b_gate, conv_mix_w=conv_mix_w, sgu_ln_g=sgu_ln_g, sgu_ln_b=sgu_ln_b,
               sgu_w=sgu_w, sgu_b=sgu_b, w_branch_att=w_branch_att, w_branch_conv=w_branch_conv,
               w_branch_sgu=w_branch_sgu, w_out=w_out, w_ffn_up=w_ffn_up, conv_ffn_w=conv_ffn_w,
               w_ffn_down=w_ffn_down)
    moms = dict(pre_mix_g=m_pre_mix_g, post_mix_g=m_post_mix_g, pre_ffn_g=m_pre_ffn_g, post_ffn_g=m_post_ffn_g,
                w_in=m_w_in, b_forget=m_b_forget, b_gate=m_b_gate, conv_mix_w=m_conv_mix_w, sgu_ln_g=m_sgu_ln_g,
                sgu_ln_b=m_sgu_ln_b, sgu_w=m_sgu_w, sgu_b=m_sgu_b, w_branch_att=m_w_branch_att,
                w_branch_conv=m_w_branch_conv, w_branch_sgu=m_w_branch_sgu, w_out=m_w_out, w_ffn_up=m_w_ffn_up,
                conv_ffn_w=m_conv_ffn_w, w_ffn_down=m_w_ffn_down)
    vels = dict(pre_mix_g=v_pre_mix_g, post_mix_g=v_post_mix_g, pre_ffn_g=v_pre_ffn_g, post_ffn_g=v_post_ffn_g,
                w_in=v_w_in, b_forget=v_b_forget, b_gate=v_b_gate, conv_mix_w=v_conv_mix_w, sgu_ln_g=v_sgu_ln_g,
                sgu_ln_b=v_sgu_ln_b, sgu_w=v_sgu_w, sgu_b=v_sgu_b, w_branch_att=v_w_branch_att,
                w_branch_conv=v_w_branch_conv, w_branch_sgu=v_w_branch_sgu, w_out=v_w_out, w_ffn_up=v_w_ffn_up,
                conv_ffn_w=v_conv_ffn_w, w_ffn_down=v_w_ffn_down)

    c_idx = lax.axis_index("c").astype(jnp.int32).reshape(1)
    me_idx = (2 * lax.axis_index("x") + lax.axis_index("y")).astype(jnp.int32).reshape(1)
    small = _gather_small(wts)

    xs = x[0]
    layers, saved = [], []
    first = _gather_begin(wts, 0, me_idx, FIRST_NAMES, "first")
    rest = _gather_begin(wts, 0, me_idx, LATE_NAMES, "late")
    lands = _gather_finish(first, xs, FIRST_NAMES, "first")
    late = lambda after: _prep_late(_gather_finish(rest, after, LATE_NAMES, "late"))
    for l in range(DEPTH):
        p = _prep_first(wts, lands, small, l)
        if l > 0:
            p.update(_prep_late(lands))
        nxt = _gather_begin(wts, l + 1, me_idx, BIG_NAMES, "all") if l + 1 < DEPTH else None
        dep = ([nxt[4]] if nxt else []) + ([rest[4]] if l == 0 else [])
        xs, sv = _layer_fwd(xs, p, dep or None, late if l == 0 else None)
        if nxt:
            lands = _gather_finish(nxt, xs, BIG_NAMES, "all")
        layers.append(p)
        saved.append(sv)
    dy, loss_part = _loss_head(xs, loss_target[0], "loss_head")
    loss = lax.psum(loss_part[0, 0], ("x", "y", "c"))

    big_red = [None] * DEPTH
    small_grads = [None] * DEPTH
    pending = None
    ffn = []
    for l in reversed(range(DEPTH)):
        early = None
        if l == 0:
            def early(ts_ffn):
                ffn.append(_rs_begin(ts_ffn, c_idx, me_idx, "ffn"))
                return ffn[0][4]
        dy, ts, small_grads[l] = _layer_bwd(dy, layers[l], saved[l], pending[4] if pending else None, early)
        if pending:
            big_red[l + 1] = _rs_finish(pending, dy, c_idx, "big")
        pending = _rs_begin(ts, c_idx, me_idx, "mix" if l == 0 else "big")
    red_ffn = _rs_finish(ffn[0], dy, c_idx, "ffn")
    grad_x = dy[None]

    rep_flat = jnp.concatenate([small_grads[l][n].reshape(-1) for l in range(DEPTH) for n, _ in REPLICATED])
    rep_flat = jnp.pad(rep_flat, (0, N_CHIPS * _REP_QUARTER - rep_flat.shape[0]))
    rows = []
    for j in range(N_CHIPS):
        pieces = [_shard_cols(small_grads[l][n], j) for l in range(DEPTH) for n, _ in SMALL_SHARDS]
        pieces.append(rep_flat[j * _REP_QUARTER:(j + 1) * _REP_QUARTER])
        rows.append(_pack_rows(pieces, SMALL_ROWS, F32))
    small_red = _reduce_scatter_chips(jnp.stack(rows), "small")
    small_all = _all_gather_chips(small_red, "gather_small")
    big_red[0] = _rs_finish(pending, small_all, c_idx, "mix") + red_ffn
    small_all = small_all.reshape(N_CHIPS, -1)

    grads = {}
    for i, (n, _) in enumerate(BIG_SHARDS):
        grads[n] = jnp.stack([big_red[l][i][:, :IN_SHARD] if n == "w_in" else big_red[l][i] for l in range(DEPTH)])
    mine_small = small_red.reshape(-1)
    parts = _unpack(mine_small, [s for _ in range(DEPTH) for _, s in SMALL_SHARDS])
    for i, (n, _) in enumerate(SMALL_SHARDS):
        grads[n] = jnp.stack([parts[l * len(SMALL_SHARDS) + i] for l in range(DEPTH)])
    off = DEPTH * _SMALL_ELEMS
    rep_all = jnp.concatenate([small_all[j, off:off + _REP_QUARTER] for j in range(N_CHIPS)])
    parts = _unpack(rep_all, [s for _ in range(DEPTH) for _, s in REPLICATED])
    for i, (n, _) in enumerate(REPLICATED):
        grads[n] = jnp.stack([parts[l * len(REPLICATED) + i] for l in range(DEPTH)])

    deltas, new_m, new_v = {}, {}, {}
    for n in WEIGHT_ORDER:
        deltas[n], new_m[n], new_v[n] = _adamw(wts[n], grads[n], moms[n], vels[n], "adamw_" + n)
    return (loss, grad_x, *[grads[n] for n in WEIGHT_ORDER], *[deltas[n] for n in WEIGHT_ORDER],
            *[new_m[n] for n in WEIGHT_ORDER], *[new_v[n] for n in WEIGHT_ORDER])
```

```python
import functools
import math

import jax
import jax.numpy as jnp
from jax import lax
from jax.experimental import pallas as pl
from jax.experimental.pallas import tpu as pltpu

F32 = jnp.float32
BF16 = jnp.bfloat16
MXU_DTYPE = jnp.bfloat16

D_MODEL = 1024
HEAD_DIM = 64
N_HEADS = 8
D_ATT = 512
D_CONV = 256
D_SGU = 256
N_GROUPS = 4
CHUNK = 128
D_FF = 2816
DEPTH = 4
RMS_EPS = 1e-6
LN_EPS = 1e-5
N_CHIPS = 4
LANES = 128
PACK_COLS = 1024
HALO = 16

ADAM_LR = 0.001
ADAM_B1 = 0.9
ADAM_B2 = 0.999
ADAM_EPS = 1e-08
ADAM_WD = 0.01
ADAM_STEP = 10

OFF_GL = 0
OFF_Q = 3 * D_MODEL
OFF_K = OFF_Q + D_ATT
OFF_V = OFF_K + D_ATT
OFF_BG = OFF_V + D_ATT
OFF_CG = OFF_BG + D_CONV
OFF_HC = OFF_CG + D_CONV
OFF_U = OFF_HC + D_CONV
OFF_VS = OFF_U + D_SGU
W_P = OFF_VS + D_SGU
F_ROWS = 16

VMEM_LIMIT = 56 * 1024 * 1024
MESH = pl.DeviceIdType.MESH


def _params(sem=None):
    if sem is None:
        return pltpu.CompilerParams(vmem_limit_bytes=VMEM_LIMIT)
    return pltpu.CompilerParams(dimension_semantics=sem, vmem_limit_bytes=VMEM_LIMIT)


def _tile(dim, pref):
    if dim <= pref:
        return dim
    if dim % pref == 0:
        return pref
    return dim


_DIMS = {"nn": (((1,), (0,)), ((), ())), "nt": (((1,), (1,)), ((), ())), "tn": (((0,), (0,)), ((), ()))}


def _mm(a, b, mode, out_dtype, name, tm, tn, tk, chip_of=None):
    if mode == "tn":
        K, M = a.shape
    else:
        M, K = a.shape
    N = b.shape[0] if mode == "nt" else b.shape[1]
    tm, tn, tk = _tile(M, tm), _tile(N // N_CHIPS if chip_of else N, tn), _tile(K, tk)
    nk = K // tk
    dims = _DIMS[mode]

    def body(a_ref, b_ref, o_ref, *acc):
        part = lax.dot_general(a_ref[...].astype(MXU_DTYPE), b_ref[...].astype(MXU_DTYPE), dims,
                               preferred_element_type=F32)
        if nk == 1:
            o_ref[...] = part.astype(o_ref.dtype)
        else:
            acc_ref = acc[0]
            k = pl.program_id(2)

            @pl.when(k == 0)
            def _():
                acc_ref[...] = part

            @pl.when(k > 0)
            def _():
                acc_ref[...] += part

            @pl.when(k == nk - 1)
            def _():
                o_ref[...] = acc_ref[...].astype(o_ref.dtype)

    if mode == "tn":
        a_spec = pl.BlockSpec((tk, tm), lambda i, j, k: (k, i))
    else:
        a_spec = pl.BlockSpec((tm, tk), lambda i, j, k: (i, k))
    if mode == "nt":
        b_spec = pl.BlockSpec((tn, tk), lambda i, j, k: (j, k))
    else:
        b_spec = pl.BlockSpec((tk, tn), lambda i, j, k: (k, j))
    if chip_of is None:
        out_spec = pl.BlockSpec((tm, tn), lambda i, j, k: (i, j))
        out_shape = jax.ShapeDtypeStruct((M, N), out_dtype)
    else:
        per = (N // N_CHIPS) // tn
        out_spec = pl.BlockSpec((None, tm, tn), lambda i, j, k: (chip_of(j // per), i, j % per))
        out_shape = jax.ShapeDtypeStruct((N_CHIPS, M, N // N_CHIPS), out_dtype)
    return pl.pallas_call(
        body,
        name=name,
        grid=(M // tm, N // tn, nk),
        in_specs=[a_spec, b_spec],
        out_specs=out_spec,
        out_shape=out_shape,
        scratch_shapes=[pltpu.VMEM((tm, tn), F32)] if nk > 1 else [],
        compiler_params=_params(("parallel", "parallel", "arbitrary")),
    )(a, b)


_GELU_K = math.sqrt(2.0 / math.pi)
_GELU_C = 0.044715


def _gelu(x):
    t = jnp.tanh(_GELU_K * (x + _GELU_C * (x * x * x)))
    return x * (0.5 * (1.0 + t))


def _gelu_and_grad(x):
    x2 = x * x
    t = jnp.tanh(_GELU_K * (x + _GELU_C * (x2 * x)))
    cdf = 0.5 * (1.0 + t)
    dcdf = 0.5 * (1.0 - t * t) * (_GELU_K * (1.0 + 3.0 * _GELU_C * x2))
    return x * cdf, cdf + x * dcdf


def _sigmoid(x):
    return 1.0 / (1.0 + jnp.exp(-x))


def _shift_down(cur, prev, k):
    h = prev.shape[0]
    ext = jnp.concatenate([prev, cur], axis=0)
    return pltpu.roll(ext, k, 0)[h:]


def _shift_up(cur, nxt, k):
    t, h = cur.shape[0], nxt.shape[0]
    ext = jnp.concatenate([cur, nxt], axis=0)
    return pltpu.roll(ext, t + h - k, 0)[:t]


def _row_sum8(x):
    t, c = x.shape
    return jnp.sum(x.reshape(t // 8, 8, c), axis=0)


_DEP = pl.BlockSpec((8, LANES), lambda i: (0, 0))


def _rms_fwd(x, g, name, dep=None):
    s, d = x.shape
    t = _tile(s, 512)

    def body(x_ref, g_ref, *rest):
        o_ref = rest[-1]
        xv = x_ref[...]
        r = lax.rsqrt(jnp.mean(xv * xv, axis=-1, keepdims=True) + RMS_EPS)
        o_ref[...] = (xv * r * g_ref[...]).astype(o_ref.dtype)

    deps = [] if dep is None else list(dep) if isinstance(dep, (list, tuple)) else [dep]
    return pl.pallas_call(
        body, name=name, grid=(s // t,),
        in_specs=[pl.BlockSpec((t, d), lambda i: (i, 0)), pl.BlockSpec((1, d), lambda i: (0, 0))] + [_DEP] * len(deps),
        out_specs=pl.BlockSpec((t, d), lambda i: (i, 0)),
        out_shape=jax.ShapeDtypeStruct((s, d), BF16),
        compiler_params=_params(("parallel",)),
    )(x, g, *deps)


def _resid_post(x, y, g, name):
    s, d = x.shape
    t = _tile(s, 512)

    def body(x_ref, y_ref, g_ref, o_ref):
        yv = y_ref[...]
        r = lax.rsqrt(jnp.mean(yv * yv, axis=-1, keepdims=True) + RMS_EPS)
        o_ref[...] = x_ref[...] + yv * r * g_ref[...]

    row = pl.BlockSpec((t, d), lambda i: (i, 0))
    return pl.pallas_call(
        body, name=name, grid=(s // t,),
        in_specs=[row, row, pl.BlockSpec((1, d), lambda i: (0, 0))],
        out_specs=row,
        out_shape=jax.ShapeDtypeStruct((s, d), F32),
        compiler_params=_params(("parallel",)),
    )(x, y, g)


def _resid_post_norm(x, y, g, g_next, name):
    s, d = x.shape
    t = _tile(s, 512)

    def body(x_ref, y_ref, g_ref, gn_ref, o_ref, xn_ref):
        yv = y_ref[...]
        r = lax.rsqrt(jnp.mean(yv * yv, axis=-1, keepdims=True) + RMS_EPS)
        x1 = x_ref[...] + yv * r * g_ref[...]
        o_ref[...] = x1
        r1 = lax.rsqrt(jnp.mean(x1 * x1, axis=-1, keepdims=True) + RMS_EPS)
        xn_ref[...] = (x1 * r1 * gn_ref[...]).astype(xn_ref.dtype)

    row = pl.BlockSpec((t, d), lambda i: (i, 0))
    vec = pl.BlockSpec((1, d), lambda i: (0, 0))
    return pl.pallas_call(
        body, name=name, grid=(s // t,),
        in_specs=[row, row, vec, vec],
        out_specs=[row, row],
        out_shape=[jax.ShapeDtypeStruct((s, d), F32), jax.ShapeDtypeStruct((s, d), BF16)],
        compiler_params=_params(("parallel",)),
    )(x, y, g, g_next)


def _rms_bwd(xin, g, dys, dres, out_dtype, name, dep=None):
    s, d = xin.shape
    t = _tile(s, 512)
    n = s // t
    n_dy = len(dys)
    has_res = dres is not None
    deps = [] if dep is None else [dep]

    def body(*refs):
        x_ref, g_ref = refs[0], refs[1]
        dy_refs = refs[2:2 + n_dy]
        pos = 2 + n_dy
        res_ref = refs[pos] if has_res else None
        pos += (1 if has_res else 0) + len(deps)
        dx_ref, dg_ref, acc_ref = refs[pos], refs[pos + 1], refs[pos + 2]
        i = pl.program_id(0)
        xv = x_ref[...]
        dy = dy_refs[0][...].astype(F32)
        for extra in dy_refs[1:]:
            dy = dy + extra[...].astype(F32)
        r = lax.rsqrt(jnp.mean(xv * xv, axis=-1, keepdims=True) + RMS_EPS)
        u = dy * g_ref[...]
        xr = xv * r
        dx = r * (u - xr * jnp.mean(u * xr, axis=-1, keepdims=True))
        if has_res:
            dx = dx + res_ref[...]
        dx_ref[...] = dx.astype(dx_ref.dtype)
        part = _row_sum8(dy * xr)

        @pl.when(i == 0)
        def _():
            acc_ref[...] = part

        @pl.when(i > 0)
        def _():
            acc_ref[...] += part

        @pl.when(i == n - 1)
        def _():
            dg_ref[...] = jnp.sum(acc_ref[...], axis=0, keepdims=True)

    row = pl.BlockSpec((t, d), lambda i: (i, 0))
    vec = pl.BlockSpec((1, d), lambda i: (0, 0))
    ins = [xin, g, *dys] + ([dres] if has_res else []) + deps
    return pl.pallas_call(
        body, name=name, grid=(n,),
        in_specs=[row, vec] + [row] * (n_dy + (1 if has_res else 0)) + [_DEP] * len(deps),
        out_specs=[row, vec],
        out_shape=[jax.ShapeDtypeStruct((s, d), out_dtype), jax.ShapeDtypeStruct((1, d), F32)],
        scratch_shapes=[pltpu.VMEM((8, d), F32)],
        compiler_params=_params(("arbitrary",)),
    )(*ins)


def _loss_head(y, target, name):
    s, d = y.shape
    t = _tile(s, 512)
    n = s // t

    def body(y_ref, t_ref, dy_ref, loss_ref, acc_ref):
        i = pl.program_id(0)
        e = y_ref[...] - t_ref[...]
        dy_ref[...] = e * (1.0 / d)
        part = _row_sum8(e * e)

        @pl.when(i == 0)
        def _():
            acc_ref[...] = part

        @pl.when(i > 0)
        def _():
            acc_ref[...] += part

        @pl.when(i == n - 1)
        def _():
            tot = jnp.sum(jnp.sum(acc_ref[...], axis=0, keepdims=True), axis=1, keepdims=True)
            loss_ref[...] = tot * (0.5 / d)

    row = pl.BlockSpec((t, d), lambda i: (i, 0))
    return pl.pallas_call(
        body, name=name, grid=(n,),
        in_specs=[row, row],
        out_specs=[row, pl.BlockSpec((1, 1), lambda i: (0, 0))],
        out_shape=[jax.ShapeDtypeStruct((s, d), F32), jax.ShapeDtypeStruct((1, 1), F32)],
        scratch_shapes=[pltpu.VMEM((8, d), F32)],
        compiler_params=_params(("arbitrary",)),
    )(y, target)


def _split3(x):
    hi = x.astype(BF16)
    r1 = x - hi.astype(F32)
    mid = r1.astype(BF16)
    lo = (r1 - mid.astype(F32)).astype(BF16)
    return hi, mid, lo


def _tri_dot(x, tri):
    hi, mid, lo = _split3(x)
    dn = _DIMS["nn"]
    out = lax.dot_general(hi, tri, dn, preferred_element_type=F32)
    out = out + lax.dot_general(mid, tri, dn, preferred_element_type=F32)
    return out + lax.dot_general(lo, tri, dn, preferred_element_type=F32)


def _log_sigmoid(z):
    return jnp.minimum(z, 0.0) - jnp.log(1.0 + jnp.exp(-jnp.abs(z)))


def _gate_fwd(f_row, b_col, name):
    rows, s = f_row.shape
    t = _tile(s, 512)
    n = s // t

    def body(f_ref, b_ref, ck_ref, carry_ref):
        i = pl.program_id(0)

        @pl.when(i == 0)
        def _():
            carry_ref[...] = jnp.zeros_like(carry_ref)

        logf = _log_sigmoid(f_ref[...] + b_ref[...])
        r = lax.broadcasted_iota(jnp.int32, (t, t), 0)
        c = lax.broadcasted_iota(jnp.int32, (t, t), 1)
        tri = jnp.where(r <= c, 1.0, 0.0).astype(BF16)
        cs = _tri_dot(logf, tri) + carry_ref[...]
        carry_ref[...] = cs[:, t - 1:t]
        terms = [part.astype(F32) for part in _split3(-cs)]
        sub = lax.broadcasted_iota(jnp.int32, (LANES, t), 0)
        for p in range(N_HEADS // 2):
            stacked = jnp.zeros((LANES, t), F32)
            for hh in range(2):
                for j, term in enumerate(terms):
                    h = 2 * p + hh
                    stacked = jnp.where(sub == 3 * hh + j, jnp.broadcast_to(term[h:h + 1, :], (LANES, t)), stacked)
            ck_ref[p] = stacked.T.astype(ck_ref.dtype)

    return pl.pallas_call(
        body, name=name, grid=(n,),
        in_specs=[pl.BlockSpec((rows, t), lambda i: (0, i)), pl.BlockSpec((rows, 1), lambda i: (0, 0))],
        out_specs=pl.BlockSpec((N_HEADS // 2, t, LANES), lambda i: (0, i, 0)),
        out_shape=jax.ShapeDtypeStruct((N_HEADS // 2, s, LANES), BF16),
        scratch_shapes=[pltpu.VMEM((rows, 1), F32)],
        compiler_params=_params(("arbitrary",)),
    )(f_row, b_col)


def _gate_bwd(f_row, b_col, dc_even, dc_odd, name):
    rows, s = f_row.shape
    t = _tile(s, 512)
    n = s // t

    def body(f_ref, b_ref, dce_ref, dco_ref, df_ref, db_ref, carry_ref, acc_ref):
        i = pl.program_id(0)

        @pl.when(i == 0)
        def _():
            carry_ref[...] = jnp.zeros_like(carry_ref)
            acc_ref[...] = jnp.zeros_like(acc_ref)

        head = lax.broadcasted_iota(jnp.int32, (rows, t), 0)
        dcv = jnp.zeros((rows, t), F32)
        for h in range(N_HEADS):
            src = dce_ref if h % 2 == 0 else dco_ref
            dcv = jnp.where(head == h, jnp.broadcast_to(src[h // 2, 0:1, :], (rows, t)), dcv)
        r = lax.broadcasted_iota(jnp.int32, (t, t), 0)
        c = lax.broadcasted_iota(jnp.int32, (t, t), 1)
        tri = jnp.where(r >= c, 1.0, 0.0).astype(BF16)
        dlogf = _tri_dot(dcv, tri) + carry_ref[...]
        carry_ref[...] = dlogf[:, 0:1]
        z = f_ref[...] + b_ref[...]
        df = dlogf * _sigmoid(-z)
        df_ref[...] = df.astype(df_ref.dtype)
        acc_ref[...] += jnp.sum(df, axis=1, keepdims=True)

        @pl.when(i == n - 1)
        def _():
            db_ref[...] = acc_ref[...]

    rev = lambda i: (0, n - 1 - i)
    dc_spec = pl.BlockSpec((N_HEADS // 2, 8, t), lambda i: (0, 0, n - 1 - i))
    return pl.pallas_call(
        body, name=name, grid=(n,),
        in_specs=[pl.BlockSpec((rows, t), rev), pl.BlockSpec((rows, 1), lambda i: (0, 0)), dc_spec, dc_spec],
        out_specs=[pl.BlockSpec((rows, t), rev), pl.BlockSpec((rows, 1), lambda i: (0, 0))],
        out_shape=[jax.ShapeDtypeStruct((rows, s), BF16), jax.ShapeDtypeStruct((rows, 1), F32)],
        scratch_shapes=[pltpu.VMEM((rows, 1), F32), pltpu.VMEM((rows, 1), F32)],
        compiler_params=_params(("arbitrary",)),
    )(f_row, b_col, dc_even, dc_odd)


_NEG = -1e30
_SCALE = HEAD_DIM ** -0.5


def _head_masks():
    lane = lax.broadcasted_iota(jnp.int32, (1, LANES), 1)
    return [lane < HEAD_DIM, lane >= HEAD_DIM]


def _attn_fwd(h, ck, name):
    s = h.shape[0]
    t = _tile(s, 512)
    n = s // t
    qb, kb, vb = OFF_Q // LANES, OFF_K // LANES, OFF_V // LANES

    pairs = [(qi, ki) for qi in range(n) for ki in range(qi + 1)]
    qi_tab = jnp.asarray([qi for qi, _ in pairs], jnp.int32)
    ki_tab = jnp.asarray([ki for _, ki in pairs], jnp.int32)

    def body(qi_ref, ki_ref, q_ref, k_ref, v_ref, ck_ref, o_ref, of_ref, lse_ref, m_ref, l_ref, acc_ref):
        qi, ki = qi_ref[pl.program_id(1)], ki_ref[pl.program_id(1)]
        masks = _head_masks()
        lane = lax.broadcasted_iota(jnp.int32, (1, LANES), 1)

        @pl.when(ki == 0)
        def _():
            m_ref[...] = jnp.full_like(m_ref, _NEG)
            l_ref[...] = jnp.zeros_like(l_ref)
            acc_ref[...] = jnp.zeros_like(acc_ref)

        def step(diag):
            q = q_ref[...] * _SCALE
            k_aug = jnp.concatenate([k_ref[...], ck_ref[0]], axis=1)
            v = v_ref[...]
            nq = max(1, t // 256)
            wq = t // nq
            chains = [(hh, j) for hh in range(2) for j in range(nq)]
            scores = []
            for hh, j in chains:
                qs = q[j * wq:(j + 1) * wq]
                ones = jnp.where((lane >= 3 * hh) & (lane < 3 * hh + 3), 1.0, 0.0).astype(q.dtype)
                q_aug = jnp.concatenate([jnp.where(masks[hh], qs, jnp.zeros_like(qs)),
                                         jnp.broadcast_to(ones, qs.shape)], axis=1)
                scores.append(lax.dot_general(k_aug, q_aug, _DIMS["nt"], preferred_element_type=F32))
            probs = []
            for (hh, j), sc in zip(chains, scores):
                cols = slice(j * wq, (j + 1) * wq)
                if diag:
                    r = lax.broadcasted_iota(jnp.int32, (t, wq), 0)
                    cc = lax.broadcasted_iota(jnp.int32, (t, wq), 1) + j * wq
                    sc = jnp.where(r <= cc, sc, _NEG)
                m_prev = m_ref[hh, :, cols]
                m_new = jnp.maximum(m_prev, jnp.max(sc, axis=0, keepdims=True))
                alpha = jnp.exp(m_prev - m_new)
                p = jnp.exp(sc - m_new)
                l_ref[hh, :, cols] = alpha * l_ref[hh, :, cols] + jnp.sum(p, axis=0, keepdims=True)
                m_ref[hh, :, cols] = m_new
                p_hi = p.astype(MXU_DTYPE)
                p_lo = (p - p_hi.astype(F32)).astype(MXU_DTYPE)
                probs.append((alpha, p_hi, p_lo))
            for (hh, j), (alpha, p_hi, p_lo) in zip(chains, probs):
                pv = (lax.dot_general(v, p_hi, _DIMS["tn"], preferred_element_type=F32)
                      + lax.dot_general(v, p_lo, _DIMS["tn"], preferred_element_type=F32))
                rows = slice(hh * HEAD_DIM, (hh + 1) * HEAD_DIM)
                cols = slice(j * wq, (j + 1) * wq)
                acc_ref[rows, cols] = alpha * acc_ref[rows, cols] + pv[rows]

        @pl.when(ki < qi)
        def _():
            step(False)

        @pl.when(ki == qi)
        def _():
            step(True)
            inv = jnp.concatenate([jnp.broadcast_to(1.0 / l_ref[hh], (HEAD_DIM, t)) for hh in range(2)], axis=0)
            out = (acc_ref[...] * inv).T
            o_ref[...] = out.astype(o_ref.dtype)
            of_ref[...] = out
            lse = jnp.concatenate([jnp.broadcast_to(m_ref[hh] + jnp.log(l_ref[hh]), (HEAD_DIM, t))
                                   for hh in range(2)], axis=0)
            lse_ref[...] = lse.T

    grid_spec = pltpu.PrefetchScalarGridSpec(
        num_scalar_prefetch=2, grid=(N_HEADS // 2, len(pairs)),
        in_specs=[
            pl.BlockSpec((t, LANES), lambda p, i, qt, kt: (qt[i], qb + p)),
            pl.BlockSpec((t, LANES), lambda p, i, qt, kt: (kt[i], kb + p)),
            pl.BlockSpec((t, LANES), lambda p, i, qt, kt: (kt[i], vb + p)),
            pl.BlockSpec((1, t, LANES), lambda p, i, qt, kt: (p, kt[i], 0)),
        ],
        out_specs=[pl.BlockSpec((t, LANES), lambda p, i, qt, kt: (qt[i], p))] * 3,
        scratch_shapes=[pltpu.VMEM((2, 1, t), F32), pltpu.VMEM((2, 1, t), F32), pltpu.VMEM((LANES, t), F32)])
    return pl.pallas_call(
        body, name=name, grid_spec=grid_spec,
        out_shape=[jax.ShapeDtypeStruct((s, D_ATT), BF16), jax.ShapeDtypeStruct((s, D_ATT), F32),
                   jax.ShapeDtypeStruct((s, D_ATT), F32)],
        compiler_params=_params(("parallel", "arbitrary")),
    )(qi_tab, ki_tab, h, h, h, ck)


def _attn_bwd(h, ck, o, lse, do, name):
    s = h.shape[0]
    t = _tile(s, 512)
    n = s // t
    qb, kb, vb = OFF_Q // LANES, OFF_K // LANES, OFF_V // LANES

    pairs = [(ki, qi) for ki in range(n) for qi in range(ki, n)]
    ki_tab = jnp.asarray([ki for ki, _ in pairs], jnp.int32)
    qi_tab = jnp.asarray([qi for _, qi in pairs], jnp.int32)

    def body(ki_ref, qi_ref, q_ref, k_ref, v_ref, ck_ref, o_ref, lse_ref, do_ref,
             dq_ref, dk_ref, dv_ref, dc0_ref, dc1_ref, dk_acc, dv_acc, dc_acc):
        ki, qi = ki_ref[pl.program_id(1)], qi_ref[pl.program_id(1)]
        masks = _head_masks()
        lane = lax.broadcasted_iota(jnp.int32, (1, LANES), 1)

        @pl.when((ki == 0) & (qi == 0))
        def _():
            dq_ref[...] = jnp.zeros_like(dq_ref)

        @pl.when(qi == ki)
        def _():
            dk_acc[...] = jnp.zeros_like(dk_acc)
            dv_acc[...] = jnp.zeros_like(dv_acc)
            dc_acc[...] = jnp.zeros_like(dc_acc)

        def step(diag):
            q = q_ref[...] * _SCALE
            k = k_ref[...]
            v = v_ref[...]
            dov = do_ref[...]
            k_aug = jnp.concatenate([k, ck_ref[0]], axis=1)
            prod_t = (dov.astype(F32) * o_ref[...]).T
            lse_t = lse_ref[...].T
            heads = []
            for hh in range(2):
                mk = masks[hh]
                qh = jnp.where(mk, q, jnp.zeros_like(q))
                kh = jnp.where(mk, k, jnp.zeros_like(k))
                doh = jnp.where(mk, dov, jnp.zeros_like(dov))
                ones = jnp.where((lane >= 3 * hh) & (lane < 3 * hh + 3), 1.0, 0.0).astype(q.dtype)
                q_aug = jnp.concatenate([qh, jnp.broadcast_to(ones, q.shape)], axis=1)
                sc = lax.dot_general(k_aug, q_aug, _DIMS["nt"], preferred_element_type=F32)
                dp = lax.dot_general(v, doh, _DIMS["nt"], preferred_element_type=F32)
                heads.append((qh, kh, doh, sc, dp))
            grads = []
            for hh, (qh, kh, doh, sc, dp) in enumerate(heads):
                rows = slice(hh * HEAD_DIM, (hh + 1) * HEAD_DIM)
                p = jnp.exp(sc - lse_t[hh * HEAD_DIM:hh * HEAD_DIM + 1, :])
                if diag:
                    r = lax.broadcasted_iota(jnp.int32, (t, t), 0)
                    cc = lax.broadcasted_iota(jnp.int32, (t, t), 1)
                    p = jnp.where(r <= cc, p, 0.0)
                delta = jnp.sum(prod_t[rows], axis=0, keepdims=True)
                ds = p * (dp - delta)
                dc_acc[hh] = dc_acc[hh] - jnp.sum(ds, axis=1, keepdims=True)
                grads.append((ds.astype(MXU_DTYPE), p.astype(MXU_DTYPE)))
            dq_blk = jnp.zeros((t, LANES), F32)
            for (qh, kh, doh, _, _), (dsb, pb) in zip(heads, grads):
                dv_acc[...] += lax.dot_general(pb, doh, _DIMS["nn"], preferred_element_type=F32)
                dk_acc[...] += lax.dot_general(dsb, qh, _DIMS["nn"], preferred_element_type=F32)
                dq_blk = dq_blk + lax.dot_general(dsb, kh, _DIMS["tn"], preferred_element_type=F32)
            rows_q = pl.ds(pl.multiple_of(qi * t, t), t)
            dq_ref[rows_q, :] = dq_ref[rows_q, :] + dq_blk * _SCALE

        @pl.when(qi > ki)
        def _():
            step(False)

        @pl.when(qi == ki)
        def _():
            step(True)

        @pl.when(qi == n - 1)
        def _():
            dk_ref[...] = dk_acc[...].astype(dk_ref.dtype)
            dv_ref[...] = dv_acc[...].astype(dv_ref.dtype)
            dc0_ref[0] = jnp.broadcast_to(dc_acc[0], (t, LANES)).T[0:8]
            dc1_ref[0] = jnp.broadcast_to(dc_acc[1], (t, LANES)).T[0:8]

    q_blk = lambda col: pl.BlockSpec((t, LANES), lambda p, i, kt, qt: (qt[i], col(p)))
    k_blk = lambda col: pl.BlockSpec((t, LANES), lambda p, i, kt, qt: (kt[i], col(p)))
    dc_blk = pl.BlockSpec((1, 8, t), lambda p, i, kt, qt: (p, 0, kt[i]))
    grid_spec = pltpu.PrefetchScalarGridSpec(
        num_scalar_prefetch=2, grid=(N_HEADS // 2, len(pairs)),
        in_specs=[q_blk(lambda p: qb + p), k_blk(lambda p: kb + p), k_blk(lambda p: vb + p),
                  pl.BlockSpec((1, t, LANES), lambda p, i, kt, qt: (p, kt[i], 0)),
                  q_blk(lambda p: p), q_blk(lambda p: p), q_blk(lambda p: p)],
        out_specs=[pl.BlockSpec((s, LANES), lambda p, i, kt, qt: (0, p)), k_blk(lambda p: p), k_blk(lambda p: p),
                   dc_blk, dc_blk],
        scratch_shapes=[pltpu.VMEM((t, LANES), F32), pltpu.VMEM((t, LANES), F32), pltpu.VMEM((2, t, 1), F32)])
    return pl.pallas_call(
        body, name=name, grid_spec=grid_spec,
        out_shape=[jax.ShapeDtypeStruct((s, D_ATT), F32), jax.ShapeDtypeStruct((s, D_ATT), BF16),
                   jax.ShapeDtypeStruct((s, D_ATT), BF16), jax.ShapeDtypeStruct((N_HEADS // 2, 8, s), F32),
                   jax.ShapeDtypeStruct((N_HEADS // 2, 8, s), F32)],
        compiler_params=_params(("parallel", "arbitrary")),
    )(ki_tab, qi_tab, h, h, h, ck, o, lse, do)


def _conv3(z, z_prev, w_ref):
    return (w_ref[2:3, :] * z + w_ref[1:2, :] * _shift_down(z, z_prev, 1)
            + w_ref[0:1, :] * _shift_down(z, z_prev, 2))


def _sconv_fwd(h, w, name):
    s = h.shape[0]
    t = _tile(s, 512)
    r = t // HALO
    c = D_CONV
    b_bg, b_cg, b_hc = OFF_BG // c, OFF_CG // c, OFF_HC // c

    def body(bg_ref, cg_ref, hc_ref, cgp_ref, hcp_ref, w_ref, y_ref):
        i = pl.program_id(0)
        live = (i > 0).astype(F32)
        z = cg_ref[...].astype(F32) * hc_ref[...].astype(F32)
        zp = cgp_ref[...].astype(F32) * hcp_ref[...].astype(F32) * live
        y_ref[...] = (bg_ref[...].astype(F32) * _conv3(z, zp, w_ref)).astype(y_ref.dtype)

    cur = lambda b: pl.BlockSpec((t, c), lambda i: (i, b))
    prev = lambda b: pl.BlockSpec((HALO, c), lambda i: (jnp.maximum(i * r - 1, 0), b))
    return pl.pallas_call(
        body, name=name, grid=(s // t,),
        in_specs=[cur(b_bg), cur(b_cg), cur(b_hc), prev(b_cg), prev(b_hc), pl.BlockSpec((8, c), lambda i: (0, 0))],
        out_specs=pl.BlockSpec((t, c), lambda i: (i, 0)),
        out_shape=jax.ShapeDtypeStruct((s, c), BF16),
        compiler_params=_params(("parallel",)),
    )(h, h, h, h, h, w)


def _sconv_bwd(h, w, dy, name):
    s = h.shape[0]
    t = _tile(s, 512)
    n = s // t
    r = t // HALO
    nh = s // HALO
    c = D_CONV
    b_bg, b_cg, b_hc = OFF_BG // c, OFF_CG // c, OFF_HC // c

    def body(bg_ref, cg_ref, hc_ref, cgp_ref, hcp_ref, bgn_ref, dy_ref, dyn_ref, w_ref, d_ref, dw_ref, acc_ref):
        i = pl.program_id(0)
        has_prev = (i > 0).astype(F32)
        has_next = (i < n - 1).astype(F32)
        bg = bg_ref[...].astype(F32)
        cg = cg_ref[...].astype(F32)
        hc = hc_ref[...].astype(F32)
        dyv = dy_ref[...].astype(F32)
        z = cg * hc
        zp = cgp_ref[...].astype(F32) * hcp_ref[...].astype(F32) * has_prev
        z1 = _shift_down(z, zp, 1)
        z2 = _shift_down(z, zp, 2)
        cz = w_ref[2:3, :] * z + w_ref[1:2, :] * z1 + w_ref[0:1, :] * z2
        dcz = dyv * bg
        dczn = dyn_ref[...].astype(F32) * bgn_ref[...].astype(F32) * has_next
        dz = (w_ref[2:3, :] * dcz + w_ref[1:2, :] * _shift_up(dcz, dczn, 1)
              + w_ref[0:1, :] * _shift_up(dcz, dczn, 2))
        d_ref[:, 0:c] = (dyv * cz).astype(d_ref.dtype)
        d_ref[:, c:2 * c] = (dz * hc).astype(d_ref.dtype)
        d_ref[:, 2 * c:3 * c] = (dz * cg).astype(d_ref.dtype)

        @pl.when(i == 0)
        def _():
            acc_ref[...] = jnp.zeros_like(acc_ref)

        acc_ref[0] += _row_sum8(dcz * z2)
        acc_ref[1] += _row_sum8(dcz * z1)
        acc_ref[2] += _row_sum8(dcz * z)

        @pl.when(i == n - 1)
        def _():
            rows = [jnp.sum(acc_ref[k], axis=0, keepdims=True) for k in range(3)]
            dw_ref[...] = jnp.concatenate(rows + [jnp.zeros((5, c), F32)], axis=0)

    cur = lambda b: pl.BlockSpec((t, c), lambda i: (i, b))
    prev = lambda b: pl.BlockSpec((HALO, c), lambda i: (jnp.maximum(i * r - 1, 0), b))
    nxt = lambda b: pl.BlockSpec((HALO, c), lambda i: (jnp.minimum((i + 1) * r, nh - 1), b))
    return pl.pallas_call(
        body, name=name, grid=(n,),
        in_specs=[cur(b_bg), cur(b_cg), cur(b_hc), prev(b_cg), prev(b_hc), nxt(b_bg),
                  cur(0), nxt(0), pl.BlockSpec((8, c), lambda i: (0, 0))],
        out_specs=[pl.BlockSpec((t, 3 * c), lambda i: (i, 0)), pl.BlockSpec((8, c), lambda i: (0, 0))],
        out_shape=[jax.ShapeDtypeStruct((s, 3 * c), BF16), jax.ShapeDtypeStruct((8, c), F32)],
        scratch_shapes=[pltpu.VMEM((3, 8, c), F32)],
        compiler_params=_params(("arbitrary",)),
    )(h, h, h, h, h, h, dy, dy, w)


def _group_masks():
    lane = lax.broadcasted_iota(jnp.int32, (1, D_SGU), 1)
    return [(lane >= g * HEAD_DIM) & (lane < (g + 1) * HEAD_DIM) for g in range(N_GROUPS)]


def _tril_weights(w_ref):
    r = lax.broadcasted_iota(jnp.int32, (CHUNK, CHUNK), 0)
    c = lax.broadcasted_iota(jnp.int32, (CHUNK, CHUNK), 1)
    return [jnp.where(r >= c, w_ref[g], 0.0).astype(MXU_DTYPE) for g in range(N_GROUPS)]


def _sgu_ln(vs, g_ref, b_ref):
    vg, dvg = _gelu_and_grad(vs)
    mu = jnp.mean(vg, axis=-1, keepdims=True)
    xc = vg - mu
    rstd = lax.rsqrt(jnp.mean(xc * xc, axis=-1, keepdims=True) + LN_EPS)
    xhat = xc * rstd
    return xhat * g_ref[...] + b_ref[...], xhat, rstd, dvg


def _sgu_fwd(h, ln_g, ln_b, w_s, bias, name):
    s = h.shape[0]
    t = _tile(s, 512)
    c = D_SGU
    b_u, b_v = OFF_U // c, OFF_VS // c

    def body(u_ref, v_ref, g_ref, b_ref, w_ref, bias_ref, y_ref):
        gm = _group_masks()
        wm = _tril_weights(w_ref)
        ug = _gelu(u_ref[...].astype(F32))
        vn, _, _, _ = _sgu_ln(v_ref[...].astype(F32), g_ref, b_ref)
        vnb = vn.astype(MXU_DTYPE)
        for ch in range(t // CHUNK):
            rows = slice(ch * CHUNK, (ch + 1) * CHUNK)
            mixed = bias_ref[...]
            for g in range(N_GROUPS):
                mg = lax.dot_general(wm[g], vnb[rows], _DIMS["nn"], preferred_element_type=F32)
                mixed = jnp.where(gm[g], mixed + mg, mixed)
            y_ref[rows, :] = (ug[rows] * mixed).astype(y_ref.dtype)

    full = lambda shp: pl.BlockSpec(shp, lambda i: (0,) * len(shp))
    return pl.pallas_call(
        body, name=name, grid=(s // t,),
        in_specs=[pl.BlockSpec((t, c), lambda i: (i, b_u)), pl.BlockSpec((t, c), lambda i: (i, b_v)),
                  full((1, c)), full((1, c)), full((N_GROUPS, CHUNK, CHUNK)), full((CHUNK, c))],
        out_specs=pl.BlockSpec((t, c), lambda i: (i, 0)),
        out_shape=jax.ShapeDtypeStruct((s, c), BF16),
        compiler_params=_params(("parallel",)),
    )(h, h, ln_g, ln_b, w_s, bias)


def _sgu_bwd(h, ln_g, ln_b, w_s, bias, dy, name):
    s = h.shape[0]
    t = _tile(s, 512)
    n = s // t
    c = D_SGU
    b_u, b_v = OFF_U // c, OFF_VS // c

    def body(u_ref, v_ref, g_ref, b_ref, w_ref, bias_ref, dy_ref,
             d_ref, dg_ref, db_ref, dw_ref, dbias_ref, dg_acc, db_acc):
        i = pl.program_id(0)
        gm = _group_masks()
        wm = _tril_weights(w_ref)

        @pl.when(i == 0)
        def _():
            dg_acc[...] = jnp.zeros_like(dg_acc)
            db_acc[...] = jnp.zeros_like(db_acc)
            dw_ref[...] = jnp.zeros_like(dw_ref)
            dbias_ref[...] = jnp.zeros_like(dbias_ref)

        ug, dug = _gelu_and_grad(u_ref[...].astype(F32))
        vn, xhat, rstd, dvg = _sgu_ln(v_ref[...].astype(F32), g_ref, b_ref)
        vnb = vn.astype(MXU_DTYPE)
        dyv = dy_ref[...].astype(F32)
        dmixed = dyv * ug
        dmb = dmixed.astype(MXU_DTYPE)
        dvn_parts = []
        for ch in range(t // CHUNK):
            rows = slice(ch * CHUNK, (ch + 1) * CHUNK)
            mixed = bias_ref[...]
            dvn = jnp.zeros((CHUNK, c), F32)
            for g in range(N_GROUPS):
                mg = lax.dot_general(wm[g], vnb[rows], _DIMS["nn"], preferred_element_type=F32)
                mixed = jnp.where(gm[g], mixed + mg, mixed)
                dvn = jnp.where(gm[g], lax.dot_general(wm[g], dmb[rows], _DIMS["tn"], preferred_element_type=F32),
                                dvn)
                dmg = jnp.where(gm[g], dmb[rows], jnp.zeros_like(dmb[rows]))
                dw_ref[g] += lax.dot_general(dmg, vnb[rows], _DIMS["nt"], preferred_element_type=F32)
            d_ref[rows, 0:c] = (dyv[rows] * mixed * dug[rows]).astype(d_ref.dtype)
            dbias_ref[...] += dmixed[rows]
            dvn_parts.append(dvn)
        dvn = jnp.concatenate(dvn_parts, axis=0)
        dg_acc[...] += _row_sum8(dvn * xhat)
        db_acc[...] += _row_sum8(dvn)
        dxh = dvn * g_ref[...]
        dvgl = rstd * (dxh - jnp.mean(dxh, axis=-1, keepdims=True)
                       - xhat * jnp.mean(dxh * xhat, axis=-1, keepdims=True))
        d_ref[:, c:2 * c] = (dvgl * dvg).astype(d_ref.dtype)

        @pl.when(i == n - 1)
        def _():
            dg_ref[...] = jnp.sum(dg_acc[...], axis=0, keepdims=True)
            db_ref[...] = jnp.sum(db_acc[...], axis=0, keepdims=True)
            r = lax.broadcasted_iota(jnp.int32, (CHUNK, CHUNK), 0)
            cc = lax.broadcasted_iota(jnp.int32, (CHUNK, CHUNK), 1)
            for g in range(N_GROUPS):
                dw_ref[g] = jnp.where(r >= cc, dw_ref[g], 0.0)

    full = lambda shp: pl.BlockSpec(shp, lambda i: (0,) * len(shp))
    return pl.pallas_call(
        body, name=name, grid=(n,),
        in_specs=[pl.BlockSpec((t, c), lambda i: (i, b_u)), pl.BlockSpec((t, c), lambda i: (i, b_v)),
                  full((1, c)), full((1, c)), full((N_GROUPS, CHUNK, CHUNK)), full((CHUNK, c)),
                  pl.BlockSpec((t, c), lambda i: (i, 0))],
        out_specs=[pl.BlockSpec((t, 2 * c), lambda i: (i, 0)), full((1, c)), full((1, c)),
                   full((N_GROUPS, CHUNK, CHUNK)), full((CHUNK, c))],
        out_shape=[jax.ShapeDtypeStruct((s, 2 * c), BF16), jax.ShapeDtypeStruct((1, c), F32),
                   jax.ShapeDtypeStruct((1, c), F32), jax.ShapeDtypeStruct((N_GROUPS, CHUNK, CHUNK), F32),
                   jax.ShapeDtypeStruct((CHUNK, c), F32)],
        scratch_shapes=[pltpu.VMEM((8, c), F32), pltpu.VMEM((8, c), F32)],
        compiler_params=_params(("arbitrary",)),
    )(h, h, ln_g, ln_b, w_s, bias, dy)


def _merge_fwd(h, acts, ws, b_gate, name):
    s = h.shape[0]
    d = D_MODEL
    t = _tile(s, 512)

    def body(gl0, gl1, gl2, a0, a1, a2, w0, w1, w2, b_ref, o_ref):
        acc = jnp.zeros((t, d), F32)
        for i, (gl, a, w) in enumerate(((gl0, a0, w0), (gl1, a1, w1), (gl2, a2, w2))):
            y = lax.dot_general(a[...], w[...], _DIMS["nn"], preferred_element_type=F32)
            acc = acc + _sigmoid(gl[...].astype(F32) + b_ref[i:i + 1, :]) * y
        o_ref[...] = acc.astype(o_ref.dtype)

    full = lambda arr: pl.BlockSpec(arr.shape, lambda i: (0, 0))
    return pl.pallas_call(
        body, name=name, grid=(s // t,),
        in_specs=[pl.BlockSpec((t, d), lambda i, b=b: (i, b)) for b in range(3)]
                 + [pl.BlockSpec((t, a.shape[1]), lambda i: (i, 0)) for a in acts]
                 + [full(w) for w in ws] + [full(b_gate)],
        out_specs=pl.BlockSpec((t, d), lambda i: (i, 0)),
        out_shape=jax.ShapeDtypeStruct((s, d), BF16),
        compiler_params=_params(("parallel",)),
    )(h, h, h, *acts, *ws, b_gate)


def _merge_bwd(h, acts, ws, b_gate, dmerged, name):
    s = h.shape[0]
    d = D_MODEL
    t = _tile(s, 512)
    n = s // t

    def body(gl0, gl1, gl2, a0, a1, a2, w0, w1, w2, b_ref, dm_ref, dy0, dy1, dy2, dgl_ref, db_ref, acc_ref):
        step = pl.program_id(0)

        @pl.when(step == 0)
        def _():
            acc_ref[...] = jnp.zeros_like(acc_ref)

        dm = dm_ref[...]
        for i, (gl, a, w, dy) in enumerate(((gl0, a0, w0, dy0), (gl1, a1, w1, dy1), (gl2, a2, w2, dy2))):
            y = lax.dot_general(a[...], w[...], _DIMS["nn"], preferred_element_type=F32)
            gate = _sigmoid(gl[...].astype(F32) + b_ref[i:i + 1, :])
            dy[...] = (dm * gate).astype(dy.dtype)
            dgl = dm * y * (gate * (1.0 - gate))
            dgl_ref[:, i * d:(i + 1) * d] = dgl.astype(dgl_ref.dtype)
            acc_ref[i] += _row_sum8(dgl)

        @pl.when(step == n - 1)
        def _():
            rows = [jnp.sum(acc_ref[k], axis=0, keepdims=True) for k in range(3)]
            db_ref[...] = jnp.concatenate(rows + [jnp.zeros((5, d), F32)], axis=0)

    full = lambda arr: pl.BlockSpec(arr.shape, lambda i: (0, 0))
    row = pl.BlockSpec((t, d), lambda i: (i, 0))
    return pl.pallas_call(
        body, name=name, grid=(n,),
        in_specs=[pl.BlockSpec((t, d), lambda i, b=b: (i, b)) for b in range(3)]
                 + [pl.BlockSpec((t, a.shape[1]), lambda i: (i, 0)) for a in acts]
                 + [full(w) for w in ws] + [full(b_gate), row],
        out_specs=[row, row, row, pl.BlockSpec((t, 3 * d), lambda i: (i, 0)), pl.BlockSpec((8, d), lambda i: (0, 0))],
        out_shape=[jax.ShapeDtypeStruct((s, d), BF16)] * 3
                  + [jax.ShapeDtypeStruct((s, IN_PAD), BF16), jax.ShapeDtypeStruct((8, d), F32)],
        scratch_shapes=[pltpu.VMEM((3, 8, d), F32)],
        compiler_params=_params(("arbitrary",)),
    )(h, h, h, *acts, *ws, b_gate, dmerged)


FF_BLK = D_FF // 2


def _ffn_act_fwd(h2, w, name):
    s = h2.shape[0]
    t = _tile(s, 512)
    r = t // HALO
    cw = 2 * FF_BLK

    def body(x_ref, xp_ref, w_ref, p_ref):
        i = pl.program_id(0)
        live = (i > 0).astype(F32)
        hc = _conv3(x_ref[...].astype(F32), xp_ref[...].astype(F32) * live, w_ref)
        p_ref[...] = (_gelu(hc[:, :FF_BLK]) * hc[:, FF_BLK:]).astype(p_ref.dtype)

    return pl.pallas_call(
        body, name=name, grid=(s // t, 2),
        in_specs=[pl.BlockSpec((t, cw), lambda i, j: (i, j)),
                  pl.BlockSpec((HALO, cw), lambda i, j: (jnp.maximum(i * r - 1, 0), j)),
                  pl.BlockSpec((8, cw), lambda i, j: (0, j))],
        out_specs=pl.BlockSpec((t, FF_BLK), lambda i, j: (i, j)),
        out_shape=jax.ShapeDtypeStruct((s, D_FF), BF16),
        compiler_params=_params(("parallel", "parallel")),
    )(h2, h2, w)


def _ffn_act_conv_bwd(h2, w, dp, name):
    s = h2.shape[0]
    t = _tile(s, 512)
    n = s // t
    r = t // HALO
    nh = s // HALO
    cw = 2 * FF_BLK

    def body(x_ref, xp_ref, xn_ref, dp_ref, dpn_ref, w_ref, dx_ref, dw_ref, acc_ref):
        i = pl.program_id(1)
        has_prev = (i > 0).astype(F32)
        has_next = (i < n - 1).astype(F32)
        x = jnp.concatenate([x_ref[...].astype(F32), xn_ref[...].astype(F32)], axis=0)
        xp = xp_ref[...].astype(F32) * has_prev
        x1 = _shift_down(x, xp, 1)
        x2 = _shift_down(x, xp, 2)
        hc = w_ref[2:3, :] * x + w_ref[1:2, :] * x1 + w_ref[0:1, :] * x2
        ga, dga = _gelu_and_grad(hc[:, :FF_BLK])
        dpv = jnp.concatenate([dp_ref[...].astype(F32), dpn_ref[...].astype(F32) * has_next], axis=0)
        dhc = jnp.concatenate([dpv * hc[:, FF_BLK:] * dga, dpv * ga], axis=1)
        cur, nxt = dhc[:t], dhc[t:]
        dx = w_ref[2:3, :] * cur + w_ref[1:2, :] * _shift_up(cur, nxt, 1) + w_ref[0:1, :] * _shift_up(cur, nxt, 2)
        dx_ref[...] = dx.astype(dx_ref.dtype)

        @pl.when(i == 0)
        def _():
            acc_ref[...] = jnp.zeros_like(acc_ref)

        acc_ref[0] += _row_sum8(cur * x2[:t])
        acc_ref[1] += _row_sum8(cur * x1[:t])
        acc_ref[2] += _row_sum8(cur * x[:t])

        @pl.when(i == n - 1)
        def _():
            rows = [jnp.sum(acc_ref[k], axis=0, keepdims=True) for k in range(3)]
            dw_ref[...] = jnp.concatenate(rows + [jnp.zeros((5, cw), F32)], axis=0)

    nxt_row = lambda j, i: jnp.minimum((i + 1) * r, nh - 1)
    return pl.pallas_call(
        body, name=name, grid=(2, n),
        in_specs=[pl.BlockSpec((t, cw), lambda j, i: (i, j)),
                  pl.BlockSpec((HALO, cw), lambda j, i: (jnp.maximum(i * r - 1, 0), j)),
                  pl.BlockSpec((HALO, cw), lambda j, i: (nxt_row(j, i), j)),
                  pl.BlockSpec((t, FF_BLK), lambda j, i: (i, j)),
                  pl.BlockSpec((HALO, FF_BLK), lambda j, i: (nxt_row(j, i), j)),
                  pl.BlockSpec((8, cw), lambda j, i: (0, j))],
        out_specs=[pl.BlockSpec((t, cw), lambda j, i: (i, j)), pl.BlockSpec((8, cw), lambda j, i: (0, j))],
        out_shape=[jax.ShapeDtypeStruct((s, 2 * D_FF), BF16), jax.ShapeDtypeStruct((8, 2 * D_FF), F32)],
        scratch_shapes=[pltpu.VMEM((3, 8, cw), F32)],
        compiler_params=_params(("parallel", "arbitrary")),
    )(h2, h2, h2, dp, dp, w)


def _adamw(w, g, m, v, name):
    shape = w.shape
    c = shape[-1]
    rows = math.prod(shape[:-1])
    to2d = lambda a: a.reshape(rows, c)
    cap = max(8, (1 << 18) // c)
    tr = rows
    for cand in (2048, 1024, 512, 256, 128, 64, 32, 16, 8):
        if cand <= cap and rows % cand == 0:
            tr = cand
            break

    def body(w_ref, g_ref, m_ref, v_ref, d_ref, nm_ref, nv_ref):
        gv = g_ref[...]
        nm = ADAM_B1 * m_ref[...] + (1.0 - ADAM_B1) * gv
        nv = ADAM_B2 * v_ref[...] + (1.0 - ADAM_B2) * (gv * gv)
        m_hat = nm / (1.0 - ADAM_B1 ** ADAM_STEP)
        v_hat = nv / (1.0 - ADAM_B2 ** ADAM_STEP)
        d_ref[...] = -ADAM_LR * (m_hat / (jnp.sqrt(v_hat) + ADAM_EPS) + ADAM_WD * w_ref[...])
        nm_ref[...] = nm
        nv_ref[...] = nv

    blk = pl.BlockSpec((tr, c), lambda i: (i, 0))
    outs = pl.pallas_call(
        body, name=name, grid=(rows // tr,),
        in_specs=[blk] * 4, out_specs=[blk] * 3,
        out_shape=[jax.ShapeDtypeStruct((rows, c), F32)] * 3,
        compiler_params=_params(("parallel",)),
    )(to2d(w), to2d(g), to2d(m), to2d(v))
    return tuple(o.reshape(shape) for o in outs)


_ANY = pl.BlockSpec(memory_space=pl.ANY)


def _place():
    x, y, c = lax.axis_index("x"), lax.axis_index("y"), lax.axis_index("c")
    others = [(1 - x, y), (x, 1 - y), (1 - x, 1 - y)]
    return x, y, c, others


def _all_gather_chips(shard, name):
    rws, cols = shard.shape
    half = rws // 2

    def body(x_ref, out_ref, send_sems, recv_sems, local_sem):
        x, y, c, others = _place()
        me = 2 * x + y
        sib = (x, y, 1 - c)

        def rows(chip, cc):
            return out_ref.at[chip, pl.ds(pl.multiple_of(cc * half, 16), half), :]

        def copy(k, src, dst, to):
            return pltpu.make_async_remote_copy(src_ref=src, dst_ref=dst, send_sem=send_sems.at[k],
                                                recv_sem=recv_sems.at[k], device_id=to, device_id_type=MESH)

        mine = pltpu.make_async_copy(x_ref, out_ref.at[me], local_sem)
        mine.start()
        my_half = x_ref.at[pl.ds(pl.multiple_of(c * half, 16), half), :]
        first = [copy(j, my_half, rows(me, c), (ox, oy, c)) for j, (ox, oy) in enumerate(others)]
        for cp in first:
            cp.start()
        passed = []
        for j, (ox, oy) in enumerate(others):
            blk = rows(2 * ox + oy, c)
            copy(j, blk, blk, (x, y, c)).wait_recv()
            fwd = copy(3 + j, blk, blk, sib)
            fwd.start()
            passed.append(fwd)
        for j, (ox, oy) in enumerate(others):
            blk = rows(2 * ox + oy, 1 - c)
            copy(3 + j, blk, blk, (x, y, c)).wait_recv()
        for cp in first + passed:
            cp.wait_send()
        mine.wait()

    return pl.pallas_call(
        body, name=name,
        in_specs=[_ANY], out_specs=_ANY,
        out_shape=jax.ShapeDtypeStruct((N_CHIPS, rws, cols), shard.dtype),
        scratch_shapes=[pltpu.SemaphoreType.DMA((6,)), pltpu.SemaphoreType.DMA((6,)), pltpu.SemaphoreType.DMA],
        compiler_params=pltpu.CompilerParams(has_side_effects=True),
    )(shard)


def _swap_halves(buf, name):
    nb, rws, cols = buf.shape
    half = rws // 2

    def body(b_ref, own_ref, sib_ref, send_sem, recv_sem, local_sem):
        x, y, c, _ = _place()
        keep = b_ref.at[:, pl.ds(pl.multiple_of(c * half, 16), half), :]
        give = b_ref.at[:, pl.ds(pl.multiple_of((1 - c) * half, 16), half), :]
        mine = pltpu.make_async_copy(keep, own_ref, local_sem)
        mine.start()
        cp = pltpu.make_async_remote_copy(src_ref=give, dst_ref=sib_ref, send_sem=send_sem, recv_sem=recv_sem,
                                          device_id=(x, y, 1 - c), device_id_type=MESH)
        cp.start()
        cp.wait()
        mine.wait()

    shp = jax.ShapeDtypeStruct((nb, half, cols), buf.dtype)
    return pl.pallas_call(
        body, name=name,
        in_specs=[_ANY], out_specs=[_ANY, _ANY], out_shape=[shp, shp],
        scratch_shapes=[pltpu.SemaphoreType.DMA, pltpu.SemaphoreType.DMA, pltpu.SemaphoreType.DMA],
        compiler_params=pltpu.CompilerParams(has_side_effects=True),
    )(buf)


def _add2(a, b, name):
    nb, rws, cols = a.shape
    t = _tile(rws, 256)
    if rws % t:
        t = rws

    def body(a_ref, b_ref, o_ref):
        o_ref[...] = (a_ref[...].astype(F32) + b_ref[...].astype(F32)).astype(o_ref.dtype)

    blk = pl.BlockSpec((1, t, cols), lambda i, j: (i, j, 0))
    return pl.pallas_call(
        body, name=name, grid=(nb, rws // t), in_specs=[blk, blk], out_specs=blk,
        out_shape=jax.ShapeDtypeStruct(a.shape, a.dtype),
        compiler_params=_params(("parallel", "parallel")),
    )(a, b)


def _exchange_chips(pre, name):
    nb, half, cols = pre.shape

    def body(p_ref, out_ref, send_sems, recv_sems, local_sem):
        x, y, c, others = _place()
        me = 2 * x + y
        mine = pltpu.make_async_copy(p_ref.at[me], out_ref.at[me], local_sem)
        mine.start()
        sends = []
        for j, (ox, oy) in enumerate(others):
            cp = pltpu.make_async_remote_copy(src_ref=p_ref.at[2 * ox + oy], dst_ref=out_ref.at[me],
                                              send_sem=send_sems.at[j], recv_sem=recv_sems.at[j],
                                              device_id=(ox, oy, c), device_id_type=MESH)
            cp.start()
            sends.append(cp)
        for j, (ox, oy) in enumerate(others):
            blk = out_ref.at[2 * ox + oy]
            pltpu.make_async_remote_copy(src_ref=blk, dst_ref=blk, send_sem=send_sems.at[j],
                                         recv_sem=recv_sems.at[j], device_id=(x, y, c),
                                         device_id_type=MESH).wait_recv()
        for cp in sends:
            cp.wait_send()
        mine.wait()

    return pl.pallas_call(
        body, name=name,
        in_specs=[_ANY], out_specs=_ANY, out_shape=jax.ShapeDtypeStruct(pre.shape, pre.dtype),
        scratch_shapes=[pltpu.SemaphoreType.DMA((3,)), pltpu.SemaphoreType.DMA((3,)), pltpu.SemaphoreType.DMA],
        compiler_params=pltpu.CompilerParams(has_side_effects=True),
    )(pre)


def _add4(parts, name):
    nb, half, cols = parts.shape
    t = _tile(half, 256)
    if half % t:
        t = half

    def body(p_ref, o_ref):
        acc = p_ref[0].astype(F32)
        for k in range(1, nb):
            acc = acc + p_ref[k].astype(F32)
        o_ref[...] = acc

    return pl.pallas_call(
        body, name=name, grid=(half // t,),
        in_specs=[pl.BlockSpec((nb, t, cols), lambda i: (0, i, 0))],
        out_specs=pl.BlockSpec((t, cols), lambda i: (i, 0)),
        out_shape=jax.ShapeDtypeStruct((half, cols), F32),
        compiler_params=_params(("parallel",)),
    )(parts)


def _join_halves(mine_half, name):
    half, cols = mine_half.shape

    def body(h_ref, out_ref, send_sem, recv_sem, local_sem):
        x, y, c, _ = _place()
        dst = out_ref.at[pl.ds(pl.multiple_of(c * half, 8), half), :]
        mine = pltpu.make_async_copy(h_ref, dst, local_sem)
        mine.start()
        cp = pltpu.make_async_remote_copy(src_ref=h_ref, dst_ref=dst, send_sem=send_sem, recv_sem=recv_sem,
                                          device_id=(x, y, 1 - c), device_id_type=MESH)
        cp.start()
        cp.wait()
        mine.wait()

    return pl.pallas_call(
        body, name=name,
        in_specs=[_ANY], out_specs=_ANY, out_shape=jax.ShapeDtypeStruct((2 * half, cols), mine_half.dtype),
        scratch_shapes=[pltpu.SemaphoreType.DMA, pltpu.SemaphoreType.DMA, pltpu.SemaphoreType.DMA],
        compiler_params=pltpu.CompilerParams(has_side_effects=True),
    )(mine_half)


def _reduce_scatter_chips(buf, tag):
    own, sib = _swap_halves(buf, "rs_swap_" + tag)
    pre = _add2(own, sib, "rs_add2_" + tag)
    parts = _exchange_chips(pre, "rs_xchg_" + tag)
    red = _add4(parts, "rs_add4_" + tag)
    return _join_halves(red, "rs_join_" + tag)


MAX_DMA_BYTES = 2 * 1024 * 1024
ROW_ALIGN = 16


def _pieces(rows, row_bytes):
    n = max(1, -(-(rows * row_bytes) // MAX_DMA_BYTES))
    step = -(-(-(-rows // n)) // ROW_ALIGN) * ROW_ALIGN
    return [(r, min(step, rows - r)) for r in range(0, rows, step)]


def _half_plan(arrays, row_axis):
    plan = []
    for a, arr in enumerate(arrays):
        row_bytes = math.prod(arr.shape[row_axis + 1:]) * arr.dtype.itemsize * (arr.shape[0] if row_axis else 1)
        plan += [(a, r0, nr) for r0, nr in _pieces(arr.shape[row_axis] // 2, row_bytes)]
    return plan


def _rows(start, size):
    return pl.ds(pl.multiple_of(start, ROW_ALIGN), size)


def _remote(src, dst, send_sems, recv_sems, k, to):
    return pltpu.make_async_remote_copy(src_ref=src, dst_ref=dst, send_sem=send_sems.at[k], recv_sem=recv_sems.at[k],
                                        device_id=to, device_id_type=MESH)


def _comm_call(body, name, ins, out_shapes, n_remote, n_local, aliases=None):
    return pl.pallas_call(
        body, name=name,
        in_specs=[_ANY] * len(ins), out_specs=[_ANY] * len(out_shapes), out_shape=out_shapes,
        scratch_shapes=[pltpu.SemaphoreType.DMA((n_remote,)), pltpu.SemaphoreType.DMA((n_remote,)),
                        pltpu.SemaphoreType.DMA((max(n_local, 1),))],
        input_output_aliases=aliases or {},
        compiler_params=pltpu.CompilerParams(has_side_effects=True),
    )(*ins)


def _cast_shard(w, l, me_idx, name):
    _, k, cols = w.shape
    tr = _tile(k, 256)
    if k % tr:
        tr = k

    def body(me_ref, w_ref, s_ref, land_ref):
        del me_ref
        v = w_ref[...].astype(BF16)
        s_ref[...] = v
        land_ref[...] = v

    grid_spec = pltpu.PrefetchScalarGridSpec(
        num_scalar_prefetch=1, grid=(k // tr,),
        in_specs=[pl.BlockSpec((None, tr, cols), lambda i, me: (l, i, 0))],
        out_specs=[pl.BlockSpec((tr, cols), lambda i, me: (i, 0)),
                   pl.BlockSpec((None, tr, cols), lambda i, me: (me[0], i, 0))])
    return pl.pallas_call(
        body, name=name, grid_spec=grid_spec,
        out_shape=[jax.ShapeDtypeStruct((k, cols), BF16), jax.ShapeDtypeStruct((N_CHIPS, k, cols), BF16)],
        compiler_params=_params(("parallel",)),
    )(me_idx, w)


def _gather_d2d(lands, name):
    n = len(lands)
    plan = _half_plan(lands, 1)
    plan = [(a, r0, nr) for a, r0, nr in plan]

    def body(*refs):
        out_refs = refs[n:2 * n]
        send_sems, recv_sems, _ = refs[2 * n:]
        x, y, c, others = _place()
        sends = []
        for i, (a, r0, nr) in enumerate(plan):
            rows = _rows(c * (lands[a].shape[1] // 2) + r0, nr)
            for j, (ox, oy) in enumerate(others):
                blk = out_refs[a].at[2 * ox + oy, rows, :]
                cp = _remote(blk, blk, send_sems, recv_sems, 3 * i + j, (x, y, 1 - c))
                cp.start()
                sends.append(cp)
        for i, (a, r0, nr) in enumerate(plan):
            rows = _rows((1 - c) * (lands[a].shape[1] // 2) + r0, nr)
            for j, (ox, oy) in enumerate(others):
                blk = out_refs[a].at[2 * ox + oy, rows, :]
                _remote(blk, blk, send_sems, recv_sems, 3 * i + j, (x, y, c)).wait_recv()
        for cp in sends:
            cp.wait_send()

    outs = [jax.ShapeDtypeStruct(a.shape, a.dtype) for a in lands]
    return _comm_call(body, name, lands, outs, 3 * len(plan), 0, aliases={a: a for a in range(n)})


def _rs_swap(ts, name):
    n = len(ts)
    plan = _half_plan(ts, 1)

    def body(*refs):
        t_refs, out_refs = refs[:n], refs[n:2 * n]
        send_sems, recv_sems, _ = refs[2 * n:]
        x, y, c, _o = _place()
        sends = []
        for i, (a, r0, nr) in enumerate(plan):
            src = t_refs[a].at[:, _rows((1 - c) * (ts[a].shape[1] // 2) + r0, nr), :]
            cp = _remote(src, out_refs[a].at[:, pl.ds(r0, nr), :], send_sems, recv_sems, i, (x, y, 1 - c))
            cp.start()
            sends.append(cp)
        for i, (a, r0, nr) in enumerate(plan):
            blk = out_refs[a].at[:, pl.ds(r0, nr), :]
            _remote(blk, blk, send_sems, recv_sems, i, (x, y, c)).wait_recv()
        for cp in sends:
            cp.wait_send()

    outs = [jax.ShapeDtypeStruct((t.shape[0], t.shape[1] // 2, t.shape[2]), t.dtype) for t in ts]
    return _comm_call(body, name, ts, outs, len(plan), 0)


def _add_halves(ts, gots, c_idx, me_idx, name):
    n = len(ts)

    def body(c_ref, me_ref, *refs):
        del c_ref
        t_refs, g_refs = refs[:n], refs[n:2 * n]
        o_refs, mine_refs = refs[2 * n:3 * n], refs[3 * n:]
        for t_ref, g_ref, o_ref, mine_ref in zip(t_refs, g_refs, o_refs, mine_refs):
            v = (t_ref[...].astype(F32) + g_ref[...].astype(F32)).astype(o_ref.dtype)
            o_ref[...] = v

            @pl.when(pl.program_id(0) == me_ref[0])
            def _():
                mine_ref[...] = v

    blks = [(1, g.shape[1], g.shape[2]) for g in gots]
    same = [pl.BlockSpec(b, lambda i, c, me: (i, 0, 0)) for b in blks]
    grid_spec = pltpu.PrefetchScalarGridSpec(
        num_scalar_prefetch=2, grid=(N_CHIPS,),
        in_specs=[pl.BlockSpec(b, lambda i, c, me: (i, c[0], 0)) for b in blks] + same,
        out_specs=same + [pl.BlockSpec(b, lambda i, c, me: (me[0], 0, 0)) for b in blks])
    shapes = [jax.ShapeDtypeStruct(g.shape, g.dtype) for g in gots]
    outs = pl.pallas_call(
        body, name=name, grid_spec=grid_spec, out_shape=shapes + shapes,
        compiler_params=_params(("arbitrary",)),
    )(c_idx, me_idx, *ts, *gots)
    return outs[:n], outs[n:]


def _add4_halves(parts, c_idx, name):
    n = len(parts)
    steps = 2

    def body(c_ref, *refs):
        del c_ref
        for p_ref, o_ref in zip(refs[:n], refs[n:]):
            acc = p_ref[0].astype(F32)
            for k in range(1, N_CHIPS):
                acc = acc + p_ref[k].astype(F32)
            o_ref[...] = acc

    grid_spec = pltpu.PrefetchScalarGridSpec(
        num_scalar_prefetch=1, grid=(steps,),
        in_specs=[pl.BlockSpec((N_CHIPS, p.shape[1] // steps, p.shape[2]), lambda i, c: (0, i, 0)) for p in parts],
        out_specs=[pl.BlockSpec((p.shape[1] // steps, p.shape[2]), lambda i, c: (c[0] * steps + i, 0))
                   for p in parts])
    return pl.pallas_call(
        body, name=name, grid_spec=grid_spec,
        out_shape=[jax.ShapeDtypeStruct((2 * p.shape[1], p.shape[2]), F32) for p in parts],
        compiler_params=_params(("parallel",)),
    )(c_idx, *parts)


def _rs_join(fulls, name):
    n = len(fulls)
    plan = _half_plan(fulls, 0)

    def body(*refs):
        out_refs = refs[n:2 * n]
        send_sems, recv_sems, _ = refs[2 * n:]
        x, y, c, _o = _place()
        sends = []
        for i, (a, r0, nr) in enumerate(plan):
            blk = out_refs[a].at[_rows(c * (fulls[a].shape[0] // 2) + r0, nr), :]
            cp = _remote(blk, blk, send_sems, recv_sems, i, (x, y, 1 - c))
            cp.start()
            sends.append(cp)
        for i, (a, r0, nr) in enumerate(plan):
            blk = out_refs[a].at[_rows((1 - c) * (fulls[a].shape[0] // 2) + r0, nr), :]
            _remote(blk, blk, send_sems, recv_sems, i, (x, y, c)).wait_recv()
        for cp in sends:
            cp.wait_send()

    outs = [jax.ShapeDtypeStruct(f.shape, f.dtype) for f in fulls]
    return _comm_call(body, name, fulls, outs, len(plan), 0, aliases={a: a for a in range(n)})


_HBM = pl.BlockSpec(memory_space=pltpu.HBM)
_SEM = pl.BlockSpec(memory_space=pltpu.SEMAPHORE)
_EFFECT = pltpu.SideEffectType.DATAFLOW_SIDE_EFFECTING


def _ici_plan(kind, a_list):
    if kind == "gather":
        return _half_plan(a_list, 0)
    plan = []
    for a, p in enumerate(a_list):
        plan += [(a, r0, nr) for r0, nr in _pieces(p.shape[1], p.shape[2] * p.dtype.itemsize)]
    return plan


def _ici_refs(kind, a_ref, b_ref, a_shape, r0, nr, c, me, peer):
    if kind == "gather":
        rows = _rows(c * (a_shape[0] // 2) + r0, nr)
        return a_ref.at[rows, :], b_ref.at[me, rows, :], b_ref.at[peer, rows, :]
    rows = pl.ds(r0, nr)
    return a_ref.at[peer, rows, :], b_ref.at[me, rows, :], b_ref.at[peer, rows, :]


def _ici_start(kind, a_list, b_list, name):
    n = len(a_list)
    plan = _ici_plan(kind, a_list)
    shapes = [a.shape for a in a_list]

    def body(*refs):
        a_refs, b_refs = refs[:n], refs[n:2 * n]
        send_sems, recv_sems = refs[2 * n], refs[2 * n + 1]
        token = refs[4 * n + 2]
        x, y, c, others = _place()
        me = 2 * x + y
        for i, (a, r0, nr) in enumerate(plan):
            for j, (ox, oy) in enumerate(others):
                src, dst, _ = _ici_refs(kind, a_refs[a], b_refs[a], shapes[a], r0, nr, c, me, 2 * ox + oy)
                _remote(src, dst, send_sems, recv_sems, 3 * i + j, (ox, oy, c)).start()
        token[...] = jnp.zeros_like(token)

    hbm = lambda v: pltpu.HBM(v.shape, v.dtype)
    ncp = 3 * len(plan)
    outs = pl.pallas_call(
        body, name=name,
        in_specs=[_HBM] * (2 * n),
        out_specs=[_SEM, _SEM] + [_HBM] * (2 * n) + [pl.BlockSpec(memory_space=pltpu.VMEM)],
        out_shape=[pltpu.SemaphoreType.DMA((ncp,)), pltpu.SemaphoreType.DMA((ncp,))]
                  + [hbm(v) for v in a_list] + [hbm(v) for v in b_list] + [jax.ShapeDtypeStruct((8, LANES), F32)],
        input_output_aliases={i: 2 + i for i in range(2 * n)},
        compiler_params=pltpu.CompilerParams(has_side_effects=_EFFECT),
    )(*[pltpu.with_memory_space_constraint(v, pltpu.HBM) for v in list(a_list) + list(b_list)])
    return outs[0], outs[1], outs[2:2 + n], outs[2 + n:2 + 2 * n], outs[2 + 2 * n]


def _ici_wait(kind, started, after, name):
    send_sems, recv_sems, a_list, b_list, _ = started
    afters = list(after) if isinstance(after, (list, tuple)) else [after]
    n = len(a_list)
    plan = _ici_plan(kind, a_list)
    shapes = [a.shape for a in a_list]

    def body(*refs):
        a_refs, b_refs = refs[:n], refs[n:2 * n]
        send_sems, recv_sems = refs[2 * n], refs[2 * n + 1]
        x, y, c, others = _place()
        me = 2 * x + y
        for i, (a, r0, nr) in enumerate(plan):
            for j, (ox, oy) in enumerate(others):
                src, dst, land = _ici_refs(kind, a_refs[a], b_refs[a], shapes[a], r0, nr, c, me, 2 * ox + oy)
                _remote(src, dst, send_sems, recv_sems, 3 * i + j, (ox, oy, c)).wait_send()
                _remote(land, land, send_sems, recv_sems, 3 * i + j, (x, y, c)).wait_recv()

    hbm = lambda v: pltpu.HBM(v.shape, v.dtype)
    outs = pl.pallas_call(
        body, name=name,
        in_specs=[_HBM] * (2 * n) + [_SEM, _SEM] + [_ANY] * len(afters),
        out_specs=[_HBM] * (2 * n),
        out_shape=[hbm(v) for v in a_list] + [hbm(v) for v in b_list],
        input_output_aliases={i: i for i in range(2 * n)},
        compiler_params=pltpu.CompilerParams(has_side_effects=_EFFECT),
    )(*a_list, *b_list, send_sems, recv_sems, *afters)
    return outs[n:]


def _rs_begin(ts, c_idx, me_idx, tag):
    got = _rs_swap(ts, "rs_swap_" + tag)
    pres, mine = _add_halves(ts, got, c_idx, me_idx, "rs_add2_" + tag)
    return _ici_start("scatter", pres, mine, "rs_xchg_start_" + tag)


def _rs_finish(started, after, c_idx, tag):
    parts = _ici_wait("scatter", started, after, "rs_xchg_wait_" + tag)
    return _rs_join(_add4_halves(parts, c_idx, "rs_add4_" + tag), "rs_join_" + tag)


def _pack_rows(pieces, rows, dtype):
    flat = jnp.concatenate([p.astype(dtype).reshape(-1) for p in pieces])
    return jnp.pad(flat, (0, rows * PACK_COLS - flat.shape[0])).reshape(rows, PACK_COLS)


def _unpack(flat, shapes):
    out, off = [], 0
    for shp in shapes:
        size = math.prod(shp)
        out.append(flat[off:off + size].reshape(shp))
        off += size
    return out


def _rows_for(n_elems, mult):
    rows = -(-n_elems // PACK_COLS)
    return -(-rows // mult) * mult


BIG_SHARDS = [("w_in", (D_MODEL, 1474)), ("w_branch_att", (D_ATT, 256)), ("w_branch_conv", (D_CONV, 256)),
              ("w_branch_sgu", (D_SGU, 256)), ("w_out", (256, D_MODEL)), ("w_ffn_up", (D_MODEL, FF_BLK)),
              ("w_ffn_down", (D_FF // N_CHIPS, D_MODEL))]
SMALL_SHARDS = [("b_gate", (3, 256)), ("conv_mix_w", (3, 64)), ("conv_ffn_w", (3, FF_BLK))]
REPLICATED = [("pre_mix_g", (D_MODEL,)), ("post_mix_g", (D_MODEL,)), ("pre_ffn_g", (D_MODEL,)),
              ("post_ffn_g", (D_MODEL,)), ("b_forget", (N_HEADS,)), ("sgu_ln_g", (D_SGU,)), ("sgu_ln_b", (D_SGU,)),
              ("sgu_w", (N_GROUPS, CHUNK, CHUNK)), ("sgu_b", (N_GROUPS, CHUNK))]
WEIGHT_ORDER = ["pre_mix_g", "post_mix_g", "pre_ffn_g", "post_ffn_g", "w_in", "b_forget", "b_gate", "conv_mix_w",
                "sgu_ln_g", "sgu_ln_b", "sgu_w", "sgu_b", "w_branch_att", "w_branch_conv", "w_branch_sgu", "w_out",
                "w_ffn_up", "conv_ffn_w", "w_ffn_down"]

_SMALL_ELEMS = sum(math.prod(s) for _, s in SMALL_SHARDS)
_REP_ELEMS = sum(math.prod(s) for _, s in REPLICATED)
_REP_QUARTER = -(-(DEPTH * _REP_ELEMS) // N_CHIPS)
SMALL_PARAM_ROWS = _rows_for(DEPTH * _SMALL_ELEMS, 32)
SMALL_ROWS = _rows_for(DEPTH * _SMALL_ELEMS + _REP_QUARTER, 32)
IN_WIDTH = 5896
IN_SHARD = IN_WIDTH // N_CHIPS
IN_SHARD_PAD = 1536
IN_PAD = 6144


def _gather_small(wts):
    shard = _pack_rows([wts[n] for n, _ in SMALL_SHARDS], SMALL_PARAM_ROWS, F32)
    full = _all_gather_chips(shard, "gather_small_params").reshape(N_CHIPS, -1)
    per_chip = [_unpack(full[j], [(DEPTH,) + s for _, s in SMALL_SHARDS]) for j in range(N_CHIPS)]
    return {n: jnp.concatenate([per_chip[j][i] for j in range(N_CHIPS)], axis=-1)
            for i, (n, _) in enumerate(SMALL_SHARDS)}


BIG_NAMES = [n for n, _ in BIG_SHARDS]
FIRST_NAMES = ["w_in"]
LATE_NAMES = BIG_NAMES[1:]


def _gather_begin(wts, l, me_idx, names, tag):
    cast = [_cast_shard(wts[n], l, me_idx, "cast_" + n) for n in names]
    return _ici_start("gather", [sh for sh, _ in cast], [ld for _, ld in cast], "gather_ici_start_" + tag)


def _gather_finish(started, after, names, tag):
    lands = _ici_wait("gather", started, after, "gather_ici_wait_" + tag)
    return dict(zip(names, _gather_d2d(lands, "gather_d2d_" + tag)))


def _pad_rows(a, rows):
    return jnp.pad(a, ((0, rows - a.shape[0]), (0, 0)))


def _whole_cols(land):
    return land.transpose(1, 0, 2).reshape(land.shape[1], -1)


_O_F = 3 * D_ATT
_O_B = _O_F + N_HEADS
_O_GL = _O_B + 3 * D_CONV + 2 * D_SGU


_LOCAL_ORDER = [(_O_GL, IN_WIDTH), (0, _O_F), (_O_B, _O_GL), (_O_F, _O_B)]


def _own_cols(land, lo, hi):
    pieces = []
    for j in range(N_CHIPS):
        a, b = max(lo, j * IN_SHARD), min(hi, (j + 1) * IN_SHARD)
        if a < b:
            pieces.append(land[j][:, a - j * IN_SHARD:b - j * IN_SHARD])
    return pieces


def _local_cols(m, lo, hi):
    pieces, off = [], 0
    for a, b in _LOCAL_ORDER:
        x, y = max(lo, a), min(hi, b)
        if x < y:
            pieces.append((x, m[:, off + x - a:off + y - a]))
        off += b - a
    pieces = [p for _, p in sorted(pieces, key=lambda t: t[0])]
    if hi > IN_WIDTH:
        pieces.append(jnp.zeros((m.shape[0], hi - max(lo, IN_WIDTH)), m.dtype))
    return pieces


def _prep_first(wts, lands, small, l):
    land = lands["w_in"]
    cf = small["conv_ffn_w"][l]
    blk = lambda a, j: a[:, j * FF_BLK:(j + 1) * FF_BLK]
    local = [piece for lo, hi in _LOCAL_ORDER for piece in _own_cols(land, lo, hi)]
    return {
        "w_p": jnp.concatenate(local + [jnp.zeros((D_MODEL, IN_PAD - IN_WIDTH), BF16)], axis=1),
        "wf_t": _pad_rows(jnp.concatenate(_own_cols(land, _O_F, _O_B), axis=1).T, F_ROWS),
        "b_forget": _pad_rows(wts["b_forget"][l].reshape(N_HEADS, 1), F_ROWS),
        "b_gate": _pad_rows(small["b_gate"][l], 8),
        "conv_mix_w": _pad_rows(small["conv_mix_w"][l], 8),
        "conv_ffn_w": _pad_rows(jnp.concatenate([blk(cf, 0), blk(cf, 2), blk(cf, 1), blk(cf, 3)], axis=1), 8),
        "pre_mix_g": wts["pre_mix_g"][l].reshape(1, -1), "post_mix_g": wts["post_mix_g"][l].reshape(1, -1),
        "pre_ffn_g": wts["pre_ffn_g"][l].reshape(1, -1), "post_ffn_g": wts["post_ffn_g"][l].reshape(1, -1),
        "ln_g": wts["sgu_ln_g"][l].reshape(1, -1), "ln_b": wts["sgu_ln_b"][l].reshape(1, -1),
        "sgu_w": wts["sgu_w"][l],
        "sgu_bias": jnp.repeat(wts["sgu_b"][l].T, HEAD_DIM, axis=1),
    }


def _prep_late(lands):
    up = lands["w_ffn_up"]
    return {
        "w_att": _whole_cols(lands["w_branch_att"]), "w_conv": _whole_cols(lands["w_branch_conv"]),
        "w_sgu": _whole_cols(lands["w_branch_sgu"]),
        "w_out": lands["w_out"].reshape(D_MODEL, D_MODEL),
        "w_up": jnp.concatenate([up[0], up[2], up[1], up[3]], axis=1),
        "w_down": lands["w_ffn_down"].reshape(D_FF, D_MODEL),
    }


def _layer_fwd(x, p, dep=None, late=None):
    s = x.shape[0]
    xn = _rms_fwd(x, p["pre_mix_g"], "rms_pre_mix", dep)
    h = _mm(xn, p["w_p"], "nn", BF16, "mm_in", s, 512, D_MODEL)
    f_row = _mm(p["wf_t"], xn, "nt", F32, "mm_forget", F_ROWS, 2048, D_MODEL)
    ck = _gate_fwd(f_row, p["b_forget"], "gate_fwd")
    o, o_f32, lse = _attn_fwd(h, ck, "attn_fwd")
    yc = _sconv_fwd(h, p["conv_mix_w"], "sconv_fwd")
    ys = _sgu_fwd(h, p["ln_g"], p["ln_b"], p["sgu_w"], p["sgu_bias"], "sgu_fwd")
    if late is not None:
        p.update(late(o))
    merged = _merge_fwd(h, (o, yc, ys), (p["w_att"], p["w_conv"], p["w_sgu"]), p["b_gate"], "merge_fwd")
    mo = _mm(merged, p["w_out"], "nn", F32, "mm_out", 2048, 512, D_MODEL)
    x1, xn2 = _resid_post_norm(x, mo, p["post_mix_g"], p["pre_ffn_g"], "post_mix")
    h2 = _mm(xn2, p["w_up"], "nn", BF16, "mm_up", 2048, 512, D_MODEL)
    pact = _ffn_act_fwd(h2, p["conv_ffn_w"], "ffn_act_fwd")
    ff = _mm(pact, p["w_down"], "nn", F32, "mm_down", 1024, D_MODEL, D_FF)
    x2 = _resid_post(x1, ff, p["post_ffn_g"], "post_ffn")
    saved = dict(x=x, xn=xn, h=h, f_row=f_row, ck=ck, o=o, o_f32=o_f32, lse=lse, yc=yc, ys=ys, merged=merged, mo=mo, x1=x1,
                 xn2=xn2, h2=h2, pact=pact, ff=ff)
    return x2, saved


def _layer_bwd(dx2, p, sv, dep=None, early=None):
    s = dx2.shape[0]
    g = {}
    same = lambda b: b
    dff, g["post_ffn_g"] = _rms_bwd(sv["ff"], p["post_ffn_g"], [dx2], None, BF16, "post_ffn_bwd", dep)
    dpact = _mm(dff, p["w_down"], "nt", BF16, "mm_down_dx", 2048, FF_BLK, D_MODEL)
    t_down = _mm(sv["pact"], dff, "tn", BF16, "mm_down_dw", 256, D_MODEL, s).reshape(N_CHIPS, -1, D_MODEL)
    dh2, dconv_ffn = _ffn_act_conv_bwd(sv["h2"], p["conv_ffn_w"], dpact, "ffn_act_conv_bwd")
    dxn2 = _mm(dh2, p["w_up"], "nt", F32, "mm_up_dx", 512, D_MODEL, 2 * D_FF)
    t_up = _mm(sv["xn2"], dh2, "tn", BF16, "mm_up_dw", 512, FF_BLK, s, chip_of=lambda b: (b % 2) * 2 + b // 2)
    dx1, g["pre_ffn_g"] = _rms_bwd(sv["x1"], p["pre_ffn_g"], [dxn2], dx2, F32, "pre_ffn_bwd")
    dep_mix = early([t_up, t_down]) if early is not None else None
    dmo, g["post_mix_g"] = _rms_bwd(sv["mo"], p["post_mix_g"], [dx1], None, BF16, "post_mix_bwd", dep_mix)
    dmerged = _mm(dmo, p["w_out"], "nt", F32, "mm_out_dx", 2048, 512, D_MODEL)
    t_out = _mm(sv["merged"], dmo, "tn", BF16, "mm_out_dw", 512, D_MODEL, s).reshape(N_CHIPS, -1, D_MODEL)
    acts = (sv["o"], sv["yc"], sv["ys"])
    ws = (p["w_att"], p["w_conv"], p["w_sgu"])
    dy_a, dy_c, dy_s, dgl, db_gate = _merge_bwd(sv["h"], acts, ws, p["b_gate"], dmerged, "merge_bwd")
    do = _mm(dy_a, p["w_att"], "nt", BF16, "mm_att_dx", 2048, D_ATT, D_MODEL)
    dyc = _mm(dy_c, p["w_conv"], "nt", BF16, "mm_conv_dx", 2048, D_CONV, D_MODEL)
    dys = _mm(dy_s, p["w_sgu"], "nt", BF16, "mm_sgu_dx", 2048, D_SGU, D_MODEL)
    t_att = _mm(sv["o"], dy_a, "tn", BF16, "mm_att_dw", D_ATT, 256, s, chip_of=same)
    t_conv = _mm(sv["yc"], dy_c, "tn", BF16, "mm_conv_dw", D_CONV, 256, s, chip_of=same)
    t_sgu = _mm(sv["ys"], dy_s, "tn", BF16, "mm_sgu_dw", D_SGU, 256, s, chip_of=same)
    d_conv, dconv_mix = _sconv_bwd(sv["h"], p["conv_mix_w"], dyc, "sconv_bwd")
    d_sgu, g["sgu_ln_g"], g["sgu_ln_b"], g["sgu_w"], dbias = _sgu_bwd(
        sv["h"], p["ln_g"], p["ln_b"], p["sgu_w"], p["sgu_bias"], dys, "sgu_bwd")
    dq, dk, dv, dc_even, dc_odd = _attn_bwd(sv["h"], sv["ck"], sv["o_f32"], sv["lse"], do, "attn_bwd")
    df, db_forget = _gate_bwd(sv["f_row"], p["b_forget"], dc_even, dc_odd, "gate_bwd")
    f_cols = jnp.concatenate([df[:N_HEADS].T, jnp.zeros((s, IN_PAD - IN_WIDTH), BF16)], axis=1)
    dh = _assemble_dh(dgl, [dq, dk, dv, d_conv, d_sgu, f_cols], "assemble_dh")
    dxn = _mm(dh, p["w_p"], "nt", F32, "mm_in_dx", 512, D_MODEL, IN_PAD)
    dw_p = _mm(sv["xn"], dh, "tn", BF16, "mm_in_dw", D_MODEL, 512, s)
    t_in = jnp.stack([jnp.concatenate(_local_cols(dw_p, j * IN_SHARD, j * IN_SHARD + IN_SHARD_PAD), axis=1)
                      for j in range(N_CHIPS)])
    dx, g["pre_mix_g"] = _rms_bwd(sv["x"], p["pre_mix_g"], [dxn], dx1, F32, "pre_mix_bwd")
    blk = lambda a, j: a[:, j * FF_BLK:(j + 1) * FF_BLK]
    g["conv_ffn_w"] = jnp.concatenate([blk(dconv_ffn, 0), blk(dconv_ffn, 2), blk(dconv_ffn, 1),
                                       blk(dconv_ffn, 3)], axis=1)[:3]
    g["conv_mix_w"] = dconv_mix[:3]
    g["b_gate"] = db_gate[:3]
    g["b_forget"] = db_forget[:N_HEADS, 0]
    g["sgu_b"] = jnp.sum(dbias.reshape(CHUNK, N_GROUPS, HEAD_DIM), axis=-1).T
    for n in ("pre_mix_g", "post_mix_g", "pre_ffn_g", "post_ffn_g", "sgu_ln_g", "sgu_ln_b"):
        g[n] = g[n].reshape(-1)
    mix = [t_in, t_att, t_conv, t_sgu, t_out]
    return dx, (mix if early is not None else mix + [t_up, t_down]), g


def _assemble_dh(dh, pieces, name):
    s = dh.shape[0]
    t = _tile(s, 512)
    width = sum(a.shape[1] for a in pieces)
    assert 2 * width == dh.shape[1]

    def body(*refs):
        out = refs[-1]
        col = 0
        for ref in refs[1:-1]:
            w = ref.shape[1]
            out[:, col:col + w] = ref[...].astype(out.dtype)
            col += w

    return pl.pallas_call(
        body, name=name, grid=(s // t,),
        in_specs=[_ANY] + [pl.BlockSpec((t, a.shape[1]), lambda i: (i, 0)) for a in pieces],
        out_specs=pl.BlockSpec((t, width), lambda i: (i, 1)),
        out_shape=jax.ShapeDtypeStruct(dh.shape, dh.dtype),
        input_output_aliases={0: 0},
        compiler_params=_params(("parallel",)),
    )(dh, *pieces)


def _shard_cols(a, j):
    w = a.shape[-1] // N_CHIPS
    return a[..., j * w:(j + 1) * w]


def kernel(x, pre_mix_g, post_mix_g, pre_ffn_g, post_ffn_g, w_in, b_forget, b_gate, conv_mix_w, sgu_ln_g, sgu_ln_b, sgu_w, sgu_b, w_branch_att, w_branch_conv, w_branch_sgu, w_out, w_ffn_up, conv_ffn_w, w_ffn_down, loss_target, m_pre_mix_g, m_post_mix_g, m_pre_ffn_g, m_post_ffn_g, m_w_in, m_b_forget, m_b_gate, m_conv_mix_w, m_sgu_ln_g, m_sgu_ln_b, m_sgu_w, m_sgu_b, m_w_branch_att, m_w_branch_conv, m_w_branch_sgu, m_w_out, m_w_ffn_up, m_conv_ffn_w, m_w_ffn_down, v_pre_mix_g, v_post_mix_g, v_pre_ffn_g, v_post_ffn_g, v_w_in, v_b_forget, v_b_gate, v_conv_mix_w, v_sgu_ln_g, v_sgu_ln_b, v_sgu_w, v_sgu_b, v_w_branch_att, v_w_branch_conv, v_w_branch_sgu, v_w_out, v_w_ffn_up, v_conv_ffn_w, v_w_ffn_down):
    wts = dict(pre_mix_g=pre_mix_g, post_mix_g=post_mix_g, pre_ffn_g=pre_ffn_g, post_ffn_g=post_ffn_g, w_in=w_in,
               b_forget=b_forget, b_gate=b_gate, conv_mix_w=conv_mix_w, sgu_ln_g=sgu_ln_g, sgu_ln_b=sgu_ln_b,
               sgu_w=sgu_w, sgu_b=sgu_b, w_branch_att=w_branch_att, w_branch_conv=w_branch_conv,
               w_branch_sgu=w_branch_sgu, w_out=w_out, w_ffn_up=w_ffn_up, conv_ffn_w=conv_ffn_w,
               w_ffn_down=w_ffn_down)
    moms = dict(pre_mix_g=m_pre_mix_g, post_mix_g=m_post_mix_g, pre_ffn_g=m_pre_ffn_g, post_ffn_g=m_post_ffn_g,
                w_in=m_w_in, b_forget=m_b_forget, b_gate=m_b_gate, conv_mix_w=m_conv_mix_w, sgu_ln_g=m_sgu_ln_g,
                sgu_ln_b=m_sgu_ln_b, sgu_w=m_sgu_w, sgu_b=m_sgu_b, w_branch_att=m_w_branch_att,
                w_branch_conv=m_w_branch_conv, w_branch_sgu=m_w_branch_sgu, w_out=m_w_out, w_ffn_up=m_w_ffn_up,
                conv_ffn_w=m_conv_ffn_w, w_ffn_down=m_w_ffn_down)
    vels = dict(pre_mix_g=v_pre_mix_g, post_mix_g=v_post_mix_g, pre_ffn_g=v_pre_ffn_g, post_ffn_g=v_post_ffn_g,
                w_in=v_w_in, b_forget=v_b_forget, b_gate=v_b_gate, conv_mix_w=v_conv_mix_w, sgu_ln_g=v_sgu_ln_g,
                sgu_ln_b=v_sgu_ln_b, sgu_w=v_sgu_w, sgu_b=v_sgu_b, w_branch_att=v_w_branch_att,
                w_branch_conv=v_w_branch_conv, w_branch_sgu=v_w_branch_sgu, w_out=v_w_out, w_ffn_up=v_w_ffn_up,
                conv_ffn_w=v_conv_ffn_w, w_ffn_down=v_w_ffn_down)

    c_idx = lax.axis_index("c").astype(jnp.int32).reshape(1)
    me_idx = (2 * lax.axis_index("x") + lax.axis_index("y")).astype(jnp.int32).reshape(1)
    small = _gather_small(wts)

    xs = x[0]
    layers, saved = [], []
    first = _gather_begin(wts, 0, me_idx, FIRST_NAMES, "first")
    rest = _gather_begin(wts, 0, me_idx, LATE_NAMES, "late")
    lands = _gather_finish(first, xs, FIRST_NAMES, "first")
    late = lambda after: _prep_late(_gather_finish(rest, after, LATE_NAMES, "late"))
    for l in range(DEPTH):
        p = _prep_first(wts, lands, small, l)
        if l > 0:
            p.update(_prep_late(lands))
        nxt = _gather_begin(wts, l + 1, me_idx, BIG_NAMES, "all") if l + 1 < DEPTH else None
        dep = ([nxt[4]] if nxt else []) + ([rest[4]] if l == 0 else [])
        xs, sv = _layer_fwd(xs, p, dep or None, late if l == 0 else None)
        if nxt:
            lands = _gather_finish(nxt, xs, BIG_NAMES, "all")
        layers.append(p)
        saved.append(sv)
    dy, loss_part = _loss_head(xs, loss_target[0], "loss_head")
    loss = lax.psum(loss_part[0, 0], ("x", "y", "c"))

    big_red = [None] * DEPTH
    small_grads = [None] * DEPTH
    pending = None
    ffn = []
    for l in reversed(range(DEPTH)):
        early = None
        if l == 0:
            def early(ts_ffn):
                ffn.append(_rs_begin(ts_ffn, c_idx, me_idx, "ffn"))
                return ffn[0][4]
        dy, ts, small_grads[l] = _layer_bwd(dy, layers[l], saved[l], pending[4] if pending else None, early)
        if pending:
            big_red[l + 1] = _rs_finish(pending, dy, c_idx, "big")
        pending = _rs_begin(ts, c_idx, me_idx, "mix" if l == 0 else "big")
    red_ffn = _rs_finish(ffn[0], dy, c_idx, "ffn")
    grad_x = dy[None]

    rep_flat = jnp.concatenate([small_grads[l][n].reshape(-1) for l in range(DEPTH) for n, _ in REPLICATED])
    rep_flat = jnp.pad(rep_flat, (0, N_CHIPS * _REP_QUARTER - rep_flat.shape[0]))
    rows = []
    for j in range(N_CHIPS):
        pieces = [_shard_cols(small_grads[l][n], j) for l in range(DEPTH) for n, _ in SMALL_SHARDS]
        pieces.append(rep_flat[j * _REP_QUARTER:(j + 1) * _REP_QUARTER])
        rows.append(_pack_rows(pieces, SMALL_ROWS, F32))
    small_red = _reduce_scatter_chips(jnp.stack(rows), "small")
    small_all = _all_gather_chips(small_red, "gather_small")
    done = {}
    for k, n in enumerate(("w_ffn_up", "w_ffn_down")):
        i = BIG_NAMES.index(n)
        g = jnp.stack([red_ffn[k]] + [big_red[l][i] for l in range(1, DEPTH)])
        done[n] = (g,) + _adamw(wts[n], g, moms[n], vels[n], "adamw_" + n)
    big_red[0] = _rs_finish(pending, [small_all] + [done[n][1] for n in done], c_idx, "mix") + red_ffn
    small_all = small_all.reshape(N_CHIPS, -1)

    grads = {}
    for i, (n, _) in enumerate(BIG_SHARDS):
        if n not in done:
            grads[n] = jnp.stack([big_red[l][i][:, :IN_SHARD] if n == "w_in" else big_red[l][i]
                                  for l in range(DEPTH)])
    mine_small = small_red.reshape(-1)
    parts = _unpack(mine_small, [s for _ in range(DEPTH) for _, s in SMALL_SHARDS])
    for i, (n, _) in enumerate(SMALL_SHARDS):
        grads[n] = jnp.stack([parts[l * len(SMALL_SHARDS) + i] for l in range(DEPTH)])
    off = DEPTH * _SMALL_ELEMS
    rep_all = jnp.concatenate([small_all[j, off:off + _REP_QUARTER] for j in range(N_CHIPS)])
    parts = _unpack(rep_all, [s for _ in range(DEPTH) for _, s in REPLICATED])
    for i, (n, _) in enumerate(REPLICATED):
        grads[n] = jnp.stack([parts[l * len(REPLICATED) + i] for l in range(DEPTH)])

    deltas, new_m, new_v = {}, {}, {}
    for n in WEIGHT_ORDER:
        if n in done:
            grads[n], deltas[n], new_m[n], new_v[n] = done[n]
        else:
            deltas[n], new_m[n], new_v[n] = _adamw(wts[n], grads[n], moms[n], vels[n], "adamw_" + n)
    return (loss, grad_x, *[grads[n] for n in WEIGHT_ORDER], *[deltas[n] for n in WEIGHT_ORDER],
            *[new_m[n] for n in WEIGHT_ORDER], *[new_v[n] for n in WEIGHT_ORDER])
```

```python
import functools
import math

import jax
import jax.numpy as jnp
from jax import lax
from jax.experimental import pallas as pl
from jax.experimental.pallas import tpu as pltpu

F32 = jnp.float32
BF16 = jnp.bfloat16
MXU_DTYPE = jnp.bfloat16

D_MODEL = 1024
HEAD_DIM = 64
N_HEADS = 8
D_ATT = 512
D_CONV = 256
D_SGU = 256
N_GROUPS = 4
CHUNK = 128
D_FF = 2816
DEPTH = 4
RMS_EPS = 1e-6
LN_EPS = 1e-5
N_CHIPS = 4
LANES = 128
PACK_COLS = 1024
HALO = 16

ADAM_LR = 0.001
ADAM_B1 = 0.9
ADAM_B2 = 0.999
ADAM_EPS = 1e-08
ADAM_WD = 0.01
ADAM_STEP = 10

OFF_GL = 0
OFF_Q = 3 * D_MODEL
OFF_K = OFF_Q + D_ATT
OFF_V = OFF_K + D_ATT
OFF_BG = OFF_V + D_ATT
OFF_CG = OFF_BG + D_CONV
OFF_HC = OFF_CG + D_CONV
OFF_U = OFF_HC + D_CONV
OFF_VS = OFF_U + D_SGU
W_P = OFF_VS + D_SGU
F_ROWS = 16

VMEM_LIMIT = 56 * 1024 * 1024
MESH = pl.DeviceIdType.MESH


def _params(sem=None):
    if sem is None:
        return pltpu.CompilerParams(vmem_limit_bytes=VMEM_LIMIT)
    return pltpu.CompilerParams(dimension_semantics=sem, vmem_limit_bytes=VMEM_LIMIT)


def _tile(dim, pref):
    if dim <= pref:
        return dim
    if dim % pref == 0:
        return pref
    return dim


_DIMS = {"nn": (((1,), (0,)), ((), ())), "nt": (((1,), (1,)), ((), ())), "tn": (((0,), (0,)), ((), ()))}


def _mm(a, b, mode, out_dtype, name, tm, tn, tk, chip_of=None):
    if mode == "tn":
        K, M = a.shape
    else:
        M, K = a.shape
    N = b.shape[0] if mode == "nt" else b.shape[1]
    tm, tn, tk = _tile(M, tm), _tile(N // N_CHIPS if chip_of else N, tn), _tile(K, tk)
    nk = K // tk
    dims = _DIMS[mode]

    def body(a_ref, b_ref, o_ref, *acc):
        part = lax.dot_general(a_ref[...].astype(MXU_DTYPE), b_ref[...].astype(MXU_DTYPE), dims,
                               preferred_element_type=F32)
        if nk == 1:
            o_ref[...] = part.astype(o_ref.dtype)
        else:
            acc_ref = acc[0]
            k = pl.program_id(2)

            @pl.when(k == 0)
            def _():
                acc_ref[...] = part

            @pl.when(k > 0)
            def _():
                acc_ref[...] += part

            @pl.when(k == nk - 1)
            def _():
                o_ref[...] = acc_ref[...].astype(o_ref.dtype)

    if mode == "tn":
        a_spec = pl.BlockSpec((tk, tm), lambda i, j, k: (k, i))
    else:
        a_spec = pl.BlockSpec((tm, tk), lambda i, j, k: (i, k))
    if mode == "nt":
        b_spec = pl.BlockSpec((tn, tk), lambda i, j, k: (j, k))
    else:
        b_spec = pl.BlockSpec((tk, tn), lambda i, j, k: (k, j))
    if chip_of is None:
        out_spec = pl.BlockSpec((tm, tn), lambda i, j, k: (i, j))
        out_shape = jax.ShapeDtypeStruct((M, N), out_dtype)
    else:
        per = (N // N_CHIPS) // tn
        out_spec = pl.BlockSpec((None, tm, tn), lambda i, j, k: (chip_of(j // per), i, j % per))
        out_shape = jax.ShapeDtypeStruct((N_CHIPS, M, N // N_CHIPS), out_dtype)
    return pl.pallas_call(
        body,
        name=name,
        grid=(M // tm, N // tn, nk),
        in_specs=[a_spec, b_spec],
        out_specs=out_spec,
        out_shape=out_shape,
        scratch_shapes=[pltpu.VMEM((tm, tn), F32)] if nk > 1 else [],
        compiler_params=_params(("parallel", "parallel", "arbitrary")),
    )(a, b)


_GELU_K = math.sqrt(2.0 / math.pi)
_GELU_C = 0.044715


def _gelu(x):
    t = jnp.tanh(_GELU_K * (x + _GELU_C * (x * x * x)))
    return x * (0.5 * (1.0 + t))


def _gelu_and_grad(x):
    x2 = x * x
    t = jnp.tanh(_GELU_K * (x + _GELU_C * (x2 * x)))
    cdf = 0.5 * (1.0 + t)
    dcdf = 0.5 * (1.0 - t * t) * (_GELU_K * (1.0 + 3.0 * _GELU_C * x2))
    return x * cdf, cdf + x * dcdf


def _sigmoid(x):
    return 1.0 / (1.0 + jnp.exp(-x))


def _shift_down(cur, prev, k):
    h = prev.shape[0]
    ext = jnp.concatenate([prev, cur], axis=0)
    return pltpu.roll(ext, k, 0)[h:]


def _shift_up(cur, nxt, k):
    t, h = cur.shape[0], nxt.shape[0]
    ext = jnp.concatenate([cur, nxt], axis=0)
    return pltpu.roll(ext, t + h - k, 0)[:t]


def _row_sum8(x):
    t, c = x.shape
    return jnp.sum(x.reshape(t // 8, 8, c), axis=0)


_DEP = pl.BlockSpec((8, LANES), lambda i: (0, 0))


def _rms_fwd(x, g, name, dep=None):
    s, d = x.shape
    t = _tile(s, 512)

    def body(x_ref, g_ref, *rest):
        o_ref = rest[-1]
        xv = x_ref[...]
        r = lax.rsqrt(jnp.mean(xv * xv, axis=-1, keepdims=True) + RMS_EPS)
        o_ref[...] = (xv * r * g_ref[...]).astype(o_ref.dtype)

    deps = [] if dep is None else list(dep) if isinstance(dep, (list, tuple)) else [dep]
    return pl.pallas_call(
        body, name=name, grid=(s // t,),
        in_specs=[pl.BlockSpec((t, d), lambda i: (i, 0)), pl.BlockSpec((1, d), lambda i: (0, 0))] + [_DEP] * len(deps),
        out_specs=pl.BlockSpec((t, d), lambda i: (i, 0)),
        out_shape=jax.ShapeDtypeStruct((s, d), BF16),
        compiler_params=_params(("parallel",)),
    )(x, g, *deps)


def _resid_post(x, y, g, name):
    s, d = x.shape
    t = _tile(s, 512)

    def body(x_ref, y_ref, g_ref, o_ref):
        yv = y_ref[...]
        r = lax.rsqrt(jnp.mean(yv * yv, axis=-1, keepdims=True) + RMS_EPS)
        o_ref[...] = x_ref[...] + yv * r * g_ref[...]

    row = pl.BlockSpec((t, d), lambda i: (i, 0))
    return pl.pallas_call(
        body, name=name, grid=(s // t,),
        in_specs=[row, row, pl.BlockSpec((1, d), lambda i: (0, 0))],
        out_specs=row,
        out_shape=jax.ShapeDtypeStruct((s, d), F32),
        compiler_params=_params(("parallel",)),
    )(x, y, g)


def _resid_post_norm(x, y, g, g_next, name):
    s, d = x.shape
    t = _tile(s, 512)

    def body(x_ref, y_ref, g_ref, gn_ref, o_ref, xn_ref):
        yv = y_ref[...]
        r = lax.rsqrt(jnp.mean(yv * yv, axis=-1, keepdims=True) + RMS_EPS)
        x1 = x_ref[...] + yv * r * g_ref[...]
        o_ref[...] = x1
        r1 = lax.rsqrt(jnp.mean(x1 * x1, axis=-1, keepdims=True) + RMS_EPS)
        xn_ref[...] = (x1 * r1 * gn_ref[...]).astype(xn_ref.dtype)

    row = pl.BlockSpec((t, d), lambda i: (i, 0))
    vec = pl.BlockSpec((1, d), lambda i: (0, 0))
    return pl.pallas_call(
        body, name=name, grid=(s // t,),
        in_specs=[row, row, vec, vec],
        out_specs=[row, row],
        out_shape=[jax.ShapeDtypeStruct((s, d), F32), jax.ShapeDtypeStruct((s, d), BF16)],
        compiler_params=_params(("parallel",)),
    )(x, y, g, g_next)


def _rms_bwd(xin, g, dys, dres, out_dtype, name, dep=None):
    s, d = xin.shape
    t = _tile(s, 512)
    n = s // t
    n_dy = len(dys)
    has_res = dres is not None
    deps = [] if dep is None else [dep]

    def body(*refs):
        x_ref, g_ref = refs[0], refs[1]
        dy_refs = refs[2:2 + n_dy]
        pos = 2 + n_dy
        res_ref = refs[pos] if has_res else None
        pos += (1 if has_res else 0) + len(deps)
        dx_ref, dg_ref, acc_ref = refs[pos], refs[pos + 1], refs[pos + 2]
        i = pl.program_id(0)
        xv = x_ref[...]
        dy = dy_refs[0][...].astype(F32)
        for extra in dy_refs[1:]:
            dy = dy + extra[...].astype(F32)
        r = lax.rsqrt(jnp.mean(xv * xv, axis=-1, keepdims=True) + RMS_EPS)
        u = dy * g_ref[...]
        xr = xv * r
        dx = r * (u - xr * jnp.mean(u * xr, axis=-1, keepdims=True))
        if has_res:
            dx = dx + res_ref[...]
        dx_ref[...] = dx.astype(dx_ref.dtype)
        part = _row_sum8(dy * xr)

        @pl.when(i == 0)
        def _():
            acc_ref[...] = part

        @pl.when(i > 0)
        def _():
            acc_ref[...] += part

        @pl.when(i == n - 1)
        def _():
            dg_ref[...] = jnp.sum(acc_ref[...], axis=0, keepdims=True)

    row = pl.BlockSpec((t, d), lambda i: (i, 0))
    vec = pl.BlockSpec((1, d), lambda i: (0, 0))
    ins = [xin, g, *dys] + ([dres] if has_res else []) + deps
    return pl.pallas_call(
        body, name=name, grid=(n,),
        in_specs=[row, vec] + [row] * (n_dy + (1 if has_res else 0)) + [_DEP] * len(deps),
        out_specs=[row, vec],
        out_shape=[jax.ShapeDtypeStruct((s, d), out_dtype), jax.ShapeDtypeStruct((1, d), F32)],
        scratch_shapes=[pltpu.VMEM((8, d), F32)],
        compiler_params=_params(("arbitrary",)),
    )(*ins)


def _loss_head(y, target, name):
    s, d = y.shape
    t = _tile(s, 512)
    n = s // t

    def body(y_ref, t_ref, dy_ref, loss_ref, acc_ref):
        i = pl.program_id(0)
        e = y_ref[...] - t_ref[...]
        dy_ref[...] = e * (1.0 / d)
        part = _row_sum8(e * e)

        @pl.when(i == 0)
        def _():
            acc_ref[...] = part

        @pl.when(i > 0)
        def _():
            acc_ref[...] += part

        @pl.when(i == n - 1)
        def _():
            tot = jnp.sum(jnp.sum(acc_ref[...], axis=0, keepdims=True), axis=1, keepdims=True)
            loss_ref[...] = tot * (0.5 / d)

    row = pl.BlockSpec((t, d), lambda i: (i, 0))
    return pl.pallas_call(
        body, name=name, grid=(n,),
        in_specs=[row, row],
        out_specs=[row, pl.BlockSpec((1, 1), lambda i: (0, 0))],
        out_shape=[jax.ShapeDtypeStruct((s, d), F32), jax.ShapeDtypeStruct((1, 1), F32)],
        scratch_shapes=[pltpu.VMEM((8, d), F32)],
        compiler_params=_params(("arbitrary",)),
    )(y, target)


def _split3(x):
    hi = x.astype(BF16)
    r1 = x - hi.astype(F32)
    mid = r1.astype(BF16)
    lo = (r1 - mid.astype(F32)).astype(BF16)
    return hi, mid, lo


def _tri_dot(x, tri):
    hi, mid, lo = _split3(x)
    dn = _DIMS["nn"]
    out = lax.dot_general(hi, tri, dn, preferred_element_type=F32)
    out = out + lax.dot_general(mid, tri, dn, preferred_element_type=F32)
    return out + lax.dot_general(lo, tri, dn, preferred_element_type=F32)


def _log_sigmoid(z):
    return jnp.minimum(z, 0.0) - jnp.log(1.0 + jnp.exp(-jnp.abs(z)))


def _gate_fwd(f_row, b_col, name):
    rows, s = f_row.shape
    t = _tile(s, 512)
    n = s // t

    def body(f_ref, b_ref, ck_ref, carry_ref):
        i = pl.program_id(0)

        @pl.when(i == 0)
        def _():
            carry_ref[...] = jnp.zeros_like(carry_ref)

        logf = _log_sigmoid(f_ref[...] + b_ref[...])
        r = lax.broadcasted_iota(jnp.int32, (t, t), 0)
        c = lax.broadcasted_iota(jnp.int32, (t, t), 1)
        tri = jnp.where(r <= c, 1.0, 0.0).astype(BF16)
        cs = _tri_dot(logf, tri) + carry_ref[...]
        carry_ref[...] = cs[:, t - 1:t]
        terms = [part.astype(F32) for part in _split3(-cs)]
        sub = lax.broadcasted_iota(jnp.int32, (LANES, t), 0)
        for p in range(N_HEADS // 2):
            stacked = jnp.zeros((LANES, t), F32)
            for hh in range(2):
                for j, term in enumerate(terms):
                    h = 2 * p + hh
                    stacked = jnp.where(sub == 3 * hh + j, jnp.broadcast_to(term[h:h + 1, :], (LANES, t)), stacked)
            ck_ref[p] = stacked.T.astype(ck_ref.dtype)

    return pl.pallas_call(
        body, name=name, grid=(n,),
        in_specs=[pl.BlockSpec((rows, t), lambda i: (0, i)), pl.BlockSpec((rows, 1), lambda i: (0, 0))],
        out_specs=pl.BlockSpec((N_HEADS // 2, t, LANES), lambda i: (0, i, 0)),
        out_shape=jax.ShapeDtypeStruct((N_HEADS // 2, s, LANES), BF16),
        scratch_shapes=[pltpu.VMEM((rows, 1), F32)],
        compiler_params=_params(("arbitrary",)),
    )(f_row, b_col)


def _gate_bwd(f_row, b_col, dc_even, dc_odd, name):
    rows, s = f_row.shape
    t = _tile(s, 512)
    n = s // t

    def body(f_ref, b_ref, dce_ref, dco_ref, df_ref, db_ref, carry_ref, acc_ref):
        i = pl.program_id(0)

        @pl.when(i == 0)
        def _():
            carry_ref[...] = jnp.zeros_like(carry_ref)
            acc_ref[...] = jnp.zeros_like(acc_ref)

        head = lax.broadcasted_iota(jnp.int32, (rows, t), 0)
        dcv = jnp.zeros((rows, t), F32)
        for h in range(N_HEADS):
            src = dce_ref if h % 2 == 0 else dco_ref
            dcv = jnp.where(head == h, jnp.broadcast_to(src[h // 2, 0:1, :], (rows, t)), dcv)
        r = lax.broadcasted_iota(jnp.int32, (t, t), 0)
        c = lax.broadcasted_iota(jnp.int32, (t, t), 1)
        tri = jnp.where(r >= c, 1.0, 0.0).astype(BF16)
        dlogf = _tri_dot(dcv, tri) + carry_ref[...]
        carry_ref[...] = dlogf[:, 0:1]
        z = f_ref[...] + b_ref[...]
        df = dlogf * _sigmoid(-z)
        df_ref[...] = df.astype(df_ref.dtype)
        acc_ref[...] += jnp.sum(df, axis=1, keepdims=True)

        @pl.when(i == n - 1)
        def _():
            db_ref[...] = acc_ref[...]

    rev = lambda i: (0, n - 1 - i)
    dc_spec = pl.BlockSpec((N_HEADS // 2, 8, t), lambda i: (0, 0, n - 1 - i))
    return pl.pallas_call(
        body, name=name, grid=(n,),
        in_specs=[pl.BlockSpec((rows, t), rev), pl.BlockSpec((rows, 1), lambda i: (0, 0)), dc_spec, dc_spec],
        out_specs=[pl.BlockSpec((rows, t), rev), pl.BlockSpec((rows, 1), lambda i: (0, 0))],
        out_shape=[jax.ShapeDtypeStruct((rows, s), BF16), jax.ShapeDtypeStruct((rows, 1), F32)],
        scratch_shapes=[pltpu.VMEM((rows, 1), F32), pltpu.VMEM((rows, 1), F32)],
        compiler_params=_params(("arbitrary",)),
    )(f_row, b_col, dc_even, dc_odd)


_NEG = -1e30
_SCALE = HEAD_DIM ** -0.5


def _head_masks():
    lane = lax.broadcasted_iota(jnp.int32, (1, LANES), 1)
    return [lane < HEAD_DIM, lane >= HEAD_DIM]


def _attn_fwd(h, ck, name):
    s = h.shape[0]
    t = _tile(s, 512)
    n = s // t
    qb, kb, vb = OFF_Q // LANES, OFF_K // LANES, OFF_V // LANES

    pairs = [(qi, ki) for qi in range(n) for ki in range(qi + 1)]
    qi_tab = jnp.asarray([qi for qi, _ in pairs], jnp.int32)
    ki_tab = jnp.asarray([ki for _, ki in pairs], jnp.int32)

    def body(qi_ref, ki_ref, q_ref, k_ref, v_ref, ck_ref, o_ref, of_ref, lse_ref, m_ref, l_ref, acc_ref):
        qi, ki = qi_ref[pl.program_id(1)], ki_ref[pl.program_id(1)]
        masks = _head_masks()
        lane = lax.broadcasted_iota(jnp.int32, (1, LANES), 1)

        @pl.when(ki == 0)
        def _():
            m_ref[...] = jnp.full_like(m_ref, _NEG)
            l_ref[...] = jnp.zeros_like(l_ref)
            acc_ref[...] = jnp.zeros_like(acc_ref)

        def step(diag):
            q = q_ref[...] * _SCALE
            k_aug = jnp.concatenate([k_ref[...], ck_ref[0]], axis=1)
            v = v_ref[...]
            nq = max(1, t // 256)
            wq = t // nq
            chains = [(hh, j) for hh in range(2) for j in range(nq)]
            scores = []
            for hh, j in chains:
                qs = q[j * wq:(j + 1) * wq]
                ones = jnp.where((lane >= 3 * hh) & (lane < 3 * hh + 3), 1.0, 0.0).astype(q.dtype)
                q_aug = jnp.concatenate([jnp.where(masks[hh], qs, jnp.zeros_like(qs)),
                                         jnp.broadcast_to(ones, qs.shape)], axis=1)
                scores.append(lax.dot_general(k_aug, q_aug, _DIMS["nt"], preferred_element_type=F32))
            probs = []
            for (hh, j), sc in zip(chains, scores):
                cols = slice(j * wq, (j + 1) * wq)
                if diag:
                    r = lax.broadcasted_iota(jnp.int32, (t, wq), 0)
                    cc = lax.broadcasted_iota(jnp.int32, (t, wq), 1) + j * wq
                    sc = jnp.where(r <= cc, sc, _NEG)
                m_prev = m_ref[hh, :, cols]
                m_new = jnp.maximum(m_prev, jnp.max(sc, axis=0, keepdims=True))
                alpha = jnp.exp(m_prev - m_new)
                p = jnp.exp(sc - m_new)
                l_ref[hh, :, cols] = alpha * l_ref[hh, :, cols] + jnp.sum(p, axis=0, keepdims=True)
                m_ref[hh, :, cols] = m_new
                p_hi = p.astype(MXU_DTYPE)
                p_lo = (p - p_hi.astype(F32)).astype(MXU_DTYPE)
                probs.append((alpha, p_hi, p_lo))
            for (hh, j), (alpha, p_hi, p_lo) in zip(chains, probs):
                pv = (lax.dot_general(v, p_hi, _DIMS["tn"], preferred_element_type=F32)
                      + lax.dot_general(v, p_lo, _DIMS["tn"], preferred_element_type=F32))
                rows = slice(hh * HEAD_DIM, (hh + 1) * HEAD_DIM)
                cols = slice(j * wq, (j + 1) * wq)
                acc_ref[rows, cols] = alpha * acc_ref[rows, cols] + pv[rows]

        @pl.when(ki < qi)
        def _():
            step(False)

        @pl.when(ki == qi)
        def _():
            step(True)
            inv = jnp.concatenate([jnp.broadcast_to(1.0 / l_ref[hh], (HEAD_DIM, t)) for hh in range(2)], axis=0)
            out = (acc_ref[...] * inv).T
            o_ref[...] = out.astype(o_ref.dtype)
            of_ref[...] = out
            lse = jnp.concatenate([jnp.broadcast_to(m_ref[hh] + jnp.log(l_ref[hh]), (HEAD_DIM, t))
                                   for hh in range(2)], axis=0)
            lse_ref[...] = lse.T

    grid_spec = pltpu.PrefetchScalarGridSpec(
        num_scalar_prefetch=2, grid=(N_HEADS // 2, len(pairs)),
        in_specs=[
            pl.BlockSpec((t, LANES), lambda p, i, qt, kt: (qt[i], qb + p)),
            pl.BlockSpec((t, LANES), lambda p, i, qt, kt: (kt[i], kb + p)),
            pl.BlockSpec((t, LANES), lambda p, i, qt, kt: (kt[i], vb + p)),
            pl.BlockSpec((1, t, LANES), lambda p, i, qt, kt: (p, kt[i], 0)),
        ],
        out_specs=[pl.BlockSpec((t, LANES), lambda p, i, qt, kt: (qt[i], p))] * 3,
        scratch_shapes=[pltpu.VMEM((2, 1, t), F32), pltpu.VMEM((2, 1, t), F32), pltpu.VMEM((LANES, t), F32)])
    return pl.pallas_call(
        body, name=name, grid_spec=grid_spec,
        out_shape=[jax.ShapeDtypeStruct((s, D_ATT), BF16), jax.ShapeDtypeStruct((s, D_ATT), F32),
                   jax.ShapeDtypeStruct((s, D_ATT), F32)],
        compiler_params=_params(("parallel", "arbitrary")),
    )(qi_tab, ki_tab, h, h, h, ck)


def _attn_bwd(h, ck, o, lse, do, name):
    s = h.shape[0]
    t = _tile(s, 512)
    n = s // t
    qb, kb, vb = OFF_Q // LANES, OFF_K // LANES, OFF_V // LANES

    pairs = [(ki, qi) for ki in range(n) for qi in range(ki, n)]
    ki_tab = jnp.asarray([ki for ki, _ in pairs], jnp.int32)
    qi_tab = jnp.asarray([qi for _, qi in pairs], jnp.int32)

    def body(ki_ref, qi_ref, q_ref, k_ref, v_ref, ck_ref, o_ref, lse_ref, do_ref,
             dq_ref, dk_ref, dv_ref, dc0_ref, dc1_ref, dk_acc, dv_acc, dc_acc):
        ki, qi = ki_ref[pl.program_id(1)], qi_ref[pl.program_id(1)]
        masks = _head_masks()
        lane = lax.broadcasted_iota(jnp.int32, (1, LANES), 1)

        @pl.when((ki == 0) & (qi == 0))
        def _():
            dq_ref[...] = jnp.zeros_like(dq_ref)

        @pl.when(qi == ki)
        def _():
            dk_acc[...] = jnp.zeros_like(dk_acc)
            dv_acc[...] = jnp.zeros_like(dv_acc)
            dc_acc[...] = jnp.zeros_like(dc_acc)

        def step(diag):
            q = q_ref[...] * _SCALE
            k = k_ref[...]
            v = v_ref[...]
            dov = do_ref[...]
            k_aug = jnp.concatenate([k, ck_ref[0]], axis=1)
            prod_t = (dov.astype(F32) * o_ref[...]).T
            lse_t = lse_ref[...].T
            heads = []
            for hh in range(2):
                mk = masks[hh]
                qh = jnp.where(mk, q, jnp.zeros_like(q))
                kh = jnp.where(mk, k, jnp.zeros_like(k))
                doh = jnp.where(mk, dov, jnp.zeros_like(dov))
                ones = jnp.where((lane >= 3 * hh) & (lane < 3 * hh + 3), 1.0, 0.0).astype(q.dtype)
                q_aug = jnp.concatenate([qh, jnp.broadcast_to(ones, q.shape)], axis=1)
                sc = lax.dot_general(k_aug, q_aug, _DIMS["nt"], preferred_element_type=F32)
                dp = lax.dot_general(v, doh, _DIMS["nt"], preferred_element_type=F32)
                heads.append((qh, kh, doh, sc, dp))
            grads = []
            for hh, (qh, kh, doh, sc, dp) in enumerate(heads):
                rows = slice(hh * HEAD_DIM, (hh + 1) * HEAD_DIM)
                p = jnp.exp(sc - lse_t[hh * HEAD_DIM:hh * HEAD_DIM + 1, :])
                if diag:
                    r = lax.broadcasted_iota(jnp.int32, (t, t), 0)
                    cc = lax.broadcasted_iota(jnp.int32, (t, t), 1)
                    p = jnp.where(r <= cc, p, 0.0)
                delta = jnp.sum(prod_t[rows], axis=0, keepdims=True)
                ds = p * (dp - delta)
                dc_acc[hh] = dc_acc[hh] - jnp.sum(ds, axis=1, keepdims=True)
                grads.append((ds.astype(MXU_DTYPE), p.astype(MXU_DTYPE)))
            dq_blk = jnp.zeros((t, LANES), F32)
            for (qh, kh, doh, _, _), (dsb, pb) in zip(heads, grads):
                dv_acc[...] += lax.dot_general(pb, doh, _DIMS["nn"], preferred_element_type=F32)
                dk_acc[...] += lax.dot_general(dsb, qh, _DIMS["nn"], preferred_element_type=F32)
                dq_blk = dq_blk + lax.dot_general(dsb, kh, _DIMS["tn"], preferred_element_type=F32)
            rows_q = pl.ds(pl.multiple_of(qi * t, t), t)
            dq_ref[rows_q, :] = dq_ref[rows_q, :] + dq_blk * _SCALE

        @pl.when(qi > ki)
        def _():
            step(False)

        @pl.when(qi == ki)
        def _():
            step(True)

        @pl.when(qi == n - 1)
        def _():
            dk_ref[...] = dk_acc[...].astype(dk_ref.dtype)
            dv_ref[...] = dv_acc[...].astype(dv_ref.dtype)
            dc0_ref[0] = jnp.broadcast_to(dc_acc[0], (t, LANES)).T[0:8]
            dc1_ref[0] = jnp.broadcast_to(dc_acc[1], (t, LANES)).T[0:8]

    q_blk = lambda col: pl.BlockSpec((t, LANES), lambda p, i, kt, qt: (qt[i], col(p)))
    k_blk = lambda col: pl.BlockSpec((t, LANES), lambda p, i, kt, qt: (kt[i], col(p)))
    dc_blk = pl.BlockSpec((1, 8, t), lambda p, i, kt, qt: (p, 0, kt[i]))
    grid_spec = pltpu.PrefetchScalarGridSpec(
        num_scalar_prefetch=2, grid=(N_HEADS // 2, len(pairs)),
        in_specs=[q_blk(lambda p: qb + p), k_blk(lambda p: kb + p), k_blk(lambda p: vb + p),
                  pl.BlockSpec((1, t, LANES), lambda p, i, kt, qt: (p, kt[i], 0)),
                  q_blk(lambda p: p), q_blk(lambda p: p), q_blk(lambda p: p)],
        out_specs=[pl.BlockSpec((s, LANES), lambda p, i, kt, qt: (0, p)), k_blk(lambda p: p), k_blk(lambda p: p),
                   dc_blk, dc_blk],
        scratch_shapes=[pltpu.VMEM((t, LANES), F32), pltpu.VMEM((t, LANES), F32), pltpu.VMEM((2, t, 1), F32)])
    return pl.pallas_call(
        body, name=name, grid_spec=grid_spec,
        out_shape=[jax.ShapeDtypeStruct((s, D_ATT), F32), jax.ShapeDtypeStruct((s, D_ATT), BF16),
                   jax.ShapeDtypeStruct((s, D_ATT), BF16), jax.ShapeDtypeStruct((N_HEADS // 2, 8, s), F32),
                   jax.ShapeDtypeStruct((N_HEADS // 2, 8, s), F32)],
        compiler_params=_params(("parallel", "arbitrary")),
    )(ki_tab, qi_tab, h, h, h, ck, o, lse, do)


def _conv3(z, z_prev, w_ref):
    return (w_ref[2:3, :] * z + w_ref[1:2, :] * _shift_down(z, z_prev, 1)
            + w_ref[0:1, :] * _shift_down(z, z_prev, 2))


def _sconv_fwd(h, w, name):
    s = h.shape[0]
    t = _tile(s, 512)
    r = t // HALO
    c = D_CONV
    b_bg, b_cg, b_hc = OFF_BG // c, OFF_CG // c, OFF_HC // c

    def body(bg_ref, cg_ref, hc_ref, cgp_ref, hcp_ref, w_ref, y_ref):
        i = pl.program_id(0)
        live = (i > 0).astype(F32)
        z = cg_ref[...].astype(F32) * hc_ref[...].astype(F32)
        zp = cgp_ref[...].astype(F32) * hcp_ref[...].astype(F32) * live
        y_ref[...] = (bg_ref[...].astype(F32) * _conv3(z, zp, w_ref)).astype(y_ref.dtype)

    cur = lambda b: pl.BlockSpec((t, c), lambda i: (i, b))
    prev = lambda b: pl.BlockSpec((HALO, c), lambda i: (jnp.maximum(i * r - 1, 0), b))
    return pl.pallas_call(
        body, name=name, grid=(s // t,),
        in_specs=[cur(b_bg), cur(b_cg), cur(b_hc), prev(b_cg), prev(b_hc), pl.BlockSpec((8, c), lambda i: (0, 0))],
        out_specs=pl.BlockSpec((t, c), lambda i: (i, 0)),
        out_shape=jax.ShapeDtypeStruct((s, c), BF16),
        compiler_params=_params(("parallel",)),
    )(h, h, h, h, h, w)


def _sconv_bwd(h, w, dy, name):
    s = h.shape[0]
    t = _tile(s, 512)
    n = s // t
    r = t // HALO
    nh = s // HALO
    c = D_CONV
    b_bg, b_cg, b_hc = OFF_BG // c, OFF_CG // c, OFF_HC // c

    def body(bg_ref, cg_ref, hc_ref, cgp_ref, hcp_ref, bgn_ref, dy_ref, dyn_ref, w_ref, d_ref, dw_ref, acc_ref):
        i = pl.program_id(0)
        has_prev = (i > 0).astype(F32)
        has_next = (i < n - 1).astype(F32)
        bg = bg_ref[...].astype(F32)
        cg = cg_ref[...].astype(F32)
        hc = hc_ref[...].astype(F32)
        dyv = dy_ref[...].astype(F32)
        z = cg * hc
        zp = cgp_ref[...].astype(F32) * hcp_ref[...].astype(F32) * has_prev
        z1 = _shift_down(z, zp, 1)
        z2 = _shift_down(z, zp, 2)
        cz = w_ref[2:3, :] * z + w_ref[1:2, :] * z1 + w_ref[0:1, :] * z2
        dcz = dyv * bg
        dczn = dyn_ref[...].astype(F32) * bgn_ref[...].astype(F32) * has_next
        dz = (w_ref[2:3, :] * dcz + w_ref[1:2, :] * _shift_up(dcz, dczn, 1)
              + w_ref[0:1, :] * _shift_up(dcz, dczn, 2))
        d_ref[:, 0:c] = (dyv * cz).astype(d_ref.dtype)
        d_ref[:, c:2 * c] = (dz * hc).astype(d_ref.dtype)
        d_ref[:, 2 * c:3 * c] = (dz * cg).astype(d_ref.dtype)

        @pl.when(i == 0)
        def _():
            acc_ref[...] = jnp.zeros_like(acc_ref)

        acc_ref[0] += _row_sum8(dcz * z2)
        acc_ref[1] += _row_sum8(dcz * z1)
        acc_ref[2] += _row_sum8(dcz * z)

        @pl.when(i == n - 1)
        def _():
            rows = [jnp.sum(acc_ref[k], axis=0, keepdims=True) for k in range(3)]
            dw_ref[...] = jnp.concatenate(rows + [jnp.zeros((5, c), F32)], axis=0)

    cur = lambda b: pl.BlockSpec((t, c), lambda i: (i, b))
    prev = lambda b: pl.BlockSpec((HALO, c), lambda i: (jnp.maximum(i * r - 1, 0), b))
    nxt = lambda b: pl.BlockSpec((HALO, c), lambda i: (jnp.minimum((i + 1) * r, nh - 1), b))
    return pl.pallas_call(
        body, name=name, grid=(n,),
        in_specs=[cur(b_bg), cur(b_cg), cur(b_hc), prev(b_cg), prev(b_hc), nxt(b_bg),
                  cur(0), nxt(0), pl.BlockSpec((8, c), lambda i: (0, 0))],
        out_specs=[pl.BlockSpec((t, 3 * c), lambda i: (i, 0)), pl.BlockSpec((8, c), lambda i: (0, 0))],
        out_shape=[jax.ShapeDtypeStruct((s, 3 * c), BF16), jax.ShapeDtypeStruct((8, c), F32)],
        scratch_shapes=[pltpu.VMEM((3, 8, c), F32)],
        compiler_params=_params(("arbitrary",)),
    )(h, h, h, h, h, h, dy, dy, w)


def _group_masks():
    lane = lax.broadcasted_iota(jnp.int32, (1, D_SGU), 1)
    return [(lane >= g * HEAD_DIM) & (lane < (g + 1) * HEAD_DIM) for g in range(N_GROUPS)]


def _tril_weights(w_ref):
    r = lax.broadcasted_iota(jnp.int32, (CHUNK, CHUNK), 0)
    c = lax.broadcasted_iota(jnp.int32, (CHUNK, CHUNK), 1)
    return [jnp.where(r >= c, w_ref[g], 0.0).astype(MXU_DTYPE) for g in range(N_GROUPS)]


def _sgu_ln(vs, g_ref, b_ref):
    vg, dvg = _gelu_and_grad(vs)
    mu = jnp.mean(vg, axis=-1, keepdims=True)
    xc = vg - mu
    rstd = lax.rsqrt(jnp.mean(xc * xc, axis=-1, keepdims=True) + LN_EPS)
    xhat = xc * rstd
    return xhat * g_ref[...] + b_ref[...], xhat, rstd, dvg


def _sgu_fwd(h, ln_g, ln_b, w_s, bias, name):
    s = h.shape[0]
    t = _tile(s, 512)
    c = D_SGU
    b_u, b_v = OFF_U // c, OFF_VS // c

    def body(u_ref, v_ref, g_ref, b_ref, w_ref, bias_ref, y_ref):
        gm = _group_masks()
        wm = _tril_weights(w_ref)
        ug = _gelu(u_ref[...].astype(F32))
        vn, _, _, _ = _sgu_ln(v_ref[...].astype(F32), g_ref, b_ref)
        vnb = vn.astype(MXU_DTYPE)
        for ch in range(t // CHUNK):
            rows = slice(ch * CHUNK, (ch + 1) * CHUNK)
            mixed = bias_ref[...]
            for g in range(N_GROUPS):
                mg = lax.dot_general(wm[g], vnb[rows], _DIMS["nn"], preferred_element_type=F32)
                mixed = jnp.where(gm[g], mixed + mg, mixed)
            y_ref[rows, :] = (ug[rows] * mixed).astype(y_ref.dtype)

    full = lambda shp: pl.BlockSpec(shp, lambda i: (0,) * len(shp))
    return pl.pallas_call(
        body, name=name, grid=(s // t,),
        in_specs=[pl.BlockSpec((t, c), lambda i: (i, b_u)), pl.BlockSpec((t, c), lambda i: (i, b_v)),
                  full((1, c)), full((1, c)), full((N_GROUPS, CHUNK, CHUNK)), full((CHUNK, c))],
        out_specs=pl.BlockSpec((t, c), lambda i: (i, 0)),
        out_shape=jax.ShapeDtypeStruct((s, c), BF16),
        compiler_params=_params(("parallel",)),
    )(h, h, ln_g, ln_b, w_s, bias)


def _sgu_bwd(h, ln_g, ln_b, w_s, bias, dy, name):
    s = h.shape[0]
    t = _tile(s, 512)
    n = s // t
    c = D_SGU
    b_u, b_v = OFF_U // c, OFF_VS // c

    def body(u_ref, v_ref, g_ref, b_ref, w_ref, bias_ref, dy_ref,
             d_ref, dg_ref, db_ref, dw_ref, dbias_ref, dg_acc, db_acc):
        i = pl.program_id(0)
        gm = _group_masks()
        wm = _tril_weights(w_ref)

        @pl.when(i == 0)
        def _():
            dg_acc[...] = jnp.zeros_like(dg_acc)
            db_acc[...] = jnp.zeros_like(db_acc)
            dw_ref[...] = jnp.zeros_like(dw_ref)
            dbias_ref[...] = jnp.zeros_like(dbias_ref)

        ug, dug = _gelu_and_grad(u_ref[...].astype(F32))
        vn, xhat, rstd, dvg = _sgu_ln(v_ref[...].astype(F32), g_ref, b_ref)
        vnb = vn.astype(MXU_DTYPE)
        dyv = dy_ref[...].astype(F32)
        dmixed = dyv * ug
        dmb = dmixed.astype(MXU_DTYPE)
        dvn_parts = []
        for ch in range(t // CHUNK):
            rows = slice(ch * CHUNK, (ch + 1) * CHUNK)
            mixed = bias_ref[...]
            dvn = jnp.zeros((CHUNK, c), F32)
            for g in range(N_GROUPS):
                mg = lax.dot_general(wm[g], vnb[rows], _DIMS["nn"], preferred_element_type=F32)
                mixed = jnp.where(gm[g], mixed + mg, mixed)
                dvn = jnp.where(gm[g], lax.dot_general(wm[g], dmb[rows], _DIMS["tn"], preferred_element_type=F32),
                                dvn)
                dmg = jnp.where(gm[g], dmb[rows], jnp.zeros_like(dmb[rows]))
                dw_ref[g] += lax.dot_general(dmg, vnb[rows], _DIMS["nt"], preferred_element_type=F32)
            d_ref[rows, 0:c] = (dyv[rows] * mixed * dug[rows]).astype(d_ref.dtype)
            dbias_ref[...] += dmixed[rows]
            dvn_parts.append(dvn)
        dvn = jnp.concatenate(dvn_parts, axis=0)
        dg_acc[...] += _row_sum8(dvn * xhat)
        db_acc[...] += _row_sum8(dvn)
        dxh = dvn * g_ref[...]
        dvgl = rstd * (dxh - jnp.mean(dxh, axis=-1, keepdims=True)
                       - xhat * jnp.mean(dxh * xhat, axis=-1, keepdims=True))
        d_ref[:, c:2 * c] = (dvgl * dvg).astype(d_ref.dtype)

        @pl.when(i == n - 1)
        def _():
            dg_ref[...] = jnp.sum(dg_acc[...], axis=0, keepdims=True)
            db_ref[...] = jnp.sum(db_acc[...], axis=0, keepdims=True)
            r = lax.broadcasted_iota(jnp.int32, (CHUNK, CHUNK), 0)
            cc = lax.broadcasted_iota(jnp.int32, (CHUNK, CHUNK), 1)
            for g in range(N_GROUPS):
                dw_ref[g] = jnp.where(r >= cc, dw_ref[g], 0.0)

    full = lambda shp: pl.BlockSpec(shp, lambda i: (0,) * len(shp))
    return pl.pallas_call(
        body, name=name, grid=(n,),
        in_specs=[pl.BlockSpec((t, c), lambda i: (i, b_u)), pl.BlockSpec((t, c), lambda i: (i, b_v)),
                  full((1, c)), full((1, c)), full((N_GROUPS, CHUNK, CHUNK)), full((CHUNK, c)),
                  pl.BlockSpec((t, c), lambda i: (i, 0))],
        out_specs=[pl.BlockSpec((t, 2 * c), lambda i: (i, 0)), full((1, c)), full((1, c)),
                   full((N_GROUPS, CHUNK, CHUNK)), full((CHUNK, c))],
        out_shape=[jax.ShapeDtypeStruct((s, 2 * c), BF16), jax.ShapeDtypeStruct((1, c), F32),
                   jax.ShapeDtypeStruct((1, c), F32), jax.ShapeDtypeStruct((N_GROUPS, CHUNK, CHUNK), F32),
                   jax.ShapeDtypeStruct((CHUNK, c), F32)],
        scratch_shapes=[pltpu.VMEM((8, c), F32), pltpu.VMEM((8, c), F32)],
        compiler_params=_params(("arbitrary",)),
    )(h, h, ln_g, ln_b, w_s, bias, dy)


def _merge_fwd(h, acts, ws, b_gate, name):
    s = h.shape[0]
    d = D_MODEL
    t = _tile(s, 512)

    def body(gl0, gl1, gl2, a0, a1, a2, w0, w1, w2, b_ref, o_ref):
        acc = jnp.zeros((t, d), F32)
        for i, (gl, a, w) in enumerate(((gl0, a0, w0), (gl1, a1, w1), (gl2, a2, w2))):
            y = lax.dot_general(a[...], w[...], _DIMS["nn"], preferred_element_type=F32)
            acc = acc + _sigmoid(gl[...].astype(F32) + b_ref[i:i + 1, :]) * y
        o_ref[...] = acc.astype(o_ref.dtype)

    full = lambda arr: pl.BlockSpec(arr.shape, lambda i: (0, 0))
    return pl.pallas_call(
        body, name=name, grid=(s // t,),
        in_specs=[pl.BlockSpec((t, d), lambda i, b=b: (i, b)) for b in range(3)]
                 + [pl.BlockSpec((t, a.shape[1]), lambda i: (i, 0)) for a in acts]
                 + [full(w) for w in ws] + [full(b_gate)],
        out_specs=pl.BlockSpec((t, d), lambda i: (i, 0)),
        out_shape=jax.ShapeDtypeStruct((s, d), BF16),
        compiler_params=_params(("parallel",)),
    )(h, h, h, *acts, *ws, b_gate)


def _merge_bwd(h, acts, ws, b_gate, dmerged, name):
    s = h.shape[0]
    d = D_MODEL
    t = _tile(s, 512)
    n = s // t

    def body(gl0, gl1, gl2, a0, a1, a2, w0, w1, w2, b_ref, dm_ref, dy0, dy1, dy2, dgl_ref, db_ref, acc_ref):
        step = pl.program_id(0)

        @pl.when(step == 0)
        def _():
            acc_ref[...] = jnp.zeros_like(acc_ref)

        dm = dm_ref[...]
        for i, (gl, a, w, dy) in enumerate(((gl0, a0, w0, dy0), (gl1, a1, w1, dy1), (gl2, a2, w2, dy2))):
            y = lax.dot_general(a[...], w[...], _DIMS["nn"], preferred_element_type=F32)
            gate = _sigmoid(gl[...].astype(F32) + b_ref[i:i + 1, :])
            dy[...] = (dm * gate).astype(dy.dtype)
            dgl = dm * y * (gate * (1.0 - gate))
            dgl_ref[:, i * d:(i + 1) * d] = dgl.astype(dgl_ref.dtype)
            acc_ref[i] += _row_sum8(dgl)

        @pl.when(step == n - 1)
        def _():
            rows = [jnp.sum(acc_ref[k], axis=0, keepdims=True) for k in range(3)]
            db_ref[...] = jnp.concatenate(rows + [jnp.zeros((5, d), F32)], axis=0)

    full = lambda arr: pl.BlockSpec(arr.shape, lambda i: (0, 0))
    row = pl.BlockSpec((t, d), lambda i: (i, 0))
    return pl.pallas_call(
        body, name=name, grid=(n,),
        in_specs=[pl.BlockSpec((t, d), lambda i, b=b: (i, b)) for b in range(3)]
                 + [pl.BlockSpec((t, a.shape[1]), lambda i: (i, 0)) for a in acts]
                 + [full(w) for w in ws] + [full(b_gate), row],
        out_specs=[row, row, row, pl.BlockSpec((t, 3 * d), lambda i: (i, 0)), pl.BlockSpec((8, d), lambda i: (0, 0))],
        out_shape=[jax.ShapeDtypeStruct((s, d), BF16)] * 3
                  + [jax.ShapeDtypeStruct((s, IN_PAD), BF16), jax.ShapeDtypeStruct((8, d), F32)],
        scratch_shapes=[pltpu.VMEM((3, 8, d), F32)],
        compiler_params=_params(("arbitrary",)),
    )(h, h, h, *acts, *ws, b_gate, dmerged)


FF_BLK = D_FF // 2


def _ffn_act_fwd(h2, w, name):
    s = h2.shape[0]
    t = _tile(s, 512)
    r = t // HALO
    cw = 2 * FF_BLK

    def body(x_ref, xp_ref, w_ref, p_ref):
        i = pl.program_id(0)
        live = (i > 0).astype(F32)
        hc = _conv3(x_ref[...].astype(F32), xp_ref[...].astype(F32) * live, w_ref)
        p_ref[...] = (_gelu(hc[:, :FF_BLK]) * hc[:, FF_BLK:]).astype(p_ref.dtype)

    return pl.pallas_call(
        body, name=name, grid=(s // t, 2),
        in_specs=[pl.BlockSpec((t, cw), lambda i, j: (i, j)),
                  pl.BlockSpec((HALO, cw), lambda i, j: (jnp.maximum(i * r - 1, 0), j)),
                  pl.BlockSpec((8, cw), lambda i, j: (0, j))],
        out_specs=pl.BlockSpec((t, FF_BLK), lambda i, j: (i, j)),
        out_shape=jax.ShapeDtypeStruct((s, D_FF), BF16),
        compiler_params=_params(("parallel", "parallel")),
    )(h2, h2, w)


def _ffn_act_conv_bwd(h2, w, dp, name):
    s = h2.shape[0]
    t = _tile(s, 512)
    n = s // t
    r = t // HALO
    nh = s // HALO
    cw = 2 * FF_BLK

    def body(x_ref, xp_ref, xn_ref, dp_ref, dpn_ref, w_ref, dx_ref, dw_ref, acc_ref):
        i = pl.program_id(1)
        has_prev = (i > 0).astype(F32)
        has_next = (i < n - 1).astype(F32)
        x = jnp.concatenate([x_ref[...].astype(F32), xn_ref[...].astype(F32)], axis=0)
        xp = xp_ref[...].astype(F32) * has_prev
        x1 = _shift_down(x, xp, 1)
        x2 = _shift_down(x, xp, 2)
        hc = w_ref[2:3, :] * x + w_ref[1:2, :] * x1 + w_ref[0:1, :] * x2
        ga, dga = _gelu_and_grad(hc[:, :FF_BLK])
        dpv = jnp.concatenate([dp_ref[...].astype(F32), dpn_ref[...].astype(F32) * has_next], axis=0)
        dhc = jnp.concatenate([dpv * hc[:, FF_BLK:] * dga, dpv * ga], axis=1)
        cur, nxt = dhc[:t], dhc[t:]
        dx = w_ref[2:3, :] * cur + w_ref[1:2, :] * _shift_up(cur, nxt, 1) + w_ref[0:1, :] * _shift_up(cur, nxt, 2)
        dx_ref[...] = dx.astype(dx_ref.dtype)

        @pl.when(i == 0)
        def _():
            acc_ref[...] = jnp.zeros_like(acc_ref)

        acc_ref[0] += _row_sum8(cur * x2[:t])
        acc_ref[1] += _row_sum8(cur * x1[:t])
        acc_ref[2] += _row_sum8(cur * x[:t])

        @pl.when(i == n - 1)
        def _():
            rows = [jnp.sum(acc_ref[k], axis=0, keepdims=True) for k in range(3)]
            dw_ref[...] = jnp.concatenate(rows + [jnp.zeros((5, cw), F32)], axis=0)

    nxt_row = lambda j, i: jnp.minimum((i + 1) * r, nh - 1)
    return pl.pallas_call(
        body, name=name, grid=(2, n),
        in_specs=[pl.BlockSpec((t, cw), lambda j, i: (i, j)),
                  pl.BlockSpec((HALO, cw), lambda j, i: (jnp.maximum(i * r - 1, 0), j)),
                  pl.BlockSpec((HALO, cw), lambda j, i: (nxt_row(j, i), j)),
                  pl.BlockSpec((t, FF_BLK), lambda j, i: (i, j)),
                  pl.BlockSpec((HALO, FF_BLK), lambda j, i: (nxt_row(j, i), j)),
                  pl.BlockSpec((8, cw), lambda j, i: (0, j))],
        out_specs=[pl.BlockSpec((t, cw), lambda j, i: (i, j)), pl.BlockSpec((8, cw), lambda j, i: (0, j))],
        out_shape=[jax.ShapeDtypeStruct((s, 2 * D_FF), BF16), jax.ShapeDtypeStruct((8, 2 * D_FF), F32)],
        scratch_shapes=[pltpu.VMEM((3, 8, cw), F32)],
        compiler_params=_params(("parallel", "arbitrary")),
    )(h2, h2, h2, dp, dp, w)


def _adamw(w, g, m, v, name, dep=None):
    shape = w.shape
    c = shape[-1]
    rows = math.prod(shape[:-1])
    to2d = lambda a: a.reshape(rows, c)
    cap = max(8, (1 << 18) // c)
    tr = rows
    for cand in (2048, 1024, 512, 256, 128, 64, 32, 16, 8):
        if cand <= cap and rows % cand == 0:
            tr = cand
            break

    deps = [] if dep is None else [dep]

    def body(w_ref, g_ref, m_ref, v_ref, *rest):
        d_ref, nm_ref, nv_ref = rest[len(deps):]
        gv = g_ref[...]
        nm = ADAM_B1 * m_ref[...] + (1.0 - ADAM_B1) * gv
        nv = ADAM_B2 * v_ref[...] + (1.0 - ADAM_B2) * (gv * gv)
        m_hat = nm / (1.0 - ADAM_B1 ** ADAM_STEP)
        v_hat = nv / (1.0 - ADAM_B2 ** ADAM_STEP)
        d_ref[...] = -ADAM_LR * (m_hat / (jnp.sqrt(v_hat) + ADAM_EPS) + ADAM_WD * w_ref[...])
        nm_ref[...] = nm
        nv_ref[...] = nv

    blk = pl.BlockSpec((tr, c), lambda i: (i, 0))
    outs = pl.pallas_call(
        body, name=name, grid=(rows // tr,),
        in_specs=[blk] * 4 + [_DEP] * len(deps), out_specs=[blk] * 3,
        out_shape=[jax.ShapeDtypeStruct((rows, c), F32)] * 3,
        compiler_params=_params(("parallel",)),
    )(to2d(w), to2d(g), to2d(m), to2d(v), *deps)
    return tuple(o.reshape(shape) for o in outs)


_ANY = pl.BlockSpec(memory_space=pl.ANY)


def _place():
    x, y, c = lax.axis_index("x"), lax.axis_index("y"), lax.axis_index("c")
    others = [(1 - x, y), (x, 1 - y), (1 - x, 1 - y)]
    return x, y, c, others


def _all_gather_chips(shard, name):
    rws, cols = shard.shape
    half = rws // 2

    def body(x_ref, out_ref, send_sems, recv_sems, local_sem):
        x, y, c, others = _place()
        me = 2 * x + y
        sib = (x, y, 1 - c)

        def rows(chip, cc):
            return out_ref.at[chip, pl.ds(pl.multiple_of(cc * half, 16), half), :]

        def copy(k, src, dst, to):
            return pltpu.make_async_remote_copy(src_ref=src, dst_ref=dst, send_sem=send_sems.at[k],
                                                recv_sem=recv_sems.at[k], device_id=to, device_id_type=MESH)

        mine = pltpu.make_async_copy(x_ref, out_ref.at[me], local_sem)
        mine.start()
        my_half = x_ref.at[pl.ds(pl.multiple_of(c * half, 16), half), :]
        first = [copy(j, my_half, rows(me, c), (ox, oy, c)) for j, (ox, oy) in enumerate(others)]
        for cp in first:
            cp.start()
        passed = []
        for j, (ox, oy) in enumerate(others):
            blk = rows(2 * ox + oy, c)
            copy(j, blk, blk, (x, y, c)).wait_recv()
            fwd = copy(3 + j, blk, blk, sib)
            fwd.start()
            passed.append(fwd)
        for j, (ox, oy) in enumerate(others):
            blk = rows(2 * ox + oy, 1 - c)
            copy(3 + j, blk, blk, (x, y, c)).wait_recv()
        for cp in first + passed:
            cp.wait_send()
        mine.wait()

    return pl.pallas_call(
        body, name=name,
        in_specs=[_ANY], out_specs=_ANY,
        out_shape=jax.ShapeDtypeStruct((N_CHIPS, rws, cols), shard.dtype),
        scratch_shapes=[pltpu.SemaphoreType.DMA((6,)), pltpu.SemaphoreType.DMA((6,)), pltpu.SemaphoreType.DMA],
        compiler_params=pltpu.CompilerParams(has_side_effects=True),
    )(shard)


def _swap_halves(buf, name, dep=None):
    nb, rws, cols = buf.shape
    half = rws // 2
    deps = [] if dep is None else [dep]

    def body(b_ref, *rest):
        own_ref, sib_ref, send_sem, recv_sem, local_sem = rest[len(deps):]
        x, y, c, _ = _place()
        keep = b_ref.at[:, pl.ds(pl.multiple_of(c * half, 16), half), :]
        give = b_ref.at[:, pl.ds(pl.multiple_of((1 - c) * half, 16), half), :]
        mine = pltpu.make_async_copy(keep, own_ref, local_sem)
        mine.start()
        cp = pltpu.make_async_remote_copy(src_ref=give, dst_ref=sib_ref, send_sem=send_sem, recv_sem=recv_sem,
                                          device_id=(x, y, 1 - c), device_id_type=MESH)
        cp.start()
        cp.wait()
        mine.wait()

    shp = jax.ShapeDtypeStruct((nb, half, cols), buf.dtype)
    return pl.pallas_call(
        body, name=name,
        in_specs=[_ANY] * (1 + len(deps)), out_specs=[_ANY, _ANY], out_shape=[shp, shp],
        scratch_shapes=[pltpu.SemaphoreType.DMA, pltpu.SemaphoreType.DMA, pltpu.SemaphoreType.DMA],
        compiler_params=pltpu.CompilerParams(has_side_effects=True),
    )(buf, *deps)


def _add2(a, b, name):
    nb, rws, cols = a.shape
    t = _tile(rws, 256)
    if rws % t:
        t = rws

    def body(a_ref, b_ref, o_ref):
        o_ref[...] = (a_ref[...].astype(F32) + b_ref[...].astype(F32)).astype(o_ref.dtype)

    blk = pl.BlockSpec((1, t, cols), lambda i, j: (i, j, 0))
    return pl.pallas_call(
        body, name=name, grid=(nb, rws // t), in_specs=[blk, blk], out_specs=blk,
        out_shape=jax.ShapeDtypeStruct(a.shape, a.dtype),
        compiler_params=_params(("parallel", "parallel")),
    )(a, b)


def _exchange_chips(pre, name):
    nb, half, cols = pre.shape

    def body(p_ref, out_ref, send_sems, recv_sems, local_sem):
        x, y, c, others = _place()
        me = 2 * x + y
        mine = pltpu.make_async_copy(p_ref.at[me], out_ref.at[me], local_sem)
        mine.start()
        sends = []
        for j, (ox, oy) in enumerate(others):
            cp = pltpu.make_async_remote_copy(src_ref=p_ref.at[2 * ox + oy], dst_ref=out_ref.at[me],
                                              send_sem=send_sems.at[j], recv_sem=recv_sems.at[j],
                                              device_id=(ox, oy, c), device_id_type=MESH)
            cp.start()
            sends.append(cp)
        for j, (ox, oy) in enumerate(others):
            blk = out_ref.at[2 * ox + oy]
            pltpu.make_async_remote_copy(src_ref=blk, dst_ref=blk, send_sem=send_sems.at[j],
                                         recv_sem=recv_sems.at[j], device_id=(x, y, c),
                                         device_id_type=MESH).wait_recv()
        for cp in sends:
            cp.wait_send()
        mine.wait()

    return pl.pallas_call(
        body, name=name,
        in_specs=[_ANY], out_specs=_ANY, out_shape=jax.ShapeDtypeStruct(pre.shape, pre.dtype),
        scratch_shapes=[pltpu.SemaphoreType.DMA((3,)), pltpu.SemaphoreType.DMA((3,)), pltpu.SemaphoreType.DMA],
        compiler_params=pltpu.CompilerParams(has_side_effects=True),
    )(pre)


def _add4(parts, name):
    nb, half, cols = parts.shape
    t = _tile(half, 256)
    if half % t:
        t = half

    def body(p_ref, o_ref):
        acc = p_ref[0].astype(F32)
        for k in range(1, nb):
            acc = acc + p_ref[k].astype(F32)
        o_ref[...] = acc

    return pl.pallas_call(
        body, name=name, grid=(half // t,),
        in_specs=[pl.BlockSpec((nb, t, cols), lambda i: (0, i, 0))],
        out_specs=pl.BlockSpec((t, cols), lambda i: (i, 0)),
        out_shape=jax.ShapeDtypeStruct((half, cols), F32),
        compiler_params=_params(("parallel",)),
    )(parts)


def _join_halves(mine_half, name):
    half, cols = mine_half.shape

    def body(h_ref, out_ref, send_sem, recv_sem, local_sem):
        x, y, c, _ = _place()
        dst = out_ref.at[pl.ds(pl.multiple_of(c * half, 8), half), :]
        mine = pltpu.make_async_copy(h_ref, dst, local_sem)
        mine.start()
        cp = pltpu.make_async_remote_copy(src_ref=h_ref, dst_ref=dst, send_sem=send_sem, recv_sem=recv_sem,
                                          device_id=(x, y, 1 - c), device_id_type=MESH)
        cp.start()
        cp.wait()
        mine.wait()

    return pl.pallas_call(
        body, name=name,
        in_specs=[_ANY], out_specs=_ANY, out_shape=jax.ShapeDtypeStruct((2 * half, cols), mine_half.dtype),
        scratch_shapes=[pltpu.SemaphoreType.DMA, pltpu.SemaphoreType.DMA, pltpu.SemaphoreType.DMA],
        compiler_params=pltpu.CompilerParams(has_side_effects=True),
    )(mine_half)


def _reduce_scatter_chips(buf, tag, dep=None):
    own, sib = _swap_halves(buf, "rs_swap_" + tag, dep)
    pre = _add2(own, sib, "rs_add2_" + tag)
    parts = _exchange_chips(pre, "rs_xchg_" + tag)
    red = _add4(parts, "rs_add4_" + tag)
    return _join_halves(red, "rs_join_" + tag)


MAX_DMA_BYTES = 2 * 1024 * 1024
ROW_ALIGN = 16


def _pieces(rows, row_bytes):
    n = max(1, -(-(rows * row_bytes) // MAX_DMA_BYTES))
    step = -(-(-(-rows // n)) // ROW_ALIGN) * ROW_ALIGN
    return [(r, min(step, rows - r)) for r in range(0, rows, step)]


def _half_plan(arrays, row_axis):
    plan = []
    for a, arr in enumerate(arrays):
        row_bytes = math.prod(arr.shape[row_axis + 1:]) * arr.dtype.itemsize * (arr.shape[0] if row_axis else 1)
        plan += [(a, r0, nr) for r0, nr in _pieces(arr.shape[row_axis] // 2, row_bytes)]
    return plan


def _rows(start, size):
    return pl.ds(pl.multiple_of(start, ROW_ALIGN), size)


def _remote(src, dst, send_sems, recv_sems, k, to):
    return pltpu.make_async_remote_copy(src_ref=src, dst_ref=dst, send_sem=send_sems.at[k], recv_sem=recv_sems.at[k],
                                        device_id=to, device_id_type=MESH)


def _comm_call(body, name, ins, out_shapes, n_remote, n_local, aliases=None):
    return pl.pallas_call(
        body, name=name,
        in_specs=[_ANY] * len(ins), out_specs=[_ANY] * len(out_shapes), out_shape=out_shapes,
        scratch_shapes=[pltpu.SemaphoreType.DMA((n_remote,)), pltpu.SemaphoreType.DMA((n_remote,)),
                        pltpu.SemaphoreType.DMA((max(n_local, 1),))],
        input_output_aliases=aliases or {},
        compiler_params=pltpu.CompilerParams(has_side_effects=True),
    )(*ins)


def _cast_shard(w, l, me_idx, name):
    _, k, cols = w.shape
    tr = _tile(k, 256)
    if k % tr:
        tr = k

    def body(me_ref, w_ref, s_ref, land_ref):
        del me_ref
        v = w_ref[...].astype(BF16)
        s_ref[...] = v
        land_ref[...] = v

    grid_spec = pltpu.PrefetchScalarGridSpec(
        num_scalar_prefetch=1, grid=(k // tr,),
        in_specs=[pl.BlockSpec((None, tr, cols), lambda i, me: (l, i, 0))],
        out_specs=[pl.BlockSpec((tr, cols), lambda i, me: (i, 0)),
                   pl.BlockSpec((None, tr, cols), lambda i, me: (me[0], i, 0))])
    return pl.pallas_call(
        body, name=name, grid_spec=grid_spec,
        out_shape=[jax.ShapeDtypeStruct((k, cols), BF16), jax.ShapeDtypeStruct((N_CHIPS, k, cols), BF16)],
        compiler_params=_params(("parallel",)),
    )(me_idx, w)


def _gather_d2d(lands, name):
    n = len(lands)
    plan = _half_plan(lands, 1)
    plan = [(a, r0, nr) for a, r0, nr in plan]

    def body(*refs):
        out_refs = refs[n:2 * n]
        send_sems, recv_sems, _ = refs[2 * n:]
        x, y, c, others = _place()
        sends = []
        for i, (a, r0, nr) in enumerate(plan):
            rows = _rows(c * (lands[a].shape[1] // 2) + r0, nr)
            for j, (ox, oy) in enumerate(others):
                blk = out_refs[a].at[2 * ox + oy, rows, :]
                cp = _remote(blk, blk, send_sems, recv_sems, 3 * i + j, (x, y, 1 - c))
                cp.start()
                sends.append(cp)
        for i, (a, r0, nr) in enumerate(plan):
            rows = _rows((1 - c) * (lands[a].shape[1] // 2) + r0, nr)
            for j, (ox, oy) in enumerate(others):
                blk = out_refs[a].at[2 * ox + oy, rows, :]
                _remote(blk, blk, send_sems, recv_sems, 3 * i + j, (x, y, c)).wait_recv()
        for cp in sends:
            cp.wait_send()

    outs = [jax.ShapeDtypeStruct(a.shape, a.dtype) for a in lands]
    return _comm_call(body, name, lands, outs, 3 * len(plan), 0, aliases={a: a for a in range(n)})


def _rs_swap(ts, name):
    n = len(ts)
    plan = _half_plan(ts, 1)

    def body(*refs):
        t_refs, out_refs = refs[:n], refs[n:2 * n]
        send_sems, recv_sems, _ = refs[2 * n:]
        x, y, c, _o = _place()
        sends = []
        for i, (a, r0, nr) in enumerate(plan):
            src = t_refs[a].at[:, _rows((1 - c) * (ts[a].shape[1] // 2) + r0, nr), :]
            cp = _remote(src, out_refs[a].at[:, pl.ds(r0, nr), :], send_sems, recv_sems, i, (x, y, 1 - c))
            cp.start()
            sends.append(cp)
        for i, (a, r0, nr) in enumerate(plan):
            blk = out_refs[a].at[:, pl.ds(r0, nr), :]
            _remote(blk, blk, send_sems, recv_sems, i, (x, y, c)).wait_recv()
        for cp in sends:
            cp.wait_send()

    outs = [jax.ShapeDtypeStruct((t.shape[0], t.shape[1] // 2, t.shape[2]), t.dtype) for t in ts]
    return _comm_call(body, name, ts, outs, len(plan), 0)


def _add_halves(ts, gots, c_idx, me_idx, name):
    n = len(ts)

    def body(c_ref, me_ref, *refs):
        del c_ref
        t_refs, g_refs = refs[:n], refs[n:2 * n]
        o_refs, mine_refs = refs[2 * n:3 * n], refs[3 * n:]
        for t_ref, g_ref, o_ref, mine_ref in zip(t_refs, g_refs, o_refs, mine_refs):
            v = (t_ref[...].astype(F32) + g_ref[...].astype(F32)).astype(o_ref.dtype)
            o_ref[...] = v

            @pl.when(pl.program_id(0) == me_ref[0])
            def _():
                mine_ref[...] = v

    blks = [(1, g.shape[1], g.shape[2]) for g in gots]
    same = [pl.BlockSpec(b, lambda i, c, me: (i, 0, 0)) for b in blks]
    grid_spec = pltpu.PrefetchScalarGridSpec(
        num_scalar_prefetch=2, grid=(N_CHIPS,),
        in_specs=[pl.BlockSpec(b, lambda i, c, me: (i, c[0], 0)) for b in blks] + same,
        out_specs=same + [pl.BlockSpec(b, lambda i, c, me: (me[0], 0, 0)) for b in blks])
    shapes = [jax.ShapeDtypeStruct(g.shape, g.dtype) for g in gots]
    outs = pl.pallas_call(
        body, name=name, grid_spec=grid_spec, out_shape=shapes + shapes,
        compiler_params=_params(("arbitrary",)),
    )(c_idx, me_idx, *ts, *gots)
    return outs[:n], outs[n:]


def _add4_halves(parts, c_idx, name):
    n = len(parts)
    steps = 2

    def body(c_ref, *refs):
        del c_ref
        for p_ref, o_ref in zip(refs[:n], refs[n:]):
            acc = p_ref[0].astype(F32)
            for k in range(1, N_CHIPS):
                acc = acc + p_ref[k].astype(F32)
            o_ref[...] = acc

    grid_spec = pltpu.PrefetchScalarGridSpec(
        num_scalar_prefetch=1, grid=(steps,),
        in_specs=[pl.BlockSpec((N_CHIPS, p.shape[1] // steps, p.shape[2]), lambda i, c: (0, i, 0)) for p in parts],
        out_specs=[pl.BlockSpec((p.shape[1] // steps, p.shape[2]), lambda i, c: (c[0] * steps + i, 0))
                   for p in parts])
    return pl.pallas_call(
        body, name=name, grid_spec=grid_spec,
        out_shape=[jax.ShapeDtypeStruct((2 * p.shape[1], p.shape[2]), F32) for p in parts],
        compiler_params=_params(("parallel",)),
    )(c_idx, *parts)


def _rs_join(fulls, name):
    n = len(fulls)
    plan = _half_plan(fulls, 0)

    def body(*refs):
        out_refs = refs[n:2 * n]
        send_sems, recv_sems, _ = refs[2 * n:]
        x, y, c, _o = _place()
        sends = []
        for i, (a, r0, nr) in enumerate(plan):
            blk = out_refs[a].at[_rows(c * (fulls[a].shape[0] // 2) + r0, nr), :]
            cp = _remote(blk, blk, send_sems, recv_sems, i, (x, y, 1 - c))
            cp.start()
            sends.append(cp)
        for i, (a, r0, nr) in enumerate(plan):
            blk = out_refs[a].at[_rows((1 - c) * (fulls[a].shape[0] // 2) + r0, nr), :]
            _remote(blk, blk, send_sems, recv_sems, i, (x, y, c)).wait_recv()
        for cp in sends:
            cp.wait_send()

    outs = [jax.ShapeDtypeStruct(f.shape, f.dtype) for f in fulls]
    return _comm_call(body, name, fulls, outs, len(plan), 0, aliases={a: a for a in range(n)})


_HBM = pl.BlockSpec(memory_space=pltpu.HBM)
_SEM = pl.BlockSpec(memory_space=pltpu.SEMAPHORE)
_EFFECT = pltpu.SideEffectType.DATAFLOW_SIDE_EFFECTING


def _ici_plan(kind, a_list):
    if kind == "gather":
        return _half_plan(a_list, 0)
    plan = []
    for a, p in enumerate(a_list):
        plan += [(a, r0, nr) for r0, nr in _pieces(p.shape[1], p.shape[2] * p.dtype.itemsize)]
    return plan


def _ici_refs(kind, a_ref, b_ref, a_shape, r0, nr, c, me, peer):
    if kind == "gather":
        rows = _rows(c * (a_shape[0] // 2) + r0, nr)
        return a_ref.at[rows, :], b_ref.at[me, rows, :], b_ref.at[peer, rows, :]
    rows = pl.ds(r0, nr)
    return a_ref.at[peer, rows, :], b_ref.at[me, rows, :], b_ref.at[peer, rows, :]


def _ici_start(kind, a_list, b_list, name):
    n = len(a_list)
    plan = _ici_plan(kind, a_list)
    shapes = [a.shape for a in a_list]

    def body(*refs):
        a_refs, b_refs = refs[:n], refs[n:2 * n]
        send_sems, recv_sems = refs[2 * n], refs[2 * n + 1]
        token = refs[4 * n + 2]
        x, y, c, others = _place()
        me = 2 * x + y
        for i, (a, r0, nr) in enumerate(plan):
            for j, (ox, oy) in enumerate(others):
                src, dst, _ = _ici_refs(kind, a_refs[a], b_refs[a], shapes[a], r0, nr, c, me, 2 * ox + oy)
                _remote(src, dst, send_sems, recv_sems, 3 * i + j, (ox, oy, c)).start()
        token[...] = jnp.zeros_like(token)

    hbm = lambda v: pltpu.HBM(v.shape, v.dtype)
    ncp = 3 * len(plan)
    outs = pl.pallas_call(
        body, name=name,
        in_specs=[_HBM] * (2 * n),
        out_specs=[_SEM, _SEM] + [_HBM] * (2 * n) + [pl.BlockSpec(memory_space=pltpu.VMEM)],
        out_shape=[pltpu.SemaphoreType.DMA((ncp,)), pltpu.SemaphoreType.DMA((ncp,))]
                  + [hbm(v) for v in a_list] + [hbm(v) for v in b_list] + [jax.ShapeDtypeStruct((8, LANES), F32)],
        input_output_aliases={i: 2 + i for i in range(2 * n)},
        compiler_params=pltpu.CompilerParams(has_side_effects=_EFFECT),
    )(*[pltpu.with_memory_space_constraint(v, pltpu.HBM) for v in list(a_list) + list(b_list)])
    return outs[0], outs[1], outs[2:2 + n], outs[2 + n:2 + 2 * n], outs[2 + 2 * n]


def _ici_wait(kind, started, after, name):
    send_sems, recv_sems, a_list, b_list, _ = started
    afters = list(after) if isinstance(after, (list, tuple)) else [after]
    n = len(a_list)
    plan = _ici_plan(kind, a_list)
    shapes = [a.shape for a in a_list]

    def body(*refs):
        a_refs, b_refs = refs[:n], refs[n:2 * n]
        send_sems, recv_sems = refs[2 * n], refs[2 * n + 1]
        x, y, c, others = _place()
        me = 2 * x + y
        for i, (a, r0, nr) in enumerate(plan):
            for j, (ox, oy) in enumerate(others):
                src, dst, land = _ici_refs(kind, a_refs[a], b_refs[a], shapes[a], r0, nr, c, me, 2 * ox + oy)
                _remote(src, dst, send_sems, recv_sems, 3 * i + j, (ox, oy, c)).wait_send()
                _remote(land, land, send_sems, recv_sems, 3 * i + j, (x, y, c)).wait_recv()

    hbm = lambda v: pltpu.HBM(v.shape, v.dtype)
    outs = pl.pallas_call(
        body, name=name,
        in_specs=[_HBM] * (2 * n) + [_SEM, _SEM] + [_ANY] * len(afters),
        out_specs=[_HBM] * (2 * n),
        out_shape=[hbm(v) for v in a_list] + [hbm(v) for v in b_list],
        input_output_aliases={i: i for i in range(2 * n)},
        compiler_params=pltpu.CompilerParams(has_side_effects=_EFFECT),
    )(*a_list, *b_list, send_sems, recv_sems, *afters)
    return outs[n:]


def _rs_begin(ts, c_idx, me_idx, tag):
    got = _rs_swap(ts, "rs_swap_" + tag)
    pres, mine = _add_halves(ts, got, c_idx, me_idx, "rs_add2_" + tag)
    return _ici_start("scatter", pres, mine, "rs_xchg_start_" + tag)


def _rs_finish(started, after, c_idx, tag):
    parts = _ici_wait("scatter", started, after, "rs_xchg_wait_" + tag)
    return _rs_join(_add4_halves(parts, c_idx, "rs_add4_" + tag), "rs_join_" + tag)


def _pack_rows(pieces, rows, dtype):
    flat = jnp.concatenate([p.astype(dtype).reshape(-1) for p in pieces])
    return jnp.pad(flat, (0, rows * PACK_COLS - flat.shape[0])).reshape(rows, PACK_COLS)


def _unpack(flat, shapes):
    out, off = [], 0
    for shp in shapes:
        size = math.prod(shp)
        out.append(flat[off:off + size].reshape(shp))
        off += size
    return out


def _rows_for(n_elems, mult):
    rows = -(-n_elems // PACK_COLS)
    return -(-rows // mult) * mult


BIG_SHARDS = [("w_in", (D_MODEL, 1474)), ("w_branch_att", (D_ATT, 256)), ("w_branch_conv", (D_CONV, 256)),
              ("w_branch_sgu", (D_SGU, 256)), ("w_out", (256, D_MODEL)), ("w_ffn_up", (D_MODEL, FF_BLK)),
              ("w_ffn_down", (D_FF // N_CHIPS, D_MODEL))]
SMALL_SHARDS = [("b_gate", (3, 256)), ("conv_mix_w", (3, 64)), ("conv_ffn_w", (3, FF_BLK))]
REPLICATED = [("pre_mix_g", (D_MODEL,)), ("post_mix_g", (D_MODEL,)), ("pre_ffn_g", (D_MODEL,)),
              ("post_ffn_g", (D_MODEL,)), ("b_forget", (N_HEADS,)), ("sgu_ln_g", (D_SGU,)), ("sgu_ln_b", (D_SGU,)),
              ("sgu_w", (N_GROUPS, CHUNK, CHUNK)), ("sgu_b", (N_GROUPS, CHUNK))]
WEIGHT_ORDER = ["pre_mix_g", "post_mix_g", "pre_ffn_g", "post_ffn_g", "w_in", "b_forget", "b_gate", "conv_mix_w",
                "sgu_ln_g", "sgu_ln_b", "sgu_w", "sgu_b", "w_branch_att", "w_branch_conv", "w_branch_sgu", "w_out",
                "w_ffn_up", "conv_ffn_w", "w_ffn_down"]

_SMALL_ELEMS = sum(math.prod(s) for _, s in SMALL_SHARDS)
_REP_ELEMS = sum(math.prod(s) for _, s in REPLICATED)
_REP_QUARTER = -(-(DEPTH * _REP_ELEMS) // N_CHIPS)
SMALL_PARAM_ROWS = _rows_for(DEPTH * _SMALL_ELEMS, 32)
SMALL_ROWS = _rows_for(DEPTH * _SMALL_ELEMS + _REP_QUARTER, 32)
IN_WIDTH = 5896
IN_SHARD = IN_WIDTH // N_CHIPS
IN_SHARD_PAD = 1536
IN_PAD = 6144


def _gather_small(wts):
    shard = _pack_rows([wts[n] for n, _ in SMALL_SHARDS], SMALL_PARAM_ROWS, F32)
    full = _all_gather_chips(shard, "gather_small_params").reshape(N_CHIPS, -1)
    per_chip = [_unpack(full[j], [(DEPTH,) + s for _, s in SMALL_SHARDS]) for j in range(N_CHIPS)]
    return {n: jnp.concatenate([per_chip[j][i] for j in range(N_CHIPS)], axis=-1)
            for i, (n, _) in enumerate(SMALL_SHARDS)}


BIG_NAMES = [n for n, _ in BIG_SHARDS]
FIRST_NAMES = ["w_in"]
LATE_NAMES = BIG_NAMES[1:]


def _gather_begin(wts, l, me_idx, names, tag):
    cast = [_cast_shard(wts[n], l, me_idx, "cast_" + n) for n in names]
    return _ici_start("gather", [sh for sh, _ in cast], [ld for _, ld in cast], "gather_ici_start_" + tag)


def _gather_finish(started, after, names, tag):
    lands = _ici_wait("gather", started, after, "gather_ici_wait_" + tag)
    return dict(zip(names, _gather_d2d(lands, "gather_d2d_" + tag)))


def _pad_rows(a, rows):
    return jnp.pad(a, ((0, rows - a.shape[0]), (0, 0)))


def _whole_cols(land):
    return land.transpose(1, 0, 2).reshape(land.shape[1], -1)


_O_F = 3 * D_ATT
_O_B = _O_F + N_HEADS
_O_GL = _O_B + 3 * D_CONV + 2 * D_SGU


_LOCAL_ORDER = [(_O_GL, IN_WIDTH), (0, _O_F), (_O_B, _O_GL), (_O_F, _O_B)]


def _own_cols(land, lo, hi):
    pieces = []
    for j in range(N_CHIPS):
        a, b = max(lo, j * IN_SHARD), min(hi, (j + 1) * IN_SHARD)
        if a < b:
            pieces.append(land[j][:, a - j * IN_SHARD:b - j * IN_SHARD])
    return pieces


def _local_cols(m, lo, hi):
    pieces, off = [], 0
    for a, b in _LOCAL_ORDER:
        x, y = max(lo, a), min(hi, b)
        if x < y:
            pieces.append((x, m[:, off + x - a:off + y - a]))
        off += b - a
    pieces = [p for _, p in sorted(pieces, key=lambda t: t[0])]
    if hi > IN_WIDTH:
        pieces.append(jnp.zeros((m.shape[0], hi - max(lo, IN_WIDTH)), m.dtype))
    return pieces


def _prep_first(wts, lands, small, l):
    land = lands["w_in"]
    cf = small["conv_ffn_w"][l]
    blk = lambda a, j: a[:, j * FF_BLK:(j + 1) * FF_BLK]
    local = [piece for lo, hi in _LOCAL_ORDER for piece in _own_cols(land, lo, hi)]
    return {
        "w_p": jnp.concatenate(local + [jnp.zeros((D_MODEL, IN_PAD - IN_WIDTH), BF16)], axis=1),
        "wf_t": _pad_rows(jnp.concatenate(_own_cols(land, _O_F, _O_B), axis=1).T, F_ROWS),
        "b_forget": _pad_rows(wts["b_forget"][l].reshape(N_HEADS, 1), F_ROWS),
        "b_gate": _pad_rows(small["b_gate"][l], 8),
        "conv_mix_w": _pad_rows(small["conv_mix_w"][l], 8),
        "conv_ffn_w": _pad_rows(jnp.concatenate([blk(cf, 0), blk(cf, 2), blk(cf, 1), blk(cf, 3)], axis=1), 8),
        "pre_mix_g": wts["pre_mix_g"][l].reshape(1, -1), "post_mix_g": wts["post_mix_g"][l].reshape(1, -1),
        "pre_ffn_g": wts["pre_ffn_g"][l].reshape(1, -1), "post_ffn_g": wts["post_ffn_g"][l].reshape(1, -1),
        "ln_g": wts["sgu_ln_g"][l].reshape(1, -1), "ln_b": wts["sgu_ln_b"][l].reshape(1, -1),
        "sgu_w": wts["sgu_w"][l],
        "sgu_bias": jnp.repeat(wts["sgu_b"][l].T, HEAD_DIM, axis=1),
    }


def _prep_late(lands):
    up = lands["w_ffn_up"]
    return {
        "w_att": _whole_cols(lands["w_branch_att"]), "w_conv": _whole_cols(lands["w_branch_conv"]),
        "w_sgu": _whole_cols(lands["w_branch_sgu"]),
        "w_out": lands["w_out"].reshape(D_MODEL, D_MODEL),
        "w_up": jnp.concatenate([up[0], up[2], up[1], up[3]], axis=1),
        "w_down": lands["w_ffn_down"].reshape(D_FF, D_MODEL),
    }


def _layer_fwd(x, p, dep=None, late=None):
    s = x.shape[0]
    xn = _rms_fwd(x, p["pre_mix_g"], "rms_pre_mix", dep)
    h = _mm(xn, p["w_p"], "nn", BF16, "mm_in", s, 512, D_MODEL)
    f_row = _mm(p["wf_t"], xn, "nt", F32, "mm_forget", F_ROWS, 2048, D_MODEL)
    ck = _gate_fwd(f_row, p["b_forget"], "gate_fwd")
    o, o_f32, lse = _attn_fwd(h, ck, "attn_fwd")
    yc = _sconv_fwd(h, p["conv_mix_w"], "sconv_fwd")
    ys = _sgu_fwd(h, p["ln_g"], p["ln_b"], p["sgu_w"], p["sgu_bias"], "sgu_fwd")
    if late is not None:
        p.update(late(o))
    merged = _merge_fwd(h, (o, yc, ys), (p["w_att"], p["w_conv"], p["w_sgu"]), p["b_gate"], "merge_fwd")
    mo = _mm(merged, p["w_out"], "nn", F32, "mm_out", 2048, 512, D_MODEL)
    x1, xn2 = _resid_post_norm(x, mo, p["post_mix_g"], p["pre_ffn_g"], "post_mix")
    h2 = _mm(xn2, p["w_up"], "nn", BF16, "mm_up", 2048, 512, D_MODEL)
    pact = _ffn_act_fwd(h2, p["conv_ffn_w"], "ffn_act_fwd")
    ff = _mm(pact, p["w_down"], "nn", F32, "mm_down", 1024, D_MODEL, D_FF)
    x2 = _resid_post(x1, ff, p["post_ffn_g"], "post_ffn")
    saved = dict(x=x, xn=xn, h=h, f_row=f_row, ck=ck, o=o, o_f32=o_f32, lse=lse, yc=yc, ys=ys, merged=merged, mo=mo, x1=x1,
                 xn2=xn2, h2=h2, pact=pact, ff=ff)
    return x2, saved


def _layer_bwd(dx2, p, sv, dep=None, early=None):
    s = dx2.shape[0]
    g = {}
    same = lambda b: b
    dff, g["post_ffn_g"] = _rms_bwd(sv["ff"], p["post_ffn_g"], [dx2], None, BF16, "post_ffn_bwd", dep)
    dpact = _mm(dff, p["w_down"], "nt", BF16, "mm_down_dx", 2048, FF_BLK, D_MODEL)
    t_down = _mm(sv["pact"], dff, "tn", BF16, "mm_down_dw", 256, D_MODEL, s).reshape(N_CHIPS, -1, D_MODEL)
    dh2, dconv_ffn = _ffn_act_conv_bwd(sv["h2"], p["conv_ffn_w"], dpact, "ffn_act_conv_bwd")
    dxn2 = _mm(dh2, p["w_up"], "nt", F32, "mm_up_dx", 512, D_MODEL, 2 * D_FF)
    t_up = _mm(sv["xn2"], dh2, "tn", BF16, "mm_up_dw", 512, FF_BLK, s, chip_of=lambda b: (b % 2) * 2 + b // 2)
    dx1, g["pre_ffn_g"] = _rms_bwd(sv["x1"], p["pre_ffn_g"], [dxn2], dx2, F32, "pre_ffn_bwd")
    dep_mix = early([t_up, t_down]) if early is not None else None
    dmo, g["post_mix_g"] = _rms_bwd(sv["mo"], p["post_mix_g"], [dx1], None, BF16, "post_mix_bwd", dep_mix)
    dmerged = _mm(dmo, p["w_out"], "nt", F32, "mm_out_dx", 2048, 512, D_MODEL)
    t_out = _mm(sv["merged"], dmo, "tn", BF16, "mm_out_dw", 512, D_MODEL, s).reshape(N_CHIPS, -1, D_MODEL)
    acts = (sv["o"], sv["yc"], sv["ys"])
    ws = (p["w_att"], p["w_conv"], p["w_sgu"])
    dy_a, dy_c, dy_s, dgl, db_gate = _merge_bwd(sv["h"], acts, ws, p["b_gate"], dmerged, "merge_bwd")
    do = _mm(dy_a, p["w_att"], "nt", BF16, "mm_att_dx", 2048, D_ATT, D_MODEL)
    dyc = _mm(dy_c, p["w_conv"], "nt", BF16, "mm_conv_dx", 2048, D_CONV, D_MODEL)
    dys = _mm(dy_s, p["w_sgu"], "nt", BF16, "mm_sgu_dx", 2048, D_SGU, D_MODEL)
    t_att = _mm(sv["o"], dy_a, "tn", BF16, "mm_att_dw", D_ATT, 256, s, chip_of=same)
    t_conv = _mm(sv["yc"], dy_c, "tn", BF16, "mm_conv_dw", D_CONV, 256, s, chip_of=same)
    t_sgu = _mm(sv["ys"], dy_s, "tn", BF16, "mm_sgu_dw", D_SGU, 256, s, chip_of=same)
    d_conv, dconv_mix = _sconv_bwd(sv["h"], p["conv_mix_w"], dyc, "sconv_bwd")
    d_sgu, g["sgu_ln_g"], g["sgu_ln_b"], g["sgu_w"], dbias = _sgu_bwd(
        sv["h"], p["ln_g"], p["ln_b"], p["sgu_w"], p["sgu_bias"], dys, "sgu_bwd")
    dq, dk, dv, dc_even, dc_odd = _attn_bwd(sv["h"], sv["ck"], sv["o_f32"], sv["lse"], do, "attn_bwd")
    df, db_forget = _gate_bwd(sv["f_row"], p["b_forget"], dc_even, dc_odd, "gate_bwd")
    f_cols = jnp.concatenate([df[:N_HEADS].T, jnp.zeros((s, IN_PAD - IN_WIDTH), BF16)], axis=1)
    dh = _assemble_dh(dgl, [dq, dk, dv, d_conv, d_sgu, f_cols], "assemble_dh")
    dxn = _mm(dh, p["w_p"], "nt", F32, "mm_in_dx", 512, D_MODEL, IN_PAD)
    dw_p = _mm(sv["xn"], dh, "tn", BF16, "mm_in_dw", D_MODEL, 512, s)
    t_in = jnp.stack([jnp.concatenate(_local_cols(dw_p, j * IN_SHARD, j * IN_SHARD + IN_SHARD_PAD), axis=1)
                      for j in range(N_CHIPS)])
    dx, g["pre_mix_g"] = _rms_bwd(sv["x"], p["pre_mix_g"], [dxn], dx1, F32, "pre_mix_bwd")
    blk = lambda a, j: a[:, j * FF_BLK:(j + 1) * FF_BLK]
    g["conv_ffn_w"] = jnp.concatenate([blk(dconv_ffn, 0), blk(dconv_ffn, 2), blk(dconv_ffn, 1),
                                       blk(dconv_ffn, 3)], axis=1)[:3]
    g["conv_mix_w"] = dconv_mix[:3]
    g["b_gate"] = db_gate[:3]
    g["b_forget"] = db_forget[:N_HEADS, 0]
    g["sgu_b"] = jnp.sum(dbias.reshape(CHUNK, N_GROUPS, HEAD_DIM), axis=-1).T
    for n in ("pre_mix_g", "post_mix_g", "pre_ffn_g", "post_ffn_g", "sgu_ln_g", "sgu_ln_b"):
        g[n] = g[n].reshape(-1)
    mix = [t_in, t_att, t_conv, t_sgu, t_out]
    return dx, (mix if early is not None else mix + [t_up, t_down]), g


def _assemble_dh(dh, pieces, name):
    s = dh.shape[0]
    t = _tile(s, 512)
    width = sum(a.shape[1] for a in pieces)
    assert 2 * width == dh.shape[1]

    def body(*refs):
        out = refs[-1]
        col = 0
        for ref in refs[1:-1]:
            w = ref.shape[1]
            out[:, col:col + w] = ref[...].astype(out.dtype)
            col += w

    return pl.pallas_call(
        body, name=name, grid=(s // t,),
        in_specs=[_ANY] + [pl.BlockSpec((t, a.shape[1]), lambda i: (i, 0)) for a in pieces],
        out_specs=pl.BlockSpec((t, width), lambda i: (i, 1)),
        out_shape=jax.ShapeDtypeStruct(dh.shape, dh.dtype),
        input_output_aliases={0: 0},
        compiler_params=_params(("parallel",)),
    )(dh, *pieces)


def _shard_cols(a, j):
    w = a.shape[-1] // N_CHIPS
    return a[..., j * w:(j + 1) * w]


def kernel(x, pre_mix_g, post_mix_g, pre_ffn_g, post_ffn_g, w_in, b_forget, b_gate, conv_mix_w, sgu_ln_g, sgu_ln_b, sgu_w, sgu_b, w_branch_att, w_branch_conv, w_branch_sgu, w_out, w_ffn_up, conv_ffn_w, w_ffn_down, loss_target, m_pre_mix_g, m_post_mix_g, m_pre_ffn_g, m_post_ffn_g, m_w_in, m_b_forget, m_b_gate, m_conv_mix_w, m_sgu_ln_g, m_sgu_ln_b, m_sgu_w, m_sgu_b, m_w_branch_att, m_w_branch_conv, m_w_branch_sgu, m_w_out, m_w_ffn_up, m_conv_ffn_w, m_w_ffn_down, v_pre_mix_g, v_post_mix_g, v_pre_ffn_g, v_post_ffn_g, v_w_in, v_b_forget, v_b_gate, v_conv_mix_w, v_sgu_ln_g, v_sgu_ln_b, v_sgu_w, v_sgu_b, v_w_branch_att, v_w_branch_conv, v_w_branch_sgu, v_w_out, v_w_ffn_up, v_conv_ffn_w, v_w_ffn_down):
    wts = dict(pre_mix_g=pre_mix_g, post_mix_g=post_mix_g, pre_ffn_g=pre_ffn_g, post_ffn_g=post_ffn_g, w_in=w_in,
               b_forget=b_forget, b_gate=b_gate, conv_mix_w=conv_mix_w, sgu_ln_g=sgu_ln_g, sgu_ln_b=sgu_ln_b,
               sgu_w=sgu_w, sgu_b=sgu_b, w_branch_att=w_branch_att, w_branch_conv=w_branch_conv,
               w_branch_sgu=w_branch_sgu, w_out=w_out, w_ffn_up=w_ffn_up, conv_ffn_w=conv_ffn_w,
               w_ffn_down=w_ffn_down)
    moms = dict(pre_mix_g=m_pre_mix_g, post_mix_g=m_post_mix_g, pre_ffn_g=m_pre_ffn_g, post_ffn_g=m_post_ffn_g,
                w_in=m_w_in, b_forget=m_b_forget, b_gate=m_b_gate, conv_mix_w=m_conv_mix_w, sgu_ln_g=m_sgu_ln_g,
                sgu_ln_b=m_sgu_ln_b, sgu_w=m_sgu_w, sgu_b=m_sgu_b, w_branch_att=m_w_branch_att,
                w_branch_conv=m_w_branch_conv, w_branch_sgu=m_w_branch_sgu, w_out=m_w_out, w_ffn_up=m_w_ffn_up,
                conv_ffn_w=m_conv_ffn_w, w_ffn_down=m_w_ffn_down)
    vels = dict(pre_mix_g=v_pre_mix_g, post_mix_g=v_post_mix_g, pre_ffn_g=v_pre_ffn_g, post_ffn_g=v_post_ffn_g,
                w_in=v_w_in, b_forget=v_b_forget, b_gate=v_b_gate, conv_mix_w=v_conv_mix_w, sgu_ln_g=v_sgu_ln_g,
                sgu_ln_b=v_sgu_ln_b, sgu_w=v_sgu_w, sgu_b=v_sgu_b, w_branch_att=v_w_branch_att,
                w_branch_conv=v_w_branch_conv, w_branch_sgu=v_w_branch_sgu, w_out=v_w_out, w_ffn_up=v_w_ffn_up,
                conv_ffn_w=v_conv_ffn_w, w_ffn_down=v_w_ffn_down)

    c_idx = lax.axis_index("c").astype(jnp.int32).reshape(1)
    me_idx = (2 * lax.axis_index("x") + lax.axis_index("y")).astype(jnp.int32).reshape(1)
    small = _gather_small(wts)

    xs = x[0]
    layers, saved = [], []
    first = _gather_begin(wts, 0, me_idx, FIRST_NAMES, "first")
    rest = _gather_begin(wts, 0, me_idx, LATE_NAMES, "late")
    lands = _gather_finish(first, xs, FIRST_NAMES, "first")
    late = lambda after: _prep_late(_gather_finish(rest, after, LATE_NAMES, "late"))
    for l in range(DEPTH):
        p = _prep_first(wts, lands, small, l)
        if l > 0:
            p.update(_prep_late(lands))
        nxt = _gather_begin(wts, l + 1, me_idx, BIG_NAMES, "all") if l + 1 < DEPTH else None
        dep = ([nxt[4]] if nxt else []) + ([rest[4]] if l == 0 else [])
        xs, sv = _layer_fwd(xs, p, dep or None, late if l == 0 else None)
        if nxt:
            lands = _gather_finish(nxt, xs, BIG_NAMES, "all")
        layers.append(p)
        saved.append(sv)
    dy, loss_part = _loss_head(xs, loss_target[0], "loss_head")
    loss = lax.psum(loss_part[0, 0], ("x", "y", "c"))

    big_red = [None] * DEPTH
    small_grads = [None] * DEPTH
    pending = None
    ffn = []
    for l in reversed(range(DEPTH)):
        early = None
        if l == 0:
            def early(ts_ffn):
                ffn.append(_rs_begin(ts_ffn, c_idx, me_idx, "ffn"))
                return ffn[0][4]
        dy, ts, small_grads[l] = _layer_bwd(dy, layers[l], saved[l], pending[4] if pending else None, early)
        if pending:
            big_red[l + 1] = _rs_finish(pending, dy, c_idx, "big")
        pending = _rs_begin(ts, c_idx, me_idx, "mix" if l == 0 else "big")
    red_ffn = _rs_finish(ffn[0], dy, c_idx, "ffn")
    grad_x = dy[None]

    rep_flat = jnp.concatenate([small_grads[l][n].reshape(-1) for l in range(DEPTH) for n, _ in REPLICATED])
    rep_flat = jnp.pad(rep_flat, (0, N_CHIPS * _REP_QUARTER - rep_flat.shape[0]))
    rows = []
    for j in range(N_CHIPS):
        pieces = [_shard_cols(small_grads[l][n], j) for l in range(DEPTH) for n, _ in SMALL_SHARDS]
        pieces.append(rep_flat[j * _REP_QUARTER:(j + 1) * _REP_QUARTER])
        rows.append(_pack_rows(pieces, SMALL_ROWS, F32))
    small_red = _reduce_scatter_chips(jnp.stack(rows), "small", pending[4])
    small_all = _all_gather_chips(small_red, "gather_small")
    done = {}
    for k, n in enumerate(("w_ffn_up", "w_ffn_down")):
        i = BIG_NAMES.index(n)
        g = jnp.stack([red_ffn[k]] + [big_red[l][i] for l in range(1, DEPTH)])
        done[n] = (g,) + _adamw(wts[n], g, moms[n], vels[n], "adamw_" + n, pending[4])
    big_red[0] = _rs_finish(pending, [small_all] + [done[n][1] for n in done], c_idx, "mix") + red_ffn
    small_all = small_all.reshape(N_CHIPS, -1)

    grads = {}
    for i, (n, _) in enumerate(BIG_SHARDS):
        if n not in done:
            grads[n] = jnp.stack([big_red[l][i][:, :IN_SHARD] if n == "w_in" else big_red[l][i]
                                  for l in range(DEPTH)])
    mine_small = small_red.reshape(-1)
    parts = _unpack(mine_small, [s for _ in range(DEPTH) for _, s in SMALL_SHARDS])
    for i, (n, _) in enumerate(SMALL_SHARDS):
        grads[n] = jnp.stack([parts[l * len(SMALL_SHARDS) + i] for l in range(DEPTH)])
    off = DEPTH * _SMALL_ELEMS
    rep_all = jnp.concatenate([small_all[j, off:off + _REP_QUARTER] for j in range(N_CHIPS)])
    parts = _unpack(rep_all, [s for _ in range(DEPTH) for _, s in REPLICATED])
    for i, (n, _) in enumerate(REPLICATED):
        grads[n] = jnp.stack([parts[l * len(REPLICATED) + i] for l in range(DEPTH)])

    deltas, new_m, new_v = {}, {}, {}
    for n in WEIGHT_ORDER:
        if n in done:
            grads[n], deltas[n], new_m[n], new_v[n] = done[n]
        else:
            deltas[n], new_m[n], new_v[n] = _adamw(wts[n], grads[n], moms[n], vels[n], "adamw_" + n)
    return (loss, grad_x, *[grads[n] for n in WEIGHT_ORDER], *[deltas[n] for n in WEIGHT_ORDER],
            *[new_m[n] for n in WEIGHT_ORDER], *[new_v[n] for n in WEIGHT_ORDER])
```

```python
import functools
import math

import jax
import jax.numpy as jnp
from jax import lax
from jax.experimental import pallas as pl
from jax.experimental.pallas import tpu as pltpu

F32 = jnp.float32
BF16 = jnp.bfloat16
MXU_DTYPE = jnp.bfloat16

D_MODEL = 1024
HEAD_DIM = 64
N_HEADS = 8
D_ATT = 512
D_CONV = 256
D_SGU = 256
N_GROUPS = 4
CHUNK = 128
D_FF = 2816
DEPTH = 4
RMS_EPS = 1e-6
LN_EPS = 1e-5
N_CHIPS = 4
LANES = 128
PACK_COLS = 1024
HALO = 16

ADAM_LR = 0.001
ADAM_B1 = 0.9
ADAM_B2 = 0.999
ADAM_EPS = 1e-08
ADAM_WD = 0.01
ADAM_STEP = 10

OFF_GL = 0
OFF_Q = 3 * D_MODEL
OFF_K = OFF_Q + D_ATT
OFF_V = OFF_K + D_ATT
OFF_BG = OFF_V + D_ATT
OFF_CG = OFF_BG + D_CONV
OFF_HC = OFF_CG + D_CONV
OFF_U = OFF_HC + D_CONV
OFF_VS = OFF_U + D_SGU
W_P = OFF_VS + D_SGU
F_ROWS = 16

VMEM_LIMIT = 56 * 1024 * 1024
MESH = pl.DeviceIdType.MESH


def _params(sem=None):
    if sem is None:
        return pltpu.CompilerParams(vmem_limit_bytes=VMEM_LIMIT)
    return pltpu.CompilerParams(dimension_semantics=sem, vmem_limit_bytes=VMEM_LIMIT)


def _tile(dim, pref):
    if dim <= pref:
        return dim
    if dim % pref == 0:
        return pref
    return dim


_DIMS = {"nn": (((1,), (0,)), ((), ())), "nt": (((1,), (1,)), ((), ())), "tn": (((0,), (0,)), ((), ()))}


def _mm(a, b, mode, out_dtype, name, tm, tn, tk, chip_of=None):
    if mode == "tn":
        K, M = a.shape
    else:
        M, K = a.shape
    N = b.shape[0] if mode == "nt" else b.shape[1]
    tm, tn, tk = _tile(M, tm), _tile(N // N_CHIPS if chip_of else N, tn), _tile(K, tk)
    nk = K // tk
    dims = _DIMS[mode]

    def body(a_ref, b_ref, o_ref, *acc):
        part = lax.dot_general(a_ref[...].astype(MXU_DTYPE), b_ref[...].astype(MXU_DTYPE), dims,
                               preferred_element_type=F32)
        if nk == 1:
            o_ref[...] = part.astype(o_ref.dtype)
        else:
            acc_ref = acc[0]
            k = pl.program_id(2)

            @pl.when(k == 0)
            def _():
                acc_ref[...] = part

            @pl.when(k > 0)
            def _():
                acc_ref[...] += part

            @pl.when(k == nk - 1)
            def _():
                o_ref[...] = acc_ref[...].astype(o_ref.dtype)

    if mode == "tn":
        a_spec = pl.BlockSpec((tk, tm), lambda i, j, k: (k, i))
    else:
        a_spec = pl.BlockSpec((tm, tk), lambda i, j, k: (i, k))
    if mode == "nt":
        b_spec = pl.BlockSpec((tn, tk), lambda i, j, k: (j, k))
    else:
        b_spec = pl.BlockSpec((tk, tn), lambda i, j, k: (k, j))
    if chip_of is None:
        out_spec = pl.BlockSpec((tm, tn), lambda i, j, k: (i, j))
        out_shape = jax.ShapeDtypeStruct((M, N), out_dtype)
    else:
        per = (N // N_CHIPS) // tn
        out_spec = pl.BlockSpec((None, tm, tn), lambda i, j, k: (chip_of(j // per), i, j % per))
        out_shape = jax.ShapeDtypeStruct((N_CHIPS, M, N // N_CHIPS), out_dtype)
    return pl.pallas_call(
        body,
        name=name,
        grid=(M // tm, N // tn, nk),
        in_specs=[a_spec, b_spec],
        out_specs=out_spec,
        out_shape=out_shape,
        scratch_shapes=[pltpu.VMEM((tm, tn), F32)] if nk > 1 else [],
        compiler_params=_params(("parallel", "parallel", "arbitrary")),
    )(a, b)


_GELU_K = math.sqrt(2.0 / math.pi)
_GELU_C = 0.044715


def _gelu(x):
    t = jnp.tanh(_GELU_K * (x + _GELU_C * (x * x * x)))
    return x * (0.5 * (1.0 + t))


def _gelu_and_grad(x):
    x2 = x * x
    t = jnp.tanh(_GELU_K * (x + _GELU_C * (x2 * x)))
    cdf = 0.5 * (1.0 + t)
    dcdf = 0.5 * (1.0 - t * t) * (_GELU_K * (1.0 + 3.0 * _GELU_C * x2))
    return x * cdf, cdf + x * dcdf


def _sigmoid(x):
    return 1.0 / (1.0 + jnp.exp(-x))


def _shift_down(cur, prev, k):
    h = prev.shape[0]
    ext = jnp.concatenate([prev, cur], axis=0)
    return pltpu.roll(ext, k, 0)[h:]


def _shift_up(cur, nxt, k):
    t, h = cur.shape[0], nxt.shape[0]
    ext = jnp.concatenate([cur, nxt], axis=0)
    return pltpu.roll(ext, t + h - k, 0)[:t]


def _row_sum8(x):
    t, c = x.shape
    return jnp.sum(x.reshape(t // 8, 8, c), axis=0)


_DEP = pl.BlockSpec((8, LANES), lambda i: (0, 0))


def _rms_fwd(x, g, name, dep=None):
    s, d = x.shape
    t = _tile(s, 512)

    def body(x_ref, g_ref, *rest):
        o_ref = rest[-1]
        xv = x_ref[...]
        r = lax.rsqrt(jnp.mean(xv * xv, axis=-1, keepdims=True) + RMS_EPS)
        o_ref[...] = (xv * r * g_ref[...]).astype(o_ref.dtype)

    deps = [] if dep is None else list(dep) if isinstance(dep, (list, tuple)) else [dep]
    return pl.pallas_call(
        body, name=name, grid=(s // t,),
        in_specs=[pl.BlockSpec((t, d), lambda i: (i, 0)), pl.BlockSpec((1, d), lambda i: (0, 0))] + [_DEP] * len(deps),
        out_specs=pl.BlockSpec((t, d), lambda i: (i, 0)),
        out_shape=jax.ShapeDtypeStruct((s, d), BF16),
        compiler_params=_params(("parallel",)),
    )(x, g, *deps)


def _resid_post(x, y, g, name):
    s, d = x.shape
    t = _tile(s, 512)

    def body(x_ref, y_ref, g_ref, o_ref):
        yv = y_ref[...]
        r = lax.rsqrt(jnp.mean(yv * yv, axis=-1, keepdims=True) + RMS_EPS)
        o_ref[...] = x_ref[...] + yv * r * g_ref[...]

    row = pl.BlockSpec((t, d), lambda i: (i, 0))
    return pl.pallas_call(
        body, name=name, grid=(s // t,),
        in_specs=[row, row, pl.BlockSpec((1, d), lambda i: (0, 0))],
        out_specs=row,
        out_shape=jax.ShapeDtypeStruct((s, d), F32),
        compiler_params=_params(("parallel",)),
    )(x, y, g)


def _resid_post_norm(x, y, g, g_next, name):
    s, d = x.shape
    t = _tile(s, 512)

    def body(x_ref, y_ref, g_ref, gn_ref, o_ref, xn_ref):
        yv = y_ref[...]
        r = lax.rsqrt(jnp.mean(yv * yv, axis=-1, keepdims=True) + RMS_EPS)
        x1 = x_ref[...] + yv * r * g_ref[...]
        o_ref[...] = x1
        r1 = lax.rsqrt(jnp.mean(x1 * x1, axis=-1, keepdims=True) + RMS_EPS)
        xn_ref[...] = (x1 * r1 * gn_ref[...]).astype(xn_ref.dtype)

    row = pl.BlockSpec((t, d), lambda i: (i, 0))
    vec = pl.BlockSpec((1, d), lambda i: (0, 0))
    return pl.pallas_call(
        body, name=name, grid=(s // t,),
        in_specs=[row, row, vec, vec],
        out_specs=[row, row],
        out_shape=[jax.ShapeDtypeStruct((s, d), F32), jax.ShapeDtypeStruct((s, d), BF16)],
        compiler_params=_params(("parallel",)),
    )(x, y, g, g_next)


def _rms_bwd(xin, g, dys, dres, out_dtype, name, dep=None):
    s, d = xin.shape
    t = _tile(s, 512)
    n = s // t
    n_dy = len(dys)
    has_res = dres is not None
    deps = [] if dep is None else [dep]

    def body(*refs):
        x_ref, g_ref = refs[0], refs[1]
        dy_refs = refs[2:2 + n_dy]
        pos = 2 + n_dy
        res_ref = refs[pos] if has_res else None
        pos += (1 if has_res else 0) + len(deps)
        dx_ref, dg_ref, acc_ref = refs[pos], refs[pos + 1], refs[pos + 2]
        i = pl.program_id(0)
        xv = x_ref[...]
        dy = dy_refs[0][...].astype(F32)
        for extra in dy_refs[1:]:
            dy = dy + extra[...].astype(F32)
        r = lax.rsqrt(jnp.mean(xv * xv, axis=-1, keepdims=True) + RMS_EPS)
        u = dy * g_ref[...]
        xr = xv * r
        dx = r * (u - xr * jnp.mean(u * xr, axis=-1, keepdims=True))
        if has_res:
            dx = dx + res_ref[...]
        dx_ref[...] = dx.astype(dx_ref.dtype)
        part = _row_sum8(dy * xr)

        @pl.when(i == 0)
        def _():
            acc_ref[...] = part

        @pl.when(i > 0)
        def _():
            acc_ref[...] += part

        @pl.when(i == n - 1)
        def _():
            dg_ref[...] = jnp.sum(acc_ref[...], axis=0, keepdims=True)

    row = pl.BlockSpec((t, d), lambda i: (i, 0))
    vec = pl.BlockSpec((1, d), lambda i: (0, 0))
    ins = [xin, g, *dys] + ([dres] if has_res else []) + deps
    return pl.pallas_call(
        body, name=name, grid=(n,),
        in_specs=[row, vec] + [row] * (n_dy + (1 if has_res else 0)) + [_DEP] * len(deps),
        out_specs=[row, vec],
        out_shape=[jax.ShapeDtypeStruct((s, d), out_dtype), jax.ShapeDtypeStruct((1, d), F32)],
        scratch_shapes=[pltpu.VMEM((8, d), F32)],
        compiler_params=_params(("arbitrary",)),
    )(*ins)


def _loss_head(y, target, name):
    s, d = y.shape
    t = _tile(s, 512)
    n = s // t

    def body(y_ref, t_ref, dy_ref, loss_ref, acc_ref):
        i = pl.program_id(0)
        e = y_ref[...] - t_ref[...]
        dy_ref[...] = e * (1.0 / d)
        part = _row_sum8(e * e)

        @pl.when(i == 0)
        def _():
            acc_ref[...] = part

        @pl.when(i > 0)
        def _():
            acc_ref[...] += part

        @pl.when(i == n - 1)
        def _():
            tot = jnp.sum(jnp.sum(acc_ref[...], axis=0, keepdims=True), axis=1, keepdims=True)
            loss_ref[...] = tot * (0.5 / d)

    row = pl.BlockSpec((t, d), lambda i: (i, 0))
    return pl.pallas_call(
        body, name=name, grid=(n,),
        in_specs=[row, row],
        out_specs=[row, pl.BlockSpec((1, 1), lambda i: (0, 0))],
        out_shape=[jax.ShapeDtypeStruct((s, d), F32), jax.ShapeDtypeStruct((1, 1), F32)],
        scratch_shapes=[pltpu.VMEM((8, d), F32)],
        compiler_params=_params(("arbitrary",)),
    )(y, target)


def _split3(x):
    hi = x.astype(BF16)
    r1 = x - hi.astype(F32)
    mid = r1.astype(BF16)
    lo = (r1 - mid.astype(F32)).astype(BF16)
    return hi, mid, lo


def _tri_dot(x, tri):
    hi, mid, lo = _split3(x)
    dn = _DIMS["nn"]
    out = lax.dot_general(hi, tri, dn, preferred_element_type=F32)
    out = out + lax.dot_general(mid, tri, dn, preferred_element_type=F32)
    return out + lax.dot_general(lo, tri, dn, preferred_element_type=F32)


def _log_sigmoid(z):
    return jnp.minimum(z, 0.0) - jnp.log(1.0 + jnp.exp(-jnp.abs(z)))


def _gate_fwd(f_row, b_col, name):
    rows, s = f_row.shape
    t = _tile(s, 512)
    n = s // t

    def body(f_ref, b_ref, ck_ref, carry_ref):
        i = pl.program_id(0)

        @pl.when(i == 0)
        def _():
            carry_ref[...] = jnp.zeros_like(carry_ref)

        logf = _log_sigmoid(f_ref[...] + b_ref[...])
        r = lax.broadcasted_iota(jnp.int32, (t, t), 0)
        c = lax.broadcasted_iota(jnp.int32, (t, t), 1)
        tri = jnp.where(r <= c, 1.0, 0.0).astype(BF16)
        cs = _tri_dot(logf, tri) + carry_ref[...]
        carry_ref[...] = cs[:, t - 1:t]
        terms = [part.astype(F32) for part in _split3(-cs)]
        sub = lax.broadcasted_iota(jnp.int32, (LANES, t), 0)
        for p in range(N_HEADS // 2):
            stacked = jnp.zeros((LANES, t), F32)
            for hh in range(2):
                for j, term in enumerate(terms):
                    h = 2 * p + hh
                    stacked = jnp.where(sub == 3 * hh + j, jnp.broadcast_to(term[h:h + 1, :], (LANES, t)), stacked)
            ck_ref[p] = stacked.T.astype(ck_ref.dtype)

    return pl.pallas_call(
        body, name=name, grid=(n,),
        in_specs=[pl.BlockSpec((rows, t), lambda i: (0, i)), pl.BlockSpec((rows, 1), lambda i: (0, 0))],
        out_specs=pl.BlockSpec((N_HEADS // 2, t, LANES), lambda i: (0, i, 0)),
        out_shape=jax.ShapeDtypeStruct((N_HEADS // 2, s, LANES), BF16),
        scratch_shapes=[pltpu.VMEM((rows, 1), F32)],
        compiler_params=_params(("arbitrary",)),
    )(f_row, b_col)


def _gate_bwd(f_row, b_col, dc_even, dc_odd, name):
    rows, s = f_row.shape
    t = _tile(s, 512)
    n = s // t

    def body(f_ref, b_ref, dce_ref, dco_ref, df_ref, db_ref, carry_ref, acc_ref):
        i = pl.program_id(0)

        @pl.when(i == 0)
        def _():
            carry_ref[...] = jnp.zeros_like(carry_ref)
            acc_ref[...] = jnp.zeros_like(acc_ref)

        head = lax.broadcasted_iota(jnp.int32, (rows, t), 0)
        dcv = jnp.zeros((rows, t), F32)
        for h in range(N_HEADS):
            src = dce_ref if h % 2 == 0 else dco_ref
            dcv = jnp.where(head == h, jnp.broadcast_to(src[h // 2, 0:1, :], (rows, t)), dcv)
        r = lax.broadcasted_iota(jnp.int32, (t, t), 0)
        c = lax.broadcasted_iota(jnp.int32, (t, t), 1)
        tri = jnp.where(r >= c, 1.0, 0.0).astype(BF16)
        dlogf = _tri_dot(dcv, tri) + carry_ref[...]
        carry_ref[...] = dlogf[:, 0:1]
        z = f_ref[...] + b_ref[...]
        df = dlogf * _sigmoid(-z)
        df_ref[...] = df.astype(df_ref.dtype)
        acc_ref[...] += jnp.sum(df, axis=1, keepdims=True)

        @pl.when(i == n - 1)
        def _():
            db_ref[...] = acc_ref[...]

    rev = lambda i: (0, n - 1 - i)
    dc_spec = pl.BlockSpec((N_HEADS // 2, 8, t), lambda i: (0, 0, n - 1 - i))
    return pl.pallas_call(
        body, name=name, grid=(n,),
        in_specs=[pl.BlockSpec((rows, t), rev), pl.BlockSpec((rows, 1), lambda i: (0, 0)), dc_spec, dc_spec],
        out_specs=[pl.BlockSpec((rows, t), rev), pl.BlockSpec((rows, 1), lambda i: (0, 0))],
        out_shape=[jax.ShapeDtypeStruct((rows, s), BF16), jax.ShapeDtypeStruct((rows, 1), F32)],
        scratch_shapes=[pltpu.VMEM((rows, 1), F32), pltpu.VMEM((rows, 1), F32)],
        compiler_params=_params(("arbitrary",)),
    )(f_row, b_col, dc_even, dc_odd)


_NEG = -1e30
_SCALE = HEAD_DIM ** -0.5


def _head_masks():
    lane = lax.broadcasted_iota(jnp.int32, (1, LANES), 1)
    return [lane < HEAD_DIM, lane >= HEAD_DIM]


def _attn_fwd(h, ck, name):
    s = h.shape[0]
    t = _tile(s, 512)
    n = s // t
    qb, kb, vb = OFF_Q // LANES, OFF_K // LANES, OFF_V // LANES

    pairs = [(qi, ki) for qi in range(n) for ki in range(qi + 1)]
    qi_tab = jnp.asarray([qi for qi, _ in pairs], jnp.int32)
    ki_tab = jnp.asarray([ki for _, ki in pairs], jnp.int32)

    def body(qi_ref, ki_ref, q_ref, k_ref, v_ref, ck_ref, o_ref, of_ref, lse_ref, m_ref, l_ref, acc_ref):
        qi, ki = qi_ref[pl.program_id(1)], ki_ref[pl.program_id(1)]
        masks = _head_masks()
        lane = lax.broadcasted_iota(jnp.int32, (1, LANES), 1)

        @pl.when(ki == 0)
        def _():
            m_ref[...] = jnp.full_like(m_ref, _NEG)
            l_ref[...] = jnp.zeros_like(l_ref)
            acc_ref[...] = jnp.zeros_like(acc_ref)

        def step(diag):
            q = q_ref[...] * _SCALE
            k_aug = jnp.concatenate([k_ref[...], ck_ref[0]], axis=1)
            v = v_ref[...]
            nq = max(1, t // 256)
            wq = t // nq
            chains = [(hh, j) for hh in range(2) for j in range(nq)]
            scores = []
            for hh, j in chains:
                qs = q[j * wq:(j + 1) * wq]
                ones = jnp.where((lane >= 3 * hh) & (lane < 3 * hh + 3), 1.0, 0.0).astype(q.dtype)
                q_aug = jnp.concatenate([jnp.where(masks[hh], qs, jnp.zeros_like(qs)),
                                         jnp.broadcast_to(ones, qs.shape)], axis=1)
                scores.append(lax.dot_general(k_aug, q_aug, _DIMS["nt"], preferred_element_type=F32))
            probs = []
            for (hh, j), sc in zip(chains, scores):
                cols = slice(j * wq, (j + 1) * wq)
                if diag:
                    r = lax.broadcasted_iota(jnp.int32, (t, wq), 0)
                    cc = lax.broadcasted_iota(jnp.int32, (t, wq), 1) + j * wq
                    sc = jnp.where(r <= cc, sc, _NEG)
                m_prev = m_ref[hh, :, cols]
                m_new = jnp.maximum(m_prev, jnp.max(sc, axis=0, keepdims=True))
                alpha = jnp.exp(m_prev - m_new)
                p = jnp.exp(sc - m_new)
                l_ref[hh, :, cols] = alpha * l_ref[hh, :, cols] + jnp.sum(p, axis=0, keepdims=True)
                m_ref[hh, :, cols] = m_new
                p_hi = p.astype(MXU_DTYPE)
                p_lo = (p - p_hi.astype(F32)).astype(MXU_DTYPE)
                probs.append((alpha, p_hi, p_lo))
            for (hh, j), (alpha, p_hi, p_lo) in zip(chains, probs):
                pv = (lax.dot_general(v, p_hi, _DIMS["tn"], preferred_element_type=F32)
                      + lax.dot_general(v, p_lo, _DIMS["tn"], preferred_element_type=F32))
                rows = slice(hh * HEAD_DIM, (hh + 1) * HEAD_DIM)
                cols = slice(j * wq, (j + 1) * wq)
                acc_ref[rows, cols] = alpha * acc_ref[rows, cols] + pv[rows]

        @pl.when(ki < qi)
        def _():
            step(False)

        @pl.when(ki == qi)
        def _():
            step(True)
            inv = jnp.concatenate([jnp.broadcast_to(1.0 / l_ref[hh], (HEAD_DIM, t)) for hh in range(2)], axis=0)
            out = (acc_ref[...] * inv).T
            o_ref[...] = out.astype(o_ref.dtype)
            of_ref[...] = out
            lse = jnp.concatenate([jnp.broadcast_to(m_ref[hh] + jnp.log(l_ref[hh]), (HEAD_DIM, t))
                                   for hh in range(2)], axis=0)
            lse_ref[...] = lse.T

    grid_spec = pltpu.PrefetchScalarGridSpec(
        num_scalar_prefetch=2, grid=(N_HEADS // 2, len(pairs)),
        in_specs=[
            pl.BlockSpec((t, LANES), lambda p, i, qt, kt: (qt[i], qb + p)),
            pl.BlockSpec((t, LANES), lambda p, i, qt, kt: (kt[i], kb + p)),
            pl.BlockSpec((t, LANES), lambda p, i, qt, kt: (kt[i], vb + p)),
            pl.BlockSpec((1, t, LANES), lambda p, i, qt, kt: (p, kt[i], 0)),
        ],
        out_specs=[pl.BlockSpec((t, LANES), lambda p, i, qt, kt: (qt[i], p))] * 3,
        scratch_shapes=[pltpu.VMEM((2, 1, t), F32), pltpu.VMEM((2, 1, t), F32), pltpu.VMEM((LANES, t), F32)])
    return pl.pallas_call(
        body, name=name, grid_spec=grid_spec,
        out_shape=[jax.ShapeDtypeStruct((s, D_ATT), BF16), jax.ShapeDtypeStruct((s, D_ATT), F32),
                   jax.ShapeDtypeStruct((s, D_ATT), F32)],
        compiler_params=_params(("parallel", "arbitrary")),
    )(qi_tab, ki_tab, h, h, h, ck)


def _attn_bwd(h, ck, o, lse, do, name):
    s = h.shape[0]
    t = _tile(s, 512)
    n = s // t
    qb, kb, vb = OFF_Q // LANES, OFF_K // LANES, OFF_V // LANES

    pairs = [(ki, qi) for ki in range(n) for qi in range(ki, n)]
    ki_tab = jnp.asarray([ki for ki, _ in pairs], jnp.int32)
    qi_tab = jnp.asarray([qi for _, qi in pairs], jnp.int32)

    def body(ki_ref, qi_ref, q_ref, k_ref, v_ref, ck_ref, o_ref, lse_ref, do_ref,
             dq_ref, dk_ref, dv_ref, dc0_ref, dc1_ref, dk_acc, dv_acc, dc_acc):
        ki, qi = ki_ref[pl.program_id(1)], qi_ref[pl.program_id(1)]
        masks = _head_masks()
        lane = lax.broadcasted_iota(jnp.int32, (1, LANES), 1)

        @pl.when((ki == 0) & (qi == 0))
        def _():
            dq_ref[...] = jnp.zeros_like(dq_ref)

        @pl.when(qi == ki)
        def _():
            dk_acc[...] = jnp.zeros_like(dk_acc)
            dv_acc[...] = jnp.zeros_like(dv_acc)
            dc_acc[...] = jnp.zeros_like(dc_acc)

        def step(diag):
            q = q_ref[...] * _SCALE
            k = k_ref[...]
            v = v_ref[...]
            dov = do_ref[...]
            k_aug = jnp.concatenate([k, ck_ref[0]], axis=1)
            prod_t = (dov.astype(F32) * o_ref[...]).T
            lse_t = lse_ref[...].T
            heads = []
            for hh in range(2):
                mk = masks[hh]
                qh = jnp.where(mk, q, jnp.zeros_like(q))
                kh = jnp.where(mk, k, jnp.zeros_like(k))
                doh = jnp.where(mk, dov, jnp.zeros_like(dov))
                ones = jnp.where((lane >= 3 * hh) & (lane < 3 * hh + 3), 1.0, 0.0).astype(q.dtype)
                q_aug = jnp.concatenate([qh, jnp.broadcast_to(ones, q.shape)], axis=1)
                sc = lax.dot_general(k_aug, q_aug, _DIMS["nt"], preferred_element_type=F32)
                dp = lax.dot_general(v, doh, _DIMS["nt"], preferred_element_type=F32)
                heads.append((qh, kh, doh, sc, dp))
            grads = []
            for hh, (qh, kh, doh, sc, dp) in enumerate(heads):
                rows = slice(hh * HEAD_DIM, (hh + 1) * HEAD_DIM)
                p = jnp.exp(sc - lse_t[hh * HEAD_DIM:hh * HEAD_DIM + 1, :])
                if diag:
                    r = lax.broadcasted_iota(jnp.int32, (t, t), 0)
                    cc = lax.broadcasted_iota(jnp.int32, (t, t), 1)
                    p = jnp.where(r <= cc, p, 0.0)
                delta = jnp.sum(prod_t[rows], axis=0, keepdims=True)
                ds = p * (dp - delta)
                dc_acc[hh] = dc_acc[hh] - jnp.sum(ds, axis=1, keepdims=True)
                grads.append((ds.astype(MXU_DTYPE), p.astype(MXU_DTYPE)))
            dq_blk = jnp.zeros((t, LANES), F32)
            for (qh, kh, doh, _, _), (dsb, pb) in zip(heads, grads):
                dv_acc[...] += lax.dot_general(pb, doh, _DIMS["nn"], preferred_element_type=F32)
                dk_acc[...] += lax.dot_general(dsb, qh, _DIMS["nn"], preferred_element_type=F32)
                dq_blk = dq_blk + lax.dot_general(dsb, kh, _DIMS["tn"], preferred_element_type=F32)
            rows_q = pl.ds(pl.multiple_of(qi * t, t), t)
            dq_ref[rows_q, :] = dq_ref[rows_q, :] + dq_blk * _SCALE

        @pl.when(qi > ki)
        def _():
            step(False)

        @pl.when(qi == ki)
        def _():
            step(True)

        @pl.when(qi == n - 1)
        def _():
            dk_ref[...] = dk_acc[...].astype(dk_ref.dtype)
            dv_ref[...] = dv_acc[...].astype(dv_ref.dtype)
            dc0_ref[0] = jnp.broadcast_to(dc_acc[0], (t, LANES)).T[0:8]
            dc1_ref[0] = jnp.broadcast_to(dc_acc[1], (t, LANES)).T[0:8]

    q_blk = lambda col: pl.BlockSpec((t, LANES), lambda p, i, kt, qt: (qt[i], col(p)))
    k_blk = lambda col: pl.BlockSpec((t, LANES), lambda p, i, kt, qt: (kt[i], col(p)))
    dc_blk = pl.BlockSpec((1, 8, t), lambda p, i, kt, qt: (p, 0, kt[i]))
    grid_spec = pltpu.PrefetchScalarGridSpec(
        num_scalar_prefetch=2, grid=(N_HEADS // 2, len(pairs)),
        in_specs=[q_blk(lambda p: qb + p), k_blk(lambda p: kb + p), k_blk(lambda p: vb + p),
                  pl.BlockSpec((1, t, LANES), lambda p, i, kt, qt: (p, kt[i], 0)),
                  q_blk(lambda p: p), q_blk(lambda p: p), q_blk(lambda p: p)],
        out_specs=[pl.BlockSpec((s, LANES), lambda p, i, kt, qt: (0, p)), k_blk(lambda p: p), k_blk(lambda p: p),
                   dc_blk, dc_blk],
        scratch_shapes=[pltpu.VMEM((t, LANES), F32), pltpu.VMEM((t, LANES), F32), pltpu.VMEM((2, t, 1), F32)])
    return pl.pallas_call(
        body, name=name, grid_spec=grid_spec,
        out_shape=[jax.ShapeDtypeStruct((s, D_ATT), F32), jax.ShapeDtypeStruct((s, D_ATT), BF16),
                   jax.ShapeDtypeStruct((s, D_ATT), BF16), jax.ShapeDtypeStruct((N_HEADS // 2, 8, s), F32),
                   jax.ShapeDtypeStruct((N_HEADS // 2, 8, s), F32)],
        compiler_params=_params(("parallel", "arbitrary")),
    )(ki_tab, qi_tab, h, h, h, ck, o, lse, do)


def _conv3(z, z_prev, w_ref):
    return (w_ref[2:3, :] * z + w_ref[1:2, :] * _shift_down(z, z_prev, 1)
            + w_ref[0:1, :] * _shift_down(z, z_prev, 2))


def _sconv_fwd(h, w, name):
    s = h.shape[0]
    t = _tile(s, 512)
    r = t // HALO
    c = D_CONV
    b_bg, b_cg, b_hc = OFF_BG // c, OFF_CG // c, OFF_HC // c

    def body(bg_ref, cg_ref, hc_ref, cgp_ref, hcp_ref, w_ref, y_ref):
        i = pl.program_id(0)
        live = (i > 0).astype(F32)
        z = cg_ref[...].astype(F32) * hc_ref[...].astype(F32)
        zp = cgp_ref[...].astype(F32) * hcp_ref[...].astype(F32) * live
        y_ref[...] = (bg_ref[...].astype(F32) * _conv3(z, zp, w_ref)).astype(y_ref.dtype)

    cur = lambda b: pl.BlockSpec((t, c), lambda i: (i, b))
    prev = lambda b: pl.BlockSpec((HALO, c), lambda i: (jnp.maximum(i * r - 1, 0), b))
    return pl.pallas_call(
        body, name=name, grid=(s // t,),
        in_specs=[cur(b_bg), cur(b_cg), cur(b_hc), prev(b_cg), prev(b_hc), pl.BlockSpec((8, c), lambda i: (0, 0))],
        out_specs=pl.BlockSpec((t, c), lambda i: (i, 0)),
        out_shape=jax.ShapeDtypeStruct((s, c), BF16),
        compiler_params=_params(("parallel",)),
    )(h, h, h, h, h, w)


def _sconv_bwd(h, w, dy, name):
    s = h.shape[0]
    t = _tile(s, 512)
    n = s // t
    r = t // HALO
    nh = s // HALO
    c = D_CONV
    b_bg, b_cg, b_hc = OFF_BG // c, OFF_CG // c, OFF_HC // c

    def body(bg_ref, cg_ref, hc_ref, cgp_ref, hcp_ref, bgn_ref, dy_ref, dyn_ref, w_ref, d_ref, dw_ref, acc_ref):
        i = pl.program_id(0)
        has_prev = (i > 0).astype(F32)
        has_next = (i < n - 1).astype(F32)
        bg = bg_ref[...].astype(F32)
        cg = cg_ref[...].astype(F32)
        hc = hc_ref[...].astype(F32)
        dyv = dy_ref[...].astype(F32)
        z = cg * hc
        zp = cgp_ref[...].astype(F32) * hcp_ref[...].astype(F32) * has_prev
        z1 = _shift_down(z, zp, 1)
        z2 = _shift_down(z, zp, 2)
        cz = w_ref[2:3, :] * z + w_ref[1:2, :] * z1 + w_ref[0:1, :] * z2
        dcz = dyv * bg
        dczn = dyn_ref[...].astype(F32) * bgn_ref[...].astype(F32) * has_next
        dz = (w_ref[2:3, :] * dcz + w_ref[1:2, :] * _shift_up(dcz, dczn, 1)
              + w_ref[0:1, :] * _shift_up(dcz, dczn, 2))
        d_ref[:, 0:c] = (dyv * cz).astype(d_ref.dtype)
        d_ref[:, c:2 * c] = (dz * hc).astype(d_ref.dtype)
        d_ref[:, 2 * c:3 * c] = (dz * cg).astype(d_ref.dtype)

        @pl.when(i == 0)
        def _():
            acc_ref[...] = jnp.zeros_like(acc_ref)

        acc_ref[0] += _row_sum8(dcz * z2)
        acc_ref[1] += _row_sum8(dcz * z1)
        acc_ref[2] += _row_sum8(dcz * z)

        @pl.when(i == n - 1)
        def _():
            rows = [jnp.sum(acc_ref[k], axis=0, keepdims=True) for k in range(3)]
            dw_ref[...] = jnp.concatenate(rows + [jnp.zeros((5, c), F32)], axis=0)

    cur = lambda b: pl.BlockSpec((t, c), lambda i: (i, b))
    prev = lambda b: pl.BlockSpec((HALO, c), lambda i: (jnp.maximum(i * r - 1, 0), b))
    nxt = lambda b: pl.BlockSpec((HALO, c), lambda i: (jnp.minimum((i + 1) * r, nh - 1), b))
    return pl.pallas_call(
        body, name=name, grid=(n,),
        in_specs=[cur(b_bg), cur(b_cg), cur(b_hc), prev(b_cg), prev(b_hc), nxt(b_bg),
                  cur(0), nxt(0), pl.BlockSpec((8, c), lambda i: (0, 0))],
        out_specs=[pl.BlockSpec((t, 3 * c), lambda i: (i, 0)), pl.BlockSpec((8, c), lambda i: (0, 0))],
        out_shape=[jax.ShapeDtypeStruct((s, 3 * c), BF16), jax.ShapeDtypeStruct((8, c), F32)],
        scratch_shapes=[pltpu.VMEM((3, 8, c), F32)],
        compiler_params=_params(("arbitrary",)),
    )(h, h, h, h, h, h, dy, dy, w)


def _group_masks():
    lane = lax.broadcasted_iota(jnp.int32, (1, D_SGU), 1)
    return [(lane >= g * HEAD_DIM) & (lane < (g + 1) * HEAD_DIM) for g in range(N_GROUPS)]


def _tril_weights(w_ref):
    r = lax.broadcasted_iota(jnp.int32, (CHUNK, CHUNK), 0)
    c = lax.broadcasted_iota(jnp.int32, (CHUNK, CHUNK), 1)
    return [jnp.where(r >= c, w_ref[g], 0.0).astype(MXU_DTYPE) for g in range(N_GROUPS)]


def _sgu_ln(vs, g_ref, b_ref):
    vg, dvg = _gelu_and_grad(vs)
    mu = jnp.mean(vg, axis=-1, keepdims=True)
    xc = vg - mu
    rstd = lax.rsqrt(jnp.mean(xc * xc, axis=-1, keepdims=True) + LN_EPS)
    xhat = xc * rstd
    return xhat * g_ref[...] + b_ref[...], xhat, rstd, dvg


def _sgu_fwd(h, ln_g, ln_b, w_s, bias, name):
    s = h.shape[0]
    t = _tile(s, 512)
    c = D_SGU
    b_u, b_v = OFF_U // c, OFF_VS // c

    def body(u_ref, v_ref, g_ref, b_ref, w_ref, bias_ref, y_ref):
        gm = _group_masks()
        wm = _tril_weights(w_ref)
        ug = _gelu(u_ref[...].astype(F32))
        vn, _, _, _ = _sgu_ln(v_ref[...].astype(F32), g_ref, b_ref)
        vnb = vn.astype(MXU_DTYPE)
        for ch in range(t // CHUNK):
            rows = slice(ch * CHUNK, (ch + 1) * CHUNK)
            mixed = bias_ref[...]
            for g in range(N_GROUPS):
                mg = lax.dot_general(wm[g], vnb[rows], _DIMS["nn"], preferred_element_type=F32)
                mixed = jnp.where(gm[g], mixed + mg, mixed)
            y_ref[rows, :] = (ug[rows] * mixed).astype(y_ref.dtype)

    full = lambda shp: pl.BlockSpec(shp, lambda i: (0,) * len(shp))
    return pl.pallas_call(
        body, name=name, grid=(s // t,),
        in_specs=[pl.BlockSpec((t, c), lambda i: (i, b_u)), pl.BlockSpec((t, c), lambda i: (i, b_v)),
                  full((1, c)), full((1, c)), full((N_GROUPS, CHUNK, CHUNK)), full((CHUNK, c))],
        out_specs=pl.BlockSpec((t, c), lambda i: (i, 0)),
        out_shape=jax.ShapeDtypeStruct((s, c), BF16),
        compiler_params=_params(("parallel",)),
    )(h, h, ln_g, ln_b, w_s, bias)


def _sgu_bwd(h, ln_g, ln_b, w_s, bias, dy, name):
    s = h.shape[0]
    t = _tile(s, 512)
    n = s // t
    c = D_SGU
    b_u, b_v = OFF_U // c, OFF_VS // c

    def body(u_ref, v_ref, g_ref, b_ref, w_ref, bias_ref, dy_ref,
             d_ref, dg_ref, db_ref, dw_ref, dbias_ref, dg_acc, db_acc):
        i = pl.program_id(0)
        gm = _group_masks()
        wm = _tril_weights(w_ref)

        @pl.when(i == 0)
        def _():
            dg_acc[...] = jnp.zeros_like(dg_acc)
            db_acc[...] = jnp.zeros_like(db_acc)
            dw_ref[...] = jnp.zeros_like(dw_ref)
            dbias_ref[...] = jnp.zeros_like(dbias_ref)

        ug, dug = _gelu_and_grad(u_ref[...].astype(F32))
        vn, xhat, rstd, dvg = _sgu_ln(v_ref[...].astype(F32), g_ref, b_ref)
        vnb = vn.astype(MXU_DTYPE)
        dyv = dy_ref[...].astype(F32)
        dmixed = dyv * ug
        dmb = dmixed.astype(MXU_DTYPE)
        dvn_parts = []
        for ch in range(t // CHUNK):
            rows = slice(ch * CHUNK, (ch + 1) * CHUNK)
            mixed = bias_ref[...]
            dvn = jnp.zeros((CHUNK, c), F32)
            for g in range(N_GROUPS):
                mg = lax.dot_general(wm[g], vnb[rows], _DIMS["nn"], preferred_element_type=F32)
                mixed = jnp.where(gm[g], mixed + mg, mixed)
                dvn = jnp.where(gm[g], lax.dot_general(wm[g], dmb[rows], _DIMS["tn"], preferred_element_type=F32),
                                dvn)
                dmg = jnp.where(gm[g], dmb[rows], jnp.zeros_like(dmb[rows]))
                dw_ref[g] += lax.dot_general(dmg, vnb[rows], _DIMS["nt"], preferred_element_type=F32)
            d_ref[rows, 0:c] = (dyv[rows] * mixed * dug[rows]).astype(d_ref.dtype)
            dbias_ref[...] += dmixed[rows]
            dvn_parts.append(dvn)
        dvn = jnp.concatenate(dvn_parts, axis=0)
        dg_acc[...] += _row_sum8(dvn * xhat)
        db_acc[...] += _row_sum8(dvn)
        dxh = dvn * g_ref[...]
        dvgl = rstd * (dxh - jnp.mean(dxh, axis=-1, keepdims=True)
                       - xhat * jnp.mean(dxh * xhat, axis=-1, keepdims=True))
        d_ref[:, c:2 * c] = (dvgl * dvg).astype(d_ref.dtype)

        @pl.when(i == n - 1)
        def _():
            dg_ref[...] = jnp.sum(dg_acc[...], axis=0, keepdims=True)
            db_ref[...] = jnp.sum(db_acc[...], axis=0, keepdims=True)
            r = lax.broadcasted_iota(jnp.int32, (CHUNK, CHUNK), 0)
            cc = lax.broadcasted_iota(jnp.int32, (CHUNK, CHUNK), 1)
            for g in range(N_GROUPS):
                dw_ref[g] = jnp.where(r >= cc, dw_ref[g], 0.0)

    full = lambda shp: pl.BlockSpec(shp, lambda i: (0,) * len(shp))
    return pl.pallas_call(
        body, name=name, grid=(n,),
        in_specs=[pl.BlockSpec((t, c), lambda i: (i, b_u)), pl.BlockSpec((t, c), lambda i: (i, b_v)),
                  full((1, c)), full((1, c)), full((N_GROUPS, CHUNK, CHUNK)), full((CHUNK, c)),
                  pl.BlockSpec((t, c), lambda i: (i, 0))],
        out_specs=[pl.BlockSpec((t, 2 * c), lambda i: (i, 0)), full((1, c)), full((1, c)),
                   full((N_GROUPS, CHUNK, CHUNK)), full((CHUNK, c))],
        out_shape=[jax.ShapeDtypeStruct((s, 2 * c), BF16), jax.ShapeDtypeStruct((1, c), F32),
                   jax.ShapeDtypeStruct((1, c), F32), jax.ShapeDtypeStruct((N_GROUPS, CHUNK, CHUNK), F32),
                   jax.ShapeDtypeStruct((CHUNK, c), F32)],
        scratch_shapes=[pltpu.VMEM((8, c), F32), pltpu.VMEM((8, c), F32)],
        compiler_params=_params(("arbitrary",)),
    )(h, h, ln_g, ln_b, w_s, bias, dy)


def _merge_fwd(h, acts, ws, b_gate, name):
    s = h.shape[0]
    d = D_MODEL
    t = _tile(s, 512)

    def body(gl0, gl1, gl2, a0, a1, a2, w0, w1, w2, b_ref, o_ref):
        acc = jnp.zeros((t, d), F32)
        for i, (gl, a, w) in enumerate(((gl0, a0, w0), (gl1, a1, w1), (gl2, a2, w2))):
            y = lax.dot_general(a[...], w[...], _DIMS["nn"], preferred_element_type=F32)
            acc = acc + _sigmoid(gl[...].astype(F32) + b_ref[i:i + 1, :]) * y
        o_ref[...] = acc.astype(o_ref.dtype)

    full = lambda arr: pl.BlockSpec(arr.shape, lambda i: (0, 0))
    return pl.pallas_call(
        body, name=name, grid=(s // t,),
        in_specs=[pl.BlockSpec((t, d), lambda i, b=b: (i, b)) for b in range(3)]
                 + [pl.BlockSpec((t, a.shape[1]), lambda i: (i, 0)) for a in acts]
                 + [full(w) for w in ws] + [full(b_gate)],
        out_specs=pl.BlockSpec((t, d), lambda i: (i, 0)),
        out_shape=jax.ShapeDtypeStruct((s, d), BF16),
        compiler_params=_params(("parallel",)),
    )(h, h, h, *acts, *ws, b_gate)


def _merge_bwd(h, acts, ws, b_gate, dmerged, name):
    s = h.shape[0]
    d = D_MODEL
    t = _tile(s, 512)
    n = s // t

    def body(gl0, gl1, gl2, a0, a1, a2, w0, w1, w2, b_ref, dm_ref, dy0, dy1, dy2, dgl_ref, db_ref, acc_ref):
        step = pl.program_id(0)

        @pl.when(step == 0)
        def _():
            acc_ref[...] = jnp.zeros_like(acc_ref)

        dm = dm_ref[...]
        for i, (gl, a, w, dy) in enumerate(((gl0, a0, w0, dy0), (gl1, a1, w1, dy1), (gl2, a2, w2, dy2))):
            y = lax.dot_general(a[...], w[...], _DIMS["nn"], preferred_element_type=F32)
            gate = _sigmoid(gl[...].astype(F32) + b_ref[i:i + 1, :])
            dy[...] = (dm * gate).astype(dy.dtype)
            dgl = dm * y * (gate * (1.0 - gate))
            dgl_ref[:, i * d:(i + 1) * d] = dgl.astype(dgl_ref.dtype)
            acc_ref[i] += _row_sum8(dgl)

        @pl.when(step == n - 1)
        def _():
            rows = [jnp.sum(acc_ref[k], axis=0, keepdims=True) for k in range(3)]
            db_ref[...] = jnp.concatenate(rows + [jnp.zeros((5, d), F32)], axis=0)

    full = lambda arr: pl.BlockSpec(arr.shape, lambda i: (0, 0))
    row = pl.BlockSpec((t, d), lambda i: (i, 0))
    return pl.pallas_call(
        body, name=name, grid=(n,),
        in_specs=[pl.BlockSpec((t, d), lambda i, b=b: (i, b)) for b in range(3)]
                 + [pl.BlockSpec((t, a.shape[1]), lambda i: (i, 0)) for a in acts]
                 + [full(w) for w in ws] + [full(b_gate), row],
        out_specs=[row, row, row, pl.BlockSpec((t, 3 * d), lambda i: (i, 0)), pl.BlockSpec((8, d), lambda i: (0, 0))],
        out_shape=[jax.ShapeDtypeStruct((s, d), BF16)] * 3
                  + [jax.ShapeDtypeStruct((s, IN_PAD), BF16), jax.ShapeDtypeStruct((8, d), F32)],
        scratch_shapes=[pltpu.VMEM((3, 8, d), F32)],
        compiler_params=_params(("arbitrary",)),
    )(h, h, h, *acts, *ws, b_gate, dmerged)


FF_BLK = D_FF // 2


def _ffn_act_fwd(h2, w, name):
    s = h2.shape[0]
    t = _tile(s, 512)
    r = t // HALO
    cw = 2 * FF_BLK

    def body(x_ref, xp_ref, w_ref, p_ref):
        i = pl.program_id(0)
        live = (i > 0).astype(F32)
        hc = _conv3(x_ref[...].astype(F32), xp_ref[...].astype(F32) * live, w_ref)
        p_ref[...] = (_gelu(hc[:, :FF_BLK]) * hc[:, FF_BLK:]).astype(p_ref.dtype)

    return pl.pallas_call(
        body, name=name, grid=(s // t, 2),
        in_specs=[pl.BlockSpec((t, cw), lambda i, j: (i, j)),
                  pl.BlockSpec((HALO, cw), lambda i, j: (jnp.maximum(i * r - 1, 0), j)),
                  pl.BlockSpec((8, cw), lambda i, j: (0, j))],
        out_specs=pl.BlockSpec((t, FF_BLK), lambda i, j: (i, j)),
        out_shape=jax.ShapeDtypeStruct((s, D_FF), BF16),
        compiler_params=_params(("parallel", "parallel")),
    )(h2, h2, w)


def _ffn_act_conv_bwd(h2, w, dp, name):
    s = h2.shape[0]
    t = _tile(s, 512)
    n = s // t
    r = t // HALO
    nh = s // HALO
    cw = 2 * FF_BLK

    def body(x_ref, xp_ref, xn_ref, dp_ref, dpn_ref, w_ref, dx_ref, dw_ref, acc_ref):
        i = pl.program_id(1)
        has_prev = (i > 0).astype(F32)
        has_next = (i < n - 1).astype(F32)
        x = jnp.concatenate([x_ref[...].astype(F32), xn_ref[...].astype(F32)], axis=0)
        xp = xp_ref[...].astype(F32) * has_prev
        x1 = _shift_down(x, xp, 1)
        x2 = _shift_down(x, xp, 2)
        hc = w_ref[2:3, :] * x + w_ref[1:2, :] * x1 + w_ref[0:1, :] * x2
        ga, dga = _gelu_and_grad(hc[:, :FF_BLK])
        dpv = jnp.concatenate([dp_ref[...].astype(F32), dpn_ref[...].astype(F32) * has_next], axis=0)
        dhc = jnp.concatenate([dpv * hc[:, FF_BLK:] * dga, dpv * ga], axis=1)
        cur, nxt = dhc[:t], dhc[t:]
        dx = w_ref[2:3, :] * cur + w_ref[1:2, :] * _shift_up(cur, nxt, 1) + w_ref[0:1, :] * _shift_up(cur, nxt, 2)
        dx_ref[...] = dx.astype(dx_ref.dtype)

        @pl.when(i == 0)
        def _():
            acc_ref[...] = jnp.zeros_like(acc_ref)

        acc_ref[0] += _row_sum8(cur * x2[:t])
        acc_ref[1] += _row_sum8(cur * x1[:t])
        acc_ref[2] += _row_sum8(cur * x[:t])

        @pl.when(i == n - 1)
        def _():
            rows = [jnp.sum(acc_ref[k], axis=0, keepdims=True) for k in range(3)]
            dw_ref[...] = jnp.concatenate(rows + [jnp.zeros((5, cw), F32)], axis=0)

    nxt_row = lambda j, i: jnp.minimum((i + 1) * r, nh - 1)
    return pl.pallas_call(
        body, name=name, grid=(2, n),
        in_specs=[pl.BlockSpec((t, cw), lambda j, i: (i, j)),
                  pl.BlockSpec((HALO, cw), lambda j, i: (jnp.maximum(i * r - 1, 0), j)),
                  pl.BlockSpec((HALO, cw), lambda j, i: (nxt_row(j, i), j)),
                  pl.BlockSpec((t, FF_BLK), lambda j, i: (i, j)),
                  pl.BlockSpec((HALO, FF_BLK), lambda j, i: (nxt_row(j, i), j)),
                  pl.BlockSpec((8, cw), lambda j, i: (0, j))],
        out_specs=[pl.BlockSpec((t, cw), lambda j, i: (i, j)), pl.BlockSpec((8, cw), lambda j, i: (0, j))],
        out_shape=[jax.ShapeDtypeStruct((s, 2 * D_FF), BF16), jax.ShapeDtypeStruct((8, 2 * D_FF), F32)],
        scratch_shapes=[pltpu.VMEM((3, 8, cw), F32)],
        compiler_params=_params(("parallel", "arbitrary")),
    )(h2, h2, h2, dp, dp, w)


def _adamw(w, g, m, v, name, dep=None):
    shape = w.shape
    c = shape[-1]
    rows = math.prod(shape[:-1])
    to2d = lambda a: a.reshape(rows, c)
    cap = max(8, (1 << 18) // c)
    tr = rows
    for cand in (2048, 1024, 512, 256, 128, 64, 32, 16, 8):
        if cand <= cap and rows % cand == 0:
            tr = cand
            break

    deps = [] if dep is None else [dep]

    def body(w_ref, g_ref, m_ref, v_ref, *rest):
        d_ref, nm_ref, nv_ref = rest[len(deps):]
        gv = g_ref[...]
        nm = ADAM_B1 * m_ref[...] + (1.0 - ADAM_B1) * gv
        nv = ADAM_B2 * v_ref[...] + (1.0 - ADAM_B2) * (gv * gv)
        m_hat = nm / (1.0 - ADAM_B1 ** ADAM_STEP)
        v_hat = nv / (1.0 - ADAM_B2 ** ADAM_STEP)
        d_ref[...] = -ADAM_LR * (m_hat / (jnp.sqrt(v_hat) + ADAM_EPS) + ADAM_WD * w_ref[...])
        nm_ref[...] = nm
        nv_ref[...] = nv

    blk = pl.BlockSpec((tr, c), lambda i: (i, 0))
    outs = pl.pallas_call(
        body, name=name, grid=(rows // tr,),
        in_specs=[blk] * 4 + [_DEP] * len(deps), out_specs=[blk] * 3,
        out_shape=[jax.ShapeDtypeStruct((rows, c), F32)] * 3,
        compiler_params=_params(("parallel",)),
    )(to2d(w), to2d(g), to2d(m), to2d(v), *deps)
    return tuple(o.reshape(shape) for o in outs)


_ANY = pl.BlockSpec(memory_space=pl.ANY)


def _place():
    x, y, c = lax.axis_index("x"), lax.axis_index("y"), lax.axis_index("c")
    others = [(1 - x, y), (x, 1 - y), (1 - x, 1 - y)]
    return x, y, c, others


def _all_gather_chips(shard, name):
    rws, cols = shard.shape
    half = rws // 2

    def body(x_ref, out_ref, send_sems, recv_sems, local_sem):
        x, y, c, others = _place()
        me = 2 * x + y
        sib = (x, y, 1 - c)

        def rows(chip, cc):
            return out_ref.at[chip, pl.ds(pl.multiple_of(cc * half, 16), half), :]

        def copy(k, src, dst, to):
            return pltpu.make_async_remote_copy(src_ref=src, dst_ref=dst, send_sem=send_sems.at[k],
                                                recv_sem=recv_sems.at[k], device_id=to, device_id_type=MESH)

        mine = pltpu.make_async_copy(x_ref, out_ref.at[me], local_sem)
        mine.start()
        my_half = x_ref.at[pl.ds(pl.multiple_of(c * half, 16), half), :]
        first = [copy(j, my_half, rows(me, c), (ox, oy, c)) for j, (ox, oy) in enumerate(others)]
        for cp in first:
            cp.start()
        passed = []
        for j, (ox, oy) in enumerate(others):
            blk = rows(2 * ox + oy, c)
            copy(j, blk, blk, (x, y, c)).wait_recv()
            fwd = copy(3 + j, blk, blk, sib)
            fwd.start()
            passed.append(fwd)
        for j, (ox, oy) in enumerate(others):
            blk = rows(2 * ox + oy, 1 - c)
            copy(3 + j, blk, blk, (x, y, c)).wait_recv()
        for cp in first + passed:
            cp.wait_send()
        mine.wait()

    return pl.pallas_call(
        body, name=name,
        in_specs=[_ANY], out_specs=_ANY,
        out_shape=jax.ShapeDtypeStruct((N_CHIPS, rws, cols), shard.dtype),
        scratch_shapes=[pltpu.SemaphoreType.DMA((6,)), pltpu.SemaphoreType.DMA((6,)), pltpu.SemaphoreType.DMA],
        compiler_params=pltpu.CompilerParams(has_side_effects=True),
    )(shard)


def _swap_halves(buf, name, dep=None):
    nb, rws, cols = buf.shape
    half = rws // 2
    deps = [] if dep is None else [dep]

    def body(b_ref, *rest):
        own_ref, sib_ref, send_sem, recv_sem, local_sem = rest[len(deps):]
        x, y, c, _ = _place()
        keep = b_ref.at[:, pl.ds(pl.multiple_of(c * half, 16), half), :]
        give = b_ref.at[:, pl.ds(pl.multiple_of((1 - c) * half, 16), half), :]
        mine = pltpu.make_async_copy(keep, own_ref, local_sem)
        mine.start()
        cp = pltpu.make_async_remote_copy(src_ref=give, dst_ref=sib_ref, send_sem=send_sem, recv_sem=recv_sem,
                                          device_id=(x, y, 1 - c), device_id_type=MESH)
        cp.start()
        cp.wait()
        mine.wait()

    shp = jax.ShapeDtypeStruct((nb, half, cols), buf.dtype)
    return pl.pallas_call(
        body, name=name,
        in_specs=[_ANY] * (1 + len(deps)), out_specs=[_ANY, _ANY], out_shape=[shp, shp],
        scratch_shapes=[pltpu.SemaphoreType.DMA, pltpu.SemaphoreType.DMA, pltpu.SemaphoreType.DMA],
        compiler_params=pltpu.CompilerParams(has_side_effects=True),
    )(buf, *deps)


def _add2(a, b, name):
    nb, rws, cols = a.shape
    t = _tile(rws, 256)
    if rws % t:
        t = rws

    def body(a_ref, b_ref, o_ref):
        o_ref[...] = (a_ref[...].astype(F32) + b_ref[...].astype(F32)).astype(o_ref.dtype)

    blk = pl.BlockSpec((1, t, cols), lambda i, j: (i, j, 0))
    return pl.pallas_call(
        body, name=name, grid=(nb, rws // t), in_specs=[blk, blk], out_specs=blk,
        out_shape=jax.ShapeDtypeStruct(a.shape, a.dtype),
        compiler_params=_params(("parallel", "parallel")),
    )(a, b)


def _exchange_chips(pre, name):
    nb, half, cols = pre.shape

    def body(p_ref, out_ref, send_sems, recv_sems, local_sem):
        x, y, c, others = _place()
        me = 2 * x + y
        mine = pltpu.make_async_copy(p_ref.at[me], out_ref.at[me], local_sem)
        mine.start()
        sends = []
        for j, (ox, oy) in enumerate(others):
            cp = pltpu.make_async_remote_copy(src_ref=p_ref.at[2 * ox + oy], dst_ref=out_ref.at[me],
                                              send_sem=send_sems.at[j], recv_sem=recv_sems.at[j],
                                              device_id=(ox, oy, c), device_id_type=MESH)
            cp.start()
            sends.append(cp)
        for j, (ox, oy) in enumerate(others):
            blk = out_ref.at[2 * ox + oy]
            pltpu.make_async_remote_copy(src_ref=blk, dst_ref=blk, send_sem=send_sems.at[j],
                                         recv_sem=recv_sems.at[j], device_id=(x, y, c),
                                         device_id_type=MESH).wait_recv()
        for cp in sends:
            cp.wait_send()
        mine.wait()

    return pl.pallas_call(
        body, name=name,
        in_specs=[_ANY], out_specs=_ANY, out_shape=jax.ShapeDtypeStruct(pre.shape, pre.dtype),
        scratch_shapes=[pltpu.SemaphoreType.DMA((3,)), pltpu.SemaphoreType.DMA((3,)), pltpu.SemaphoreType.DMA],
        compiler_params=pltpu.CompilerParams(has_side_effects=True),
    )(pre)


def _add4(parts, name):
    nb, half, cols = parts.shape
    t = _tile(half, 256)
    if half % t:
        t = half

    def body(p_ref, o_ref):
        acc = p_ref[0].astype(F32)
        for k in range(1, nb):
            acc = acc + p_ref[k].astype(F32)
        o_ref[...] = acc

    return pl.pallas_call(
        body, name=name, grid=(half // t,),
        in_specs=[pl.BlockSpec((nb, t, cols), lambda i: (0, i, 0))],
        out_specs=pl.BlockSpec((t, cols), lambda i: (i, 0)),
        out_shape=jax.ShapeDtypeStruct((half, cols), F32),
        compiler_params=_params(("parallel",)),
    )(parts)


def _join_halves(mine_half, name):
    half, cols = mine_half.shape

    def body(h_ref, out_ref, send_sem, recv_sem, local_sem):
        x, y, c, _ = _place()
        dst = out_ref.at[pl.ds(pl.multiple_of(c * half, 8), half), :]
        mine = pltpu.make_async_copy(h_ref, dst, local_sem)
        mine.start()
        cp = pltpu.make_async_remote_copy(src_ref=h_ref, dst_ref=dst, send_sem=send_sem, recv_sem=recv_sem,
                                          device_id=(x, y, 1 - c), device_id_type=MESH)
        cp.start()
        cp.wait()
        mine.wait()

    return pl.pallas_call(
        body, name=name,
        in_specs=[_ANY], out_specs=_ANY, out_shape=jax.ShapeDtypeStruct((2 * half, cols), mine_half.dtype),
        scratch_shapes=[pltpu.SemaphoreType.DMA, pltpu.SemaphoreType.DMA, pltpu.SemaphoreType.DMA],
        compiler_params=pltpu.CompilerParams(has_side_effects=True),
    )(mine_half)


def _reduce_scatter_chips(buf, tag, dep=None):
    own, sib = _swap_halves(buf, "rs_swap_" + tag, dep)
    pre = _add2(own, sib, "rs_add2_" + tag)
    parts = _exchange_chips(pre, "rs_xchg_" + tag)
    red = _add4(parts, "rs_add4_" + tag)
    return _join_halves(red, "rs_join_" + tag)


MAX_DMA_BYTES = 2 * 1024 * 1024
ROW_ALIGN = 16


def _pieces(rows, row_bytes):
    n = max(1, -(-(rows * row_bytes) // MAX_DMA_BYTES))
    step = -(-(-(-rows // n)) // ROW_ALIGN) * ROW_ALIGN
    return [(r, min(step, rows - r)) for r in range(0, rows, step)]


def _half_plan(arrays, row_axis):
    plan = []
    for a, arr in enumerate(arrays):
        row_bytes = math.prod(arr.shape[row_axis + 1:]) * arr.dtype.itemsize * (arr.shape[0] if row_axis else 1)
        plan += [(a, r0, nr) for r0, nr in _pieces(arr.shape[row_axis] // 2, row_bytes)]
    return plan


def _rows(start, size):
    return pl.ds(pl.multiple_of(start, ROW_ALIGN), size)


def _remote(src, dst, send_sems, recv_sems, k, to):
    return pltpu.make_async_remote_copy(src_ref=src, dst_ref=dst, send_sem=send_sems.at[k], recv_sem=recv_sems.at[k],
                                        device_id=to, device_id_type=MESH)


def _comm_call(body, name, ins, out_shapes, n_remote, n_local, aliases=None):
    return pl.pallas_call(
        body, name=name,
        in_specs=[_ANY] * len(ins), out_specs=[_ANY] * len(out_shapes), out_shape=out_shapes,
        scratch_shapes=[pltpu.SemaphoreType.DMA((n_remote,)), pltpu.SemaphoreType.DMA((n_remote,)),
                        pltpu.SemaphoreType.DMA((max(n_local, 1),))],
        input_output_aliases=aliases or {},
        compiler_params=pltpu.CompilerParams(has_side_effects=True),
    )(*ins)


def _cast_shard(w, l, me_idx, name):
    _, k, cols = w.shape
    tr = _tile(k, 256)
    if k % tr:
        tr = k

    def body(me_ref, w_ref, s_ref, land_ref):
        del me_ref
        v = w_ref[...].astype(BF16)
        s_ref[...] = v
        land_ref[...] = v

    grid_spec = pltpu.PrefetchScalarGridSpec(
        num_scalar_prefetch=1, grid=(k // tr,),
        in_specs=[pl.BlockSpec((None, tr, cols), lambda i, me: (l, i, 0))],
        out_specs=[pl.BlockSpec((tr, cols), lambda i, me: (i, 0)),
                   pl.BlockSpec((None, tr, cols), lambda i, me: (me[0], i, 0))])
    return pl.pallas_call(
        body, name=name, grid_spec=grid_spec,
        out_shape=[jax.ShapeDtypeStruct((k, cols), BF16), jax.ShapeDtypeStruct((N_CHIPS, k, cols), BF16)],
        compiler_params=_params(("parallel",)),
    )(me_idx, w)


def _gather_d2d(lands, name):
    n = len(lands)
    plan = _half_plan(lands, 1)
    plan = [(a, r0, nr) for a, r0, nr in plan]

    def body(*refs):
        out_refs = refs[n:2 * n]
        send_sems, recv_sems, _ = refs[2 * n:]
        x, y, c, others = _place()
        sends = []
        for i, (a, r0, nr) in enumerate(plan):
            rows = _rows(c * (lands[a].shape[1] // 2) + r0, nr)
            for j, (ox, oy) in enumerate(others):
                blk = out_refs[a].at[2 * ox + oy, rows, :]
                cp = _remote(blk, blk, send_sems, recv_sems, 3 * i + j, (x, y, 1 - c))
                cp.start()
                sends.append(cp)
        for i, (a, r0, nr) in enumerate(plan):
            rows = _rows((1 - c) * (lands[a].shape[1] // 2) + r0, nr)
            for j, (ox, oy) in enumerate(others):
                blk = out_refs[a].at[2 * ox + oy, rows, :]
                _remote(blk, blk, send_sems, recv_sems, 3 * i + j, (x, y, c)).wait_recv()
        for cp in sends:
            cp.wait_send()

    outs = [jax.ShapeDtypeStruct(a.shape, a.dtype) for a in lands]
    return _comm_call(body, name, lands, outs, 3 * len(plan), 0, aliases={a: a for a in range(n)})


def _rs_swap(ts, name):
    n = len(ts)
    plan = _half_plan(ts, 1)

    def body(*refs):
        t_refs, out_refs = refs[:n], refs[n:2 * n]
        send_sems, recv_sems, _ = refs[2 * n:]
        x, y, c, _o = _place()
        sends = []
        for i, (a, r0, nr) in enumerate(plan):
            src = t_refs[a].at[:, _rows((1 - c) * (ts[a].shape[1] // 2) + r0, nr), :]
            cp = _remote(src, out_refs[a].at[:, pl.ds(r0, nr), :], send_sems, recv_sems, i, (x, y, 1 - c))
            cp.start()
            sends.append(cp)
        for i, (a, r0, nr) in enumerate(plan):
            blk = out_refs[a].at[:, pl.ds(r0, nr), :]
            _remote(blk, blk, send_sems, recv_sems, i, (x, y, c)).wait_recv()
        for cp in sends:
            cp.wait_send()

    outs = [jax.ShapeDtypeStruct((t.shape[0], t.shape[1] // 2, t.shape[2]), t.dtype) for t in ts]
    return _comm_call(body, name, ts, outs, len(plan), 0)


def _add_halves(ts, gots, c_idx, me_idx, name):
    n = len(ts)

    def body(c_ref, me_ref, *refs):
        del c_ref
        t_refs, g_refs = refs[:n], refs[n:2 * n]
        o_refs, mine_refs = refs[2 * n:3 * n], refs[3 * n:]
        for t_ref, g_ref, o_ref, mine_ref in zip(t_refs, g_refs, o_refs, mine_refs):
            v = (t_ref[...].astype(F32) + g_ref[...].astype(F32)).astype(o_ref.dtype)
            o_ref[...] = v

            @pl.when(pl.program_id(0) == me_ref[0])
            def _():
                mine_ref[...] = v

    blks = [(1, g.shape[1], g.shape[2]) for g in gots]
    same = [pl.BlockSpec(b, lambda i, c, me: (i, 0, 0)) for b in blks]
    grid_spec = pltpu.PrefetchScalarGridSpec(
        num_scalar_prefetch=2, grid=(N_CHIPS,),
        in_specs=[pl.BlockSpec(b, lambda i, c, me: (i, c[0], 0)) for b in blks] + same,
        out_specs=same + [pl.BlockSpec(b, lambda i, c, me: (me[0], 0, 0)) for b in blks])
    shapes = [jax.ShapeDtypeStruct(g.shape, g.dtype) for g in gots]
    outs = pl.pallas_call(
        body, name=name, grid_spec=grid_spec, out_shape=shapes + shapes,
        compiler_params=_params(("arbitrary",)),
    )(c_idx, me_idx, *ts, *gots)
    return outs[:n], outs[n:]


def _add4_halves(parts, c_idx, name):
    n = len(parts)
    steps = 2

    def body(c_ref, *refs):
        del c_ref
        for p_ref, o_ref in zip(refs[:n], refs[n:]):
            acc = p_ref[0].astype(F32)
            for k in range(1, N_CHIPS):
                acc = acc + p_ref[k].astype(F32)
            o_ref[...] = acc

    grid_spec = pltpu.PrefetchScalarGridSpec(
        num_scalar_prefetch=1, grid=(steps,),
        in_specs=[pl.BlockSpec((N_CHIPS, p.shape[1] // steps, p.shape[2]), lambda i, c: (0, i, 0)) for p in parts],
        out_specs=[pl.BlockSpec((p.shape[1] // steps, p.shape[2]), lambda i, c: (c[0] * steps + i, 0))
                   for p in parts])
    return pl.pallas_call(
        body, name=name, grid_spec=grid_spec,
        out_shape=[jax.ShapeDtypeStruct((2 * p.shape[1], p.shape[2]), F32) for p in parts],
        compiler_params=_params(("parallel",)),
    )(c_idx, *parts)


def _rs_join(fulls, name):
    n = len(fulls)
    plan = _half_plan(fulls, 0)

    def body(*refs):
        out_refs = refs[n:2 * n]
        send_sems, recv_sems, _ = refs[2 * n:]
        x, y, c, _o = _place()
        sends = []
        for i, (a, r0, nr) in enumerate(plan):
            blk = out_refs[a].at[_rows(c * (fulls[a].shape[0] // 2) + r0, nr), :]
            cp = _remote(blk, blk, send_sems, recv_sems, i, (x, y, 1 - c))
            cp.start()
            sends.append(cp)
        for i, (a, r0, nr) in enumerate(plan):
            blk = out_refs[a].at[_rows((1 - c) * (fulls[a].shape[0] // 2) + r0, nr), :]
            _remote(blk, blk, send_sems, recv_sems, i, (x, y, c)).wait_recv()
        for cp in sends:
            cp.wait_send()

    outs = [jax.ShapeDtypeStruct(f.shape, f.dtype) for f in fulls]
    return _comm_call(body, name, fulls, outs, len(plan), 0, aliases={a: a for a in range(n)})


_HBM = pl.BlockSpec(memory_space=pltpu.HBM)
_SEM = pl.BlockSpec(memory_space=pltpu.SEMAPHORE)
_EFFECT = pltpu.SideEffectType.DATAFLOW_SIDE_EFFECTING


def _ici_plan(kind, a_list):
    if kind == "gather":
        return _half_plan(a_list, 0)
    plan = []
    for a, p in enumerate(a_list):
        plan += [(a, r0, nr) for r0, nr in _pieces(p.shape[1], p.shape[2] * p.dtype.itemsize)]
    return plan


def _ici_refs(kind, a_ref, b_ref, a_shape, r0, nr, c, me, peer):
    if kind == "gather":
        rows = _rows(c * (a_shape[0] // 2) + r0, nr)
        return a_ref.at[rows, :], b_ref.at[me, rows, :], b_ref.at[peer, rows, :]
    rows = pl.ds(r0, nr)
    return a_ref.at[peer, rows, :], b_ref.at[me, rows, :], b_ref.at[peer, rows, :]


def _ici_start(kind, a_list, b_list, name):
    n = len(a_list)
    plan = _ici_plan(kind, a_list)
    shapes = [a.shape for a in a_list]

    def body(*refs):
        a_refs, b_refs = refs[:n], refs[n:2 * n]
        send_sems, recv_sems = refs[2 * n], refs[2 * n + 1]
        token = refs[4 * n + 2]
        x, y, c, others = _place()
        me = 2 * x + y
        for i, (a, r0, nr) in enumerate(plan):
            for j, (ox, oy) in enumerate(others):
                src, dst, _ = _ici_refs(kind, a_refs[a], b_refs[a], shapes[a], r0, nr, c, me, 2 * ox + oy)
                _remote(src, dst, send_sems, recv_sems, 3 * i + j, (ox, oy, c)).start()
        token[...] = jnp.zeros_like(token)

    hbm = lambda v: pltpu.HBM(v.shape, v.dtype)
    ncp = 3 * len(plan)
    outs = pl.pallas_call(
        body, name=name,
        in_specs=[_HBM] * (2 * n),
        out_specs=[_SEM, _SEM] + [_HBM] * (2 * n) + [pl.BlockSpec(memory_space=pltpu.VMEM)],
        out_shape=[pltpu.SemaphoreType.DMA((ncp,)), pltpu.SemaphoreType.DMA((ncp,))]
                  + [hbm(v) for v in a_list] + [hbm(v) for v in b_list] + [jax.ShapeDtypeStruct((8, LANES), F32)],
        input_output_aliases={i: 2 + i for i in range(2 * n)},
        compiler_params=pltpu.CompilerParams(has_side_effects=_EFFECT),
    )(*[pltpu.with_memory_space_constraint(v, pltpu.HBM) for v in list(a_list) + list(b_list)])
    return outs[0], outs[1], outs[2:2 + n], outs[2 + n:2 + 2 * n], outs[2 + 2 * n]


def _ici_wait(kind, started, after, name):
    send_sems, recv_sems, a_list, b_list, _ = started
    afters = list(after) if isinstance(after, (list, tuple)) else [after]
    n = len(a_list)
    plan = _ici_plan(kind, a_list)
    shapes = [a.shape for a in a_list]

    def body(*refs):
        a_refs, b_refs = refs[:n], refs[n:2 * n]
        send_sems, recv_sems = refs[2 * n], refs[2 * n + 1]
        x, y, c, others = _place()
        me = 2 * x + y
        for i, (a, r0, nr) in enumerate(plan):
            for j, (ox, oy) in enumerate(others):
                src, dst, land = _ici_refs(kind, a_refs[a], b_refs[a], shapes[a], r0, nr, c, me, 2 * ox + oy)
                _remote(src, dst, send_sems, recv_sems, 3 * i + j, (ox, oy, c)).wait_send()
                _remote(land, land, send_sems, recv_sems, 3 * i + j, (x, y, c)).wait_recv()

    hbm = lambda v: pltpu.HBM(v.shape, v.dtype)
    outs = pl.pallas_call(
        body, name=name,
        in_specs=[_HBM] * (2 * n) + [_SEM, _SEM] + [_ANY] * len(afters),
        out_specs=[_HBM] * (2 * n),
        out_shape=[hbm(v) for v in a_list] + [hbm(v) for v in b_list],
        input_output_aliases={i: i for i in range(2 * n)},
        compiler_params=pltpu.CompilerParams(has_side_effects=_EFFECT),
    )(*a_list, *b_list, send_sems, recv_sems, *afters)
    return outs[n:]


def _rs_begin(ts, c_idx, me_idx, tag):
    got = _rs_swap(ts, "rs_swap_" + tag)
    pres, mine = _add_halves(ts, got, c_idx, me_idx, "rs_add2_" + tag)
    return _ici_start("scatter", pres, mine, "rs_xchg_start_" + tag)


def _rs_finish(started, after, c_idx, tag):
    parts = _ici_wait("scatter", started, after, "rs_xchg_wait_" + tag)
    return _rs_join(_add4_halves(parts, c_idx, "rs_add4_" + tag), "rs_join_" + tag)


def _pack_rows(pieces, rows, dtype):
    flat = jnp.concatenate([p.astype(dtype).reshape(-1) for p in pieces])
    return jnp.pad(flat, (0, rows * PACK_COLS - flat.shape[0])).reshape(rows, PACK_COLS)


def _unpack(flat, shapes):
    out, off = [], 0
    for shp in shapes:
        size = math.prod(shp)
        out.append(flat[off:off + size].reshape(shp))
        off += size
    return out


def _rows_for(n_elems, mult):
    rows = -(-n_elems // PACK_COLS)
    return -(-rows // mult) * mult


BIG_SHARDS = [("w_in", (D_MODEL, 1474)), ("w_branch_att", (D_ATT, 256)), ("w_branch_conv", (D_CONV, 256)),
              ("w_branch_sgu", (D_SGU, 256)), ("w_out", (256, D_MODEL)), ("w_ffn_up", (D_MODEL, FF_BLK)),
              ("w_ffn_down", (D_FF // N_CHIPS, D_MODEL))]
SMALL_SHARDS = [("b_gate", (3, 256)), ("conv_mix_w", (3, 64)), ("conv_ffn_w", (3, FF_BLK))]
REPLICATED = [("pre_mix_g", (D_MODEL,)), ("post_mix_g", (D_MODEL,)), ("pre_ffn_g", (D_MODEL,)),
              ("post_ffn_g", (D_MODEL,)), ("b_forget", (N_HEADS,)), ("sgu_ln_g", (D_SGU,)), ("sgu_ln_b", (D_SGU,)),
              ("sgu_w", (N_GROUPS, CHUNK, CHUNK)), ("sgu_b", (N_GROUPS, CHUNK))]
WEIGHT_ORDER = ["pre_mix_g", "post_mix_g", "pre_ffn_g", "post_ffn_g", "w_in", "b_forget", "b_gate", "conv_mix_w",
                "sgu_ln_g", "sgu_ln_b", "sgu_w", "sgu_b", "w_branch_att", "w_branch_conv", "w_branch_sgu", "w_out",
                "w_ffn_up", "conv_ffn_w", "w_ffn_down"]

_SMALL_ELEMS = sum(math.prod(s) for _, s in SMALL_SHARDS)
_REP_ELEMS = sum(math.prod(s) for _, s in REPLICATED)
_REP_QUARTER = -(-(DEPTH * _REP_ELEMS) // N_CHIPS)
SMALL_PARAM_ROWS = _rows_for(DEPTH * _SMALL_ELEMS, 32)
SMALL_ROWS = _rows_for(DEPTH * _SMALL_ELEMS + _REP_QUARTER, 32)
IN_WIDTH = 5896
IN_SHARD = IN_WIDTH // N_CHIPS
IN_SHARD_PAD = 1536
IN_PAD = 6144


def _gather_small(wts):
    shard = _pack_rows([wts[n] for n, _ in SMALL_SHARDS], SMALL_PARAM_ROWS, F32)
    full = _all_gather_chips(shard, "gather_small_params").reshape(N_CHIPS, -1)
    per_chip = [_unpack(full[j], [(DEPTH,) + s for _, s in SMALL_SHARDS]) for j in range(N_CHIPS)]
    return {n: jnp.concatenate([per_chip[j][i] for j in range(N_CHIPS)], axis=-1)
            for i, (n, _) in enumerate(SMALL_SHARDS)}


BIG_NAMES = [n for n, _ in BIG_SHARDS]
FIRST_NAMES = ["w_in"]
LATE_NAMES = BIG_NAMES[1:]


def _gather_begin(wts, l, me_idx, names, tag):
    cast = [_cast_shard(wts[n], l, me_idx, "cast_" + n) for n in names]
    return _ici_start("gather", [sh for sh, _ in cast], [ld for _, ld in cast], "gather_ici_start_" + tag)


def _gather_finish(started, after, names, tag):
    lands = _ici_wait("gather", started, after, "gather_ici_wait_" + tag)
    return dict(zip(names, _gather_d2d(lands, "gather_d2d_" + tag)))


def _pad_rows(a, rows):
    return jnp.pad(a, ((0, rows - a.shape[0]), (0, 0)))


def _whole_cols(land):
    return land.transpose(1, 0, 2).reshape(land.shape[1], -1)


_O_F = 3 * D_ATT
_O_B = _O_F + N_HEADS
_O_GL = _O_B + 3 * D_CONV + 2 * D_SGU


_LOCAL_ORDER = [(_O_GL, IN_WIDTH), (0, _O_F), (_O_B, _O_GL), (_O_F, _O_B)]


def _own_cols(land, lo, hi):
    pieces = []
    for j in range(N_CHIPS):
        a, b = max(lo, j * IN_SHARD), min(hi, (j + 1) * IN_SHARD)
        if a < b:
            pieces.append(land[j][:, a - j * IN_SHARD:b - j * IN_SHARD])
    return pieces


def _local_cols(m, lo, hi):
    pieces, off = [], 0
    for a, b in _LOCAL_ORDER:
        x, y = max(lo, a), min(hi, b)
        if x < y:
            pieces.append((x, m[:, off + x - a:off + y - a]))
        off += b - a
    pieces = [p for _, p in sorted(pieces, key=lambda t: t[0])]
    if hi > IN_WIDTH:
        pieces.append(jnp.zeros((m.shape[0], hi - max(lo, IN_WIDTH)), m.dtype))
    return pieces


def _prep_first(wts, lands, small, l):
    land = lands["w_in"]
    cf = small["conv_ffn_w"][l]
    blk = lambda a, j: a[:, j * FF_BLK:(j + 1) * FF_BLK]
    local = [piece for lo, hi in _LOCAL_ORDER for piece in _own_cols(land, lo, hi)]
    return {
        "w_p": jnp.concatenate(local + [jnp.zeros((D_MODEL, IN_PAD - IN_WIDTH), BF16)], axis=1),
        "wf_t": _pad_rows(jnp.concatenate(_own_cols(land, _O_F, _O_B), axis=1).T, F_ROWS),
        "b_forget": _pad_rows(wts["b_forget"][l].reshape(N_HEADS, 1), F_ROWS),
        "b_gate": _pad_rows(small["b_gate"][l], 8),
        "conv_mix_w": _pad_rows(small["conv_mix_w"][l], 8),
        "conv_ffn_w": _pad_rows(jnp.concatenate([blk(cf, 0), blk(cf, 2), blk(cf, 1), blk(cf, 3)], axis=1), 8),
        "pre_mix_g": wts["pre_mix_g"][l].reshape(1, -1), "post_mix_g": wts["post_mix_g"][l].reshape(1, -1),
        "pre_ffn_g": wts["pre_ffn_g"][l].reshape(1, -1), "post_ffn_g": wts["post_ffn_g"][l].reshape(1, -1),
        "ln_g": wts["sgu_ln_g"][l].reshape(1, -1), "ln_b": wts["sgu_ln_b"][l].reshape(1, -1),
        "sgu_w": wts["sgu_w"][l],
        "sgu_bias": jnp.repeat(wts["sgu_b"][l].T, HEAD_DIM, axis=1),
    }


def _prep_late(lands):
    up = lands["w_ffn_up"]
    return {
        "w_att": _whole_cols(lands["w_branch_att"]), "w_conv": _whole_cols(lands["w_branch_conv"]),
        "w_sgu": _whole_cols(lands["w_branch_sgu"]),
        "w_out": lands["w_out"].reshape(D_MODEL, D_MODEL),
        "w_up": jnp.concatenate([up[0], up[2], up[1], up[3]], axis=1),
        "w_down": lands["w_ffn_down"].reshape(D_FF, D_MODEL),
    }


def _layer_fwd(x, p, dep=None, late=None):
    s = x.shape[0]
    xn = _rms_fwd(x, p["pre_mix_g"], "rms_pre_mix", dep)
    h = _mm(xn, p["w_p"], "nn", BF16, "mm_in", s, 512, D_MODEL)
    f_row = _mm(p["wf_t"], xn, "nt", F32, "mm_forget", F_ROWS, 2048, D_MODEL)
    ck = _gate_fwd(f_row, p["b_forget"], "gate_fwd")
    o, o_f32, lse = _attn_fwd(h, ck, "attn_fwd")
    yc = _sconv_fwd(h, p["conv_mix_w"], "sconv_fwd")
    ys = _sgu_fwd(h, p["ln_g"], p["ln_b"], p["sgu_w"], p["sgu_bias"], "sgu_fwd")
    if late is not None:
        p.update(late(o))
    merged = _merge_fwd(h, (o, yc, ys), (p["w_att"], p["w_conv"], p["w_sgu"]), p["b_gate"], "merge_fwd")
    mo = _mm(merged, p["w_out"], "nn", F32, "mm_out", 2048, 512, D_MODEL)
    x1, xn2 = _resid_post_norm(x, mo, p["post_mix_g"], p["pre_ffn_g"], "post_mix")
    h2 = _mm(xn2, p["w_up"], "nn", BF16, "mm_up", 2048, 512, D_MODEL)
    pact = _ffn_act_fwd(h2, p["conv_ffn_w"], "ffn_act_fwd")
    ff = _mm(pact, p["w_down"], "nn", F32, "mm_down", 1024, D_MODEL, D_FF)
    x2 = _resid_post(x1, ff, p["post_ffn_g"], "post_ffn")
    saved = dict(x=x, xn=xn, h=h, f_row=f_row, ck=ck, o=o, o_f32=o_f32, lse=lse, yc=yc, ys=ys, merged=merged, mo=mo, x1=x1,
                 xn2=xn2, h2=h2, pact=pact, ff=ff)
    return x2, saved


def _layer_bwd(dx2, p, sv, dep=None, early=None):
    s = dx2.shape[0]
    g = {}
    same = lambda b: b
    dff, g["post_ffn_g"] = _rms_bwd(sv["ff"], p["post_ffn_g"], [dx2], None, BF16, "post_ffn_bwd", dep)
    dpact = _mm(dff, p["w_down"], "nt", BF16, "mm_down_dx", 2048, FF_BLK, D_MODEL)
    t_down = _mm(sv["pact"], dff, "tn", BF16, "mm_down_dw", 256, D_MODEL, s).reshape(N_CHIPS, -1, D_MODEL)
    dh2, dconv_ffn = _ffn_act_conv_bwd(sv["h2"], p["conv_ffn_w"], dpact, "ffn_act_conv_bwd")
    dxn2 = _mm(dh2, p["w_up"], "nt", F32, "mm_up_dx", 512, D_MODEL, 2 * D_FF)
    t_up = _mm(sv["xn2"], dh2, "tn", BF16, "mm_up_dw", 512, FF_BLK, s, chip_of=lambda b: (b % 2) * 2 + b // 2)
    dx1, g["pre_ffn_g"] = _rms_bwd(sv["x1"], p["pre_ffn_g"], [dxn2], dx2, F32, "pre_ffn_bwd")
    dep_mix = early([t_up, t_down]) if early is not None else None
    dmo, g["post_mix_g"] = _rms_bwd(sv["mo"], p["post_mix_g"], [dx1], None, BF16, "post_mix_bwd", dep_mix)
    dmerged = _mm(dmo, p["w_out"], "nt", F32, "mm_out_dx", 2048, 512, D_MODEL)
    t_out = _mm(sv["merged"], dmo, "tn", BF16, "mm_out_dw", 512, D_MODEL, s).reshape(N_CHIPS, -1, D_MODEL)
    acts = (sv["o"], sv["yc"], sv["ys"])
    ws = (p["w_att"], p["w_conv"], p["w_sgu"])
    dy_a, dy_c, dy_s, dgl, db_gate = _merge_bwd(sv["h"], acts, ws, p["b_gate"], dmerged, "merge_bwd")
    do = _mm(dy_a, p["w_att"], "nt", BF16, "mm_att_dx", 2048, D_ATT, D_MODEL)
    dyc = _mm(dy_c, p["w_conv"], "nt", BF16, "mm_conv_dx", 2048, D_CONV, D_MODEL)
    dys = _mm(dy_s, p["w_sgu"], "nt", BF16, "mm_sgu_dx", 2048, D_SGU, D_MODEL)
    t_att = _mm(sv["o"], dy_a, "tn", BF16, "mm_att_dw", D_ATT, 256, s, chip_of=same)
    t_conv = _mm(sv["yc"], dy_c, "tn", BF16, "mm_conv_dw", D_CONV, 256, s, chip_of=same)
    t_sgu = _mm(sv["ys"], dy_s, "tn", BF16, "mm_sgu_dw", D_SGU, 256, s, chip_of=same)
    d_conv, dconv_mix = _sconv_bwd(sv["h"], p["conv_mix_w"], dyc, "sconv_bwd")
    d_sgu, g["sgu_ln_g"], g["sgu_ln_b"], g["sgu_w"], dbias = _sgu_bwd(
        sv["h"], p["ln_g"], p["ln_b"], p["sgu_w"], p["sgu_bias"], dys, "sgu_bwd")
    dq, dk, dv, dc_even, dc_odd = _attn_bwd(sv["h"], sv["ck"], sv["o_f32"], sv["lse"], do, "attn_bwd")
    df, db_forget = _gate_bwd(sv["f_row"], p["b_forget"], dc_even, dc_odd, "gate_bwd")
    f_cols = jnp.concatenate([df[:N_HEADS].T, jnp.zeros((s, IN_PAD - IN_WIDTH), BF16)], axis=1)
    dh = _assemble_dh(dgl, [dq, dk, dv, d_conv, d_sgu, f_cols], "assemble_dh")
    dxn = _mm(dh, p["w_p"], "nt", F32, "mm_in_dx", 512, D_MODEL, IN_PAD)
    dw_p = _mm(sv["xn"], dh, "tn", BF16, "mm_in_dw", D_MODEL, 512, s)
    t_in = jnp.stack([jnp.concatenate(_local_cols(dw_p, j * IN_SHARD, j * IN_SHARD + IN_SHARD_PAD), axis=1)
                      for j in range(N_CHIPS)])
    dx, g["pre_mix_g"] = _rms_bwd(sv["x"], p["pre_mix_g"], [dxn], dx1, F32, "pre_mix_bwd")
    blk = lambda a, j: a[:, j * FF_BLK:(j + 1) * FF_BLK]
    g["conv_ffn_w"] = jnp.concatenate([blk(dconv_ffn, 0), blk(dconv_ffn, 2), blk(dconv_ffn, 1),
                                       blk(dconv_ffn, 3)], axis=1)[:3]
    g["conv_mix_w"] = dconv_mix[:3]
    g["b_gate"] = db_gate[:3]
    g["b_forget"] = db_forget[:N_HEADS, 0]
    g["sgu_b"] = jnp.sum(dbias.reshape(CHUNK, N_GROUPS, HEAD_DIM), axis=-1).T
    for n in ("pre_mix_g", "post_mix_g", "pre_ffn_g", "post_ffn_g", "sgu_ln_g", "sgu_ln_b"):
        g[n] = g[n].reshape(-1)
    mix = [t_in, t_att, t_conv, t_sgu, t_out]
    return dx, (mix if early is not None else mix + [t_up, t_down]), g


def _assemble_dh(dh, pieces, name):
    s = dh.shape[0]
    t = _tile(s, 512)
    width = sum(a.shape[1] for a in pieces)
    assert 2 * width == dh.shape[1]

    def body(*refs):
        out = refs[-1]
        col = 0
        for ref in refs[1:-1]:
            w = ref.shape[1]
            out[:, col:col + w] = ref[...].astype(out.dtype)
            col += w

    return pl.pallas_call(
        body, name=name, grid=(s // t,),
        in_specs=[_ANY] + [pl.BlockSpec((t, a.shape[1]), lambda i: (i, 0)) for a in pieces],
        out_specs=pl.BlockSpec((t, width), lambda i: (i, 1)),
        out_shape=jax.ShapeDtypeStruct(dh.shape, dh.dtype),
        input_output_aliases={0: 0},
        compiler_params=_params(("parallel",)),
    )(dh, *pieces)


def _shard_cols(a, j):
    w = a.shape[-1] // N_CHIPS
    return a[..., j * w:(j + 1) * w]


def kernel(x, pre_mix_g, post_mix_g, pre_ffn_g, post_ffn_g, w_in, b_forget, b_gate, conv_mix_w, sgu_ln_g, sgu_ln_b, sgu_w, sgu_b, w_branch_att, w_branch_conv, w_branch_sgu, w_out, w_ffn_up, conv_ffn_w, w_ffn_down, loss_target, m_pre_mix_g, m_post_mix_g, m_pre_ffn_g, m_post_ffn_g, m_w_in, m_b_forget, m_b_gate, m_conv_mix_w, m_sgu_ln_g, m_sgu_ln_b, m_sgu_w, m_sgu_b, m_w_branch_att, m_w_branch_conv, m_w_branch_sgu, m_w_out, m_w_ffn_up, m_conv_ffn_w, m_w_ffn_down, v_pre_mix_g, v_post_mix_g, v_pre_ffn_g, v_post_ffn_g, v_w_in, v_b_forget, v_b_gate, v_conv_mix_w, v_sgu_ln_g, v_sgu_ln_b, v_sgu_w, v_sgu_b, v_w_branch_att, v_w_branch_conv, v_w_branch_sgu, v_w_out, v_w_ffn_up, v_conv_ffn_w, v_w_ffn_down):
    wts = dict(pre_mix_g=pre_mix_g, post_mix_g=post_mix_g, pre_ffn_g=pre_ffn_g, post_ffn_g=post_ffn_g, w_in=w_in,
               b_forget=b_forget, b_gate=b_gate, conv_mix_w=conv_mix_w, sgu_ln_g=sgu_ln_g, sgu_ln_b=sgu_ln_b,
               sgu_w=sgu_w, sgu_b=sgu_b, w_branch_att=w_branch_att, w_branch_conv=w_branch_conv,
               w_branch_sgu=w_branch_sgu, w_out=w_out, w_ffn_up=w_ffn_up, conv_ffn_w=conv_ffn_w,
               w_ffn_down=w_ffn_down)
    moms = dict(pre_mix_g=m_pre_mix_g, post_mix_g=m_post_mix_g, pre_ffn_g=m_pre_ffn_g, post_ffn_g=m_post_ffn_g,
                w_in=m_w_in, b_forget=m_b_forget, b_gate=m_b_gate, conv_mix_w=m_conv_mix_w, sgu_ln_g=m_sgu_ln_g,
                sgu_ln_b=m_sgu_ln_b, sgu_w=m_sgu_w, sgu_b=m_sgu_b, w_branch_att=m_w_branch_att,
                w_branch_conv=m_w_branch_conv, w_branch_sgu=m_w_branch_sgu, w_out=m_w_out, w_ffn_up=m_w_ffn_up,
                conv_ffn_w=m_conv_ffn_w, w_ffn_down=m_w_ffn_down)
    vels = dict(pre_mix_g=v_pre_mix_g, post_mix_g=v_post_mix_g, pre_ffn_g=v_pre_ffn_g, post_ffn_g=v_post_ffn_g,
                w_in=v_w_in, b_forget=v_b_forget, b_gate=v_b_gate, conv_mix_w=v_conv_mix_w, sgu_ln_g=v_sgu_ln_g,
                sgu_ln_b=v_sgu_ln_b, sgu_w=v_sgu_w, sgu_b=v_sgu_b, w_branch_att=v_w_branch_att,
                w_branch_conv=v_w_branch_conv, w_branch_sgu=v_w_branch_sgu, w_out=v_w_out, w_ffn_up=v_w_ffn_up,
                conv_ffn_w=v_conv_ffn_w, w_ffn_down=v_w_ffn_down)

    c_idx = lax.axis_index("c").astype(jnp.int32).reshape(1)
    me_idx = (2 * lax.axis_index("x") + lax.axis_index("y")).astype(jnp.int32).reshape(1)
    small = _gather_small(wts)

    xs = x[0]
    layers, saved = [], []
    first = _gather_begin(wts, 0, me_idx, FIRST_NAMES, "first")
    rest = _gather_begin(wts, 0, me_idx, LATE_NAMES, "late")
    lands = _gather_finish(first, xs, FIRST_NAMES, "first")
    late = lambda after: _prep_late(_gather_finish(rest, after, LATE_NAMES, "late"))
    for l in range(DEPTH):
        p = _prep_first(wts, lands, small, l)
        if l > 0:
            p.update(_prep_late(lands))
        nxt = _gather_begin(wts, l + 1, me_idx, BIG_NAMES, "all") if l + 1 < DEPTH else None
        dep = ([nxt[4]] if nxt else []) + ([rest[4]] if l == 0 else [])
        xs, sv = _layer_fwd(xs, p, dep or None, late if l == 0 else None)
        if nxt:
            lands = _gather_finish(nxt, xs, BIG_NAMES, "all")
        layers.append(p)
        saved.append(sv)
    dy, loss_part = _loss_head(xs, loss_target[0], "loss_head")
    loss = lax.psum(loss_part[0, 0], ("x", "y", "c"))

    big_red = [None] * DEPTH
    small_grads = [None] * DEPTH
    pending = None
    ffn = []
    for l in reversed(range(DEPTH)):
        early = None
        if l == 0:
            def early(ts_ffn):
                ffn.append(_rs_begin(ts_ffn, c_idx, me_idx, "ffn"))
                return ffn[0][4]
        dy, ts, small_grads[l] = _layer_bwd(dy, layers[l], saved[l], pending[4] if pending else None, early)
        if pending:
            big_red[l + 1] = _rs_finish(pending, dy, c_idx, "big")
        pending = _rs_begin(ts, c_idx, me_idx, "mix" if l == 0 else "big")
    red_ffn = _rs_finish(ffn[0], dy, c_idx, "ffn")
    grad_x = dy[None]

    done = {}
    for k, n in enumerate(("w_ffn_up", "w_ffn_down")):
        i = BIG_NAMES.index(n)
        g = jnp.stack([red_ffn[k]] + [big_red[l][i] for l in range(1, DEPTH)])
        done[n] = (g,) + _adamw(wts[n], g, moms[n], vels[n], "adamw_" + n, pending[4])

    rep_flat = jnp.concatenate([small_grads[l][n].reshape(-1) for l in range(DEPTH) for n, _ in REPLICATED])
    rep_flat = jnp.pad(rep_flat, (0, N_CHIPS * _REP_QUARTER - rep_flat.shape[0]))
    rows = []
    for j in range(N_CHIPS):
        pieces = [_shard_cols(small_grads[l][n], j) for l in range(DEPTH) for n, _ in SMALL_SHARDS]
        pieces.append(rep_flat[j * _REP_QUARTER:(j + 1) * _REP_QUARTER])
        rows.append(_pack_rows(pieces, SMALL_ROWS, F32))
    small_red = _reduce_scatter_chips(jnp.stack(rows), "small", done["w_ffn_down"][1])
    small_all = _all_gather_chips(small_red, "gather_small")
    big_red[0] = _rs_finish(pending, [small_all] + [done[n][1] for n in done], c_idx, "mix") + red_ffn
    small_all = small_all.reshape(N_CHIPS, -1)

    grads = {}
    for i, (n, _) in enumerate(BIG_SHARDS):
        if n not in done:
            grads[n] = jnp.stack([big_red[l][i][:, :IN_SHARD] if n == "w_in" else big_red[l][i]
                                  for l in range(DEPTH)])
    mine_small = small_red.reshape(-1)
    parts = _unpack(mine_small, [s for _ in range(DEPTH) for _, s in SMALL_SHARDS])
    for i, (n, _) in enumerate(SMALL_SHARDS):
        grads[n] = jnp.stack([parts[l * len(SMALL_SHARDS) + i] for l in range(DEPTH)])
    off = DEPTH * _SMALL_ELEMS
    rep_all = jnp.concatenate([small_all[j, off:off + _REP_QUARTER] for j in range(N_CHIPS)])
    parts = _unpack(rep_all, [s for _ in range(DEPTH) for _, s in REPLICATED])
    for i, (n, _) in enumerate(REPLICATED):
        grads[n] = jnp.stack([parts[l * len(REPLICATED) + i] for l in range(DEPTH)])

    deltas, new_m, new_v = {}, {}, {}
    for n in WEIGHT_ORDER:
        if n in done:
            grads[n], deltas[n], new_m[n], new_v[n] = done[n]
        else:
            deltas[n], new_m[n], new_v[n] = _adamw(wts[n], grads[n], moms[n], vels[n], "adamw_" + n)
    return (loss, grad_x, *[grads[n] for n in WEIGHT_ORDER], *[deltas[n] for n in WEIGHT_ORDER],
            *[new_m[n] for n in WEIGHT_ORDER], *[new_v[n] for n in WEIGHT_ORDER])
```

```python
import functools
import math

import jax
import jax.numpy as jnp
from jax import lax
from jax.experimental import pallas as pl
from jax.experimental.pallas import tpu as pltpu

F32 = jnp.float32
BF16 = jnp.bfloat16
MXU_DTYPE = jnp.bfloat16

D_MODEL = 1024
HEAD_DIM = 64
N_HEADS = 8
D_ATT = 512
D_CONV = 256
D_SGU = 256
N_GROUPS = 4
CHUNK = 128
D_FF = 2816
DEPTH = 4
RMS_EPS = 1e-6
LN_EPS = 1e-5
N_CHIPS = 4
LANES = 128
PACK_COLS = 1024
HALO = 16

ADAM_LR = 0.001
ADAM_B1 = 0.9
ADAM_B2 = 0.999
ADAM_EPS = 1e-08
ADAM_WD = 0.01
ADAM_STEP = 10

OFF_GL = 0
OFF_Q = 3 * D_MODEL
OFF_K = OFF_Q + D_ATT
OFF_V = OFF_K + D_ATT
OFF_BG = OFF_V + D_ATT
OFF_CG = OFF_BG + D_CONV
OFF_HC = OFF_CG + D_CONV
OFF_U = OFF_HC + D_CONV
OFF_VS = OFF_U + D_SGU
W_P = OFF_VS + D_SGU
F_ROWS = 16

VMEM_LIMIT = 56 * 1024 * 1024
MESH = pl.DeviceIdType.MESH


def _params(sem=None):
    if sem is None:
        return pltpu.CompilerParams(vmem_limit_bytes=VMEM_LIMIT)
    return pltpu.CompilerParams(dimension_semantics=sem, vmem_limit_bytes=VMEM_LIMIT)


def _tile(dim, pref):
    if dim <= pref:
        return dim
    if dim % pref == 0:
        return pref
    return dim


_DIMS = {"nn": (((1,), (0,)), ((), ())), "nt": (((1,), (1,)), ((), ())), "tn": (((0,), (0,)), ((), ()))}


def _mm(a, b, mode, out_dtype, name, tm, tn, tk, chip_of=None):
    if mode == "tn":
        K, M = a.shape
    else:
        M, K = a.shape
    N = b.shape[0] if mode == "nt" else b.shape[1]
    tm, tn, tk = _tile(M, tm), _tile(N // N_CHIPS if chip_of else N, tn), _tile(K, tk)
    nk = K // tk
    dims = _DIMS[mode]

    def body(a_ref, b_ref, o_ref, *acc):
        part = lax.dot_general(a_ref[...].astype(MXU_DTYPE), b_ref[...].astype(MXU_DTYPE), dims,
                               preferred_element_type=F32)
        if nk == 1:
            o_ref[...] = part.astype(o_ref.dtype)
        else:
            acc_ref = acc[0]
            k = pl.program_id(2)

            @pl.when(k == 0)
            def _():
                acc_ref[...] = part

            @pl.when(k > 0)
            def _():
                acc_ref[...] += part

            @pl.when(k == nk - 1)
            def _():
                o_ref[...] = acc_ref[...].astype(o_ref.dtype)

    if mode == "tn":
        a_spec = pl.BlockSpec((tk, tm), lambda i, j, k: (k, i))
    else:
        a_spec = pl.BlockSpec((tm, tk), lambda i, j, k: (i, k))
    if mode == "nt":
        b_spec = pl.BlockSpec((tn, tk), lambda i, j, k: (j, k))
    else:
        b_spec = pl.BlockSpec((tk, tn), lambda i, j, k: (k, j))
    if chip_of is None:
        out_spec = pl.BlockSpec((tm, tn), lambda i, j, k: (i, j))
        out_shape = jax.ShapeDtypeStruct((M, N), out_dtype)
    else:
        per = (N // N_CHIPS) // tn
        out_spec = pl.BlockSpec((None, tm, tn), lambda i, j, k: (chip_of(j // per), i, j % per))
        out_shape = jax.ShapeDtypeStruct((N_CHIPS, M, N // N_CHIPS), out_dtype)
    return pl.pallas_call(
        body,
        name=name,
        grid=(M // tm, N // tn, nk),
        in_specs=[a_spec, b_spec],
        out_specs=out_spec,
        out_shape=out_shape,
        scratch_shapes=[pltpu.VMEM((tm, tn), F32)] if nk > 1 else [],
        compiler_params=_params(("parallel", "parallel", "arbitrary")),
    )(a, b)


_GELU_K = math.sqrt(2.0 / math.pi)
_GELU_C = 0.044715


def _gelu(x):
    t = jnp.tanh(_GELU_K * (x + _GELU_C * (x * x * x)))
    return x * (0.5 * (1.0 + t))


def _gelu_and_grad(x):
    x2 = x * x
    t = jnp.tanh(_GELU_K * (x + _GELU_C * (x2 * x)))
    cdf = 0.5 * (1.0 + t)
    dcdf = 0.5 * (1.0 - t * t) * (_GELU_K * (1.0 + 3.0 * _GELU_C * x2))
    return x * cdf, cdf + x * dcdf


def _sigmoid(x):
    return 1.0 / (1.0 + jnp.exp(-x))


def _shift_down(cur, prev, k):
    h = prev.shape[0]
    ext = jnp.concatenate([prev, cur], axis=0)
    return pltpu.roll(ext, k, 0)[h:]


def _shift_up(cur, nxt, k):
    t, h = cur.shape[0], nxt.shape[0]
    ext = jnp.concatenate([cur, nxt], axis=0)
    return pltpu.roll(ext, t + h - k, 0)[:t]


def _row_sum8(x):
    t, c = x.shape
    return jnp.sum(x.reshape(t // 8, 8, c), axis=0)


_DEP = pl.BlockSpec((8, LANES), lambda i: (0, 0))


def _rms_fwd(x, g, name, dep=None):
    s, d = x.shape
    t = _tile(s, 512)

    def body(x_ref, g_ref, *rest):
        o_ref = rest[-1]
        xv = x_ref[...]
        r = lax.rsqrt(jnp.mean(xv * xv, axis=-1, keepdims=True) + RMS_EPS)
        o_ref[...] = (xv * r * g_ref[...]).astype(o_ref.dtype)

    deps = [] if dep is None else list(dep) if isinstance(dep, (list, tuple)) else [dep]
    return pl.pallas_call(
        body, name=name, grid=(s // t,),
        in_specs=[pl.BlockSpec((t, d), lambda i: (i, 0)), pl.BlockSpec((1, d), lambda i: (0, 0))] + [_DEP] * len(deps),
        out_specs=pl.BlockSpec((t, d), lambda i: (i, 0)),
        out_shape=jax.ShapeDtypeStruct((s, d), BF16),
        compiler_params=_params(("parallel",)),
    )(x, g, *deps)


def _resid_post(x, y, g, name):
    s, d = x.shape
    t = _tile(s, 512)

    def body(x_ref, y_ref, g_ref, o_ref):
        yv = y_ref[...]
        r = lax.rsqrt(jnp.mean(yv * yv, axis=-1, keepdims=True) + RMS_EPS)
        o_ref[...] = x_ref[...] + yv * r * g_ref[...]

    row = pl.BlockSpec((t, d), lambda i: (i, 0))
    return pl.pallas_call(
        body, name=name, grid=(s // t,),
        in_specs=[row, row, pl.BlockSpec((1, d), lambda i: (0, 0))],
        out_specs=row,
        out_shape=jax.ShapeDtypeStruct((s, d), F32),
        compiler_params=_params(("parallel",)),
    )(x, y, g)


def _resid_post_norm(x, y, g, g_next, name):
    s, d = x.shape
    t = _tile(s, 512)

    def body(x_ref, y_ref, g_ref, gn_ref, o_ref, xn_ref):
        yv = y_ref[...]
        r = lax.rsqrt(jnp.mean(yv * yv, axis=-1, keepdims=True) + RMS_EPS)
        x1 = x_ref[...] + yv * r * g_ref[...]
        o_ref[...] = x1
        r1 = lax.rsqrt(jnp.mean(x1 * x1, axis=-1, keepdims=True) + RMS_EPS)
        xn_ref[...] = (x1 * r1 * gn_ref[...]).astype(xn_ref.dtype)

    row = pl.BlockSpec((t, d), lambda i: (i, 0))
    vec = pl.BlockSpec((1, d), lambda i: (0, 0))
    return pl.pallas_call(
        body, name=name, grid=(s // t,),
        in_specs=[row, row, vec, vec],
        out_specs=[row, row],
        out_shape=[jax.ShapeDtypeStruct((s, d), F32), jax.ShapeDtypeStruct((s, d), BF16)],
        compiler_params=_params(("parallel",)),
    )(x, y, g, g_next)


def _rms_bwd(xin, g, dys, dres, out_dtype, name, dep=None):
    s, d = xin.shape
    t = _tile(s, 512)
    n = s // t
    n_dy = len(dys)
    has_res = dres is not None
    deps = [] if dep is None else [dep]

    def body(*refs):
        x_ref, g_ref = refs[0], refs[1]
        dy_refs = refs[2:2 + n_dy]
        pos = 2 + n_dy
        res_ref = refs[pos] if has_res else None
        pos += (1 if has_res else 0) + len(deps)
        dx_ref, dg_ref, acc_ref = refs[pos], refs[pos + 1], refs[pos + 2]
        i = pl.program_id(0)
        xv = x_ref[...]
        dy = dy_refs[0][...].astype(F32)
        for extra in dy_refs[1:]:
            dy = dy + extra[...].astype(F32)
        r = lax.rsqrt(jnp.mean(xv * xv, axis=-1, keepdims=True) + RMS_EPS)
        u = dy * g_ref[...]
        xr = xv * r
        dx = r * (u - xr * jnp.mean(u * xr, axis=-1, keepdims=True))
        if has_res:
            dx = dx + res_ref[...]
        dx_ref[...] = dx.astype(dx_ref.dtype)
        part = _row_sum8(dy * xr)

        @pl.when(i == 0)
        def _():
            acc_ref[...] = part

        @pl.when(i > 0)
        def _():
            acc_ref[...] += part

        @pl.when(i == n - 1)
        def _():
            dg_ref[...] = jnp.sum(acc_ref[...], axis=0, keepdims=True)

    row = pl.BlockSpec((t, d), lambda i: (i, 0))
    vec = pl.BlockSpec((1, d), lambda i: (0, 0))
    ins = [xin, g, *dys] + ([dres] if has_res else []) + deps
    return pl.pallas_call(
        body, name=name, grid=(n,),
        in_specs=[row, vec] + [row] * (n_dy + (1 if has_res else 0)) + [_DEP] * len(deps),
        out_specs=[row, vec],
        out_shape=[jax.ShapeDtypeStruct((s, d), out_dtype), jax.ShapeDtypeStruct((1, d), F32)],
        scratch_shapes=[pltpu.VMEM((8, d), F32)],
        compiler_params=_params(("arbitrary",)),
    )(*ins)


def _loss_head(y, target, name):
    s, d = y.shape
    t = _tile(s, 512)
    n = s // t

    def body(y_ref, t_ref, dy_ref, loss_ref, acc_ref):
        i = pl.program_id(0)
        e = y_ref[...] - t_ref[...]
        dy_ref[...] = e * (1.0 / d)
        part = _row_sum8(e * e)

        @pl.when(i == 0)
        def _():
            acc_ref[...] = part

        @pl.when(i > 0)
        def _():
            acc_ref[...] += part

        @pl.when(i == n - 1)
        def _():
            tot = jnp.sum(jnp.sum(acc_ref[...], axis=0, keepdims=True), axis=1, keepdims=True)
            loss_ref[...] = tot * (0.5 / d)

    row = pl.BlockSpec((t, d), lambda i: (i, 0))
    return pl.pallas_call(
        body, name=name, grid=(n,),
        in_specs=[row, row],
        out_specs=[row, pl.BlockSpec((1, 1), lambda i: (0, 0))],
        out_shape=[jax.ShapeDtypeStruct((s, d), F32), jax.ShapeDtypeStruct((1, 1), F32)],
        scratch_shapes=[pltpu.VMEM((8, d), F32)],
        compiler_params=_params(("arbitrary",)),
    )(y, target)


def _split3(x):
    hi = x.astype(BF16)
    r1 = x - hi.astype(F32)
    mid = r1.astype(BF16)
    lo = (r1 - mid.astype(F32)).astype(BF16)
    return hi, mid, lo


def _tri_dot(x, tri):
    hi, mid, lo = _split3(x)
    dn = _DIMS["nn"]
    out = lax.dot_general(hi, tri, dn, preferred_element_type=F32)
    out = out + lax.dot_general(mid, tri, dn, preferred_element_type=F32)
    return out + lax.dot_general(lo, tri, dn, preferred_element_type=F32)


def _log_sigmoid(z):
    return jnp.minimum(z, 0.0) - jnp.log(1.0 + jnp.exp(-jnp.abs(z)))


def _gate_fwd(f_row, b_col, name):
    rows, s = f_row.shape
    t = _tile(s, 512)
    n = s // t

    def body(f_ref, b_ref, ck_ref, carry_ref):
        i = pl.program_id(0)

        @pl.when(i == 0)
        def _():
            carry_ref[...] = jnp.zeros_like(carry_ref)

        logf = _log_sigmoid(f_ref[...] + b_ref[...])
        r = lax.broadcasted_iota(jnp.int32, (t, t), 0)
        c = lax.broadcasted_iota(jnp.int32, (t, t), 1)
        tri = jnp.where(r <= c, 1.0, 0.0).astype(BF16)
        cs = _tri_dot(logf, tri) + carry_ref[...]
        carry_ref[...] = cs[:, t - 1:t]
        terms = [part.astype(F32) for part in _split3(-cs)]
        sub = lax.broadcasted_iota(jnp.int32, (LANES, t), 0)
        for p in range(N_HEADS // 2):
            stacked = jnp.zeros((LANES, t), F32)
            for hh in range(2):
                for j, term in enumerate(terms):
                    h = 2 * p + hh
                    stacked = jnp.where(sub == 3 * hh + j, jnp.broadcast_to(term[h:h + 1, :], (LANES, t)), stacked)
            ck_ref[p] = stacked.T.astype(ck_ref.dtype)

    return pl.pallas_call(
        body, name=name, grid=(n,),
        in_specs=[pl.BlockSpec((rows, t), lambda i: (0, i)), pl.BlockSpec((rows, 1), lambda i: (0, 0))],
        out_specs=pl.BlockSpec((N_HEADS // 2, t, LANES), lambda i: (0, i, 0)),
        out_shape=jax.ShapeDtypeStruct((N_HEADS // 2, s, LANES), BF16),
        scratch_shapes=[pltpu.VMEM((rows, 1), F32)],
        compiler_params=_params(("arbitrary",)),
    )(f_row, b_col)


def _gate_bwd(f_row, b_col, dc_even, dc_odd, name):
    rows, s = f_row.shape
    t = _tile(s, 512)
    n = s // t

    def body(f_ref, b_ref, dce_ref, dco_ref, df_ref, db_ref, carry_ref, acc_ref):
        i = pl.program_id(0)

        @pl.when(i == 0)
        def _():
            carry_ref[...] = jnp.zeros_like(carry_ref)
            acc_ref[...] = jnp.zeros_like(acc_ref)

        head = lax.broadcasted_iota(jnp.int32, (rows, t), 0)
        dcv = jnp.zeros((rows, t), F32)
        for h in range(N_HEADS):
            src = dce_ref if h % 2 == 0 else dco_ref
            dcv = jnp.where(head == h, jnp.broadcast_to(src[h // 2, 0:1, :], (rows, t)), dcv)
        r = lax.broadcasted_iota(jnp.int32, (t, t), 0)
        c = lax.broadcasted_iota(jnp.int32, (t, t), 1)
        tri = jnp.where(r >= c, 1.0, 0.0).astype(BF16)
        dlogf = _tri_dot(dcv, tri) + carry_ref[...]
        carry_ref[...] = dlogf[:, 0:1]
        z = f_ref[...] + b_ref[...]
        df = dlogf * _sigmoid(-z)
        df_ref[...] = df.astype(df_ref.dtype)
        acc_ref[...] += jnp.sum(df, axis=1, keepdims=True)

        @pl.when(i == n - 1)
        def _():
            db_ref[...] = acc_ref[...]

    rev = lambda i: (0, n - 1 - i)
    dc_spec = pl.BlockSpec((N_HEADS // 2, 8, t), lambda i: (0, 0, n - 1 - i))
    return pl.pallas_call(
        body, name=name, grid=(n,),
        in_specs=[pl.BlockSpec((rows, t), rev), pl.BlockSpec((rows, 1), lambda i: (0, 0)), dc_spec, dc_spec],
        out_specs=[pl.BlockSpec((rows, t), rev), pl.BlockSpec((rows, 1), lambda i: (0, 0))],
        out_shape=[jax.ShapeDtypeStruct((rows, s), BF16), jax.ShapeDtypeStruct((rows, 1), F32)],
        scratch_shapes=[pltpu.VMEM((rows, 1), F32), pltpu.VMEM((rows, 1), F32)],
        compiler_params=_params(("arbitrary",)),
    )(f_row, b_col, dc_even, dc_odd)


_NEG = -1e30
_SCALE = HEAD_DIM ** -0.5


def _head_masks():
    lane = lax.broadcasted_iota(jnp.int32, (1, LANES), 1)
    return [lane < HEAD_DIM, lane >= HEAD_DIM]


def _attn_fwd(h, ck, name):
    s = h.shape[0]
    t = _tile(s, 512)
    n = s // t
    qb, kb, vb = OFF_Q // LANES, OFF_K // LANES, OFF_V // LANES

    pairs = [(qi, ki) for qi in range(n) for ki in range(qi + 1)]
    qi_tab = jnp.asarray([qi for qi, _ in pairs], jnp.int32)
    ki_tab = jnp.asarray([ki for _, ki in pairs], jnp.int32)

    def body(qi_ref, ki_ref, q_ref, k_ref, v_ref, ck_ref, o_ref, of_ref, lse_ref, m_ref, l_ref, acc_ref):
        qi, ki = qi_ref[pl.program_id(1)], ki_ref[pl.program_id(1)]
        masks = _head_masks()
        lane = lax.broadcasted_iota(jnp.int32, (1, LANES), 1)

        @pl.when(ki == 0)
        def _():
            m_ref[...] = jnp.full_like(m_ref, _NEG)
            l_ref[...] = jnp.zeros_like(l_ref)
            acc_ref[...] = jnp.zeros_like(acc_ref)

        def step(diag):
            q = q_ref[...] * _SCALE
            k_aug = jnp.concatenate([k_ref[...], ck_ref[0]], axis=1)
            v = v_ref[...]
            nq = max(1, t // 256)
            wq = t // nq
            chains = [(hh, j) for hh in range(2) for j in range(nq)]
            scores = []
            for hh, j in chains:
                qs = q[j * wq:(j + 1) * wq]
                ones = jnp.where((lane >= 3 * hh) & (lane < 3 * hh + 3), 1.0, 0.0).astype(q.dtype)
                q_aug = jnp.concatenate([jnp.where(masks[hh], qs, jnp.zeros_like(qs)),
                                         jnp.broadcast_to(ones, qs.shape)], axis=1)
                scores.append(lax.dot_general(k_aug, q_aug, _DIMS["nt"], preferred_element_type=F32))
            probs = []
            for (hh, j), sc in zip(chains, scores):
                cols = slice(j * wq, (j + 1) * wq)
                if diag:
                    r = lax.broadcasted_iota(jnp.int32, (t, wq), 0)
                    cc = lax.broadcasted_iota(jnp.int32, (t, wq), 1) + j * wq
                    sc = jnp.where(r <= cc, sc, _NEG)
                m_prev = m_ref[hh, :, cols]
                m_new = jnp.maximum(m_prev, jnp.max(sc, axis=0, keepdims=True))
                alpha = jnp.exp(m_prev - m_new)
                p = jnp.exp(sc - m_new)
                l_ref[hh, :, cols] = alpha * l_ref[hh, :, cols] + jnp.sum(p, axis=0, keepdims=True)
                m_ref[hh, :, cols] = m_new
                p_hi = p.astype(MXU_DTYPE)
                p_lo = (p - p_hi.astype(F32)).astype(MXU_DTYPE)
                probs.append((alpha, p_hi, p_lo))
            for (hh, j), (alpha, p_hi, p_lo) in zip(chains, probs):
                pv = (lax.dot_general(v, p_hi, _DIMS["tn"], preferred_element_type=F32)
                      + lax.dot_general(v, p_lo, _DIMS["tn"], preferred_element_type=F32))
                rows = slice(hh * HEAD_DIM, (hh + 1) * HEAD_DIM)
                cols = slice(j * wq, (j + 1) * wq)
                acc_ref[rows, cols] = alpha * acc_ref[rows, cols] + pv[rows]

        @pl.when(ki < qi)
        def _():
            step(False)

        @pl.when(ki == qi)
        def _():
            step(True)
            inv = jnp.concatenate([jnp.broadcast_to(1.0 / l_ref[hh], (HEAD_DIM, t)) for hh in range(2)], axis=0)
            out = (acc_ref[...] * inv).T
            o_ref[...] = out.astype(o_ref.dtype)
            of_ref[...] = out
            lse = jnp.concatenate([jnp.broadcast_to(m_ref[hh] + jnp.log(l_ref[hh]), (HEAD_DIM, t))
                                   for hh in range(2)], axis=0)
            lse_ref[...] = lse.T

    grid_spec = pltpu.PrefetchScalarGridSpec(
        num_scalar_prefetch=2, grid=(N_HEADS // 2, len(pairs)),
        in_specs=[
            pl.BlockSpec((t, LANES), lambda p, i, qt, kt: (qt[i], qb + p)),
            pl.BlockSpec((t, LANES), lambda p, i, qt, kt: (kt[i], kb + p)),
            pl.BlockSpec((t, LANES), lambda p, i, qt, kt: (kt[i], vb + p)),
            pl.BlockSpec((1, t, LANES), lambda p, i, qt, kt: (p, kt[i], 0)),
        ],
        out_specs=[pl.BlockSpec((t, LANES), lambda p, i, qt, kt: (qt[i], p))] * 3,
        scratch_shapes=[pltpu.VMEM((2, 1, t), F32), pltpu.VMEM((2, 1, t), F32), pltpu.VMEM((LANES, t), F32)])
    return pl.pallas_call(
        body, name=name, grid_spec=grid_spec,
        out_shape=[jax.ShapeDtypeStruct((s, D_ATT), BF16), jax.ShapeDtypeStruct((s, D_ATT), F32),
                   jax.ShapeDtypeStruct((s, D_ATT), F32)],
        compiler_params=_params(("parallel", "arbitrary")),
    )(qi_tab, ki_tab, h, h, h, ck)


def _attn_bwd(h, ck, o, lse, do, name):
    s = h.shape[0]
    t = _tile(s, 512)
    n = s // t
    qb, kb, vb = OFF_Q // LANES, OFF_K // LANES, OFF_V // LANES

    pairs = [(ki, qi) for ki in range(n) for qi in range(ki, n)]
    ki_tab = jnp.asarray([ki for ki, _ in pairs], jnp.int32)
    qi_tab = jnp.asarray([qi for _, qi in pairs], jnp.int32)

    def body(ki_ref, qi_ref, q_ref, k_ref, v_ref, ck_ref, o_ref, lse_ref, do_ref,
             dq_ref, dk_ref, dv_ref, dc0_ref, dc1_ref, dk_acc, dv_acc, dc_acc):
        ki, qi = ki_ref[pl.program_id(1)], qi_ref[pl.program_id(1)]
        masks = _head_masks()
        lane = lax.broadcasted_iota(jnp.int32, (1, LANES), 1)

        @pl.when((ki == 0) & (qi == 0))
        def _():
            dq_ref[...] = jnp.zeros_like(dq_ref)

        @pl.when(qi == ki)
        def _():
            dk_acc[...] = jnp.zeros_like(dk_acc)
            dv_acc[...] = jnp.zeros_like(dv_acc)
            dc_acc[...] = jnp.zeros_like(dc_acc)

        def step(diag):
            q = q_ref[...] * _SCALE
            k = k_ref[...]
            v = v_ref[...]
            dov = do_ref[...]
            k_aug = jnp.concatenate([k, ck_ref[0]], axis=1)
            prod_t = (dov.astype(F32) * o_ref[...]).T
            lse_t = lse_ref[...].T
            heads = []
            for hh in range(2):
                mk = masks[hh]
                qh = jnp.where(mk, q, jnp.zeros_like(q))
                kh = jnp.where(mk, k, jnp.zeros_like(k))
                doh = jnp.where(mk, dov, jnp.zeros_like(dov))
                ones = jnp.where((lane >= 3 * hh) & (lane < 3 * hh + 3), 1.0, 0.0).astype(q.dtype)
                q_aug = jnp.concatenate([qh, jnp.broadcast_to(ones, q.shape)], axis=1)
                sc = lax.dot_general(k_aug, q_aug, _DIMS["nt"], preferred_element_type=F32)
                dp = lax.dot_general(v, doh, _DIMS["nt"], preferred_element_type=F32)
                heads.append((qh, kh, doh, sc, dp))
            grads = []
            for hh, (qh, kh, doh, sc, dp) in enumerate(heads):
                rows = slice(hh * HEAD_DIM, (hh + 1) * HEAD_DIM)
                p = jnp.exp(sc - lse_t[hh * HEAD_DIM:hh * HEAD_DIM + 1, :])
                if diag:
                    r = lax.broadcasted_iota(jnp.int32, (t, t), 0)
                    cc = lax.broadcasted_iota(jnp.int32, (t, t), 1)
                    p = jnp.where(r <= cc, p, 0.0)
                delta = jnp.sum(prod_t[rows], axis=0, keepdims=True)
                ds = p * (dp - delta)
                dc_acc[hh] = dc_acc[hh] - jnp.sum(ds, axis=1, keepdims=True)
                grads.append((ds.astype(MXU_DTYPE), p.astype(MXU_DTYPE)))
            dq_blk = jnp.zeros((t, LANES), F32)
            dv_blk = jnp.zeros((t, LANES), F32)
            dk_blk = jnp.zeros((t, LANES), F32)
            for (qh, kh, doh, _, _), (dsb, pb) in zip(heads, grads):
                dv_blk = dv_blk + lax.dot_general(pb, doh, _DIMS["nn"], preferred_element_type=F32)
                dk_blk = dk_blk + lax.dot_general(dsb, qh, _DIMS["nn"], preferred_element_type=F32)
                dq_blk = dq_blk + lax.dot_general(dsb, kh, _DIMS["tn"], preferred_element_type=F32)
            dv_acc[...] += dv_blk
            dk_acc[...] += dk_blk
            rows_q = pl.ds(pl.multiple_of(qi * t, t), t)
            dq_ref[rows_q, :] = dq_ref[rows_q, :] + dq_blk * _SCALE

        @pl.when(qi > ki)
        def _():
            step(False)

        @pl.when(qi == ki)
        def _():
            step(True)

        @pl.when(qi == n - 1)
        def _():
            dk_ref[...] = dk_acc[...].astype(dk_ref.dtype)
            dv_ref[...] = dv_acc[...].astype(dv_ref.dtype)
            dc0_ref[0] = jnp.broadcast_to(dc_acc[0], (t, LANES)).T[0:8]
            dc1_ref[0] = jnp.broadcast_to(dc_acc[1], (t, LANES)).T[0:8]

    q_blk = lambda col: pl.BlockSpec((t, LANES), lambda p, i, kt, qt: (qt[i], col(p)))
    k_blk = lambda col: pl.BlockSpec((t, LANES), lambda p, i, kt, qt: (kt[i], col(p)))
    dc_blk = pl.BlockSpec((1, 8, t), lambda p, i, kt, qt: (p, 0, kt[i]))
    grid_spec = pltpu.PrefetchScalarGridSpec(
        num_scalar_prefetch=2, grid=(N_HEADS // 2, len(pairs)),
        in_specs=[q_blk(lambda p: qb + p), k_blk(lambda p: kb + p), k_blk(lambda p: vb + p),
                  pl.BlockSpec((1, t, LANES), lambda p, i, kt, qt: (p, kt[i], 0)),
                  q_blk(lambda p: p), q_blk(lambda p: p), q_blk(lambda p: p)],
        out_specs=[pl.BlockSpec((s, LANES), lambda p, i, kt, qt: (0, p)), k_blk(lambda p: p), k_blk(lambda p: p),
                   dc_blk, dc_blk],
        scratch_shapes=[pltpu.VMEM((t, LANES), F32), pltpu.VMEM((t, LANES), F32), pltpu.VMEM((2, t, 1), F32)])
    return pl.pallas_call(
        body, name=name, grid_spec=grid_spec,
        out_shape=[jax.ShapeDtypeStruct((s, D_ATT), F32), jax.ShapeDtypeStruct((s, D_ATT), BF16),
                   jax.ShapeDtypeStruct((s, D_ATT), BF16), jax.ShapeDtypeStruct((N_HEADS // 2, 8, s), F32),
                   jax.ShapeDtypeStruct((N_HEADS // 2, 8, s), F32)],
        compiler_params=_params(("parallel", "arbitrary")),
    )(ki_tab, qi_tab, h, h, h, ck, o, lse, do)


def _conv3(z, z_prev, w_ref):
    return (w_ref[2:3, :] * z + w_ref[1:2, :] * _shift_down(z, z_prev, 1)
            + w_ref[0:1, :] * _shift_down(z, z_prev, 2))


def _sconv_fwd(h, w, name):
    s = h.shape[0]
    t = _tile(s, 512)
    r = t // HALO
    c = D_CONV
    b_bg, b_cg, b_hc = OFF_BG // c, OFF_CG // c, OFF_HC // c

    def body(bg_ref, cg_ref, hc_ref, cgp_ref, hcp_ref, w_ref, y_ref):
        i = pl.program_id(0)
        live = (i > 0).astype(F32)
        z = cg_ref[...].astype(F32) * hc_ref[...].astype(F32)
        zp = cgp_ref[...].astype(F32) * hcp_ref[...].astype(F32) * live
        y_ref[...] = (bg_ref[...].astype(F32) * _conv3(z, zp, w_ref)).astype(y_ref.dtype)

    cur = lambda b: pl.BlockSpec((t, c), lambda i: (i, b))
    prev = lambda b: pl.BlockSpec((HALO, c), lambda i: (jnp.maximum(i * r - 1, 0), b))
    return pl.pallas_call(
        body, name=name, grid=(s // t,),
        in_specs=[cur(b_bg), cur(b_cg), cur(b_hc), prev(b_cg), prev(b_hc), pl.BlockSpec((8, c), lambda i: (0, 0))],
        out_specs=pl.BlockSpec((t, c), lambda i: (i, 0)),
        out_shape=jax.ShapeDtypeStruct((s, c), BF16),
        compiler_params=_params(("parallel",)),
    )(h, h, h, h, h, w)


def _sconv_bwd(h, w, dy, name):
    s = h.shape[0]
    t = _tile(s, 512)
    n = s // t
    r = t // HALO
    nh = s // HALO
    c = D_CONV
    b_bg, b_cg, b_hc = OFF_BG // c, OFF_CG // c, OFF_HC // c

    def body(bg_ref, cg_ref, hc_ref, cgp_ref, hcp_ref, bgn_ref, dy_ref, dyn_ref, w_ref, d_ref, dw_ref, acc_ref):
        i = pl.program_id(0)
        has_prev = (i > 0).astype(F32)
        has_next = (i < n - 1).astype(F32)
        bg = bg_ref[...].astype(F32)
        cg = cg_ref[...].astype(F32)
        hc = hc_ref[...].astype(F32)
        dyv = dy_ref[...].astype(F32)
        z = cg * hc
        zp = cgp_ref[...].astype(F32) * hcp_ref[...].astype(F32) * has_prev
        z1 = _shift_down(z, zp, 1)
        z2 = _shift_down(z, zp, 2)
        cz = w_ref[2:3, :] * z + w_ref[1:2, :] * z1 + w_ref[0:1, :] * z2
        dcz = dyv * bg
        dczn = dyn_ref[...].astype(F32) * bgn_ref[...].astype(F32) * has_next
        dz = (w_ref[2:3, :] * dcz + w_ref[1:2, :] * _shift_up(dcz, dczn, 1)
              + w_ref[0:1, :] * _shift_up(dcz, dczn, 2))
        d_ref[:, 0:c] = (dyv * cz).astype(d_ref.dtype)
        d_ref[:, c:2 * c] = (dz * hc).astype(d_ref.dtype)
        d_ref[:, 2 * c:3 * c] = (dz * cg).astype(d_ref.dtype)

        @pl.when(i == 0)
        def _():
            acc_ref[...] = jnp.zeros_like(acc_ref)

        acc_ref[0] += _row_sum8(dcz * z2)
        acc_ref[1] += _row_sum8(dcz * z1)
        acc_ref[2] += _row_sum8(dcz * z)

        @pl.when(i == n - 1)
        def _():
            rows = [jnp.sum(acc_ref[k], axis=0, keepdims=True) for k in range(3)]
            dw_ref[...] = jnp.concatenate(rows + [jnp.zeros((5, c), F32)], axis=0)

    cur = lambda b: pl.BlockSpec((t, c), lambda i: (i, b))
    prev = lambda b: pl.BlockSpec((HALO, c), lambda i: (jnp.maximum(i * r - 1, 0), b))
    nxt = lambda b: pl.BlockSpec((HALO, c), lambda i: (jnp.minimum((i + 1) * r, nh - 1), b))
    return pl.pallas_call(
        body, name=name, grid=(n,),
        in_specs=[cur(b_bg), cur(b_cg), cur(b_hc), prev(b_cg), prev(b_hc), nxt(b_bg),
                  cur(0), nxt(0), pl.BlockSpec((8, c), lambda i: (0, 0))],
        out_specs=[pl.BlockSpec((t, 3 * c), lambda i: (i, 0)), pl.BlockSpec((8, c), lambda i: (0, 0))],
        out_shape=[jax.ShapeDtypeStruct((s, 3 * c), BF16), jax.ShapeDtypeStruct((8, c), F32)],
        scratch_shapes=[pltpu.VMEM((3, 8, c), F32)],
        compiler_params=_params(("arbitrary",)),
    )(h, h, h, h, h, h, dy, dy, w)


def _group_masks():
    lane = lax.broadcasted_iota(jnp.int32, (1, D_SGU), 1)
    return [(lane >= g * HEAD_DIM) & (lane < (g + 1) * HEAD_DIM) for g in range(N_GROUPS)]


def _tril_weights(w_ref):
    r = lax.broadcasted_iota(jnp.int32, (CHUNK, CHUNK), 0)
    c = lax.broadcasted_iota(jnp.int32, (CHUNK, CHUNK), 1)
    return [jnp.where(r >= c, w_ref[g], 0.0).astype(MXU_DTYPE) for g in range(N_GROUPS)]


def _sgu_ln(vs, g_ref, b_ref):
    vg, dvg = _gelu_and_grad(vs)
    mu = jnp.mean(vg, axis=-1, keepdims=True)
    xc = vg - mu
    rstd = lax.rsqrt(jnp.mean(xc * xc, axis=-1, keepdims=True) + LN_EPS)
    xhat = xc * rstd
    return xhat * g_ref[...] + b_ref[...], xhat, rstd, dvg


def _sgu_fwd(h, ln_g, ln_b, w_s, bias, name):
    s = h.shape[0]
    t = _tile(s, 512)
    c = D_SGU
    b_u, b_v = OFF_U // c, OFF_VS // c

    def body(u_ref, v_ref, g_ref, b_ref, w_ref, bias_ref, y_ref):
        gm = _group_masks()
        wm = _tril_weights(w_ref)
        ug = _gelu(u_ref[...].astype(F32))
        vn, _, _, _ = _sgu_ln(v_ref[...].astype(F32), g_ref, b_ref)
        vnb = vn.astype(MXU_DTYPE)
        for ch in range(t // CHUNK):
            rows = slice(ch * CHUNK, (ch + 1) * CHUNK)
            mixed = bias_ref[...]
            for g in range(N_GROUPS):
                mg = lax.dot_general(wm[g], vnb[rows], _DIMS["nn"], preferred_element_type=F32)
                mixed = jnp.where(gm[g], mixed + mg, mixed)
            y_ref[rows, :] = (ug[rows] * mixed).astype(y_ref.dtype)

    full = lambda shp: pl.BlockSpec(shp, lambda i: (0,) * len(shp))
    return pl.pallas_call(
        body, name=name, grid=(s // t,),
        in_specs=[pl.BlockSpec((t, c), lambda i: (i, b_u)), pl.BlockSpec((t, c), lambda i: (i, b_v)),
                  full((1, c)), full((1, c)), full((N_GROUPS, CHUNK, CHUNK)), full((CHUNK, c))],
        out_specs=pl.BlockSpec((t, c), lambda i: (i, 0)),
        out_shape=jax.ShapeDtypeStruct((s, c), BF16),
        compiler_params=_params(("parallel",)),
    )(h, h, ln_g, ln_b, w_s, bias)


def _sgu_bwd(h, ln_g, ln_b, w_s, bias, dy, name):
    s = h.shape[0]
    t = _tile(s, 512)
    n = s // t
    c = D_SGU
    b_u, b_v = OFF_U // c, OFF_VS // c

    def body(u_ref, v_ref, g_ref, b_ref, w_ref, bias_ref, dy_ref,
             d_ref, dg_ref, db_ref, dw_ref, dbias_ref, dg_acc, db_acc):
        i = pl.program_id(0)
        gm = _group_masks()
        wm = _tril_weights(w_ref)

        @pl.when(i == 0)
        def _():
            dg_acc[...] = jnp.zeros_like(dg_acc)
            db_acc[...] = jnp.zeros_like(db_acc)
            dw_ref[...] = jnp.zeros_like(dw_ref)
            dbias_ref[...] = jnp.zeros_like(dbias_ref)

        ug, dug = _gelu_and_grad(u_ref[...].astype(F32))
        vn, xhat, rstd, dvg = _sgu_ln(v_ref[...].astype(F32), g_ref, b_ref)
        vnb = vn.astype(MXU_DTYPE)
        dyv = dy_ref[...].astype(F32)
        dmixed = dyv * ug
        dmb = dmixed.astype(MXU_DTYPE)
        dvn_parts = []
        for ch in range(t // CHUNK):
            rows = slice(ch * CHUNK, (ch + 1) * CHUNK)
            mixed = bias_ref[...]
            dvn = jnp.zeros((CHUNK, c), F32)
            for g in range(N_GROUPS):
                mg = lax.dot_general(wm[g], vnb[rows], _DIMS["nn"], preferred_element_type=F32)
                mixed = jnp.where(gm[g], mixed + mg, mixed)
                dvn = jnp.where(gm[g], lax.dot_general(wm[g], dmb[rows], _DIMS["tn"], preferred_element_type=F32),
                                dvn)
                dmg = jnp.where(gm[g], dmb[rows], jnp.zeros_like(dmb[rows]))
                dw_ref[g] += lax.dot_general(dmg, vnb[rows], _DIMS["nt"], preferred_element_type=F32)
            d_ref[rows, 0:c] = (dyv[rows] * mixed * dug[rows]).astype(d_ref.dtype)
            dbias_ref[...] += dmixed[rows]
            dvn_parts.append(dvn)
        dvn = jnp.concatenate(dvn_parts, axis=0)
        dg_acc[...] += _row_sum8(dvn * xhat)
        db_acc[...] += _row_sum8(dvn)
        dxh = dvn * g_ref[...]
        dvgl = rstd * (dxh - jnp.mean(dxh, axis=-1, keepdims=True)
                       - xhat * jnp.mean(dxh * xhat, axis=-1, keepdims=True))
        d_ref[:, c:2 * c] = (dvgl * dvg).astype(d_ref.dtype)

        @pl.when(i == n - 1)
        def _():
            dg_ref[...] = jnp.sum(dg_acc[...], axis=0, keepdims=True)
            db_ref[...] = jnp.sum(db_acc[...], axis=0, keepdims=True)
            r = lax.broadcasted_iota(jnp.int32, (CHUNK, CHUNK), 0)
            cc = lax.broadcasted_iota(jnp.int32, (CHUNK, CHUNK), 1)
            for g in range(N_GROUPS):
                dw_ref[g] = jnp.where(r >= cc, dw_ref[g], 0.0)

    full = lambda shp: pl.BlockSpec(shp, lambda i: (0,) * len(shp))
    return pl.pallas_call(
        body, name=name, grid=(n,),
        in_specs=[pl.BlockSpec((t, c), lambda i: (i, b_u)), pl.BlockSpec((t, c), lambda i: (i, b_v)),
                  full((1, c)), full((1, c)), full((N_GROUPS, CHUNK, CHUNK)), full((CHUNK, c)),
                  pl.BlockSpec((t, c), lambda i: (i, 0))],
        out_specs=[pl.BlockSpec((t, 2 * c), lambda i: (i, 0)), full((1, c)), full((1, c)),
                   full((N_GROUPS, CHUNK, CHUNK)), full((CHUNK, c))],
        out_shape=[jax.ShapeDtypeStruct((s, 2 * c), BF16), jax.ShapeDtypeStruct((1, c), F32),
                   jax.ShapeDtypeStruct((1, c), F32), jax.ShapeDtypeStruct((N_GROUPS, CHUNK, CHUNK), F32),
                   jax.ShapeDtypeStruct((CHUNK, c), F32)],
        scratch_shapes=[pltpu.VMEM((8, c), F32), pltpu.VMEM((8, c), F32)],
        compiler_params=_params(("arbitrary",)),
    )(h, h, ln_g, ln_b, w_s, bias, dy)


def _merge_fwd(h, acts, ws, b_gate, name):
    s = h.shape[0]
    d = D_MODEL
    t = _tile(s, 512)

    def body(gl0, gl1, gl2, a0, a1, a2, w0, w1, w2, b_ref, o_ref):
        acc = jnp.zeros((t, d), F32)
        for i, (gl, a, w) in enumerate(((gl0, a0, w0), (gl1, a1, w1), (gl2, a2, w2))):
            y = lax.dot_general(a[...], w[...], _DIMS["nn"], preferred_element_type=F32)
            acc = acc + _sigmoid(gl[...].astype(F32) + b_ref[i:i + 1, :]) * y
        o_ref[...] = acc.astype(o_ref.dtype)

    full = lambda arr: pl.BlockSpec(arr.shape, lambda i: (0, 0))
    return pl.pallas_call(
        body, name=name, grid=(s // t,),
        in_specs=[pl.BlockSpec((t, d), lambda i, b=b: (i, b)) for b in range(3)]
                 + [pl.BlockSpec((t, a.shape[1]), lambda i: (i, 0)) for a in acts]
                 + [full(w) for w in ws] + [full(b_gate)],
        out_specs=pl.BlockSpec((t, d), lambda i: (i, 0)),
        out_shape=jax.ShapeDtypeStruct((s, d), BF16),
        compiler_params=_params(("parallel",)),
    )(h, h, h, *acts, *ws, b_gate)


def _merge_bwd(h, acts, ws, b_gate, dmerged, name):
    s = h.shape[0]
    d = D_MODEL
    t = _tile(s, 512)
    n = s // t

    def body(gl0, gl1, gl2, a0, a1, a2, w0, w1, w2, b_ref, dm_ref, dy0, dy1, dy2, dgl_ref, db_ref, acc_ref):
        step = pl.program_id(0)

        @pl.when(step == 0)
        def _():
            acc_ref[...] = jnp.zeros_like(acc_ref)

        dm = dm_ref[...]
        for i, (gl, a, w, dy) in enumerate(((gl0, a0, w0, dy0), (gl1, a1, w1, dy1), (gl2, a2, w2, dy2))):
            y = lax.dot_general(a[...], w[...], _DIMS["nn"], preferred_element_type=F32)
            gate = _sigmoid(gl[...].astype(F32) + b_ref[i:i + 1, :])
            dy[...] = (dm * gate).astype(dy.dtype)
            dgl = dm * y * (gate * (1.0 - gate))
            dgl_ref[:, i * d:(i + 1) * d] = dgl.astype(dgl_ref.dtype)
            acc_ref[i] += _row_sum8(dgl)

        @pl.when(step == n - 1)
        def _():
            rows = [jnp.sum(acc_ref[k], axis=0, keepdims=True) for k in range(3)]
            db_ref[...] = jnp.concatenate(rows + [jnp.zeros((5, d), F32)], axis=0)

    full = lambda arr: pl.BlockSpec(arr.shape, lambda i: (0, 0))
    row = pl.BlockSpec((t, d), lambda i: (i, 0))
    return pl.pallas_call(
        body, name=name, grid=(n,),
        in_specs=[pl.BlockSpec((t, d), lambda i, b=b: (i, b)) for b in range(3)]
                 + [pl.BlockSpec((t, a.shape[1]), lambda i: (i, 0)) for a in acts]
                 + [full(w) for w in ws] + [full(b_gate), row],
        out_specs=[row, row, row, pl.BlockSpec((t, 3 * d), lambda i: (i, 0)), pl.BlockSpec((8, d), lambda i: (0, 0))],
        out_shape=[jax.ShapeDtypeStruct((s, d), BF16)] * 3
                  + [jax.ShapeDtypeStruct((s, IN_PAD), BF16), jax.ShapeDtypeStruct((8, d), F32)],
        scratch_shapes=[pltpu.VMEM((3, 8, d), F32)],
        compiler_params=_params(("arbitrary",)),
    )(h, h, h, *acts, *ws, b_gate, dmerged)


FF_BLK = D_FF // 2


def _ffn_act_fwd(h2, w, name):
    s = h2.shape[0]
    t = _tile(s, 512)
    r = t // HALO
    cw = 2 * FF_BLK

    def body(x_ref, xp_ref, w_ref, p_ref):
        i = pl.program_id(0)
        live = (i > 0).astype(F32)
        hc = _conv3(x_ref[...].astype(F32), xp_ref[...].astype(F32) * live, w_ref)
        p_ref[...] = (_gelu(hc[:, :FF_BLK]) * hc[:, FF_BLK:]).astype(p_ref.dtype)

    return pl.pallas_call(
        body, name=name, grid=(s // t, 2),
        in_specs=[pl.BlockSpec((t, cw), lambda i, j: (i, j)),
                  pl.BlockSpec((HALO, cw), lambda i, j: (jnp.maximum(i * r - 1, 0), j)),
                  pl.BlockSpec((8, cw), lambda i, j: (0, j))],
        out_specs=pl.BlockSpec((t, FF_BLK), lambda i, j: (i, j)),
        out_shape=jax.ShapeDtypeStruct((s, D_FF), BF16),
        compiler_params=_params(("parallel", "parallel")),
    )(h2, h2, w)


def _ffn_act_conv_bwd(h2, w, dp, name):
    s = h2.shape[0]
    t = _tile(s, 512)
    n = s // t
    r = t // HALO
    nh = s // HALO
    cw = 2 * FF_BLK

    def body(x_ref, xp_ref, xn_ref, dp_ref, dpn_ref, w_ref, dx_ref, dw_ref, acc_ref):
        i = pl.program_id(1)
        has_prev = (i > 0).astype(F32)
        has_next = (i < n - 1).astype(F32)
        x = jnp.concatenate([x_ref[...].astype(F32), xn_ref[...].astype(F32)], axis=0)
        xp = xp_ref[...].astype(F32) * has_prev
        x1 = _shift_down(x, xp, 1)
        x2 = _shift_down(x, xp, 2)
        hc = w_ref[2:3, :] * x + w_ref[1:2, :] * x1 + w_ref[0:1, :] * x2
        ga, dga = _gelu_and_grad(hc[:, :FF_BLK])
        dpv = jnp.concatenate([dp_ref[...].astype(F32), dpn_ref[...].astype(F32) * has_next], axis=0)
        dhc = jnp.concatenate([dpv * hc[:, FF_BLK:] * dga, dpv * ga], axis=1)
        cur, nxt = dhc[:t], dhc[t:]
        dx = w_ref[2:3, :] * cur + w_ref[1:2, :] * _shift_up(cur, nxt, 1) + w_ref[0:1, :] * _shift_up(cur, nxt, 2)
        dx_ref[...] = dx.astype(dx_ref.dtype)

        @pl.when(i == 0)
        def _():
            acc_ref[...] = jnp.zeros_like(acc_ref)

        acc_ref[0] += _row_sum8(cur * x2[:t])
        acc_ref[1] += _row_sum8(cur * x1[:t])
        acc_ref[2] += _row_sum8(cur * x[:t])

        @pl.when(i == n - 1)
        def _():
            rows = [jnp.sum(acc_ref[k], axis=0, keepdims=True) for k in range(3)]
            dw_ref[...] = jnp.concatenate(rows + [jnp.zeros((5, cw), F32)], axis=0)

    nxt_row = lambda j, i: jnp.minimum((i + 1) * r, nh - 1)
    return pl.pallas_call(
        body, name=name, grid=(2, n),
        in_specs=[pl.BlockSpec((t, cw), lambda j, i: (i, j)),
                  pl.BlockSpec((HALO, cw), lambda j, i: (jnp.maximum(i * r - 1, 0), j)),
                  pl.BlockSpec((HALO, cw), lambda j, i: (nxt_row(j, i), j)),
                  pl.BlockSpec((t, FF_BLK), lambda j, i: (i, j)),
                  pl.BlockSpec((HALO, FF_BLK), lambda j, i: (nxt_row(j, i), j)),
                  pl.BlockSpec((8, cw), lambda j, i: (0, j))],
        out_specs=[pl.BlockSpec((t, cw), lambda j, i: (i, j)), pl.BlockSpec((8, cw), lambda j, i: (0, j))],
        out_shape=[jax.ShapeDtypeStruct((s, 2 * D_FF), BF16), jax.ShapeDtypeStruct((8, 2 * D_FF), F32)],
        scratch_shapes=[pltpu.VMEM((3, 8, cw), F32)],
        compiler_params=_params(("parallel", "arbitrary")),
    )(h2, h2, h2, dp, dp, w)


def _adamw(w, g, m, v, name, dep=None):
    shape = w.shape
    c = shape[-1]
    rows = math.prod(shape[:-1])
    to2d = lambda a: a.reshape(rows, c)
    cap = max(8, (1 << 18) // c)
    tr = rows
    for cand in (2048, 1024, 512, 256, 128, 64, 32, 16, 8):
        if cand <= cap and rows % cand == 0:
            tr = cand
            break

    deps = [] if dep is None else [dep]

    def body(w_ref, g_ref, m_ref, v_ref, *rest):
        d_ref, nm_ref, nv_ref = rest[len(deps):]
        gv = g_ref[...]
        nm = ADAM_B1 * m_ref[...] + (1.0 - ADAM_B1) * gv
        nv = ADAM_B2 * v_ref[...] + (1.0 - ADAM_B2) * (gv * gv)
        m_hat = nm / (1.0 - ADAM_B1 ** ADAM_STEP)
        v_hat = nv / (1.0 - ADAM_B2 ** ADAM_STEP)
        d_ref[...] = -ADAM_LR * (m_hat / (jnp.sqrt(v_hat) + ADAM_EPS) + ADAM_WD * w_ref[...])
        nm_ref[...] = nm
        nv_ref[...] = nv

    blk = pl.BlockSpec((tr, c), lambda i: (i, 0))
    outs = pl.pallas_call(
        body, name=name, grid=(rows // tr,),
        in_specs=[blk] * 4 + [_DEP] * len(deps), out_specs=[blk] * 3,
        out_shape=[jax.ShapeDtypeStruct((rows, c), F32)] * 3,
        compiler_params=_params(("parallel",)),
    )(to2d(w), to2d(g), to2d(m), to2d(v), *deps)
    return tuple(o.reshape(shape) for o in outs)


_ANY = pl.BlockSpec(memory_space=pl.ANY)


def _place():
    x, y, c = lax.axis_index("x"), lax.axis_index("y"), lax.axis_index("c")
    others = [(1 - x, y), (x, 1 - y), (1 - x, 1 - y)]
    return x, y, c, others


def _all_gather_chips(shard, name):
    rws, cols = shard.shape
    half = rws // 2

    def body(x_ref, out_ref, send_sems, recv_sems, local_sem):
        x, y, c, others = _place()
        me = 2 * x + y
        sib = (x, y, 1 - c)

        def rows(chip, cc):
            return out_ref.at[chip, pl.ds(pl.multiple_of(cc * half, 16), half), :]

        def copy(k, src, dst, to):
            return pltpu.make_async_remote_copy(src_ref=src, dst_ref=dst, send_sem=send_sems.at[k],
                                                recv_sem=recv_sems.at[k], device_id=to, device_id_type=MESH)

        mine = pltpu.make_async_copy(x_ref, out_ref.at[me], local_sem)
        mine.start()
        my_half = x_ref.at[pl.ds(pl.multiple_of(c * half, 16), half), :]
        first = [copy(j, my_half, rows(me, c), (ox, oy, c)) for j, (ox, oy) in enumerate(others)]
        for cp in first:
            cp.start()
        passed = []
        for j, (ox, oy) in enumerate(others):
            blk = rows(2 * ox + oy, c)
            copy(j, blk, blk, (x, y, c)).wait_recv()
            fwd = copy(3 + j, blk, blk, sib)
            fwd.start()
            passed.append(fwd)
        for j, (ox, oy) in enumerate(others):
            blk = rows(2 * ox + oy, 1 - c)
            copy(3 + j, blk, blk, (x, y, c)).wait_recv()
        for cp in first + passed:
            cp.wait_send()
        mine.wait()

    return pl.pallas_call(
        body, name=name,
        in_specs=[_ANY], out_specs=_ANY,
        out_shape=jax.ShapeDtypeStruct((N_CHIPS, rws, cols), shard.dtype),
        scratch_shapes=[pltpu.SemaphoreType.DMA((6,)), pltpu.SemaphoreType.DMA((6,)), pltpu.SemaphoreType.DMA],
        compiler_params=pltpu.CompilerParams(has_side_effects=True),
    )(shard)


def _swap_halves(buf, name, dep=None):
    nb, rws, cols = buf.shape
    half = rws // 2
    deps = [] if dep is None else [dep]

    def body(b_ref, *rest):
        own_ref, sib_ref, send_sem, recv_sem, local_sem = rest[len(deps):]
        x, y, c, _ = _place()
        keep = b_ref.at[:, pl.ds(pl.multiple_of(c * half, 16), half), :]
        give = b_ref.at[:, pl.ds(pl.multiple_of((1 - c) * half, 16), half), :]
        mine = pltpu.make_async_copy(keep, own_ref, local_sem)
        mine.start()
        cp = pltpu.make_async_remote_copy(src_ref=give, dst_ref=sib_ref, send_sem=send_sem, recv_sem=recv_sem,
                                          device_id=(x, y, 1 - c), device_id_type=MESH)
        cp.start()
        cp.wait()
        mine.wait()

    shp = jax.ShapeDtypeStruct((nb, half, cols), buf.dtype)
    return pl.pallas_call(
        body, name=name,
        in_specs=[_ANY] * (1 + len(deps)), out_specs=[_ANY, _ANY], out_shape=[shp, shp],
        scratch_shapes=[pltpu.SemaphoreType.DMA, pltpu.SemaphoreType.DMA, pltpu.SemaphoreType.DMA],
        compiler_params=pltpu.CompilerParams(has_side_effects=True),
    )(buf, *deps)


def _add2(a, b, name):
    nb, rws, cols = a.shape
    t = _tile(rws, 256)
    if rws % t:
        t = rws

    def body(a_ref, b_ref, o_ref):
        o_ref[...] = (a_ref[...].astype(F32) + b_ref[...].astype(F32)).astype(o_ref.dtype)

    blk = pl.BlockSpec((1, t, cols), lambda i, j: (i, j, 0))
    return pl.pallas_call(
        body, name=name, grid=(nb, rws // t), in_specs=[blk, blk], out_specs=blk,
        out_shape=jax.ShapeDtypeStruct(a.shape, a.dtype),
        compiler_params=_params(("parallel", "parallel")),
    )(a, b)


def _exchange_chips(pre, name):
    nb, half, cols = pre.shape

    def body(p_ref, out_ref, send_sems, recv_sems, local_sem):
        x, y, c, others = _place()
        me = 2 * x + y
        mine = pltpu.make_async_copy(p_ref.at[me], out_ref.at[me], local_sem)
        mine.start()
        sends = []
        for j, (ox, oy) in enumerate(others):
            cp = pltpu.make_async_remote_copy(src_ref=p_ref.at[2 * ox + oy], dst_ref=out_ref.at[me],
                                              send_sem=send_sems.at[j], recv_sem=recv_sems.at[j],
                                              device_id=(ox, oy, c), device_id_type=MESH)
            cp.start()
            sends.append(cp)
        for j, (ox, oy) in enumerate(others):
            blk = out_ref.at[2 * ox + oy]
            pltpu.make_async_remote_copy(src_ref=blk, dst_ref=blk, send_sem=send_sems.at[j],
                                         recv_sem=recv_sems.at[j], device_id=(x, y, c),
                                         device_id_type=MESH).wait_recv()
        for cp in sends:
            cp.wait_send()
        mine.wait()

    return pl.pallas_call(
        body, name=name,
        in_specs=[_ANY], out_specs=_ANY, out_shape=jax.ShapeDtypeStruct(pre.shape, pre.dtype),
        scratch_shapes=[pltpu.SemaphoreType.DMA((3,)), pltpu.SemaphoreType.DMA((3,)), pltpu.SemaphoreType.DMA],
        compiler_params=pltpu.CompilerParams(has_side_effects=True),
    )(pre)


def _add4(parts, name):
    nb, half, cols = parts.shape
    t = _tile(half, 256)
    if half % t:
        t = half

    def body(p_ref, o_ref):
        acc = p_ref[0].astype(F32)
        for k in range(1, nb):
            acc = acc + p_ref[k].astype(F32)
        o_ref[...] = acc

    return pl.pallas_call(
        body, name=name, grid=(half // t,),
        in_specs=[pl.BlockSpec((nb, t, cols), lambda i: (0, i, 0))],
        out_specs=pl.BlockSpec((t, cols), lambda i: (i, 0)),
        out_shape=jax.ShapeDtypeStruct((half, cols), F32),
        compiler_params=_params(("parallel",)),
    )(parts)


def _join_halves(mine_half, name):
    half, cols = mine_half.shape

    def body(h_ref, out_ref, send_sem, recv_sem, local_sem):
        x, y, c, _ = _place()
        dst = out_ref.at[pl.ds(pl.multiple_of(c * half, 8), half), :]
        mine = pltpu.make_async_copy(h_ref, dst, local_sem)
        mine.start()
        cp = pltpu.make_async_remote_copy(src_ref=h_ref, dst_ref=dst, send_sem=send_sem, recv_sem=recv_sem,
                                          device_id=(x, y, 1 - c), device_id_type=MESH)
        cp.start()
        cp.wait()
        mine.wait()

    return pl.pallas_call(
        body, name=name,
        in_specs=[_ANY], out_specs=_ANY, out_shape=jax.ShapeDtypeStruct((2 * half, cols), mine_half.dtype),
        scratch_shapes=[pltpu.SemaphoreType.DMA, pltpu.SemaphoreType.DMA, pltpu.SemaphoreType.DMA],
        compiler_params=pltpu.CompilerParams(has_side_effects=True),
    )(mine_half)


def _reduce_scatter_chips(buf, tag, dep=None):
    own, sib = _swap_halves(buf, "rs_swap_" + tag, dep)
    pre = _add2(own, sib, "rs_add2_" + tag)
    parts = _exchange_chips(pre, "rs_xchg_" + tag)
    red = _add4(parts, "rs_add4_" + tag)
    return _join_halves(red, "rs_join_" + tag)


MAX_DMA_BYTES = 2 * 1024 * 1024
ROW_ALIGN = 16


def _pieces(rows, row_bytes):
    n = max(1, -(-(rows * row_bytes) // MAX_DMA_BYTES))
    step = -(-(-(-rows // n)) // ROW_ALIGN) * ROW_ALIGN
    return [(r, min(step, rows - r)) for r in range(0, rows, step)]


def _half_plan(arrays, row_axis):
    plan = []
    for a, arr in enumerate(arrays):
        row_bytes = math.prod(arr.shape[row_axis + 1:]) * arr.dtype.itemsize * (arr.shape[0] if row_axis else 1)
        plan += [(a, r0, nr) for r0, nr in _pieces(arr.shape[row_axis] // 2, row_bytes)]
    return plan


def _rows(start, size):
    return pl.ds(pl.multiple_of(start, ROW_ALIGN), size)


def _remote(src, dst, send_sems, recv_sems, k, to):
    return pltpu.make_async_remote_copy(src_ref=src, dst_ref=dst, send_sem=send_sems.at[k], recv_sem=recv_sems.at[k],
                                        device_id=to, device_id_type=MESH)


def _comm_call(body, name, ins, out_shapes, n_remote, n_local, aliases=None):
    return pl.pallas_call(
        body, name=name,
        in_specs=[_ANY] * len(ins), out_specs=[_ANY] * len(out_shapes), out_shape=out_shapes,
        scratch_shapes=[pltpu.SemaphoreType.DMA((n_remote,)), pltpu.SemaphoreType.DMA((n_remote,)),
                        pltpu.SemaphoreType.DMA((max(n_local, 1),))],
        input_output_aliases=aliases or {},
        compiler_params=pltpu.CompilerParams(has_side_effects=True),
    )(*ins)


def _cast_shard(w, l, me_idx, name):
    _, k, cols = w.shape
    tr = _tile(k, 256)
    if k % tr:
        tr = k

    def body(me_ref, w_ref, s_ref, land_ref):
        del me_ref
        v = w_ref[...].astype(BF16)
        s_ref[...] = v
        land_ref[...] = v

    grid_spec = pltpu.PrefetchScalarGridSpec(
        num_scalar_prefetch=1, grid=(k // tr,),
        in_specs=[pl.BlockSpec((None, tr, cols), lambda i, me: (l, i, 0))],
        out_specs=[pl.BlockSpec((tr, cols), lambda i, me: (i, 0)),
                   pl.BlockSpec((None, tr, cols), lambda i, me: (me[0], i, 0))])
    return pl.pallas_call(
        body, name=name, grid_spec=grid_spec,
        out_shape=[jax.ShapeDtypeStruct((k, cols), BF16), jax.ShapeDtypeStruct((N_CHIPS, k, cols), BF16)],
        compiler_params=_params(("parallel",)),
    )(me_idx, w)


def _gather_d2d(lands, name):
    n = len(lands)
    plan = _half_plan(lands, 1)
    plan = [(a, r0, nr) for a, r0, nr in plan]

    def body(*refs):
        out_refs = refs[n:2 * n]
        send_sems, recv_sems, _ = refs[2 * n:]
        x, y, c, others = _place()
        sends = []
        for i, (a, r0, nr) in enumerate(plan):
            rows = _rows(c * (lands[a].shape[1] // 2) + r0, nr)
            for j, (ox, oy) in enumerate(others):
                blk = out_refs[a].at[2 * ox + oy, rows, :]
                cp = _remote(blk, blk, send_sems, recv_sems, 3 * i + j, (x, y, 1 - c))
                cp.start()
                sends.append(cp)
        for i, (a, r0, nr) in enumerate(plan):
            rows = _rows((1 - c) * (lands[a].shape[1] // 2) + r0, nr)
            for j, (ox, oy) in enumerate(others):
                blk = out_refs[a].at[2 * ox + oy, rows, :]
                _remote(blk, blk, send_sems, recv_sems, 3 * i + j, (x, y, c)).wait_recv()
        for cp in sends:
            cp.wait_send()

    outs = [jax.ShapeDtypeStruct(a.shape, a.dtype) for a in lands]
    return _comm_call(body, name, lands, outs, 3 * len(plan), 0, aliases={a: a for a in range(n)})


def _rs_swap(ts, name):
    n = len(ts)
    plan = _half_plan(ts, 1)

    def body(*refs):
        t_refs, out_refs = refs[:n], refs[n:2 * n]
        send_sems, recv_sems, _ = refs[2 * n:]
        x, y, c, _o = _place()
        sends = []
        for i, (a, r0, nr) in enumerate(plan):
            src = t_refs[a].at[:, _rows((1 - c) * (ts[a].shape[1] // 2) + r0, nr), :]
            cp = _remote(src, out_refs[a].at[:, pl.ds(r0, nr), :], send_sems, recv_sems, i, (x, y, 1 - c))
            cp.start()
            sends.append(cp)
        for i, (a, r0, nr) in enumerate(plan):
            blk = out_refs[a].at[:, pl.ds(r0, nr), :]
            _remote(blk, blk, send_sems, recv_sems, i, (x, y, c)).wait_recv()
        for cp in sends:
            cp.wait_send()

    outs = [jax.ShapeDtypeStruct((t.shape[0], t.shape[1] // 2, t.shape[2]), t.dtype) for t in ts]
    return _comm_call(body, name, ts, outs, len(plan), 0)


def _add_halves(ts, gots, c_idx, me_idx, name):
    n = len(ts)

    def body(c_ref, me_ref, *refs):
        del c_ref
        t_refs, g_refs = refs[:n], refs[n:2 * n]
        o_refs, mine_refs = refs[2 * n:3 * n], refs[3 * n:]
        for t_ref, g_ref, o_ref, mine_ref in zip(t_refs, g_refs, o_refs, mine_refs):
            v = (t_ref[...].astype(F32) + g_ref[...].astype(F32)).astype(o_ref.dtype)
            o_ref[...] = v

            @pl.when(pl.program_id(0) == me_ref[0])
            def _():
                mine_ref[...] = v

    blks = [(1, g.shape[1], g.shape[2]) for g in gots]
    same = [pl.BlockSpec(b, lambda i, c, me: (i, 0, 0)) for b in blks]
    grid_spec = pltpu.PrefetchScalarGridSpec(
        num_scalar_prefetch=2, grid=(N_CHIPS,),
        in_specs=[pl.BlockSpec(b, lambda i, c, me: (i, c[0], 0)) for b in blks] + same,
        out_specs=same + [pl.BlockSpec(b, lambda i, c, me: (me[0], 0, 0)) for b in blks])
    shapes = [jax.ShapeDtypeStruct(g.shape, g.dtype) for g in gots]
    outs = pl.pallas_call(
        body, name=name, grid_spec=grid_spec, out_shape=shapes + shapes,
        compiler_params=_params(("arbitrary",)),
    )(c_idx, me_idx, *ts, *gots)
    return outs[:n], outs[n:]


def _add4_halves(parts, c_idx, name):
    n = len(parts)
    steps = 2

    def body(c_ref, *refs):
        del c_ref
        for p_ref, o_ref in zip(refs[:n], refs[n:]):
            acc = p_ref[0].astype(F32)
            for k in range(1, N_CHIPS):
                acc = acc + p_ref[k].astype(F32)
            o_ref[...] = acc

    grid_spec = pltpu.PrefetchScalarGridSpec(
        num_scalar_prefetch=1, grid=(steps,),
        in_specs=[pl.BlockSpec((N_CHIPS, p.shape[1] // steps, p.shape[2]), lambda i, c: (0, i, 0)) for p in parts],
        out_specs=[pl.BlockSpec((p.shape[1] // steps, p.shape[2]), lambda i, c: (c[0] * steps + i, 0))
                   for p in parts])
    return pl.pallas_call(
        body, name=name, grid_spec=grid_spec,
        out_shape=[jax.ShapeDtypeStruct((2 * p.shape[1], p.shape[2]), F32) for p in parts],
        compiler_params=_params(("parallel",)),
    )(c_idx, *parts)


def _rs_join(fulls, name):
    n = len(fulls)
    plan = _half_plan(fulls, 0)

    def body(*refs):
        out_refs = refs[n:2 * n]
        send_sems, recv_sems, _ = refs[2 * n:]
        x, y, c, _o = _place()
        sends = []
        for i, (a, r0, nr) in enumerate(plan):
            blk = out_refs[a].at[_rows(c * (fulls[a].shape[0] // 2) + r0, nr), :]
            cp = _remote(blk, blk, send_sems, recv_sems, i, (x, y, 1 - c))
            cp.start()
            sends.append(cp)
        for i, (a, r0, nr) in enumerate(plan):
            blk = out_refs[a].at[_rows((1 - c) * (fulls[a].shape[0] // 2) + r0, nr), :]
            _remote(blk, blk, send_sems, recv_sems, i, (x, y, c)).wait_recv()
        for cp in sends:
            cp.wait_send()

    outs = [jax.ShapeDtypeStruct(f.shape, f.dtype) for f in fulls]
    return _comm_call(body, name, fulls, outs, len(plan), 0, aliases={a: a for a in range(n)})


_HBM = pl.BlockSpec(memory_space=pltpu.HBM)
_SEM = pl.BlockSpec(memory_space=pltpu.SEMAPHORE)
_EFFECT = pltpu.SideEffectType.DATAFLOW_SIDE_EFFECTING


def _ici_plan(kind, a_list):
    if kind == "gather":
        return _half_plan(a_list, 0)
    plan = []
    for a, p in enumerate(a_list):
        plan += [(a, r0, nr) for r0, nr in _pieces(p.shape[1], p.shape[2] * p.dtype.itemsize)]
    return plan


def _ici_refs(kind, a_ref, b_ref, a_shape, r0, nr, c, me, peer):
    if kind == "gather":
        rows = _rows(c * (a_shape[0] // 2) + r0, nr)
        return a_ref.at[rows, :], b_ref.at[me, rows, :], b_ref.at[peer, rows, :]
    rows = pl.ds(r0, nr)
    return a_ref.at[peer, rows, :], b_ref.at[me, rows, :], b_ref.at[peer, rows, :]


def _ici_start(kind, a_list, b_list, name):
    n = len(a_list)
    plan = _ici_plan(kind, a_list)
    shapes = [a.shape for a in a_list]

    def body(*refs):
        a_refs, b_refs = refs[:n], refs[n:2 * n]
        send_sems, recv_sems = refs[2 * n], refs[2 * n + 1]
        token = refs[4 * n + 2]
        x, y, c, others = _place()
        me = 2 * x + y
        for i, (a, r0, nr) in enumerate(plan):
            for j, (ox, oy) in enumerate(others):
                src, dst, _ = _ici_refs(kind, a_refs[a], b_refs[a], shapes[a], r0, nr, c, me, 2 * ox + oy)
                _remote(src, dst, send_sems, recv_sems, 3 * i + j, (ox, oy, c)).start()
        token[...] = jnp.zeros_like(token)

    hbm = lambda v: pltpu.HBM(v.shape, v.dtype)
    ncp = 3 * len(plan)
    outs = pl.pallas_call(
        body, name=name,
        in_specs=[_HBM] * (2 * n),
        out_specs=[_SEM, _SEM] + [_HBM] * (2 * n) + [pl.BlockSpec(memory_space=pltpu.VMEM)],
        out_shape=[pltpu.SemaphoreType.DMA((ncp,)), pltpu.SemaphoreType.DMA((ncp,))]
                  + [hbm(v) for v in a_list] + [hbm(v) for v in b_list] + [jax.ShapeDtypeStruct((8, LANES), F32)],
        input_output_aliases={i: 2 + i for i in range(2 * n)},
        compiler_params=pltpu.CompilerParams(has_side_effects=_EFFECT),
    )(*[pltpu.with_memory_space_constraint(v, pltpu.HBM) for v in list(a_list) + list(b_list)])
    return outs[0], outs[1], outs[2:2 + n], outs[2 + n:2 + 2 * n], outs[2 + 2 * n]


def _ici_wait(kind, started, after, name):
    send_sems, recv_sems, a_list, b_list, _ = started
    afters = list(after) if isinstance(after, (list, tuple)) else [after]
    n = len(a_list)
    plan = _ici_plan(kind, a_list)
    shapes = [a.shape for a in a_list]

    def body(*refs):
        a_refs, b_refs = refs[:n], refs[n:2 * n]
        send_sems, recv_sems = refs[2 * n], refs[2 * n + 1]
        x, y, c, others = _place()
        me = 2 * x + y
        for i, (a, r0, nr) in enumerate(plan):
            for j, (ox, oy) in enumerate(others):
                src, dst, land = _ici_refs(kind, a_refs[a], b_refs[a], shapes[a], r0, nr, c, me, 2 * ox + oy)
                _remote(src, dst, send_sems, recv_sems, 3 * i + j, (ox, oy, c)).wait_send()
                _remote(land, land, send_sems, recv_sems, 3 * i + j, (x, y, c)).wait_recv()

    hbm = lambda v: pltpu.HBM(v.shape, v.dtype)
    outs = pl.pallas_call(
        body, name=name,
        in_specs=[_HBM] * (2 * n) + [_SEM, _SEM] + [_ANY] * len(afters),
        out_specs=[_HBM] * (2 * n),
        out_shape=[hbm(v) for v in a_list] + [hbm(v) for v in b_list],
        input_output_aliases={i: i for i in range(2 * n)},
        compiler_params=pltpu.CompilerParams(has_side_effects=_EFFECT),
    )(*a_list, *b_list, send_sems, recv_sems, *afters)
    return outs[n:]


def _rs_begin(ts, c_idx, me_idx, tag):
    got = _rs_swap(ts, "rs_swap_" + tag)
    pres, mine = _add_halves(ts, got, c_idx, me_idx, "rs_add2_" + tag)
    return _ici_start("scatter", pres, mine, "rs_xchg_start_" + tag)


def _rs_finish(started, after, c_idx, tag):
    parts = _ici_wait("scatter", started, after, "rs_xchg_wait_" + tag)
    return _rs_join(_add4_halves(parts, c_idx, "rs_add4_" + tag), "rs_join_" + tag)


def _pack_rows(pieces, rows, dtype):
    flat = jnp.concatenate([p.astype(dtype).reshape(-1) for p in pieces])
    return jnp.pad(flat, (0, rows * PACK_COLS - flat.shape[0])).reshape(rows, PACK_COLS)


def _unpack(flat, shapes):
    out, off = [], 0
    for shp in shapes:
        size = math.prod(shp)
        out.append(flat[off:off + size].reshape(shp))
        off += size
    return out


def _rows_for(n_elems, mult):
    rows = -(-n_elems // PACK_COLS)
    return -(-rows // mult) * mult


BIG_SHARDS = [("w_in", (D_MODEL, 1474)), ("w_branch_att", (D_ATT, 256)), ("w_branch_conv", (D_CONV, 256)),
              ("w_branch_sgu", (D_SGU, 256)), ("w_out", (256, D_MODEL)), ("w_ffn_up", (D_MODEL, FF_BLK)),
              ("w_ffn_down", (D_FF // N_CHIPS, D_MODEL))]
SMALL_SHARDS = [("b_gate", (3, 256)), ("conv_mix_w", (3, 64)), ("conv_ffn_w", (3, FF_BLK))]
REPLICATED = [("pre_mix_g", (D_MODEL,)), ("post_mix_g", (D_MODEL,)), ("pre_ffn_g", (D_MODEL,)),
              ("post_ffn_g", (D_MODEL,)), ("b_forget", (N_HEADS,)), ("sgu_ln_g", (D_SGU,)), ("sgu_ln_b", (D_SGU,)),
              ("sgu_w", (N_GROUPS, CHUNK, CHUNK)), ("sgu_b", (N_GROUPS, CHUNK))]
WEIGHT_ORDER = ["pre_mix_g", "post_mix_g", "pre_ffn_g", "post_ffn_g", "w_in", "b_forget", "b_gate", "conv_mix_w",
                "sgu_ln_g", "sgu_ln_b", "sgu_w", "sgu_b", "w_branch_att", "w_branch_conv", "w_branch_sgu", "w_out",
                "w_ffn_up", "conv_ffn_w", "w_ffn_down"]

_SMALL_ELEMS = sum(math.prod(s) for _, s in SMALL_SHARDS)
_REP_ELEMS = sum(math.prod(s) for _, s in REPLICATED)
_REP_QUARTER = -(-(DEPTH * _REP_ELEMS) // N_CHIPS)
SMALL_PARAM_ROWS = _rows_for(DEPTH * _SMALL_ELEMS, 32)
SMALL_ROWS = _rows_for(DEPTH * _SMALL_ELEMS + _REP_QUARTER, 32)
IN_WIDTH = 5896
IN_SHARD = IN_WIDTH // N_CHIPS
IN_SHARD_PAD = 1536
IN_PAD = 6144


def _gather_small(wts):
    shard = _pack_rows([wts[n] for n, _ in SMALL_SHARDS], SMALL_PARAM_ROWS, F32)
    full = _all_gather_chips(shard, "gather_small_params").reshape(N_CHIPS, -1)
    per_chip = [_unpack(full[j], [(DEPTH,) + s for _, s in SMALL_SHARDS]) for j in range(N_CHIPS)]
    return {n: jnp.concatenate([per_chip[j][i] for j in range(N_CHIPS)], axis=-1)
            for i, (n, _) in enumerate(SMALL_SHARDS)}


BIG_NAMES = [n for n, _ in BIG_SHARDS]
FIRST_NAMES = ["w_in"]
LATE_NAMES = BIG_NAMES[1:]


def _gather_begin(wts, l, me_idx, names, tag):
    cast = [_cast_shard(wts[n], l, me_idx, "cast_" + n) for n in names]
    return _ici_start("gather", [sh for sh, _ in cast], [ld for _, ld in cast], "gather_ici_start_" + tag)


def _gather_finish(started, after, names, tag):
    lands = _ici_wait("gather", started, after, "gather_ici_wait_" + tag)
    return dict(zip(names, _gather_d2d(lands, "gather_d2d_" + tag)))


def _pad_rows(a, rows):
    return jnp.pad(a, ((0, rows - a.shape[0]), (0, 0)))


def _whole_cols(land):
    return land.transpose(1, 0, 2).reshape(land.shape[1], -1)


_O_F = 3 * D_ATT
_O_B = _O_F + N_HEADS
_O_GL = _O_B + 3 * D_CONV + 2 * D_SGU


_LOCAL_ORDER = [(_O_GL, IN_WIDTH), (0, _O_F), (_O_B, _O_GL), (_O_F, _O_B)]


def _own_cols(land, lo, hi):
    pieces = []
    for j in range(N_CHIPS):
        a, b = max(lo, j * IN_SHARD), min(hi, (j + 1) * IN_SHARD)
        if a < b:
            pieces.append(land[j][:, a - j * IN_SHARD:b - j * IN_SHARD])
    return pieces


def _local_cols(m, lo, hi):
    pieces, off = [], 0
    for a, b in _LOCAL_ORDER:
        x, y = max(lo, a), min(hi, b)
        if x < y:
            pieces.append((x, m[:, off + x - a:off + y - a]))
        off += b - a
    pieces = [p for _, p in sorted(pieces, key=lambda t: t[0])]
    if hi > IN_WIDTH:
        pieces.append(jnp.zeros((m.shape[0], hi - max(lo, IN_WIDTH)), m.dtype))
    return pieces


def _prep_first(wts, lands, small, l):
    land = lands["w_in"]
    cf = small["conv_ffn_w"][l]
    blk = lambda a, j: a[:, j * FF_BLK:(j + 1) * FF_BLK]
    local = [piece for lo, hi in _LOCAL_ORDER for piece in _own_cols(land, lo, hi)]
    return {
        "w_p": jnp.concatenate(local + [jnp.zeros((D_MODEL, IN_PAD - IN_WIDTH), BF16)], axis=1),
        "wf_t": _pad_rows(jnp.concatenate(_own_cols(land, _O_F, _O_B), axis=1).T, F_ROWS),
        "b_forget": _pad_rows(wts["b_forget"][l].reshape(N_HEADS, 1), F_ROWS),
        "b_gate": _pad_rows(small["b_gate"][l], 8),
        "conv_mix_w": _pad_rows(small["conv_mix_w"][l], 8),
        "conv_ffn_w": _pad_rows(jnp.concatenate([blk(cf, 0), blk(cf, 2), blk(cf, 1), blk(cf, 3)], axis=1), 8),
        "pre_mix_g": wts["pre_mix_g"][l].reshape(1, -1), "post_mix_g": wts["post_mix_g"][l].reshape(1, -1),
        "pre_ffn_g": wts["pre_ffn_g"][l].reshape(1, -1), "post_ffn_g": wts["post_ffn_g"][l].reshape(1, -1),
        "ln_g": wts["sgu_ln_g"][l].reshape(1, -1), "ln_b": wts["sgu_ln_b"][l].reshape(1, -1),
        "sgu_w": wts["sgu_w"][l],
        "sgu_bias": jnp.repeat(wts["sgu_b"][l].T, HEAD_DIM, axis=1),
    }


def _prep_late(lands):
    up = lands["w_ffn_up"]
    return {
        "w_att": _whole_cols(lands["w_branch_att"]), "w_conv": _whole_cols(lands["w_branch_conv"]),
        "w_sgu": _whole_cols(lands["w_branch_sgu"]),
        "w_out": lands["w_out"].reshape(D_MODEL, D_MODEL),
        "w_up": jnp.concatenate([up[0], up[2], up[1], up[3]], axis=1),
        "w_down": lands["w_ffn_down"].reshape(D_FF, D_MODEL),
    }


def _layer_fwd(x, p, dep=None, late=None):
    s = x.shape[0]
    xn = _rms_fwd(x, p["pre_mix_g"], "rms_pre_mix", dep)
    h = _mm(xn, p["w_p"], "nn", BF16, "mm_in", s, 512, D_MODEL)
    f_row = _mm(p["wf_t"], xn, "nt", F32, "mm_forget", F_ROWS, 2048, D_MODEL)
    ck = _gate_fwd(f_row, p["b_forget"], "gate_fwd")
    o, o_f32, lse = _attn_fwd(h, ck, "attn_fwd")
    yc = _sconv_fwd(h, p["conv_mix_w"], "sconv_fwd")
    ys = _sgu_fwd(h, p["ln_g"], p["ln_b"], p["sgu_w"], p["sgu_bias"], "sgu_fwd")
    if late is not None:
        p.update(late(o))
    merged = _merge_fwd(h, (o, yc, ys), (p["w_att"], p["w_conv"], p["w_sgu"]), p["b_gate"], "merge_fwd")
    mo = _mm(merged, p["w_out"], "nn", F32, "mm_out", 2048, 512, D_MODEL)
    x1, xn2 = _resid_post_norm(x, mo, p["post_mix_g"], p["pre_ffn_g"], "post_mix")
    h2 = _mm(xn2, p["w_up"], "nn", BF16, "mm_up", 2048, FF_BLK, D_MODEL)
    pact = _ffn_act_fwd(h2, p["conv_ffn_w"], "ffn_act_fwd")
    ff = _mm(pact, p["w_down"], "nn", F32, "mm_down", 1024, D_MODEL, D_FF)
    x2 = _resid_post(x1, ff, p["post_ffn_g"], "post_ffn")
    saved = dict(x=x, xn=xn, h=h, f_row=f_row, ck=ck, o=o, o_f32=o_f32, lse=lse, yc=yc, ys=ys, merged=merged, mo=mo, x1=x1,
                 xn2=xn2, h2=h2, pact=pact, ff=ff)
    return x2, saved


def _layer_bwd(dx2, p, sv, dep=None, early=None):
    s = dx2.shape[0]
    g = {}
    same = lambda b: b
    dff, g["post_ffn_g"] = _rms_bwd(sv["ff"], p["post_ffn_g"], [dx2], None, BF16, "post_ffn_bwd", dep)
    dpact = _mm(dff, p["w_down"], "nt", BF16, "mm_down_dx", 2048, FF_BLK, D_MODEL)
    t_down = _mm(sv["pact"], dff, "tn", BF16, "mm_down_dw", 256, D_MODEL, s).reshape(N_CHIPS, -1, D_MODEL)
    dh2, dconv_ffn = _ffn_act_conv_bwd(sv["h2"], p["conv_ffn_w"], dpact, "ffn_act_conv_bwd")
    dxn2 = _mm(dh2, p["w_up"], "nt", F32, "mm_up_dx", 512, D_MODEL, 2 * D_FF)
    t_up = _mm(sv["xn2"], dh2, "tn", BF16, "mm_up_dw", 512, FF_BLK, s, chip_of=lambda b: (b % 2) * 2 + b // 2)
    dx1, g["pre_ffn_g"] = _rms_bwd(sv["x1"], p["pre_ffn_g"], [dxn2], dx2, F32, "pre_ffn_bwd")
    dep_mix = early([t_up, t_down]) if early is not None else None
    dmo, g["post_mix_g"] = _rms_bwd(sv["mo"], p["post_mix_g"], [dx1], None, BF16, "post_mix_bwd", dep_mix)
    dmerged = _mm(dmo, p["w_out"], "nt", F32, "mm_out_dx", 2048, 512, D_MODEL)
    t_out = _mm(sv["merged"], dmo, "tn", BF16, "mm_out_dw", 512, D_MODEL, s).reshape(N_CHIPS, -1, D_MODEL)
    acts = (sv["o"], sv["yc"], sv["ys"])
    ws = (p["w_att"], p["w_conv"], p["w_sgu"])
    dy_a, dy_c, dy_s, dgl, db_gate = _merge_bwd(sv["h"], acts, ws, p["b_gate"], dmerged, "merge_bwd")
    do = _mm(dy_a, p["w_att"], "nt", BF16, "mm_att_dx", 2048, D_ATT, D_MODEL)
    dyc = _mm(dy_c, p["w_conv"], "nt", BF16, "mm_conv_dx", 2048, D_CONV, D_MODEL)
    dys = _mm(dy_s, p["w_sgu"], "nt", BF16, "mm_sgu_dx", 2048, D_SGU, D_MODEL)
    t_att = _mm(sv["o"], dy_a, "tn", BF16, "mm_att_dw", D_ATT, 256, s, chip_of=same)
    t_conv = _mm(sv["yc"], dy_c, "tn", BF16, "mm_conv_dw", D_CONV, 256, s, chip_of=same)
    t_sgu = _mm(sv["ys"], dy_s, "tn", BF16, "mm_sgu_dw", D_SGU, 256, s, chip_of=same)
    d_conv, dconv_mix = _sconv_bwd(sv["h"], p["conv_mix_w"], dyc, "sconv_bwd")
    d_sgu, g["sgu_ln_g"], g["sgu_ln_b"], g["sgu_w"], dbias = _sgu_bwd(
        sv["h"], p["ln_g"], p["ln_b"], p["sgu_w"], p["sgu_bias"], dys, "sgu_bwd")
    dq, dk, dv, dc_even, dc_odd = _attn_bwd(sv["h"], sv["ck"], sv["o_f32"], sv["lse"], do, "attn_bwd")
    df, db_forget = _gate_bwd(sv["f_row"], p["b_forget"], dc_even, dc_odd, "gate_bwd")
    f_cols = jnp.concatenate([df[:N_HEADS].T, jnp.zeros((s, IN_PAD - IN_WIDTH), BF16)], axis=1)
    dh = _assemble_dh(dgl, [dq, dk, dv, d_conv, d_sgu, f_cols], "assemble_dh")
    dxn = _mm(dh, p["w_p"], "nt", F32, "mm_in_dx", 512, D_MODEL, IN_PAD)
    dw_p = _mm(sv["xn"], dh, "tn", BF16, "mm_in_dw", D_MODEL, 512, s)
    t_in = jnp.stack([jnp.concatenate(_local_cols(dw_p, j * IN_SHARD, j * IN_SHARD + IN_SHARD_PAD), axis=1)
                      for j in range(N_CHIPS)])
    dx, g["pre_mix_g"] = _rms_bwd(sv["x"], p["pre_mix_g"], [dxn], dx1, F32, "pre_mix_bwd")
    blk = lambda a, j: a[:, j * FF_BLK:(j + 1) * FF_BLK]
    g["conv_ffn_w"] = jnp.concatenate([blk(dconv_ffn, 0), blk(dconv_ffn, 2), blk(dconv_ffn, 1),
                                       blk(dconv_ffn, 3)], axis=1)[:3]
    g["conv_mix_w"] = dconv_mix[:3]
    g["b_gate"] = db_gate[:3]
    g["b_forget"] = db_forget[:N_HEADS, 0]
    g["sgu_b"] = jnp.sum(dbias.reshape(CHUNK, N_GROUPS, HEAD_DIM), axis=-1).T
    for n in ("pre_mix_g", "post_mix_g", "pre_ffn_g", "post_ffn_g", "sgu_ln_g", "sgu_ln_b"):
        g[n] = g[n].reshape(-1)
    mix = [t_in, t_att, t_conv, t_sgu, t_out]
    return dx, (mix if early is not None else mix + [t_up, t_down]), g


def _assemble_dh(dh, pieces, name):
    s = dh.shape[0]
    t = _tile(s, 512)
    width = sum(a.shape[1] for a in pieces)
    assert 2 * width == dh.shape[1]

    def body(*refs):
        out = refs[-1]
        col = 0
        for ref in refs[1:-1]:
            w = ref.shape[1]
            out[:, col:col + w] = ref[...].astype(out.dtype)
            col += w

    return pl.pallas_call(
        body, name=name, grid=(s // t,),
        in_specs=[_ANY] + [pl.BlockSpec((t, a.shape[1]), lambda i: (i, 0)) for a in pieces],
        out_specs=pl.BlockSpec((t, width), lambda i: (i, 1)),
        out_shape=jax.ShapeDtypeStruct(dh.shape, dh.dtype),
        input_output_aliases={0: 0},
        compiler_params=_params(("parallel",)),
    )(dh, *pieces)


def _shard_cols(a, j):
    w = a.shape[-1] // N_CHIPS
    return a[..., j * w:(j + 1) * w]


def kernel(x, pre_mix_g, post_mix_g, pre_ffn_g, post_ffn_g, w_in, b_forget, b_gate, conv_mix_w, sgu_ln_g, sgu_ln_b, sgu_w, sgu_b, w_branch_att, w_branch_conv, w_branch_sgu, w_out, w_ffn_up, conv_ffn_w, w_ffn_down, loss_target, m_pre_mix_g, m_post_mix_g, m_pre_ffn_g, m_post_ffn_g, m_w_in, m_b_forget, m_b_gate, m_conv_mix_w, m_sgu_ln_g, m_sgu_ln_b, m_sgu_w, m_sgu_b, m_w_branch_att, m_w_branch_conv, m_w_branch_sgu, m_w_out, m_w_ffn_up, m_conv_ffn_w, m_w_ffn_down, v_pre_mix_g, v_post_mix_g, v_pre_ffn_g, v_post_ffn_g, v_w_in, v_b_forget, v_b_gate, v_conv_mix_w, v_sgu_ln_g, v_sgu_ln_b, v_sgu_w, v_sgu_b, v_w_branch_att, v_w_branch_conv, v_w_branch_sgu, v_w_out, v_w_ffn_up, v_conv_ffn_w, v_w_ffn_down):
    wts = dict(pre_mix_g=pre_mix_g, post_mix_g=post_mix_g, pre_ffn_g=pre_ffn_g, post_ffn_g=post_ffn_g, w_in=w_in,
               b_forget=b_forget, b_gate=b_gate, conv_mix_w=conv_mix_w, sgu_ln_g=sgu_ln_g, sgu_ln_b=sgu_ln_b,
               sgu_w=sgu_w, sgu_b=sgu_b, w_branch_att=w_branch_att, w_branch_conv=w_branch_conv,
               w_branch_sgu=w_branch_sgu, w_out=w_out, w_ffn_up=w_ffn_up, conv_ffn_w=conv_ffn_w,
               w_ffn_down=w_ffn_down)
    moms = dict(pre_mix_g=m_pre_mix_g, post_mix_g=m_post_mix_g, pre_ffn_g=m_pre_ffn_g, post_ffn_g=m_post_ffn_g,
                w_in=m_w_in, b_forget=m_b_forget, b_gate=m_b_gate, conv_mix_w=m_conv_mix_w, sgu_ln_g=m_sgu_ln_g,
                sgu_ln_b=m_sgu_ln_b, sgu_w=m_sgu_w, sgu_b=m_sgu_b, w_branch_att=m_w_branch_att,
                w_branch_conv=m_w_branch_conv, w_branch_sgu=m_w_branch_sgu, w_out=m_w_out, w_ffn_up=m_w_ffn_up,
                conv_ffn_w=m_conv_ffn_w, w_ffn_down=m_w_ffn_down)
    vels = dict(pre_mix_g=v_pre_mix_g, post_mix_g=v_post_mix_g, pre_ffn_g=v_pre_ffn_g, post_ffn_g=v_post_ffn_g,
                w_in=v_w_in, b_forget=v_b_forget, b_gate=v_b_gate, conv_mix_w=v_conv_mix_w, sgu_ln_g=v_sgu_ln_g,
                sgu_ln_b=v_sgu_ln_b, sgu_w=v_sgu_w, sgu_b=v_sgu_b, w_branch_att=v_w_branch_att,
                w_branch_conv=v_w_branch_conv, w_branch_sgu=v_w_branch_sgu, w_out=v_w_out, w_ffn_up=v_w_ffn_up,
                conv_ffn_w=v_conv_ffn_w, w_ffn_down=v_w_ffn_down)

    c_idx = lax.axis_index("c").astype(jnp.int32).reshape(1)
    me_idx = (2 * lax.axis_index("x") + lax.axis_index("y")).astype(jnp.int32).reshape(1)
    small = _gather_small(wts)

    xs = x[0]
    layers, saved = [], []
    first = _gather_begin(wts, 0, me_idx, FIRST_NAMES, "first")
    rest = _gather_begin(wts, 0, me_idx, LATE_NAMES, "late")
    lands = _gather_finish(first, xs, FIRST_NAMES, "first")
    late = lambda after: _prep_late(_gather_finish(rest, after, LATE_NAMES, "late"))
    for l in range(DEPTH):
        p = _prep_first(wts, lands, small, l)
        if l > 0:
            p.update(_prep_late(lands))
        nxt = _gather_begin(wts, l + 1, me_idx, BIG_NAMES, "all") if l + 1 < DEPTH else None
        dep = ([nxt[4]] if nxt else []) + ([rest[4]] if l == 0 else [])
        xs, sv = _layer_fwd(xs, p, dep or None, late if l == 0 else None)
        if nxt:
            lands = _gather_finish(nxt, xs, BIG_NAMES, "all")
        layers.append(p)
        saved.append(sv)
    dy, loss_part = _loss_head(xs, loss_target[0], "loss_head")
    loss = lax.psum(loss_part[0, 0], ("x", "y", "c"))

    big_red = [None] * DEPTH
    small_grads = [None] * DEPTH
    pending = None
    ffn = []
    for l in reversed(range(DEPTH)):
        early = None
        if l == 0:
            def early(ts_ffn):
                ffn.append(_rs_begin(ts_ffn, c_idx, me_idx, "ffn"))
                return ffn[0][4]
        dy, ts, small_grads[l] = _layer_bwd(dy, layers[l], saved[l], pending[4] if pending else None, early)
        if pending:
            big_red[l + 1] = _rs_finish(pending, dy, c_idx, "big")
        pending = _rs_begin(ts, c_idx, me_idx, "mix" if l == 0 else "big")
    red_ffn = _rs_finish(ffn[0], dy, c_idx, "ffn")
    grad_x = dy[None]

    done = {}
    for k, n in enumerate(("w_ffn_up", "w_ffn_down")):
        i = BIG_NAMES.index(n)
        g = jnp.stack([red_ffn[k]] + [big_red[l][i] for l in range(1, DEPTH)])
        done[n] = (g,) + _adamw(wts[n], g, moms[n], vels[n], "adamw_" + n, pending[4])

    rep_flat = jnp.concatenate([small_grads[l][n].reshape(-1) for l in range(DEPTH) for n, _ in REPLICATED])
    rep_flat = jnp.pad(rep_flat, (0, N_CHIPS * _REP_QUARTER - rep_flat.shape[0]))
    rows = []
    for j in range(N_CHIPS):
        pieces = [_shard_cols(small_grads[l][n], j) for l in range(DEPTH) for n, _ in SMALL_SHARDS]
        pieces.append(rep_flat[j * _REP_QUARTER:(j + 1) * _REP_QUARTER])
        rows.append(_pack_rows(pieces, SMALL_ROWS, F32))
    small_red = _reduce_scatter_chips(jnp.stack(rows), "small", done["w_ffn_down"][1])
    small_all = _all_gather_chips(small_red, "gather_small")
    big_red[0] = _rs_finish(pending, [small_all] + [done[n][1] for n in done], c_idx, "mix") + red_ffn
    small_all = small_all.reshape(N_CHIPS, -1)

    grads = {}
    for i, (n, _) in enumerate(BIG_SHARDS):
        if n not in done:
            grads[n] = jnp.stack([big_red[l][i][:, :IN_SHARD] if n == "w_in" else big_red[l][i]
                                  for l in range(DEPTH)])
    mine_small = small_red.reshape(-1)
    parts = _unpack(mine_small, [s for _ in range(DEPTH) for _, s in SMALL_SHARDS])
    for i, (n, _) in enumerate(SMALL_SHARDS):
        grads[n] = jnp.stack([parts[l * len(SMALL_SHARDS) + i] for l in range(DEPTH)])
    off = DEPTH * _SMALL_ELEMS
    rep_all = jnp.concatenate([small_all[j, off:off + _REP_QUARTER] for j in range(N_CHIPS)])
    parts = _unpack(rep_all, [s for _ in range(DEPTH) for _, s in REPLICATED])
    for i, (n, _) in enumerate(REPLICATED):
        grads[n] = jnp.stack([parts[l * len(REPLICATED) + i] for l in range(DEPTH)])

    deltas, new_m, new_v = {}, {}, {}
    for n in WEIGHT_ORDER:
        if n in done:
            grads[n], deltas[n], new_m[n], new_v[n] = done[n]
        else:
            deltas[n], new_m[n], new_v[n] = _adamw(wts[n], grads[n], moms[n], vels[n], "adamw_" + n)
    return (loss, grad_x, *[grads[n] for n in WEIGHT_ORDER], *[deltas[n] for n in WEIGHT_ORDER],
            *[new_m[n] for n in WEIGHT_ORDER], *[new_v[n] for n in WEIGHT_ORDER])
```

```python
import functools
import math

import jax
import jax.numpy as jnp
from jax import lax
from jax.experimental import pallas as pl
from jax.experimental.pallas import tpu as pltpu

F32 = jnp.float32
BF16 = jnp.bfloat16
MXU_DTYPE = jnp.bfloat16

D_MODEL = 1024
HEAD_DIM = 64
N_HEADS = 8
D_ATT = 512
D_CONV = 256
D_SGU = 256
N_GROUPS = 4
CHUNK = 128
D_FF = 2816
DEPTH = 4
RMS_EPS = 1e-6
LN_EPS = 1e-5
N_CHIPS = 4
LANES = 128
PACK_COLS = 1024
HALO = 16

ADAM_LR = 0.001
ADAM_B1 = 0.9
ADAM_B2 = 0.999
ADAM_EPS = 1e-08
ADAM_WD = 0.01
ADAM_STEP = 10

OFF_GL = 0
OFF_Q = 3 * D_MODEL
OFF_K = OFF_Q + D_ATT
OFF_V = OFF_K + D_ATT
OFF_BG = OFF_V + D_ATT
OFF_CG = OFF_BG + D_CONV
OFF_HC = OFF_CG + D_CONV
OFF_U = OFF_HC + D_CONV
OFF_VS = OFF_U + D_SGU
W_P = OFF_VS + D_SGU
F_ROWS = 16

VMEM_LIMIT = 56 * 1024 * 1024
MESH = pl.DeviceIdType.MESH


def _params(sem=None):
    if sem is None:
        return pltpu.CompilerParams(vmem_limit_bytes=VMEM_LIMIT)
    return pltpu.CompilerParams(dimension_semantics=sem, vmem_limit_bytes=VMEM_LIMIT)


def _tile(dim, pref):
    if dim <= pref:
        return dim
    if dim % pref == 0:
        return pref
    return dim


_DIMS = {"nn": (((1,), (0,)), ((), ())), "nt": (((1,), (1,)), ((), ())), "tn": (((0,), (0,)), ((), ()))}


def _mm(a, b, mode, out_dtype, name, tm, tn, tk, chip_of=None):
    if mode == "tn":
        K, M = a.shape
    else:
        M, K = a.shape
    N = b.shape[0] if mode == "nt" else b.shape[1]
    tm, tn, tk = _tile(M, tm), _tile(N // N_CHIPS if chip_of else N, tn), _tile(K, tk)
    nk = K // tk
    dims = _DIMS[mode]

    def body(a_ref, b_ref, o_ref, *acc):
        part = lax.dot_general(a_ref[...].astype(MXU_DTYPE), b_ref[...].astype(MXU_DTYPE), dims,
                               preferred_element_type=F32)
        if nk == 1:
            o_ref[...] = part.astype(o_ref.dtype)
        else:
            acc_ref = acc[0]
            k = pl.program_id(2)

            @pl.when(k == 0)
            def _():
                acc_ref[...] = part

            @pl.when(k > 0)
            def _():
                acc_ref[...] += part

            @pl.when(k == nk - 1)
            def _():
                o_ref[...] = acc_ref[...].astype(o_ref.dtype)

    if mode == "tn":
        a_spec = pl.BlockSpec((tk, tm), lambda i, j, k: (k, i))
    else:
        a_spec = pl.BlockSpec((tm, tk), lambda i, j, k: (i, k))
    if mode == "nt":
        b_spec = pl.BlockSpec((tn, tk), lambda i, j, k: (j, k))
    else:
        b_spec = pl.BlockSpec((tk, tn), lambda i, j, k: (k, j))
    if chip_of is None:
        out_spec = pl.BlockSpec((tm, tn), lambda i, j, k: (i, j))
        out_shape = jax.ShapeDtypeStruct((M, N), out_dtype)
    else:
        per = (N // N_CHIPS) // tn
        out_spec = pl.BlockSpec((None, tm, tn), lambda i, j, k: (chip_of(j // per), i, j % per))
        out_shape = jax.ShapeDtypeStruct((N_CHIPS, M, N // N_CHIPS), out_dtype)
    return pl.pallas_call(
        body,
        name=name,
        grid=(M // tm, N // tn, nk),
        in_specs=[a_spec, b_spec],
        out_specs=out_spec,
        out_shape=out_shape,
        scratch_shapes=[pltpu.VMEM((tm, tn), F32)] if nk > 1 else [],
        compiler_params=_params(("parallel", "parallel", "arbitrary")),
    )(a, b)


_GELU_K = math.sqrt(2.0 / math.pi)
_GELU_C = 0.044715


def _gelu(x):
    t = jnp.tanh(x * (_GELU_K + (_GELU_K * _GELU_C) * (x * x)))
    return x * (0.5 + 0.5 * t)


def _gelu_and_grad(x):
    x2 = x * x
    t = jnp.tanh(x * (_GELU_K + (_GELU_K * _GELU_C) * x2))
    cdf = 0.5 + 0.5 * t
    dcdf = (1.0 - t * t) * (0.5 * _GELU_K + (1.5 * _GELU_K * _GELU_C) * x2)
    return x * cdf, cdf + x * dcdf


def _sigmoid(x):
    return 1.0 / (1.0 + jnp.exp(-x))


def _shift_down(cur, prev, k):
    h = prev.shape[0]
    ext = jnp.concatenate([prev, cur], axis=0)
    return pltpu.roll(ext, k, 0)[h:]


def _shift_up(cur, nxt, k):
    t, h = cur.shape[0], nxt.shape[0]
    ext = jnp.concatenate([cur, nxt], axis=0)
    return pltpu.roll(ext, t + h - k, 0)[:t]


def _row_sum8(x):
    t, c = x.shape
    return jnp.sum(x.reshape(t // 8, 8, c), axis=0)


_DEP = pl.BlockSpec((8, LANES), lambda i: (0, 0))


def _rms_fwd(x, g, name, dep=None):
    s, d = x.shape
    t = _tile(s, 512)

    def body(x_ref, g_ref, *rest):
        o_ref = rest[-1]
        xv = x_ref[...]
        r = lax.rsqrt(jnp.mean(xv * xv, axis=-1, keepdims=True) + RMS_EPS)
        o_ref[...] = (xv * r * g_ref[...]).astype(o_ref.dtype)

    deps = [] if dep is None else list(dep) if isinstance(dep, (list, tuple)) else [dep]
    return pl.pallas_call(
        body, name=name, grid=(s // t,),
        in_specs=[pl.BlockSpec((t, d), lambda i: (i, 0)), pl.BlockSpec((1, d), lambda i: (0, 0))] + [_DEP] * len(deps),
        out_specs=pl.BlockSpec((t, d), lambda i: (i, 0)),
        out_shape=jax.ShapeDtypeStruct((s, d), BF16),
        compiler_params=_params(("parallel",)),
    )(x, g, *deps)


def _resid_post(x, y, g, name):
    s, d = x.shape
    t = _tile(s, 512)

    def body(x_ref, y_ref, g_ref, o_ref):
        yv = y_ref[...]
        r = lax.rsqrt(jnp.mean(yv * yv, axis=-1, keepdims=True) + RMS_EPS)
        o_ref[...] = x_ref[...] + yv * r * g_ref[...]

    row = pl.BlockSpec((t, d), lambda i: (i, 0))
    return pl.pallas_call(
        body, name=name, grid=(s // t,),
        in_specs=[row, row, pl.BlockSpec((1, d), lambda i: (0, 0))],
        out_specs=row,
        out_shape=jax.ShapeDtypeStruct((s, d), F32),
        compiler_params=_params(("parallel",)),
    )(x, y, g)


def _resid_post_norm(x, y, g, g_next, name):
    s, d = x.shape
    t = _tile(s, 512)

    def body(x_ref, y_ref, g_ref, gn_ref, o_ref, xn_ref):
        yv = y_ref[...]
        r = lax.rsqrt(jnp.mean(yv * yv, axis=-1, keepdims=True) + RMS_EPS)
        x1 = x_ref[...] + yv * r * g_ref[...]
        o_ref[...] = x1
        r1 = lax.rsqrt(jnp.mean(x1 * x1, axis=-1, keepdims=True) + RMS_EPS)
        xn_ref[...] = (x1 * r1 * gn_ref[...]).astype(xn_ref.dtype)

    row = pl.BlockSpec((t, d), lambda i: (i, 0))
    vec = pl.BlockSpec((1, d), lambda i: (0, 0))
    return pl.pallas_call(
        body, name=name, grid=(s // t,),
        in_specs=[row, row, vec, vec],
        out_specs=[row, row],
        out_shape=[jax.ShapeDtypeStruct((s, d), F32), jax.ShapeDtypeStruct((s, d), BF16)],
        compiler_params=_params(("parallel",)),
    )(x, y, g, g_next)


def _rms_bwd(xin, g, dys, dres, out_dtype, name, dep=None):
    s, d = xin.shape
    t = _tile(s, 512)
    n = s // t
    n_dy = len(dys)
    has_res = dres is not None
    deps = [] if dep is None else [dep]

    def body(*refs):
        x_ref, g_ref = refs[0], refs[1]
        dy_refs = refs[2:2 + n_dy]
        pos = 2 + n_dy
        res_ref = refs[pos] if has_res else None
        pos += (1 if has_res else 0) + len(deps)
        dx_ref, dg_ref, acc_ref = refs[pos], refs[pos + 1], refs[pos + 2]
        i = pl.program_id(0)
        xv = x_ref[...]
        dy = dy_refs[0][...].astype(F32)
        for extra in dy_refs[1:]:
            dy = dy + extra[...].astype(F32)
        r = lax.rsqrt(jnp.mean(xv * xv, axis=-1, keepdims=True) + RMS_EPS)
        u = dy * g_ref[...]
        xr = xv * r
        dx = r * (u - xr * jnp.mean(u * xr, axis=-1, keepdims=True))
        if has_res:
            dx = dx + res_ref[...]
        dx_ref[...] = dx.astype(dx_ref.dtype)
        part = _row_sum8(dy * xr)

        @pl.when(i == 0)
        def _():
            acc_ref[...] = part

        @pl.when(i > 0)
        def _():
            acc_ref[...] += part

        @pl.when(i == n - 1)
        def _():
            dg_ref[...] = jnp.sum(acc_ref[...], axis=0, keepdims=True)

    row = pl.BlockSpec((t, d), lambda i: (i, 0))
    vec = pl.BlockSpec((1, d), lambda i: (0, 0))
    ins = [xin, g, *dys] + ([dres] if has_res else []) + deps
    return pl.pallas_call(
        body, name=name, grid=(n,),
        in_specs=[row, vec] + [row] * (n_dy + (1 if has_res else 0)) + [_DEP] * len(deps),
        out_specs=[row, vec],
        out_shape=[jax.ShapeDtypeStruct((s, d), out_dtype), jax.ShapeDtypeStruct((1, d), F32)],
        scratch_shapes=[pltpu.VMEM((8, d), F32)],
        compiler_params=_params(("arbitrary",)),
    )(*ins)


def _loss_head(y, target, name):
    s, d = y.shape
    t = _tile(s, 512)
    n = s // t

    def body(y_ref, t_ref, dy_ref, loss_ref, acc_ref):
        i = pl.program_id(0)
        e = y_ref[...] - t_ref[...]
        dy_ref[...] = e * (1.0 / d)
        part = _row_sum8(e * e)

        @pl.when(i == 0)
        def _():
            acc_ref[...] = part

        @pl.when(i > 0)
        def _():
            acc_ref[...] += part

        @pl.when(i == n - 1)
        def _():
            tot = jnp.sum(jnp.sum(acc_ref[...], axis=0, keepdims=True), axis=1, keepdims=True)
            loss_ref[...] = tot * (0.5 / d)

    row = pl.BlockSpec((t, d), lambda i: (i, 0))
    return pl.pallas_call(
        body, name=name, grid=(n,),
        in_specs=[row, row],
        out_specs=[row, pl.BlockSpec((1, 1), lambda i: (0, 0))],
        out_shape=[jax.ShapeDtypeStruct((s, d), F32), jax.ShapeDtypeStruct((1, 1), F32)],
        scratch_shapes=[pltpu.VMEM((8, d), F32)],
        compiler_params=_params(("arbitrary",)),
    )(y, target)


def _split3(x):
    hi = x.astype(BF16)
    r1 = x - hi.astype(F32)
    mid = r1.astype(BF16)
    lo = (r1 - mid.astype(F32)).astype(BF16)
    return hi, mid, lo


def _tri_dot(x, tri):
    hi, mid, lo = _split3(x)
    dn = _DIMS["nn"]
    out = lax.dot_general(hi, tri, dn, preferred_element_type=F32)
    out = out + lax.dot_general(mid, tri, dn, preferred_element_type=F32)
    return out + lax.dot_general(lo, tri, dn, preferred_element_type=F32)


def _log_sigmoid(z):
    return jnp.minimum(z, 0.0) - jnp.log(1.0 + jnp.exp(-jnp.abs(z)))


def _gate_fwd(f_row, b_col, name):
    rows, s = f_row.shape
    t = _tile(s, 512)
    n = s // t

    def body(f_ref, b_ref, ck_ref, carry_ref):
        i = pl.program_id(0)

        @pl.when(i == 0)
        def _():
            carry_ref[...] = jnp.zeros_like(carry_ref)

        logf = _log_sigmoid(f_ref[...] + b_ref[...])
        r = lax.broadcasted_iota(jnp.int32, (t, t), 0)
        c = lax.broadcasted_iota(jnp.int32, (t, t), 1)
        tri = jnp.where(r <= c, 1.0, 0.0).astype(BF16)
        cs = _tri_dot(logf, tri) + carry_ref[...]
        carry_ref[...] = cs[:, t - 1:t]
        terms = [part.astype(F32) for part in _split3(-cs)]
        sub = lax.broadcasted_iota(jnp.int32, (LANES, t), 0)
        for p in range(N_HEADS // 2):
            stacked = jnp.zeros((LANES, t), F32)
            for hh in range(2):
                for j, term in enumerate(terms):
                    h = 2 * p + hh
                    stacked = jnp.where(sub == 3 * hh + j, jnp.broadcast_to(term[h:h + 1, :], (LANES, t)), stacked)
            ck_ref[p] = stacked.T.astype(ck_ref.dtype)

    return pl.pallas_call(
        body, name=name, grid=(n,),
        in_specs=[pl.BlockSpec((rows, t), lambda i: (0, i)), pl.BlockSpec((rows, 1), lambda i: (0, 0))],
        out_specs=pl.BlockSpec((N_HEADS // 2, t, LANES), lambda i: (0, i, 0)),
        out_shape=jax.ShapeDtypeStruct((N_HEADS // 2, s, LANES), BF16),
        scratch_shapes=[pltpu.VMEM((rows, 1), F32)],
        compiler_params=_params(("arbitrary",)),
    )(f_row, b_col)


def _gate_bwd(f_row, b_col, dc_even, dc_odd, name):
    rows, s = f_row.shape
    t = _tile(s, 512)
    n = s // t

    def body(f_ref, b_ref, dce_ref, dco_ref, df_ref, db_ref, carry_ref, acc_ref):
        i = pl.program_id(0)

        @pl.when(i == 0)
        def _():
            carry_ref[...] = jnp.zeros_like(carry_ref)
            acc_ref[...] = jnp.zeros_like(acc_ref)

        head = lax.broadcasted_iota(jnp.int32, (rows, t), 0)
        dcv = jnp.zeros((rows, t), F32)
        for h in range(N_HEADS):
            src = dce_ref if h % 2 == 0 else dco_ref
            dcv = jnp.where(head == h, jnp.broadcast_to(src[h // 2, 0:1, :], (rows, t)), dcv)
        r = lax.broadcasted_iota(jnp.int32, (t, t), 0)
        c = lax.broadcasted_iota(jnp.int32, (t, t), 1)
        tri = jnp.where(r >= c, 1.0, 0.0).astype(BF16)
        dlogf = _tri_dot(dcv, tri) + carry_ref[...]
        carry_ref[...] = dlogf[:, 0:1]
        z = f_ref[...] + b_ref[...]
        df = dlogf * _sigmoid(-z)
        df_ref[...] = df.astype(df_ref.dtype)
        acc_ref[...] += jnp.sum(df, axis=1, keepdims=True)

        @pl.when(i == n - 1)
        def _():
            db_ref[...] = acc_ref[...]

    rev = lambda i: (0, n - 1 - i)
    dc_spec = pl.BlockSpec((N_HEADS // 2, 8, t), lambda i: (0, 0, n - 1 - i))
    return pl.pallas_call(
        body, name=name, grid=(n,),
        in_specs=[pl.BlockSpec((rows, t), rev), pl.BlockSpec((rows, 1), lambda i: (0, 0)), dc_spec, dc_spec],
        out_specs=[pl.BlockSpec((rows, t), rev), pl.BlockSpec((rows, 1), lambda i: (0, 0))],
        out_shape=[jax.ShapeDtypeStruct((rows, s), BF16), jax.ShapeDtypeStruct((rows, 1), F32)],
        scratch_shapes=[pltpu.VMEM((rows, 1), F32), pltpu.VMEM((rows, 1), F32)],
        compiler_params=_params(("arbitrary",)),
    )(f_row, b_col, dc_even, dc_odd)


_NEG = -1e30
_SCALE = HEAD_DIM ** -0.5


def _head_masks():
    lane = lax.broadcasted_iota(jnp.int32, (1, LANES), 1)
    return [lane < HEAD_DIM, lane >= HEAD_DIM]


def _attn_fwd(h, ck, name):
    s = h.shape[0]
    t = _tile(s, 512)
    n = s // t
    qb, kb, vb = OFF_Q // LANES, OFF_K // LANES, OFF_V // LANES

    pairs = [(qi, ki) for qi in range(n) for ki in range(qi + 1)]
    qi_tab = jnp.asarray([qi for qi, _ in pairs], jnp.int32)
    ki_tab = jnp.asarray([ki for _, ki in pairs], jnp.int32)

    def body(qi_ref, ki_ref, q_ref, k_ref, v_ref, ck_ref, o_ref, of_ref, lse_ref, m_ref, l_ref, acc_ref):
        qi, ki = qi_ref[pl.program_id(1)], ki_ref[pl.program_id(1)]
        masks = _head_masks()
        lane = lax.broadcasted_iota(jnp.int32, (1, LANES), 1)

        @pl.when(ki == 0)
        def _():
            m_ref[...] = jnp.full_like(m_ref, _NEG)
            l_ref[...] = jnp.zeros_like(l_ref)
            acc_ref[...] = jnp.zeros_like(acc_ref)

        def step(diag):
            q = q_ref[...] * _SCALE
            k_aug = jnp.concatenate([k_ref[...], ck_ref[0]], axis=1)
            v = v_ref[...]
            nq = max(1, t // 256)
            wq = t // nq
            chains = [(hh, j) for hh in range(2) for j in range(nq)]
            scores = []
            for hh, j in chains:
                qs = q[j * wq:(j + 1) * wq]
                ones = jnp.where((lane >= 3 * hh) & (lane < 3 * hh + 3), 1.0, 0.0).astype(q.dtype)
                q_aug = jnp.concatenate([jnp.where(masks[hh], qs, jnp.zeros_like(qs)),
                                         jnp.broadcast_to(ones, qs.shape)], axis=1)
                scores.append(lax.dot_general(k_aug, q_aug, _DIMS["nt"], preferred_element_type=F32))
            probs = []
            for (hh, j), sc in zip(chains, scores):
                cols = slice(j * wq, (j + 1) * wq)
                if diag:
                    r = lax.broadcasted_iota(jnp.int32, (t, wq), 0)
                    cc = lax.broadcasted_iota(jnp.int32, (t, wq), 1) + j * wq
                    sc = jnp.where(r <= cc, sc, _NEG)
                m_prev = m_ref[hh, :, cols]
                m_new = jnp.maximum(m_prev, jnp.max(sc, axis=0, keepdims=True))
                alpha = jnp.exp(m_prev - m_new)
                p = jnp.exp(sc - m_new)
                l_ref[hh, :, cols] = alpha * l_ref[hh, :, cols] + jnp.sum(p, axis=0, keepdims=True)
                m_ref[hh, :, cols] = m_new
                p_hi = p.astype(MXU_DTYPE)
                p_lo = (p - p_hi.astype(F32)).astype(MXU_DTYPE)
                probs.append((alpha, p_hi, p_lo))
            for (hh, j), (alpha, p_hi, p_lo) in zip(chains, probs):
                pv = (lax.dot_general(v, p_hi, _DIMS["tn"], preferred_element_type=F32)
                      + lax.dot_general(v, p_lo, _DIMS["tn"], preferred_element_type=F32))
                rows = slice(hh * HEAD_DIM, (hh + 1) * HEAD_DIM)
                cols = slice(j * wq, (j + 1) * wq)
                acc_ref[rows, cols] = alpha * acc_ref[rows, cols] + pv[rows]

        @pl.when(ki < qi)
        def _():
            step(False)

        @pl.when(ki == qi)
        def _():
            step(True)
            inv = jnp.concatenate([jnp.broadcast_to(1.0 / l_ref[hh], (HEAD_DIM, t)) for hh in range(2)], axis=0)
            out = (acc_ref[...] * inv).T
            o_ref[...] = out.astype(o_ref.dtype)
            of_ref[...] = out
            lse = jnp.concatenate([jnp.broadcast_to(m_ref[hh] + jnp.log(l_ref[hh]), (HEAD_DIM, t))
                                   for hh in range(2)], axis=0)
            lse_ref[...] = lse.T

    grid_spec = pltpu.PrefetchScalarGridSpec(
        num_scalar_prefetch=2, grid=(N_HEADS // 2, len(pairs)),
        in_specs=[
            pl.BlockSpec((t, LANES), lambda p, i, qt, kt: (qt[i], qb + p)),
            pl.BlockSpec((t, LANES), lambda p, i, qt, kt: (kt[i], kb + p)),
            pl.BlockSpec((t, LANES), lambda p, i, qt, kt: (kt[i], vb + p)),
            pl.BlockSpec((1, t, LANES), lambda p, i, qt, kt: (p, kt[i], 0)),
        ],
        out_specs=[pl.BlockSpec((t, LANES), lambda p, i, qt, kt: (qt[i], p))] * 3,
        scratch_shapes=[pltpu.VMEM((2, 1, t), F32), pltpu.VMEM((2, 1, t), F32), pltpu.VMEM((LANES, t), F32)])
    return pl.pallas_call(
        body, name=name, grid_spec=grid_spec,
        out_shape=[jax.ShapeDtypeStruct((s, D_ATT), BF16), jax.ShapeDtypeStruct((s, D_ATT), F32),
                   jax.ShapeDtypeStruct((s, D_ATT), F32)],
        compiler_params=_params(("parallel", "arbitrary")),
    )(qi_tab, ki_tab, h, h, h, ck)


def _attn_bwd(h, ck, o, lse, do, name):
    s = h.shape[0]
    t = _tile(s, 512)
    n = s // t
    qb, kb, vb = OFF_Q // LANES, OFF_K // LANES, OFF_V // LANES

    pairs = [(ki, qi) for ki in range(n) for qi in range(ki, n)]
    ki_tab = jnp.asarray([ki for ki, _ in pairs], jnp.int32)
    qi_tab = jnp.asarray([qi for _, qi in pairs], jnp.int32)

    def body(ki_ref, qi_ref, q_ref, k_ref, v_ref, ck_ref, o_ref, lse_ref, do_ref,
             dq_ref, dk_ref, dv_ref, dc0_ref, dc1_ref, dk_acc, dv_acc, dc_acc):
        ki, qi = ki_ref[pl.program_id(1)], qi_ref[pl.program_id(1)]
        masks = _head_masks()
        lane = lax.broadcasted_iota(jnp.int32, (1, LANES), 1)

        @pl.when((ki == 0) & (qi == 0))
        def _():
            dq_ref[...] = jnp.zeros_like(dq_ref)

        @pl.when(qi == ki)
        def _():
            dk_acc[...] = jnp.zeros_like(dk_acc)
            dv_acc[...] = jnp.zeros_like(dv_acc)
            dc_acc[...] = jnp.zeros_like(dc_acc)

        def step(diag):
            q = q_ref[...] * _SCALE
            k = k_ref[...]
            v = v_ref[...]
            dov = do_ref[...]
            k_aug = jnp.concatenate([k, ck_ref[0]], axis=1)
            prod_t = (dov.astype(F32) * o_ref[...]).T
            lse_t = lse_ref[...].T
            heads = []
            for hh in range(2):
                mk = masks[hh]
                qh = jnp.where(mk, q, jnp.zeros_like(q))
                kh = jnp.where(mk, k, jnp.zeros_like(k))
                doh = jnp.where(mk, dov, jnp.zeros_like(dov))
                ones = jnp.where((lane >= 3 * hh) & (lane < 3 * hh + 3), 1.0, 0.0).astype(q.dtype)
                q_aug = jnp.concatenate([qh, jnp.broadcast_to(ones, q.shape)], axis=1)
                sc = lax.dot_general(k_aug, q_aug, _DIMS["nt"], preferred_element_type=F32)
                dp = lax.dot_general(v, doh, _DIMS["nt"], preferred_element_type=F32)
                heads.append((qh, kh, doh, sc, dp))
            grads = []
            for hh, (qh, kh, doh, sc, dp) in enumerate(heads):
                rows = slice(hh * HEAD_DIM, (hh + 1) * HEAD_DIM)
                p = jnp.exp(sc - lse_t[hh * HEAD_DIM:hh * HEAD_DIM + 1, :])
                if diag:
                    r = lax.broadcasted_iota(jnp.int32, (t, t), 0)
                    cc = lax.broadcasted_iota(jnp.int32, (t, t), 1)
                    p = jnp.where(r <= cc, p, 0.0)
                delta = jnp.sum(prod_t[rows], axis=0, keepdims=True)
                ds = p * (dp - delta)
                dc_acc[hh] = dc_acc[hh] - jnp.sum(ds, axis=1, keepdims=True)
                grads.append((ds.astype(MXU_DTYPE), p.astype(MXU_DTYPE)))
            dq_blk = jnp.zeros((t, LANES), F32)
            for (qh, kh, doh, _, _), (dsb, pb) in zip(heads, grads):
                dv_acc[...] += lax.dot_general(pb, doh, _DIMS["nn"], preferred_element_type=F32)
                dk_acc[...] += lax.dot_general(dsb, qh, _DIMS["nn"], preferred_element_type=F32)
                dq_blk = dq_blk + lax.dot_general(dsb, kh, _DIMS["tn"], preferred_element_type=F32)
            rows_q = pl.ds(pl.multiple_of(qi * t, t), t)
            dq_ref[rows_q, :] = dq_ref[rows_q, :] + dq_blk * _SCALE

        @pl.when(qi > ki)
        def _():
            step(False)

        @pl.when(qi == ki)
        def _():
            step(True)

        @pl.when(qi == n - 1)
        def _():
            dk_ref[...] = dk_acc[...].astype(dk_ref.dtype)
            dv_ref[...] = dv_acc[...].astype(dv_ref.dtype)
            dc0_ref[0] = jnp.broadcast_to(dc_acc[0], (t, LANES)).T[0:8]
            dc1_ref[0] = jnp.broadcast_to(dc_acc[1], (t, LANES)).T[0:8]

    q_blk = lambda col: pl.BlockSpec((t, LANES), lambda p, i, kt, qt: (qt[i], col(p)))
    k_blk = lambda col: pl.BlockSpec((t, LANES), lambda p, i, kt, qt: (kt[i], col(p)))
    dc_blk = pl.BlockSpec((1, 8, t), lambda p, i, kt, qt: (p, 0, kt[i]))
    grid_spec = pltpu.PrefetchScalarGridSpec(
        num_scalar_prefetch=2, grid=(N_HEADS // 2, len(pairs)),
        in_specs=[q_blk(lambda p: qb + p), k_blk(lambda p: kb + p), k_blk(lambda p: vb + p),
                  pl.BlockSpec((1, t, LANES), lambda p, i, kt, qt: (p, kt[i], 0)),
                  q_blk(lambda p: p), q_blk(lambda p: p), q_blk(lambda p: p)],
        out_specs=[pl.BlockSpec((s, LANES), lambda p, i, kt, qt: (0, p)), k_blk(lambda p: p), k_blk(lambda p: p),
                   dc_blk, dc_blk],
        scratch_shapes=[pltpu.VMEM((t, LANES), F32), pltpu.VMEM((t, LANES), F32), pltpu.VMEM((2, t, 1), F32)])
    return pl.pallas_call(
        body, name=name, grid_spec=grid_spec,
        out_shape=[jax.ShapeDtypeStruct((s, D_ATT), F32), jax.ShapeDtypeStruct((s, D_ATT), BF16),
                   jax.ShapeDtypeStruct((s, D_ATT), BF16), jax.ShapeDtypeStruct((N_HEADS // 2, 8, s), F32),
                   jax.ShapeDtypeStruct((N_HEADS // 2, 8, s), F32)],
        compiler_params=_params(("parallel", "arbitrary")),
    )(ki_tab, qi_tab, h, h, h, ck, o, lse, do)


def _conv3(z, z_prev, w_ref):
    return (w_ref[2:3, :] * z + w_ref[1:2, :] * _shift_down(z, z_prev, 1)
            + w_ref[0:1, :] * _shift_down(z, z_prev, 2))


def _sconv_fwd(h, w, name):
    s = h.shape[0]
    t = _tile(s, 512)
    r = t // HALO
    c = D_CONV
    b_bg, b_cg, b_hc = OFF_BG // c, OFF_CG // c, OFF_HC // c

    def body(bg_ref, cg_ref, hc_ref, cgp_ref, hcp_ref, w_ref, y_ref):
        i = pl.program_id(0)
        live = (i > 0).astype(F32)
        z = cg_ref[...].astype(F32) * hc_ref[...].astype(F32)
        zp = cgp_ref[...].astype(F32) * hcp_ref[...].astype(F32) * live
        y_ref[...] = (bg_ref[...].astype(F32) * _conv3(z, zp, w_ref)).astype(y_ref.dtype)

    cur = lambda b: pl.BlockSpec((t, c), lambda i: (i, b))
    prev = lambda b: pl.BlockSpec((HALO, c), lambda i: (jnp.maximum(i * r - 1, 0), b))
    return pl.pallas_call(
        body, name=name, grid=(s // t,),
        in_specs=[cur(b_bg), cur(b_cg), cur(b_hc), prev(b_cg), prev(b_hc), pl.BlockSpec((8, c), lambda i: (0, 0))],
        out_specs=pl.BlockSpec((t, c), lambda i: (i, 0)),
        out_shape=jax.ShapeDtypeStruct((s, c), BF16),
        compiler_params=_params(("parallel",)),
    )(h, h, h, h, h, w)


def _sconv_bwd(h, w, dy, name):
    s = h.shape[0]
    t = _tile(s, 512)
    n = s // t
    r = t // HALO
    nh = s // HALO
    c = D_CONV
    b_bg, b_cg, b_hc = OFF_BG // c, OFF_CG // c, OFF_HC // c

    def body(bg_ref, cg_ref, hc_ref, cgp_ref, hcp_ref, bgn_ref, dy_ref, dyn_ref, w_ref, d_ref, dw_ref, acc_ref):
        i = pl.program_id(0)
        has_prev = (i > 0).astype(F32)
        has_next = (i < n - 1).astype(F32)
        bg = bg_ref[...].astype(F32)
        cg = cg_ref[...].astype(F32)
        hc = hc_ref[...].astype(F32)
        dyv = dy_ref[...].astype(F32)
        z = cg * hc
        zp = cgp_ref[...].astype(F32) * hcp_ref[...].astype(F32) * has_prev
        z1 = _shift_down(z, zp, 1)
        z2 = _shift_down(z, zp, 2)
        cz = w_ref[2:3, :] * z + w_ref[1:2, :] * z1 + w_ref[0:1, :] * z2
        dcz = dyv * bg
        dczn = dyn_ref[...].astype(F32) * bgn_ref[...].astype(F32) * has_next
        dz = (w_ref[2:3, :] * dcz + w_ref[1:2, :] * _shift_up(dcz, dczn, 1)
              + w_ref[0:1, :] * _shift_up(dcz, dczn, 2))
        d_ref[:, 0:c] = (dyv * cz).astype(d_ref.dtype)
        d_ref[:, c:2 * c] = (dz * hc).astype(d_ref.dtype)
        d_ref[:, 2 * c:3 * c] = (dz * cg).astype(d_ref.dtype)

        @pl.when(i == 0)
        def _():
            acc_ref[...] = jnp.zeros_like(acc_ref)

        acc_ref[0] += _row_sum8(dcz * z2)
        acc_ref[1] += _row_sum8(dcz * z1)
        acc_ref[2] += _row_sum8(dcz * z)

        @pl.when(i == n - 1)
        def _():
            rows = [jnp.sum(acc_ref[k], axis=0, keepdims=True) for k in range(3)]
            dw_ref[...] = jnp.concatenate(rows + [jnp.zeros((5, c), F32)], axis=0)

    cur = lambda b: pl.BlockSpec((t, c), lambda i: (i, b))
    prev = lambda b: pl.BlockSpec((HALO, c), lambda i: (jnp.maximum(i * r - 1, 0), b))
    nxt = lambda b: pl.BlockSpec((HALO, c), lambda i: (jnp.minimum((i + 1) * r, nh - 1), b))
    return pl.pallas_call(
        body, name=name, grid=(n,),
        in_specs=[cur(b_bg), cur(b_cg), cur(b_hc), prev(b_cg), prev(b_hc), nxt(b_bg),
                  cur(0), nxt(0), pl.BlockSpec((8, c), lambda i: (0, 0))],
        out_specs=[pl.BlockSpec((t, 3 * c), lambda i: (i, 0)), pl.BlockSpec((8, c), lambda i: (0, 0))],
        out_shape=[jax.ShapeDtypeStruct((s, 3 * c), BF16), jax.ShapeDtypeStruct((8, c), F32)],
        scratch_shapes=[pltpu.VMEM((3, 8, c), F32)],
        compiler_params=_params(("arbitrary",)),
    )(h, h, h, h, h, h, dy, dy, w)


def _group_masks():
    lane = lax.broadcasted_iota(jnp.int32, (1, D_SGU), 1)
    return [(lane >= g * HEAD_DIM) & (lane < (g + 1) * HEAD_DIM) for g in range(N_GROUPS)]


def _tril_weights(w_ref):
    r = lax.broadcasted_iota(jnp.int32, (CHUNK, CHUNK), 0)
    c = lax.broadcasted_iota(jnp.int32, (CHUNK, CHUNK), 1)
    return [jnp.where(r >= c, w_ref[g], 0.0).astype(MXU_DTYPE) for g in range(N_GROUPS)]


def _sgu_ln(vs, g_ref, b_ref):
    vg, dvg = _gelu_and_grad(vs)
    mu = jnp.mean(vg, axis=-1, keepdims=True)
    xc = vg - mu
    rstd = lax.rsqrt(jnp.mean(xc * xc, axis=-1, keepdims=True) + LN_EPS)
    xhat = xc * rstd
    return xhat * g_ref[...] + b_ref[...], xhat, rstd, dvg


def _sgu_fwd(h, ln_g, ln_b, w_s, bias, name):
    s = h.shape[0]
    t = _tile(s, 512)
    c = D_SGU
    b_u, b_v = OFF_U // c, OFF_VS // c

    def body(u_ref, v_ref, g_ref, b_ref, w_ref, bias_ref, y_ref):
        gm = _group_masks()
        wm = _tril_weights(w_ref)
        ug = _gelu(u_ref[...].astype(F32))
        vn, _, _, _ = _sgu_ln(v_ref[...].astype(F32), g_ref, b_ref)
        vnb = vn.astype(MXU_DTYPE)
        for ch in range(t // CHUNK):
            rows = slice(ch * CHUNK, (ch + 1) * CHUNK)
            mixed = bias_ref[...]
            for g in range(N_GROUPS):
                mg = lax.dot_general(wm[g], vnb[rows], _DIMS["nn"], preferred_element_type=F32)
                mixed = jnp.where(gm[g], mixed + mg, mixed)
            y_ref[rows, :] = (ug[rows] * mixed).astype(y_ref.dtype)

    full = lambda shp: pl.BlockSpec(shp, lambda i: (0,) * len(shp))
    return pl.pallas_call(
        body, name=name, grid=(s // t,),
        in_specs=[pl.BlockSpec((t, c), lambda i: (i, b_u)), pl.BlockSpec((t, c), lambda i: (i, b_v)),
                  full((1, c)), full((1, c)), full((N_GROUPS, CHUNK, CHUNK)), full((CHUNK, c))],
        out_specs=pl.BlockSpec((t, c), lambda i: (i, 0)),
        out_shape=jax.ShapeDtypeStruct((s, c), BF16),
        compiler_params=_params(("parallel",)),
    )(h, h, ln_g, ln_b, w_s, bias)


def _sgu_bwd(h, ln_g, ln_b, w_s, bias, dy, name):
    s = h.shape[0]
    t = _tile(s, 512)
    n = s // t
    c = D_SGU
    b_u, b_v = OFF_U // c, OFF_VS // c

    def body(u_ref, v_ref, g_ref, b_ref, w_ref, bias_ref, dy_ref,
             d_ref, dg_ref, db_ref, dw_ref, dbias_ref, dg_acc, db_acc):
        i = pl.program_id(0)
        gm = _group_masks()
        wm = _tril_weights(w_ref)

        @pl.when(i == 0)
        def _():
            dg_acc[...] = jnp.zeros_like(dg_acc)
            db_acc[...] = jnp.zeros_like(db_acc)
            dw_ref[...] = jnp.zeros_like(dw_ref)
            dbias_ref[...] = jnp.zeros_like(dbias_ref)

        ug, dug = _gelu_and_grad(u_ref[...].astype(F32))
        vn, xhat, rstd, dvg = _sgu_ln(v_ref[...].astype(F32), g_ref, b_ref)
        vnb = vn.astype(MXU_DTYPE)
        dyv = dy_ref[...].astype(F32)
        dmixed = dyv * ug
        dmb = dmixed.astype(MXU_DTYPE)
        dvn_parts = []
        for ch in range(t // CHUNK):
            rows = slice(ch * CHUNK, (ch + 1) * CHUNK)
            mixed = bias_ref[...]
            dvn = jnp.zeros((CHUNK, c), F32)
            for g in range(N_GROUPS):
                mg = lax.dot_general(wm[g], vnb[rows], _DIMS["nn"], preferred_element_type=F32)
                mixed = jnp.where(gm[g], mixed + mg, mixed)
                dvn = jnp.where(gm[g], lax.dot_general(wm[g], dmb[rows], _DIMS["tn"], preferred_element_type=F32),
                                dvn)
                dmg = jnp.where(gm[g], dmb[rows], jnp.zeros_like(dmb[rows]))
                dw_ref[g] += lax.dot_general(dmg, vnb[rows], _DIMS["nt"], preferred_element_type=F32)
            d_ref[rows, 0:c] = (dyv[rows] * mixed * dug[rows]).astype(d_ref.dtype)
            dbias_ref[...] += dmixed[rows]
            dvn_parts.append(dvn)
        dvn = jnp.concatenate(dvn_parts, axis=0)
        dg_acc[...] += _row_sum8(dvn * xhat)
        db_acc[...] += _row_sum8(dvn)
        dxh = dvn * g_ref[...]
        dvgl = rstd * (dxh - jnp.mean(dxh, axis=-1, keepdims=True)
                       - xhat * jnp.mean(dxh * xhat, axis=-1, keepdims=True))
        d_ref[:, c:2 * c] = (dvgl * dvg).astype(d_ref.dtype)

        @pl.when(i == n - 1)
        def _():
            dg_ref[...] = jnp.sum(dg_acc[...], axis=0, keepdims=True)
            db_ref[...] = jnp.sum(db_acc[...], axis=0, keepdims=True)
            r = lax.broadcasted_iota(jnp.int32, (CHUNK, CHUNK), 0)
            cc = lax.broadcasted_iota(jnp.int32, (CHUNK, CHUNK), 1)
            for g in range(N_GROUPS):
                dw_ref[g] = jnp.where(r >= cc, dw_ref[g], 0.0)

    full = lambda shp: pl.BlockSpec(shp, lambda i: (0,) * len(shp))
    return pl.pallas_call(
        body, name=name, grid=(n,),
        in_specs=[pl.BlockSpec((t, c), lambda i: (i, b_u)), pl.BlockSpec((t, c), lambda i: (i, b_v)),
                  full((1, c)), full((1, c)), full((N_GROUPS, CHUNK, CHUNK)), full((CHUNK, c)),
                  pl.BlockSpec((t, c), lambda i: (i, 0))],
        out_specs=[pl.BlockSpec((t, 2 * c), lambda i: (i, 0)), full((1, c)), full((1, c)),
                   full((N_GROUPS, CHUNK, CHUNK)), full((CHUNK, c))],
        out_shape=[jax.ShapeDtypeStruct((s, 2 * c), BF16), jax.ShapeDtypeStruct((1, c), F32),
                   jax.ShapeDtypeStruct((1, c), F32), jax.ShapeDtypeStruct((N_GROUPS, CHUNK, CHUNK), F32),
                   jax.ShapeDtypeStruct((CHUNK, c), F32)],
        scratch_shapes=[pltpu.VMEM((8, c), F32), pltpu.VMEM((8, c), F32)],
        compiler_params=_params(("arbitrary",)),
    )(h, h, ln_g, ln_b, w_s, bias, dy)


def _merge_fwd(h, acts, ws, b_gate, name):
    s = h.shape[0]
    d = D_MODEL
    t = _tile(s, 512)

    def body(gl0, gl1, gl2, a0, a1, a2, w0, w1, w2, b_ref, o_ref):
        acc = jnp.zeros((t, d), F32)
        for i, (gl, a, w) in enumerate(((gl0, a0, w0), (gl1, a1, w1), (gl2, a2, w2))):
            y = lax.dot_general(a[...], w[...], _DIMS["nn"], preferred_element_type=F32)
            acc = acc + _sigmoid(gl[...].astype(F32) + b_ref[i:i + 1, :]) * y
        o_ref[...] = acc.astype(o_ref.dtype)

    full = lambda arr: pl.BlockSpec(arr.shape, lambda i: (0, 0))
    return pl.pallas_call(
        body, name=name, grid=(s // t,),
        in_specs=[pl.BlockSpec((t, d), lambda i, b=b: (i, b)) for b in range(3)]
                 + [pl.BlockSpec((t, a.shape[1]), lambda i: (i, 0)) for a in acts]
                 + [full(w) for w in ws] + [full(b_gate)],
        out_specs=pl.BlockSpec((t, d), lambda i: (i, 0)),
        out_shape=jax.ShapeDtypeStruct((s, d), BF16),
        compiler_params=_params(("parallel",)),
    )(h, h, h, *acts, *ws, b_gate)


def _merge_bwd(h, acts, ws, b_gate, dmerged, name):
    s = h.shape[0]
    d = D_MODEL
    t = _tile(s, 512)
    n = s // t

    def body(gl0, gl1, gl2, a0, a1, a2, w0, w1, w2, b_ref, dm_ref, dy0, dy1, dy2, dgl_ref, db_ref, acc_ref):
        step = pl.program_id(0)

        @pl.when(step == 0)
        def _():
            acc_ref[...] = jnp.zeros_like(acc_ref)

        dm = dm_ref[...]
        for i, (gl, a, w, dy) in enumerate(((gl0, a0, w0, dy0), (gl1, a1, w1, dy1), (gl2, a2, w2, dy2))):
            y = lax.dot_general(a[...], w[...], _DIMS["nn"], preferred_element_type=F32)
            gate = _sigmoid(gl[...].astype(F32) + b_ref[i:i + 1, :])
            dy[...] = (dm * gate).astype(dy.dtype)
            dgl = dm * y * (gate * (1.0 - gate))
            dgl_ref[:, i * d:(i + 1) * d] = dgl.astype(dgl_ref.dtype)
            acc_ref[i] += _row_sum8(dgl)

        @pl.when(step == n - 1)
        def _():
            rows = [jnp.sum(acc_ref[k], axis=0, keepdims=True) for k in range(3)]
            db_ref[...] = jnp.concatenate(rows + [jnp.zeros((5, d), F32)], axis=0)

    full = lambda arr: pl.BlockSpec(arr.shape, lambda i: (0, 0))
    row = pl.BlockSpec((t, d), lambda i: (i, 0))
    return pl.pallas_call(
        body, name=name, grid=(n,),
        in_specs=[pl.BlockSpec((t, d), lambda i, b=b: (i, b)) for b in range(3)]
                 + [pl.BlockSpec((t, a.shape[1]), lambda i: (i, 0)) for a in acts]
                 + [full(w) for w in ws] + [full(b_gate), row],
        out_specs=[row, row, row, pl.BlockSpec((t, 3 * d), lambda i: (i, 0)), pl.BlockSpec((8, d), lambda i: (0, 0))],
        out_shape=[jax.ShapeDtypeStruct((s, d), BF16)] * 3
                  + [jax.ShapeDtypeStruct((s, IN_PAD), BF16), jax.ShapeDtypeStruct((8, d), F32)],
        scratch_shapes=[pltpu.VMEM((3, 8, d), F32)],
        compiler_params=_params(("arbitrary",)),
    )(h, h, h, *acts, *ws, b_gate, dmerged)


FF_BLK = D_FF // 2


def _ffn_act_fwd(h2, w, name):
    s = h2.shape[0]
    t = _tile(s, 512)
    r = t // HALO
    cw = 2 * FF_BLK

    def body(x_ref, xp_ref, w_ref, p_ref):
        i = pl.program_id(0)
        live = (i > 0).astype(F32)
        hc = _conv3(x_ref[...].astype(F32), xp_ref[...].astype(F32) * live, w_ref)
        p_ref[...] = (_gelu(hc[:, :FF_BLK]) * hc[:, FF_BLK:]).astype(p_ref.dtype)

    return pl.pallas_call(
        body, name=name, grid=(s // t, 2),
        in_specs=[pl.BlockSpec((t, cw), lambda i, j: (i, j)),
                  pl.BlockSpec((HALO, cw), lambda i, j: (jnp.maximum(i * r - 1, 0), j)),
                  pl.BlockSpec((8, cw), lambda i, j: (0, j))],
        out_specs=pl.BlockSpec((t, FF_BLK), lambda i, j: (i, j)),
        out_shape=jax.ShapeDtypeStruct((s, D_FF), BF16),
        compiler_params=_params(("parallel", "parallel")),
    )(h2, h2, w)


def _ffn_act_conv_bwd(h2, w, dp, name):
    s = h2.shape[0]
    t = _tile(s, 512)
    n = s // t
    r = t // HALO
    nh = s // HALO
    cw = 2 * FF_BLK

    def body(x_ref, xp_ref, xn_ref, dp_ref, dpn_ref, w_ref, dx_ref, dw_ref, acc_ref):
        i = pl.program_id(1)
        has_prev = (i > 0).astype(F32)
        has_next = (i < n - 1).astype(F32)
        x = jnp.concatenate([x_ref[...].astype(F32), xn_ref[...].astype(F32)], axis=0)
        xp = xp_ref[...].astype(F32) * has_prev
        x1 = _shift_down(x, xp, 1)
        x2 = _shift_down(x, xp, 2)
        hc = w_ref[2:3, :] * x + w_ref[1:2, :] * x1 + w_ref[0:1, :] * x2
        ga, dga = _gelu_and_grad(hc[:, :FF_BLK])
        dpv = jnp.concatenate([dp_ref[...].astype(F32), dpn_ref[...].astype(F32) * has_next], axis=0)
        dhc = jnp.concatenate([dpv * hc[:, FF_BLK:] * dga, dpv * ga], axis=1)
        cur, nxt = dhc[:t], dhc[t:]
        dx = w_ref[2:3, :] * cur + w_ref[1:2, :] * _shift_up(cur, nxt, 1) + w_ref[0:1, :] * _shift_up(cur, nxt, 2)
        dx_ref[...] = dx.astype(dx_ref.dtype)

        @pl.when(i == 0)
        def _():
            acc_ref[...] = jnp.zeros_like(acc_ref)

        acc_ref[0] += _row_sum8(cur * x2[:t])
        acc_ref[1] += _row_sum8(cur * x1[:t])
        acc_ref[2] += _row_sum8(cur * x[:t])

        @pl.when(i == n - 1)
        def _():
            rows = [jnp.sum(acc_ref[k], axis=0, keepdims=True) for k in range(3)]
            dw_ref[...] = jnp.concatenate(rows + [jnp.zeros((5, cw), F32)], axis=0)

    nxt_row = lambda j, i: jnp.minimum((i + 1) * r, nh - 1)
    return pl.pallas_call(
        body, name=name, grid=(2, n),
        in_specs=[pl.BlockSpec((t, cw), lambda j, i: (i, j)),
                  pl.BlockSpec((HALO, cw), lambda j, i: (jnp.maximum(i * r - 1, 0), j)),
                  pl.BlockSpec((HALO, cw), lambda j, i: (nxt_row(j, i), j)),
                  pl.BlockSpec((t, FF_BLK), lambda j, i: (i, j)),
                  pl.BlockSpec((HALO, FF_BLK), lambda j, i: (nxt_row(j, i), j)),
                  pl.BlockSpec((8, cw), lambda j, i: (0, j))],
        out_specs=[pl.BlockSpec((t, cw), lambda j, i: (i, j)), pl.BlockSpec((8, cw), lambda j, i: (0, j))],
        out_shape=[jax.ShapeDtypeStruct((s, 2 * D_FF), BF16), jax.ShapeDtypeStruct((8, 2 * D_FF), F32)],
        scratch_shapes=[pltpu.VMEM((3, 8, cw), F32)],
        compiler_params=_params(("parallel", "arbitrary")),
    )(h2, h2, h2, dp, dp, w)


def _adamw(w, g, m, v, name, dep=None):
    shape = w.shape
    c = shape[-1]
    rows = math.prod(shape[:-1])
    to2d = lambda a: a.reshape(rows, c)
    cap = max(8, (1 << 18) // c)
    tr = rows
    for cand in (2048, 1024, 512, 256, 128, 64, 32, 16, 8):
        if cand <= cap and rows % cand == 0:
            tr = cand
            break

    deps = [] if dep is None else [dep]

    def body(w_ref, g_ref, m_ref, v_ref, *rest):
        d_ref, nm_ref, nv_ref = rest[len(deps):]
        gv = g_ref[...]
        nm = ADAM_B1 * m_ref[...] + (1.0 - ADAM_B1) * gv
        nv = ADAM_B2 * v_ref[...] + (1.0 - ADAM_B2) * (gv * gv)
        m_hat = nm / (1.0 - ADAM_B1 ** ADAM_STEP)
        v_hat = nv / (1.0 - ADAM_B2 ** ADAM_STEP)
        d_ref[...] = -ADAM_LR * (m_hat / (jnp.sqrt(v_hat) + ADAM_EPS) + ADAM_WD * w_ref[...])
        nm_ref[...] = nm
        nv_ref[...] = nv

    blk = pl.BlockSpec((tr, c), lambda i: (i, 0))
    outs = pl.pallas_call(
        body, name=name, grid=(rows // tr,),
        in_specs=[blk] * 4 + [_DEP] * len(deps), out_specs=[blk] * 3,
        out_shape=[jax.ShapeDtypeStruct((rows, c), F32)] * 3,
        compiler_params=_params(("parallel",)),
    )(to2d(w), to2d(g), to2d(m), to2d(v), *deps)
    return tuple(o.reshape(shape) for o in outs)


_ANY = pl.BlockSpec(memory_space=pl.ANY)


def _place():
    x, y, c = lax.axis_index("x"), lax.axis_index("y"), lax.axis_index("c")
    others = [(1 - x, y), (x, 1 - y), (1 - x, 1 - y)]
    return x, y, c, others


def _all_gather_chips(shard, name):
    rws, cols = shard.shape
    half = rws // 2

    def body(x_ref, out_ref, send_sems, recv_sems, local_sem):
        x, y, c, others = _place()
        me = 2 * x + y
        sib = (x, y, 1 - c)

        def rows(chip, cc):
            return out_ref.at[chip, pl.ds(pl.multiple_of(cc * half, 16), half), :]

        def copy(k, src, dst, to):
            return pltpu.make_async_remote_copy(src_ref=src, dst_ref=dst, send_sem=send_sems.at[k],
                                                recv_sem=recv_sems.at[k], device_id=to, device_id_type=MESH)

        mine = pltpu.make_async_copy(x_ref, out_ref.at[me], local_sem)
        mine.start()
        my_half = x_ref.at[pl.ds(pl.multiple_of(c * half, 16), half), :]
        first = [copy(j, my_half, rows(me, c), (ox, oy, c)) for j, (ox, oy) in enumerate(others)]
        for cp in first:
            cp.start()
        passed = []
        for j, (ox, oy) in enumerate(others):
            blk = rows(2 * ox + oy, c)
            copy(j, blk, blk, (x, y, c)).wait_recv()
            fwd = copy(3 + j, blk, blk, sib)
            fwd.start()
            passed.append(fwd)
        for j, (ox, oy) in enumerate(others):
            blk = rows(2 * ox + oy, 1 - c)
            copy(3 + j, blk, blk, (x, y, c)).wait_recv()
        for cp in first + passed:
            cp.wait_send()
        mine.wait()

    return pl.pallas_call(
        body, name=name,
        in_specs=[_ANY], out_specs=_ANY,
        out_shape=jax.ShapeDtypeStruct((N_CHIPS, rws, cols), shard.dtype),
        scratch_shapes=[pltpu.SemaphoreType.DMA((6,)), pltpu.SemaphoreType.DMA((6,)), pltpu.SemaphoreType.DMA],
        compiler_params=pltpu.CompilerParams(has_side_effects=True),
    )(shard)


def _swap_halves(buf, name, dep=None):
    nb, rws, cols = buf.shape
    half = rws // 2
    deps = [] if dep is None else [dep]

    def body(b_ref, *rest):
        own_ref, sib_ref, send_sem, recv_sem, local_sem = rest[len(deps):]
        x, y, c, _ = _place()
        keep = b_ref.at[:, pl.ds(pl.multiple_of(c * half, 16), half), :]
        give = b_ref.at[:, pl.ds(pl.multiple_of((1 - c) * half, 16), half), :]
        mine = pltpu.make_async_copy(keep, own_ref, local_sem)
        mine.start()
        cp = pltpu.make_async_remote_copy(src_ref=give, dst_ref=sib_ref, send_sem=send_sem, recv_sem=recv_sem,
                                          device_id=(x, y, 1 - c), device_id_type=MESH)
        cp.start()
        cp.wait()
        mine.wait()

    shp = jax.ShapeDtypeStruct((nb, half, cols), buf.dtype)
    return pl.pallas_call(
        body, name=name,
        in_specs=[_ANY] * (1 + len(deps)), out_specs=[_ANY, _ANY], out_shape=[shp, shp],
        scratch_shapes=[pltpu.SemaphoreType.DMA, pltpu.SemaphoreType.DMA, pltpu.SemaphoreType.DMA],
        compiler_params=pltpu.CompilerParams(has_side_effects=True),
    )(buf, *deps)


def _add2(a, b, name):
    nb, rws, cols = a.shape
    t = _tile(rws, 256)
    if rws % t:
        t = rws

    def body(a_ref, b_ref, o_ref):
        o_ref[...] = (a_ref[...].astype(F32) + b_ref[...].astype(F32)).astype(o_ref.dtype)

    blk = pl.BlockSpec((1, t, cols), lambda i, j: (i, j, 0))
    return pl.pallas_call(
        body, name=name, grid=(nb, rws // t), in_specs=[blk, blk], out_specs=blk,
        out_shape=jax.ShapeDtypeStruct(a.shape, a.dtype),
        compiler_params=_params(("parallel", "parallel")),
    )(a, b)


def _exchange_chips(pre, name):
    nb, half, cols = pre.shape

    def body(p_ref, out_ref, send_sems, recv_sems, local_sem):
        x, y, c, others = _place()
        me = 2 * x + y
        mine = pltpu.make_async_copy(p_ref.at[me], out_ref.at[me], local_sem)
        mine.start()
        sends = []
        for j, (ox, oy) in enumerate(others):
            cp = pltpu.make_async_remote_copy(src_ref=p_ref.at[2 * ox + oy], dst_ref=out_ref.at[me],
                                              send_sem=send_sems.at[j], recv_sem=recv_sems.at[j],
                                              device_id=(ox, oy, c), device_id_type=MESH)
            cp.start()
            sends.append(cp)
        for j, (ox, oy) in enumerate(others):
            blk = out_ref.at[2 * ox + oy]
            pltpu.make_async_remote_copy(src_ref=blk, dst_ref=blk, send_sem=send_sems.at[j],
                                         recv_sem=recv_sems.at[j], device_id=(x, y, c),
                                         device_id_type=MESH).wait_recv()
        for cp in sends:
            cp.wait_send()
        mine.wait()

    return pl.pallas_call(
        body, name=name,
        in_specs=[_ANY], out_specs=_ANY, out_shape=jax.ShapeDtypeStruct(pre.shape, pre.dtype),
        scratch_shapes=[pltpu.SemaphoreType.DMA((3,)), pltpu.SemaphoreType.DMA((3,)), pltpu.SemaphoreType.DMA],
        compiler_params=pltpu.CompilerParams(has_side_effects=True),
    )(pre)


def _add4(parts, name):
    nb, half, cols = parts.shape
    t = _tile(half, 256)
    if half % t:
        t = half

    def body(p_ref, o_ref):
        acc = p_ref[0].astype(F32)
        for k in range(1, nb):
            acc = acc + p_ref[k].astype(F32)
        o_ref[...] = acc

    return pl.pallas_call(
        body, name=name, grid=(half // t,),
        in_specs=[pl.BlockSpec((nb, t, cols), lambda i: (0, i, 0))],
        out_specs=pl.BlockSpec((t, cols), lambda i: (i, 0)),
        out_shape=jax.ShapeDtypeStruct((half, cols), F32),
        compiler_params=_params(("parallel",)),
    )(parts)


def _join_halves(mine_half, name):
    half, cols = mine_half.shape

    def body(h_ref, out_ref, send_sem, recv_sem, local_sem):
        x, y, c, _ = _place()
        dst = out_ref.at[pl.ds(pl.multiple_of(c * half, 8), half), :]
        mine = pltpu.make_async_copy(h_ref, dst, local_sem)
        mine.start()
        cp = pltpu.make_async_remote_copy(src_ref=h_ref, dst_ref=dst, send_sem=send_sem, recv_sem=recv_sem,
                                          device_id=(x, y, 1 - c), device_id_type=MESH)
        cp.start()
        cp.wait()
        mine.wait()

    return pl.pallas_call(
        body, name=name,
        in_specs=[_ANY], out_specs=_ANY, out_shape=jax.ShapeDtypeStruct((2 * half, cols), mine_half.dtype),
        scratch_shapes=[pltpu.SemaphoreType.DMA, pltpu.SemaphoreType.DMA, pltpu.SemaphoreType.DMA],
        compiler_params=pltpu.CompilerParams(has_side_effects=True),
    )(mine_half)


def _reduce_scatter_chips(buf, tag, dep=None):
    own, sib = _swap_halves(buf, "rs_swap_" + tag, dep)
    pre = _add2(own, sib, "rs_add2_" + tag)
    parts = _exchange_chips(pre, "rs_xchg_" + tag)
    red = _add4(parts, "rs_add4_" + tag)
    return _join_halves(red, "rs_join_" + tag)


MAX_DMA_BYTES = 2 * 1024 * 1024
ROW_ALIGN = 16


def _pieces(rows, row_bytes):
    n = max(1, -(-(rows * row_bytes) // MAX_DMA_BYTES))
    step = -(-(-(-rows // n)) // ROW_ALIGN) * ROW_ALIGN
    return [(r, min(step, rows - r)) for r in range(0, rows, step)]


def _half_plan(arrays, row_axis):
    plan = []
    for a, arr in enumerate(arrays):
        row_bytes = math.prod(arr.shape[row_axis + 1:]) * arr.dtype.itemsize * (arr.shape[0] if row_axis else 1)
        plan += [(a, r0, nr) for r0, nr in _pieces(arr.shape[row_axis] // 2, row_bytes)]
    return plan


def _rows(start, size):
    return pl.ds(pl.multiple_of(start, ROW_ALIGN), size)


def _remote(src, dst, send_sems, recv_sems, k, to):
    return pltpu.make_async_remote_copy(src_ref=src, dst_ref=dst, send_sem=send_sems.at[k], recv_sem=recv_sems.at[k],
                                        device_id=to, device_id_type=MESH)


def _comm_call(body, name, ins, out_shapes, n_remote, n_local, aliases=None):
    return pl.pallas_call(
        body, name=name,
        in_specs=[_ANY] * len(ins), out_specs=[_ANY] * len(out_shapes), out_shape=out_shapes,
        scratch_shapes=[pltpu.SemaphoreType.DMA((n_remote,)), pltpu.SemaphoreType.DMA((n_remote,)),
                        pltpu.SemaphoreType.DMA((max(n_local, 1),))],
        input_output_aliases=aliases or {},
        compiler_params=pltpu.CompilerParams(has_side_effects=True),
    )(*ins)


def _cast_shard(w, l, me_idx, name):
    _, k, cols = w.shape
    tr = _tile(k, 256)
    if k % tr:
        tr = k

    def body(me_ref, w_ref, s_ref, land_ref):
        del me_ref
        v = w_ref[...].astype(BF16)
        s_ref[...] = v
        land_ref[...] = v

    grid_spec = pltpu.PrefetchScalarGridSpec(
        num_scalar_prefetch=1, grid=(k // tr,),
        in_specs=[pl.BlockSpec((None, tr, cols), lambda i, me: (l, i, 0))],
        out_specs=[pl.BlockSpec((tr, cols), lambda i, me: (i, 0)),
                   pl.BlockSpec((None, tr, cols), lambda i, me: (me[0], i, 0))])
    return pl.pallas_call(
        body, name=name, grid_spec=grid_spec,
        out_shape=[jax.ShapeDtypeStruct((k, cols), BF16), jax.ShapeDtypeStruct((N_CHIPS, k, cols), BF16)],
        compiler_params=_params(("parallel",)),
    )(me_idx, w)


def _gather_d2d(lands, name):
    n = len(lands)
    plan = _half_plan(lands, 1)
    plan = [(a, r0, nr) for a, r0, nr in plan]

    def body(*refs):
        out_refs = refs[n:2 * n]
        send_sems, recv_sems, _ = refs[2 * n:]
        x, y, c, others = _place()
        sends = []
        for i, (a, r0, nr) in enumerate(plan):
            rows = _rows(c * (lands[a].shape[1] // 2) + r0, nr)
            for j, (ox, oy) in enumerate(others):
                blk = out_refs[a].at[2 * ox + oy, rows, :]
                cp = _remote(blk, blk, send_sems, recv_sems, 3 * i + j, (x, y, 1 - c))
                cp.start()
                sends.append(cp)
        for i, (a, r0, nr) in enumerate(plan):
            rows = _rows((1 - c) * (lands[a].shape[1] // 2) + r0, nr)
            for j, (ox, oy) in enumerate(others):
                blk = out_refs[a].at[2 * ox + oy, rows, :]
                _remote(blk, blk, send_sems, recv_sems, 3 * i + j, (x, y, c)).wait_recv()
        for cp in sends:
            cp.wait_send()

    outs = [jax.ShapeDtypeStruct(a.shape, a.dtype) for a in lands]
    return _comm_call(body, name, lands, outs, 3 * len(plan), 0, aliases={a: a for a in range(n)})


def _rs_swap(ts, name):
    n = len(ts)
    plan = _half_plan(ts, 1)

    def body(*refs):
        t_refs, out_refs = refs[:n], refs[n:2 * n]
        send_sems, recv_sems, _ = refs[2 * n:]
        x, y, c, _o = _place()
        sends = []
        for i, (a, r0, nr) in enumerate(plan):
            src = t_refs[a].at[:, _rows((1 - c) * (ts[a].shape[1] // 2) + r0, nr), :]
            cp = _remote(src, out_refs[a].at[:, pl.ds(r0, nr), :], send_sems, recv_sems, i, (x, y, 1 - c))
            cp.start()
            sends.append(cp)
        for i, (a, r0, nr) in enumerate(plan):
            blk = out_refs[a].at[:, pl.ds(r0, nr), :]
            _remote(blk, blk, send_sems, recv_sems, i, (x, y, c)).wait_recv()
        for cp in sends:
            cp.wait_send()

    outs = [jax.ShapeDtypeStruct((t.shape[0], t.shape[1] // 2, t.shape[2]), t.dtype) for t in ts]
    return _comm_call(body, name, ts, outs, len(plan), 0)


def _add_halves(ts, gots, c_idx, me_idx, name):
    n = len(ts)

    def body(c_ref, me_ref, *refs):
        del c_ref
        t_refs, g_refs = refs[:n], refs[n:2 * n]
        o_refs, mine_refs = refs[2 * n:3 * n], refs[3 * n:]
        for t_ref, g_ref, o_ref, mine_ref in zip(t_refs, g_refs, o_refs, mine_refs):
            v = (t_ref[...].astype(F32) + g_ref[...].astype(F32)).astype(o_ref.dtype)
            o_ref[...] = v

            @pl.when(pl.program_id(0) == me_ref[0])
            def _():
                mine_ref[...] = v

    blks = [(1, g.shape[1], g.shape[2]) for g in gots]
    same = [pl.BlockSpec(b, lambda i, c, me: (i, 0, 0)) for b in blks]
    grid_spec = pltpu.PrefetchScalarGridSpec(
        num_scalar_prefetch=2, grid=(N_CHIPS,),
        in_specs=[pl.BlockSpec(b, lambda i, c, me: (i, c[0], 0)) for b in blks] + same,
        out_specs=same + [pl.BlockSpec(b, lambda i, c, me: (me[0], 0, 0)) for b in blks])
    shapes = [jax.ShapeDtypeStruct(g.shape, g.dtype) for g in gots]
    outs = pl.pallas_call(
        body, name=name, grid_spec=grid_spec, out_shape=shapes + shapes,
        compiler_params=_params(("arbitrary",)),
    )(c_idx, me_idx, *ts, *gots)
    return outs[:n], outs[n:]


def _add4_halves(parts, c_idx, name):
    n = len(parts)
    steps = 2

    def body(c_ref, *refs):
        del c_ref
        for p_ref, o_ref in zip(refs[:n], refs[n:]):
            acc = p_ref[0].astype(F32)
            for k in range(1, N_CHIPS):
                acc = acc + p_ref[k].astype(F32)
            o_ref[...] = acc

    grid_spec = pltpu.PrefetchScalarGridSpec(
        num_scalar_prefetch=1, grid=(steps,),
        in_specs=[pl.BlockSpec((N_CHIPS, p.shape[1] // steps, p.shape[2]), lambda i, c: (0, i, 0)) for p in parts],
        out_specs=[pl.BlockSpec((p.shape[1] // steps, p.shape[2]), lambda i, c: (c[0] * steps + i, 0))
                   for p in parts])
    return pl.pallas_call(
        body, name=name, grid_spec=grid_spec,
        out_shape=[jax.ShapeDtypeStruct((2 * p.shape[1], p.shape[2]), F32) for p in parts],
        compiler_params=_params(("parallel",)),
    )(c_idx, *parts)


def _rs_join(fulls, name):
    n = len(fulls)
    plan = _half_plan(fulls, 0)

    def body(*refs):
        out_refs = refs[n:2 * n]
        send_sems, recv_sems, _ = refs[2 * n:]
        x, y, c, _o = _place()
        sends = []
        for i, (a, r0, nr) in enumerate(plan):
            blk = out_refs[a].at[_rows(c * (fulls[a].shape[0] // 2) + r0, nr), :]
            cp = _remote(blk, blk, send_sems, recv_sems, i, (x, y, 1 - c))
            cp.start()
            sends.append(cp)
        for i, (a, r0, nr) in enumerate(plan):
            blk = out_refs[a].at[_rows((1 - c) * (fulls[a].shape[0] // 2) + r0, nr), :]
            _remote(blk, blk, send_sems, recv_sems, i, (x, y, c)).wait_recv()
        for cp in sends:
            cp.wait_send()

    outs = [jax.ShapeDtypeStruct(f.shape, f.dtype) for f in fulls]
    return _comm_call(body, name, fulls, outs, len(plan), 0, aliases={a: a for a in range(n)})


_HBM = pl.BlockSpec(memory_space=pltpu.HBM)
_SEM = pl.BlockSpec(memory_space=pltpu.SEMAPHORE)
_EFFECT = pltpu.SideEffectType.DATAFLOW_SIDE_EFFECTING


def _ici_plan(kind, a_list):
    if kind == "gather":
        return _half_plan(a_list, 0)
    plan = []
    for a, p in enumerate(a_list):
        plan += [(a, r0, nr) for r0, nr in _pieces(p.shape[1], p.shape[2] * p.dtype.itemsize)]
    return plan


def _ici_refs(kind, a_ref, b_ref, a_shape, r0, nr, c, me, peer):
    if kind == "gather":
        rows = _rows(c * (a_shape[0] // 2) + r0, nr)
        return a_ref.at[rows, :], b_ref.at[me, rows, :], b_ref.at[peer, rows, :]
    rows = pl.ds(r0, nr)
    return a_ref.at[peer, rows, :], b_ref.at[me, rows, :], b_ref.at[peer, rows, :]


def _ici_start(kind, a_list, b_list, name):
    n = len(a_list)
    plan = _ici_plan(kind, a_list)
    shapes = [a.shape for a in a_list]

    def body(*refs):
        a_refs, b_refs = refs[:n], refs[n:2 * n]
        send_sems, recv_sems = refs[2 * n], refs[2 * n + 1]
        token = refs[4 * n + 2]
        x, y, c, others = _place()
        me = 2 * x + y
        for i, (a, r0, nr) in enumerate(plan):
            for j, (ox, oy) in enumerate(others):
                src, dst, _ = _ici_refs(kind, a_refs[a], b_refs[a], shapes[a], r0, nr, c, me, 2 * ox + oy)
                _remote(src, dst, send_sems, recv_sems, 3 * i + j, (ox, oy, c)).start()
        token[...] = jnp.zeros_like(token)

    hbm = lambda v: pltpu.HBM(v.shape, v.dtype)
    ncp = 3 * len(plan)
    outs = pl.pallas_call(
        body, name=name,
        in_specs=[_HBM] * (2 * n),
        out_specs=[_SEM, _SEM] + [_HBM] * (2 * n) + [pl.BlockSpec(memory_space=pltpu.VMEM)],
        out_shape=[pltpu.SemaphoreType.DMA((ncp,)), pltpu.SemaphoreType.DMA((ncp,))]
                  + [hbm(v) for v in a_list] + [hbm(v) for v in b_list] + [jax.ShapeDtypeStruct((8, LANES), F32)],
        input_output_aliases={i: 2 + i for i in range(2 * n)},
        compiler_params=pltpu.CompilerParams(has_side_effects=_EFFECT),
    )(*[pltpu.with_memory_space_constraint(v, pltpu.HBM) for v in list(a_list) + list(b_list)])
    return outs[0], outs[1], outs[2:2 + n], outs[2 + n:2 + 2 * n], outs[2 + 2 * n]


def _ici_wait(kind, started, after, name):
    send_sems, recv_sems, a_list, b_list, _ = started
    afters = list(after) if isinstance(after, (list, tuple)) else [after]
    n = len(a_list)
    plan = _ici_plan(kind, a_list)
    shapes = [a.shape for a in a_list]

    def body(*refs):
        a_refs, b_refs = refs[:n], refs[n:2 * n]
        send_sems, recv_sems = refs[2 * n], refs[2 * n + 1]
        x, y, c, others = _place()
        me = 2 * x + y
        for i, (a, r0, nr) in enumerate(plan):
            for j, (ox, oy) in enumerate(others):
                src, dst, land = _ici_refs(kind, a_refs[a], b_refs[a], shapes[a], r0, nr, c, me, 2 * ox + oy)
                _remote(src, dst, send_sems, recv_sems, 3 * i + j, (ox, oy, c)).wait_send()
                _remote(land, land, send_sems, recv_sems, 3 * i + j, (x, y, c)).wait_recv()

    hbm = lambda v: pltpu.HBM(v.shape, v.dtype)
    outs = pl.pallas_call(
        body, name=name,
        in_specs=[_HBM] * (2 * n) + [_SEM, _SEM] + [_ANY] * len(afters),
        out_specs=[_HBM] * (2 * n),
        out_shape=[hbm(v) for v in a_list] + [hbm(v) for v in b_list],
        input_output_aliases={i: i for i in range(2 * n)},
        compiler_params=pltpu.CompilerParams(has_side_effects=_EFFECT),
    )(*a_list, *b_list, send_sems, recv_sems, *afters)
    return outs[n:]


def _rs_begin(ts, c_idx, me_idx, tag):
    got = _rs_swap(ts, "rs_swap_" + tag)
    pres, mine = _add_halves(ts, got, c_idx, me_idx, "rs_add2_" + tag)
    return _ici_start("scatter", pres, mine, "rs_xchg_start_" + tag)


def _rs_finish(started, after, c_idx, tag):
    parts = _ici_wait("scatter", started, after, "rs_xchg_wait_" + tag)
    return _rs_join(_add4_halves(parts, c_idx, "rs_add4_" + tag), "rs_join_" + tag)


def _pack_rows(pieces, rows, dtype):
    flat = jnp.concatenate([p.astype(dtype).reshape(-1) for p in pieces])
    return jnp.pad(flat, (0, rows * PACK_COLS - flat.shape[0])).reshape(rows, PACK_COLS)


def _unpack(flat, shapes):
    out, off = [], 0
    for shp in shapes:
        size = math.prod(shp)
        out.append(flat[off:off + size].reshape(shp))
        off += size
    return out


def _rows_for(n_elems, mult):
    rows = -(-n_elems // PACK_COLS)
    return -(-rows // mult) * mult


BIG_SHARDS = [("w_in", (D_MODEL, 1474)), ("w_branch_att", (D_ATT, 256)), ("w_branch_conv", (D_CONV, 256)),
              ("w_branch_sgu", (D_SGU, 256)), ("w_out", (256, D_MODEL)), ("w_ffn_up", (D_MODEL, FF_BLK)),
              ("w_ffn_down", (D_FF // N_CHIPS, D_MODEL))]
SMALL_SHARDS = [("b_gate", (3, 256)), ("conv_mix_w", (3, 64)), ("conv_ffn_w", (3, FF_BLK))]
REPLICATED = [("pre_mix_g", (D_MODEL,)), ("post_mix_g", (D_MODEL,)), ("pre_ffn_g", (D_MODEL,)),
              ("post_ffn_g", (D_MODEL,)), ("b_forget", (N_HEADS,)), ("sgu_ln_g", (D_SGU,)), ("sgu_ln_b", (D_SGU,)),
              ("sgu_w", (N_GROUPS, CHUNK, CHUNK)), ("sgu_b", (N_GROUPS, CHUNK))]
WEIGHT_ORDER = ["pre_mix_g", "post_mix_g", "pre_ffn_g", "post_ffn_g", "w_in", "b_forget", "b_gate", "conv_mix_w",
                "sgu_ln_g", "sgu_ln_b", "sgu_w", "sgu_b", "w_branch_att", "w_branch_conv", "w_branch_sgu", "w_out",
                "w_ffn_up", "conv_ffn_w", "w_ffn_down"]

_SMALL_ELEMS = sum(math.prod(s) for _, s in SMALL_SHARDS)
_REP_ELEMS = sum(math.prod(s) for _, s in REPLICATED)
_REP_QUARTER = -(-(DEPTH * _REP_ELEMS) // N_CHIPS)
SMALL_PARAM_ROWS = _rows_for(DEPTH * _SMALL_ELEMS, 32)
SMALL_ROWS = _rows_for(DEPTH * _SMALL_ELEMS + _REP_QUARTER, 32)
IN_WIDTH = 5896
IN_SHARD = IN_WIDTH // N_CHIPS
IN_SHARD_PAD = 1536
IN_PAD = 6144


def _gather_small(wts):
    shard = _pack_rows([wts[n] for n, _ in SMALL_SHARDS], SMALL_PARAM_ROWS, F32)
    full = _all_gather_chips(shard, "gather_small_params").reshape(N_CHIPS, -1)
    per_chip = [_unpack(full[j], [(DEPTH,) + s for _, s in SMALL_SHARDS]) for j in range(N_CHIPS)]
    return {n: jnp.concatenate([per_chip[j][i] for j in range(N_CHIPS)], axis=-1)
            for i, (n, _) in enumerate(SMALL_SHARDS)}


BIG_NAMES = [n for n, _ in BIG_SHARDS]
FIRST_NAMES = ["w_in"]
LATE_NAMES = BIG_NAMES[1:]


def _gather_begin(wts, l, me_idx, names, tag):
    cast = [_cast_shard(wts[n], l, me_idx, "cast_" + n) for n in names]
    return _ici_start("gather", [sh for sh, _ in cast], [ld for _, ld in cast], "gather_ici_start_" + tag)


def _gather_finish(started, after, names, tag):
    lands = _ici_wait("gather", started, after, "gather_ici_wait_" + tag)
    return dict(zip(names, _gather_d2d(lands, "gather_d2d_" + tag)))


def _pad_rows(a, rows):
    return jnp.pad(a, ((0, rows - a.shape[0]), (0, 0)))


def _whole_cols(land):
    return land.transpose(1, 0, 2).reshape(land.shape[1], -1)


_O_F = 3 * D_ATT
_O_B = _O_F + N_HEADS
_O_GL = _O_B + 3 * D_CONV + 2 * D_SGU


_LOCAL_ORDER = [(_O_GL, IN_WIDTH), (0, _O_F), (_O_B, _O_GL), (_O_F, _O_B)]


def _own_cols(land, lo, hi):
    pieces = []
    for j in range(N_CHIPS):
        a, b = max(lo, j * IN_SHARD), min(hi, (j + 1) * IN_SHARD)
        if a < b:
            pieces.append(land[j][:, a - j * IN_SHARD:b - j * IN_SHARD])
    return pieces


def _local_cols(m, lo, hi):
    pieces, off = [], 0
    for a, b in _LOCAL_ORDER:
        x, y = max(lo, a), min(hi, b)
        if x < y:
            pieces.append((x, m[:, off + x - a:off + y - a]))
        off += b - a
    pieces = [p for _, p in sorted(pieces, key=lambda t: t[0])]
    if hi > IN_WIDTH:
        pieces.append(jnp.zeros((m.shape[0], hi - max(lo, IN_WIDTH)), m.dtype))
    return pieces


def _prep_first(wts, lands, small, l):
    land = lands["w_in"]
    cf = small["conv_ffn_w"][l]
    blk = lambda a, j: a[:, j * FF_BLK:(j + 1) * FF_BLK]
    local = [piece for lo, hi in _LOCAL_ORDER for piece in _own_cols(land, lo, hi)]
    return {
        "w_p": jnp.concatenate(local + [jnp.zeros((D_MODEL, IN_PAD - IN_WIDTH), BF16)], axis=1),
        "wf_t": _pad_rows(jnp.concatenate(_own_cols(land, _O_F, _O_B), axis=1).T, F_ROWS),
        "b_forget": _pad_rows(wts["b_forget"][l].reshape(N_HEADS, 1), F_ROWS),
        "b_gate": _pad_rows(small["b_gate"][l], 8),
        "conv_mix_w": _pad_rows(small["conv_mix_w"][l], 8),
        "conv_ffn_w": _pad_rows(jnp.concatenate([blk(cf, 0), blk(cf, 2), blk(cf, 1), blk(cf, 3)], axis=1), 8),
        "pre_mix_g": wts["pre_mix_g"][l].reshape(1, -1), "post_mix_g": wts["post_mix_g"][l].reshape(1, -1),
        "pre_ffn_g": wts["pre_ffn_g"][l].reshape(1, -1), "post_ffn_g": wts["post_ffn_g"][l].reshape(1, -1),
        "ln_g": wts["sgu_ln_g"][l].reshape(1, -1), "ln_b": wts["sgu_ln_b"][l].reshape(1, -1),
        "sgu_w": wts["sgu_w"][l],
        "sgu_bias": jnp.repeat(wts["sgu_b"][l].T, HEAD_DIM, axis=1),
    }


def _prep_late(lands):
    up = lands["w_ffn_up"]
    return {
        "w_att": _whole_cols(lands["w_branch_att"]), "w_conv": _whole_cols(lands["w_branch_conv"]),
        "w_sgu": _whole_cols(lands["w_branch_sgu"]),
        "w_out": lands["w_out"].reshape(D_MODEL, D_MODEL),
        "w_up": jnp.concatenate([up[0], up[2], up[1], up[3]], axis=1),
        "w_down": lands["w_ffn_down"].reshape(D_FF, D_MODEL),
    }


def _layer_fwd(x, p, dep=None, late=None):
    s = x.shape[0]
    xn = _rms_fwd(x, p["pre_mix_g"], "rms_pre_mix", dep)
    h = _mm(xn, p["w_p"], "nn", BF16, "mm_in", s, 512, D_MODEL)
    f_row = _mm(p["wf_t"], xn, "nt", F32, "mm_forget", F_ROWS, 2048, D_MODEL)
    ck = _gate_fwd(f_row, p["b_forget"], "gate_fwd")
    o, o_f32, lse = _attn_fwd(h, ck, "attn_fwd")
    yc = _sconv_fwd(h, p["conv_mix_w"], "sconv_fwd")
    ys = _sgu_fwd(h, p["ln_g"], p["ln_b"], p["sgu_w"], p["sgu_bias"], "sgu_fwd")
    if late is not None:
        p.update(late(o))
    merged = _merge_fwd(h, (o, yc, ys), (p["w_att"], p["w_conv"], p["w_sgu"]), p["b_gate"], "merge_fwd")
    mo = _mm(merged, p["w_out"], "nn", F32, "mm_out", 2048, 512, D_MODEL)
    x1, xn2 = _resid_post_norm(x, mo, p["post_mix_g"], p["pre_ffn_g"], "post_mix")
    h2 = _mm(xn2, p["w_up"], "nn", BF16, "mm_up", 2048, 512, D_MODEL)
    pact = _ffn_act_fwd(h2, p["conv_ffn_w"], "ffn_act_fwd")
    ff = _mm(pact, p["w_down"], "nn", F32, "mm_down", 1024, D_MODEL, D_FF)
    x2 = _resid_post(x1, ff, p["post_ffn_g"], "post_ffn")
    saved = dict(x=x, xn=xn, h=h, f_row=f_row, ck=ck, o=o, o_f32=o_f32, lse=lse, yc=yc, ys=ys, merged=merged, mo=mo, x1=x1,
                 xn2=xn2, h2=h2, pact=pact, ff=ff)
    return x2, saved


def _layer_bwd(dx2, p, sv, dep=None, early=None):
    s = dx2.shape[0]
    g = {}
    same = lambda b: b
    dff, g["post_ffn_g"] = _rms_bwd(sv["ff"], p["post_ffn_g"], [dx2], None, BF16, "post_ffn_bwd", dep)
    dpact = _mm(dff, p["w_down"], "nt", BF16, "mm_down_dx", 2048, FF_BLK, D_MODEL)
    t_down = _mm(sv["pact"], dff, "tn", BF16, "mm_down_dw", 256, D_MODEL, s).reshape(N_CHIPS, -1, D_MODEL)
    dh2, dconv_ffn = _ffn_act_conv_bwd(sv["h2"], p["conv_ffn_w"], dpact, "ffn_act_conv_bwd")
    dxn2 = _mm(dh2, p["w_up"], "nt", F32, "mm_up_dx", 512, D_MODEL, 2 * D_FF)
    t_up = _mm(sv["xn2"], dh2, "tn", BF16, "mm_up_dw", 512, FF_BLK, s, chip_of=lambda b: (b % 2) * 2 + b // 2)
    dx1, g["pre_ffn_g"] = _rms_bwd(sv["x1"], p["pre_ffn_g"], [dxn2], dx2, F32, "pre_ffn_bwd")
    dep_mix = early([t_up, t_down]) if early is not None else None
    dmo, g["post_mix_g"] = _rms_bwd(sv["mo"], p["post_mix_g"], [dx1], None, BF16, "post_mix_bwd", dep_mix)
    dmerged = _mm(dmo, p["w_out"], "nt", F32, "mm_out_dx", 2048, 512, D_MODEL)
    t_out = _mm(sv["merged"], dmo, "tn", BF16, "mm_out_dw", 512, D_MODEL, s).reshape(N_CHIPS, -1, D_MODEL)
    acts = (sv["o"], sv["yc"], sv["ys"])
    ws = (p["w_att"], p["w_conv"], p["w_sgu"])
    dy_a, dy_c, dy_s, dgl, db_gate = _merge_bwd(sv["h"], acts, ws, p["b_gate"], dmerged, "merge_bwd")
    do = _mm(dy_a, p["w_att"], "nt", BF16, "mm_att_dx", 2048, D_ATT, D_MODEL)
    dyc = _mm(dy_c, p["w_conv"], "nt", BF16, "mm_conv_dx", 2048, D_CONV, D_MODEL)
    dys = _mm(dy_s, p["w_sgu"], "nt", BF16, "mm_sgu_dx", 2048, D_SGU, D_MODEL)
    t_att = _mm(sv["o"], dy_a, "tn", BF16, "mm_att_dw", D_ATT, 256, s, chip_of=same)
    t_conv = _mm(sv["yc"], dy_c, "tn", BF16, "mm_conv_dw", D_CONV, 256, s, chip_of=same)
    t_sgu = _mm(sv["ys"], dy_s, "tn", BF16, "mm_sgu_dw", D_SGU, 256, s, chip_of=same)
    d_conv, dconv_mix = _sconv_bwd(sv["h"], p["conv_mix_w"], dyc, "sconv_bwd")
    d_sgu, g["sgu_ln_g"], g["sgu_ln_b"], g["sgu_w"], dbias = _sgu_bwd(
        sv["h"], p["ln_g"], p["ln_b"], p["sgu_w"], p["sgu_bias"], dys, "sgu_bwd")
    dq, dk, dv, dc_even, dc_odd = _attn_bwd(sv["h"], sv["ck"], sv["o_f32"], sv["lse"], do, "attn_bwd")
    df, db_forget = _gate_bwd(sv["f_row"], p["b_forget"], dc_even, dc_odd, "gate_bwd")
    f_cols = jnp.concatenate([df[:N_HEADS].T, jnp.zeros((s, IN_PAD - IN_WIDTH), BF16)], axis=1)
    dh = _assemble_dh(dgl, [dq, dk, dv, d_conv, d_sgu, f_cols], "assemble_dh")
    dxn = _mm(dh, p["w_p"], "nt", F32, "mm_in_dx", 512, D_MODEL, IN_PAD)
    dw_p = _mm(sv["xn"], dh, "tn", BF16, "mm_in_dw", D_MODEL, 512, s)
    t_in = jnp.stack([jnp.concatenate(_local_cols(dw_p, j * IN_SHARD, j * IN_SHARD + IN_SHARD_PAD), axis=1)
                      for j in range(N_CHIPS)])
    dx, g["pre_mix_g"] = _rms_bwd(sv["x"], p["pre_mix_g"], [dxn], dx1, F32, "pre_mix_bwd")
    blk = lambda a, j: a[:, j * FF_BLK:(j + 1) * FF_BLK]
    g["conv_ffn_w"] = jnp.concatenate([blk(dconv_ffn, 0), blk(dconv_ffn, 2), blk(dconv_ffn, 1),
                                       blk(dconv_ffn, 3)], axis=1)[:3]
    g["conv_mix_w"] = dconv_mix[:3]
    g["b_gate"] = db_gate[:3]
    g["b_forget"] = db_forget[:N_HEADS, 0]
    g["sgu_b"] = jnp.sum(dbias.reshape(CHUNK, N_GROUPS, HEAD_DIM), axis=-1).T
    for n in ("pre_mix_g", "post_mix_g", "pre_ffn_g", "post_ffn_g", "sgu_ln_g", "sgu_ln_b"):
        g[n] = g[n].reshape(-1)
    mix = [t_in, t_att, t_conv, t_sgu, t_out]
    return dx, (mix if early is not None else mix + [t_up, t_down]), g


def _assemble_dh(dh, pieces, name):
    s = dh.shape[0]
    t = _tile(s, 512)
    width = sum(a.shape[1] for a in pieces)
    assert 2 * width == dh.shape[1]

    def body(*refs):
        out = refs[-1]
        col = 0
        for ref in refs[1:-1]:
            w = ref.shape[1]
            out[:, col:col + w] = ref[...].astype(out.dtype)
            col += w

    return pl.pallas_call(
        body, name=name, grid=(s // t,),
        in_specs=[_ANY] + [pl.BlockSpec((t, a.shape[1]), lambda i: (i, 0)) for a in pieces],
        out_specs=pl.BlockSpec((t, width), lambda i: (i, 1)),
        out_shape=jax.ShapeDtypeStruct(dh.shape, dh.dtype),
        input_output_aliases={0: 0},
        compiler_params=_params(("parallel",)),
    )(dh, *pieces)


def _shard_cols(a, j):
    w = a.shape[-1] // N_CHIPS
    return a[..., j * w:(j + 1) * w]


def kernel(x, pre_mix_g, post_mix_g, pre_ffn_g, post_ffn_g, w_in, b_forget, b_gate, conv_mix_w, sgu_ln_g, sgu_ln_b, sgu_w, sgu_b, w_branch_att, w_branch_conv, w_branch_sgu, w_out, w_ffn_up, conv_ffn_w, w_ffn_down, loss_target, m_pre_mix_g, m_post_mix_g, m_pre_ffn_g, m_post_ffn_g, m_w_in, m_b_forget, m_b_gate, m_conv_mix_w, m_sgu_ln_g, m_sgu_ln_b, m_sgu_w, m_sgu_b, m_w_branch_att, m_w_branch_conv, m_w_branch_sgu, m_w_out, m_w_ffn_up, m_conv_ffn_w, m_w_ffn_down, v_pre_mix_g, v_post_mix_g, v_pre_ffn_g, v_post_ffn_g, v_w_in, v_b_forget, v_b_gate, v_conv_mix_w, v_sgu_ln_g, v_sgu_ln_b, v_sgu_w, v_sgu_b, v_w_branch_att, v_w_branch_conv, v_w_branch_sgu, v_w_out, v_w_ffn_up, v_conv_ffn_w, v_w_ffn_down):
    wts = dict(pre_mix_g=pre_mix_g, post_mix_g=post_mix_g, pre_ffn_g=pre_ffn_g, post_ffn_g=post_ffn_g, w_in=w_in,
               b_forget=b_forget, b_gate=b_gate, conv_mix_w=conv_mix_w, sgu_ln_g=sgu_ln_g, sgu_ln_b=sgu_ln_b,
               sgu_w=sgu_w, sgu_b=sgu_b, w_branch_att=w_branch_att, w_branch_conv=w_branch_conv,
               w_branch_sgu=w_branch_sgu, w_out=w_out, w_ffn_up=w_ffn_up, conv_ffn_w=conv_ffn_w,
               w_ffn_down=w_ffn_down)
    moms = dict(pre_mix_g=m_pre_mix_g, post_mix_g=m_post_mix_g, pre_ffn_g=m_pre_ffn_g, post_ffn_g=m_post_ffn_g,
                w_in=m_w_in, b_forget=m_b_forget, b_gate=m_b_gate, conv_mix_w=m_conv_mix_w, sgu_ln_g=m_sgu_ln_g,
                sgu_ln_b=m_sgu_ln_b, sgu_w=m_sgu_w, sgu_b=m_sgu_b, w_branch_att=m_w_branch_att,
                w_branch_conv=m_w_branch_conv, w_branch_sgu=m_w_branch_sgu, w_out=m_w_out, w_ffn_up=m_w_ffn_up,
                conv_ffn_w=m_conv_ffn_w, w_ffn_down=m_w_ffn_down)
    vels = dict(pre_mix_g=v_pre_mix_g, post_mix_g=v_post_mix_g, pre_ffn_g=v_pre_ffn_g, post_ffn_g=v_post_ffn_g,
                w_in=v_w_in, b_forget=v_b_forget, b_gate=v_b_gate, conv_mix_w=v_conv_mix_w, sgu_ln_g=v_sgu_ln_g,
                sgu_ln_b=v_sgu_ln_b, sgu_w=v_sgu_w, sgu_b=v_sgu_b, w_branch_att=v_w_branch_att,
                w_branch_conv=v_w_branch_conv, w_branch_sgu=v_w_branch_sgu, w_out=v_w_out, w_ffn_up=v_w_ffn_up,
                conv_ffn_w=v_conv_ffn_w, w_ffn_down=v_w_ffn_down)

    c_idx = lax.axis_index("c").astype(jnp.int32).reshape(1)
    me_idx = (2 * lax.axis_index("x") + lax.axis_index("y")).astype(jnp.int32).reshape(1)
    small = _gather_small(wts)

    xs = x[0]
    layers, saved = [], []
    first = _gather_begin(wts, 0, me_idx, FIRST_NAMES, "first")
    rest = _gather_begin(wts, 0, me_idx, LATE_NAMES, "late")
    lands = _gather_finish(first, xs, FIRST_NAMES, "first")
    late = lambda after: _prep_late(_gather_finish(rest, after, LATE_NAMES, "late"))
    for l in range(DEPTH):
        p = _prep_first(wts, lands, small, l)
        if l > 0:
            p.update(_prep_late(lands))
        nxt = _gather_begin(wts, l + 1, me_idx, BIG_NAMES, "all") if l + 1 < DEPTH else None
        dep = ([nxt[4]] if nxt else []) + ([rest[4]] if l == 0 else [])
        xs, sv = _layer_fwd(xs, p, dep or None, late if l == 0 else None)
        if nxt:
            lands = _gather_finish(nxt, xs, BIG_NAMES, "all")
        layers.append(p)
        saved.append(sv)
    dy, loss_part = _loss_head(xs, loss_target[0], "loss_head")
    loss = lax.psum(loss_part[0, 0], ("x", "y", "c"))

    big_red = [None] * DEPTH
    small_grads = [None] * DEPTH
    pending = None
    ffn = []
    for l in reversed(range(DEPTH)):
        early = None
        if l == 0:
            def early(ts_ffn):
                ffn.append(_rs_begin(ts_ffn, c_idx, me_idx, "ffn"))
                return ffn[0][4]
        dy, ts, small_grads[l] = _layer_bwd(dy, layers[l], saved[l], pending[4] if pending else None, early)
        if pending:
            big_red[l + 1] = _rs_finish(pending, dy, c_idx, "big")
        pending = _rs_begin(ts, c_idx, me_idx, "mix" if l == 0 else "big")
    red_ffn = _rs_finish(ffn[0], dy, c_idx, "ffn")
    grad_x = dy[None]

    done = {}
    for k, n in enumerate(("w_ffn_up", "w_ffn_down")):
        i = BIG_NAMES.index(n)
        g = jnp.stack([red_ffn[k]] + [big_red[l][i] for l in range(1, DEPTH)])
        done[n] = (g,) + _adamw(wts[n], g, moms[n], vels[n], "adamw_" + n, pending[4])

    rep_flat = jnp.concatenate([small_grads[l][n].reshape(-1) for l in range(DEPTH) for n, _ in REPLICATED])
    rep_flat = jnp.pad(rep_flat, (0, N_CHIPS * _REP_QUARTER - rep_flat.shape[0]))
    rows = []
    for j in range(N_CHIPS):
        pieces = [_shard_cols(small_grads[l][n], j) for l in range(DEPTH) for n, _ in SMALL_SHARDS]
        pieces.append(rep_flat[j * _REP_QUARTER:(j + 1) * _REP_QUARTER])
        rows.append(_pack_rows(pieces, SMALL_ROWS, F32))
    small_red = _reduce_scatter_chips(jnp.stack(rows), "small", done["w_ffn_down"][1])
    small_all = _all_gather_chips(small_red, "gather_small")
    big_red[0] = _rs_finish(pending, [small_all] + [done[n][1] for n in done], c_idx, "mix") + red_ffn
    small_all = small_all.reshape(N_CHIPS, -1)

    grads = {}
    for i, (n, _) in enumerate(BIG_SHARDS):
        if n not in done:
            grads[n] = jnp.stack([big_red[l][i][:, :IN_SHARD] if n == "w_in" else big_red[l][i]
                                  for l in range(DEPTH)])
    mine_small = small_red.reshape(-1)
    parts = _unpack(mine_small, [s for _ in range(DEPTH) for _, s in SMALL_SHARDS])
    for i, (n, _) in enumerate(SMALL_SHARDS):
        grads[n] = jnp.stack([parts[l * len(SMALL_SHARDS) + i] for l in range(DEPTH)])
    off = DEPTH * _SMALL_ELEMS
    rep_all = jnp.concatenate([small_all[j, off:off + _REP_QUARTER] for j in range(N_CHIPS)])
    parts = _unpack(rep_all, [s for _ in range(DEPTH) for _, s in REPLICATED])
    for i, (n, _) in enumerate(REPLICATED):
        grads[n] = jnp.stack([parts[l * len(REPLICATED) + i] for l in range(DEPTH)])

    deltas, new_m, new_v = {}, {}, {}
    for n in WEIGHT_ORDER:
        if n in done:
            grads[n], deltas[n], new_m[n], new_v[n] = done[n]
        else:
            deltas[n], new_m[n], new_v[n] = _adamw(wts[n], grads[n], moms[n], vels[n], "adamw_" + n)
    return (loss, grad_x, *[grads[n] for n in WEIGHT_ORDER], *[deltas[n] for n in WEIGHT_ORDER],
            *[new_m[n] for n in WEIGHT_ORDER], *[new_v[n] for n in WEIGHT_ORDER])
```

```python
import functools
import math

import jax
import jax.numpy as jnp
from jax import lax
from jax.experimental import pallas as pl
from jax.experimental.pallas import tpu as pltpu

F32 = jnp.float32
BF16 = jnp.bfloat16
MXU_DTYPE = jnp.bfloat16

D_MODEL = 1024
HEAD_DIM = 64
N_HEADS = 8
D_ATT = 512
D_CONV = 256
D_SGU = 256
N_GROUPS = 4
CHUNK = 128
D_FF = 2816
DEPTH = 4
RMS_EPS = 1e-6
LN_EPS = 1e-5
N_CHIPS = 4
LANES = 128
PACK_COLS = 1024
HALO = 16

ADAM_LR = 0.001
ADAM_B1 = 0.9
ADAM_B2 = 0.999
ADAM_EPS = 1e-08
ADAM_WD = 0.01
ADAM_STEP = 10

OFF_GL = 0
OFF_Q = 3 * D_MODEL
OFF_K = OFF_Q + D_ATT
OFF_V = OFF_K + D_ATT
OFF_BG = OFF_V + D_ATT
OFF_CG = OFF_BG + D_CONV
OFF_HC = OFF_CG + D_CONV
OFF_U = OFF_HC + D_CONV
OFF_VS = OFF_U + D_SGU
W_P = OFF_VS + D_SGU
F_ROWS = 16

VMEM_LIMIT = 56 * 1024 * 1024
MESH = pl.DeviceIdType.MESH


def _params(sem=None):
    if sem is None:
        return pltpu.CompilerParams(vmem_limit_bytes=VMEM_LIMIT)
    return pltpu.CompilerParams(dimension_semantics=sem, vmem_limit_bytes=VMEM_LIMIT)


def _tile(dim, pref):
    if dim <= pref:
        return dim
    if dim % pref == 0:
        return pref
    return dim


_DIMS = {"nn": (((1,), (0,)), ((), ())), "nt": (((1,), (1,)), ((), ())), "tn": (((0,), (0,)), ((), ()))}


def _mm(a, b, mode, out_dtype, name, tm, tn, tk, chip_of=None):
    if mode == "tn":
        K, M = a.shape
    else:
        M, K = a.shape
    N = b.shape[0] if mode == "nt" else b.shape[1]
    tm, tn, tk = _tile(M, tm), _tile(N // N_CHIPS if chip_of else N, tn), _tile(K, tk)
    nk = K // tk
    dims = _DIMS[mode]

    def body(a_ref, b_ref, o_ref, *acc):
        part = lax.dot_general(a_ref[...].astype(MXU_DTYPE), b_ref[...].astype(MXU_DTYPE), dims,
                               preferred_element_type=F32)
        if nk == 1:
            o_ref[...] = part.astype(o_ref.dtype)
        else:
            acc_ref = acc[0]
            k = pl.program_id(2)

            @pl.when(k == 0)
            def _():
                acc_ref[...] = part

            @pl.when(k > 0)
            def _():
                acc_ref[...] += part

            @pl.when(k == nk - 1)
            def _():
                o_ref[...] = acc_ref[...].astype(o_ref.dtype)

    if mode == "tn":
        a_spec = pl.BlockSpec((tk, tm), lambda i, j, k: (k, i))
    else:
        a_spec = pl.BlockSpec((tm, tk), lambda i, j, k: (i, k))
    if mode == "nt":
        b_spec = pl.BlockSpec((tn, tk), lambda i, j, k: (j, k))
    else:
        b_spec = pl.BlockSpec((tk, tn), lambda i, j, k: (k, j))
    if chip_of is None:
        out_spec = pl.BlockSpec((tm, tn), lambda i, j, k: (i, j))
        out_shape = jax.ShapeDtypeStruct((M, N), out_dtype)
    else:
        per = (N // N_CHIPS) // tn
        out_spec = pl.BlockSpec((None, tm, tn), lambda i, j, k: (chip_of(j // per), i, j % per))
        out_shape = jax.ShapeDtypeStruct((N_CHIPS, M, N // N_CHIPS), out_dtype)
    return pl.pallas_call(
        body,
        name=name,
        grid=(M // tm, N // tn, nk),
        in_specs=[a_spec, b_spec],
        out_specs=out_spec,
        out_shape=out_shape,
        scratch_shapes=[pltpu.VMEM((tm, tn), F32)] if nk > 1 else [],
        compiler_params=_params(("parallel", "parallel", "arbitrary")),
    )(a, b)


_GELU_K = math.sqrt(2.0 / math.pi)
_GELU_C = 0.044715


def _gelu(x):
    t = jnp.tanh(x * (_GELU_K + (_GELU_K * _GELU_C) * (x * x)))
    return x * (0.5 + 0.5 * t)


def _gelu_and_grad(x):
    x2 = x * x
    t = jnp.tanh(x * (_GELU_K + (_GELU_K * _GELU_C) * x2))
    cdf = 0.5 + 0.5 * t
    dcdf = (1.0 - t * t) * (0.5 * _GELU_K + (1.5 * _GELU_K * _GELU_C) * x2)
    return x * cdf, cdf + x * dcdf


def _sigmoid(x):
    return 0.5 + 0.5 * jnp.tanh(0.5 * x)


def _shift_down(cur, prev, k):
    h = prev.shape[0]
    ext = jnp.concatenate([prev, cur], axis=0)
    return pltpu.roll(ext, k, 0)[h:]


def _shift_up(cur, nxt, k):
    t, h = cur.shape[0], nxt.shape[0]
    ext = jnp.concatenate([cur, nxt], axis=0)
    return pltpu.roll(ext, t + h - k, 0)[:t]


def _row_sum8(x):
    t, c = x.shape
    return jnp.sum(x.reshape(t // 8, 8, c), axis=0)


_DEP = pl.BlockSpec((8, LANES), lambda i: (0, 0))


def _rms_fwd(x, g, name, dep=None):
    s, d = x.shape
    t = _tile(s, 512)

    def body(x_ref, g_ref, *rest):
        o_ref = rest[-1]
        xv = x_ref[...]
        r = lax.rsqrt(jnp.mean(xv * xv, axis=-1, keepdims=True) + RMS_EPS)
        o_ref[...] = (xv * r * g_ref[...]).astype(o_ref.dtype)

    deps = [] if dep is None else list(dep) if isinstance(dep, (list, tuple)) else [dep]
    return pl.pallas_call(
        body, name=name, grid=(s // t,),
        in_specs=[pl.BlockSpec((t, d), lambda i: (i, 0)), pl.BlockSpec((1, d), lambda i: (0, 0))] + [_DEP] * len(deps),
        out_specs=pl.BlockSpec((t, d), lambda i: (i, 0)),
        out_shape=jax.ShapeDtypeStruct((s, d), BF16),
        compiler_params=_params(("parallel",)),
    )(x, g, *deps)


def _resid_post(x, y, g, name):
    s, d = x.shape
    t = _tile(s, 512)

    def body(x_ref, y_ref, g_ref, o_ref):
        yv = y_ref[...]
        r = lax.rsqrt(jnp.mean(yv * yv, axis=-1, keepdims=True) + RMS_EPS)
        o_ref[...] = x_ref[...] + yv * r * g_ref[...]

    row = pl.BlockSpec((t, d), lambda i: (i, 0))
    return pl.pallas_call(
        body, name=name, grid=(s // t,),
        in_specs=[row, row, pl.BlockSpec((1, d), lambda i: (0, 0))],
        out_specs=row,
        out_shape=jax.ShapeDtypeStruct((s, d), F32),
        compiler_params=_params(("parallel",)),
    )(x, y, g)


def _resid_post_norm(x, y, g, g_next, name):
    s, d = x.shape
    t = _tile(s, 512)

    def body(x_ref, y_ref, g_ref, gn_ref, o_ref, xn_ref):
        yv = y_ref[...]
        r = lax.rsqrt(jnp.mean(yv * yv, axis=-1, keepdims=True) + RMS_EPS)
        x1 = x_ref[...] + yv * r * g_ref[...]
        o_ref[...] = x1
        r1 = lax.rsqrt(jnp.mean(x1 * x1, axis=-1, keepdims=True) + RMS_EPS)
        xn_ref[...] = (x1 * r1 * gn_ref[...]).astype(xn_ref.dtype)

    row = pl.BlockSpec((t, d), lambda i: (i, 0))
    vec = pl.BlockSpec((1, d), lambda i: (0, 0))
    return pl.pallas_call(
        body, name=name, grid=(s // t,),
        in_specs=[row, row, vec, vec],
        out_specs=[row, row],
        out_shape=[jax.ShapeDtypeStruct((s, d), F32), jax.ShapeDtypeStruct((s, d), BF16)],
        compiler_params=_params(("parallel",)),
    )(x, y, g, g_next)


def _rms_bwd(xin, g, dys, dres, out_dtype, name, dep=None):
    s, d = xin.shape
    t = _tile(s, 512)
    n = s // t
    n_dy = len(dys)
    has_res = dres is not None
    deps = [] if dep is None else [dep]

    def body(*refs):
        x_ref, g_ref = refs[0], refs[1]
        dy_refs = refs[2:2 + n_dy]
        pos = 2 + n_dy
        res_ref = refs[pos] if has_res else None
        pos += (1 if has_res else 0) + len(deps)
        dx_ref, dg_ref, acc_ref = refs[pos], refs[pos + 1], refs[pos + 2]
        i = pl.program_id(0)
        xv = x_ref[...]
        dy = dy_refs[0][...].astype(F32)
        for extra in dy_refs[1:]:
            dy = dy + extra[...].astype(F32)
        r = lax.rsqrt(jnp.mean(xv * xv, axis=-1, keepdims=True) + RMS_EPS)
        u = dy * g_ref[...]
        xr = xv * r
        dx = r * (u - xr * jnp.mean(u * xr, axis=-1, keepdims=True))
        if has_res:
            dx = dx + res_ref[...]
        dx_ref[...] = dx.astype(dx_ref.dtype)
        part = _row_sum8(dy * xr)

        @pl.when(i == 0)
        def _():
            acc_ref[...] = part

        @pl.when(i > 0)
        def _():
            acc_ref[...] += part

        @pl.when(i == n - 1)
        def _():
            dg_ref[...] = jnp.sum(acc_ref[...], axis=0, keepdims=True)

    row = pl.BlockSpec((t, d), lambda i: (i, 0))
    vec = pl.BlockSpec((1, d), lambda i: (0, 0))
    ins = [xin, g, *dys] + ([dres] if has_res else []) + deps
    return pl.pallas_call(
        body, name=name, grid=(n,),
        in_specs=[row, vec] + [row] * (n_dy + (1 if has_res else 0)) + [_DEP] * len(deps),
        out_specs=[row, vec],
        out_shape=[jax.ShapeDtypeStruct((s, d), out_dtype), jax.ShapeDtypeStruct((1, d), F32)],
        scratch_shapes=[pltpu.VMEM((8, d), F32)],
        compiler_params=_params(("arbitrary",)),
    )(*ins)


def _loss_head(y, target, name):
    s, d = y.shape
    t = _tile(s, 512)
    n = s // t

    def body(y_ref, t_ref, dy_ref, loss_ref, acc_ref):
        i = pl.program_id(0)
        e = y_ref[...] - t_ref[...]
        dy_ref[...] = e * (1.0 / d)
        part = _row_sum8(e * e)

        @pl.when(i == 0)
        def _():
            acc_ref[...] = part

        @pl.when(i > 0)
        def _():
            acc_ref[...] += part

        @pl.when(i == n - 1)
        def _():
            tot = jnp.sum(jnp.sum(acc_ref[...], axis=0, keepdims=True), axis=1, keepdims=True)
            loss_ref[...] = tot * (0.5 / d)

    row = pl.BlockSpec((t, d), lambda i: (i, 0))
    return pl.pallas_call(
        body, name=name, grid=(n,),
        in_specs=[row, row],
        out_specs=[row, pl.BlockSpec((1, 1), lambda i: (0, 0))],
        out_shape=[jax.ShapeDtypeStruct((s, d), F32), jax.ShapeDtypeStruct((1, 1), F32)],
        scratch_shapes=[pltpu.VMEM((8, d), F32)],
        compiler_params=_params(("arbitrary",)),
    )(y, target)


def _split3(x):
    hi = x.astype(BF16)
    r1 = x - hi.astype(F32)
    mid = r1.astype(BF16)
    lo = (r1 - mid.astype(F32)).astype(BF16)
    return hi, mid, lo


def _tri_dot(x, tri):
    hi, mid, lo = _split3(x)
    dn = _DIMS["nn"]
    out = lax.dot_general(hi, tri, dn, preferred_element_type=F32)
    out = out + lax.dot_general(mid, tri, dn, preferred_element_type=F32)
    return out + lax.dot_general(lo, tri, dn, preferred_element_type=F32)


def _log_sigmoid(z):
    return jnp.minimum(z, 0.0) - jnp.log(1.0 + jnp.exp(-jnp.abs(z)))


def _gate_fwd(f_row, b_col, name):
    rows, s = f_row.shape
    t = _tile(s, 512)
    n = s // t

    def body(f_ref, b_ref, ck_ref, carry_ref):
        i = pl.program_id(0)

        @pl.when(i == 0)
        def _():
            carry_ref[...] = jnp.zeros_like(carry_ref)

        logf = _log_sigmoid(f_ref[...] + b_ref[...])
        r = lax.broadcasted_iota(jnp.int32, (t, t), 0)
        c = lax.broadcasted_iota(jnp.int32, (t, t), 1)
        tri = jnp.where(r <= c, 1.0, 0.0).astype(BF16)
        cs = _tri_dot(logf, tri) + carry_ref[...]
        carry_ref[...] = cs[:, t - 1:t]
        terms = [part.astype(F32) for part in _split3(-cs)]
        sub = lax.broadcasted_iota(jnp.int32, (LANES, t), 0)
        for p in range(N_HEADS // 2):
            stacked = jnp.zeros((LANES, t), F32)
            for hh in range(2):
                for j, term in enumerate(terms):
                    h = 2 * p + hh
                    stacked = jnp.where(sub == 3 * hh + j, jnp.broadcast_to(term[h:h + 1, :], (LANES, t)), stacked)
            ck_ref[p] = stacked.T.astype(ck_ref.dtype)

    return pl.pallas_call(
        body, name=name, grid=(n,),
        in_specs=[pl.BlockSpec((rows, t), lambda i: (0, i)), pl.BlockSpec((rows, 1), lambda i: (0, 0))],
        out_specs=pl.BlockSpec((N_HEADS // 2, t, LANES), lambda i: (0, i, 0)),
        out_shape=jax.ShapeDtypeStruct((N_HEADS // 2, s, LANES), BF16),
        scratch_shapes=[pltpu.VMEM((rows, 1), F32)],
        compiler_params=_params(("arbitrary",)),
    )(f_row, b_col)


def _gate_bwd(f_row, b_col, dc_even, dc_odd, name):
    rows, s = f_row.shape
    t = _tile(s, 512)
    n = s // t

    def body(f_ref, b_ref, dce_ref, dco_ref, df_ref, db_ref, carry_ref, acc_ref):
        i = pl.program_id(0)

        @pl.when(i == 0)
        def _():
            carry_ref[...] = jnp.zeros_like(carry_ref)
            acc_ref[...] = jnp.zeros_like(acc_ref)

        head = lax.broadcasted_iota(jnp.int32, (rows, t), 0)
        dcv = jnp.zeros((rows, t), F32)
        for h in range(N_HEADS):
            src = dce_ref if h % 2 == 0 else dco_ref
            dcv = jnp.where(head == h, jnp.broadcast_to(src[h // 2, 0:1, :], (rows, t)), dcv)
        r = lax.broadcasted_iota(jnp.int32, (t, t), 0)
        c = lax.broadcasted_iota(jnp.int32, (t, t), 1)
        tri = jnp.where(r >= c, 1.0, 0.0).astype(BF16)
        dlogf = _tri_dot(dcv, tri) + carry_ref[...]
        carry_ref[...] = dlogf[:, 0:1]
        z = f_ref[...] + b_ref[...]
        df = dlogf * _sigmoid(-z)
        df_ref[...] = df.astype(df_ref.dtype)
        acc_ref[...] += jnp.sum(df, axis=1, keepdims=True)

        @pl.when(i == n - 1)
        def _():
            db_ref[...] = acc_ref[...]

    rev = lambda i: (0, n - 1 - i)
    dc_spec = pl.BlockSpec((N_HEADS // 2, 8, t), lambda i: (0, 0, n - 1 - i))
    return pl.pallas_call(
        body, name=name, grid=(n,),
        in_specs=[pl.BlockSpec((rows, t), rev), pl.BlockSpec((rows, 1), lambda i: (0, 0)), dc_spec, dc_spec],
        out_specs=[pl.BlockSpec((rows, t), rev), pl.BlockSpec((rows, 1), lambda i: (0, 0))],
        out_shape=[jax.ShapeDtypeStruct((rows, s), BF16), jax.ShapeDtypeStruct((rows, 1), F32)],
        scratch_shapes=[pltpu.VMEM((rows, 1), F32), pltpu.VMEM((rows, 1), F32)],
        compiler_params=_params(("arbitrary",)),
    )(f_row, b_col, dc_even, dc_odd)


_NEG = -1e30
_SCALE = HEAD_DIM ** -0.5


def _head_masks():
    lane = lax.broadcasted_iota(jnp.int32, (1, LANES), 1)
    return [lane < HEAD_DIM, lane >= HEAD_DIM]


def _attn_fwd(h, ck, name):
    s = h.shape[0]
    t = _tile(s, 512)
    n = s // t
    qb, kb, vb = OFF_Q // LANES, OFF_K // LANES, OFF_V // LANES

    pairs = [(qi, ki) for qi in range(n) for ki in range(qi + 1)]
    qi_tab = jnp.asarray([qi for qi, _ in pairs], jnp.int32)
    ki_tab = jnp.asarray([ki for _, ki in pairs], jnp.int32)

    def body(qi_ref, ki_ref, q_ref, k_ref, v_ref, ck_ref, o_ref, of_ref, lse_ref, m_ref, l_ref, acc_ref):
        qi, ki = qi_ref[pl.program_id(1)], ki_ref[pl.program_id(1)]
        masks = _head_masks()
        lane = lax.broadcasted_iota(jnp.int32, (1, LANES), 1)

        @pl.when(ki == 0)
        def _():
            m_ref[...] = jnp.full_like(m_ref, _NEG)
            l_ref[...] = jnp.zeros_like(l_ref)
            acc_ref[...] = jnp.zeros_like(acc_ref)

        def step(diag):
            q = q_ref[...] * _SCALE
            k_aug = jnp.concatenate([k_ref[...], ck_ref[0]], axis=1)
            v = v_ref[...]
            nq = max(1, t // 256)
            wq = t // nq
            chains = [(hh, j) for hh in range(2) for j in range(nq)]
            scores = []
            for hh, j in chains:
                qs = q[j * wq:(j + 1) * wq]
                ones = jnp.where((lane >= 3 * hh) & (lane < 3 * hh + 3), 1.0, 0.0).astype(q.dtype)
                q_aug = jnp.concatenate([jnp.where(masks[hh], qs, jnp.zeros_like(qs)),
                                         jnp.broadcast_to(ones, qs.shape)], axis=1)
                scores.append(lax.dot_general(k_aug, q_aug, _DIMS["nt"], preferred_element_type=F32))
            probs = []
            for (hh, j), sc in zip(chains, scores):
                cols = slice(j * wq, (j + 1) * wq)
                if diag:
                    r = lax.broadcasted_iota(jnp.int32, (t, wq), 0)
                    cc = lax.broadcasted_iota(jnp.int32, (t, wq), 1) + j * wq
                    sc = jnp.where(r <= cc, sc, _NEG)
                m_prev = m_ref[hh, :, cols]
                m_new = jnp.maximum(m_prev, jnp.max(sc, axis=0, keepdims=True))
                alpha = jnp.exp(m_prev - m_new)
                p = jnp.exp(sc - m_new)
                l_ref[hh, :, cols] = alpha * l_ref[hh, :, cols] + jnp.sum(p, axis=0, keepdims=True)
                m_ref[hh, :, cols] = m_new
                p_hi = p.astype(MXU_DTYPE)
                p_lo = (p - p_hi.astype(F32)).astype(MXU_DTYPE)
                probs.append((alpha, p_hi, p_lo))
            for (hh, j), (alpha, p_hi, p_lo) in zip(chains, probs):
                pv = (lax.dot_general(v, p_hi, _DIMS["tn"], preferred_element_type=F32)
                      + lax.dot_general(v, p_lo, _DIMS["tn"], preferred_element_type=F32))
                rows = slice(hh * HEAD_DIM, (hh + 1) * HEAD_DIM)
                cols = slice(j * wq, (j + 1) * wq)
                acc_ref[rows, cols] = alpha * acc_ref[rows, cols] + pv[rows]

        @pl.when(ki < qi)
        def _():
            step(False)

        @pl.when(ki == qi)
        def _():
            step(True)
            inv = jnp.concatenate([jnp.broadcast_to(1.0 / l_ref[hh], (HEAD_DIM, t)) for hh in range(2)], axis=0)
            out = (acc_ref[...] * inv).T
            o_ref[...] = out.astype(o_ref.dtype)
            of_ref[...] = out
            lse = jnp.concatenate([jnp.broadcast_to(m_ref[hh] + jnp.log(l_ref[hh]), (HEAD_DIM, t))
                                   for hh in range(2)], axis=0)
            lse_ref[...] = lse.T

    grid_spec = pltpu.PrefetchScalarGridSpec(
        num_scalar_prefetch=2, grid=(N_HEADS // 2, len(pairs)),
        in_specs=[
            pl.BlockSpec((t, LANES), lambda p, i, qt, kt: (qt[i], qb + p)),
            pl.BlockSpec((t, LANES), lambda p, i, qt, kt: (kt[i], kb + p)),
            pl.BlockSpec((t, LANES), lambda p, i, qt, kt: (kt[i], vb + p)),
            pl.BlockSpec((1, t, LANES), lambda p, i, qt, kt: (p, kt[i], 0)),
        ],
        out_specs=[pl.BlockSpec((t, LANES), lambda p, i, qt, kt: (qt[i], p))] * 3,
        scratch_shapes=[pltpu.VMEM((2, 1, t), F32), pltpu.VMEM((2, 1, t), F32), pltpu.VMEM((LANES, t), F32)])
    return pl.pallas_call(
        body, name=name, grid_spec=grid_spec,
        out_shape=[jax.ShapeDtypeStruct((s, D_ATT), BF16), jax.ShapeDtypeStruct((s, D_ATT), F32),
                   jax.ShapeDtypeStruct((s, D_ATT), F32)],
        compiler_params=_params(("parallel", "arbitrary")),
    )(qi_tab, ki_tab, h, h, h, ck)


def _attn_bwd(h, ck, o, lse, do, name):
    s = h.shape[0]
    t = _tile(s, 512)
    n = s // t
    qb, kb, vb = OFF_Q // LANES, OFF_K // LANES, OFF_V // LANES

    pairs = [(ki, qi) for ki in range(n) for qi in range(ki, n)]
    ki_tab = jnp.asarray([ki for ki, _ in pairs], jnp.int32)
    qi_tab = jnp.asarray([qi for _, qi in pairs], jnp.int32)

    def body(ki_ref, qi_ref, q_ref, k_ref, v_ref, ck_ref, o_ref, lse_ref, do_ref,
             dq_ref, dk_ref, dv_ref, dc0_ref, dc1_ref, dk_acc, dv_acc, dc_acc):
        ki, qi = ki_ref[pl.program_id(1)], qi_ref[pl.program_id(1)]
        masks = _head_masks()
        lane = lax.broadcasted_iota(jnp.int32, (1, LANES), 1)

        @pl.when((ki == 0) & (qi == 0))
        def _():
            dq_ref[...] = jnp.zeros_like(dq_ref)

        @pl.when(qi == ki)
        def _():
            dk_acc[...] = jnp.zeros_like(dk_acc)
            dv_acc[...] = jnp.zeros_like(dv_acc)
            dc_acc[...] = jnp.zeros_like(dc_acc)

        def step(diag):
            q = q_ref[...] * _SCALE
            k = k_ref[...]
            v = v_ref[...]
            dov = do_ref[...]
            k_aug = jnp.concatenate([k, ck_ref[0]], axis=1)
            prod_t = (dov.astype(F32) * o_ref[...]).T
            lse_t = lse_ref[...].T
            heads = []
            for hh in range(2):
                mk = masks[hh]
                qh = jnp.where(mk, q, jnp.zeros_like(q))
                kh = jnp.where(mk, k, jnp.zeros_like(k))
                doh = jnp.where(mk, dov, jnp.zeros_like(dov))
                ones = jnp.where((lane >= 3 * hh) & (lane < 3 * hh + 3), 1.0, 0.0).astype(q.dtype)
                q_aug = jnp.concatenate([qh, jnp.broadcast_to(ones, q.shape)], axis=1)
                sc = lax.dot_general(k_aug, q_aug, _DIMS["nt"], preferred_element_type=F32)
                dp = lax.dot_general(v, doh, _DIMS["nt"], preferred_element_type=F32)
                heads.append((qh, kh, doh, sc, dp))
            grads = []
            for hh, (qh, kh, doh, sc, dp) in enumerate(heads):
                rows = slice(hh * HEAD_DIM, (hh + 1) * HEAD_DIM)
                p = jnp.exp(sc - lse_t[hh * HEAD_DIM:hh * HEAD_DIM + 1, :])
                if diag:
                    r = lax.broadcasted_iota(jnp.int32, (t, t), 0)
                    cc = lax.broadcasted_iota(jnp.int32, (t, t), 1)
                    p = jnp.where(r <= cc, p, 0.0)
                delta = jnp.sum(prod_t[rows], axis=0, keepdims=True)
                ds = p * (dp - delta)
                dc_acc[hh] = dc_acc[hh] - jnp.sum(ds, axis=1, keepdims=True)
                grads.append((ds.astype(MXU_DTYPE), p.astype(MXU_DTYPE)))
            dq_blk = jnp.zeros((t, LANES), F32)
            for (qh, kh, doh, _, _), (dsb, pb) in zip(heads, grads):
                dv_acc[...] += lax.dot_general(pb, doh, _DIMS["nn"], preferred_element_type=F32)
                dk_acc[...] += lax.dot_general(dsb, qh, _DIMS["nn"], preferred_element_type=F32)
                dq_blk = dq_blk + lax.dot_general(dsb, kh, _DIMS["tn"], preferred_element_type=F32)
            rows_q = pl.ds(pl.multiple_of(qi * t, t), t)
            dq_ref[rows_q, :] = dq_ref[rows_q, :] + dq_blk * _SCALE

        @pl.when(qi > ki)
        def _():
            step(False)

        @pl.when(qi == ki)
        def _():
            step(True)

        @pl.when(qi == n - 1)
        def _():
            dk_ref[...] = dk_acc[...].astype(dk_ref.dtype)
            dv_ref[...] = dv_acc[...].astype(dv_ref.dtype)
            dc0_ref[0] = jnp.broadcast_to(dc_acc[0], (t, LANES)).T[0:8]
            dc1_ref[0] = jnp.broadcast_to(dc_acc[1], (t, LANES)).T[0:8]

    q_blk = lambda col: pl.BlockSpec((t, LANES), lambda p, i, kt, qt: (qt[i], col(p)))
    k_blk = lambda col: pl.BlockSpec((t, LANES), lambda p, i, kt, qt: (kt[i], col(p)))
    dc_blk = pl.BlockSpec((1, 8, t), lambda p, i, kt, qt: (p, 0, kt[i]))
    grid_spec = pltpu.PrefetchScalarGridSpec(
        num_scalar_prefetch=2, grid=(N_HEADS // 2, len(pairs)),
        in_specs=[q_blk(lambda p: qb + p), k_blk(lambda p: kb + p), k_blk(lambda p: vb + p),
                  pl.BlockSpec((1, t, LANES), lambda p, i, kt, qt: (p, kt[i], 0)),
                  q_blk(lambda p: p), q_blk(lambda p: p), q_blk(lambda p: p)],
        out_specs=[pl.BlockSpec((s, LANES), lambda p, i, kt, qt: (0, p)), k_blk(lambda p: p), k_blk(lambda p: p),
                   dc_blk, dc_blk],
        scratch_shapes=[pltpu.VMEM((t, LANES), F32), pltpu.VMEM((t, LANES), F32), pltpu.VMEM((2, t, 1), F32)])
    return pl.pallas_call(
        body, name=name, grid_spec=grid_spec,
        out_shape=[jax.ShapeDtypeStruct((s, D_ATT), F32), jax.ShapeDtypeStruct((s, D_ATT), BF16),
                   jax.ShapeDtypeStruct((s, D_ATT), BF16), jax.ShapeDtypeStruct((N_HEADS // 2, 8, s), F32),
                   jax.ShapeDtypeStruct((N_HEADS // 2, 8, s), F32)],
        compiler_params=_params(("parallel", "arbitrary")),
    )(ki_tab, qi_tab, h, h, h, ck, o, lse, do)


def _conv3(z, z_prev, w_ref):
    return (w_ref[2:3, :] * z + w_ref[1:2, :] * _shift_down(z, z_prev, 1)
            + w_ref[0:1, :] * _shift_down(z, z_prev, 2))


def _sconv_fwd(h, w, name):
    s = h.shape[0]
    t = _tile(s, 512)
    r = t // HALO
    c = D_CONV
    b_bg, b_cg, b_hc = OFF_BG // c, OFF_CG // c, OFF_HC // c

    def body(bg_ref, cg_ref, hc_ref, cgp_ref, hcp_ref, w_ref, y_ref):
        i = pl.program_id(0)
        live = (i > 0).astype(F32)
        z = cg_ref[...].astype(F32) * hc_ref[...].astype(F32)
        zp = cgp_ref[...].astype(F32) * hcp_ref[...].astype(F32) * live
        y_ref[...] = (bg_ref[...].astype(F32) * _conv3(z, zp, w_ref)).astype(y_ref.dtype)

    cur = lambda b: pl.BlockSpec((t, c), lambda i: (i, b))
    prev = lambda b: pl.BlockSpec((HALO, c), lambda i: (jnp.maximum(i * r - 1, 0), b))
    return pl.pallas_call(
        body, name=name, grid=(s // t,),
        in_specs=[cur(b_bg), cur(b_cg), cur(b_hc), prev(b_cg), prev(b_hc), pl.BlockSpec((8, c), lambda i: (0, 0))],
        out_specs=pl.BlockSpec((t, c), lambda i: (i, 0)),
        out_shape=jax.ShapeDtypeStruct((s, c), BF16),
        compiler_params=_params(("parallel",)),
    )(h, h, h, h, h, w)


def _sconv_bwd(h, w, dy, name):
    s = h.shape[0]
    t = _tile(s, 512)
    n = s // t
    r = t // HALO
    nh = s // HALO
    c = D_CONV
    b_bg, b_cg, b_hc = OFF_BG // c, OFF_CG // c, OFF_HC // c

    def body(bg_ref, cg_ref, hc_ref, cgp_ref, hcp_ref, bgn_ref, dy_ref, dyn_ref, w_ref, d_ref, dw_ref, acc_ref):
        i = pl.program_id(0)
        has_prev = (i > 0).astype(F32)
        has_next = (i < n - 1).astype(F32)
        bg = bg_ref[...].astype(F32)
        cg = cg_ref[...].astype(F32)
        hc = hc_ref[...].astype(F32)
        dyv = dy_ref[...].astype(F32)
        z = cg * hc
        zp = cgp_ref[...].astype(F32) * hcp_ref[...].astype(F32) * has_prev
        z1 = _shift_down(z, zp, 1)
        z2 = _shift_down(z, zp, 2)
        cz = w_ref[2:3, :] * z + w_ref[1:2, :] * z1 + w_ref[0:1, :] * z2
        dcz = dyv * bg
        dczn = dyn_ref[...].astype(F32) * bgn_ref[...].astype(F32) * has_next
        dz = (w_ref[2:3, :] * dcz + w_ref[1:2, :] * _shift_up(dcz, dczn, 1)
              + w_ref[0:1, :] * _shift_up(dcz, dczn, 2))
        d_ref[:, 0:c] = (dyv * cz).astype(d_ref.dtype)
        d_ref[:, c:2 * c] = (dz * hc).astype(d_ref.dtype)
        d_ref[:, 2 * c:3 * c] = (dz * cg).astype(d_ref.dtype)

        @pl.when(i == 0)
        def _():
            acc_ref[...] = jnp.zeros_like(acc_ref)

        acc_ref[0] += _row_sum8(dcz * z2)
        acc_ref[1] += _row_sum8(dcz * z1)
        acc_ref[2] += _row_sum8(dcz * z)

        @pl.when(i == n - 1)
        def _():
            rows = [jnp.sum(acc_ref[k], axis=0, keepdims=True) for k in range(3)]
            dw_ref[...] = jnp.concatenate(rows + [jnp.zeros((5, c), F32)], axis=0)

    cur = lambda b: pl.BlockSpec((t, c), lambda i: (i, b))
    prev = lambda b: pl.BlockSpec((HALO, c), lambda i: (jnp.maximum(i * r - 1, 0), b))
    nxt = lambda b: pl.BlockSpec((HALO, c), lambda i: (jnp.minimum((i + 1) * r, nh - 1), b))
    return pl.pallas_call(
        body, name=name, grid=(n,),
        in_specs=[cur(b_bg), cur(b_cg), cur(b_hc), prev(b_cg), prev(b_hc), nxt(b_bg),
                  cur(0), nxt(0), pl.BlockSpec((8, c), lambda i: (0, 0))],
        out_specs=[pl.BlockSpec((t, 3 * c), lambda i: (i, 0)), pl.BlockSpec((8, c), lambda i: (0, 0))],
        out_shape=[jax.ShapeDtypeStruct((s, 3 * c), BF16), jax.ShapeDtypeStruct((8, c), F32)],
        scratch_shapes=[pltpu.VMEM((3, 8, c), F32)],
        compiler_params=_params(("arbitrary",)),
    )(h, h, h, h, h, h, dy, dy, w)


def _group_masks():
    lane = lax.broadcasted_iota(jnp.int32, (1, D_SGU), 1)
    return [(lane >= g * HEAD_DIM) & (lane < (g + 1) * HEAD_DIM) for g in range(N_GROUPS)]


def _tril_weights(w_ref):
    r = lax.broadcasted_iota(jnp.int32, (CHUNK, CHUNK), 0)
    c = lax.broadcasted_iota(jnp.int32, (CHUNK, CHUNK), 1)
    return [jnp.where(r >= c, w_ref[g], 0.0).astype(MXU_DTYPE) for g in range(N_GROUPS)]


def _sgu_ln(vs, g_ref, b_ref):
    vg, dvg = _gelu_and_grad(vs)
    mu = jnp.mean(vg, axis=-1, keepdims=True)
    xc = vg - mu
    rstd = lax.rsqrt(jnp.mean(xc * xc, axis=-1, keepdims=True) + LN_EPS)
    xhat = xc * rstd
    return xhat * g_ref[...] + b_ref[...], xhat, rstd, dvg


def _sgu_fwd(h, ln_g, ln_b, w_s, bias, name):
    s = h.shape[0]
    t = _tile(s, 512)
    c = D_SGU
    b_u, b_v = OFF_U // c, OFF_VS // c

    def body(u_ref, v_ref, g_ref, b_ref, w_ref, bias_ref, y_ref):
        gm = _group_masks()
        wm = _tril_weights(w_ref)
        ug = _gelu(u_ref[...].astype(F32))
        vn, _, _, _ = _sgu_ln(v_ref[...].astype(F32), g_ref, b_ref)
        vnb = vn.astype(MXU_DTYPE)
        for ch in range(t // CHUNK):
            rows = slice(ch * CHUNK, (ch + 1) * CHUNK)
            mixed = bias_ref[...]
            for g in range(N_GROUPS):
                mg = lax.dot_general(wm[g], vnb[rows], _DIMS["nn"], preferred_element_type=F32)
                mixed = jnp.where(gm[g], mixed + mg, mixed)
            y_ref[rows, :] = (ug[rows] * mixed).astype(y_ref.dtype)

    full = lambda shp: pl.BlockSpec(shp, lambda i: (0,) * len(shp))
    return pl.pallas_call(
        body, name=name, grid=(s // t,),
        in_specs=[pl.BlockSpec((t, c), lambda i: (i, b_u)), pl.BlockSpec((t, c), lambda i: (i, b_v)),
                  full((1, c)), full((1, c)), full((N_GROUPS, CHUNK, CHUNK)), full((CHUNK, c))],
        out_specs=pl.BlockSpec((t, c), lambda i: (i, 0)),
        out_shape=jax.ShapeDtypeStruct((s, c), BF16),
        compiler_params=_params(("parallel",)),
    )(h, h, ln_g, ln_b, w_s, bias)


def _sgu_bwd(h, ln_g, ln_b, w_s, bias, dy, name):
    s = h.shape[0]
    t = _tile(s, 512)
    n = s // t
    c = D_SGU
    b_u, b_v = OFF_U // c, OFF_VS // c

    def body(u_ref, v_ref, g_ref, b_ref, w_ref, bias_ref, dy_ref,
             d_ref, dg_ref, db_ref, dw_ref, dbias_ref, dg_acc, db_acc):
        i = pl.program_id(0)
        gm = _group_masks()
        wm = _tril_weights(w_ref)

        @pl.when(i == 0)
        def _():
            dg_acc[...] = jnp.zeros_like(dg_acc)
            db_acc[...] = jnp.zeros_like(db_acc)
            dw_ref[...] = jnp.zeros_like(dw_ref)
            dbias_ref[...] = jnp.zeros_like(dbias_ref)

        ug, dug = _gelu_and_grad(u_ref[...].astype(F32))
        vn, xhat, rstd, dvg = _sgu_ln(v_ref[...].astype(F32), g_ref, b_ref)
        vnb = vn.astype(MXU_DTYPE)
        dyv = dy_ref[...].astype(F32)
        dmixed = dyv * ug
        dmb = dmixed.astype(MXU_DTYPE)
        dvn_parts = []
        for ch in range(t // CHUNK):
            rows = slice(ch * CHUNK, (ch + 1) * CHUNK)
            mixed = bias_ref[...]
            dvn = jnp.zeros((CHUNK, c), F32)
            for g in range(N_GROUPS):
                mg = lax.dot_general(wm[g], vnb[rows], _DIMS["nn"], preferred_element_type=F32)
                mixed = jnp.where(gm[g], mixed + mg, mixed)
                dvn = jnp.where(gm[g], lax.dot_general(wm[g], dmb[rows], _DIMS["tn"], preferred_element_type=F32),
                                dvn)
                dmg = jnp.where(gm[g], dmb[rows], jnp.zeros_like(dmb[rows]))
                dw_ref[g] += lax.dot_general(dmg, vnb[rows], _DIMS["nt"], preferred_element_type=F32)
            d_ref[rows, 0:c] = (dyv[rows] * mixed * dug[rows]).astype(d_ref.dtype)
            dbias_ref[...] += dmixed[rows]
            dvn_parts.append(dvn)
        dvn = jnp.concatenate(dvn_parts, axis=0)
        dg_acc[...] += _row_sum8(dvn * xhat)
        db_acc[...] += _row_sum8(dvn)
        dxh = dvn * g_ref[...]
        dvgl = rstd * (dxh - jnp.mean(dxh, axis=-1, keepdims=True)
                       - xhat * jnp.mean(dxh * xhat, axis=-1, keepdims=True))
        d_ref[:, c:2 * c] = (dvgl * dvg).astype(d_ref.dtype)

        @pl.when(i == n - 1)
        def _():
            dg_ref[...] = jnp.sum(dg_acc[...], axis=0, keepdims=True)
            db_ref[...] = jnp.sum(db_acc[...], axis=0, keepdims=True)
            r = lax.broadcasted_iota(jnp.int32, (CHUNK, CHUNK), 0)
            cc = lax.broadcasted_iota(jnp.int32, (CHUNK, CHUNK), 1)
            for g in range(N_GROUPS):
                dw_ref[g] = jnp.where(r >= cc, dw_ref[g], 0.0)

    full = lambda shp: pl.BlockSpec(shp, lambda i: (0,) * len(shp))
    return pl.pallas_call(
        body, name=name, grid=(n,),
        in_specs=[pl.BlockSpec((t, c), lambda i: (i, b_u)), pl.BlockSpec((t, c), lambda i: (i, b_v)),
                  full((1, c)), full((1, c)), full((N_GROUPS, CHUNK, CHUNK)), full((CHUNK, c)),
                  pl.BlockSpec((t, c), lambda i: (i, 0))],
        out_specs=[pl.BlockSpec((t, 2 * c), lambda i: (i, 0)), full((1, c)), full((1, c)),
                   full((N_GROUPS, CHUNK, CHUNK)), full((CHUNK, c))],
        out_shape=[jax.ShapeDtypeStruct((s, 2 * c), BF16), jax.ShapeDtypeStruct((1, c), F32),
                   jax.ShapeDtypeStruct((1, c), F32), jax.ShapeDtypeStruct((N_GROUPS, CHUNK, CHUNK), F32),
                   jax.ShapeDtypeStruct((CHUNK, c), F32)],
        scratch_shapes=[pltpu.VMEM((8, c), F32), pltpu.VMEM((8, c), F32)],
        compiler_params=_params(("arbitrary",)),
    )(h, h, ln_g, ln_b, w_s, bias, dy)


def _merge_fwd(h, acts, ws, b_gate, name):
    s = h.shape[0]
    d = D_MODEL
    t = _tile(s, 512)

    def body(gl0, gl1, gl2, a0, a1, a2, w0, w1, w2, b_ref, o_ref):
        acc = jnp.zeros((t, d), F32)
        for i, (gl, a, w) in enumerate(((gl0, a0, w0), (gl1, a1, w1), (gl2, a2, w2))):
            y = lax.dot_general(a[...], w[...], _DIMS["nn"], preferred_element_type=F32)
            acc = acc + _sigmoid(gl[...].astype(F32) + b_ref[i:i + 1, :]) * y
        o_ref[...] = acc.astype(o_ref.dtype)

    full = lambda arr: pl.BlockSpec(arr.shape, lambda i: (0, 0))
    return pl.pallas_call(
        body, name=name, grid=(s // t,),
        in_specs=[pl.BlockSpec((t, d), lambda i, b=b: (i, b)) for b in range(3)]
                 + [pl.BlockSpec((t, a.shape[1]), lambda i: (i, 0)) for a in acts]
                 + [full(w) for w in ws] + [full(b_gate)],
        out_specs=pl.BlockSpec((t, d), lambda i: (i, 0)),
        out_shape=jax.ShapeDtypeStruct((s, d), BF16),
        compiler_params=_params(("parallel",)),
    )(h, h, h, *acts, *ws, b_gate)


def _merge_bwd(h, acts, ws, b_gate, dmerged, name):
    s = h.shape[0]
    d = D_MODEL
    t = _tile(s, 512)
    n = s // t

    def body(gl0, gl1, gl2, a0, a1, a2, w0, w1, w2, b_ref, dm_ref, dy0, dy1, dy2, dgl_ref, db_ref, acc_ref):
        step = pl.program_id(0)

        @pl.when(step == 0)
        def _():
            acc_ref[...] = jnp.zeros_like(acc_ref)

        dm = dm_ref[...]
        for i, (gl, a, w, dy) in enumerate(((gl0, a0, w0, dy0), (gl1, a1, w1, dy1), (gl2, a2, w2, dy2))):
            y = lax.dot_general(a[...], w[...], _DIMS["nn"], preferred_element_type=F32)
            gate = _sigmoid(gl[...].astype(F32) + b_ref[i:i + 1, :])
            dy[...] = (dm * gate).astype(dy.dtype)
            dgl = dm * y * (gate * (1.0 - gate))
            dgl_ref[:, i * d:(i + 1) * d] = dgl.astype(dgl_ref.dtype)
            acc_ref[i] += _row_sum8(dgl)

        @pl.when(step == n - 1)
        def _():
            rows = [jnp.sum(acc_ref[k], axis=0, keepdims=True) for k in range(3)]
            db_ref[...] = jnp.concatenate(rows + [jnp.zeros((5, d), F32)], axis=0)

    full = lambda arr: pl.BlockSpec(arr.shape, lambda i: (0, 0))
    row = pl.BlockSpec((t, d), lambda i: (i, 0))
    return pl.pallas_call(
        body, name=name, grid=(n,),
        in_specs=[pl.BlockSpec((t, d), lambda i, b=b: (i, b)) for b in range(3)]
                 + [pl.BlockSpec((t, a.shape[1]), lambda i: (i, 0)) for a in acts]
                 + [full(w) for w in ws] + [full(b_gate), row],
        out_specs=[row, row, row, pl.BlockSpec((t, 3 * d), lambda i: (i, 0)), pl.BlockSpec((8, d), lambda i: (0, 0))],
        out_shape=[jax.ShapeDtypeStruct((s, d), BF16)] * 3
                  + [jax.ShapeDtypeStruct((s, IN_PAD), BF16), jax.ShapeDtypeStruct((8, d), F32)],
        scratch_shapes=[pltpu.VMEM((3, 8, d), F32)],
        compiler_params=_params(("arbitrary",)),
    )(h, h, h, *acts, *ws, b_gate, dmerged)


FF_BLK = D_FF // 2


def _ffn_act_fwd(h2, w, name):
    s = h2.shape[0]
    t = _tile(s, 512)
    r = t // HALO
    cw = 2 * FF_BLK

    def body(x_ref, xp_ref, w_ref, p_ref):
        i = pl.program_id(0)
        live = (i > 0).astype(F32)
        hc = _conv3(x_ref[...].astype(F32), xp_ref[...].astype(F32) * live, w_ref)
        p_ref[...] = (_gelu(hc[:, :FF_BLK]) * hc[:, FF_BLK:]).astype(p_ref.dtype)

    return pl.pallas_call(
        body, name=name, grid=(s // t, 2),
        in_specs=[pl.BlockSpec((t, cw), lambda i, j: (i, j)),
                  pl.BlockSpec((HALO, cw), lambda i, j: (jnp.maximum(i * r - 1, 0), j)),
                  pl.BlockSpec((8, cw), lambda i, j: (0, j))],
        out_specs=pl.BlockSpec((t, FF_BLK), lambda i, j: (i, j)),
        out_shape=jax.ShapeDtypeStruct((s, D_FF), BF16),
        compiler_params=_params(("parallel", "parallel")),
    )(h2, h2, w)


def _ffn_act_conv_bwd(h2, w, dp, name):
    s = h2.shape[0]
    t = _tile(s, 512)
    n = s // t
    r = t // HALO
    nh = s // HALO
    cw = 2 * FF_BLK

    def body(x_ref, xp_ref, xn_ref, dp_ref, dpn_ref, w_ref, dx_ref, dw_ref, acc_ref):
        i = pl.program_id(1)
        has_prev = (i > 0).astype(F32)
        has_next = (i < n - 1).astype(F32)
        x = jnp.concatenate([x_ref[...].astype(F32), xn_ref[...].astype(F32)], axis=0)
        xp = xp_ref[...].astype(F32) * has_prev
        x1 = _shift_down(x, xp, 1)
        x2 = _shift_down(x, xp, 2)
        hc = w_ref[2:3, :] * x + w_ref[1:2, :] * x1 + w_ref[0:1, :] * x2
        ga, dga = _gelu_and_grad(hc[:, :FF_BLK])
        dpv = jnp.concatenate([dp_ref[...].astype(F32), dpn_ref[...].astype(F32) * has_next], axis=0)
        dhc = jnp.concatenate([dpv * hc[:, FF_BLK:] * dga, dpv * ga], axis=1)
        cur, nxt = dhc[:t], dhc[t:]
        dx = w_ref[2:3, :] * cur + w_ref[1:2, :] * _shift_up(cur, nxt, 1) + w_ref[0:1, :] * _shift_up(cur, nxt, 2)
        dx_ref[...] = dx.astype(dx_ref.dtype)

        @pl.when(i == 0)
        def _():
            acc_ref[...] = jnp.zeros_like(acc_ref)

        acc_ref[0] += _row_sum8(cur * x2[:t])
        acc_ref[1] += _row_sum8(cur * x1[:t])
        acc_ref[2] += _row_sum8(cur * x[:t])

        @pl.when(i == n - 1)
        def _():
            rows = [jnp.sum(acc_ref[k], axis=0, keepdims=True) for k in range(3)]
            dw_ref[...] = jnp.concatenate(rows + [jnp.zeros((5, cw), F32)], axis=0)

    nxt_row = lambda j, i: jnp.minimum((i + 1) * r, nh - 1)
    return pl.pallas_call(
        body, name=name, grid=(2, n),
        in_specs=[pl.BlockSpec((t, cw), lambda j, i: (i, j)),
                  pl.BlockSpec((HALO, cw), lambda j, i: (jnp.maximum(i * r - 1, 0), j)),
                  pl.BlockSpec((HALO, cw), lambda j, i: (nxt_row(j, i), j)),
                  pl.BlockSpec((t, FF_BLK), lambda j, i: (i, j)),
                  pl.BlockSpec((HALO, FF_BLK), lambda j, i: (nxt_row(j, i), j)),
                  pl.BlockSpec((8, cw), lambda j, i: (0, j))],
        out_specs=[pl.BlockSpec((t, cw), lambda j, i: (i, j)), pl.BlockSpec((8, cw), lambda j, i: (0, j))],
        out_shape=[jax.ShapeDtypeStruct((s, 2 * D_FF), BF16), jax.ShapeDtypeStruct((8, 2 * D_FF), F32)],
        scratch_shapes=[pltpu.VMEM((3, 8, cw), F32)],
        compiler_params=_params(("parallel", "arbitrary")),
    )(h2, h2, h2, dp, dp, w)


def _adamw(w, g, m, v, name, dep=None):
    shape = w.shape
    c = shape[-1]
    rows = math.prod(shape[:-1])
    to2d = lambda a: a.reshape(rows, c)
    cap = max(8, (1 << 18) // c)
    tr = rows
    for cand in (2048, 1024, 512, 256, 128, 64, 32, 16, 8):
        if cand <= cap and rows % cand == 0:
            tr = cand
            break

    deps = [] if dep is None else [dep]

    def body(w_ref, g_ref, m_ref, v_ref, *rest):
        d_ref, nm_ref, nv_ref = rest[len(deps):]
        gv = g_ref[...]
        nm = ADAM_B1 * m_ref[...] + (1.0 - ADAM_B1) * gv
        nv = ADAM_B2 * v_ref[...] + (1.0 - ADAM_B2) * (gv * gv)
        m_hat = nm / (1.0 - ADAM_B1 ** ADAM_STEP)
        v_hat = nv / (1.0 - ADAM_B2 ** ADAM_STEP)
        d_ref[...] = -ADAM_LR * (m_hat / (jnp.sqrt(v_hat) + ADAM_EPS) + ADAM_WD * w_ref[...])
        nm_ref[...] = nm
        nv_ref[...] = nv

    blk = pl.BlockSpec((tr, c), lambda i: (i, 0))
    outs = pl.pallas_call(
        body, name=name, grid=(rows // tr,),
        in_specs=[blk] * 4 + [_DEP] * len(deps), out_specs=[blk] * 3,
        out_shape=[jax.ShapeDtypeStruct((rows, c), F32)] * 3,
        compiler_params=_params(("parallel",)),
    )(to2d(w), to2d(g), to2d(m), to2d(v), *deps)
    return tuple(o.reshape(shape) for o in outs)


_ANY = pl.BlockSpec(memory_space=pl.ANY)


def _place():
    x, y, c = lax.axis_index("x"), lax.axis_index("y"), lax.axis_index("c")
    others = [(1 - x, y), (x, 1 - y), (1 - x, 1 - y)]
    return x, y, c, others


def _all_gather_chips(shard, name):
    rws, cols = shard.shape
    half = rws // 2

    def body(x_ref, out_ref, send_sems, recv_sems, local_sem):
        x, y, c, others = _place()
        me = 2 * x + y
        sib = (x, y, 1 - c)

        def rows(chip, cc):
            return out_ref.at[chip, pl.ds(pl.multiple_of(cc * half, 16), half), :]

        def copy(k, src, dst, to):
            return pltpu.make_async_remote_copy(src_ref=src, dst_ref=dst, send_sem=send_sems.at[k],
                                                recv_sem=recv_sems.at[k], device_id=to, device_id_type=MESH)

        mine = pltpu.make_async_copy(x_ref, out_ref.at[me], local_sem)
        mine.start()
        my_half = x_ref.at[pl.ds(pl.multiple_of(c * half, 16), half), :]
        first = [copy(j, my_half, rows(me, c), (ox, oy, c)) for j, (ox, oy) in enumerate(others)]
        for cp in first:
            cp.start()
        passed = []
        for j, (ox, oy) in enumerate(others):
            blk = rows(2 * ox + oy, c)
            copy(j, blk, blk, (x, y, c)).wait_recv()
            fwd = copy(3 + j, blk, blk, sib)
            fwd.start()
            passed.append(fwd)
        for j, (ox, oy) in enumerate(others):
            blk = rows(2 * ox + oy, 1 - c)
            copy(3 + j, blk, blk, (x, y, c)).wait_recv()
        for cp in first + passed:
            cp.wait_send()
        mine.wait()

    return pl.pallas_call(
        body, name=name,
        in_specs=[_ANY], out_specs=_ANY,
        out_shape=jax.ShapeDtypeStruct((N_CHIPS, rws, cols), shard.dtype),
        scratch_shapes=[pltpu.SemaphoreType.DMA((6,)), pltpu.SemaphoreType.DMA((6,)), pltpu.SemaphoreType.DMA],
        compiler_params=pltpu.CompilerParams(has_side_effects=True),
    )(shard)


def _swap_halves(buf, name, dep=None):
    nb, rws, cols = buf.shape
    half = rws // 2
    deps = [] if dep is None else [dep]

    def body(b_ref, *rest):
        own_ref, sib_ref, send_sem, recv_sem, local_sem = rest[len(deps):]
        x, y, c, _ = _place()
        keep = b_ref.at[:, pl.ds(pl.multiple_of(c * half, 16), half), :]
        give = b_ref.at[:, pl.ds(pl.multiple_of((1 - c) * half, 16), half), :]
        mine = pltpu.make_async_copy(keep, own_ref, local_sem)
        mine.start()
        cp = pltpu.make_async_remote_copy(src_ref=give, dst_ref=sib_ref, send_sem=send_sem, recv_sem=recv_sem,
                                          device_id=(x, y, 1 - c), device_id_type=MESH)
        cp.start()
        cp.wait()
        mine.wait()

    shp = jax.ShapeDtypeStruct((nb, half, cols), buf.dtype)
    return pl.pallas_call(
        body, name=name,
        in_specs=[_ANY] * (1 + len(deps)), out_specs=[_ANY, _ANY], out_shape=[shp, shp],
        scratch_shapes=[pltpu.SemaphoreType.DMA, pltpu.SemaphoreType.DMA, pltpu.SemaphoreType.DMA],
        compiler_params=pltpu.CompilerParams(has_side_effects=True),
    )(buf, *deps)


def _add2(a, b, name):
    nb, rws, cols = a.shape
    t = _tile(rws, 256)
    if rws % t:
        t = rws

    def body(a_ref, b_ref, o_ref):
        o_ref[...] = (a_ref[...].astype(F32) + b_ref[...].astype(F32)).astype(o_ref.dtype)

    blk = pl.BlockSpec((1, t, cols), lambda i, j: (i, j, 0))
    return pl.pallas_call(
        body, name=name, grid=(nb, rws // t), in_specs=[blk, blk], out_specs=blk,
        out_shape=jax.ShapeDtypeStruct(a.shape, a.dtype),
        compiler_params=_params(("parallel", "parallel")),
    )(a, b)


def _exchange_chips(pre, name):
    nb, half, cols = pre.shape

    def body(p_ref, out_ref, send_sems, recv_sems, local_sem):
        x, y, c, others = _place()
        me = 2 * x + y
        mine = pltpu.make_async_copy(p_ref.at[me], out_ref.at[me], local_sem)
        mine.start()
        sends = []
        for j, (ox, oy) in enumerate(others):
            cp = pltpu.make_async_remote_copy(src_ref=p_ref.at[2 * ox + oy], dst_ref=out_ref.at[me],
                                              send_sem=send_sems.at[j], recv_sem=recv_sems.at[j],
                                              device_id=(ox, oy, c), device_id_type=MESH)
            cp.start()
            sends.append(cp)
        for j, (ox, oy) in enumerate(others):
            blk = out_ref.at[2 * ox + oy]
            pltpu.make_async_remote_copy(src_ref=blk, dst_ref=blk, send_sem=send_sems.at[j],
                                         recv_sem=recv_sems.at[j], device_id=(x, y, c),
                                         device_id_type=MESH).wait_recv()
        for cp in sends:
            cp.wait_send()
        mine.wait()

    return pl.pallas_call(
        body, name=name,
        in_specs=[_ANY], out_specs=_ANY, out_shape=jax.ShapeDtypeStruct(pre.shape, pre.dtype),
        scratch_shapes=[pltpu.SemaphoreType.DMA((3,)), pltpu.SemaphoreType.DMA((3,)), pltpu.SemaphoreType.DMA],
        compiler_params=pltpu.CompilerParams(has_side_effects=True),
    )(pre)


def _add4(parts, name):
    nb, half, cols = parts.shape
    t = _tile(half, 256)
    if half % t:
        t = half

    def body(p_ref, o_ref):
        acc = p_ref[0].astype(F32)
        for k in range(1, nb):
            acc = acc + p_ref[k].astype(F32)
        o_ref[...] = acc

    return pl.pallas_call(
        body, name=name, grid=(half // t,),
        in_specs=[pl.BlockSpec((nb, t, cols), lambda i: (0, i, 0))],
        out_specs=pl.BlockSpec((t, cols), lambda i: (i, 0)),
        out_shape=jax.ShapeDtypeStruct((half, cols), F32),
        compiler_params=_params(("parallel",)),
    )(parts)


def _join_halves(mine_half, name):
    half, cols = mine_half.shape

    def body(h_ref, out_ref, send_sem, recv_sem, local_sem):
        x, y, c, _ = _place()
        dst = out_ref.at[pl.ds(pl.multiple_of(c * half, 8), half), :]
        mine = pltpu.make_async_copy(h_ref, dst, local_sem)
        mine.start()
        cp = pltpu.make_async_remote_copy(src_ref=h_ref, dst_ref=dst, send_sem=send_sem, recv_sem=recv_sem,
                                          device_id=(x, y, 1 - c), device_id_type=MESH)
        cp.start()
        cp.wait()
        mine.wait()

    return pl.pallas_call(
        body, name=name,
        in_specs=[_ANY], out_specs=_ANY, out_shape=jax.ShapeDtypeStruct((2 * half, cols), mine_half.dtype),
        scratch_shapes=[pltpu.SemaphoreType.DMA, pltpu.SemaphoreType.DMA, pltpu.SemaphoreType.DMA],
        compiler_params=pltpu.CompilerParams(has_side_effects=True),
    )(mine_half)


def _reduce_scatter_chips(buf, tag, dep=None):
    own, sib = _swap_halves(buf, "rs_swap_" + tag, dep)
    pre = _add2(own, sib, "rs_add2_" + tag)
    parts = _exchange_chips(pre, "rs_xchg_" + tag)
    red = _add4(parts, "rs_add4_" + tag)
    return _join_halves(red, "rs_join_" + tag)


MAX_DMA_BYTES = 2 * 1024 * 1024
ROW_ALIGN = 16


def _pieces(rows, row_bytes):
    n = max(1, -(-(rows * row_bytes) // MAX_DMA_BYTES))
    step = -(-(-(-rows // n)) // ROW_ALIGN) * ROW_ALIGN
    return [(r, min(step, rows - r)) for r in range(0, rows, step)]


def _half_plan(arrays, row_axis):
    plan = []
    for a, arr in enumerate(arrays):
        row_bytes = math.prod(arr.shape[row_axis + 1:]) * arr.dtype.itemsize * (arr.shape[0] if row_axis else 1)
        plan += [(a, r0, nr) for r0, nr in _pieces(arr.shape[row_axis] // 2, row_bytes)]
    return plan


def _rows(start, size):
    return pl.ds(pl.multiple_of(start, ROW_ALIGN), size)


def _remote(src, dst, send_sems, recv_sems, k, to):
    return pltpu.make_async_remote_copy(src_ref=src, dst_ref=dst, send_sem=send_sems.at[k], recv_sem=recv_sems.at[k],
                                        device_id=to, device_id_type=MESH)


def _comm_call(body, name, ins, out_shapes, n_remote, n_local, aliases=None):
    return pl.pallas_call(
        body, name=name,
        in_specs=[_ANY] * len(ins), out_specs=[_ANY] * len(out_shapes), out_shape=out_shapes,
        scratch_shapes=[pltpu.SemaphoreType.DMA((n_remote,)), pltpu.SemaphoreType.DMA((n_remote,)),
                        pltpu.SemaphoreType.DMA((max(n_local, 1),))],
        input_output_aliases=aliases or {},
        compiler_params=pltpu.CompilerParams(has_side_effects=True),
    )(*ins)


def _cast_shard(w, l, me_idx, name):
    _, k, cols = w.shape
    tr = _tile(k, 256)
    if k % tr:
        tr = k

    def body(me_ref, w_ref, s_ref, land_ref):
        del me_ref
        v = w_ref[...].astype(BF16)
        s_ref[...] = v
        land_ref[...] = v

    grid_spec = pltpu.PrefetchScalarGridSpec(
        num_scalar_prefetch=1, grid=(k // tr,),
        in_specs=[pl.BlockSpec((None, tr, cols), lambda i, me: (l, i, 0))],
        out_specs=[pl.BlockSpec((tr, cols), lambda i, me: (i, 0)),
                   pl.BlockSpec((None, tr, cols), lambda i, me: (me[0], i, 0))])
    return pl.pallas_call(
        body, name=name, grid_spec=grid_spec,
        out_shape=[jax.ShapeDtypeStruct((k, cols), BF16), jax.ShapeDtypeStruct((N_CHIPS, k, cols), BF16)],
        compiler_params=_params(("parallel",)),
    )(me_idx, w)


def _gather_d2d(lands, name):
    n = len(lands)
    plan = _half_plan(lands, 1)
    plan = [(a, r0, nr) for a, r0, nr in plan]

    def body(*refs):
        out_refs = refs[n:2 * n]
        send_sems, recv_sems, _ = refs[2 * n:]
        x, y, c, others = _place()
        sends = []
        for i, (a, r0, nr) in enumerate(plan):
            rows = _rows(c * (lands[a].shape[1] // 2) + r0, nr)
            for j, (ox, oy) in enumerate(others):
                blk = out_refs[a].at[2 * ox + oy, rows, :]
                cp = _remote(blk, blk, send_sems, recv_sems, 3 * i + j, (x, y, 1 - c))
                cp.start()
                sends.append(cp)
        for i, (a, r0, nr) in enumerate(plan):
            rows = _rows((1 - c) * (lands[a].shape[1] // 2) + r0, nr)
            for j, (ox, oy) in enumerate(others):
                blk = out_refs[a].at[2 * ox + oy, rows, :]
                _remote(blk, blk, send_sems, recv_sems, 3 * i + j, (x, y, c)).wait_recv()
        for cp in sends:
            cp.wait_send()

    outs = [jax.ShapeDtypeStruct(a.shape, a.dtype) for a in lands]
    return _comm_call(body, name, lands, outs, 3 * len(plan), 0, aliases={a: a for a in range(n)})


def _rs_swap(ts, name):
    n = len(ts)
    plan = _half_plan(ts, 1)

    def body(*refs):
        t_refs, out_refs = refs[:n], refs[n:2 * n]
        send_sems, recv_sems, _ = refs[2 * n:]
        x, y, c, _o = _place()
        sends = []
        for i, (a, r0, nr) in enumerate(plan):
            src = t_refs[a].at[:, _rows((1 - c) * (ts[a].shape[1] // 2) + r0, nr), :]
            cp = _remote(src, out_refs[a].at[:, pl.ds(r0, nr), :], send_sems, recv_sems, i, (x, y, 1 - c))
            cp.start()
            sends.append(cp)
        for i, (a, r0, nr) in enumerate(plan):
            blk = out_refs[a].at[:, pl.ds(r0, nr), :]
            _remote(blk, blk, send_sems, recv_sems, i, (x, y, c)).wait_recv()
        for cp in sends:
            cp.wait_send()

    outs = [jax.ShapeDtypeStruct((t.shape[0], t.shape[1] // 2, t.shape[2]), t.dtype) for t in ts]
    return _comm_call(body, name, ts, outs, len(plan), 0)


def _add_halves(ts, gots, c_idx, me_idx, name):
    n = len(ts)

    def body(c_ref, me_ref, *refs):
        del c_ref
        t_refs, g_refs = refs[:n], refs[n:2 * n]
        o_refs, mine_refs = refs[2 * n:3 * n], refs[3 * n:]
        for t_ref, g_ref, o_ref, mine_ref in zip(t_refs, g_refs, o_refs, mine_refs):
            v = (t_ref[...].astype(F32) + g_ref[...].astype(F32)).astype(o_ref.dtype)
            o_ref[...] = v

            @pl.when(pl.program_id(0) == me_ref[0])
            def _():
                mine_ref[...] = v

    blks = [(1, g.shape[1], g.shape[2]) for g in gots]
    same = [pl.BlockSpec(b, lambda i, c, me: (i, 0, 0)) for b in blks]
    grid_spec = pltpu.PrefetchScalarGridSpec(
        num_scalar_prefetch=2, grid=(N_CHIPS,),
        in_specs=[pl.BlockSpec(b, lambda i, c, me: (i, c[0], 0)) for b in blks] + same,
        out_specs=same + [pl.BlockSpec(b, lambda i, c, me: (me[0], 0, 0)) for b in blks])
    shapes = [jax.ShapeDtypeStruct(g.shape, g.dtype) for g in gots]
    outs = pl.pallas_call(
        body, name=name, grid_spec=grid_spec, out_shape=shapes + shapes,
        compiler_params=_params(("arbitrary",)),
    )(c_idx, me_idx, *ts, *gots)
    return outs[:n], outs[n:]


def _add4_halves(parts, c_idx, name):
    n = len(parts)
    steps = 2

    def body(c_ref, *refs):
        del c_ref
        for p_ref, o_ref in zip(refs[:n], refs[n:]):
            acc = p_ref[0].astype(F32)
            for k in range(1, N_CHIPS):
                acc = acc + p_ref[k].astype(F32)
            o_ref[...] = acc

    grid_spec = pltpu.PrefetchScalarGridSpec(
        num_scalar_prefetch=1, grid=(steps,),
        in_specs=[pl.BlockSpec((N_CHIPS, p.shape[1] // steps, p.shape[2]), lambda i, c: (0, i, 0)) for p in parts],
        out_specs=[pl.BlockSpec((p.shape[1] // steps, p.shape[2]), lambda i, c: (c[0] * steps + i, 0))
                   for p in parts])
    return pl.pallas_call(
        body, name=name, grid_spec=grid_spec,
        out_shape=[jax.ShapeDtypeStruct((2 * p.shape[1], p.shape[2]), F32) for p in parts],
        compiler_params=_params(("parallel",)),
    )(c_idx, *parts)


def _rs_join(fulls, name):
    n = len(fulls)
    plan = _half_plan(fulls, 0)

    def body(*refs):
        out_refs = refs[n:2 * n]
        send_sems, recv_sems, _ = refs[2 * n:]
        x, y, c, _o = _place()
        sends = []
        for i, (a, r0, nr) in enumerate(plan):
            blk = out_refs[a].at[_rows(c * (fulls[a].shape[0] // 2) + r0, nr), :]
            cp = _remote(blk, blk, send_sems, recv_sems, i, (x, y, 1 - c))
            cp.start()
            sends.append(cp)
        for i, (a, r0, nr) in enumerate(plan):
            blk = out_refs[a].at[_rows((1 - c) * (fulls[a].shape[0] // 2) + r0, nr), :]
            _remote(blk, blk, send_sems, recv_sems, i, (x, y, c)).wait_recv()
        for cp in sends:
            cp.wait_send()

    outs = [jax.ShapeDtypeStruct(f.shape, f.dtype) for f in fulls]
    return _comm_call(body, name, fulls, outs, len(plan), 0, aliases={a: a for a in range(n)})


_HBM = pl.BlockSpec(memory_space=pltpu.HBM)
_SEM = pl.BlockSpec(memory_space=pltpu.SEMAPHORE)
_EFFECT = pltpu.SideEffectType.DATAFLOW_SIDE_EFFECTING


def _ici_plan(kind, a_list):
    if kind == "gather":
        return _half_plan(a_list, 0)
    plan = []
    for a, p in enumerate(a_list):
        plan += [(a, r0, nr) for r0, nr in _pieces(p.shape[1], p.shape[2] * p.dtype.itemsize)]
    return plan


def _ici_refs(kind, a_ref, b_ref, a_shape, r0, nr, c, me, peer):
    if kind == "gather":
        rows = _rows(c * (a_shape[0] // 2) + r0, nr)
        return a_ref.at[rows, :], b_ref.at[me, rows, :], b_ref.at[peer, rows, :]
    rows = pl.ds(r0, nr)
    return a_ref.at[peer, rows, :], b_ref.at[me, rows, :], b_ref.at[peer, rows, :]


def _ici_start(kind, a_list, b_list, name):
    n = len(a_list)
    plan = _ici_plan(kind, a_list)
    shapes = [a.shape for a in a_list]

    def body(*refs):
        a_refs, b_refs = refs[:n], refs[n:2 * n]
        send_sems, recv_sems = refs[2 * n], refs[2 * n + 1]
        token = refs[4 * n + 2]
        x, y, c, others = _place()
        me = 2 * x + y
        for i, (a, r0, nr) in enumerate(plan):
            for j, (ox, oy) in enumerate(others):
                src, dst, _ = _ici_refs(kind, a_refs[a], b_refs[a], shapes[a], r0, nr, c, me, 2 * ox + oy)
                _remote(src, dst, send_sems, recv_sems, 3 * i + j, (ox, oy, c)).start()
        token[...] = jnp.zeros_like(token)

    hbm = lambda v: pltpu.HBM(v.shape, v.dtype)
    ncp = 3 * len(plan)
    outs = pl.pallas_call(
        body, name=name,
        in_specs=[_HBM] * (2 * n),
        out_specs=[_SEM, _SEM] + [_HBM] * (2 * n) + [pl.BlockSpec(memory_space=pltpu.VMEM)],
        out_shape=[pltpu.SemaphoreType.DMA((ncp,)), pltpu.SemaphoreType.DMA((ncp,))]
                  + [hbm(v) for v in a_list] + [hbm(v) for v in b_list] + [jax.ShapeDtypeStruct((8, LANES), F32)],
        input_output_aliases={i: 2 + i for i in range(2 * n)},
        compiler_params=pltpu.CompilerParams(has_side_effects=_EFFECT),
    )(*[pltpu.with_memory_space_constraint(v, pltpu.HBM) for v in list(a_list) + list(b_list)])
    return outs[0], outs[1], outs[2:2 + n], outs[2 + n:2 + 2 * n], outs[2 + 2 * n]


def _ici_wait(kind, started, after, name):
    send_sems, recv_sems, a_list, b_list, _ = started
    afters = list(after) if isinstance(after, (list, tuple)) else [after]
    n = len(a_list)
    plan = _ici_plan(kind, a_list)
    shapes = [a.shape for a in a_list]

    def body(*refs):
        a_refs, b_refs = refs[:n], refs[n:2 * n]
        send_sems, recv_sems = refs[2 * n], refs[2 * n + 1]
        x, y, c, others = _place()
        me = 2 * x + y
        for i, (a, r0, nr) in enumerate(plan):
            for j, (ox, oy) in enumerate(others):
                src, dst, land = _ici_refs(kind, a_refs[a], b_refs[a], shapes[a], r0, nr, c, me, 2 * ox + oy)
                _remote(src, dst, send_sems, recv_sems, 3 * i + j, (ox, oy, c)).wait_send()
                _remote(land, land, send_sems, recv_sems, 3 * i + j, (x, y, c)).wait_recv()

    hbm = lambda v: pltpu.HBM(v.shape, v.dtype)
    outs = pl.pallas_call(
        body, name=name,
        in_specs=[_HBM] * (2 * n) + [_SEM, _SEM] + [_ANY] * len(afters),
        out_specs=[_HBM] * (2 * n),
        out_shape=[hbm(v) for v in a_list] + [hbm(v) for v in b_list],
        input_output_aliases={i: i for i in range(2 * n)},
        compiler_params=pltpu.CompilerParams(has_side_effects=_EFFECT),
    )(*a_list, *b_list, send_sems, recv_sems, *afters)
    return outs[n:]


def _rs_begin(ts, c_idx, me_idx, tag):
    got = _rs_swap(ts, "rs_swap_" + tag)
    pres, mine = _add_halves(ts, got, c_idx, me_idx, "rs_add2_" + tag)
    return _ici_start("scatter", pres, mine, "rs_xchg_start_" + tag)


def _rs_finish(started, after, c_idx, tag):
    parts = _ici_wait("scatter", started, after, "rs_xchg_wait_" + tag)
    return _rs_join(_add4_halves(parts, c_idx, "rs_add4_" + tag), "rs_join_" + tag)


def _pack_rows(pieces, rows, dtype):
    flat = jnp.concatenate([p.astype(dtype).reshape(-1) for p in pieces])
    return jnp.pad(flat, (0, rows * PACK_COLS - flat.shape[0])).reshape(rows, PACK_COLS)


def _unpack(flat, shapes):
    out, off = [], 0
    for shp in shapes:
        size = math.prod(shp)
        out.append(flat[off:off + size].reshape(shp))
        off += size
    return out


def _rows_for(n_elems, mult):
    rows = -(-n_elems // PACK_COLS)
    return -(-rows // mult) * mult


BIG_SHARDS = [("w_in", (D_MODEL, 1474)), ("w_branch_att", (D_ATT, 256)), ("w_branch_conv", (D_CONV, 256)),
              ("w_branch_sgu", (D_SGU, 256)), ("w_out", (256, D_MODEL)), ("w_ffn_up", (D_MODEL, FF_BLK)),
              ("w_ffn_down", (D_FF // N_CHIPS, D_MODEL))]
SMALL_SHARDS = [("b_gate", (3, 256)), ("conv_mix_w", (3, 64)), ("conv_ffn_w", (3, FF_BLK))]
REPLICATED = [("pre_mix_g", (D_MODEL,)), ("post_mix_g", (D_MODEL,)), ("pre_ffn_g", (D_MODEL,)),
              ("post_ffn_g", (D_MODEL,)), ("b_forget", (N_HEADS,)), ("sgu_ln_g", (D_SGU,)), ("sgu_ln_b", (D_SGU,)),
              ("sgu_w", (N_GROUPS, CHUNK, CHUNK)), ("sgu_b", (N_GROUPS, CHUNK))]
WEIGHT_ORDER = ["pre_mix_g", "post_mix_g", "pre_ffn_g", "post_ffn_g", "w_in", "b_forget", "b_gate", "conv_mix_w",
                "sgu_ln_g", "sgu_ln_b", "sgu_w", "sgu_b", "w_branch_att", "w_branch_conv", "w_branch_sgu", "w_out",
                "w_ffn_up", "conv_ffn_w", "w_ffn_down"]

_SMALL_ELEMS = sum(math.prod(s) for _, s in SMALL_SHARDS)
_REP_ELEMS = sum(math.prod(s) for _, s in REPLICATED)
_REP_QUARTER = -(-(DEPTH * _REP_ELEMS) // N_CHIPS)
SMALL_PARAM_ROWS = _rows_for(DEPTH * _SMALL_ELEMS, 32)
SMALL_ROWS = _rows_for(DEPTH * _SMALL_ELEMS + _REP_QUARTER, 32)
IN_WIDTH = 5896
IN_SHARD = IN_WIDTH // N_CHIPS
IN_SHARD_PAD = 1536
IN_PAD = 6144


def _gather_small(wts):
    shard = _pack_rows([wts[n] for n, _ in SMALL_SHARDS], SMALL_PARAM_ROWS, F32)
    full = _all_gather_chips(shard, "gather_small_params").reshape(N_CHIPS, -1)
    per_chip = [_unpack(full[j], [(DEPTH,) + s for _, s in SMALL_SHARDS]) for j in range(N_CHIPS)]
    return {n: jnp.concatenate([per_chip[j][i] for j in range(N_CHIPS)], axis=-1)
            for i, (n, _) in enumerate(SMALL_SHARDS)}


BIG_NAMES = [n for n, _ in BIG_SHARDS]
FIRST_NAMES = ["w_in"]
LATE_NAMES = BIG_NAMES[1:]


def _gather_begin(wts, l, me_idx, names, tag):
    cast = [_cast_shard(wts[n], l, me_idx, "cast_" + n) for n in names]
    return _ici_start("gather", [sh for sh, _ in cast], [ld for _, ld in cast], "gather_ici_start_" + tag)


def _gather_finish(started, after, names, tag):
    lands = _ici_wait("gather", started, after, "gather_ici_wait_" + tag)
    return dict(zip(names, _gather_d2d(lands, "gather_d2d_" + tag)))


def _pad_rows(a, rows):
    return jnp.pad(a, ((0, rows - a.shape[0]), (0, 0)))


def _whole_cols(land):
    return land.transpose(1, 0, 2).reshape(land.shape[1], -1)


_O_F = 3 * D_ATT
_O_B = _O_F + N_HEADS
_O_GL = _O_B + 3 * D_CONV + 2 * D_SGU


_LOCAL_ORDER = [(_O_GL, IN_WIDTH), (0, _O_F), (_O_B, _O_GL), (_O_F, _O_B)]


def _own_cols(land, lo, hi):
    pieces = []
    for j in range(N_CHIPS):
        a, b = max(lo, j * IN_SHARD), min(hi, (j + 1) * IN_SHARD)
        if a < b:
            pieces.append(land[j][:, a - j * IN_SHARD:b - j * IN_SHARD])
    return pieces


def _local_cols(m, lo, hi):
    pieces, off = [], 0
    for a, b in _LOCAL_ORDER:
        x, y = max(lo, a), min(hi, b)
        if x < y:
            pieces.append((x, m[:, off + x - a:off + y - a]))
        off += b - a
    pieces = [p for _, p in sorted(pieces, key=lambda t: t[0])]
    if hi > IN_WIDTH:
        pieces.append(jnp.zeros((m.shape[0], hi - max(lo, IN_WIDTH)), m.dtype))
    return pieces


def _prep_first(wts, lands, small, l):
    land = lands["w_in"]
    cf = small["conv_ffn_w"][l]
    blk = lambda a, j: a[:, j * FF_BLK:(j + 1) * FF_BLK]
    local = [piece for lo, hi in _LOCAL_ORDER for piece in _own_cols(land, lo, hi)]
    return {
        "w_p": jnp.concatenate(local + [jnp.zeros((D_MODEL, IN_PAD - IN_WIDTH), BF16)], axis=1),
        "wf_t": _pad_rows(jnp.concatenate(_own_cols(land, _O_F, _O_B), axis=1).T, F_ROWS),
        "b_forget": _pad_rows(wts["b_forget"][l].reshape(N_HEADS, 1), F_ROWS),
        "b_gate": _pad_rows(small["b_gate"][l], 8),
        "conv_mix_w": _pad_rows(small["conv_mix_w"][l], 8),
        "conv_ffn_w": _pad_rows(jnp.concatenate([blk(cf, 0), blk(cf, 2), blk(cf, 1), blk(cf, 3)], axis=1), 8),
        "pre_mix_g": wts["pre_mix_g"][l].reshape(1, -1), "post_mix_g": wts["post_mix_g"][l].reshape(1, -1),
        "pre_ffn_g": wts["pre_ffn_g"][l].reshape(1, -1), "post_ffn_g": wts["post_ffn_g"][l].reshape(1, -1),
        "ln_g": wts["sgu_ln_g"][l].reshape(1, -1), "ln_b": wts["sgu_ln_b"][l].reshape(1, -1),
        "sgu_w": wts["sgu_w"][l],
        "sgu_bias": jnp.repeat(wts["sgu_b"][l].T, HEAD_DIM, axis=1),
    }


def _prep_late(lands):
    up = lands["w_ffn_up"]
    return {
        "w_att": _whole_cols(lands["w_branch_att"]), "w_conv": _whole_cols(lands["w_branch_conv"]),
        "w_sgu": _whole_cols(lands["w_branch_sgu"]),
        "w_out": lands["w_out"].reshape(D_MODEL, D_MODEL),
        "w_up": jnp.concatenate([up[0], up[2], up[1], up[3]], axis=1),
        "w_down": lands["w_ffn_down"].reshape(D_FF, D_MODEL),
    }


def _layer_fwd(x, p, dep=None, late=None):
    s = x.shape[0]
    xn = _rms_fwd(x, p["pre_mix_g"], "rms_pre_mix", dep)
    h = _mm(xn, p["w_p"], "nn", BF16, "mm_in", s, 512, D_MODEL)
    f_row = _mm(p["wf_t"], xn, "nt", F32, "mm_forget", F_ROWS, 2048, D_MODEL)
    ck = _gate_fwd(f_row, p["b_forget"], "gate_fwd")
    o, o_f32, lse = _attn_fwd(h, ck, "attn_fwd")
    yc = _sconv_fwd(h, p["conv_mix_w"], "sconv_fwd")
    ys = _sgu_fwd(h, p["ln_g"], p["ln_b"], p["sgu_w"], p["sgu_bias"], "sgu_fwd")
    if late is not None:
        p.update(late(o))
    merged = _merge_fwd(h, (o, yc, ys), (p["w_att"], p["w_conv"], p["w_sgu"]), p["b_gate"], "merge_fwd")
    mo = _mm(merged, p["w_out"], "nn", F32, "mm_out", 2048, 512, D_MODEL)
    x1, xn2 = _resid_post_norm(x, mo, p["post_mix_g"], p["pre_ffn_g"], "post_mix")
    h2 = _mm(xn2, p["w_up"], "nn", BF16, "mm_up", 2048, 512, D_MODEL)
    pact = _ffn_act_fwd(h2, p["conv_ffn_w"], "ffn_act_fwd")
    ff = _mm(pact, p["w_down"], "nn", F32, "mm_down", 1024, D_MODEL, D_FF)
    x2 = _resid_post(x1, ff, p["post_ffn_g"], "post_ffn")
    saved = dict(x=x, xn=xn, h=h, f_row=f_row, ck=ck, o=o, o_f32=o_f32, lse=lse, yc=yc, ys=ys, merged=merged, mo=mo, x1=x1,
                 xn2=xn2, h2=h2, pact=pact, ff=ff)
    return x2, saved


def _layer_bwd(dx2, p, sv, dep=None, early=None):
    s = dx2.shape[0]
    g = {}
    same = lambda b: b
    dff, g["post_ffn_g"] = _rms_bwd(sv["ff"], p["post_ffn_g"], [dx2], None, BF16, "post_ffn_bwd", dep)
    dpact = _mm(dff, p["w_down"], "nt", BF16, "mm_down_dx", 2048, FF_BLK, D_MODEL)
    t_down = _mm(sv["pact"], dff, "tn", BF16, "mm_down_dw", 256, D_MODEL, s).reshape(N_CHIPS, -1, D_MODEL)
    dh2, dconv_ffn = _ffn_act_conv_bwd(sv["h2"], p["conv_ffn_w"], dpact, "ffn_act_conv_bwd")
    dxn2 = _mm(dh2, p["w_up"], "nt", F32, "mm_up_dx", 512, D_MODEL, 2 * D_FF)
    t_up = _mm(sv["xn2"], dh2, "tn", BF16, "mm_up_dw", 512, FF_BLK, s, chip_of=lambda b: (b % 2) * 2 + b // 2)
    dx1, g["pre_ffn_g"] = _rms_bwd(sv["x1"], p["pre_ffn_g"], [dxn2], dx2, F32, "pre_ffn_bwd")
    dep_mix = early([t_up, t_down]) if early is not None else None
    dmo, g["post_mix_g"] = _rms_bwd(sv["mo"], p["post_mix_g"], [dx1], None, BF16, "post_mix_bwd", dep_mix)
    dmerged = _mm(dmo, p["w_out"], "nt", F32, "mm_out_dx", 2048, 512, D_MODEL)
    t_out = _mm(sv["merged"], dmo, "tn", BF16, "mm_out_dw", 512, D_MODEL, s).reshape(N_CHIPS, -1, D_MODEL)
    acts = (sv["o"], sv["yc"], sv["ys"])
    ws = (p["w_att"], p["w_conv"], p["w_sgu"])
    dy_a, dy_c, dy_s, dgl, db_gate = _merge_bwd(sv["h"], acts, ws, p["b_gate"], dmerged, "merge_bwd")
    do = _mm(dy_a, p["w_att"], "nt", BF16, "mm_att_dx", 2048, D_ATT, D_MODEL)
    dyc = _mm(dy_c, p["w_conv"], "nt", BF16, "mm_conv_dx", 2048, D_CONV, D_MODEL)
    dys = _mm(dy_s, p["w_sgu"], "nt", BF16, "mm_sgu_dx", 2048, D_SGU, D_MODEL)
    t_att = _mm(sv["o"], dy_a, "tn", BF16, "mm_att_dw", D_ATT, 256, s, chip_of=same)
    t_conv = _mm(sv["yc"], dy_c, "tn", BF16, "mm_conv_dw", D_CONV, 256, s, chip_of=same)
    t_sgu = _mm(sv["ys"], dy_s, "tn", BF16, "mm_sgu_dw", D_SGU, 256, s, chip_of=same)
    d_conv, dconv_mix = _sconv_bwd(sv["h"], p["conv_mix_w"], dyc, "sconv_bwd")
    d_sgu, g["sgu_ln_g"], g["sgu_ln_b"], g["sgu_w"], dbias = _sgu_bwd(
        sv["h"], p["ln_g"], p["ln_b"], p["sgu_w"], p["sgu_bias"], dys, "sgu_bwd")
    dq, dk, dv, dc_even, dc_odd = _attn_bwd(sv["h"], sv["ck"], sv["o_f32"], sv["lse"], do, "attn_bwd")
    df, db_forget = _gate_bwd(sv["f_row"], p["b_forget"], dc_even, dc_odd, "gate_bwd")
    f_cols = jnp.concatenate([df[:N_HEADS].T, jnp.zeros((s, IN_PAD - IN_WIDTH), BF16)], axis=1)
    dh = _assemble_dh(dgl, [dq, dk, dv, d_conv, d_sgu, f_cols], "assemble_dh")
    dxn = _mm(dh, p["w_p"], "nt", F32, "mm_in_dx", 512, D_MODEL, IN_PAD)
    dw_p = _mm(sv["xn"], dh, "tn", BF16, "mm_in_dw", D_MODEL, 512, s)
    t_in = jnp.stack([jnp.concatenate(_local_cols(dw_p, j * IN_SHARD, j * IN_SHARD + IN_SHARD_PAD), axis=1)
                      for j in range(N_CHIPS)])
    dx, g["pre_mix_g"] = _rms_bwd(sv["x"], p["pre_mix_g"], [dxn], dx1, F32, "pre_mix_bwd")
    blk = lambda a, j: a[:, j * FF_BLK:(j + 1) * FF_BLK]
    g["conv_ffn_w"] = jnp.concatenate([blk(dconv_ffn, 0), blk(dconv_ffn, 2), blk(dconv_ffn, 1),
                                       blk(dconv_ffn, 3)], axis=1)[:3]
    g["conv_mix_w"] = dconv_mix[:3]
    g["b_gate"] = db_gate[:3]
    g["b_forget"] = db_forget[:N_HEADS, 0]
    g["sgu_b"] = jnp.sum(dbias.reshape(CHUNK, N_GROUPS, HEAD_DIM), axis=-1).T
    for n in ("pre_mix_g", "post_mix_g", "pre_ffn_g", "post_ffn_g", "sgu_ln_g", "sgu_ln_b"):
        g[n] = g[n].reshape(-1)
    mix = [t_in, t_att, t_conv, t_sgu, t_out]
    return dx, (mix if early is not None else mix + [t_up, t_down]), g


def _assemble_dh(dh, pieces, name):
    s = dh.shape[0]
    t = _tile(s, 512)
    width = sum(a.shape[1] for a in pieces)
    assert 2 * width == dh.shape[1]

    def body(*refs):
        out = refs[-1]
        col = 0
        for ref in refs[1:-1]:
            w = ref.shape[1]
            out[:, col:col + w] = ref[...].astype(out.dtype)
            col += w

    return pl.pallas_call(
        body, name=name, grid=(s // t,),
        in_specs=[_ANY] + [pl.BlockSpec((t, a.shape[1]), lambda i: (i, 0)) for a in pieces],
        out_specs=pl.BlockSpec((t, width), lambda i: (i, 1)),
        out_shape=jax.ShapeDtypeStruct(dh.shape, dh.dtype),
        input_output_aliases={0: 0},
        compiler_params=_params(("parallel",)),
    )(dh, *pieces)


def _shard_cols(a, j):
    w = a.shape[-1] // N_CHIPS
    return a[..., j * w:(j + 1) * w]


def kernel(x, pre_mix_g, post_mix_g, pre_ffn_g, post_ffn_g, w_in, b_forget, b_gate, conv_mix_w, sgu_ln_g, sgu_ln_b, sgu_w, sgu_b, w_branch_att, w_branch_conv, w_branch_sgu, w_out, w_ffn_up, conv_ffn_w, w_ffn_down, loss_target, m_pre_mix_g, m_post_mix_g, m_pre_ffn_g, m_post_ffn_g, m_w_in, m_b_forget, m_b_gate, m_conv_mix_w, m_sgu_ln_g, m_sgu_ln_b, m_sgu_w, m_sgu_b, m_w_branch_att, m_w_branch_conv, m_w_branch_sgu, m_w_out, m_w_ffn_up, m_conv_ffn_w, m_w_ffn_down, v_pre_mix_g, v_post_mix_g, v_pre_ffn_g, v_post_ffn_g, v_w_in, v_b_forget, v_b_gate, v_conv_mix_w, v_sgu_ln_g, v_sgu_ln_b, v_sgu_w, v_sgu_b, v_w_branch_att, v_w_branch_conv, v_w_branch_sgu, v_w_out, v_w_ffn_up, v_conv_ffn_w, v_w_ffn_down):
    wts = dict(pre_mix_g=pre_mix_g, post_mix_g=post_mix_g, pre_ffn_g=pre_ffn_g, post_ffn_g=post_ffn_g, w_in=w_in,
               b_forget=b_forget, b_gate=b_gate, conv_mix_w=conv_mix_w, sgu_ln_g=sgu_ln_g, sgu_ln_b=sgu_ln_b,
               sgu_w=sgu_w, sgu_b=sgu_b, w_branch_att=w_branch_att, w_branch_conv=w_branch_conv,
               w_branch_sgu=w_branch_sgu, w_out=w_out, w_ffn_up=w_ffn_up, conv_ffn_w=conv_ffn_w,
               w_ffn_down=w_ffn_down)
    moms = dict(pre_mix_g=m_pre_mix_g, post_mix_g=m_post_mix_g, pre_ffn_g=m_pre_ffn_g, post_ffn_g=m_post_ffn_g,
                w_in=m_w_in, b_forget=m_b_forget, b_gate=m_b_gate, conv_mix_w=m_conv_mix_w, sgu_ln_g=m_sgu_ln_g,
                sgu_ln_b=m_sgu_ln_b, sgu_w=m_sgu_w, sgu_b=m_sgu_b, w_branch_att=m_w_branch_att,
                w_branch_conv=m_w_branch_conv, w_branch_sgu=m_w_branch_sgu, w_out=m_w_out, w_ffn_up=m_w_ffn_up,
                conv_ffn_w=m_conv_ffn_w, w_ffn_down=m_w_ffn_down)
    vels = dict(pre_mix_g=v_pre_mix_g, post_mix_g=v_post_mix_g, pre_ffn_g=v_pre_ffn_g, post_ffn_g=v_post_ffn_g,
                w_in=v_w_in, b_forget=v_b_forget, b_gate=v_b_gate, conv_mix_w=v_conv_mix_w, sgu_ln_g=v_sgu_ln_g,
                sgu_ln_b=v_sgu_ln_b, sgu_w=v_sgu_w, sgu_b=v_sgu_b, w_branch_att=v_w_branch_att,
                w_branch_conv=v_w_branch_conv, w_branch_sgu=v_w_branch_sgu, w_out=v_w_out, w_ffn_up=v_w_ffn_up,
                conv_ffn_w=v_conv_ffn_w, w_ffn_down=v_w_ffn_down)

    c_idx = lax.axis_index("c").astype(jnp.int32).reshape(1)
    me_idx = (2 * lax.axis_index("x") + lax.axis_index("y")).astype(jnp.int32).reshape(1)
    small = _gather_small(wts)

    xs = x[0]
    layers, saved = [], []
    first = _gather_begin(wts, 0, me_idx, FIRST_NAMES, "first")
    rest = _gather_begin(wts, 0, me_idx, LATE_NAMES, "late")
    lands = _gather_finish(first, xs, FIRST_NAMES, "first")
    late = lambda after: _prep_late(_gather_finish(rest, after, LATE_NAMES, "late"))
    for l in range(DEPTH):
        p = _prep_first(wts, lands, small, l)
        if l > 0:
            p.update(_prep_late(lands))
        nxt = _gather_begin(wts, l + 1, me_idx, BIG_NAMES, "all") if l + 1 < DEPTH else None
        dep = ([nxt[4]] if nxt else []) + ([rest[4]] if l == 0 else [])
        xs, sv = _layer_fwd(xs, p, dep or None, late if l == 0 else None)
        if nxt:
            lands = _gather_finish(nxt, xs, BIG_NAMES, "all")
        layers.append(p)
        saved.append(sv)
    dy, loss_part = _loss_head(xs, loss_target[0], "loss_head")
    loss = lax.psum(loss_part[0, 0], ("x", "y", "c"))

    big_red = [None] * DEPTH
    small_grads = [None] * DEPTH
    pending = None
    ffn = []
    for l in reversed(range(DEPTH)):
        early = None
        if l == 0:
            def early(ts_ffn):
                ffn.append(_rs_begin(ts_ffn, c_idx, me_idx, "ffn"))
                return ffn[0][4]
        dy, ts, small_grads[l] = _layer_bwd(dy, layers[l], saved[l], pending[4] if pending else None, early)
        if pending:
            big_red[l + 1] = _rs_finish(pending, dy, c_idx, "big")
        pending = _rs_begin(ts, c_idx, me_idx, "mix" if l == 0 else "big")
    red_ffn = _rs_finish(ffn[0], dy, c_idx, "ffn")
    grad_x = dy[None]

    done = {}
    for k, n in enumerate(("w_ffn_up", "w_ffn_down")):
        i = BIG_NAMES.index(n)
        g = jnp.stack([red_ffn[k]] + [big_red[l][i] for l in range(1, DEPTH)])
        done[n] = (g,) + _adamw(wts[n], g, moms[n], vels[n], "adamw_" + n, pending[4])

    rep_flat = jnp.concatenate([small_grads[l][n].reshape(-1) for l in range(DEPTH) for n, _ in REPLICATED])
    rep_flat = jnp.pad(rep_flat, (0, N_CHIPS * _REP_QUARTER - rep_flat.shape[0]))
    rows = []
    for j in range(N_CHIPS):
        pieces = [_shard_cols(small_grads[l][n], j) for l in range(DEPTH) for n, _ in SMALL_SHARDS]
        pieces.append(rep_flat[j * _REP_QUARTER:(j + 1) * _REP_QUARTER])
        rows.append(_pack_rows(pieces, SMALL_ROWS, F32))
    small_red = _reduce_scatter_chips(jnp.stack(rows), "small", done["w_ffn_down"][1])
    small_all = _all_gather_chips(small_red, "gather_small")
    big_red[0] = _rs_finish(pending, [small_all] + [done[n][1] for n in done], c_idx, "mix") + red_ffn
    small_all = small_all.reshape(N_CHIPS, -1)

    grads = {}
    for i, (n, _) in enumerate(BIG_SHARDS):
        if n not in done:
            grads[n] = jnp.stack([big_red[l][i][:, :IN_SHARD] if n == "w_in" else big_red[l][i]
                                  for l in range(DEPTH)])
    mine_small = small_red.reshape(-1)
    parts = _unpack(mine_small, [s for _ in range(DEPTH) for _, s in SMALL_SHARDS])
    for i, (n, _) in enumerate(SMALL_SHARDS):
        grads[n] = jnp.stack([parts[l * len(SMALL_SHARDS) + i] for l in range(DEPTH)])
    off = DEPTH * _SMALL_ELEMS
    rep_all = jnp.concatenate([small_all[j, off:off + _REP_QUARTER] for j in range(N_CHIPS)])
    parts = _unpack(rep_all, [s for _ in range(DEPTH) for _, s in REPLICATED])
    for i, (n, _) in enumerate(REPLICATED):
        grads[n] = jnp.stack([parts[l * len(REPLICATED) + i] for l in range(DEPTH)])

    deltas, new_m, new_v = {}, {}, {}
    for n in WEIGHT_ORDER:
        if n in done:
            grads[n], deltas[n], new_m[n], new_v[n] = done[n]
        else:
            deltas[n], new_m[n], new_v[n] = _adamw(wts[n], grads[n], moms[n], vels[n], "adamw_" + n)
    return (loss, grad_x, *[grads[n] for n in WEIGHT_ORDER], *[deltas[n] for n in WEIGHT_ORDER],
            *[new_m[n] for n in WEIGHT_ORDER], *[new_v[n] for n in WEIGHT_ORDER])
```
